```python
import jax, jax.numpy as jnp
from jax import lax
import numpy as np

D_MODEL = 2048
BATCH = 8
SEQ = 2048
DEPTH = 1

CHUNK = 64
Q_BLOCK = 128
CONV_WIDTH = D_MODEL // 2
CONV_GROUPS = 8
CONV_K = 3
V_DIM = 128
N_HEADS = (D_MODEL - CONV_WIDTH) // V_DIM
QK_NOPE = 128
QK_ROPE = 64
Q_RANK = 768
KV_RANK = 512
ATTN_WIDTH = N_HEADS * V_DIM
MIX_WIDTH = CONV_WIDTH + ATTN_WIDTH
IN_WIDTH = 3 * CONV_WIDTH + Q_RANK + KV_RANK + QK_ROPE
D_FF = 4 * D_MODEL
ROPE_THETA = 10000.0
EPS = 1e-6
NEG_INF = -1e30

kernel_name = "hybrid_conv_mla_sandwich_block"


def rms_norm(x, g):
    x32 = x.astype(jnp.float32)
    y = x32 * lax.rsqrt(jnp.mean(x32 * x32, axis=-1, keepdims=True) + EPS)
    return (y * g.astype(jnp.float32)).astype(x.dtype)


def group_rms_norm(x, g, n_groups):
    b, s, w = x.shape
    y = rms_norm(x.reshape(b, s, n_groups, w // n_groups), g.reshape(n_groups, w // n_groups))
    return y.reshape(b, s, w)


def apply_rope(x, cos, sin):
    x32 = x.astype(jnp.float32)
    x1, x2 = jnp.split(x32, 2, axis=-1)
    return jnp.concatenate([x1 * cos - x2 * sin, x2 * cos + x1 * sin], axis=-1).astype(x.dtype)


def short_conv_causal(u, w):
    rhs = w[:, None, :].astype(u.dtype)
    return lax.conv_general_dilated(
        u, rhs, window_strides=(1,), padding=[(CONV_K - 1, 0)],
        dimension_numbers=("NWC", "WIO", "NWC"), feature_group_count=u.shape[-1])


def chunk_causal_mla(q_nope, q_rope, k_nope, k_rope, v):
    s_len = q_nope.shape[1]
    scale = (QK_NOPE + QK_ROPE) ** -0.5
    chunk_id = jnp.arange(s_len) // CHUNK
    outs = []
    for start in range(0, s_len, Q_BLOCK):
        end = start + Q_BLOCK
        s = (jnp.einsum("bqhd,bkhd->bhqk", q_nope[:, start:end], k_nope[:, :end],
                        preferred_element_type=jnp.float32)
             + jnp.einsum("bqhr,bkr->bhqk", q_rope[:, start:end], k_rope[:, :end],
                          preferred_element_type=jnp.float32)) * scale
        visible = chunk_id[None, :end] <= chunk_id[start:end, None]
        s = jnp.where(visible, s, NEG_INF)
        p = jax.nn.softmax(s, axis=-1).astype(v.dtype)
        outs.append(jnp.einsum("bhqk,bkhd->bqhd", p, v[:, :end]))
    return jnp.concatenate(outs, axis=1)


def hybrid_mixer(h, w_in, conv_w, q_norm_g, w_uq, kv_norm_g, w_ukv, conv_out_g, attn_out_g, w_o):
    b, s, _ = h.shape
    proj = h @ w_in
    cuts = np.cumsum([CONV_WIDTH, CONV_WIDTH, CONV_WIDTH, Q_RANK, KV_RANK]).tolist()
    u, gate_b, gate_c, c_q, c_kv, k_rope_raw = jnp.split(proj, cuts, axis=-1)

    y_conv = gate_b * short_conv_causal(gate_c * u, conv_w)
    y_conv = group_rms_norm(y_conv, conv_out_g, CONV_GROUPS)

    q = (rms_norm(c_q, q_norm_g) @ w_uq).reshape(b, s, N_HEADS, QK_NOPE + QK_ROPE)
    q_nope, q_rope = q[..., :QK_NOPE], q[..., QK_NOPE:]
    kv = (rms_norm(c_kv, kv_norm_g) @ w_ukv).reshape(b, s, N_HEADS, QK_NOPE + V_DIM)
    k_nope, v = kv[..., :QK_NOPE], kv[..., QK_NOPE:]
    pos = jnp.arange(s, dtype=jnp.float32)
    inv_freq = jnp.power(ROPE_THETA, -jnp.arange(0, QK_ROPE, 2, dtype=jnp.float32) / QK_ROPE)
    ang = pos[:, None] * inv_freq[None, :]
    cos, sin = jnp.cos(ang), jnp.sin(ang)
    q_rope = apply_rope(q_rope, cos[None, :, None, :], sin[None, :, None, :])
    k_rope = apply_rope(k_rope_raw, cos[None], sin[None])
    o = chunk_causal_mla(q_nope, q_rope, k_nope, k_rope, v)
    y_attn = group_rms_norm(o.reshape(b, s, ATTN_WIDTH), attn_out_g, N_HEADS)

    return jnp.concatenate([y_conv, y_attn], axis=-1) @ w_o


def _fwd_setup_inputs(seed: int = 0) -> dict:
    key = jax.random.key(seed)
    ks = jax.random.split(key, 17)

    def w(k, shape, fan_in):
        return jax.random.normal(k, shape, jnp.float32) * (fan_in ** -0.5)

    def gain(k, n):
        return 1.0 + 0.05 * jax.random.normal(k, (DEPTH, n), jnp.float32)

    return {
        "x": jax.random.normal(ks[0], (BATCH, SEQ, D_MODEL), jnp.float32),
        "pre_mix_g": gain(ks[1], D_MODEL),
        "w_in": w(ks[2], (DEPTH, D_MODEL, IN_WIDTH), D_MODEL),
        "conv_w": w(ks[3], (DEPTH, CONV_K, CONV_WIDTH), CONV_K),
        "q_norm_g": gain(ks[4], Q_RANK),
        "w_uq": w(ks[5], (DEPTH, Q_RANK, N_HEADS * (QK_NOPE + QK_ROPE)), Q_RANK),
        "kv_norm_g": gain(ks[6], KV_RANK),
        "w_ukv": w(ks[7], (DEPTH, KV_RANK, N_HEADS * (QK_NOPE + V_DIM)), KV_RANK),
        "conv_out_g": gain(ks[8], CONV_WIDTH),
        "attn_out_g": gain(ks[9], ATTN_WIDTH),
        "w_o": w(ks[10], (DEPTH, MIX_WIDTH, D_MODEL), MIX_WIDTH),
        "post_mix_g": gain(ks[11], D_MODEL),
        "pre_mlp_g": gain(ks[12], D_MODEL),
        "w_up": w(ks[13], (DEPTH, D_MODEL, D_FF), D_MODEL),
        "w_down": w(ks[14], (DEPTH, D_FF, D_MODEL), D_FF),
        "post_mlp_g": gain(ks[15], D_MODEL),
    }


def _fwd_reference(x, pre_mix_g, w_in, conv_w, q_norm_g, w_uq, kv_norm_g, w_ukv, conv_out_g,
              attn_out_g, w_o, post_mix_g, pre_mlp_g, w_up, w_down, post_mlp_g):
    for l in range(DEPTH):
        h = rms_norm(x, pre_mix_g[l])
        y = hybrid_mixer(h, w_in[l], conv_w[l], q_norm_g[l], w_uq[l], kv_norm_g[l], w_ukv[l],
                         conv_out_g[l], attn_out_g[l], w_o[l])
        x = x + rms_norm(y, post_mix_g[l])
        h = rms_norm(x, pre_mlp_g[l])
        m = jnp.square(jax.nn.relu(h @ w_up[l])) @ w_down[l]
        x = x + rms_norm(m, post_mlp_g[l])
    return x


import jax as _jax
import jax.numpy as _jnp

TWIN_FORMAT = 'train_step'
FWD_PARAMS = ['x', 'pre_mix_g', 'w_in', 'conv_w', 'q_norm_g', 'w_uq', 'kv_norm_g', 'w_ukv', 'conv_out_g', 'attn_out_g', 'w_o', 'post_mix_g', 'pre_mlp_g', 'w_up', 'w_down', 'post_mlp_g']
TWIN_WEIGHTS = ['pre_mix_g', 'w_in', 'conv_w', 'q_norm_g', 'w_uq', 'kv_norm_g', 'w_ukv', 'conv_out_g', 'attn_out_g', 'w_o', 'post_mix_g', 'pre_mlp_g', 'w_up', 'w_down', 'post_mlp_g']
TWIN_DIFF_INPUT = 'x'
TWIN_INPUTS = ['x', 'pre_mix_g', 'w_in', 'conv_w', 'q_norm_g', 'w_uq', 'kv_norm_g', 'w_ukv', 'conv_out_g', 'attn_out_g', 'w_o', 'post_mix_g', 'pre_mlp_g', 'w_up', 'w_down', 'post_mlp_g', 'loss_target', 'm_pre_mix_g', 'm_w_in', 'm_conv_w', 'm_q_norm_g', 'm_w_uq', 'm_kv_norm_g', 'm_w_ukv', 'm_conv_out_g', 'm_attn_out_g', 'm_w_o', 'm_post_mix_g', 'm_pre_mlp_g', 'm_w_up', 'm_w_down', 'm_post_mlp_g', 'v_pre_mix_g', 'v_w_in', 'v_conv_w', 'v_q_norm_g', 'v_w_uq', 'v_kv_norm_g', 'v_w_ukv', 'v_conv_out_g', 'v_attn_out_g', 'v_w_o', 'v_post_mix_g', 'v_pre_mlp_g', 'v_w_up', 'v_w_down', 'v_post_mlp_g']
TWIN_OUTPUTS = ['loss', 'grad_x', 'grad_pre_mix_g', 'grad_w_in', 'grad_conv_w', 'grad_q_norm_g', 'grad_w_uq', 'grad_kv_norm_g', 'grad_w_ukv', 'grad_conv_out_g', 'grad_attn_out_g', 'grad_w_o', 'grad_post_mix_g', 'grad_pre_mlp_g', 'grad_w_up', 'grad_w_down', 'grad_post_mlp_g', 'delta_pre_mix_g', 'delta_w_in', 'delta_conv_w', 'delta_q_norm_g', 'delta_w_uq', 'delta_kv_norm_g', 'delta_w_ukv', 'delta_conv_out_g', 'delta_attn_out_g', 'delta_w_o', 'delta_post_mix_g', 'delta_pre_mlp_g', 'delta_w_up', 'delta_w_down', 'delta_post_mlp_g', 'new_m_pre_mix_g', 'new_m_w_in', 'new_m_conv_w', 'new_m_q_norm_g', 'new_m_w_uq', 'new_m_kv_norm_g', 'new_m_w_ukv', 'new_m_conv_out_g', 'new_m_attn_out_g', 'new_m_w_o', 'new_m_post_mix_g', 'new_m_pre_mlp_g', 'new_m_w_up', 'new_m_w_down', 'new_m_post_mlp_g', 'new_v_pre_mix_g', 'new_v_w_in', 'new_v_conv_w', 'new_v_q_norm_g', 'new_v_w_uq', 'new_v_kv_norm_g', 'new_v_w_ukv', 'new_v_conv_out_g', 'new_v_attn_out_g', 'new_v_w_o', 'new_v_post_mix_g', 'new_v_pre_mlp_g', 'new_v_w_up', 'new_v_w_down', 'new_v_post_mlp_g']
TWIN_LEAF_KINDS = {'loss': 'loss', 'grad_x': 'grad_x', 'grad_pre_mix_g': 'grad_w', 'grad_w_in': 'grad_w', 'grad_conv_w': 'grad_w', 'grad_q_norm_g': 'grad_w', 'grad_w_uq': 'grad_w', 'grad_kv_norm_g': 'grad_w', 'grad_w_ukv': 'grad_w', 'grad_conv_out_g': 'grad_w', 'grad_attn_out_g': 'grad_w', 'grad_w_o': 'grad_w', 'grad_post_mix_g': 'grad_w', 'grad_pre_mlp_g': 'grad_w', 'grad_w_up': 'grad_w', 'grad_w_down': 'grad_w', 'grad_post_mlp_g': 'grad_w', 'delta_pre_mix_g': 'delta_w', 'delta_w_in': 'delta_w', 'delta_conv_w': 'delta_w', 'delta_q_norm_g': 'delta_w', 'delta_w_uq': 'delta_w', 'delta_kv_norm_g': 'delta_w', 'delta_w_ukv': 'delta_w', 'delta_conv_out_g': 'delta_w', 'delta_attn_out_g': 'delta_w', 'delta_w_o': 'delta_w', 'delta_post_mix_g': 'delta_w', 'delta_pre_mlp_g': 'delta_w', 'delta_w_up': 'delta_w', 'delta_w_down': 'delta_w', 'delta_post_mlp_g': 'delta_w', 'new_m_pre_mix_g': 'new_m', 'new_m_w_in': 'new_m', 'new_m_conv_w': 'new_m', 'new_m_q_norm_g': 'new_m', 'new_m_w_uq': 'new_m', 'new_m_kv_norm_g': 'new_m', 'new_m_w_ukv': 'new_m', 'new_m_conv_out_g': 'new_m', 'new_m_attn_out_g': 'new_m', 'new_m_w_o': 'new_m', 'new_m_post_mix_g': 'new_m', 'new_m_pre_mlp_g': 'new_m', 'new_m_w_up': 'new_m', 'new_m_w_down': 'new_m', 'new_m_post_mlp_g': 'new_m', 'new_v_pre_mix_g': 'new_v', 'new_v_w_in': 'new_v', 'new_v_conv_w': 'new_v', 'new_v_q_norm_g': 'new_v', 'new_v_w_uq': 'new_v', 'new_v_kv_norm_g': 'new_v', 'new_v_w_ukv': 'new_v', 'new_v_conv_out_g': 'new_v', 'new_v_attn_out_g': 'new_v', 'new_v_w_o': 'new_v', 'new_v_post_mix_g': 'new_v', 'new_v_pre_mlp_g': 'new_v', 'new_v_w_up': 'new_v', 'new_v_w_down': 'new_v', 'new_v_post_mlp_g': 'new_v'}


def _forward(args):
    return _fwd_reference(*[args[k] for k in FWD_PARAMS])


def _output_shape():
    out = _jax.eval_shape(lambda: _forward(_fwd_setup_inputs(0)))
    return out.shape, out.dtype

N_MICROBATCH = 1
ADAM_LR = 0.001
ADAM_B1 = 0.9
ADAM_B2 = 0.999
ADAM_EPS = 1e-08
ADAM_WD = 0.01
ADAM_STEP = 10
PER_EXAMPLE_BATCH_AXIS = {'x': 0, 'loss_target': 0}
SHARED_INPUTS = []
_WEIGHT_DTYPES = {'pre_mix_g': _jnp.float32, 'w_in': _jnp.float32, 'conv_w': _jnp.float32, 'q_norm_g': _jnp.float32, 'w_uq': _jnp.float32, 'kv_norm_g': _jnp.float32, 'w_ukv': _jnp.float32, 'conv_out_g': _jnp.float32, 'attn_out_g': _jnp.float32, 'w_o': _jnp.float32, 'post_mix_g': _jnp.float32, 'pre_mlp_g': _jnp.float32, 'w_up': _jnp.float32, 'w_down': _jnp.float32, 'post_mlp_g': _jnp.float32}
MOMENT_SCALE = {'pre_mix_g': 4.834588e-01, 'w_in': 3.360638e-01, 'conv_w': 1.630010e-01, 'q_norm_g': 1.864690e-01, 'w_uq': 1.389881e-01, 'kv_norm_g': 1.032318e+00, 'w_ukv': 4.604703e-01, 'conv_out_g': 2.050223e-01, 'attn_out_g': 6.267255e-01, 'w_o': 4.532320e-01, 'post_mix_g': 7.987011e+00, 'pre_mlp_g': 2.370537e-01, 'w_up': 1.181386e-01, 'w_down': 4.561821e-01, 'post_mlp_g': 8.248557e+00}


def _to_microbatches(a, axis):
    t = _jnp.moveaxis(a, axis, 0)
    t = t.reshape((N_MICROBATCH, t.shape[0] // N_MICROBATCH) + t.shape[1:])
    return _jnp.moveaxis(t, 1, axis + 1)


def setup_inputs(seed: int = 0) -> dict:
    inp = _fwd_setup_inputs(seed)
    key = _jax.random.fold_in(_jax.random.key(seed), 7919)
    shape, _ = _output_shape()
    out = dict(inp)
    out["loss_target"] = _jax.random.normal(_jax.random.fold_in(key, 0), shape, _jnp.float32)
    for i, name in enumerate(TWIN_WEIGHTS):
        w = inp[name].astype(_jnp.float32)
        if MOMENT_SCALE is None:
            s = _jnp.sqrt(_jnp.mean(_jnp.square(w)) + 1e-30)
        else:
            s = MOMENT_SCALE[name]
        km, kv = _jax.random.split(_jax.random.fold_in(key, i + 1))
        out[name] = w
        out["m_" + name] = s * _jax.random.normal(km, w.shape, _jnp.float32)
        out["v_" + name] = (s * s) * _jax.random.uniform(kv, w.shape, _jnp.float32, 0.5, 1.5)
    if N_MICROBATCH > 1:
        for name, axis in PER_EXAMPLE_BATCH_AXIS.items():
            out[name] = _to_microbatches(out[name], axis)
    return {'x': out['x'], 'pre_mix_g': out['pre_mix_g'], 'w_in': out['w_in'], 'conv_w': out['conv_w'], 'q_norm_g': out['q_norm_g'], 'w_uq': out['w_uq'], 'kv_norm_g': out['kv_norm_g'], 'w_ukv': out['w_ukv'], 'conv_out_g': out['conv_out_g'], 'attn_out_g': out['attn_out_g'], 'w_o': out['w_o'], 'post_mix_g': out['post_mix_g'], 'pre_mlp_g': out['pre_mlp_g'], 'w_up': out['w_up'], 'w_down': out['w_down'], 'post_mlp_g': out['post_mlp_g'], 'loss_target': out['loss_target'], 'm_pre_mix_g': out['m_pre_mix_g'], 'm_w_in': out['m_w_in'], 'm_conv_w': out['m_conv_w'], 'm_q_norm_g': out['m_q_norm_g'], 'm_w_uq': out['m_w_uq'], 'm_kv_norm_g': out['m_kv_norm_g'], 'm_w_ukv': out['m_w_ukv'], 'm_conv_out_g': out['m_conv_out_g'], 'm_attn_out_g': out['m_attn_out_g'], 'm_w_o': out['m_w_o'], 'm_post_mix_g': out['m_post_mix_g'], 'm_pre_mlp_g': out['m_pre_mlp_g'], 'm_w_up': out['m_w_up'], 'm_w_down': out['m_w_down'], 'm_post_mlp_g': out['m_post_mlp_g'], 'v_pre_mix_g': out['v_pre_mix_g'], 'v_w_in': out['v_w_in'], 'v_conv_w': out['v_conv_w'], 'v_q_norm_g': out['v_q_norm_g'], 'v_w_uq': out['v_w_uq'], 'v_kv_norm_g': out['v_kv_norm_g'], 'v_w_ukv': out['v_w_ukv'], 'v_conv_out_g': out['v_conv_out_g'], 'v_attn_out_g': out['v_attn_out_g'], 'v_w_o': out['v_w_o'], 'v_post_mix_g': out['v_post_mix_g'], 'v_pre_mlp_g': out['v_pre_mlp_g'], 'v_w_up': out['v_w_up'], 'v_w_down': out['v_w_down'], 'v_post_mlp_g': out['v_post_mlp_g']}


def _loss(weights, diff, rest, loss_target):
    with _jax.named_scope("forward"):
        args = {**rest, TWIN_DIFF_INPUT: diff, **{k: w.astype(_WEIGHT_DTYPES[k]) for k, w in weights.items()}}
        y = _forward(args)
    with _jax.named_scope("loss_head"):
        err = _jnp.square(y.astype(_jnp.float32) - loss_target)
        return 0.5 * _jnp.sum(_jnp.mean(err, axis=-1)) if err.ndim else 0.5 * err


def _adamw(w, g, m, v):
    m = ADAM_B1 * m + (1.0 - ADAM_B1) * g
    v = ADAM_B2 * v + (1.0 - ADAM_B2) * _jnp.square(g)
    m_hat = m / (1.0 - ADAM_B1 ** ADAM_STEP)
    v_hat = v / (1.0 - ADAM_B2 ** ADAM_STEP)
    delta = -ADAM_LR * (m_hat / (_jnp.sqrt(v_hat) + ADAM_EPS) + ADAM_WD * w)
    return delta, m, v


def reference(x, pre_mix_g, w_in, conv_w, q_norm_g, w_uq, kv_norm_g, w_ukv, conv_out_g, attn_out_g, w_o, post_mix_g, pre_mlp_g, w_up, w_down, post_mlp_g, loss_target, m_pre_mix_g, m_w_in, m_conv_w, m_q_norm_g, m_w_uq, m_kv_norm_g, m_w_ukv, m_conv_out_g, m_attn_out_g, m_w_o, m_post_mix_g, m_pre_mlp_g, m_w_up, m_w_down, m_post_mlp_g, v_pre_mix_g, v_w_in, v_conv_w, v_q_norm_g, v_w_uq, v_kv_norm_g, v_w_ukv, v_conv_out_g, v_attn_out_g, v_w_o, v_post_mix_g, v_pre_mlp_g, v_w_up, v_w_down, v_post_mlp_g):
    given = dict(x=x, pre_mix_g=pre_mix_g, w_in=w_in, conv_w=conv_w, q_norm_g=q_norm_g, w_uq=w_uq, kv_norm_g=kv_norm_g, w_ukv=w_ukv, conv_out_g=conv_out_g, attn_out_g=attn_out_g, w_o=w_o, post_mix_g=post_mix_g, pre_mlp_g=pre_mlp_g, w_up=w_up, w_down=w_down, post_mlp_g=post_mlp_g, loss_target=loss_target, m_pre_mix_g=m_pre_mix_g, m_w_in=m_w_in, m_conv_w=m_conv_w, m_q_norm_g=m_q_norm_g, m_w_uq=m_w_uq, m_kv_norm_g=m_kv_norm_g, m_w_ukv=m_w_ukv, m_conv_out_g=m_conv_out_g, m_attn_out_g=m_attn_out_g, m_w_o=m_w_o, m_post_mix_g=m_post_mix_g, m_pre_mlp_g=m_pre_mlp_g, m_w_up=m_w_up, m_w_down=m_w_down, m_post_mlp_g=m_post_mlp_g, v_pre_mix_g=v_pre_mix_g, v_w_in=v_w_in, v_conv_w=v_conv_w, v_q_norm_g=v_q_norm_g, v_w_uq=v_w_uq, v_kv_norm_g=v_kv_norm_g, v_w_ukv=v_w_ukv, v_conv_out_g=v_conv_out_g, v_attn_out_g=v_attn_out_g, v_w_o=v_w_o, v_post_mix_g=v_post_mix_g, v_pre_mlp_g=v_pre_mlp_g, v_w_up=v_w_up, v_w_down=v_w_down, v_post_mlp_g=v_post_mlp_g)
    weights = {n: given[n] for n in TWIN_WEIGHTS}
    shared = {n: given[n] for n in SHARED_INPUTS}
    per_example = {n: given[n] for n in ['x']}
    grad_fn = _jax.value_and_grad(_loss, argnums=(0, 1))

    def one_microbatch(ex, loss_target):
        ex = dict(ex)
        diff = ex.pop(TWIN_DIFF_INPUT)
        return grad_fn(weights, diff, {**shared, **ex}, loss_target)

    if N_MICROBATCH == 1:
        loss, (grad_w, grad_x) = one_microbatch(per_example, given["loss_target"])
    else:
        def body(carry, xs):
            loss_sum, grad_sum = carry
            l_k, (gw_k, gx_k) = one_microbatch(xs[0], xs[1])
            with _jax.named_scope("update"):
                return (loss_sum + l_k, _jax.tree.map(_jnp.add, grad_sum, gw_k)), gx_k

        init = (_jnp.zeros((), _jnp.float32), _jax.tree.map(_jnp.zeros_like, weights))
        (loss, grad_w), grad_x = _jax.lax.scan(body, init, (per_example, given["loss_target"]))
    with _jax.named_scope("update"):
        delta_w, new_m, new_v = {}, {}, {}
        for n in TWIN_WEIGHTS:
            delta_w[n], new_m[n], new_v[n] = _adamw(weights[n], grad_w[n], given["m_" + n], given["v_" + n])
    return (loss, grad_x, *[grad_w[n] for n in TWIN_WEIGHTS], *[delta_w[n] for n in TWIN_WEIGHTS],
            *[new_m[n] for n in TWIN_WEIGHTS], *[new_v[n] for n in TWIN_WEIGHTS])
```

```python
import jax
import jax.numpy as jnp
from jax import lax
from jax.experimental import pallas as pl
from jax.experimental.pallas import tpu as pltpu

F32 = jnp.float32
BF16 = jnp.bfloat16

EPS = 1e-6
NEG_INF = -1e30
HEAD = 128
ROPE = 64
QK = HEAD + ROPE
CHUNK = 64
ROPE_THETA = 10000.0
ADAM_LR, ADAM_B1, ADAM_B2, ADAM_EPS, ADAM_WD, ADAM_STEP = 0.001, 0.9, 0.999, 1e-08, 0.01, 10

LANE = 128
SUBLANE = 8
VMEM_LIMIT_BYTES = 56 * 1024 * 1024

N_DEV = 8
N_CHIP = 4
MESH = pl.DeviceIdType.MESH


def _params(*sem):
    return pltpu.CompilerParams(dimension_semantics=sem, vmem_limit_bytes=VMEM_LIMIT_BYTES)


def _sublane_sum(v):
    r, w = v.shape
    return jnp.sum(v.reshape(r // SUBLANE, SUBLANE, w), axis=0)


def _rstd(x):
    return lax.rsqrt(jnp.mean(x * x, axis=-1, keepdims=True) + EPS)


def _rms_bwd(x, g, dy):
    r = _rstd(x)
    xh = x * r
    dxh = dy * g
    dx = r * (dxh - xh * jnp.mean(dxh * xh, axis=-1, keepdims=True))
    return dx, dy * xh


def _accumulate(ref, val, step):
    @pl.when(step == 0)
    def _():
        ref[...] = val

    @pl.when(step > 0)
    def _():
        ref[...] += val


NN = ((1,), (0,))
NT = ((1,), (1,))
TN = ((0,), (0,))


def _matmul(name, a, b, *, grid, a_spec, b_spec, out_shape, out_specs, contract, nk=1, acc_shape=None,
            extras=(), extra_specs=(), epilogue=None):
    multi = isinstance(out_shape, (tuple, list))
    out_shapes = tuple(out_shape) if multi else (out_shape,)
    n_out = len(out_shapes)
    n_extra = len(extras)

    def body(a_ref, b_ref, *rest):
        x_refs = rest[:n_extra]
        o_refs = rest[n_extra:n_extra + n_out]

        def emit(acc):
            vals = epilogue(acc, *[r[...] for r in x_refs]) if epilogue else (acc,)
            for r, v in zip(o_refs, vals):
                r[...] = v.astype(r.dtype)

        p = lax.dot_general(a_ref[...], b_ref[...], (contract, ((), ())), preferred_element_type=F32)
        if nk == 1:
            emit(p)
        else:
            acc_ref = rest[n_extra + n_out]
            k = pl.program_id(2)
            _accumulate(acc_ref, p, k)

            @pl.when(k == nk - 1)
            def _():
                emit(acc_ref[...])

    sem = ("parallel", "parallel") + (("arbitrary",) if nk > 1 else ())
    return pl.pallas_call(
        body, name=name, grid=grid,
        in_specs=[a_spec, b_spec, *extra_specs],
        out_specs=out_specs if multi else out_specs,
        out_shape=out_shape,
        scratch_shapes=[pltpu.VMEM(acc_shape, F32)] if nk > 1 else [],
        compiler_params=_params(*sem),
    )(a, b, *extras)


def _fit(n, tile):
    if n <= tile:
        return n
    t = tile - tile % LANE
    while n % t:
        t -= LANE
    return t


def _mm_nn(name, a, b, out_dtype, tm, tn):
    m, k = a.shape
    n = b.shape[1]
    tm, tn = _fit(m, tm), _fit(n, tn)
    return _matmul(name, a, b, grid=(m // tm, n // tn),
                   a_spec=pl.BlockSpec((tm, k), lambda i, j: (i, 0)),
                   b_spec=pl.BlockSpec((k, tn), lambda i, j: (0, j)),
                   out_shape=jax.ShapeDtypeStruct((m, n), out_dtype),
                   out_specs=pl.BlockSpec((tm, tn), lambda i, j: (i, j)), contract=NN)


def _mm_nt(name, a, b, out_dtype, tm, tn):
    m, k = a.shape
    n = b.shape[0]
    tm, tn = _fit(m, tm), _fit(n, tn)
    return _matmul(name, a, b, grid=(m // tm, n // tn),
                   a_spec=pl.BlockSpec((tm, k), lambda i, j: (i, 0)),
                   b_spec=pl.BlockSpec((tn, k), lambda i, j: (j, 0)),
                   out_shape=jax.ShapeDtypeStruct((m, n), out_dtype),
                   out_specs=pl.BlockSpec((tm, tn), lambda i, j: (i, j)), contract=NT)


def _mm_tn(name, a, b, out_dtype, tm, tn):
    s, m = a.shape
    n = b.shape[1]
    tm, tn = _fit(m, tm), _fit(n, tn)
    return _matmul(name, a, b, grid=(m // tm, n // tn),
                   a_spec=pl.BlockSpec((s, tm), lambda i, j: (0, i)),
                   b_spec=pl.BlockSpec((s, tn), lambda i, j: (0, j)),
                   out_shape=jax.ShapeDtypeStruct((m, n), out_dtype),
                   out_specs=pl.BlockSpec((tm, tn), lambda i, j: (i, j)), contract=TN)


ROWS = 256


def _row_spec(rows, width):
    return pl.BlockSpec((rows, width), lambda i: (i, 0))


def _fixed_spec(rows, width):
    return pl.BlockSpec((rows, width), lambda i: (0, 0))


def _rms_fwd(name, x, g):
    s, w = x.shape
    rows = min(ROWS, s)

    def body(x_ref, g_ref, o_ref):
        xv = x_ref[...]
        o_ref[...] = (xv * _rstd(xv) * g_ref[...]).astype(o_ref.dtype)

    return pl.pallas_call(
        body, name=name, grid=(s // rows,),
        in_specs=[_row_spec(rows, w), _fixed_spec(1, w)],
        out_specs=_row_spec(rows, w),
        out_shape=jax.ShapeDtypeStruct((s, w), BF16),
        compiler_params=_params("parallel"),
    )(x, g)


def _rms_bwd_call(name, x, g, dy, out_dtype):
    s, w = x.shape
    rows = min(ROWS, s)

    def body(x_ref, g_ref, dy_ref, dx_ref, dg_ref):
        dx, dgc = _rms_bwd(x_ref[...], g_ref[...], dy_ref[...].astype(F32))
        dx_ref[...] = dx.astype(dx_ref.dtype)
        _accumulate(dg_ref, _sublane_sum(dgc), pl.program_id(0))

    return pl.pallas_call(
        body, name=name, grid=(s // rows,),
        in_specs=[_row_spec(rows, w), _fixed_spec(1, w), _row_spec(rows, w)],
        out_specs=[_row_spec(rows, w), _fixed_spec(SUBLANE, w)],
        out_shape=[jax.ShapeDtypeStruct((s, w), out_dtype), jax.ShapeDtypeStruct((SUBLANE, w), F32)],
        compiler_params=_params("arbitrary"),
    )(x, g, dy)


def _mid_fwd(x, y, g_post, g_pre):
    s, w = x.shape
    rows = min(ROWS, s)

    def body(x_ref, y_ref, gp_ref, gq_ref, x2_ref, h2_ref):
        yv = y_ref[...]
        x2 = x_ref[...] + yv * _rstd(yv) * gp_ref[...]
        x2_ref[...] = x2
        h2_ref[...] = (x2 * _rstd(x2) * gq_ref[...]).astype(h2_ref.dtype)

    return pl.pallas_call(
        body, name="mid_fwd", grid=(s // rows,),
        in_specs=[_row_spec(rows, w), _row_spec(rows, w), _fixed_spec(1, w), _fixed_spec(1, w)],
        out_specs=[_row_spec(rows, w), _row_spec(rows, w)],
        out_shape=[jax.ShapeDtypeStruct((s, w), F32), jax.ShapeDtypeStruct((s, w), BF16)],
        compiler_params=_params("parallel"),
    )(x, y, g_post, g_pre)


def _head(m, x2, tgt, g):
    s, w = m.shape
    rows = min(ROWS, s)

    def body(m_ref, x2_ref, t_ref, g_ref, dout_ref, dm_ref, dg_ref, loss_ref):
        mv = m_ref[...]
        gv = g_ref[...]
        out = x2_ref[...] + mv * _rstd(mv) * gv
        err = out - t_ref[...]
        dout = err * (1.0 / w)
        dout_ref[...] = dout
        dm, dgc = _rms_bwd(mv, gv, dout)
        dm_ref[...] = dm.astype(dm_ref.dtype)
        sq = err * err
        lanes = sq[:, 0:LANE]
        for j in range(1, w // LANE):
            lanes = lanes + sq[:, j * LANE:(j + 1) * LANE]
        step = pl.program_id(0)
        _accumulate(dg_ref, _sublane_sum(dgc), step)
        _accumulate(loss_ref, _sublane_sum(lanes) * (0.5 / w), step)

    return pl.pallas_call(
        body, name="head", grid=(s // rows,),
        in_specs=[_row_spec(rows, w), _row_spec(rows, w), _row_spec(rows, w), _fixed_spec(1, w)],
        out_specs=[_row_spec(rows, w), _row_spec(rows, w), _fixed_spec(SUBLANE, w), _fixed_spec(SUBLANE, LANE)],
        out_shape=[jax.ShapeDtypeStruct((s, w), F32), jax.ShapeDtypeStruct((s, w), BF16),
                   jax.ShapeDtypeStruct((SUBLANE, w), F32), jax.ShapeDtypeStruct((SUBLANE, LANE), F32)],
        compiler_params=_params("arbitrary"),
    )(m, x2, tgt, g)


def _mid_bwd(x2, y, d_out, d_h2, g_pre, g_post):
    s, w = x2.shape
    rows = min(ROWS, s)

    def body(x2_ref, y_ref, dout_ref, dh2_ref, gq_ref, gp_ref, dx2_ref, dy_ref, dgq_ref, dgp_ref):
        dx, dgq = _rms_bwd(x2_ref[...], gq_ref[...], dh2_ref[...])
        dx2 = dout_ref[...] + dx
        dx2_ref[...] = dx2
        dy, dgp = _rms_bwd(y_ref[...], gp_ref[...], dx2)
        dy_ref[...] = dy.astype(dy_ref.dtype)
        step = pl.program_id(0)
        _accumulate(dgq_ref, _sublane_sum(dgq), step)
        _accumulate(dgp_ref, _sublane_sum(dgp), step)

    return pl.pallas_call(
        body, name="mid_bwd", grid=(s // rows,),
        in_specs=[_row_spec(rows, w)] * 4 + [_fixed_spec(1, w)] * 2,
        out_specs=[_row_spec(rows, w), _row_spec(rows, w), _fixed_spec(SUBLANE, w), _fixed_spec(SUBLANE, w)],
        out_shape=[jax.ShapeDtypeStruct((s, w), F32), jax.ShapeDtypeStruct((s, w), BF16),
                   jax.ShapeDtypeStruct((SUBLANE, w), F32), jax.ShapeDtypeStruct((SUBLANE, w), F32)],
        compiler_params=_params("arbitrary"),
    )(x2, y, d_out, d_h2, g_pre, g_post)


def _first_bwd(x, g, d_h1, d_x2):
    s, w = x.shape
    rows = min(ROWS, s)

    def body(x_ref, g_ref, dh_ref, dx2_ref, dx_ref, dg_ref):
        dx, dgc = _rms_bwd(x_ref[...], g_ref[...], dh_ref[...])
        dx_ref[...] = dx2_ref[...] + dx
        _accumulate(dg_ref, _sublane_sum(dgc), pl.program_id(0))

    return pl.pallas_call(
        body, name="first_bwd", grid=(s // rows,),
        in_specs=[_row_spec(rows, w), _fixed_spec(1, w), _row_spec(rows, w), _row_spec(rows, w)],
        out_specs=[_row_spec(rows, w), _fixed_spec(SUBLANE, w)],
        out_shape=[jax.ShapeDtypeStruct((s, w), F32), jax.ShapeDtypeStruct((SUBLANE, w), F32)],
        compiler_params=_params("arbitrary"),
    )(x, g, d_h1, d_x2)


def _shift_down(v, k):
    t = lax.broadcasted_iota(jnp.int32, v.shape, 0)
    return jnp.where(t >= k, pltpu.roll(v, k, 0), 0.0)


def _shift_up(v, k):
    n = v.shape[0]
    t = lax.broadcasted_iota(jnp.int32, v.shape, 0)
    return jnp.where(t < n - k, pltpu.roll(v, n - k, 0), 0.0)


def _conv_core(u, b, c, w):
    z = c * u
    conv = w[0:1, :] * _shift_down(z, 2) + w[1:2, :] * _shift_down(z, 1) + w[2:3, :] * z
    return z, conv, b * conv


def _conv_fwd(proj, conv_w, g, n_groups):
    s = proj.shape[0]

    def body(u_ref, b_ref, c_ref, w_ref, g_ref, o_ref):
        _, _, yr = _conv_core(u_ref[...], b_ref[...], c_ref[...], w_ref[...])
        o_ref[...] = (yr * _rstd(yr) * g_ref[...]).astype(o_ref.dtype)

    col = lambda k: pl.BlockSpec((s, HEAD), lambda i: (0, k * n_groups + i))
    return pl.pallas_call(
        body, name="conv_fwd", grid=(n_groups,),
        in_specs=[col(0), col(1), col(2), pl.BlockSpec((3, HEAD), lambda i: (0, i)), pl.BlockSpec((1, HEAD), lambda i: (0, i))],
        out_specs=pl.BlockSpec((s, HEAD), lambda i: (0, i)),
        out_shape=jax.ShapeDtypeStruct((s, n_groups * HEAD), BF16),
        compiler_params=_params("parallel"),
    )(proj, proj, proj, conv_w, g)


def _conv_bwd(proj, d_mix, conv_w, g, n_groups):
    s = proj.shape[0]
    width = n_groups * HEAD

    def body(u_ref, b_ref, c_ref, dy_ref, w_ref, g_ref, du_ref, db_ref, dc_ref, dg_ref, dw_ref):
        u, b, c, w = u_ref[...], b_ref[...], c_ref[...], w_ref[...]
        z, conv, yr = _conv_core(u, b, c, w)
        dyr, dgc = _rms_bwd(yr, g_ref[...], dy_ref[...])
        dconv = dyr * b
        db_ref[...] = (dyr * conv).astype(db_ref.dtype)
        dz = w[2:3, :] * dconv + w[1:2, :] * _shift_up(dconv, 1) + w[0:1, :] * _shift_up(dconv, 2)
        dc_ref[...] = (dz * u).astype(dc_ref.dtype)
        du_ref[...] = (dz * c).astype(du_ref.dtype)
        dg_ref[...] = _sublane_sum(dgc)
        dw_ref[0] = _sublane_sum(dconv * _shift_down(z, 2))
        dw_ref[1] = _sublane_sum(dconv * _shift_down(z, 1))
        dw_ref[2] = _sublane_sum(dconv * z)

    col = lambda k: pl.BlockSpec((s, HEAD), lambda i: (0, k * n_groups + i))
    grp = pl.BlockSpec((s, HEAD), lambda i: (0, i))
    return pl.pallas_call(
        body, name="conv_bwd", grid=(n_groups,),
        in_specs=[col(0), col(1), col(2), grp, pl.BlockSpec((3, HEAD), lambda i: (0, i)), pl.BlockSpec((1, HEAD), lambda i: (0, i))],
        out_specs=[grp, grp, grp, pl.BlockSpec((SUBLANE, HEAD), lambda i: (0, i)),
                   pl.BlockSpec((3, SUBLANE, HEAD), lambda i: (0, 0, i))],
        out_shape=[jax.ShapeDtypeStruct((s, width), BF16)] * 3
        + [jax.ShapeDtypeStruct((SUBLANE, width), F32), jax.ShapeDtypeStruct((3, SUBLANE, width), F32)],
        compiler_params=_params("parallel"),
    )(proj, proj, proj, d_mix, conv_w, g)


def _rope_tables(s, n_heads):
    pos = jnp.arange(s, dtype=F32)
    inv_freq = jnp.power(ROPE_THETA, -jnp.arange(0, ROPE, 2, dtype=F32) / ROPE)
    ang = pos[:, None] * inv_freq[None, :]
    cos, sin = jnp.cos(ang), jnp.sin(ang)
    cs = jnp.concatenate([cos, cos], axis=1)
    sn = jnp.concatenate([-sin, sin], axis=1)
    pad = jnp.zeros((s, LANE - ROPE), F32)
    return (jnp.tile(cs, (1, n_heads)), jnp.tile(sn, (1, n_heads)),
            jnp.concatenate([cs, pad], axis=1), jnp.concatenate([sn, pad], axis=1))


def _swap_halves(v):
    w = v.shape[1]
    lane = lax.broadcasted_iota(jnp.int32, v.shape, 1)
    first = (lane % ROPE) < (ROPE // 2)
    return jnp.where(first, pltpu.roll(v, w - ROPE // 2, 1), pltpu.roll(v, ROPE // 2, 1))


def _pack_heads(q, kv, kr, tables, n_heads):
    s = q.shape[0]
    rows = min(ROWS, s)
    cq, sq, ck, sk = tables
    wq = n_heads * ROPE

    def body(q_ref, kv_ref, kr_ref, cq_ref, sq_ref, ck_ref, sk_ref, qo_ref, ko_ref, vo_ref):
        qr = q_ref[:, n_heads * HEAD:]
        qr = qr * cq_ref[...] + _swap_halves(qr) * sq_ref[...]
        krv = kr_ref[...]
        krv = krv * ck_ref[...] + _swap_halves(krv) * sk_ref[...]
        for h in range(n_heads):
            qo_ref[h] = jnp.concatenate([q_ref[:, h * HEAD:(h + 1) * HEAD], qr[:, h * ROPE:(h + 1) * ROPE]], axis=1).astype(BF16)
            ko_ref[h] = jnp.concatenate([kv_ref[:, 2 * h * HEAD:(2 * h + 1) * HEAD], krv[:, :ROPE]], axis=1).astype(BF16)
            vo_ref[h] = kv_ref[:, (2 * h + 1) * HEAD:(2 * h + 2) * HEAD].astype(BF16)

    hs = lambda w: pl.BlockSpec((n_heads, rows, w), lambda i: (0, i, 0))
    return pl.pallas_call(
        body, name="pack_heads", grid=(s // rows,),
        in_specs=[_row_spec(rows, q.shape[1]), _row_spec(rows, kv.shape[1]), _row_spec(rows, LANE),
                  _row_spec(rows, wq), _row_spec(rows, wq), _row_spec(rows, LANE), _row_spec(rows, LANE)],
        out_specs=[hs(QK), hs(QK), hs(HEAD)],
        out_shape=[jax.ShapeDtypeStruct((n_heads, s, QK), BF16), jax.ShapeDtypeStruct((n_heads, s, QK), BF16),
                   jax.ShapeDtypeStruct((n_heads, s, HEAD), BF16)],
        compiler_params=_params("parallel"),
    )(q, kv, kr, cq, sq, ck, sk)


def _unpack_heads(dq, dk, dv, tables, n_heads):
    s = dq.shape[1]
    rows = min(ROWS, s)
    cq, sq, ck, sk = tables
    wq = n_heads * ROPE

    def body(dq_ref, dk_ref, dv_ref, cq_ref, sq_ref, ck_ref, sk_ref, qo_ref, kvo_ref, kro_ref):
        dqr = jnp.concatenate([dq_ref[h][:, HEAD:] for h in range(n_heads)], axis=1)
        dqr = dqr * cq_ref[...] - _swap_halves(dqr) * sq_ref[...]
        dkr = dk_ref[0][:, HEAD:]
        for h in range(1, n_heads):
            dkr = dkr + dk_ref[h][:, HEAD:]
        dkr = jnp.concatenate([dkr, jnp.zeros((rows, LANE - ROPE), F32)], axis=1)
        dkr = dkr * ck_ref[...] - _swap_halves(dkr) * sk_ref[...]
        kro_ref[...] = dkr.astype(kro_ref.dtype)
        qo_ref[:, n_heads * HEAD:] = dqr.astype(qo_ref.dtype)
        for h in range(n_heads):
            qo_ref[:, h * HEAD:(h + 1) * HEAD] = dq_ref[h][:, :HEAD].astype(qo_ref.dtype)
            kvo_ref[:, 2 * h * HEAD:(2 * h + 1) * HEAD] = dk_ref[h][:, :HEAD].astype(kvo_ref.dtype)
            kvo_ref[:, (2 * h + 1) * HEAD:(2 * h + 2) * HEAD] = dv_ref[h].astype(kvo_ref.dtype)

    hs = lambda w: pl.BlockSpec((n_heads, rows, w), lambda i: (0, i, 0))
    return pl.pallas_call(
        body, name="unpack_heads", grid=(s // rows,),
        in_specs=[hs(QK), hs(QK), hs(HEAD), _row_spec(rows, wq), _row_spec(rows, wq), _row_spec(rows, LANE), _row_spec(rows, LANE)],
        out_specs=[_row_spec(rows, n_heads * QK), _row_spec(rows, 2 * n_heads * HEAD), _row_spec(rows, LANE)],
        out_shape=[jax.ShapeDtypeStruct((s, n_heads * QK), BF16), jax.ShapeDtypeStruct((s, 2 * n_heads * HEAD), BF16),
                   jax.ShapeDtypeStruct((s, LANE), BF16)],
        compiler_params=_params("parallel"),
    )(dq, dk, dv, cq, sq, ck, sk)


TQ = 256


def _probs(q, k, q0):
    sc = lax.dot_general(q, k, (NT, ((), ())), preferred_element_type=F32) * (QK ** -0.5)
    row = lax.broadcasted_iota(jnp.int32, sc.shape, 0) + q0
    colk = lax.broadcasted_iota(jnp.int32, sc.shape, 1)
    sc = jnp.where(colk // CHUNK <= row // CHUNK, sc, NEG_INF)
    e = jnp.exp(sc - jnp.max(sc, axis=-1, keepdims=True))
    return e / jnp.sum(e, axis=-1, keepdims=True)


def _attn_fwd(q, k, v, g):
    n_heads, s, _ = q.shape
    tq = min(TQ, s)

    def body(q_ref, k_ref, v_ref, g_ref, o_ref, y_ref):
        p = _probs(q_ref[...], k_ref[...], pl.program_id(1) * tq)
        o = jnp.dot(p.astype(BF16), v_ref[...], preferred_element_type=F32)
        o_ref[...] = o
        y_ref[...] = (o * _rstd(o) * g_ref[...]).astype(y_ref.dtype)

    return pl.pallas_call(
        body, name="attn_fwd", grid=(n_heads, s // tq),
        in_specs=[pl.BlockSpec((None, tq, QK), lambda h, i: (h, i, 0)),
                  pl.BlockSpec((None, s, QK), lambda h, i: (h, 0, 0)),
                  pl.BlockSpec((None, s, HEAD), lambda h, i: (h, 0, 0)),
                  pl.BlockSpec((1, HEAD), lambda h, i: (0, h))],
        out_specs=[pl.BlockSpec((None, tq, HEAD), lambda h, i: (h, i, 0)),
                   pl.BlockSpec((tq, HEAD), lambda h, i: (i, h))],
        out_shape=[jax.ShapeDtypeStruct((n_heads, s, HEAD), F32), jax.ShapeDtypeStruct((s, n_heads * HEAD), BF16)],
        compiler_params=_params("parallel", "parallel"),
    )(q, k, v, g)


def _attn_bwd(q, k, v, o, d_mix, g, col0):
    n_heads, s, _ = q.shape
    tq = min(TQ, s)

    def body(q_ref, k_ref, v_ref, o_ref, dy_ref, g_ref, dq_ref, dk_ref, dv_ref, dg_ref):
        i = pl.program_id(1)
        qv, kv_, vv = q_ref[...], k_ref[...], v_ref[...]
        do, dgc = _rms_bwd(o_ref[...], g_ref[...], dy_ref[...])
        do = do.astype(BF16)
        p = _probs(qv, kv_, i * tq)
        dp = lax.dot_general(do, vv, (NT, ((), ())), preferred_element_type=F32)
        ds = (p * (dp - jnp.sum(p * dp, axis=-1, keepdims=True)) * (QK ** -0.5)).astype(BF16)
        dq_ref[...] = jnp.dot(ds, kv_, preferred_element_type=F32)
        _accumulate(dk_ref, lax.dot_general(ds, qv, (TN, ((), ())), preferred_element_type=F32), i)
        _accumulate(dv_ref, lax.dot_general(p.astype(BF16), do, (TN, ((), ())), preferred_element_type=F32), i)
        _accumulate(dg_ref, _sublane_sum(dgc), i)

    c0 = col0 // HEAD
    return pl.pallas_call(
        body, name="attn_bwd", grid=(n_heads, s // tq),
        in_specs=[pl.BlockSpec((None, tq, QK), lambda h, i: (h, i, 0)),
                  pl.BlockSpec((None, s, QK), lambda h, i: (h, 0, 0)),
                  pl.BlockSpec((None, s, HEAD), lambda h, i: (h, 0, 0)),
                  pl.BlockSpec((None, tq, HEAD), lambda h, i: (h, i, 0)),
                  pl.BlockSpec((tq, HEAD), lambda h, i: (i, c0 + h)),
                  pl.BlockSpec((1, HEAD), lambda h, i: (0, h))],
        out_specs=[pl.BlockSpec((None, tq, QK), lambda h, i: (h, i, 0)),
                   pl.BlockSpec((None, s, QK), lambda h, i: (h, 0, 0)),
                   pl.BlockSpec((None, s, HEAD), lambda h, i: (h, 0, 0)),
                   pl.BlockSpec((SUBLANE, HEAD), lambda h, i: (0, h))],
        out_shape=[jax.ShapeDtypeStruct((n_heads, s, QK), F32), jax.ShapeDtypeStruct((n_heads, s, QK), F32),
                   jax.ShapeDtypeStruct((n_heads, s, HEAD), F32), jax.ShapeDtypeStruct((SUBLANE, n_heads * HEAD), F32)],
        compiler_params=_params("parallel", "arbitrary"),
    )(q, k, v, o, d_mix, g)


TILE_M = 1024
TILE_N = 1024


def _up_fwd(h2, w_up):
    s, d = h2.shape
    nb, _, fb = w_up.shape
    tm = min(TILE_M,s)

    def epilogue(acc):
        r = jnp.maximum(acc, 0.0)
        return r * r, r

    blk = pl.BlockSpec((tm, fb), lambda i, j: (i, j))
    return _matmul("up_fwd", h2, w_up, grid=(s // tm, nb),
                   a_spec=pl.BlockSpec((tm, d), lambda i, j: (i, 0)),
                   b_spec=pl.BlockSpec((None, d, fb), lambda i, j: (j, 0, 0)),
                   out_shape=[jax.ShapeDtypeStruct((s, nb * fb), BF16)] * 2, out_specs=[blk, blk],
                   contract=NN, epilogue=epilogue)


def _down_fwd(a, w_down):
    s, f = a.shape
    d = w_down.shape[1]
    tm, tn, tk = min(TILE_M,s), min(TILE_N,d), 2048
    nk = f // tk
    return _matmul("down_fwd", a, w_down, grid=(s // tm, d // tn, nk),
                   a_spec=pl.BlockSpec((tm, tk), lambda i, j, k: (i, k)),
                   b_spec=pl.BlockSpec((tk, tn), lambda i, j, k: (k, j)),
                   out_shape=jax.ShapeDtypeStruct((s, d), F32),
                   out_specs=pl.BlockSpec((tm, tn), lambda i, j, k: (i, j)),
                   contract=NN, nk=nk, acc_shape=(tm, tn))


def _down_bwd_act(d_m, w_down, r):
    s, d = d_m.shape
    f = w_down.shape[0]
    tm, tn = min(TILE_M,s), min(TILE_N,f)
    blk = pl.BlockSpec((tm, tn), lambda i, j: (i, j))
    return _matmul("down_bwd_act", d_m, w_down, grid=(s // tm, f // tn),
                   a_spec=pl.BlockSpec((tm, d), lambda i, j: (i, 0)),
                   b_spec=pl.BlockSpec((tn, d), lambda i, j: (j, 0)),
                   out_shape=jax.ShapeDtypeStruct((s, f), BF16), out_specs=blk, contract=NT,
                   extras=(r,), extra_specs=(blk,),
                   epilogue=lambda acc, rv: (acc * (2.0 * rv.astype(F32)),))


def _up_bwd_act(d_up, w_up):
    s, _ = d_up.shape
    nb, d, fb = w_up.shape
    tm, tn = min(TILE_M,s), min(TILE_N,d)
    return _matmul("up_bwd_act", d_up, w_up, grid=(s // tm, d // tn, nb),
                   a_spec=pl.BlockSpec((tm, fb), lambda i, j, k: (i, k)),
                   b_spec=pl.BlockSpec((None, tn, fb), lambda i, j, k: (k, j, 0)),
                   out_shape=jax.ShapeDtypeStruct((s, d), F32),
                   out_specs=pl.BlockSpec((tm, tn), lambda i, j, k: (i, j)),
                   contract=NT, nk=nb, acc_shape=(tm, tn))


def _up_bwd_w(h2, d_up, nb):
    s, d = h2.shape
    fb = d_up.shape[1] // nb
    tm = min(TILE_M,d)
    return _matmul("up_bwd_w", h2, d_up, grid=(d // tm, nb),
                   a_spec=pl.BlockSpec((s, tm), lambda i, j: (0, i)),
                   b_spec=pl.BlockSpec((s, fb), lambda i, j: (0, j)),
                   out_shape=jax.ShapeDtypeStruct((nb, d, fb), BF16),
                   out_specs=pl.BlockSpec((None, tm, fb), lambda i, j: (j, i, 0)), contract=TN)


def _local_step(x, tgt, gains, conv_w, w_in, w_uq, w_ukv, w_o, w_up, w_down):
    pre_mix_g, q_norm_g, kv_norm_g, conv_out_g, attn_out_g, post_mix_g, pre_mlp_g, post_mlp_g = gains
    s, d = x.shape
    conv_width = conv_w.shape[1]
    n_groups = conv_width // HEAD
    r_q, r_kv = w_uq.shape[0], w_ukv.shape[0]
    n_heads = w_uq.shape[1] // QK
    in_width = w_in.shape[1]
    c_q0 = 3 * conv_width
    c_kv0 = c_q0 + r_q
    c_kr0 = c_kv0 + r_kv
    in_pad = -(-in_width // LANE) * LANE
    tn_in = in_pad // 5 if in_pad % (5 * LANE) == 0 else LANE

    w_in_p = jnp.pad(w_in, ((0, 0), (0, in_pad - in_width)))
    wq3 = w_uq.reshape(r_q, n_heads, QK)
    w_uq_p = jnp.concatenate([wq3[:, :, :HEAD].reshape(r_q, n_heads * HEAD), wq3[:, :, HEAD:].reshape(r_q, n_heads * ROPE)], axis=1)
    tables = _rope_tables(s, n_heads)

    h1 = _rms_fwd("pre_mix_norm", x, pre_mix_g)
    proj = _mm_nn("in_proj", h1, w_in_p, F32, TILE_M,tn_in)
    y_conv = _conv_fwd(proj, conv_w, conv_out_g, n_groups)
    c_q = proj[:, c_q0:c_kv0]
    c_kv = proj[:, c_kv0:c_kr0]
    qn = _rms_fwd("q_norm", c_q, q_norm_g)
    kvn = _rms_fwd("kv_norm", c_kv, kv_norm_g)
    q = _mm_nn("q_up", qn, w_uq_p, F32, TILE_M, TILE_N)
    kv = _mm_nn("kv_up", kvn, w_ukv, F32, TILE_M, TILE_N)
    kr = proj[:, c_kr0:c_kr0 + LANE]
    qh, kh, vh = _pack_heads(q, kv, kr, tables, n_heads)
    o, y_attn = _attn_fwd(qh, kh, vh, attn_out_g)
    mix = jnp.concatenate([y_conv, y_attn], axis=1)
    y = _mm_nn("out_proj", mix, w_o, F32, TILE_M, TILE_N)
    x2, h2 = _mid_fwd(x, y, post_mix_g, pre_mlp_g)
    a, r = _up_fwd(h2, w_up)
    m = _down_fwd(a, w_down)

    d_out, d_m, dg_post_mlp, loss_part = _head(m, x2, tgt, post_mlp_g)
    d_up = _down_bwd_act(d_m, w_down, r)
    gw_down = _mm_tn("down_bwd_w", a, d_m, BF16, TILE_M, TILE_N)
    d_h2 = _up_bwd_act(d_up, w_up)
    gw_up = _up_bwd_w(h2, d_up, w_up.shape[0])
    d_x2, d_y, dg_pre_mlp, dg_post_mix = _mid_bwd(x2, y, d_out, d_h2, pre_mlp_g, post_mix_g)
    d_mix = _mm_nt("out_proj_bwd_act", d_y, w_o, F32, TILE_M, TILE_N)
    gw_o = _mm_tn("out_proj_bwd_w", mix, d_y, BF16, TILE_M, TILE_N)
    dqh, dkh, dvh, dg_attn = _attn_bwd(qh, kh, vh, o, d_mix, attn_out_g, conv_width)
    d_q, d_kv, d_kr = _unpack_heads(dqh, dkh, dvh, tables, n_heads)
    d_qn = _mm_nt("q_up_bwd_act", d_q, w_uq_p, F32, TILE_M, TILE_N)
    d_kvn = _mm_nt("kv_up_bwd_act", d_kv, w_ukv, F32, TILE_M, TILE_N)
    gw_uq_p = _mm_tn("q_up_bwd_w", qn, d_q, BF16, TILE_M, TILE_N)
    gw_ukv = _mm_tn("kv_up_bwd_w", kvn, d_kv, BF16, TILE_M, TILE_N)
    d_cq, dg_q = _rms_bwd_call("q_norm_bwd", c_q, q_norm_g, d_qn, BF16)
    d_ckv, dg_kv = _rms_bwd_call("kv_norm_bwd", c_kv, kv_norm_g, d_kvn, BF16)
    d_u, d_b, d_c, dg_conv, dw_conv = _conv_bwd(proj, d_mix, conv_w, conv_out_g, n_groups)
    d_proj = jnp.concatenate([d_u, d_b, d_c, d_cq, d_ckv, d_kr[:, :in_pad - c_kr0]], axis=1)
    d_h1 = _mm_nt("in_proj_bwd_act", d_proj, w_in_p, F32, TILE_M,512)
    gw_in_p = _mm_tn("in_proj_bwd_w", h1, d_proj, BF16, TILE_M, tn_in)
    grad_x, dg_pre_mix = _first_bwd(x, pre_mix_g, d_h1, d_x2)

    gw_in = gw_in_p[:, :in_width]
    gq_n = gw_uq_p[:, :n_heads * HEAD].reshape(r_q, n_heads, HEAD)
    gq_r = gw_uq_p[:, n_heads * HEAD:].reshape(r_q, n_heads, ROPE)
    gw_uq = jnp.concatenate([gq_n, gq_r], axis=2).reshape(r_q, n_heads * QK)

    small = [dg_pre_mix, dg_q, dg_kv, dg_conv, dg_attn, dg_post_mix, dg_pre_mlp, dg_post_mlp,
             dw_conv[0], dw_conv[1], dw_conv[2], loss_part]
    return grad_x, (gw_in, gw_uq, gw_ukv, gw_o, gw_up, gw_down), jnp.concatenate(small, axis=1)


ANY = pl.BlockSpec(memory_space=pl.ANY)


def _place():
    x, y, c = lax.axis_index("x"), lax.axis_index("y"), lax.axis_index("c")
    other_chips = [(1 - x, y), (x, 1 - y), (1 - x, 1 - y)]
    return x, y, c, other_chips


def _all_gather(shards):
    n = len(shards)

    def body(*refs):
        ins, outs = refs[:n], refs[n:2 * n]
        send_sems, recv_sems, local_sems = refs[2 * n:]
        x, y, c, chips = _place()
        me, sibling = (x, y, c), (x, y, 1 - c)

        def block(px, py, pc):
            return 4 * px + 2 * py + pc

        def copy(w, k, owner, to, src=None):
            dst = outs[w].at[block(*owner)]
            return pltpu.make_async_remote_copy(
                src_ref=dst if src is None else src, dst_ref=dst,
                send_sem=send_sems.at[w, k], recv_sem=recv_sems.at[w, k], device_id=to, device_id_type=MESH)

        mine = [pltpu.make_async_copy(ins[w], outs[w].at[block(*me)], local_sems.at[w]) for w in range(n)]
        for cp in mine:
            cp.start()
        first = []
        for w in range(n):
            first.append(copy(w, 0, me, sibling, src=ins[w]))
            first += [copy(w, 1 + j, me, (*chip, c), src=ins[w]) for j, chip in enumerate(chips)]
        for cp in first:
            cp.start()
        passed = []
        for j, chip in enumerate(chips):
            for w in range(n):
                copy(w, 1 + j, (*chip, c), me).wait_recv()
                fwd = copy(w, 4 + j, (*chip, c), sibling)
                fwd.start()
                passed.append(fwd)
        for w in range(n):
            copy(w, 0, sibling, me).wait_recv()
            for j, chip in enumerate(chips):
                copy(w, 4 + j, (*chip, 1 - c), me).wait_recv()
        for cp in first + passed:
            cp.wait_send()
        for cp in mine:
            cp.wait()

    return pl.pallas_call(
        body, name="all_gather_weights",
        in_specs=[ANY] * n, out_specs=[ANY] * n,
        out_shape=[jax.ShapeDtypeStruct((N_DEV, *a.shape), a.dtype) for a in shards],
        scratch_shapes=[pltpu.SemaphoreType.DMA((n, 7)), pltpu.SemaphoreType.DMA((n, 7)), pltpu.SemaphoreType.DMA((n,))],
    )(*shards)


def _pair_exchange(grads):
    n = len(grads)

    def body(*refs):
        ins, recv, kept = refs[:n], refs[n:2 * n], refs[2 * n:3 * n]
        send_sems, recv_sems, local_sems = refs[3 * n:]
        x, y, c, _ = _place()
        sends, locals_ = [], []
        for w in range(n):
            for k in range(N_CHIP):
                sends.append(pltpu.make_async_remote_copy(
                    src_ref=ins[w].at[2 * k + 1 - c], dst_ref=recv[w].at[k],
                    send_sem=send_sems.at[w, k], recv_sem=recv_sems.at[w, k],
                    device_id=(x, y, 1 - c), device_id_type=MESH))
                locals_.append(pltpu.make_async_copy(ins[w].at[2 * k + c], kept[w].at[k], local_sems.at[w, k]))
        for cp in sends + locals_:
            cp.start()
        for cp in sends:
            cp.wait()
        for cp in locals_:
            cp.wait()

    quarter = [jax.ShapeDtypeStruct((N_CHIP, *g.shape[1:]), g.dtype) for g in grads]
    out = pl.pallas_call(
        body, name="grad_pair_exchange",
        in_specs=[ANY] * n, out_specs=[ANY] * (2 * n), out_shape=quarter + quarter,
        scratch_shapes=[pltpu.SemaphoreType.DMA((n, N_CHIP))] * 3,
    )(*grads)
    return out[:n], out[n:]


def _pair_sum(kept, received):
    _, r, c = kept.shape
    rows = min(ROWS, r)
    assert r % rows == 0

    def body(a_ref, b_ref, o_ref):
        o_ref[...] = (a_ref[...].astype(F32) + b_ref[...].astype(F32)).astype(o_ref.dtype)

    spec = pl.BlockSpec((None, rows, c), lambda k, i: (k, i, 0))
    return pl.pallas_call(
        body, name="grad_pair_sum", grid=(N_CHIP, r // rows),
        in_specs=[spec, spec], out_specs=spec, out_shape=jax.ShapeDtypeStruct(kept.shape, kept.dtype),
        compiler_params=_params("parallel", "parallel"),
    )(kept, received)


def _chip_exchange(sums):
    n = len(sums)

    def body(*refs):
        ins, outs = refs[:n], refs[n:2 * n]
        send_sems, recv_sems, local_sems = refs[2 * n:]
        x, y, c, chips = _place()
        mine = 2 * x + y
        sends, locals_ = [], []
        for w in range(n):
            locals_.append(pltpu.make_async_copy(ins[w].at[mine], outs[w].at[mine], local_sems.at[w]))
            for j, (px, py) in enumerate(chips):
                sends.append(pltpu.make_async_remote_copy(
                    src_ref=ins[w].at[2 * px + py], dst_ref=outs[w].at[mine],
                    send_sem=send_sems.at[w, j], recv_sem=recv_sems.at[w, j],
                    device_id=(px, py, c), device_id_type=MESH))
        for cp in sends + locals_:
            cp.start()
        for cp in sends:
            cp.wait()
        for cp in locals_:
            cp.wait()

    return pl.pallas_call(
        body, name="grad_chip_exchange",
        in_specs=[ANY] * n, out_specs=[ANY] * n,
        out_shape=[jax.ShapeDtypeStruct(a.shape, a.dtype) for a in sums],
        scratch_shapes=[pltpu.SemaphoreType.DMA((n, 3)), pltpu.SemaphoreType.DMA((n, 3)), pltpu.SemaphoreType.DMA((n,))],
    )(*sums)


def _small_all_reduce(part):
    _, w = part.shape

    def body(p_ref, o_ref, buf, send_sems, recv_sems):
        x, y, c, _ = _place()
        me = 4 * x + 2 * y + c
        buf[me] = p_ref[...]
        copies = []
        for k in range(1, N_DEV):
            dx, dy, dc = (k >> 2) & 1, (k >> 1) & 1, k & 1
            copies.append(pltpu.make_async_remote_copy(
                src_ref=p_ref, dst_ref=buf.at[me], send_sem=send_sems.at[k - 1], recv_sem=recv_sems.at[k - 1],
                device_id=(x ^ dx, y ^ dy, c ^ dc), device_id_type=MESH))
        for cp in copies:
            cp.start()
        for cp in copies:
            cp.wait()
        tot = buf[0]
        for d in range(1, N_DEV):
            tot = tot + buf[d]
        tot = jnp.sum(tot, axis=0, keepdims=True)
        o_ref[...] = tot
        loss = jnp.sum(tot[:, w - LANE:], axis=1, keepdims=True)
        o_ref[:, w - LANE:] = jnp.broadcast_to(loss, (1, LANE))

    return pl.pallas_call(
        body, name="small_all_reduce",
        in_specs=[pl.BlockSpec(memory_space=pltpu.VMEM)], out_specs=pl.BlockSpec(memory_space=pltpu.VMEM),
        out_shape=jax.ShapeDtypeStruct((1, w), F32),
        scratch_shapes=[pltpu.VMEM((N_DEV, SUBLANE, w), F32), pltpu.SemaphoreType.DMA((N_DEV - 1,)), pltpu.SemaphoreType.DMA((N_DEV - 1,))],
        compiler_params=pltpu.CompilerParams(vmem_limit_bytes=VMEM_LIMIT_BYTES),
    )(part)


def _adamw(w, g, m, v):
    m = ADAM_B1 * m + (1.0 - ADAM_B1) * g
    v = ADAM_B2 * v + (1.0 - ADAM_B2) * (g * g)
    m_hat = m / (1.0 - ADAM_B1 ** ADAM_STEP)
    v_hat = v / (1.0 - ADAM_B2 ** ADAM_STEP)
    delta = -ADAM_LR * (m_hat / (jnp.sqrt(v_hat) + ADAM_EPS) + ADAM_WD * w)
    return delta, m, v


def _sum_adam(name, parts, w, m, v):
    r, c = w.shape
    rows = min(ROWS, r)
    assert r % rows == 0

    def body(p_ref, w_ref, m_ref, v_ref, g_ref, d_ref, mo_ref, vo_ref):
        g = p_ref[0].astype(F32)
        for k in range(1, N_CHIP):
            g = g + p_ref[k].astype(F32)
        g_ref[...] = g
        d_ref[...], mo_ref[...], vo_ref[...] = _adamw(w_ref[...], g, m_ref[...], v_ref[...])

    blk = pl.BlockSpec((rows, c), lambda i: (i, 0))
    out = jax.ShapeDtypeStruct((r, c), F32)
    return pl.pallas_call(
        body, name=name, grid=(r // rows,),
        in_specs=[pl.BlockSpec((N_CHIP, rows, c), lambda i: (0, i, 0)), blk, blk, blk],
        out_specs=[blk] * 4, out_shape=[out] * 4,
        compiler_params=_params("parallel"),
    )(parts, w, m, v)


def _adam_small(w, g, m, v):
    def body(w_ref, g_ref, m_ref, v_ref, d_ref, mo_ref, vo_ref):
        d_ref[...], mo_ref[...], vo_ref[...] = _adamw(w_ref[...], g_ref[...], m_ref[...], v_ref[...])

    out = jax.ShapeDtypeStruct(w.shape, F32)
    return pl.pallas_call(body, name="adam_small", out_shape=[out] * 3)(w, g, m, v)


def kernel(x, pre_mix_g, w_in, conv_w, q_norm_g, w_uq, kv_norm_g, w_ukv, conv_out_g, attn_out_g, w_o, post_mix_g, pre_mlp_g, w_up, w_down, post_mlp_g, loss_target, m_pre_mix_g, m_w_in, m_conv_w, m_q_norm_g, m_w_uq, m_kv_norm_g, m_w_ukv, m_conv_out_g, m_attn_out_g, m_w_o, m_post_mix_g, m_pre_mlp_g, m_w_up, m_w_down, m_post_mlp_g, v_pre_mix_g, v_w_in, v_conv_w, v_q_norm_g, v_w_uq, v_kv_norm_g, v_w_ukv, v_conv_out_g, v_attn_out_g, v_w_o, v_post_mix_g, v_pre_mlp_g, v_w_up, v_w_down, v_post_mlp_g):
    me = 4 * lax.axis_index("x") + 2 * lax.axis_index("y") + lax.axis_index("c")
    gains = (pre_mix_g, q_norm_g, kv_norm_g, conv_out_g, attn_out_g, post_mix_g, pre_mlp_g, post_mlp_g)
    gain_m = (m_pre_mix_g, m_q_norm_g, m_kv_norm_g, m_conv_out_g, m_attn_out_g, m_post_mix_g, m_pre_mlp_g, m_post_mlp_g)
    gain_v = (v_pre_mix_g, v_q_norm_g, v_kv_norm_g, v_conv_out_g, v_attn_out_g, v_post_mix_g, v_pre_mlp_g, v_post_mlp_g)
    big = (w_in[0], w_uq[0], w_ukv[0], w_o[0], w_up[0], w_down[0])
    big_m = (m_w_in[0], m_w_uq[0], m_w_ukv[0], m_w_o[0], m_w_up[0], m_w_down[0])
    big_v = (v_w_in[0], v_w_uq[0], v_w_ukv[0], v_w_o[0], v_w_up[0], v_w_down[0])

    taps = jnp.pad(conv_w[0], ((0, SUBLANE - conv_w.shape[1]), (0, 0)))
    full = _all_gather([b.astype(BF16) for b in big] + [taps])
    cols = lambda a: a.transpose(1, 0, 2).reshape(a.shape[1], N_DEV * a.shape[2])
    rows = lambda a: a.reshape(N_DEV * a.shape[1], a.shape[2])
    conv_full = cols(full[6])[:conv_w.shape[1]]

    grad_x, gws, small = _local_step(x[0], loss_target[0], gains, conv_full,
                                     cols(full[0]), cols(full[1]), cols(full[2]), rows(full[3]), full[4], rows(full[5]))

    col_blocks = lambda g: g.reshape(g.shape[0], N_DEV, g.shape[1] // N_DEV).transpose(1, 0, 2)
    row_blocks = lambda g: g.reshape(N_DEV, g.shape[0] // N_DEV, g.shape[1])
    blocks = [col_blocks(gws[0]), col_blocks(gws[1]), col_blocks(gws[2]), row_blocks(gws[3]), gws[4], row_blocks(gws[5])]
    received, kept = _pair_exchange(blocks)
    sums = [_pair_sum(k, r) for k, r in zip(kept, received)]
    parts = _chip_exchange(sums)
    total = _small_all_reduce(small)

    names = ("w_in", "w_uq", "w_ukv", "w_o", "w_up", "w_down")
    big_out = [_sum_adam("adam_" + nm, p, w, m, v) for nm, p, w, m, v in zip(names, parts, big, big_m, big_v)]

    widths = [g.shape[1] for g in gains]
    offs = [sum(widths[:i]) for i in range(len(widths) + 1)]
    gain_grads = [total[:, offs[i]:offs[i + 1]] for i in range(len(widths))]
    cw = conv_w.shape[2]
    conv_total = conv_full.shape[1]
    conv_grad = jnp.concatenate(
        [lax.dynamic_slice_in_dim(total[:, offs[-1] + t * conv_total:offs[-1] + (t + 1) * conv_total], me * cw, cw, axis=1)
         for t in range(conv_w.shape[1])], axis=1)
    flat = lambda a: a.reshape(1, -1)
    pack = lambda gs, cv: jnp.concatenate([*gs, flat(cv)], axis=1)
    small_g = jnp.concatenate([*gain_grads, conv_grad], axis=1)
    small_d, small_m, small_v = _adam_small(pack(gains, conv_w), small_g, pack(gain_m, m_conv_w), pack(gain_v, v_conv_w))

    def unpack(row):
        gs = [row[:, offs[i]:offs[i + 1]] for i in range(len(widths))]
        return gs, row[:, offs[-1]:].reshape(conv_w.shape)

    loss = total[0, total.shape[1] - 1]
    order = (0, "w_in", "conv", 1, "w_uq", 2, "w_ukv", 3, 4, "w_o", 5, 6, "w_up", "w_down", 7)

    def assemble(gain_list, conv_item, big_list):
        by_name = dict(zip(names, big_list))
        out = []
        for item in order:
            if item == "conv":
                out.append(conv_item)
            elif isinstance(item, int):
                out.append(gain_list[item])
            else:
                out.append(by_name[item][None])
        return out

    g_gains, g_conv = unpack(small_g)
    d_gains, d_conv = unpack(small_d)
    m_gains, m_conv = unpack(small_m)
    v_gains, v_conv = unpack(small_v)
    outs = [loss, grad_x[None]]
    outs += assemble(g_gains, g_conv, [o[0] for o in big_out])
    outs += assemble(d_gains, d_conv, [o[1] for o in big_out])
    outs += assemble(m_gains, m_conv, [o[2] for o in big_out])
    outs += assemble(v_gains, v_conv, [o[3] for o in big_out])
    return tuple(outs)
```

```python
import jax
import jax.numpy as jnp
from jax import lax
from jax.experimental import pallas as pl
from jax.experimental.pallas import tpu as pltpu

F32 = jnp.float32
BF16 = jnp.bfloat16

EPS = 1e-6
NEG_INF = -1e30
HEAD = 128
ROPE = 64
QK = HEAD + ROPE
CHUNK = 64
ROPE_THETA = 10000.0
ADAM_LR, ADAM_B1, ADAM_B2, ADAM_EPS, ADAM_WD, ADAM_STEP = 0.001, 0.9, 0.999, 1e-08, 0.01, 10

LANE = 128
SUBLANE = 8
VMEM_LIMIT_BYTES = 56 * 1024 * 1024

N_DEV = 8
N_CHIP = 4
MESH = pl.DeviceIdType.MESH


def _params(*sem):
    return pltpu.CompilerParams(dimension_semantics=sem, vmem_limit_bytes=VMEM_LIMIT_BYTES)


def _sublane_sum(v):
    r, w = v.shape
    return jnp.sum(v.reshape(r // SUBLANE, SUBLANE, w), axis=0)


def _rstd(x):
    return lax.rsqrt(jnp.mean(x * x, axis=-1, keepdims=True) + EPS)


def _rms_bwd(x, g, dy):
    r = _rstd(x)
    xh = x * r
    dxh = dy * g
    dx = r * (dxh - xh * jnp.mean(dxh * xh, axis=-1, keepdims=True))
    return dx, dy * xh


def _accumulate(ref, val, step):
    @pl.when(step == 0)
    def _():
        ref[...] = val

    @pl.when(step > 0)
    def _():
        ref[...] += val


NN = ((1,), (0,))
NT = ((1,), (1,))
TN = ((0,), (0,))


def _matmul(name, a, b, *, grid, a_spec, b_spec, out_shape, out_specs, contract, nk=1, acc_shape=None,
            extras=(), extra_specs=(), epilogue=None):
    multi = isinstance(out_shape, (tuple, list))
    out_shapes = tuple(out_shape) if multi else (out_shape,)
    n_out = len(out_shapes)
    n_extra = len(extras)

    def body(a_ref, b_ref, *rest):
        x_refs = rest[:n_extra]
        o_refs = rest[n_extra:n_extra + n_out]

        def emit(acc):
            vals = epilogue(acc, *[r[...] for r in x_refs]) if epilogue else (acc,)
            for r, v in zip(o_refs, vals):
                r[...] = v.astype(r.dtype)

        p = lax.dot_general(a_ref[...], b_ref[...], (contract, ((), ())), preferred_element_type=F32)
        if nk == 1:
            emit(p)
        else:
            acc_ref = rest[n_extra + n_out]
            k = pl.program_id(2)
            _accumulate(acc_ref, p, k)

            @pl.when(k == nk - 1)
            def _():
                emit(acc_ref[...])

    sem = ("parallel", "parallel") + (("arbitrary",) if nk > 1 else ())
    return pl.pallas_call(
        body, name=name, grid=grid,
        in_specs=[a_spec, b_spec, *extra_specs],
        out_specs=out_specs if multi else out_specs,
        out_shape=out_shape,
        scratch_shapes=[pltpu.VMEM(acc_shape, F32)] if nk > 1 else [],
        compiler_params=_params(*sem),
    )(a, b, *extras)


def _fit(n, tile):
    if n <= tile:
        return n
    t = tile - tile % LANE
    while n % t:
        t -= LANE
    return t


def _mm_nn(name, a, b, out_dtype, tm, tn):
    m, k = a.shape
    n = b.shape[1]
    tm, tn = _fit(m, tm), _fit(n, tn)
    return _matmul(name, a, b, grid=(m // tm, n // tn),
                   a_spec=pl.BlockSpec((tm, k), lambda i, j: (i, 0)),
                   b_spec=pl.BlockSpec((k, tn), lambda i, j: (0, j)),
                   out_shape=jax.ShapeDtypeStruct((m, n), out_dtype),
                   out_specs=pl.BlockSpec((tm, tn), lambda i, j: (i, j)), contract=NN)


def _mm_nt(name, a, b, out_dtype, tm, tn):
    m, k = a.shape
    n = b.shape[0]
    tm, tn = _fit(m, tm), _fit(n, tn)
    return _matmul(name, a, b, grid=(m // tm, n // tn),
                   a_spec=pl.BlockSpec((tm, k), lambda i, j: (i, 0)),
                   b_spec=pl.BlockSpec((tn, k), lambda i, j: (j, 0)),
                   out_shape=jax.ShapeDtypeStruct((m, n), out_dtype),
                   out_specs=pl.BlockSpec((tm, tn), lambda i, j: (i, j)), contract=NT)


def _mm_tn(name, a, b, out_dtype, tm, tn):
    s, m = a.shape
    n = b.shape[1]
    tm, tn = _fit(m, tm), _fit(n, tn)
    return _matmul(name, a, b, grid=(m // tm, n // tn),
                   a_spec=pl.BlockSpec((s, tm), lambda i, j: (0, i)),
                   b_spec=pl.BlockSpec((s, tn), lambda i, j: (0, j)),
                   out_shape=jax.ShapeDtypeStruct((m, n), out_dtype),
                   out_specs=pl.BlockSpec((tm, tn), lambda i, j: (i, j)), contract=TN)


ROWS = 256


def _row_spec(rows, width):
    return pl.BlockSpec((rows, width), lambda i: (i, 0))


def _fixed_spec(rows, width):
    return pl.BlockSpec((rows, width), lambda i: (0, 0))


def _rms_fwd(name, x, g):
    s, w = x.shape
    rows = min(ROWS, s)

    def body(x_ref, g_ref, o_ref):
        xv = x_ref[...]
        o_ref[...] = (xv * _rstd(xv) * g_ref[...]).astype(o_ref.dtype)

    return pl.pallas_call(
        body, name=name, grid=(s // rows,),
        in_specs=[_row_spec(rows, w), _fixed_spec(1, w)],
        out_specs=_row_spec(rows, w),
        out_shape=jax.ShapeDtypeStruct((s, w), BF16),
        compiler_params=_params("parallel"),
    )(x, g)


def _rms_bwd_call(name, x, g, dy, out_dtype):
    s, w = x.shape
    rows = min(ROWS, s)

    def body(x_ref, g_ref, dy_ref, dx_ref, dg_ref):
        dx, dgc = _rms_bwd(x_ref[...], g_ref[...], dy_ref[...].astype(F32))
        dx_ref[...] = dx.astype(dx_ref.dtype)
        _accumulate(dg_ref, _sublane_sum(dgc), pl.program_id(0))

    return pl.pallas_call(
        body, name=name, grid=(s // rows,),
        in_specs=[_row_spec(rows, w), _fixed_spec(1, w), _row_spec(rows, w)],
        out_specs=[_row_spec(rows, w), _fixed_spec(SUBLANE, w)],
        out_shape=[jax.ShapeDtypeStruct((s, w), out_dtype), jax.ShapeDtypeStruct((SUBLANE, w), F32)],
        compiler_params=_params("arbitrary"),
    )(x, g, dy)


def _mid_fwd(x, y, g_post, g_pre):
    s, w = x.shape
    rows = min(ROWS, s)

    def body(x_ref, y_ref, gp_ref, gq_ref, x2_ref, h2_ref):
        yv = y_ref[...]
        x2 = x_ref[...] + yv * _rstd(yv) * gp_ref[...]
        x2_ref[...] = x2
        h2_ref[...] = (x2 * _rstd(x2) * gq_ref[...]).astype(h2_ref.dtype)

    return pl.pallas_call(
        body, name="mid_fwd", grid=(s // rows,),
        in_specs=[_row_spec(rows, w), _row_spec(rows, w), _fixed_spec(1, w), _fixed_spec(1, w)],
        out_specs=[_row_spec(rows, w), _row_spec(rows, w)],
        out_shape=[jax.ShapeDtypeStruct((s, w), F32), jax.ShapeDtypeStruct((s, w), BF16)],
        compiler_params=_params("parallel"),
    )(x, y, g_post, g_pre)


def _head(m, x2, tgt, g):
    s, w = m.shape
    rows = min(ROWS, s)

    def body(m_ref, x2_ref, t_ref, g_ref, dout_ref, dm_ref, dg_ref, loss_ref):
        mv = m_ref[...]
        gv = g_ref[...]
        out = x2_ref[...] + mv * _rstd(mv) * gv
        err = out - t_ref[...]
        dout = err * (1.0 / w)
        dout_ref[...] = dout
        dm, dgc = _rms_bwd(mv, gv, dout)
        dm_ref[...] = dm.astype(dm_ref.dtype)
        sq = err * err
        lanes = sq[:, 0:LANE]
        for j in range(1, w // LANE):
            lanes = lanes + sq[:, j * LANE:(j + 1) * LANE]
        step = pl.program_id(0)
        _accumulate(dg_ref, _sublane_sum(dgc), step)
        _accumulate(loss_ref, _sublane_sum(lanes) * (0.5 / w), step)

    return pl.pallas_call(
        body, name="head", grid=(s // rows,),
        in_specs=[_row_spec(rows, w), _row_spec(rows, w), _row_spec(rows, w), _fixed_spec(1, w)],
        out_specs=[_row_spec(rows, w), _row_spec(rows, w), _fixed_spec(SUBLANE, w), _fixed_spec(SUBLANE, LANE)],
        out_shape=[jax.ShapeDtypeStruct((s, w), F32), jax.ShapeDtypeStruct((s, w), BF16),
                   jax.ShapeDtypeStruct((SUBLANE, w), F32), jax.ShapeDtypeStruct((SUBLANE, LANE), F32)],
        compiler_params=_params("arbitrary"),
    )(m, x2, tgt, g)


def _mid_bwd(x2, y, d_out, d_h2, g_pre, g_post):
    s, w = x2.shape
    rows = min(ROWS, s)

    def body(x2_ref, y_ref, dout_ref, dh2_ref, gq_ref, gp_ref, dx2_ref, dy_ref, dgq_ref, dgp_ref):
        dx, dgq = _rms_bwd(x2_ref[...], gq_ref[...], dh2_ref[...])
        dx2 = dout_ref[...] + dx
        dx2_ref[...] = dx2
        dy, dgp = _rms_bwd(y_ref[...], gp_ref[...], dx2)
        dy_ref[...] = dy.astype(dy_ref.dtype)
        step = pl.program_id(0)
        _accumulate(dgq_ref, _sublane_sum(dgq), step)
        _accumulate(dgp_ref, _sublane_sum(dgp), step)

    return pl.pallas_call(
        body, name="mid_bwd", grid=(s // rows,),
        in_specs=[_row_spec(rows, w)] * 4 + [_fixed_spec(1, w)] * 2,
        out_specs=[_row_spec(rows, w), _row_spec(rows, w), _fixed_spec(SUBLANE, w), _fixed_spec(SUBLANE, w)],
        out_shape=[jax.ShapeDtypeStruct((s, w), F32), jax.ShapeDtypeStruct((s, w), BF16),
                   jax.ShapeDtypeStruct((SUBLANE, w), F32), jax.ShapeDtypeStruct((SUBLANE, w), F32)],
        compiler_params=_params("arbitrary"),
    )(x2, y, d_out, d_h2, g_pre, g_post)


def _first_bwd(x, g, d_h1, d_x2):
    s, w = x.shape
    rows = min(ROWS, s)

    def body(x_ref, g_ref, dh_ref, dx2_ref, dx_ref, dg_ref):
        dx, dgc = _rms_bwd(x_ref[...], g_ref[...], dh_ref[...])
        dx_ref[...] = dx2_ref[...] + dx
        _accumulate(dg_ref, _sublane_sum(dgc), pl.program_id(0))

    return pl.pallas_call(
        body, name="first_bwd", grid=(s // rows,),
        in_specs=[_row_spec(rows, w), _fixed_spec(1, w), _row_spec(rows, w), _row_spec(rows, w)],
        out_specs=[_row_spec(rows, w), _fixed_spec(SUBLANE, w)],
        out_shape=[jax.ShapeDtypeStruct((s, w), F32), jax.ShapeDtypeStruct((SUBLANE, w), F32)],
        compiler_params=_params("arbitrary"),
    )(x, g, d_h1, d_x2)


def _shift_down(v, k):
    t = lax.broadcasted_iota(jnp.int32, v.shape, 0)
    return jnp.where(t >= k, pltpu.roll(v, k, 0), 0.0)


def _shift_up(v, k):
    n = v.shape[0]
    t = lax.broadcasted_iota(jnp.int32, v.shape, 0)
    return jnp.where(t < n - k, pltpu.roll(v, n - k, 0), 0.0)


def _conv_core(u, b, c, w):
    z = c * u
    conv = w[0:1, :] * _shift_down(z, 2) + w[1:2, :] * _shift_down(z, 1) + w[2:3, :] * z
    return z, conv, b * conv


def _conv_fwd(proj, conv_w, g, n_groups):
    s = proj.shape[0]

    def body(u_ref, b_ref, c_ref, w_ref, g_ref, o_ref):
        _, _, yr = _conv_core(u_ref[...], b_ref[...], c_ref[...], w_ref[...])
        o_ref[...] = (yr * _rstd(yr) * g_ref[...]).astype(o_ref.dtype)

    col = lambda k: pl.BlockSpec((s, HEAD), lambda i: (0, k * n_groups + i))
    return pl.pallas_call(
        body, name="conv_fwd", grid=(n_groups,),
        in_specs=[col(0), col(1), col(2), pl.BlockSpec((3, HEAD), lambda i: (0, i)), pl.BlockSpec((1, HEAD), lambda i: (0, i))],
        out_specs=pl.BlockSpec((s, HEAD), lambda i: (0, i)),
        out_shape=jax.ShapeDtypeStruct((s, n_groups * HEAD), BF16),
        compiler_params=_params("parallel"),
    )(proj, proj, proj, conv_w, g)


def _conv_bwd(proj, d_mix, conv_w, g, n_groups):
    s = proj.shape[0]
    width = n_groups * HEAD

    def body(u_ref, b_ref, c_ref, dy_ref, w_ref, g_ref, du_ref, db_ref, dc_ref, dg_ref, dw_ref):
        u, b, c, w = u_ref[...], b_ref[...], c_ref[...], w_ref[...]
        z, conv, yr = _conv_core(u, b, c, w)
        dyr, dgc = _rms_bwd(yr, g_ref[...], dy_ref[...])
        dconv = dyr * b
        db_ref[...] = (dyr * conv).astype(db_ref.dtype)
        dz = w[2:3, :] * dconv + w[1:2, :] * _shift_up(dconv, 1) + w[0:1, :] * _shift_up(dconv, 2)
        dc_ref[...] = (dz * u).astype(dc_ref.dtype)
        du_ref[...] = (dz * c).astype(du_ref.dtype)
        dg_ref[...] = _sublane_sum(dgc)
        dw_ref[0] = _sublane_sum(dconv * _shift_down(z, 2))
        dw_ref[1] = _sublane_sum(dconv * _shift_down(z, 1))
        dw_ref[2] = _sublane_sum(dconv * z)

    col = lambda k: pl.BlockSpec((s, HEAD), lambda i: (0, k * n_groups + i))
    grp = pl.BlockSpec((s, HEAD), lambda i: (0, i))
    return pl.pallas_call(
        body, name="conv_bwd", grid=(n_groups,),
        in_specs=[col(0), col(1), col(2), grp, pl.BlockSpec((3, HEAD), lambda i: (0, i)), pl.BlockSpec((1, HEAD), lambda i: (0, i))],
        out_specs=[grp, grp, grp, pl.BlockSpec((SUBLANE, HEAD), lambda i: (0, i)),
                   pl.BlockSpec((3, SUBLANE, HEAD), lambda i: (0, 0, i))],
        out_shape=[jax.ShapeDtypeStruct((s, width), BF16)] * 3
        + [jax.ShapeDtypeStruct((SUBLANE, width), F32), jax.ShapeDtypeStruct((3, SUBLANE, width), F32)],
        compiler_params=_params("parallel"),
    )(proj, proj, proj, d_mix, conv_w, g)


def _rope_tables(s, n_heads):
    pos = jnp.arange(s, dtype=F32)
    inv_freq = jnp.power(ROPE_THETA, -jnp.arange(0, ROPE, 2, dtype=F32) / ROPE)
    ang = pos[:, None] * inv_freq[None, :]
    cos, sin = jnp.cos(ang), jnp.sin(ang)
    cs = jnp.concatenate([cos, cos], axis=1)
    sn = jnp.concatenate([-sin, sin], axis=1)
    pad = jnp.zeros((s, LANE - ROPE), F32)
    return (jnp.tile(cs, (1, n_heads)), jnp.tile(sn, (1, n_heads)),
            jnp.concatenate([cs, pad], axis=1), jnp.concatenate([sn, pad], axis=1))


def _swap_halves(v):
    w = v.shape[1]
    lane = lax.broadcasted_iota(jnp.int32, v.shape, 1)
    first = (lane % ROPE) < (ROPE // 2)
    return jnp.where(first, pltpu.roll(v, w - ROPE // 2, 1), pltpu.roll(v, ROPE // 2, 1))


def _pack_heads(q, kv, kr, tables, n_heads):
    s = q.shape[0]
    rows = min(ROWS, s)
    cq, sq, ck, sk = tables
    wq = n_heads * ROPE

    def body(q_ref, kv_ref, kr_ref, cq_ref, sq_ref, ck_ref, sk_ref, qo_ref, ko_ref, vo_ref):
        qr = q_ref[:, n_heads * HEAD:]
        qr = qr * cq_ref[...] + _swap_halves(qr) * sq_ref[...]
        krv = kr_ref[...]
        krv = krv * ck_ref[...] + _swap_halves(krv) * sk_ref[...]
        for h in range(n_heads):
            qo_ref[h] = jnp.concatenate([q_ref[:, h * HEAD:(h + 1) * HEAD], qr[:, h * ROPE:(h + 1) * ROPE]], axis=1).astype(BF16)
            ko_ref[h] = jnp.concatenate([kv_ref[:, 2 * h * HEAD:(2 * h + 1) * HEAD], krv[:, :ROPE]], axis=1).astype(BF16)
            vo_ref[h] = kv_ref[:, (2 * h + 1) * HEAD:(2 * h + 2) * HEAD].astype(BF16)

    hs = lambda w: pl.BlockSpec((n_heads, rows, w), lambda i: (0, i, 0))
    return pl.pallas_call(
        body, name="pack_heads", grid=(s // rows,),
        in_specs=[_row_spec(rows, q.shape[1]), _row_spec(rows, kv.shape[1]), _row_spec(rows, LANE),
                  _row_spec(rows, wq), _row_spec(rows, wq), _row_spec(rows, LANE), _row_spec(rows, LANE)],
        out_specs=[hs(QK), hs(QK), hs(HEAD)],
        out_shape=[jax.ShapeDtypeStruct((n_heads, s, QK), BF16), jax.ShapeDtypeStruct((n_heads, s, QK), BF16),
                   jax.ShapeDtypeStruct((n_heads, s, HEAD), BF16)],
        compiler_params=_params("parallel"),
    )(q, kv, kr, cq, sq, ck, sk)


def _unpack_heads(dq, dk, dv, tables, n_heads):
    s = dq.shape[1]
    rows = min(ROWS, s)
    cq, sq, ck, sk = tables
    wq = n_heads * ROPE

    def body(dq_ref, dk_ref, dv_ref, cq_ref, sq_ref, ck_ref, sk_ref, qo_ref, kvo_ref, kro_ref):
        dqr = jnp.concatenate([dq_ref[h][:, HEAD:] for h in range(n_heads)], axis=1)
        dqr = dqr * cq_ref[...] - _swap_halves(dqr) * sq_ref[...]
        dkr = dk_ref[0][:, HEAD:]
        for h in range(1, n_heads):
            dkr = dkr + dk_ref[h][:, HEAD:]
        dkr = jnp.concatenate([dkr, jnp.zeros((rows, LANE - ROPE), F32)], axis=1)
        dkr = dkr * ck_ref[...] - _swap_halves(dkr) * sk_ref[...]
        kro_ref[...] = dkr.astype(kro_ref.dtype)
        qo_ref[:, n_heads * HEAD:] = dqr.astype(qo_ref.dtype)
        for h in range(n_heads):
            qo_ref[:, h * HEAD:(h + 1) * HEAD] = dq_ref[h][:, :HEAD].astype(qo_ref.dtype)
            kvo_ref[:, 2 * h * HEAD:(2 * h + 1) * HEAD] = dk_ref[h][:, :HEAD].astype(kvo_ref.dtype)
            kvo_ref[:, (2 * h + 1) * HEAD:(2 * h + 2) * HEAD] = dv_ref[h].astype(kvo_ref.dtype)

    hs = lambda w: pl.BlockSpec((n_heads, rows, w), lambda i: (0, i, 0))
    return pl.pallas_call(
        body, name="unpack_heads", grid=(s // rows,),
        in_specs=[hs(QK), hs(QK), hs(HEAD), _row_spec(rows, wq), _row_spec(rows, wq), _row_spec(rows, LANE), _row_spec(rows, LANE)],
        out_specs=[_row_spec(rows, n_heads * QK), _row_spec(rows, 2 * n_heads * HEAD), _row_spec(rows, LANE)],
        out_shape=[jax.ShapeDtypeStruct((s, n_heads * QK), BF16), jax.ShapeDtypeStruct((s, 2 * n_heads * HEAD), BF16),
                   jax.ShapeDtypeStruct((s, LANE), BF16)],
        compiler_params=_params("parallel"),
    )(dq, dk, dv, cq, sq, ck, sk)


TQ = 256


def _probs(q, k, q0):
    sc = lax.dot_general(q, k, (NT, ((), ())), preferred_element_type=F32) * (QK ** -0.5)
    row = lax.broadcasted_iota(jnp.int32, sc.shape, 0) + q0
    colk = lax.broadcasted_iota(jnp.int32, sc.shape, 1)
    sc = jnp.where(colk // CHUNK <= row // CHUNK, sc, NEG_INF)
    e = jnp.exp(sc - jnp.max(sc, axis=-1, keepdims=True))
    return e / jnp.sum(e, axis=-1, keepdims=True)


def _attn_fwd(q, k, v, g):
    n_heads, s, _ = q.shape
    tq = min(TQ, s)

    def body(q_ref, k_ref, v_ref, g_ref, o_ref, y_ref):
        p = _probs(q_ref[...], k_ref[...], pl.program_id(1) * tq)
        o = jnp.dot(p.astype(BF16), v_ref[...], preferred_element_type=F32)
        o_ref[...] = o
        y_ref[...] = (o * _rstd(o) * g_ref[...]).astype(y_ref.dtype)

    return pl.pallas_call(
        body, name="attn_fwd", grid=(n_heads, s // tq),
        in_specs=[pl.BlockSpec((None, tq, QK), lambda h, i: (h, i, 0)),
                  pl.BlockSpec((None, s, QK), lambda h, i: (h, 0, 0)),
                  pl.BlockSpec((None, s, HEAD), lambda h, i: (h, 0, 0)),
                  pl.BlockSpec((1, HEAD), lambda h, i: (0, h))],
        out_specs=[pl.BlockSpec((None, tq, HEAD), lambda h, i: (h, i, 0)),
                   pl.BlockSpec((tq, HEAD), lambda h, i: (i, h))],
        out_shape=[jax.ShapeDtypeStruct((n_heads, s, HEAD), F32), jax.ShapeDtypeStruct((s, n_heads * HEAD), BF16)],
        compiler_params=_params("parallel", "parallel"),
    )(q, k, v, g)


def _attn_bwd(q, k, v, o, d_mix, g, col0):
    n_heads, s, _ = q.shape
    tq = min(TQ, s)

    def body(q_ref, k_ref, v_ref, o_ref, dy_ref, g_ref, dq_ref, dk_ref, dv_ref, dg_ref):
        i = pl.program_id(1)
        qv, kv_, vv = q_ref[...], k_ref[...], v_ref[...]
        do, dgc = _rms_bwd(o_ref[...], g_ref[...], dy_ref[...])
        do = do.astype(BF16)
        p = _probs(qv, kv_, i * tq)
        dp = lax.dot_general(do, vv, (NT, ((), ())), preferred_element_type=F32)
        ds = (p * (dp - jnp.sum(p * dp, axis=-1, keepdims=True)) * (QK ** -0.5)).astype(BF16)
        dq_ref[...] = jnp.dot(ds, kv_, preferred_element_type=F32)
        _accumulate(dk_ref, lax.dot_general(ds, qv, (TN, ((), ())), preferred_element_type=F32), i)
        _accumulate(dv_ref, lax.dot_general(p.astype(BF16), do, (TN, ((), ())), preferred_element_type=F32), i)
        _accumulate(dg_ref, _sublane_sum(dgc), i)

    c0 = col0 // HEAD
    return pl.pallas_call(
        body, name="attn_bwd", grid=(n_heads, s // tq),
        in_specs=[pl.BlockSpec((None, tq, QK), lambda h, i: (h, i, 0)),
                  pl.BlockSpec((None, s, QK), lambda h, i: (h, 0, 0)),
                  pl.BlockSpec((None, s, HEAD), lambda h, i: (h, 0, 0)),
                  pl.BlockSpec((None, tq, HEAD), lambda h, i: (h, i, 0)),
                  pl.BlockSpec((tq, HEAD), lambda h, i: (i, c0 + h)),
                  pl.BlockSpec((1, HEAD), lambda h, i: (0, h))],
        out_specs=[pl.BlockSpec((None, tq, QK), lambda h, i: (h, i, 0)),
                   pl.BlockSpec((None, s, QK), lambda h, i: (h, 0, 0)),
                   pl.BlockSpec((None, s, HEAD), lambda h, i: (h, 0, 0)),
                   pl.BlockSpec((SUBLANE, HEAD), lambda h, i: (0, h))],
        out_shape=[jax.ShapeDtypeStruct((n_heads, s, QK), F32), jax.ShapeDtypeStruct((n_heads, s, QK), F32),
                   jax.ShapeDtypeStruct((n_heads, s, HEAD), F32), jax.ShapeDtypeStruct((SUBLANE, n_heads * HEAD), F32)],
        compiler_params=_params("parallel", "arbitrary"),
    )(q, k, v, o, d_mix, g)


TILE_M = 1024
TILE_N = 1024


def _up_fwd(h2, w_up):
    s, d = h2.shape
    nb, _, fb = w_up.shape
    tm = min(TILE_M,s)

    def epilogue(acc):
        r = jnp.maximum(acc, 0.0)
        return r * r, r

    blk = pl.BlockSpec((tm, fb), lambda i, j: (i, j))
    return _matmul("up_fwd", h2, w_up, grid=(s // tm, nb),
                   a_spec=pl.BlockSpec((tm, d), lambda i, j: (i, 0)),
                   b_spec=pl.BlockSpec((None, d, fb), lambda i, j: (j, 0, 0)),
                   out_shape=[jax.ShapeDtypeStruct((s, nb * fb), BF16)] * 2, out_specs=[blk, blk],
                   contract=NN, epilogue=epilogue)


def _down_fwd(a, w_down):
    s, f = a.shape
    d = w_down.shape[1]
    tm, tn, tk = min(TILE_M,s), min(TILE_N,d), 2048
    nk = f // tk
    return _matmul("down_fwd", a, w_down, grid=(s // tm, d // tn, nk),
                   a_spec=pl.BlockSpec((tm, tk), lambda i, j, k: (i, k)),
                   b_spec=pl.BlockSpec((tk, tn), lambda i, j, k: (k, j)),
                   out_shape=jax.ShapeDtypeStruct((s, d), F32),
                   out_specs=pl.BlockSpec((tm, tn), lambda i, j, k: (i, j)),
                   contract=NN, nk=nk, acc_shape=(tm, tn))


def _down_bwd_act(d_m, w_down, r):
    s, d = d_m.shape
    f = w_down.shape[0]
    tm, tn = min(TILE_M,s), min(TILE_N,f)
    blk = pl.BlockSpec((tm, tn), lambda i, j: (i, j))
    return _matmul("down_bwd_act", d_m, w_down, grid=(s // tm, f // tn),
                   a_spec=pl.BlockSpec((tm, d), lambda i, j: (i, 0)),
                   b_spec=pl.BlockSpec((tn, d), lambda i, j: (j, 0)),
                   out_shape=jax.ShapeDtypeStruct((s, f), BF16), out_specs=blk, contract=NT,
                   extras=(r,), extra_specs=(blk,),
                   epilogue=lambda acc, rv: (acc * (2.0 * rv.astype(F32)),))


def _up_bwd_act(d_up, w_up):
    s, _ = d_up.shape
    nb, d, fb = w_up.shape
    tm, tn = min(TILE_M,s), min(TILE_N,d)
    return _matmul("up_bwd_act", d_up, w_up, grid=(s // tm, d // tn, nb),
                   a_spec=pl.BlockSpec((tm, fb), lambda i, j, k: (i, k)),
                   b_spec=pl.BlockSpec((None, tn, fb), lambda i, j, k: (k, j, 0)),
                   out_shape=jax.ShapeDtypeStruct((s, d), F32),
                   out_specs=pl.BlockSpec((tm, tn), lambda i, j, k: (i, j)),
                   contract=NT, nk=nb, acc_shape=(tm, tn))


def _up_bwd_w(h2, d_up, nb):
    s, d = h2.shape
    fb = d_up.shape[1] // nb
    tm = min(TILE_M,d)
    return _matmul("up_bwd_w", h2, d_up, grid=(d // tm, nb),
                   a_spec=pl.BlockSpec((s, tm), lambda i, j: (0, i)),
                   b_spec=pl.BlockSpec((s, fb), lambda i, j: (0, j)),
                   out_shape=jax.ShapeDtypeStruct((nb, d, fb), BF16),
                   out_specs=pl.BlockSpec((None, tm, fb), lambda i, j: (j, i, 0)), contract=TN)


def _local_step(x, tgt, gains, conv_w, w_in, w_uq, w_ukv, w_o, w_up, w_down):
    pre_mix_g, q_norm_g, kv_norm_g, conv_out_g, attn_out_g, post_mix_g, pre_mlp_g, post_mlp_g = gains
    s, d = x.shape
    conv_width = conv_w.shape[1]
    n_groups = conv_width // HEAD
    r_q, r_kv = w_uq.shape[0], w_ukv.shape[0]
    n_heads = w_uq.shape[1] // QK
    in_width = w_in.shape[1]
    c_q0 = 3 * conv_width
    c_kv0 = c_q0 + r_q
    c_kr0 = c_kv0 + r_kv
    in_pad = -(-in_width // LANE) * LANE
    tn_in = in_pad // 5 if in_pad % (5 * LANE) == 0 else LANE

    w_in_p = jnp.pad(w_in, ((0, 0), (0, in_pad - in_width)))
    wq3 = w_uq.reshape(r_q, n_heads, QK)
    w_uq_p = jnp.concatenate([wq3[:, :, :HEAD].reshape(r_q, n_heads * HEAD), wq3[:, :, HEAD:].reshape(r_q, n_heads * ROPE)], axis=1)
    tables = _rope_tables(s, n_heads)

    h1 = _rms_fwd("pre_mix_norm", x, pre_mix_g)
    proj = _mm_nn("in_proj", h1, w_in_p, F32, TILE_M,tn_in)
    y_conv = _conv_fwd(proj, conv_w, conv_out_g, n_groups)
    c_q = proj[:, c_q0:c_kv0]
    c_kv = proj[:, c_kv0:c_kr0]
    qn = _rms_fwd("q_norm", c_q, q_norm_g)
    kvn = _rms_fwd("kv_norm", c_kv, kv_norm_g)
    q = _mm_nn("q_up", qn, w_uq_p, F32, TILE_M, TILE_N)
    kv = _mm_nn("kv_up", kvn, w_ukv, F32, TILE_M, TILE_N)
    kr = proj[:, c_kr0:c_kr0 + LANE]
    qh, kh, vh = _pack_heads(q, kv, kr, tables, n_heads)
    o, y_attn = _attn_fwd(qh, kh, vh, attn_out_g)
    mix = jnp.concatenate([y_conv, y_attn], axis=1)
    y = _mm_nn("out_proj", mix, w_o, F32, TILE_M, TILE_N)
    x2, h2 = _mid_fwd(x, y, post_mix_g, pre_mlp_g)
    a, r = _up_fwd(h2, w_up)
    m = _down_fwd(a, w_down)

    d_out, d_m, dg_post_mlp, loss_part = _head(m, x2, tgt, post_mlp_g)
    d_up = _down_bwd_act(d_m, w_down, r)
    gw_down = _mm_tn("down_bwd_w", a, d_m, BF16, TILE_M, TILE_N)
    d_h2 = _up_bwd_act(d_up, w_up)
    gw_up = _up_bwd_w(h2, d_up, w_up.shape[0])
    d_x2, d_y, dg_pre_mlp, dg_post_mix = _mid_bwd(x2, y, d_out, d_h2, pre_mlp_g, post_mix_g)
    d_mix = _mm_nt("out_proj_bwd_act", d_y, w_o, F32, TILE_M, TILE_N)
    gw_o = _mm_tn("out_proj_bwd_w", mix, d_y, BF16, TILE_M, TILE_N)
    dqh, dkh, dvh, dg_attn = _attn_bwd(qh, kh, vh, o, d_mix, attn_out_g, conv_width)
    d_q, d_kv, d_kr = _unpack_heads(dqh, dkh, dvh, tables, n_heads)
    d_qn = _mm_nt("q_up_bwd_act", d_q, w_uq_p, F32, TILE_M, TILE_N)
    d_kvn = _mm_nt("kv_up_bwd_act", d_kv, w_ukv, F32, TILE_M, TILE_N)
    gw_uq_p = _mm_tn("q_up_bwd_w", qn, d_q, BF16, TILE_M, TILE_N)
    gw_ukv = _mm_tn("kv_up_bwd_w", kvn, d_kv, BF16, TILE_M, TILE_N)
    d_cq, dg_q = _rms_bwd_call("q_norm_bwd", c_q, q_norm_g, d_qn, BF16)
    d_ckv, dg_kv = _rms_bwd_call("kv_norm_bwd", c_kv, kv_norm_g, d_kvn, BF16)
    d_u, d_b, d_c, dg_conv, dw_conv = _conv_bwd(proj, d_mix, conv_w, conv_out_g, n_groups)
    d_proj = jnp.concatenate([d_u, d_b, d_c, d_cq, d_ckv, d_kr[:, :in_pad - c_kr0]], axis=1)
    d_h1 = _mm_nt("in_proj_bwd_act", d_proj, w_in_p, F32, TILE_M,512)
    gw_in_p = _mm_tn("in_proj_bwd_w", h1, d_proj, BF16, TILE_M, tn_in)
    grad_x, dg_pre_mix = _first_bwd(x, pre_mix_g, d_h1, d_x2)

    gw_in = gw_in_p[:, :in_width]
    gq_n = gw_uq_p[:, :n_heads * HEAD].reshape(r_q, n_heads, HEAD)
    gq_r = gw_uq_p[:, n_heads * HEAD:].reshape(r_q, n_heads, ROPE)
    gw_uq = jnp.concatenate([gq_n, gq_r], axis=2).reshape(r_q, n_heads * QK)

    small = [dg_pre_mix, dg_q, dg_kv, dg_conv, dg_attn, dg_post_mix, dg_pre_mlp, dg_post_mlp,
             dw_conv[0], dw_conv[1], dw_conv[2], loss_part]
    return grad_x, (gw_in, gw_uq, gw_ukv, gw_o, gw_up, gw_down), jnp.concatenate(small, axis=1)


ANY = pl.BlockSpec(memory_space=pl.ANY)


def _place():
    x, y, c = lax.axis_index("x"), lax.axis_index("y"), lax.axis_index("c")
    other_chips = [(1 - x, y), (x, 1 - y), (1 - x, 1 - y)]
    return x, y, c, other_chips


def _all_gather(shards):
    n = len(shards)

    def body(*refs):
        ins, outs = refs[:n], refs[n:2 * n]
        send_sems, recv_sems, local_sems = refs[2 * n:]
        x, y, c, chips = _place()
        me, sibling = (x, y, c), (x, y, 1 - c)

        def block(px, py, pc):
            return 4 * px + 2 * py + pc

        def copy(w, k, owner, to, src=None):
            dst = outs[w].at[block(*owner)]
            return pltpu.make_async_remote_copy(
                src_ref=dst if src is None else src, dst_ref=dst,
                send_sem=send_sems.at[w, k], recv_sem=recv_sems.at[w, k], device_id=to, device_id_type=MESH)

        mine = [pltpu.make_async_copy(ins[w], outs[w].at[block(*me)], local_sems.at[w]) for w in range(n)]
        for cp in mine:
            cp.start()
        first = []
        for w in range(n):
            first.append(copy(w, 0, me, sibling, src=ins[w]))
            first += [copy(w, 1 + j, me, (*chip, c), src=ins[w]) for j, chip in enumerate(chips)]
        for cp in first:
            cp.start()
        passed = []
        for j, chip in enumerate(chips):
            for w in range(n):
                copy(w, 1 + j, (*chip, c), me).wait_recv()
                fwd = copy(w, 4 + j, (*chip, c), sibling)
                fwd.start()
                passed.append(fwd)
        for w in range(n):
            copy(w, 0, sibling, me).wait_recv()
            for j, chip in enumerate(chips):
                copy(w, 4 + j, (*chip, 1 - c), me).wait_recv()
        for cp in first + passed:
            cp.wait_send()
        for cp in mine:
            cp.wait()

    return pl.pallas_call(
        body, name="all_gather_weights",
        in_specs=[ANY] * n, out_specs=[ANY] * n,
        out_shape=[jax.ShapeDtypeStruct((N_DEV, *a.shape), a.dtype) for a in shards],
        scratch_shapes=[pltpu.SemaphoreType.DMA((n, 7)), pltpu.SemaphoreType.DMA((n, 7)), pltpu.SemaphoreType.DMA((n,))],
    )(*shards)


def _pair_exchange(grads):
    n = len(grads)

    def body(*refs):
        ins, recv = refs[:n], refs[n:2 * n]
        send_sems, recv_sems = refs[2 * n:]
        x, y, c, _ = _place()
        sends = []
        for w in range(n):
            for k in range(N_CHIP):
                sends.append(pltpu.make_async_remote_copy(
                    src_ref=ins[w].at[2 * k + 1 - c], dst_ref=recv[w].at[k],
                    send_sem=send_sems.at[w, k], recv_sem=recv_sems.at[w, k],
                    device_id=(x, y, 1 - c), device_id_type=MESH))
        for cp in sends:
            cp.start()
        for cp in sends:
            cp.wait()

    return pl.pallas_call(
        body, name="grad_pair_exchange",
        in_specs=[ANY] * n, out_specs=[ANY] * n,
        out_shape=[jax.ShapeDtypeStruct((N_CHIP, *g.shape[1:]), g.dtype) for g in grads],
        scratch_shapes=[pltpu.SemaphoreType.DMA((n, N_CHIP))] * 2,
    )(*grads)


def _pair_sum(grad, received, core):
    _, r, c = received.shape
    rows = min(ROWS, r)
    assert r % rows == 0

    def body(core_ref, a_ref, b_ref, o_ref):
        o_ref[...] = (a_ref[...].astype(F32) + b_ref[...].astype(F32)).astype(o_ref.dtype)

    spec = pl.BlockSpec((None, rows, c), lambda k, i, core_ref: (k, i, 0))
    return pl.pallas_call(
        body, name="grad_pair_sum",
        grid_spec=pltpu.PrefetchScalarGridSpec(
            num_scalar_prefetch=1, grid=(N_CHIP, r // rows),
            in_specs=[pl.BlockSpec((None, None, rows, c), lambda k, i, core_ref: (k, core_ref[0], i, 0)), spec],
            out_specs=spec),
        out_shape=jax.ShapeDtypeStruct(received.shape, received.dtype),
        compiler_params=_params("parallel", "parallel"),
    )(core, grad.reshape(N_CHIP, 2, r, c), received)


def _chip_exchange(sums):
    n = len(sums)

    def body(*refs):
        ins, outs = refs[:n], refs[n:2 * n]
        send_sems, recv_sems, local_sems = refs[2 * n:]
        x, y, c, chips = _place()
        mine = 2 * x + y
        sends, locals_ = [], []
        for w in range(n):
            locals_.append(pltpu.make_async_copy(ins[w].at[mine], outs[w].at[mine], local_sems.at[w]))
            for j, (px, py) in enumerate(chips):
                sends.append(pltpu.make_async_remote_copy(
                    src_ref=ins[w].at[2 * px + py], dst_ref=outs[w].at[mine],
                    send_sem=send_sems.at[w, j], recv_sem=recv_sems.at[w, j],
                    device_id=(px, py, c), device_id_type=MESH))
        for cp in sends + locals_:
            cp.start()
        for cp in sends:
            cp.wait()
        for cp in locals_:
            cp.wait()

    return pl.pallas_call(
        body, name="grad_chip_exchange",
        in_specs=[ANY] * n, out_specs=[ANY] * n,
        out_shape=[jax.ShapeDtypeStruct(a.shape, a.dtype) for a in sums],
        scratch_shapes=[pltpu.SemaphoreType.DMA((n, 3)), pltpu.SemaphoreType.DMA((n, 3)), pltpu.SemaphoreType.DMA((n,))],
    )(*sums)


def _small_all_reduce(part):
    _, w = part.shape

    def body(p_ref, o_ref, buf, send_sems, recv_sems):
        x, y, c, _ = _place()
        me = 4 * x + 2 * y + c
        buf[me] = p_ref[...]
        copies = []
        for k in range(1, N_DEV):
            dx, dy, dc = (k >> 2) & 1, (k >> 1) & 1, k & 1
            copies.append(pltpu.make_async_remote_copy(
                src_ref=p_ref, dst_ref=buf.at[me], send_sem=send_sems.at[k - 1], recv_sem=recv_sems.at[k - 1],
                device_id=(x ^ dx, y ^ dy, c ^ dc), device_id_type=MESH))
        for cp in copies:
            cp.start()
        for cp in copies:
            cp.wait()
        tot = buf[0]
        for d in range(1, N_DEV):
            tot = tot + buf[d]
        tot = jnp.sum(tot, axis=0, keepdims=True)
        o_ref[...] = tot
        loss = jnp.sum(tot[:, w - LANE:], axis=1, keepdims=True)
        o_ref[:, w - LANE:] = jnp.broadcast_to(loss, (1, LANE))

    return pl.pallas_call(
        body, name="small_all_reduce",
        in_specs=[pl.BlockSpec(memory_space=pltpu.VMEM)], out_specs=pl.BlockSpec(memory_space=pltpu.VMEM),
        out_shape=jax.ShapeDtypeStruct((1, w), F32),
        scratch_shapes=[pltpu.VMEM((N_DEV, SUBLANE, w), F32), pltpu.SemaphoreType.DMA((N_DEV - 1,)), pltpu.SemaphoreType.DMA((N_DEV - 1,))],
        compiler_params=pltpu.CompilerParams(vmem_limit_bytes=VMEM_LIMIT_BYTES),
    )(part)


def _adamw(w, g, m, v):
    m = ADAM_B1 * m + (1.0 - ADAM_B1) * g
    v = ADAM_B2 * v + (1.0 - ADAM_B2) * (g * g)
    m_hat = m / (1.0 - ADAM_B1 ** ADAM_STEP)
    v_hat = v / (1.0 - ADAM_B2 ** ADAM_STEP)
    delta = -ADAM_LR * (m_hat / (jnp.sqrt(v_hat) + ADAM_EPS) + ADAM_WD * w)
    return delta, m, v


def _sum_adam(name, parts, w, m, v):
    r, c = w.shape
    rows = min(ROWS, r)
    assert r % rows == 0

    def body(p_ref, w_ref, m_ref, v_ref, g_ref, d_ref, mo_ref, vo_ref):
        g = p_ref[0].astype(F32)
        for k in range(1, N_CHIP):
            g = g + p_ref[k].astype(F32)
        g_ref[...] = g
        d_ref[...], mo_ref[...], vo_ref[...] = _adamw(w_ref[...], g, m_ref[...], v_ref[...])

    blk = pl.BlockSpec((rows, c), lambda i: (i, 0))
    out = jax.ShapeDtypeStruct((r, c), F32)
    return pl.pallas_call(
        body, name=name, grid=(r // rows,),
        in_specs=[pl.BlockSpec((N_CHIP, rows, c), lambda i: (0, i, 0)), blk, blk, blk],
        out_specs=[blk] * 4, out_shape=[out] * 4,
        compiler_params=_params("parallel"),
    )(parts, w, m, v)


def _adam_small(w, g, m, v):
    def body(w_ref, g_ref, m_ref, v_ref, d_ref, mo_ref, vo_ref):
        d_ref[...], mo_ref[...], vo_ref[...] = _adamw(w_ref[...], g_ref[...], m_ref[...], v_ref[...])

    out = jax.ShapeDtypeStruct(w.shape, F32)
    return pl.pallas_call(body, name="adam_small", out_shape=[out] * 3)(w, g, m, v)


def kernel(x, pre_mix_g, w_in, conv_w, q_norm_g, w_uq, kv_norm_g, w_ukv, conv_out_g, attn_out_g, w_o, post_mix_g, pre_mlp_g, w_up, w_down, post_mlp_g, loss_target, m_pre_mix_g, m_w_in, m_conv_w, m_q_norm_g, m_w_uq, m_kv_norm_g, m_w_ukv, m_conv_out_g, m_attn_out_g, m_w_o, m_post_mix_g, m_pre_mlp_g, m_w_up, m_w_down, m_post_mlp_g, v_pre_mix_g, v_w_in, v_conv_w, v_q_norm_g, v_w_uq, v_kv_norm_g, v_w_ukv, v_conv_out_g, v_attn_out_g, v_w_o, v_post_mix_g, v_pre_mlp_g, v_w_up, v_w_down, v_post_mlp_g):
    me = 4 * lax.axis_index("x") + 2 * lax.axis_index("y") + lax.axis_index("c")
    gains = (pre_mix_g, q_norm_g, kv_norm_g, conv_out_g, attn_out_g, post_mix_g, pre_mlp_g, post_mlp_g)
    gain_m = (m_pre_mix_g, m_q_norm_g, m_kv_norm_g, m_conv_out_g, m_attn_out_g, m_post_mix_g, m_pre_mlp_g, m_post_mlp_g)
    gain_v = (v_pre_mix_g, v_q_norm_g, v_kv_norm_g, v_conv_out_g, v_attn_out_g, v_post_mix_g, v_pre_mlp_g, v_post_mlp_g)
    big = (w_in[0], w_uq[0], w_ukv[0], w_o[0], w_up[0], w_down[0])
    big_m = (m_w_in[0], m_w_uq[0], m_w_ukv[0], m_w_o[0], m_w_up[0], m_w_down[0])
    big_v = (v_w_in[0], v_w_uq[0], v_w_ukv[0], v_w_o[0], v_w_up[0], v_w_down[0])

    taps = jnp.pad(conv_w[0], ((0, SUBLANE - conv_w.shape[1]), (0, 0)))
    full = _all_gather([b.astype(BF16) for b in big] + [taps])
    cols = lambda a: a.transpose(1, 0, 2).reshape(a.shape[1], N_DEV * a.shape[2])
    rows = lambda a: a.reshape(N_DEV * a.shape[1], a.shape[2])
    conv_full = cols(full[6])[:conv_w.shape[1]]

    grad_x, gws, small = _local_step(x[0], loss_target[0], gains, conv_full,
                                     cols(full[0]), cols(full[1]), cols(full[2]), rows(full[3]), full[4], rows(full[5]))

    col_blocks = lambda g: g.reshape(g.shape[0], N_DEV, g.shape[1] // N_DEV).transpose(1, 0, 2)
    row_blocks = lambda g: g.reshape(N_DEV, g.shape[0] // N_DEV, g.shape[1])
    blocks = [col_blocks(gws[0]), col_blocks(gws[1]), col_blocks(gws[2]), row_blocks(gws[3]), gws[4], row_blocks(gws[5])]
    received = _pair_exchange(blocks)
    core = lax.axis_index("c").astype(jnp.int32).reshape(1)
    sums = [_pair_sum(g, r, core) for g, r in zip(blocks, received)]
    parts = _chip_exchange(sums)
    total = _small_all_reduce(small)

    names = ("w_in", "w_uq", "w_ukv", "w_o", "w_up", "w_down")
    big_out = [_sum_adam("adam_" + nm, p, w, m, v) for nm, p, w, m, v in zip(names, parts, big, big_m, big_v)]

    widths = [g.shape[1] for g in gains]
    offs = [sum(widths[:i]) for i in range(len(widths) + 1)]
    gain_grads = [total[:, offs[i]:offs[i + 1]] for i in range(len(widths))]
    cw = conv_w.shape[2]
    conv_total = conv_full.shape[1]
    conv_grad = jnp.concatenate(
        [lax.dynamic_slice_in_dim(total[:, offs[-1] + t * conv_total:offs[-1] + (t + 1) * conv_total], me * cw, cw, axis=1)
         for t in range(conv_w.shape[1])], axis=1)
    flat = lambda a: a.reshape(1, -1)
    pack = lambda gs, cv: jnp.concatenate([*gs, flat(cv)], axis=1)
    small_g = jnp.concatenate([*gain_grads, conv_grad], axis=1)
    small_d, small_m, small_v = _adam_small(pack(gains, conv_w), small_g, pack(gain_m, m_conv_w), pack(gain_v, v_conv_w))

    def unpack(row):
        gs = [row[:, offs[i]:offs[i + 1]] for i in range(len(widths))]
        return gs, row[:, offs[-1]:].reshape(conv_w.shape)

    loss = total[0, total.shape[1] - 1]
    order = (0, "w_in", "conv", 1, "w_uq", 2, "w_ukv", 3, 4, "w_o", 5, 6, "w_up", "w_down", 7)

    def assemble(gain_list, conv_item, big_list):
        by_name = dict(zip(names, big_list))
        out = []
        for item in order:
            if item == "conv":
                out.append(conv_item)
            elif isinstance(item, int):
                out.append(gain_list[item])
            else:
                out.append(by_name[item][None])
        return out

    g_gains, g_conv = unpack(small_g)
    d_gains, d_conv = unpack(small_d)
    m_gains, m_conv = unpack(small_m)
    v_gains, v_conv = unpack(small_v)
    outs = [loss, grad_x[None]]
    outs += assemble(g_gains, g_conv, [o[0] for o in big_out])
    outs += assemble(d_gains, d_conv, [o[1] for o in big_out])
    outs += assemble(m_gains, m_conv, [o[2] for o in big_out])
    outs += assemble(v_gains, v_conv, [o[3] for o in big_out])
    return tuple(outs)
```

```python
import jax
import jax.numpy as jnp
from jax import lax
from jax.experimental import pallas as pl
from jax.experimental.pallas import tpu as pltpu

F32 = jnp.float32
BF16 = jnp.bfloat16

EPS = 1e-6
NEG_INF = -1e30
HEAD = 128
ROPE = 64
QK = HEAD + ROPE
CHUNK = 64
ROPE_THETA = 10000.0
ADAM_LR, ADAM_B1, ADAM_B2, ADAM_EPS, ADAM_WD, ADAM_STEP = 0.001, 0.9, 0.999, 1e-08, 0.01, 10

LANE = 128
SUBLANE = 8
VMEM_LIMIT_BYTES = 56 * 1024 * 1024

N_DEV = 8
N_CHIP = 4
MESH = pl.DeviceIdType.MESH


def _params(*sem):
    return pltpu.CompilerParams(dimension_semantics=sem, vmem_limit_bytes=VMEM_LIMIT_BYTES)


ANY = pl.BlockSpec(memory_space=pl.ANY)


def _call(body, *, in_specs, after=(), **kw):
    n_in, n_after = len(in_specs), len(after)

    def ordered(*refs):
        body(*refs[:n_in], *refs[n_in + n_after:])

    call = pl.pallas_call(ordered, in_specs=[*in_specs, *[ANY] * n_after], **kw)
    return lambda *operands: call(*operands, *after)


def _sublane_sum(v):
    r, w = v.shape
    return jnp.sum(v.reshape(r // SUBLANE, SUBLANE, w), axis=0)


def _rstd(x):
    return lax.rsqrt(jnp.mean(x * x, axis=-1, keepdims=True) + EPS)


def _rms_bwd(x, g, dy):
    r = _rstd(x)
    xh = x * r
    dxh = dy * g
    dx = r * (dxh - xh * jnp.mean(dxh * xh, axis=-1, keepdims=True))
    return dx, dy * xh


def _accumulate(ref, val, step):
    @pl.when(step == 0)
    def _():
        ref[...] = val

    @pl.when(step > 0)
    def _():
        ref[...] += val


NN = ((1,), (0,))
NT = ((1,), (1,))
TN = ((0,), (0,))


def _matmul(name, a, b, *, grid, a_spec, b_spec, out_shape, out_specs, contract, nk=1, acc_shape=None,
            extras=(), extra_specs=(), epilogue=None, after=()):
    multi = isinstance(out_shape, (tuple, list))
    out_shapes = tuple(out_shape) if multi else (out_shape,)
    n_out = len(out_shapes)
    n_extra = len(extras)

    def body(a_ref, b_ref, *rest):
        x_refs = rest[:n_extra]
        o_refs = rest[n_extra:n_extra + n_out]

        def emit(acc):
            vals = epilogue(acc, *[r[...] for r in x_refs]) if epilogue else (acc,)
            for r, v in zip(o_refs, vals):
                r[...] = v.astype(r.dtype)

        p = lax.dot_general(a_ref[...], b_ref[...], (contract, ((), ())), preferred_element_type=F32)
        if nk == 1:
            emit(p)
        else:
            acc_ref = rest[n_extra + n_out]
            k = pl.program_id(2)
            _accumulate(acc_ref, p, k)

            @pl.when(k == nk - 1)
            def _():
                emit(acc_ref[...])

    sem = ("parallel", "parallel") + (("arbitrary",) if nk > 1 else ())
    return _call(
        body, name=name, grid=grid, after=after,
        in_specs=[a_spec, b_spec, *extra_specs],
        out_specs=out_specs,
        out_shape=out_shape,
        scratch_shapes=[pltpu.VMEM(acc_shape, F32)] if nk > 1 else [],
        compiler_params=_params(*sem),
    )(a, b, *extras)


def _fit(n, tile):
    if n <= tile:
        return n
    t = tile - tile % LANE
    while n % t:
        t -= LANE
    return t


def _mm_nn(name, a, b, out_dtype, tm, tn):
    m, k = a.shape
    n = b.shape[1]
    tm, tn = _fit(m, tm), _fit(n, tn)
    return _matmul(name, a, b, grid=(m // tm, n // tn),
                   a_spec=pl.BlockSpec((tm, k), lambda i, j: (i, 0)),
                   b_spec=pl.BlockSpec((k, tn), lambda i, j: (0, j)),
                   out_shape=jax.ShapeDtypeStruct((m, n), out_dtype),
                   out_specs=pl.BlockSpec((tm, tn), lambda i, j: (i, j)), contract=NN)


def _mm_nt(name, a, b, out_dtype, tm, tn):
    m, k = a.shape
    n = b.shape[0]
    tm, tn = _fit(m, tm), _fit(n, tn)
    return _matmul(name, a, b, grid=(m // tm, n // tn),
                   a_spec=pl.BlockSpec((tm, k), lambda i, j: (i, 0)),
                   b_spec=pl.BlockSpec((tn, k), lambda i, j: (j, 0)),
                   out_shape=jax.ShapeDtypeStruct((m, n), out_dtype),
                   out_specs=pl.BlockSpec((tm, tn), lambda i, j: (i, j)), contract=NT)


def _mm_tn(name, a, b, out_dtype, tm, tn):
    s, m = a.shape
    n = b.shape[1]
    tm, tn = _fit(m, tm), _fit(n, tn)
    return _matmul(name, a, b, grid=(m // tm, n // tn),
                   a_spec=pl.BlockSpec((s, tm), lambda i, j: (0, i)),
                   b_spec=pl.BlockSpec((s, tn), lambda i, j: (0, j)),
                   out_shape=jax.ShapeDtypeStruct((m, n), out_dtype),
                   out_specs=pl.BlockSpec((tm, tn), lambda i, j: (i, j)), contract=TN)


ROWS = 256


def _row_spec(rows, width):
    return pl.BlockSpec((rows, width), lambda i: (i, 0))


def _fixed_spec(rows, width):
    return pl.BlockSpec((rows, width), lambda i: (0, 0))


def _rms_fwd(name, x, g, after=()):
    s, w = x.shape
    rows = min(ROWS, s)

    def body(x_ref, g_ref, o_ref):
        xv = x_ref[...]
        o_ref[...] = (xv * _rstd(xv) * g_ref[...]).astype(o_ref.dtype)

    return _call(
        body, name=name, grid=(s // rows,), after=after,
        in_specs=[_row_spec(rows, w), _fixed_spec(1, w)],
        out_specs=_row_spec(rows, w),
        out_shape=jax.ShapeDtypeStruct((s, w), BF16),
        compiler_params=_params("parallel"),
    )(x, g)


def _rms_bwd_call(name, x, g, dy, out_dtype, after=()):
    s, w = x.shape
    rows = min(ROWS, s)

    def body(x_ref, g_ref, dy_ref, dx_ref, dg_ref):
        dx, dgc = _rms_bwd(x_ref[...], g_ref[...], dy_ref[...].astype(F32))
        dx_ref[...] = dx.astype(dx_ref.dtype)
        _accumulate(dg_ref, _sublane_sum(dgc), pl.program_id(0))

    return _call(
        body, name=name, grid=(s // rows,), after=after,
        in_specs=[_row_spec(rows, w), _fixed_spec(1, w), _row_spec(rows, w)],
        out_specs=[_row_spec(rows, w), _fixed_spec(SUBLANE, w)],
        out_shape=[jax.ShapeDtypeStruct((s, w), out_dtype), jax.ShapeDtypeStruct((SUBLANE, w), F32)],
        compiler_params=_params("arbitrary"),
    )(x, g, dy)


def _mid_fwd(x, y, g_post, g_pre):
    s, w = x.shape
    rows = min(ROWS, s)

    def body(x_ref, y_ref, gp_ref, gq_ref, x2_ref, h2_ref):
        yv = y_ref[...]
        x2 = x_ref[...] + yv * _rstd(yv) * gp_ref[...]
        x2_ref[...] = x2
        h2_ref[...] = (x2 * _rstd(x2) * gq_ref[...]).astype(h2_ref.dtype)

    return pl.pallas_call(
        body, name="mid_fwd", grid=(s // rows,),
        in_specs=[_row_spec(rows, w), _row_spec(rows, w), _fixed_spec(1, w), _fixed_spec(1, w)],
        out_specs=[_row_spec(rows, w), _row_spec(rows, w)],
        out_shape=[jax.ShapeDtypeStruct((s, w), F32), jax.ShapeDtypeStruct((s, w), BF16)],
        compiler_params=_params("parallel"),
    )(x, y, g_post, g_pre)


def _head(m, x2, tgt, g):
    s, w = m.shape
    rows = min(ROWS, s)

    def body(m_ref, x2_ref, t_ref, g_ref, dout_ref, dm_ref, dg_ref, loss_ref):
        mv = m_ref[...]
        gv = g_ref[...]
        out = x2_ref[...] + mv * _rstd(mv) * gv
        err = out - t_ref[...]
        dout = err * (1.0 / w)
        dout_ref[...] = dout
        dm, dgc = _rms_bwd(mv, gv, dout)
        dm_ref[...] = dm.astype(dm_ref.dtype)
        sq = err * err
        lanes = sq[:, 0:LANE]
        for j in range(1, w // LANE):
            lanes = lanes + sq[:, j * LANE:(j + 1) * LANE]
        step = pl.program_id(0)
        _accumulate(dg_ref, _sublane_sum(dgc), step)
        _accumulate(loss_ref, _sublane_sum(lanes) * (0.5 / w), step)

    return pl.pallas_call(
        body, name="head", grid=(s // rows,),
        in_specs=[_row_spec(rows, w), _row_spec(rows, w), _row_spec(rows, w), _fixed_spec(1, w)],
        out_specs=[_row_spec(rows, w), _row_spec(rows, w), _fixed_spec(SUBLANE, w), _fixed_spec(SUBLANE, LANE)],
        out_shape=[jax.ShapeDtypeStruct((s, w), F32), jax.ShapeDtypeStruct((s, w), BF16),
                   jax.ShapeDtypeStruct((SUBLANE, w), F32), jax.ShapeDtypeStruct((SUBLANE, LANE), F32)],
        compiler_params=_params("arbitrary"),
    )(m, x2, tgt, g)


def _mid_bwd(x2, y, d_out, d_h2, g_pre, g_post, after=()):
    s, w = x2.shape
    rows = min(ROWS, s)

    def body(x2_ref, y_ref, dout_ref, dh2_ref, gq_ref, gp_ref, dx2_ref, dy_ref, dgq_ref, dgp_ref):
        dx, dgq = _rms_bwd(x2_ref[...], gq_ref[...], dh2_ref[...])
        dx2 = dout_ref[...] + dx
        dx2_ref[...] = dx2
        dy, dgp = _rms_bwd(y_ref[...], gp_ref[...], dx2)
        dy_ref[...] = dy.astype(dy_ref.dtype)
        step = pl.program_id(0)
        _accumulate(dgq_ref, _sublane_sum(dgq), step)
        _accumulate(dgp_ref, _sublane_sum(dgp), step)

    return _call(
        body, name="mid_bwd", grid=(s // rows,), after=after,
        in_specs=[_row_spec(rows, w)] * 4 + [_fixed_spec(1, w)] * 2,
        out_specs=[_row_spec(rows, w), _row_spec(rows, w), _fixed_spec(SUBLANE, w), _fixed_spec(SUBLANE, w)],
        out_shape=[jax.ShapeDtypeStruct((s, w), F32), jax.ShapeDtypeStruct((s, w), BF16),
                   jax.ShapeDtypeStruct((SUBLANE, w), F32), jax.ShapeDtypeStruct((SUBLANE, w), F32)],
        compiler_params=_params("arbitrary"),
    )(x2, y, d_out, d_h2, g_pre, g_post)


def _first_bwd(x, g, d_h1, d_x2, after=()):
    s, w = x.shape
    rows = min(ROWS, s)

    def body(x_ref, g_ref, dh_ref, dx2_ref, dx_ref, dg_ref):
        dx, dgc = _rms_bwd(x_ref[...], g_ref[...], dh_ref[...])
        dx_ref[...] = dx2_ref[...] + dx
        _accumulate(dg_ref, _sublane_sum(dgc), pl.program_id(0))

    return _call(
        body, name="first_bwd", grid=(s // rows,), after=after,
        in_specs=[_row_spec(rows, w), _fixed_spec(1, w), _row_spec(rows, w), _row_spec(rows, w)],
        out_specs=[_row_spec(rows, w), _fixed_spec(SUBLANE, w)],
        out_shape=[jax.ShapeDtypeStruct((s, w), F32), jax.ShapeDtypeStruct((SUBLANE, w), F32)],
        compiler_params=_params("arbitrary"),
    )(x, g, d_h1, d_x2)


def _shift_down(v, k):
    t = lax.broadcasted_iota(jnp.int32, v.shape, 0)
    return jnp.where(t >= k, pltpu.roll(v, k, 0), 0.0)


def _shift_up(v, k):
    n = v.shape[0]
    t = lax.broadcasted_iota(jnp.int32, v.shape, 0)
    return jnp.where(t < n - k, pltpu.roll(v, n - k, 0), 0.0)


def _conv_core(u, b, c, w):
    z = c * u
    conv = w[0:1, :] * _shift_down(z, 2) + w[1:2, :] * _shift_down(z, 1) + w[2:3, :] * z
    return z, conv, b * conv


def _conv_fwd(proj, conv_w, g, n_groups):
    s = proj.shape[0]

    def body(u_ref, b_ref, c_ref, w_ref, g_ref, o_ref):
        _, _, yr = _conv_core(u_ref[...], b_ref[...], c_ref[...], w_ref[...])
        o_ref[...] = (yr * _rstd(yr) * g_ref[...]).astype(o_ref.dtype)

    col = lambda k: pl.BlockSpec((s, HEAD), lambda i: (0, k * n_groups + i))
    return pl.pallas_call(
        body, name="conv_fwd", grid=(n_groups,),
        in_specs=[col(0), col(1), col(2), pl.BlockSpec((3, HEAD), lambda i: (0, i)), pl.BlockSpec((1, HEAD), lambda i: (0, i))],
        out_specs=pl.BlockSpec((s, HEAD), lambda i: (0, i)),
        out_shape=jax.ShapeDtypeStruct((s, n_groups * HEAD), BF16),
        compiler_params=_params("parallel"),
    )(proj, proj, proj, conv_w, g)


def _conv_bwd(proj, d_mix, conv_w, g, n_groups):
    s = proj.shape[0]
    width = n_groups * HEAD

    def body(u_ref, b_ref, c_ref, dy_ref, w_ref, g_ref, du_ref, db_ref, dc_ref, dg_ref, dw_ref):
        u, b, c, w = u_ref[...], b_ref[...], c_ref[...], w_ref[...]
        z, conv, yr = _conv_core(u, b, c, w)
        dyr, dgc = _rms_bwd(yr, g_ref[...], dy_ref[...])
        dconv = dyr * b
        db_ref[...] = (dyr * conv).astype(db_ref.dtype)
        dz = w[2:3, :] * dconv + w[1:2, :] * _shift_up(dconv, 1) + w[0:1, :] * _shift_up(dconv, 2)
        dc_ref[...] = (dz * u).astype(dc_ref.dtype)
        du_ref[...] = (dz * c).astype(du_ref.dtype)
        dg_ref[...] = _sublane_sum(dgc)
        dw_ref[0] = _sublane_sum(dconv * _shift_down(z, 2))
        dw_ref[1] = _sublane_sum(dconv * _shift_down(z, 1))
        dw_ref[2] = _sublane_sum(dconv * z)

    col = lambda k: pl.BlockSpec((s, HEAD), lambda i: (0, k * n_groups + i))
    grp = pl.BlockSpec((s, HEAD), lambda i: (0, i))
    return pl.pallas_call(
        body, name="conv_bwd", grid=(n_groups,),
        in_specs=[col(0), col(1), col(2), grp, pl.BlockSpec((3, HEAD), lambda i: (0, i)), pl.BlockSpec((1, HEAD), lambda i: (0, i))],
        out_specs=[grp, grp, grp, pl.BlockSpec((SUBLANE, HEAD), lambda i: (0, i)),
                   pl.BlockSpec((3, SUBLANE, HEAD), lambda i: (0, 0, i))],
        out_shape=[jax.ShapeDtypeStruct((s, width), BF16)] * 3
        + [jax.ShapeDtypeStruct((SUBLANE, width), F32), jax.ShapeDtypeStruct((3, SUBLANE, width), F32)],
        compiler_params=_params("parallel"),
    )(proj, proj, proj, d_mix, conv_w, g)


def _rope_tables(s, n_heads):
    pos = jnp.arange(s, dtype=F32)
    inv_freq = jnp.power(ROPE_THETA, -jnp.arange(0, ROPE, 2, dtype=F32) / ROPE)
    ang = pos[:, None] * inv_freq[None, :]
    cos, sin = jnp.cos(ang), jnp.sin(ang)
    cs = jnp.concatenate([cos, cos], axis=1)
    sn = jnp.concatenate([-sin, sin], axis=1)
    pad = jnp.zeros((s, LANE - ROPE), F32)
    return (jnp.tile(cs, (1, n_heads)), jnp.tile(sn, (1, n_heads)),
            jnp.concatenate([cs, pad], axis=1), jnp.concatenate([sn, pad], axis=1))


def _swap_halves(v):
    w = v.shape[1]
    lane = lax.broadcasted_iota(jnp.int32, v.shape, 1)
    first = (lane % ROPE) < (ROPE // 2)
    return jnp.where(first, pltpu.roll(v, w - ROPE // 2, 1), pltpu.roll(v, ROPE // 2, 1))


def _pack_heads(q, kv, kr, tables, n_heads):
    s = q.shape[0]
    rows = min(ROWS, s)
    cq, sq, ck, sk = tables
    wq = n_heads * ROPE

    def body(q_ref, kv_ref, kr_ref, cq_ref, sq_ref, ck_ref, sk_ref, qo_ref, ko_ref, vo_ref):
        qr = q_ref[:, n_heads * HEAD:]
        qr = qr * cq_ref[...] + _swap_halves(qr) * sq_ref[...]
        krv = kr_ref[...]
        krv = krv * ck_ref[...] + _swap_halves(krv) * sk_ref[...]
        for h in range(n_heads):
            qo_ref[h] = jnp.concatenate([q_ref[:, h * HEAD:(h + 1) * HEAD], qr[:, h * ROPE:(h + 1) * ROPE]], axis=1).astype(BF16)
            ko_ref[h] = jnp.concatenate([kv_ref[:, 2 * h * HEAD:(2 * h + 1) * HEAD], krv[:, :ROPE]], axis=1).astype(BF16)
            vo_ref[h] = kv_ref[:, (2 * h + 1) * HEAD:(2 * h + 2) * HEAD].astype(BF16)

    hs = lambda w: pl.BlockSpec((n_heads, rows, w), lambda i: (0, i, 0))
    return pl.pallas_call(
        body, name="pack_heads", grid=(s // rows,),
        in_specs=[_row_spec(rows, q.shape[1]), _row_spec(rows, kv.shape[1]), _row_spec(rows, LANE),
                  _row_spec(rows, wq), _row_spec(rows, wq), _row_spec(rows, LANE), _row_spec(rows, LANE)],
        out_specs=[hs(QK), hs(QK), hs(HEAD)],
        out_shape=[jax.ShapeDtypeStruct((n_heads, s, QK), BF16), jax.ShapeDtypeStruct((n_heads, s, QK), BF16),
                   jax.ShapeDtypeStruct((n_heads, s, HEAD), BF16)],
        compiler_params=_params("parallel"),
    )(q, kv, kr, cq, sq, ck, sk)


def _unpack_heads(dq, dk, dv, tables, n_heads):
    s = dq.shape[1]
    rows = min(ROWS, s)
    cq, sq, ck, sk = tables
    wq = n_heads * ROPE

    def body(dq_ref, dk_ref, dv_ref, cq_ref, sq_ref, ck_ref, sk_ref, qo_ref, kvo_ref, kro_ref):
        dqr = jnp.concatenate([dq_ref[h][:, HEAD:] for h in range(n_heads)], axis=1)
        dqr = dqr * cq_ref[...] - _swap_halves(dqr) * sq_ref[...]
        dkr = dk_ref[0][:, HEAD:]
        for h in range(1, n_heads):
            dkr = dkr + dk_ref[h][:, HEAD:]
        dkr = jnp.concatenate([dkr, jnp.zeros((rows, LANE - ROPE), F32)], axis=1)
        dkr = dkr * ck_ref[...] - _swap_halves(dkr) * sk_ref[...]
        kro_ref[...] = dkr.astype(kro_ref.dtype)
        qo_ref[:, n_heads * HEAD:] = dqr.astype(qo_ref.dtype)
        for h in range(n_heads):
            qo_ref[:, h * HEAD:(h + 1) * HEAD] = dq_ref[h][:, :HEAD].astype(qo_ref.dtype)
            kvo_ref[:, 2 * h * HEAD:(2 * h + 1) * HEAD] = dk_ref[h][:, :HEAD].astype(kvo_ref.dtype)
            kvo_ref[:, (2 * h + 1) * HEAD:(2 * h + 2) * HEAD] = dv_ref[h].astype(kvo_ref.dtype)

    hs = lambda w: pl.BlockSpec((n_heads, rows, w), lambda i: (0, i, 0))
    return pl.pallas_call(
        body, name="unpack_heads", grid=(s // rows,),
        in_specs=[hs(QK), hs(QK), hs(HEAD), _row_spec(rows, wq), _row_spec(rows, wq), _row_spec(rows, LANE), _row_spec(rows, LANE)],
        out_specs=[_row_spec(rows, n_heads * QK), _row_spec(rows, 2 * n_heads * HEAD), _row_spec(rows, LANE)],
        out_shape=[jax.ShapeDtypeStruct((s, n_heads * QK), BF16), jax.ShapeDtypeStruct((s, 2 * n_heads * HEAD), BF16),
                   jax.ShapeDtypeStruct((s, LANE), BF16)],
        compiler_params=_params("parallel"),
    )(dq, dk, dv, cq, sq, ck, sk)


TQ = 256


def _probs(q, k, q0):
    sc = lax.dot_general(q, k, (NT, ((), ())), preferred_element_type=F32) * (QK ** -0.5)
    row = lax.broadcasted_iota(jnp.int32, sc.shape, 0) + q0
    colk = lax.broadcasted_iota(jnp.int32, sc.shape, 1)
    sc = jnp.where(colk // CHUNK <= row // CHUNK, sc, NEG_INF)
    e = jnp.exp(sc - jnp.max(sc, axis=-1, keepdims=True))
    return e / jnp.sum(e, axis=-1, keepdims=True)


def _attn_fwd(q, k, v, g):
    n_heads, s, _ = q.shape
    tq = min(TQ, s)

    def body(q_ref, k_ref, v_ref, g_ref, o_ref, y_ref):
        p = _probs(q_ref[...], k_ref[...], pl.program_id(1) * tq)
        o = jnp.dot(p.astype(BF16), v_ref[...], preferred_element_type=F32)
        o_ref[...] = o
        y_ref[...] = (o * _rstd(o) * g_ref[...]).astype(y_ref.dtype)

    return pl.pallas_call(
        body, name="attn_fwd", grid=(n_heads, s // tq),
        in_specs=[pl.BlockSpec((None, tq, QK), lambda h, i: (h, i, 0)),
                  pl.BlockSpec((None, s, QK), lambda h, i: (h, 0, 0)),
                  pl.BlockSpec((None, s, HEAD), lambda h, i: (h, 0, 0)),
                  pl.BlockSpec((1, HEAD), lambda h, i: (0, h))],
        out_specs=[pl.BlockSpec((None, tq, HEAD), lambda h, i: (h, i, 0)),
                   pl.BlockSpec((tq, HEAD), lambda h, i: (i, h))],
        out_shape=[jax.ShapeDtypeStruct((n_heads, s, HEAD), F32), jax.ShapeDtypeStruct((s, n_heads * HEAD), BF16)],
        compiler_params=_params("parallel", "parallel"),
    )(q, k, v, g)


def _attn_bwd(q, k, v, o, d_mix, g, col0, after=()):
    n_heads, s, _ = q.shape
    tq = min(TQ, s)

    def body(q_ref, k_ref, v_ref, o_ref, dy_ref, g_ref, dq_ref, dk_ref, dv_ref, dg_ref):
        i = pl.program_id(1)
        qv, kv_, vv = q_ref[...], k_ref[...], v_ref[...]
        do, dgc = _rms_bwd(o_ref[...], g_ref[...], dy_ref[...])
        do = do.astype(BF16)
        p = _probs(qv, kv_, i * tq)
        dp = lax.dot_general(do, vv, (NT, ((), ())), preferred_element_type=F32)
        ds = (p * (dp - jnp.sum(p * dp, axis=-1, keepdims=True)) * (QK ** -0.5)).astype(BF16)
        dq_ref[...] = jnp.dot(ds, kv_, preferred_element_type=F32)
        _accumulate(dk_ref, lax.dot_general(ds, qv, (TN, ((), ())), preferred_element_type=F32), i)
        _accumulate(dv_ref, lax.dot_general(p.astype(BF16), do, (TN, ((), ())), preferred_element_type=F32), i)
        _accumulate(dg_ref, _sublane_sum(dgc), i)

    c0 = col0 // HEAD
    return _call(
        body, name="attn_bwd", grid=(n_heads, s // tq), after=after,
        in_specs=[pl.BlockSpec((None, tq, QK), lambda h, i: (h, i, 0)),
                  pl.BlockSpec((None, s, QK), lambda h, i: (h, 0, 0)),
                  pl.BlockSpec((None, s, HEAD), lambda h, i: (h, 0, 0)),
                  pl.BlockSpec((None, tq, HEAD), lambda h, i: (h, i, 0)),
                  pl.BlockSpec((tq, HEAD), lambda h, i: (i, c0 + h)),
                  pl.BlockSpec((1, HEAD), lambda h, i: (0, h))],
        out_specs=[pl.BlockSpec((None, tq, QK), lambda h, i: (h, i, 0)),
                   pl.BlockSpec((None, s, QK), lambda h, i: (h, 0, 0)),
                   pl.BlockSpec((None, s, HEAD), lambda h, i: (h, 0, 0)),
                   pl.BlockSpec((SUBLANE, HEAD), lambda h, i: (0, h))],
        out_shape=[jax.ShapeDtypeStruct((n_heads, s, QK), F32), jax.ShapeDtypeStruct((n_heads, s, QK), F32),
                   jax.ShapeDtypeStruct((n_heads, s, HEAD), F32), jax.ShapeDtypeStruct((SUBLANE, n_heads * HEAD), F32)],
        compiler_params=_params("parallel", "arbitrary"),
    )(q, k, v, o, d_mix, g)


TILE_M = 1024
TILE_N = 1024


def _up_fwd(h2, w_up):
    s, d = h2.shape
    nb, _, fb = w_up.shape
    tm = min(TILE_M,s)

    def epilogue(acc):
        r = jnp.maximum(acc, 0.0)
        return r * r, r

    blk = pl.BlockSpec((tm, fb), lambda i, j: (i, j))
    return _matmul("up_fwd", h2, w_up, grid=(s // tm, nb),
                   a_spec=pl.BlockSpec((tm, d), lambda i, j: (i, 0)),
                   b_spec=pl.BlockSpec((None, d, fb), lambda i, j: (j, 0, 0)),
                   out_shape=[jax.ShapeDtypeStruct((s, nb * fb), BF16)] * 2, out_specs=[blk, blk],
                   contract=NN, epilogue=epilogue)


def _down_fwd(a, w_down):
    s, f = a.shape
    d = w_down.shape[1]
    tm, tn, tk = min(TILE_M,s), min(TILE_N,d), 2048
    nk = f // tk
    return _matmul("down_fwd", a, w_down, grid=(s // tm, d // tn, nk),
                   a_spec=pl.BlockSpec((tm, tk), lambda i, j, k: (i, k)),
                   b_spec=pl.BlockSpec((tk, tn), lambda i, j, k: (k, j)),
                   out_shape=jax.ShapeDtypeStruct((s, d), F32),
                   out_specs=pl.BlockSpec((tm, tn), lambda i, j, k: (i, j)),
                   contract=NN, nk=nk, acc_shape=(tm, tn))


def _down_bwd_act(d_m, w_down, r):
    s, d = d_m.shape
    f = w_down.shape[0]
    tm, tn = min(TILE_M,s), min(TILE_N,f)
    blk = pl.BlockSpec((tm, tn), lambda i, j: (i, j))
    return _matmul("down_bwd_act", d_m, w_down, grid=(s // tm, f // tn),
                   a_spec=pl.BlockSpec((tm, d), lambda i, j: (i, 0)),
                   b_spec=pl.BlockSpec((tn, d), lambda i, j: (j, 0)),
                   out_shape=jax.ShapeDtypeStruct((s, f), BF16), out_specs=blk, contract=NT,
                   extras=(r,), extra_specs=(blk,),
                   epilogue=lambda acc, rv: (acc * (2.0 * rv.astype(F32)),))


def _up_bwd_act(d_up, w_up, after=()):
    s, _ = d_up.shape
    nb, d, fb = w_up.shape
    tm, tn = min(TILE_M,s), min(TILE_N,d)
    return _matmul("up_bwd_act", d_up, w_up, grid=(s // tm, d // tn, nb), after=after,
                   a_spec=pl.BlockSpec((tm, fb), lambda i, j, k: (i, k)),
                   b_spec=pl.BlockSpec((None, tn, fb), lambda i, j, k: (k, j, 0)),
                   out_shape=jax.ShapeDtypeStruct((s, d), F32),
                   out_specs=pl.BlockSpec((tm, tn), lambda i, j, k: (i, j)),
                   contract=NT, nk=nb, acc_shape=(tm, tn))


def _up_bwd_w(h2, d_up, nb):
    s, d = h2.shape
    fb = d_up.shape[1] // nb
    tm = min(TILE_M,d)
    return _matmul("up_bwd_w", h2, d_up, grid=(d // tm, nb),
                   a_spec=pl.BlockSpec((s, tm), lambda i, j: (0, i)),
                   b_spec=pl.BlockSpec((s, fb), lambda i, j: (0, j)),
                   out_shape=jax.ShapeDtypeStruct((nb, d, fb), BF16),
                   out_specs=pl.BlockSpec((None, tm, fb), lambda i, j: (j, i, 0)), contract=TN)


def _in_pad(in_width):
    return -(-in_width // LANE) * LANE


def _permute_q_cols(w_uq, n_heads):
    r = w_uq.shape[0]
    w3 = w_uq.reshape(r, n_heads, QK)
    return jnp.concatenate([w3[:, :, :HEAD].reshape(r, n_heads * HEAD), w3[:, :, HEAD:].reshape(r, n_heads * ROPE)], axis=1)


def _unpermute_q_cols(w, n_heads):
    r = w.shape[0]
    nope = w[:, :n_heads * HEAD].reshape(r, n_heads, HEAD)
    rope = w[:, n_heads * HEAD:].reshape(r, n_heads, ROPE)
    return jnp.concatenate([nope, rope], axis=2).reshape(r, n_heads * QK)


def _local_step(x, tgt, gains, weights, grads, first_after=()):
    pre_mix_g, q_norm_g, kv_norm_g, conv_out_g, attn_out_g, post_mix_g, pre_mlp_g, post_mlp_g = gains
    s, d = x.shape
    conv_width = conv_out_g.shape[1]
    n_groups = conv_width // HEAD
    r_q, r_kv = q_norm_g.shape[1], kv_norm_g.shape[1]
    n_heads = attn_out_g.shape[1] // HEAD
    c_q0 = 3 * conv_width
    c_kv0 = c_q0 + r_q
    c_kr0 = c_kv0 + r_kv
    in_pad = _in_pad(c_kr0 + ROPE)
    tn_in = in_pad // 5 if in_pad % (5 * LANE) == 0 else LANE
    tables = _rope_tables(s, n_heads)

    h1 = _rms_fwd("pre_mix_norm", x, pre_mix_g, after=first_after)
    w_in_p, conv_w = weights(0, (h1,))
    proj = _mm_nn("in_proj", h1, w_in_p, F32, TILE_M,tn_in)
    y_conv = _conv_fwd(proj, conv_w, conv_out_g, n_groups)
    c_q = proj[:, c_q0:c_kv0]
    c_kv = proj[:, c_kv0:c_kr0]
    qn = _rms_fwd("q_norm", c_q, q_norm_g)
    kvn = _rms_fwd("kv_norm", c_kv, kv_norm_g)
    w_uq_p, w_ukv, w_o = weights(1, (y_conv, qn, kvn))
    q = _mm_nn("q_up", qn, w_uq_p, F32, TILE_M, TILE_N)
    kv = _mm_nn("kv_up", kvn, w_ukv, F32, TILE_M, TILE_N)
    kr = proj[:, c_kr0:c_kr0 + LANE]
    qh, kh, vh = _pack_heads(q, kv, kr, tables, n_heads)
    o, y_attn = _attn_fwd(qh, kh, vh, attn_out_g)
    mix = jnp.concatenate([y_conv, y_attn], axis=1)
    y = _mm_nn("out_proj", mix, w_o, F32, TILE_M, TILE_N)
    x2, h2 = _mid_fwd(x, y, post_mix_g, pre_mlp_g)
    (w_up,) = weights(2, (h2,))
    a, r = _up_fwd(h2, w_up)
    (w_down,) = weights(3, (a,))
    m = _down_fwd(a, w_down)

    d_out, d_m, dg_post_mlp, loss_part = _head(m, x2, tgt, post_mlp_g)
    d_up = _down_bwd_act(d_m, w_down, r)
    gw_down = _mm_tn("down_bwd_w", a, d_m, BF16, TILE_M, TILE_N)
    d_h2 = _up_bwd_act(d_up, w_up, after=grads(0, (gw_down,)))
    gw_up = _up_bwd_w(h2, d_up, w_up.shape[0])
    d_x2, d_y, dg_pre_mlp, dg_post_mix = _mid_bwd(x2, y, d_out, d_h2, pre_mlp_g, post_mix_g, after=grads(1, (gw_up,)))
    d_mix = _mm_nt("out_proj_bwd_act", d_y, w_o, F32, TILE_M, TILE_N)
    gw_o = _mm_tn("out_proj_bwd_w", mix, d_y, BF16, TILE_M, TILE_N)
    dqh, dkh, dvh, dg_attn = _attn_bwd(qh, kh, vh, o, d_mix, attn_out_g, conv_width, after=grads(2, (gw_o,)))
    d_q, d_kv, d_kr = _unpack_heads(dqh, dkh, dvh, tables, n_heads)
    d_qn = _mm_nt("q_up_bwd_act", d_q, w_uq_p, F32, TILE_M, TILE_N)
    d_kvn = _mm_nt("kv_up_bwd_act", d_kv, w_ukv, F32, TILE_M, TILE_N)
    gw_uq_p = _mm_tn("q_up_bwd_w", qn, d_q, BF16, TILE_M, TILE_N)
    gw_ukv = _mm_tn("kv_up_bwd_w", kvn, d_kv, BF16, TILE_M, TILE_N)
    d_cq, dg_q = _rms_bwd_call("q_norm_bwd", c_q, q_norm_g, d_qn, BF16, after=grads(3, (gw_uq_p, gw_ukv)))
    d_ckv, dg_kv = _rms_bwd_call("kv_norm_bwd", c_kv, kv_norm_g, d_kvn, BF16)
    d_u, d_b, d_c, dg_conv, dw_conv = _conv_bwd(proj, d_mix, conv_w, conv_out_g, n_groups)
    d_proj = jnp.concatenate([d_u, d_b, d_c, d_cq, d_ckv, d_kr[:, :in_pad - c_kr0]], axis=1)
    d_h1 = _mm_nt("in_proj_bwd_act", d_proj, w_in_p, F32, TILE_M,512)
    gw_in_p = _mm_tn("in_proj_bwd_w", h1, d_proj, BF16, TILE_M, tn_in)
    grad_x, dg_pre_mix = _first_bwd(x, pre_mix_g, d_h1, d_x2, after=grads(4, (gw_in_p,)))

    small = [dg_pre_mix, dg_q, dg_kv, dg_conv, dg_attn, dg_post_mix, dg_pre_mlp, dg_post_mlp,
             dw_conv[0], dw_conv[1], dw_conv[2], loss_part]
    return grad_x, jnp.concatenate(small, axis=1)


HBM = pl.BlockSpec(memory_space=pltpu.HBM)
SEM = pl.BlockSpec(memory_space=pltpu.SEMAPHORE)
IN_VMEM = pl.BlockSpec(memory_space=pltpu.VMEM)
SPLIT = pltpu.CompilerParams(has_side_effects=pltpu.SideEffectType.DATAFLOW_SIDE_EFFECTING)


def _in_hbm(a):
    return pltpu.with_memory_space_constraint(a, pltpu.HBM)


def _hbm_like(a):
    return pltpu.HBM(a.shape, a.dtype)


def _place():
    x, y, c = lax.axis_index("x"), lax.axis_index("y"), lax.axis_index("c")
    other_chips = [(1 - x, y), (x, 1 - y), (1 - x, 1 - y)]
    return x, y, c, other_chips


def _block(px, py, pc):
    return 4 * px + 2 * py + pc


def _gather_start(shards, groups):
    n, ng = len(shards), len(groups)
    lands = [lax.empty((N_DEV, *a.shape), a.dtype) for a in shards]

    def body(*refs):
        src, land = refs[:n], refs[n:2 * n]
        sems, token = refs[2 * n:2 * n + 2 * ng], refs[-1]
        x, y, c, chips = _place()
        targets = [(x, y, 1 - c)] + [(*chip, c) for chip in chips]
        for gi, group in enumerate(groups):
            for i, w in enumerate(group):
                for k, to in enumerate(targets):
                    pltpu.make_async_remote_copy(
                        src_ref=src[w], dst_ref=land[w].at[_block(x, y, c)],
                        send_sem=sems[2 * gi].at[4 * i + k], recv_sem=sems[2 * gi + 1].at[4 * i + k],
                        device_id=to, device_id_type=MESH).start()
        token[...] = jnp.zeros_like(token)

    sem_shapes = [pltpu.SemaphoreType.DMA((4 * len(g),)) for g in groups for _ in range(2)]
    out = pl.pallas_call(
        body, name="gather_start",
        in_specs=[HBM] * (2 * n),
        out_specs=[SEM] * (2 * ng) + [HBM] * (2 * n) + [IN_VMEM],
        out_shape=sem_shapes + [_hbm_like(a) for a in shards] + [_hbm_like(a) for a in lands]
        + [jax.ShapeDtypeStruct((SUBLANE, LANE), F32)],
        input_output_aliases={i: 2 * ng + i for i in range(2 * n)},
        compiler_params=SPLIT,
    )(*[_in_hbm(a) for a in shards], *[_in_hbm(a) for a in lands])
    sems = [(out[2 * gi], out[2 * gi + 1]) for gi in range(ng)]
    return sems, out[2 * ng:2 * ng + n], out[2 * ng + n:2 * ng + 2 * n], out[-1]


def _gather_forward(name, shards, lands, send1, recv1, after):
    n = len(lands)

    def body(*refs):
        src, land = refs[:n], refs[n:2 * n]
        s1, r1 = refs[2 * n], refs[2 * n + 1]
        s2, r2 = refs[2 * n + 2 + len(after)], refs[2 * n + 3 + len(after)]
        x, y, c, chips = _place()
        me, sibling = (x, y, c), (x, y, 1 - c)
        for j, chip in enumerate(chips):
            for i in range(n):
                blk = land[i].at[_block(*chip, c)]
                pltpu.make_async_remote_copy(src_ref=blk, dst_ref=blk, send_sem=s1.at[4 * i + 1 + j], recv_sem=r1.at[4 * i + 1 + j],
                                             device_id=me, device_id_type=MESH).wait_recv()
                pltpu.make_async_remote_copy(src_ref=blk, dst_ref=blk, send_sem=s2.at[3 * i + j], recv_sem=r2.at[3 * i + j],
                                             device_id=sibling, device_id_type=MESH).start()
        for i in range(n):
            blk = land[i].at[_block(x, y, 1 - c)]
            pltpu.make_async_remote_copy(src_ref=blk, dst_ref=blk, send_sem=s1.at[4 * i], recv_sem=r1.at[4 * i],
                                         device_id=me, device_id_type=MESH).wait_recv()
            for k in range(4):
                pltpu.make_async_remote_copy(src_ref=src[i], dst_ref=land[i].at[_block(x, y, c)], send_sem=s1.at[4 * i + k],
                                             recv_sem=r1.at[4 * i + k], device_id=sibling, device_id_type=MESH).wait_send()

    sem = pltpu.SemaphoreType.DMA((3 * n,))
    out = pl.pallas_call(
        body, name=name,
        in_specs=[HBM] * (2 * n) + [SEM, SEM] + [ANY] * len(after),
        out_specs=[SEM, SEM] + [HBM] * n,
        out_shape=[sem, sem] + [_hbm_like(a) for a in lands],
        input_output_aliases={n + i: 2 + i for i in range(n)},
        compiler_params=SPLIT,
    )(*shards, *lands, send1, recv1, *after)
    return (out[0], out[1]), out[2:]


def _gather_wait(name, lands, send2, recv2):
    n = len(lands)

    def body(*refs):
        land, s2, r2 = refs[:n], refs[n], refs[n + 1]
        x, y, c, chips = _place()
        me = (x, y, c)
        for i in range(n):
            for j, chip in enumerate(chips):
                got = land[i].at[_block(*chip, 1 - c)]
                pltpu.make_async_remote_copy(src_ref=got, dst_ref=got, send_sem=s2.at[3 * i + j], recv_sem=r2.at[3 * i + j],
                                             device_id=me, device_id_type=MESH).wait_recv()
                sent = land[i].at[_block(*chip, c)]
                pltpu.make_async_remote_copy(src_ref=sent, dst_ref=sent, send_sem=s2.at[3 * i + j], recv_sem=r2.at[3 * i + j],
                                             device_id=me, device_id_type=MESH).wait_send()

    return pl.pallas_call(
        body, name=name,
        in_specs=[HBM] * n + [SEM, SEM], out_specs=[HBM] * n, out_shape=[_hbm_like(a) for a in lands],
        input_output_aliases={i: i for i in range(n)},
        compiler_params=SPLIT,
    )(*lands, send2, recv2)


def _pair_exchange(name, grads):
    n = len(grads)

    def body(*refs):
        ins, recv = refs[:n], refs[n:2 * n]
        send_sems, recv_sems = refs[2 * n:]
        x, y, c, _ = _place()
        sends = []
        for w in range(n):
            for k in range(N_CHIP):
                sends.append(pltpu.make_async_remote_copy(
                    src_ref=ins[w].at[2 * k + 1 - c], dst_ref=recv[w].at[k],
                    send_sem=send_sems.at[w, k], recv_sem=recv_sems.at[w, k],
                    device_id=(x, y, 1 - c), device_id_type=MESH))
        for cp in sends:
            cp.start()
        for cp in sends:
            cp.wait()

    return pl.pallas_call(
        body, name=name,
        in_specs=[ANY] * n, out_specs=[ANY] * n,
        out_shape=[jax.ShapeDtypeStruct((N_CHIP, *g.shape[1:]), g.dtype) for g in grads],
        scratch_shapes=[pltpu.SemaphoreType.DMA((n, N_CHIP))] * 2,
    )(*grads)


def _pair_sum(name, grad, received, core):
    _, r, c = received.shape
    rows = min(ROWS, r)
    assert r % rows == 0

    def body(core_ref, a_ref, b_ref, o_ref):
        o_ref[...] = (a_ref[...].astype(F32) + b_ref[...].astype(F32)).astype(o_ref.dtype)

    spec = pl.BlockSpec((None, rows, c), lambda k, i, core_ref: (k, i, 0))
    return pl.pallas_call(
        body, name=name,
        grid_spec=pltpu.PrefetchScalarGridSpec(
            num_scalar_prefetch=1, grid=(N_CHIP, r // rows),
            in_specs=[pl.BlockSpec((None, None, rows, c), lambda k, i, core_ref: (k, core_ref[0], i, 0)), spec],
            out_specs=spec),
        out_shape=jax.ShapeDtypeStruct(received.shape, received.dtype),
        compiler_params=_params("parallel", "parallel"),
    )(core, grad.reshape(N_CHIP, 2, r, c), received)


def _chip_send_start(name, sums):
    n = len(sums)
    lands = [lax.empty(a.shape, a.dtype) for a in sums]

    def body(*refs):
        src, land = refs[:n], refs[n:2 * n]
        send, recv, token = refs[2 * n], refs[2 * n + 1], refs[-1]
        x, y, c, chips = _place()
        for w in range(n):
            for j, (px, py) in enumerate(chips):
                pltpu.make_async_remote_copy(
                    src_ref=src[w].at[2 * px + py], dst_ref=land[w].at[2 * x + y],
                    send_sem=send.at[3 * w + j], recv_sem=recv.at[3 * w + j],
                    device_id=(px, py, c), device_id_type=MESH).start()
        token[...] = jnp.zeros_like(token)

    sem = pltpu.SemaphoreType.DMA((3 * n,))
    out = pl.pallas_call(
        body, name=name,
        in_specs=[HBM] * (2 * n),
        out_specs=[SEM, SEM] + [HBM] * (2 * n) + [IN_VMEM],
        out_shape=[sem, sem] + [_hbm_like(a) for a in sums] + [_hbm_like(a) for a in lands]
        + [jax.ShapeDtypeStruct((SUBLANE, LANE), F32)],
        input_output_aliases={i: 2 + i for i in range(2 * n)},
        compiler_params=SPLIT,
    )(*[_in_hbm(a) for a in sums], *[_in_hbm(a) for a in lands])
    return (out[0], out[1]), out[2:2 + n], out[2 + n:2 + 2 * n], out[-1]


def _chip_send_wait(groups, after):
    counts = [len(g[1]) for g in groups]
    n = sum(counts)

    def body(*refs):
        src, land = refs[:n], refs[n:2 * n]
        sems = refs[2 * n:2 * n + 2 * len(groups)]
        x, y, c, chips = _place()
        w = 0
        for gi, count in enumerate(counts):
            for i in range(count):
                for j, (px, py) in enumerate(chips):
                    pltpu.make_async_remote_copy(
                        src_ref=src[w].at[2 * px + py], dst_ref=land[w].at[2 * px + py],
                        send_sem=sems[2 * gi].at[3 * i + j], recv_sem=sems[2 * gi + 1].at[3 * i + j],
                        device_id=(px, py, c), device_id_type=MESH).wait()
                w += 1

    sums = [a for g in groups for a in g[1]]
    lands = [a for g in groups for a in g[2]]
    sems = [s for g in groups for s in g[0]]
    return pl.pallas_call(
        body, name="chip_send_wait",
        in_specs=[HBM] * (2 * n) + [SEM] * len(sems) + [ANY] * len(after),
        out_specs=[HBM] * n, out_shape=[_hbm_like(a) for a in lands],
        input_output_aliases={n + i: i for i in range(n)},
        compiler_params=SPLIT,
    )(*sums, *lands, *sems, *after)


def _small_all_reduce(part):
    _, w = part.shape

    def body(p_ref, o_ref, buf, send_sems, recv_sems):
        x, y, c, _ = _place()
        me = 4 * x + 2 * y + c
        buf[me] = p_ref[...]
        copies = []
        for k in range(1, N_DEV):
            dx, dy, dc = (k >> 2) & 1, (k >> 1) & 1, k & 1
            copies.append(pltpu.make_async_remote_copy(
                src_ref=p_ref, dst_ref=buf.at[me], send_sem=send_sems.at[k - 1], recv_sem=recv_sems.at[k - 1],
                device_id=(x ^ dx, y ^ dy, c ^ dc), device_id_type=MESH))
        for cp in copies:
            cp.start()
        for cp in copies:
            cp.wait()
        tot = buf[0]
        for d in range(1, N_DEV):
            tot = tot + buf[d]
        tot = jnp.sum(tot, axis=0, keepdims=True)
        o_ref[...] = tot
        loss = jnp.sum(tot[:, w - LANE:], axis=1, keepdims=True)
        o_ref[:, w - LANE:] = jnp.broadcast_to(loss, (1, LANE))

    return pl.pallas_call(
        body, name="small_all_reduce",
        in_specs=[pl.BlockSpec(memory_space=pltpu.VMEM)], out_specs=pl.BlockSpec(memory_space=pltpu.VMEM),
        out_shape=jax.ShapeDtypeStruct((1, w), F32),
        scratch_shapes=[pltpu.VMEM((N_DEV, SUBLANE, w), F32), pltpu.SemaphoreType.DMA((N_DEV - 1,)), pltpu.SemaphoreType.DMA((N_DEV - 1,))],
        compiler_params=pltpu.CompilerParams(vmem_limit_bytes=VMEM_LIMIT_BYTES),
    )(part)


def _adamw(w, g, m, v):
    m = ADAM_B1 * m + (1.0 - ADAM_B1) * g
    v = ADAM_B2 * v + (1.0 - ADAM_B2) * (g * g)
    m_hat = m / (1.0 - ADAM_B1 ** ADAM_STEP)
    v_hat = v / (1.0 - ADAM_B2 ** ADAM_STEP)
    delta = -ADAM_LR * (m_hat / (jnp.sqrt(v_hat) + ADAM_EPS) + ADAM_WD * w)
    return delta, m, v


def _sum_adam(name, parts, sums, chip, w, m, v):
    r, c = w.shape
    rows = min(ROWS, r)
    assert r % rows == 0

    def body(chip_ref, p_ref, own_ref, w_ref, m_ref, v_ref, g_ref, d_ref, mo_ref, vo_ref):
        g = None
        for k in range(N_CHIP):
            term = jnp.where(chip_ref[0] == k, own_ref[...], p_ref[k]).astype(F32)
            g = term if g is None else g + term
        g_ref[...] = g
        d_ref[...], mo_ref[...], vo_ref[...] = _adamw(w_ref[...], g, m_ref[...], v_ref[...])

    blk = pl.BlockSpec((rows, c), lambda i, chip_ref: (i, 0))
    out = jax.ShapeDtypeStruct((r, c), F32)
    return pl.pallas_call(
        body, name=name,
        grid_spec=pltpu.PrefetchScalarGridSpec(
            num_scalar_prefetch=1, grid=(r // rows,),
            in_specs=[pl.BlockSpec((N_CHIP, rows, c), lambda i, chip_ref: (0, i, 0)),
                      pl.BlockSpec((None, rows, c), lambda i, chip_ref: (chip_ref[0], i, 0)), blk, blk, blk],
            out_specs=[blk] * 4),
        out_shape=[out] * 4,
        compiler_params=_params("parallel"),
    )(chip, parts, sums, w, m, v)


def _adam_small(w, g, m, v):
    def body(w_ref, g_ref, m_ref, v_ref, d_ref, mo_ref, vo_ref):
        d_ref[...], mo_ref[...], vo_ref[...] = _adamw(w_ref[...], g_ref[...], m_ref[...], v_ref[...])

    out = jax.ShapeDtypeStruct(w.shape, F32)
    return pl.pallas_call(body, name="adam_small", out_shape=[out] * 3)(w, g, m, v)


def kernel(x, pre_mix_g, w_in, conv_w, q_norm_g, w_uq, kv_norm_g, w_ukv, conv_out_g, attn_out_g, w_o, post_mix_g, pre_mlp_g, w_up, w_down, post_mlp_g, loss_target, m_pre_mix_g, m_w_in, m_conv_w, m_q_norm_g, m_w_uq, m_kv_norm_g, m_w_ukv, m_conv_out_g, m_attn_out_g, m_w_o, m_post_mix_g, m_pre_mlp_g, m_w_up, m_w_down, m_post_mlp_g, v_pre_mix_g, v_w_in, v_conv_w, v_q_norm_g, v_w_uq, v_kv_norm_g, v_w_ukv, v_conv_out_g, v_attn_out_g, v_w_o, v_post_mix_g, v_pre_mlp_g, v_w_up, v_w_down, v_post_mlp_g):
    me = 4 * lax.axis_index("x") + 2 * lax.axis_index("y") + lax.axis_index("c")
    core = lax.axis_index("c").astype(jnp.int32).reshape(1)
    chip = (2 * lax.axis_index("x") + lax.axis_index("y")).astype(jnp.int32).reshape(1)
    gains = (pre_mix_g, q_norm_g, kv_norm_g, conv_out_g, attn_out_g, post_mix_g, pre_mlp_g, post_mlp_g)
    gain_m = (m_pre_mix_g, m_q_norm_g, m_kv_norm_g, m_conv_out_g, m_attn_out_g, m_post_mix_g, m_pre_mlp_g, m_post_mlp_g)
    gain_v = (v_pre_mix_g, v_q_norm_g, v_kv_norm_g, v_conv_out_g, v_attn_out_g, v_post_mix_g, v_pre_mlp_g, v_post_mlp_g)
    names = ("w_in", "w_uq", "w_ukv", "w_o", "w_up", "w_down")
    big = dict(zip(names, (w_in[0], w_uq[0], w_ukv[0], w_o[0], w_up[0], w_down[0])))
    big_m = dict(zip(names, (m_w_in[0], m_w_uq[0], m_w_ukv[0], m_w_o[0], m_w_up[0], m_w_down[0])))
    big_v = dict(zip(names, (v_w_in[0], v_w_uq[0], v_w_ukv[0], v_w_o[0], v_w_up[0], v_w_down[0])))
    n_heads = attn_out_g.shape[1] // HEAD
    n_taps = conv_w.shape[1]

    gathered = ("w_in", "conv", "w_uq", "w_ukv", "w_o", "w_up", "w_down")
    gather_groups = ((0, 1), (2, 3, 4), (5,), (6,))
    taps = jnp.pad(conv_w[0], ((0, SUBLANE - n_taps), (0, 0)))
    shards = [taps if nm == "conv" else big[nm].astype(BF16) for nm in gathered]
    sems1, shards, lands, token = _gather_start(shards, gather_groups)

    cols = lambda a: a.transpose(1, 0, 2).reshape(a.shape[1], N_DEV * a.shape[2])
    rows = lambda a: a.reshape(N_DEV * a.shape[1], a.shape[2])
    ready = {
        "w_in": lambda a: jnp.pad(cols(a), ((0, 0), (0, _in_pad(N_DEV * a.shape[2]) - N_DEV * a.shape[2]))),
        "conv": lambda a: cols(a)[:n_taps],
        "w_uq": lambda a: _permute_q_cols(cols(a), n_heads),
        "w_ukv": cols, "w_o": rows, "w_up": lambda a: a, "w_down": rows,
    }

    def weights(group, after):
        idx = gather_groups[group]
        sems2, mid = _gather_forward(f"gather_forward_{group}", [shards[i] for i in idx], [lands[i] for i in idx],
                                     *sems1[group], after)
        full = _gather_wait(f"gather_wait_{group}", mid, *sems2)
        out = []
        for i, a in zip(idx, full):
            a = lax.dynamic_update_index_in_dim(a, shards[i], me, 0)
            out.append(ready[gathered[i]](a))
        return out

    col_blocks = lambda g: g.reshape(g.shape[0], N_DEV, g.shape[1] // N_DEV).transpose(1, 0, 2)
    row_blocks = lambda g: g.reshape(N_DEV, g.shape[0] // N_DEV, g.shape[1])
    in_width = N_DEV * w_in.shape[2]
    grad_groups = (("w_down",), ("w_up",), ("w_o",), ("w_uq", "w_ukv"), ("w_in",))
    to_blocks = {
        "w_in": lambda g: col_blocks(g[:, :in_width]),
        "w_uq": lambda g: col_blocks(_unpermute_q_cols(g, n_heads)),
        "w_ukv": col_blocks, "w_o": row_blocks, "w_up": lambda g: g, "w_down": row_blocks,
    }
    in_flight = []

    def grads(group, arrays):
        nms = grad_groups[group]
        blocks = [to_blocks[nm](g) for nm, g in zip(nms, arrays)]
        received = _pair_exchange(f"pair_exchange_{group}", blocks)
        sums = [_pair_sum(f"pair_sum_{nm}", g, r, core) for nm, g, r in zip(nms, blocks, received)]
        sems, sums, parts, tok = _chip_send_start(f"chip_send_start_{group}", sums)
        in_flight.append((sems, sums, parts))
        return (tok,)

    grad_x, small = _local_step(x[0], loss_target[0], gains, weights, grads, first_after=(token,))
    parts = _chip_send_wait(in_flight, (grad_x,))
    total = _small_all_reduce(small)

    flat_names = [nm for grp in grad_groups for nm in grp]
    flat_sums = [a for _, s, _ in in_flight for a in s]
    big_out = {nm: _sum_adam("adam_" + nm, p, s, chip, big[nm], big_m[nm], big_v[nm])
               for nm, p, s in zip(flat_names, parts, flat_sums)}
    big_out = [big_out[nm] for nm in names]

    widths = [g.shape[1] for g in gains]
    offs = [sum(widths[:i]) for i in range(len(widths) + 1)]
    gain_grads = [total[:, offs[i]:offs[i + 1]] for i in range(len(widths))]
    cw = conv_w.shape[2]
    conv_total = N_DEV * cw
    conv_grad = jnp.concatenate(
        [lax.dynamic_slice_in_dim(total[:, offs[-1] + t * conv_total:offs[-1] + (t + 1) * conv_total], me * cw, cw, axis=1)
         for t in range(conv_w.shape[1])], axis=1)
    flat = lambda a: a.reshape(1, -1)
    pack = lambda gs, cv: jnp.concatenate([*gs, flat(cv)], axis=1)
    small_g = jnp.concatenate([*gain_grads, conv_grad], axis=1)
    small_d, small_m, small_v = _adam_small(pack(gains, conv_w), small_g, pack(gain_m, m_conv_w), pack(gain_v, v_conv_w))

    def unpack(row):
        gs = [row[:, offs[i]:offs[i + 1]] for i in range(len(widths))]
        return gs, row[:, offs[-1]:].reshape(conv_w.shape)

    loss = total[0, total.shape[1] - 1]
    order = (0, "w_in", "conv", 1, "w_uq", 2, "w_ukv", 3, 4, "w_o", 5, 6, "w_up", "w_down", 7)

    def assemble(gain_list, conv_item, big_list):
        by_name = dict(zip(names, big_list))
        out = []
        for item in order:
            if item == "conv":
                out.append(conv_item)
            elif isinstance(item, int):
                out.append(gain_list[item])
            else:
                out.append(by_name[item][None])
        return out

    g_gains, g_conv = unpack(small_g)
    d_gains, d_conv = unpack(small_d)
    m_gains, m_conv = unpack(small_m)
    v_gains, v_conv = unpack(small_v)
    outs = [loss, grad_x[None]]
    outs += assemble(g_gains, g_conv, [o[0] for o in big_out])
    outs += assemble(d_gains, d_conv, [o[1] for o in big_out])
    outs += assemble(m_gains, m_conv, [o[2] for o in big_out])
    outs += assemble(v_gains, v_conv, [o[3] for o in big_out])
    return tuple(outs)
```

```python
import jax
import jax.numpy as jnp
from jax import lax
from jax.experimental import pallas as pl
from jax.experimental.pallas import tpu as pltpu

F32 = jnp.float32
BF16 = jnp.bfloat16

EPS = 1e-6
NEG_INF = -1e30
HEAD = 128
ROPE = 64
QK = HEAD + ROPE
CHUNK = 64
ROPE_THETA = 10000.0
ADAM_LR, ADAM_B1, ADAM_B2, ADAM_EPS, ADAM_WD, ADAM_STEP = 0.001, 0.9, 0.999, 1e-08, 0.01, 10

LANE = 128
SUBLANE = 8
VMEM_LIMIT_BYTES = 56 * 1024 * 1024

N_DEV = 8
N_CHIP = 4
MESH = pl.DeviceIdType.MESH


def _params(*sem):
    return pltpu.CompilerParams(dimension_semantics=sem, vmem_limit_bytes=VMEM_LIMIT_BYTES)


ANY = pl.BlockSpec(memory_space=pl.ANY)


def _call(body, *, in_specs, after=(), **kw):
    n_in, n_after = len(in_specs), len(after)

    def ordered(*refs):
        body(*refs[:n_in], *refs[n_in + n_after:])

    call = pl.pallas_call(ordered, in_specs=[*in_specs, *[ANY] * n_after], **kw)
    return lambda *operands: call(*operands, *after)


def _sublane_sum(v):
    r, w = v.shape
    return jnp.sum(v.reshape(r // SUBLANE, SUBLANE, w), axis=0)


def _rstd(x):
    return lax.rsqrt(jnp.mean(x * x, axis=-1, keepdims=True) + EPS)


def _rms_bwd(x, g, dy):
    r = _rstd(x)
    xh = x * r
    dxh = dy * g
    dx = r * (dxh - xh * jnp.mean(dxh * xh, axis=-1, keepdims=True))
    return dx, dy * xh


def _accumulate(ref, val, step):
    @pl.when(step == 0)
    def _():
        ref[...] = val

    @pl.when(step > 0)
    def _():
        ref[...] += val


NN = ((1,), (0,))
NT = ((1,), (1,))
TN = ((0,), (0,))


def _matmul(name, a, b, *, grid, a_spec, b_spec, out_shape, out_specs, contract, nk=1, acc_shape=None,
            extras=(), extra_specs=(), epilogue=None, after=()):
    multi = isinstance(out_shape, (tuple, list))
    out_shapes = tuple(out_shape) if multi else (out_shape,)
    n_out = len(out_shapes)
    n_extra = len(extras)

    def body(a_ref, b_ref, *rest):
        x_refs = rest[:n_extra]
        o_refs = rest[n_extra:n_extra + n_out]

        def emit(acc):
            vals = epilogue(acc, *[r[...] for r in x_refs]) if epilogue else (acc,)
            for r, v in zip(o_refs, vals):
                r[...] = v.astype(r.dtype)

        p = lax.dot_general(a_ref[...], b_ref[...], (contract, ((), ())), preferred_element_type=F32)
        if nk == 1:
            emit(p)
        else:
            acc_ref = rest[n_extra + n_out]
            k = pl.program_id(2)
            _accumulate(acc_ref, p, k)

            @pl.when(k == nk - 1)
            def _():
                emit(acc_ref[...])

    sem = ("parallel", "parallel") + (("arbitrary",) if nk > 1 else ())
    return _call(
        body, name=name, grid=grid, after=after,
        in_specs=[a_spec, b_spec, *extra_specs],
        out_specs=out_specs,
        out_shape=out_shape,
        scratch_shapes=[pltpu.VMEM(acc_shape, F32)] if nk > 1 else [],
        compiler_params=_params(*sem),
    )(a, b, *extras)


def _fit(n, tile):
    if n <= tile:
        return n
    t = tile - tile % LANE
    while n % t:
        t -= LANE
    return t


def _mm_nn(name, a, b, out_dtype, tm, tn):
    m, k = a.shape
    n = b.shape[1]
    tm, tn = _fit(m, tm), _fit(n, tn)
    return _matmul(name, a, b, grid=(m // tm, n // tn),
                   a_spec=pl.BlockSpec((tm, k), lambda i, j: (i, 0)),
                   b_spec=pl.BlockSpec((k, tn), lambda i, j: (0, j)),
                   out_shape=jax.ShapeDtypeStruct((m, n), out_dtype),
                   out_specs=pl.BlockSpec((tm, tn), lambda i, j: (i, j)), contract=NN)


def _mm_nt(name, a, b, out_dtype, tm, tn, after=()):
    m, k = a.shape
    n = b.shape[0]
    tm, tn = _fit(m, tm), _fit(n, tn)
    return _matmul(name, a, b, grid=(m // tm, n // tn), after=after,
                   a_spec=pl.BlockSpec((tm, k), lambda i, j: (i, 0)),
                   b_spec=pl.BlockSpec((tn, k), lambda i, j: (j, 0)),
                   out_shape=jax.ShapeDtypeStruct((m, n), out_dtype),
                   out_specs=pl.BlockSpec((tm, tn), lambda i, j: (i, j)), contract=NT)


def _mm_tn(name, a, b, out_dtype, tm, tn):
    s, m = a.shape
    n = b.shape[1]
    tm, tn = _fit(m, tm), _fit(n, tn)
    return _matmul(name, a, b, grid=(m // tm, n // tn),
                   a_spec=pl.BlockSpec((s, tm), lambda i, j: (0, i)),
                   b_spec=pl.BlockSpec((s, tn), lambda i, j: (0, j)),
                   out_shape=jax.ShapeDtypeStruct((m, n), out_dtype),
                   out_specs=pl.BlockSpec((tm, tn), lambda i, j: (i, j)), contract=TN)


ROWS = 256


def _row_spec(rows, width):
    return pl.BlockSpec((rows, width), lambda i: (i, 0))


def _fixed_spec(rows, width):
    return pl.BlockSpec((rows, width), lambda i: (0, 0))


def _rms_fwd(name, x, g, after=()):
    s, w = x.shape
    rows = min(ROWS, s)

    def body(x_ref, g_ref, o_ref):
        xv = x_ref[...]
        o_ref[...] = (xv * _rstd(xv) * g_ref[...]).astype(o_ref.dtype)

    return _call(
        body, name=name, grid=(s // rows,), after=after,
        in_specs=[_row_spec(rows, w), _fixed_spec(1, w)],
        out_specs=_row_spec(rows, w),
        out_shape=jax.ShapeDtypeStruct((s, w), BF16),
        compiler_params=_params("parallel"),
    )(x, g)


def _rms_bwd_call(name, x, g, dy, out_dtype, after=()):
    s, w = x.shape
    rows = min(ROWS, s)

    def body(x_ref, g_ref, dy_ref, dx_ref, dg_ref):
        dx, dgc = _rms_bwd(x_ref[...], g_ref[...], dy_ref[...].astype(F32))
        dx_ref[...] = dx.astype(dx_ref.dtype)
        _accumulate(dg_ref, _sublane_sum(dgc), pl.program_id(0))

    return _call(
        body, name=name, grid=(s // rows,), after=after,
        in_specs=[_row_spec(rows, w), _fixed_spec(1, w), _row_spec(rows, w)],
        out_specs=[_row_spec(rows, w), _fixed_spec(SUBLANE, w)],
        out_shape=[jax.ShapeDtypeStruct((s, w), out_dtype), jax.ShapeDtypeStruct((SUBLANE, w), F32)],
        compiler_params=_params("arbitrary"),
    )(x, g, dy)


def _mid_fwd(x, y, g_post, g_pre):
    s, w = x.shape
    rows = min(ROWS, s)

    def body(x_ref, y_ref, gp_ref, gq_ref, x2_ref, h2_ref):
        yv = y_ref[...]
        x2 = x_ref[...] + yv * _rstd(yv) * gp_ref[...]
        x2_ref[...] = x2
        h2_ref[...] = (x2 * _rstd(x2) * gq_ref[...]).astype(h2_ref.dtype)

    return pl.pallas_call(
        body, name="mid_fwd", grid=(s // rows,),
        in_specs=[_row_spec(rows, w), _row_spec(rows, w), _fixed_spec(1, w), _fixed_spec(1, w)],
        out_specs=[_row_spec(rows, w), _row_spec(rows, w)],
        out_shape=[jax.ShapeDtypeStruct((s, w), F32), jax.ShapeDtypeStruct((s, w), BF16)],
        compiler_params=_params("parallel"),
    )(x, y, g_post, g_pre)


def _head(m, x2, tgt, g):
    s, w = m.shape
    rows = min(ROWS, s)

    def body(m_ref, x2_ref, t_ref, g_ref, dout_ref, dm_ref, dg_ref, loss_ref):
        mv = m_ref[...]
        gv = g_ref[...]
        out = x2_ref[...] + mv * _rstd(mv) * gv
        err = out - t_ref[...]
        dout = err * (1.0 / w)
        dout_ref[...] = dout
        dm, dgc = _rms_bwd(mv, gv, dout)
        dm_ref[...] = dm.astype(dm_ref.dtype)
        sq = err * err
        lanes = sq[:, 0:LANE]
        for j in range(1, w // LANE):
            lanes = lanes + sq[:, j * LANE:(j + 1) * LANE]
        step = pl.program_id(0)
        _accumulate(dg_ref, _sublane_sum(dgc), step)
        _accumulate(loss_ref, _sublane_sum(lanes) * (0.5 / w), step)

    return pl.pallas_call(
        body, name="head", grid=(s // rows,),
        in_specs=[_row_spec(rows, w), _row_spec(rows, w), _row_spec(rows, w), _fixed_spec(1, w)],
        out_specs=[_row_spec(rows, w), _row_spec(rows, w), _fixed_spec(SUBLANE, w), _fixed_spec(SUBLANE, LANE)],
        out_shape=[jax.ShapeDtypeStruct((s, w), F32), jax.ShapeDtypeStruct((s, w), BF16),
                   jax.ShapeDtypeStruct((SUBLANE, w), F32), jax.ShapeDtypeStruct((SUBLANE, LANE), F32)],
        compiler_params=_params("arbitrary"),
    )(m, x2, tgt, g)


def _mid_bwd(x2, y, d_out, d_h2, g_pre, g_post, after=()):
    s, w = x2.shape
    rows = min(ROWS, s)

    def body(x2_ref, y_ref, dout_ref, dh2_ref, gq_ref, gp_ref, dx2_ref, dy_ref, dgq_ref, dgp_ref):
        dx, dgq = _rms_bwd(x2_ref[...], gq_ref[...], dh2_ref[...])
        dx2 = dout_ref[...] + dx
        dx2_ref[...] = dx2
        dy, dgp = _rms_bwd(y_ref[...], gp_ref[...], dx2)
        dy_ref[...] = dy.astype(dy_ref.dtype)
        step = pl.program_id(0)
        _accumulate(dgq_ref, _sublane_sum(dgq), step)
        _accumulate(dgp_ref, _sublane_sum(dgp), step)

    return _call(
        body, name="mid_bwd", grid=(s // rows,), after=after,
        in_specs=[_row_spec(rows, w)] * 4 + [_fixed_spec(1, w)] * 2,
        out_specs=[_row_spec(rows, w), _row_spec(rows, w), _fixed_spec(SUBLANE, w), _fixed_spec(SUBLANE, w)],
        out_shape=[jax.ShapeDtypeStruct((s, w), F32), jax.ShapeDtypeStruct((s, w), BF16),
                   jax.ShapeDtypeStruct((SUBLANE, w), F32), jax.ShapeDtypeStruct((SUBLANE, w), F32)],
        compiler_params=_params("arbitrary"),
    )(x2, y, d_out, d_h2, g_pre, g_post)


def _first_bwd(x, g, d_h1, d_x2, after=()):
    s, w = x.shape
    rows = min(ROWS, s)

    def body(x_ref, g_ref, dh_ref, dx2_ref, dx_ref, dg_ref):
        dx, dgc = _rms_bwd(x_ref[...], g_ref[...], dh_ref[...])
        dx_ref[...] = dx2_ref[...] + dx
        _accumulate(dg_ref, _sublane_sum(dgc), pl.program_id(0))

    return _call(
        body, name="first_bwd", grid=(s // rows,), after=after,
        in_specs=[_row_spec(rows, w), _fixed_spec(1, w), _row_spec(rows, w), _row_spec(rows, w)],
        out_specs=[_row_spec(rows, w), _fixed_spec(SUBLANE, w)],
        out_shape=[jax.ShapeDtypeStruct((s, w), F32), jax.ShapeDtypeStruct((SUBLANE, w), F32)],
        compiler_params=_params("arbitrary"),
    )(x, g, d_h1, d_x2)


def _shift_down(v, k):
    t = lax.broadcasted_iota(jnp.int32, v.shape, 0)
    return jnp.where(t >= k, pltpu.roll(v, k, 0), 0.0)


def _shift_up(v, k):
    n = v.shape[0]
    t = lax.broadcasted_iota(jnp.int32, v.shape, 0)
    return jnp.where(t < n - k, pltpu.roll(v, n - k, 0), 0.0)


def _conv_core(u, b, c, w):
    z = c * u
    conv = w[0:1, :] * _shift_down(z, 2) + w[1:2, :] * _shift_down(z, 1) + w[2:3, :] * z
    return z, conv, b * conv


def _conv_fwd(proj, conv_w, g, n_groups):
    s = proj.shape[0]

    def body(u_ref, b_ref, c_ref, w_ref, g_ref, o_ref):
        _, _, yr = _conv_core(u_ref[...], b_ref[...], c_ref[...], w_ref[...])
        o_ref[...] = (yr * _rstd(yr) * g_ref[...]).astype(o_ref.dtype)

    col = lambda k: pl.BlockSpec((s, HEAD), lambda i: (0, k * n_groups + i))
    return pl.pallas_call(
        body, name="conv_fwd", grid=(n_groups,),
        in_specs=[col(0), col(1), col(2), pl.BlockSpec((3, HEAD), lambda i: (0, i)), pl.BlockSpec((1, HEAD), lambda i: (0, i))],
        out_specs=pl.BlockSpec((s, HEAD), lambda i: (0, i)),
        out_shape=jax.ShapeDtypeStruct((s, n_groups * HEAD), BF16),
        compiler_params=_params("parallel"),
    )(proj, proj, proj, conv_w, g)


def _conv_bwd(proj, d_mix, conv_w, g, n_groups):
    s = proj.shape[0]
    width = n_groups * HEAD

    def body(u_ref, b_ref, c_ref, dy_ref, w_ref, g_ref, du_ref, db_ref, dc_ref, dg_ref, dw_ref):
        u, b, c, w = u_ref[...], b_ref[...], c_ref[...], w_ref[...]
        z, conv, yr = _conv_core(u, b, c, w)
        dyr, dgc = _rms_bwd(yr, g_ref[...], dy_ref[...])
        dconv = dyr * b
        db_ref[...] = (dyr * conv).astype(db_ref.dtype)
        dz = w[2:3, :] * dconv + w[1:2, :] * _shift_up(dconv, 1) + w[0:1, :] * _shift_up(dconv, 2)
        dc_ref[...] = (dz * u).astype(dc_ref.dtype)
        du_ref[...] = (dz * c).astype(du_ref.dtype)
        dg_ref[...] = _sublane_sum(dgc)
        dw_ref[0] = _sublane_sum(dconv * _shift_down(z, 2))
        dw_ref[1] = _sublane_sum(dconv * _shift_down(z, 1))
        dw_ref[2] = _sublane_sum(dconv * z)

    col = lambda k: pl.BlockSpec((s, HEAD), lambda i: (0, k * n_groups + i))
    grp = pl.BlockSpec((s, HEAD), lambda i: (0, i))
    return pl.pallas_call(
        body, name="conv_bwd", grid=(n_groups,),
        in_specs=[col(0), col(1), col(2), grp, pl.BlockSpec((3, HEAD), lambda i: (0, i)), pl.BlockSpec((1, HEAD), lambda i: (0, i))],
        out_specs=[grp, grp, grp, pl.BlockSpec((SUBLANE, HEAD), lambda i: (0, i)),
                   pl.BlockSpec((3, SUBLANE, HEAD), lambda i: (0, 0, i))],
        out_shape=[jax.ShapeDtypeStruct((s, width), BF16)] * 3
        + [jax.ShapeDtypeStruct((SUBLANE, width), F32), jax.ShapeDtypeStruct((3, SUBLANE, width), F32)],
        compiler_params=_params("parallel"),
    )(proj, proj, proj, d_mix, conv_w, g)


def _rope_tables(s, n_heads):
    pos = jnp.arange(s, dtype=F32)
    inv_freq = jnp.power(ROPE_THETA, -jnp.arange(0, ROPE, 2, dtype=F32) / ROPE)
    ang = pos[:, None] * inv_freq[None, :]
    cos, sin = jnp.cos(ang), jnp.sin(ang)
    cs = jnp.concatenate([cos, cos], axis=1)
    sn = jnp.concatenate([-sin, sin], axis=1)
    pad = jnp.zeros((s, LANE - ROPE), F32)
    return (jnp.tile(cs, (1, n_heads)), jnp.tile(sn, (1, n_heads)),
            jnp.concatenate([cs, pad], axis=1), jnp.concatenate([sn, pad], axis=1))


def _swap_halves(v):
    w = v.shape[1]
    lane = lax.broadcasted_iota(jnp.int32, v.shape, 1)
    first = (lane % ROPE) < (ROPE // 2)
    return jnp.where(first, pltpu.roll(v, w - ROPE // 2, 1), pltpu.roll(v, ROPE // 2, 1))


def _pack_heads(q, kv, kr, tables, n_heads):
    s = q.shape[0]
    rows = min(ROWS, s)
    cq, sq, ck, sk = tables
    wq = n_heads * ROPE

    def body(q_ref, kv_ref, kr_ref, cq_ref, sq_ref, ck_ref, sk_ref, qo_ref, ko_ref, vo_ref):
        qr = q_ref[:, n_heads * HEAD:]
        qr = qr * cq_ref[...] + _swap_halves(qr) * sq_ref[...]
        krv = kr_ref[...]
        krv = krv * ck_ref[...] + _swap_halves(krv) * sk_ref[...]
        for h in range(n_heads):
            qo_ref[h] = jnp.concatenate([q_ref[:, h * HEAD:(h + 1) * HEAD], qr[:, h * ROPE:(h + 1) * ROPE]], axis=1).astype(BF16)
            ko_ref[h] = jnp.concatenate([kv_ref[:, 2 * h * HEAD:(2 * h + 1) * HEAD], krv[:, :ROPE]], axis=1).astype(BF16)
            vo_ref[h] = kv_ref[:, (2 * h + 1) * HEAD:(2 * h + 2) * HEAD].astype(BF16)

    hs = lambda w: pl.BlockSpec((n_heads, rows, w), lambda i: (0, i, 0))
    return pl.pallas_call(
        body, name="pack_heads", grid=(s // rows,),
        in_specs=[_row_spec(rows, q.shape[1]), _row_spec(rows, kv.shape[1]), _row_spec(rows, LANE),
                  _row_spec(rows, wq), _row_spec(rows, wq), _row_spec(rows, LANE), _row_spec(rows, LANE)],
        out_specs=[hs(QK), hs(QK), hs(HEAD)],
        out_shape=[jax.ShapeDtypeStruct((n_heads, s, QK), BF16), jax.ShapeDtypeStruct((n_heads, s, QK), BF16),
                   jax.ShapeDtypeStruct((n_heads, s, HEAD), BF16)],
        compiler_params=_params("parallel"),
    )(q, kv, kr, cq, sq, ck, sk)


def _unpack_heads(dq, dk, dv, tables, n_heads):
    s = dq.shape[1]
    rows = min(ROWS, s)
    cq, sq, ck, sk = tables
    wq = n_heads * ROPE

    def body(dq_ref, dk_ref, dv_ref, cq_ref, sq_ref, ck_ref, sk_ref, qo_ref, kvo_ref, kro_ref):
        dqr = jnp.concatenate([dq_ref[h][:, HEAD:] for h in range(n_heads)], axis=1)
        dqr = dqr * cq_ref[...] - _swap_halves(dqr) * sq_ref[...]
        dkr = dk_ref[0][:, HEAD:]
        for h in range(1, n_heads):
            dkr = dkr + dk_ref[h][:, HEAD:]
        dkr = jnp.concatenate([dkr, jnp.zeros((rows, LANE - ROPE), F32)], axis=1)
        dkr = dkr * ck_ref[...] - _swap_halves(dkr) * sk_ref[...]
        kro_ref[...] = dkr.astype(kro_ref.dtype)
        qo_ref[:, n_heads * HEAD:] = dqr.astype(qo_ref.dtype)
        for h in range(n_heads):
            qo_ref[:, h * HEAD:(h + 1) * HEAD] = dq_ref[h][:, :HEAD].astype(qo_ref.dtype)
            kvo_ref[:, 2 * h * HEAD:(2 * h + 1) * HEAD] = dk_ref[h][:, :HEAD].astype(kvo_ref.dtype)
            kvo_ref[:, (2 * h + 1) * HEAD:(2 * h + 2) * HEAD] = dv_ref[h].astype(kvo_ref.dtype)

    hs = lambda w: pl.BlockSpec((n_heads, rows, w), lambda i: (0, i, 0))
    return pl.pallas_call(
        body, name="unpack_heads", grid=(s // rows,),
        in_specs=[hs(QK), hs(QK), hs(HEAD), _row_spec(rows, wq), _row_spec(rows, wq), _row_spec(rows, LANE), _row_spec(rows, LANE)],
        out_specs=[_row_spec(rows, n_heads * QK), _row_spec(rows, 2 * n_heads * HEAD), _row_spec(rows, LANE)],
        out_shape=[jax.ShapeDtypeStruct((s, n_heads * QK), BF16), jax.ShapeDtypeStruct((s, 2 * n_heads * HEAD), BF16),
                   jax.ShapeDtypeStruct((s, LANE), BF16)],
        compiler_params=_params("parallel"),
    )(dq, dk, dv, cq, sq, ck, sk)


TQ = 256


def _probs(q, k):
    tq, n_keys = q.shape[0], k.shape[0]
    sc = lax.dot_general(q, k, (NT, ((), ())), preferred_element_type=F32) * (QK ** -0.5)
    row = lax.broadcasted_iota(jnp.int32, (tq, tq), 0)
    col = lax.broadcasted_iota(jnp.int32, (tq, tq), 1)
    own = jnp.where(col // CHUNK <= row // CHUNK, sc[:, n_keys - tq:], NEG_INF)
    sc = own if n_keys == tq else jnp.concatenate([sc[:, :n_keys - tq], own], axis=1)
    e = jnp.exp(sc - jnp.max(sc, axis=-1, keepdims=True))
    return e / jnp.sum(e, axis=-1, keepdims=True)


def _per_query_block(n_blocks, branch):
    i = pl.program_id(1)
    for c in range(n_blocks):
        pl.when(i == c)(lambda c=c: branch(c))


def _attn_fwd(q, k, v, g):
    n_heads, s, _ = q.shape
    tq = min(TQ, s)
    assert tq % CHUNK == 0 and s % tq == 0

    def body(q_ref, k_ref, v_ref, g_ref, o_ref, y_ref):
        def branch(c):
            n_keys = (c + 1) * tq
            p = _probs(q_ref[...], k_ref[0:n_keys, :])
            o = jnp.dot(p.astype(BF16), v_ref[0:n_keys, :], preferred_element_type=F32)
            o_ref[...] = o
            y_ref[...] = (o * _rstd(o) * g_ref[...]).astype(y_ref.dtype)

        _per_query_block(s // tq, branch)

    return pl.pallas_call(
        body, name="attn_fwd", grid=(n_heads, s // tq),
        in_specs=[pl.BlockSpec((None, tq, QK), lambda h, i: (h, i, 0)),
                  pl.BlockSpec((None, s, QK), lambda h, i: (h, 0, 0)),
                  pl.BlockSpec((None, s, HEAD), lambda h, i: (h, 0, 0)),
                  pl.BlockSpec((1, HEAD), lambda h, i: (0, h))],
        out_specs=[pl.BlockSpec((None, tq, HEAD), lambda h, i: (h, i, 0)),
                   pl.BlockSpec((tq, HEAD), lambda h, i: (i, h))],
        out_shape=[jax.ShapeDtypeStruct((n_heads, s, HEAD), F32), jax.ShapeDtypeStruct((s, n_heads * HEAD), BF16)],
        compiler_params=_params("parallel", "parallel"),
    )(q, k, v, g)


def _attn_bwd(q, k, v, o, d_mix, g, col0, after=()):
    n_heads, s, _ = q.shape
    tq = min(TQ, s)

    def body(q_ref, k_ref, v_ref, o_ref, dy_ref, g_ref, dq_ref, dk_ref, dv_ref, dg_ref):
        def branch(c):
            n_keys = (c + 1) * tq
            qv, kv_, vv = q_ref[...], k_ref[0:n_keys, :], v_ref[0:n_keys, :]
            do, dgc = _rms_bwd(o_ref[...], g_ref[...], dy_ref[...])
            do = do.astype(BF16)
            p = _probs(qv, kv_)
            dp = lax.dot_general(do, vv, (NT, ((), ())), preferred_element_type=F32)
            ds = (p * (dp - jnp.sum(p * dp, axis=-1, keepdims=True)) * (QK ** -0.5)).astype(BF16)
            dq_ref[...] = jnp.dot(ds, kv_, preferred_element_type=F32)
            dk = lax.dot_general(ds, qv, (TN, ((), ())), preferred_element_type=F32)
            dv = lax.dot_general(p.astype(BF16), do, (TN, ((), ())), preferred_element_type=F32)
            if c == 0:
                dk_ref[...] = jnp.zeros_like(dk_ref)
                dv_ref[...] = jnp.zeros_like(dv_ref)
                dk_ref[0:n_keys, :] = dk
                dv_ref[0:n_keys, :] = dv
                dg_ref[...] = _sublane_sum(dgc)
            else:
                dk_ref[0:n_keys, :] += dk
                dv_ref[0:n_keys, :] += dv
                dg_ref[...] += _sublane_sum(dgc)

        _per_query_block(s // tq, branch)

    c0 = col0 // HEAD
    return _call(
        body, name="attn_bwd", grid=(n_heads, s // tq), after=after,
        in_specs=[pl.BlockSpec((None, tq, QK), lambda h, i: (h, i, 0)),
                  pl.BlockSpec((None, s, QK), lambda h, i: (h, 0, 0)),
                  pl.BlockSpec((None, s, HEAD), lambda h, i: (h, 0, 0)),
                  pl.BlockSpec((None, tq, HEAD), lambda h, i: (h, i, 0)),
                  pl.BlockSpec((tq, HEAD), lambda h, i: (i, c0 + h)),
                  pl.BlockSpec((1, HEAD), lambda h, i: (0, h))],
        out_specs=[pl.BlockSpec((None, tq, QK), lambda h, i: (h, i, 0)),
                   pl.BlockSpec((None, s, QK), lambda h, i: (h, 0, 0)),
                   pl.BlockSpec((None, s, HEAD), lambda h, i: (h, 0, 0)),
                   pl.BlockSpec((SUBLANE, HEAD), lambda h, i: (0, h))],
        out_shape=[jax.ShapeDtypeStruct((n_heads, s, QK), F32), jax.ShapeDtypeStruct((n_heads, s, QK), F32),
                   jax.ShapeDtypeStruct((n_heads, s, HEAD), F32), jax.ShapeDtypeStruct((SUBLANE, n_heads * HEAD), F32)],
        compiler_params=_params("parallel", "arbitrary"),
    )(q, k, v, o, d_mix, g)


TILE_M = 1024
TILE_N = 1024


def _up_fwd(h2, w_up):
    s, d = h2.shape
    nb, _, fb = w_up.shape
    tm = min(TILE_M,s)

    def epilogue(acc):
        r = jnp.maximum(acc, 0.0)
        return r * r, r

    blk = pl.BlockSpec((tm, fb), lambda i, j: (i, j))
    return _matmul("up_fwd", h2, w_up, grid=(s // tm, nb),
                   a_spec=pl.BlockSpec((tm, d), lambda i, j: (i, 0)),
                   b_spec=pl.BlockSpec((None, d, fb), lambda i, j: (j, 0, 0)),
                   out_shape=[jax.ShapeDtypeStruct((s, nb * fb), BF16)] * 2, out_specs=[blk, blk],
                   contract=NN, epilogue=epilogue)


def _down_fwd(a, w_down):
    s, f = a.shape
    d = w_down.shape[1]
    tm, tn, tk = min(TILE_M,s), min(TILE_N,d), 2048
    nk = f // tk
    return _matmul("down_fwd", a, w_down, grid=(s // tm, d // tn, nk),
                   a_spec=pl.BlockSpec((tm, tk), lambda i, j, k: (i, k)),
                   b_spec=pl.BlockSpec((tk, tn), lambda i, j, k: (k, j)),
                   out_shape=jax.ShapeDtypeStruct((s, d), F32),
                   out_specs=pl.BlockSpec((tm, tn), lambda i, j, k: (i, j)),
                   contract=NN, nk=nk, acc_shape=(tm, tn))


def _down_bwd_act(d_m, w_down, r):
    s, d = d_m.shape
    f = w_down.shape[0]
    tm, tn = min(TILE_M,s), min(TILE_N,f)
    blk = pl.BlockSpec((tm, tn), lambda i, j: (i, j))
    return _matmul("down_bwd_act", d_m, w_down, grid=(s // tm, f // tn),
                   a_spec=pl.BlockSpec((tm, d), lambda i, j: (i, 0)),
                   b_spec=pl.BlockSpec((tn, d), lambda i, j: (j, 0)),
                   out_shape=jax.ShapeDtypeStruct((s, f), BF16), out_specs=blk, contract=NT,
                   extras=(r,), extra_specs=(blk,),
                   epilogue=lambda acc, rv: (acc * (2.0 * rv.astype(F32)),))


def _up_bwd_act(d_up, w_up, after=()):
    s, _ = d_up.shape
    nb, d, fb = w_up.shape
    tm, tn = min(TILE_M,s), min(TILE_N,d)
    return _matmul("up_bwd_act", d_up, w_up, grid=(s // tm, d // tn, nb), after=after,
                   a_spec=pl.BlockSpec((tm, fb), lambda i, j, k: (i, k)),
                   b_spec=pl.BlockSpec((None, tn, fb), lambda i, j, k: (k, j, 0)),
                   out_shape=jax.ShapeDtypeStruct((s, d), F32),
                   out_specs=pl.BlockSpec((tm, tn), lambda i, j, k: (i, j)),
                   contract=NT, nk=nb, acc_shape=(tm, tn))


def _up_bwd_w(h2, d_up, nb):
    s, d = h2.shape
    fb = d_up.shape[1] // nb
    tm = min(TILE_M,d)
    return _matmul("up_bwd_w", h2, d_up, grid=(d // tm, nb),
                   a_spec=pl.BlockSpec((s, tm), lambda i, j: (0, i)),
                   b_spec=pl.BlockSpec((s, fb), lambda i, j: (0, j)),
                   out_shape=jax.ShapeDtypeStruct((nb, d, fb), BF16),
                   out_specs=pl.BlockSpec((None, tm, fb), lambda i, j: (j, i, 0)), contract=TN)


def _in_pad(in_width):
    return -(-in_width // LANE) * LANE


def _permute_q_cols(w_uq, n_heads):
    r = w_uq.shape[0]
    w3 = w_uq.reshape(r, n_heads, QK)
    return jnp.concatenate([w3[:, :, :HEAD].reshape(r, n_heads * HEAD), w3[:, :, HEAD:].reshape(r, n_heads * ROPE)], axis=1)


def _unpermute_q_cols(w, n_heads):
    r = w.shape[0]
    nope = w[:, :n_heads * HEAD].reshape(r, n_heads, HEAD)
    rope = w[:, n_heads * HEAD:].reshape(r, n_heads, ROPE)
    return jnp.concatenate([nope, rope], axis=2).reshape(r, n_heads * QK)


def _local_step(x, tgt, gains, weights, grads, first_after=()):
    pre_mix_g, q_norm_g, kv_norm_g, conv_out_g, attn_out_g, post_mix_g, pre_mlp_g, post_mlp_g = gains
    s, d = x.shape
    conv_width = conv_out_g.shape[1]
    n_groups = conv_width // HEAD
    r_q, r_kv = q_norm_g.shape[1], kv_norm_g.shape[1]
    n_heads = attn_out_g.shape[1] // HEAD
    c_q0 = 3 * conv_width
    c_kv0 = c_q0 + r_q
    c_kr0 = c_kv0 + r_kv
    in_pad = _in_pad(c_kr0 + ROPE)
    tn_in = in_pad // 5 if in_pad % (5 * LANE) == 0 else LANE
    tables = _rope_tables(s, n_heads)

    h1 = _rms_fwd("pre_mix_norm", x, pre_mix_g, after=first_after)
    w_in_p, conv_w = weights(0, (h1,))
    proj = _mm_nn("in_proj", h1, w_in_p, F32, TILE_M,tn_in)
    y_conv = _conv_fwd(proj, conv_w, conv_out_g, n_groups)
    c_q = proj[:, c_q0:c_kv0]
    c_kv = proj[:, c_kv0:c_kr0]
    qn = _rms_fwd("q_norm", c_q, q_norm_g)
    kvn = _rms_fwd("kv_norm", c_kv, kv_norm_g)
    w_uq_p, w_ukv, w_o = weights(1, (y_conv, qn, kvn))
    q = _mm_nn("q_up", qn, w_uq_p, F32, TILE_M, TILE_N)
    kv = _mm_nn("kv_up", kvn, w_ukv, F32, TILE_M, TILE_N)
    kr = proj[:, c_kr0:c_kr0 + LANE]
    qh, kh, vh = _pack_heads(q, kv, kr, tables, n_heads)
    o, y_attn = _attn_fwd(qh, kh, vh, attn_out_g)
    mix = jnp.concatenate([y_conv, y_attn], axis=1)
    y = _mm_nn("out_proj", mix, w_o, F32, TILE_M, TILE_N)
    x2, h2 = _mid_fwd(x, y, post_mix_g, pre_mlp_g)
    (w_up,) = weights(2, (h2,))
    a, r = _up_fwd(h2, w_up)
    (w_down,) = weights(3, (a,))
    m = _down_fwd(a, w_down)

    d_out, d_m, dg_post_mlp, loss_part = _head(m, x2, tgt, post_mlp_g)
    d_up = _down_bwd_act(d_m, w_down, r)
    gw_down = _mm_tn("down_bwd_w", a, d_m, BF16, TILE_M, TILE_N)
    d_h2 = _up_bwd_act(d_up, w_up, after=grads(0, (gw_down,)))
    gw_up = _up_bwd_w(h2, d_up, w_up.shape[0])
    d_x2, d_y, dg_pre_mlp, dg_post_mix = _mid_bwd(x2, y, d_out, d_h2, pre_mlp_g, post_mix_g, after=grads(1, (gw_up,)))
    d_mix = _mm_nt("out_proj_bwd_act", d_y, w_o, F32, TILE_M, TILE_N)
    gw_o = _mm_tn("out_proj_bwd_w", mix, d_y, BF16, TILE_M, TILE_N)
    dqh, dkh, dvh, dg_attn = _attn_bwd(qh, kh, vh, o, d_mix, attn_out_g, conv_width, after=grads(2, (gw_o,)))
    d_q, d_kv, d_kr = _unpack_heads(dqh, dkh, dvh, tables, n_heads)
    d_qn = _mm_nt("q_up_bwd_act", d_q, w_uq_p, F32, TILE_M, TILE_N)
    d_kvn = _mm_nt("kv_up_bwd_act", d_kv, w_ukv, F32, TILE_M, TILE_N)
    gw_uq_p = _mm_tn("q_up_bwd_w", qn, d_q, BF16, TILE_M, TILE_N)
    gw_ukv = _mm_tn("kv_up_bwd_w", kvn, d_kv, BF16, TILE_M, TILE_N)
    d_cq, dg_q = _rms_bwd_call("q_norm_bwd", c_q, q_norm_g, d_qn, BF16, after=grads(3, (gw_uq_p, gw_ukv)))
    d_ckv, dg_kv = _rms_bwd_call("kv_norm_bwd", c_kv, kv_norm_g, d_kvn, BF16)
    d_u, d_b, d_c, dg_conv, dw_conv = _conv_bwd(proj, d_mix, conv_w, conv_out_g, n_groups)
    d_proj = jnp.concatenate([d_u, d_b, d_c, d_cq, d_ckv, d_kr[:, :in_pad - c_kr0]], axis=1)
    gw_in_p = _mm_tn("in_proj_bwd_w", h1, d_proj, BF16, TILE_M, tn_in)
    d_h1 = _mm_nt("in_proj_bwd_act", d_proj, w_in_p, F32, TILE_M, 512, after=grads(4, (gw_in_p,)))
    grad_x, dg_pre_mix = _first_bwd(x, pre_mix_g, d_h1, d_x2)

    small = [dg_pre_mix, dg_q, dg_kv, dg_conv, dg_attn, dg_post_mix, dg_pre_mlp, dg_post_mlp,
             dw_conv[0], dw_conv[1], dw_conv[2], loss_part]
    return grad_x, jnp.concatenate(small, axis=1)


HBM = pl.BlockSpec(memory_space=pltpu.HBM)
SEM = pl.BlockSpec(memory_space=pltpu.SEMAPHORE)
IN_VMEM = pl.BlockSpec(memory_space=pltpu.VMEM)
SPLIT = pltpu.CompilerParams(has_side_effects=pltpu.SideEffectType.DATAFLOW_SIDE_EFFECTING)


def _in_hbm(a):
    return pltpu.with_memory_space_constraint(a, pltpu.HBM)


def _hbm_like(a):
    return pltpu.HBM(a.shape, a.dtype)


def _place():
    x, y, c = lax.axis_index("x"), lax.axis_index("y"), lax.axis_index("c")
    other_chips = [(1 - x, y), (x, 1 - y), (1 - x, 1 - y)]
    return x, y, c, other_chips


def _block(px, py, pc):
    return 4 * px + 2 * py + pc


def _gather_start(shards, groups):
    n, ng = len(shards), len(groups)
    lands = [lax.empty((N_DEV, *a.shape), a.dtype) for a in shards]

    def body(*refs):
        src, land = refs[:n], refs[n:2 * n]
        sems, token = refs[2 * n:2 * n + 2 * ng], refs[-1]
        x, y, c, chips = _place()
        targets = [(x, y, 1 - c)] + [(*chip, c) for chip in chips]
        for gi, group in enumerate(groups):
            for i, w in enumerate(group):
                for k, to in enumerate(targets):
                    pltpu.make_async_remote_copy(
                        src_ref=src[w], dst_ref=land[w].at[_block(x, y, c)],
                        send_sem=sems[2 * gi].at[4 * i + k], recv_sem=sems[2 * gi + 1].at[4 * i + k],
                        device_id=to, device_id_type=MESH).start()
        token[...] = jnp.zeros_like(token)

    sem_shapes = [pltpu.SemaphoreType.DMA((4 * len(g),)) for g in groups for _ in range(2)]
    out = pl.pallas_call(
        body, name="gather_start",
        in_specs=[HBM] * (2 * n),
        out_specs=[SEM] * (2 * ng) + [HBM] * (2 * n) + [IN_VMEM],
        out_shape=sem_shapes + [_hbm_like(a) for a in shards] + [_hbm_like(a) for a in lands]
        + [jax.ShapeDtypeStruct((SUBLANE, LANE), F32)],
        input_output_aliases={i: 2 * ng + i for i in range(2 * n)},
        compiler_params=SPLIT,
    )(*[_in_hbm(a) for a in shards], *[_in_hbm(a) for a in lands])
    sems = [(out[2 * gi], out[2 * gi + 1]) for gi in range(ng)]
    return sems, out[2 * ng:2 * ng + n], out[2 * ng + n:2 * ng + 2 * n], out[-1]


def _gather_forward(name, shards, lands, send1, recv1, after):
    n = len(lands)

    def body(*refs):
        src, land = refs[:n], refs[n:2 * n]
        s1, r1 = refs[2 * n], refs[2 * n + 1]
        s2, r2 = refs[2 * n + 2 + len(after)], refs[2 * n + 3 + len(after)]
        x, y, c, chips = _place()
        me, sibling = (x, y, c), (x, y, 1 - c)
        for j, chip in enumerate(chips):
            for i in range(n):
                blk = land[i].at[_block(*chip, c)]
                pltpu.make_async_remote_copy(src_ref=blk, dst_ref=blk, send_sem=s1.at[4 * i + 1 + j], recv_sem=r1.at[4 * i + 1 + j],
                                             device_id=me, device_id_type=MESH).wait_recv()
                pltpu.make_async_remote_copy(src_ref=blk, dst_ref=blk, send_sem=s2.at[3 * i + j], recv_sem=r2.at[3 * i + j],
                                             device_id=sibling, device_id_type=MESH).start()
        for i in range(n):
            blk = land[i].at[_block(x, y, 1 - c)]
            pltpu.make_async_remote_copy(src_ref=blk, dst_ref=blk, send_sem=s1.at[4 * i], recv_sem=r1.at[4 * i],
                                         device_id=me, device_id_type=MESH).wait_recv()
            for k in range(4):
                pltpu.make_async_remote_copy(src_ref=src[i], dst_ref=land[i].at[_block(x, y, c)], send_sem=s1.at[4 * i + k],
                                             recv_sem=r1.at[4 * i + k], device_id=sibling, device_id_type=MESH).wait_send()

    sem = pltpu.SemaphoreType.DMA((3 * n,))
    out = pl.pallas_call(
        body, name=name,
        in_specs=[HBM] * (2 * n) + [SEM, SEM] + [ANY] * len(after),
        out_specs=[SEM, SEM] + [HBM] * n,
        out_shape=[sem, sem] + [_hbm_like(a) for a in lands],
        input_output_aliases={n + i: 2 + i for i in range(n)},
        compiler_params=SPLIT,
    )(*shards, *lands, send1, recv1, *after)
    return (out[0], out[1]), out[2:]


def _gather_wait(name, lands, send2, recv2):
    n = len(lands)

    def body(*refs):
        land, s2, r2 = refs[:n], refs[n], refs[n + 1]
        x, y, c, chips = _place()
        me = (x, y, c)
        for i in range(n):
            for j, chip in enumerate(chips):
                got = land[i].at[_block(*chip, 1 - c)]
                pltpu.make_async_remote_copy(src_ref=got, dst_ref=got, send_sem=s2.at[3 * i + j], recv_sem=r2.at[3 * i + j],
                                             device_id=me, device_id_type=MESH).wait_recv()
                sent = land[i].at[_block(*chip, c)]
                pltpu.make_async_remote_copy(src_ref=sent, dst_ref=sent, send_sem=s2.at[3 * i + j], recv_sem=r2.at[3 * i + j],
                                             device_id=me, device_id_type=MESH).wait_send()

    return pl.pallas_call(
        body, name=name,
        in_specs=[HBM] * n + [SEM, SEM], out_specs=[HBM] * n, out_shape=[_hbm_like(a) for a in lands],
        input_output_aliases={i: i for i in range(n)},
        compiler_params=SPLIT,
    )(*lands, send2, recv2)


def _pair_exchange(name, grads):
    n = len(grads)

    def body(*refs):
        ins, recv = refs[:n], refs[n:2 * n]
        send_sems, recv_sems = refs[2 * n:]
        x, y, c, _ = _place()
        sends = []
        for w in range(n):
            for k in range(N_CHIP):
                sends.append(pltpu.make_async_remote_copy(
                    src_ref=ins[w].at[2 * k + 1 - c], dst_ref=recv[w].at[k],
                    send_sem=send_sems.at[w, k], recv_sem=recv_sems.at[w, k],
                    device_id=(x, y, 1 - c), device_id_type=MESH))
        for cp in sends:
            cp.start()
        for cp in sends:
            cp.wait()

    return pl.pallas_call(
        body, name=name,
        in_specs=[ANY] * n, out_specs=[ANY] * n,
        out_shape=[jax.ShapeDtypeStruct((N_CHIP, *g.shape[1:]), g.dtype) for g in grads],
        scratch_shapes=[pltpu.SemaphoreType.DMA((n, N_CHIP))] * 2,
    )(*grads)


def _pair_sum(name, grad, received, core):
    _, r, c = received.shape
    rows = min(ROWS, r)
    assert r % rows == 0

    def body(core_ref, a_ref, b_ref, o_ref):
        o_ref[...] = (a_ref[...].astype(F32) + b_ref[...].astype(F32)).astype(o_ref.dtype)

    spec = pl.BlockSpec((None, rows, c), lambda k, i, core_ref: (k, i, 0))
    return pl.pallas_call(
        body, name=name,
        grid_spec=pltpu.PrefetchScalarGridSpec(
            num_scalar_prefetch=1, grid=(N_CHIP, r // rows),
            in_specs=[pl.BlockSpec((None, None, rows, c), lambda k, i, core_ref: (k, core_ref[0], i, 0)), spec],
            out_specs=spec),
        out_shape=jax.ShapeDtypeStruct(received.shape, received.dtype),
        compiler_params=_params("parallel", "parallel"),
    )(core, grad.reshape(N_CHIP, 2, r, c), received)


def _chip_send_start(name, sums):
    n = len(sums)
    lands = [lax.empty(a.shape, a.dtype) for a in sums]

    def body(*refs):
        src, land = refs[:n], refs[n:2 * n]
        send, recv, token = refs[2 * n], refs[2 * n + 1], refs[-1]
        x, y, c, chips = _place()
        for w in range(n):
            for j, (px, py) in enumerate(chips):
                pltpu.make_async_remote_copy(
                    src_ref=src[w].at[2 * px + py], dst_ref=land[w].at[2 * x + y],
                    send_sem=send.at[3 * w + j], recv_sem=recv.at[3 * w + j],
                    device_id=(px, py, c), device_id_type=MESH).start()
        token[...] = jnp.zeros_like(token)

    sem = pltpu.SemaphoreType.DMA((3 * n,))
    out = pl.pallas_call(
        body, name=name,
        in_specs=[HBM] * (2 * n),
        out_specs=[SEM, SEM] + [HBM] * (2 * n) + [IN_VMEM],
        out_shape=[sem, sem] + [_hbm_like(a) for a in sums] + [_hbm_like(a) for a in lands]
        + [jax.ShapeDtypeStruct((SUBLANE, LANE), F32)],
        input_output_aliases={i: 2 + i for i in range(2 * n)},
        compiler_params=SPLIT,
    )(*[_in_hbm(a) for a in sums], *[_in_hbm(a) for a in lands])
    return (out[0], out[1]), out[2:2 + n], out[2 + n:2 + 2 * n], out[-1]


def _chip_send_wait(name, groups, after):
    counts = [len(g[1]) for g in groups]
    n = sum(counts)

    def body(*refs):
        src, land = refs[:n], refs[n:2 * n]
        sems = refs[2 * n:2 * n + 2 * len(groups)]
        x, y, c, chips = _place()
        w = 0
        for gi, count in enumerate(counts):
            for i in range(count):
                for j, (px, py) in enumerate(chips):
                    pltpu.make_async_remote_copy(
                        src_ref=src[w].at[2 * px + py], dst_ref=land[w].at[2 * px + py],
                        send_sem=sems[2 * gi].at[3 * i + j], recv_sem=sems[2 * gi + 1].at[3 * i + j],
                        device_id=(px, py, c), device_id_type=MESH).wait()
                w += 1

    sums = [a for g in groups for a in g[1]]
    lands = [a for g in groups for a in g[2]]
    sems = [s for g in groups for s in g[0]]
    return pl.pallas_call(
        body, name=name,
        in_specs=[HBM] * (2 * n) + [SEM] * len(sems) + [ANY] * len(after),
        out_specs=[HBM] * n, out_shape=[_hbm_like(a) for a in lands],
        input_output_aliases={n + i: i for i in range(n)},
        compiler_params=SPLIT,
    )(*sums, *lands, *sems, *after)


def _small_all_reduce(part, after=()):
    _, w = part.shape

    def body(p_ref, *rest):
        o_ref, buf, send_sems, recv_sems = rest[len(after):]
        x, y, c, _ = _place()
        me = 4 * x + 2 * y + c
        buf[me] = p_ref[...]
        copies = []
        for k in range(1, N_DEV):
            dx, dy, dc = (k >> 2) & 1, (k >> 1) & 1, k & 1
            copies.append(pltpu.make_async_remote_copy(
                src_ref=p_ref, dst_ref=buf.at[me], send_sem=send_sems.at[k - 1], recv_sem=recv_sems.at[k - 1],
                device_id=(x ^ dx, y ^ dy, c ^ dc), device_id_type=MESH))
        for cp in copies:
            cp.start()
        for cp in copies:
            cp.wait()
        tot = buf[0]
        for d in range(1, N_DEV):
            tot = tot + buf[d]
        tot = jnp.sum(tot, axis=0, keepdims=True)
        o_ref[...] = tot
        loss = jnp.sum(tot[:, w - LANE:], axis=1, keepdims=True)
        o_ref[:, w - LANE:] = jnp.broadcast_to(loss, (1, LANE))

    return pl.pallas_call(
        body, name="small_all_reduce",
        in_specs=[IN_VMEM] + [ANY] * len(after), out_specs=IN_VMEM,
        out_shape=jax.ShapeDtypeStruct((1, w), F32),
        scratch_shapes=[pltpu.VMEM((N_DEV, SUBLANE, w), F32), pltpu.SemaphoreType.DMA((N_DEV - 1,)), pltpu.SemaphoreType.DMA((N_DEV - 1,))],
        compiler_params=pltpu.CompilerParams(vmem_limit_bytes=VMEM_LIMIT_BYTES),
    )(part, *after)


def _adamw(w, g, m, v):
    m = ADAM_B1 * m + (1.0 - ADAM_B1) * g
    v = ADAM_B2 * v + (1.0 - ADAM_B2) * (g * g)
    m_hat = m / (1.0 - ADAM_B1 ** ADAM_STEP)
    v_hat = v / (1.0 - ADAM_B2 ** ADAM_STEP)
    delta = -ADAM_LR * (m_hat / (jnp.sqrt(v_hat) + ADAM_EPS) + ADAM_WD * w)
    return delta, m, v


def _sum_adam(name, parts, sums, chip, w, m, v, after=()):
    r, c = w.shape
    rows = min(ROWS, r)
    assert r % rows == 0
    n_after = len(after)

    def body(chip_ref, p_ref, own_ref, w_ref, m_ref, v_ref, *rest):
        g_ref, d_ref, mo_ref, vo_ref = rest[n_after:]
        g = None
        for k in range(N_CHIP):
            term = jnp.where(chip_ref[0] == k, own_ref[...], p_ref[k]).astype(F32)
            g = term if g is None else g + term
        g_ref[...] = g
        d_ref[...], mo_ref[...], vo_ref[...] = _adamw(w_ref[...], g, m_ref[...], v_ref[...])

    blk = pl.BlockSpec((rows, c), lambda i, chip_ref: (i, 0))
    out = jax.ShapeDtypeStruct((r, c), F32)
    return pl.pallas_call(
        body, name=name,
        grid_spec=pltpu.PrefetchScalarGridSpec(
            num_scalar_prefetch=1, grid=(r // rows,),
            in_specs=[pl.BlockSpec((N_CHIP, rows, c), lambda i, chip_ref: (0, i, 0)),
                      pl.BlockSpec((None, rows, c), lambda i, chip_ref: (chip_ref[0], i, 0)), blk, blk, blk]
            + [ANY] * n_after,
            out_specs=[blk] * 4),
        out_shape=[out] * 4,
        compiler_params=_params("parallel"),
    )(chip, parts, sums, w, m, v, *after)


def _adam_gains(total, ws, ms, vs):
    n = len(ws)
    widths = [w.shape[1] for w in ws]

    def body(t_ref, *refs):
        w_refs, m_refs, v_refs, outs = refs[:n], refs[n:2 * n], refs[2 * n:3 * n], refs[3 * n:]
        off = 0
        for i in range(n):
            g = t_ref[:, off:off + widths[i]]
            off += widths[i]
            g_ref, d_ref, mo_ref, vo_ref = outs[4 * i:4 * i + 4]
            g_ref[...] = g
            d_ref[...], mo_ref[...], vo_ref[...] = _adamw(w_refs[i][...], g, m_refs[i][...], v_refs[i][...])

    out = pl.pallas_call(
        body, name="adam_gains",
        out_shape=[jax.ShapeDtypeStruct(w.shape, F32) for w in ws for _ in range(4)],
    )(total, *ws, *ms, *vs)
    return [tuple(out[4 * i:4 * i + 4]) for i in range(n)]


def _adam_taps(total, first_col, device, w, m, v):
    _, n_taps, cw = w.shape
    col_block = lambda t, dev: (0, first_col // cw + t * N_DEV + dev[0])
    tap = pl.BlockSpec((None, 1, cw), lambda t, dev: (t, 0, 0))

    def body(dev_ref, t_ref, w_ref, m_ref, v_ref, g_ref, d_ref, mo_ref, vo_ref):
        g = t_ref[...]
        g_ref[...] = g
        d_ref[...], mo_ref[...], vo_ref[...] = _adamw(w_ref[...], g, m_ref[...], v_ref[...])

    shape3 = (n_taps, 1, cw)
    out = pl.pallas_call(
        body, name="adam_taps",
        grid_spec=pltpu.PrefetchScalarGridSpec(
            num_scalar_prefetch=1, grid=(n_taps,),
            in_specs=[pl.BlockSpec((1, cw), col_block), tap, tap, tap], out_specs=[tap] * 4),
        out_shape=[jax.ShapeDtypeStruct(shape3, F32)] * 4,
    )(device, total, w.reshape(shape3), m.reshape(shape3), v.reshape(shape3))
    return tuple(o.reshape(w.shape) for o in out)


def kernel(x, pre_mix_g, w_in, conv_w, q_norm_g, w_uq, kv_norm_g, w_ukv, conv_out_g, attn_out_g, w_o, post_mix_g, pre_mlp_g, w_up, w_down, post_mlp_g, loss_target, m_pre_mix_g, m_w_in, m_conv_w, m_q_norm_g, m_w_uq, m_kv_norm_g, m_w_ukv, m_conv_out_g, m_attn_out_g, m_w_o, m_post_mix_g, m_pre_mlp_g, m_w_up, m_w_down, m_post_mlp_g, v_pre_mix_g, v_w_in, v_conv_w, v_q_norm_g, v_w_uq, v_kv_norm_g, v_w_ukv, v_conv_out_g, v_attn_out_g, v_w_o, v_post_mix_g, v_pre_mlp_g, v_w_up, v_w_down, v_post_mlp_g):
    me = 4 * lax.axis_index("x") + 2 * lax.axis_index("y") + lax.axis_index("c")
    core = lax.axis_index("c").astype(jnp.int32).reshape(1)
    chip = (2 * lax.axis_index("x") + lax.axis_index("y")).astype(jnp.int32).reshape(1)
    gains = (pre_mix_g, q_norm_g, kv_norm_g, conv_out_g, attn_out_g, post_mix_g, pre_mlp_g, post_mlp_g)
    gain_m = (m_pre_mix_g, m_q_norm_g, m_kv_norm_g, m_conv_out_g, m_attn_out_g, m_post_mix_g, m_pre_mlp_g, m_post_mlp_g)
    gain_v = (v_pre_mix_g, v_q_norm_g, v_kv_norm_g, v_conv_out_g, v_attn_out_g, v_post_mix_g, v_pre_mlp_g, v_post_mlp_g)
    names = ("w_in", "w_uq", "w_ukv", "w_o", "w_up", "w_down")
    big = dict(zip(names, (w_in[0], w_uq[0], w_ukv[0], w_o[0], w_up[0], w_down[0])))
    big_m = dict(zip(names, (m_w_in[0], m_w_uq[0], m_w_ukv[0], m_w_o[0], m_w_up[0], m_w_down[0])))
    big_v = dict(zip(names, (v_w_in[0], v_w_uq[0], v_w_ukv[0], v_w_o[0], v_w_up[0], v_w_down[0])))
    n_heads = attn_out_g.shape[1] // HEAD
    n_taps = conv_w.shape[1]

    gathered = ("w_in", "conv", "w_uq", "w_ukv", "w_o", "w_up", "w_down")
    gather_groups = ((0, 1), (2, 3, 4), (5,), (6,))
    taps = jnp.pad(conv_w[0], ((0, SUBLANE - n_taps), (0, 0)))
    shards = [taps if nm == "conv" else big[nm].astype(BF16) for nm in gathered]
    sems1, shards, lands, token = _gather_start(shards, gather_groups)

    cols = lambda a: a.transpose(1, 0, 2).reshape(a.shape[1], N_DEV * a.shape[2])
    rows = lambda a: a.reshape(N_DEV * a.shape[1], a.shape[2])
    ready = {
        "w_in": lambda a: jnp.pad(cols(a), ((0, 0), (0, _in_pad(N_DEV * a.shape[2]) - N_DEV * a.shape[2]))),
        "conv": lambda a: cols(a)[:n_taps],
        "w_uq": lambda a: _permute_q_cols(cols(a), n_heads),
        "w_ukv": cols, "w_o": rows, "w_up": lambda a: a, "w_down": rows,
    }

    def weights(group, after):
        idx = gather_groups[group]
        sems2, mid = _gather_forward(f"gather_forward_{group}", [shards[i] for i in idx], [lands[i] for i in idx],
                                     *sems1[group], after)
        full = _gather_wait(f"gather_wait_{group}", mid, *sems2)
        out = []
        for i, a in zip(idx, full):
            a = lax.dynamic_update_index_in_dim(a, shards[i], me, 0)
            out.append(ready[gathered[i]](a))
        return out

    col_blocks = lambda g: g.reshape(g.shape[0], N_DEV, g.shape[1] // N_DEV).transpose(1, 0, 2)
    row_blocks = lambda g: g.reshape(N_DEV, g.shape[0] // N_DEV, g.shape[1])
    in_width = N_DEV * w_in.shape[2]
    grad_groups = (("w_down",), ("w_up",), ("w_o",), ("w_uq", "w_ukv"), ("w_in",))
    to_blocks = {
        "w_in": lambda g: col_blocks(g[:, :in_width]),
        "w_uq": lambda g: col_blocks(_unpermute_q_cols(g, n_heads)),
        "w_ukv": col_blocks, "w_o": row_blocks, "w_up": lambda g: g, "w_down": row_blocks,
    }
    in_flight = []

    def grads(group, arrays):
        nms = grad_groups[group]
        blocks = [to_blocks[nm](g) for nm, g in zip(nms, arrays)]
        received = _pair_exchange(f"pair_exchange_{group}", blocks)
        sums = [_pair_sum(f"pair_sum_{nm}", g, r, core) for nm, g, r in zip(nms, blocks, received)]
        sems, sums, parts, tok = _chip_send_start(f"chip_send_start_{group}", sums)
        in_flight.append((sems, sums, parts))
        return (tok,)

    grad_x, small = _local_step(x[0], loss_target[0], gains, weights, grads, first_after=(token,))

    big_out = {}

    def update(tag, first, last, after):
        groups = in_flight[first:last]
        parts = _chip_send_wait("chip_send_wait_" + tag, groups, after)
        nms = [nm for grp in grad_groups[first:last] for nm in grp]
        sums = [a for _, s, _ in groups for a in s]
        for nm, p, s in zip(nms, parts, sums):
            big_out[nm] = _sum_adam("adam_" + nm, p, s, chip, big[nm], big_m[nm], big_v[nm], after=after)
            after = (big_out[nm][0],)
        return after

    after = update("early", 0, len(in_flight) - 1, (grad_x,))
    total = _small_all_reduce(small, after=after)
    update("late", len(in_flight) - 1, len(in_flight), (total,))
    big_out = [big_out[nm] for nm in names]

    gain_out = _adam_gains(total, gains, gain_m, gain_v)
    taps_out = _adam_taps(total, sum(g.shape[1] for g in gains), me.astype(jnp.int32).reshape(1), conv_w, m_conv_w, v_conv_w)
    loss = total[0, total.shape[1] - 1]

    order = (0, "w_in", "conv", 1, "w_uq", 2, "w_ukv", 3, 4, "w_o", 5, 6, "w_up", "w_down", 7)
    by_name = dict(zip(names, big_out))
    outs = [loss, grad_x[None]]
    for kind in range(4):
        for item in order:
            if item == "conv":
                outs.append(taps_out[kind])
            elif isinstance(item, int):
                outs.append(gain_out[item][kind])
            else:
                outs.append(by_name[item][kind][None])
    return tuple(outs)
```

```python
import jax
import jax.numpy as jnp
from jax import lax
from jax.experimental import pallas as pl
from jax.experimental.pallas import tpu as pltpu

F32 = jnp.float32
BF16 = jnp.bfloat16

EPS = 1e-6
NEG_INF = -1e30
HEAD = 128
ROPE = 64
QK = HEAD + ROPE
CHUNK = 64
ROPE_THETA = 10000.0
ADAM_LR, ADAM_B1, ADAM_B2, ADAM_EPS, ADAM_WD, ADAM_STEP = 0.001, 0.9, 0.999, 1e-08, 0.01, 10

LANE = 128
SUBLANE = 8
VMEM_LIMIT_BYTES = 56 * 1024 * 1024

N_DEV = 8
N_CHIP = 4
MESH = pl.DeviceIdType.MESH


def _params(*sem):
    return pltpu.CompilerParams(dimension_semantics=sem, vmem_limit_bytes=VMEM_LIMIT_BYTES)


ANY = pl.BlockSpec(memory_space=pl.ANY)


def _call(body, *, in_specs, after=(), **kw):
    n_in, n_after = len(in_specs), len(after)

    def ordered(*refs):
        body(*refs[:n_in], *refs[n_in + n_after:])

    call = pl.pallas_call(ordered, in_specs=[*in_specs, *[ANY] * n_after], **kw)
    return lambda *operands: call(*operands, *after)


def _sublane_sum(v):
    r, w = v.shape
    return jnp.sum(v.reshape(r // SUBLANE, SUBLANE, w), axis=0)


def _rstd(x):
    return lax.rsqrt(jnp.mean(x * x, axis=-1, keepdims=True) + EPS)


def _rms_bwd(x, g, dy):
    r = _rstd(x)
    xh = x * r
    dxh = dy * g
    dx = r * (dxh - xh * jnp.mean(dxh * xh, axis=-1, keepdims=True))
    return dx, dy * xh


def _accumulate(ref, val, step):
    @pl.when(step == 0)
    def _():
        ref[...] = val

    @pl.when(step > 0)
    def _():
        ref[...] += val


NN = ((1,), (0,))
NT = ((1,), (1,))
TN = ((0,), (0,))


def _matmul(name, a, b, *, grid, a_spec, b_spec, out_shape, out_specs, contract, nk=1, acc_shape=None,
            extras=(), extra_specs=(), epilogue=None, after=()):
    multi = isinstance(out_shape, (tuple, list))
    out_shapes = tuple(out_shape) if multi else (out_shape,)
    n_out = len(out_shapes)
    n_extra = len(extras)

    def body(a_ref, b_ref, *rest):
        x_refs = rest[:n_extra]
        o_refs = rest[n_extra:n_extra + n_out]

        def emit(acc):
            vals = epilogue(acc, *[r[...] for r in x_refs]) if epilogue else (acc,)
            for r, v in zip(o_refs, vals):
                r[...] = v.astype(r.dtype)

        p = lax.dot_general(a_ref[...], b_ref[...], (contract, ((), ())), preferred_element_type=F32)
        if nk == 1:
            emit(p)
        else:
            acc_ref = rest[n_extra + n_out]
            k = pl.program_id(2)
            _accumulate(acc_ref, p, k)

            @pl.when(k == nk - 1)
            def _():
                emit(acc_ref[...])

    sem = ("parallel", "parallel") + (("arbitrary",) if nk > 1 else ())
    return _call(
        body, name=name, grid=grid, after=after,
        in_specs=[a_spec, b_spec, *extra_specs],
        out_specs=out_specs,
        out_shape=out_shape,
        scratch_shapes=[pltpu.VMEM(acc_shape, F32)] if nk > 1 else [],
        compiler_params=_params(*sem),
    )(a, b, *extras)


def _fit(n, tile):
    if n <= tile:
        return n
    t = tile - tile % LANE
    while n % t:
        t -= LANE
    return t


def _mm_nn(name, a, b, out_dtype, tm, tn):
    m, k = a.shape
    n = b.shape[1]
    tm, tn = _fit(m, tm), _fit(n, tn)
    return _matmul(name, a, b, grid=(m // tm, n // tn),
                   a_spec=pl.BlockSpec((tm, k), lambda i, j: (i, 0)),
                   b_spec=pl.BlockSpec((k, tn), lambda i, j: (0, j)),
                   out_shape=jax.ShapeDtypeStruct((m, n), out_dtype),
                   out_specs=pl.BlockSpec((tm, tn), lambda i, j: (i, j)), contract=NN)


def _mm_nt(name, a, b, out_dtype, tm, tn, after=()):
    m, k = a.shape
    n = b.shape[0]
    tm, tn = _fit(m, tm), _fit(n, tn)
    return _matmul(name, a, b, grid=(m // tm, n // tn), after=after,
                   a_spec=pl.BlockSpec((tm, k), lambda i, j: (i, 0)),
                   b_spec=pl.BlockSpec((tn, k), lambda i, j: (j, 0)),
                   out_shape=jax.ShapeDtypeStruct((m, n), out_dtype),
                   out_specs=pl.BlockSpec((tm, tn), lambda i, j: (i, j)), contract=NT)


def _mm_tn(name, a, b, out_dtype, tm, tn):
    s, m = a.shape
    n = b.shape[1]
    tm, tn = _fit(m, tm), _fit(n, tn)
    return _matmul(name, a, b, grid=(m // tm, n // tn),
                   a_spec=pl.BlockSpec((s, tm), lambda i, j: (0, i)),
                   b_spec=pl.BlockSpec((s, tn), lambda i, j: (0, j)),
                   out_shape=jax.ShapeDtypeStruct((m, n), out_dtype),
                   out_specs=pl.BlockSpec((tm, tn), lambda i, j: (i, j)), contract=TN)


ROWS = 256


def _row_spec(rows, width):
    return pl.BlockSpec((rows, width), lambda i: (i, 0))


def _fixed_spec(rows, width):
    return pl.BlockSpec((rows, width), lambda i: (0, 0))


def _rms_fwd(name, x, g, after=()):
    s, w = x.shape
    rows = min(ROWS, s)

    def body(x_ref, g_ref, o_ref):
        xv = x_ref[...]
        o_ref[...] = (xv * _rstd(xv) * g_ref[...]).astype(o_ref.dtype)

    return _call(
        body, name=name, grid=(s // rows,), after=after,
        in_specs=[_row_spec(rows, w), _fixed_spec(1, w)],
        out_specs=_row_spec(rows, w),
        out_shape=jax.ShapeDtypeStruct((s, w), BF16),
        compiler_params=_params("parallel"),
    )(x, g)


def _rms_bwd_call(name, x, g, dy, out_dtype, after=()):
    s, w = x.shape
    rows = min(ROWS, s)

    def body(x_ref, g_ref, dy_ref, dx_ref, dg_ref):
        dx, dgc = _rms_bwd(x_ref[...], g_ref[...], dy_ref[...].astype(F32))
        dx_ref[...] = dx.astype(dx_ref.dtype)
        _accumulate(dg_ref, _sublane_sum(dgc), pl.program_id(0))

    return _call(
        body, name=name, grid=(s // rows,), after=after,
        in_specs=[_row_spec(rows, w), _fixed_spec(1, w), _row_spec(rows, w)],
        out_specs=[_row_spec(rows, w), _fixed_spec(SUBLANE, w)],
        out_shape=[jax.ShapeDtypeStruct((s, w), out_dtype), jax.ShapeDtypeStruct((SUBLANE, w), F32)],
        compiler_params=_params("arbitrary"),
    )(x, g, dy)


def _mid_fwd(x, y, g_post, g_pre):
    s, w = x.shape
    rows = min(ROWS, s)

    def body(x_ref, y_ref, gp_ref, gq_ref, x2_ref, h2_ref):
        yv = y_ref[...]
        x2 = x_ref[...] + yv * _rstd(yv) * gp_ref[...]
        x2_ref[...] = x2
        h2_ref[...] = (x2 * _rstd(x2) * gq_ref[...]).astype(h2_ref.dtype)

    return pl.pallas_call(
        body, name="mid_fwd", grid=(s // rows,),
        in_specs=[_row_spec(rows, w), _row_spec(rows, w), _fixed_spec(1, w), _fixed_spec(1, w)],
        out_specs=[_row_spec(rows, w), _row_spec(rows, w)],
        out_shape=[jax.ShapeDtypeStruct((s, w), F32), jax.ShapeDtypeStruct((s, w), BF16)],
        compiler_params=_params("parallel"),
    )(x, y, g_post, g_pre)


def _head(m, x2, tgt, g):
    s, w = m.shape
    rows = min(ROWS, s)

    def body(m_ref, x2_ref, t_ref, g_ref, dout_ref, dm_ref, dg_ref, loss_ref):
        mv = m_ref[...]
        gv = g_ref[...]
        out = x2_ref[...] + mv * _rstd(mv) * gv
        err = out - t_ref[...]
        dout = err * (1.0 / w)
        dout_ref[...] = dout
        dm, dgc = _rms_bwd(mv, gv, dout)
        dm_ref[...] = dm.astype(dm_ref.dtype)
        sq = err * err
        lanes = sq[:, 0:LANE]
        for j in range(1, w // LANE):
            lanes = lanes + sq[:, j * LANE:(j + 1) * LANE]
        step = pl.program_id(0)
        _accumulate(dg_ref, _sublane_sum(dgc), step)
        _accumulate(loss_ref, _sublane_sum(lanes) * (0.5 / w), step)

    return pl.pallas_call(
        body, name="head", grid=(s // rows,),
        in_specs=[_row_spec(rows, w), _row_spec(rows, w), _row_spec(rows, w), _fixed_spec(1, w)],
        out_specs=[_row_spec(rows, w), _row_spec(rows, w), _fixed_spec(SUBLANE, w), _fixed_spec(SUBLANE, LANE)],
        out_shape=[jax.ShapeDtypeStruct((s, w), F32), jax.ShapeDtypeStruct((s, w), BF16),
                   jax.ShapeDtypeStruct((SUBLANE, w), F32), jax.ShapeDtypeStruct((SUBLANE, LANE), F32)],
        compiler_params=_params("arbitrary"),
    )(m, x2, tgt, g)


def _mid_bwd(x2, y, d_out, d_h2, g_pre, g_post, after=()):
    s, w = x2.shape
    rows = min(ROWS, s)

    def body(x2_ref, y_ref, dout_ref, dh2_ref, gq_ref, gp_ref, dx2_ref, dy_ref, dgq_ref, dgp_ref):
        dx, dgq = _rms_bwd(x2_ref[...], gq_ref[...], dh2_ref[...])
        dx2 = dout_ref[...] + dx
        dx2_ref[...] = dx2
        dy, dgp = _rms_bwd(y_ref[...], gp_ref[...], dx2)
        dy_ref[...] = dy.astype(dy_ref.dtype)
        step = pl.program_id(0)
        _accumulate(dgq_ref, _sublane_sum(dgq), step)
        _accumulate(dgp_ref, _sublane_sum(dgp), step)

    return _call(
        body, name="mid_bwd", grid=(s // rows,), after=after,
        in_specs=[_row_spec(rows, w)] * 4 + [_fixed_spec(1, w)] * 2,
        out_specs=[_row_spec(rows, w), _row_spec(rows, w), _fixed_spec(SUBLANE, w), _fixed_spec(SUBLANE, w)],
        out_shape=[jax.ShapeDtypeStruct((s, w), F32), jax.ShapeDtypeStruct((s, w), BF16),
                   jax.ShapeDtypeStruct((SUBLANE, w), F32), jax.ShapeDtypeStruct((SUBLANE, w), F32)],
        compiler_params=_params("arbitrary"),
    )(x2, y, d_out, d_h2, g_pre, g_post)


def _first_bwd(x, g, d_h1, d_x2, after=()):
    s, w = x.shape
    rows = min(ROWS, s)

    def body(x_ref, g_ref, dh_ref, dx2_ref, dx_ref, dg_ref):
        dx, dgc = _rms_bwd(x_ref[...], g_ref[...], dh_ref[...])
        dx_ref[...] = dx2_ref[...] + dx
        _accumulate(dg_ref, _sublane_sum(dgc), pl.program_id(0))

    return _call(
        body, name="first_bwd", grid=(s // rows,), after=after,
        in_specs=[_row_spec(rows, w), _fixed_spec(1, w), _row_spec(rows, w), _row_spec(rows, w)],
        out_specs=[_row_spec(rows, w), _fixed_spec(SUBLANE, w)],
        out_shape=[jax.ShapeDtypeStruct((s, w), F32), jax.ShapeDtypeStruct((SUBLANE, w), F32)],
        compiler_params=_params("arbitrary"),
    )(x, g, d_h1, d_x2)


def _shift_down(v, k):
    t = lax.broadcasted_iota(jnp.int32, v.shape, 0)
    return jnp.where(t >= k, pltpu.roll(v, k, 0), 0.0)


def _shift_up(v, k):
    n = v.shape[0]
    t = lax.broadcasted_iota(jnp.int32, v.shape, 0)
    return jnp.where(t < n - k, pltpu.roll(v, n - k, 0), 0.0)


def _conv_core(u, b, c, w):
    z = c * u
    conv = w[0:1, :] * _shift_down(z, 2) + w[1:2, :] * _shift_down(z, 1) + w[2:3, :] * z
    return z, conv, b * conv


def _conv_fwd(proj, conv_w, g, n_groups):
    s = proj.shape[0]

    def body(u_ref, b_ref, c_ref, w_ref, g_ref, o_ref):
        _, _, yr = _conv_core(u_ref[...], b_ref[...], c_ref[...], w_ref[...])
        o_ref[...] = (yr * _rstd(yr) * g_ref[...]).astype(o_ref.dtype)

    col = lambda k: pl.BlockSpec((s, HEAD), lambda i: (0, k * n_groups + i))
    return pl.pallas_call(
        body, name="conv_fwd", grid=(n_groups,),
        in_specs=[col(0), col(1), col(2), pl.BlockSpec((3, HEAD), lambda i: (0, i)), pl.BlockSpec((1, HEAD), lambda i: (0, i))],
        out_specs=pl.BlockSpec((s, HEAD), lambda i: (0, i)),
        out_shape=jax.ShapeDtypeStruct((s, n_groups * HEAD), BF16),
        compiler_params=_params("parallel"),
    )(proj, proj, proj, conv_w, g)


def _conv_bwd(proj, d_mix, conv_w, g, n_groups):
    s = proj.shape[0]
    width = n_groups * HEAD

    def body(u_ref, b_ref, c_ref, dy_ref, w_ref, g_ref, du_ref, db_ref, dc_ref, dg_ref, dw_ref):
        u, b, c, w = u_ref[...], b_ref[...], c_ref[...], w_ref[...]
        z, conv, yr = _conv_core(u, b, c, w)
        dyr, dgc = _rms_bwd(yr, g_ref[...], dy_ref[...])
        dconv = dyr * b
        db_ref[...] = (dyr * conv).astype(db_ref.dtype)
        dz = w[2:3, :] * dconv + w[1:2, :] * _shift_up(dconv, 1) + w[0:1, :] * _shift_up(dconv, 2)
        dc_ref[...] = (dz * u).astype(dc_ref.dtype)
        du_ref[...] = (dz * c).astype(du_ref.dtype)
        dg_ref[...] = _sublane_sum(dgc)
        dw_ref[0] = _sublane_sum(dconv * _shift_down(z, 2))
        dw_ref[1] = _sublane_sum(dconv * _shift_down(z, 1))
        dw_ref[2] = _sublane_sum(dconv * z)

    col = lambda k: pl.BlockSpec((s, HEAD), lambda i: (0, k * n_groups + i))
    grp = pl.BlockSpec((s, HEAD), lambda i: (0, i))
    return pl.pallas_call(
        body, name="conv_bwd", grid=(n_groups,),
        in_specs=[col(0), col(1), col(2), grp, pl.BlockSpec((3, HEAD), lambda i: (0, i)), pl.BlockSpec((1, HEAD), lambda i: (0, i))],
        out_specs=[grp, grp, grp, pl.BlockSpec((SUBLANE, HEAD), lambda i: (0, i)),
                   pl.BlockSpec((3, SUBLANE, HEAD), lambda i: (0, 0, i))],
        out_shape=[jax.ShapeDtypeStruct((s, width), BF16)] * 3
        + [jax.ShapeDtypeStruct((SUBLANE, width), F32), jax.ShapeDtypeStruct((3, SUBLANE, width), F32)],
        compiler_params=_params("parallel"),
    )(proj, proj, proj, d_mix, conv_w, g)


def _rope_tables(s, n_heads):
    pos = jnp.arange(s, dtype=F32)
    inv_freq = jnp.power(ROPE_THETA, -jnp.arange(0, ROPE, 2, dtype=F32) / ROPE)
    ang = pos[:, None] * inv_freq[None, :]
    cos, sin = jnp.cos(ang), jnp.sin(ang)
    cs = jnp.concatenate([cos, cos], axis=1)
    sn = jnp.concatenate([-sin, sin], axis=1)
    pad = jnp.zeros((s, LANE - ROPE), F32)
    return (jnp.tile(cs, (1, n_heads)), jnp.tile(sn, (1, n_heads)),
            jnp.concatenate([cs, pad], axis=1), jnp.concatenate([sn, pad], axis=1))


def _swap_halves(v):
    w = v.shape[1]
    lane = lax.broadcasted_iota(jnp.int32, v.shape, 1)
    first = (lane % ROPE) < (ROPE // 2)
    return jnp.where(first, pltpu.roll(v, w - ROPE // 2, 1), pltpu.roll(v, ROPE // 2, 1))


def _pack_heads(q, kv, kr, tables, n_heads):
    s = q.shape[0]
    rows = min(ROWS, s)
    cq, sq, ck, sk = tables
    wq = n_heads * ROPE

    def body(q_ref, kv_ref, kr_ref, cq_ref, sq_ref, ck_ref, sk_ref, qo_ref, ko_ref, vo_ref):
        qr = q_ref[:, n_heads * HEAD:]
        qr = qr * cq_ref[...] + _swap_halves(qr) * sq_ref[...]
        krv = kr_ref[...]
        krv = krv * ck_ref[...] + _swap_halves(krv) * sk_ref[...]
        for h in range(n_heads):
            qo_ref[h] = jnp.concatenate([q_ref[:, h * HEAD:(h + 1) * HEAD], qr[:, h * ROPE:(h + 1) * ROPE]], axis=1).astype(BF16)
            ko_ref[h] = jnp.concatenate([kv_ref[:, 2 * h * HEAD:(2 * h + 1) * HEAD], krv[:, :ROPE]], axis=1).astype(BF16)
            vo_ref[h] = kv_ref[:, (2 * h + 1) * HEAD:(2 * h + 2) * HEAD].astype(BF16)

    hs = lambda w: pl.BlockSpec((n_heads, rows, w), lambda i: (0, i, 0))
    return pl.pallas_call(
        body, name="pack_heads", grid=(s // rows,),
        in_specs=[_row_spec(rows, q.shape[1]), _row_spec(rows, kv.shape[1]), _row_spec(rows, LANE),
                  _row_spec(rows, wq), _row_spec(rows, wq), _row_spec(rows, LANE), _row_spec(rows, LANE)],
        out_specs=[hs(QK), hs(QK), hs(HEAD)],
        out_shape=[jax.ShapeDtypeStruct((n_heads, s, QK), BF16), jax.ShapeDtypeStruct((n_heads, s, QK), BF16),
                   jax.ShapeDtypeStruct((n_heads, s, HEAD), BF16)],
        compiler_params=_params("parallel"),
    )(q, kv, kr, cq, sq, ck, sk)


def _unpack_heads(dq, dk, dv, tables, n_heads):
    s = dq.shape[1]
    rows = min(ROWS, s)
    cq, sq, ck, sk = tables
    wq = n_heads * ROPE

    def body(dq_ref, dk_ref, dv_ref, cq_ref, sq_ref, ck_ref, sk_ref, qo_ref, kvo_ref, kro_ref):
        dqr = jnp.concatenate([dq_ref[h][:, HEAD:] for h in range(n_heads)], axis=1)
        dqr = dqr * cq_ref[...] - _swap_halves(dqr) * sq_ref[...]
        dkr = dk_ref[0][:, HEAD:]
        for h in range(1, n_heads):
            dkr = dkr + dk_ref[h][:, HEAD:]
        dkr = jnp.concatenate([dkr, jnp.zeros((rows, LANE - ROPE), F32)], axis=1)
        dkr = dkr * ck_ref[...] - _swap_halves(dkr) * sk_ref[...]
        kro_ref[...] = dkr.astype(kro_ref.dtype)
        qo_ref[:, n_heads * HEAD:] = dqr.astype(qo_ref.dtype)
        for h in range(n_heads):
            qo_ref[:, h * HEAD:(h + 1) * HEAD] = dq_ref[h][:, :HEAD].astype(qo_ref.dtype)
            kvo_ref[:, 2 * h * HEAD:(2 * h + 1) * HEAD] = dk_ref[h][:, :HEAD].astype(kvo_ref.dtype)
            kvo_ref[:, (2 * h + 1) * HEAD:(2 * h + 2) * HEAD] = dv_ref[h].astype(kvo_ref.dtype)

    hs = lambda w: pl.BlockSpec((n_heads, rows, w), lambda i: (0, i, 0))
    return pl.pallas_call(
        body, name="unpack_heads", grid=(s // rows,),
        in_specs=[hs(QK), hs(QK), hs(HEAD), _row_spec(rows, wq), _row_spec(rows, wq), _row_spec(rows, LANE), _row_spec(rows, LANE)],
        out_specs=[_row_spec(rows, n_heads * QK), _row_spec(rows, 2 * n_heads * HEAD), _row_spec(rows, LANE)],
        out_shape=[jax.ShapeDtypeStruct((s, n_heads * QK), BF16), jax.ShapeDtypeStruct((s, 2 * n_heads * HEAD), BF16),
                   jax.ShapeDtypeStruct((s, LANE), BF16)],
        compiler_params=_params("parallel"),
    )(dq, dk, dv, cq, sq, ck, sk)


TQ = 256


def _probs(q, k):
    tq, n_keys = q.shape[0], k.shape[0]
    sc = lax.dot_general(q, k, (NT, ((), ())), preferred_element_type=F32) * (QK ** -0.5)
    row = lax.broadcasted_iota(jnp.int32, (tq, tq), 0)
    col = lax.broadcasted_iota(jnp.int32, (tq, tq), 1)
    own = jnp.where(col // CHUNK <= row // CHUNK, sc[:, n_keys - tq:], NEG_INF)
    sc = own if n_keys == tq else jnp.concatenate([sc[:, :n_keys - tq], own], axis=1)
    e = jnp.exp(sc - jnp.max(sc, axis=-1, keepdims=True))
    return e / jnp.sum(e, axis=-1, keepdims=True)


def _per_query_block(n_blocks, branch):
    i = pl.program_id(1)
    for c in range(n_blocks):
        pl.when(i == c)(lambda c=c: branch(c))


def _attn_fwd(q, k, v, g):
    n_heads, s, _ = q.shape
    tq = min(TQ, s)
    assert tq % CHUNK == 0 and s % tq == 0

    def body(q_ref, k_ref, v_ref, g_ref, o_ref, y_ref):
        def branch(c):
            n_keys = (c + 1) * tq
            p = _probs(q_ref[...], k_ref[0:n_keys, :])
            o = jnp.dot(p.astype(BF16), v_ref[0:n_keys, :], preferred_element_type=F32)
            o_ref[...] = o
            y_ref[...] = (o * _rstd(o) * g_ref[...]).astype(y_ref.dtype)

        _per_query_block(s // tq, branch)

    return pl.pallas_call(
        body, name="attn_fwd", grid=(n_heads, s // tq),
        in_specs=[pl.BlockSpec((None, tq, QK), lambda h, i: (h, i, 0)),
                  pl.BlockSpec((None, s, QK), lambda h, i: (h, 0, 0)),
                  pl.BlockSpec((None, s, HEAD), lambda h, i: (h, 0, 0)),
                  pl.BlockSpec((1, HEAD), lambda h, i: (0, h))],
        out_specs=[pl.BlockSpec((None, tq, HEAD), lambda h, i: (h, i, 0)),
                   pl.BlockSpec((tq, HEAD), lambda h, i: (i, h))],
        out_shape=[jax.ShapeDtypeStruct((n_heads, s, HEAD), F32), jax.ShapeDtypeStruct((s, n_heads * HEAD), BF16)],
        compiler_params=_params("parallel", "parallel"),
    )(q, k, v, g)


def _attn_bwd(q, k, v, o, d_mix, g, col0, after=()):
    n_heads, s, _ = q.shape
    tq = min(TQ, s)

    def body(q_ref, k_ref, v_ref, o_ref, dy_ref, g_ref, dq_ref, dk_ref, dv_ref, dg_ref):
        def branch(c):
            n_keys = (c + 1) * tq
            qv, kv_, vv = q_ref[...], k_ref[0:n_keys, :], v_ref[0:n_keys, :]
            do, dgc = _rms_bwd(o_ref[...], g_ref[...], dy_ref[...])
            do = do.astype(BF16)
            p = _probs(qv, kv_)
            dp = lax.dot_general(do, vv, (NT, ((), ())), preferred_element_type=F32)
            ds = (p * (dp - jnp.sum(p * dp, axis=-1, keepdims=True)) * (QK ** -0.5)).astype(BF16)
            dq_ref[...] = jnp.dot(ds, kv_, preferred_element_type=F32)
            dk = lax.dot_general(ds, qv, (TN, ((), ())), preferred_element_type=F32)
            dv = lax.dot_general(p.astype(BF16), do, (TN, ((), ())), preferred_element_type=F32)
            if c == 0:
                dk_ref[...] = jnp.zeros_like(dk_ref)
                dv_ref[...] = jnp.zeros_like(dv_ref)
                dk_ref[0:n_keys, :] = dk
                dv_ref[0:n_keys, :] = dv
                dg_ref[...] = _sublane_sum(dgc)
            else:
                dk_ref[0:n_keys, :] += dk
                dv_ref[0:n_keys, :] += dv
                dg_ref[...] += _sublane_sum(dgc)

        _per_query_block(s // tq, branch)

    c0 = col0 // HEAD
    return _call(
        body, name="attn_bwd", grid=(n_heads, s // tq), after=after,
        in_specs=[pl.BlockSpec((None, tq, QK), lambda h, i: (h, i, 0)),
                  pl.BlockSpec((None, s, QK), lambda h, i: (h, 0, 0)),
                  pl.BlockSpec((None, s, HEAD), lambda h, i: (h, 0, 0)),
                  pl.BlockSpec((None, tq, HEAD), lambda h, i: (h, i, 0)),
                  pl.BlockSpec((tq, HEAD), lambda h, i: (i, c0 + h)),
                  pl.BlockSpec((1, HEAD), lambda h, i: (0, h))],
        out_specs=[pl.BlockSpec((None, tq, QK), lambda h, i: (h, i, 0)),
                   pl.BlockSpec((None, s, QK), lambda h, i: (h, 0, 0)),
                   pl.BlockSpec((None, s, HEAD), lambda h, i: (h, 0, 0)),
                   pl.BlockSpec((SUBLANE, HEAD), lambda h, i: (0, h))],
        out_shape=[jax.ShapeDtypeStruct((n_heads, s, QK), F32), jax.ShapeDtypeStruct((n_heads, s, QK), F32),
                   jax.ShapeDtypeStruct((n_heads, s, HEAD), F32), jax.ShapeDtypeStruct((SUBLANE, n_heads * HEAD), F32)],
        compiler_params=_params("parallel", "arbitrary"),
    )(q, k, v, o, d_mix, g)


TILE_M = 1024
TILE_N = 1024


def _up_fwd(h2, w_up):
    s, d = h2.shape
    nb, _, fb = w_up.shape
    tm = min(TILE_M,s)

    def epilogue(acc):
        r = jnp.maximum(acc, 0.0)
        return r * r, r

    blk = pl.BlockSpec((tm, fb), lambda i, j: (i, j))
    return _matmul("up_fwd", h2, w_up, grid=(s // tm, nb),
                   a_spec=pl.BlockSpec((tm, d), lambda i, j: (i, 0)),
                   b_spec=pl.BlockSpec((None, d, fb), lambda i, j: (j, 0, 0)),
                   out_shape=[jax.ShapeDtypeStruct((s, nb * fb), BF16)] * 2, out_specs=[blk, blk],
                   contract=NN, epilogue=epilogue)


def _down_fwd(a, w_down):
    s, f = a.shape
    d = w_down.shape[1]
    tm, tn, tk = min(TILE_M,s), min(TILE_N,d), 2048
    nk = f // tk
    return _matmul("down_fwd", a, w_down, grid=(s // tm, d // tn, nk),
                   a_spec=pl.BlockSpec((tm, tk), lambda i, j, k: (i, k)),
                   b_spec=pl.BlockSpec((tk, tn), lambda i, j, k: (k, j)),
                   out_shape=jax.ShapeDtypeStruct((s, d), F32),
                   out_specs=pl.BlockSpec((tm, tn), lambda i, j, k: (i, j)),
                   contract=NN, nk=nk, acc_shape=(tm, tn))


def _down_bwd_act(d_m, w_down, r, after=()):
    s, d = d_m.shape
    f = w_down.shape[0]
    tm, tn = min(TILE_M,s), min(TILE_N,f)
    blk = pl.BlockSpec((tm, tn), lambda i, j: (i, j))
    return _matmul("down_bwd_act", d_m, w_down, grid=(s // tm, f // tn), after=after,
                   a_spec=pl.BlockSpec((tm, d), lambda i, j: (i, 0)),
                   b_spec=pl.BlockSpec((tn, d), lambda i, j: (j, 0)),
                   out_shape=jax.ShapeDtypeStruct((s, f), BF16), out_specs=blk, contract=NT,
                   extras=(r,), extra_specs=(blk,),
                   epilogue=lambda acc, rv: (acc * (2.0 * rv.astype(F32)),))


def _up_bwd_act(d_up, w_up, after=()):
    s, _ = d_up.shape
    nb, d, fb = w_up.shape
    tm, tn = min(TILE_M,s), min(TILE_N,d)
    return _matmul("up_bwd_act", d_up, w_up, grid=(s // tm, d // tn, nb), after=after,
                   a_spec=pl.BlockSpec((tm, fb), lambda i, j, k: (i, k)),
                   b_spec=pl.BlockSpec((None, tn, fb), lambda i, j, k: (k, j, 0)),
                   out_shape=jax.ShapeDtypeStruct((s, d), F32),
                   out_specs=pl.BlockSpec((tm, tn), lambda i, j, k: (i, j)),
                   contract=NT, nk=nb, acc_shape=(tm, tn))


def _half_grad(name, a, b, core, home, received, after, *, grid, a_block, a_map, b_block, b_map, o_block, o_map, out_shape):
    n_after = len(after)
    pick = (lambda ref: ref[0]) if home else (lambda ref: 1 - ref[0])

    def body(core_ref, a_ref, b_ref, *rest):
        acc = lax.dot_general(a_ref[...], b_ref[...], (TN, ((), ())), preferred_element_type=F32)
        if received is not None:
            acc = acc + rest[0][...].astype(F32)
        rest[-1][...] = acc.astype(rest[-1].dtype)

    wrap = lambda fn: (lambda i, j, core_ref: fn(i, j, pick(core_ref)))
    o_spec = pl.BlockSpec(o_block, wrap(o_map))
    extra = [] if received is None else [o_spec]
    operands = [] if received is None else [received]
    return pl.pallas_call(
        body, name=name,
        grid_spec=pltpu.PrefetchScalarGridSpec(
            num_scalar_prefetch=1, grid=grid,
            in_specs=[pl.BlockSpec(a_block, wrap(a_map)), pl.BlockSpec(b_block, wrap(b_map))] + extra + [ANY] * n_after,
            out_specs=o_spec),
        out_shape=out_shape,
        compiler_params=_params("parallel", "parallel"),
    )(core, a, b, *operands, *after)


def _down_half_grad(name, a, d_m, core, home, received=None, after=()):
    s, f = a.shape
    d = d_m.shape[1]
    r = f // N_DEV
    tn = min(TILE_N, d)
    return _half_grad(name, a, d_m, core, home, received, after, grid=(N_CHIP, d // tn),
                      a_block=(s, r), a_map=lambda k, j, p: (0, 2 * k + p),
                      b_block=(s, tn), b_map=lambda k, j, p: (0, j),
                      o_block=(None, r, tn), o_map=lambda k, j, p: (k, 0, j),
                      out_shape=jax.ShapeDtypeStruct((N_CHIP, r, d), BF16))


def _up_half_grad(name, h2, d_up, core, home, received=None, after=()):
    s, d = h2.shape
    fb = d_up.shape[1] // N_DEV
    tm = min(TILE_M, d)
    return _half_grad(name, h2, d_up, core, home, received, after, grid=(d // tm, N_CHIP),
                      a_block=(s, tm), a_map=lambda i, k, p: (0, i),
                      b_block=(s, fb), b_map=lambda i, k, p: (0, 2 * k + p),
                      o_block=(None, tm, fb), o_map=lambda i, k, p: (k, i, 0),
                      out_shape=jax.ShapeDtypeStruct((N_CHIP, d, fb), BF16))


def _in_pad(in_width):
    return -(-in_width // LANE) * LANE


def _permute_q_cols(w_uq, n_heads):
    r = w_uq.shape[0]
    w3 = w_uq.reshape(r, n_heads, QK)
    return jnp.concatenate([w3[:, :, :HEAD].reshape(r, n_heads * HEAD), w3[:, :, HEAD:].reshape(r, n_heads * ROPE)], axis=1)


def _unpermute_q_cols(w, n_heads):
    r = w.shape[0]
    nope = w[:, :n_heads * HEAD].reshape(r, n_heads, HEAD)
    rope = w[:, n_heads * HEAD:].reshape(r, n_heads, ROPE)
    return jnp.concatenate([nope, rope], axis=2).reshape(r, n_heads * QK)


def _local_step(x, tgt, gains, weights, grads, first_after=()):
    pre_mix_g, q_norm_g, kv_norm_g, conv_out_g, attn_out_g, post_mix_g, pre_mlp_g, post_mlp_g = gains
    s, d = x.shape
    conv_width = conv_out_g.shape[1]
    n_groups = conv_width // HEAD
    r_q, r_kv = q_norm_g.shape[1], kv_norm_g.shape[1]
    n_heads = attn_out_g.shape[1] // HEAD
    c_q0 = 3 * conv_width
    c_kv0 = c_q0 + r_q
    c_kr0 = c_kv0 + r_kv
    in_pad = _in_pad(c_kr0 + ROPE)
    tn_in = in_pad // 5 if in_pad % (5 * LANE) == 0 else LANE
    tables = _rope_tables(s, n_heads)

    h1 = _rms_fwd("pre_mix_norm", x, pre_mix_g, after=first_after)
    weights.forward(0, (h1, *tables))
    w_in_p, conv_w = weights.ready(0, ())
    proj = _mm_nn("in_proj", h1, w_in_p, F32, TILE_M,tn_in)
    y_conv = _conv_fwd(proj, conv_w, conv_out_g, n_groups)
    c_q = proj[:, c_q0:c_kv0]
    c_kv = proj[:, c_kv0:c_kr0]
    qn = _rms_fwd("q_norm", c_q, q_norm_g)
    kvn = _rms_fwd("kv_norm", c_kv, kv_norm_g)
    weights.forward(1, (y_conv, qn, kvn))
    w_uq_p, w_ukv, w_o = weights.ready(1, ())
    q = _mm_nn("q_up", qn, w_uq_p, F32, TILE_M, TILE_N)
    kv = _mm_nn("kv_up", kvn, w_ukv, F32, TILE_M, TILE_N)
    weights.forward(2, (q, kv))
    kr = proj[:, c_kr0:c_kr0 + LANE]
    qh, kh, vh = _pack_heads(q, kv, kr, tables, n_heads)
    o, y_attn = _attn_fwd(qh, kh, vh, attn_out_g)
    mix = jnp.concatenate([y_conv, y_attn], axis=1)
    y = _mm_nn("out_proj", mix, w_o, F32, TILE_M, TILE_N)
    weights.forward(3, (y,))
    x2, h2 = _mid_fwd(x, y, post_mix_g, pre_mlp_g)
    (w_up,) = weights.ready(2, (h2,))
    a, r = _up_fwd(h2, w_up)
    (w_down,) = weights.ready(3, (a,))
    m = _down_fwd(a, w_down)

    d_out, d_m, dg_post_mlp, loss_part = _head(m, x2, tgt, post_mlp_g)
    core = grads.core
    away = _down_half_grad("down_bwd_w_away", a, d_m, core, home=False)
    d_up = _down_bwd_act(d_m, w_down, r, after=grads.send_away(0, away))
    sums = _down_half_grad("down_bwd_w_home", a, d_m, core, home=True, received=grads.received(0, (d_up,)))
    away = _up_half_grad("up_bwd_w_away", h2, d_up, core, home=False, after=grads.send_sums(0, (sums,)))
    d_h2 = _up_bwd_act(d_up, w_up, after=grads.send_away(1, away))
    sums = _up_half_grad("up_bwd_w_home", h2, d_up, core, home=True, received=grads.received(1, (d_h2,)))
    d_x2, d_y, dg_pre_mlp, dg_post_mix = _mid_bwd(x2, y, d_out, d_h2, pre_mlp_g, post_mix_g, after=grads.send_sums(1, (sums,)))
    d_mix = _mm_nt("out_proj_bwd_act", d_y, w_o, F32, TILE_M, TILE_N)
    gw_o = _mm_tn("out_proj_bwd_w", mix, d_y, BF16, TILE_M, TILE_N)
    dqh, dkh, dvh, dg_attn = _attn_bwd(qh, kh, vh, o, d_mix, attn_out_g, conv_width, after=grads.full(2, (gw_o,)))
    d_q, d_kv, d_kr = _unpack_heads(dqh, dkh, dvh, tables, n_heads)
    d_qn = _mm_nt("q_up_bwd_act", d_q, w_uq_p, F32, TILE_M, TILE_N)
    d_kvn = _mm_nt("kv_up_bwd_act", d_kv, w_ukv, F32, TILE_M, TILE_N)
    gw_uq_p = _mm_tn("q_up_bwd_w", qn, d_q, BF16, TILE_M, TILE_N)
    gw_ukv = _mm_tn("kv_up_bwd_w", kvn, d_kv, BF16, TILE_M, TILE_N)
    d_cq, dg_q = _rms_bwd_call("q_norm_bwd", c_q, q_norm_g, d_qn, BF16, after=grads.full(3, (gw_uq_p, gw_ukv)))
    d_ckv, dg_kv = _rms_bwd_call("kv_norm_bwd", c_kv, kv_norm_g, d_kvn, BF16)
    d_u, d_b, d_c, dg_conv, dw_conv = _conv_bwd(proj, d_mix, conv_w, conv_out_g, n_groups)
    d_proj = jnp.concatenate([d_u, d_b, d_c, d_cq, d_ckv, d_kr[:, :in_pad - c_kr0]], axis=1)
    gw_in_p = _mm_tn("in_proj_bwd_w", h1, d_proj, BF16, TILE_M, tn_in)
    d_h1 = _mm_nt("in_proj_bwd_act", d_proj, w_in_p, F32, TILE_M, 512, after=grads.full(4, (gw_in_p,)))
    grad_x, dg_pre_mix = _first_bwd(x, pre_mix_g, d_h1, d_x2)

    small = [dg_pre_mix, dg_q, dg_kv, dg_conv, dg_attn, dg_post_mix, dg_pre_mlp, dg_post_mlp,
             dw_conv[0], dw_conv[1], dw_conv[2], loss_part]
    return grad_x, jnp.concatenate(small, axis=1)


HBM = pl.BlockSpec(memory_space=pltpu.HBM)
SEM = pl.BlockSpec(memory_space=pltpu.SEMAPHORE)
IN_VMEM = pl.BlockSpec(memory_space=pltpu.VMEM)
SPLIT = pltpu.CompilerParams(has_side_effects=pltpu.SideEffectType.DATAFLOW_SIDE_EFFECTING)


def _in_hbm(a):
    return pltpu.with_memory_space_constraint(a, pltpu.HBM)


def _hbm_like(a):
    return pltpu.HBM(a.shape, a.dtype)


def _place():
    x, y, c = lax.axis_index("x"), lax.axis_index("y"), lax.axis_index("c")
    other_chips = [(1 - x, y), (x, 1 - y), (1 - x, 1 - y)]
    return x, y, c, other_chips


def _block(px, py, pc):
    return 4 * px + 2 * py + pc


def _gather_start(name, shards, groups):
    n, ng = len(shards), len(groups)
    lands = [lax.empty((N_DEV, *a.shape), a.dtype) for a in shards]

    def body(*refs):
        src, land = refs[:n], refs[n:2 * n]
        sems, token = refs[2 * n:2 * n + 2 * ng], refs[-1]
        x, y, c, chips = _place()
        targets = [(x, y, 1 - c)] + [(*chip, c) for chip in chips]
        for gi, group in enumerate(groups):
            for i, w in enumerate(group):
                for k, to in enumerate(targets):
                    pltpu.make_async_remote_copy(
                        src_ref=src[w], dst_ref=land[w].at[_block(x, y, c)],
                        send_sem=sems[2 * gi].at[4 * i + k], recv_sem=sems[2 * gi + 1].at[4 * i + k],
                        device_id=to, device_id_type=MESH).start()
        token[...] = jnp.zeros_like(token)

    sem_shapes = [pltpu.SemaphoreType.DMA((4 * len(g),)) for g in groups for _ in range(2)]
    out = pl.pallas_call(
        body, name=name,
        in_specs=[HBM] * (2 * n),
        out_specs=[SEM] * (2 * ng) + [HBM] * (2 * n) + [IN_VMEM],
        out_shape=sem_shapes + [_hbm_like(a) for a in shards] + [_hbm_like(a) for a in lands]
        + [jax.ShapeDtypeStruct((SUBLANE, LANE), F32)],
        input_output_aliases={i: 2 * ng + i for i in range(2 * n)},
        compiler_params=SPLIT,
    )(*[_in_hbm(a) for a in shards], *[_in_hbm(a) for a in lands])
    sems = [(out[2 * gi], out[2 * gi + 1]) for gi in range(ng)]
    return sems, out[2 * ng:2 * ng + n], out[2 * ng + n:2 * ng + 2 * n], out[-1]


def _gather_forward(name, shards, lands, send1, recv1, after):
    n = len(lands)

    def body(*refs):
        src, land = refs[:n], refs[n:2 * n]
        s1, r1 = refs[2 * n], refs[2 * n + 1]
        s2, r2 = refs[2 * n + 2 + len(after)], refs[2 * n + 3 + len(after)]
        x, y, c, chips = _place()
        me, sibling = (x, y, c), (x, y, 1 - c)
        for j, chip in enumerate(chips):
            for i in range(n):
                blk = land[i].at[_block(*chip, c)]
                pltpu.make_async_remote_copy(src_ref=blk, dst_ref=blk, send_sem=s1.at[4 * i + 1 + j], recv_sem=r1.at[4 * i + 1 + j],
                                             device_id=me, device_id_type=MESH).wait_recv()
                pltpu.make_async_remote_copy(src_ref=blk, dst_ref=blk, send_sem=s2.at[3 * i + j], recv_sem=r2.at[3 * i + j],
                                             device_id=sibling, device_id_type=MESH).start()
        for i in range(n):
            blk = land[i].at[_block(x, y, 1 - c)]
            pltpu.make_async_remote_copy(src_ref=blk, dst_ref=blk, send_sem=s1.at[4 * i], recv_sem=r1.at[4 * i],
                                         device_id=me, device_id_type=MESH).wait_recv()
            for k in range(4):
                pltpu.make_async_remote_copy(src_ref=src[i], dst_ref=land[i].at[_block(x, y, c)], send_sem=s1.at[4 * i + k],
                                             recv_sem=r1.at[4 * i + k], device_id=sibling, device_id_type=MESH).wait_send()

    sem = pltpu.SemaphoreType.DMA((3 * n,))
    out = pl.pallas_call(
        body, name=name,
        in_specs=[HBM] * (2 * n) + [SEM, SEM] + [ANY] * len(after),
        out_specs=[SEM, SEM] + [HBM] * n,
        out_shape=[sem, sem] + [_hbm_like(a) for a in lands],
        input_output_aliases={n + i: 2 + i for i in range(n)},
        compiler_params=SPLIT,
    )(*shards, *lands, send1, recv1, *after)
    return (out[0], out[1]), out[2:]


def _gather_wait(name, lands, send2, recv2, after):
    n = len(lands)

    def body(*refs):
        land, s2, r2 = refs[:n], refs[n], refs[n + 1]
        x, y, c, chips = _place()
        me = (x, y, c)
        for i in range(n):
            for j, chip in enumerate(chips):
                got = land[i].at[_block(*chip, 1 - c)]
                pltpu.make_async_remote_copy(src_ref=got, dst_ref=got, send_sem=s2.at[3 * i + j], recv_sem=r2.at[3 * i + j],
                                             device_id=me, device_id_type=MESH).wait_recv()
                sent = land[i].at[_block(*chip, c)]
                pltpu.make_async_remote_copy(src_ref=sent, dst_ref=sent, send_sem=s2.at[3 * i + j], recv_sem=r2.at[3 * i + j],
                                             device_id=me, device_id_type=MESH).wait_send()

    return pl.pallas_call(
        body, name=name,
        in_specs=[HBM] * n + [SEM, SEM] + [ANY] * len(after), out_specs=[HBM] * n, out_shape=[_hbm_like(a) for a in lands],
        input_output_aliases={i: i for i in range(n)},
        compiler_params=SPLIT,
    )(*lands, send2, recv2, *after)


def _pair_exchange(name, grads):
    n = len(grads)

    def body(*refs):
        ins, recv = refs[:n], refs[n:2 * n]
        send_sems, recv_sems = refs[2 * n:]
        x, y, c, _ = _place()
        sends = []
        for w in range(n):
            for k in range(N_CHIP):
                sends.append(pltpu.make_async_remote_copy(
                    src_ref=ins[w].at[2 * k + 1 - c], dst_ref=recv[w].at[k],
                    send_sem=send_sems.at[w, k], recv_sem=recv_sems.at[w, k],
                    device_id=(x, y, 1 - c), device_id_type=MESH))
        for cp in sends:
            cp.start()
        for cp in sends:
            cp.wait()

    return pl.pallas_call(
        body, name=name,
        in_specs=[ANY] * n, out_specs=[ANY] * n,
        out_shape=[jax.ShapeDtypeStruct((N_CHIP, *g.shape[1:]), g.dtype) for g in grads],
        scratch_shapes=[pltpu.SemaphoreType.DMA((n, N_CHIP))] * 2,
    )(*grads)


def _pair_sum(name, grad, received, core):
    _, r, c = received.shape
    rows = min(ROWS, r)
    assert r % rows == 0

    def body(core_ref, a_ref, b_ref, o_ref):
        o_ref[...] = (a_ref[...].astype(F32) + b_ref[...].astype(F32)).astype(o_ref.dtype)

    spec = pl.BlockSpec((None, rows, c), lambda k, i, core_ref: (k, i, 0))
    return pl.pallas_call(
        body, name=name,
        grid_spec=pltpu.PrefetchScalarGridSpec(
            num_scalar_prefetch=1, grid=(N_CHIP, r // rows),
            in_specs=[pl.BlockSpec((None, None, rows, c), lambda k, i, core_ref: (k, core_ref[0], i, 0)), spec],
            out_specs=spec),
        out_shape=jax.ShapeDtypeStruct(received.shape, received.dtype),
        compiler_params=_params("parallel", "parallel"),
    )(core, grad.reshape(N_CHIP, 2, r, c), received)


def _pair_send_start(name, away):
    land = lax.empty(away.shape, away.dtype)

    def body(src, dst, send, recv, src_thru, dst_thru, token):
        x, y, c, _ = _place()
        pltpu.make_async_remote_copy(src_ref=src, dst_ref=dst, send_sem=send, recv_sem=recv,
                                     device_id=(x, y, 1 - c), device_id_type=MESH).start()
        token[...] = jnp.zeros_like(token)

    sem = pltpu.SemaphoreType.DMA(())
    out = pl.pallas_call(
        body, name=name,
        in_specs=[HBM, HBM], out_specs=[SEM, SEM, HBM, HBM, IN_VMEM],
        out_shape=[sem, sem, _hbm_like(away), _hbm_like(land), jax.ShapeDtypeStruct((SUBLANE, LANE), F32)],
        input_output_aliases={0: 2, 1: 3},
        compiler_params=SPLIT,
    )(_in_hbm(away), _in_hbm(land))
    return (out[0], out[1]), out[2], out[3], out[4]


def _pair_send_wait(name, sems, src, land, after):
    def body(src_ref, dst_ref, send, recv, *rest):
        x, y, c, _ = _place()
        pltpu.make_async_remote_copy(src_ref=src_ref, dst_ref=dst_ref, send_sem=send, recv_sem=recv,
                                     device_id=(x, y, 1 - c), device_id_type=MESH).wait()

    return pl.pallas_call(
        body, name=name,
        in_specs=[HBM, HBM, SEM, SEM] + [ANY] * len(after), out_specs=HBM, out_shape=_hbm_like(land),
        input_output_aliases={1: 0},
        compiler_params=SPLIT,
    )(src, land, *sems, *after)


def _chip_send_start(name, sums):
    n = len(sums)
    lands = [lax.empty(a.shape, a.dtype) for a in sums]

    def body(*refs):
        src, land = refs[:n], refs[n:2 * n]
        send, recv, token = refs[2 * n], refs[2 * n + 1], refs[-1]
        x, y, c, chips = _place()
        for w in range(n):
            for j, (px, py) in enumerate(chips):
                pltpu.make_async_remote_copy(
                    src_ref=src[w].at[2 * px + py], dst_ref=land[w].at[2 * x + y],
                    send_sem=send.at[3 * w + j], recv_sem=recv.at[3 * w + j],
                    device_id=(px, py, c), device_id_type=MESH).start()
        token[...] = jnp.zeros_like(token)

    sem = pltpu.SemaphoreType.DMA((3 * n,))
    out = pl.pallas_call(
        body, name=name,
        in_specs=[HBM] * (2 * n),
        out_specs=[SEM, SEM] + [HBM] * (2 * n) + [IN_VMEM],
        out_shape=[sem, sem] + [_hbm_like(a) for a in sums] + [_hbm_like(a) for a in lands]
        + [jax.ShapeDtypeStruct((SUBLANE, LANE), F32)],
        input_output_aliases={i: 2 + i for i in range(2 * n)},
        compiler_params=SPLIT,
    )(*[_in_hbm(a) for a in sums], *[_in_hbm(a) for a in lands])
    return (out[0], out[1]), out[2:2 + n], out[2 + n:2 + 2 * n], out[-1]


def _chip_send_wait(name, groups, after):
    counts = [len(g[1]) for g in groups]
    n = sum(counts)

    def body(*refs):
        src, land = refs[:n], refs[n:2 * n]
        sems = refs[2 * n:2 * n + 2 * len(groups)]
        x, y, c, chips = _place()
        w = 0
        for gi, count in enumerate(counts):
            for i in range(count):
                for j, (px, py) in enumerate(chips):
                    pltpu.make_async_remote_copy(
                        src_ref=src[w].at[2 * px + py], dst_ref=land[w].at[2 * px + py],
                        send_sem=sems[2 * gi].at[3 * i + j], recv_sem=sems[2 * gi + 1].at[3 * i + j],
                        device_id=(px, py, c), device_id_type=MESH).wait()
                w += 1

    sums = [a for g in groups for a in g[1]]
    lands = [a for g in groups for a in g[2]]
    sems = [s for g in groups for s in g[0]]
    return pl.pallas_call(
        body, name=name,
        in_specs=[HBM] * (2 * n) + [SEM] * len(sems) + [ANY] * len(after),
        out_specs=[HBM] * n, out_shape=[_hbm_like(a) for a in lands],
        input_output_aliases={n + i: i for i in range(n)},
        compiler_params=SPLIT,
    )(*sums, *lands, *sems, *after)


def _small_all_reduce(part, after=()):
    _, w = part.shape

    def body(p_ref, *rest):
        o_ref, buf, send_sems, recv_sems = rest[len(after):]
        x, y, c, _ = _place()
        me = 4 * x + 2 * y + c
        buf[me] = jnp.sum(p_ref[...], axis=0, keepdims=True)
        copies = []
        for k in range(1, N_DEV):
            dx, dy, dc = (k >> 2) & 1, (k >> 1) & 1, k & 1
            copies.append(pltpu.make_async_remote_copy(
                src_ref=buf.at[me], dst_ref=buf.at[me], send_sem=send_sems.at[k - 1], recv_sem=recv_sems.at[k - 1],
                device_id=(x ^ dx, y ^ dy, c ^ dc), device_id_type=MESH))
        for cp in copies:
            cp.start()
        for cp in copies:
            cp.wait()
        tot = buf[0]
        for d in range(1, N_DEV):
            tot = tot + buf[d]
        o_ref[...] = tot
        loss = jnp.sum(tot[:, w - LANE:], axis=1, keepdims=True)
        o_ref[:, w - LANE:] = jnp.broadcast_to(loss, (1, LANE))

    return pl.pallas_call(
        body, name="small_all_reduce",
        in_specs=[IN_VMEM] + [ANY] * len(after), out_specs=IN_VMEM,
        out_shape=jax.ShapeDtypeStruct((1, w), F32),
        scratch_shapes=[pltpu.VMEM((N_DEV, 1, w), F32), pltpu.SemaphoreType.DMA((N_DEV - 1,)), pltpu.SemaphoreType.DMA((N_DEV - 1,))],
        compiler_params=pltpu.CompilerParams(vmem_limit_bytes=VMEM_LIMIT_BYTES),
    )(part, *after)


def _adamw(w, g, m, v):
    m = ADAM_B1 * m + (1.0 - ADAM_B1) * g
    v = ADAM_B2 * v + (1.0 - ADAM_B2) * (g * g)
    m_hat = m / (1.0 - ADAM_B1 ** ADAM_STEP)
    v_hat = v / (1.0 - ADAM_B2 ** ADAM_STEP)
    delta = -ADAM_LR * (m_hat / (jnp.sqrt(v_hat) + ADAM_EPS) + ADAM_WD * w)
    return delta, m, v


def _sum_adam(name, parts, sums, chip, w, m, v, after=()):
    _, r, c = w.shape
    rows = min(ROWS, r)
    assert r % rows == 0
    n_after = len(after)

    def body(chip_ref, p_ref, own_ref, w_ref, m_ref, v_ref, *rest):
        g_ref, d_ref, mo_ref, vo_ref = rest[n_after:]
        g = None
        for k in range(N_CHIP):
            term = jnp.where(chip_ref[0] == k, own_ref[...], p_ref[k]).astype(F32)
            g = term if g is None else g + term
        g_ref[...] = g
        d_ref[...], mo_ref[...], vo_ref[...] = _adamw(w_ref[...], g, m_ref[...], v_ref[...])

    blk = pl.BlockSpec((None, rows, c), lambda i, chip_ref: (0, i, 0))
    out = jax.ShapeDtypeStruct((1, r, c), F32)
    return pl.pallas_call(
        body, name=name,
        grid_spec=pltpu.PrefetchScalarGridSpec(
            num_scalar_prefetch=1, grid=(r // rows,),
            in_specs=[pl.BlockSpec((N_CHIP, rows, c), lambda i, chip_ref: (0, i, 0)),
                      pl.BlockSpec((None, rows, c), lambda i, chip_ref: (chip_ref[0], i, 0)), blk, blk, blk]
            + [ANY] * n_after,
            out_specs=[blk] * 4),
        out_shape=[out] * 4,
        compiler_params=_params("parallel"),
    )(chip, parts, sums, w, m, v, *after)


def _adam_gains(total, ws, ms, vs):
    n = len(ws)
    widths = [w.shape[1] for w in ws]

    def body(t_ref, *refs):
        w_refs, m_refs, v_refs, outs = refs[:n], refs[n:2 * n], refs[2 * n:3 * n], refs[3 * n:]
        off = 0
        for i in range(n):
            g = t_ref[:, off:off + widths[i]]
            off += widths[i]
            g_ref, d_ref, mo_ref, vo_ref = outs[4 * i:4 * i + 4]
            g_ref[...] = g
            d_ref[...], mo_ref[...], vo_ref[...] = _adamw(w_refs[i][...], g, m_refs[i][...], v_refs[i][...])

    out = pl.pallas_call(
        body, name="adam_gains",
        out_shape=[jax.ShapeDtypeStruct(w.shape, F32) for w in ws for _ in range(4)],
    )(total, *ws, *ms, *vs)
    return [tuple(out[4 * i:4 * i + 4]) for i in range(n)]


def _adam_taps(total, first_col, device, w, m, v):
    _, n_taps, cw = w.shape
    col_block = lambda t, dev: (0, first_col // cw + t * N_DEV + dev[0])
    tap = pl.BlockSpec((None, 1, cw), lambda t, dev: (t, 0, 0))

    def body(dev_ref, t_ref, w_ref, m_ref, v_ref, g_ref, d_ref, mo_ref, vo_ref):
        g = t_ref[...]
        g_ref[...] = g
        d_ref[...], mo_ref[...], vo_ref[...] = _adamw(w_ref[...], g, m_ref[...], v_ref[...])

    shape3 = (n_taps, 1, cw)
    out = pl.pallas_call(
        body, name="adam_taps",
        grid_spec=pltpu.PrefetchScalarGridSpec(
            num_scalar_prefetch=1, grid=(n_taps,),
            in_specs=[pl.BlockSpec((1, cw), col_block), tap, tap, tap], out_specs=[tap] * 4),
        out_shape=[jax.ShapeDtypeStruct(shape3, F32)] * 4,
    )(device, total, w.reshape(shape3), m.reshape(shape3), v.reshape(shape3))
    return tuple(o.reshape(w.shape) for o in out)


def kernel(x, pre_mix_g, w_in, conv_w, q_norm_g, w_uq, kv_norm_g, w_ukv, conv_out_g, attn_out_g, w_o, post_mix_g, pre_mlp_g, w_up, w_down, post_mlp_g, loss_target, m_pre_mix_g, m_w_in, m_conv_w, m_q_norm_g, m_w_uq, m_kv_norm_g, m_w_ukv, m_conv_out_g, m_attn_out_g, m_w_o, m_post_mix_g, m_pre_mlp_g, m_w_up, m_w_down, m_post_mlp_g, v_pre_mix_g, v_w_in, v_conv_w, v_q_norm_g, v_w_uq, v_kv_norm_g, v_w_ukv, v_conv_out_g, v_attn_out_g, v_w_o, v_post_mix_g, v_pre_mlp_g, v_w_up, v_w_down, v_post_mlp_g):
    me = 4 * lax.axis_index("x") + 2 * lax.axis_index("y") + lax.axis_index("c")
    core = lax.axis_index("c").astype(jnp.int32).reshape(1)
    chip = (2 * lax.axis_index("x") + lax.axis_index("y")).astype(jnp.int32).reshape(1)
    gains = (pre_mix_g, q_norm_g, kv_norm_g, conv_out_g, attn_out_g, post_mix_g, pre_mlp_g, post_mlp_g)
    gain_m = (m_pre_mix_g, m_q_norm_g, m_kv_norm_g, m_conv_out_g, m_attn_out_g, m_post_mix_g, m_pre_mlp_g, m_post_mlp_g)
    gain_v = (v_pre_mix_g, v_q_norm_g, v_kv_norm_g, v_conv_out_g, v_attn_out_g, v_post_mix_g, v_pre_mlp_g, v_post_mlp_g)
    names = ("w_in", "w_uq", "w_ukv", "w_o", "w_up", "w_down")
    big = dict(zip(names, (w_in, w_uq, w_ukv, w_o, w_up, w_down)))
    big_m = dict(zip(names, (m_w_in, m_w_uq, m_w_ukv, m_w_o, m_w_up, m_w_down)))
    big_v = dict(zip(names, (v_w_in, v_w_uq, v_w_ukv, v_w_o, v_w_up, v_w_down)))
    n_heads = attn_out_g.shape[1] // HEAD
    n_taps = conv_w.shape[1]

    gathered = ("w_in", "conv", "w_uq", "w_ukv", "w_o", "w_up", "w_down")
    gather_groups = ((0, 1), (2, 3, 4), (5,), (6,))
    taps = jnp.pad(conv_w[0], ((0, SUBLANE - n_taps), (0, 0)))
    sems_a, shards_a, lands_a, token_a = _gather_start("gather_start_first", [w_in[0].astype(BF16), taps], ((0, 1),))
    behind = token_a[0, 0]
    sems_b, shards_b, lands_b, token = _gather_start(
        "gather_start_rest", [(big[nm][0] + behind).astype(BF16) for nm in gathered[2:]], ((0, 1, 2), (3,), (4,)))
    sems1, shards, lands = sems_a + sems_b, [*shards_a, *shards_b], [*lands_a, *lands_b]

    cols = lambda a: jnp.concatenate([a[j] for j in range(N_DEV)], axis=1)
    rows = lambda a: a.reshape(N_DEV * a.shape[1], a.shape[2])
    ready = {
        "w_in": lambda a: jnp.concatenate([a[j] for j in range(N_DEV)]
                                          + [jnp.zeros((a.shape[1], _in_pad(N_DEV * a.shape[2]) - N_DEV * a.shape[2]), a.dtype)], axis=1),
        "conv": lambda a: cols(a)[:n_taps],
        "w_uq": lambda a: _permute_q_cols(cols(a), n_heads),
        "w_ukv": cols, "w_o": rows, "w_up": lambda a: a, "w_down": rows,
    }

    class Weights:
        def __init__(self):
            self.passed = {}

        def forward(self, group, after):
            idx = gather_groups[group]
            self.passed[group] = _gather_forward(f"gather_forward_{group}", [shards[i] for i in idx], [lands[i] for i in idx],
                                                 *sems1[group], after)

        def ready(self, group, after):
            sems2, mid = self.passed[group]
            full = _gather_wait(f"gather_wait_{group}", mid, *sems2, after)
            out = []
            for i, a in zip(gather_groups[group], full):
                a = lax.dynamic_update_index_in_dim(a, shards[i], me, 0)
                out.append(ready[gathered[i]](a))
            return out

    weights = Weights()

    col_blocks = lambda g: g.reshape(g.shape[0], N_DEV, g.shape[1] // N_DEV).transpose(1, 0, 2)
    row_blocks = lambda g: g.reshape(N_DEV, g.shape[0] // N_DEV, g.shape[1])
    in_shard = w_in.shape[2]
    grad_groups = (("w_down",), ("w_up",), ("w_o",), ("w_uq", "w_ukv"), ("w_in",))
    to_blocks = {
        "w_in": lambda g: jnp.stack([g[:, j * in_shard:(j + 1) * in_shard] for j in range(N_DEV)]),
        "w_uq": lambda g: col_blocks(_unpermute_q_cols(g, n_heads)),
        "w_ukv": col_blocks, "w_o": row_blocks, "w_up": lambda g: g, "w_down": row_blocks,
    }
    in_flight = []

    class Grads:
        def __init__(self):
            self.core = core
            self.away = {}

        def send_sums(self, group, sums):
            sems, sums, parts, tok = _chip_send_start(f"chip_send_start_{group}", list(sums))
            in_flight.append((sems, sums, parts))
            return (tok,)

        def full(self, group, arrays):
            nms = grad_groups[group]
            blocks = [to_blocks[nm](g) for nm, g in zip(nms, arrays)]
            received = _pair_exchange(f"pair_exchange_{group}", blocks)
            return self.send_sums(group, [_pair_sum(f"pair_sum_{nm}", g, r, core) for nm, g, r in zip(nms, blocks, received)])

        def send_away(self, group, half):
            sems, src, land, tok = _pair_send_start(f"pair_send_start_{group}", half)
            self.away[group] = (sems, src, land)
            return (tok,)

        def received(self, group, after):
            sems, src, land = self.away[group]
            return _pair_send_wait(f"pair_send_wait_{group}", sems, src, land, after)

    grad_x, small = _local_step(x[0], loss_target[0], gains, weights, Grads(), first_after=(token,))

    big_out = {}

    def update(tag, first, last, after):
        groups = in_flight[first:last]
        parts = _chip_send_wait("chip_send_wait_" + tag, groups, after)
        nms = [nm for grp in grad_groups[first:last] for nm in grp]
        sums = [a for _, s, _ in groups for a in s]
        for nm, p, s in zip(nms, parts, sums):
            big_out[nm] = _sum_adam("adam_" + nm, p, s, chip, big[nm], big_m[nm], big_v[nm], after=after)
            after = (big_out[nm][0],)
        return after

    after = update("early", 0, len(in_flight) - 1, (grad_x,))
    total = _small_all_reduce(small, after=after)
    update("late", len(in_flight) - 1, len(in_flight), (total,))
    big_out = [big_out[nm] for nm in names]

    gain_out = _adam_gains(total, gains, gain_m, gain_v)
    taps_out = _adam_taps(total, sum(g.shape[1] for g in gains), me.astype(jnp.int32).reshape(1), conv_w, m_conv_w, v_conv_w)
    loss = total[0, total.shape[1] - 1]

    order = (0, "w_in", "conv", 1, "w_uq", 2, "w_ukv", 3, 4, "w_o", 5, 6, "w_up", "w_down", 7)
    by_name = dict(zip(names, big_out))
    outs = [loss, grad_x[None]]
    for kind in range(4):
        for item in order:
            if item == "conv":
                outs.append(taps_out[kind])
            elif isinstance(item, int):
                outs.append(gain_out[item][kind])
            else:
                outs.append(by_name[item][kind])
    return tuple(outs)
```

```python
import jax
import jax.numpy as jnp
from jax import lax
from jax.experimental import pallas as pl
from jax.experimental.pallas import tpu as pltpu

F32 = jnp.float32
BF16 = jnp.bfloat16

EPS = 1e-6
NEG_INF = -1e30
HEAD = 128
ROPE = 64
QK = HEAD + ROPE
CHUNK = 64
ROPE_THETA = 10000.0
ADAM_LR, ADAM_B1, ADAM_B2, ADAM_EPS, ADAM_WD, ADAM_STEP = 0.001, 0.9, 0.999, 1e-08, 0.01, 10

LANE = 128
SUBLANE = 8
VMEM_LIMIT_BYTES = 56 * 1024 * 1024

N_DEV = 8
N_CHIP = 4
MESH = pl.DeviceIdType.MESH


def _params(*sem):
    return pltpu.CompilerParams(dimension_semantics=sem, vmem_limit_bytes=VMEM_LIMIT_BYTES)


ANY = pl.BlockSpec(memory_space=pl.ANY)


def _call(body, *, in_specs, after=(), **kw):
    n_in, n_after = len(in_specs), len(after)

    def ordered(*refs):
        body(*refs[:n_in], *refs[n_in + n_after:])

    call = pl.pallas_call(ordered, in_specs=[*in_specs, *[ANY] * n_after], **kw)
    return lambda *operands: call(*operands, *after)


def _sublane_sum(v):
    r, w = v.shape
    return jnp.sum(v.reshape(r // SUBLANE, SUBLANE, w), axis=0)


def _rstd(x):
    return lax.rsqrt(jnp.mean(x * x, axis=-1, keepdims=True) + EPS)


def _rms_bwd(x, g, dy):
    r = _rstd(x)
    xh = x * r
    dxh = dy * g
    dx = r * (dxh - xh * jnp.mean(dxh * xh, axis=-1, keepdims=True))
    return dx, dy * xh


def _accumulate(ref, val, step):
    @pl.when(step == 0)
    def _():
        ref[...] = val

    @pl.when(step > 0)
    def _():
        ref[...] += val


NN = ((1,), (0,))
NT = ((1,), (1,))
TN = ((0,), (0,))


def _matmul(name, a, b, *, grid, a_spec, b_spec, out_shape, out_specs, contract, nk=1, acc_shape=None,
            extras=(), extra_specs=(), epilogue=None, after=()):
    multi = isinstance(out_shape, (tuple, list))
    out_shapes = tuple(out_shape) if multi else (out_shape,)
    n_out = len(out_shapes)
    n_extra = len(extras)

    def body(a_ref, b_ref, *rest):
        x_refs = rest[:n_extra]
        o_refs = rest[n_extra:n_extra + n_out]

        def emit(acc):
            vals = epilogue(acc, *[r[...] for r in x_refs]) if epilogue else (acc,)
            for r, v in zip(o_refs, vals):
                r[...] = v.astype(r.dtype)

        p = lax.dot_general(a_ref[...], b_ref[...], (contract, ((), ())), preferred_element_type=F32)
        if nk == 1:
            emit(p)
        else:
            acc_ref = rest[n_extra + n_out]
            k = pl.program_id(2)
            _accumulate(acc_ref, p, k)

            @pl.when(k == nk - 1)
            def _():
                emit(acc_ref[...])

    sem = ("parallel", "parallel") + (("arbitrary",) if nk > 1 else ())
    return _call(
        body, name=name, grid=grid, after=after,
        in_specs=[a_spec, b_spec, *extra_specs],
        out_specs=out_specs,
        out_shape=out_shape,
        scratch_shapes=[pltpu.VMEM(acc_shape, F32)] if nk > 1 else [],
        compiler_params=_params(*sem),
    )(a, b, *extras)


def _fit(n, tile):
    if n <= tile:
        return n
    t = tile - tile % LANE
    while n % t:
        t -= LANE
    return t


def _mm_nn(name, a, b, out_dtype, tm, tn):
    m, k = a.shape
    n = b.shape[1]
    tm, tn = _fit(m, tm), _fit(n, tn)
    return _matmul(name, a, b, grid=(m // tm, n // tn),
                   a_spec=pl.BlockSpec((tm, k), lambda i, j: (i, 0)),
                   b_spec=pl.BlockSpec((k, tn), lambda i, j: (0, j)),
                   out_shape=jax.ShapeDtypeStruct((m, n), out_dtype),
                   out_specs=pl.BlockSpec((tm, tn), lambda i, j: (i, j)), contract=NN)


def _mm_nt(name, a, b, out_dtype, tm, tn, after=()):
    m, k = a.shape
    n = b.shape[0]
    tm, tn = _fit(m, tm), _fit(n, tn)
    return _matmul(name, a, b, grid=(m // tm, n // tn), after=after,
                   a_spec=pl.BlockSpec((tm, k), lambda i, j: (i, 0)),
                   b_spec=pl.BlockSpec((tn, k), lambda i, j: (j, 0)),
                   out_shape=jax.ShapeDtypeStruct((m, n), out_dtype),
                   out_specs=pl.BlockSpec((tm, tn), lambda i, j: (i, j)), contract=NT)


def _mm_tn(name, a, b, out_dtype, tm, tn):
    s, m = a.shape
    n = b.shape[1]
    tm, tn = _fit(m, tm), _fit(n, tn)
    return _matmul(name, a, b, grid=(m // tm, n // tn),
                   a_spec=pl.BlockSpec((s, tm), lambda i, j: (0, i)),
                   b_spec=pl.BlockSpec((s, tn), lambda i, j: (0, j)),
                   out_shape=jax.ShapeDtypeStruct((m, n), out_dtype),
                   out_specs=pl.BlockSpec((tm, tn), lambda i, j: (i, j)), contract=TN)


ROWS = 256


def _row_spec(rows, width):
    return pl.BlockSpec((rows, width), lambda i: (i, 0))


def _fixed_spec(rows, width):
    return pl.BlockSpec((rows, width), lambda i: (0, 0))


def _rms_fwd(name, x, g, after=()):
    s, w = x.shape
    rows = min(ROWS, s)

    def body(x_ref, g_ref, o_ref):
        xv = x_ref[...]
        o_ref[...] = (xv * _rstd(xv) * g_ref[...]).astype(o_ref.dtype)

    return _call(
        body, name=name, grid=(s // rows,), after=after,
        in_specs=[_row_spec(rows, w), _fixed_spec(1, w)],
        out_specs=_row_spec(rows, w),
        out_shape=jax.ShapeDtypeStruct((s, w), BF16),
        compiler_params=_params("parallel"),
    )(x, g)


def _rms_bwd_call(name, x, g, dy, out_dtype, after=()):
    s, w = x.shape
    rows = min(ROWS, s)

    def body(x_ref, g_ref, dy_ref, dx_ref, dg_ref):
        dx, dgc = _rms_bwd(x_ref[...], g_ref[...], dy_ref[...].astype(F32))
        dx_ref[...] = dx.astype(dx_ref.dtype)
        _accumulate(dg_ref, _sublane_sum(dgc), pl.program_id(0))

    return _call(
        body, name=name, grid=(s // rows,), after=after,
        in_specs=[_row_spec(rows, w), _fixed_spec(1, w), _row_spec(rows, w)],
        out_specs=[_row_spec(rows, w), _fixed_spec(SUBLANE, w)],
        out_shape=[jax.ShapeDtypeStruct((s, w), out_dtype), jax.ShapeDtypeStruct((SUBLANE, w), F32)],
        compiler_params=_params("arbitrary"),
    )(x, g, dy)


def _mid_fwd(x, y, g_post, g_pre, after=()):
    s, w = x.shape
    rows = min(ROWS, s)

    def body(x_ref, y_ref, gp_ref, gq_ref, x2_ref, h2_ref):
        yv = y_ref[...]
        x2 = x_ref[...] + yv * _rstd(yv) * gp_ref[...]
        x2_ref[...] = x2
        h2_ref[...] = (x2 * _rstd(x2) * gq_ref[...]).astype(h2_ref.dtype)

    return _call(
        body, name="mid_fwd", grid=(s // rows,), after=after,
        in_specs=[_row_spec(rows, w), _row_spec(rows, w), _fixed_spec(1, w), _fixed_spec(1, w)],
        out_specs=[_row_spec(rows, w), _row_spec(rows, w)],
        out_shape=[jax.ShapeDtypeStruct((s, w), F32), jax.ShapeDtypeStruct((s, w), BF16)],
        compiler_params=_params("parallel"),
    )(x, y, g_post, g_pre)


def _head(m, x2, tgt, g):
    s, w = m.shape
    rows = min(ROWS, s)

    def body(m_ref, x2_ref, t_ref, g_ref, dout_ref, dm_ref, dg_ref, loss_ref):
        mv = m_ref[...]
        gv = g_ref[...]
        out = x2_ref[...] + mv * _rstd(mv) * gv
        err = out - t_ref[...]
        dout = err * (1.0 / w)
        dout_ref[...] = dout
        dm, dgc = _rms_bwd(mv, gv, dout)
        dm_ref[...] = dm.astype(dm_ref.dtype)
        sq = err * err
        lanes = sq[:, 0:LANE]
        for j in range(1, w // LANE):
            lanes = lanes + sq[:, j * LANE:(j + 1) * LANE]
        step = pl.program_id(0)
        _accumulate(dg_ref, _sublane_sum(dgc), step)
        _accumulate(loss_ref, _sublane_sum(lanes) * (0.5 / w), step)

    return pl.pallas_call(
        body, name="head", grid=(s // rows,),
        in_specs=[_row_spec(rows, w), _row_spec(rows, w), _row_spec(rows, w), _fixed_spec(1, w)],
        out_specs=[_row_spec(rows, w), _row_spec(rows, w), _fixed_spec(SUBLANE, w), _fixed_spec(SUBLANE, LANE)],
        out_shape=[jax.ShapeDtypeStruct((s, w), F32), jax.ShapeDtypeStruct((s, w), BF16),
                   jax.ShapeDtypeStruct((SUBLANE, w), F32), jax.ShapeDtypeStruct((SUBLANE, LANE), F32)],
        compiler_params=_params("arbitrary"),
    )(m, x2, tgt, g)


def _mid_bwd(x2, y, d_out, d_h2, g_pre, g_post, after=()):
    s, w = x2.shape
    rows = min(ROWS, s)

    def body(x2_ref, y_ref, dout_ref, dh2_ref, gq_ref, gp_ref, dx2_ref, dy_ref, dgq_ref, dgp_ref):
        dx, dgq = _rms_bwd(x2_ref[...], gq_ref[...], dh2_ref[...])
        dx2 = dout_ref[...] + dx
        dx2_ref[...] = dx2
        dy, dgp = _rms_bwd(y_ref[...], gp_ref[...], dx2)
        dy_ref[...] = dy.astype(dy_ref.dtype)
        step = pl.program_id(0)
        _accumulate(dgq_ref, _sublane_sum(dgq), step)
        _accumulate(dgp_ref, _sublane_sum(dgp), step)

    return _call(
        body, name="mid_bwd", grid=(s // rows,), after=after,
        in_specs=[_row_spec(rows, w)] * 4 + [_fixed_spec(1, w)] * 2,
        out_specs=[_row_spec(rows, w), _row_spec(rows, w), _fixed_spec(SUBLANE, w), _fixed_spec(SUBLANE, w)],
        out_shape=[jax.ShapeDtypeStruct((s, w), F32), jax.ShapeDtypeStruct((s, w), BF16),
                   jax.ShapeDtypeStruct((SUBLANE, w), F32), jax.ShapeDtypeStruct((SUBLANE, w), F32)],
        compiler_params=_params("arbitrary"),
    )(x2, y, d_out, d_h2, g_pre, g_post)


def _first_bwd(x, g, d_h1, d_x2, after=()):
    s, w = x.shape
    rows = min(ROWS, s)

    def body(x_ref, g_ref, dh_ref, dx2_ref, dx_ref, dg_ref):
        dx, dgc = _rms_bwd(x_ref[...], g_ref[...], dh_ref[...])
        dx_ref[...] = dx2_ref[...] + dx
        _accumulate(dg_ref, _sublane_sum(dgc), pl.program_id(0))

    return _call(
        body, name="first_bwd", grid=(s // rows,), after=after,
        in_specs=[_row_spec(rows, w), _fixed_spec(1, w), _row_spec(rows, w), _row_spec(rows, w)],
        out_specs=[_row_spec(rows, w), _fixed_spec(SUBLANE, w)],
        out_shape=[jax.ShapeDtypeStruct((s, w), F32), jax.ShapeDtypeStruct((SUBLANE, w), F32)],
        compiler_params=_params("arbitrary"),
    )(x, g, d_h1, d_x2)


def _shift_down(v, k):
    t = lax.broadcasted_iota(jnp.int32, v.shape, 0)
    return jnp.where(t >= k, pltpu.roll(v, k, 0), 0.0)


def _shift_up(v, k):
    n = v.shape[0]
    t = lax.broadcasted_iota(jnp.int32, v.shape, 0)
    return jnp.where(t < n - k, pltpu.roll(v, n - k, 0), 0.0)


def _conv_core(u, b, c, w):
    z = c * u
    conv = w[0:1, :] * _shift_down(z, 2) + w[1:2, :] * _shift_down(z, 1) + w[2:3, :] * z
    return z, conv, b * conv


def _conv_fwd(proj, conv_w, g, n_groups):
    s = proj.shape[0]

    def body(u_ref, b_ref, c_ref, w_ref, g_ref, o_ref):
        _, _, yr = _conv_core(u_ref[...], b_ref[...], c_ref[...], w_ref[...])
        o_ref[...] = (yr * _rstd(yr) * g_ref[...]).astype(o_ref.dtype)

    col = lambda k: pl.BlockSpec((s, HEAD), lambda i: (0, k * n_groups + i))
    return pl.pallas_call(
        body, name="conv_fwd", grid=(n_groups,),
        in_specs=[col(0), col(1), col(2), pl.BlockSpec((3, HEAD), lambda i: (0, i)), pl.BlockSpec((1, HEAD), lambda i: (0, i))],
        out_specs=pl.BlockSpec((s, HEAD), lambda i: (0, i)),
        out_shape=jax.ShapeDtypeStruct((s, n_groups * HEAD), BF16),
        compiler_params=_params("parallel"),
    )(proj, proj, proj, conv_w, g)


def _conv_bwd(proj, d_mix, conv_w, g, n_groups):
    s = proj.shape[0]
    width = n_groups * HEAD

    def body(u_ref, b_ref, c_ref, dy_ref, w_ref, g_ref, du_ref, db_ref, dc_ref, dg_ref, dw_ref):
        u, b, c, w = u_ref[...], b_ref[...], c_ref[...], w_ref[...]
        z, conv, yr = _conv_core(u, b, c, w)
        dyr, dgc = _rms_bwd(yr, g_ref[...], dy_ref[...])
        dconv = dyr * b
        db_ref[...] = (dyr * conv).astype(db_ref.dtype)
        dz = w[2:3, :] * dconv + w[1:2, :] * _shift_up(dconv, 1) + w[0:1, :] * _shift_up(dconv, 2)
        dc_ref[...] = (dz * u).astype(dc_ref.dtype)
        du_ref[...] = (dz * c).astype(du_ref.dtype)
        dg_ref[...] = _sublane_sum(dgc)
        dw_ref[0] = _sublane_sum(dconv * _shift_down(z, 2))
        dw_ref[1] = _sublane_sum(dconv * _shift_down(z, 1))
        dw_ref[2] = _sublane_sum(dconv * z)

    col = lambda k: pl.BlockSpec((s, HEAD), lambda i: (0, k * n_groups + i))
    grp = pl.BlockSpec((s, HEAD), lambda i: (0, i))
    return pl.pallas_call(
        body, name="conv_bwd", grid=(n_groups,),
        in_specs=[col(0), col(1), col(2), grp, pl.BlockSpec((3, HEAD), lambda i: (0, i)), pl.BlockSpec((1, HEAD), lambda i: (0, i))],
        out_specs=[grp, grp, grp, pl.BlockSpec((SUBLANE, HEAD), lambda i: (0, i)),
                   pl.BlockSpec((3, SUBLANE, HEAD), lambda i: (0, 0, i))],
        out_shape=[jax.ShapeDtypeStruct((s, width), BF16)] * 3
        + [jax.ShapeDtypeStruct((SUBLANE, width), F32), jax.ShapeDtypeStruct((3, SUBLANE, width), F32)],
        compiler_params=_params("parallel"),
    )(proj, proj, proj, d_mix, conv_w, g)


def _rope_tables(s, n_heads):
    pos = jnp.arange(s, dtype=F32)
    inv_freq = jnp.power(ROPE_THETA, -jnp.arange(0, ROPE, 2, dtype=F32) / ROPE)
    ang = pos[:, None] * inv_freq[None, :]
    cos, sin = jnp.cos(ang), jnp.sin(ang)
    cs = jnp.concatenate([cos, cos], axis=1)
    sn = jnp.concatenate([-sin, sin], axis=1)
    pad = jnp.zeros((s, LANE - ROPE), F32)
    return (jnp.tile(cs, (1, n_heads)), jnp.tile(sn, (1, n_heads)),
            jnp.concatenate([cs, pad], axis=1), jnp.concatenate([sn, pad], axis=1))


def _swap_halves(v):
    w = v.shape[1]
    lane = lax.broadcasted_iota(jnp.int32, v.shape, 1)
    first = (lane % ROPE) < (ROPE // 2)
    return jnp.where(first, pltpu.roll(v, w - ROPE // 2, 1), pltpu.roll(v, ROPE // 2, 1))


def _pack_heads(q, kv, kr, tables, n_heads, after=()):
    s = q.shape[0]
    rows = min(ROWS, s)
    cq, sq, ck, sk = tables
    wq = n_heads * ROPE

    def body(q_ref, kv_ref, kr_ref, cq_ref, sq_ref, ck_ref, sk_ref, qo_ref, ko_ref, vo_ref):
        qr = q_ref[:, n_heads * HEAD:]
        qr = qr * cq_ref[...] + _swap_halves(qr) * sq_ref[...]
        krv = kr_ref[...]
        krv = krv * ck_ref[...] + _swap_halves(krv) * sk_ref[...]
        for h in range(n_heads):
            qo_ref[h] = jnp.concatenate([q_ref[:, h * HEAD:(h + 1) * HEAD], qr[:, h * ROPE:(h + 1) * ROPE]], axis=1).astype(BF16)
            ko_ref[h] = jnp.concatenate([kv_ref[:, 2 * h * HEAD:(2 * h + 1) * HEAD], krv[:, :ROPE]], axis=1).astype(BF16)
            vo_ref[h] = kv_ref[:, (2 * h + 1) * HEAD:(2 * h + 2) * HEAD].astype(BF16)

    hs = lambda w: pl.BlockSpec((n_heads, rows, w), lambda i: (0, i, 0))
    return _call(
        body, name="pack_heads", grid=(s // rows,), after=after,
        in_specs=[_row_spec(rows, q.shape[1]), _row_spec(rows, kv.shape[1]), _row_spec(rows, LANE),
                  _row_spec(rows, wq), _row_spec(rows, wq), _row_spec(rows, LANE), _row_spec(rows, LANE)],
        out_specs=[hs(QK), hs(QK), hs(HEAD)],
        out_shape=[jax.ShapeDtypeStruct((n_heads, s, QK), BF16), jax.ShapeDtypeStruct((n_heads, s, QK), BF16),
                   jax.ShapeDtypeStruct((n_heads, s, HEAD), BF16)],
        compiler_params=_params("parallel"),
    )(q, kv, kr, cq, sq, ck, sk)


def _unpack_heads(dq, dk, dv, tables, n_heads):
    s = dq.shape[1]
    rows = min(ROWS, s)
    cq, sq, ck, sk = tables
    wq = n_heads * ROPE

    def body(dq_ref, dk_ref, dv_ref, cq_ref, sq_ref, ck_ref, sk_ref, qo_ref, kvo_ref, kro_ref):
        dqr = jnp.concatenate([dq_ref[h][:, HEAD:] for h in range(n_heads)], axis=1)
        dqr = dqr * cq_ref[...] - _swap_halves(dqr) * sq_ref[...]
        dkr = dk_ref[0][:, HEAD:]
        for h in range(1, n_heads):
            dkr = dkr + dk_ref[h][:, HEAD:]
        dkr = jnp.concatenate([dkr, jnp.zeros((rows, LANE - ROPE), F32)], axis=1)
        dkr = dkr * ck_ref[...] - _swap_halves(dkr) * sk_ref[...]
        kro_ref[...] = dkr.astype(kro_ref.dtype)
        qo_ref[:, n_heads * HEAD:] = dqr.astype(qo_ref.dtype)
        for h in range(n_heads):
            qo_ref[:, h * HEAD:(h + 1) * HEAD] = dq_ref[h][:, :HEAD].astype(qo_ref.dtype)
            kvo_ref[:, 2 * h * HEAD:(2 * h + 1) * HEAD] = dk_ref[h][:, :HEAD].astype(kvo_ref.dtype)
            kvo_ref[:, (2 * h + 1) * HEAD:(2 * h + 2) * HEAD] = dv_ref[h].astype(kvo_ref.dtype)

    hs = lambda w: pl.BlockSpec((n_heads, rows, w), lambda i: (0, i, 0))
    return pl.pallas_call(
        body, name="unpack_heads", grid=(s // rows,),
        in_specs=[hs(QK), hs(QK), hs(HEAD), _row_spec(rows, wq), _row_spec(rows, wq), _row_spec(rows, LANE), _row_spec(rows, LANE)],
        out_specs=[_row_spec(rows, n_heads * QK), _row_spec(rows, 2 * n_heads * HEAD), _row_spec(rows, LANE)],
        out_shape=[jax.ShapeDtypeStruct((s, n_heads * QK), BF16), jax.ShapeDtypeStruct((s, 2 * n_heads * HEAD), BF16),
                   jax.ShapeDtypeStruct((s, LANE), BF16)],
        compiler_params=_params("parallel"),
    )(dq, dk, dv, cq, sq, ck, sk)


TQ = 256


def _probs(q, k):
    tq, n_keys = q.shape[0], k.shape[0]
    sc = lax.dot_general(q, k, (NT, ((), ())), preferred_element_type=F32) * (QK ** -0.5)
    row = lax.broadcasted_iota(jnp.int32, (tq, tq), 0)
    col = lax.broadcasted_iota(jnp.int32, (tq, tq), 1)
    own = jnp.where(col // CHUNK <= row // CHUNK, sc[:, n_keys - tq:], NEG_INF)
    sc = own if n_keys == tq else jnp.concatenate([sc[:, :n_keys - tq], own], axis=1)
    e = jnp.exp(sc - jnp.max(sc, axis=-1, keepdims=True))
    return e / jnp.sum(e, axis=-1, keepdims=True)


def _per_query_block(n_blocks, branch):
    i = pl.program_id(1)
    for c in range(n_blocks):
        pl.when(i == c)(lambda c=c: branch(c))


def _attn_fwd(q, k, v, g):
    n_heads, s, _ = q.shape
    tq = min(TQ, s)
    assert tq % CHUNK == 0 and s % tq == 0

    def body(q_ref, k_ref, v_ref, g_ref, o_ref, y_ref):
        def branch(c):
            n_keys = (c + 1) * tq
            p = _probs(q_ref[...], k_ref[0:n_keys, :])
            o = jnp.dot(p.astype(BF16), v_ref[0:n_keys, :], preferred_element_type=F32)
            o_ref[...] = o
            y_ref[...] = (o * _rstd(o) * g_ref[...]).astype(y_ref.dtype)

        _per_query_block(s // tq, branch)

    return pl.pallas_call(
        body, name="attn_fwd", grid=(n_heads, s // tq),
        in_specs=[pl.BlockSpec((None, tq, QK), lambda h, i: (h, i, 0)),
                  pl.BlockSpec((None, s, QK), lambda h, i: (h, 0, 0)),
                  pl.BlockSpec((None, s, HEAD), lambda h, i: (h, 0, 0)),
                  pl.BlockSpec((1, HEAD), lambda h, i: (0, h))],
        out_specs=[pl.BlockSpec((None, tq, HEAD), lambda h, i: (h, i, 0)),
                   pl.BlockSpec((tq, HEAD), lambda h, i: (i, h))],
        out_shape=[jax.ShapeDtypeStruct((n_heads, s, HEAD), F32), jax.ShapeDtypeStruct((s, n_heads * HEAD), BF16)],
        compiler_params=_params("parallel", "parallel"),
    )(q, k, v, g)


def _attn_bwd(q, k, v, o, d_mix, g, col0, after=()):
    n_heads, s, _ = q.shape
    tq = min(TQ, s)

    def body(q_ref, k_ref, v_ref, o_ref, dy_ref, g_ref, dq_ref, dk_ref, dv_ref, dg_ref):
        def branch(c):
            n_keys = (c + 1) * tq
            qv, kv_, vv = q_ref[...], k_ref[0:n_keys, :], v_ref[0:n_keys, :]
            do, dgc = _rms_bwd(o_ref[...], g_ref[...], dy_ref[...])
            do = do.astype(BF16)
            p = _probs(qv, kv_)
            dp = lax.dot_general(do, vv, (NT, ((), ())), preferred_element_type=F32)
            ds = (p * (dp - jnp.sum(p * dp, axis=-1, keepdims=True)) * (QK ** -0.5)).astype(BF16)
            dq_ref[...] = jnp.dot(ds, kv_, preferred_element_type=F32)
            dk = lax.dot_general(ds, qv, (TN, ((), ())), preferred_element_type=F32)
            dv = lax.dot_general(p.astype(BF16), do, (TN, ((), ())), preferred_element_type=F32)
            if c == 0:
                dk_ref[...] = jnp.zeros_like(dk_ref)
                dv_ref[...] = jnp.zeros_like(dv_ref)
                dk_ref[0:n_keys, :] = dk
                dv_ref[0:n_keys, :] = dv
                dg_ref[...] = _sublane_sum(dgc)
            else:
                dk_ref[0:n_keys, :] += dk
                dv_ref[0:n_keys, :] += dv
                dg_ref[...] += _sublane_sum(dgc)

        _per_query_block(s // tq, branch)

    c0 = col0 // HEAD
    return _call(
        body, name="attn_bwd", grid=(n_heads, s // tq), after=after,
        in_specs=[pl.BlockSpec((None, tq, QK), lambda h, i: (h, i, 0)),
                  pl.BlockSpec((None, s, QK), lambda h, i: (h, 0, 0)),
                  pl.BlockSpec((None, s, HEAD), lambda h, i: (h, 0, 0)),
                  pl.BlockSpec((None, tq, HEAD), lambda h, i: (h, i, 0)),
                  pl.BlockSpec((tq, HEAD), lambda h, i: (i, c0 + h)),
                  pl.BlockSpec((1, HEAD), lambda h, i: (0, h))],
        out_specs=[pl.BlockSpec((None, tq, QK), lambda h, i: (h, i, 0)),
                   pl.BlockSpec((None, s, QK), lambda h, i: (h, 0, 0)),
                   pl.BlockSpec((None, s, HEAD), lambda h, i: (h, 0, 0)),
                   pl.BlockSpec((SUBLANE, HEAD), lambda h, i: (0, h))],
        out_shape=[jax.ShapeDtypeStruct((n_heads, s, QK), F32), jax.ShapeDtypeStruct((n_heads, s, QK), F32),
                   jax.ShapeDtypeStruct((n_heads, s, HEAD), F32), jax.ShapeDtypeStruct((SUBLANE, n_heads * HEAD), F32)],
        compiler_params=_params("parallel", "arbitrary"),
    )(q, k, v, o, d_mix, g)


TILE_M = 1024
TILE_N = 1024


def _up_fwd(h2, w_up):
    s, d = h2.shape
    nb, _, fb = w_up.shape
    tm = min(TILE_M,s)

    def epilogue(acc):
        r = jnp.maximum(acc, 0.0)
        return r * r, r

    blk = pl.BlockSpec((tm, fb), lambda i, j: (i, j))
    return _matmul("up_fwd", h2, w_up, grid=(s // tm, nb),
                   a_spec=pl.BlockSpec((tm, d), lambda i, j: (i, 0)),
                   b_spec=pl.BlockSpec((None, d, fb), lambda i, j: (j, 0, 0)),
                   out_shape=[jax.ShapeDtypeStruct((s, nb * fb), BF16)] * 2, out_specs=[blk, blk],
                   contract=NN, epilogue=epilogue)


def _down_fwd(a, w_down):
    s, f = a.shape
    d = w_down.shape[1]
    tm, tn, tk = min(TILE_M,s), min(TILE_N,d), 2048
    nk = f // tk
    return _matmul("down_fwd", a, w_down, grid=(s // tm, d // tn, nk),
                   a_spec=pl.BlockSpec((tm, tk), lambda i, j, k: (i, k)),
                   b_spec=pl.BlockSpec((tk, tn), lambda i, j, k: (k, j)),
                   out_shape=jax.ShapeDtypeStruct((s, d), F32),
                   out_specs=pl.BlockSpec((tm, tn), lambda i, j, k: (i, j)),
                   contract=NN, nk=nk, acc_shape=(tm, tn))


def _down_bwd_act(d_m, w_down, r, after=()):
    s, d = d_m.shape
    f = w_down.shape[0]
    tm, tn = min(TILE_M,s), min(TILE_N,f)
    blk = pl.BlockSpec((tm, tn), lambda i, j: (i, j))
    return _matmul("down_bwd_act", d_m, w_down, grid=(s // tm, f // tn), after=after,
                   a_spec=pl.BlockSpec((tm, d), lambda i, j: (i, 0)),
                   b_spec=pl.BlockSpec((tn, d), lambda i, j: (j, 0)),
                   out_shape=jax.ShapeDtypeStruct((s, f), BF16), out_specs=blk, contract=NT,
                   extras=(r,), extra_specs=(blk,),
                   epilogue=lambda acc, rv: (acc * (2.0 * rv.astype(F32)),))


def _up_bwd_act(d_up, w_up, after=()):
    s, _ = d_up.shape
    nb, d, fb = w_up.shape
    tm, tn = min(TILE_M,s), min(TILE_N,d)
    return _matmul("up_bwd_act", d_up, w_up, grid=(s // tm, d // tn, nb), after=after,
                   a_spec=pl.BlockSpec((tm, fb), lambda i, j, k: (i, k)),
                   b_spec=pl.BlockSpec((None, tn, fb), lambda i, j, k: (k, j, 0)),
                   out_shape=jax.ShapeDtypeStruct((s, d), F32),
                   out_specs=pl.BlockSpec((tm, tn), lambda i, j, k: (i, j)),
                   contract=NT, nk=nb, acc_shape=(tm, tn))


def _half_grad(name, a, b, core, home, received, after, *, grid, a_block, a_map, b_block, b_map, o_block, o_map, out_shape):
    n_after = len(after)
    pick = (lambda ref: ref[0]) if home else (lambda ref: 1 - ref[0])

    def body(core_ref, a_ref, b_ref, *rest):
        acc = lax.dot_general(a_ref[...], b_ref[...], (TN, ((), ())), preferred_element_type=F32)
        if received is not None:
            acc = acc + rest[0][...].astype(F32)
        rest[-1][...] = acc.astype(rest[-1].dtype)

    wrap = lambda fn: (lambda i, j, core_ref: fn(i, j, pick(core_ref)))
    o_spec = pl.BlockSpec(o_block, wrap(o_map))
    extra = [] if received is None else [o_spec]
    operands = [] if received is None else [received]
    return pl.pallas_call(
        body, name=name,
        grid_spec=pltpu.PrefetchScalarGridSpec(
            num_scalar_prefetch=1, grid=grid,
            in_specs=[pl.BlockSpec(a_block, wrap(a_map)), pl.BlockSpec(b_block, wrap(b_map))] + extra + [ANY] * n_after,
            out_specs=o_spec),
        out_shape=out_shape,
        compiler_params=_params("parallel", "parallel"),
    )(core, a, b, *operands, *after)


def _down_half_grad(name, a, d_m, core, home, received=None, after=()):
    s, f = a.shape
    d = d_m.shape[1]
    r = f // N_DEV
    tn = min(TILE_N, d)
    return _half_grad(name, a, d_m, core, home, received, after, grid=(N_CHIP, d // tn),
                      a_block=(s, r), a_map=lambda k, j, p: (0, 2 * k + p),
                      b_block=(s, tn), b_map=lambda k, j, p: (0, j),
                      o_block=(None, r, tn), o_map=lambda k, j, p: (k, 0, j),
                      out_shape=jax.ShapeDtypeStruct((N_CHIP, r, d), BF16))


def _up_half_grad(name, h2, d_up, core, home, received=None, after=()):
    s, d = h2.shape
    fb = d_up.shape[1] // N_DEV
    tm = min(TILE_M, d)
    return _half_grad(name, h2, d_up, core, home, received, after, grid=(d // tm, N_CHIP),
                      a_block=(s, tm), a_map=lambda i, k, p: (0, i),
                      b_block=(s, fb), b_map=lambda i, k, p: (0, 2 * k + p),
                      o_block=(None, tm, fb), o_map=lambda i, k, p: (k, i, 0),
                      out_shape=jax.ShapeDtypeStruct((N_CHIP, d, fb), BF16))


def _in_pad(in_width):
    return -(-in_width // LANE) * LANE


def _join_col_shards(blocks):
    n, r, w = blocks.shape
    rows = min(ROWS, r)
    width = _in_pad(n * w)

    def body(x_ref, o_ref):
        tail = [jnp.zeros((rows, width - n * w), o_ref.dtype)] if width > n * w else []
        o_ref[...] = jnp.concatenate([x_ref[j] for j in range(n)] + tail, axis=1)

    return pl.pallas_call(
        body, name="join_col_shards", grid=(r // rows,),
        in_specs=[pl.BlockSpec((n, rows, w), lambda i: (0, i, 0))], out_specs=_row_spec(rows, width),
        out_shape=jax.ShapeDtypeStruct((r, width), blocks.dtype),
        compiler_params=_params("parallel"),
    )(blocks)


def _split_col_shards(g, n, w):
    r, width = g.shape
    rows = min(ROWS, r)

    def body(x_ref, o_ref):
        for j in range(n):
            o_ref[j] = x_ref[:, j * w:(j + 1) * w]

    return pl.pallas_call(
        body, name="split_col_shards", grid=(r // rows,),
        in_specs=[_row_spec(rows, width)], out_specs=pl.BlockSpec((n, rows, w), lambda i: (0, i, 0)),
        out_shape=jax.ShapeDtypeStruct((n, r, w), g.dtype),
        compiler_params=_params("parallel"),
    )(g)


def _permute_q_cols(w_uq, n_heads):
    r = w_uq.shape[0]
    w3 = w_uq.reshape(r, n_heads, QK)
    return jnp.concatenate([w3[:, :, :HEAD].reshape(r, n_heads * HEAD), w3[:, :, HEAD:].reshape(r, n_heads * ROPE)], axis=1)


def _unpermute_q_cols(w, n_heads):
    r = w.shape[0]
    nope = w[:, :n_heads * HEAD].reshape(r, n_heads, HEAD)
    rope = w[:, n_heads * HEAD:].reshape(r, n_heads, ROPE)
    return jnp.concatenate([nope, rope], axis=2).reshape(r, n_heads * QK)


def _local_step(x, tgt, gains, weights, grads, first_after=()):
    pre_mix_g, q_norm_g, kv_norm_g, conv_out_g, attn_out_g, post_mix_g, pre_mlp_g, post_mlp_g = gains
    s, d = x.shape
    conv_width = conv_out_g.shape[1]
    n_groups = conv_width // HEAD
    r_q, r_kv = q_norm_g.shape[1], kv_norm_g.shape[1]
    n_heads = attn_out_g.shape[1] // HEAD
    c_q0 = 3 * conv_width
    c_kv0 = c_q0 + r_q
    c_kr0 = c_kv0 + r_kv
    in_pad = _in_pad(c_kr0 + ROPE)
    tn_in = in_pad // 5 if in_pad % (5 * LANE) == 0 else LANE
    tables = _rope_tables(s, n_heads)

    h1 = _rms_fwd("pre_mix_norm", x, pre_mix_g, after=first_after)
    weights.forward(0, (h1, *tables))
    w_in_p, conv_w = weights.ready(0, ())
    proj = _mm_nn("in_proj", h1, w_in_p, F32, TILE_M,tn_in)
    y_conv = _conv_fwd(proj, conv_w, conv_out_g, n_groups)
    c_q = proj[:, c_q0:c_kv0]
    c_kv = proj[:, c_kv0:c_kr0]
    qn = _rms_fwd("q_norm", c_q, q_norm_g)
    kvn = _rms_fwd("kv_norm", c_kv, kv_norm_g)
    weights.forward(1, (y_conv, qn, kvn))
    w_uq_p, w_ukv, w_o = weights.ready(1, ())
    q = _mm_nn("q_up", qn, w_uq_p, F32, TILE_M, TILE_N)
    kv = _mm_nn("kv_up", kvn, w_ukv, F32, TILE_M, TILE_N)
    kr = proj[:, c_kr0:c_kr0 + LANE]
    qh, kh, vh = _pack_heads(q, kv, kr, tables, n_heads, after=weights.forward(2, (q, kv)))
    o, y_attn = _attn_fwd(qh, kh, vh, attn_out_g)
    mix = jnp.concatenate([y_conv, y_attn], axis=1)
    y = _mm_nn("out_proj", mix, w_o, F32, TILE_M, TILE_N)
    x2, h2 = _mid_fwd(x, y, post_mix_g, pre_mlp_g, after=weights.forward(3, (y,)))
    (w_up,) = weights.ready(2, (h2,))
    a, r = _up_fwd(h2, w_up)
    (w_down,) = weights.ready(3, (a,))
    m = _down_fwd(a, w_down)

    d_out, d_m, dg_post_mlp, loss_part = _head(m, x2, tgt, post_mlp_g)
    core = grads.core
    away = _down_half_grad("down_bwd_w_away", a, d_m, core, home=False)
    d_up = _down_bwd_act(d_m, w_down, r, after=grads.send_away(0, away))
    sums = _down_half_grad("down_bwd_w_home", a, d_m, core, home=True, received=grads.received(0, (d_up,)))
    away = _up_half_grad("up_bwd_w_away", h2, d_up, core, home=False, after=grads.send_sums(0, (sums,)))
    d_h2 = _up_bwd_act(d_up, w_up, after=grads.send_away(1, away))
    sums = _up_half_grad("up_bwd_w_home", h2, d_up, core, home=True, received=grads.received(1, (d_h2,)))
    d_x2, d_y, dg_pre_mlp, dg_post_mix = _mid_bwd(x2, y, d_out, d_h2, pre_mlp_g, post_mix_g, after=grads.send_sums(1, (sums,)))
    d_mix = _mm_nt("out_proj_bwd_act", d_y, w_o, F32, TILE_M, TILE_N)
    gw_o = _mm_tn("out_proj_bwd_w", mix, d_y, BF16, TILE_M, TILE_N)
    dqh, dkh, dvh, dg_attn = _attn_bwd(qh, kh, vh, o, d_mix, attn_out_g, conv_width, after=grads.full(2, (gw_o,)))
    d_q, d_kv, d_kr = _unpack_heads(dqh, dkh, dvh, tables, n_heads)
    d_qn = _mm_nt("q_up_bwd_act", d_q, w_uq_p, F32, TILE_M, TILE_N)
    d_kvn = _mm_nt("kv_up_bwd_act", d_kv, w_ukv, F32, TILE_M, TILE_N)
    gw_uq_p = _mm_tn("q_up_bwd_w", qn, d_q, BF16, TILE_M, TILE_N)
    gw_ukv = _mm_tn("kv_up_bwd_w", kvn, d_kv, BF16, TILE_M, TILE_N)
    d_cq, dg_q = _rms_bwd_call("q_norm_bwd", c_q, q_norm_g, d_qn, BF16, after=grads.full(3, (gw_uq_p, gw_ukv)))
    d_ckv, dg_kv = _rms_bwd_call("kv_norm_bwd", c_kv, kv_norm_g, d_kvn, BF16)
    d_u, d_b, d_c, dg_conv, dw_conv = _conv_bwd(proj, d_mix, conv_w, conv_out_g, n_groups)
    d_proj = jnp.concatenate([d_u, d_b, d_c, d_cq, d_ckv, d_kr[:, :in_pad - c_kr0]], axis=1)
    gw_in_p = _mm_tn("in_proj_bwd_w", h1, d_proj, BF16, TILE_M, tn_in)
    d_h1 = _mm_nt("in_proj_bwd_act", d_proj, w_in_p, F32, TILE_M, 512, after=grads.full(4, (gw_in_p,)))
    grad_x, dg_pre_mix = _first_bwd(x, pre_mix_g, d_h1, d_x2)

    small = [dg_pre_mix, dg_q, dg_kv, dg_conv, dg_attn, dg_post_mix, dg_pre_mlp, dg_post_mlp,
             dw_conv[0], dw_conv[1], dw_conv[2], loss_part]
    return grad_x, jnp.concatenate(small, axis=1)


HBM = pl.BlockSpec(memory_space=pltpu.HBM)
SEM = pl.BlockSpec(memory_space=pltpu.SEMAPHORE)
IN_VMEM = pl.BlockSpec(memory_space=pltpu.VMEM)
SPLIT = pltpu.CompilerParams(has_side_effects=pltpu.SideEffectType.DATAFLOW_SIDE_EFFECTING)


def _in_hbm(a):
    return pltpu.with_memory_space_constraint(a, pltpu.HBM)


def _hbm_like(a):
    return pltpu.HBM(a.shape, a.dtype)


def _place():
    x, y, c = lax.axis_index("x"), lax.axis_index("y"), lax.axis_index("c")
    other_chips = [(1 - x, y), (x, 1 - y), (1 - x, 1 - y)]
    return x, y, c, other_chips


def _block(px, py, pc):
    return 4 * px + 2 * py + pc


def _gather_start(name, shards, groups):
    n, ng = len(shards), len(groups)
    lands = [lax.empty((N_DEV, *a.shape), a.dtype) for a in shards]

    def body(*refs):
        src, land = refs[:n], refs[n:2 * n]
        sems, token = refs[2 * n:2 * n + 2 * ng], refs[-1]
        x, y, c, chips = _place()
        targets = [(x, y, 1 - c)] + [(*chip, c) for chip in chips]
        for gi, group in enumerate(groups):
            for i, w in enumerate(group):
                for k, to in enumerate(targets):
                    pltpu.make_async_remote_copy(
                        src_ref=src[w], dst_ref=land[w].at[_block(x, y, c)],
                        send_sem=sems[2 * gi].at[4 * i + k], recv_sem=sems[2 * gi + 1].at[4 * i + k],
                        device_id=to, device_id_type=MESH).start()
        token[...] = jnp.zeros_like(token)

    sem_shapes = [pltpu.SemaphoreType.DMA((4 * len(g),)) for g in groups for _ in range(2)]
    out = pl.pallas_call(
        body, name=name,
        in_specs=[HBM] * (2 * n),
        out_specs=[SEM] * (2 * ng) + [HBM] * (2 * n) + [IN_VMEM],
        out_shape=sem_shapes + [_hbm_like(a) for a in shards] + [_hbm_like(a) for a in lands]
        + [jax.ShapeDtypeStruct((SUBLANE, LANE), F32)],
        input_output_aliases={i: 2 * ng + i for i in range(2 * n)},
        compiler_params=SPLIT,
    )(*[_in_hbm(a) for a in shards], *[_in_hbm(a) for a in lands])
    sems = [(out[2 * gi], out[2 * gi + 1]) for gi in range(ng)]
    return sems, out[2 * ng:2 * ng + n], out[2 * ng + n:2 * ng + 2 * n], out[-1]


def _gather_forward(name, shards, lands, send1, recv1, after):
    n = len(lands)

    def body(*refs):
        src, land = refs[:n], refs[n:2 * n]
        s1, r1 = refs[2 * n], refs[2 * n + 1]
        s2, r2 = refs[2 * n + 2 + len(after)], refs[2 * n + 3 + len(after)]
        x, y, c, chips = _place()
        me, sibling = (x, y, c), (x, y, 1 - c)
        for j, chip in enumerate(chips):
            for i in range(n):
                blk = land[i].at[_block(*chip, c)]
                pltpu.make_async_remote_copy(src_ref=blk, dst_ref=blk, send_sem=s1.at[4 * i + 1 + j], recv_sem=r1.at[4 * i + 1 + j],
                                             device_id=me, device_id_type=MESH).wait_recv()
                pltpu.make_async_remote_copy(src_ref=blk, dst_ref=blk, send_sem=s2.at[3 * i + j], recv_sem=r2.at[3 * i + j],
                                             device_id=sibling, device_id_type=MESH).start()
        for i in range(n):
            blk = land[i].at[_block(x, y, 1 - c)]
            pltpu.make_async_remote_copy(src_ref=blk, dst_ref=blk, send_sem=s1.at[4 * i], recv_sem=r1.at[4 * i],
                                         device_id=me, device_id_type=MESH).wait_recv()
            for k in range(4):
                pltpu.make_async_remote_copy(src_ref=src[i], dst_ref=land[i].at[_block(x, y, c)], send_sem=s1.at[4 * i + k],
                                             recv_sem=r1.at[4 * i + k], device_id=sibling, device_id_type=MESH).wait_send()

    sem = pltpu.SemaphoreType.DMA((3 * n,))
    out = pl.pallas_call(
        body, name=name,
        in_specs=[HBM] * (2 * n) + [SEM, SEM] + [ANY] * len(after),
        out_specs=[SEM, SEM] + [HBM] * n,
        out_shape=[sem, sem] + [_hbm_like(a) for a in lands],
        input_output_aliases={n + i: 2 + i for i in range(n)},
        compiler_params=SPLIT,
    )(*shards, *lands, send1, recv1, *after)
    return (out[0], out[1]), out[2:]


def _gather_wait(name, lands, send2, recv2, after):
    n = len(lands)

    def body(*refs):
        land, s2, r2 = refs[:n], refs[n], refs[n + 1]
        x, y, c, chips = _place()
        me = (x, y, c)
        for i in range(n):
            for j, chip in enumerate(chips):
                got = land[i].at[_block(*chip, 1 - c)]
                pltpu.make_async_remote_copy(src_ref=got, dst_ref=got, send_sem=s2.at[3 * i + j], recv_sem=r2.at[3 * i + j],
                                             device_id=me, device_id_type=MESH).wait_recv()
                sent = land[i].at[_block(*chip, c)]
                pltpu.make_async_remote_copy(src_ref=sent, dst_ref=sent, send_sem=s2.at[3 * i + j], recv_sem=r2.at[3 * i + j],
                                             device_id=me, device_id_type=MESH).wait_send()

    return pl.pallas_call(
        body, name=name,
        in_specs=[HBM] * n + [SEM, SEM] + [ANY] * len(after), out_specs=[HBM] * n, out_shape=[_hbm_like(a) for a in lands],
        input_output_aliases={i: i for i in range(n)},
        compiler_params=SPLIT,
    )(*lands, send2, recv2, *after)


def _pair_exchange(name, grads):
    n = len(grads)

    def body(*refs):
        ins, recv = refs[:n], refs[n:2 * n]
        send_sems, recv_sems = refs[2 * n:]
        x, y, c, _ = _place()
        sends = []
        for w in range(n):
            for k in range(N_CHIP):
                sends.append(pltpu.make_async_remote_copy(
                    src_ref=ins[w].at[2 * k + 1 - c], dst_ref=recv[w].at[k],
                    send_sem=send_sems.at[w, k], recv_sem=recv_sems.at[w, k],
                    device_id=(x, y, 1 - c), device_id_type=MESH))
        for cp in sends:
            cp.start()
        for cp in sends:
            cp.wait()

    return pl.pallas_call(
        body, name=name,
        in_specs=[ANY] * n, out_specs=[ANY] * n,
        out_shape=[jax.ShapeDtypeStruct((N_CHIP, *g.shape[1:]), g.dtype) for g in grads],
        scratch_shapes=[pltpu.SemaphoreType.DMA((n, N_CHIP))] * 2,
    )(*grads)


def _pair_sum(name, grad, received, core):
    _, r, c = received.shape
    rows = min(ROWS, r)
    assert r % rows == 0

    def body(core_ref, a_ref, b_ref, o_ref):
        o_ref[...] = (a_ref[...].astype(F32) + b_ref[...].astype(F32)).astype(o_ref.dtype)

    spec = pl.BlockSpec((None, rows, c), lambda k, i, core_ref: (k, i, 0))
    return pl.pallas_call(
        body, name=name,
        grid_spec=pltpu.PrefetchScalarGridSpec(
            num_scalar_prefetch=1, grid=(N_CHIP, r // rows),
            in_specs=[pl.BlockSpec((None, None, rows, c), lambda k, i, core_ref: (k, core_ref[0], i, 0)), spec],
            out_specs=spec),
        out_shape=jax.ShapeDtypeStruct(received.shape, received.dtype),
        compiler_params=_params("parallel", "parallel"),
    )(core, grad.reshape(N_CHIP, 2, r, c), received)


def _pair_send_start(name, away):
    land = lax.empty(away.shape, away.dtype)

    def body(src, dst, send, recv, src_thru, dst_thru, token):
        x, y, c, _ = _place()
        pltpu.make_async_remote_copy(src_ref=src, dst_ref=dst, send_sem=send, recv_sem=recv,
                                     device_id=(x, y, 1 - c), device_id_type=MESH).start()
        token[...] = jnp.zeros_like(token)

    sem = pltpu.SemaphoreType.DMA(())
    out = pl.pallas_call(
        body, name=name,
        in_specs=[HBM, HBM], out_specs=[SEM, SEM, HBM, HBM, IN_VMEM],
        out_shape=[sem, sem, _hbm_like(away), _hbm_like(land), jax.ShapeDtypeStruct((SUBLANE, LANE), F32)],
        input_output_aliases={0: 2, 1: 3},
        compiler_params=SPLIT,
    )(_in_hbm(away), _in_hbm(land))
    return (out[0], out[1]), out[2], out[3], out[4]


def _pair_send_wait(name, sems, src, land, after):
    def body(src_ref, dst_ref, send, recv, *rest):
        x, y, c, _ = _place()
        pltpu.make_async_remote_copy(src_ref=src_ref, dst_ref=dst_ref, send_sem=send, recv_sem=recv,
                                     device_id=(x, y, 1 - c), device_id_type=MESH).wait()

    return pl.pallas_call(
        body, name=name,
        in_specs=[HBM, HBM, SEM, SEM] + [ANY] * len(after), out_specs=HBM, out_shape=_hbm_like(land),
        input_output_aliases={1: 0},
        compiler_params=SPLIT,
    )(src, land, *sems, *after)


def _chip_send_start(name, sums):
    n = len(sums)
    lands = [lax.empty(a.shape, a.dtype) for a in sums]

    def body(*refs):
        src, land = refs[:n], refs[n:2 * n]
        send, recv, token = refs[2 * n], refs[2 * n + 1], refs[-1]
        x, y, c, chips = _place()
        for w in range(n):
            for j, (px, py) in enumerate(chips):
                pltpu.make_async_remote_copy(
                    src_ref=src[w].at[2 * px + py], dst_ref=land[w].at[2 * x + y],
                    send_sem=send.at[3 * w + j], recv_sem=recv.at[3 * w + j],
                    device_id=(px, py, c), device_id_type=MESH).start()
        token[...] = jnp.zeros_like(token)

    sem = pltpu.SemaphoreType.DMA((3 * n,))
    out = pl.pallas_call(
        body, name=name,
        in_specs=[HBM] * (2 * n),
        out_specs=[SEM, SEM] + [HBM] * (2 * n) + [IN_VMEM],
        out_shape=[sem, sem] + [_hbm_like(a) for a in sums] + [_hbm_like(a) for a in lands]
        + [jax.ShapeDtypeStruct((SUBLANE, LANE), F32)],
        input_output_aliases={i: 2 + i for i in range(2 * n)},
        compiler_params=SPLIT,
    )(*[_in_hbm(a) for a in sums], *[_in_hbm(a) for a in lands])
    return (out[0], out[1]), out[2:2 + n], out[2 + n:2 + 2 * n], out[-1]


def _chip_send_wait(name, groups, after):
    counts = [len(g[1]) for g in groups]
    n = sum(counts)

    def body(*refs):
        src, land = refs[:n], refs[n:2 * n]
        sems = refs[2 * n:2 * n + 2 * len(groups)]
        x, y, c, chips = _place()
        w = 0
        for gi, count in enumerate(counts):
            for i in range(count):
                for j, (px, py) in enumerate(chips):
                    pltpu.make_async_remote_copy(
                        src_ref=src[w].at[2 * px + py], dst_ref=land[w].at[2 * px + py],
                        send_sem=sems[2 * gi].at[3 * i + j], recv_sem=sems[2 * gi + 1].at[3 * i + j],
                        device_id=(px, py, c), device_id_type=MESH).wait()
                w += 1

    sums = [a for g in groups for a in g[1]]
    lands = [a for g in groups for a in g[2]]
    sems = [s for g in groups for s in g[0]]
    return pl.pallas_call(
        body, name=name,
        in_specs=[HBM] * (2 * n) + [SEM] * len(sems) + [ANY] * len(after),
        out_specs=[HBM] * n, out_shape=[_hbm_like(a) for a in lands],
        input_output_aliases={n + i: i for i in range(n)},
        compiler_params=SPLIT,
    )(*sums, *lands, *sems, *after)


def _small_all_reduce(part, after=()):
    _, w = part.shape

    def body(p_ref, *rest):
        o_ref, buf, send_sems, recv_sems = rest[len(after):]
        x, y, c, _ = _place()
        me = 4 * x + 2 * y + c
        buf[me] = jnp.sum(p_ref[...], axis=0, keepdims=True)
        copies = []
        for k in range(1, N_DEV):
            dx, dy, dc = (k >> 2) & 1, (k >> 1) & 1, k & 1
            copies.append(pltpu.make_async_remote_copy(
                src_ref=buf.at[me], dst_ref=buf.at[me], send_sem=send_sems.at[k - 1], recv_sem=recv_sems.at[k - 1],
                device_id=(x ^ dx, y ^ dy, c ^ dc), device_id_type=MESH))
        for cp in copies:
            cp.start()
        for cp in copies:
            cp.wait()
        tot = buf[0]
        for d in range(1, N_DEV):
            tot = tot + buf[d]
        o_ref[...] = tot
        loss = jnp.sum(tot[:, w - LANE:], axis=1, keepdims=True)
        o_ref[:, w - LANE:] = jnp.broadcast_to(loss, (1, LANE))

    return pl.pallas_call(
        body, name="small_all_reduce",
        in_specs=[IN_VMEM] + [ANY] * len(after), out_specs=IN_VMEM,
        out_shape=jax.ShapeDtypeStruct((1, w), F32),
        scratch_shapes=[pltpu.VMEM((N_DEV, 1, w), F32), pltpu.SemaphoreType.DMA((N_DEV - 1,)), pltpu.SemaphoreType.DMA((N_DEV - 1,))],
        compiler_params=pltpu.CompilerParams(vmem_limit_bytes=VMEM_LIMIT_BYTES),
    )(part, *after)


def _adamw(w, g, m, v):
    m = ADAM_B1 * m + (1.0 - ADAM_B1) * g
    v = ADAM_B2 * v + (1.0 - ADAM_B2) * (g * g)
    m_hat = m / (1.0 - ADAM_B1 ** ADAM_STEP)
    v_hat = v / (1.0 - ADAM_B2 ** ADAM_STEP)
    delta = -ADAM_LR * (m_hat / (jnp.sqrt(v_hat) + ADAM_EPS) + ADAM_WD * w)
    return delta, m, v


def _sum_adam(name, parts, sums, chip, w, m, v, after=()):
    _, r, c = w.shape
    rows = min(ROWS, r)
    assert r % rows == 0
    n_after = len(after)

    def body(chip_ref, p_ref, own_ref, w_ref, m_ref, v_ref, *rest):
        g_ref, d_ref, mo_ref, vo_ref = rest[n_after:]
        g = None
        for k in range(N_CHIP):
            term = jnp.where(chip_ref[0] == k, own_ref[...], p_ref[k]).astype(F32)
            g = term if g is None else g + term
        g_ref[...] = g
        d_ref[...], mo_ref[...], vo_ref[...] = _adamw(w_ref[...], g, m_ref[...], v_ref[...])

    blk = pl.BlockSpec((None, rows, c), lambda i, chip_ref: (0, i, 0))
    out = jax.ShapeDtypeStruct((1, r, c), F32)
    return pl.pallas_call(
        body, name=name,
        grid_spec=pltpu.PrefetchScalarGridSpec(
            num_scalar_prefetch=1, grid=(r // rows,),
            in_specs=[pl.BlockSpec((N_CHIP, rows, c), lambda i, chip_ref: (0, i, 0)),
                      pl.BlockSpec((None, rows, c), lambda i, chip_ref: (chip_ref[0], i, 0)), blk, blk, blk]
            + [ANY] * n_after,
            out_specs=[blk] * 4),
        out_shape=[out] * 4,
        compiler_params=_params("parallel"),
    )(chip, parts, sums, w, m, v, *after)


def _adam_gains(total, ws, ms, vs):
    n = len(ws)
    widths = [w.shape[1] for w in ws]

    def body(t_ref, *refs):
        w_refs, m_refs, v_refs, outs = refs[:n], refs[n:2 * n], refs[2 * n:3 * n], refs[3 * n:]
        off = 0
        for i in range(n):
            g = t_ref[:, off:off + widths[i]]
            off += widths[i]
            g_ref, d_ref, mo_ref, vo_ref = outs[4 * i:4 * i + 4]
            g_ref[...] = g
            d_ref[...], mo_ref[...], vo_ref[...] = _adamw(w_refs[i][...], g, m_refs[i][...], v_refs[i][...])

    out = pl.pallas_call(
        body, name="adam_gains",
        out_shape=[jax.ShapeDtypeStruct(w.shape, F32) for w in ws for _ in range(4)],
    )(total, *ws, *ms, *vs)
    return [tuple(out[4 * i:4 * i + 4]) for i in range(n)]


def _adam_taps(total, first_col, device, w, m, v):
    _, n_taps, cw = w.shape
    col_block = lambda t, dev: (0, first_col // cw + t * N_DEV + dev[0])
    tap = pl.BlockSpec((None, 1, cw), lambda t, dev: (t, 0, 0))

    def body(dev_ref, t_ref, w_ref, m_ref, v_ref, g_ref, d_ref, mo_ref, vo_ref):
        g = t_ref[...]
        g_ref[...] = g
        d_ref[...], mo_ref[...], vo_ref[...] = _adamw(w_ref[...], g, m_ref[...], v_ref[...])

    shape3 = (n_taps, 1, cw)
    out = pl.pallas_call(
        body, name="adam_taps",
        grid_spec=pltpu.PrefetchScalarGridSpec(
            num_scalar_prefetch=1, grid=(n_taps,),
            in_specs=[pl.BlockSpec((1, cw), col_block), tap, tap, tap], out_specs=[tap] * 4),
        out_shape=[jax.ShapeDtypeStruct(shape3, F32)] * 4,
    )(device, total, w.reshape(shape3), m.reshape(shape3), v.reshape(shape3))
    return tuple(o.reshape(w.shape) for o in out)


def kernel(x, pre_mix_g, w_in, conv_w, q_norm_g, w_uq, kv_norm_g, w_ukv, conv_out_g, attn_out_g, w_o, post_mix_g, pre_mlp_g, w_up, w_down, post_mlp_g, loss_target, m_pre_mix_g, m_w_in, m_conv_w, m_q_norm_g, m_w_uq, m_kv_norm_g, m_w_ukv, m_conv_out_g, m_attn_out_g, m_w_o, m_post_mix_g, m_pre_mlp_g, m_w_up, m_w_down, m_post_mlp_g, v_pre_mix_g, v_w_in, v_conv_w, v_q_norm_g, v_w_uq, v_kv_norm_g, v_w_ukv, v_conv_out_g, v_attn_out_g, v_w_o, v_post_mix_g, v_pre_mlp_g, v_w_up, v_w_down, v_post_mlp_g):
    me = 4 * lax.axis_index("x") + 2 * lax.axis_index("y") + lax.axis_index("c")
    core = lax.axis_index("c").astype(jnp.int32).reshape(1)
    chip = (2 * lax.axis_index("x") + lax.axis_index("y")).astype(jnp.int32).reshape(1)
    gains = (pre_mix_g, q_norm_g, kv_norm_g, conv_out_g, attn_out_g, post_mix_g, pre_mlp_g, post_mlp_g)
    gain_m = (m_pre_mix_g, m_q_norm_g, m_kv_norm_g, m_conv_out_g, m_attn_out_g, m_post_mix_g, m_pre_mlp_g, m_post_mlp_g)
    gain_v = (v_pre_mix_g, v_q_norm_g, v_kv_norm_g, v_conv_out_g, v_attn_out_g, v_post_mix_g, v_pre_mlp_g, v_post_mlp_g)
    names = ("w_in", "w_uq", "w_ukv", "w_o", "w_up", "w_down")
    big = dict(zip(names, (w_in, w_uq, w_ukv, w_o, w_up, w_down)))
    big_m = dict(zip(names, (m_w_in, m_w_uq, m_w_ukv, m_w_o, m_w_up, m_w_down)))
    big_v = dict(zip(names, (v_w_in, v_w_uq, v_w_ukv, v_w_o, v_w_up, v_w_down)))
    n_heads = attn_out_g.shape[1] // HEAD
    n_taps = conv_w.shape[1]

    gathered = ("w_in", "conv", "w_uq", "w_ukv", "w_o", "w_up", "w_down")
    gather_groups = ((0, 1), (2, 3, 4), (5,), (6,))
    taps = jnp.pad(conv_w[0], ((0, SUBLANE - n_taps), (0, 0)))
    sems_a, shards_a, lands_a, token_a = _gather_start("gather_start_first", [w_in[0].astype(BF16), taps], ((0, 1),))
    behind = token_a[0, 0]
    sems_b, shards_b, lands_b, token = _gather_start(
        "gather_start_rest", [(big[nm][0] + behind).astype(BF16) for nm in gathered[2:]], ((0, 1, 2), (3,), (4,)))
    sems1, shards, lands = sems_a + sems_b, [*shards_a, *shards_b], [*lands_a, *lands_b]

    cols = lambda a: jnp.concatenate([a[j] for j in range(N_DEV)], axis=1)
    rows = lambda a: a.reshape(N_DEV * a.shape[1], a.shape[2])
    ready = {
        "w_in": _join_col_shards,
        "conv": lambda a: cols(a)[:n_taps],
        "w_uq": lambda a: _permute_q_cols(cols(a), n_heads),
        "w_ukv": cols, "w_o": rows, "w_up": lambda a: a, "w_down": rows,
    }

    class Weights:
        def __init__(self):
            self.passed = {}

        def forward(self, group, after):
            idx = gather_groups[group]
            self.passed[group] = _gather_forward(f"gather_forward_{group}", [shards[i] for i in idx], [lands[i] for i in idx],
                                                 *sems1[group], after)
            return tuple(self.passed[group][1])

        def ready(self, group, after):
            sems2, mid = self.passed[group]
            full = _gather_wait(f"gather_wait_{group}", mid, *sems2, after)
            out = []
            for i, a in zip(gather_groups[group], full):
                a = lax.dynamic_update_index_in_dim(a, shards[i], me, 0)
                out.append(ready[gathered[i]](a))
            return out

    weights = Weights()

    col_blocks = lambda g: g.reshape(g.shape[0], N_DEV, g.shape[1] // N_DEV).transpose(1, 0, 2)
    row_blocks = lambda g: g.reshape(N_DEV, g.shape[0] // N_DEV, g.shape[1])
    in_shard = w_in.shape[2]
    grad_groups = (("w_down",), ("w_up",), ("w_o",), ("w_uq", "w_ukv"), ("w_in",))
    to_blocks = {
        "w_in": lambda g: _split_col_shards(g, N_DEV, in_shard),
        "w_uq": lambda g: col_blocks(_unpermute_q_cols(g, n_heads)),
        "w_ukv": col_blocks, "w_o": row_blocks, "w_up": lambda g: g, "w_down": row_blocks,
    }
    in_flight = []

    class Grads:
        def __init__(self):
            self.core = core
            self.away = {}

        def send_sums(self, group, sums):
            sems, sums, parts, tok = _chip_send_start(f"chip_send_start_{group}", list(sums))
            in_flight.append((sems, sums, parts))
            return (tok,)

        def full(self, group, arrays):
            nms = grad_groups[group]
            blocks = [to_blocks[nm](g) for nm, g in zip(nms, arrays)]
            received = _pair_exchange(f"pair_exchange_{group}", blocks)
            return self.send_sums(group, [_pair_sum(f"pair_sum_{nm}", g, r, core) for nm, g, r in zip(nms, blocks, received)])

        def send_away(self, group, half):
            sems, src, land, tok = _pair_send_start(f"pair_send_start_{group}", half)
            self.away[group] = (sems, src, land)
            return (tok,)

        def received(self, group, after):
            sems, src, land = self.away[group]
            return _pair_send_wait(f"pair_send_wait_{group}", sems, src, land, after)

    grad_x, small = _local_step(x[0], loss_target[0], gains, weights, Grads(), first_after=(token,))

    big_out = {}

    def update(tag, first, last, after):
        groups = in_flight[first:last]
        parts = _chip_send_wait("chip_send_wait_" + tag, groups, after)
        nms = [nm for grp in grad_groups[first:last] for nm in grp]
        sums = [a for _, s, _ in groups for a in s]
        for nm, p, s in zip(nms, parts, sums):
            big_out[nm] = _sum_adam("adam_" + nm, p, s, chip, big[nm], big_m[nm], big_v[nm], after=after)
            after = (big_out[nm][0],)
        return after

    after = update("early", 0, len(in_flight) - 1, (grad_x,))
    total = _small_all_reduce(small, after=after)
    update("late", len(in_flight) - 1, len(in_flight), (total,))
    big_out = [big_out[nm] for nm in names]

    gain_out = _adam_gains(total, gains, gain_m, gain_v)
    taps_out = _adam_taps(total, sum(g.shape[1] for g in gains), me.astype(jnp.int32).reshape(1), conv_w, m_conv_w, v_conv_w)
    loss = total[0, total.shape[1] - 1]

    order = (0, "w_in", "conv", 1, "w_uq", 2, "w_ukv", 3, 4, "w_o", 5, 6, "w_up", "w_down", 7)
    by_name = dict(zip(names, big_out))
    outs = [loss, grad_x[None]]
    for kind in range(4):
        for item in order:
            if item == "conv":
                outs.append(taps_out[kind])
            elif isinstance(item, int):
                outs.append(gain_out[item][kind])
            else:
                outs.append(by_name[item][kind])
    return tuple(outs)
```

```python
import math

import jax
import jax.numpy as jnp
from jax import lax
from jax.experimental import pallas as pl
from jax.experimental.pallas import tpu as pltpu

F32 = jnp.float32
BF16 = jnp.bfloat16

EPS = 1e-6
NEG_INF = -1e30
HEAD = 128
ROPE = 64
QK = HEAD + ROPE
CHUNK = 64
ROPE_THETA = 10000.0
ADAM_LR, ADAM_B1, ADAM_B2, ADAM_EPS, ADAM_WD, ADAM_STEP = 0.001, 0.9, 0.999, 1e-08, 0.01, 10

LANE = 128
SUBLANE = 8
VMEM_LIMIT_BYTES = 56 * 1024 * 1024

N_DEV = 8
N_CHIP = 4
MESH = pl.DeviceIdType.MESH


def _params(*sem):
    return pltpu.CompilerParams(dimension_semantics=sem, vmem_limit_bytes=VMEM_LIMIT_BYTES)


ANY = pl.BlockSpec(memory_space=pl.ANY)


def _call(body, *, in_specs, after=(), **kw):
    n_in, n_after = len(in_specs), len(after)

    def ordered(*refs):
        body(*refs[:n_in], *refs[n_in + n_after:])

    call = pl.pallas_call(ordered, in_specs=[*in_specs, *[ANY] * n_after], **kw)
    return lambda *operands: call(*operands, *after)


def _sublane_sum(v):
    r, w = v.shape
    return jnp.sum(v.reshape(r // SUBLANE, SUBLANE, w), axis=0)


def _rstd(x):
    return lax.rsqrt(jnp.mean(x * x, axis=-1, keepdims=True) + EPS)


def _rms_bwd(x, g, dy):
    r = _rstd(x)
    xh = x * r
    dxh = dy * g
    dx = r * (dxh - xh * jnp.mean(dxh * xh, axis=-1, keepdims=True))
    return dx, dy * xh


def _accumulate(ref, val, step):
    @pl.when(step == 0)
    def _():
        ref[...] = val

    @pl.when(step > 0)
    def _():
        ref[...] += val


NN = ((1,), (0,))
NT = ((1,), (1,))
TN = ((0,), (0,))


def _matmul(name, a, b, *, grid, a_spec, b_spec, out_shape, out_specs, contract, nk=1, acc_shape=None,
            extras=(), extra_specs=(), epilogue=None, after=()):
    multi = isinstance(out_shape, (tuple, list))
    out_shapes = tuple(out_shape) if multi else (out_shape,)
    n_out = len(out_shapes)
    n_extra = len(extras)

    def body(a_ref, b_ref, *rest):
        x_refs = rest[:n_extra]
        o_refs = rest[n_extra:n_extra + n_out]

        def emit(acc):
            vals = epilogue(acc, *[r[...] for r in x_refs]) if epilogue else (acc,)
            for r, v in zip(o_refs, vals):
                r[...] = v.astype(r.dtype)

        p = lax.dot_general(a_ref[...], b_ref[...], (contract, ((), ())), preferred_element_type=F32)
        if nk == 1:
            emit(p)
        else:
            acc_ref = rest[n_extra + n_out]
            k = pl.program_id(2)
            _accumulate(acc_ref, p, k)

            @pl.when(k == nk - 1)
            def _():
                emit(acc_ref[...])

    sem = ("parallel", "parallel") + (("arbitrary",) if nk > 1 else ())
    return _call(
        body, name=name, grid=grid, after=after,
        in_specs=[a_spec, b_spec, *extra_specs],
        out_specs=out_specs,
        out_shape=out_shape,
        scratch_shapes=[pltpu.VMEM(acc_shape, F32)] if nk > 1 else [],
        compiler_params=_params(*sem),
    )(a, b, *extras)


def _fit(n, tile):
    if n <= tile:
        return n
    t = tile - tile % LANE
    while n % t:
        t -= LANE
    return t


def _mm_nn(name, a, b, out_dtype, tm, tn, after=()):
    m, k = a.shape
    n = b.shape[1]
    tm, tn = _fit(m, tm), _fit(n, tn)
    return _matmul(name, a, b, grid=(m // tm, n // tn), after=after,
                   a_spec=pl.BlockSpec((tm, k), lambda i, j: (i, 0)),
                   b_spec=pl.BlockSpec((k, tn), lambda i, j: (0, j)),
                   out_shape=jax.ShapeDtypeStruct((m, n), out_dtype),
                   out_specs=pl.BlockSpec((tm, tn), lambda i, j: (i, j)), contract=NN)


def _mm_nt(name, a, b, out_dtype, tm, tn, after=()):
    m, k = a.shape
    n = b.shape[0]
    tm, tn = _fit(m, tm), _fit(n, tn)
    return _matmul(name, a, b, grid=(m // tm, n // tn), after=after,
                   a_spec=pl.BlockSpec((tm, k), lambda i, j: (i, 0)),
                   b_spec=pl.BlockSpec((tn, k), lambda i, j: (j, 0)),
                   out_shape=jax.ShapeDtypeStruct((m, n), out_dtype),
                   out_specs=pl.BlockSpec((tm, tn), lambda i, j: (i, j)), contract=NT)


def _mm_tn(name, a, b, out_dtype, tm, tn):
    s, m = a.shape
    n = b.shape[1]
    tm, tn = _fit(m, tm), _fit(n, tn)
    return _matmul(name, a, b, grid=(m // tm, n // tn),
                   a_spec=pl.BlockSpec((s, tm), lambda i, j: (0, i)),
                   b_spec=pl.BlockSpec((s, tn), lambda i, j: (0, j)),
                   out_shape=jax.ShapeDtypeStruct((m, n), out_dtype),
                   out_specs=pl.BlockSpec((tm, tn), lambda i, j: (i, j)), contract=TN)


ROWS = 256


def _row_spec(rows, width):
    return pl.BlockSpec((rows, width), lambda i: (i, 0))


def _fixed_spec(rows, width):
    return pl.BlockSpec((rows, width), lambda i: (0, 0))


def _column_pieces(rows, start, width):
    piece = math.gcd(start, width)
    assert piece % LANE == 0
    return [pl.BlockSpec((rows, piece), lambda i, b=start // piece + p: (i, b)) for p in range(width // piece)]


def _rms_fwd(name, x, g, cols=None, after=()):
    s = x.shape[0]
    start, w = cols or (0, x.shape[1])
    rows = min(ROWS, s)
    pieces = _column_pieces(rows, start, w) if cols else [_row_spec(rows, w)]
    n = len(pieces)

    def body(*refs):
        g_ref, o_ref = refs[n:]
        xv = refs[0][...] if n == 1 else jnp.concatenate([r[...] for r in refs[:n]], axis=1)
        o_ref[...] = (xv * _rstd(xv) * g_ref[...]).astype(o_ref.dtype)

    return _call(
        body, name=name, grid=(s // rows,), after=after,
        in_specs=[*pieces, _fixed_spec(1, w)],
        out_specs=_row_spec(rows, w),
        out_shape=jax.ShapeDtypeStruct((s, w), BF16),
        compiler_params=_params("parallel"),
    )(*[x] * n, g)


def _rms_bwd_call(name, x, g, dy, out_dtype, cols=None, after=()):
    s = x.shape[0]
    start, w = cols or (0, x.shape[1])
    rows = min(ROWS, s)
    pieces = _column_pieces(rows, start, w) if cols else [_row_spec(rows, w)]
    n = len(pieces)

    def body(*refs):
        g_ref, dy_ref, dx_ref, dg_ref = refs[n:]
        xv = refs[0][...] if n == 1 else jnp.concatenate([r[...] for r in refs[:n]], axis=1)
        dx, dgc = _rms_bwd(xv, g_ref[...], dy_ref[...].astype(F32))
        dx_ref[...] = dx.astype(dx_ref.dtype)
        _accumulate(dg_ref, _sublane_sum(dgc), pl.program_id(0))

    return _call(
        body, name=name, grid=(s // rows,), after=after,
        in_specs=[*pieces, _fixed_spec(1, w), _row_spec(rows, w)],
        out_specs=[_row_spec(rows, w), _fixed_spec(SUBLANE, w)],
        out_shape=[jax.ShapeDtypeStruct((s, w), out_dtype), jax.ShapeDtypeStruct((SUBLANE, w), F32)],
        compiler_params=_params("arbitrary"),
    )(*[x] * n, g, dy)


def _mid_fwd(x, y, g_post, g_pre, after=()):
    s, w = x.shape
    rows = min(ROWS, s)

    def body(x_ref, y_ref, gp_ref, gq_ref, x2_ref, h2_ref):
        yv = y_ref[...]
        x2 = x_ref[...] + yv * _rstd(yv) * gp_ref[...]
        x2_ref[...] = x2
        h2_ref[...] = (x2 * _rstd(x2) * gq_ref[...]).astype(h2_ref.dtype)

    return _call(
        body, name="mid_fwd", grid=(s // rows,), after=after,
        in_specs=[_row_spec(rows, w), _row_spec(rows, w), _fixed_spec(1, w), _fixed_spec(1, w)],
        out_specs=[_row_spec(rows, w), _row_spec(rows, w)],
        out_shape=[jax.ShapeDtypeStruct((s, w), F32), jax.ShapeDtypeStruct((s, w), BF16)],
        compiler_params=_params("parallel"),
    )(x, y, g_post, g_pre)


def _head(m, x2, tgt, g):
    s, w = m.shape
    rows = min(ROWS, s)

    def body(m_ref, x2_ref, t_ref, g_ref, dout_ref, dm_ref, dg_ref, loss_ref):
        mv = m_ref[...]
        gv = g_ref[...]
        out = x2_ref[...] + mv * _rstd(mv) * gv
        err = out - t_ref[...]
        dout = err * (1.0 / w)
        dout_ref[...] = dout
        dm, dgc = _rms_bwd(mv, gv, dout)
        dm_ref[...] = dm.astype(dm_ref.dtype)
        sq = err * err
        lanes = sq[:, 0:LANE]
        for j in range(1, w // LANE):
            lanes = lanes + sq[:, j * LANE:(j + 1) * LANE]
        step = pl.program_id(0)
        _accumulate(dg_ref, _sublane_sum(dgc), step)
        _accumulate(loss_ref, _sublane_sum(lanes) * (0.5 / w), step)

    return pl.pallas_call(
        body, name="head", grid=(s // rows,),
        in_specs=[_row_spec(rows, w), _row_spec(rows, w), _row_spec(rows, w), _fixed_spec(1, w)],
        out_specs=[_row_spec(rows, w), _row_spec(rows, w), _fixed_spec(SUBLANE, w), _fixed_spec(SUBLANE, LANE)],
        out_shape=[jax.ShapeDtypeStruct((s, w), F32), jax.ShapeDtypeStruct((s, w), BF16),
                   jax.ShapeDtypeStruct((SUBLANE, w), F32), jax.ShapeDtypeStruct((SUBLANE, LANE), F32)],
        compiler_params=_params("arbitrary"),
    )(m, x2, tgt, g)


def _mid_bwd(x2, y, d_out, d_h2, g_pre, g_post, after=()):
    s, w = x2.shape
    rows = min(ROWS, s)

    def body(x2_ref, y_ref, dout_ref, dh2_ref, gq_ref, gp_ref, dx2_ref, dy_ref, dgq_ref, dgp_ref):
        dx, dgq = _rms_bwd(x2_ref[...], gq_ref[...], dh2_ref[...])
        dx2 = dout_ref[...] + dx
        dx2_ref[...] = dx2
        dy, dgp = _rms_bwd(y_ref[...], gp_ref[...], dx2)
        dy_ref[...] = dy.astype(dy_ref.dtype)
        step = pl.program_id(0)
        _accumulate(dgq_ref, _sublane_sum(dgq), step)
        _accumulate(dgp_ref, _sublane_sum(dgp), step)

    return _call(
        body, name="mid_bwd", grid=(s // rows,), after=after,
        in_specs=[_row_spec(rows, w)] * 4 + [_fixed_spec(1, w)] * 2,
        out_specs=[_row_spec(rows, w), _row_spec(rows, w), _fixed_spec(SUBLANE, w), _fixed_spec(SUBLANE, w)],
        out_shape=[jax.ShapeDtypeStruct((s, w), F32), jax.ShapeDtypeStruct((s, w), BF16),
                   jax.ShapeDtypeStruct((SUBLANE, w), F32), jax.ShapeDtypeStruct((SUBLANE, w), F32)],
        compiler_params=_params("arbitrary"),
    )(x2, y, d_out, d_h2, g_pre, g_post)


def _first_bwd(x, g, d_h1, d_x2, after=()):
    s, w = x.shape
    rows = min(ROWS, s)

    def body(x_ref, g_ref, dh_ref, dx2_ref, dx_ref, dg_ref):
        dx, dgc = _rms_bwd(x_ref[...], g_ref[...], dh_ref[...])
        dx_ref[...] = dx2_ref[...] + dx
        _accumulate(dg_ref, _sublane_sum(dgc), pl.program_id(0))

    return _call(
        body, name="first_bwd", grid=(s // rows,), after=after,
        in_specs=[_row_spec(rows, w), _fixed_spec(1, w), _row_spec(rows, w), _row_spec(rows, w)],
        out_specs=[_row_spec(rows, w), _fixed_spec(SUBLANE, w)],
        out_shape=[jax.ShapeDtypeStruct((s, w), F32), jax.ShapeDtypeStruct((SUBLANE, w), F32)],
        compiler_params=_params("arbitrary"),
    )(x, g, d_h1, d_x2)


def _shift_down(v, k):
    t = lax.broadcasted_iota(jnp.int32, v.shape, 0)
    return jnp.where(t >= k, pltpu.roll(v, k, 0), 0.0)


def _shift_up(v, k):
    n = v.shape[0]
    t = lax.broadcasted_iota(jnp.int32, v.shape, 0)
    return jnp.where(t < n - k, pltpu.roll(v, n - k, 0), 0.0)


def _conv_core(u, b, c, w):
    z = c * u
    conv = w[0:1, :] * _shift_down(z, 2) + w[1:2, :] * _shift_down(z, 1) + w[2:3, :] * z
    return z, conv, b * conv


def _conv_fwd(proj, conv_w, g, n_groups):
    s = proj.shape[0]

    def body(u_ref, b_ref, c_ref, w_ref, g_ref, o_ref):
        _, _, yr = _conv_core(u_ref[...], b_ref[...], c_ref[...], w_ref[...])
        o_ref[...] = (yr * _rstd(yr) * g_ref[...]).astype(o_ref.dtype)

    col = lambda k: pl.BlockSpec((s, HEAD), lambda i: (0, k * n_groups + i))
    return pl.pallas_call(
        body, name="conv_fwd", grid=(n_groups,),
        in_specs=[col(0), col(1), col(2), pl.BlockSpec((3, HEAD), lambda i: (0, i)), pl.BlockSpec((1, HEAD), lambda i: (0, i))],
        out_specs=pl.BlockSpec((s, HEAD), lambda i: (0, i)),
        out_shape=jax.ShapeDtypeStruct((s, n_groups * HEAD), BF16),
        compiler_params=_params("parallel"),
    )(proj, proj, proj, conv_w, g)


def _conv_bwd(proj, d_mix, conv_w, g, n_groups):
    s = proj.shape[0]
    width = n_groups * HEAD

    def body(u_ref, b_ref, c_ref, dy_ref, w_ref, g_ref, du_ref, db_ref, dc_ref, dg_ref, dw_ref):
        u, b, c, w = u_ref[...], b_ref[...], c_ref[...], w_ref[...]
        z, conv, yr = _conv_core(u, b, c, w)
        dyr, dgc = _rms_bwd(yr, g_ref[...], dy_ref[...])
        dconv = dyr * b
        db_ref[...] = (dyr * conv).astype(db_ref.dtype)
        dz = w[2:3, :] * dconv + w[1:2, :] * _shift_up(dconv, 1) + w[0:1, :] * _shift_up(dconv, 2)
        dc_ref[...] = (dz * u).astype(dc_ref.dtype)
        du_ref[...] = (dz * c).astype(du_ref.dtype)
        dg_ref[...] = _sublane_sum(dgc)
        dw_ref[0] = _sublane_sum(dconv * _shift_down(z, 2))
        dw_ref[1] = _sublane_sum(dconv * _shift_down(z, 1))
        dw_ref[2] = _sublane_sum(dconv * z)

    col = lambda k: pl.BlockSpec((s, HEAD), lambda i: (0, k * n_groups + i))
    grp = pl.BlockSpec((s, HEAD), lambda i: (0, i))
    return pl.pallas_call(
        body, name="conv_bwd", grid=(n_groups,),
        in_specs=[col(0), col(1), col(2), grp, pl.BlockSpec((3, HEAD), lambda i: (0, i)), pl.BlockSpec((1, HEAD), lambda i: (0, i))],
        out_specs=[grp, grp, grp, pl.BlockSpec((SUBLANE, HEAD), lambda i: (0, i)),
                   pl.BlockSpec((3, SUBLANE, HEAD), lambda i: (0, 0, i))],
        out_shape=[jax.ShapeDtypeStruct((s, width), BF16)] * 3
        + [jax.ShapeDtypeStruct((SUBLANE, width), F32), jax.ShapeDtypeStruct((3, SUBLANE, width), F32)],
        compiler_params=_params("parallel"),
    )(proj, proj, proj, d_mix, conv_w, g)


def _rope_tables(s, n_heads):
    pos = jnp.arange(s, dtype=F32)
    inv_freq = jnp.power(ROPE_THETA, -jnp.arange(0, ROPE, 2, dtype=F32) / ROPE)
    ang = pos[:, None] * inv_freq[None, :]
    cos, sin = jnp.cos(ang), jnp.sin(ang)
    cs = jnp.concatenate([cos, cos], axis=1)
    sn = jnp.concatenate([-sin, sin], axis=1)
    pad = jnp.zeros((s, LANE - ROPE), F32)
    return (jnp.tile(cs, (1, n_heads)), jnp.tile(sn, (1, n_heads)),
            jnp.concatenate([cs, pad], axis=1), jnp.concatenate([sn, pad], axis=1))


def _swap_halves(v):
    w = v.shape[1]
    lane = lax.broadcasted_iota(jnp.int32, v.shape, 1)
    first = (lane % ROPE) < (ROPE // 2)
    return jnp.where(first, pltpu.roll(v, w - ROPE // 2, 1), pltpu.roll(v, ROPE // 2, 1))


def _pack_heads(q, kv, proj, kr_col, tables, n_heads, after=()):
    s = q.shape[0]
    rows = min(ROWS, s)
    cq, sq, ck, sk = tables
    wq = n_heads * ROPE

    def body(q_ref, kv_ref, kr_ref, cq_ref, sq_ref, ck_ref, sk_ref, qo_ref, ko_ref, vo_ref):
        qr = q_ref[:, n_heads * HEAD:]
        qr = qr * cq_ref[...] + _swap_halves(qr) * sq_ref[...]
        krv = kr_ref[...]
        krv = krv * ck_ref[...] + _swap_halves(krv) * sk_ref[...]
        for h in range(n_heads):
            qo_ref[h] = jnp.concatenate([q_ref[:, h * HEAD:(h + 1) * HEAD], qr[:, h * ROPE:(h + 1) * ROPE]], axis=1).astype(BF16)
            ko_ref[h] = jnp.concatenate([kv_ref[:, 2 * h * HEAD:(2 * h + 1) * HEAD], krv[:, :ROPE]], axis=1).astype(BF16)
            vo_ref[h] = kv_ref[:, (2 * h + 1) * HEAD:(2 * h + 2) * HEAD].astype(BF16)

    hs = lambda w: pl.BlockSpec((n_heads, rows, w), lambda i: (0, i, 0))
    return _call(
        body, name="pack_heads", grid=(s // rows,), after=after,
        in_specs=[_row_spec(rows, q.shape[1]), _row_spec(rows, kv.shape[1]), pl.BlockSpec((rows, LANE), lambda i: (i, kr_col // LANE)),
                  _row_spec(rows, wq), _row_spec(rows, wq), _row_spec(rows, LANE), _row_spec(rows, LANE)],
        out_specs=[hs(QK), hs(QK), hs(HEAD)],
        out_shape=[jax.ShapeDtypeStruct((n_heads, s, QK), BF16), jax.ShapeDtypeStruct((n_heads, s, QK), BF16),
                   jax.ShapeDtypeStruct((n_heads, s, HEAD), BF16)],
        compiler_params=_params("parallel"),
    )(q, kv, proj, cq, sq, ck, sk)


def _unpack_heads(dq, dk, dv, tables, n_heads):
    s = dq.shape[1]
    rows = min(ROWS, s)
    cq, sq, ck, sk = tables
    wq = n_heads * ROPE

    def body(dq_ref, dk_ref, dv_ref, cq_ref, sq_ref, ck_ref, sk_ref, qo_ref, kvo_ref, kro_ref):
        dqr = jnp.concatenate([dq_ref[h][:, HEAD:] for h in range(n_heads)], axis=1)
        dqr = dqr * cq_ref[...] - _swap_halves(dqr) * sq_ref[...]
        dkr = dk_ref[0][:, HEAD:]
        for h in range(1, n_heads):
            dkr = dkr + dk_ref[h][:, HEAD:]
        dkr = jnp.concatenate([dkr, jnp.zeros((rows, LANE - ROPE), F32)], axis=1)
        dkr = dkr * ck_ref[...] - _swap_halves(dkr) * sk_ref[...]
        kro_ref[...] = dkr.astype(kro_ref.dtype)
        qo_ref[:, n_heads * HEAD:] = dqr.astype(qo_ref.dtype)
        for h in range(n_heads):
            qo_ref[:, h * HEAD:(h + 1) * HEAD] = dq_ref[h][:, :HEAD].astype(qo_ref.dtype)
            kvo_ref[:, 2 * h * HEAD:(2 * h + 1) * HEAD] = dk_ref[h][:, :HEAD].astype(kvo_ref.dtype)
            kvo_ref[:, (2 * h + 1) * HEAD:(2 * h + 2) * HEAD] = dv_ref[h].astype(kvo_ref.dtype)

    hs = lambda w: pl.BlockSpec((n_heads, rows, w), lambda i: (0, i, 0))
    return pl.pallas_call(
        body, name="unpack_heads", grid=(s // rows,),
        in_specs=[hs(QK), hs(QK), hs(HEAD), _row_spec(rows, wq), _row_spec(rows, wq), _row_spec(rows, LANE), _row_spec(rows, LANE)],
        out_specs=[_row_spec(rows, n_heads * QK), _row_spec(rows, 2 * n_heads * HEAD), _row_spec(rows, LANE)],
        out_shape=[jax.ShapeDtypeStruct((s, n_heads * QK), BF16), jax.ShapeDtypeStruct((s, 2 * n_heads * HEAD), BF16),
                   jax.ShapeDtypeStruct((s, LANE), BF16)],
        compiler_params=_params("parallel"),
    )(dq, dk, dv, cq, sq, ck, sk)


TQ = 256


LOG2_E = 1.4426950408889634


def _softmax_parts(q, k):
    tq, n_keys = q.shape[0], k.shape[0]
    sc = lax.dot_general(q, k, (NT, ((), ())), preferred_element_type=F32) * (QK ** -0.5 * LOG2_E)
    row = lax.broadcasted_iota(jnp.int32, (tq, tq), 0)
    col = lax.broadcasted_iota(jnp.int32, (tq, tq), 1)
    own = jnp.where(col // CHUNK <= row // CHUNK, sc[:, n_keys - tq:], NEG_INF)
    sc = own if n_keys == tq else jnp.concatenate([sc[:, :n_keys - tq], own], axis=1)
    e = jnp.exp2(sc - jnp.max(sc, axis=-1, keepdims=True))
    return e, 1.0 / jnp.sum(e, axis=-1, keepdims=True)


def _attn_fwd(q, k, v, g):
    n_heads, s, _ = q.shape
    tq = min(TQ, s)
    assert tq % CHUNK == 0 and s % tq == 0

    def body(q_ref, k_ref, v_ref, g_ref, o_ref, y_ref):
        for c in range(s // tq):
            rows, n_keys = pl.ds(c * tq, tq), (c + 1) * tq
            e, inv = _softmax_parts(q_ref[rows, :], k_ref[0:n_keys, :])
            o = jnp.dot(e.astype(BF16), v_ref[0:n_keys, :], preferred_element_type=F32) * inv
            o_ref[rows, :] = o
            y_ref[rows, :] = (o * _rstd(o) * g_ref[...]).astype(y_ref.dtype)

    head = lambda w: pl.BlockSpec((None, s, w), lambda h: (h, 0, 0))
    return pl.pallas_call(
        body, name="attn_fwd", grid=(n_heads,),
        in_specs=[head(QK), head(QK), head(HEAD), pl.BlockSpec((1, HEAD), lambda h: (0, h))],
        out_specs=[head(HEAD), pl.BlockSpec((s, HEAD), lambda h: (0, h))],
        out_shape=[jax.ShapeDtypeStruct((n_heads, s, HEAD), F32), jax.ShapeDtypeStruct((s, n_heads * HEAD), BF16)],
        compiler_params=_params("parallel"),
    )(q, k, v, g)


def _attn_bwd(q, k, v, o, d_mix, g, col0, after=()):
    n_heads, s, _ = q.shape
    tq = min(TQ, s)

    def body(q_ref, k_ref, v_ref, o_ref, dy_ref, g_ref, dq_ref, dk_ref, dv_ref, dg_ref):
        dg = None
        for c in reversed(range(s // tq)):
            rows, n_keys = pl.ds(c * tq, tq), (c + 1) * tq
            qv, kv_, vv = q_ref[rows, :], k_ref[0:n_keys, :], v_ref[0:n_keys, :]
            do, dgc = _rms_bwd(o_ref[rows, :], g_ref[...], dy_ref[rows, :])
            do = do.astype(BF16)
            dg = _sublane_sum(dgc) if dg is None else dg + _sublane_sum(dgc)
            e, inv = _softmax_parts(qv, kv_)
            p = e * inv
            dp = lax.dot_general(do, vv, (NT, ((), ())), preferred_element_type=F32)
            ds = (p * (dp - jnp.sum(p * dp, axis=-1, keepdims=True)) * (QK ** -0.5)).astype(BF16)
            dq_ref[rows, :] = jnp.dot(ds, kv_, preferred_element_type=F32)
            dk = lax.dot_general(ds, qv, (TN, ((), ())), preferred_element_type=F32)
            dv = lax.dot_general(p.astype(BF16), do, (TN, ((), ())), preferred_element_type=F32)
            if n_keys == s:
                dk_ref[...] = dk
                dv_ref[...] = dv
            else:
                dk_ref[0:n_keys, :] += dk
                dv_ref[0:n_keys, :] += dv
        dg_ref[...] = dg

    c0 = col0 // HEAD
    head = lambda w: pl.BlockSpec((None, s, w), lambda h: (h, 0, 0))
    return _call(
        body, name="attn_bwd", grid=(n_heads,), after=after,
        in_specs=[head(QK), head(QK), head(HEAD), head(HEAD), pl.BlockSpec((s, HEAD), lambda h: (0, c0 + h)),
                  pl.BlockSpec((1, HEAD), lambda h: (0, h))],
        out_specs=[head(QK), head(QK), head(HEAD), pl.BlockSpec((SUBLANE, HEAD), lambda h: (0, h))],
        out_shape=[jax.ShapeDtypeStruct((n_heads, s, QK), F32), jax.ShapeDtypeStruct((n_heads, s, QK), F32),
                   jax.ShapeDtypeStruct((n_heads, s, HEAD), F32), jax.ShapeDtypeStruct((SUBLANE, n_heads * HEAD), F32)],
        compiler_params=_params("parallel"),
    )(q, k, v, o, d_mix, g)


TILE_M = 1024
TILE_N = 1024


def _up_fwd(h2, w_up):
    s, d = h2.shape
    nb, _, fb = w_up.shape
    tm = min(TILE_M,s)

    def epilogue(acc):
        r = jnp.maximum(acc, 0.0)
        return r * r, r

    blk = pl.BlockSpec((tm, fb), lambda i, j: (i, j))
    return _matmul("up_fwd", h2, w_up, grid=(s // tm, nb),
                   a_spec=pl.BlockSpec((tm, d), lambda i, j: (i, 0)),
                   b_spec=pl.BlockSpec((None, d, fb), lambda i, j: (j, 0, 0)),
                   out_shape=[jax.ShapeDtypeStruct((s, nb * fb), BF16)] * 2, out_specs=[blk, blk],
                   contract=NN, epilogue=epilogue)


def _down_fwd(a, w_down):
    s, f = a.shape
    d = w_down.shape[1]
    tm, tn, tk = min(TILE_M,s), min(TILE_N,d), 2048
    nk = f // tk
    return _matmul("down_fwd", a, w_down, grid=(s // tm, d // tn, nk),
                   a_spec=pl.BlockSpec((tm, tk), lambda i, j, k: (i, k)),
                   b_spec=pl.BlockSpec((tk, tn), lambda i, j, k: (k, j)),
                   out_shape=jax.ShapeDtypeStruct((s, d), F32),
                   out_specs=pl.BlockSpec((tm, tn), lambda i, j, k: (i, j)),
                   contract=NN, nk=nk, acc_shape=(tm, tn))


def _down_bwd_act(d_m, w_down, r, after=()):
    s, d = d_m.shape
    f = w_down.shape[0]
    tm, tn = min(TILE_M,s), min(TILE_N,f)
    blk = pl.BlockSpec((tm, tn), lambda i, j: (i, j))
    return _matmul("down_bwd_act", d_m, w_down, grid=(s // tm, f // tn), after=after,
                   a_spec=pl.BlockSpec((tm, d), lambda i, j: (i, 0)),
                   b_spec=pl.BlockSpec((tn, d), lambda i, j: (j, 0)),
                   out_shape=jax.ShapeDtypeStruct((s, f), BF16), out_specs=blk, contract=NT,
                   extras=(r,), extra_specs=(blk,),
                   epilogue=lambda acc, rv: (acc * (2.0 * rv.astype(F32)),))


def _up_bwd_act(d_up, w_up, after=()):
    s, _ = d_up.shape
    nb, d, fb = w_up.shape
    tm, tn = min(TILE_M,s), min(TILE_N,d)
    return _matmul("up_bwd_act", d_up, w_up, grid=(s // tm, d // tn, nb), after=after,
                   a_spec=pl.BlockSpec((tm, fb), lambda i, j, k: (i, k)),
                   b_spec=pl.BlockSpec((None, tn, fb), lambda i, j, k: (k, j, 0)),
                   out_shape=jax.ShapeDtypeStruct((s, d), F32),
                   out_specs=pl.BlockSpec((tm, tn), lambda i, j, k: (i, j)),
                   contract=NT, nk=nb, acc_shape=(tm, tn))


def _half_grad(name, a, b, core, home, received, after, *, grid, a_block, a_map, b_block, b_map, o_block, o_map, out_shape):
    n_after = len(after)
    pick = (lambda ref: ref[0]) if home else (lambda ref: 1 - ref[0])

    def body(core_ref, a_ref, b_ref, *rest):
        acc = lax.dot_general(a_ref[...], b_ref[...], (TN, ((), ())), preferred_element_type=F32)
        if received is not None:
            acc = acc + rest[0][...].astype(F32)
        rest[-1][...] = acc.astype(rest[-1].dtype)

    wrap = lambda fn: (lambda i, j, core_ref: fn(i, j, pick(core_ref)))
    o_spec = pl.BlockSpec(o_block, wrap(o_map))
    extra = [] if received is None else [o_spec]
    operands = [] if received is None else [received]
    return pl.pallas_call(
        body, name=name,
        grid_spec=pltpu.PrefetchScalarGridSpec(
            num_scalar_prefetch=1, grid=grid,
            in_specs=[pl.BlockSpec(a_block, wrap(a_map)), pl.BlockSpec(b_block, wrap(b_map))] + extra + [ANY] * n_after,
            out_specs=o_spec),
        out_shape=out_shape,
        compiler_params=_params("parallel", "parallel"),
    )(core, a, b, *operands, *after)


def _down_half_grad(name, a, d_m, core, home, received=None, after=()):
    s, f = a.shape
    d = d_m.shape[1]
    r = f // N_DEV
    tn = min(TILE_N, d)
    return _half_grad(name, a, d_m, core, home, received, after, grid=(N_CHIP, d // tn),
                      a_block=(s, r), a_map=lambda k, j, p: (0, 2 * k + p),
                      b_block=(s, tn), b_map=lambda k, j, p: (0, j),
                      o_block=(None, r, tn), o_map=lambda k, j, p: (k, 0, j),
                      out_shape=jax.ShapeDtypeStruct((N_CHIP, r, d), BF16))


def _up_half_grad(name, h2, d_up, core, home, received=None, after=()):
    s, d = h2.shape
    fb = d_up.shape[1] // N_DEV
    tm = min(TILE_M, d)
    return _half_grad(name, h2, d_up, core, home, received, after, grid=(d // tm, N_CHIP),
                      a_block=(s, tm), a_map=lambda i, k, p: (0, i),
                      b_block=(s, fb), b_map=lambda i, k, p: (0, 2 * k + p),
                      o_block=(None, tm, fb), o_map=lambda i, k, p: (k, i, 0),
                      out_shape=jax.ShapeDtypeStruct((N_CHIP, d, fb), BF16))


def _in_pad(in_width):
    return -(-in_width // LANE) * LANE


def _join_col_shards(blocks):
    n, r, w = blocks.shape
    rows = min(ROWS, r)
    width = _in_pad(n * w)

    def body(x_ref, o_ref):
        tail = [jnp.zeros((rows, width - n * w), o_ref.dtype)] if width > n * w else []
        o_ref[...] = jnp.concatenate([x_ref[j] for j in range(n)] + tail, axis=1)

    return pl.pallas_call(
        body, name="join_col_shards", grid=(r // rows,),
        in_specs=[pl.BlockSpec((n, rows, w), lambda i: (0, i, 0))], out_specs=_row_spec(rows, width),
        out_shape=jax.ShapeDtypeStruct((r, width), blocks.dtype),
        compiler_params=_params("parallel"),
    )(blocks)


def _split_col_shards(g, n, w):
    r, width = g.shape
    rows = min(ROWS, r)

    def body(x_ref, o_ref):
        for j in range(n):
            o_ref[j] = x_ref[:, j * w:(j + 1) * w]

    return pl.pallas_call(
        body, name="split_col_shards", grid=(r // rows,),
        in_specs=[_row_spec(rows, width)], out_specs=pl.BlockSpec((n, rows, w), lambda i: (0, i, 0)),
        out_shape=jax.ShapeDtypeStruct((n, r, w), g.dtype),
        compiler_params=_params("parallel"),
    )(g)


def _permute_q_cols(w_uq, n_heads):
    r = w_uq.shape[0]
    w3 = w_uq.reshape(r, n_heads, QK)
    return jnp.concatenate([w3[:, :, :HEAD].reshape(r, n_heads * HEAD), w3[:, :, HEAD:].reshape(r, n_heads * ROPE)], axis=1)


def _unpermute_q_cols(w, n_heads):
    r = w.shape[0]
    nope = w[:, :n_heads * HEAD].reshape(r, n_heads, HEAD)
    rope = w[:, n_heads * HEAD:].reshape(r, n_heads, ROPE)
    return jnp.concatenate([nope, rope], axis=2).reshape(r, n_heads * QK)


def _local_step(x, tgt, gains, weights, grads, first_after=()):
    pre_mix_g, q_norm_g, kv_norm_g, conv_out_g, attn_out_g, post_mix_g, pre_mlp_g, post_mlp_g = gains
    s, d = x.shape
    conv_width = conv_out_g.shape[1]
    n_groups = conv_width // HEAD
    r_q, r_kv = q_norm_g.shape[1], kv_norm_g.shape[1]
    n_heads = attn_out_g.shape[1] // HEAD
    c_q0 = 3 * conv_width
    c_kv0 = c_q0 + r_q
    c_kr0 = c_kv0 + r_kv
    in_pad = _in_pad(c_kr0 + ROPE)
    tn_in = in_pad // 5 if in_pad % (5 * LANE) == 0 else LANE
    tables = _rope_tables(s, n_heads)

    h1 = _rms_fwd("pre_mix_norm", x, pre_mix_g, after=first_after)
    weights.forward(0, (h1, *tables))
    w_in_p, conv_w = weights.ready(0, ())
    proj = _mm_nn("in_proj", h1, w_in_p, F32, TILE_M,tn_in)
    y_conv = _conv_fwd(proj, conv_w, conv_out_g, n_groups)
    qn = _rms_fwd("q_norm", proj, q_norm_g, cols=(c_q0, r_q))
    kvn = _rms_fwd("kv_norm", proj, kv_norm_g, cols=(c_kv0, r_kv))
    weights.forward(1, (y_conv, qn, kvn))
    w_uq_p, w_ukv, w_o = weights.ready(1, ())
    q = _mm_nn("q_up", qn, w_uq_p, F32, TILE_M, TILE_N)
    kv = _mm_nn("kv_up", kvn, w_ukv, F32, TILE_M, TILE_N)
    qh, kh, vh = _pack_heads(q, kv, proj, c_kr0, tables, n_heads)
    o, y_attn = _attn_fwd(qh, kh, vh, attn_out_g)
    mix = jnp.concatenate([y_conv, y_attn], axis=1)
    y = _mm_nn("out_proj", mix, w_o, F32, TILE_M, TILE_N, after=weights.forward(2, (y_attn,)))
    x2, h2 = _mid_fwd(x, y, post_mix_g, pre_mlp_g)
    (w_up,) = weights.ready(2, (h2,))
    a, r = _up_fwd(h2, w_up)
    weights.forward(3, (a,))
    (w_down,) = weights.ready(3, ())
    m = _down_fwd(a, w_down)

    d_out, d_m, dg_post_mlp, loss_part = _head(m, x2, tgt, post_mlp_g)
    core = grads.core
    away = _down_half_grad("down_bwd_w_away", a, d_m, core, home=False)
    d_up = _down_bwd_act(d_m, w_down, r, after=grads.send_away(0, away))
    sums = _down_half_grad("down_bwd_w_home", a, d_m, core, home=True, received=grads.received(0, (d_up,)))
    away = _up_half_grad("up_bwd_w_away", h2, d_up, core, home=False, after=grads.send_sums(0, (sums,)))
    d_h2 = _up_bwd_act(d_up, w_up, after=grads.send_away(1, away))
    sums = _up_half_grad("up_bwd_w_home", h2, d_up, core, home=True, received=grads.received(1, (d_h2,)))
    d_x2, d_y, dg_pre_mlp, dg_post_mix = _mid_bwd(x2, y, d_out, d_h2, pre_mlp_g, post_mix_g, after=grads.send_sums(1, (sums,)))
    d_mix = _mm_nt("out_proj_bwd_act", d_y, w_o, F32, TILE_M, TILE_N)
    gw_o = _mm_tn("out_proj_bwd_w", mix, d_y, BF16, TILE_M, TILE_N)
    dqh, dkh, dvh, dg_attn = _attn_bwd(qh, kh, vh, o, d_mix, attn_out_g, conv_width, after=grads.full(2, (gw_o,)))
    d_q, d_kv, d_kr = _unpack_heads(dqh, dkh, dvh, tables, n_heads)
    d_qn = _mm_nt("q_up_bwd_act", d_q, w_uq_p, F32, TILE_M, TILE_N)
    d_kvn = _mm_nt("kv_up_bwd_act", d_kv, w_ukv, F32, TILE_M, TILE_N)
    gw_uq_p = _mm_tn("q_up_bwd_w", qn, d_q, BF16, TILE_M, TILE_N)
    gw_ukv = _mm_tn("kv_up_bwd_w", kvn, d_kv, BF16, TILE_M, TILE_N)
    d_cq, dg_q = _rms_bwd_call("q_norm_bwd", proj, q_norm_g, d_qn, BF16, cols=(c_q0, r_q), after=grads.full(3, (gw_uq_p, gw_ukv)))
    d_ckv, dg_kv = _rms_bwd_call("kv_norm_bwd", proj, kv_norm_g, d_kvn, BF16, cols=(c_kv0, r_kv))
    d_u, d_b, d_c, dg_conv, dw_conv = _conv_bwd(proj, d_mix, conv_w, conv_out_g, n_groups)
    d_proj = jnp.concatenate([d_u, d_b, d_c, d_cq, d_ckv, d_kr[:, :in_pad - c_kr0]], axis=1)
    gw_in_p = _mm_tn("in_proj_bwd_w", h1, d_proj, BF16, TILE_M, tn_in)
    d_h1 = _mm_nt("in_proj_bwd_act", d_proj, w_in_p, F32, TILE_M, 512, after=grads.full(4, (gw_in_p,)))
    grad_x, dg_pre_mix = _first_bwd(x, pre_mix_g, d_h1, d_x2)

    small = [dg_pre_mix, dg_q, dg_kv, dg_conv, dg_attn, dg_post_mix, dg_pre_mlp, dg_post_mlp,
             dw_conv[0], dw_conv[1], dw_conv[2], loss_part]
    return grad_x, jnp.concatenate(small, axis=1)


HBM = pl.BlockSpec(memory_space=pltpu.HBM)
SEM = pl.BlockSpec(memory_space=pltpu.SEMAPHORE)
IN_VMEM = pl.BlockSpec(memory_space=pltpu.VMEM)
SPLIT = pltpu.CompilerParams(has_side_effects=pltpu.SideEffectType.DATAFLOW_SIDE_EFFECTING)


def _in_hbm(a):
    return pltpu.with_memory_space_constraint(a, pltpu.HBM)


def _hbm_like(a):
    return pltpu.HBM(a.shape, a.dtype)


def _place():
    x, y, c = lax.axis_index("x"), lax.axis_index("y"), lax.axis_index("c")
    other_chips = [(1 - x, y), (x, 1 - y), (1 - x, 1 - y)]
    return x, y, c, other_chips


def _block(px, py, pc):
    return 4 * px + 2 * py + pc


def _gather_start(name, shards, groups):
    n, ng = len(shards), len(groups)
    lands = [lax.empty((N_DEV, *a.shape), a.dtype) for a in shards]

    def body(*refs):
        src, land = refs[:n], refs[n:2 * n]
        sems, token = refs[2 * n:2 * n + 2 * ng], refs[-1]
        x, y, c, chips = _place()
        targets = [(x, y, 1 - c)] + [(*chip, c) for chip in chips]
        for gi, group in enumerate(groups):
            for i, w in enumerate(group):
                for k, to in enumerate(targets):
                    pltpu.make_async_remote_copy(
                        src_ref=src[w], dst_ref=land[w].at[_block(x, y, c)],
                        send_sem=sems[2 * gi].at[4 * i + k], recv_sem=sems[2 * gi + 1].at[4 * i + k],
                        device_id=to, device_id_type=MESH).start()
        token[...] = jnp.zeros_like(token)

    sem_shapes = [pltpu.SemaphoreType.DMA((4 * len(g),)) for g in groups for _ in range(2)]
    out = pl.pallas_call(
        body, name=name,
        in_specs=[HBM] * (2 * n),
        out_specs=[SEM] * (2 * ng) + [HBM] * (2 * n) + [IN_VMEM],
        out_shape=sem_shapes + [_hbm_like(a) for a in shards] + [_hbm_like(a) for a in lands]
        + [jax.ShapeDtypeStruct((SUBLANE, LANE), F32)],
        input_output_aliases={i: 2 * ng + i for i in range(2 * n)},
        compiler_params=SPLIT,
    )(*[_in_hbm(a) for a in shards], *[_in_hbm(a) for a in lands])
    sems = [(out[2 * gi], out[2 * gi + 1]) for gi in range(ng)]
    return sems, out[2 * ng:2 * ng + n], out[2 * ng + n:2 * ng + 2 * n], out[-1]


def _gather_forward(name, shards, lands, send1, recv1, after):
    n = len(lands)

    def body(*refs):
        src, land = refs[:n], refs[n:2 * n]
        s1, r1 = refs[2 * n], refs[2 * n + 1]
        s2, r2 = refs[2 * n + 2 + len(after)], refs[2 * n + 3 + len(after)]
        x, y, c, chips = _place()
        me, sibling = (x, y, c), (x, y, 1 - c)
        for j, chip in enumerate(chips):
            for i in range(n):
                blk = land[i].at[_block(*chip, c)]
                pltpu.make_async_remote_copy(src_ref=blk, dst_ref=blk, send_sem=s1.at[4 * i + 1 + j], recv_sem=r1.at[4 * i + 1 + j],
                                             device_id=me, device_id_type=MESH).wait_recv()
                pltpu.make_async_remote_copy(src_ref=blk, dst_ref=blk, send_sem=s2.at[3 * i + j], recv_sem=r2.at[3 * i + j],
                                             device_id=sibling, device_id_type=MESH).start()
        for i in range(n):
            blk = land[i].at[_block(x, y, 1 - c)]
            pltpu.make_async_remote_copy(src_ref=blk, dst_ref=blk, send_sem=s1.at[4 * i], recv_sem=r1.at[4 * i],
                                         device_id=me, device_id_type=MESH).wait_recv()
            for k in range(4):
                pltpu.make_async_remote_copy(src_ref=src[i], dst_ref=land[i].at[_block(x, y, c)], send_sem=s1.at[4 * i + k],
                                             recv_sem=r1.at[4 * i + k], device_id=sibling, device_id_type=MESH).wait_send()

    sem = pltpu.SemaphoreType.DMA((3 * n,))
    out = pl.pallas_call(
        body, name=name,
        in_specs=[HBM] * (2 * n) + [SEM, SEM] + [ANY] * len(after),
        out_specs=[SEM, SEM] + [HBM] * n,
        out_shape=[sem, sem] + [_hbm_like(a) for a in lands],
        input_output_aliases={n + i: 2 + i for i in range(n)},
        compiler_params=SPLIT,
    )(*shards, *lands, send1, recv1, *after)
    return (out[0], out[1]), out[2:]


def _gather_wait(name, lands, send2, recv2, after):
    n = len(lands)

    def body(*refs):
        land, s2, r2 = refs[:n], refs[n], refs[n + 1]
        x, y, c, chips = _place()
        me = (x, y, c)
        for i in range(n):
            for j, chip in enumerate(chips):
                got = land[i].at[_block(*chip, 1 - c)]
                pltpu.make_async_remote_copy(src_ref=got, dst_ref=got, send_sem=s2.at[3 * i + j], recv_sem=r2.at[3 * i + j],
                                             device_id=me, device_id_type=MESH).wait_recv()
                sent = land[i].at[_block(*chip, c)]
                pltpu.make_async_remote_copy(src_ref=sent, dst_ref=sent, send_sem=s2.at[3 * i + j], recv_sem=r2.at[3 * i + j],
                                             device_id=me, device_id_type=MESH).wait_send()

    return pl.pallas_call(
        body, name=name,
        in_specs=[HBM] * n + [SEM, SEM] + [ANY] * len(after), out_specs=[HBM] * n, out_shape=[_hbm_like(a) for a in lands],
        input_output_aliases={i: i for i in range(n)},
        compiler_params=SPLIT,
    )(*lands, send2, recv2, *after)


def _pair_exchange(name, grads):
    n = len(grads)

    def body(*refs):
        ins, recv = refs[:n], refs[n:2 * n]
        send_sems, recv_sems = refs[2 * n:]
        x, y, c, _ = _place()
        sends = []
        for w in range(n):
            for k in range(N_CHIP):
                sends.append(pltpu.make_async_remote_copy(
                    src_ref=ins[w].at[2 * k + 1 - c], dst_ref=recv[w].at[k],
                    send_sem=send_sems.at[w, k], recv_sem=recv_sems.at[w, k],
                    device_id=(x, y, 1 - c), device_id_type=MESH))
        for cp in sends:
            cp.start()
        for cp in sends:
            cp.wait()

    return pl.pallas_call(
        body, name=name,
        in_specs=[ANY] * n, out_specs=[ANY] * n,
        out_shape=[jax.ShapeDtypeStruct((N_CHIP, *g.shape[1:]), g.dtype) for g in grads],
        scratch_shapes=[pltpu.SemaphoreType.DMA((n, N_CHIP))] * 2,
    )(*grads)


def _pair_sum(name, grad, received, core):
    _, r, c = received.shape
    rows = min(ROWS, r)
    assert r % rows == 0

    def body(core_ref, a_ref, b_ref, o_ref):
        o_ref[...] = (a_ref[...].astype(F32) + b_ref[...].astype(F32)).astype(o_ref.dtype)

    spec = pl.BlockSpec((None, rows, c), lambda k, i, core_ref: (k, i, 0))
    return pl.pallas_call(
        body, name=name,
        grid_spec=pltpu.PrefetchScalarGridSpec(
            num_scalar_prefetch=1, grid=(N_CHIP, r // rows),
            in_specs=[pl.BlockSpec((None, None, rows, c), lambda k, i, core_ref: (k, core_ref[0], i, 0)), spec],
            out_specs=spec),
        out_shape=jax.ShapeDtypeStruct(received.shape, received.dtype),
        compiler_params=_params("parallel", "parallel"),
    )(core, grad.reshape(N_CHIP, 2, r, c), received)


def _pair_send_start(name, away):
    land = lax.empty(away.shape, away.dtype)

    def body(src, dst, send, recv, src_thru, dst_thru, token):
        x, y, c, _ = _place()
        pltpu.make_async_remote_copy(src_ref=src, dst_ref=dst, send_sem=send, recv_sem=recv,
                                     device_id=(x, y, 1 - c), device_id_type=MESH).start()
        token[...] = jnp.zeros_like(token)

    sem = pltpu.SemaphoreType.DMA(())
    out = pl.pallas_call(
        body, name=name,
        in_specs=[HBM, HBM], out_specs=[SEM, SEM, HBM, HBM, IN_VMEM],
        out_shape=[sem, sem, _hbm_like(away), _hbm_like(land), jax.ShapeDtypeStruct((SUBLANE, LANE), F32)],
        input_output_aliases={0: 2, 1: 3},
        compiler_params=SPLIT,
    )(_in_hbm(away), _in_hbm(land))
    return (out[0], out[1]), out[2], out[3], out[4]


def _pair_send_wait(name, sems, src, land, after):
    def body(src_ref, dst_ref, send, recv, *rest):
        x, y, c, _ = _place()
        pltpu.make_async_remote_copy(src_ref=src_ref, dst_ref=dst_ref, send_sem=send, recv_sem=recv,
                                     device_id=(x, y, 1 - c), device_id_type=MESH).wait()

    return pl.pallas_call(
        body, name=name,
        in_specs=[HBM, HBM, SEM, SEM] + [ANY] * len(after), out_specs=HBM, out_shape=_hbm_like(land),
        input_output_aliases={1: 0},
        compiler_params=SPLIT,
    )(src, land, *sems, *after)


def _chip_send_start(name, sums):
    n = len(sums)
    lands = [lax.empty(a.shape, a.dtype) for a in sums]

    def body(*refs):
        src, land = refs[:n], refs[n:2 * n]
        send, recv, token = refs[2 * n], refs[2 * n + 1], refs[-1]
        x, y, c, chips = _place()
        for w in range(n):
            for j, (px, py) in enumerate(chips):
                pltpu.make_async_remote_copy(
                    src_ref=src[w].at[2 * px + py], dst_ref=land[w].at[2 * x + y],
                    send_sem=send.at[3 * w + j], recv_sem=recv.at[3 * w + j],
                    device_id=(px, py, c), device_id_type=MESH).start()
        token[...] = jnp.zeros_like(token)

    sem = pltpu.SemaphoreType.DMA((3 * n,))
    out = pl.pallas_call(
        body, name=name,
        in_specs=[HBM] * (2 * n),
        out_specs=[SEM, SEM] + [HBM] * (2 * n) + [IN_VMEM],
        out_shape=[sem, sem] + [_hbm_like(a) for a in sums] + [_hbm_like(a) for a in lands]
        + [jax.ShapeDtypeStruct((SUBLANE, LANE), F32)],
        input_output_aliases={i: 2 + i for i in range(2 * n)},
        compiler_params=SPLIT,
    )(*[_in_hbm(a) for a in sums], *[_in_hbm(a) for a in lands])
    return (out[0], out[1]), out[2:2 + n], out[2 + n:2 + 2 * n], out[-1]


def _chip_send_wait(name, groups, after):
    counts = [len(g[1]) for g in groups]
    n = sum(counts)

    def body(*refs):
        src, land = refs[:n], refs[n:2 * n]
        sems = refs[2 * n:2 * n + 2 * len(groups)]
        x, y, c, chips = _place()
        w = 0
        for gi, count in enumerate(counts):
            for i in range(count):
                for j, (px, py) in enumerate(chips):
                    pltpu.make_async_remote_copy(
                        src_ref=src[w].at[2 * px + py], dst_ref=land[w].at[2 * px + py],
                        send_sem=sems[2 * gi].at[3 * i + j], recv_sem=sems[2 * gi + 1].at[3 * i + j],
                        device_id=(px, py, c), device_id_type=MESH).wait()
                w += 1

    sums = [a for g in groups for a in g[1]]
    lands = [a for g in groups for a in g[2]]
    sems = [s for g in groups for s in g[0]]
    return pl.pallas_call(
        body, name=name,
        in_specs=[HBM] * (2 * n) + [SEM] * len(sems) + [ANY] * len(after),
        out_specs=[HBM] * n, out_shape=[_hbm_like(a) for a in lands],
        input_output_aliases={n + i: i for i in range(n)},
        compiler_params=SPLIT,
    )(*sums, *lands, *sems, *after)


def _small_all_reduce(part, after=()):
    _, w = part.shape

    def body(p_ref, *rest):
        o_ref, buf, send_sems, recv_sems = rest[len(after):]
        x, y, c, _ = _place()
        me = 4 * x + 2 * y + c
        buf[me] = jnp.sum(p_ref[...], axis=0, keepdims=True)
        copies = []
        for k in range(1, N_DEV):
            dx, dy, dc = (k >> 2) & 1, (k >> 1) & 1, k & 1
            copies.append(pltpu.make_async_remote_copy(
                src_ref=buf.at[me], dst_ref=buf.at[me], send_sem=send_sems.at[k - 1], recv_sem=recv_sems.at[k - 1],
                device_id=(x ^ dx, y ^ dy, c ^ dc), device_id_type=MESH))
        for cp in copies:
            cp.start()
        for cp in copies:
            cp.wait()
        tot = buf[0]
        for d in range(1, N_DEV):
            tot = tot + buf[d]
        o_ref[...] = tot
        loss = jnp.sum(tot[:, w - LANE:], axis=1, keepdims=True)
        o_ref[:, w - LANE:] = jnp.broadcast_to(loss, (1, LANE))

    return pl.pallas_call(
        body, name="small_all_reduce",
        in_specs=[IN_VMEM] + [ANY] * len(after), out_specs=IN_VMEM,
        out_shape=jax.ShapeDtypeStruct((1, w), F32),
        scratch_shapes=[pltpu.VMEM((N_DEV, 1, w), F32), pltpu.SemaphoreType.DMA((N_DEV - 1,)), pltpu.SemaphoreType.DMA((N_DEV - 1,))],
        compiler_params=pltpu.CompilerParams(vmem_limit_bytes=VMEM_LIMIT_BYTES),
    )(part, *after)


def _adamw(w, g, m, v):
    m = ADAM_B1 * m + (1.0 - ADAM_B1) * g
    v = ADAM_B2 * v + (1.0 - ADAM_B2) * (g * g)
    m_hat = m / (1.0 - ADAM_B1 ** ADAM_STEP)
    v_hat = v / (1.0 - ADAM_B2 ** ADAM_STEP)
    delta = -ADAM_LR * (m_hat / (jnp.sqrt(v_hat) + ADAM_EPS) + ADAM_WD * w)
    return delta, m, v


def _sum_adam(name, parts, sums, chip, w, m, v, after=()):
    _, r, c = w.shape
    rows = min(ROWS, r)
    assert r % rows == 0
    n_after = len(after)

    def body(chip_ref, p_ref, own_ref, w_ref, m_ref, v_ref, *rest):
        g_ref, d_ref, mo_ref, vo_ref = rest[n_after:]
        g = None
        for k in range(N_CHIP):
            term = jnp.where(chip_ref[0] == k, own_ref[...], p_ref[k]).astype(F32)
            g = term if g is None else g + term
        g_ref[...] = g
        d_ref[...], mo_ref[...], vo_ref[...] = _adamw(w_ref[...], g, m_ref[...], v_ref[...])

    blk = pl.BlockSpec((None, rows, c), lambda i, chip_ref: (0, i, 0))
    out = jax.ShapeDtypeStruct((1, r, c), F32)
    return pl.pallas_call(
        body, name=name,
        grid_spec=pltpu.PrefetchScalarGridSpec(
            num_scalar_prefetch=1, grid=(r // rows,),
            in_specs=[pl.BlockSpec((N_CHIP, rows, c), lambda i, chip_ref: (0, i, 0)),
                      pl.BlockSpec((None, rows, c), lambda i, chip_ref: (chip_ref[0], i, 0)), blk, blk, blk]
            + [ANY] * n_after,
            out_specs=[blk] * 4),
        out_shape=[out] * 4,
        compiler_params=_params("parallel"),
    )(chip, parts, sums, w, m, v, *after)


def _adam_gains(total, ws, ms, vs):
    n = len(ws)
    widths = [w.shape[1] for w in ws]

    def body(t_ref, *refs):
        w_refs, m_refs, v_refs, outs = refs[:n], refs[n:2 * n], refs[2 * n:3 * n], refs[3 * n:]
        off = 0
        for i in range(n):
            g = t_ref[:, off:off + widths[i]]
            off += widths[i]
            g_ref, d_ref, mo_ref, vo_ref = outs[4 * i:4 * i + 4]
            g_ref[...] = g
            d_ref[...], mo_ref[...], vo_ref[...] = _adamw(w_refs[i][...], g, m_refs[i][...], v_refs[i][...])

    out = pl.pallas_call(
        body, name="adam_gains",
        out_shape=[jax.ShapeDtypeStruct(w.shape, F32) for w in ws for _ in range(4)],
    )(total, *ws, *ms, *vs)
    return [tuple(out[4 * i:4 * i + 4]) for i in range(n)]


def _adam_taps(total, first_col, device, w, m, v):
    _, n_taps, cw = w.shape
    col_block = lambda t, dev: (0, first_col // cw + t * N_DEV + dev[0])
    tap = pl.BlockSpec((None, 1, cw), lambda t, dev: (t, 0, 0))

    def body(dev_ref, t_ref, w_ref, m_ref, v_ref, g_ref, d_ref, mo_ref, vo_ref):
        g = t_ref[...]
        g_ref[...] = g
        d_ref[...], mo_ref[...], vo_ref[...] = _adamw(w_ref[...], g, m_ref[...], v_ref[...])

    shape3 = (n_taps, 1, cw)
    out = pl.pallas_call(
        body, name="adam_taps",
        grid_spec=pltpu.PrefetchScalarGridSpec(
            num_scalar_prefetch=1, grid=(n_taps,),
            in_specs=[pl.BlockSpec((1, cw), col_block), tap, tap, tap], out_specs=[tap] * 4),
        out_shape=[jax.ShapeDtypeStruct(shape3, F32)] * 4,
    )(device, total, w.reshape(shape3), m.reshape(shape3), v.reshape(shape3))
    return tuple(o.reshape(w.shape) for o in out)


def kernel(x, pre_mix_g, w_in, conv_w, q_norm_g, w_uq, kv_norm_g, w_ukv, conv_out_g, attn_out_g, w_o, post_mix_g, pre_mlp_g, w_up, w_down, post_mlp_g, loss_target, m_pre_mix_g, m_w_in, m_conv_w, m_q_norm_g, m_w_uq, m_kv_norm_g, m_w_ukv, m_conv_out_g, m_attn_out_g, m_w_o, m_post_mix_g, m_pre_mlp_g, m_w_up, m_w_down, m_post_mlp_g, v_pre_mix_g, v_w_in, v_conv_w, v_q_norm_g, v_w_uq, v_kv_norm_g, v_w_ukv, v_conv_out_g, v_attn_out_g, v_w_o, v_post_mix_g, v_pre_mlp_g, v_w_up, v_w_down, v_post_mlp_g):
    me = 4 * lax.axis_index("x") + 2 * lax.axis_index("y") + lax.axis_index("c")
    core = lax.axis_index("c").astype(jnp.int32).reshape(1)
    chip = (2 * lax.axis_index("x") + lax.axis_index("y")).astype(jnp.int32).reshape(1)
    gains = (pre_mix_g, q_norm_g, kv_norm_g, conv_out_g, attn_out_g, post_mix_g, pre_mlp_g, post_mlp_g)
    gain_m = (m_pre_mix_g, m_q_norm_g, m_kv_norm_g, m_conv_out_g, m_attn_out_g, m_post_mix_g, m_pre_mlp_g, m_post_mlp_g)
    gain_v = (v_pre_mix_g, v_q_norm_g, v_kv_norm_g, v_conv_out_g, v_attn_out_g, v_post_mix_g, v_pre_mlp_g, v_post_mlp_g)
    names = ("w_in", "w_uq", "w_ukv", "w_o", "w_up", "w_down")
    big = dict(zip(names, (w_in, w_uq, w_ukv, w_o, w_up, w_down)))
    big_m = dict(zip(names, (m_w_in, m_w_uq, m_w_ukv, m_w_o, m_w_up, m_w_down)))
    big_v = dict(zip(names, (v_w_in, v_w_uq, v_w_ukv, v_w_o, v_w_up, v_w_down)))
    n_heads = attn_out_g.shape[1] // HEAD
    n_taps = conv_w.shape[1]

    gathered = ("w_in", "conv", "w_uq", "w_ukv", "w_o", "w_up", "w_down")
    gather_groups = ((0, 1), (2, 3, 4), (5,), (6,))
    taps = jnp.pad(conv_w[0], ((0, SUBLANE - n_taps), (0, 0)))
    sems_a, shards_a, lands_a, token_a = _gather_start("gather_start_first", [w_in[0].astype(BF16), taps], ((0, 1),))
    behind = token_a[0, 0]
    sems_b, shards_b, lands_b, token = _gather_start(
        "gather_start_rest", [(big[nm][0] + behind).astype(BF16) for nm in gathered[2:]], ((0, 1, 2), (3,), (4,)))
    sems1, shards, lands = sems_a + sems_b, [*shards_a, *shards_b], [*lands_a, *lands_b]

    cols = lambda a: jnp.concatenate([a[j] for j in range(N_DEV)], axis=1)
    rows = lambda a: a.reshape(N_DEV * a.shape[1], a.shape[2])
    ready = {
        "w_in": _join_col_shards,
        "conv": lambda a: cols(a)[:n_taps],
        "w_uq": lambda a: _permute_q_cols(cols(a), n_heads),
        "w_ukv": cols, "w_o": rows, "w_up": lambda a: a, "w_down": rows,
    }

    class Weights:
        def __init__(self):
            self.passed = {}

        def forward(self, group, after):
            idx = gather_groups[group]
            self.passed[group] = _gather_forward(f"gather_forward_{group}", [shards[i] for i in idx], [lands[i] for i in idx],
                                                 *sems1[group], after)
            return tuple(self.passed[group][1])

        def ready(self, group, after):
            sems2, mid = self.passed[group]
            full = _gather_wait(f"gather_wait_{group}", mid, *sems2, after)
            out = []
            for i, a in zip(gather_groups[group], full):
                a = lax.dynamic_update_index_in_dim(a, shards[i], me, 0)
                out.append(ready[gathered[i]](a))
            return out

    weights = Weights()

    col_blocks = lambda g: g.reshape(g.shape[0], N_DEV, g.shape[1] // N_DEV).transpose(1, 0, 2)
    row_blocks = lambda g: g.reshape(N_DEV, g.shape[0] // N_DEV, g.shape[1])
    in_shard = w_in.shape[2]
    grad_groups = (("w_down",), ("w_up",), ("w_o",), ("w_uq", "w_ukv"), ("w_in",))
    to_blocks = {
        "w_in": lambda g: _split_col_shards(g, N_DEV, in_shard),
        "w_uq": lambda g: col_blocks(_unpermute_q_cols(g, n_heads)),
        "w_ukv": col_blocks, "w_o": row_blocks, "w_up": lambda g: g, "w_down": row_blocks,
    }
    in_flight = []

    class Grads:
        def __init__(self):
            self.core = core
            self.away = {}

        def send_sums(self, group, sums):
            sems, sums, parts, tok = _chip_send_start(f"chip_send_start_{group}", list(sums))
            in_flight.append((sems, sums, parts))
            return (tok,)

        def full(self, group, arrays):
            nms = grad_groups[group]
            blocks = [to_blocks[nm](g) for nm, g in zip(nms, arrays)]
            received = _pair_exchange(f"pair_exchange_{group}", blocks)
            return self.send_sums(group, [_pair_sum(f"pair_sum_{nm}", g, r, core) for nm, g, r in zip(nms, blocks, received)])

        def send_away(self, group, half):
            sems, src, land, tok = _pair_send_start(f"pair_send_start_{group}", half)
            self.away[group] = (sems, src, land)
            return (tok,)

        def received(self, group, after):
            sems, src, land = self.away[group]
            return _pair_send_wait(f"pair_send_wait_{group}", sems, src, land, after)

    grad_x, small = _local_step(x[0], loss_target[0], gains, weights, Grads(), first_after=(token,))

    big_out = {}

    def update(tag, first, last, after):
        groups = in_flight[first:last]
        parts = _chip_send_wait("chip_send_wait_" + tag, groups, after)
        nms = [nm for grp in grad_groups[first:last] for nm in grp]
        sums = [a for _, s, _ in groups for a in s]
        for nm, p, s in zip(nms, parts, sums):
            big_out[nm] = _sum_adam("adam_" + nm, p, s, chip, big[nm], big_m[nm], big_v[nm], after=after)
            after = (big_out[nm][0],)
        return after

    after = update("early", 0, len(in_flight) - 1, (grad_x,))
    total = _small_all_reduce(small, after=after)
    update("late", len(in_flight) - 1, len(in_flight), (total,))
    big_out = [big_out[nm] for nm in names]

    gain_out = _adam_gains(total, gains, gain_m, gain_v)
    taps_out = _adam_taps(total, sum(g.shape[1] for g in gains), me.astype(jnp.int32).reshape(1), conv_w, m_conv_w, v_conv_w)
    loss = total[0, total.shape[1] - 1]

    order = (0, "w_in", "conv", 1, "w_uq", 2, "w_ukv", 3, 4, "w_o", 5, 6, "w_up", "w_down", 7)
    by_name = dict(zip(names, big_out))
    outs = [loss, grad_x[None]]
    for kind in range(4):
        for item in order:
            if item == "conv":
                outs.append(taps_out[kind])
            elif isinstance(item, int):
                outs.append(gain_out[item][kind])
            else:
                outs.append(by_name[item][kind])
    return tuple(outs)
```

```python
import math

import jax
import jax.numpy as jnp
from jax import lax
from jax.experimental import pallas as pl
from jax.experimental.pallas import tpu as pltpu

F32 = jnp.float32
BF16 = jnp.bfloat16

EPS = 1e-6
NEG_INF = -1e30
HEAD = 128
ROPE = 64
QK = HEAD + ROPE
CHUNK = 64
ROPE_THETA = 10000.0
ADAM_LR, ADAM_B1, ADAM_B2, ADAM_EPS, ADAM_WD, ADAM_STEP = 0.001, 0.9, 0.999, 1e-08, 0.01, 10

LANE = 128
SUBLANE = 8
VMEM_LIMIT_BYTES = 56 * 1024 * 1024

N_DEV = 8
N_CHIP = 4
MESH = pl.DeviceIdType.MESH


def _params(*sem):
    return pltpu.CompilerParams(dimension_semantics=sem, vmem_limit_bytes=VMEM_LIMIT_BYTES)


ANY = pl.BlockSpec(memory_space=pl.ANY)


def _call(body, *, in_specs, after=(), **kw):
    n_in, n_after = len(in_specs), len(after)

    def ordered(*refs):
        body(*refs[:n_in], *refs[n_in + n_after:])

    call = pl.pallas_call(ordered, in_specs=[*in_specs, *[ANY] * n_after], **kw)
    return lambda *operands: call(*operands, *after)


def _sublane_sum(v):
    r, w = v.shape
    return jnp.sum(v.reshape(r // SUBLANE, SUBLANE, w), axis=0)


def _rstd(x):
    return lax.rsqrt(jnp.mean(x * x, axis=-1, keepdims=True) + EPS)


def _rms_bwd(x, g, dy):
    r = _rstd(x)
    xh = x * r
    dxh = dy * g
    dx = r * (dxh - xh * jnp.mean(dxh * xh, axis=-1, keepdims=True))
    return dx, dy * xh


def _accumulate(ref, val, step):
    @pl.when(step == 0)
    def _():
        ref[...] = val

    @pl.when(step > 0)
    def _():
        ref[...] += val


NN = ((1,), (0,))
NT = ((1,), (1,))
TN = ((0,), (0,))


def _matmul(name, a, b, *, grid, a_spec, b_spec, out_shape, out_specs, contract, nk=1, acc_shape=None,
            extras=(), extra_specs=(), epilogue=None, after=()):
    multi = isinstance(out_shape, (tuple, list))
    out_shapes = tuple(out_shape) if multi else (out_shape,)
    n_out = len(out_shapes)
    n_extra = len(extras)

    def body(a_ref, b_ref, *rest):
        x_refs = rest[:n_extra]
        o_refs = rest[n_extra:n_extra + n_out]

        def emit(acc):
            vals = epilogue(acc, *[r[...] for r in x_refs]) if epilogue else (acc,)
            for r, v in zip(o_refs, vals):
                r[...] = v.astype(r.dtype)

        p = lax.dot_general(a_ref[...], b_ref[...], (contract, ((), ())), preferred_element_type=F32)
        if nk == 1:
            emit(p)
        else:
            acc_ref = rest[n_extra + n_out]
            k = pl.program_id(2)
            _accumulate(acc_ref, p, k)

            @pl.when(k == nk - 1)
            def _():
                emit(acc_ref[...])

    sem = ("parallel", "parallel") + (("arbitrary",) if nk > 1 else ())
    return _call(
        body, name=name, grid=grid, after=after,
        in_specs=[a_spec, b_spec, *extra_specs],
        out_specs=out_specs,
        out_shape=out_shape,
        scratch_shapes=[pltpu.VMEM(acc_shape, F32)] if nk > 1 else [],
        compiler_params=_params(*sem),
    )(a, b, *extras)


def _fit(n, tile):
    if n <= tile:
        return n
    t = tile - tile % LANE
    while n % t:
        t -= LANE
    return t


def _mm_nn(name, a, b, out_dtype, tm, tn, after=()):
    m, k = a.shape
    n = b.shape[1]
    tm, tn = _fit(m, tm), _fit(n, tn)
    return _matmul(name, a, b, grid=(m // tm, n // tn), after=after,
                   a_spec=pl.BlockSpec((tm, k), lambda i, j: (i, 0)),
                   b_spec=pl.BlockSpec((k, tn), lambda i, j: (0, j)),
                   out_shape=jax.ShapeDtypeStruct((m, n), out_dtype),
                   out_specs=pl.BlockSpec((tm, tn), lambda i, j: (i, j)), contract=NN)


def _mm_nt(name, a, b, out_dtype, tm, tn, after=()):
    m, k = a.shape
    n = b.shape[0]
    tm, tn = _fit(m, tm), _fit(n, tn)
    return _matmul(name, a, b, grid=(m // tm, n // tn), after=after,
                   a_spec=pl.BlockSpec((tm, k), lambda i, j: (i, 0)),
                   b_spec=pl.BlockSpec((tn, k), lambda i, j: (j, 0)),
                   out_shape=jax.ShapeDtypeStruct((m, n), out_dtype),
                   out_specs=pl.BlockSpec((tm, tn), lambda i, j: (i, j)), contract=NT)


def _mm_tn(name, a, b, out_dtype, tm, tn):
    s, m = a.shape
    n = b.shape[1]
    tm, tn = _fit(m, tm), _fit(n, tn)
    return _matmul(name, a, b, grid=(m // tm, n // tn),
                   a_spec=pl.BlockSpec((s, tm), lambda i, j: (0, i)),
                   b_spec=pl.BlockSpec((s, tn), lambda i, j: (0, j)),
                   out_shape=jax.ShapeDtypeStruct((m, n), out_dtype),
                   out_specs=pl.BlockSpec((tm, tn), lambda i, j: (i, j)), contract=TN)


ROWS = 256


def _row_spec(rows, width):
    return pl.BlockSpec((rows, width), lambda i: (i, 0))


def _fixed_spec(rows, width):
    return pl.BlockSpec((rows, width), lambda i: (0, 0))


def _column_pieces(rows, start, width):
    piece = math.gcd(start, width)
    assert piece % LANE == 0
    return [pl.BlockSpec((rows, piece), lambda i, b=start // piece + p: (i, b)) for p in range(width // piece)]


def _rms_fwd(name, x, g, cols=None, after=()):
    s = x.shape[0]
    start, w = cols or (0, x.shape[1])
    rows = min(ROWS, s)
    pieces = _column_pieces(rows, start, w) if cols else [_row_spec(rows, w)]
    n = len(pieces)

    def body(*refs):
        g_ref, o_ref = refs[n:]
        xv = refs[0][...] if n == 1 else jnp.concatenate([r[...] for r in refs[:n]], axis=1)
        o_ref[...] = (xv * _rstd(xv) * g_ref[...]).astype(o_ref.dtype)

    return _call(
        body, name=name, grid=(s // rows,), after=after,
        in_specs=[*pieces, _fixed_spec(1, w)],
        out_specs=_row_spec(rows, w),
        out_shape=jax.ShapeDtypeStruct((s, w), BF16),
        compiler_params=_params("parallel"),
    )(*[x] * n, g)


def _rms_bwd_call(name, x, g, dy, out_dtype, cols=None, after=()):
    s = x.shape[0]
    start, w = cols or (0, x.shape[1])
    rows = min(ROWS, s)
    pieces = _column_pieces(rows, start, w) if cols else [_row_spec(rows, w)]
    n = len(pieces)

    def body(*refs):
        g_ref, dy_ref, dx_ref, dg_ref = refs[n:]
        xv = refs[0][...] if n == 1 else jnp.concatenate([r[...] for r in refs[:n]], axis=1)
        dx, dgc = _rms_bwd(xv, g_ref[...], dy_ref[...].astype(F32))
        dx_ref[...] = dx.astype(dx_ref.dtype)
        _accumulate(dg_ref, _sublane_sum(dgc), pl.program_id(0))

    return _call(
        body, name=name, grid=(s // rows,), after=after,
        in_specs=[*pieces, _fixed_spec(1, w), _row_spec(rows, w)],
        out_specs=[_row_spec(rows, w), _fixed_spec(SUBLANE, w)],
        out_shape=[jax.ShapeDtypeStruct((s, w), out_dtype), jax.ShapeDtypeStruct((SUBLANE, w), F32)],
        compiler_params=_params("arbitrary"),
    )(*[x] * n, g, dy)


def _mid_fwd(x, y, g_post, g_pre, after=()):
    s, w = x.shape
    rows = min(ROWS, s)

    def body(x_ref, y_ref, gp_ref, gq_ref, x2_ref, h2_ref):
        yv = y_ref[...]
        x2 = x_ref[...] + yv * _rstd(yv) * gp_ref[...]
        x2_ref[...] = x2
        h2_ref[...] = (x2 * _rstd(x2) * gq_ref[...]).astype(h2_ref.dtype)

    return _call(
        body, name="mid_fwd", grid=(s // rows,), after=after,
        in_specs=[_row_spec(rows, w), _row_spec(rows, w), _fixed_spec(1, w), _fixed_spec(1, w)],
        out_specs=[_row_spec(rows, w), _row_spec(rows, w)],
        out_shape=[jax.ShapeDtypeStruct((s, w), F32), jax.ShapeDtypeStruct((s, w), BF16)],
        compiler_params=_params("parallel"),
    )(x, y, g_post, g_pre)


def _head(m, x2, tgt, g):
    s, w = m.shape
    rows = min(ROWS, s)

    def body(m_ref, x2_ref, t_ref, g_ref, dout_ref, dm_ref, dg_ref, loss_ref):
        mv = m_ref[...]
        gv = g_ref[...]
        out = x2_ref[...] + mv * _rstd(mv) * gv
        err = out - t_ref[...]
        dout = err * (1.0 / w)
        dout_ref[...] = dout
        dm, dgc = _rms_bwd(mv, gv, dout)
        dm_ref[...] = dm.astype(dm_ref.dtype)
        sq = err * err
        lanes = sq[:, 0:LANE]
        for j in range(1, w // LANE):
            lanes = lanes + sq[:, j * LANE:(j + 1) * LANE]
        step = pl.program_id(0)
        _accumulate(dg_ref, _sublane_sum(dgc), step)
        _accumulate(loss_ref, _sublane_sum(lanes) * (0.5 / w), step)

    return pl.pallas_call(
        body, name="head", grid=(s // rows,),
        in_specs=[_row_spec(rows, w), _row_spec(rows, w), _row_spec(rows, w), _fixed_spec(1, w)],
        out_specs=[_row_spec(rows, w), _row_spec(rows, w), _fixed_spec(SUBLANE, w), _fixed_spec(SUBLANE, LANE)],
        out_shape=[jax.ShapeDtypeStruct((s, w), F32), jax.ShapeDtypeStruct((s, w), BF16),
                   jax.ShapeDtypeStruct((SUBLANE, w), F32), jax.ShapeDtypeStruct((SUBLANE, LANE), F32)],
        compiler_params=_params("arbitrary"),
    )(m, x2, tgt, g)


def _mid_bwd(x2, y, d_out, d_h2, g_pre, g_post, after=()):
    s, w = x2.shape
    rows = min(ROWS, s)

    def body(x2_ref, y_ref, dout_ref, dh2_ref, gq_ref, gp_ref, dx2_ref, dy_ref, dgq_ref, dgp_ref):
        dx, dgq = _rms_bwd(x2_ref[...], gq_ref[...], dh2_ref[...])
        dx2 = dout_ref[...] + dx
        dx2_ref[...] = dx2
        dy, dgp = _rms_bwd(y_ref[...], gp_ref[...], dx2)
        dy_ref[...] = dy.astype(dy_ref.dtype)
        step = pl.program_id(0)
        _accumulate(dgq_ref, _sublane_sum(dgq), step)
        _accumulate(dgp_ref, _sublane_sum(dgp), step)

    return _call(
        body, name="mid_bwd", grid=(s // rows,), after=after,
        in_specs=[_row_spec(rows, w)] * 4 + [_fixed_spec(1, w)] * 2,
        out_specs=[_row_spec(rows, w), _row_spec(rows, w), _fixed_spec(SUBLANE, w), _fixed_spec(SUBLANE, w)],
        out_shape=[jax.ShapeDtypeStruct((s, w), F32), jax.ShapeDtypeStruct((s, w), BF16),
                   jax.ShapeDtypeStruct((SUBLANE, w), F32), jax.ShapeDtypeStruct((SUBLANE, w), F32)],
        compiler_params=_params("arbitrary"),
    )(x2, y, d_out, d_h2, g_pre, g_post)


def _first_bwd(x, g, d_h1, d_x2, after=()):
    s, w = x.shape
    rows = min(ROWS, s)

    def body(x_ref, g_ref, dh_ref, dx2_ref, dx_ref, dg_ref):
        dx, dgc = _rms_bwd(x_ref[...], g_ref[...], dh_ref[...])
        dx_ref[...] = dx2_ref[...] + dx
        _accumulate(dg_ref, _sublane_sum(dgc), pl.program_id(0))

    return _call(
        body, name="first_bwd", grid=(s // rows,), after=after,
        in_specs=[_row_spec(rows, w), _fixed_spec(1, w), _row_spec(rows, w), _row_spec(rows, w)],
        out_specs=[_row_spec(rows, w), _fixed_spec(SUBLANE, w)],
        out_shape=[jax.ShapeDtypeStruct((s, w), F32), jax.ShapeDtypeStruct((SUBLANE, w), F32)],
        compiler_params=_params("arbitrary"),
    )(x, g, d_h1, d_x2)


def _shift_down(v, k):
    t = lax.broadcasted_iota(jnp.int32, v.shape, 0)
    return jnp.where(t >= k, pltpu.roll(v, k, 0), 0.0)


def _shift_up(v, k):
    n = v.shape[0]
    t = lax.broadcasted_iota(jnp.int32, v.shape, 0)
    return jnp.where(t < n - k, pltpu.roll(v, n - k, 0), 0.0)


def _conv_core(u, b, c, w):
    z = c * u
    conv = w[0:1, :] * _shift_down(z, 2) + w[1:2, :] * _shift_down(z, 1) + w[2:3, :] * z
    return z, conv, b * conv


def _conv_fwd(proj, conv_w, g, n_groups):
    s = proj.shape[0]

    def body(u_ref, b_ref, c_ref, w_ref, g_ref, o_ref):
        _, _, yr = _conv_core(u_ref[...], b_ref[...], c_ref[...], w_ref[...])
        o_ref[...] = (yr * _rstd(yr) * g_ref[...]).astype(o_ref.dtype)

    col = lambda k: pl.BlockSpec((s, HEAD), lambda i: (0, k * n_groups + i))
    return pl.pallas_call(
        body, name="conv_fwd", grid=(n_groups,),
        in_specs=[col(0), col(1), col(2), pl.BlockSpec((3, HEAD), lambda i: (0, i)), pl.BlockSpec((1, HEAD), lambda i: (0, i))],
        out_specs=pl.BlockSpec((s, HEAD), lambda i: (0, i)),
        out_shape=jax.ShapeDtypeStruct((s, n_groups * HEAD), BF16),
        compiler_params=_params("parallel"),
    )(proj, proj, proj, conv_w, g)


def _conv_bwd(proj, d_mix, conv_w, g, n_groups):
    s = proj.shape[0]
    width = n_groups * HEAD

    def body(u_ref, b_ref, c_ref, dy_ref, w_ref, g_ref, du_ref, db_ref, dc_ref, dg_ref, dw_ref):
        u, b, c, w = u_ref[...], b_ref[...], c_ref[...], w_ref[...]
        z, conv, yr = _conv_core(u, b, c, w)
        dyr, dgc = _rms_bwd(yr, g_ref[...], dy_ref[...])
        dconv = dyr * b
        db_ref[...] = (dyr * conv).astype(db_ref.dtype)
        dz = w[2:3, :] * dconv + w[1:2, :] * _shift_up(dconv, 1) + w[0:1, :] * _shift_up(dconv, 2)
        dc_ref[...] = (dz * u).astype(dc_ref.dtype)
        du_ref[...] = (dz * c).astype(du_ref.dtype)
        dg_ref[...] = _sublane_sum(dgc)
        dw_ref[0] = _sublane_sum(dconv * _shift_down(z, 2))
        dw_ref[1] = _sublane_sum(dconv * _shift_down(z, 1))
        dw_ref[2] = _sublane_sum(dconv * z)

    col = lambda k: pl.BlockSpec((s, HEAD), lambda i: (0, k * n_groups + i))
    grp = pl.BlockSpec((s, HEAD), lambda i: (0, i))
    return pl.pallas_call(
        body, name="conv_bwd", grid=(n_groups,),
        in_specs=[col(0), col(1), col(2), grp, pl.BlockSpec((3, HEAD), lambda i: (0, i)), pl.BlockSpec((1, HEAD), lambda i: (0, i))],
        out_specs=[grp, grp, grp, pl.BlockSpec((SUBLANE, HEAD), lambda i: (0, i)),
                   pl.BlockSpec((3, SUBLANE, HEAD), lambda i: (0, 0, i))],
        out_shape=[jax.ShapeDtypeStruct((s, width), BF16)] * 3
        + [jax.ShapeDtypeStruct((SUBLANE, width), F32), jax.ShapeDtypeStruct((3, SUBLANE, width), F32)],
        compiler_params=_params("parallel"),
    )(proj, proj, proj, d_mix, conv_w, g)


def _rope_tables(s, n_heads):
    pos = jnp.arange(s, dtype=F32)
    inv_freq = jnp.power(ROPE_THETA, -jnp.arange(0, ROPE, 2, dtype=F32) / ROPE)
    ang = pos[:, None] * inv_freq[None, :]
    cos, sin = jnp.cos(ang), jnp.sin(ang)
    cs = jnp.concatenate([cos, cos], axis=1)
    sn = jnp.concatenate([-sin, sin], axis=1)
    pad = jnp.zeros((s, LANE - ROPE), F32)
    return (jnp.tile(cs, (1, n_heads)), jnp.tile(sn, (1, n_heads)),
            jnp.concatenate([cs, pad], axis=1), jnp.concatenate([sn, pad], axis=1))


def _swap_halves(v):
    w = v.shape[1]
    lane = lax.broadcasted_iota(jnp.int32, v.shape, 1)
    first = (lane % ROPE) < (ROPE // 2)
    return jnp.where(first, pltpu.roll(v, w - ROPE // 2, 1), pltpu.roll(v, ROPE // 2, 1))


def _pack_heads(q, kv, proj, kr_col, tables, n_heads, after=()):
    s = q.shape[0]
    rows = min(ROWS, s)
    cq, sq, ck, sk = tables
    wq = n_heads * ROPE

    def body(q_ref, kv_ref, kr_ref, cq_ref, sq_ref, ck_ref, sk_ref, qo_ref, ko_ref, vo_ref):
        qr = q_ref[:, n_heads * HEAD:]
        qr = qr * cq_ref[...] + _swap_halves(qr) * sq_ref[...]
        krv = kr_ref[...]
        krv = krv * ck_ref[...] + _swap_halves(krv) * sk_ref[...]
        for h in range(n_heads):
            qo_ref[h] = jnp.concatenate([q_ref[:, h * HEAD:(h + 1) * HEAD], qr[:, h * ROPE:(h + 1) * ROPE]], axis=1).astype(BF16)
            ko_ref[h] = jnp.concatenate([kv_ref[:, 2 * h * HEAD:(2 * h + 1) * HEAD], krv[:, :ROPE]], axis=1).astype(BF16)
            vo_ref[h] = kv_ref[:, (2 * h + 1) * HEAD:(2 * h + 2) * HEAD].astype(BF16)

    hs = lambda w: pl.BlockSpec((n_heads, rows, w), lambda i: (0, i, 0))
    return _call(
        body, name="pack_heads", grid=(s // rows,), after=after,
        in_specs=[_row_spec(rows, q.shape[1]), _row_spec(rows, kv.shape[1]), pl.BlockSpec((rows, LANE), lambda i: (i, kr_col // LANE)),
                  _row_spec(rows, wq), _row_spec(rows, wq), _row_spec(rows, LANE), _row_spec(rows, LANE)],
        out_specs=[hs(QK), hs(QK), hs(HEAD)],
        out_shape=[jax.ShapeDtypeStruct((n_heads, s, QK), BF16), jax.ShapeDtypeStruct((n_heads, s, QK), BF16),
                   jax.ShapeDtypeStruct((n_heads, s, HEAD), BF16)],
        compiler_params=_params("parallel"),
    )(q, kv, proj, cq, sq, ck, sk)


def _unpack_heads(dq, dk, dv, tables, n_heads):
    s = dq.shape[1]
    rows = min(ROWS, s)
    cq, sq, ck, sk = tables
    wq = n_heads * ROPE

    def body(dq_ref, dk_ref, dv_ref, cq_ref, sq_ref, ck_ref, sk_ref, qo_ref, kvo_ref, kro_ref):
        dqr = jnp.concatenate([dq_ref[h][:, HEAD:] for h in range(n_heads)], axis=1)
        dqr = dqr * cq_ref[...] - _swap_halves(dqr) * sq_ref[...]
        dkr = dk_ref[0][:, HEAD:]
        for h in range(1, n_heads):
            dkr = dkr + dk_ref[h][:, HEAD:]
        dkr = jnp.concatenate([dkr, jnp.zeros((rows, LANE - ROPE), F32)], axis=1)
        dkr = dkr * ck_ref[...] - _swap_halves(dkr) * sk_ref[...]
        kro_ref[...] = dkr.astype(kro_ref.dtype)
        qo_ref[:, n_heads * HEAD:] = dqr.astype(qo_ref.dtype)
        for h in range(n_heads):
            qo_ref[:, h * HEAD:(h + 1) * HEAD] = dq_ref[h][:, :HEAD].astype(qo_ref.dtype)
            kvo_ref[:, 2 * h * HEAD:(2 * h + 1) * HEAD] = dk_ref[h][:, :HEAD].astype(kvo_ref.dtype)
            kvo_ref[:, (2 * h + 1) * HEAD:(2 * h + 2) * HEAD] = dv_ref[h].astype(kvo_ref.dtype)

    hs = lambda w: pl.BlockSpec((n_heads, rows, w), lambda i: (0, i, 0))
    return pl.pallas_call(
        body, name="unpack_heads", grid=(s // rows,),
        in_specs=[hs(QK), hs(QK), hs(HEAD), _row_spec(rows, wq), _row_spec(rows, wq), _row_spec(rows, LANE), _row_spec(rows, LANE)],
        out_specs=[_row_spec(rows, n_heads * QK), _row_spec(rows, 2 * n_heads * HEAD), _row_spec(rows, LANE)],
        out_shape=[jax.ShapeDtypeStruct((s, n_heads * QK), BF16), jax.ShapeDtypeStruct((s, 2 * n_heads * HEAD), BF16),
                   jax.ShapeDtypeStruct((s, LANE), BF16)],
        compiler_params=_params("parallel"),
    )(dq, dk, dv, cq, sq, ck, sk)


TQ = 256


LOG2_E = 1.4426950408889634


def _softmax_parts(q, k):
    tq, n_keys = q.shape[0], k.shape[0]
    sc = lax.dot_general(q, k, (NT, ((), ())), preferred_element_type=F32) * (QK ** -0.5 * LOG2_E)
    row = lax.broadcasted_iota(jnp.int32, (tq, tq), 0)
    col = lax.broadcasted_iota(jnp.int32, (tq, tq), 1)
    own = jnp.where(col // CHUNK <= row // CHUNK, sc[:, n_keys - tq:], NEG_INF)
    sc = own if n_keys == tq else jnp.concatenate([sc[:, :n_keys - tq], own], axis=1)
    e = jnp.exp2(sc - jnp.max(sc, axis=-1, keepdims=True))
    return e, 1.0 / jnp.sum(e, axis=-1, keepdims=True)


def _attn_fwd(q, k, v, g):
    n_heads, s, _ = q.shape
    tq = min(TQ, s)
    assert tq % CHUNK == 0 and s % tq == 0

    def body(q_ref, k_ref, v_ref, g_ref, o_ref, y_ref):
        for c in range(s // tq):
            rows, n_keys = pl.ds(c * tq, tq), (c + 1) * tq
            e, inv = _softmax_parts(q_ref[rows, :], k_ref[0:n_keys, :])
            o = jnp.dot(e.astype(BF16), v_ref[0:n_keys, :], preferred_element_type=F32) * inv
            o_ref[rows, :] = o
            y_ref[rows, :] = (o * _rstd(o) * g_ref[...]).astype(y_ref.dtype)

    head = lambda w: pl.BlockSpec((None, s, w), lambda h: (h, 0, 0))
    return pl.pallas_call(
        body, name="attn_fwd", grid=(n_heads,),
        in_specs=[head(QK), head(QK), head(HEAD), pl.BlockSpec((1, HEAD), lambda h: (0, h))],
        out_specs=[head(HEAD), pl.BlockSpec((s, HEAD), lambda h: (0, h))],
        out_shape=[jax.ShapeDtypeStruct((n_heads, s, HEAD), F32), jax.ShapeDtypeStruct((s, n_heads * HEAD), BF16)],
        compiler_params=_params("parallel"),
    )(q, k, v, g)


def _attn_bwd(q, k, v, o, d_mix, g, col0, after=()):
    n_heads, s, _ = q.shape
    tq = min(TQ, s)

    def body(q_ref, k_ref, v_ref, o_ref, dy_ref, g_ref, dq_ref, dk_ref, dv_ref, dg_ref):
        dg = None
        for c in reversed(range(s // tq)):
            rows, n_keys = pl.ds(c * tq, tq), (c + 1) * tq
            qv, kv_, vv = q_ref[rows, :], k_ref[0:n_keys, :], v_ref[0:n_keys, :]
            do, dgc = _rms_bwd(o_ref[rows, :], g_ref[...], dy_ref[rows, :])
            do = do.astype(BF16)
            dg = _sublane_sum(dgc) if dg is None else dg + _sublane_sum(dgc)
            e, inv = _softmax_parts(qv, kv_)
            p = e * inv
            dp = lax.dot_general(do, vv, (NT, ((), ())), preferred_element_type=F32)
            ds = (p * (dp - jnp.sum(p * dp, axis=-1, keepdims=True)) * (QK ** -0.5)).astype(BF16)
            dq_ref[rows, :] = jnp.dot(ds, kv_, preferred_element_type=F32)
            dk = lax.dot_general(ds, qv, (TN, ((), ())), preferred_element_type=F32)
            dv = lax.dot_general(p.astype(BF16), do, (TN, ((), ())), preferred_element_type=F32)
            if n_keys == s:
                dk_ref[...] = dk
                dv_ref[...] = dv
            else:
                dk_ref[0:n_keys, :] += dk
                dv_ref[0:n_keys, :] += dv
        dg_ref[...] = dg

    c0 = col0 // HEAD
    head = lambda w: pl.BlockSpec((None, s, w), lambda h: (h, 0, 0))
    return _call(
        body, name="attn_bwd", grid=(n_heads,), after=after,
        in_specs=[head(QK), head(QK), head(HEAD), head(HEAD), pl.BlockSpec((s, HEAD), lambda h: (0, c0 + h)),
                  pl.BlockSpec((1, HEAD), lambda h: (0, h))],
        out_specs=[head(QK), head(QK), head(HEAD), pl.BlockSpec((SUBLANE, HEAD), lambda h: (0, h))],
        out_shape=[jax.ShapeDtypeStruct((n_heads, s, QK), F32), jax.ShapeDtypeStruct((n_heads, s, QK), F32),
                   jax.ShapeDtypeStruct((n_heads, s, HEAD), F32), jax.ShapeDtypeStruct((SUBLANE, n_heads * HEAD), F32)],
        compiler_params=_params("parallel"),
    )(q, k, v, o, d_mix, g)


TILE_M = 1024
TILE_N = 1024


def _up_fwd(h2, w_up):
    s, d = h2.shape
    nb, _, fb = w_up.shape
    tm = min(TILE_M,s)

    def epilogue(acc):
        r = jnp.maximum(acc, 0.0)
        return r * r, r

    blk = pl.BlockSpec((tm, fb), lambda i, j: (i, j))
    return _matmul("up_fwd", h2, w_up, grid=(s // tm, nb),
                   a_spec=pl.BlockSpec((tm, d), lambda i, j: (i, 0)),
                   b_spec=pl.BlockSpec((None, d, fb), lambda i, j: (j, 0, 0)),
                   out_shape=[jax.ShapeDtypeStruct((s, nb * fb), BF16)] * 2, out_specs=[blk, blk],
                   contract=NN, epilogue=epilogue)


def _down_fwd(a, w_down):
    s, f = a.shape
    d = w_down.shape[1]
    tm, tn, tk = min(TILE_M,s), min(TILE_N,d), 2048
    nk = f // tk
    return _matmul("down_fwd", a, w_down, grid=(s // tm, d // tn, nk),
                   a_spec=pl.BlockSpec((tm, tk), lambda i, j, k: (i, k)),
                   b_spec=pl.BlockSpec((tk, tn), lambda i, j, k: (k, j)),
                   out_shape=jax.ShapeDtypeStruct((s, d), F32),
                   out_specs=pl.BlockSpec((tm, tn), lambda i, j, k: (i, j)),
                   contract=NN, nk=nk, acc_shape=(tm, tn))


def _down_bwd_act(d_m, w_down, r, after=()):
    s, d = d_m.shape
    f = w_down.shape[0]
    tm, tn = min(TILE_M,s), min(TILE_N,f)
    blk = pl.BlockSpec((tm, tn), lambda i, j: (i, j))
    return _matmul("down_bwd_act", d_m, w_down, grid=(s // tm, f // tn), after=after,
                   a_spec=pl.BlockSpec((tm, d), lambda i, j: (i, 0)),
                   b_spec=pl.BlockSpec((tn, d), lambda i, j: (j, 0)),
                   out_shape=jax.ShapeDtypeStruct((s, f), BF16), out_specs=blk, contract=NT,
                   extras=(r,), extra_specs=(blk,),
                   epilogue=lambda acc, rv: (acc * (2.0 * rv.astype(F32)),))


def _up_bwd_act(d_up, w_up, after=()):
    s, _ = d_up.shape
    nb, d, fb = w_up.shape
    tm, tn = min(TILE_M,s), min(TILE_N,d)
    return _matmul("up_bwd_act", d_up, w_up, grid=(s // tm, d // tn, nb), after=after,
                   a_spec=pl.BlockSpec((tm, fb), lambda i, j, k: (i, k)),
                   b_spec=pl.BlockSpec((None, tn, fb), lambda i, j, k: (k, j, 0)),
                   out_shape=jax.ShapeDtypeStruct((s, d), F32),
                   out_specs=pl.BlockSpec((tm, tn), lambda i, j, k: (i, j)),
                   contract=NT, nk=nb, acc_shape=(tm, tn))


def _half_grad(name, a, b, core, home, received, after, *, grid, a_block, a_map, b_block, b_map, o_block, o_map, out_shape):
    n_after = len(after)
    pick = (lambda ref: ref[0]) if home else (lambda ref: 1 - ref[0])

    def body(core_ref, a_ref, b_ref, *rest):
        acc = lax.dot_general(a_ref[...], b_ref[...], (TN, ((), ())), preferred_element_type=F32)
        if received is not None:
            acc = acc + rest[0][...].astype(F32)
        rest[-1][...] = acc.astype(rest[-1].dtype)

    wrap = lambda fn: (lambda i, j, core_ref: fn(i, j, pick(core_ref)))
    o_spec = pl.BlockSpec(o_block, wrap(o_map))
    extra = [] if received is None else [o_spec]
    operands = [] if received is None else [received]
    return pl.pallas_call(
        body, name=name,
        grid_spec=pltpu.PrefetchScalarGridSpec(
            num_scalar_prefetch=1, grid=grid,
            in_specs=[pl.BlockSpec(a_block, wrap(a_map)), pl.BlockSpec(b_block, wrap(b_map))] + extra + [ANY] * n_after,
            out_specs=o_spec),
        out_shape=out_shape,
        compiler_params=_params("parallel", "parallel"),
    )(core, a, b, *operands, *after)


def _down_half_grad(name, a, d_m, core, home, received=None, after=()):
    s, f = a.shape
    d = d_m.shape[1]
    r = f // N_DEV
    tn = min(TILE_N, d)
    return _half_grad(name, a, d_m, core, home, received, after, grid=(N_CHIP, d // tn),
                      a_block=(s, r), a_map=lambda k, j, p: (0, 2 * k + p),
                      b_block=(s, tn), b_map=lambda k, j, p: (0, j),
                      o_block=(None, r, tn), o_map=lambda k, j, p: (k, 0, j),
                      out_shape=jax.ShapeDtypeStruct((N_CHIP, r, d), BF16))


def _up_half_grad(name, h2, d_up, core, home, received=None, after=()):
    s, d = h2.shape
    fb = d_up.shape[1] // N_DEV
    tm = min(TILE_M, d)
    return _half_grad(name, h2, d_up, core, home, received, after, grid=(d // tm, N_CHIP),
                      a_block=(s, tm), a_map=lambda i, k, p: (0, i),
                      b_block=(s, fb), b_map=lambda i, k, p: (0, 2 * k + p),
                      o_block=(None, tm, fb), o_map=lambda i, k, p: (k, i, 0),
                      out_shape=jax.ShapeDtypeStruct((N_CHIP, d, fb), BF16))


def _in_pad(in_width):
    return -(-in_width // LANE) * LANE


def _join_col_shards(blocks):
    n, r, w = blocks.shape
    rows = min(ROWS, r)
    width = _in_pad(n * w)

    def body(x_ref, o_ref):
        tail = [jnp.zeros((rows, width - n * w), o_ref.dtype)] if width > n * w else []
        o_ref[...] = jnp.concatenate([x_ref[j] for j in range(n)] + tail, axis=1)

    return pl.pallas_call(
        body, name="join_col_shards", grid=(r // rows,),
        in_specs=[pl.BlockSpec((n, rows, w), lambda i: (0, i, 0))], out_specs=_row_spec(rows, width),
        out_shape=jax.ShapeDtypeStruct((r, width), blocks.dtype),
        compiler_params=_params("parallel"),
    )(blocks)


def _permute_q_cols(w_uq, n_heads):
    r = w_uq.shape[0]
    w3 = w_uq.reshape(r, n_heads, QK)
    return jnp.concatenate([w3[:, :, :HEAD].reshape(r, n_heads * HEAD), w3[:, :, HEAD:].reshape(r, n_heads * ROPE)], axis=1)


def _unpermute_q_rows(wt, n_heads):
    r = wt.shape[1]
    nope = wt[:n_heads * HEAD].reshape(n_heads, HEAD, r)
    rope = wt[n_heads * HEAD:].reshape(n_heads, ROPE, r)
    return jnp.concatenate([nope, rope], axis=1).reshape(n_heads * QK, r)


def _local_step(x, tgt, gains, weights, grads, first_after=()):
    pre_mix_g, q_norm_g, kv_norm_g, conv_out_g, attn_out_g, post_mix_g, pre_mlp_g, post_mlp_g = gains
    s, d = x.shape
    conv_width = conv_out_g.shape[1]
    n_groups = conv_width // HEAD
    r_q, r_kv = q_norm_g.shape[1], kv_norm_g.shape[1]
    n_heads = attn_out_g.shape[1] // HEAD
    c_q0 = 3 * conv_width
    c_kv0 = c_q0 + r_q
    c_kr0 = c_kv0 + r_kv
    in_pad = _in_pad(c_kr0 + ROPE)
    tn_in = in_pad // 5 if in_pad % (5 * LANE) == 0 else LANE
    tables = _rope_tables(s, n_heads)

    h1 = _rms_fwd("pre_mix_norm", x, pre_mix_g, after=first_after)
    weights.forward(0, (h1, *tables))
    w_in_p, conv_w = weights.ready(0, ())
    proj = _mm_nn("in_proj", h1, w_in_p, F32, TILE_M,tn_in)
    y_conv = _conv_fwd(proj, conv_w, conv_out_g, n_groups)
    qn = _rms_fwd("q_norm", proj, q_norm_g, cols=(c_q0, r_q))
    kvn = _rms_fwd("kv_norm", proj, kv_norm_g, cols=(c_kv0, r_kv))
    weights.forward(1, (y_conv, qn, kvn))
    w_uq_p, w_ukv, w_o = weights.ready(1, ())
    q = _mm_nn("q_up", qn, w_uq_p, F32, TILE_M, TILE_N)
    kv = _mm_nn("kv_up", kvn, w_ukv, F32, TILE_M, TILE_N)
    qh, kh, vh = _pack_heads(q, kv, proj, c_kr0, tables, n_heads)
    o, y_attn = _attn_fwd(qh, kh, vh, attn_out_g)
    mix = jnp.concatenate([y_conv, y_attn], axis=1)
    y = _mm_nn("out_proj", mix, w_o, F32, TILE_M, TILE_N, after=weights.forward(2, (y_attn,)))
    x2, h2 = _mid_fwd(x, y, post_mix_g, pre_mlp_g)
    (w_up,) = weights.ready(2, (h2,))
    a, r = _up_fwd(h2, w_up)
    weights.forward(3, (a,))
    (w_down,) = weights.ready(3, ())
    m = _down_fwd(a, w_down)

    d_out, d_m, dg_post_mlp, loss_part = _head(m, x2, tgt, post_mlp_g)
    core = grads.core
    away = _down_half_grad("down_bwd_w_away", a, d_m, core, home=False)
    d_up = _down_bwd_act(d_m, w_down, r, after=grads.send_away(0, away))
    sums = _down_half_grad("down_bwd_w_home", a, d_m, core, home=True, received=grads.received(0, (d_up,)))
    away = _up_half_grad("up_bwd_w_away", h2, d_up, core, home=False, after=grads.send_sums(0, (sums,)))
    d_h2 = _up_bwd_act(d_up, w_up, after=grads.send_away(1, away))
    sums = _up_half_grad("up_bwd_w_home", h2, d_up, core, home=True, received=grads.received(1, (d_h2,)))
    d_x2, d_y, dg_pre_mlp, dg_post_mix = _mid_bwd(x2, y, d_out, d_h2, pre_mlp_g, post_mix_g, after=grads.send_sums(1, (sums,)))
    d_mix = _mm_nt("out_proj_bwd_act", d_y, w_o, F32, TILE_M, TILE_N)
    gw_o = _mm_tn("out_proj_bwd_w", mix, d_y, BF16, TILE_M, TILE_N)
    dqh, dkh, dvh, dg_attn = _attn_bwd(qh, kh, vh, o, d_mix, attn_out_g, conv_width, after=grads.full(2, (gw_o,)))
    d_q, d_kv, d_kr = _unpack_heads(dqh, dkh, dvh, tables, n_heads)
    d_qn = _mm_nt("q_up_bwd_act", d_q, w_uq_p, F32, TILE_M, TILE_N)
    d_kvn = _mm_nt("kv_up_bwd_act", d_kv, w_ukv, F32, TILE_M, TILE_N)
    gw_uq_t = _mm_tn("q_up_bwd_w", d_q, qn, F32, TILE_M, TILE_N)
    gw_ukv = _mm_tn("kv_up_bwd_w", kvn, d_kv, BF16, TILE_M, TILE_N)
    d_cq, dg_q = _rms_bwd_call("q_norm_bwd", proj, q_norm_g, d_qn, BF16, cols=(c_q0, r_q), after=grads.full(3, (gw_uq_t, gw_ukv)))
    d_ckv, dg_kv = _rms_bwd_call("kv_norm_bwd", proj, kv_norm_g, d_kvn, BF16, cols=(c_kv0, r_kv))
    d_u, d_b, d_c, dg_conv, dw_conv = _conv_bwd(proj, d_mix, conv_w, conv_out_g, n_groups)
    d_proj = jnp.concatenate([d_u, d_b, d_c, d_cq, d_ckv, d_kr[:, :in_pad - c_kr0]], axis=1)
    gw_in_t = _mm_tn("in_proj_bwd_w", d_proj, h1, F32, tn_in, TILE_N)
    d_h1 = _mm_nt("in_proj_bwd_act", d_proj, w_in_p, F32, TILE_M, 512, after=grads.full(4, (gw_in_t,)))
    grad_x, dg_pre_mix = _first_bwd(x, pre_mix_g, d_h1, d_x2)

    small = [dg_pre_mix, dg_q, dg_kv, dg_conv, dg_attn, dg_post_mix, dg_pre_mlp, dg_post_mlp,
             dw_conv[0], dw_conv[1], dw_conv[2], loss_part]
    return grad_x, jnp.concatenate(small, axis=1)


HBM = pl.BlockSpec(memory_space=pltpu.HBM)
SEM = pl.BlockSpec(memory_space=pltpu.SEMAPHORE)
IN_VMEM = pl.BlockSpec(memory_space=pltpu.VMEM)
SPLIT = pltpu.CompilerParams(has_side_effects=pltpu.SideEffectType.DATAFLOW_SIDE_EFFECTING)


def _in_hbm(a):
    return pltpu.with_memory_space_constraint(a, pltpu.HBM)


def _hbm_like(a):
    return pltpu.HBM(a.shape, a.dtype)


def _place():
    x, y, c = lax.axis_index("x"), lax.axis_index("y"), lax.axis_index("c")
    other_chips = [(1 - x, y), (x, 1 - y), (1 - x, 1 - y)]
    return x, y, c, other_chips


def _block(px, py, pc):
    return 4 * px + 2 * py + pc


def _gather_start(name, shards, groups):
    n, ng = len(shards), len(groups)
    lands = [lax.empty((N_DEV, *a.shape), a.dtype) for a in shards]

    def body(*refs):
        src, land = refs[:n], refs[n:2 * n]
        sems, token = refs[2 * n:2 * n + 2 * ng], refs[-1]
        x, y, c, chips = _place()
        targets = [(x, y, 1 - c)] + [(*chip, c) for chip in chips]
        for gi, group in enumerate(groups):
            for i, w in enumerate(group):
                for k, to in enumerate(targets):
                    pltpu.make_async_remote_copy(
                        src_ref=src[w], dst_ref=land[w].at[_block(x, y, c)],
                        send_sem=sems[2 * gi].at[4 * i + k], recv_sem=sems[2 * gi + 1].at[4 * i + k],
                        device_id=to, device_id_type=MESH).start()
        token[...] = jnp.zeros_like(token)

    sem_shapes = [pltpu.SemaphoreType.DMA((4 * len(g),)) for g in groups for _ in range(2)]
    out = pl.pallas_call(
        body, name=name,
        in_specs=[HBM] * (2 * n),
        out_specs=[SEM] * (2 * ng) + [HBM] * (2 * n) + [IN_VMEM],
        out_shape=sem_shapes + [_hbm_like(a) for a in shards] + [_hbm_like(a) for a in lands]
        + [jax.ShapeDtypeStruct((SUBLANE, LANE), F32)],
        input_output_aliases={i: 2 * ng + i for i in range(2 * n)},
        compiler_params=SPLIT,
    )(*[_in_hbm(a) for a in shards], *[_in_hbm(a) for a in lands])
    sems = [(out[2 * gi], out[2 * gi + 1]) for gi in range(ng)]
    return sems, out[2 * ng:2 * ng + n], out[2 * ng + n:2 * ng + 2 * n], out[-1]


def _gather_forward(name, shards, lands, send1, recv1, after):
    n = len(lands)

    def body(*refs):
        src, land = refs[:n], refs[n:2 * n]
        s1, r1 = refs[2 * n], refs[2 * n + 1]
        s2, r2 = refs[2 * n + 2 + len(after)], refs[2 * n + 3 + len(after)]
        x, y, c, chips = _place()
        me, sibling = (x, y, c), (x, y, 1 - c)
        for j, chip in enumerate(chips):
            for i in range(n):
                blk = land[i].at[_block(*chip, c)]
                pltpu.make_async_remote_copy(src_ref=blk, dst_ref=blk, send_sem=s1.at[4 * i + 1 + j], recv_sem=r1.at[4 * i + 1 + j],
                                             device_id=me, device_id_type=MESH).wait_recv()
                pltpu.make_async_remote_copy(src_ref=blk, dst_ref=blk, send_sem=s2.at[3 * i + j], recv_sem=r2.at[3 * i + j],
                                             device_id=sibling, device_id_type=MESH).start()
        for i in range(n):
            blk = land[i].at[_block(x, y, 1 - c)]
            pltpu.make_async_remote_copy(src_ref=blk, dst_ref=blk, send_sem=s1.at[4 * i], recv_sem=r1.at[4 * i],
                                         device_id=me, device_id_type=MESH).wait_recv()
            for k in range(4):
                pltpu.make_async_remote_copy(src_ref=src[i], dst_ref=land[i].at[_block(x, y, c)], send_sem=s1.at[4 * i + k],
                                             recv_sem=r1.at[4 * i + k], device_id=sibling, device_id_type=MESH).wait_send()

    sem = pltpu.SemaphoreType.DMA((3 * n,))
    out = pl.pallas_call(
        body, name=name,
        in_specs=[HBM] * (2 * n) + [SEM, SEM] + [ANY] * len(after),
        out_specs=[SEM, SEM] + [HBM] * n,
        out_shape=[sem, sem] + [_hbm_like(a) for a in lands],
        input_output_aliases={n + i: 2 + i for i in range(n)},
        compiler_params=SPLIT,
    )(*shards, *lands, send1, recv1, *after)
    return (out[0], out[1]), out[2:]


def _gather_wait(name, lands, send2, recv2, after):
    n = len(lands)

    def body(*refs):
        land, s2, r2 = refs[:n], refs[n], refs[n + 1]
        x, y, c, chips = _place()
        me = (x, y, c)
        for i in range(n):
            for j, chip in enumerate(chips):
                got = land[i].at[_block(*chip, 1 - c)]
                pltpu.make_async_remote_copy(src_ref=got, dst_ref=got, send_sem=s2.at[3 * i + j], recv_sem=r2.at[3 * i + j],
                                             device_id=me, device_id_type=MESH).wait_recv()
                sent = land[i].at[_block(*chip, c)]
                pltpu.make_async_remote_copy(src_ref=sent, dst_ref=sent, send_sem=s2.at[3 * i + j], recv_sem=r2.at[3 * i + j],
                                             device_id=me, device_id_type=MESH).wait_send()

    return pl.pallas_call(
        body, name=name,
        in_specs=[HBM] * n + [SEM, SEM] + [ANY] * len(after), out_specs=[HBM] * n, out_shape=[_hbm_like(a) for a in lands],
        input_output_aliases={i: i for i in range(n)},
        compiler_params=SPLIT,
    )(*lands, send2, recv2, *after)


def _pair_exchange(name, grads, shard_rows):
    n = len(grads)
    shapes = [(g.shape[1:] if r is None else (r, g.shape[1])) for g, r in zip(grads, shard_rows)]

    def body(*refs):
        ins, recv = refs[:n], refs[n:2 * n]
        send_sems, recv_sems = refs[2 * n:]
        x, y, c, _ = _place()
        sends = []
        for w in range(n):
            for k in range(N_CHIP):
                j, r = 2 * k + 1 - c, shard_rows[w]
                src = ins[w].at[j] if r is None else ins[w].at[pl.ds(pl.multiple_of(j * r, SUBLANE), r), :]
                sends.append(pltpu.make_async_remote_copy(
                    src_ref=src, dst_ref=recv[w].at[k],
                    send_sem=send_sems.at[w, k], recv_sem=recv_sems.at[w, k],
                    device_id=(x, y, 1 - c), device_id_type=MESH))
        for cp in sends:
            cp.start()
        for cp in sends:
            cp.wait()

    return pl.pallas_call(
        body, name=name,
        in_specs=[ANY] * n, out_specs=[ANY] * n,
        out_shape=[jax.ShapeDtypeStruct((N_CHIP, *shape), g.dtype) for g, shape in zip(grads, shapes)],
        scratch_shapes=[pltpu.SemaphoreType.DMA((n, N_CHIP))] * 2,
    )(*grads)


def _pair_sum_rows(name, grad, received, core):
    _, r, c = received.shape
    tc = _fit(c, 512)

    def body(core_ref, a_ref, b_ref, o_ref):
        o_ref[...] = (a_ref[...] + b_ref[...]).astype(o_ref.dtype)

    spec = pl.BlockSpec((None, r, tc), lambda k, i, core_ref: (k, 0, i))
    return pl.pallas_call(
        body, name=name,
        grid_spec=pltpu.PrefetchScalarGridSpec(
            num_scalar_prefetch=1, grid=(N_CHIP, c // tc),
            in_specs=[pl.BlockSpec((r, tc), lambda k, i, core_ref: (2 * k + core_ref[0], i)), spec],
            out_specs=spec),
        out_shape=jax.ShapeDtypeStruct(received.shape, BF16),
        compiler_params=_params("parallel", "parallel"),
    )(core, grad, received)


def _pair_sum(name, grad, received, core):
    _, r, c = received.shape
    rows = min(ROWS, r)
    assert r % rows == 0

    def body(core_ref, a_ref, b_ref, o_ref):
        o_ref[...] = (a_ref[...].astype(F32) + b_ref[...].astype(F32)).astype(o_ref.dtype)

    spec = pl.BlockSpec((None, rows, c), lambda k, i, core_ref: (k, i, 0))
    return pl.pallas_call(
        body, name=name,
        grid_spec=pltpu.PrefetchScalarGridSpec(
            num_scalar_prefetch=1, grid=(N_CHIP, r // rows),
            in_specs=[pl.BlockSpec((None, None, rows, c), lambda k, i, core_ref: (k, core_ref[0], i, 0)), spec],
            out_specs=spec),
        out_shape=jax.ShapeDtypeStruct(received.shape, received.dtype),
        compiler_params=_params("parallel", "parallel"),
    )(core, grad.reshape(N_CHIP, 2, r, c), received)


def _pair_send_start(name, away):
    land = lax.empty(away.shape, away.dtype)

    def body(src, dst, send, recv, src_thru, dst_thru, token):
        x, y, c, _ = _place()
        pltpu.make_async_remote_copy(src_ref=src, dst_ref=dst, send_sem=send, recv_sem=recv,
                                     device_id=(x, y, 1 - c), device_id_type=MESH).start()
        token[...] = jnp.zeros_like(token)

    sem = pltpu.SemaphoreType.DMA(())
    out = pl.pallas_call(
        body, name=name,
        in_specs=[HBM, HBM], out_specs=[SEM, SEM, HBM, HBM, IN_VMEM],
        out_shape=[sem, sem, _hbm_like(away), _hbm_like(land), jax.ShapeDtypeStruct((SUBLANE, LANE), F32)],
        input_output_aliases={0: 2, 1: 3},
        compiler_params=SPLIT,
    )(_in_hbm(away), _in_hbm(land))
    return (out[0], out[1]), out[2], out[3], out[4]


def _pair_send_wait(name, sems, src, land, after):
    def body(src_ref, dst_ref, send, recv, *rest):
        x, y, c, _ = _place()
        pltpu.make_async_remote_copy(src_ref=src_ref, dst_ref=dst_ref, send_sem=send, recv_sem=recv,
                                     device_id=(x, y, 1 - c), device_id_type=MESH).wait()

    return pl.pallas_call(
        body, name=name,
        in_specs=[HBM, HBM, SEM, SEM] + [ANY] * len(after), out_specs=HBM, out_shape=_hbm_like(land),
        input_output_aliases={1: 0},
        compiler_params=SPLIT,
    )(src, land, *sems, *after)


def _chip_send_start(name, sums):
    n = len(sums)
    lands = [lax.empty(a.shape, a.dtype) for a in sums]

    def body(*refs):
        src, land = refs[:n], refs[n:2 * n]
        send, recv, token = refs[2 * n], refs[2 * n + 1], refs[-1]
        x, y, c, chips = _place()
        for w in range(n):
            for j, (px, py) in enumerate(chips):
                pltpu.make_async_remote_copy(
                    src_ref=src[w].at[2 * px + py], dst_ref=land[w].at[2 * x + y],
                    send_sem=send.at[3 * w + j], recv_sem=recv.at[3 * w + j],
                    device_id=(px, py, c), device_id_type=MESH).start()
        token[...] = jnp.zeros_like(token)

    sem = pltpu.SemaphoreType.DMA((3 * n,))
    out = pl.pallas_call(
        body, name=name,
        in_specs=[HBM] * (2 * n),
        out_specs=[SEM, SEM] + [HBM] * (2 * n) + [IN_VMEM],
        out_shape=[sem, sem] + [_hbm_like(a) for a in sums] + [_hbm_like(a) for a in lands]
        + [jax.ShapeDtypeStruct((SUBLANE, LANE), F32)],
        input_output_aliases={i: 2 + i for i in range(2 * n)},
        compiler_params=SPLIT,
    )(*[_in_hbm(a) for a in sums], *[_in_hbm(a) for a in lands])
    return (out[0], out[1]), out[2:2 + n], out[2 + n:2 + 2 * n], out[-1]


def _chip_send_wait(name, groups, after):
    counts = [len(g[1]) for g in groups]
    n = sum(counts)

    def body(*refs):
        src, land = refs[:n], refs[n:2 * n]
        sems = refs[2 * n:2 * n + 2 * len(groups)]
        x, y, c, chips = _place()
        w = 0
        for gi, count in enumerate(counts):
            for i in range(count):
                for j, (px, py) in enumerate(chips):
                    pltpu.make_async_remote_copy(
                        src_ref=src[w].at[2 * px + py], dst_ref=land[w].at[2 * px + py],
                        send_sem=sems[2 * gi].at[3 * i + j], recv_sem=sems[2 * gi + 1].at[3 * i + j],
                        device_id=(px, py, c), device_id_type=MESH).wait()
                w += 1

    sums = [a for g in groups for a in g[1]]
    lands = [a for g in groups for a in g[2]]
    sems = [s for g in groups for s in g[0]]
    return pl.pallas_call(
        body, name=name,
        in_specs=[HBM] * (2 * n) + [SEM] * len(sems) + [ANY] * len(after),
        out_specs=[HBM] * n, out_shape=[_hbm_like(a) for a in lands],
        input_output_aliases={n + i: i for i in range(n)},
        compiler_params=SPLIT,
    )(*sums, *lands, *sems, *after)


def _small_all_reduce(part, after=()):
    _, w = part.shape

    def body(p_ref, *rest):
        o_ref, buf, send_sems, recv_sems = rest[len(after):]
        x, y, c, _ = _place()
        me = 4 * x + 2 * y + c
        buf[me] = jnp.sum(p_ref[...], axis=0, keepdims=True)
        copies = []
        for k in range(1, N_DEV):
            dx, dy, dc = (k >> 2) & 1, (k >> 1) & 1, k & 1
            copies.append(pltpu.make_async_remote_copy(
                src_ref=buf.at[me], dst_ref=buf.at[me], send_sem=send_sems.at[k - 1], recv_sem=recv_sems.at[k - 1],
                device_id=(x ^ dx, y ^ dy, c ^ dc), device_id_type=MESH))
        for cp in copies:
            cp.start()
        for cp in copies:
            cp.wait()
        tot = buf[0]
        for d in range(1, N_DEV):
            tot = tot + buf[d]
        o_ref[...] = tot
        loss = jnp.sum(tot[:, w - LANE:], axis=1, keepdims=True)
        o_ref[:, w - LANE:] = jnp.broadcast_to(loss, (1, LANE))

    return pl.pallas_call(
        body, name="small_all_reduce",
        in_specs=[IN_VMEM] + [ANY] * len(after), out_specs=IN_VMEM,
        out_shape=jax.ShapeDtypeStruct((1, w), F32),
        scratch_shapes=[pltpu.VMEM((N_DEV, 1, w), F32), pltpu.SemaphoreType.DMA((N_DEV - 1,)), pltpu.SemaphoreType.DMA((N_DEV - 1,))],
        compiler_params=pltpu.CompilerParams(vmem_limit_bytes=VMEM_LIMIT_BYTES),
    )(part, *after)


def _adamw(w, g, m, v):
    m = ADAM_B1 * m + (1.0 - ADAM_B1) * g
    v = ADAM_B2 * v + (1.0 - ADAM_B2) * (g * g)
    m_hat = m / (1.0 - ADAM_B1 ** ADAM_STEP)
    v_hat = v / (1.0 - ADAM_B2 ** ADAM_STEP)
    delta = -ADAM_LR * (m_hat / (jnp.sqrt(v_hat) + ADAM_EPS) + ADAM_WD * w)
    return delta, m, v


def _sum_adam(name, parts, sums, chip, w, m, v, after=()):
    _, r, c = w.shape
    n_after = len(after)
    by_rows = r % ROWS == 0 or r < ROWS
    tr, tc = (min(ROWS, r), c) if by_rows else (r, _fit(c, 512))
    at = (lambda i: (i, 0)) if by_rows else (lambda i: (0, i))

    def body(chip_ref, p_ref, own_ref, w_ref, m_ref, v_ref, *rest):
        g_ref, d_ref, mo_ref, vo_ref = rest[n_after:]
        g = None
        for k in range(N_CHIP):
            term = jnp.where(chip_ref[0] == k, own_ref[...], p_ref[k]).astype(F32)
            g = term if g is None else g + term
        g_ref[...] = g
        d_ref[...], mo_ref[...], vo_ref[...] = _adamw(w_ref[...], g, m_ref[...], v_ref[...])

    blk = pl.BlockSpec((None, tr, tc), lambda i, chip_ref: (0, *at(i)))
    out = jax.ShapeDtypeStruct((1, r, c), F32)
    return pl.pallas_call(
        body, name=name,
        grid_spec=pltpu.PrefetchScalarGridSpec(
            num_scalar_prefetch=1, grid=(r // tr if by_rows else c // tc,),
            in_specs=[pl.BlockSpec((N_CHIP, tr, tc), lambda i, chip_ref: (0, *at(i))),
                      pl.BlockSpec((None, tr, tc), lambda i, chip_ref: (chip_ref[0], *at(i))), blk, blk, blk]
            + [ANY] * n_after,
            out_specs=[blk] * 4),
        out_shape=[out] * 4,
        compiler_params=_params("parallel"),
    )(chip, parts, sums, w, m, v, *after)


def _adam_gains(total, ws, ms, vs):
    n = len(ws)
    widths = [w.shape[1] for w in ws]

    def body(t_ref, *refs):
        w_refs, m_refs, v_refs, outs = refs[:n], refs[n:2 * n], refs[2 * n:3 * n], refs[3 * n:]
        off = 0
        for i in range(n):
            g = t_ref[:, off:off + widths[i]]
            off += widths[i]
            g_ref, d_ref, mo_ref, vo_ref = outs[4 * i:4 * i + 4]
            g_ref[...] = g
            d_ref[...], mo_ref[...], vo_ref[...] = _adamw(w_refs[i][...], g, m_refs[i][...], v_refs[i][...])

    out = pl.pallas_call(
        body, name="adam_gains",
        out_shape=[jax.ShapeDtypeStruct(w.shape, F32) for w in ws for _ in range(4)],
    )(total, *ws, *ms, *vs)
    return [tuple(out[4 * i:4 * i + 4]) for i in range(n)]


def _adam_taps(total, first_col, device, w, m, v):
    _, n_taps, cw = w.shape
    col_block = lambda t, dev: (0, first_col // cw + t * N_DEV + dev[0])
    tap = pl.BlockSpec((None, 1, cw), lambda t, dev: (t, 0, 0))

    def body(dev_ref, t_ref, w_ref, m_ref, v_ref, g_ref, d_ref, mo_ref, vo_ref):
        g = t_ref[...]
        g_ref[...] = g
        d_ref[...], mo_ref[...], vo_ref[...] = _adamw(w_ref[...], g, m_ref[...], v_ref[...])

    shape3 = (n_taps, 1, cw)
    out = pl.pallas_call(
        body, name="adam_taps",
        grid_spec=pltpu.PrefetchScalarGridSpec(
            num_scalar_prefetch=1, grid=(n_taps,),
            in_specs=[pl.BlockSpec((1, cw), col_block), tap, tap, tap], out_specs=[tap] * 4),
        out_shape=[jax.ShapeDtypeStruct(shape3, F32)] * 4,
    )(device, total, w.reshape(shape3), m.reshape(shape3), v.reshape(shape3))
    return tuple(o.reshape(w.shape) for o in out)


def kernel(x, pre_mix_g, w_in, conv_w, q_norm_g, w_uq, kv_norm_g, w_ukv, conv_out_g, attn_out_g, w_o, post_mix_g, pre_mlp_g, w_up, w_down, post_mlp_g, loss_target, m_pre_mix_g, m_w_in, m_conv_w, m_q_norm_g, m_w_uq, m_kv_norm_g, m_w_ukv, m_conv_out_g, m_attn_out_g, m_w_o, m_post_mix_g, m_pre_mlp_g, m_w_up, m_w_down, m_post_mlp_g, v_pre_mix_g, v_w_in, v_conv_w, v_q_norm_g, v_w_uq, v_kv_norm_g, v_w_ukv, v_conv_out_g, v_attn_out_g, v_w_o, v_post_mix_g, v_pre_mlp_g, v_w_up, v_w_down, v_post_mlp_g):
    me = 4 * lax.axis_index("x") + 2 * lax.axis_index("y") + lax.axis_index("c")
    core = lax.axis_index("c").astype(jnp.int32).reshape(1)
    chip = (2 * lax.axis_index("x") + lax.axis_index("y")).astype(jnp.int32).reshape(1)
    gains = (pre_mix_g, q_norm_g, kv_norm_g, conv_out_g, attn_out_g, post_mix_g, pre_mlp_g, post_mlp_g)
    gain_m = (m_pre_mix_g, m_q_norm_g, m_kv_norm_g, m_conv_out_g, m_attn_out_g, m_post_mix_g, m_pre_mlp_g, m_post_mlp_g)
    gain_v = (v_pre_mix_g, v_q_norm_g, v_kv_norm_g, v_conv_out_g, v_attn_out_g, v_post_mix_g, v_pre_mlp_g, v_post_mlp_g)
    names = ("w_in", "w_uq", "w_ukv", "w_o", "w_up", "w_down")
    big = dict(zip(names, (w_in, w_uq, w_ukv, w_o, w_up, w_down)))
    big_m = dict(zip(names, (m_w_in, m_w_uq, m_w_ukv, m_w_o, m_w_up, m_w_down)))
    big_v = dict(zip(names, (v_w_in, v_w_uq, v_w_ukv, v_w_o, v_w_up, v_w_down)))
    n_heads = attn_out_g.shape[1] // HEAD
    n_taps = conv_w.shape[1]

    gathered = ("w_in", "conv", "w_uq", "w_ukv", "w_o", "w_up", "w_down")
    gather_groups = ((0, 1), (2, 3, 4), (5,), (6,))
    taps = jnp.pad(conv_w[0], ((0, SUBLANE - n_taps), (0, 0)))
    sems_a, shards_a, lands_a, token_a = _gather_start("gather_start_first", [w_in[0].astype(BF16), taps], ((0, 1),))
    behind = token_a[0, 0]
    sems_b, shards_b, lands_b, token = _gather_start(
        "gather_start_rest", [(big[nm][0] + behind).astype(BF16) for nm in gathered[2:]], ((0, 1, 2), (3,), (4,)))
    sems1, shards, lands = sems_a + sems_b, [*shards_a, *shards_b], [*lands_a, *lands_b]

    cols = lambda a: jnp.concatenate([a[j] for j in range(N_DEV)], axis=1)
    rows = lambda a: a.reshape(N_DEV * a.shape[1], a.shape[2])
    ready = {
        "w_in": _join_col_shards,
        "conv": lambda a: cols(a)[:n_taps],
        "w_uq": lambda a: _permute_q_cols(cols(a), n_heads),
        "w_ukv": cols, "w_o": rows, "w_up": lambda a: a, "w_down": rows,
    }

    class Weights:
        def __init__(self):
            self.passed = {}

        def forward(self, group, after):
            idx = gather_groups[group]
            self.passed[group] = _gather_forward(f"gather_forward_{group}", [shards[i] for i in idx], [lands[i] for i in idx],
                                                 *sems1[group], after)
            return tuple(self.passed[group][1])

        def ready(self, group, after):
            sems2, mid = self.passed[group]
            full = _gather_wait(f"gather_wait_{group}", mid, *sems2, after)
            out = []
            for i, a in zip(gather_groups[group], full):
                a = lax.dynamic_update_index_in_dim(a, shards[i], me, 0)
                out.append(ready[gathered[i]](a))
            return out

    weights = Weights()

    col_blocks = lambda g: g.reshape(g.shape[0], N_DEV, g.shape[1] // N_DEV).transpose(1, 0, 2)
    row_blocks = lambda g: g.reshape(N_DEV, g.shape[0] // N_DEV, g.shape[1])
    grad_groups = (("w_down",), ("w_up",), ("w_o",), ("w_uq", "w_ukv"), ("w_in",))
    transposed = {"w_in": w_in.shape[2], "w_uq": w_uq.shape[2]}
    to_blocks = {
        "w_in": lambda g: g, "w_uq": lambda g: _unpermute_q_rows(g, n_heads),
        "w_ukv": col_blocks, "w_o": row_blocks, "w_up": lambda g: g, "w_down": row_blocks,
    }
    in_flight = []

    class Grads:
        def __init__(self):
            self.core = core
            self.away = {}

        def send_sums(self, group, sums):
            sems, sums, parts, tok = _chip_send_start(f"chip_send_start_{group}", list(sums))
            in_flight.append((sems, sums, parts))
            return (tok,)

        def full(self, group, arrays):
            nms = grad_groups[group]
            blocks = [to_blocks[nm](g) for nm, g in zip(nms, arrays)]
            received = _pair_exchange(f"pair_exchange_{group}", blocks, [transposed.get(nm) for nm in nms])
            sums = [(_pair_sum_rows if nm in transposed else _pair_sum)(f"pair_sum_{nm}", g, r, core)
                    for nm, g, r in zip(nms, blocks, received)]
            return self.send_sums(group, sums)

        def send_away(self, group, half):
            sems, src, land, tok = _pair_send_start(f"pair_send_start_{group}", half)
            self.away[group] = (sems, src, land)
            return (tok,)

        def received(self, group, after):
            sems, src, land = self.away[group]
            return _pair_send_wait(f"pair_send_wait_{group}", sems, src, land, after)

    grad_x, small = _local_step(x[0], loss_target[0], gains, weights, Grads(), first_after=(token,))

    big_out = {}

    def update(tag, first, last, after):
        groups = in_flight[first:last]
        parts = _chip_send_wait("chip_send_wait_" + tag, groups, after)
        nms = [nm for grp in grad_groups[first:last] for nm in grp]
        sums = [a for _, s, _ in groups for a in s]
        for nm, p, s in zip(nms, parts, sums):
            view = (lambda a: jnp.swapaxes(a, 1, 2)) if nm in transposed else (lambda a: a)
            out = _sum_adam("adam_" + nm, p, s, chip, view(big[nm]), view(big_m[nm]), view(big_v[nm]), after=after)
            after = (out[0],)
            big_out[nm] = [view(o) for o in out]
        return after

    after = update("early", 0, len(in_flight) - 1, (grad_x,))
    total = _small_all_reduce(small, after=after)
    update("late", len(in_flight) - 1, len(in_flight), (total,))
    big_out = [big_out[nm] for nm in names]

    gain_out = _adam_gains(total, gains, gain_m, gain_v)
    taps_out = _adam_taps(total, sum(g.shape[1] for g in gains), me.astype(jnp.int32).reshape(1), conv_w, m_conv_w, v_conv_w)
    loss = total[0, total.shape[1] - 1]

    order = (0, "w_in", "conv", 1, "w_uq", 2, "w_ukv", 3, 4, "w_o", 5, 6, "w_up", "w_down", 7)
    by_name = dict(zip(names, big_out))
    outs = [loss, grad_x[None]]
    for kind in range(4):
        for item in order:
            if item == "conv":
                outs.append(taps_out[kind])
            elif isinstance(item, int):
                outs.append(gain_out[item][kind])
            else:
                outs.append(by_name[item][kind])
    return tuple(outs)
```

```python
import math

import jax
import jax.numpy as jnp
from jax import lax
from jax.experimental import pallas as pl
from jax.experimental.pallas import tpu as pltpu

F32 = jnp.float32
BF16 = jnp.bfloat16

EPS = 1e-6
NEG_INF = -1e30
HEAD = 128
ROPE = 64
QK = HEAD + ROPE
CHUNK = 64
ROPE_THETA = 10000.0
ADAM_LR, ADAM_B1, ADAM_B2, ADAM_EPS, ADAM_WD, ADAM_STEP = 0.001, 0.9, 0.999, 1e-08, 0.01, 10

LANE = 128
SUBLANE = 8
VMEM_LIMIT_BYTES = 56 * 1024 * 1024

N_DEV = 8
N_CHIP = 4
MESH = pl.DeviceIdType.MESH


def _params(*sem):
    return pltpu.CompilerParams(dimension_semantics=sem, vmem_limit_bytes=VMEM_LIMIT_BYTES)


ANY = pl.BlockSpec(memory_space=pl.ANY)


def _call(body, *, in_specs, after=(), **kw):
    n_in, n_after = len(in_specs), len(after)

    def ordered(*refs):
        body(*refs[:n_in], *refs[n_in + n_after:])

    call = pl.pallas_call(ordered, in_specs=[*in_specs, *[ANY] * n_after], **kw)
    return lambda *operands: call(*operands, *after)


def _sublane_sum(v):
    r, w = v.shape
    return jnp.sum(v.reshape(r // SUBLANE, SUBLANE, w), axis=0)


def _rstd(x):
    return lax.rsqrt(jnp.mean(x * x, axis=-1, keepdims=True) + EPS)


def _rms_bwd(x, g, dy):
    r = _rstd(x)
    xh = x * r
    dxh = dy * g
    dx = r * (dxh - xh * jnp.mean(dxh * xh, axis=-1, keepdims=True))
    return dx, dy * xh


def _accumulate(ref, val, step):
    @pl.when(step == 0)
    def _():
        ref[...] = val

    @pl.when(step > 0)
    def _():
        ref[...] += val


NN = ((1,), (0,))
NT = ((1,), (1,))
TN = ((0,), (0,))


def _matmul(name, a, b, *, grid, a_spec, b_spec, out_shape, out_specs, contract, nk=1, acc_shape=None,
            extras=(), extra_specs=(), epilogue=None, after=()):
    multi = isinstance(out_shape, (tuple, list))
    out_shapes = tuple(out_shape) if multi else (out_shape,)
    n_out = len(out_shapes)
    n_extra = len(extras)

    def body(a_ref, b_ref, *rest):
        x_refs = rest[:n_extra]
        o_refs = rest[n_extra:n_extra + n_out]

        def emit(acc):
            vals = epilogue(acc, *[r[...] for r in x_refs]) if epilogue else (acc,)
            for r, v in zip(o_refs, vals):
                r[...] = v.astype(r.dtype)

        p = lax.dot_general(a_ref[...], b_ref[...], (contract, ((), ())), preferred_element_type=F32)
        if nk == 1:
            emit(p)
        else:
            acc_ref = rest[n_extra + n_out]
            k = pl.program_id(2)
            _accumulate(acc_ref, p, k)

            @pl.when(k == nk - 1)
            def _():
                emit(acc_ref[...])

    sem = ("parallel", "parallel") + (("arbitrary",) if nk > 1 else ())
    return _call(
        body, name=name, grid=grid, after=after,
        in_specs=[a_spec, b_spec, *extra_specs],
        out_specs=out_specs,
        out_shape=out_shape,
        scratch_shapes=[pltpu.VMEM(acc_shape, F32)] if nk > 1 else [],
        compiler_params=_params(*sem),
    )(a, b, *extras)


def _fit(n, tile):
    if n <= tile:
        return n
    t = tile - tile % LANE
    while n % t:
        t -= LANE
    return t


def _mm_nn(name, a, b, out_dtype, tm, tn, after=()):
    m, k = a.shape
    n = b.shape[1]
    tm, tn = _fit(m, tm), _fit(n, tn)
    return _matmul(name, a, b, grid=(m // tm, n // tn), after=after,
                   a_spec=pl.BlockSpec((tm, k), lambda i, j: (i, 0)),
                   b_spec=pl.BlockSpec((k, tn), lambda i, j: (0, j)),
                   out_shape=jax.ShapeDtypeStruct((m, n), out_dtype),
                   out_specs=pl.BlockSpec((tm, tn), lambda i, j: (i, j)), contract=NN)


def _mm_nt(name, a, b, out_dtype, tm, tn, after=()):
    m, k = a.shape
    n = b.shape[0]
    tm, tn = _fit(m, tm), _fit(n, tn)
    return _matmul(name, a, b, grid=(m // tm, n // tn), after=after,
                   a_spec=pl.BlockSpec((tm, k), lambda i, j: (i, 0)),
                   b_spec=pl.BlockSpec((tn, k), lambda i, j: (j, 0)),
                   out_shape=jax.ShapeDtypeStruct((m, n), out_dtype),
                   out_specs=pl.BlockSpec((tm, tn), lambda i, j: (i, j)), contract=NT)


def _mm_tn(name, a, b, out_dtype, tm, tn):
    s, m = a.shape
    n = b.shape[1]
    tm, tn = _fit(m, tm), _fit(n, tn)
    return _matmul(name, a, b, grid=(m // tm, n // tn),
                   a_spec=pl.BlockSpec((s, tm), lambda i, j: (0, i)),
                   b_spec=pl.BlockSpec((s, tn), lambda i, j: (0, j)),
                   out_shape=jax.ShapeDtypeStruct((m, n), out_dtype),
                   out_specs=pl.BlockSpec((tm, tn), lambda i, j: (i, j)), contract=TN)


ROWS = 256


def _row_spec(rows, width):
    return pl.BlockSpec((rows, width), lambda i: (i, 0))


def _fixed_spec(rows, width):
    return pl.BlockSpec((rows, width), lambda i: (0, 0))


def _column_pieces(rows, start, width):
    piece = math.gcd(start, width)
    assert piece % LANE == 0
    return [pl.BlockSpec((rows, piece), lambda i, b=start // piece + p: (i, b)) for p in range(width // piece)]


def _rms_fwd(name, x, g, cols=None, after=()):
    s = x.shape[0]
    start, w = cols or (0, x.shape[1])
    rows = min(ROWS, s)
    pieces = _column_pieces(rows, start, w) if cols else [_row_spec(rows, w)]
    n = len(pieces)

    def body(*refs):
        g_ref, o_ref = refs[n:]
        xv = refs[0][...] if n == 1 else jnp.concatenate([r[...] for r in refs[:n]], axis=1)
        o_ref[...] = (xv * _rstd(xv) * g_ref[...]).astype(o_ref.dtype)

    return _call(
        body, name=name, grid=(s // rows,), after=after,
        in_specs=[*pieces, _fixed_spec(1, w)],
        out_specs=_row_spec(rows, w),
        out_shape=jax.ShapeDtypeStruct((s, w), BF16),
        compiler_params=_params("parallel"),
    )(*[x] * n, g)


def _rms_bwd_call(name, x, g, dy, out_dtype, cols=None, after=()):
    s = x.shape[0]
    start, w = cols or (0, x.shape[1])
    rows = min(ROWS, s)
    pieces = _column_pieces(rows, start, w) if cols else [_row_spec(rows, w)]
    n = len(pieces)

    def body(*refs):
        g_ref, dy_ref, dx_ref, dg_ref = refs[n:]
        xv = refs[0][...] if n == 1 else jnp.concatenate([r[...] for r in refs[:n]], axis=1)
        dx, dgc = _rms_bwd(xv, g_ref[...], dy_ref[...].astype(F32))
        dx_ref[...] = dx.astype(dx_ref.dtype)
        _accumulate(dg_ref, _sublane_sum(dgc), pl.program_id(0))

    return _call(
        body, name=name, grid=(s // rows,), after=after,
        in_specs=[*pieces, _fixed_spec(1, w), _row_spec(rows, w)],
        out_specs=[_row_spec(rows, w), _fixed_spec(SUBLANE, w)],
        out_shape=[jax.ShapeDtypeStruct((s, w), out_dtype), jax.ShapeDtypeStruct((SUBLANE, w), F32)],
        compiler_params=_params("arbitrary"),
    )(*[x] * n, g, dy)


def _mid_fwd(x, y, g_post, g_pre, after=()):
    s, w = x.shape
    rows = min(ROWS, s)

    def body(x_ref, y_ref, gp_ref, gq_ref, x2_ref, h2_ref):
        yv = y_ref[...]
        x2 = x_ref[...] + yv * _rstd(yv) * gp_ref[...]
        x2_ref[...] = x2
        h2_ref[...] = (x2 * _rstd(x2) * gq_ref[...]).astype(h2_ref.dtype)

    return _call(
        body, name="mid_fwd", grid=(s // rows,), after=after,
        in_specs=[_row_spec(rows, w), _row_spec(rows, w), _fixed_spec(1, w), _fixed_spec(1, w)],
        out_specs=[_row_spec(rows, w), _row_spec(rows, w)],
        out_shape=[jax.ShapeDtypeStruct((s, w), F32), jax.ShapeDtypeStruct((s, w), BF16)],
        compiler_params=_params("parallel"),
    )(x, y, g_post, g_pre)


def _head(m, x2, tgt, g):
    s, w = m.shape
    rows = min(ROWS, s)

    def body(m_ref, x2_ref, t_ref, g_ref, dout_ref, dm_ref, dg_ref, loss_ref):
        mv = m_ref[...]
        gv = g_ref[...]
        out = x2_ref[...] + mv * _rstd(mv) * gv
        err = out - t_ref[...]
        dout = err * (1.0 / w)
        dout_ref[...] = dout
        dm, dgc = _rms_bwd(mv, gv, dout)
        dm_ref[...] = dm.astype(dm_ref.dtype)
        sq = err * err
        lanes = sq[:, 0:LANE]
        for j in range(1, w // LANE):
            lanes = lanes + sq[:, j * LANE:(j + 1) * LANE]
        step = pl.program_id(0)
        _accumulate(dg_ref, _sublane_sum(dgc), step)
        _accumulate(loss_ref, _sublane_sum(lanes) * (0.5 / w), step)

    return pl.pallas_call(
        body, name="head", grid=(s // rows,),
        in_specs=[_row_spec(rows, w), _row_spec(rows, w), _row_spec(rows, w), _fixed_spec(1, w)],
        out_specs=[_row_spec(rows, w), _row_spec(rows, w), _fixed_spec(SUBLANE, w), _fixed_spec(SUBLANE, LANE)],
        out_shape=[jax.ShapeDtypeStruct((s, w), F32), jax.ShapeDtypeStruct((s, w), BF16),
                   jax.ShapeDtypeStruct((SUBLANE, w), F32), jax.ShapeDtypeStruct((SUBLANE, LANE), F32)],
        compiler_params=_params("arbitrary"),
    )(m, x2, tgt, g)


def _mid_bwd(x2, y, d_out, d_h2, g_pre, g_post, after=()):
    s, w = x2.shape
    rows = min(ROWS, s)

    def body(x2_ref, y_ref, dout_ref, dh2_ref, gq_ref, gp_ref, dx2_ref, dy_ref, dgq_ref, dgp_ref):
        dx, dgq = _rms_bwd(x2_ref[...], gq_ref[...], dh2_ref[...])
        dx2 = dout_ref[...] + dx
        dx2_ref[...] = dx2
        dy, dgp = _rms_bwd(y_ref[...], gp_ref[...], dx2)
        dy_ref[...] = dy.astype(dy_ref.dtype)
        step = pl.program_id(0)
        _accumulate(dgq_ref, _sublane_sum(dgq), step)
        _accumulate(dgp_ref, _sublane_sum(dgp), step)

    return _call(
        body, name="mid_bwd", grid=(s // rows,), after=after,
        in_specs=[_row_spec(rows, w)] * 4 + [_fixed_spec(1, w)] * 2,
        out_specs=[_row_spec(rows, w), _row_spec(rows, w), _fixed_spec(SUBLANE, w), _fixed_spec(SUBLANE, w)],
        out_shape=[jax.ShapeDtypeStruct((s, w), F32), jax.ShapeDtypeStruct((s, w), BF16),
                   jax.ShapeDtypeStruct((SUBLANE, w), F32), jax.ShapeDtypeStruct((SUBLANE, w), F32)],
        compiler_params=_params("arbitrary"),
    )(x2, y, d_out, d_h2, g_pre, g_post)


def _first_bwd(x, g, d_h1, d_x2, after=()):
    s, w = x.shape
    rows = min(ROWS, s)

    def body(x_ref, g_ref, dh_ref, dx2_ref, dx_ref, dg_ref):
        dx, dgc = _rms_bwd(x_ref[...], g_ref[...], dh_ref[...])
        dx_ref[...] = dx2_ref[...] + dx
        _accumulate(dg_ref, _sublane_sum(dgc), pl.program_id(0))

    return _call(
        body, name="first_bwd", grid=(s // rows,), after=after,
        in_specs=[_row_spec(rows, w), _fixed_spec(1, w), _row_spec(rows, w), _row_spec(rows, w)],
        out_specs=[_row_spec(rows, w), _fixed_spec(SUBLANE, w)],
        out_shape=[jax.ShapeDtypeStruct((s, w), F32), jax.ShapeDtypeStruct((SUBLANE, w), F32)],
        compiler_params=_params("arbitrary"),
    )(x, g, d_h1, d_x2)


def _shift_down(v, k):
    t = lax.broadcasted_iota(jnp.int32, v.shape, 0)
    return jnp.where(t >= k, pltpu.roll(v, k, 0), 0.0)


def _shift_up(v, k):
    n = v.shape[0]
    t = lax.broadcasted_iota(jnp.int32, v.shape, 0)
    return jnp.where(t < n - k, pltpu.roll(v, n - k, 0), 0.0)


def _conv_core(u, b, c, w):
    z = c * u
    conv = w[0:1, :] * _shift_down(z, 2) + w[1:2, :] * _shift_down(z, 1) + w[2:3, :] * z
    return z, conv, b * conv


def _conv_fwd(proj, conv_w, g, n_groups):
    s = proj.shape[0]

    def body(u_ref, b_ref, c_ref, w_ref, g_ref, o_ref):
        _, _, yr = _conv_core(u_ref[...], b_ref[...], c_ref[...], w_ref[...])
        o_ref[...] = (yr * _rstd(yr) * g_ref[...]).astype(o_ref.dtype)

    col = lambda k: pl.BlockSpec((s, HEAD), lambda i: (0, k * n_groups + i))
    return pl.pallas_call(
        body, name="conv_fwd", grid=(n_groups,),
        in_specs=[col(0), col(1), col(2), pl.BlockSpec((3, HEAD), lambda i: (0, i)), pl.BlockSpec((1, HEAD), lambda i: (0, i))],
        out_specs=pl.BlockSpec((s, HEAD), lambda i: (0, i)),
        out_shape=jax.ShapeDtypeStruct((s, n_groups * HEAD), BF16),
        compiler_params=_params("parallel"),
    )(proj, proj, proj, conv_w, g)


def _conv_bwd(proj, d_mix, conv_w, g, n_groups):
    s = proj.shape[0]
    width = n_groups * HEAD

    def body(u_ref, b_ref, c_ref, dy_ref, w_ref, g_ref, du_ref, db_ref, dc_ref, dg_ref, dw_ref):
        u, b, c, w = u_ref[...], b_ref[...], c_ref[...], w_ref[...]
        z, conv, yr = _conv_core(u, b, c, w)
        dyr, dgc = _rms_bwd(yr, g_ref[...], dy_ref[...])
        dconv = dyr * b
        db_ref[...] = (dyr * conv).astype(db_ref.dtype)
        dz = w[2:3, :] * dconv + w[1:2, :] * _shift_up(dconv, 1) + w[0:1, :] * _shift_up(dconv, 2)
        dc_ref[...] = (dz * u).astype(dc_ref.dtype)
        du_ref[...] = (dz * c).astype(du_ref.dtype)
        dg_ref[...] = _sublane_sum(dgc)
        dw_ref[0] = _sublane_sum(dconv * _shift_down(z, 2))
        dw_ref[1] = _sublane_sum(dconv * _shift_down(z, 1))
        dw_ref[2] = _sublane_sum(dconv * z)

    col = lambda k: pl.BlockSpec((s, HEAD), lambda i: (0, k * n_groups + i))
    grp = pl.BlockSpec((s, HEAD), lambda i: (0, i))
    return pl.pallas_call(
        body, name="conv_bwd", grid=(n_groups,),
        in_specs=[col(0), col(1), col(2), grp, pl.BlockSpec((3, HEAD), lambda i: (0, i)), pl.BlockSpec((1, HEAD), lambda i: (0, i))],
        out_specs=[grp, grp, grp, pl.BlockSpec((SUBLANE, HEAD), lambda i: (0, i)),
                   pl.BlockSpec((3, SUBLANE, HEAD), lambda i: (0, 0, i))],
        out_shape=[jax.ShapeDtypeStruct((s, width), BF16)] * 3
        + [jax.ShapeDtypeStruct((SUBLANE, width), F32), jax.ShapeDtypeStruct((3, SUBLANE, width), F32)],
        compiler_params=_params("parallel"),
    )(proj, proj, proj, d_mix, conv_w, g)


def _rope_tables(s, n_heads):
    pos = jnp.arange(s, dtype=F32)
    inv_freq = jnp.power(ROPE_THETA, -jnp.arange(0, ROPE, 2, dtype=F32) / ROPE)
    ang = pos[:, None] * inv_freq[None, :]
    cos, sin = jnp.cos(ang), jnp.sin(ang)
    cs = jnp.concatenate([cos, cos], axis=1)
    sn = jnp.concatenate([-sin, sin], axis=1)
    pad = jnp.zeros((s, LANE - ROPE), F32)
    return (jnp.tile(cs, (1, n_heads)), jnp.tile(sn, (1, n_heads)),
            jnp.concatenate([cs, pad], axis=1), jnp.concatenate([sn, pad], axis=1))


def _swap_halves(v):
    w = v.shape[1]
    lane = lax.broadcasted_iota(jnp.int32, v.shape, 1)
    first = (lane % ROPE) < (ROPE // 2)
    return jnp.where(first, pltpu.roll(v, w - ROPE // 2, 1), pltpu.roll(v, ROPE // 2, 1))


def _pack_heads(q, kv, proj, kr_col, tables, n_heads, after=()):
    s = q.shape[0]
    rows = min(ROWS, s)
    cq, sq, ck, sk = tables
    wq = n_heads * ROPE

    def body(q_ref, kv_ref, kr_ref, cq_ref, sq_ref, ck_ref, sk_ref, qo_ref, ko_ref, vo_ref):
        qr = q_ref[:, n_heads * HEAD:]
        qr = qr * cq_ref[...] + _swap_halves(qr) * sq_ref[...]
        krv = kr_ref[...]
        krv = krv * ck_ref[...] + _swap_halves(krv) * sk_ref[...]
        for h in range(n_heads):
            qo_ref[h] = jnp.concatenate([q_ref[:, h * HEAD:(h + 1) * HEAD], qr[:, h * ROPE:(h + 1) * ROPE]], axis=1).astype(BF16)
            ko_ref[h] = jnp.concatenate([kv_ref[:, 2 * h * HEAD:(2 * h + 1) * HEAD], krv[:, :ROPE]], axis=1).astype(BF16)
            vo_ref[h] = kv_ref[:, (2 * h + 1) * HEAD:(2 * h + 2) * HEAD].astype(BF16)

    hs = lambda w: pl.BlockSpec((n_heads, rows, w), lambda i: (0, i, 0))
    return _call(
        body, name="pack_heads", grid=(s // rows,), after=after,
        in_specs=[_row_spec(rows, q.shape[1]), _row_spec(rows, kv.shape[1]), pl.BlockSpec((rows, LANE), lambda i: (i, kr_col // LANE)),
                  _row_spec(rows, wq), _row_spec(rows, wq), _row_spec(rows, LANE), _row_spec(rows, LANE)],
        out_specs=[hs(QK), hs(QK), hs(HEAD)],
        out_shape=[jax.ShapeDtypeStruct((n_heads, s, QK), BF16), jax.ShapeDtypeStruct((n_heads, s, QK), BF16),
                   jax.ShapeDtypeStruct((n_heads, s, HEAD), BF16)],
        compiler_params=_params("parallel"),
    )(q, kv, proj, cq, sq, ck, sk)


def _unpack_heads(dq, dk, dv, tables, n_heads):
    s = dq.shape[1]
    rows = min(ROWS, s)
    cq, sq, ck, sk = tables
    wq = n_heads * ROPE

    def body(dq_ref, dk_ref, dv_ref, cq_ref, sq_ref, ck_ref, sk_ref, qo_ref, kvo_ref, kro_ref):
        dqr = jnp.concatenate([dq_ref[h][:, HEAD:] for h in range(n_heads)], axis=1)
        dqr = dqr * cq_ref[...] - _swap_halves(dqr) * sq_ref[...]
        dkr = dk_ref[0][:, HEAD:]
        for h in range(1, n_heads):
            dkr = dkr + dk_ref[h][:, HEAD:]
        dkr = jnp.concatenate([dkr, jnp.zeros((rows, LANE - ROPE), F32)], axis=1)
        dkr = dkr * ck_ref[...] - _swap_halves(dkr) * sk_ref[...]
        kro_ref[...] = dkr.astype(kro_ref.dtype)
        qo_ref[:, n_heads * HEAD:] = dqr.astype(qo_ref.dtype)
        for h in range(n_heads):
            qo_ref[:, h * HEAD:(h + 1) * HEAD] = dq_ref[h][:, :HEAD].astype(qo_ref.dtype)
            kvo_ref[:, 2 * h * HEAD:(2 * h + 1) * HEAD] = dk_ref[h][:, :HEAD].astype(kvo_ref.dtype)
            kvo_ref[:, (2 * h + 1) * HEAD:(2 * h + 2) * HEAD] = dv_ref[h].astype(kvo_ref.dtype)

    hs = lambda w: pl.BlockSpec((n_heads, rows, w), lambda i: (0, i, 0))
    return pl.pallas_call(
        body, name="unpack_heads", grid=(s // rows,),
        in_specs=[hs(QK), hs(QK), hs(HEAD), _row_spec(rows, wq), _row_spec(rows, wq), _row_spec(rows, LANE), _row_spec(rows, LANE)],
        out_specs=[_row_spec(rows, n_heads * QK), _row_spec(rows, 2 * n_heads * HEAD), _row_spec(rows, LANE)],
        out_shape=[jax.ShapeDtypeStruct((s, n_heads * QK), BF16), jax.ShapeDtypeStruct((s, 2 * n_heads * HEAD), BF16),
                   jax.ShapeDtypeStruct((s, LANE), BF16)],
        compiler_params=_params("parallel"),
    )(dq, dk, dv, cq, sq, ck, sk)


TQ = 256


LOG2_E = 1.4426950408889634


def _softmax_parts(q, k):
    tq, n_keys = q.shape[0], k.shape[0]
    sc = lax.dot_general(q, k, (NT, ((), ())), preferred_element_type=F32) * (QK ** -0.5 * LOG2_E)
    row = lax.broadcasted_iota(jnp.int32, (tq, tq), 0)
    col = lax.broadcasted_iota(jnp.int32, (tq, tq), 1)
    own = jnp.where(col // CHUNK <= row // CHUNK, sc[:, n_keys - tq:], NEG_INF)
    sc = own if n_keys == tq else jnp.concatenate([sc[:, :n_keys - tq], own], axis=1)
    e = jnp.exp2(sc - jnp.max(sc, axis=-1, keepdims=True))
    return e, 1.0 / jnp.sum(e, axis=-1, keepdims=True)


def _attn_fwd(q, k, v, g):
    n_heads, s, _ = q.shape
    tq = min(TQ, s)
    assert tq % CHUNK == 0 and s % tq == 0

    def body(q_ref, k_ref, v_ref, g_ref, o_ref, y_ref):
        for c in range(s // tq):
            rows, n_keys = pl.ds(c * tq, tq), (c + 1) * tq
            e, inv = _softmax_parts(q_ref[rows, :], k_ref[0:n_keys, :])
            o = jnp.dot(e.astype(BF16), v_ref[0:n_keys, :], preferred_element_type=F32) * inv
            o_ref[rows, :] = o
            y_ref[rows, :] = (o * _rstd(o) * g_ref[...]).astype(y_ref.dtype)

    head = lambda w: pl.BlockSpec((None, s, w), lambda h: (h, 0, 0))
    return pl.pallas_call(
        body, name="attn_fwd", grid=(n_heads,),
        in_specs=[head(QK), head(QK), head(HEAD), pl.BlockSpec((1, HEAD), lambda h: (0, h))],
        out_specs=[head(HEAD), pl.BlockSpec((s, HEAD), lambda h: (0, h))],
        out_shape=[jax.ShapeDtypeStruct((n_heads, s, HEAD), F32), jax.ShapeDtypeStruct((s, n_heads * HEAD), BF16)],
        compiler_params=_params("parallel"),
    )(q, k, v, g)


def _attn_bwd(q, k, v, o, d_mix, g, col0, after=()):
    n_heads, s, _ = q.shape
    tq = min(TQ, s)

    def body(q_ref, k_ref, v_ref, o_ref, dy_ref, g_ref, dq_ref, dk_ref, dv_ref, dg_ref):
        dg = None
        for c in reversed(range(s // tq)):
            rows, n_keys = pl.ds(c * tq, tq), (c + 1) * tq
            qv, kv_, vv = q_ref[rows, :], k_ref[0:n_keys, :], v_ref[0:n_keys, :]
            do, dgc = _rms_bwd(o_ref[rows, :], g_ref[...], dy_ref[rows, :])
            do = do.astype(BF16)
            dg = _sublane_sum(dgc) if dg is None else dg + _sublane_sum(dgc)
            e, inv = _softmax_parts(qv, kv_)
            p = e * inv
            dp = lax.dot_general(do, vv, (NT, ((), ())), preferred_element_type=F32)
            ds = (p * (dp - jnp.sum(p * dp, axis=-1, keepdims=True)) * (QK ** -0.5)).astype(BF16)
            dq_ref[rows, :] = jnp.dot(ds, kv_, preferred_element_type=F32)
            dk = lax.dot_general(ds, qv, (TN, ((), ())), preferred_element_type=F32)
            dv = lax.dot_general(p.astype(BF16), do, (TN, ((), ())), preferred_element_type=F32)
            if n_keys == s:
                dk_ref[...] = dk
                dv_ref[...] = dv
            else:
                dk_ref[0:n_keys, :] += dk
                dv_ref[0:n_keys, :] += dv
        dg_ref[...] = dg

    c0 = col0 // HEAD
    head = lambda w: pl.BlockSpec((None, s, w), lambda h: (h, 0, 0))
    return _call(
        body, name="attn_bwd", grid=(n_heads,), after=after,
        in_specs=[head(QK), head(QK), head(HEAD), head(HEAD), pl.BlockSpec((s, HEAD), lambda h: (0, c0 + h)),
                  pl.BlockSpec((1, HEAD), lambda h: (0, h))],
        out_specs=[head(QK), head(QK), head(HEAD), pl.BlockSpec((SUBLANE, HEAD), lambda h: (0, h))],
        out_shape=[jax.ShapeDtypeStruct((n_heads, s, QK), F32), jax.ShapeDtypeStruct((n_heads, s, QK), F32),
                   jax.ShapeDtypeStruct((n_heads, s, HEAD), F32), jax.ShapeDtypeStruct((SUBLANE, n_heads * HEAD), F32)],
        compiler_params=_params("parallel"),
    )(q, k, v, o, d_mix, g)


TILE_M = 1024
TILE_N = 1024


def _up_fwd(h2, w_up):
    s, d = h2.shape
    nb, _, fb = w_up.shape
    tm = min(TILE_M,s)

    def epilogue(acc):
        r = jnp.maximum(acc, 0.0)
        return r * r, r

    blk = pl.BlockSpec((tm, fb), lambda i, j: (i, j))
    return _matmul("up_fwd", h2, w_up, grid=(s // tm, nb),
                   a_spec=pl.BlockSpec((tm, d), lambda i, j: (i, 0)),
                   b_spec=pl.BlockSpec((None, d, fb), lambda i, j: (j, 0, 0)),
                   out_shape=[jax.ShapeDtypeStruct((s, nb * fb), BF16)] * 2, out_specs=[blk, blk],
                   contract=NN, epilogue=epilogue)


def _down_fwd(a, w_down):
    s, f = a.shape
    d = w_down.shape[1]
    tm, tn, tk = min(TILE_M,s), min(TILE_N,d), 2048
    nk = f // tk
    return _matmul("down_fwd", a, w_down, grid=(s // tm, d // tn, nk),
                   a_spec=pl.BlockSpec((tm, tk), lambda i, j, k: (i, k)),
                   b_spec=pl.BlockSpec((tk, tn), lambda i, j, k: (k, j)),
                   out_shape=jax.ShapeDtypeStruct((s, d), F32),
                   out_specs=pl.BlockSpec((tm, tn), lambda i, j, k: (i, j)),
                   contract=NN, nk=nk, acc_shape=(tm, tn))


def _in_proj_bwd_first(d_proj, w_in_p, x, g, d_x2, after=()):
    s, width = d_proj.shape
    d = w_in_p.shape[0]
    tm = min(512, s)
    tk = _fit(width, 1024)
    nk = width // tk

    def body(a_ref, w_ref, x_ref, g_ref, dx2_ref, dx_ref, dg_ref, acc_ref):
        i, k = pl.program_id(0), pl.program_id(1)
        _accumulate(acc_ref, lax.dot_general(a_ref[...], w_ref[...], (NT, ((), ())), preferred_element_type=F32), k)

        @pl.when(k == nk - 1)
        def _():
            dx, dgc = _rms_bwd(x_ref[...], g_ref[...], acc_ref[...])
            dx_ref[...] = dx2_ref[...] + dx
            _accumulate(dg_ref, _sublane_sum(dgc), i)

    rows = pl.BlockSpec((tm, d), lambda i, k: (i, 0))
    fixed = lambda r, w: pl.BlockSpec((r, w), lambda i, k: (0, 0))
    return _call(
        body, name="in_proj_bwd_act_first", grid=(s // tm, nk), after=after,
        in_specs=[pl.BlockSpec((tm, tk), lambda i, k: (i, k)), pl.BlockSpec((d, tk), lambda i, k: (0, k)), rows, fixed(1, d), rows],
        out_specs=[rows, fixed(SUBLANE, d)],
        out_shape=[jax.ShapeDtypeStruct((s, d), F32), jax.ShapeDtypeStruct((SUBLANE, d), F32)],
        scratch_shapes=[pltpu.VMEM((tm, d), F32)],
        compiler_params=_params("arbitrary", "arbitrary"),
    )(d_proj, w_in_p, x, g, d_x2)


def _down_bwd_act(d_m, w_down, r, after=()):
    s, d = d_m.shape
    f = w_down.shape[0]
    tm, tn = min(TILE_M,s), min(TILE_N,f)
    blk = pl.BlockSpec((tm, tn), lambda i, j: (i, j))
    return _matmul("down_bwd_act", d_m, w_down, grid=(s // tm, f // tn), after=after,
                   a_spec=pl.BlockSpec((tm, d), lambda i, j: (i, 0)),
                   b_spec=pl.BlockSpec((tn, d), lambda i, j: (j, 0)),
                   out_shape=jax.ShapeDtypeStruct((s, f), BF16), out_specs=blk, contract=NT,
                   extras=(r,), extra_specs=(blk,),
                   epilogue=lambda acc, rv: (acc * (2.0 * rv.astype(F32)),))


def _up_bwd_act(d_up, w_up, after=()):
    s, _ = d_up.shape
    nb, d, fb = w_up.shape
    tm, tn = min(2 * TILE_M, s), min(TILE_N,d)
    return _matmul("up_bwd_act", d_up, w_up, grid=(s // tm, d // tn, nb), after=after,
                   a_spec=pl.BlockSpec((tm, fb), lambda i, j, k: (i, k)),
                   b_spec=pl.BlockSpec((None, tn, fb), lambda i, j, k: (k, j, 0)),
                   out_shape=jax.ShapeDtypeStruct((s, d), F32),
                   out_specs=pl.BlockSpec((tm, tn), lambda i, j, k: (i, j)),
                   contract=NT, nk=nb, acc_shape=(tm, tn))


def _half_grad(name, a, b, core, home, received, after, *, grid, a_block, a_map, b_block, b_map, o_block, o_map, out_shape):
    n_after = len(after)
    pick = (lambda ref: ref[0]) if home else (lambda ref: 1 - ref[0])

    def body(core_ref, a_ref, b_ref, *rest):
        acc = lax.dot_general(a_ref[...], b_ref[...], (TN, ((), ())), preferred_element_type=F32)
        if received is not None:
            acc = acc + rest[0][...].astype(F32)
        rest[-1][...] = acc.astype(rest[-1].dtype)

    wrap = lambda fn: (lambda i, j, core_ref: fn(i, j, pick(core_ref)))
    o_spec = pl.BlockSpec(o_block, wrap(o_map))
    extra = [] if received is None else [o_spec]
    operands = [] if received is None else [received]
    return pl.pallas_call(
        body, name=name,
        grid_spec=pltpu.PrefetchScalarGridSpec(
            num_scalar_prefetch=1, grid=grid,
            in_specs=[pl.BlockSpec(a_block, wrap(a_map)), pl.BlockSpec(b_block, wrap(b_map))] + extra + [ANY] * n_after,
            out_specs=o_spec),
        out_shape=out_shape,
        compiler_params=_params("parallel", "parallel"),
    )(core, a, b, *operands, *after)


def _down_half_grad(name, a, d_m, core, home, received=None, after=()):
    s, f = a.shape
    d = d_m.shape[1]
    r = f // N_DEV
    tn = min(TILE_N, d)
    return _half_grad(name, a, d_m, core, home, received, after, grid=(N_CHIP, d // tn),
                      a_block=(s, r), a_map=lambda k, j, p: (0, 2 * k + p),
                      b_block=(s, tn), b_map=lambda k, j, p: (0, j),
                      o_block=(None, r, tn), o_map=lambda k, j, p: (k, 0, j),
                      out_shape=jax.ShapeDtypeStruct((N_CHIP, r, d), BF16))


def _up_half_grad(name, h2, d_up, core, home, received=None, after=()):
    s, d = h2.shape
    fb = d_up.shape[1] // N_DEV
    tm = min(TILE_M, d)
    return _half_grad(name, h2, d_up, core, home, received, after, grid=(d // tm, N_CHIP),
                      a_block=(s, tm), a_map=lambda i, k, p: (0, i),
                      b_block=(s, fb), b_map=lambda i, k, p: (0, 2 * k + p),
                      o_block=(None, tm, fb), o_map=lambda i, k, p: (k, i, 0),
                      out_shape=jax.ShapeDtypeStruct((N_CHIP, d, fb), BF16))


def _in_pad(in_width):
    return -(-in_width // LANE) * LANE


def _join_col_shards(blocks):
    n, r, w = blocks.shape
    rows = min(ROWS, r)
    width = _in_pad(n * w)

    def body(x_ref, o_ref):
        tail = [jnp.zeros((rows, width - n * w), o_ref.dtype)] if width > n * w else []
        o_ref[...] = jnp.concatenate([x_ref[j] for j in range(n)] + tail, axis=1)

    return pl.pallas_call(
        body, name="join_col_shards", grid=(r // rows,),
        in_specs=[pl.BlockSpec((n, rows, w), lambda i: (0, i, 0))], out_specs=_row_spec(rows, width),
        out_shape=jax.ShapeDtypeStruct((r, width), blocks.dtype),
        compiler_params=_params("parallel"),
    )(blocks)


def _permute_q_cols(w_uq, n_heads):
    r = w_uq.shape[0]
    w3 = w_uq.reshape(r, n_heads, QK)
    return jnp.concatenate([w3[:, :, :HEAD].reshape(r, n_heads * HEAD), w3[:, :, HEAD:].reshape(r, n_heads * ROPE)], axis=1)


def _unpermute_q_rows(wt, n_heads):
    r = wt.shape[1]
    nope = wt[:n_heads * HEAD].reshape(n_heads, HEAD, r)
    rope = wt[n_heads * HEAD:].reshape(n_heads, ROPE, r)
    return jnp.concatenate([nope, rope], axis=1).reshape(n_heads * QK, r)


def _local_step(x, tgt, gains, weights, grads, first_after=()):
    pre_mix_g, q_norm_g, kv_norm_g, conv_out_g, attn_out_g, post_mix_g, pre_mlp_g, post_mlp_g = gains
    s, d = x.shape
    conv_width = conv_out_g.shape[1]
    n_groups = conv_width // HEAD
    r_q, r_kv = q_norm_g.shape[1], kv_norm_g.shape[1]
    n_heads = attn_out_g.shape[1] // HEAD
    c_q0 = 3 * conv_width
    c_kv0 = c_q0 + r_q
    c_kr0 = c_kv0 + r_kv
    in_pad = _in_pad(c_kr0 + ROPE)
    tn_in = in_pad // 5 if in_pad % (5 * LANE) == 0 else LANE
    tables = _rope_tables(s, n_heads)

    h1 = _rms_fwd("pre_mix_norm", x, pre_mix_g, after=first_after)
    weights.forward(0, (h1, *tables))
    w_in_p, conv_w = weights.ready(0, ())
    proj = _mm_nn("in_proj", h1, w_in_p, F32, 2 * TILE_M, tn_in)
    y_conv = _conv_fwd(proj, conv_w, conv_out_g, n_groups)
    qn = _rms_fwd("q_norm", proj, q_norm_g, cols=(c_q0, r_q))
    kvn = _rms_fwd("kv_norm", proj, kv_norm_g, cols=(c_kv0, r_kv))
    weights.forward(1, (y_conv, qn, kvn))
    w_uq_p, w_ukv, w_o = weights.ready(1, ())
    q = _mm_nn("q_up", qn, w_uq_p, F32, TILE_M, TILE_N)
    kv = _mm_nn("kv_up", kvn, w_ukv, F32, TILE_M, TILE_N)
    qh, kh, vh = _pack_heads(q, kv, proj, c_kr0, tables, n_heads)
    o, y_attn = _attn_fwd(qh, kh, vh, attn_out_g)
    mix = jnp.concatenate([y_conv, y_attn], axis=1)
    y = _mm_nn("out_proj", mix, w_o, F32, TILE_M, TILE_N, after=weights.forward(2, (y_attn,)))
    x2, h2 = _mid_fwd(x, y, post_mix_g, pre_mlp_g)
    (w_up,) = weights.ready(2, (h2,))
    a, r = _up_fwd(h2, w_up)
    weights.forward(3, (a,))
    (w_down,) = weights.ready(3, ())
    m = _down_fwd(a, w_down)

    d_out, d_m, dg_post_mlp, loss_part = _head(m, x2, tgt, post_mlp_g)
    core = grads.core
    away = _down_half_grad("down_bwd_w_away", a, d_m, core, home=False)
    d_up = _down_bwd_act(d_m, w_down, r, after=grads.send_away(0, away))
    sums = _down_half_grad("down_bwd_w_home", a, d_m, core, home=True, received=grads.received(0, (d_up,)))
    away = _up_half_grad("up_bwd_w_away", h2, d_up, core, home=False, after=grads.send_sums(0, (sums,)))
    d_h2 = _up_bwd_act(d_up, w_up, after=grads.send_away(1, away))
    sums = _up_half_grad("up_bwd_w_home", h2, d_up, core, home=True, received=grads.received(1, (d_h2,)))
    d_x2, d_y, dg_pre_mlp, dg_post_mix = _mid_bwd(x2, y, d_out, d_h2, pre_mlp_g, post_mix_g, after=grads.send_sums(1, (sums,)))
    d_mix = _mm_nt("out_proj_bwd_act", d_y, w_o, F32, TILE_M, TILE_N)
    gw_o = _mm_tn("out_proj_bwd_w", mix, d_y, BF16, TILE_M, TILE_N)
    dqh, dkh, dvh, dg_attn = _attn_bwd(qh, kh, vh, o, d_mix, attn_out_g, conv_width, after=grads.full(2, (gw_o,)))
    d_q, d_kv, d_kr = _unpack_heads(dqh, dkh, dvh, tables, n_heads)
    d_qn = _mm_nt("q_up_bwd_act", d_q, w_uq_p, F32, TILE_M, TILE_N)
    d_kvn = _mm_nt("kv_up_bwd_act", d_kv, w_ukv, F32, TILE_M, TILE_N)
    gw_uq_t = _mm_tn("q_up_bwd_w", d_q, qn, F32, TILE_M, TILE_N)
    gw_ukv = _mm_tn("kv_up_bwd_w", kvn, d_kv, BF16, TILE_M, TILE_N)
    d_cq, dg_q = _rms_bwd_call("q_norm_bwd", proj, q_norm_g, d_qn, BF16, cols=(c_q0, r_q), after=grads.full(3, (gw_uq_t, gw_ukv)))
    d_ckv, dg_kv = _rms_bwd_call("kv_norm_bwd", proj, kv_norm_g, d_kvn, BF16, cols=(c_kv0, r_kv))
    d_u, d_b, d_c, dg_conv, dw_conv = _conv_bwd(proj, d_mix, conv_w, conv_out_g, n_groups)
    d_proj = jnp.concatenate([d_u, d_b, d_c, d_cq, d_ckv, d_kr[:, :in_pad - c_kr0]], axis=1)
    gw_in_t = _mm_tn("in_proj_bwd_w", d_proj, h1, F32, tn_in, TILE_N)
    grad_x, dg_pre_mix = _in_proj_bwd_first(d_proj, w_in_p, x, pre_mix_g, d_x2, after=grads.full(4, (gw_in_t,)))

    small = [dg_pre_mix, dg_q, dg_kv, dg_conv, dg_attn, dg_post_mix, dg_pre_mlp, dg_post_mlp,
             dw_conv[0], dw_conv[1], dw_conv[2], loss_part]
    return grad_x, jnp.concatenate(small, axis=1)


HBM = pl.BlockSpec(memory_space=pltpu.HBM)
SEM = pl.BlockSpec(memory_space=pltpu.SEMAPHORE)
IN_VMEM = pl.BlockSpec(memory_space=pltpu.VMEM)
SPLIT = pltpu.CompilerParams(has_side_effects=pltpu.SideEffectType.DATAFLOW_SIDE_EFFECTING)


def _in_hbm(a):
    return pltpu.with_memory_space_constraint(a, pltpu.HBM)


def _hbm_like(a):
    return pltpu.HBM(a.shape, a.dtype)


def _place():
    x, y, c = lax.axis_index("x"), lax.axis_index("y"), lax.axis_index("c")
    other_chips = [(1 - x, y), (x, 1 - y), (1 - x, 1 - y)]
    return x, y, c, other_chips


def _block(px, py, pc):
    return 4 * px + 2 * py + pc


def _gather_start(name, shards, groups):
    n, ng = len(shards), len(groups)
    lands = [lax.empty((N_DEV, *a.shape), a.dtype) for a in shards]

    def body(*refs):
        src, land = refs[:n], refs[n:2 * n]
        sems, token = refs[2 * n:2 * n + 2 * ng], refs[-1]
        x, y, c, chips = _place()
        targets = [(x, y, 1 - c)] + [(*chip, c) for chip in chips]
        for gi, group in enumerate(groups):
            for i, w in enumerate(group):
                for k, to in enumerate(targets):
                    pltpu.make_async_remote_copy(
                        src_ref=src[w], dst_ref=land[w].at[_block(x, y, c)],
                        send_sem=sems[2 * gi].at[4 * i + k], recv_sem=sems[2 * gi + 1].at[4 * i + k],
                        device_id=to, device_id_type=MESH).start()
        token[...] = jnp.zeros_like(token)

    sem_shapes = [pltpu.SemaphoreType.DMA((4 * len(g),)) for g in groups for _ in range(2)]
    out = pl.pallas_call(
        body, name=name,
        in_specs=[HBM] * (2 * n),
        out_specs=[SEM] * (2 * ng) + [HBM] * (2 * n) + [IN_VMEM],
        out_shape=sem_shapes + [_hbm_like(a) for a in shards] + [_hbm_like(a) for a in lands]
        + [jax.ShapeDtypeStruct((SUBLANE, LANE), F32)],
        input_output_aliases={i: 2 * ng + i for i in range(2 * n)},
        compiler_params=SPLIT,
    )(*[_in_hbm(a) for a in shards], *[_in_hbm(a) for a in lands])
    sems = [(out[2 * gi], out[2 * gi + 1]) for gi in range(ng)]
    return sems, out[2 * ng:2 * ng + n], out[2 * ng + n:2 * ng + 2 * n], out[-1]


def _gather_forward(name, shards, lands, send1, recv1, after):
    n = len(lands)

    def body(*refs):
        src, land = refs[:n], refs[n:2 * n]
        s1, r1 = refs[2 * n], refs[2 * n + 1]
        s2, r2 = refs[2 * n + 2 + len(after)], refs[2 * n + 3 + len(after)]
        x, y, c, chips = _place()
        me, sibling = (x, y, c), (x, y, 1 - c)
        for j, chip in enumerate(chips):
            for i in range(n):
                blk = land[i].at[_block(*chip, c)]
                pltpu.make_async_remote_copy(src_ref=blk, dst_ref=blk, send_sem=s1.at[4 * i + 1 + j], recv_sem=r1.at[4 * i + 1 + j],
                                             device_id=me, device_id_type=MESH).wait_recv()
                pltpu.make_async_remote_copy(src_ref=blk, dst_ref=blk, send_sem=s2.at[3 * i + j], recv_sem=r2.at[3 * i + j],
                                             device_id=sibling, device_id_type=MESH).start()
        for i in range(n):
            blk = land[i].at[_block(x, y, 1 - c)]
            pltpu.make_async_remote_copy(src_ref=blk, dst_ref=blk, send_sem=s1.at[4 * i], recv_sem=r1.at[4 * i],
                                         device_id=me, device_id_type=MESH).wait_recv()
            for k in range(4):
                pltpu.make_async_remote_copy(src_ref=src[i], dst_ref=land[i].at[_block(x, y, c)], send_sem=s1.at[4 * i + k],
                                             recv_sem=r1.at[4 * i + k], device_id=sibling, device_id_type=MESH).wait_send()

    sem = pltpu.SemaphoreType.DMA((3 * n,))
    out = pl.pallas_call(
        body, name=name,
        in_specs=[HBM] * (2 * n) + [SEM, SEM] + [ANY] * len(after),
        out_specs=[SEM, SEM] + [HBM] * n,
        out_shape=[sem, sem] + [_hbm_like(a) for a in lands],
        input_output_aliases={n + i: 2 + i for i in range(n)},
        compiler_params=SPLIT,
    )(*shards, *lands, send1, recv1, *after)
    return (out[0], out[1]), out[2:]


def _gather_wait(name, lands, send2, recv2, after):
    n = len(lands)

    def body(*refs):
        land, s2, r2 = refs[:n], refs[n], refs[n + 1]
        x, y, c, chips = _place()
        me = (x, y, c)
        for i in range(n):
            for j, chip in enumerate(chips):
                got = land[i].at[_block(*chip, 1 - c)]
                pltpu.make_async_remote_copy(src_ref=got, dst_ref=got, send_sem=s2.at[3 * i + j], recv_sem=r2.at[3 * i + j],
                                             device_id=me, device_id_type=MESH).wait_recv()
                sent = land[i].at[_block(*chip, c)]
                pltpu.make_async_remote_copy(src_ref=sent, dst_ref=sent, send_sem=s2.at[3 * i + j], recv_sem=r2.at[3 * i + j],
                                             device_id=me, device_id_type=MESH).wait_send()

    return pl.pallas_call(
        body, name=name,
        in_specs=[HBM] * n + [SEM, SEM] + [ANY] * len(after), out_specs=[HBM] * n, out_shape=[_hbm_like(a) for a in lands],
        input_output_aliases={i: i for i in range(n)},
        compiler_params=SPLIT,
    )(*lands, send2, recv2, *after)


def _pair_exchange(name, grads, shard_rows):
    n = len(grads)
    shapes = [(g.shape[1:] if r is None else (r, g.shape[1])) for g, r in zip(grads, shard_rows)]

    def body(*refs):
        ins, recv = refs[:n], refs[n:2 * n]
        send_sems, recv_sems = refs[2 * n:]
        x, y, c, _ = _place()
        sends = []
        for w in range(n):
            for k in range(N_CHIP):
                j, r = 2 * k + 1 - c, shard_rows[w]
                src = ins[w].at[j] if r is None else ins[w].at[pl.ds(pl.multiple_of(j * r, SUBLANE), r), :]
                sends.append(pltpu.make_async_remote_copy(
                    src_ref=src, dst_ref=recv[w].at[k],
                    send_sem=send_sems.at[w, k], recv_sem=recv_sems.at[w, k],
                    device_id=(x, y, 1 - c), device_id_type=MESH))
        for cp in sends:
            cp.start()
        for cp in sends:
            cp.wait()

    return pl.pallas_call(
        body, name=name,
        in_specs=[ANY] * n, out_specs=[ANY] * n,
        out_shape=[jax.ShapeDtypeStruct((N_CHIP, *shape), g.dtype) for g, shape in zip(grads, shapes)],
        scratch_shapes=[pltpu.SemaphoreType.DMA((n, N_CHIP))] * 2,
    )(*grads)


def _pair_sum_rows(name, grad, received, core):
    _, r, c = received.shape
    tc = _fit(c, 512)

    def body(core_ref, a_ref, b_ref, o_ref):
        o_ref[...] = (a_ref[...] + b_ref[...]).astype(o_ref.dtype)

    spec = pl.BlockSpec((None, r, tc), lambda k, i, core_ref: (k, 0, i))
    return pl.pallas_call(
        body, name=name,
        grid_spec=pltpu.PrefetchScalarGridSpec(
            num_scalar_prefetch=1, grid=(N_CHIP, c // tc),
            in_specs=[pl.BlockSpec((r, tc), lambda k, i, core_ref: (2 * k + core_ref[0], i)), spec],
            out_specs=spec),
        out_shape=jax.ShapeDtypeStruct(received.shape, BF16),
        compiler_params=_params("parallel", "parallel"),
    )(core, grad, received)


def _pair_sum(name, grad, received, core):
    _, r, c = received.shape
    rows = min(ROWS, r)
    assert r % rows == 0

    def body(core_ref, a_ref, b_ref, o_ref):
        o_ref[...] = (a_ref[...].astype(F32) + b_ref[...].astype(F32)).astype(o_ref.dtype)

    spec = pl.BlockSpec((None, rows, c), lambda k, i, core_ref: (k, i, 0))
    return pl.pallas_call(
        body, name=name,
        grid_spec=pltpu.PrefetchScalarGridSpec(
            num_scalar_prefetch=1, grid=(N_CHIP, r // rows),
            in_specs=[pl.BlockSpec((None, None, rows, c), lambda k, i, core_ref: (k, core_ref[0], i, 0)), spec],
            out_specs=spec),
        out_shape=jax.ShapeDtypeStruct(received.shape, received.dtype),
        compiler_params=_params("parallel", "parallel"),
    )(core, grad.reshape(N_CHIP, 2, r, c), received)


def _pair_send_start(name, away):
    land = lax.empty(away.shape, away.dtype)

    def body(src, dst, send, recv, src_thru, dst_thru, token):
        x, y, c, _ = _place()
        pltpu.make_async_remote_copy(src_ref=src, dst_ref=dst, send_sem=send, recv_sem=recv,
                                     device_id=(x, y, 1 - c), device_id_type=MESH).start()
        token[...] = jnp.zeros_like(token)

    sem = pltpu.SemaphoreType.DMA(())
    out = pl.pallas_call(
        body, name=name,
        in_specs=[HBM, HBM], out_specs=[SEM, SEM, HBM, HBM, IN_VMEM],
        out_shape=[sem, sem, _hbm_like(away), _hbm_like(land), jax.ShapeDtypeStruct((SUBLANE, LANE), F32)],
        input_output_aliases={0: 2, 1: 3},
        compiler_params=SPLIT,
    )(_in_hbm(away), _in_hbm(land))
    return (out[0], out[1]), out[2], out[3], out[4]


def _pair_send_wait(name, sems, src, land, after):
    def body(src_ref, dst_ref, send, recv, *rest):
        x, y, c, _ = _place()
        pltpu.make_async_remote_copy(src_ref=src_ref, dst_ref=dst_ref, send_sem=send, recv_sem=recv,
                                     device_id=(x, y, 1 - c), device_id_type=MESH).wait()

    return pl.pallas_call(
        body, name=name,
        in_specs=[HBM, HBM, SEM, SEM] + [ANY] * len(after), out_specs=HBM, out_shape=_hbm_like(land),
        input_output_aliases={1: 0},
        compiler_params=SPLIT,
    )(src, land, *sems, *after)


def _chip_send_start(name, sums):
    n = len(sums)
    lands = [lax.empty(a.shape, a.dtype) for a in sums]

    def body(*refs):
        src, land = refs[:n], refs[n:2 * n]
        send, recv, token = refs[2 * n], refs[2 * n + 1], refs[-1]
        x, y, c, chips = _place()
        for w in range(n):
            for j, (px, py) in enumerate(chips):
                pltpu.make_async_remote_copy(
                    src_ref=src[w].at[2 * px + py], dst_ref=land[w].at[2 * x + y],
                    send_sem=send.at[3 * w + j], recv_sem=recv.at[3 * w + j],
                    device_id=(px, py, c), device_id_type=MESH).start()
        token[...] = jnp.zeros_like(token)

    sem = pltpu.SemaphoreType.DMA((3 * n,))
    out = pl.pallas_call(
        body, name=name,
        in_specs=[HBM] * (2 * n),
        out_specs=[SEM, SEM] + [HBM] * (2 * n) + [IN_VMEM],
        out_shape=[sem, sem] + [_hbm_like(a) for a in sums] + [_hbm_like(a) for a in lands]
        + [jax.ShapeDtypeStruct((SUBLANE, LANE), F32)],
        input_output_aliases={i: 2 + i for i in range(2 * n)},
        compiler_params=SPLIT,
    )(*[_in_hbm(a) for a in sums], *[_in_hbm(a) for a in lands])
    return (out[0], out[1]), out[2:2 + n], out[2 + n:2 + 2 * n], out[-1]


def _chip_send_wait(name, groups, after):
    counts = [len(g[1]) for g in groups]
    n = sum(counts)

    def body(*refs):
        src, land = refs[:n], refs[n:2 * n]
        sems = refs[2 * n:2 * n + 2 * len(groups)]
        x, y, c, chips = _place()
        w = 0
        for gi, count in enumerate(counts):
            for i in range(count):
                for j, (px, py) in enumerate(chips):
                    pltpu.make_async_remote_copy(
                        src_ref=src[w].at[2 * px + py], dst_ref=land[w].at[2 * px + py],
                        send_sem=sems[2 * gi].at[3 * i + j], recv_sem=sems[2 * gi + 1].at[3 * i + j],
                        device_id=(px, py, c), device_id_type=MESH).wait()
                w += 1

    sums = [a for g in groups for a in g[1]]
    lands = [a for g in groups for a in g[2]]
    sems = [s for g in groups for s in g[0]]
    return pl.pallas_call(
        body, name=name,
        in_specs=[HBM] * (2 * n) + [SEM] * len(sems) + [ANY] * len(after),
        out_specs=[HBM] * n, out_shape=[_hbm_like(a) for a in lands],
        input_output_aliases={n + i: i for i in range(n)},
        compiler_params=SPLIT,
    )(*sums, *lands, *sems, *after)


def _small_all_reduce(part, after=()):
    _, w = part.shape

    def body(p_ref, *rest):
        o_ref, buf, send_sems, recv_sems = rest[len(after):]
        x, y, c, _ = _place()
        me = 4 * x + 2 * y + c
        buf[me] = jnp.sum(p_ref[...], axis=0, keepdims=True)
        copies = []
        for k in range(1, N_DEV):
            dx, dy, dc = (k >> 2) & 1, (k >> 1) & 1, k & 1
            copies.append(pltpu.make_async_remote_copy(
                src_ref=buf.at[me], dst_ref=buf.at[me], send_sem=send_sems.at[k - 1], recv_sem=recv_sems.at[k - 1],
                device_id=(x ^ dx, y ^ dy, c ^ dc), device_id_type=MESH))
        for cp in copies:
            cp.start()
        for cp in copies:
            cp.wait()
        tot = buf[0]
        for d in range(1, N_DEV):
            tot = tot + buf[d]
        o_ref[...] = tot
        loss = jnp.sum(tot[:, w - LANE:], axis=1, keepdims=True)
        o_ref[:, w - LANE:] = jnp.broadcast_to(loss, (1, LANE))

    return pl.pallas_call(
        body, name="small_all_reduce",
        in_specs=[IN_VMEM] + [ANY] * len(after), out_specs=IN_VMEM,
        out_shape=jax.ShapeDtypeStruct((1, w), F32),
        scratch_shapes=[pltpu.VMEM((N_DEV, 1, w), F32), pltpu.SemaphoreType.DMA((N_DEV - 1,)), pltpu.SemaphoreType.DMA((N_DEV - 1,))],
        compiler_params=pltpu.CompilerParams(vmem_limit_bytes=VMEM_LIMIT_BYTES),
    )(part, *after)


def _adamw(w, g, m, v):
    m = ADAM_B1 * m + (1.0 - ADAM_B1) * g
    v = ADAM_B2 * v + (1.0 - ADAM_B2) * (g * g)
    m_hat = m / (1.0 - ADAM_B1 ** ADAM_STEP)
    v_hat = v / (1.0 - ADAM_B2 ** ADAM_STEP)
    delta = -ADAM_LR * (m_hat / (jnp.sqrt(v_hat) + ADAM_EPS) + ADAM_WD * w)
    return delta, m, v


def _sum_adam(name, parts, sums, chip, w, m, v, after=()):
    _, r, c = w.shape
    n_after = len(after)
    by_rows = r % ROWS == 0 or r < ROWS
    tr, tc = (min(ROWS, r), c) if by_rows else (r, _fit(c, 512))
    at = (lambda i: (i, 0)) if by_rows else (lambda i: (0, i))

    def body(chip_ref, p_ref, own_ref, w_ref, m_ref, v_ref, *rest):
        g_ref, d_ref, mo_ref, vo_ref = rest[n_after:]
        g = None
        for k in range(N_CHIP):
            term = jnp.where(chip_ref[0] == k, own_ref[...], p_ref[k]).astype(F32)
            g = term if g is None else g + term
        g_ref[...] = g
        d_ref[...], mo_ref[...], vo_ref[...] = _adamw(w_ref[...], g, m_ref[...], v_ref[...])

    blk = pl.BlockSpec((None, tr, tc), lambda i, chip_ref: (0, *at(i)))
    out = jax.ShapeDtypeStruct((1, r, c), F32)
    return pl.pallas_call(
        body, name=name,
        grid_spec=pltpu.PrefetchScalarGridSpec(
            num_scalar_prefetch=1, grid=(r // tr if by_rows else c // tc,),
            in_specs=[pl.BlockSpec((N_CHIP, tr, tc), lambda i, chip_ref: (0, *at(i))),
                      pl.BlockSpec((None, tr, tc), lambda i, chip_ref: (chip_ref[0], *at(i))), blk, blk, blk]
            + [ANY] * n_after,
            out_specs=[blk] * 4),
        out_shape=[out] * 4,
        compiler_params=_params("parallel"),
    )(chip, parts, sums, w, m, v, *after)


def _adam_gains(total, ws, ms, vs):
    n = len(ws)
    widths = [w.shape[1] for w in ws]

    def body(t_ref, *refs):
        w_refs, m_refs, v_refs, outs = refs[:n], refs[n:2 * n], refs[2 * n:3 * n], refs[3 * n:]
        off = 0
        for i in range(n):
            g = t_ref[:, off:off + widths[i]]
            off += widths[i]
            g_ref, d_ref, mo_ref, vo_ref = outs[4 * i:4 * i + 4]
            g_ref[...] = g
            d_ref[...], mo_ref[...], vo_ref[...] = _adamw(w_refs[i][...], g, m_refs[i][...], v_refs[i][...])

    out = pl.pallas_call(
        body, name="adam_gains",
        out_shape=[jax.ShapeDtypeStruct(w.shape, F32) for w in ws for _ in range(4)],
    )(total, *ws, *ms, *vs)
    return [tuple(out[4 * i:4 * i + 4]) for i in range(n)]


def _adam_taps(total, first_col, device, w, m, v):
    _, n_taps, cw = w.shape
    col_block = lambda t, dev: (0, first_col // cw + t * N_DEV + dev[0])
    tap = pl.BlockSpec((None, 1, cw), lambda t, dev: (t, 0, 0))

    def body(dev_ref, t_ref, w_ref, m_ref, v_ref, g_ref, d_ref, mo_ref, vo_ref):
        g = t_ref[...]
        g_ref[...] = g
        d_ref[...], mo_ref[...], vo_ref[...] = _adamw(w_ref[...], g, m_ref[...], v_ref[...])

    shape3 = (n_taps, 1, cw)
    out = pl.pallas_call(
        body, name="adam_taps",
        grid_spec=pltpu.PrefetchScalarGridSpec(
            num_scalar_prefetch=1, grid=(n_taps,),
            in_specs=[pl.BlockSpec((1, cw), col_block), tap, tap, tap], out_specs=[tap] * 4),
        out_shape=[jax.ShapeDtypeStruct(shape3, F32)] * 4,
    )(device, total, w.reshape(shape3), m.reshape(shape3), v.reshape(shape3))
    return tuple(o.reshape(w.shape) for o in out)


def kernel(x, pre_mix_g, w_in, conv_w, q_norm_g, w_uq, kv_norm_g, w_ukv, conv_out_g, attn_out_g, w_o, post_mix_g, pre_mlp_g, w_up, w_down, post_mlp_g, loss_target, m_pre_mix_g, m_w_in, m_conv_w, m_q_norm_g, m_w_uq, m_kv_norm_g, m_w_ukv, m_conv_out_g, m_attn_out_g, m_w_o, m_post_mix_g, m_pre_mlp_g, m_w_up, m_w_down, m_post_mlp_g, v_pre_mix_g, v_w_in, v_conv_w, v_q_norm_g, v_w_uq, v_kv_norm_g, v_w_ukv, v_conv_out_g, v_attn_out_g, v_w_o, v_post_mix_g, v_pre_mlp_g, v_w_up, v_w_down, v_post_mlp_g):
    me = 4 * lax.axis_index("x") + 2 * lax.axis_index("y") + lax.axis_index("c")
    core = lax.axis_index("c").astype(jnp.int32).reshape(1)
    chip = (2 * lax.axis_index("x") + lax.axis_index("y")).astype(jnp.int32).reshape(1)
    gains = (pre_mix_g, q_norm_g, kv_norm_g, conv_out_g, attn_out_g, post_mix_g, pre_mlp_g, post_mlp_g)
    gain_m = (m_pre_mix_g, m_q_norm_g, m_kv_norm_g, m_conv_out_g, m_attn_out_g, m_post_mix_g, m_pre_mlp_g, m_post_mlp_g)
    gain_v = (v_pre_mix_g, v_q_norm_g, v_kv_norm_g, v_conv_out_g, v_attn_out_g, v_post_mix_g, v_pre_mlp_g, v_post_mlp_g)
    names = ("w_in", "w_uq", "w_ukv", "w_o", "w_up", "w_down")
    big = dict(zip(names, (w_in, w_uq, w_ukv, w_o, w_up, w_down)))
    big_m = dict(zip(names, (m_w_in, m_w_uq, m_w_ukv, m_w_o, m_w_up, m_w_down)))
    big_v = dict(zip(names, (v_w_in, v_w_uq, v_w_ukv, v_w_o, v_w_up, v_w_down)))
    n_heads = attn_out_g.shape[1] // HEAD
    n_taps = conv_w.shape[1]

    gathered = ("w_in", "conv", "w_uq", "w_ukv", "w_o", "w_up", "w_down")
    gather_groups = ((0, 1), (2, 3, 4), (5,), (6,))
    taps = jnp.pad(conv_w[0], ((0, SUBLANE - n_taps), (0, 0)))
    sems_a, shards_a, lands_a, token_a = _gather_start("gather_start_first", [w_in[0].astype(BF16), taps], ((0, 1),))
    behind = token_a[0, 0]
    sems_b, shards_b, lands_b, token = _gather_start(
        "gather_start_rest", [(big[nm][0] + behind).astype(BF16) for nm in gathered[2:]], ((0, 1, 2), (3,), (4,)))
    sems1, shards, lands = sems_a + sems_b, [*shards_a, *shards_b], [*lands_a, *lands_b]

    cols = lambda a: jnp.concatenate([a[j] for j in range(N_DEV)], axis=1)
    rows = lambda a: a.reshape(N_DEV * a.shape[1], a.shape[2])
    ready = {
        "w_in": _join_col_shards,
        "conv": lambda a: cols(a)[:n_taps],
        "w_uq": lambda a: _permute_q_cols(cols(a), n_heads),
        "w_ukv": cols, "w_o": rows, "w_up": lambda a: a, "w_down": rows,
    }

    class Weights:
        def __init__(self):
            self.passed = {}

        def forward(self, group, after):
            idx = gather_groups[group]
            self.passed[group] = _gather_forward(f"gather_forward_{group}", [shards[i] for i in idx], [lands[i] for i in idx],
                                                 *sems1[group], after)
            return tuple(self.passed[group][1])

        def ready(self, group, after):
            sems2, mid = self.passed[group]
            full = _gather_wait(f"gather_wait_{group}", mid, *sems2, after)
            out = []
            for i, a in zip(gather_groups[group], full):
                a = lax.dynamic_update_index_in_dim(a, shards[i], me, 0)
                out.append(ready[gathered[i]](a))
            return out

    weights = Weights()

    col_blocks = lambda g: g.reshape(g.shape[0], N_DEV, g.shape[1] // N_DEV).transpose(1, 0, 2)
    row_blocks = lambda g: g.reshape(N_DEV, g.shape[0] // N_DEV, g.shape[1])
    grad_groups = (("w_down",), ("w_up",), ("w_o",), ("w_uq", "w_ukv"), ("w_in",))
    transposed = {"w_in": w_in.shape[2], "w_uq": w_uq.shape[2]}
    to_blocks = {
        "w_in": lambda g: g, "w_uq": lambda g: _unpermute_q_rows(g, n_heads),
        "w_ukv": col_blocks, "w_o": row_blocks, "w_up": lambda g: g, "w_down": row_blocks,
    }
    in_flight = []

    class Grads:
        def __init__(self):
            self.core = core
            self.away = {}

        def send_sums(self, group, sums):
            sems, sums, parts, tok = _chip_send_start(f"chip_send_start_{group}", list(sums))
            in_flight.append((sems, sums, parts))
            return (tok,)

        def full(self, group, arrays):
            nms = grad_groups[group]
            blocks = [to_blocks[nm](g) for nm, g in zip(nms, arrays)]
            received = _pair_exchange(f"pair_exchange_{group}", blocks, [transposed.get(nm) for nm in nms])
            sums = [(_pair_sum_rows if nm in transposed else _pair_sum)(f"pair_sum_{nm}", g, r, core)
                    for nm, g, r in zip(nms, blocks, received)]
            return self.send_sums(group, sums)

        def send_away(self, group, half):
            sems, src, land, tok = _pair_send_start(f"pair_send_start_{group}", half)
            self.away[group] = (sems, src, land)
            return (tok,)

        def received(self, group, after):
            sems, src, land = self.away[group]
            return _pair_send_wait(f"pair_send_wait_{group}", sems, src, land, after)

    grad_x, small = _local_step(x[0], loss_target[0], gains, weights, Grads(), first_after=(token,))

    big_out = {}

    def update(tag, first, last, after):
        groups = in_flight[first:last]
        parts = _chip_send_wait("chip_send_wait_" + tag, groups, after)
        nms = [nm for grp in grad_groups[first:last] for nm in grp]
        sums = [a for _, s, _ in groups for a in s]
        for nm, p, s in zip(nms, parts, sums):
            view = (lambda a: jnp.swapaxes(a, 1, 2)) if nm in transposed else (lambda a: a)
            out = _sum_adam("adam_" + nm, p, s, chip, view(big[nm]), view(big_m[nm]), view(big_v[nm]), after=after)
            after = (out[0],)
            big_out[nm] = [view(o) for o in out]
        return after

    after = update("early", 0, len(in_flight) - 1, (grad_x,))
    total = _small_all_reduce(small, after=after)
    update("late", len(in_flight) - 1, len(in_flight), (total,))
    big_out = [big_out[nm] for nm in names]

    gain_out = _adam_gains(total, gains, gain_m, gain_v)
    taps_out = _adam_taps(total, sum(g.shape[1] for g in gains), me.astype(jnp.int32).reshape(1), conv_w, m_conv_w, v_conv_w)
    loss = total[0, total.shape[1] - 1]

    order = (0, "w_in", "conv", 1, "w_uq", 2, "w_ukv", 3, 4, "w_o", 5, 6, "w_up", "w_down", 7)
    by_name = dict(zip(names, big_out))
    outs = [loss, grad_x[None]]
    for kind in range(4):
        for item in order:
            if item == "conv":
                outs.append(taps_out[kind])
            elif isinstance(item, int):
                outs.append(gain_out[item][kind])
            else:
                outs.append(by_name[item][kind])
    return tuple(outs)
```

```python
import math

import jax
import jax.numpy as jnp
from jax import lax
from jax.experimental import pallas as pl
from jax.experimental.pallas import tpu as pltpu

F32 = jnp.float32
BF16 = jnp.bfloat16

EPS = 1e-6
NEG_INF = -1e30
HEAD = 128
ROPE = 64
QK = HEAD + ROPE
CHUNK = 64
ROPE_THETA = 10000.0
ADAM_LR, ADAM_B1, ADAM_B2, ADAM_EPS, ADAM_WD, ADAM_STEP = 0.001, 0.9, 0.999, 1e-08, 0.01, 10

LANE = 128
SUBLANE = 8
VMEM_LIMIT_BYTES = 56 * 1024 * 1024

N_DEV = 8
N_CHIP = 4
MESH = pl.DeviceIdType.MESH


def _params(*sem):
    return pltpu.CompilerParams(dimension_semantics=sem, vmem_limit_bytes=VMEM_LIMIT_BYTES)


ANY = pl.BlockSpec(memory_space=pl.ANY)


def _call(body, *, in_specs, after=(), **kw):
    n_in, n_after = len(in_specs), len(after)

    def ordered(*refs):
        body(*refs[:n_in], *refs[n_in + n_after:])

    call = pl.pallas_call(ordered, in_specs=[*in_specs, *[ANY] * n_after], **kw)
    return lambda *operands: call(*operands, *after)


def _sublane_sum(v):
    r, w = v.shape
    return jnp.sum(v.reshape(r // SUBLANE, SUBLANE, w), axis=0)


def _rstd(x):
    return lax.rsqrt(jnp.mean(x * x, axis=-1, keepdims=True) + EPS)


def _rms_bwd(x, g, dy):
    r = _rstd(x)
    xh = x * r
    dxh = dy * g
    dx = r * (dxh - xh * jnp.mean(dxh * xh, axis=-1, keepdims=True))
    return dx, dy * xh


def _accumulate(ref, val, step):
    @pl.when(step == 0)
    def _():
        ref[...] = val

    @pl.when(step > 0)
    def _():
        ref[...] += val


NN = ((1,), (0,))
NT = ((1,), (1,))
TN = ((0,), (0,))


def _matmul(name, a, b, *, grid, a_spec, b_spec, out_shape, out_specs, contract, nk=1, acc_shape=None,
            extras=(), extra_specs=(), epilogue=None, after=()):
    multi = isinstance(out_shape, (tuple, list))
    out_shapes = tuple(out_shape) if multi else (out_shape,)
    n_out = len(out_shapes)
    n_extra = len(extras)

    def body(a_ref, b_ref, *rest):
        x_refs = rest[:n_extra]
        o_refs = rest[n_extra:n_extra + n_out]

        def emit(acc):
            vals = epilogue(acc, *[r[...] for r in x_refs]) if epilogue else (acc,)
            for r, v in zip(o_refs, vals):
                r[...] = v.astype(r.dtype)

        p = lax.dot_general(a_ref[...], b_ref[...], (contract, ((), ())), preferred_element_type=F32)
        if nk == 1:
            emit(p)
        else:
            acc_ref = rest[n_extra + n_out]
            k = pl.program_id(2)
            _accumulate(acc_ref, p, k)

            @pl.when(k == nk - 1)
            def _():
                emit(acc_ref[...])

    sem = ("parallel", "parallel") + (("arbitrary",) if nk > 1 else ())
    return _call(
        body, name=name, grid=grid, after=after,
        in_specs=[a_spec, b_spec, *extra_specs],
        out_specs=out_specs,
        out_shape=out_shape,
        scratch_shapes=[pltpu.VMEM(acc_shape, F32)] if nk > 1 else [],
        compiler_params=_params(*sem),
    )(a, b, *extras)


def _fit(n, tile):
    if n <= tile:
        return n
    t = tile - tile % LANE
    while n % t:
        t -= LANE
    return t


def _mm_nn(name, a, b, out_dtype, tm, tn, after=()):
    m, k = a.shape
    n = b.shape[1]
    tm, tn = _fit(m, tm), _fit(n, tn)
    return _matmul(name, a, b, grid=(m // tm, n // tn), after=after,
                   a_spec=pl.BlockSpec((tm, k), lambda i, j: (i, 0)),
                   b_spec=pl.BlockSpec((k, tn), lambda i, j: (0, j)),
                   out_shape=jax.ShapeDtypeStruct((m, n), out_dtype),
                   out_specs=pl.BlockSpec((tm, tn), lambda i, j: (i, j)), contract=NN)


def _mm_nt(name, a, b, out_dtype, tm, tn, after=()):
    m, k = a.shape
    n = b.shape[0]
    tm, tn = _fit(m, tm), _fit(n, tn)
    return _matmul(name, a, b, grid=(m // tm, n // tn), after=after,
                   a_spec=pl.BlockSpec((tm, k), lambda i, j: (i, 0)),
                   b_spec=pl.BlockSpec((tn, k), lambda i, j: (j, 0)),
                   out_shape=jax.ShapeDtypeStruct((m, n), out_dtype),
                   out_specs=pl.BlockSpec((tm, tn), lambda i, j: (i, j)), contract=NT)


def _mm_tn(name, a, b, out_dtype, tm, tn):
    s, m = a.shape
    n = b.shape[1]
    tm, tn = _fit(m, tm), _fit(n, tn)
    return _matmul(name, a, b, grid=(m // tm, n // tn),
                   a_spec=pl.BlockSpec((s, tm), lambda i, j: (0, i)),
                   b_spec=pl.BlockSpec((s, tn), lambda i, j: (0, j)),
                   out_shape=jax.ShapeDtypeStruct((m, n), out_dtype),
                   out_specs=pl.BlockSpec((tm, tn), lambda i, j: (i, j)), contract=TN)


ROWS = 256


def _row_spec(rows, width):
    return pl.BlockSpec((rows, width), lambda i: (i, 0))


def _fixed_spec(rows, width):
    return pl.BlockSpec((rows, width), lambda i: (0, 0))


def _column_pieces(rows, start, width):
    piece = math.gcd(start, width)
    assert piece % LANE == 0
    return [pl.BlockSpec((rows, piece), lambda i, b=start // piece + p: (i, b)) for p in range(width // piece)]


def _rms_fwd(name, x, g, cols=None, after=()):
    s = x.shape[0]
    start, w = cols or (0, x.shape[1])
    rows = min(ROWS, s)
    pieces = _column_pieces(rows, start, w) if cols else [_row_spec(rows, w)]
    n = len(pieces)

    def body(*refs):
        g_ref, o_ref = refs[n:]
        xv = refs[0][...] if n == 1 else jnp.concatenate([r[...] for r in refs[:n]], axis=1)
        o_ref[...] = (xv * _rstd(xv) * g_ref[...]).astype(o_ref.dtype)

    return _call(
        body, name=name, grid=(s // rows,), after=after,
        in_specs=[*pieces, _fixed_spec(1, w)],
        out_specs=_row_spec(rows, w),
        out_shape=jax.ShapeDtypeStruct((s, w), BF16),
        compiler_params=_params("parallel"),
    )(*[x] * n, g)


def _rms_bwd_call(name, x, g, dy, out_dtype, cols=None, after=()):
    s = x.shape[0]
    start, w = cols or (0, x.shape[1])
    rows = min(ROWS, s)
    pieces = _column_pieces(rows, start, w) if cols else [_row_spec(rows, w)]
    n = len(pieces)

    def body(*refs):
        g_ref, dy_ref, dx_ref, dg_ref = refs[n:]
        xv = refs[0][...] if n == 1 else jnp.concatenate([r[...] for r in refs[:n]], axis=1)
        dx, dgc = _rms_bwd(xv, g_ref[...], dy_ref[...].astype(F32))
        dx_ref[...] = dx.astype(dx_ref.dtype)
        _accumulate(dg_ref, _sublane_sum(dgc), pl.program_id(0))

    return _call(
        body, name=name, grid=(s // rows,), after=after,
        in_specs=[*pieces, _fixed_spec(1, w), _row_spec(rows, w)],
        out_specs=[_row_spec(rows, w), _fixed_spec(SUBLANE, w)],
        out_shape=[jax.ShapeDtypeStruct((s, w), out_dtype), jax.ShapeDtypeStruct((SUBLANE, w), F32)],
        compiler_params=_params("arbitrary"),
    )(*[x] * n, g, dy)


def _mid_fwd(x, y, g_post, g_pre, after=()):
    s, w = x.shape
    rows = min(ROWS, s)

    def body(x_ref, y_ref, gp_ref, gq_ref, x2_ref, h2_ref):
        yv = y_ref[...]
        x2 = x_ref[...] + yv * _rstd(yv) * gp_ref[...]
        x2_ref[...] = x2
        h2_ref[...] = (x2 * _rstd(x2) * gq_ref[...]).astype(h2_ref.dtype)

    return _call(
        body, name="mid_fwd", grid=(s // rows,), after=after,
        in_specs=[_row_spec(rows, w), _row_spec(rows, w), _fixed_spec(1, w), _fixed_spec(1, w)],
        out_specs=[_row_spec(rows, w), _row_spec(rows, w)],
        out_shape=[jax.ShapeDtypeStruct((s, w), F32), jax.ShapeDtypeStruct((s, w), BF16)],
        compiler_params=_params("parallel"),
    )(x, y, g_post, g_pre)


def _head(m, x2, tgt, g):
    s, w = m.shape
    rows = min(ROWS, s)

    def body(m_ref, x2_ref, t_ref, g_ref, dout_ref, dm_ref, dg_ref, loss_ref):
        mv = m_ref[...]
        gv = g_ref[...]
        out = x2_ref[...] + mv * _rstd(mv) * gv
        err = out - t_ref[...]
        dout = err * (1.0 / w)
        dout_ref[...] = dout
        dm, dgc = _rms_bwd(mv, gv, dout)
        dm_ref[...] = dm.astype(dm_ref.dtype)
        sq = err * err
        lanes = sq[:, 0:LANE]
        for j in range(1, w // LANE):
            lanes = lanes + sq[:, j * LANE:(j + 1) * LANE]
        step = pl.program_id(0)
        _accumulate(dg_ref, _sublane_sum(dgc), step)
        _accumulate(loss_ref, _sublane_sum(lanes) * (0.5 / w), step)

    return pl.pallas_call(
        body, name="head", grid=(s // rows,),
        in_specs=[_row_spec(rows, w), _row_spec(rows, w), _row_spec(rows, w), _fixed_spec(1, w)],
        out_specs=[_row_spec(rows, w), _row_spec(rows, w), _fixed_spec(SUBLANE, w), _fixed_spec(SUBLANE, LANE)],
        out_shape=[jax.ShapeDtypeStruct((s, w), F32), jax.ShapeDtypeStruct((s, w), BF16),
                   jax.ShapeDtypeStruct((SUBLANE, w), F32), jax.ShapeDtypeStruct((SUBLANE, LANE), F32)],
        compiler_params=_params("arbitrary"),
    )(m, x2, tgt, g)


def _mid_bwd(x2, y, d_out, d_h2, g_pre, g_post, after=()):
    s, w = x2.shape
    rows = min(ROWS, s)

    def body(x2_ref, y_ref, dout_ref, dh2_ref, gq_ref, gp_ref, dx2_ref, dy_ref, dgq_ref, dgp_ref):
        dx, dgq = _rms_bwd(x2_ref[...], gq_ref[...], dh2_ref[...])
        dx2 = dout_ref[...] + dx
        dx2_ref[...] = dx2
        dy, dgp = _rms_bwd(y_ref[...], gp_ref[...], dx2)
        dy_ref[...] = dy.astype(dy_ref.dtype)
        step = pl.program_id(0)
        _accumulate(dgq_ref, _sublane_sum(dgq), step)
        _accumulate(dgp_ref, _sublane_sum(dgp), step)

    return _call(
        body, name="mid_bwd", grid=(s // rows,), after=after,
        in_specs=[_row_spec(rows, w)] * 4 + [_fixed_spec(1, w)] * 2,
        out_specs=[_row_spec(rows, w), _row_spec(rows, w), _fixed_spec(SUBLANE, w), _fixed_spec(SUBLANE, w)],
        out_shape=[jax.ShapeDtypeStruct((s, w), F32), jax.ShapeDtypeStruct((s, w), BF16),
                   jax.ShapeDtypeStruct((SUBLANE, w), F32), jax.ShapeDtypeStruct((SUBLANE, w), F32)],
        compiler_params=_params("arbitrary"),
    )(x2, y, d_out, d_h2, g_pre, g_post)


def _first_bwd(x, g, d_h1, d_x2, after=()):
    s, w = x.shape
    rows = min(ROWS, s)

    def body(x_ref, g_ref, dh_ref, dx2_ref, dx_ref, dg_ref):
        dx, dgc = _rms_bwd(x_ref[...], g_ref[...], dh_ref[...])
        dx_ref[...] = dx2_ref[...] + dx
        _accumulate(dg_ref, _sublane_sum(dgc), pl.program_id(0))

    return _call(
        body, name="first_bwd", grid=(s // rows,), after=after,
        in_specs=[_row_spec(rows, w), _fixed_spec(1, w), _row_spec(rows, w), _row_spec(rows, w)],
        out_specs=[_row_spec(rows, w), _fixed_spec(SUBLANE, w)],
        out_shape=[jax.ShapeDtypeStruct((s, w), F32), jax.ShapeDtypeStruct((SUBLANE, w), F32)],
        compiler_params=_params("arbitrary"),
    )(x, g, d_h1, d_x2)


def _shift_down(v, k):
    t = lax.broadcasted_iota(jnp.int32, v.shape, 0)
    return jnp.where(t >= k, pltpu.roll(v, k, 0), 0.0)


def _shift_up(v, k):
    n = v.shape[0]
    t = lax.broadcasted_iota(jnp.int32, v.shape, 0)
    return jnp.where(t < n - k, pltpu.roll(v, n - k, 0), 0.0)


def _conv_core(u, b, c, w):
    z = c * u
    conv = w[0:1, :] * _shift_down(z, 2) + w[1:2, :] * _shift_down(z, 1) + w[2:3, :] * z
    return z, conv, b * conv


def _conv_fwd(proj, conv_w, g, n_groups):
    s = proj.shape[0]

    def body(u_ref, b_ref, c_ref, w_ref, g_ref, o_ref):
        _, _, yr = _conv_core(u_ref[...], b_ref[...], c_ref[...], w_ref[...])
        o_ref[...] = (yr * _rstd(yr) * g_ref[...]).astype(o_ref.dtype)

    col = lambda k: pl.BlockSpec((s, HEAD), lambda i: (0, k * n_groups + i))
    return pl.pallas_call(
        body, name="conv_fwd", grid=(n_groups,),
        in_specs=[col(0), col(1), col(2), pl.BlockSpec((3, HEAD), lambda i: (0, i)), pl.BlockSpec((1, HEAD), lambda i: (0, i))],
        out_specs=pl.BlockSpec((s, HEAD), lambda i: (0, i)),
        out_shape=jax.ShapeDtypeStruct((s, n_groups * HEAD), BF16),
        compiler_params=_params("parallel"),
    )(proj, proj, proj, conv_w, g)


def _conv_bwd(proj, d_mix, conv_w, g, n_groups):
    s = proj.shape[0]
    width = n_groups * HEAD

    def body(u_ref, b_ref, c_ref, dy_ref, w_ref, g_ref, du_ref, db_ref, dc_ref, dg_ref, dw_ref):
        u, b, c, w = u_ref[...], b_ref[...], c_ref[...], w_ref[...]
        z, conv, yr = _conv_core(u, b, c, w)
        dyr, dgc = _rms_bwd(yr, g_ref[...], dy_ref[...])
        dconv = dyr * b
        db_ref[...] = (dyr * conv).astype(db_ref.dtype)
        dz = w[2:3, :] * dconv + w[1:2, :] * _shift_up(dconv, 1) + w[0:1, :] * _shift_up(dconv, 2)
        dc_ref[...] = (dz * u).astype(dc_ref.dtype)
        du_ref[...] = (dz * c).astype(du_ref.dtype)
        dg_ref[...] = _sublane_sum(dgc)
        dw_ref[0] = _sublane_sum(dconv * _shift_down(z, 2))
        dw_ref[1] = _sublane_sum(dconv * _shift_down(z, 1))
        dw_ref[2] = _sublane_sum(dconv * z)

    col = lambda k: pl.BlockSpec((s, HEAD), lambda i: (0, k * n_groups + i))
    grp = pl.BlockSpec((s, HEAD), lambda i: (0, i))
    return pl.pallas_call(
        body, name="conv_bwd", grid=(n_groups,),
        in_specs=[col(0), col(1), col(2), grp, pl.BlockSpec((3, HEAD), lambda i: (0, i)), pl.BlockSpec((1, HEAD), lambda i: (0, i))],
        out_specs=[grp, grp, grp, pl.BlockSpec((SUBLANE, HEAD), lambda i: (0, i)),
                   pl.BlockSpec((3, SUBLANE, HEAD), lambda i: (0, 0, i))],
        out_shape=[jax.ShapeDtypeStruct((s, width), BF16)] * 3
        + [jax.ShapeDtypeStruct((SUBLANE, width), F32), jax.ShapeDtypeStruct((3, SUBLANE, width), F32)],
        compiler_params=_params("parallel"),
    )(proj, proj, proj, d_mix, conv_w, g)


def _rope_tables(s, n_heads):
    pos = jnp.arange(s, dtype=F32)
    inv_freq = jnp.power(ROPE_THETA, -jnp.arange(0, ROPE, 2, dtype=F32) / ROPE)
    ang = pos[:, None] * inv_freq[None, :]
    cos, sin = jnp.cos(ang), jnp.sin(ang)
    cs = jnp.concatenate([cos, cos], axis=1)
    sn = jnp.concatenate([-sin, sin], axis=1)
    pad = jnp.zeros((s, LANE - ROPE), F32)
    return (jnp.tile(cs, (1, n_heads)), jnp.tile(sn, (1, n_heads)),
            jnp.concatenate([cs, pad], axis=1), jnp.concatenate([sn, pad], axis=1))


def _swap_halves(v):
    w = v.shape[1]
    lane = lax.broadcasted_iota(jnp.int32, v.shape, 1)
    first = (lane % ROPE) < (ROPE // 2)
    return jnp.where(first, pltpu.roll(v, w - ROPE // 2, 1), pltpu.roll(v, ROPE // 2, 1))


def _pack_heads(q, kv, proj, kr_col, tables, n_heads, after=()):
    s = q.shape[0]
    rows = min(ROWS, s)
    cq, sq, ck, sk = tables
    wq = n_heads * ROPE

    def body(q_ref, kv_ref, kr_ref, cq_ref, sq_ref, ck_ref, sk_ref, qo_ref, ko_ref, vo_ref):
        qr = q_ref[:, n_heads * HEAD:]
        qr = qr * cq_ref[...] + _swap_halves(qr) * sq_ref[...]
        krv = kr_ref[...]
        krv = krv * ck_ref[...] + _swap_halves(krv) * sk_ref[...]
        for h in range(n_heads):
            qo_ref[h] = jnp.concatenate([q_ref[:, h * HEAD:(h + 1) * HEAD], qr[:, h * ROPE:(h + 1) * ROPE]], axis=1).astype(BF16)
            ko_ref[h] = jnp.concatenate([kv_ref[:, 2 * h * HEAD:(2 * h + 1) * HEAD], krv[:, :ROPE]], axis=1).astype(BF16)
            vo_ref[h] = kv_ref[:, (2 * h + 1) * HEAD:(2 * h + 2) * HEAD].astype(BF16)

    hs = lambda w: pl.BlockSpec((n_heads, rows, w), lambda i: (0, i, 0))
    return _call(
        body, name="pack_heads", grid=(s // rows,), after=after,
        in_specs=[_row_spec(rows, q.shape[1]), _row_spec(rows, kv.shape[1]), pl.BlockSpec((rows, LANE), lambda i: (i, kr_col // LANE)),
                  _row_spec(rows, wq), _row_spec(rows, wq), _row_spec(rows, LANE), _row_spec(rows, LANE)],
        out_specs=[hs(QK), hs(QK), hs(HEAD)],
        out_shape=[jax.ShapeDtypeStruct((n_heads, s, QK), BF16), jax.ShapeDtypeStruct((n_heads, s, QK), BF16),
                   jax.ShapeDtypeStruct((n_heads, s, HEAD), BF16)],
        compiler_params=_params("parallel"),
    )(q, kv, proj, cq, sq, ck, sk)


def _unpack_heads(dq, dk, dv, tables, n_heads):
    s = dq.shape[1]
    rows = min(ROWS, s)
    cq, sq, ck, sk = tables
    wq = n_heads * ROPE

    def body(dq_ref, dk_ref, dv_ref, cq_ref, sq_ref, ck_ref, sk_ref, qo_ref, kvo_ref, kro_ref):
        dqr = jnp.concatenate([dq_ref[h][:, HEAD:] for h in range(n_heads)], axis=1)
        dqr = dqr * cq_ref[...] - _swap_halves(dqr) * sq_ref[...]
        dkr = dk_ref[0][:, HEAD:]
        for h in range(1, n_heads):
            dkr = dkr + dk_ref[h][:, HEAD:]
        dkr = jnp.concatenate([dkr, jnp.zeros((rows, LANE - ROPE), F32)], axis=1)
        dkr = dkr * ck_ref[...] - _swap_halves(dkr) * sk_ref[...]
        kro_ref[...] = dkr.astype(kro_ref.dtype)
        qo_ref[:, n_heads * HEAD:] = dqr.astype(qo_ref.dtype)
        for h in range(n_heads):
            qo_ref[:, h * HEAD:(h + 1) * HEAD] = dq_ref[h][:, :HEAD].astype(qo_ref.dtype)
            kvo_ref[:, 2 * h * HEAD:(2 * h + 1) * HEAD] = dk_ref[h][:, :HEAD].astype(kvo_ref.dtype)
            kvo_ref[:, (2 * h + 1) * HEAD:(2 * h + 2) * HEAD] = dv_ref[h].astype(kvo_ref.dtype)

    hs = lambda w: pl.BlockSpec((n_heads, rows, w), lambda i: (0, i, 0))
    return pl.pallas_call(
        body, name="unpack_heads", grid=(s // rows,),
        in_specs=[hs(QK), hs(QK), hs(HEAD), _row_spec(rows, wq), _row_spec(rows, wq), _row_spec(rows, LANE), _row_spec(rows, LANE)],
        out_specs=[_row_spec(rows, n_heads * QK), _row_spec(rows, 2 * n_heads * HEAD), _row_spec(rows, LANE)],
        out_shape=[jax.ShapeDtypeStruct((s, n_heads * QK), BF16), jax.ShapeDtypeStruct((s, 2 * n_heads * HEAD), BF16),
                   jax.ShapeDtypeStruct((s, LANE), BF16)],
        compiler_params=_params("parallel"),
    )(dq, dk, dv, cq, sq, ck, sk)


TQ = 256


LOG2_E = 1.4426950408889634


def _softmax_parts(q, k):
    tq, n_keys = q.shape[0], k.shape[0]
    sc = lax.dot_general(q, k, (NT, ((), ())), preferred_element_type=F32) * (QK ** -0.5 * LOG2_E)
    row = lax.broadcasted_iota(jnp.int32, (tq, tq), 0)
    col = lax.broadcasted_iota(jnp.int32, (tq, tq), 1)
    own = jnp.where(col // CHUNK <= row // CHUNK, sc[:, n_keys - tq:], NEG_INF)
    sc = own if n_keys == tq else jnp.concatenate([sc[:, :n_keys - tq], own], axis=1)
    e = jnp.exp2(sc - jnp.max(sc, axis=-1, keepdims=True))
    return e, 1.0 / jnp.sum(e, axis=-1, keepdims=True)


def _attn_fwd(q, k, v, g):
    n_heads, s, _ = q.shape
    tq = min(TQ, s)
    assert tq % CHUNK == 0 and s % tq == 0

    def body(q_ref, k_ref, v_ref, g_ref, o_ref, y_ref):
        for c in range(s // tq):
            rows, n_keys = pl.ds(c * tq, tq), (c + 1) * tq
            e, inv = _softmax_parts(q_ref[rows, :], k_ref[0:n_keys, :])
            o = jnp.dot(e.astype(BF16), v_ref[0:n_keys, :], preferred_element_type=F32) * inv
            o_ref[rows, :] = o
            y_ref[rows, :] = (o * _rstd(o) * g_ref[...]).astype(y_ref.dtype)

    head = lambda w: pl.BlockSpec((None, s, w), lambda h: (h, 0, 0))
    return pl.pallas_call(
        body, name="attn_fwd", grid=(n_heads,),
        in_specs=[head(QK), head(QK), head(HEAD), pl.BlockSpec((1, HEAD), lambda h: (0, h))],
        out_specs=[head(HEAD), pl.BlockSpec((s, HEAD), lambda h: (0, h))],
        out_shape=[jax.ShapeDtypeStruct((n_heads, s, HEAD), F32), jax.ShapeDtypeStruct((s, n_heads * HEAD), BF16)],
        compiler_params=_params("parallel"),
    )(q, k, v, g)


def _attn_bwd(q, k, v, o, d_mix, g, col0, after=()):
    n_heads, s, _ = q.shape
    tq = min(TQ, s)

    def body(q_ref, k_ref, v_ref, o_ref, dy_ref, g_ref, dq_ref, dk_ref, dv_ref, dg_ref):
        dg = None
        for c in reversed(range(s // tq)):
            rows, n_keys = pl.ds(c * tq, tq), (c + 1) * tq
            qv, kv_, vv = q_ref[rows, :], k_ref[0:n_keys, :], v_ref[0:n_keys, :]
            do, dgc = _rms_bwd(o_ref[rows, :], g_ref[...], dy_ref[rows, :])
            do = do.astype(BF16)
            dg = _sublane_sum(dgc) if dg is None else dg + _sublane_sum(dgc)
            e, inv = _softmax_parts(qv, kv_)
            p = e * inv
            dp = lax.dot_general(do, vv, (NT, ((), ())), preferred_element_type=F32)
            ds = (p * (dp - jnp.sum(p * dp, axis=-1, keepdims=True)) * (QK ** -0.5)).astype(BF16)
            dq_ref[rows, :] = jnp.dot(ds, kv_, preferred_element_type=F32)
            dk = lax.dot_general(ds, qv, (TN, ((), ())), preferred_element_type=F32)
            dv = lax.dot_general(p.astype(BF16), do, (TN, ((), ())), preferred_element_type=F32)
            if n_keys == s:
                dk_ref[...] = dk
                dv_ref[...] = dv
            else:
                dk_ref[0:n_keys, :] += dk
                dv_ref[0:n_keys, :] += dv
        dg_ref[...] = dg

    c0 = col0 // HEAD
    head = lambda w: pl.BlockSpec((None, s, w), lambda h: (h, 0, 0))
    return _call(
        body, name="attn_bwd", grid=(n_heads,), after=after,
        in_specs=[head(QK), head(QK), head(HEAD), head(HEAD), pl.BlockSpec((s, HEAD), lambda h: (0, c0 + h)),
                  pl.BlockSpec((1, HEAD), lambda h: (0, h))],
        out_specs=[head(QK), head(QK), head(HEAD), pl.BlockSpec((SUBLANE, HEAD), lambda h: (0, h))],
        out_shape=[jax.ShapeDtypeStruct((n_heads, s, QK), F32), jax.ShapeDtypeStruct((n_heads, s, QK), F32),
                   jax.ShapeDtypeStruct((n_heads, s, HEAD), F32), jax.ShapeDtypeStruct((SUBLANE, n_heads * HEAD), F32)],
        compiler_params=_params("parallel"),
    )(q, k, v, o, d_mix, g)


TILE_M = 1024
TILE_N = 1024


def _up_fwd(h2, w_up):
    s, d = h2.shape
    nb, _, fb = w_up.shape
    tm = min(TILE_M,s)

    def epilogue(acc):
        r = jnp.maximum(acc, 0.0)
        return r * r, r

    blk = pl.BlockSpec((tm, fb), lambda i, j: (i, j))
    return _matmul("up_fwd", h2, w_up, grid=(s // tm, nb),
                   a_spec=pl.BlockSpec((tm, d), lambda i, j: (i, 0)),
                   b_spec=pl.BlockSpec((None, d, fb), lambda i, j: (j, 0, 0)),
                   out_shape=[jax.ShapeDtypeStruct((s, nb * fb), BF16)] * 2, out_specs=[blk, blk],
                   contract=NN, epilogue=epilogue)


def _down_fwd(a, w_down):
    s, f = a.shape
    d = w_down.shape[1]
    tm, tn, tk = min(TILE_M,s), min(TILE_N,d), 2048
    nk = f // tk
    return _matmul("down_fwd", a, w_down, grid=(s // tm, d // tn, nk),
                   a_spec=pl.BlockSpec((tm, tk), lambda i, j, k: (i, k)),
                   b_spec=pl.BlockSpec((tk, tn), lambda i, j, k: (k, j)),
                   out_shape=jax.ShapeDtypeStruct((s, d), F32),
                   out_specs=pl.BlockSpec((tm, tn), lambda i, j, k: (i, j)),
                   contract=NN, nk=nk, acc_shape=(tm, tn))


def _down_bwd_act(d_m, w_down, r, after=()):
    s, d = d_m.shape
    f = w_down.shape[0]
    tm, tn = min(TILE_M,s), min(TILE_N,f)
    blk = pl.BlockSpec((tm, tn), lambda i, j: (i, j))
    return _matmul("down_bwd_act", d_m, w_down, grid=(s // tm, f // tn), after=after,
                   a_spec=pl.BlockSpec((tm, d), lambda i, j: (i, 0)),
                   b_spec=pl.BlockSpec((tn, d), lambda i, j: (j, 0)),
                   out_shape=jax.ShapeDtypeStruct((s, f), BF16), out_specs=blk, contract=NT,
                   extras=(r,), extra_specs=(blk,),
                   epilogue=lambda acc, rv: (acc * (2.0 * rv.astype(F32)),))


def _up_bwd_act(d_up, w_up, after=()):
    s, _ = d_up.shape
    nb, d, fb = w_up.shape
    tm, tn = min(2 * TILE_M, s), min(TILE_N,d)
    return _matmul("up_bwd_act", d_up, w_up, grid=(s // tm, d // tn, nb), after=after,
                   a_spec=pl.BlockSpec((tm, fb), lambda i, j, k: (i, k)),
                   b_spec=pl.BlockSpec((None, tn, fb), lambda i, j, k: (k, j, 0)),
                   out_shape=jax.ShapeDtypeStruct((s, d), F32),
                   out_specs=pl.BlockSpec((tm, tn), lambda i, j, k: (i, j)),
                   contract=NT, nk=nb, acc_shape=(tm, tn))


def _half_grad(name, a, b, core, home, received, after, *, grid, a_block, a_map, b_block, b_map, o_block, o_map, out_shape):
    n_after = len(after)
    pick = (lambda ref: ref[0]) if home else (lambda ref: 1 - ref[0])

    def body(core_ref, a_ref, b_ref, *rest):
        acc = lax.dot_general(a_ref[...], b_ref[...], (TN, ((), ())), preferred_element_type=F32)
        if received is not None:
            acc = acc + rest[0][...].astype(F32)
        rest[-1][...] = acc.astype(rest[-1].dtype)

    wrap = lambda fn: (lambda i, j, core_ref: fn(i, j, pick(core_ref)))
    o_spec = pl.BlockSpec(o_block, wrap(o_map))
    extra = [] if received is None else [o_spec]
    operands = [] if received is None else [received]
    return pl.pallas_call(
        body, name=name,
        grid_spec=pltpu.PrefetchScalarGridSpec(
            num_scalar_prefetch=1, grid=grid,
            in_specs=[pl.BlockSpec(a_block, wrap(a_map)), pl.BlockSpec(b_block, wrap(b_map))] + extra + [ANY] * n_after,
            out_specs=o_spec),
        out_shape=out_shape,
        compiler_params=_params("parallel", "parallel"),
    )(core, a, b, *operands, *after)


def _down_half_grad(name, a, d_m, core, home, received=None, after=()):
    s, f = a.shape
    d = d_m.shape[1]
    r = f // N_DEV
    tn = min(TILE_N, d)
    return _half_grad(name, a, d_m, core, home, received, after, grid=(N_CHIP, d // tn),
                      a_block=(s, r), a_map=lambda k, j, p: (0, 2 * k + p),
                      b_block=(s, tn), b_map=lambda k, j, p: (0, j),
                      o_block=(None, r, tn), o_map=lambda k, j, p: (k, 0, j),
                      out_shape=jax.ShapeDtypeStruct((N_CHIP, r, d), BF16))


def _up_half_grad(name, h2, d_up, core, home, received=None, after=()):
    s, d = h2.shape
    fb = d_up.shape[1] // N_DEV
    tm = min(TILE_M, d)
    return _half_grad(name, h2, d_up, core, home, received, after, grid=(d // tm, N_CHIP),
                      a_block=(s, tm), a_map=lambda i, k, p: (0, i),
                      b_block=(s, fb), b_map=lambda i, k, p: (0, 2 * k + p),
                      o_block=(None, tm, fb), o_map=lambda i, k, p: (k, i, 0),
                      out_shape=jax.ShapeDtypeStruct((N_CHIP, d, fb), BF16))


def _in_pad(in_width):
    return -(-in_width // LANE) * LANE


def _join_col_shards(blocks):
    n, r, w = blocks.shape
    rows = min(ROWS, r)
    width = _in_pad(n * w)

    def body(x_ref, o_ref):
        tail = [jnp.zeros((rows, width - n * w), o_ref.dtype)] if width > n * w else []
        o_ref[...] = jnp.concatenate([x_ref[j] for j in range(n)] + tail, axis=1)

    return pl.pallas_call(
        body, name="join_col_shards", grid=(r // rows,),
        in_specs=[pl.BlockSpec((n, rows, w), lambda i: (0, i, 0))], out_specs=_row_spec(rows, width),
        out_shape=jax.ShapeDtypeStruct((r, width), blocks.dtype),
        compiler_params=_params("parallel"),
    )(blocks)


def _permute_q_cols(w_uq, n_heads):
    r = w_uq.shape[0]
    w3 = w_uq.reshape(r, n_heads, QK)
    return jnp.concatenate([w3[:, :, :HEAD].reshape(r, n_heads * HEAD), w3[:, :, HEAD:].reshape(r, n_heads * ROPE)], axis=1)


def _unpermute_q_rows(wt, n_heads):
    r = wt.shape[1]
    nope = wt[:n_heads * HEAD].reshape(n_heads, HEAD, r)
    rope = wt[n_heads * HEAD:].reshape(n_heads, ROPE, r)
    return jnp.concatenate([nope, rope], axis=1).reshape(n_heads * QK, r)


def _local_step(x, tgt, gains, weights, grads, first_after=()):
    pre_mix_g, q_norm_g, kv_norm_g, conv_out_g, attn_out_g, post_mix_g, pre_mlp_g, post_mlp_g = gains
    s, d = x.shape
    conv_width = conv_out_g.shape[1]
    n_groups = conv_width // HEAD
    r_q, r_kv = q_norm_g.shape[1], kv_norm_g.shape[1]
    n_heads = attn_out_g.shape[1] // HEAD
    c_q0 = 3 * conv_width
    c_kv0 = c_q0 + r_q
    c_kr0 = c_kv0 + r_kv
    in_pad = _in_pad(c_kr0 + ROPE)
    tn_in = in_pad // 5 if in_pad % (5 * LANE) == 0 else LANE
    tables = _rope_tables(s, n_heads)

    h1 = _rms_fwd("pre_mix_norm", x, pre_mix_g, after=first_after)
    weights.forward(0, (h1, *tables))
    w_in_p, conv_w = weights.ready(0, ())
    proj = _mm_nn("in_proj", h1, w_in_p, F32, TILE_M, tn_in)
    y_conv = _conv_fwd(proj, conv_w, conv_out_g, n_groups)
    qn = _rms_fwd("q_norm", proj, q_norm_g, cols=(c_q0, r_q))
    kvn = _rms_fwd("kv_norm", proj, kv_norm_g, cols=(c_kv0, r_kv))
    weights.forward(1, (y_conv, qn, kvn))
    w_uq_p, w_ukv, w_o = weights.ready(1, ())
    q = _mm_nn("q_up", qn, w_uq_p, F32, TILE_M, TILE_N, after=weights.forward(2, (w_o,)))
    kv = _mm_nn("kv_up", kvn, w_ukv, F32, TILE_M, TILE_N)
    qh, kh, vh = _pack_heads(q, kv, proj, c_kr0, tables, n_heads)
    o, y_attn = _attn_fwd(qh, kh, vh, attn_out_g)
    mix = jnp.concatenate([y_conv, y_attn], axis=1)
    y = _mm_nn("out_proj", mix, w_o, F32, TILE_M, TILE_N, after=weights.forward(3, (y_attn,)))
    x2, h2 = _mid_fwd(x, y, post_mix_g, pre_mlp_g, after=weights.relay(2, (y,)))
    (w_up,) = weights.ready(2, (h2,))
    a, r = _up_fwd(h2, w_up)
    weights.relay(3, (a,))
    (w_down,) = weights.ready(3, ())
    m = _down_fwd(a, w_down)

    d_out, d_m, dg_post_mlp, loss_part = _head(m, x2, tgt, post_mlp_g)
    core = grads.core
    away = _down_half_grad("down_bwd_w_away", a, d_m, core, home=False)
    d_up = _down_bwd_act(d_m, w_down, r, after=grads.send_away(0, away))
    sums = _down_half_grad("down_bwd_w_home", a, d_m, core, home=True, received=grads.received(0, (d_up,)))
    away = _up_half_grad("up_bwd_w_away", h2, d_up, core, home=False, after=grads.send_sums(0, (sums,)))
    d_h2 = _up_bwd_act(d_up, w_up, after=grads.send_away(1, away))
    sums = _up_half_grad("up_bwd_w_home", h2, d_up, core, home=True, received=grads.received(1, (d_h2,)))
    d_x2, d_y, dg_pre_mlp, dg_post_mix = _mid_bwd(x2, y, d_out, d_h2, pre_mlp_g, post_mix_g, after=grads.send_sums(1, (sums,)))
    d_mix = _mm_nt("out_proj_bwd_act", d_y, w_o, F32, TILE_M, TILE_N)
    gw_o = _mm_tn("out_proj_bwd_w", mix, d_y, BF16, TILE_M, TILE_N)
    dqh, dkh, dvh, dg_attn = _attn_bwd(qh, kh, vh, o, d_mix, attn_out_g, conv_width, after=grads.full(2, (gw_o,)))
    d_q, d_kv, d_kr = _unpack_heads(dqh, dkh, dvh, tables, n_heads)
    d_qn = _mm_nt("q_up_bwd_act", d_q, w_uq_p, F32, TILE_M, TILE_N)
    d_kvn = _mm_nt("kv_up_bwd_act", d_kv, w_ukv, F32, TILE_M, TILE_N)
    gw_uq_t = _mm_tn("q_up_bwd_w", d_q, qn, F32, TILE_M, TILE_N)
    gw_ukv = _mm_tn("kv_up_bwd_w", kvn, d_kv, BF16, TILE_M, TILE_N)
    d_cq, dg_q = _rms_bwd_call("q_norm_bwd", proj, q_norm_g, d_qn, BF16, cols=(c_q0, r_q), after=grads.full(3, (gw_uq_t, gw_ukv)))
    d_ckv, dg_kv = _rms_bwd_call("kv_norm_bwd", proj, kv_norm_g, d_kvn, BF16, cols=(c_kv0, r_kv))
    d_u, d_b, d_c, dg_conv, dw_conv = _conv_bwd(proj, d_mix, conv_w, conv_out_g, n_groups)
    d_proj = jnp.concatenate([d_u, d_b, d_c, d_cq, d_ckv, d_kr[:, :in_pad - c_kr0]], axis=1)
    gw_in_t = _mm_tn("in_proj_bwd_w", d_proj, h1, F32, tn_in, TILE_N)
    d_h1 = _mm_nt("in_proj_bwd_act", d_proj, w_in_p, F32, TILE_M, 512, after=grads.send_away(4, gw_in_t))
    grad_x, dg_pre_mix = _first_bwd(x, pre_mix_g, d_h1, d_x2, after=grads.full(4, (gw_in_t,), received=(d_h1,)))

    small = [dg_pre_mix, dg_q, dg_kv, dg_conv, dg_attn, dg_post_mix, dg_pre_mlp, dg_post_mlp,
             dw_conv[0], dw_conv[1], dw_conv[2], loss_part]
    return grad_x, jnp.concatenate(small, axis=1)


HBM = pl.BlockSpec(memory_space=pltpu.HBM)
SEM = pl.BlockSpec(memory_space=pltpu.SEMAPHORE)
IN_VMEM = pl.BlockSpec(memory_space=pltpu.VMEM)
SPLIT = pltpu.CompilerParams(has_side_effects=pltpu.SideEffectType.DATAFLOW_SIDE_EFFECTING)


def _in_hbm(a):
    return pltpu.with_memory_space_constraint(a, pltpu.HBM)


def _hbm_like(a):
    return pltpu.HBM(a.shape, a.dtype)


def _place():
    x, y, c = lax.axis_index("x"), lax.axis_index("y"), lax.axis_index("c")
    other_chips = [(1 - x, y), (x, 1 - y), (1 - x, 1 - y)]
    return x, y, c, other_chips


def _block(px, py, pc):
    return 4 * px + 2 * py + pc


def _relay_route(x, y, c):
    came_from = ((1 - x) * (1 - c) + x * c, y * (1 - c) + (1 - y) * c)
    goes_to = (x * (1 - c) + (1 - x) * c, (1 - y) * (1 - c) + y * c)
    return came_from, goes_to


def _gather_start(name, shards, groups, relayed=()):
    n, ng = len(shards), len(groups)
    lands = [lax.empty((N_DEV, *a.shape), a.dtype) for a in shards]

    def body(*refs):
        src, land = refs[:n], refs[n:2 * n]
        sems, token = refs[2 * n:2 * n + 2 * ng], refs[-1]
        x, y, c, chips = _place()
        targets = [(x, y, 1 - c)] + [(*chip, c) for chip in chips]
        for gi, group in enumerate(groups):
            for i, w in enumerate(group):
                for k, to in enumerate(targets[:3] if gi in relayed else targets):
                    pltpu.make_async_remote_copy(
                        src_ref=src[w], dst_ref=land[w].at[_block(x, y, c)],
                        send_sem=sems[2 * gi].at[4 * i + k], recv_sem=sems[2 * gi + 1].at[4 * i + k],
                        device_id=to, device_id_type=MESH).start()
        token[...] = jnp.zeros_like(token)

    sem_shapes = [pltpu.SemaphoreType.DMA((4 * len(g),)) for g in groups for _ in range(2)]
    out = pl.pallas_call(
        body, name=name,
        in_specs=[HBM] * (2 * n),
        out_specs=[SEM] * (2 * ng) + [HBM] * (2 * n) + [IN_VMEM],
        out_shape=sem_shapes + [_hbm_like(a) for a in shards] + [_hbm_like(a) for a in lands]
        + [jax.ShapeDtypeStruct((SUBLANE, LANE), F32)],
        input_output_aliases={i: 2 * ng + i for i in range(2 * n)},
        compiler_params=SPLIT,
    )(*[_in_hbm(a) for a in shards], *[_in_hbm(a) for a in lands])
    sems = [(out[2 * gi], out[2 * gi + 1]) for gi in range(ng)]
    return sems, out[2 * ng:2 * ng + n], out[2 * ng + n:2 * ng + 2 * n], out[-1]


def _gather_forward(name, shards, lands, send1, recv1, after, relayed=False):
    n = len(lands)

    def body(*refs):
        src, land = refs[:n], refs[n:2 * n]
        s1, r1 = refs[2 * n], refs[2 * n + 1]
        s2, r2 = refs[2 * n + 2 + len(after)], refs[2 * n + 3 + len(after)]
        x, y, c, chips = _place()
        me, sibling = (x, y, c), (x, y, 1 - c)
        for j, chip in enumerate(chips[:2] if relayed else chips):
            for i in range(n):
                blk = land[i].at[_block(*chip, c)]
                pltpu.make_async_remote_copy(src_ref=blk, dst_ref=blk, send_sem=s1.at[4 * i + 1 + j], recv_sem=r1.at[4 * i + 1 + j],
                                             device_id=me, device_id_type=MESH).wait_recv()
                pltpu.make_async_remote_copy(src_ref=blk, dst_ref=blk, send_sem=s2.at[3 * i + j], recv_sem=r2.at[3 * i + j],
                                             device_id=sibling, device_id_type=MESH).start()
        if relayed:
            came_from, goes_to = _relay_route(x, y, c)
            for i in range(n):
                blk = land[i].at[_block(*came_from, c)]
                pltpu.make_async_remote_copy(src_ref=blk, dst_ref=blk, send_sem=s2.at[3 * i + 2], recv_sem=r2.at[3 * i + 2],
                                             device_id=(*goes_to, c), device_id_type=MESH).start()
        for i in range(n):
            blk = land[i].at[_block(x, y, 1 - c)]
            pltpu.make_async_remote_copy(src_ref=blk, dst_ref=blk, send_sem=s1.at[4 * i], recv_sem=r1.at[4 * i],
                                         device_id=me, device_id_type=MESH).wait_recv()
            for k in range(3 if relayed else 4):
                pltpu.make_async_remote_copy(src_ref=src[i], dst_ref=land[i].at[_block(x, y, c)], send_sem=s1.at[4 * i + k],
                                             recv_sem=r1.at[4 * i + k], device_id=sibling, device_id_type=MESH).wait_send()

    sem = pltpu.SemaphoreType.DMA((3 * n,))
    out = pl.pallas_call(
        body, name=name,
        in_specs=[HBM] * (2 * n) + [SEM, SEM] + [ANY] * len(after),
        out_specs=[SEM, SEM] + [HBM] * n,
        out_shape=[sem, sem] + [_hbm_like(a) for a in lands],
        input_output_aliases={n + i: 2 + i for i in range(n)},
        compiler_params=SPLIT,
    )(*shards, *lands, send1, recv1, *after)
    return (out[0], out[1]), out[2:]


def _gather_relay_forward(name, lands, send2, recv2, after):
    n = len(lands)

    def body(*refs):
        land, s2, r2 = refs[:n], refs[n], refs[n + 1]
        s3, r3 = refs[n + 2 + len(after)], refs[n + 3 + len(after)]
        x, y, c, _ = _place()
        me, sibling = (x, y, c), (x, y, 1 - c)
        came_from, _ = _relay_route(x, y, c)
        for i in range(n):
            blk = land[i].at[_block(1 - x, 1 - y, c)]
            pltpu.make_async_remote_copy(src_ref=blk, dst_ref=blk, send_sem=s2.at[3 * i + 2], recv_sem=r2.at[3 * i + 2],
                                         device_id=me, device_id_type=MESH).wait_recv()
            pltpu.make_async_remote_copy(src_ref=blk, dst_ref=blk, send_sem=s3.at[i], recv_sem=r3.at[i],
                                         device_id=sibling, device_id_type=MESH).start()
            sent = land[i].at[_block(*came_from, c)]
            pltpu.make_async_remote_copy(src_ref=sent, dst_ref=sent, send_sem=s2.at[3 * i + 2], recv_sem=r2.at[3 * i + 2],
                                         device_id=me, device_id_type=MESH).wait_send()

    sem = pltpu.SemaphoreType.DMA((n,))
    out = pl.pallas_call(
        body, name=name,
        in_specs=[HBM] * n + [SEM, SEM] + [ANY] * len(after),
        out_specs=[SEM, SEM] + [HBM] * n,
        out_shape=[sem, sem] + [_hbm_like(a) for a in lands],
        input_output_aliases={i: 2 + i for i in range(n)},
        compiler_params=SPLIT,
    )(*lands, send2, recv2, *after)
    return (out[0], out[1]), out[2:]


def _gather_wait(name, lands, send2, recv2, after, relay_sems=None):
    n = len(lands)
    n_sems = 2 if relay_sems is None else 4

    def body(*refs):
        land, s2, r2 = refs[:n], refs[n], refs[n + 1]
        x, y, c, chips = _place()
        me = (x, y, c)
        for i in range(n):
            for j, chip in enumerate(chips if relay_sems is None else chips[:2]):
                got = land[i].at[_block(*chip, 1 - c)]
                pltpu.make_async_remote_copy(src_ref=got, dst_ref=got, send_sem=s2.at[3 * i + j], recv_sem=r2.at[3 * i + j],
                                             device_id=me, device_id_type=MESH).wait_recv()
                sent = land[i].at[_block(*chip, c)]
                pltpu.make_async_remote_copy(src_ref=sent, dst_ref=sent, send_sem=s2.at[3 * i + j], recv_sem=r2.at[3 * i + j],
                                             device_id=me, device_id_type=MESH).wait_send()
            if relay_sems is not None:
                s3, r3 = refs[n + 2], refs[n + 3]
                got = land[i].at[_block(1 - x, 1 - y, 1 - c)]
                pltpu.make_async_remote_copy(src_ref=got, dst_ref=got, send_sem=s3.at[i], recv_sem=r3.at[i],
                                             device_id=me, device_id_type=MESH).wait_recv()
                sent = land[i].at[_block(1 - x, 1 - y, c)]
                pltpu.make_async_remote_copy(src_ref=sent, dst_ref=sent, send_sem=s3.at[i], recv_sem=r3.at[i],
                                             device_id=me, device_id_type=MESH).wait_send()

    return pl.pallas_call(
        body, name=name,
        in_specs=[HBM] * n + [SEM] * n_sems + [ANY] * len(after), out_specs=[HBM] * n, out_shape=[_hbm_like(a) for a in lands],
        input_output_aliases={i: i for i in range(n)},
        compiler_params=SPLIT,
    )(*lands, send2, recv2, *(relay_sems or ()), *after)


def _pair_exchange(name, grads, shard_rows):
    n = len(grads)
    shapes = [(g.shape[1:] if r is None else (r, g.shape[1])) for g, r in zip(grads, shard_rows)]

    def body(*refs):
        ins, recv = refs[:n], refs[n:2 * n]
        send_sems, recv_sems = refs[2 * n:]
        x, y, c, _ = _place()
        sends = []
        for w in range(n):
            for k in range(N_CHIP):
                j, r = 2 * k + 1 - c, shard_rows[w]
                src = ins[w].at[j] if r is None else ins[w].at[pl.ds(pl.multiple_of(j * r, SUBLANE), r), :]
                sends.append(pltpu.make_async_remote_copy(
                    src_ref=src, dst_ref=recv[w].at[k],
                    send_sem=send_sems.at[w, k], recv_sem=recv_sems.at[w, k],
                    device_id=(x, y, 1 - c), device_id_type=MESH))
        for cp in sends:
            cp.start()
        for cp in sends:
            cp.wait()

    return pl.pallas_call(
        body, name=name,
        in_specs=[ANY] * n, out_specs=[ANY] * n,
        out_shape=[jax.ShapeDtypeStruct((N_CHIP, *shape), g.dtype) for g, shape in zip(grads, shapes)],
        scratch_shapes=[pltpu.SemaphoreType.DMA((n, N_CHIP))] * 2,
    )(*grads)


def _pair_sum_rows(name, grad, received, core):
    _, r, c = received.shape
    tc = _fit(c, 512)

    def body(core_ref, a_ref, b_ref, o_ref):
        o_ref[...] = (a_ref[...] + b_ref[...]).astype(o_ref.dtype)

    spec = pl.BlockSpec((None, r, tc), lambda k, i, core_ref: (k, 0, i))
    return pl.pallas_call(
        body, name=name,
        grid_spec=pltpu.PrefetchScalarGridSpec(
            num_scalar_prefetch=1, grid=(N_CHIP, c // tc),
            in_specs=[pl.BlockSpec((r, tc), lambda k, i, core_ref: (2 * k + core_ref[0], i)), spec],
            out_specs=spec),
        out_shape=jax.ShapeDtypeStruct(received.shape, BF16),
        compiler_params=_params("parallel", "parallel"),
    )(core, grad, received)


def _pair_sum(name, grad, received, core):
    _, r, c = received.shape
    rows = min(ROWS, r)
    assert r % rows == 0

    def body(core_ref, a_ref, b_ref, o_ref):
        o_ref[...] = (a_ref[...].astype(F32) + b_ref[...].astype(F32)).astype(o_ref.dtype)

    spec = pl.BlockSpec((None, rows, c), lambda k, i, core_ref: (k, i, 0))
    return pl.pallas_call(
        body, name=name,
        grid_spec=pltpu.PrefetchScalarGridSpec(
            num_scalar_prefetch=1, grid=(N_CHIP, r // rows),
            in_specs=[pl.BlockSpec((None, None, rows, c), lambda k, i, core_ref: (k, core_ref[0], i, 0)), spec],
            out_specs=spec),
        out_shape=jax.ShapeDtypeStruct(received.shape, received.dtype),
        compiler_params=_params("parallel", "parallel"),
    )(core, grad.reshape(N_CHIP, 2, r, c), received)


def _away_shard(src, k, c, shard_rows):
    if shard_rows is None:
        return src.at[k]
    return src.at[pl.ds(pl.multiple_of((2 * k + 1 - c) * shard_rows, SUBLANE), shard_rows), :]


def _pair_send_start(name, away, shard_rows=None):
    shape = away.shape if shard_rows is None else (N_CHIP, shard_rows, away.shape[1])
    land = lax.empty(shape, away.dtype)

    def body(src, dst, send, recv, src_thru, dst_thru, token):
        x, y, c, _ = _place()
        for k in range(N_CHIP):
            pltpu.make_async_remote_copy(src_ref=_away_shard(src, k, c, shard_rows), dst_ref=dst.at[k], send_sem=send.at[k],
                                         recv_sem=recv.at[k], device_id=(x, y, 1 - c), device_id_type=MESH).start()
        token[...] = jnp.zeros_like(token)

    sem = pltpu.SemaphoreType.DMA((N_CHIP,))
    out = pl.pallas_call(
        body, name=name,
        in_specs=[HBM, HBM], out_specs=[SEM, SEM, HBM, HBM, IN_VMEM],
        out_shape=[sem, sem, _hbm_like(away), _hbm_like(land), jax.ShapeDtypeStruct((SUBLANE, LANE), F32)],
        input_output_aliases={0: 2, 1: 3},
        compiler_params=SPLIT,
    )(_in_hbm(away), _in_hbm(land))
    return (out[0], out[1]), out[2], out[3], out[4]


def _pair_send_wait(name, sems, src, land, after, shard_rows=None):
    def body(src_ref, dst_ref, send, recv, *rest):
        x, y, c, _ = _place()
        for k in range(N_CHIP):
            pltpu.make_async_remote_copy(src_ref=_away_shard(src_ref, k, c, shard_rows), dst_ref=dst_ref.at[k], send_sem=send.at[k],
                                         recv_sem=recv.at[k], device_id=(x, y, 1 - c), device_id_type=MESH).wait()

    return pl.pallas_call(
        body, name=name,
        in_specs=[HBM, HBM, SEM, SEM] + [ANY] * len(after), out_specs=HBM, out_shape=_hbm_like(land),
        input_output_aliases={1: 0},
        compiler_params=SPLIT,
    )(src, land, *sems, *after)


def _chip_send_start(name, sums):
    n = len(sums)
    lands = [lax.empty(a.shape, a.dtype) for a in sums]

    def body(*refs):
        src, land = refs[:n], refs[n:2 * n]
        send, recv, token = refs[2 * n], refs[2 * n + 1], refs[-1]
        x, y, c, chips = _place()
        for w in range(n):
            for j, (px, py) in enumerate(chips):
                pltpu.make_async_remote_copy(
                    src_ref=src[w].at[2 * px + py], dst_ref=land[w].at[2 * x + y],
                    send_sem=send.at[3 * w + j], recv_sem=recv.at[3 * w + j],
                    device_id=(px, py, c), device_id_type=MESH).start()
        token[...] = jnp.zeros_like(token)

    sem = pltpu.SemaphoreType.DMA((3 * n,))
    out = pl.pallas_call(
        body, name=name,
        in_specs=[HBM] * (2 * n),
        out_specs=[SEM, SEM] + [HBM] * (2 * n) + [IN_VMEM],
        out_shape=[sem, sem] + [_hbm_like(a) for a in sums] + [_hbm_like(a) for a in lands]
        + [jax.ShapeDtypeStruct((SUBLANE, LANE), F32)],
        input_output_aliases={i: 2 + i for i in range(2 * n)},
        compiler_params=SPLIT,
    )(*[_in_hbm(a) for a in sums], *[_in_hbm(a) for a in lands])
    return (out[0], out[1]), out[2:2 + n], out[2 + n:2 + 2 * n], out[-1]


def _chip_send_wait(name, groups, after):
    counts = [len(g[1]) for g in groups]
    n = sum(counts)

    def body(*refs):
        src, land = refs[:n], refs[n:2 * n]
        sems = refs[2 * n:2 * n + 2 * len(groups)]
        x, y, c, chips = _place()
        w = 0
        for gi, count in enumerate(counts):
            for i in range(count):
                for j, (px, py) in enumerate(chips):
                    pltpu.make_async_remote_copy(
                        src_ref=src[w].at[2 * px + py], dst_ref=land[w].at[2 * px + py],
                        send_sem=sems[2 * gi].at[3 * i + j], recv_sem=sems[2 * gi + 1].at[3 * i + j],
                        device_id=(px, py, c), device_id_type=MESH).wait()
                w += 1

    sums = [a for g in groups for a in g[1]]
    lands = [a for g in groups for a in g[2]]
    sems = [s for g in groups for s in g[0]]
    return pl.pallas_call(
        body, name=name,
        in_specs=[HBM] * (2 * n) + [SEM] * len(sems) + [ANY] * len(after),
        out_specs=[HBM] * n, out_shape=[_hbm_like(a) for a in lands],
        input_output_aliases={n + i: i for i in range(n)},
        compiler_params=SPLIT,
    )(*sums, *lands, *sems, *after)


def _small_all_reduce(part, after=()):
    _, w = part.shape

    def body(p_ref, *rest):
        o_ref, buf, send_sems, recv_sems = rest[len(after):]
        x, y, c, _ = _place()
        me = 4 * x + 2 * y + c
        buf[me] = jnp.sum(p_ref[...], axis=0, keepdims=True)
        copies = []
        for k in range(1, N_DEV):
            dx, dy, dc = (k >> 2) & 1, (k >> 1) & 1, k & 1
            copies.append(pltpu.make_async_remote_copy(
                src_ref=buf.at[me], dst_ref=buf.at[me], send_sem=send_sems.at[k - 1], recv_sem=recv_sems.at[k - 1],
                device_id=(x ^ dx, y ^ dy, c ^ dc), device_id_type=MESH))
        for cp in copies:
            cp.start()
        for cp in copies:
            cp.wait()
        tot = buf[0]
        for d in range(1, N_DEV):
            tot = tot + buf[d]
        o_ref[...] = tot
        loss = jnp.sum(tot[:, w - LANE:], axis=1, keepdims=True)
        o_ref[:, w - LANE:] = jnp.broadcast_to(loss, (1, LANE))

    return pl.pallas_call(
        body, name="small_all_reduce",
        in_specs=[IN_VMEM] + [ANY] * len(after), out_specs=IN_VMEM,
        out_shape=jax.ShapeDtypeStruct((1, w), F32),
        scratch_shapes=[pltpu.VMEM((N_DEV, 1, w), F32), pltpu.SemaphoreType.DMA((N_DEV - 1,)), pltpu.SemaphoreType.DMA((N_DEV - 1,))],
        compiler_params=pltpu.CompilerParams(vmem_limit_bytes=VMEM_LIMIT_BYTES),
    )(part, *after)


def _adamw(w, g, m, v):
    m = ADAM_B1 * m + (1.0 - ADAM_B1) * g
    v = ADAM_B2 * v + (1.0 - ADAM_B2) * (g * g)
    m_hat = m / (1.0 - ADAM_B1 ** ADAM_STEP)
    v_hat = v / (1.0 - ADAM_B2 ** ADAM_STEP)
    delta = -ADAM_LR * (m_hat / (jnp.sqrt(v_hat) + ADAM_EPS) + ADAM_WD * w)
    return delta, m, v


def _sum_adam(name, parts, sums, chip, w, m, v, after=()):
    _, r, c = w.shape
    n_after = len(after)
    by_rows = r % ROWS == 0 or r < ROWS
    tr, tc = (min(ROWS, r), c) if by_rows else (r, _fit(c, 512))
    at = (lambda i: (i, 0)) if by_rows else (lambda i: (0, i))

    def body(chip_ref, p_ref, own_ref, w_ref, m_ref, v_ref, *rest):
        g_ref, d_ref, mo_ref, vo_ref = rest[n_after:]
        g = None
        for k in range(N_CHIP):
            term = jnp.where(chip_ref[0] == k, own_ref[...], p_ref[k]).astype(F32)
            g = term if g is None else g + term
        g_ref[...] = g
        d_ref[...], mo_ref[...], vo_ref[...] = _adamw(w_ref[...], g, m_ref[...], v_ref[...])

    blk = pl.BlockSpec((None, tr, tc), lambda i, chip_ref: (0, *at(i)))
    out = jax.ShapeDtypeStruct((1, r, c), F32)
    return pl.pallas_call(
        body, name=name,
        grid_spec=pltpu.PrefetchScalarGridSpec(
            num_scalar_prefetch=1, grid=(r // tr if by_rows else c // tc,),
            in_specs=[pl.BlockSpec((N_CHIP, tr, tc), lambda i, chip_ref: (0, *at(i))),
                      pl.BlockSpec((None, tr, tc), lambda i, chip_ref: (chip_ref[0], *at(i))), blk, blk, blk]
            + [ANY] * n_after,
            out_specs=[blk] * 4),
        out_shape=[out] * 4,
        compiler_params=_params("parallel"),
    )(chip, parts, sums, w, m, v, *after)


def _adam_gains(total, ws, ms, vs):
    n = len(ws)
    widths = [w.shape[1] for w in ws]

    def body(t_ref, *refs):
        w_refs, m_refs, v_refs, outs = refs[:n], refs[n:2 * n], refs[2 * n:3 * n], refs[3 * n:]
        off = 0
        for i in range(n):
            g = t_ref[:, off:off + widths[i]]
            off += widths[i]
            g_ref, d_ref, mo_ref, vo_ref = outs[4 * i:4 * i + 4]
            g_ref[...] = g
            d_ref[...], mo_ref[...], vo_ref[...] = _adamw(w_refs[i][...], g, m_refs[i][...], v_refs[i][...])

    out = pl.pallas_call(
        body, name="adam_gains",
        out_shape=[jax.ShapeDtypeStruct(w.shape, F32) for w in ws for _ in range(4)],
    )(total, *ws, *ms, *vs)
    return [tuple(out[4 * i:4 * i + 4]) for i in range(n)]


def _adam_taps(total, first_col, device, w, m, v):
    _, n_taps, cw = w.shape
    col_block = lambda t, dev: (0, first_col // cw + t * N_DEV + dev[0])
    tap = pl.BlockSpec((None, 1, cw), lambda t, dev: (t, 0, 0))

    def body(dev_ref, t_ref, w_ref, m_ref, v_ref, g_ref, d_ref, mo_ref, vo_ref):
        g = t_ref[...]
        g_ref[...] = g
        d_ref[...], mo_ref[...], vo_ref[...] = _adamw(w_ref[...], g, m_ref[...], v_ref[...])

    shape3 = (n_taps, 1, cw)
    out = pl.pallas_call(
        body, name="adam_taps",
        grid_spec=pltpu.PrefetchScalarGridSpec(
            num_scalar_prefetch=1, grid=(n_taps,),
            in_specs=[pl.BlockSpec((1, cw), col_block), tap, tap, tap], out_specs=[tap] * 4),
        out_shape=[jax.ShapeDtypeStruct(shape3, F32)] * 4,
    )(device, total, w.reshape(shape3), m.reshape(shape3), v.reshape(shape3))
    return tuple(o.reshape(w.shape) for o in out)


def kernel(x, pre_mix_g, w_in, conv_w, q_norm_g, w_uq, kv_norm_g, w_ukv, conv_out_g, attn_out_g, w_o, post_mix_g, pre_mlp_g, w_up, w_down, post_mlp_g, loss_target, m_pre_mix_g, m_w_in, m_conv_w, m_q_norm_g, m_w_uq, m_kv_norm_g, m_w_ukv, m_conv_out_g, m_attn_out_g, m_w_o, m_post_mix_g, m_pre_mlp_g, m_w_up, m_w_down, m_post_mlp_g, v_pre_mix_g, v_w_in, v_conv_w, v_q_norm_g, v_w_uq, v_kv_norm_g, v_w_ukv, v_conv_out_g, v_attn_out_g, v_w_o, v_post_mix_g, v_pre_mlp_g, v_w_up, v_w_down, v_post_mlp_g):
    me = 4 * lax.axis_index("x") + 2 * lax.axis_index("y") + lax.axis_index("c")
    core = lax.axis_index("c").astype(jnp.int32).reshape(1)
    chip = (2 * lax.axis_index("x") + lax.axis_index("y")).astype(jnp.int32).reshape(1)
    gains = (pre_mix_g, q_norm_g, kv_norm_g, conv_out_g, attn_out_g, post_mix_g, pre_mlp_g, post_mlp_g)
    gain_m = (m_pre_mix_g, m_q_norm_g, m_kv_norm_g, m_conv_out_g, m_attn_out_g, m_post_mix_g, m_pre_mlp_g, m_post_mlp_g)
    gain_v = (v_pre_mix_g, v_q_norm_g, v_kv_norm_g, v_conv_out_g, v_attn_out_g, v_post_mix_g, v_pre_mlp_g, v_post_mlp_g)
    names = ("w_in", "w_uq", "w_ukv", "w_o", "w_up", "w_down")
    big = dict(zip(names, (w_in, w_uq, w_ukv, w_o, w_up, w_down)))
    big_m = dict(zip(names, (m_w_in, m_w_uq, m_w_ukv, m_w_o, m_w_up, m_w_down)))
    big_v = dict(zip(names, (v_w_in, v_w_uq, v_w_ukv, v_w_o, v_w_up, v_w_down)))
    n_heads = attn_out_g.shape[1] // HEAD
    n_taps = conv_w.shape[1]

    gathered = ("w_in", "conv", "w_uq", "w_ukv", "w_o", "w_up", "w_down")
    gather_groups = ((0, 1), (2, 3, 4), (5,), (6,))
    taps = jnp.pad(conv_w[0], ((0, SUBLANE - n_taps), (0, 0)))
    sems_a, shards_a, lands_a, token_a = _gather_start("gather_start_first", [w_in[0].astype(BF16), taps], ((0, 1),))
    behind = token_a[0, 0]
    sems_b, shards_b, lands_b, token = _gather_start(
        "gather_start_rest", [(big[nm][0] + behind).astype(BF16) for nm in gathered[2:]], ((0, 1, 2), (3,), (4,)), relayed=(1, 2))
    sems1, shards, lands = sems_a + sems_b, [*shards_a, *shards_b], [*lands_a, *lands_b]
    relayed_groups = (2, 3)

    cols = lambda a: jnp.concatenate([a[j] for j in range(N_DEV)], axis=1)
    rows = lambda a: a.reshape(N_DEV * a.shape[1], a.shape[2])
    ready = {
        "w_in": _join_col_shards,
        "conv": lambda a: cols(a)[:n_taps],
        "w_uq": lambda a: _permute_q_cols(cols(a), n_heads),
        "w_ukv": cols, "w_o": rows, "w_up": lambda a: a, "w_down": rows,
    }

    class Weights:
        def __init__(self):
            self.passed, self.relayed = {}, {}

        def forward(self, group, after):
            idx = gather_groups[group]
            self.passed[group] = _gather_forward(f"gather_forward_{group}", [shards[i] for i in idx], [lands[i] for i in idx],
                                                 *sems1[group], after, relayed=group in relayed_groups)
            return tuple(self.passed[group][1])

        def relay(self, group, after):
            sems2, mid = self.passed[group]
            self.relayed[group], mid = _gather_relay_forward(f"gather_relay_{group}", mid, *sems2, after)
            self.passed[group] = (sems2, mid)
            return tuple(mid)

        def ready(self, group, after):
            sems2, mid = self.passed[group]
            full = _gather_wait(f"gather_wait_{group}", mid, *sems2, after, relay_sems=self.relayed.get(group))
            out = []
            for i, a in zip(gather_groups[group], full):
                a = lax.dynamic_update_index_in_dim(a, shards[i], me, 0)
                out.append(ready[gathered[i]](a))
            return out

    weights = Weights()

    col_blocks = lambda g: g.reshape(g.shape[0], N_DEV, g.shape[1] // N_DEV).transpose(1, 0, 2)
    row_blocks = lambda g: g.reshape(N_DEV, g.shape[0] // N_DEV, g.shape[1])
    grad_groups = (("w_down",), ("w_up",), ("w_o",), ("w_uq", "w_ukv"), ("w_in",))
    transposed = {"w_in": w_in.shape[2], "w_uq": w_uq.shape[2]}
    to_blocks = {
        "w_in": lambda g: g, "w_uq": lambda g: _unpermute_q_rows(g, n_heads),
        "w_ukv": col_blocks, "w_o": row_blocks, "w_up": lambda g: g, "w_down": row_blocks,
    }
    in_flight = []

    class Grads:
        def __init__(self):
            self.core = core
            self.away = {}

        def send_sums(self, group, sums):
            sems, sums, parts, tok = _chip_send_start(f"chip_send_start_{group}", list(sums))
            in_flight.append((sems, sums, parts))
            return (tok,)

        def full(self, group, arrays, received=None):
            nms = grad_groups[group]
            if received is None:
                blocks = [to_blocks[nm](g) for nm, g in zip(nms, arrays)]
                got = _pair_exchange(f"pair_exchange_{group}", blocks, [transposed.get(nm) for nm in nms])
            else:
                blocks, got = [self.away[group][1]], [self.received(group, received)]
            sums = [(_pair_sum_rows if nm in transposed else _pair_sum)(f"pair_sum_{nm}", g, r, core)
                    for nm, g, r in zip(nms, blocks, got)]
            return self.send_sums(group, sums)

        def send_away(self, group, half):
            nm = grad_groups[group][0]
            rows = transposed.get(nm)
            sems, src, land, tok = _pair_send_start(f"pair_send_start_{group}", half if rows is None else to_blocks[nm](half), rows)
            self.away[group] = (sems, src, land, rows)
            return (tok,)

        def received(self, group, after):
            sems, src, land, rows = self.away[group]
            return _pair_send_wait(f"pair_send_wait_{group}", sems, src, land, after, rows)

    grad_x, small = _local_step(x[0], loss_target[0], gains, weights, Grads(), first_after=(token,))

    big_out = {}

    def update(tag, first, last, after):
        groups = in_flight[first:last]
        parts = _chip_send_wait("chip_send_wait_" + tag, groups, after)
        nms = [nm for grp in grad_groups[first:last] for nm in grp]
        sums = [a for _, s, _ in groups for a in s]
        for nm, p, s in zip(nms, parts, sums):
            view = (lambda a: jnp.swapaxes(a, 1, 2)) if nm in transposed else (lambda a: a)
            out = _sum_adam("adam_" + nm, p, s, chip, view(big[nm]), view(big_m[nm]), view(big_v[nm]), after=after)
            after = (out[0],)
            big_out[nm] = [view(o) for o in out]
        return after

    after = update("early", 0, len(in_flight) - 1, (grad_x,))
    total = _small_all_reduce(small, after=after)
    update("late", len(in_flight) - 1, len(in_flight), (total,))
    big_out = [big_out[nm] for nm in names]

    gain_out = _adam_gains(total, gains, gain_m, gain_v)
    taps_out = _adam_taps(total, sum(g.shape[1] for g in gains), me.astype(jnp.int32).reshape(1), conv_w, m_conv_w, v_conv_w)
    loss = total[0, total.shape[1] - 1]

    order = (0, "w_in", "conv", 1, "w_uq", 2, "w_ukv", 3, 4, "w_o", 5, 6, "w_up", "w_down", 7)
    by_name = dict(zip(names, big_out))
    outs = [loss, grad_x[None]]
    for kind in range(4):
        for item in order:
            if item == "conv":
                outs.append(taps_out[kind])
            elif isinstance(item, int):
                outs.append(gain_out[item][kind])
            else:
                outs.append(by_name[item][kind])
    return tuple(outs)
```

```python
import math

import jax
import jax.numpy as jnp
from jax import lax
from jax.experimental import pallas as pl
from jax.experimental.pallas import tpu as pltpu

F32 = jnp.float32
BF16 = jnp.bfloat16

EPS = 1e-6
NEG_INF = -1e30
HEAD = 128
ROPE = 64
QK = HEAD + ROPE
CHUNK = 64
ROPE_THETA = 10000.0
ADAM_LR, ADAM_B1, ADAM_B2, ADAM_EPS, ADAM_WD, ADAM_STEP = 0.001, 0.9, 0.999, 1e-08, 0.01, 10

LANE = 128
SUBLANE = 8
VMEM_LIMIT_BYTES = 56 * 1024 * 1024

N_DEV = 8
N_CHIP = 4
MESH = pl.DeviceIdType.MESH


def _params(*sem):
    return pltpu.CompilerParams(dimension_semantics=sem, vmem_limit_bytes=VMEM_LIMIT_BYTES)


ANY = pl.BlockSpec(memory_space=pl.ANY)


def _call(body, *, in_specs, after=(), **kw):
    n_in, n_after = len(in_specs), len(after)

    def ordered(*refs):
        body(*refs[:n_in], *refs[n_in + n_after:])

    call = pl.pallas_call(ordered, in_specs=[*in_specs, *[ANY] * n_after], **kw)
    return lambda *operands: call(*operands, *after)


def _sublane_sum(v):
    r, w = v.shape
    return jnp.sum(v.reshape(r // SUBLANE, SUBLANE, w), axis=0)


def _rstd(x):
    return lax.rsqrt(jnp.mean(x * x, axis=-1, keepdims=True) + EPS)


def _rms_bwd(x, g, dy):
    r = _rstd(x)
    xh = x * r
    dxh = dy * g
    dx = r * (dxh - xh * jnp.mean(dxh * xh, axis=-1, keepdims=True))
    return dx, dy * xh


def _accumulate(ref, val, step):
    @pl.when(step == 0)
    def _():
        ref[...] = val

    @pl.when(step > 0)
    def _():
        ref[...] += val


NN = ((1,), (0,))
NT = ((1,), (1,))
TN = ((0,), (0,))


def _matmul(name, a, b, *, grid, a_spec, b_spec, out_shape, out_specs, contract, nk=1, acc_shape=None,
            extras=(), extra_specs=(), epilogue=None, after=()):
    multi = isinstance(out_shape, (tuple, list))
    out_shapes = tuple(out_shape) if multi else (out_shape,)
    n_out = len(out_shapes)
    n_extra = len(extras)

    def body(a_ref, b_ref, *rest):
        x_refs = rest[:n_extra]
        o_refs = rest[n_extra:n_extra + n_out]

        def emit(acc):
            vals = epilogue(acc, *[r[...] for r in x_refs]) if epilogue else (acc,)
            for r, v in zip(o_refs, vals):
                r[...] = v.astype(r.dtype)

        p = lax.dot_general(a_ref[...], b_ref[...], (contract, ((), ())), preferred_element_type=F32)
        if nk == 1:
            emit(p)
        else:
            acc_ref = rest[n_extra + n_out]
            k = pl.program_id(2)
            _accumulate(acc_ref, p, k)

            @pl.when(k == nk - 1)
            def _():
                emit(acc_ref[...])

    sem = ("parallel", "parallel") + (("arbitrary",) if nk > 1 else ())
    return _call(
        body, name=name, grid=grid, after=after,
        in_specs=[a_spec, b_spec, *extra_specs],
        out_specs=out_specs,
        out_shape=out_shape,
        scratch_shapes=[pltpu.VMEM(acc_shape, F32)] if nk > 1 else [],
        compiler_params=_params(*sem),
    )(a, b, *extras)


def _fit(n, tile):
    if n <= tile:
        return n
    t = tile - tile % LANE
    while n % t:
        t -= LANE
    return t


def _mm_nn(name, a, b, out_dtype, tm, tn, after=()):
    m, k = a.shape
    n = b.shape[1]
    tm, tn = _fit(m, tm), _fit(n, tn)
    return _matmul(name, a, b, grid=(m // tm, n // tn), after=after,
                   a_spec=pl.BlockSpec((tm, k), lambda i, j: (i, 0)),
                   b_spec=pl.BlockSpec((k, tn), lambda i, j: (0, j)),
                   out_shape=jax.ShapeDtypeStruct((m, n), out_dtype),
                   out_specs=pl.BlockSpec((tm, tn), lambda i, j: (i, j)), contract=NN)


def _mm_nt(name, a, b, out_dtype, tm, tn, after=()):
    m, k = a.shape
    n = b.shape[0]
    tm, tn = _fit(m, tm), _fit(n, tn)
    return _matmul(name, a, b, grid=(m // tm, n // tn), after=after,
                   a_spec=pl.BlockSpec((tm, k), lambda i, j: (i, 0)),
                   b_spec=pl.BlockSpec((tn, k), lambda i, j: (j, 0)),
                   out_shape=jax.ShapeDtypeStruct((m, n), out_dtype),
                   out_specs=pl.BlockSpec((tm, tn), lambda i, j: (i, j)), contract=NT)


def _mm_tn(name, a, b, out_dtype, tm, tn):
    s, m = a.shape
    n = b.shape[1]
    tm, tn = _fit(m, tm), _fit(n, tn)
    return _matmul(name, a, b, grid=(m // tm, n // tn),
                   a_spec=pl.BlockSpec((s, tm), lambda i, j: (0, i)),
                   b_spec=pl.BlockSpec((s, tn), lambda i, j: (0, j)),
                   out_shape=jax.ShapeDtypeStruct((m, n), out_dtype),
                   out_specs=pl.BlockSpec((tm, tn), lambda i, j: (i, j)), contract=TN)


ROWS = 256


def _row_spec(rows, width):
    return pl.BlockSpec((rows, width), lambda i: (i, 0))


def _fixed_spec(rows, width):
    return pl.BlockSpec((rows, width), lambda i: (0, 0))


def _column_pieces(rows, start, width):
    piece = math.gcd(start, width)
    assert piece % LANE == 0
    return [pl.BlockSpec((rows, piece), lambda i, b=start // piece + p: (i, b)) for p in range(width // piece)]


def _rms_fwd(name, x, g, cols=None, after=()):
    s = x.shape[0]
    start, w = cols or (0, x.shape[1])
    rows = min(ROWS, s)
    pieces = _column_pieces(rows, start, w) if cols else [_row_spec(rows, w)]
    n = len(pieces)

    def body(*refs):
        g_ref, o_ref = refs[n:]
        xv = refs[0][...] if n == 1 else jnp.concatenate([r[...] for r in refs[:n]], axis=1)
        o_ref[...] = (xv * _rstd(xv) * g_ref[...]).astype(o_ref.dtype)

    return _call(
        body, name=name, grid=(s // rows,), after=after,
        in_specs=[*pieces, _fixed_spec(1, w)],
        out_specs=_row_spec(rows, w),
        out_shape=jax.ShapeDtypeStruct((s, w), BF16),
        compiler_params=_params("parallel"),
    )(*[x] * n, g)


def _rms_bwd_call(name, x, g, dy, out_dtype, cols=None, after=()):
    s = x.shape[0]
    start, w = cols or (0, x.shape[1])
    rows = min(ROWS, s)
    pieces = _column_pieces(rows, start, w) if cols else [_row_spec(rows, w)]
    n = len(pieces)

    def body(*refs):
        g_ref, dy_ref, dx_ref, dg_ref = refs[n:]
        xv = refs[0][...] if n == 1 else jnp.concatenate([r[...] for r in refs[:n]], axis=1)
        dx, dgc = _rms_bwd(xv, g_ref[...], dy_ref[...].astype(F32))
        dx_ref[...] = dx.astype(dx_ref.dtype)
        _accumulate(dg_ref, _sublane_sum(dgc), pl.program_id(0))

    return _call(
        body, name=name, grid=(s // rows,), after=after,
        in_specs=[*pieces, _fixed_spec(1, w), _row_spec(rows, w)],
        out_specs=[_row_spec(rows, w), _fixed_spec(SUBLANE, w)],
        out_shape=[jax.ShapeDtypeStruct((s, w), out_dtype), jax.ShapeDtypeStruct((SUBLANE, w), F32)],
        compiler_params=_params("arbitrary"),
    )(*[x] * n, g, dy)


def _mid_fwd(x, y, g_post, g_pre, after=()):
    s, w = x.shape
    rows = min(ROWS, s)

    def body(x_ref, y_ref, gp_ref, gq_ref, x2_ref, h2_ref):
        yv = y_ref[...]
        x2 = x_ref[...] + yv * _rstd(yv) * gp_ref[...]
        x2_ref[...] = x2
        h2_ref[...] = (x2 * _rstd(x2) * gq_ref[...]).astype(h2_ref.dtype)

    return _call(
        body, name="mid_fwd", grid=(s // rows,), after=after,
        in_specs=[_row_spec(rows, w), _row_spec(rows, w), _fixed_spec(1, w), _fixed_spec(1, w)],
        out_specs=[_row_spec(rows, w), _row_spec(rows, w)],
        out_shape=[jax.ShapeDtypeStruct((s, w), F32), jax.ShapeDtypeStruct((s, w), BF16)],
        compiler_params=_params("parallel"),
    )(x, y, g_post, g_pre)


def _head(m, x2, tgt, g):
    s, w = m.shape
    rows = min(ROWS, s)

    def body(m_ref, x2_ref, t_ref, g_ref, dout_ref, dm_ref, dg_ref, loss_ref):
        mv = m_ref[...]
        gv = g_ref[...]
        out = x2_ref[...] + mv * _rstd(mv) * gv
        err = out - t_ref[...]
        dout = err * (1.0 / w)
        dout_ref[...] = dout
        dm, dgc = _rms_bwd(mv, gv, dout)
        dm_ref[...] = dm.astype(dm_ref.dtype)
        sq = err * err
        lanes = sq[:, 0:LANE]
        for j in range(1, w // LANE):
            lanes = lanes + sq[:, j * LANE:(j + 1) * LANE]
        step = pl.program_id(0)
        _accumulate(dg_ref, _sublane_sum(dgc), step)
        _accumulate(loss_ref, _sublane_sum(lanes) * (0.5 / w), step)

    return pl.pallas_call(
        body, name="head", grid=(s // rows,),
        in_specs=[_row_spec(rows, w), _row_spec(rows, w), _row_spec(rows, w), _fixed_spec(1, w)],
        out_specs=[_row_spec(rows, w), _row_spec(rows, w), _fixed_spec(SUBLANE, w), _fixed_spec(SUBLANE, LANE)],
        out_shape=[jax.ShapeDtypeStruct((s, w), F32), jax.ShapeDtypeStruct((s, w), BF16),
                   jax.ShapeDtypeStruct((SUBLANE, w), F32), jax.ShapeDtypeStruct((SUBLANE, LANE), F32)],
        compiler_params=_params("arbitrary"),
    )(m, x2, tgt, g)


def _mid_bwd(x2, y, d_out, d_h2, g_pre, g_post, after=()):
    s, w = x2.shape
    rows = min(ROWS, s)

    def body(x2_ref, y_ref, dout_ref, dh2_ref, gq_ref, gp_ref, dx2_ref, dy_ref, dgq_ref, dgp_ref):
        dx, dgq = _rms_bwd(x2_ref[...], gq_ref[...], dh2_ref[...])
        dx2 = dout_ref[...] + dx
        dx2_ref[...] = dx2
        dy, dgp = _rms_bwd(y_ref[...], gp_ref[...], dx2)
        dy_ref[...] = dy.astype(dy_ref.dtype)
        step = pl.program_id(0)
        _accumulate(dgq_ref, _sublane_sum(dgq), step)
        _accumulate(dgp_ref, _sublane_sum(dgp), step)

    return _call(
        body, name="mid_bwd", grid=(s // rows,), after=after,
        in_specs=[_row_spec(rows, w)] * 4 + [_fixed_spec(1, w)] * 2,
        out_specs=[_row_spec(rows, w), _row_spec(rows, w), _fixed_spec(SUBLANE, w), _fixed_spec(SUBLANE, w)],
        out_shape=[jax.ShapeDtypeStruct((s, w), F32), jax.ShapeDtypeStruct((s, w), BF16),
                   jax.ShapeDtypeStruct((SUBLANE, w), F32), jax.ShapeDtypeStruct((SUBLANE, w), F32)],
        compiler_params=_params("arbitrary"),
    )(x2, y, d_out, d_h2, g_pre, g_post)


def _first_bwd(x, g, d_h1, d_x2, after=()):
    s, w = x.shape
    rows = min(ROWS, s)

    def body(x_ref, g_ref, dh_ref, dx2_ref, dx_ref, dg_ref):
        dx, dgc = _rms_bwd(x_ref[...], g_ref[...], dh_ref[...])
        dx_ref[...] = dx2_ref[...] + dx
        _accumulate(dg_ref, _sublane_sum(dgc), pl.program_id(0))

    return _call(
        body, name="first_bwd", grid=(s // rows,), after=after,
        in_specs=[_row_spec(rows, w), _fixed_spec(1, w), _row_spec(rows, w), _row_spec(rows, w)],
        out_specs=[_row_spec(rows, w), _fixed_spec(SUBLANE, w)],
        out_shape=[jax.ShapeDtypeStruct((s, w), F32), jax.ShapeDtypeStruct((SUBLANE, w), F32)],
        compiler_params=_params("arbitrary"),
    )(x, g, d_h1, d_x2)


def _shift_down(v, k):
    t = lax.broadcasted_iota(jnp.int32, v.shape, 0)
    return jnp.where(t >= k, pltpu.roll(v, k, 0), 0.0)


def _shift_up(v, k):
    n = v.shape[0]
    t = lax.broadcasted_iota(jnp.int32, v.shape, 0)
    return jnp.where(t < n - k, pltpu.roll(v, n - k, 0), 0.0)


def _conv_core(u, b, c, w):
    z = c * u
    conv = w[0:1, :] * _shift_down(z, 2) + w[1:2, :] * _shift_down(z, 1) + w[2:3, :] * z
    return z, conv, b * conv


def _conv_fwd(proj, conv_w, g, n_groups):
    s = proj.shape[0]

    def body(u_ref, b_ref, c_ref, w_ref, g_ref, o_ref):
        _, _, yr = _conv_core(u_ref[...], b_ref[...], c_ref[...], w_ref[...])
        o_ref[...] = (yr * _rstd(yr) * g_ref[...]).astype(o_ref.dtype)

    col = lambda k: pl.BlockSpec((s, HEAD), lambda i: (0, k * n_groups + i))
    return pl.pallas_call(
        body, name="conv_fwd", grid=(n_groups,),
        in_specs=[col(0), col(1), col(2), pl.BlockSpec((3, HEAD), lambda i: (0, i)), pl.BlockSpec((1, HEAD), lambda i: (0, i))],
        out_specs=pl.BlockSpec((s, HEAD), lambda i: (0, i)),
        out_shape=jax.ShapeDtypeStruct((s, n_groups * HEAD), BF16),
        compiler_params=_params("parallel"),
    )(proj, proj, proj, conv_w, g)


def _conv_bwd(proj, d_mix, conv_w, g, n_groups):
    s = proj.shape[0]
    width = n_groups * HEAD

    def body(u_ref, b_ref, c_ref, dy_ref, w_ref, g_ref, du_ref, db_ref, dc_ref, dg_ref, dw_ref):
        u, b, c, w = u_ref[...], b_ref[...], c_ref[...], w_ref[...]
        z, conv, yr = _conv_core(u, b, c, w)
        dyr, dgc = _rms_bwd(yr, g_ref[...], dy_ref[...])
        dconv = dyr * b
        db_ref[...] = (dyr * conv).astype(db_ref.dtype)
        dz = w[2:3, :] * dconv + w[1:2, :] * _shift_up(dconv, 1) + w[0:1, :] * _shift_up(dconv, 2)
        dc_ref[...] = (dz * u).astype(dc_ref.dtype)
        du_ref[...] = (dz * c).astype(du_ref.dtype)
        dg_ref[...] = _sublane_sum(dgc)
        dw_ref[0] = _sublane_sum(dconv * _shift_down(z, 2))
        dw_ref[1] = _sublane_sum(dconv * _shift_down(z, 1))
        dw_ref[2] = _sublane_sum(dconv * z)

    col = lambda k: pl.BlockSpec((s, HEAD), lambda i: (0, k * n_groups + i))
    grp = pl.BlockSpec((s, HEAD), lambda i: (0, i))
    return pl.pallas_call(
        body, name="conv_bwd", grid=(n_groups,),
        in_specs=[col(0), col(1), col(2), grp, pl.BlockSpec((3, HEAD), lambda i: (0, i)), pl.BlockSpec((1, HEAD), lambda i: (0, i))],
        out_specs=[grp, grp, grp, pl.BlockSpec((SUBLANE, HEAD), lambda i: (0, i)),
                   pl.BlockSpec((3, SUBLANE, HEAD), lambda i: (0, 0, i))],
        out_shape=[jax.ShapeDtypeStruct((s, width), BF16)] * 3
        + [jax.ShapeDtypeStruct((SUBLANE, width), F32), jax.ShapeDtypeStruct((3, SUBLANE, width), F32)],
        compiler_params=_params("parallel"),
    )(proj, proj, proj, d_mix, conv_w, g)


def _rope_tables(s, n_heads):
    pos = jnp.arange(s, dtype=F32)
    inv_freq = jnp.power(ROPE_THETA, -jnp.arange(0, ROPE, 2, dtype=F32) / ROPE)
    ang = pos[:, None] * inv_freq[None, :]
    cos, sin = jnp.cos(ang), jnp.sin(ang)
    cs = jnp.concatenate([cos, cos], axis=1)
    sn = jnp.concatenate([-sin, sin], axis=1)
    pad = jnp.zeros((s, LANE - ROPE), F32)
    return (jnp.tile(cs, (1, n_heads)), jnp.tile(sn, (1, n_heads)),
            jnp.concatenate([cs, pad], axis=1), jnp.concatenate([sn, pad], axis=1))


def _swap_halves(v):
    w = v.shape[1]
    lane = lax.broadcasted_iota(jnp.int32, v.shape, 1)
    first = (lane % ROPE) < (ROPE // 2)
    return jnp.where(first, pltpu.roll(v, w - ROPE // 2, 1), pltpu.roll(v, ROPE // 2, 1))


def _pack_heads(q, kv, proj, kr_col, tables, n_heads, after=()):
    s = q.shape[0]
    rows = min(ROWS, s)
    cq, sq, ck, sk = tables
    wq = n_heads * ROPE

    def body(q_ref, kv_ref, kr_ref, cq_ref, sq_ref, ck_ref, sk_ref, qo_ref, ko_ref, vo_ref):
        qr = q_ref[:, n_heads * HEAD:]
        qr = qr * cq_ref[...] + _swap_halves(qr) * sq_ref[...]
        krv = kr_ref[...]
        krv = krv * ck_ref[...] + _swap_halves(krv) * sk_ref[...]
        for h in range(n_heads):
            qo_ref[h] = jnp.concatenate([q_ref[:, h * HEAD:(h + 1) * HEAD], qr[:, h * ROPE:(h + 1) * ROPE]], axis=1).astype(BF16)
            ko_ref[h] = jnp.concatenate([kv_ref[:, 2 * h * HEAD:(2 * h + 1) * HEAD], krv[:, :ROPE]], axis=1).astype(BF16)
            vo_ref[h] = kv_ref[:, (2 * h + 1) * HEAD:(2 * h + 2) * HEAD].astype(BF16)

    hs = lambda w: pl.BlockSpec((n_heads, rows, w), lambda i: (0, i, 0))
    return _call(
        body, name="pack_heads", grid=(s // rows,), after=after,
        in_specs=[_row_spec(rows, q.shape[1]), _row_spec(rows, kv.shape[1]), pl.BlockSpec((rows, LANE), lambda i: (i, kr_col // LANE)),
                  _row_spec(rows, wq), _row_spec(rows, wq), _row_spec(rows, LANE), _row_spec(rows, LANE)],
        out_specs=[hs(QK), hs(QK), hs(HEAD)],
        out_shape=[jax.ShapeDtypeStruct((n_heads, s, QK), BF16), jax.ShapeDtypeStruct((n_heads, s, QK), BF16),
                   jax.ShapeDtypeStruct((n_heads, s, HEAD), BF16)],
        compiler_params=_params("parallel"),
    )(q, kv, proj, cq, sq, ck, sk)


def _unpack_heads(dq, dk, dv, tables, n_heads):
    s = dq.shape[1]
    rows = min(ROWS, s)
    cq, sq, ck, sk = tables
    wq = n_heads * ROPE

    def body(dq_ref, dk_ref, dv_ref, cq_ref, sq_ref, ck_ref, sk_ref, qo_ref, kvo_ref, kro_ref):
        dqr = jnp.concatenate([dq_ref[h][:, HEAD:] for h in range(n_heads)], axis=1)
        dqr = dqr * cq_ref[...] - _swap_halves(dqr) * sq_ref[...]
        dkr = dk_ref[0][:, HEAD:]
        for h in range(1, n_heads):
            dkr = dkr + dk_ref[h][:, HEAD:]
        dkr = jnp.concatenate([dkr, jnp.zeros((rows, LANE - ROPE), F32)], axis=1)
        dkr = dkr * ck_ref[...] - _swap_halves(dkr) * sk_ref[...]
        kro_ref[...] = dkr.astype(kro_ref.dtype)
        qo_ref[:, n_heads * HEAD:] = dqr.astype(qo_ref.dtype)
        for h in range(n_heads):
            qo_ref[:, h * HEAD:(h + 1) * HEAD] = dq_ref[h][:, :HEAD].astype(qo_ref.dtype)
            kvo_ref[:, 2 * h * HEAD:(2 * h + 1) * HEAD] = dk_ref[h][:, :HEAD].astype(kvo_ref.dtype)
            kvo_ref[:, (2 * h + 1) * HEAD:(2 * h + 2) * HEAD] = dv_ref[h].astype(kvo_ref.dtype)

    hs = lambda w: pl.BlockSpec((n_heads, rows, w), lambda i: (0, i, 0))
    return pl.pallas_call(
        body, name="unpack_heads", grid=(s // rows,),
        in_specs=[hs(QK), hs(QK), hs(HEAD), _row_spec(rows, wq), _row_spec(rows, wq), _row_spec(rows, LANE), _row_spec(rows, LANE)],
        out_specs=[_row_spec(rows, n_heads * QK), _row_spec(rows, 2 * n_heads * HEAD), _row_spec(rows, LANE)],
        out_shape=[jax.ShapeDtypeStruct((s, n_heads * QK), BF16), jax.ShapeDtypeStruct((s, 2 * n_heads * HEAD), BF16),
                   jax.ShapeDtypeStruct((s, LANE), BF16)],
        compiler_params=_params("parallel"),
    )(dq, dk, dv, cq, sq, ck, sk)


TQ = 256


LOG2_E = 1.4426950408889634


def _softmax_parts(q, k):
    tq, n_keys = q.shape[0], k.shape[0]
    sc = lax.dot_general(q, k, (NT, ((), ())), preferred_element_type=F32) * (QK ** -0.5 * LOG2_E)
    row = lax.broadcasted_iota(jnp.int32, (tq, tq), 0)
    col = lax.broadcasted_iota(jnp.int32, (tq, tq), 1)
    own = jnp.where(col // CHUNK <= row // CHUNK, sc[:, n_keys - tq:], NEG_INF)
    sc = own if n_keys == tq else jnp.concatenate([sc[:, :n_keys - tq], own], axis=1)
    e = jnp.exp2(sc - jnp.max(sc, axis=-1, keepdims=True))
    return e, 1.0 / jnp.sum(e, axis=-1, keepdims=True)


def _attn_fwd(q, k, v, g):
    n_heads, s, _ = q.shape
    tq = min(TQ, s)
    assert tq % CHUNK == 0 and s % tq == 0

    def body(q_ref, k_ref, v_ref, g_ref, o_ref, y_ref):
        for c in range(s // tq):
            rows, n_keys = pl.ds(c * tq, tq), (c + 1) * tq
            e, inv = _softmax_parts(q_ref[rows, :], k_ref[0:n_keys, :])
            o = jnp.dot(e.astype(BF16), v_ref[0:n_keys, :], preferred_element_type=F32) * inv
            o_ref[rows, :] = o
            y_ref[rows, :] = (o * _rstd(o) * g_ref[...]).astype(y_ref.dtype)

    head = lambda w: pl.BlockSpec((None, s, w), lambda h: (h, 0, 0))
    return pl.pallas_call(
        body, name="attn_fwd", grid=(n_heads,),
        in_specs=[head(QK), head(QK), head(HEAD), pl.BlockSpec((1, HEAD), lambda h: (0, h))],
        out_specs=[head(HEAD), pl.BlockSpec((s, HEAD), lambda h: (0, h))],
        out_shape=[jax.ShapeDtypeStruct((n_heads, s, HEAD), F32), jax.ShapeDtypeStruct((s, n_heads * HEAD), BF16)],
        compiler_params=_params("parallel"),
    )(q, k, v, g)


def _attn_bwd(q, k, v, o, d_mix, g, col0, after=()):
    n_heads, s, _ = q.shape
    tq = min(TQ, s)

    def body(q_ref, k_ref, v_ref, o_ref, dy_ref, g_ref, dq_ref, dk_ref, dv_ref, dg_ref):
        dg = None
        for c in reversed(range(s // tq)):
            rows, n_keys = pl.ds(c * tq, tq), (c + 1) * tq
            qv, kv_, vv = q_ref[rows, :], k_ref[0:n_keys, :], v_ref[0:n_keys, :]
            do, dgc = _rms_bwd(o_ref[rows, :], g_ref[...], dy_ref[rows, :])
            do = do.astype(BF16)
            dg = _sublane_sum(dgc) if dg is None else dg + _sublane_sum(dgc)
            e, inv = _softmax_parts(qv, kv_)
            p = e * inv
            dp = lax.dot_general(do, vv, (NT, ((), ())), preferred_element_type=F32)
            ds = (p * (dp - jnp.sum(p * dp, axis=-1, keepdims=True)) * (QK ** -0.5)).astype(BF16)
            dq_ref[rows, :] = jnp.dot(ds, kv_, preferred_element_type=F32)
            dk = lax.dot_general(ds, qv, (TN, ((), ())), preferred_element_type=F32)
            dv = lax.dot_general(p.astype(BF16), do, (TN, ((), ())), preferred_element_type=F32)
            if n_keys == s:
                dk_ref[...] = dk
                dv_ref[...] = dv
            else:
                dk_ref[0:n_keys, :] += dk
                dv_ref[0:n_keys, :] += dv
        dg_ref[...] = dg

    c0 = col0 // HEAD
    head = lambda w: pl.BlockSpec((None, s, w), lambda h: (h, 0, 0))
    return _call(
        body, name="attn_bwd", grid=(n_heads,), after=after,
        in_specs=[head(QK), head(QK), head(HEAD), head(HEAD), pl.BlockSpec((s, HEAD), lambda h: (0, c0 + h)),
                  pl.BlockSpec((1, HEAD), lambda h: (0, h))],
        out_specs=[head(QK), head(QK), head(HEAD), pl.BlockSpec((SUBLANE, HEAD), lambda h: (0, h))],
        out_shape=[jax.ShapeDtypeStruct((n_heads, s, QK), F32), jax.ShapeDtypeStruct((n_heads, s, QK), F32),
                   jax.ShapeDtypeStruct((n_heads, s, HEAD), F32), jax.ShapeDtypeStruct((SUBLANE, n_heads * HEAD), F32)],
        compiler_params=_params("parallel"),
    )(q, k, v, o, d_mix, g)


TILE_M = 1024
TILE_N = 1024


def _up_fwd(h2, w_up):
    s, d = h2.shape
    nb, _, fb = w_up.shape
    tm = min(TILE_M,s)

    def epilogue(acc):
        r = jnp.maximum(acc, 0.0)
        return r * r, r

    blk = pl.BlockSpec((tm, fb), lambda i, j: (i, j))
    return _matmul("up_fwd", h2, w_up, grid=(s // tm, nb),
                   a_spec=pl.BlockSpec((tm, d), lambda i, j: (i, 0)),
                   b_spec=pl.BlockSpec((None, d, fb), lambda i, j: (j, 0, 0)),
                   out_shape=[jax.ShapeDtypeStruct((s, nb * fb), BF16)] * 2, out_specs=[blk, blk],
                   contract=NN, epilogue=epilogue)


def _down_fwd(a, w_down):
    s, f = a.shape
    d = w_down.shape[1]
    tm, tn, tk = min(TILE_M,s), min(TILE_N,d), 2048
    nk = f // tk
    return _matmul("down_fwd", a, w_down, grid=(s // tm, d // tn, nk),
                   a_spec=pl.BlockSpec((tm, tk), lambda i, j, k: (i, k)),
                   b_spec=pl.BlockSpec((tk, tn), lambda i, j, k: (k, j)),
                   out_shape=jax.ShapeDtypeStruct((s, d), F32),
                   out_specs=pl.BlockSpec((tm, tn), lambda i, j, k: (i, j)),
                   contract=NN, nk=nk, acc_shape=(tm, tn))


def _down_bwd_act(d_m, w_down, r, after=()):
    s, d = d_m.shape
    f = w_down.shape[0]
    tm, tn = min(TILE_M,s), min(TILE_N,f)
    blk = pl.BlockSpec((tm, tn), lambda i, j: (i, j))
    return _matmul("down_bwd_act", d_m, w_down, grid=(s // tm, f // tn), after=after,
                   a_spec=pl.BlockSpec((tm, d), lambda i, j: (i, 0)),
                   b_spec=pl.BlockSpec((tn, d), lambda i, j: (j, 0)),
                   out_shape=jax.ShapeDtypeStruct((s, f), BF16), out_specs=blk, contract=NT,
                   extras=(r,), extra_specs=(blk,),
                   epilogue=lambda acc, rv: (acc * (2.0 * rv.astype(F32)),))


def _up_bwd_act(d_up, w_up, after=()):
    s, _ = d_up.shape
    nb, d, fb = w_up.shape
    tm, tn = min(2 * TILE_M, s), min(TILE_N,d)
    return _matmul("up_bwd_act", d_up, w_up, grid=(s // tm, d // tn, nb), after=after,
                   a_spec=pl.BlockSpec((tm, fb), lambda i, j, k: (i, k)),
                   b_spec=pl.BlockSpec((None, tn, fb), lambda i, j, k: (k, j, 0)),
                   out_shape=jax.ShapeDtypeStruct((s, d), F32),
                   out_specs=pl.BlockSpec((tm, tn), lambda i, j, k: (i, j)),
                   contract=NT, nk=nb, acc_shape=(tm, tn))


def _half_grad(name, a, b, core, home, received, after, *, grid, a_block, a_map, b_block, b_map, o_block, o_map, out_shape):
    n_after = len(after)
    pick = (lambda ref: ref[0]) if home else (lambda ref: 1 - ref[0])

    def body(core_ref, a_ref, b_ref, *rest):
        acc = lax.dot_general(a_ref[...], b_ref[...], (TN, ((), ())), preferred_element_type=F32)
        if received is not None:
            acc = acc + rest[0][...].astype(F32)
        rest[-1][...] = acc.astype(rest[-1].dtype)

    wrap = lambda fn: (lambda i, j, core_ref: fn(i, j, pick(core_ref)))
    o_spec = pl.BlockSpec(o_block, wrap(o_map))
    extra = [] if received is None else [o_spec]
    operands = [] if received is None else [received]
    return pl.pallas_call(
        body, name=name,
        grid_spec=pltpu.PrefetchScalarGridSpec(
            num_scalar_prefetch=1, grid=grid,
            in_specs=[pl.BlockSpec(a_block, wrap(a_map)), pl.BlockSpec(b_block, wrap(b_map))] + extra + [ANY] * n_after,
            out_specs=o_spec),
        out_shape=out_shape,
        compiler_params=_params("parallel", "parallel"),
    )(core, a, b, *operands, *after)


def _down_half_grad(name, a, d_m, core, home, received=None, after=()):
    s, f = a.shape
    d = d_m.shape[1]
    r = f // N_DEV
    tn = min(TILE_N, d)
    return _half_grad(name, a, d_m, core, home, received, after, grid=(N_CHIP, d // tn),
                      a_block=(s, r), a_map=lambda k, j, p: (0, 2 * k + p),
                      b_block=(s, tn), b_map=lambda k, j, p: (0, j),
                      o_block=(None, r, tn), o_map=lambda k, j, p: (k, 0, j),
                      out_shape=jax.ShapeDtypeStruct((N_CHIP, r, d), BF16))


def _up_half_grad(name, h2, d_up, core, home, received=None, after=()):
    s, d = h2.shape
    fb = d_up.shape[1] // N_DEV
    tm = min(TILE_M, d)
    return _half_grad(name, h2, d_up, core, home, received, after, grid=(d // tm, N_CHIP),
                      a_block=(s, tm), a_map=lambda i, k, p: (0, i),
                      b_block=(s, fb), b_map=lambda i, k, p: (0, 2 * k + p),
                      o_block=(None, tm, fb), o_map=lambda i, k, p: (k, i, 0),
                      out_shape=jax.ShapeDtypeStruct((N_CHIP, d, fb), BF16))


def _in_pad(in_width):
    return -(-in_width // LANE) * LANE


def _join_col_shards(blocks):
    n, r, w = blocks.shape
    rows = min(ROWS, r)
    width = _in_pad(n * w)

    def body(x_ref, o_ref):
        tail = [jnp.zeros((rows, width - n * w), o_ref.dtype)] if width > n * w else []
        o_ref[...] = jnp.concatenate([x_ref[j] for j in range(n)] + tail, axis=1)

    return pl.pallas_call(
        body, name="join_col_shards", grid=(r // rows,),
        in_specs=[pl.BlockSpec((n, rows, w), lambda i: (0, i, 0))], out_specs=_row_spec(rows, width),
        out_shape=jax.ShapeDtypeStruct((r, width), blocks.dtype),
        compiler_params=_params("parallel"),
    )(blocks)


def _permute_q_cols(w_uq, n_heads):
    r = w_uq.shape[0]
    w3 = w_uq.reshape(r, n_heads, QK)
    return jnp.concatenate([w3[:, :, :HEAD].reshape(r, n_heads * HEAD), w3[:, :, HEAD:].reshape(r, n_heads * ROPE)], axis=1)


def _unpermute_q_rows(wt, n_heads):
    r = wt.shape[1]
    nope = wt[:n_heads * HEAD].reshape(n_heads, HEAD, r)
    rope = wt[n_heads * HEAD:].reshape(n_heads, ROPE, r)
    return jnp.concatenate([nope, rope], axis=1).reshape(n_heads * QK, r)


def _local_step(x, tgt, gains, weights, grads, first_after=()):
    pre_mix_g, q_norm_g, kv_norm_g, conv_out_g, attn_out_g, post_mix_g, pre_mlp_g, post_mlp_g = gains
    s, d = x.shape
    conv_width = conv_out_g.shape[1]
    n_groups = conv_width // HEAD
    r_q, r_kv = q_norm_g.shape[1], kv_norm_g.shape[1]
    n_heads = attn_out_g.shape[1] // HEAD
    c_q0 = 3 * conv_width
    c_kv0 = c_q0 + r_q
    c_kr0 = c_kv0 + r_kv
    in_pad = _in_pad(c_kr0 + ROPE)
    tn_in = in_pad // 5 if in_pad % (5 * LANE) == 0 else LANE
    tables = _rope_tables(s, n_heads)

    h1 = _rms_fwd("pre_mix_norm", x, pre_mix_g, after=first_after)
    weights.forward(0, (h1, *tables))
    weights.relay(0, ())
    w_in_p, conv_w = weights.ready(0, ())
    proj = _mm_nn("in_proj", h1, w_in_p, F32, TILE_M, tn_in)
    y_conv = _conv_fwd(proj, conv_w, conv_out_g, n_groups)
    qn = _rms_fwd("q_norm", proj, q_norm_g, cols=(c_q0, r_q))
    kvn = _rms_fwd("kv_norm", proj, kv_norm_g, cols=(c_kv0, r_kv))
    weights.forward(1, (y_conv, qn, kvn))
    w_uq_p, w_ukv, w_o = weights.ready(1, ())
    q = _mm_nn("q_up", qn, w_uq_p, F32, TILE_M, TILE_N)
    kv = _mm_nn("kv_up", kvn, w_ukv, F32, TILE_M, TILE_N)
    qh, kh, vh = _pack_heads(q, kv, proj, c_kr0, tables, n_heads, after=weights.forward(2, (q, kv)))
    o, y_attn = _attn_fwd(qh, kh, vh, attn_out_g)
    mix = jnp.concatenate([y_conv, y_attn], axis=1)
    y = _mm_nn("out_proj", mix, w_o, F32, TILE_M, TILE_N, after=weights.forward(3, (y_attn,)))
    x2, h2 = _mid_fwd(x, y, post_mix_g, pre_mlp_g)
    weights.relay(2, (h2,))
    (w_up,) = weights.ready(2, ())
    a, r = _up_fwd(h2, w_up)
    weights.relay(3, (a,))
    (w_down,) = weights.ready(3, ())
    m = _down_fwd(a, w_down)

    d_out, d_m, dg_post_mlp, loss_part = _head(m, x2, tgt, post_mlp_g)
    core = grads.core
    away = _down_half_grad("down_bwd_w_away", a, d_m, core, home=False)
    d_up = _down_bwd_act(d_m, w_down, r, after=grads.send_away(0, away))
    sums = _down_half_grad("down_bwd_w_home", a, d_m, core, home=True, received=grads.received(0, (d_up,)))
    away = _up_half_grad("up_bwd_w_away", h2, d_up, core, home=False, after=grads.send_sums(0, (sums,)))
    d_h2 = _up_bwd_act(d_up, w_up, after=grads.send_away(1, away))
    sums = _up_half_grad("up_bwd_w_home", h2, d_up, core, home=True, received=grads.received(1, (d_h2,)))
    d_x2, d_y, dg_pre_mlp, dg_post_mix = _mid_bwd(x2, y, d_out, d_h2, pre_mlp_g, post_mix_g, after=grads.send_sums(1, (sums,)))
    d_mix = _mm_nt("out_proj_bwd_act", d_y, w_o, F32, TILE_M, TILE_N)
    gw_o = _mm_tn("out_proj_bwd_w", mix, d_y, BF16, TILE_M, TILE_N)
    dqh, dkh, dvh, dg_attn = _attn_bwd(qh, kh, vh, o, d_mix, attn_out_g, conv_width, after=grads.full(2, (gw_o,)))
    d_q, d_kv, d_kr = _unpack_heads(dqh, dkh, dvh, tables, n_heads)
    d_qn = _mm_nt("q_up_bwd_act", d_q, w_uq_p, F32, TILE_M, TILE_N)
    d_kvn = _mm_nt("kv_up_bwd_act", d_kv, w_ukv, F32, TILE_M, TILE_N)
    gw_uq_t = _mm_tn("q_up_bwd_w", d_q, qn, F32, TILE_M, TILE_N)
    gw_ukv = _mm_tn("kv_up_bwd_w", kvn, d_kv, BF16, TILE_M, TILE_N)
    d_cq, dg_q = _rms_bwd_call("q_norm_bwd", proj, q_norm_g, d_qn, BF16, cols=(c_q0, r_q), after=grads.full(3, (gw_uq_t, gw_ukv)))
    d_ckv, dg_kv = _rms_bwd_call("kv_norm_bwd", proj, kv_norm_g, d_kvn, BF16, cols=(c_kv0, r_kv))
    d_u, d_b, d_c, dg_conv, dw_conv = _conv_bwd(proj, d_mix, conv_w, conv_out_g, n_groups)
    d_proj = jnp.concatenate([d_u, d_b, d_c, d_cq, d_ckv, d_kr[:, :in_pad - c_kr0]], axis=1)
    gw_in_t = _mm_tn("in_proj_bwd_w", d_proj, h1, F32, tn_in, TILE_N)
    d_h1 = _mm_nt("in_proj_bwd_act", d_proj, w_in_p, F32, TILE_M, 512, after=grads.send_away(4, gw_in_t))
    grad_x, dg_pre_mix = _first_bwd(x, pre_mix_g, d_h1, d_x2, after=grads.full(4, (gw_in_t,), received=(d_h1,)))

    small = [dg_pre_mix, dg_q, dg_kv, dg_conv, dg_attn, dg_post_mix, dg_pre_mlp, dg_post_mlp,
             dw_conv[0], dw_conv[1], dw_conv[2], loss_part]
    return grad_x, jnp.concatenate(small, axis=1)


HBM = pl.BlockSpec(memory_space=pltpu.HBM)
SEM = pl.BlockSpec(memory_space=pltpu.SEMAPHORE)
IN_VMEM = pl.BlockSpec(memory_space=pltpu.VMEM)
SPLIT = pltpu.CompilerParams(has_side_effects=pltpu.SideEffectType.DATAFLOW_SIDE_EFFECTING)


def _in_hbm(a):
    return pltpu.with_memory_space_constraint(a, pltpu.HBM)


def _hbm_like(a):
    return pltpu.HBM(a.shape, a.dtype)


def _place():
    x, y, c = lax.axis_index("x"), lax.axis_index("y"), lax.axis_index("c")
    other_chips = [(1 - x, y), (x, 1 - y), (1 - x, 1 - y)]
    return x, y, c, other_chips


def _block(px, py, pc):
    return 4 * px + 2 * py + pc


def _relay_route(x, y, c):
    came_from = ((1 - x) * (1 - c) + x * c, y * (1 - c) + (1 - y) * c)
    goes_to = (x * (1 - c) + (1 - x) * c, (1 - y) * (1 - c) + y * c)
    return came_from, goes_to


def _gather_start(name, shards, groups, relayed=(), after=()):
    n, ng = len(shards), len(groups)
    lands = [lax.empty((N_DEV, *a.shape), a.dtype) for a in shards]

    def body(*refs):
        src, land = refs[:n], refs[n:2 * n]
        sems, token = refs[2 * n + len(after):2 * n + len(after) + 2 * ng], refs[-1]
        x, y, c, chips = _place()
        targets = [(x, y, 1 - c)] + [(*chip, c) for chip in chips]
        for gi, group in enumerate(groups):
            for i, w in enumerate(group):
                for k, to in enumerate(targets[:3] if gi in relayed else targets):
                    pltpu.make_async_remote_copy(
                        src_ref=src[w], dst_ref=land[w].at[_block(x, y, c)],
                        send_sem=sems[2 * gi].at[4 * i + k], recv_sem=sems[2 * gi + 1].at[4 * i + k],
                        device_id=to, device_id_type=MESH).start()
        token[...] = jnp.zeros_like(token)

    sem_shapes = [pltpu.SemaphoreType.DMA((4 * len(g),)) for g in groups for _ in range(2)]
    out = pl.pallas_call(
        body, name=name,
        in_specs=[HBM] * (2 * n) + [ANY] * len(after),
        out_specs=[SEM] * (2 * ng) + [HBM] * (2 * n) + [IN_VMEM],
        out_shape=sem_shapes + [_hbm_like(a) for a in shards] + [_hbm_like(a) for a in lands]
        + [jax.ShapeDtypeStruct((SUBLANE, LANE), F32)],
        input_output_aliases={i: 2 * ng + i for i in range(2 * n)},
        compiler_params=SPLIT,
    )(*[_in_hbm(a) for a in shards], *[_in_hbm(a) for a in lands], *after)
    sems = [(out[2 * gi], out[2 * gi + 1]) for gi in range(ng)]
    return sems, out[2 * ng:2 * ng + n], out[2 * ng + n:2 * ng + 2 * n], out[-1]


def _gather_forward(name, shards, lands, send1, recv1, after, relayed=False):
    n = len(lands)

    def body(*refs):
        src, land = refs[:n], refs[n:2 * n]
        s1, r1 = refs[2 * n], refs[2 * n + 1]
        s2, r2 = refs[2 * n + 2 + len(after)], refs[2 * n + 3 + len(after)]
        x, y, c, chips = _place()
        me, sibling = (x, y, c), (x, y, 1 - c)
        for j, chip in enumerate(chips[:2] if relayed else chips):
            for i in range(n):
                blk = land[i].at[_block(*chip, c)]
                pltpu.make_async_remote_copy(src_ref=blk, dst_ref=blk, send_sem=s1.at[4 * i + 1 + j], recv_sem=r1.at[4 * i + 1 + j],
                                             device_id=me, device_id_type=MESH).wait_recv()
                pltpu.make_async_remote_copy(src_ref=blk, dst_ref=blk, send_sem=s2.at[3 * i + j], recv_sem=r2.at[3 * i + j],
                                             device_id=sibling, device_id_type=MESH).start()
        if relayed:
            came_from, goes_to = _relay_route(x, y, c)
            for i in range(n):
                blk = land[i].at[_block(*came_from, c)]
                pltpu.make_async_remote_copy(src_ref=blk, dst_ref=blk, send_sem=s2.at[3 * i + 2], recv_sem=r2.at[3 * i + 2],
                                             device_id=(*goes_to, c), device_id_type=MESH).start()
        for i in range(n):
            blk = land[i].at[_block(x, y, 1 - c)]
            pltpu.make_async_remote_copy(src_ref=blk, dst_ref=blk, send_sem=s1.at[4 * i], recv_sem=r1.at[4 * i],
                                         device_id=me, device_id_type=MESH).wait_recv()
            for k in range(3 if relayed else 4):
                pltpu.make_async_remote_copy(src_ref=src[i], dst_ref=land[i].at[_block(x, y, c)], send_sem=s1.at[4 * i + k],
                                             recv_sem=r1.at[4 * i + k], device_id=sibling, device_id_type=MESH).wait_send()

    sem = pltpu.SemaphoreType.DMA((3 * n,))
    out = pl.pallas_call(
        body, name=name,
        in_specs=[HBM] * (2 * n) + [SEM, SEM] + [ANY] * len(after),
        out_specs=[SEM, SEM] + [HBM] * n,
        out_shape=[sem, sem] + [_hbm_like(a) for a in lands],
        input_output_aliases={n + i: 2 + i for i in range(n)},
        compiler_params=SPLIT,
    )(*shards, *lands, send1, recv1, *after)
    return (out[0], out[1]), out[2:]


def _gather_relay_forward(name, lands, send2, recv2, after):
    n = len(lands)

    def body(*refs):
        land, s2, r2 = refs[:n], refs[n], refs[n + 1]
        s3, r3 = refs[n + 2 + len(after)], refs[n + 3 + len(after)]
        x, y, c, _ = _place()
        me, sibling = (x, y, c), (x, y, 1 - c)
        came_from, _ = _relay_route(x, y, c)
        for i in range(n):
            blk = land[i].at[_block(1 - x, 1 - y, c)]
            pltpu.make_async_remote_copy(src_ref=blk, dst_ref=blk, send_sem=s2.at[3 * i + 2], recv_sem=r2.at[3 * i + 2],
                                         device_id=me, device_id_type=MESH).wait_recv()
            pltpu.make_async_remote_copy(src_ref=blk, dst_ref=blk, send_sem=s3.at[i], recv_sem=r3.at[i],
                                         device_id=sibling, device_id_type=MESH).start()
            sent = land[i].at[_block(*came_from, c)]
            pltpu.make_async_remote_copy(src_ref=sent, dst_ref=sent, send_sem=s2.at[3 * i + 2], recv_sem=r2.at[3 * i + 2],
                                         device_id=me, device_id_type=MESH).wait_send()

    sem = pltpu.SemaphoreType.DMA((n,))
    out = pl.pallas_call(
        body, name=name,
        in_specs=[HBM] * n + [SEM, SEM] + [ANY] * len(after),
        out_specs=[SEM, SEM] + [HBM] * n,
        out_shape=[sem, sem] + [_hbm_like(a) for a in lands],
        input_output_aliases={i: 2 + i for i in range(n)},
        compiler_params=SPLIT,
    )(*lands, send2, recv2, *after)
    return (out[0], out[1]), out[2:]


def _gather_wait(name, lands, send2, recv2, after, relay_sems=None):
    n = len(lands)
    n_sems = 2 if relay_sems is None else 4

    def body(*refs):
        land, s2, r2 = refs[:n], refs[n], refs[n + 1]
        x, y, c, chips = _place()
        me = (x, y, c)
        for i in range(n):
            for j, chip in enumerate(chips if relay_sems is None else chips[:2]):
                got = land[i].at[_block(*chip, 1 - c)]
                pltpu.make_async_remote_copy(src_ref=got, dst_ref=got, send_sem=s2.at[3 * i + j], recv_sem=r2.at[3 * i + j],
                                             device_id=me, device_id_type=MESH).wait_recv()
                sent = land[i].at[_block(*chip, c)]
                pltpu.make_async_remote_copy(src_ref=sent, dst_ref=sent, send_sem=s2.at[3 * i + j], recv_sem=r2.at[3 * i + j],
                                             device_id=me, device_id_type=MESH).wait_send()
            if relay_sems is not None:
                s3, r3 = refs[n + 2], refs[n + 3]
                got = land[i].at[_block(1 - x, 1 - y, 1 - c)]
                pltpu.make_async_remote_copy(src_ref=got, dst_ref=got, send_sem=s3.at[i], recv_sem=r3.at[i],
                                             device_id=me, device_id_type=MESH).wait_recv()
                sent = land[i].at[_block(1 - x, 1 - y, c)]
                pltpu.make_async_remote_copy(src_ref=sent, dst_ref=sent, send_sem=s3.at[i], recv_sem=r3.at[i],
                                             device_id=me, device_id_type=MESH).wait_send()

    return pl.pallas_call(
        body, name=name,
        in_specs=[HBM] * n + [SEM] * n_sems + [ANY] * len(after), out_specs=[HBM] * n, out_shape=[_hbm_like(a) for a in lands],
        input_output_aliases={i: i for i in range(n)},
        compiler_params=SPLIT,
    )(*lands, send2, recv2, *(relay_sems or ()), *after)


def _pair_exchange(name, grads, shard_rows):
    n = len(grads)
    shapes = [(g.shape[1:] if r is None else (r, g.shape[1])) for g, r in zip(grads, shard_rows)]

    def body(*refs):
        ins, recv = refs[:n], refs[n:2 * n]
        send_sems, recv_sems = refs[2 * n:]
        x, y, c, _ = _place()
        sends = []
        for w in range(n):
            for k in range(N_CHIP):
                j, r = 2 * k + 1 - c, shard_rows[w]
                src = ins[w].at[j] if r is None else ins[w].at[pl.ds(pl.multiple_of(j * r, SUBLANE), r), :]
                sends.append(pltpu.make_async_remote_copy(
                    src_ref=src, dst_ref=recv[w].at[k],
                    send_sem=send_sems.at[w, k], recv_sem=recv_sems.at[w, k],
                    device_id=(x, y, 1 - c), device_id_type=MESH))
        for cp in sends:
            cp.start()
        for cp in sends:
            cp.wait()

    return pl.pallas_call(
        body, name=name,
        in_specs=[ANY] * n, out_specs=[ANY] * n,
        out_shape=[jax.ShapeDtypeStruct((N_CHIP, *shape), g.dtype) for g, shape in zip(grads, shapes)],
        scratch_shapes=[pltpu.SemaphoreType.DMA((n, N_CHIP))] * 2,
    )(*grads)


def _pair_sum_rows(name, grad, received, core):
    _, r, c = received.shape
    tc = _fit(c, 512)

    def body(core_ref, a_ref, b_ref, o_ref):
        o_ref[...] = (a_ref[...] + b_ref[...]).astype(o_ref.dtype)

    spec = pl.BlockSpec((None, r, tc), lambda k, i, core_ref: (k, 0, i))
    return pl.pallas_call(
        body, name=name,
        grid_spec=pltpu.PrefetchScalarGridSpec(
            num_scalar_prefetch=1, grid=(N_CHIP, c // tc),
            in_specs=[pl.BlockSpec((r, tc), lambda k, i, core_ref: (2 * k + core_ref[0], i)), spec],
            out_specs=spec),
        out_shape=jax.ShapeDtypeStruct(received.shape, BF16),
        compiler_params=_params("parallel", "parallel"),
    )(core, grad, received)


def _pair_sum(name, grad, received, core):
    _, r, c = received.shape
    rows = min(ROWS, r)
    assert r % rows == 0

    def body(core_ref, a_ref, b_ref, o_ref):
        o_ref[...] = (a_ref[...].astype(F32) + b_ref[...].astype(F32)).astype(o_ref.dtype)

    spec = pl.BlockSpec((None, rows, c), lambda k, i, core_ref: (k, i, 0))
    return pl.pallas_call(
        body, name=name,
        grid_spec=pltpu.PrefetchScalarGridSpec(
            num_scalar_prefetch=1, grid=(N_CHIP, r // rows),
            in_specs=[pl.BlockSpec((None, None, rows, c), lambda k, i, core_ref: (k, core_ref[0], i, 0)), spec],
            out_specs=spec),
        out_shape=jax.ShapeDtypeStruct(received.shape, received.dtype),
        compiler_params=_params("parallel", "parallel"),
    )(core, grad.reshape(N_CHIP, 2, r, c), received)


def _away_shard(src, k, c, shard_rows):
    if shard_rows is None:
        return src.at[k]
    return src.at[pl.ds(pl.multiple_of((2 * k + 1 - c) * shard_rows, SUBLANE), shard_rows), :]


def _pair_send_start(name, away, shard_rows=None):
    shape = away.shape if shard_rows is None else (N_CHIP, shard_rows, away.shape[1])
    land = lax.empty(shape, away.dtype)

    def body(src, dst, send, recv, src_thru, dst_thru, token):
        x, y, c, _ = _place()
        for k in range(N_CHIP):
            pltpu.make_async_remote_copy(src_ref=_away_shard(src, k, c, shard_rows), dst_ref=dst.at[k], send_sem=send.at[k],
                                         recv_sem=recv.at[k], device_id=(x, y, 1 - c), device_id_type=MESH).start()
        token[...] = jnp.zeros_like(token)

    sem = pltpu.SemaphoreType.DMA((N_CHIP,))
    out = pl.pallas_call(
        body, name=name,
        in_specs=[HBM, HBM], out_specs=[SEM, SEM, HBM, HBM, IN_VMEM],
        out_shape=[sem, sem, _hbm_like(away), _hbm_like(land), jax.ShapeDtypeStruct((SUBLANE, LANE), F32)],
        input_output_aliases={0: 2, 1: 3},
        compiler_params=SPLIT,
    )(_in_hbm(away), _in_hbm(land))
    return (out[0], out[1]), out[2], out[3], out[4]


def _pair_send_wait(name, sems, src, land, after, shard_rows=None):
    def body(src_ref, dst_ref, send, recv, *rest):
        x, y, c, _ = _place()
        for k in range(N_CHIP):
            pltpu.make_async_remote_copy(src_ref=_away_shard(src_ref, k, c, shard_rows), dst_ref=dst_ref.at[k], send_sem=send.at[k],
                                         recv_sem=recv.at[k], device_id=(x, y, 1 - c), device_id_type=MESH).wait()

    return pl.pallas_call(
        body, name=name,
        in_specs=[HBM, HBM, SEM, SEM] + [ANY] * len(after), out_specs=HBM, out_shape=_hbm_like(land),
        input_output_aliases={1: 0},
        compiler_params=SPLIT,
    )(src, land, *sems, *after)


def _chip_send_start(name, sums):
    n = len(sums)
    lands = [lax.empty(a.shape, a.dtype) for a in sums]

    def body(*refs):
        src, land = refs[:n], refs[n:2 * n]
        send, recv, token = refs[2 * n], refs[2 * n + 1], refs[-1]
        x, y, c, chips = _place()
        for w in range(n):
            for j, (px, py) in enumerate(chips):
                pltpu.make_async_remote_copy(
                    src_ref=src[w].at[2 * px + py], dst_ref=land[w].at[2 * x + y],
                    send_sem=send.at[3 * w + j], recv_sem=recv.at[3 * w + j],
                    device_id=(px, py, c), device_id_type=MESH).start()
        token[...] = jnp.zeros_like(token)

    sem = pltpu.SemaphoreType.DMA((3 * n,))
    out = pl.pallas_call(
        body, name=name,
        in_specs=[HBM] * (2 * n),
        out_specs=[SEM, SEM] + [HBM] * (2 * n) + [IN_VMEM],
        out_shape=[sem, sem] + [_hbm_like(a) for a in sums] + [_hbm_like(a) for a in lands]
        + [jax.ShapeDtypeStruct((SUBLANE, LANE), F32)],
        input_output_aliases={i: 2 + i for i in range(2 * n)},
        compiler_params=SPLIT,
    )(*[_in_hbm(a) for a in sums], *[_in_hbm(a) for a in lands])
    return (out[0], out[1]), out[2:2 + n], out[2 + n:2 + 2 * n], out[-1]


def _chip_send_wait(name, groups, after):
    counts = [len(g[1]) for g in groups]
    n = sum(counts)

    def body(*refs):
        src, land = refs[:n], refs[n:2 * n]
        sems = refs[2 * n:2 * n + 2 * len(groups)]
        x, y, c, chips = _place()
        w = 0
        for gi, count in enumerate(counts):
            for i in range(count):
                for j, (px, py) in enumerate(chips):
                    pltpu.make_async_remote_copy(
                        src_ref=src[w].at[2 * px + py], dst_ref=land[w].at[2 * px + py],
                        send_sem=sems[2 * gi].at[3 * i + j], recv_sem=sems[2 * gi + 1].at[3 * i + j],
                        device_id=(px, py, c), device_id_type=MESH).wait()
                w += 1

    sums = [a for g in groups for a in g[1]]
    lands = [a for g in groups for a in g[2]]
    sems = [s for g in groups for s in g[0]]
    return pl.pallas_call(
        body, name=name,
        in_specs=[HBM] * (2 * n) + [SEM] * len(sems) + [ANY] * len(after),
        out_specs=[HBM] * n, out_shape=[_hbm_like(a) for a in lands],
        input_output_aliases={n + i: i for i in range(n)},
        compiler_params=SPLIT,
    )(*sums, *lands, *sems, *after)


def _small_all_reduce(part, after=()):
    _, w = part.shape

    def body(p_ref, *rest):
        o_ref, buf, send_sems, recv_sems = rest[len(after):]
        x, y, c, _ = _place()
        me = 4 * x + 2 * y + c
        buf[me] = jnp.sum(p_ref[...], axis=0, keepdims=True)
        copies = []
        for k in range(1, N_DEV):
            dx, dy, dc = (k >> 2) & 1, (k >> 1) & 1, k & 1
            copies.append(pltpu.make_async_remote_copy(
                src_ref=buf.at[me], dst_ref=buf.at[me], send_sem=send_sems.at[k - 1], recv_sem=recv_sems.at[k - 1],
                device_id=(x ^ dx, y ^ dy, c ^ dc), device_id_type=MESH))
        for cp in copies:
            cp.start()
        for cp in copies:
            cp.wait()
        tot = buf[0]
        for d in range(1, N_DEV):
            tot = tot + buf[d]
        o_ref[...] = tot
        loss = jnp.sum(tot[:, w - LANE:], axis=1, keepdims=True)
        o_ref[:, w - LANE:] = jnp.broadcast_to(loss, (1, LANE))

    return pl.pallas_call(
        body, name="small_all_reduce",
        in_specs=[IN_VMEM] + [ANY] * len(after), out_specs=IN_VMEM,
        out_shape=jax.ShapeDtypeStruct((1, w), F32),
        scratch_shapes=[pltpu.VMEM((N_DEV, 1, w), F32), pltpu.SemaphoreType.DMA((N_DEV - 1,)), pltpu.SemaphoreType.DMA((N_DEV - 1,))],
        compiler_params=pltpu.CompilerParams(vmem_limit_bytes=VMEM_LIMIT_BYTES),
    )(part, *after)


def _adamw(w, g, m, v):
    m = ADAM_B1 * m + (1.0 - ADAM_B1) * g
    v = ADAM_B2 * v + (1.0 - ADAM_B2) * (g * g)
    m_hat = m / (1.0 - ADAM_B1 ** ADAM_STEP)
    v_hat = v / (1.0 - ADAM_B2 ** ADAM_STEP)
    delta = -ADAM_LR * (m_hat / (jnp.sqrt(v_hat) + ADAM_EPS) + ADAM_WD * w)
    return delta, m, v


def _sum_adam(name, parts, sums, chip, w, m, v, after=()):
    _, r, c = w.shape
    n_after = len(after)
    by_rows = r % ROWS == 0 or r < ROWS
    tr, tc = (min(ROWS, r), c) if by_rows else (r, _fit(c, 512))
    at = (lambda i: (i, 0)) if by_rows else (lambda i: (0, i))

    def body(chip_ref, p_ref, own_ref, w_ref, m_ref, v_ref, *rest):
        g_ref, d_ref, mo_ref, vo_ref = rest[n_after:]
        g = None
        for k in range(N_CHIP):
            term = jnp.where(chip_ref[0] == k, own_ref[...], p_ref[k]).astype(F32)
            g = term if g is None else g + term
        g_ref[...] = g
        d_ref[...], mo_ref[...], vo_ref[...] = _adamw(w_ref[...], g, m_ref[...], v_ref[...])

    blk = pl.BlockSpec((None, tr, tc), lambda i, chip_ref: (0, *at(i)))
    out = jax.ShapeDtypeStruct((1, r, c), F32)
    return pl.pallas_call(
        body, name=name,
        grid_spec=pltpu.PrefetchScalarGridSpec(
            num_scalar_prefetch=1, grid=(r // tr if by_rows else c // tc,),
            in_specs=[pl.BlockSpec((N_CHIP, tr, tc), lambda i, chip_ref: (0, *at(i))),
                      pl.BlockSpec((None, tr, tc), lambda i, chip_ref: (chip_ref[0], *at(i))), blk, blk, blk]
            + [ANY] * n_after,
            out_specs=[blk] * 4),
        out_shape=[out] * 4,
        compiler_params=_params("parallel"),
    )(chip, parts, sums, w, m, v, *after)


def _adam_gains(total, ws, ms, vs):
    n = len(ws)
    widths = [w.shape[1] for w in ws]

    def body(t_ref, *refs):
        w_refs, m_refs, v_refs, outs = refs[:n], refs[n:2 * n], refs[2 * n:3 * n], refs[3 * n:]
        off = 0
        for i in range(n):
            g = t_ref[:, off:off + widths[i]]
            off += widths[i]
            g_ref, d_ref, mo_ref, vo_ref = outs[4 * i:4 * i + 4]
            g_ref[...] = g
            d_ref[...], mo_ref[...], vo_ref[...] = _adamw(w_refs[i][...], g, m_refs[i][...], v_refs[i][...])

    out = pl.pallas_call(
        body, name="adam_gains",
        out_shape=[jax.ShapeDtypeStruct(w.shape, F32) for w in ws for _ in range(4)],
    )(total, *ws, *ms, *vs)
    return [tuple(out[4 * i:4 * i + 4]) for i in range(n)]


def _adam_taps(total, first_col, device, w, m, v):
    _, n_taps, cw = w.shape
    col_block = lambda t, dev: (0, first_col // cw + t * N_DEV + dev[0])
    tap = pl.BlockSpec((None, 1, cw), lambda t, dev: (t, 0, 0))

    def body(dev_ref, t_ref, w_ref, m_ref, v_ref, g_ref, d_ref, mo_ref, vo_ref):
        g = t_ref[...]
        g_ref[...] = g
        d_ref[...], mo_ref[...], vo_ref[...] = _adamw(w_ref[...], g, m_ref[...], v_ref[...])

    shape3 = (n_taps, 1, cw)
    out = pl.pallas_call(
        body, name="adam_taps",
        grid_spec=pltpu.PrefetchScalarGridSpec(
            num_scalar_prefetch=1, grid=(n_taps,),
            in_specs=[pl.BlockSpec((1, cw), col_block), tap, tap, tap], out_specs=[tap] * 4),
        out_shape=[jax.ShapeDtypeStruct(shape3, F32)] * 4,
    )(device, total, w.reshape(shape3), m.reshape(shape3), v.reshape(shape3))
    return tuple(o.reshape(w.shape) for o in out)


def kernel(x, pre_mix_g, w_in, conv_w, q_norm_g, w_uq, kv_norm_g, w_ukv, conv_out_g, attn_out_g, w_o, post_mix_g, pre_mlp_g, w_up, w_down, post_mlp_g, loss_target, m_pre_mix_g, m_w_in, m_conv_w, m_q_norm_g, m_w_uq, m_kv_norm_g, m_w_ukv, m_conv_out_g, m_attn_out_g, m_w_o, m_post_mix_g, m_pre_mlp_g, m_w_up, m_w_down, m_post_mlp_g, v_pre_mix_g, v_w_in, v_conv_w, v_q_norm_g, v_w_uq, v_kv_norm_g, v_w_ukv, v_conv_out_g, v_attn_out_g, v_w_o, v_post_mix_g, v_pre_mlp_g, v_w_up, v_w_down, v_post_mlp_g):
    me = 4 * lax.axis_index("x") + 2 * lax.axis_index("y") + lax.axis_index("c")
    core = lax.axis_index("c").astype(jnp.int32).reshape(1)
    chip = (2 * lax.axis_index("x") + lax.axis_index("y")).astype(jnp.int32).reshape(1)
    gains = (pre_mix_g, q_norm_g, kv_norm_g, conv_out_g, attn_out_g, post_mix_g, pre_mlp_g, post_mlp_g)
    gain_m = (m_pre_mix_g, m_q_norm_g, m_kv_norm_g, m_conv_out_g, m_attn_out_g, m_post_mix_g, m_pre_mlp_g, m_post_mlp_g)
    gain_v = (v_pre_mix_g, v_q_norm_g, v_kv_norm_g, v_conv_out_g, v_attn_out_g, v_post_mix_g, v_pre_mlp_g, v_post_mlp_g)
    names = ("w_in", "w_uq", "w_ukv", "w_o", "w_up", "w_down")
    big = dict(zip(names, (w_in, w_uq, w_ukv, w_o, w_up, w_down)))
    big_m = dict(zip(names, (m_w_in, m_w_uq, m_w_ukv, m_w_o, m_w_up, m_w_down)))
    big_v = dict(zip(names, (v_w_in, v_w_uq, v_w_ukv, v_w_o, v_w_up, v_w_down)))
    n_heads = attn_out_g.shape[1] // HEAD
    n_taps = conv_w.shape[1]

    gathered = ("w_in", "conv", "w_uq", "w_ukv", "w_o", "w_up", "w_down")
    gather_groups = ((0, 1), (2, 3, 4), (5,), (6,))
    taps = jnp.pad(conv_w[0], ((0, SUBLANE - n_taps), (0, 0)))
    relayed_groups = (0, 2, 3)
    sems1, shards, lands, token = _gather_start("gather_start_first", [w_in[0].astype(BF16), taps], ((0, 1),), relayed=(0,))
    sems1, shards, lands = list(sems1), list(shards), list(lands)
    behind = token[0, 0]
    rest = [(big[nm][0] + behind).astype(BF16) for nm in gathered[2:]]

    def start_rest(after):
        sems_b, shards_b, lands_b, started = _gather_start("gather_start_rest", rest, ((0, 1, 2), (3,), (4,)), relayed=(1, 2),
                                                          after=after)
        sems1.extend(sems_b)
        shards.extend(shards_b)
        lands.extend(lands_b)
        return started

    cols = lambda a: jnp.concatenate([a[j] for j in range(N_DEV)], axis=1)
    rows = lambda a: a.reshape(N_DEV * a.shape[1], a.shape[2])
    ready = {
        "w_in": _join_col_shards,
        "conv": lambda a: cols(a)[:n_taps],
        "w_uq": lambda a: _permute_q_cols(cols(a), n_heads),
        "w_ukv": cols, "w_o": rows, "w_up": lambda a: a, "w_down": rows,
    }

    class Weights:
        def __init__(self):
            self.passed, self.relayed = {}, {}

        def forward(self, group, after):
            idx = gather_groups[group]
            self.passed[group] = _gather_forward(f"gather_forward_{group}", [shards[i] for i in idx], [lands[i] for i in idx],
                                                 *sems1[group], after, relayed=group in relayed_groups)
            if group == 0:
                self.rest_started = start_rest(tuple(self.passed[0][1]))
            return tuple(self.passed[group][1])

        def relay(self, group, after):
            sems2, mid = self.passed[group]
            if group == 0:
                after = (*after, self.rest_started)
            self.relayed[group], mid = _gather_relay_forward(f"gather_relay_{group}", mid, *sems2, after)
            self.passed[group] = (sems2, mid)
            return tuple(mid)

        def ready(self, group, after):
            sems2, mid = self.passed[group]
            full = _gather_wait(f"gather_wait_{group}", mid, *sems2, after, relay_sems=self.relayed.get(group))
            out = []
            for i, a in zip(gather_groups[group], full):
                a = lax.dynamic_update_index_in_dim(a, shards[i], me, 0)
                out.append(ready[gathered[i]](a))
            return out

    weights = Weights()

    col_blocks = lambda g: g.reshape(g.shape[0], N_DEV, g.shape[1] // N_DEV).transpose(1, 0, 2)
    row_blocks = lambda g: g.reshape(N_DEV, g.shape[0] // N_DEV, g.shape[1])
    grad_groups = (("w_down",), ("w_up",), ("w_o",), ("w_uq", "w_ukv"), ("w_in",))
    transposed = {"w_in": w_in.shape[2], "w_uq": w_uq.shape[2]}
    to_blocks = {
        "w_in": lambda g: g, "w_uq": lambda g: _unpermute_q_rows(g, n_heads),
        "w_ukv": col_blocks, "w_o": row_blocks, "w_up": lambda g: g, "w_down": row_blocks,
    }
    in_flight = []

    class Grads:
        def __init__(self):
            self.core = core
            self.away = {}

        def send_sums(self, group, sums):
            sems, sums, parts, tok = _chip_send_start(f"chip_send_start_{group}", list(sums))
            in_flight.append((sems, sums, parts))
            return (tok,)

        def full(self, group, arrays, received=None):
            nms = grad_groups[group]
            if received is None:
                blocks = [to_blocks[nm](g) for nm, g in zip(nms, arrays)]
                got = _pair_exchange(f"pair_exchange_{group}", blocks, [transposed.get(nm) for nm in nms])
            else:
                blocks, got = [self.away[group][1]], [self.received(group, received)]
            sums = [(_pair_sum_rows if nm in transposed else _pair_sum)(f"pair_sum_{nm}", g, r, core)
                    for nm, g, r in zip(nms, blocks, got)]
            return self.send_sums(group, sums)

        def send_away(self, group, half):
            nm = grad_groups[group][0]
            rows = transposed.get(nm)
            sems, src, land, tok = _pair_send_start(f"pair_send_start_{group}", half if rows is None else to_blocks[nm](half), rows)
            self.away[group] = (sems, src, land, rows)
            return (tok,)

        def received(self, group, after):
            sems, src, land, rows = self.away[group]
            return _pair_send_wait(f"pair_send_wait_{group}", sems, src, land, after, rows)

    grad_x, small = _local_step(x[0], loss_target[0], gains, weights, Grads(), first_after=(token,))

    big_out = {}

    def update(tag, first, last, after):
        groups = in_flight[first:last]
        parts = _chip_send_wait("chip_send_wait_" + tag, groups, after)
        nms = [nm for grp in grad_groups[first:last] for nm in grp]
        sums = [a for _, s, _ in groups for a in s]
        for nm, p, s in zip(nms, parts, sums):
            view = (lambda a: jnp.swapaxes(a, 1, 2)) if nm in transposed else (lambda a: a)
            out = _sum_adam("adam_" + nm, p, s, chip, view(big[nm]), view(big_m[nm]), view(big_v[nm]), after=after)
            after = (out[0],)
            big_out[nm] = [view(o) for o in out]
        return after

    after = update("early", 0, len(in_flight) - 1, (grad_x,))
    total = _small_all_reduce(small, after=after)
    update("late", len(in_flight) - 1, len(in_flight), (total,))
    big_out = [big_out[nm] for nm in names]

    gain_out = _adam_gains(total, gains, gain_m, gain_v)
    taps_out = _adam_taps(total, sum(g.shape[1] for g in gains), me.astype(jnp.int32).reshape(1), conv_w, m_conv_w, v_conv_w)
    loss = total[0, total.shape[1] - 1]

    order = (0, "w_in", "conv", 1, "w_uq", 2, "w_ukv", 3, 4, "w_o", 5, 6, "w_up", "w_down", 7)
    by_name = dict(zip(names, big_out))
    outs = [loss, grad_x[None]]
    for kind in range(4):
        for item in order:
            if item == "conv":
                outs.append(taps_out[kind])
            elif isinstance(item, int):
                outs.append(gain_out[item][kind])
            else:
                outs.append(by_name[item][kind])
    return tuple(outs)
```

```python
import math

import jax
import jax.numpy as jnp
from jax import lax
from jax.experimental import pallas as pl
from jax.experimental.pallas import tpu as pltpu

F32 = jnp.float32
BF16 = jnp.bfloat16

EPS = 1e-6
NEG_INF = -1e30
HEAD = 128
ROPE = 64
QK = HEAD + ROPE
CHUNK = 64
ROPE_THETA = 10000.0
ADAM_LR, ADAM_B1, ADAM_B2, ADAM_EPS, ADAM_WD, ADAM_STEP = 0.001, 0.9, 0.999, 1e-08, 0.01, 10

LANE = 128
SUBLANE = 8
VMEM_LIMIT_BYTES = 56 * 1024 * 1024

N_DEV = 8
N_CHIP = 4
MESH = pl.DeviceIdType.MESH


def _params(*sem):
    return pltpu.CompilerParams(dimension_semantics=sem, vmem_limit_bytes=VMEM_LIMIT_BYTES)


ANY = pl.BlockSpec(memory_space=pl.ANY)


def _call(body, *, in_specs, after=(), **kw):
    n_in, n_after = len(in_specs), len(after)

    def ordered(*refs):
        body(*refs[:n_in], *refs[n_in + n_after:])

    call = pl.pallas_call(ordered, in_specs=[*in_specs, *[ANY] * n_after], **kw)
    return lambda *operands: call(*operands, *after)


def _sublane_sum(v):
    r, w = v.shape
    return jnp.sum(v.reshape(r // SUBLANE, SUBLANE, w), axis=0)


def _rstd(x):
    return lax.rsqrt(jnp.mean(x * x, axis=-1, keepdims=True) + EPS)


def _rms_bwd(x, g, dy):
    r = _rstd(x)
    xh = x * r
    dxh = dy * g
    dx = r * (dxh - xh * jnp.mean(dxh * xh, axis=-1, keepdims=True))
    return dx, dy * xh


def _accumulate(ref, val, step):
    @pl.when(step == 0)
    def _():
        ref[...] = val

    @pl.when(step > 0)
    def _():
        ref[...] += val


NN = ((1,), (0,))
NT = ((1,), (1,))
TN = ((0,), (0,))


def _matmul(name, a, b, *, grid, a_spec, b_spec, out_shape, out_specs, contract, nk=1, acc_shape=None,
            extras=(), extra_specs=(), epilogue=None, after=()):
    multi = isinstance(out_shape, (tuple, list))
    out_shapes = tuple(out_shape) if multi else (out_shape,)
    n_out = len(out_shapes)
    n_extra = len(extras)

    def body(a_ref, b_ref, *rest):
        x_refs = rest[:n_extra]
        o_refs = rest[n_extra:n_extra + n_out]

        def emit(acc):
            vals = epilogue(acc, *[r[...] for r in x_refs]) if epilogue else (acc,)
            for r, v in zip(o_refs, vals):
                r[...] = v.astype(r.dtype)

        p = lax.dot_general(a_ref[...], b_ref[...], (contract, ((), ())), preferred_element_type=F32)
        if nk == 1:
            emit(p)
        else:
            acc_ref = rest[n_extra + n_out]
            k = pl.program_id(2)
            _accumulate(acc_ref, p, k)

            @pl.when(k == nk - 1)
            def _():
                emit(acc_ref[...])

    sem = ("parallel", "parallel") + (("arbitrary",) if nk > 1 else ())
    return _call(
        body, name=name, grid=grid, after=after,
        in_specs=[a_spec, b_spec, *extra_specs],
        out_specs=out_specs,
        out_shape=out_shape,
        scratch_shapes=[pltpu.VMEM(acc_shape, F32)] if nk > 1 else [],
        compiler_params=_params(*sem),
    )(a, b, *extras)


def _fit(n, tile):
    if n <= tile:
        return n
    t = tile - tile % LANE
    while n % t:
        t -= LANE
    return t


def _mm_nn(name, a, b, out_dtype, tm, tn, after=()):
    m, k = a.shape
    n = b.shape[1]
    tm, tn = _fit(m, tm), _fit(n, tn)
    return _matmul(name, a, b, grid=(m // tm, n // tn), after=after,
                   a_spec=pl.BlockSpec((tm, k), lambda i, j: (i, 0)),
                   b_spec=pl.BlockSpec((k, tn), lambda i, j: (0, j)),
                   out_shape=jax.ShapeDtypeStruct((m, n), out_dtype),
                   out_specs=pl.BlockSpec((tm, tn), lambda i, j: (i, j)), contract=NN)


def _mm_nt(name, a, b, out_dtype, tm, tn, after=()):
    m, k = a.shape
    n = b.shape[0]
    tm, tn = _fit(m, tm), _fit(n, tn)
    return _matmul(name, a, b, grid=(m // tm, n // tn), after=after,
                   a_spec=pl.BlockSpec((tm, k), lambda i, j: (i, 0)),
                   b_spec=pl.BlockSpec((tn, k), lambda i, j: (j, 0)),
                   out_shape=jax.ShapeDtypeStruct((m, n), out_dtype),
                   out_specs=pl.BlockSpec((tm, tn), lambda i, j: (i, j)), contract=NT)


def _mm_tn(name, a, b, out_dtype, tm, tn):
    s, m = a.shape
    n = b.shape[1]
    tm, tn = _fit(m, tm), _fit(n, tn)
    return _matmul(name, a, b, grid=(m // tm, n // tn),
                   a_spec=pl.BlockSpec((s, tm), lambda i, j: (0, i)),
                   b_spec=pl.BlockSpec((s, tn), lambda i, j: (0, j)),
                   out_shape=jax.ShapeDtypeStruct((m, n), out_dtype),
                   out_specs=pl.BlockSpec((tm, tn), lambda i, j: (i, j)), contract=TN)


ROWS = 256


def _row_spec(rows, width):
    return pl.BlockSpec((rows, width), lambda i: (i, 0))


def _fixed_spec(rows, width):
    return pl.BlockSpec((rows, width), lambda i: (0, 0))


def _column_pieces(rows, start, width):
    piece = math.gcd(start, width)
    assert piece % LANE == 0
    return [pl.BlockSpec((rows, piece), lambda i, b=start // piece + p: (i, b)) for p in range(width // piece)]


def _rms_fwd(name, x, g, cols=None, after=()):
    s = x.shape[0]
    start, w = cols or (0, x.shape[1])
    rows = min(ROWS, s)
    pieces = _column_pieces(rows, start, w) if cols else [_row_spec(rows, w)]
    n = len(pieces)

    def body(*refs):
        g_ref, o_ref = refs[n:]
        xv = refs[0][...] if n == 1 else jnp.concatenate([r[...] for r in refs[:n]], axis=1)
        o_ref[...] = (xv * _rstd(xv) * g_ref[...]).astype(o_ref.dtype)

    return _call(
        body, name=name, grid=(s // rows,), after=after,
        in_specs=[*pieces, _fixed_spec(1, w)],
        out_specs=_row_spec(rows, w),
        out_shape=jax.ShapeDtypeStruct((s, w), BF16),
        compiler_params=_params("parallel"),
    )(*[x] * n, g)


def _rms_bwd_call(name, x, g, dy, out_dtype, cols=None, after=()):
    s = x.shape[0]
    start, w = cols or (0, x.shape[1])
    rows = min(ROWS, s)
    pieces = _column_pieces(rows, start, w) if cols else [_row_spec(rows, w)]
    n = len(pieces)

    def body(*refs):
        g_ref, dy_ref, dx_ref, dg_ref = refs[n:]
        xv = refs[0][...] if n == 1 else jnp.concatenate([r[...] for r in refs[:n]], axis=1)
        dx, dgc = _rms_bwd(xv, g_ref[...], dy_ref[...].astype(F32))
        dx_ref[...] = dx.astype(dx_ref.dtype)
        _accumulate(dg_ref, _sublane_sum(dgc), pl.program_id(0))

    return _call(
        body, name=name, grid=(s // rows,), after=after,
        in_specs=[*pieces, _fixed_spec(1, w), _row_spec(rows, w)],
        out_specs=[_row_spec(rows, w), _fixed_spec(SUBLANE, w)],
        out_shape=[jax.ShapeDtypeStruct((s, w), out_dtype), jax.ShapeDtypeStruct((SUBLANE, w), F32)],
        compiler_params=_params("arbitrary"),
    )(*[x] * n, g, dy)


def _mid_fwd(x, y, g_post, g_pre, after=()):
    s, w = x.shape
    rows = min(ROWS, s)

    def body(x_ref, y_ref, gp_ref, gq_ref, x2_ref, h2_ref):
        yv = y_ref[...]
        x2 = x_ref[...] + yv * _rstd(yv) * gp_ref[...]
        x2_ref[...] = x2
        h2_ref[...] = (x2 * _rstd(x2) * gq_ref[...]).astype(h2_ref.dtype)

    return _call(
        body, name="mid_fwd", grid=(s // rows,), after=after,
        in_specs=[_row_spec(rows, w), _row_spec(rows, w), _fixed_spec(1, w), _fixed_spec(1, w)],
        out_specs=[_row_spec(rows, w), _row_spec(rows, w)],
        out_shape=[jax.ShapeDtypeStruct((s, w), F32), jax.ShapeDtypeStruct((s, w), BF16)],
        compiler_params=_params("parallel"),
    )(x, y, g_post, g_pre)


def _head(m, x2, tgt, g):
    s, w = m.shape
    rows = min(ROWS, s)

    def body(m_ref, x2_ref, t_ref, g_ref, dout_ref, dm_ref, dg_ref, loss_ref):
        mv = m_ref[...]
        gv = g_ref[...]
        out = x2_ref[...] + mv * _rstd(mv) * gv
        err = out - t_ref[...]
        dout = err * (1.0 / w)
        dout_ref[...] = dout
        dm, dgc = _rms_bwd(mv, gv, dout)
        dm_ref[...] = dm.astype(dm_ref.dtype)
        sq = err * err
        lanes = sq[:, 0:LANE]
        for j in range(1, w // LANE):
            lanes = lanes + sq[:, j * LANE:(j + 1) * LANE]
        step = pl.program_id(0)
        _accumulate(dg_ref, _sublane_sum(dgc), step)
        _accumulate(loss_ref, _sublane_sum(lanes) * (0.5 / w), step)

    return pl.pallas_call(
        body, name="head", grid=(s // rows,),
        in_specs=[_row_spec(rows, w), _row_spec(rows, w), _row_spec(rows, w), _fixed_spec(1, w)],
        out_specs=[_row_spec(rows, w), _row_spec(rows, w), _fixed_spec(SUBLANE, w), _fixed_spec(SUBLANE, LANE)],
        out_shape=[jax.ShapeDtypeStruct((s, w), F32), jax.ShapeDtypeStruct((s, w), BF16),
                   jax.ShapeDtypeStruct((SUBLANE, w), F32), jax.ShapeDtypeStruct((SUBLANE, LANE), F32)],
        compiler_params=_params("arbitrary"),
    )(m, x2, tgt, g)


def _mid_bwd(x2, y, d_out, d_h2, g_pre, g_post, after=()):
    s, w = x2.shape
    rows = min(ROWS, s)

    def body(x2_ref, y_ref, dout_ref, dh2_ref, gq_ref, gp_ref, dx2_ref, dy_ref, dgq_ref, dgp_ref):
        dx, dgq = _rms_bwd(x2_ref[...], gq_ref[...], dh2_ref[...])
        dx2 = dout_ref[...] + dx
        dx2_ref[...] = dx2
        dy, dgp = _rms_bwd(y_ref[...], gp_ref[...], dx2)
        dy_ref[...] = dy.astype(dy_ref.dtype)
        step = pl.program_id(0)
        _accumulate(dgq_ref, _sublane_sum(dgq), step)
        _accumulate(dgp_ref, _sublane_sum(dgp), step)

    return _call(
        body, name="mid_bwd", grid=(s // rows,), after=after,
        in_specs=[_row_spec(rows, w)] * 4 + [_fixed_spec(1, w)] * 2,
        out_specs=[_row_spec(rows, w), _row_spec(rows, w), _fixed_spec(SUBLANE, w), _fixed_spec(SUBLANE, w)],
        out_shape=[jax.ShapeDtypeStruct((s, w), F32), jax.ShapeDtypeStruct((s, w), BF16),
                   jax.ShapeDtypeStruct((SUBLANE, w), F32), jax.ShapeDtypeStruct((SUBLANE, w), F32)],
        compiler_params=_params("arbitrary"),
    )(x2, y, d_out, d_h2, g_pre, g_post)


def _first_bwd(x, g, d_h1, d_x2, after=()):
    s, w = x.shape
    rows = min(ROWS, s)

    def body(x_ref, g_ref, dh_ref, dx2_ref, dx_ref, dg_ref):
        dx, dgc = _rms_bwd(x_ref[...], g_ref[...], dh_ref[...])
        dx_ref[...] = dx2_ref[...] + dx
        _accumulate(dg_ref, _sublane_sum(dgc), pl.program_id(0))

    return _call(
        body, name="first_bwd", grid=(s // rows,), after=after,
        in_specs=[_row_spec(rows, w), _fixed_spec(1, w), _row_spec(rows, w), _row_spec(rows, w)],
        out_specs=[_row_spec(rows, w), _fixed_spec(SUBLANE, w)],
        out_shape=[jax.ShapeDtypeStruct((s, w), F32), jax.ShapeDtypeStruct((SUBLANE, w), F32)],
        compiler_params=_params("arbitrary"),
    )(x, g, d_h1, d_x2)


def _shift_down(v, k):
    t = lax.broadcasted_iota(jnp.int32, v.shape, 0)
    return jnp.where(t >= k, pltpu.roll(v, k, 0), 0.0)


def _shift_up(v, k):
    n = v.shape[0]
    t = lax.broadcasted_iota(jnp.int32, v.shape, 0)
    return jnp.where(t < n - k, pltpu.roll(v, n - k, 0), 0.0)


def _conv_core(u, b, c, w):
    z = c * u
    conv = w[0:1, :] * _shift_down(z, 2) + w[1:2, :] * _shift_down(z, 1) + w[2:3, :] * z
    return z, conv, b * conv


def _conv_fwd(proj, conv_w, g, n_groups, out_width):
    s = proj.shape[0]

    def body(u_ref, b_ref, c_ref, w_ref, g_ref, o_ref):
        _, _, yr = _conv_core(u_ref[...], b_ref[...], c_ref[...], w_ref[...])
        o_ref[...] = (yr * _rstd(yr) * g_ref[...]).astype(o_ref.dtype)

    col = lambda k: pl.BlockSpec((s, HEAD), lambda i: (0, k * n_groups + i))
    return pl.pallas_call(
        body, name="conv_fwd", grid=(n_groups,),
        in_specs=[col(0), col(1), col(2), pl.BlockSpec((3, HEAD), lambda i: (0, i)), pl.BlockSpec((1, HEAD), lambda i: (0, i))],
        out_specs=pl.BlockSpec((s, HEAD), lambda i: (0, i)),
        out_shape=jax.ShapeDtypeStruct((s, out_width), BF16),
        compiler_params=_params("parallel"),
    )(proj, proj, proj, conv_w, g)


def _conv_bwd(proj, d_mix, conv_w, g, n_groups):
    s = proj.shape[0]
    width = n_groups * HEAD

    def body(u_ref, b_ref, c_ref, dy_ref, w_ref, g_ref, du_ref, db_ref, dc_ref, dg_ref, dw_ref):
        u, b, c, w = u_ref[...], b_ref[...], c_ref[...], w_ref[...]
        z, conv, yr = _conv_core(u, b, c, w)
        dyr, dgc = _rms_bwd(yr, g_ref[...], dy_ref[...])
        dconv = dyr * b
        db_ref[...] = (dyr * conv).astype(db_ref.dtype)
        dz = w[2:3, :] * dconv + w[1:2, :] * _shift_up(dconv, 1) + w[0:1, :] * _shift_up(dconv, 2)
        dc_ref[...] = (dz * u).astype(dc_ref.dtype)
        du_ref[...] = (dz * c).astype(du_ref.dtype)
        dg_ref[...] = _sublane_sum(dgc)
        dw_ref[0] = _sublane_sum(dconv * _shift_down(z, 2))
        dw_ref[1] = _sublane_sum(dconv * _shift_down(z, 1))
        dw_ref[2] = _sublane_sum(dconv * z)

    col = lambda k: pl.BlockSpec((s, HEAD), lambda i: (0, k * n_groups + i))
    grp = pl.BlockSpec((s, HEAD), lambda i: (0, i))
    return pl.pallas_call(
        body, name="conv_bwd", grid=(n_groups,),
        in_specs=[col(0), col(1), col(2), grp, pl.BlockSpec((3, HEAD), lambda i: (0, i)), pl.BlockSpec((1, HEAD), lambda i: (0, i))],
        out_specs=[grp, grp, grp, pl.BlockSpec((SUBLANE, HEAD), lambda i: (0, i)),
                   pl.BlockSpec((3, SUBLANE, HEAD), lambda i: (0, 0, i))],
        out_shape=[jax.ShapeDtypeStruct((s, width), BF16)] * 3
        + [jax.ShapeDtypeStruct((SUBLANE, width), F32), jax.ShapeDtypeStruct((3, SUBLANE, width), F32)],
        compiler_params=_params("parallel"),
    )(proj, proj, proj, d_mix, conv_w, g)


def _rope_tables(s, n_heads):
    pos = jnp.arange(s, dtype=F32)
    inv_freq = jnp.power(ROPE_THETA, -jnp.arange(0, ROPE, 2, dtype=F32) / ROPE)
    ang = pos[:, None] * inv_freq[None, :]
    cos, sin = jnp.cos(ang), jnp.sin(ang)
    cs = jnp.concatenate([cos, cos], axis=1)
    sn = jnp.concatenate([-sin, sin], axis=1)
    pad = jnp.zeros((s, LANE - ROPE), F32)
    return (jnp.tile(cs, (1, n_heads)), jnp.tile(sn, (1, n_heads)),
            jnp.concatenate([cs, pad], axis=1), jnp.concatenate([sn, pad], axis=1))


def _swap_halves(v):
    w = v.shape[1]
    lane = lax.broadcasted_iota(jnp.int32, v.shape, 1)
    first = (lane % ROPE) < (ROPE // 2)
    return jnp.where(first, pltpu.roll(v, w - ROPE // 2, 1), pltpu.roll(v, ROPE // 2, 1))


def _pack_heads(q, kv, proj, kr_col, tables, n_heads, after=()):
    s = q.shape[0]
    rows = min(ROWS, s)
    cq, sq, ck, sk = tables
    wq = n_heads * ROPE

    def body(q_ref, kv_ref, kr_ref, cq_ref, sq_ref, ck_ref, sk_ref, qo_ref, ko_ref, vo_ref):
        qr = q_ref[:, n_heads * HEAD:]
        qr = qr * cq_ref[...] + _swap_halves(qr) * sq_ref[...]
        krv = kr_ref[...]
        krv = krv * ck_ref[...] + _swap_halves(krv) * sk_ref[...]
        for h in range(n_heads):
            qo_ref[h] = jnp.concatenate([q_ref[:, h * HEAD:(h + 1) * HEAD], qr[:, h * ROPE:(h + 1) * ROPE]], axis=1).astype(BF16)
            ko_ref[h] = jnp.concatenate([kv_ref[:, 2 * h * HEAD:(2 * h + 1) * HEAD], krv[:, :ROPE]], axis=1).astype(BF16)
            vo_ref[h] = kv_ref[:, (2 * h + 1) * HEAD:(2 * h + 2) * HEAD].astype(BF16)

    hs = lambda w: pl.BlockSpec((n_heads, rows, w), lambda i: (0, i, 0))
    return _call(
        body, name="pack_heads", grid=(s // rows,), after=after,
        in_specs=[_row_spec(rows, q.shape[1]), _row_spec(rows, kv.shape[1]), pl.BlockSpec((rows, LANE), lambda i: (i, kr_col // LANE)),
                  _row_spec(rows, wq), _row_spec(rows, wq), _row_spec(rows, LANE), _row_spec(rows, LANE)],
        out_specs=[hs(QK), hs(QK), hs(HEAD)],
        out_shape=[jax.ShapeDtypeStruct((n_heads, s, QK), BF16), jax.ShapeDtypeStruct((n_heads, s, QK), BF16),
                   jax.ShapeDtypeStruct((n_heads, s, HEAD), BF16)],
        compiler_params=_params("parallel"),
    )(q, kv, proj, cq, sq, ck, sk)


def _unpack_heads(dq, dk, dv, tables, n_heads):
    s = dq.shape[1]
    rows = min(ROWS, s)
    cq, sq, ck, sk = tables
    wq = n_heads * ROPE

    def body(dq_ref, dk_ref, dv_ref, cq_ref, sq_ref, ck_ref, sk_ref, qo_ref, kvo_ref, kro_ref):
        dqr = jnp.concatenate([dq_ref[h][:, HEAD:] for h in range(n_heads)], axis=1)
        dqr = dqr * cq_ref[...] - _swap_halves(dqr) * sq_ref[...]
        dkr = dk_ref[0][:, HEAD:]
        for h in range(1, n_heads):
            dkr = dkr + dk_ref[h][:, HEAD:]
        dkr = jnp.concatenate([dkr, jnp.zeros((rows, LANE - ROPE), F32)], axis=1)
        dkr = dkr * ck_ref[...] - _swap_halves(dkr) * sk_ref[...]
        kro_ref[...] = dkr.astype(kro_ref.dtype)
        qo_ref[:, n_heads * HEAD:] = dqr.astype(qo_ref.dtype)
        for h in range(n_heads):
            qo_ref[:, h * HEAD:(h + 1) * HEAD] = dq_ref[h][:, :HEAD].astype(qo_ref.dtype)
            kvo_ref[:, 2 * h * HEAD:(2 * h + 1) * HEAD] = dk_ref[h][:, :HEAD].astype(kvo_ref.dtype)
            kvo_ref[:, (2 * h + 1) * HEAD:(2 * h + 2) * HEAD] = dv_ref[h].astype(kvo_ref.dtype)

    hs = lambda w: pl.BlockSpec((n_heads, rows, w), lambda i: (0, i, 0))
    return pl.pallas_call(
        body, name="unpack_heads", grid=(s // rows,),
        in_specs=[hs(QK), hs(QK), hs(HEAD), _row_spec(rows, wq), _row_spec(rows, wq), _row_spec(rows, LANE), _row_spec(rows, LANE)],
        out_specs=[_row_spec(rows, n_heads * QK), _row_spec(rows, 2 * n_heads * HEAD), _row_spec(rows, LANE)],
        out_shape=[jax.ShapeDtypeStruct((s, n_heads * QK), BF16), jax.ShapeDtypeStruct((s, 2 * n_heads * HEAD), BF16),
                   jax.ShapeDtypeStruct((s, LANE), BF16)],
        compiler_params=_params("parallel"),
    )(dq, dk, dv, cq, sq, ck, sk)


TQ = 256


LOG2_E = 1.4426950408889634


def _softmax_parts(q, k):
    tq, n_keys = q.shape[0], k.shape[0]
    sc = lax.dot_general(q, k, (NT, ((), ())), preferred_element_type=F32) * (QK ** -0.5 * LOG2_E)
    row = lax.broadcasted_iota(jnp.int32, (tq, tq), 0)
    col = lax.broadcasted_iota(jnp.int32, (tq, tq), 1)
    own = jnp.where(col // CHUNK <= row // CHUNK, sc[:, n_keys - tq:], NEG_INF)
    sc = own if n_keys == tq else jnp.concatenate([sc[:, :n_keys - tq], own], axis=1)
    e = jnp.exp2(sc - jnp.max(sc, axis=-1, keepdims=True))
    return e, 1.0 / jnp.sum(e, axis=-1, keepdims=True)


def _attn_fwd(q, k, v, g, mix, col0):
    n_heads, s, _ = q.shape
    tq = min(TQ, s)
    assert tq % CHUNK == 0 and s % tq == 0

    def body(q_ref, k_ref, v_ref, g_ref, mix_ref, o_ref, y_ref):
        for c in range(s // tq):
            rows, n_keys = pl.ds(c * tq, tq), (c + 1) * tq
            e, inv = _softmax_parts(q_ref[rows, :], k_ref[0:n_keys, :])
            o = jnp.dot(e.astype(BF16), v_ref[0:n_keys, :], preferred_element_type=F32) * inv
            o_ref[rows, :] = o
            y_ref[rows, :] = (o * _rstd(o) * g_ref[...]).astype(y_ref.dtype)

    head = lambda w: pl.BlockSpec((None, s, w), lambda h: (h, 0, 0))
    return pl.pallas_call(
        body, name="attn_fwd", grid=(n_heads,),
        in_specs=[head(QK), head(QK), head(HEAD), pl.BlockSpec((1, HEAD), lambda h: (0, h)), ANY],
        out_specs=[head(HEAD), pl.BlockSpec((s, HEAD), lambda h: (0, col0 // HEAD + h))],
        out_shape=[jax.ShapeDtypeStruct((n_heads, s, HEAD), F32), jax.ShapeDtypeStruct(mix.shape, mix.dtype)],
        input_output_aliases={4: 1},
        compiler_params=_params("parallel"),
    )(q, k, v, g, mix)


def _attn_bwd(q, k, v, o, d_mix, g, col0, after=()):
    n_heads, s, _ = q.shape
    tq = min(TQ, s)

    def body(q_ref, k_ref, v_ref, o_ref, dy_ref, g_ref, dq_ref, dk_ref, dv_ref, dg_ref):
        dg = None
        for c in reversed(range(s // tq)):
            rows, n_keys = pl.ds(c * tq, tq), (c + 1) * tq
            qv, kv_, vv = q_ref[rows, :], k_ref[0:n_keys, :], v_ref[0:n_keys, :]
            do, dgc = _rms_bwd(o_ref[rows, :], g_ref[...], dy_ref[rows, :])
            do = do.astype(BF16)
            dg = _sublane_sum(dgc) if dg is None else dg + _sublane_sum(dgc)
            e, inv = _softmax_parts(qv, kv_)
            p = e * inv
            dp = lax.dot_general(do, vv, (NT, ((), ())), preferred_element_type=F32)
            ds = (p * (dp - jnp.sum(p * dp, axis=-1, keepdims=True)) * (QK ** -0.5)).astype(BF16)
            dq_ref[rows, :] = jnp.dot(ds, kv_, preferred_element_type=F32)
            dk = lax.dot_general(ds, qv, (TN, ((), ())), preferred_element_type=F32)
            dv = lax.dot_general(p.astype(BF16), do, (TN, ((), ())), preferred_element_type=F32)
            if n_keys == s:
                dk_ref[...] = dk
                dv_ref[...] = dv
            else:
                dk_ref[0:n_keys, :] += dk
                dv_ref[0:n_keys, :] += dv
        dg_ref[...] = dg

    c0 = col0 // HEAD
    head = lambda w: pl.BlockSpec((None, s, w), lambda h: (h, 0, 0))
    return _call(
        body, name="attn_bwd", grid=(n_heads,), after=after,
        in_specs=[head(QK), head(QK), head(HEAD), head(HEAD), pl.BlockSpec((s, HEAD), lambda h: (0, c0 + h)),
                  pl.BlockSpec((1, HEAD), lambda h: (0, h))],
        out_specs=[head(QK), head(QK), head(HEAD), pl.BlockSpec((SUBLANE, HEAD), lambda h: (0, h))],
        out_shape=[jax.ShapeDtypeStruct((n_heads, s, QK), F32), jax.ShapeDtypeStruct((n_heads, s, QK), F32),
                   jax.ShapeDtypeStruct((n_heads, s, HEAD), F32), jax.ShapeDtypeStruct((SUBLANE, n_heads * HEAD), F32)],
        compiler_params=_params("parallel"),
    )(q, k, v, o, d_mix, g)


TILE_M = 1024
TILE_N = 1024


def _up_fwd(h2, w_up):
    s, d = h2.shape
    nb, _, fb = w_up.shape
    tm = min(TILE_M,s)

    def epilogue(acc):
        r = jnp.maximum(acc, 0.0)
        return r * r, r

    blk = pl.BlockSpec((tm, fb), lambda i, j: (i, j))
    return _matmul("up_fwd", h2, w_up, grid=(s // tm, nb),
                   a_spec=pl.BlockSpec((tm, d), lambda i, j: (i, 0)),
                   b_spec=pl.BlockSpec((None, d, fb), lambda i, j: (j, 0, 0)),
                   out_shape=[jax.ShapeDtypeStruct((s, nb * fb), BF16)] * 2, out_specs=[blk, blk],
                   contract=NN, epilogue=epilogue)


def _down_fwd(a, w_down):
    s, f = a.shape
    d = w_down.shape[1]
    tm, tn, tk = min(TILE_M,s), min(TILE_N,d), 2048
    nk = f // tk
    return _matmul("down_fwd", a, w_down, grid=(s // tm, d // tn, nk),
                   a_spec=pl.BlockSpec((tm, tk), lambda i, j, k: (i, k)),
                   b_spec=pl.BlockSpec((tk, tn), lambda i, j, k: (k, j)),
                   out_shape=jax.ShapeDtypeStruct((s, d), F32),
                   out_specs=pl.BlockSpec((tm, tn), lambda i, j, k: (i, j)),
                   contract=NN, nk=nk, acc_shape=(tm, tn))


def _down_bwd_act(d_m, w_down, r, after=()):
    s, d = d_m.shape
    f = w_down.shape[0]
    tm, tn = min(TILE_M,s), min(TILE_N,f)
    blk = pl.BlockSpec((tm, tn), lambda i, j: (i, j))
    return _matmul("down_bwd_act", d_m, w_down, grid=(s // tm, f // tn), after=after,
                   a_spec=pl.BlockSpec((tm, d), lambda i, j: (i, 0)),
                   b_spec=pl.BlockSpec((tn, d), lambda i, j: (j, 0)),
                   out_shape=jax.ShapeDtypeStruct((s, f), BF16), out_specs=blk, contract=NT,
                   extras=(r,), extra_specs=(blk,),
                   epilogue=lambda acc, rv: (acc * (2.0 * rv.astype(F32)),))


def _up_bwd_act(d_up, w_up, after=()):
    s, _ = d_up.shape
    nb, d, fb = w_up.shape
    tm, tn = min(2 * TILE_M, s), min(TILE_N,d)
    return _matmul("up_bwd_act", d_up, w_up, grid=(s // tm, d // tn, nb), after=after,
                   a_spec=pl.BlockSpec((tm, fb), lambda i, j, k: (i, k)),
                   b_spec=pl.BlockSpec((None, tn, fb), lambda i, j, k: (k, j, 0)),
                   out_shape=jax.ShapeDtypeStruct((s, d), F32),
                   out_specs=pl.BlockSpec((tm, tn), lambda i, j, k: (i, j)),
                   contract=NT, nk=nb, acc_shape=(tm, tn))


def _half_grad(name, a, b, core, home, received, after, *, grid, a_block, a_map, b_block, b_map, o_block, o_map, out_shape):
    n_after = len(after)
    pick = (lambda ref: ref[0]) if home else (lambda ref: 1 - ref[0])

    def body(core_ref, a_ref, b_ref, *rest):
        acc = lax.dot_general(a_ref[...], b_ref[...], (TN, ((), ())), preferred_element_type=F32)
        if received is not None:
            acc = acc + rest[0][...].astype(F32)
        rest[-1][...] = acc.astype(rest[-1].dtype)

    wrap = lambda fn: (lambda i, j, core_ref: fn(i, j, pick(core_ref)))
    o_spec = pl.BlockSpec(o_block, wrap(o_map))
    extra = [] if received is None else [o_spec]
    operands = [] if received is None else [received]
    return pl.pallas_call(
        body, name=name,
        grid_spec=pltpu.PrefetchScalarGridSpec(
            num_scalar_prefetch=1, grid=grid,
            in_specs=[pl.BlockSpec(a_block, wrap(a_map)), pl.BlockSpec(b_block, wrap(b_map))] + extra + [ANY] * n_after,
            out_specs=o_spec),
        out_shape=out_shape,
        compiler_params=_params("parallel", "parallel"),
    )(core, a, b, *operands, *after)


def _down_half_grad(name, a, d_m, core, home, received=None, after=()):
    s, f = a.shape
    d = d_m.shape[1]
    r = f // N_DEV
    tn = min(TILE_N, d)
    return _half_grad(name, a, d_m, core, home, received, after, grid=(N_CHIP, d // tn),
                      a_block=(s, r), a_map=lambda k, j, p: (0, 2 * k + p),
                      b_block=(s, tn), b_map=lambda k, j, p: (0, j),
                      o_block=(None, r, tn), o_map=lambda k, j, p: (k, 0, j),
                      out_shape=jax.ShapeDtypeStruct((N_CHIP, r, d), BF16))


def _up_half_grad(name, h2, d_up, core, home, received=None, after=()):
    s, d = h2.shape
    fb = d_up.shape[1] // N_DEV
    tm = min(TILE_M, d)
    return _half_grad(name, h2, d_up, core, home, received, after, grid=(d // tm, N_CHIP),
                      a_block=(s, tm), a_map=lambda i, k, p: (0, i),
                      b_block=(s, fb), b_map=lambda i, k, p: (0, 2 * k + p),
                      o_block=(None, tm, fb), o_map=lambda i, k, p: (k, i, 0),
                      out_shape=jax.ShapeDtypeStruct((N_CHIP, d, fb), BF16))


def _in_pad(in_width):
    return -(-in_width // LANE) * LANE


def _join_col_shards(blocks):
    n, r, w = blocks.shape
    rows = min(ROWS, r)
    width = _in_pad(n * w)

    def body(x_ref, o_ref):
        tail = [jnp.zeros((rows, width - n * w), o_ref.dtype)] if width > n * w else []
        o_ref[...] = jnp.concatenate([x_ref[j] for j in range(n)] + tail, axis=1)

    return pl.pallas_call(
        body, name="join_col_shards", grid=(r // rows,),
        in_specs=[pl.BlockSpec((n, rows, w), lambda i: (0, i, 0))], out_specs=_row_spec(rows, width),
        out_shape=jax.ShapeDtypeStruct((r, width), blocks.dtype),
        compiler_params=_params("parallel"),
    )(blocks)


def _permute_q_cols(w_uq, n_heads):
    r = w_uq.shape[0]
    w3 = w_uq.reshape(r, n_heads, QK)
    return jnp.concatenate([w3[:, :, :HEAD].reshape(r, n_heads * HEAD), w3[:, :, HEAD:].reshape(r, n_heads * ROPE)], axis=1)


def _unpermute_q_rows(wt, n_heads):
    r = wt.shape[1]
    nope = wt[:n_heads * HEAD].reshape(n_heads, HEAD, r)
    rope = wt[n_heads * HEAD:].reshape(n_heads, ROPE, r)
    return jnp.concatenate([nope, rope], axis=1).reshape(n_heads * QK, r)


def _local_step(x, tgt, gains, weights, grads, first_after=()):
    pre_mix_g, q_norm_g, kv_norm_g, conv_out_g, attn_out_g, post_mix_g, pre_mlp_g, post_mlp_g = gains
    s, d = x.shape
    conv_width = conv_out_g.shape[1]
    n_groups = conv_width // HEAD
    r_q, r_kv = q_norm_g.shape[1], kv_norm_g.shape[1]
    n_heads = attn_out_g.shape[1] // HEAD
    c_q0 = 3 * conv_width
    c_kv0 = c_q0 + r_q
    c_kr0 = c_kv0 + r_kv
    in_pad = _in_pad(c_kr0 + ROPE)
    tn_in = in_pad // 5 if in_pad % (5 * LANE) == 0 else LANE
    tables = _rope_tables(s, n_heads)

    h1 = _rms_fwd("pre_mix_norm", x, pre_mix_g, after=first_after)
    weights.forward(0, (h1, *tables))
    weights.relay(0, ())
    w_in_p, conv_w = weights.ready(0, ())
    proj = _mm_nn("in_proj", h1, w_in_p, F32, TILE_M, tn_in)
    y_conv = _conv_fwd(proj, conv_w, conv_out_g, n_groups, conv_width + n_heads * HEAD)
    qn = _rms_fwd("q_norm", proj, q_norm_g, cols=(c_q0, r_q))
    kvn = _rms_fwd("kv_norm", proj, kv_norm_g, cols=(c_kv0, r_kv))
    weights.forward(1, (y_conv, qn, kvn))
    w_uq_p, w_ukv, w_o = weights.ready(1, ())
    q = _mm_nn("q_up", qn, w_uq_p, F32, TILE_M, TILE_N)
    kv = _mm_nn("kv_up", kvn, w_ukv, F32, TILE_M, TILE_N)
    qh, kh, vh = _pack_heads(q, kv, proj, c_kr0, tables, n_heads, after=weights.forward(2, (q, kv)))
    o, mix = _attn_fwd(qh, kh, vh, attn_out_g, y_conv, conv_width)
    y = _mm_nn("out_proj", mix, w_o, F32, TILE_M, TILE_N, after=weights.forward(3, (mix,)))
    x2, h2 = _mid_fwd(x, y, post_mix_g, pre_mlp_g)
    weights.relay(2, (h2,))
    (w_up,) = weights.ready(2, ())
    a, r = _up_fwd(h2, w_up)
    weights.relay(3, (a,))
    (w_down,) = weights.ready(3, ())
    m = _down_fwd(a, w_down)

    d_out, d_m, dg_post_mlp, loss_part = _head(m, x2, tgt, post_mlp_g)
    core = grads.core
    away = _down_half_grad("down_bwd_w_away", a, d_m, core, home=False)
    d_up = _down_bwd_act(d_m, w_down, r, after=grads.send_away(0, away))
    sums = _down_half_grad("down_bwd_w_home", a, d_m, core, home=True, received=grads.received(0, (d_up,)))
    away = _up_half_grad("up_bwd_w_away", h2, d_up, core, home=False, after=grads.send_sums(0, (sums,)))
    d_h2 = _up_bwd_act(d_up, w_up, after=grads.send_away(1, away))
    sums = _up_half_grad("up_bwd_w_home", h2, d_up, core, home=True, received=grads.received(1, (d_h2,)))
    d_x2, d_y, dg_pre_mlp, dg_post_mix = _mid_bwd(x2, y, d_out, d_h2, pre_mlp_g, post_mix_g, after=grads.send_sums(1, (sums,)))
    d_mix = _mm_nt("out_proj_bwd_act", d_y, w_o, F32, TILE_M, TILE_N)
    gw_o = _mm_tn("out_proj_bwd_w", mix, d_y, BF16, TILE_M, TILE_N)
    dqh, dkh, dvh, dg_attn = _attn_bwd(qh, kh, vh, o, d_mix, attn_out_g, conv_width, after=grads.full(2, (gw_o,)))
    d_q, d_kv, d_kr = _unpack_heads(dqh, dkh, dvh, tables, n_heads)
    d_qn = _mm_nt("q_up_bwd_act", d_q, w_uq_p, F32, TILE_M, TILE_N)
    d_kvn = _mm_nt("kv_up_bwd_act", d_kv, w_ukv, F32, TILE_M, TILE_N)
    gw_uq_t = _mm_tn("q_up_bwd_w", d_q, qn, F32, TILE_M, TILE_N)
    gw_ukv = _mm_tn("kv_up_bwd_w", kvn, d_kv, BF16, TILE_M, TILE_N)
    d_cq, dg_q = _rms_bwd_call("q_norm_bwd", proj, q_norm_g, d_qn, BF16, cols=(c_q0, r_q), after=grads.full(3, (gw_uq_t, gw_ukv)))
    d_ckv, dg_kv = _rms_bwd_call("kv_norm_bwd", proj, kv_norm_g, d_kvn, BF16, cols=(c_kv0, r_kv))
    d_u, d_b, d_c, dg_conv, dw_conv = _conv_bwd(proj, d_mix, conv_w, conv_out_g, n_groups)
    d_proj = jnp.concatenate([d_u, d_b, d_c, d_cq, d_ckv, d_kr[:, :in_pad - c_kr0]], axis=1)
    gw_in_t = _mm_tn("in_proj_bwd_w", d_proj, h1, F32, tn_in, TILE_N)
    d_h1 = _mm_nt("in_proj_bwd_act", d_proj, w_in_p, F32, TILE_M, 512, after=grads.send_away(4, gw_in_t))
    grad_x, dg_pre_mix = _first_bwd(x, pre_mix_g, d_h1, d_x2, after=grads.full(4, (gw_in_t,), received=(d_h1,)))

    small = [dg_pre_mix, dg_q, dg_kv, dg_conv, dg_attn, dg_post_mix, dg_pre_mlp, dg_post_mlp,
             dw_conv[0], dw_conv[1], dw_conv[2], loss_part]
    return grad_x, jnp.concatenate(small, axis=1)


HBM = pl.BlockSpec(memory_space=pltpu.HBM)
SEM = pl.BlockSpec(memory_space=pltpu.SEMAPHORE)
IN_VMEM = pl.BlockSpec(memory_space=pltpu.VMEM)
SPLIT = pltpu.CompilerParams(has_side_effects=pltpu.SideEffectType.DATAFLOW_SIDE_EFFECTING)


def _in_hbm(a):
    return pltpu.with_memory_space_constraint(a, pltpu.HBM)


def _hbm_like(a):
    return pltpu.HBM(a.shape, a.dtype)


def _place():
    x, y, c = lax.axis_index("x"), lax.axis_index("y"), lax.axis_index("c")
    other_chips = [(1 - x, y), (x, 1 - y), (1 - x, 1 - y)]
    return x, y, c, other_chips


def _block(px, py, pc):
    return 4 * px + 2 * py + pc


def _await(block, sem):
    pltpu.make_async_copy(block, block, sem).wait()


def _relay_route(x, y, c):
    came_from = ((1 - x) * (1 - c) + x * c, y * (1 - c) + (1 - y) * c)
    goes_to = (x * (1 - c) + (1 - x) * c, (1 - y) * (1 - c) + y * c)
    return came_from, goes_to


def _gather_start(name, shards, groups, relayed=(), after=()):
    n, ng = len(shards), len(groups)
    lands = [lax.empty((N_DEV, *a.shape), a.dtype) for a in shards]

    def body(*refs):
        src, land = refs[:n], refs[n:2 * n]
        sems, token = refs[2 * n + len(after):2 * n + len(after) + 2 * ng], refs[-1]
        x, y, c, chips = _place()
        targets = [(x, y, 1 - c)] + [(*chip, c) for chip in chips]
        for gi, group in enumerate(groups):
            for i, w in enumerate(group):
                for k, to in enumerate(targets[:3] if gi in relayed else targets):
                    pltpu.make_async_remote_copy(
                        src_ref=src[w], dst_ref=land[w].at[_block(x, y, c)],
                        send_sem=sems[2 * gi].at[4 * i + k], recv_sem=sems[2 * gi + 1].at[4 * i + k],
                        device_id=to, device_id_type=MESH).start()
        token[...] = jnp.zeros_like(token)

    sem_shapes = [pltpu.SemaphoreType.DMA((4 * len(g),)) for g in groups for _ in range(2)]
    out = pl.pallas_call(
        body, name=name,
        in_specs=[HBM] * (2 * n) + [ANY] * len(after),
        out_specs=[SEM] * (2 * ng) + [HBM] * (2 * n) + [IN_VMEM],
        out_shape=sem_shapes + [_hbm_like(a) for a in shards] + [_hbm_like(a) for a in lands]
        + [jax.ShapeDtypeStruct((SUBLANE, LANE), F32)],
        input_output_aliases={i: 2 * ng + i for i in range(2 * n)},
        compiler_params=SPLIT,
    )(*[_in_hbm(a) for a in shards], *[_in_hbm(a) for a in lands], *after)
    sems = [(out[2 * gi], out[2 * gi + 1]) for gi in range(ng)]
    return sems, out[2 * ng:2 * ng + n], out[2 * ng + n:2 * ng + 2 * n], out[-1]


def _gather_forward(name, shards, lands, send1, recv1, after, relayed=False):
    n = len(lands)

    def body(*refs):
        src, land = refs[:n], refs[n:2 * n]
        s1, r1 = refs[2 * n], refs[2 * n + 1]
        s2, r2 = refs[2 * n + 2 + len(after)], refs[2 * n + 3 + len(after)]
        x, y, c, chips = _place()
        me, sibling = (x, y, c), (x, y, 1 - c)
        for j, chip in enumerate(chips[:2] if relayed else chips):
            for i in range(n):
                blk = land[i].at[_block(*chip, c)]
                pltpu.make_async_remote_copy(src_ref=blk, dst_ref=blk, send_sem=s1.at[4 * i + 1 + j], recv_sem=r1.at[4 * i + 1 + j],
                                             device_id=me, device_id_type=MESH).wait_recv()
                pltpu.make_async_remote_copy(src_ref=blk, dst_ref=blk, send_sem=s2.at[3 * i + j], recv_sem=r2.at[3 * i + j],
                                             device_id=sibling, device_id_type=MESH).start()
        if relayed:
            came_from, goes_to = _relay_route(x, y, c)
            for i in range(n):
                blk = land[i].at[_block(*came_from, c)]
                pltpu.make_async_remote_copy(src_ref=blk, dst_ref=blk, send_sem=s2.at[3 * i + 2], recv_sem=r2.at[3 * i + 2],
                                             device_id=(*goes_to, c), device_id_type=MESH).start()
        for i in range(n):
            blk = land[i].at[_block(x, y, 1 - c)]
            pltpu.make_async_remote_copy(src_ref=blk, dst_ref=blk, send_sem=s1.at[4 * i], recv_sem=r1.at[4 * i],
                                         device_id=me, device_id_type=MESH).wait_recv()
            for k in range(3 if relayed else 4):
                pltpu.make_async_remote_copy(src_ref=src[i], dst_ref=land[i].at[_block(x, y, c)], send_sem=s1.at[4 * i + k],
                                             recv_sem=r1.at[4 * i + k], device_id=sibling, device_id_type=MESH).wait_send()

    sem = pltpu.SemaphoreType.DMA((3 * n,))
    out = pl.pallas_call(
        body, name=name,
        in_specs=[HBM] * (2 * n) + [SEM, SEM] + [ANY] * len(after),
        out_specs=[SEM, SEM] + [HBM] * n,
        out_shape=[sem, sem] + [_hbm_like(a) for a in lands],
        input_output_aliases={n + i: 2 + i for i in range(n)},
        compiler_params=SPLIT,
    )(*shards, *lands, send1, recv1, *after)
    return (out[0], out[1]), out[2:]


def _gather_relay_forward(name, lands, send2, recv2, after):
    n = len(lands)

    def body(*refs):
        land, s2, r2 = refs[:n], refs[n], refs[n + 1]
        s3, r3 = refs[n + 2 + len(after)], refs[n + 3 + len(after)]
        x, y, c, _ = _place()
        me, sibling = (x, y, c), (x, y, 1 - c)
        came_from, _ = _relay_route(x, y, c)
        for i in range(n):
            blk = land[i].at[_block(1 - x, 1 - y, c)]
            pltpu.make_async_remote_copy(src_ref=blk, dst_ref=blk, send_sem=s2.at[3 * i + 2], recv_sem=r2.at[3 * i + 2],
                                         device_id=me, device_id_type=MESH).wait_recv()
            pltpu.make_async_remote_copy(src_ref=blk, dst_ref=blk, send_sem=s3.at[i], recv_sem=r3.at[i],
                                         device_id=sibling, device_id_type=MESH).start()
            sent = land[i].at[_block(*came_from, c)]
            pltpu.make_async_remote_copy(src_ref=sent, dst_ref=sent, send_sem=s2.at[3 * i + 2], recv_sem=r2.at[3 * i + 2],
                                         device_id=me, device_id_type=MESH).wait_send()

    sem = pltpu.SemaphoreType.DMA((n,))
    out = pl.pallas_call(
        body, name=name,
        in_specs=[HBM] * n + [SEM, SEM] + [ANY] * len(after),
        out_specs=[SEM, SEM] + [HBM] * n,
        out_shape=[sem, sem] + [_hbm_like(a) for a in lands],
        input_output_aliases={i: 2 + i for i in range(n)},
        compiler_params=SPLIT,
    )(*lands, send2, recv2, *after)
    return (out[0], out[1]), out[2:]


def _gather_wait(name, lands, send2, recv2, after, relay_sems=None):
    n = len(lands)
    n_sems = 2 if relay_sems is None else 4

    def body(*refs):
        land, s2, r2 = refs[:n], refs[n], refs[n + 1]
        for i in range(n):
            for j in range(3 if relay_sems is None else 2):
                _await(land[i].at[0], r2.at[3 * i + j])
                _await(land[i].at[0], s2.at[3 * i + j])
            if relay_sems is not None:
                _await(land[i].at[0], refs[n + 3].at[i])
                _await(land[i].at[0], refs[n + 2].at[i])

    return pl.pallas_call(
        body, name=name,
        in_specs=[HBM] * n + [SEM] * n_sems + [ANY] * len(after), out_specs=[HBM] * n, out_shape=[_hbm_like(a) for a in lands],
        input_output_aliases={i: i for i in range(n)},
        compiler_params=SPLIT,
    )(*lands, send2, recv2, *(relay_sems or ()), *after)


def _pair_exchange(name, grads, shard_rows):
    n = len(grads)
    shapes = [(g.shape[1:] if r is None else (r, g.shape[1])) for g, r in zip(grads, shard_rows)]

    def body(*refs):
        ins, recv = refs[:n], refs[n:2 * n]
        send_sems, recv_sems = refs[2 * n:]
        x, y, c, _ = _place()
        sends = []
        for w in range(n):
            for k in range(N_CHIP):
                j, r = 2 * k + 1 - c, shard_rows[w]
                src = ins[w].at[j] if r is None else ins[w].at[pl.ds(pl.multiple_of(j * r, SUBLANE), r), :]
                sends.append(pltpu.make_async_remote_copy(
                    src_ref=src, dst_ref=recv[w].at[k],
                    send_sem=send_sems.at[w, k], recv_sem=recv_sems.at[w, k],
                    device_id=(x, y, 1 - c), device_id_type=MESH))
        for cp in sends:
            cp.start()
        for cp in sends:
            cp.wait()

    return pl.pallas_call(
        body, name=name,
        in_specs=[ANY] * n, out_specs=[ANY] * n,
        out_shape=[jax.ShapeDtypeStruct((N_CHIP, *shape), g.dtype) for g, shape in zip(grads, shapes)],
        scratch_shapes=[pltpu.SemaphoreType.DMA((n, N_CHIP))] * 2,
    )(*grads)


def _pair_sum_rows(name, grad, received, core):
    _, r, c = received.shape
    tc = _fit(c, 512)

    def body(core_ref, a_ref, b_ref, o_ref):
        o_ref[...] = (a_ref[...] + b_ref[...]).astype(o_ref.dtype)

    spec = pl.BlockSpec((None, r, tc), lambda k, i, core_ref: (k, 0, i))
    return pl.pallas_call(
        body, name=name,
        grid_spec=pltpu.PrefetchScalarGridSpec(
            num_scalar_prefetch=1, grid=(N_CHIP, c // tc),
            in_specs=[pl.BlockSpec((r, tc), lambda k, i, core_ref: (2 * k + core_ref[0], i)), spec],
            out_specs=spec),
        out_shape=jax.ShapeDtypeStruct(received.shape, BF16),
        compiler_params=_params("parallel", "parallel"),
    )(core, grad, received)


def _pair_sum(name, grad, received, core):
    _, r, c = received.shape
    rows = min(ROWS, r)
    assert r % rows == 0

    def body(core_ref, a_ref, b_ref, o_ref):
        o_ref[...] = (a_ref[...].astype(F32) + b_ref[...].astype(F32)).astype(o_ref.dtype)

    spec = pl.BlockSpec((None, rows, c), lambda k, i, core_ref: (k, i, 0))
    return pl.pallas_call(
        body, name=name,
        grid_spec=pltpu.PrefetchScalarGridSpec(
            num_scalar_prefetch=1, grid=(N_CHIP, r // rows),
            in_specs=[pl.BlockSpec((None, None, rows, c), lambda k, i, core_ref: (k, core_ref[0], i, 0)), spec],
            out_specs=spec),
        out_shape=jax.ShapeDtypeStruct(received.shape, received.dtype),
        compiler_params=_params("parallel", "parallel"),
    )(core, grad.reshape(N_CHIP, 2, r, c), received)


def _away_shard(src, k, c, shard_rows):
    if shard_rows is None:
        return src.at[k]
    return src.at[pl.ds(pl.multiple_of((2 * k + 1 - c) * shard_rows, SUBLANE), shard_rows), :]


def _pair_send_start(name, away, shard_rows=None):
    shape = away.shape if shard_rows is None else (N_CHIP, shard_rows, away.shape[1])
    land = lax.empty(shape, away.dtype)

    def body(src, dst, send, recv, src_thru, dst_thru, token):
        x, y, c, _ = _place()
        for k in range(N_CHIP):
            pltpu.make_async_remote_copy(src_ref=_away_shard(src, k, c, shard_rows), dst_ref=dst.at[k], send_sem=send.at[k],
                                         recv_sem=recv.at[k], device_id=(x, y, 1 - c), device_id_type=MESH).start()
        token[...] = jnp.zeros_like(token)

    sem = pltpu.SemaphoreType.DMA((N_CHIP,))
    out = pl.pallas_call(
        body, name=name,
        in_specs=[HBM, HBM], out_specs=[SEM, SEM, HBM, HBM, IN_VMEM],
        out_shape=[sem, sem, _hbm_like(away), _hbm_like(land), jax.ShapeDtypeStruct((SUBLANE, LANE), F32)],
        input_output_aliases={0: 2, 1: 3},
        compiler_params=SPLIT,
    )(_in_hbm(away), _in_hbm(land))
    return (out[0], out[1]), out[2], out[3], out[4]


def _pair_send_wait(name, sems, src, land, after, shard_rows=None):
    def body(src_ref, dst_ref, send, recv, *rest):
        for k in range(N_CHIP):
            _await(dst_ref.at[k], send.at[k])
            _await(dst_ref.at[k], recv.at[k])

    return pl.pallas_call(
        body, name=name,
        in_specs=[HBM, HBM, SEM, SEM] + [ANY] * len(after), out_specs=HBM, out_shape=_hbm_like(land),
        input_output_aliases={1: 0},
        compiler_params=SPLIT,
    )(src, land, *sems, *after)


def _chip_send_start(name, sums):
    n = len(sums)
    lands = [lax.empty(a.shape, a.dtype) for a in sums]

    def body(*refs):
        src, land = refs[:n], refs[n:2 * n]
        send, recv, token = refs[2 * n], refs[2 * n + 1], refs[-1]
        x, y, c, chips = _place()
        for w in range(n):
            for j, (px, py) in enumerate(chips):
                pltpu.make_async_remote_copy(
                    src_ref=src[w].at[2 * px + py], dst_ref=land[w].at[2 * x + y],
                    send_sem=send.at[3 * w + j], recv_sem=recv.at[3 * w + j],
                    device_id=(px, py, c), device_id_type=MESH).start()
        token[...] = jnp.zeros_like(token)

    sem = pltpu.SemaphoreType.DMA((3 * n,))
    out = pl.pallas_call(
        body, name=name,
        in_specs=[HBM] * (2 * n),
        out_specs=[SEM, SEM] + [HBM] * (2 * n) + [IN_VMEM],
        out_shape=[sem, sem] + [_hbm_like(a) for a in sums] + [_hbm_like(a) for a in lands]
        + [jax.ShapeDtypeStruct((SUBLANE, LANE), F32)],
        input_output_aliases={i: 2 + i for i in range(2 * n)},
        compiler_params=SPLIT,
    )(*[_in_hbm(a) for a in sums], *[_in_hbm(a) for a in lands])
    return (out[0], out[1]), out[2:2 + n], out[2 + n:2 + 2 * n], out[-1]


def _chip_send_wait(name, groups, after):
    counts = [len(g[1]) for g in groups]
    n = sum(counts)

    def body(*refs):
        land = refs[n:2 * n]
        sems = refs[2 * n:2 * n + 2 * len(groups)]
        w = 0
        for gi, count in enumerate(counts):
            for i in range(count):
                for j in range(3):
                    _await(land[w].at[0], sems[2 * gi].at[3 * i + j])
                    _await(land[w].at[0], sems[2 * gi + 1].at[3 * i + j])
                w += 1

    sums = [a for g in groups for a in g[1]]
    lands = [a for g in groups for a in g[2]]
    sems = [s for g in groups for s in g[0]]
    return pl.pallas_call(
        body, name=name,
        in_specs=[HBM] * (2 * n) + [SEM] * len(sems) + [ANY] * len(after),
        out_specs=[HBM] * n, out_shape=[_hbm_like(a) for a in lands],
        input_output_aliases={n + i: i for i in range(n)},
        compiler_params=SPLIT,
    )(*sums, *lands, *sems, *after)


def _small_all_reduce(part, after=()):
    _, w = part.shape

    def body(p_ref, *rest):
        o_ref, buf, send_sems, recv_sems = rest[len(after):]
        x, y, c, _ = _place()
        me = 4 * x + 2 * y + c
        buf[me] = jnp.sum(p_ref[...], axis=0, keepdims=True)
        copies = []
        for k in range(1, N_DEV):
            dx, dy, dc = (k >> 2) & 1, (k >> 1) & 1, k & 1
            copies.append(pltpu.make_async_remote_copy(
                src_ref=buf.at[me], dst_ref=buf.at[me], send_sem=send_sems.at[k - 1], recv_sem=recv_sems.at[k - 1],
                device_id=(x ^ dx, y ^ dy, c ^ dc), device_id_type=MESH))
        for cp in copies:
            cp.start()
        for cp in copies:
            cp.wait()
        tot = buf[0]
        for d in range(1, N_DEV):
            tot = tot + buf[d]
        o_ref[...] = tot
        loss = jnp.sum(tot[:, w - LANE:], axis=1, keepdims=True)
        o_ref[:, w - LANE:] = jnp.broadcast_to(loss, (1, LANE))

    return pl.pallas_call(
        body, name="small_all_reduce",
        in_specs=[IN_VMEM] + [ANY] * len(after), out_specs=IN_VMEM,
        out_shape=jax.ShapeDtypeStruct((1, w), F32),
        scratch_shapes=[pltpu.VMEM((N_DEV, 1, w), F32), pltpu.SemaphoreType.DMA((N_DEV - 1,)), pltpu.SemaphoreType.DMA((N_DEV - 1,))],
        compiler_params=pltpu.CompilerParams(vmem_limit_bytes=VMEM_LIMIT_BYTES),
    )(part, *after)


def _adamw(w, g, m, v):
    m = ADAM_B1 * m + (1.0 - ADAM_B1) * g
    v = ADAM_B2 * v + (1.0 - ADAM_B2) * (g * g)
    m_hat = m / (1.0 - ADAM_B1 ** ADAM_STEP)
    v_hat = v / (1.0 - ADAM_B2 ** ADAM_STEP)
    delta = -ADAM_LR * (m_hat / (jnp.sqrt(v_hat) + ADAM_EPS) + ADAM_WD * w)
    return delta, m, v


def _sum_adam(name, parts, sums, chip, w, m, v, after=()):
    _, r, c = w.shape
    n_after = len(after)
    by_rows = r % ROWS == 0 or r < ROWS
    tr, tc = (min(ROWS, r), c) if by_rows else (r, _fit(c, 512))
    at = (lambda i: (i, 0)) if by_rows else (lambda i: (0, i))

    def body(chip_ref, p_ref, own_ref, w_ref, m_ref, v_ref, *rest):
        g_ref, d_ref, mo_ref, vo_ref = rest[n_after:]
        g = None
        for k in range(N_CHIP):
            term = jnp.where(chip_ref[0] == k, own_ref[...], p_ref[k]).astype(F32)
            g = term if g is None else g + term
        g_ref[...] = g
        d_ref[...], mo_ref[...], vo_ref[...] = _adamw(w_ref[...], g, m_ref[...], v_ref[...])

    blk = pl.BlockSpec((None, tr, tc), lambda i, chip_ref: (0, *at(i)))
    out = jax.ShapeDtypeStruct((1, r, c), F32)
    return pl.pallas_call(
        body, name=name,
        grid_spec=pltpu.PrefetchScalarGridSpec(
            num_scalar_prefetch=1, grid=(r // tr if by_rows else c // tc,),
            in_specs=[pl.BlockSpec((N_CHIP, tr, tc), lambda i, chip_ref: (0, *at(i))),
                      pl.BlockSpec((None, tr, tc), lambda i, chip_ref: (chip_ref[0], *at(i))), blk, blk, blk]
            + [ANY] * n_after,
            out_specs=[blk] * 4),
        out_shape=[out] * 4,
        compiler_params=_params("parallel"),
    )(chip, parts, sums, w, m, v, *after)


def _adam_gains(total, ws, ms, vs):
    n = len(ws)
    widths = [w.shape[1] for w in ws]

    def body(t_ref, *refs):
        w_refs, m_refs, v_refs, outs = refs[:n], refs[n:2 * n], refs[2 * n:3 * n], refs[3 * n:]
        off = 0
        for i in range(n):
            g = t_ref[:, off:off + widths[i]]
            off += widths[i]
            g_ref, d_ref, mo_ref, vo_ref = outs[4 * i:4 * i + 4]
            g_ref[...] = g
            d_ref[...], mo_ref[...], vo_ref[...] = _adamw(w_refs[i][...], g, m_refs[i][...], v_refs[i][...])

    out = pl.pallas_call(
        body, name="adam_gains",
        out_shape=[jax.ShapeDtypeStruct(w.shape, F32) for w in ws for _ in range(4)],
    )(total, *ws, *ms, *vs)
    return [tuple(out[4 * i:4 * i + 4]) for i in range(n)]


def _adam_taps(total, first_col, device, w, m, v):
    _, n_taps, cw = w.shape
    col_block = lambda t, dev: (0, first_col // cw + t * N_DEV + dev[0])
    tap = pl.BlockSpec((None, 1, cw), lambda t, dev: (t, 0, 0))

    def body(dev_ref, t_ref, w_ref, m_ref, v_ref, g_ref, d_ref, mo_ref, vo_ref):
        g = t_ref[...]
        g_ref[...] = g
        d_ref[...], mo_ref[...], vo_ref[...] = _adamw(w_ref[...], g, m_ref[...], v_ref[...])

    shape3 = (n_taps, 1, cw)
    out = pl.pallas_call(
        body, name="adam_taps",
        grid_spec=pltpu.PrefetchScalarGridSpec(
            num_scalar_prefetch=1, grid=(n_taps,),
            in_specs=[pl.BlockSpec((1, cw), col_block), tap, tap, tap], out_specs=[tap] * 4),
        out_shape=[jax.ShapeDtypeStruct(shape3, F32)] * 4,
    )(device, total, w.reshape(shape3), m.reshape(shape3), v.reshape(shape3))
    return tuple(o.reshape(w.shape) for o in out)


def kernel(x, pre_mix_g, w_in, conv_w, q_norm_g, w_uq, kv_norm_g, w_ukv, conv_out_g, attn_out_g, w_o, post_mix_g, pre_mlp_g, w_up, w_down, post_mlp_g, loss_target, m_pre_mix_g, m_w_in, m_conv_w, m_q_norm_g, m_w_uq, m_kv_norm_g, m_w_ukv, m_conv_out_g, m_attn_out_g, m_w_o, m_post_mix_g, m_pre_mlp_g, m_w_up, m_w_down, m_post_mlp_g, v_pre_mix_g, v_w_in, v_conv_w, v_q_norm_g, v_w_uq, v_kv_norm_g, v_w_ukv, v_conv_out_g, v_attn_out_g, v_w_o, v_post_mix_g, v_pre_mlp_g, v_w_up, v_w_down, v_post_mlp_g):
    me = 4 * lax.axis_index("x") + 2 * lax.axis_index("y") + lax.axis_index("c")
    core = lax.axis_index("c").astype(jnp.int32).reshape(1)
    chip = (2 * lax.axis_index("x") + lax.axis_index("y")).astype(jnp.int32).reshape(1)
    gains = (pre_mix_g, q_norm_g, kv_norm_g, conv_out_g, attn_out_g, post_mix_g, pre_mlp_g, post_mlp_g)
    gain_m = (m_pre_mix_g, m_q_norm_g, m_kv_norm_g, m_conv_out_g, m_attn_out_g, m_post_mix_g, m_pre_mlp_g, m_post_mlp_g)
    gain_v = (v_pre_mix_g, v_q_norm_g, v_kv_norm_g, v_conv_out_g, v_attn_out_g, v_post_mix_g, v_pre_mlp_g, v_post_mlp_g)
    names = ("w_in", "w_uq", "w_ukv", "w_o", "w_up", "w_down")
    big = dict(zip(names, (w_in, w_uq, w_ukv, w_o, w_up, w_down)))
    big_m = dict(zip(names, (m_w_in, m_w_uq, m_w_ukv, m_w_o, m_w_up, m_w_down)))
    big_v = dict(zip(names, (v_w_in, v_w_uq, v_w_ukv, v_w_o, v_w_up, v_w_down)))
    n_heads = attn_out_g.shape[1] // HEAD
    n_taps = conv_w.shape[1]

    gathered = ("w_in", "conv", "w_uq", "w_ukv", "w_o", "w_up", "w_down")
    gather_groups = ((0, 1), (2, 3, 4), (5,), (6,))
    taps = jnp.pad(conv_w[0], ((0, SUBLANE - n_taps), (0, 0)))
    relayed_groups = (0, 2, 3)
    sems1, shards, lands, token = _gather_start("gather_start_first", [w_in[0].astype(BF16), taps], ((0, 1),), relayed=(0,))
    sems1, shards, lands = list(sems1), list(shards), list(lands)
    behind = token[0, 0]
    rest = [(big[nm][0] + behind).astype(BF16) for nm in gathered[2:]]

    def start_rest(after):
        sems_b, shards_b, lands_b, started = _gather_start("gather_start_rest", rest, ((0, 1, 2), (3,), (4,)), relayed=(1, 2),
                                                          after=after)
        sems1.extend(sems_b)
        shards.extend(shards_b)
        lands.extend(lands_b)
        return started

    cols = lambda a: jnp.concatenate([a[j] for j in range(N_DEV)], axis=1)
    rows = lambda a: a.reshape(N_DEV * a.shape[1], a.shape[2])
    ready = {
        "w_in": _join_col_shards,
        "conv": lambda a: cols(a)[:n_taps],
        "w_uq": lambda a: _permute_q_cols(cols(a), n_heads),
        "w_ukv": cols, "w_o": rows, "w_up": lambda a: a, "w_down": rows,
    }

    class Weights:
        def __init__(self):
            self.passed, self.relayed = {}, {}

        def forward(self, group, after):
            idx = gather_groups[group]
            if group == 0:
                after = (*after, *rest)
            self.passed[group] = _gather_forward(f"gather_forward_{group}", [shards[i] for i in idx], [lands[i] for i in idx],
                                                 *sems1[group], after, relayed=group in relayed_groups)
            if group == 0:
                self.rest_started = start_rest(tuple(self.passed[0][1]))
            return tuple(self.passed[group][1])

        def relay(self, group, after):
            sems2, mid = self.passed[group]
            if group == 0:
                after = (*after, self.rest_started)
            self.relayed[group], mid = _gather_relay_forward(f"gather_relay_{group}", mid, *sems2, after)
            self.passed[group] = (sems2, mid)
            return tuple(mid)

        def ready(self, group, after):
            sems2, mid = self.passed[group]
            full = _gather_wait(f"gather_wait_{group}", mid, *sems2, after, relay_sems=self.relayed.get(group))
            out = []
            for i, a in zip(gather_groups[group], full):
                a = lax.dynamic_update_index_in_dim(a, shards[i], me, 0)
                out.append(ready[gathered[i]](a))
            return out

    weights = Weights()

    col_blocks = lambda g: g.reshape(g.shape[0], N_DEV, g.shape[1] // N_DEV).transpose(1, 0, 2)
    row_blocks = lambda g: g.reshape(N_DEV, g.shape[0] // N_DEV, g.shape[1])
    grad_groups = (("w_down",), ("w_up",), ("w_o",), ("w_uq", "w_ukv"), ("w_in",))
    transposed = {"w_in": w_in.shape[2], "w_uq": w_uq.shape[2]}
    to_blocks = {
        "w_in": lambda g: g, "w_uq": lambda g: _unpermute_q_rows(g, n_heads),
        "w_ukv": col_blocks, "w_o": row_blocks, "w_up": lambda g: g, "w_down": row_blocks,
    }
    in_flight = []

    class Grads:
        def __init__(self):
            self.core = core
            self.away = {}

        def send_sums(self, group, sums):
            sems, sums, parts, tok = _chip_send_start(f"chip_send_start_{group}", list(sums))
            in_flight.append((sems, sums, parts))
            return (tok,)

        def full(self, group, arrays, received=None):
            nms = grad_groups[group]
            if received is None:
                blocks = [to_blocks[nm](g) for nm, g in zip(nms, arrays)]
                got = _pair_exchange(f"pair_exchange_{group}", blocks, [transposed.get(nm) for nm in nms])
            else:
                blocks, got = [self.away[group][1]], [self.received(group, received)]
            sums = [(_pair_sum_rows if nm in transposed else _pair_sum)(f"pair_sum_{nm}", g, r, core)
                    for nm, g, r in zip(nms, blocks, got)]
            return self.send_sums(group, sums)

        def send_away(self, group, half):
            nm = grad_groups[group][0]
            rows = transposed.get(nm)
            sems, src, land, tok = _pair_send_start(f"pair_send_start_{group}", half if rows is None else to_blocks[nm](half), rows)
            self.away[group] = (sems, src, land, rows)
            return (tok,)

        def received(self, group, after):
            sems, src, land, rows = self.away[group]
            return _pair_send_wait(f"pair_send_wait_{group}", sems, src, land, after, rows)

    grad_x, small = _local_step(x[0], loss_target[0], gains, weights, Grads(), first_after=(token,))

    big_out = {}

    def update(tag, first, last, after):
        groups = in_flight[first:last]
        parts = _chip_send_wait("chip_send_wait_" + tag, groups, after)
        nms = [nm for grp in grad_groups[first:last] for nm in grp]
        sums = [a for _, s, _ in groups for a in s]
        for nm, p, s in zip(nms, parts, sums):
            view = (lambda a: jnp.swapaxes(a, 1, 2)) if nm in transposed else (lambda a: a)
            out = _sum_adam("adam_" + nm, p, s, chip, view(big[nm]), view(big_m[nm]), view(big_v[nm]), after=after)
            after = (out[0],)
            big_out[nm] = [view(o) for o in out]
        return after

    after = update("early", 0, len(in_flight) - 1, (grad_x,))
    total = _small_all_reduce(small, after=after)
    update("late", len(in_flight) - 1, len(in_flight), (total,))
    big_out = [big_out[nm] for nm in names]

    gain_out = _adam_gains(total, gains, gain_m, gain_v)
    taps_out = _adam_taps(total, sum(g.shape[1] for g in gains), me.astype(jnp.int32).reshape(1), conv_w, m_conv_w, v_conv_w)
    loss = total[0, total.shape[1] - 1]

    order = (0, "w_in", "conv", 1, "w_uq", 2, "w_ukv", 3, 4, "w_o", 5, 6, "w_up", "w_down", 7)
    by_name = dict(zip(names, big_out))
    outs = [loss, grad_x[None]]
    for kind in range(4):
        for item in order:
            if item == "conv":
                outs.append(taps_out[kind])
            elif isinstance(item, int):
                outs.append(gain_out[item][kind])
            else:
                outs.append(by_name[item][kind])
    return tuple(outs)
```

```python
import math

import jax
import jax.numpy as jnp
from jax import lax
from jax.experimental import pallas as pl
from jax.experimental.pallas import tpu as pltpu

F32 = jnp.float32
BF16 = jnp.bfloat16

EPS = 1e-6
NEG_INF = -1e30
HEAD = 128
ROPE = 64
QK = HEAD + ROPE
CHUNK = 64
ROPE_THETA = 10000.0
ADAM_LR, ADAM_B1, ADAM_B2, ADAM_EPS, ADAM_WD, ADAM_STEP = 0.001, 0.9, 0.999, 1e-08, 0.01, 10

LANE = 128
SUBLANE = 8
VMEM_LIMIT_BYTES = 56 * 1024 * 1024

N_DEV = 8
N_CHIP = 4
MESH = pl.DeviceIdType.MESH


def _params(*sem):
    return pltpu.CompilerParams(dimension_semantics=sem, vmem_limit_bytes=VMEM_LIMIT_BYTES)


ANY = pl.BlockSpec(memory_space=pl.ANY)


def _call(body, *, in_specs, after=(), **kw):
    n_in, n_after = len(in_specs), len(after)

    def ordered(*refs):
        body(*refs[:n_in], *refs[n_in + n_after:])

    call = pl.pallas_call(ordered, in_specs=[*in_specs, *[ANY] * n_after], **kw)
    return lambda *operands: call(*operands, *after)


def _sublane_sum(v):
    r, w = v.shape
    return jnp.sum(v.reshape(r // SUBLANE, SUBLANE, w), axis=0)


def _rstd(x):
    return lax.rsqrt(jnp.mean(x * x, axis=-1, keepdims=True) + EPS)


def _rms_bwd(x, g, dy):
    r = _rstd(x)
    xh = x * r
    dxh = dy * g
    dx = r * (dxh - xh * jnp.mean(dxh * xh, axis=-1, keepdims=True))
    return dx, dy * xh


def _accumulate(ref, val, step):
    @pl.when(step == 0)
    def _():
        ref[...] = val

    @pl.when(step > 0)
    def _():
        ref[...] += val


NN = ((1,), (0,))
NT = ((1,), (1,))
TN = ((0,), (0,))


def _matmul(name, a, b, *, grid, a_spec, b_spec, out_shape, out_specs, contract, nk=1, acc_shape=None,
            extras=(), extra_specs=(), epilogue=None, after=()):
    multi = isinstance(out_shape, (tuple, list))
    out_shapes = tuple(out_shape) if multi else (out_shape,)
    n_out = len(out_shapes)
    n_extra = len(extras)

    def body(a_ref, b_ref, *rest):
        x_refs = rest[:n_extra]
        o_refs = rest[n_extra:n_extra + n_out]

        def emit(acc):
            vals = epilogue(acc, *[r[...] for r in x_refs]) if epilogue else (acc,)
            for r, v in zip(o_refs, vals):
                r[...] = v.astype(r.dtype)

        p = lax.dot_general(a_ref[...], b_ref[...], (contract, ((), ())), preferred_element_type=F32)
        if nk == 1:
            emit(p)
        else:
            acc_ref = rest[n_extra + n_out]
            k = pl.program_id(2)
            _accumulate(acc_ref, p, k)

            @pl.when(k == nk - 1)
            def _():
                emit(acc_ref[...])

    sem = ("parallel", "parallel") + (("arbitrary",) if nk > 1 else ())
    return _call(
        body, name=name, grid=grid, after=after,
        in_specs=[a_spec, b_spec, *extra_specs],
        out_specs=out_specs,
        out_shape=out_shape,
        scratch_shapes=[pltpu.VMEM(acc_shape, F32)] if nk > 1 else [],
        compiler_params=_params(*sem),
    )(a, b, *extras)


def _fit(n, tile):
    if n <= tile:
        return n
    t = tile - tile % LANE
    while n % t:
        t -= LANE
    return t


def _mm_nn(name, a, b, out_dtype, tm, tn, after=()):
    m, k = a.shape
    n = b.shape[1]
    tm, tn = _fit(m, tm), _fit(n, tn)
    return _matmul(name, a, b, grid=(m // tm, n // tn), after=after,
                   a_spec=pl.BlockSpec((tm, k), lambda i, j: (i, 0)),
                   b_spec=pl.BlockSpec((k, tn), lambda i, j: (0, j)),
                   out_shape=jax.ShapeDtypeStruct((m, n), out_dtype),
                   out_specs=pl.BlockSpec((tm, tn), lambda i, j: (i, j)), contract=NN)


def _mm_nt(name, a, b, out_dtype, tm, tn, after=()):
    m, k = a.shape
    n = b.shape[0]
    tm, tn = _fit(m, tm), _fit(n, tn)
    return _matmul(name, a, b, grid=(m // tm, n // tn), after=after,
                   a_spec=pl.BlockSpec((tm, k), lambda i, j: (i, 0)),
                   b_spec=pl.BlockSpec((tn, k), lambda i, j: (j, 0)),
                   out_shape=jax.ShapeDtypeStruct((m, n), out_dtype),
                   out_specs=pl.BlockSpec((tm, tn), lambda i, j: (i, j)), contract=NT)


def _mm_tn(name, a, b, out_dtype, tm, tn):
    s, m = a.shape
    n = b.shape[1]
    tm, tn = _fit(m, tm), _fit(n, tn)
    return _matmul(name, a, b, grid=(m // tm, n // tn),
                   a_spec=pl.BlockSpec((s, tm), lambda i, j: (0, i)),
                   b_spec=pl.BlockSpec((s, tn), lambda i, j: (0, j)),
                   out_shape=jax.ShapeDtypeStruct((m, n), out_dtype),
                   out_specs=pl.BlockSpec((tm, tn), lambda i, j: (i, j)), contract=TN)


ROWS = 256


def _row_spec(rows, width):
    return pl.BlockSpec((rows, width), lambda i: (i, 0))


def _fixed_spec(rows, width):
    return pl.BlockSpec((rows, width), lambda i: (0, 0))


def _column_pieces(rows, start, width):
    piece = math.gcd(start, width)
    assert piece % LANE == 0
    return [pl.BlockSpec((rows, piece), lambda i, b=start // piece + p: (i, b)) for p in range(width // piece)]


def _rms_fwd(name, x, g, cols=None, after=()):
    s = x.shape[0]
    start, w = cols or (0, x.shape[1])
    rows = min(ROWS, s)
    pieces = _column_pieces(rows, start, w) if cols else [_row_spec(rows, w)]
    n = len(pieces)

    def body(*refs):
        g_ref, o_ref = refs[n:]
        xv = refs[0][...] if n == 1 else jnp.concatenate([r[...] for r in refs[:n]], axis=1)
        o_ref[...] = (xv * _rstd(xv) * g_ref[...]).astype(o_ref.dtype)

    return _call(
        body, name=name, grid=(s // rows,), after=after,
        in_specs=[*pieces, _fixed_spec(1, w)],
        out_specs=_row_spec(rows, w),
        out_shape=jax.ShapeDtypeStruct((s, w), BF16),
        compiler_params=_params("parallel"),
    )(*[x] * n, g)


def _rms_bwd_call(name, x, g, dy, out_dtype, cols=None, after=()):
    s = x.shape[0]
    start, w = cols or (0, x.shape[1])
    rows = min(ROWS, s)
    pieces = _column_pieces(rows, start, w) if cols else [_row_spec(rows, w)]
    n = len(pieces)

    def body(*refs):
        g_ref, dy_ref, dx_ref, dg_ref = refs[n:]
        xv = refs[0][...] if n == 1 else jnp.concatenate([r[...] for r in refs[:n]], axis=1)
        dx, dgc = _rms_bwd(xv, g_ref[...], dy_ref[...].astype(F32))
        dx_ref[...] = dx.astype(dx_ref.dtype)
        _accumulate(dg_ref, _sublane_sum(dgc), pl.program_id(0))

    return _call(
        body, name=name, grid=(s // rows,), after=after,
        in_specs=[*pieces, _fixed_spec(1, w), _row_spec(rows, w)],
        out_specs=[_row_spec(rows, w), _fixed_spec(SUBLANE, w)],
        out_shape=[jax.ShapeDtypeStruct((s, w), out_dtype), jax.ShapeDtypeStruct((SUBLANE, w), F32)],
        compiler_params=_params("arbitrary"),
    )(*[x] * n, g, dy)


def _mid_fwd(x, y, g_post, g_pre, after=()):
    s, w = x.shape
    rows = min(ROWS, s)

    def body(x_ref, y_ref, gp_ref, gq_ref, x2_ref, h2_ref):
        yv = y_ref[...]
        x2 = x_ref[...] + yv * _rstd(yv) * gp_ref[...]
        x2_ref[...] = x2
        h2_ref[...] = (x2 * _rstd(x2) * gq_ref[...]).astype(h2_ref.dtype)

    return _call(
        body, name="mid_fwd", grid=(s // rows,), after=after,
        in_specs=[_row_spec(rows, w), _row_spec(rows, w), _fixed_spec(1, w), _fixed_spec(1, w)],
        out_specs=[_row_spec(rows, w), _row_spec(rows, w)],
        out_shape=[jax.ShapeDtypeStruct((s, w), F32), jax.ShapeDtypeStruct((s, w), BF16)],
        compiler_params=_params("parallel"),
    )(x, y, g_post, g_pre)


def _head(m, x2, tgt, g):
    s, w = m.shape
    rows = min(ROWS, s)

    def body(m_ref, x2_ref, t_ref, g_ref, dout_ref, dm_ref, dg_ref, loss_ref):
        mv = m_ref[...]
        gv = g_ref[...]
        out = x2_ref[...] + mv * _rstd(mv) * gv
        err = out - t_ref[...]
        dout = err * (1.0 / w)
        dout_ref[...] = dout
        dm, dgc = _rms_bwd(mv, gv, dout)
        dm_ref[...] = dm.astype(dm_ref.dtype)
        sq = err * err
        lanes = sq[:, 0:LANE]
        for j in range(1, w // LANE):
            lanes = lanes + sq[:, j * LANE:(j + 1) * LANE]
        step = pl.program_id(0)
        _accumulate(dg_ref, _sublane_sum(dgc), step)
        _accumulate(loss_ref, _sublane_sum(lanes) * (0.5 / w), step)

    return pl.pallas_call(
        body, name="head", grid=(s // rows,),
        in_specs=[_row_spec(rows, w), _row_spec(rows, w), _row_spec(rows, w), _fixed_spec(1, w)],
        out_specs=[_row_spec(rows, w), _row_spec(rows, w), _fixed_spec(SUBLANE, w), _fixed_spec(SUBLANE, LANE)],
        out_shape=[jax.ShapeDtypeStruct((s, w), F32), jax.ShapeDtypeStruct((s, w), BF16),
                   jax.ShapeDtypeStruct((SUBLANE, w), F32), jax.ShapeDtypeStruct((SUBLANE, LANE), F32)],
        compiler_params=_params("arbitrary"),
    )(m, x2, tgt, g)


def _mid_bwd(x2, y, d_out, d_h2, g_pre, g_post, after=()):
    s, w = x2.shape
    rows = min(ROWS, s)

    def body(x2_ref, y_ref, dout_ref, dh2_ref, gq_ref, gp_ref, dx2_ref, dy_ref, dgq_ref, dgp_ref):
        dx, dgq = _rms_bwd(x2_ref[...], gq_ref[...], dh2_ref[...])
        dx2 = dout_ref[...] + dx
        dx2_ref[...] = dx2
        dy, dgp = _rms_bwd(y_ref[...], gp_ref[...], dx2)
        dy_ref[...] = dy.astype(dy_ref.dtype)
        step = pl.program_id(0)
        _accumulate(dgq_ref, _sublane_sum(dgq), step)
        _accumulate(dgp_ref, _sublane_sum(dgp), step)

    return _call(
        body, name="mid_bwd", grid=(s // rows,), after=after,
        in_specs=[_row_spec(rows, w)] * 4 + [_fixed_spec(1, w)] * 2,
        out_specs=[_row_spec(rows, w), _row_spec(rows, w), _fixed_spec(SUBLANE, w), _fixed_spec(SUBLANE, w)],
        out_shape=[jax.ShapeDtypeStruct((s, w), F32), jax.ShapeDtypeStruct((s, w), BF16),
                   jax.ShapeDtypeStruct((SUBLANE, w), F32), jax.ShapeDtypeStruct((SUBLANE, w), F32)],
        compiler_params=_params("arbitrary"),
    )(x2, y, d_out, d_h2, g_pre, g_post)


def _first_bwd(x, g, d_h1, d_x2, after=()):
    s, w = x.shape
    rows = min(ROWS, s)

    def body(x_ref, g_ref, dh_ref, dx2_ref, dx_ref, dg_ref):
        dx, dgc = _rms_bwd(x_ref[...], g_ref[...], dh_ref[...])
        dx_ref[...] = dx2_ref[...] + dx
        _accumulate(dg_ref, _sublane_sum(dgc), pl.program_id(0))

    return _call(
        body, name="first_bwd", grid=(s // rows,), after=after,
        in_specs=[_row_spec(rows, w), _fixed_spec(1, w), _row_spec(rows, w), _row_spec(rows, w)],
        out_specs=[_row_spec(rows, w), _fixed_spec(SUBLANE, w)],
        out_shape=[jax.ShapeDtypeStruct((s, w), F32), jax.ShapeDtypeStruct((SUBLANE, w), F32)],
        compiler_params=_params("arbitrary"),
    )(x, g, d_h1, d_x2)


def _shift_down(v, k):
    t = lax.broadcasted_iota(jnp.int32, v.shape, 0)
    return jnp.where(t >= k, pltpu.roll(v, k, 0), 0.0)


def _shift_up(v, k):
    n = v.shape[0]
    t = lax.broadcasted_iota(jnp.int32, v.shape, 0)
    return jnp.where(t < n - k, pltpu.roll(v, n - k, 0), 0.0)


def _conv_core(u, b, c, w):
    z = c * u
    conv = w[0:1, :] * _shift_down(z, 2) + w[1:2, :] * _shift_down(z, 1) + w[2:3, :] * z
    return z, conv, b * conv


def _conv_fwd(proj, conv_w, g, n_groups, out_width):
    s = proj.shape[0]

    def body(u_ref, b_ref, c_ref, w_ref, g_ref, o_ref):
        _, _, yr = _conv_core(u_ref[...], b_ref[...], c_ref[...], w_ref[...])
        o_ref[...] = (yr * _rstd(yr) * g_ref[...]).astype(o_ref.dtype)

    col = lambda k: pl.BlockSpec((s, HEAD), lambda i: (0, k * n_groups + i))
    return pl.pallas_call(
        body, name="conv_fwd", grid=(n_groups,),
        in_specs=[col(0), col(1), col(2), pl.BlockSpec((3, HEAD), lambda i: (0, i)), pl.BlockSpec((1, HEAD), lambda i: (0, i))],
        out_specs=pl.BlockSpec((s, HEAD), lambda i: (0, i)),
        out_shape=jax.ShapeDtypeStruct((s, out_width), BF16),
        compiler_params=_params("parallel"),
    )(proj, proj, proj, conv_w, g)


def _conv_bwd(proj, d_mix, conv_w, g, n_groups):
    s = proj.shape[0]
    width = n_groups * HEAD

    def body(u_ref, b_ref, c_ref, dy_ref, w_ref, g_ref, du_ref, db_ref, dc_ref, dg_ref, dw_ref):
        u, b, c, w = u_ref[...], b_ref[...], c_ref[...], w_ref[...]
        z, conv, yr = _conv_core(u, b, c, w)
        dyr, dgc = _rms_bwd(yr, g_ref[...], dy_ref[...])
        dconv = dyr * b
        db_ref[...] = (dyr * conv).astype(db_ref.dtype)
        dz = w[2:3, :] * dconv + w[1:2, :] * _shift_up(dconv, 1) + w[0:1, :] * _shift_up(dconv, 2)
        dc_ref[...] = (dz * u).astype(dc_ref.dtype)
        du_ref[...] = (dz * c).astype(du_ref.dtype)
        dg_ref[...] = _sublane_sum(dgc)
        dw_ref[0] = _sublane_sum(dconv * _shift_down(z, 2))
        dw_ref[1] = _sublane_sum(dconv * _shift_down(z, 1))
        dw_ref[2] = _sublane_sum(dconv * z)

    col = lambda k: pl.BlockSpec((s, HEAD), lambda i: (0, k * n_groups + i))
    grp = pl.BlockSpec((s, HEAD), lambda i: (0, i))
    return pl.pallas_call(
        body, name="conv_bwd", grid=(n_groups,),
        in_specs=[col(0), col(1), col(2), grp, pl.BlockSpec((3, HEAD), lambda i: (0, i)), pl.BlockSpec((1, HEAD), lambda i: (0, i))],
        out_specs=[grp, grp, grp, pl.BlockSpec((SUBLANE, HEAD), lambda i: (0, i)),
                   pl.BlockSpec((3, SUBLANE, HEAD), lambda i: (0, 0, i))],
        out_shape=[jax.ShapeDtypeStruct((s, width), BF16)] * 3
        + [jax.ShapeDtypeStruct((SUBLANE, width), F32), jax.ShapeDtypeStruct((3, SUBLANE, width), F32)],
        compiler_params=_params("parallel"),
    )(proj, proj, proj, d_mix, conv_w, g)


def _rope_tables(s, n_heads):
    pos = jnp.arange(s, dtype=F32)
    inv_freq = jnp.power(ROPE_THETA, -jnp.arange(0, ROPE, 2, dtype=F32) / ROPE)
    ang = pos[:, None] * inv_freq[None, :]
    cos, sin = jnp.cos(ang), jnp.sin(ang)
    cs = jnp.concatenate([cos, cos], axis=1)
    sn = jnp.concatenate([-sin, sin], axis=1)
    pad = jnp.zeros((s, LANE - ROPE), F32)
    return (jnp.tile(cs, (1, n_heads)), jnp.tile(sn, (1, n_heads)),
            jnp.concatenate([cs, pad], axis=1), jnp.concatenate([sn, pad], axis=1))


def _swap_halves(v):
    w = v.shape[1]
    lane = lax.broadcasted_iota(jnp.int32, v.shape, 1)
    first = (lane % ROPE) < (ROPE // 2)
    return jnp.where(first, pltpu.roll(v, w - ROPE // 2, 1), pltpu.roll(v, ROPE // 2, 1))


def _pack_heads(q, kv, proj, kr_col, tables, n_heads, after=()):
    s = q.shape[0]
    rows = min(ROWS, s)
    cq, sq, ck, sk = tables
    wq = n_heads * ROPE

    def body(q_ref, kv_ref, kr_ref, cq_ref, sq_ref, ck_ref, sk_ref, qo_ref, ko_ref, vo_ref):
        qr = q_ref[:, n_heads * HEAD:]
        qr = qr * cq_ref[...] + _swap_halves(qr) * sq_ref[...]
        krv = kr_ref[...]
        krv = krv * ck_ref[...] + _swap_halves(krv) * sk_ref[...]
        for h in range(n_heads):
            qo_ref[h] = jnp.concatenate([q_ref[:, h * HEAD:(h + 1) * HEAD], qr[:, h * ROPE:(h + 1) * ROPE]], axis=1).astype(BF16)
            ko_ref[h] = jnp.concatenate([kv_ref[:, 2 * h * HEAD:(2 * h + 1) * HEAD], krv[:, :ROPE]], axis=1).astype(BF16)
            vo_ref[h] = kv_ref[:, (2 * h + 1) * HEAD:(2 * h + 2) * HEAD].astype(BF16)

    hs = lambda w: pl.BlockSpec((n_heads, rows, w), lambda i: (0, i, 0))
    return _call(
        body, name="pack_heads", grid=(s // rows,), after=after,
        in_specs=[_row_spec(rows, q.shape[1]), _row_spec(rows, kv.shape[1]), pl.BlockSpec((rows, LANE), lambda i: (i, kr_col // LANE)),
                  _row_spec(rows, wq), _row_spec(rows, wq), _row_spec(rows, LANE), _row_spec(rows, LANE)],
        out_specs=[hs(QK), hs(QK), hs(HEAD)],
        out_shape=[jax.ShapeDtypeStruct((n_heads, s, QK), BF16), jax.ShapeDtypeStruct((n_heads, s, QK), BF16),
                   jax.ShapeDtypeStruct((n_heads, s, HEAD), BF16)],
        compiler_params=_params("parallel"),
    )(q, kv, proj, cq, sq, ck, sk)


def _unpack_heads(dq, dk, dv, tables, n_heads):
    s = dq.shape[1]
    rows = min(ROWS, s)
    cq, sq, ck, sk = tables
    wq = n_heads * ROPE

    def body(dq_ref, dk_ref, dv_ref, cq_ref, sq_ref, ck_ref, sk_ref, qo_ref, kvo_ref, kro_ref):
        dqr = jnp.concatenate([dq_ref[h][:, HEAD:] for h in range(n_heads)], axis=1)
        dqr = dqr * cq_ref[...] - _swap_halves(dqr) * sq_ref[...]
        dkr = dk_ref[0][:, HEAD:]
        for h in range(1, n_heads):
            dkr = dkr + dk_ref[h][:, HEAD:]
        dkr = jnp.concatenate([dkr, jnp.zeros((rows, LANE - ROPE), F32)], axis=1)
        dkr = dkr * ck_ref[...] - _swap_halves(dkr) * sk_ref[...]
        kro_ref[...] = dkr.astype(kro_ref.dtype)
        qo_ref[:, n_heads * HEAD:] = dqr.astype(qo_ref.dtype)
        for h in range(n_heads):
            qo_ref[:, h * HEAD:(h + 1) * HEAD] = dq_ref[h][:, :HEAD].astype(qo_ref.dtype)
            kvo_ref[:, 2 * h * HEAD:(2 * h + 1) * HEAD] = dk_ref[h][:, :HEAD].astype(kvo_ref.dtype)
            kvo_ref[:, (2 * h + 1) * HEAD:(2 * h + 2) * HEAD] = dv_ref[h].astype(kvo_ref.dtype)

    hs = lambda w: pl.BlockSpec((n_heads, rows, w), lambda i: (0, i, 0))
    return pl.pallas_call(
        body, name="unpack_heads", grid=(s // rows,),
        in_specs=[hs(QK), hs(QK), hs(HEAD), _row_spec(rows, wq), _row_spec(rows, wq), _row_spec(rows, LANE), _row_spec(rows, LANE)],
        out_specs=[_row_spec(rows, n_heads * QK), _row_spec(rows, 2 * n_heads * HEAD), _row_spec(rows, LANE)],
        out_shape=[jax.ShapeDtypeStruct((s, n_heads * QK), BF16), jax.ShapeDtypeStruct((s, 2 * n_heads * HEAD), BF16),
                   jax.ShapeDtypeStruct((s, LANE), BF16)],
        compiler_params=_params("parallel"),
    )(dq, dk, dv, cq, sq, ck, sk)


TQ = 256


LOG2_E = 1.4426950408889634


def _softmax_parts(q, k):
    tq, n_keys = q.shape[0], k.shape[0]
    sc = lax.dot_general(q, k, (NT, ((), ())), preferred_element_type=F32) * (QK ** -0.5 * LOG2_E)
    row = lax.broadcasted_iota(jnp.int32, (tq, tq), 0)
    col = lax.broadcasted_iota(jnp.int32, (tq, tq), 1)
    own = jnp.where(col // CHUNK <= row // CHUNK, sc[:, n_keys - tq:], NEG_INF)
    sc = own if n_keys == tq else jnp.concatenate([sc[:, :n_keys - tq], own], axis=1)
    e = jnp.exp2(sc - jnp.max(sc, axis=-1, keepdims=True))
    return e, 1.0 / jnp.sum(e, axis=-1, keepdims=True)


def _attn_fwd(q, k, v, g, mix, col0):
    n_heads, s, _ = q.shape
    tq = min(TQ, s)
    assert tq % CHUNK == 0 and s % tq == 0

    def body(q_ref, k_ref, v_ref, g_ref, mix_ref, o_ref, y_ref):
        for c in range(s // tq):
            rows, n_keys = pl.ds(c * tq, tq), (c + 1) * tq
            e, inv = _softmax_parts(q_ref[rows, :], k_ref[0:n_keys, :])
            o = jnp.dot(e.astype(BF16), v_ref[0:n_keys, :], preferred_element_type=F32) * inv
            o_ref[rows, :] = o
            y_ref[rows, :] = (o * _rstd(o) * g_ref[...]).astype(y_ref.dtype)

    head = lambda w: pl.BlockSpec((None, s, w), lambda h: (h, 0, 0))
    return pl.pallas_call(
        body, name="attn_fwd", grid=(n_heads,),
        in_specs=[head(QK), head(QK), head(HEAD), pl.BlockSpec((1, HEAD), lambda h: (0, h)), ANY],
        out_specs=[head(HEAD), pl.BlockSpec((s, HEAD), lambda h: (0, col0 // HEAD + h))],
        out_shape=[jax.ShapeDtypeStruct((n_heads, s, HEAD), F32), jax.ShapeDtypeStruct(mix.shape, mix.dtype)],
        input_output_aliases={4: 1},
        compiler_params=_params("parallel"),
    )(q, k, v, g, mix)


def _attn_bwd(q, k, v, o, d_mix, g, col0, after=()):
    n_heads, s, _ = q.shape
    tq = min(TQ, s)

    def body(q_ref, k_ref, v_ref, o_ref, dy_ref, g_ref, dq_ref, dk_ref, dv_ref, dg_ref):
        dg = None
        for c in reversed(range(s // tq)):
            rows, n_keys = pl.ds(c * tq, tq), (c + 1) * tq
            qv, kv_, vv = q_ref[rows, :], k_ref[0:n_keys, :], v_ref[0:n_keys, :]
            do, dgc = _rms_bwd(o_ref[rows, :], g_ref[...], dy_ref[rows, :])
            do = do.astype(BF16)
            dg = _sublane_sum(dgc) if dg is None else dg + _sublane_sum(dgc)
            e, inv = _softmax_parts(qv, kv_)
            p = e * inv
            dp = lax.dot_general(do, vv, (NT, ((), ())), preferred_element_type=F32)
            ds = (p * (dp - jnp.sum(p * dp, axis=-1, keepdims=True)) * (QK ** -0.5)).astype(BF16)
            dq_ref[rows, :] = jnp.dot(ds, kv_, preferred_element_type=F32)
            dk = lax.dot_general(ds, qv, (TN, ((), ())), preferred_element_type=F32)
            dv = lax.dot_general(p.astype(BF16), do, (TN, ((), ())), preferred_element_type=F32)
            if n_keys == s:
                dk_ref[...] = dk
                dv_ref[...] = dv
            else:
                dk_ref[0:n_keys, :] += dk
                dv_ref[0:n_keys, :] += dv
        dg_ref[...] = dg

    c0 = col0 // HEAD
    head = lambda w: pl.BlockSpec((None, s, w), lambda h: (h, 0, 0))
    return _call(
        body, name="attn_bwd", grid=(n_heads,), after=after,
        in_specs=[head(QK), head(QK), head(HEAD), head(HEAD), pl.BlockSpec((s, HEAD), lambda h: (0, c0 + h)),
                  pl.BlockSpec((1, HEAD), lambda h: (0, h))],
        out_specs=[head(QK), head(QK), head(HEAD), pl.BlockSpec((SUBLANE, HEAD), lambda h: (0, h))],
        out_shape=[jax.ShapeDtypeStruct((n_heads, s, QK), F32), jax.ShapeDtypeStruct((n_heads, s, QK), F32),
                   jax.ShapeDtypeStruct((n_heads, s, HEAD), F32), jax.ShapeDtypeStruct((SUBLANE, n_heads * HEAD), F32)],
        compiler_params=_params("parallel"),
    )(q, k, v, o, d_mix, g)


TILE_M = 1024
TILE_N = 1024


def _up_fwd(h2, w_up):
    s, d = h2.shape
    nb, _, fb = w_up.shape
    tm = min(TILE_M,s)

    def epilogue(acc):
        r = jnp.maximum(acc, 0.0)
        return r * r, r

    blk = pl.BlockSpec((tm, fb), lambda i, j: (i, j))
    return _matmul("up_fwd", h2, w_up, grid=(s // tm, nb),
                   a_spec=pl.BlockSpec((tm, d), lambda i, j: (i, 0)),
                   b_spec=pl.BlockSpec((None, d, fb), lambda i, j: (j, 0, 0)),
                   out_shape=[jax.ShapeDtypeStruct((s, nb * fb), BF16)] * 2, out_specs=[blk, blk],
                   contract=NN, epilogue=epilogue)


def _down_fwd(a, w_down):
    s, f = a.shape
    d = w_down.shape[1]
    tm, tn, tk = min(TILE_M,s), min(TILE_N,d), 2048
    nk = f // tk
    return _matmul("down_fwd", a, w_down, grid=(s // tm, d // tn, nk),
                   a_spec=pl.BlockSpec((tm, tk), lambda i, j, k: (i, k)),
                   b_spec=pl.BlockSpec((tk, tn), lambda i, j, k: (k, j)),
                   out_shape=jax.ShapeDtypeStruct((s, d), F32),
                   out_specs=pl.BlockSpec((tm, tn), lambda i, j, k: (i, j)),
                   contract=NN, nk=nk, acc_shape=(tm, tn))


def _down_bwd_act(d_m, w_down, r, after=()):
    s, d = d_m.shape
    f = w_down.shape[0]
    tm, tn = min(TILE_M,s), min(TILE_N,f)
    blk = pl.BlockSpec((tm, tn), lambda i, j: (i, j))
    return _matmul("down_bwd_act", d_m, w_down, grid=(s // tm, f // tn), after=after,
                   a_spec=pl.BlockSpec((tm, d), lambda i, j: (i, 0)),
                   b_spec=pl.BlockSpec((tn, d), lambda i, j: (j, 0)),
                   out_shape=jax.ShapeDtypeStruct((s, f), BF16), out_specs=blk, contract=NT,
                   extras=(r,), extra_specs=(blk,),
                   epilogue=lambda acc, rv: (acc * (2.0 * rv.astype(F32)),))


def _up_bwd_act(d_up, w_up, after=()):
    s, _ = d_up.shape
    nb, d, fb = w_up.shape
    tm, tn = min(2 * TILE_M, s), min(TILE_N,d)
    return _matmul("up_bwd_act", d_up, w_up, grid=(s // tm, d // tn, nb), after=after,
                   a_spec=pl.BlockSpec((tm, fb), lambda i, j, k: (i, k)),
                   b_spec=pl.BlockSpec((None, tn, fb), lambda i, j, k: (k, j, 0)),
                   out_shape=jax.ShapeDtypeStruct((s, d), F32),
                   out_specs=pl.BlockSpec((tm, tn), lambda i, j, k: (i, j)),
                   contract=NT, nk=nb, acc_shape=(tm, tn))


def _half_grad(name, a, b, core, home, received, after, *, grid, a_block, a_map, b_block, b_map, o_block, o_map, out_shape):
    n_after = len(after)
    pick = (lambda ref: ref[0]) if home else (lambda ref: 1 - ref[0])

    def body(core_ref, a_ref, b_ref, *rest):
        acc = lax.dot_general(a_ref[...], b_ref[...], (TN, ((), ())), preferred_element_type=F32)
        if received is not None:
            acc = acc + rest[0][...].astype(F32)
        rest[-1][...] = acc.astype(rest[-1].dtype)

    wrap = lambda fn: (lambda i, j, core_ref: fn(i, j, pick(core_ref)))
    o_spec = pl.BlockSpec(o_block, wrap(o_map))
    extra = [] if received is None else [o_spec]
    operands = [] if received is None else [received]
    return pl.pallas_call(
        body, name=name,
        grid_spec=pltpu.PrefetchScalarGridSpec(
            num_scalar_prefetch=1, grid=grid,
            in_specs=[pl.BlockSpec(a_block, wrap(a_map)), pl.BlockSpec(b_block, wrap(b_map))] + extra + [ANY] * n_after,
            out_specs=o_spec),
        out_shape=out_shape,
        compiler_params=_params("parallel", "parallel"),
    )(core, a, b, *operands, *after)


def _down_half_grad(name, a, d_m, core, home, received=None, after=()):
    s, f = a.shape
    d = d_m.shape[1]
    r = f // N_DEV
    tn = min(TILE_N, d)
    return _half_grad(name, a, d_m, core, home, received, after, grid=(N_CHIP, d // tn),
                      a_block=(s, r), a_map=lambda k, j, p: (0, 2 * k + p),
                      b_block=(s, tn), b_map=lambda k, j, p: (0, j),
                      o_block=(None, r, tn), o_map=lambda k, j, p: (k, 0, j),
                      out_shape=jax.ShapeDtypeStruct((N_CHIP, r, d), BF16))


def _up_half_grad(name, h2, d_up, core, home, received=None, after=()):
    s, d = h2.shape
    fb = d_up.shape[1] // N_DEV
    tm = min(TILE_M, d)
    return _half_grad(name, h2, d_up, core, home, received, after, grid=(d // tm, N_CHIP),
                      a_block=(s, tm), a_map=lambda i, k, p: (0, i),
                      b_block=(s, fb), b_map=lambda i, k, p: (0, 2 * k + p),
                      o_block=(None, tm, fb), o_map=lambda i, k, p: (k, i, 0),
                      out_shape=jax.ShapeDtypeStruct((N_CHIP, d, fb), BF16))


def _in_pad(in_width):
    return -(-in_width // LANE) * LANE


def _join_col_shards(blocks):
    n, r, w = blocks.shape
    rows = min(ROWS, r)
    width = _in_pad(n * w)

    def body(x_ref, o_ref):
        tail = [jnp.zeros((rows, width - n * w), o_ref.dtype)] if width > n * w else []
        o_ref[...] = jnp.concatenate([x_ref[j] for j in range(n)] + tail, axis=1)

    return pl.pallas_call(
        body, name="join_col_shards", grid=(r // rows,),
        in_specs=[pl.BlockSpec((n, rows, w), lambda i: (0, i, 0))], out_specs=_row_spec(rows, width),
        out_shape=jax.ShapeDtypeStruct((r, width), blocks.dtype),
        compiler_params=_params("parallel"),
    )(blocks)


def _permute_q_cols(w_uq, n_heads):
    r = w_uq.shape[0]
    w3 = w_uq.reshape(r, n_heads, QK)
    return jnp.concatenate([w3[:, :, :HEAD].reshape(r, n_heads * HEAD), w3[:, :, HEAD:].reshape(r, n_heads * ROPE)], axis=1)


def _unpermute_q_rows(wt, n_heads):
    r = wt.shape[1]
    nope = wt[:n_heads * HEAD].reshape(n_heads, HEAD, r)
    rope = wt[n_heads * HEAD:].reshape(n_heads, ROPE, r)
    return jnp.concatenate([nope, rope], axis=1).reshape(n_heads * QK, r)


def _local_step(x, tgt, gains, weights, grads, first_after=()):
    pre_mix_g, q_norm_g, kv_norm_g, conv_out_g, attn_out_g, post_mix_g, pre_mlp_g, post_mlp_g = gains
    s, d = x.shape
    conv_width = conv_out_g.shape[1]
    n_groups = conv_width // HEAD
    r_q, r_kv = q_norm_g.shape[1], kv_norm_g.shape[1]
    n_heads = attn_out_g.shape[1] // HEAD
    c_q0 = 3 * conv_width
    c_kv0 = c_q0 + r_q
    c_kr0 = c_kv0 + r_kv
    in_pad = _in_pad(c_kr0 + ROPE)
    tn_in = in_pad // 5 if in_pad % (5 * LANE) == 0 else LANE
    tables = _rope_tables(s, n_heads)

    h1 = _rms_fwd("pre_mix_norm", x, pre_mix_g, after=first_after)
    weights.forward(0, (h1, *tables))
    weights.relay(0, ())
    w_in_p, conv_w = weights.ready(0, ())
    proj = _mm_nn("in_proj", h1, w_in_p, F32, TILE_M, tn_in)
    y_conv = _conv_fwd(proj, conv_w, conv_out_g, n_groups, conv_width + n_heads * HEAD)
    qn = _rms_fwd("q_norm", proj, q_norm_g, cols=(c_q0, r_q))
    kvn = _rms_fwd("kv_norm", proj, kv_norm_g, cols=(c_kv0, r_kv))
    weights.forward(1, (y_conv, qn, kvn))
    w_uq_p, w_ukv, w_o = weights.ready(1, ())
    q = _mm_nn("q_up", qn, w_uq_p, F32, TILE_M, TILE_N)
    kv = _mm_nn("kv_up", kvn, w_ukv, F32, TILE_M, TILE_N)
    qh, kh, vh = _pack_heads(q, kv, proj, c_kr0, tables, n_heads, after=weights.forward(2, (q, kv)))
    o, mix = _attn_fwd(qh, kh, vh, attn_out_g, y_conv, conv_width)
    y = _mm_nn("out_proj", mix, w_o, F32, TILE_M, TILE_N, after=weights.forward(3, (mix,)))
    x2, h2 = _mid_fwd(x, y, post_mix_g, pre_mlp_g)
    weights.relay(2, (h2,))
    (w_up,) = weights.ready(2, ())
    a, r = _up_fwd(h2, w_up)
    weights.relay(3, (a,))
    (w_down,) = weights.ready(3, ())
    m = _down_fwd(a, w_down)

    d_out, d_m, dg_post_mlp, loss_part = _head(m, x2, tgt, post_mlp_g)
    core = grads.core
    away = _down_half_grad("down_bwd_w_away", a, d_m, core, home=False)
    d_up = _down_bwd_act(d_m, w_down, r, after=grads.send_away(0, away))
    sums = _down_half_grad("down_bwd_w_home", a, d_m, core, home=True, received=grads.received(0, (d_up,)))
    away = _up_half_grad("up_bwd_w_away", h2, d_up, core, home=False, after=grads.send_sums(0, (sums,)))
    d_h2 = _up_bwd_act(d_up, w_up, after=grads.send_away(1, away))
    sums = _up_half_grad("up_bwd_w_home", h2, d_up, core, home=True, received=grads.received(1, (d_h2,)))
    d_x2, d_y, dg_pre_mlp, dg_post_mix = _mid_bwd(x2, y, d_out, d_h2, pre_mlp_g, post_mix_g, after=grads.send_sums(1, (sums,)))
    d_mix = _mm_nt("out_proj_bwd_act", d_y, w_o, F32, TILE_M, TILE_N)
    gw_o = _mm_tn("out_proj_bwd_w", mix, d_y, BF16, TILE_M, TILE_N)
    dqh, dkh, dvh, dg_attn = _attn_bwd(qh, kh, vh, o, d_mix, attn_out_g, conv_width, after=grads.full(2, (gw_o,)))
    d_q, d_kv, d_kr = _unpack_heads(dqh, dkh, dvh, tables, n_heads)
    d_qn = _mm_nt("q_up_bwd_act", d_q, w_uq_p, F32, TILE_M, TILE_N)
    d_kvn = _mm_nt("kv_up_bwd_act", d_kv, w_ukv, F32, TILE_M, TILE_N)
    gw_uq_t = _mm_tn("q_up_bwd_w", d_q, qn, F32, TILE_M, TILE_N)
    gw_ukv = _mm_tn("kv_up_bwd_w", kvn, d_kv, BF16, TILE_M, TILE_N)
    d_cq, dg_q = _rms_bwd_call("q_norm_bwd", proj, q_norm_g, d_qn, BF16, cols=(c_q0, r_q), after=grads.full(3, (gw_uq_t, gw_ukv)))
    d_ckv, dg_kv = _rms_bwd_call("kv_norm_bwd", proj, kv_norm_g, d_kvn, BF16, cols=(c_kv0, r_kv))
    d_u, d_b, d_c, dg_conv, dw_conv = _conv_bwd(proj, d_mix, conv_w, conv_out_g, n_groups)
    d_proj = jnp.concatenate([d_u, d_b, d_c, d_cq, d_ckv, d_kr[:, :in_pad - c_kr0]], axis=1)
    gw_in_t = _mm_tn("in_proj_bwd_w", d_proj, h1, F32, tn_in, TILE_N)
    d_h1 = _mm_nt("in_proj_bwd_act", d_proj, w_in_p, F32, TILE_M, 512, after=grads.send_away(4, gw_in_t))
    grad_x, dg_pre_mix = _first_bwd(x, pre_mix_g, d_h1, d_x2, after=grads.full(4, (gw_in_t,), received=(d_h1,)))

    small = [dg_pre_mix, dg_q, dg_kv, dg_conv, dg_attn, dg_post_mix, dg_pre_mlp, dg_post_mlp,
             dw_conv[0], dw_conv[1], dw_conv[2], loss_part]
    return grad_x, jnp.concatenate(small, axis=1)


HBM = pl.BlockSpec(memory_space=pltpu.HBM)
SEM = pl.BlockSpec(memory_space=pltpu.SEMAPHORE)
IN_VMEM = pl.BlockSpec(memory_space=pltpu.VMEM)
SPLIT = pltpu.CompilerParams(has_side_effects=pltpu.SideEffectType.DATAFLOW_SIDE_EFFECTING)


def _in_hbm(a):
    return pltpu.with_memory_space_constraint(a, pltpu.HBM)


def _hbm_like(a):
    return pltpu.HBM(a.shape, a.dtype)


def _place():
    x, y, c = lax.axis_index("x"), lax.axis_index("y"), lax.axis_index("c")
    other_chips = [(1 - x, y), (x, 1 - y), (1 - x, 1 - y)]
    return x, y, c, other_chips


def _block(px, py, pc):
    return 4 * px + 2 * py + pc


def _await(block, sem):
    pltpu.make_async_copy(block, block, sem).wait()


def _relay_route(x, y, c):
    came_from = ((1 - x) * (1 - c) + x * c, y * (1 - c) + (1 - y) * c)
    goes_to = (x * (1 - c) + (1 - x) * c, (1 - y) * (1 - c) + y * c)
    return came_from, goes_to


def _gather_start(name, shards, groups, relayed=(), after=()):
    n, ng = len(shards), len(groups)
    lands = [lax.empty((N_DEV, *a.shape), a.dtype) for a in shards]

    def body(*refs):
        src, land = refs[:n], refs[n:2 * n]
        sems, token = refs[2 * n + len(after):2 * n + len(after) + 2 * ng], refs[-1]
        x, y, c, chips = _place()
        targets = [(x, y, 1 - c)] + [(*chip, c) for chip in chips]
        for gi, group in enumerate(groups):
            for i, w in enumerate(group):
                for k, to in enumerate(targets[:3] if gi in relayed else targets):
                    pltpu.make_async_remote_copy(
                        src_ref=src[w], dst_ref=land[w].at[_block(x, y, c)],
                        send_sem=sems[2 * gi].at[4 * i + k], recv_sem=sems[2 * gi + 1].at[4 * i + k],
                        device_id=to, device_id_type=MESH).start()
        token[...] = jnp.zeros_like(token)

    sem_shapes = [pltpu.SemaphoreType.DMA((4 * len(g),)) for g in groups for _ in range(2)]
    out = pl.pallas_call(
        body, name=name,
        in_specs=[HBM] * (2 * n) + [ANY] * len(after),
        out_specs=[SEM] * (2 * ng) + [HBM] * (2 * n) + [IN_VMEM],
        out_shape=sem_shapes + [_hbm_like(a) for a in shards] + [_hbm_like(a) for a in lands]
        + [jax.ShapeDtypeStruct((SUBLANE, LANE), F32)],
        input_output_aliases={i: 2 * ng + i for i in range(2 * n)},
        compiler_params=SPLIT,
    )(*[_in_hbm(a) for a in shards], *[_in_hbm(a) for a in lands], *after)
    sems = [(out[2 * gi], out[2 * gi + 1]) for gi in range(ng)]
    return sems, out[2 * ng:2 * ng + n], out[2 * ng + n:2 * ng + 2 * n], out[-1]


def _gather_forward(name, shards, lands, send1, recv1, after, relayed=False):
    n = len(lands)

    def body(*refs):
        src, land = refs[:n], refs[n:2 * n]
        s1, r1 = refs[2 * n], refs[2 * n + 1]
        s2, r2 = refs[2 * n + 2 + len(after)], refs[2 * n + 3 + len(after)]
        x, y, c, chips = _place()
        me, sibling = (x, y, c), (x, y, 1 - c)
        for j, chip in enumerate(chips[:2] if relayed else chips):
            for i in range(n):
                blk = land[i].at[_block(*chip, c)]
                pltpu.make_async_remote_copy(src_ref=blk, dst_ref=blk, send_sem=s1.at[4 * i + 1 + j], recv_sem=r1.at[4 * i + 1 + j],
                                             device_id=me, device_id_type=MESH).wait_recv()
                pltpu.make_async_remote_copy(src_ref=blk, dst_ref=blk, send_sem=s2.at[3 * i + j], recv_sem=r2.at[3 * i + j],
                                             device_id=sibling, device_id_type=MESH).start()
        if relayed:
            came_from, goes_to = _relay_route(x, y, c)
            for i in range(n):
                blk = land[i].at[_block(*came_from, c)]
                pltpu.make_async_remote_copy(src_ref=blk, dst_ref=blk, send_sem=s2.at[3 * i + 2], recv_sem=r2.at[3 * i + 2],
                                             device_id=(*goes_to, c), device_id_type=MESH).start()
        for i in range(n):
            blk = land[i].at[_block(x, y, 1 - c)]
            pltpu.make_async_remote_copy(src_ref=blk, dst_ref=blk, send_sem=s1.at[4 * i], recv_sem=r1.at[4 * i],
                                         device_id=me, device_id_type=MESH).wait_recv()
            for k in range(3 if relayed else 4):
                pltpu.make_async_remote_copy(src_ref=src[i], dst_ref=land[i].at[_block(x, y, c)], send_sem=s1.at[4 * i + k],
                                             recv_sem=r1.at[4 * i + k], device_id=sibling, device_id_type=MESH).wait_send()

    sem = pltpu.SemaphoreType.DMA((3 * n,))
    out = pl.pallas_call(
        body, name=name,
        in_specs=[HBM] * (2 * n) + [SEM, SEM] + [ANY] * len(after),
        out_specs=[SEM, SEM] + [HBM] * n,
        out_shape=[sem, sem] + [_hbm_like(a) for a in lands],
        input_output_aliases={n + i: 2 + i for i in range(n)},
        compiler_params=SPLIT,
    )(*shards, *lands, send1, recv1, *after)
    return (out[0], out[1]), out[2:]


def _gather_relay_forward(name, lands, send2, recv2, after):
    n = len(lands)

    def body(*refs):
        land, s2, r2 = refs[:n], refs[n], refs[n + 1]
        s3, r3 = refs[n + 2 + len(after)], refs[n + 3 + len(after)]
        x, y, c, _ = _place()
        me, sibling = (x, y, c), (x, y, 1 - c)
        came_from, _ = _relay_route(x, y, c)
        for i in range(n):
            blk = land[i].at[_block(1 - x, 1 - y, c)]
            pltpu.make_async_remote_copy(src_ref=blk, dst_ref=blk, send_sem=s2.at[3 * i + 2], recv_sem=r2.at[3 * i + 2],
                                         device_id=me, device_id_type=MESH).wait_recv()
            pltpu.make_async_remote_copy(src_ref=blk, dst_ref=blk, send_sem=s3.at[i], recv_sem=r3.at[i],
                                         device_id=sibling, device_id_type=MESH).start()
            sent = land[i].at[_block(*came_from, c)]
            pltpu.make_async_remote_copy(src_ref=sent, dst_ref=sent, send_sem=s2.at[3 * i + 2], recv_sem=r2.at[3 * i + 2],
                                         device_id=me, device_id_type=MESH).wait_send()

    sem = pltpu.SemaphoreType.DMA((n,))
    out = pl.pallas_call(
        body, name=name,
        in_specs=[HBM] * n + [SEM, SEM] + [ANY] * len(after),
        out_specs=[SEM, SEM] + [HBM] * n,
        out_shape=[sem, sem] + [_hbm_like(a) for a in lands],
        input_output_aliases={i: 2 + i for i in range(n)},
        compiler_params=SPLIT,
    )(*lands, send2, recv2, *after)
    return (out[0], out[1]), out[2:]


def _gather_wait(name, lands, send2, recv2, after, relay_sems=None):
    n = len(lands)
    n_sems = 2 if relay_sems is None else 4

    def body(*refs):
        land, s2, r2 = refs[:n], refs[n], refs[n + 1]
        for i in range(n):
            for j in range(3 if relay_sems is None else 2):
                _await(land[i].at[0], r2.at[3 * i + j])
                _await(land[i].at[0], s2.at[3 * i + j])
            if relay_sems is not None:
                _await(land[i].at[0], refs[n + 3].at[i])
                _await(land[i].at[0], refs[n + 2].at[i])

    return pl.pallas_call(
        body, name=name,
        in_specs=[HBM] * n + [SEM] * n_sems + [ANY] * len(after), out_specs=[HBM] * n, out_shape=[_hbm_like(a) for a in lands],
        input_output_aliases={i: i for i in range(n)},
        compiler_params=SPLIT,
    )(*lands, send2, recv2, *(relay_sems or ()), *after)


def _pair_exchange(name, grads, shard_rows):
    n = len(grads)
    shapes = [(g.shape[1:] if r is None else (r, g.shape[1])) for g, r in zip(grads, shard_rows)]

    def body(*refs):
        ins, recv = refs[:n], refs[n:2 * n]
        send_sems, recv_sems = refs[2 * n:]
        x, y, c, _ = _place()
        sends = []
        for w in range(n):
            for k in range(N_CHIP):
                j, r = 2 * k + 1 - c, shard_rows[w]
                src = ins[w].at[j] if r is None else ins[w].at[pl.ds(pl.multiple_of(j * r, SUBLANE), r), :]
                sends.append(pltpu.make_async_remote_copy(
                    src_ref=src, dst_ref=recv[w].at[k],
                    send_sem=send_sems.at[w, k], recv_sem=recv_sems.at[w, k],
                    device_id=(x, y, 1 - c), device_id_type=MESH))
        for cp in sends:
            cp.start()
        for cp in sends:
            cp.wait()

    return pl.pallas_call(
        body, name=name,
        in_specs=[ANY] * n, out_specs=[ANY] * n,
        out_shape=[jax.ShapeDtypeStruct((N_CHIP, *shape), g.dtype) for g, shape in zip(grads, shapes)],
        scratch_shapes=[pltpu.SemaphoreType.DMA((n, N_CHIP))] * 2,
    )(*grads)


def _pair_sum_rows(name, grad, received, core):
    _, r, c = received.shape
    tc = _fit(c, 512)

    def body(core_ref, a_ref, b_ref, o_ref):
        o_ref[...] = (a_ref[...] + b_ref[...]).astype(o_ref.dtype)

    spec = pl.BlockSpec((None, r, tc), lambda k, i, core_ref: (k, 0, i))
    return pl.pallas_call(
        body, name=name,
        grid_spec=pltpu.PrefetchScalarGridSpec(
            num_scalar_prefetch=1, grid=(N_CHIP, c // tc),
            in_specs=[pl.BlockSpec((r, tc), lambda k, i, core_ref: (2 * k + core_ref[0], i)), spec],
            out_specs=spec),
        out_shape=jax.ShapeDtypeStruct(received.shape, BF16),
        compiler_params=_params("parallel", "parallel"),
    )(core, grad, received)


def _pair_sum(name, grad, received, core):
    _, r, c = received.shape
    rows = min(ROWS, r)
    assert r % rows == 0

    def body(core_ref, a_ref, b_ref, o_ref):
        o_ref[...] = (a_ref[...].astype(F32) + b_ref[...].astype(F32)).astype(o_ref.dtype)

    spec = pl.BlockSpec((None, rows, c), lambda k, i, core_ref: (k, i, 0))
    return pl.pallas_call(
        body, name=name,
        grid_spec=pltpu.PrefetchScalarGridSpec(
            num_scalar_prefetch=1, grid=(N_CHIP, r // rows),
            in_specs=[pl.BlockSpec((None, None, rows, c), lambda k, i, core_ref: (k, core_ref[0], i, 0)), spec],
            out_specs=spec),
        out_shape=jax.ShapeDtypeStruct(received.shape, received.dtype),
        compiler_params=_params("parallel", "parallel"),
    )(core, grad.reshape(N_CHIP, 2, r, c), received)


def _away_shard(src, k, c, shard_rows):
    if shard_rows is None:
        return src.at[k]
    return src.at[pl.ds(pl.multiple_of((2 * k + 1 - c) * shard_rows, SUBLANE), shard_rows), :]


def _pair_send_start(name, away, shard_rows=None):
    shape = away.shape if shard_rows is None else (N_CHIP, shard_rows, away.shape[1])
    land = lax.empty(shape, away.dtype)

    def body(src, dst, send, recv, src_thru, dst_thru, token):
        x, y, c, _ = _place()
        for k in range(N_CHIP):
            pltpu.make_async_remote_copy(src_ref=_away_shard(src, k, c, shard_rows), dst_ref=dst.at[k], send_sem=send.at[k],
                                         recv_sem=recv.at[k], device_id=(x, y, 1 - c), device_id_type=MESH).start()
        token[...] = jnp.zeros_like(token)

    sem = pltpu.SemaphoreType.DMA((N_CHIP,))
    out = pl.pallas_call(
        body, name=name,
        in_specs=[HBM, HBM], out_specs=[SEM, SEM, HBM, HBM, IN_VMEM],
        out_shape=[sem, sem, _hbm_like(away), _hbm_like(land), jax.ShapeDtypeStruct((SUBLANE, LANE), F32)],
        input_output_aliases={0: 2, 1: 3},
        compiler_params=SPLIT,
    )(_in_hbm(away), _in_hbm(land))
    return (out[0], out[1]), out[2], out[3], out[4]


def _pair_send_wait(name, sems, src, land, after, shard_rows=None):
    def body(src_ref, dst_ref, send, recv, *rest):
        for k in range(N_CHIP):
            _await(dst_ref.at[k], send.at[k])
            _await(dst_ref.at[k], recv.at[k])

    return pl.pallas_call(
        body, name=name,
        in_specs=[HBM, HBM, SEM, SEM] + [ANY] * len(after), out_specs=HBM, out_shape=_hbm_like(land),
        input_output_aliases={1: 0},
        compiler_params=SPLIT,
    )(src, land, *sems, *after)


def _chip_send_start(name, sums):
    n = len(sums)
    lands = [lax.empty(a.shape, a.dtype) for a in sums]

    def body(*refs):
        src, land = refs[:n], refs[n:2 * n]
        send, recv, token = refs[2 * n], refs[2 * n + 1], refs[-1]
        x, y, c, chips = _place()
        for w in range(n):
            for j, (px, py) in enumerate(chips):
                pltpu.make_async_remote_copy(
                    src_ref=src[w].at[2 * px + py], dst_ref=land[w].at[2 * x + y],
                    send_sem=send.at[3 * w + j], recv_sem=recv.at[3 * w + j],
                    device_id=(px, py, c), device_id_type=MESH).start()
        token[...] = jnp.zeros_like(token)

    sem = pltpu.SemaphoreType.DMA((3 * n,))
    out = pl.pallas_call(
        body, name=name,
        in_specs=[HBM] * (2 * n),
        out_specs=[SEM, SEM] + [HBM] * (2 * n) + [IN_VMEM],
        out_shape=[sem, sem] + [_hbm_like(a) for a in sums] + [_hbm_like(a) for a in lands]
        + [jax.ShapeDtypeStruct((SUBLANE, LANE), F32)],
        input_output_aliases={i: 2 + i for i in range(2 * n)},
        compiler_params=SPLIT,
    )(*[_in_hbm(a) for a in sums], *[_in_hbm(a) for a in lands])
    return (out[0], out[1]), out[2:2 + n], out[2 + n:2 + 2 * n], out[-1]


def _chip_send_wait(name, groups, after):
    counts = [len(g[1]) for g in groups]
    n = sum(counts)

    def body(*refs):
        land = refs[n:2 * n]
        sems = refs[2 * n:2 * n + 2 * len(groups)]
        w = 0
        for gi, count in enumerate(counts):
            for i in range(count):
                for j in range(3):
                    _await(land[w].at[0], sems[2 * gi].at[3 * i + j])
                    _await(land[w].at[0], sems[2 * gi + 1].at[3 * i + j])
                w += 1

    sums = [a for g in groups for a in g[1]]
    lands = [a for g in groups for a in g[2]]
    sems = [s for g in groups for s in g[0]]
    return pl.pallas_call(
        body, name=name,
        in_specs=[HBM] * (2 * n) + [SEM] * len(sems) + [ANY] * len(after),
        out_specs=[HBM] * n, out_shape=[_hbm_like(a) for a in lands],
        input_output_aliases={n + i: i for i in range(n)},
        compiler_params=SPLIT,
    )(*sums, *lands, *sems, *after)


def _small_all_reduce(part, after=()):
    _, w = part.shape

    def body(p_ref, *rest):
        o_ref, buf, send_sems, recv_sems = rest[len(after):]
        x, y, c, _ = _place()
        me = 4 * x + 2 * y + c
        buf[me] = jnp.sum(p_ref[...], axis=0, keepdims=True)
        copies = []
        for k in range(1, N_DEV):
            dx, dy, dc = (k >> 2) & 1, (k >> 1) & 1, k & 1
            copies.append(pltpu.make_async_remote_copy(
                src_ref=buf.at[me], dst_ref=buf.at[me], send_sem=send_sems.at[k - 1], recv_sem=recv_sems.at[k - 1],
                device_id=(x ^ dx, y ^ dy, c ^ dc), device_id_type=MESH))
        for cp in copies:
            cp.start()
        for cp in copies:
            cp.wait()
        tot = buf[0]
        for d in range(1, N_DEV):
            tot = tot + buf[d]
        o_ref[...] = tot
        loss = jnp.sum(tot[:, w - LANE:], axis=1, keepdims=True)
        o_ref[:, w - LANE:] = jnp.broadcast_to(loss, (1, LANE))

    return pl.pallas_call(
        body, name="small_all_reduce",
        in_specs=[IN_VMEM] + [ANY] * len(after), out_specs=IN_VMEM,
        out_shape=jax.ShapeDtypeStruct((1, w), F32),
        scratch_shapes=[pltpu.VMEM((N_DEV, 1, w), F32), pltpu.SemaphoreType.DMA((N_DEV - 1,)), pltpu.SemaphoreType.DMA((N_DEV - 1,))],
        compiler_params=pltpu.CompilerParams(vmem_limit_bytes=VMEM_LIMIT_BYTES),
    )(part, *after)


def _adamw(w, g, m, v):
    m = ADAM_B1 * m + (1.0 - ADAM_B1) * g
    v = ADAM_B2 * v + (1.0 - ADAM_B2) * (g * g)
    m_hat = m / (1.0 - ADAM_B1 ** ADAM_STEP)
    v_hat = v / (1.0 - ADAM_B2 ** ADAM_STEP)
    delta = -ADAM_LR * (m_hat / (jnp.sqrt(v_hat) + ADAM_EPS) + ADAM_WD * w)
    return delta, m, v


def _sum_adam(name, parts, sums, chip, w, m, v, after=()):
    _, r, c = w.shape
    n_after = len(after)
    by_rows = r % ROWS == 0 or r < ROWS
    tr, tc = (min(ROWS, r), c) if by_rows else (r, _fit(c, 512))
    at = (lambda i: (i, 0)) if by_rows else (lambda i: (0, i))

    def body(chip_ref, p_ref, own_ref, w_ref, m_ref, v_ref, *rest):
        g_ref, d_ref, mo_ref, vo_ref = rest[n_after:]
        g = None
        for k in range(N_CHIP):
            term = jnp.where(chip_ref[0] == k, own_ref[...], p_ref[k]).astype(F32)
            g = term if g is None else g + term
        g_ref[...] = g
        d_ref[...], mo_ref[...], vo_ref[...] = _adamw(w_ref[...], g, m_ref[...], v_ref[...])

    blk = pl.BlockSpec((None, tr, tc), lambda i, chip_ref: (0, *at(i)))
    out = jax.ShapeDtypeStruct((1, r, c), F32)
    return pl.pallas_call(
        body, name=name,
        grid_spec=pltpu.PrefetchScalarGridSpec(
            num_scalar_prefetch=1, grid=(r // tr if by_rows else c // tc,),
            in_specs=[pl.BlockSpec((N_CHIP, tr, tc), lambda i, chip_ref: (0, *at(i))),
                      pl.BlockSpec((None, tr, tc), lambda i, chip_ref: (chip_ref[0], *at(i))), blk, blk, blk]
            + [ANY] * n_after,
            out_specs=[blk] * 4),
        out_shape=[out] * 4,
        compiler_params=_params("parallel"),
    )(chip, parts, sums, w, m, v, *after)


def _adam_gains(total, ws, ms, vs):
    n = len(ws)
    widths = [w.shape[1] for w in ws]

    def body(t_ref, *refs):
        w_refs, m_refs, v_refs, outs = refs[:n], refs[n:2 * n], refs[2 * n:3 * n], refs[3 * n:]
        off = 0
        for i in range(n):
            g = t_ref[:, off:off + widths[i]]
            off += widths[i]
            g_ref, d_ref, mo_ref, vo_ref = outs[4 * i:4 * i + 4]
            g_ref[...] = g
            d_ref[...], mo_ref[...], vo_ref[...] = _adamw(w_refs[i][...], g, m_refs[i][...], v_refs[i][...])

    out = pl.pallas_call(
        body, name="adam_gains",
        out_shape=[jax.ShapeDtypeStruct(w.shape, F32) for w in ws for _ in range(4)],
    )(total, *ws, *ms, *vs)
    return [tuple(out[4 * i:4 * i + 4]) for i in range(n)]


def _adam_taps(total, first_col, device, w, m, v):
    _, n_taps, cw = w.shape
    col_block = lambda t, dev: (0, first_col // cw + t * N_DEV + dev[0])
    tap = pl.BlockSpec((None, 1, cw), lambda t, dev: (t, 0, 0))

    def body(dev_ref, t_ref, w_ref, m_ref, v_ref, g_ref, d_ref, mo_ref, vo_ref):
        g = t_ref[...]
        g_ref[...] = g
        d_ref[...], mo_ref[...], vo_ref[...] = _adamw(w_ref[...], g, m_ref[...], v_ref[...])

    shape3 = (n_taps, 1, cw)
    out = pl.pallas_call(
        body, name="adam_taps",
        grid_spec=pltpu.PrefetchScalarGridSpec(
            num_scalar_prefetch=1, grid=(n_taps,),
            in_specs=[pl.BlockSpec((1, cw), col_block), tap, tap, tap], out_specs=[tap] * 4),
        out_shape=[jax.ShapeDtypeStruct(shape3, F32)] * 4,
    )(device, total, w.reshape(shape3), m.reshape(shape3), v.reshape(shape3))
    return tuple(o.reshape(w.shape) for o in out)


def kernel(x, pre_mix_g, w_in, conv_w, q_norm_g, w_uq, kv_norm_g, w_ukv, conv_out_g, attn_out_g, w_o, post_mix_g, pre_mlp_g, w_up, w_down, post_mlp_g, loss_target, m_pre_mix_g, m_w_in, m_conv_w, m_q_norm_g, m_w_uq, m_kv_norm_g, m_w_ukv, m_conv_out_g, m_attn_out_g, m_w_o, m_post_mix_g, m_pre_mlp_g, m_w_up, m_w_down, m_post_mlp_g, v_pre_mix_g, v_w_in, v_conv_w, v_q_norm_g, v_w_uq, v_kv_norm_g, v_w_ukv, v_conv_out_g, v_attn_out_g, v_w_o, v_post_mix_g, v_pre_mlp_g, v_w_up, v_w_down, v_post_mlp_g):
    me = 4 * lax.axis_index("x") + 2 * lax.axis_index("y") + lax.axis_index("c")
    core = lax.axis_index("c").astype(jnp.int32).reshape(1)
    chip = (2 * lax.axis_index("x") + lax.axis_index("y")).astype(jnp.int32).reshape(1)
    gains = (pre_mix_g, q_norm_g, kv_norm_g, conv_out_g, attn_out_g, post_mix_g, pre_mlp_g, post_mlp_g)
    gain_m = (m_pre_mix_g, m_q_norm_g, m_kv_norm_g, m_conv_out_g, m_attn_out_g, m_post_mix_g, m_pre_mlp_g, m_post_mlp_g)
    gain_v = (v_pre_mix_g, v_q_norm_g, v_kv_norm_g, v_conv_out_g, v_attn_out_g, v_post_mix_g, v_pre_mlp_g, v_post_mlp_g)
    names = ("w_in", "w_uq", "w_ukv", "w_o", "w_up", "w_down")
    big = dict(zip(names, (w_in, w_uq, w_ukv, w_o, w_up, w_down)))
    big_m = dict(zip(names, (m_w_in, m_w_uq, m_w_ukv, m_w_o, m_w_up, m_w_down)))
    big_v = dict(zip(names, (v_w_in, v_w_uq, v_w_ukv, v_w_o, v_w_up, v_w_down)))
    n_heads = attn_out_g.shape[1] // HEAD
    n_taps = conv_w.shape[1]

    gathered = ("w_in", "conv", "w_uq", "w_ukv", "w_o", "w_up", "w_down")
    gather_groups = ((0, 1), (2, 3, 4), (5,), (6,))
    taps = jnp.pad(conv_w[0], ((0, SUBLANE - n_taps), (0, 0)))
    relayed_groups = (0, 2, 3)
    sems1, shards, lands, token = _gather_start("gather_start_first", [w_in[0].astype(BF16), taps], ((0, 1),), relayed=(0,))
    sems1, shards, lands = list(sems1), list(shards), list(lands)
    behind = token[0, 0]
    rest = [(big[nm][0] + behind).astype(BF16) for nm in gathered[2:]]

    def start_rest(after):
        sems_b, shards_b, lands_b, started = _gather_start("gather_start_rest", rest, ((0, 1, 2), (3,), (4,)), relayed=(1, 2),
                                                          after=after)
        sems1.extend(sems_b)
        shards.extend(shards_b)
        lands.extend(lands_b)
        return started

    cols = lambda a: jnp.concatenate([a[j] for j in range(N_DEV)], axis=1)
    rows = lambda a: a.reshape(N_DEV * a.shape[1], a.shape[2])
    ready = {
        "w_in": _join_col_shards,
        "conv": lambda a: cols(a)[:n_taps],
        "w_uq": lambda a: _permute_q_cols(cols(a), n_heads),
        "w_ukv": cols, "w_o": rows, "w_up": lambda a: a, "w_down": rows,
    }

    class Weights:
        def __init__(self):
            self.passed, self.relayed = {}, {}

        def forward(self, group, after):
            idx = gather_groups[group]
            if group == 0:
                after = (*after, *rest)
            self.passed[group] = _gather_forward(f"gather_forward_{group}", [shards[i] for i in idx], [lands[i] for i in idx],
                                                 *sems1[group], after, relayed=group in relayed_groups)
            return tuple(self.passed[group][1])

        def relay(self, group, after):
            sems2, mid = self.passed[group]
            self.relayed[group], mid = _gather_relay_forward(f"gather_relay_{group}", mid, *sems2, after)
            self.passed[group] = (sems2, mid)
            if group == 0:
                start_rest(tuple(mid))
            return tuple(mid)

        def ready(self, group, after):
            sems2, mid = self.passed[group]
            full = _gather_wait(f"gather_wait_{group}", mid, *sems2, after, relay_sems=self.relayed.get(group))
            out = []
            for i, a in zip(gather_groups[group], full):
                a = lax.dynamic_update_index_in_dim(a, shards[i], me, 0)
                out.append(ready[gathered[i]](a))
            return out

    weights = Weights()

    col_blocks = lambda g: g.reshape(g.shape[0], N_DEV, g.shape[1] // N_DEV).transpose(1, 0, 2)
    row_blocks = lambda g: g.reshape(N_DEV, g.shape[0] // N_DEV, g.shape[1])
    grad_groups = (("w_down",), ("w_up",), ("w_o",), ("w_uq", "w_ukv"), ("w_in",))
    transposed = {"w_in": w_in.shape[2], "w_uq": w_uq.shape[2]}
    to_blocks = {
        "w_in": lambda g: g, "w_uq": lambda g: _unpermute_q_rows(g, n_heads),
        "w_ukv": col_blocks, "w_o": row_blocks, "w_up": lambda g: g, "w_down": row_blocks,
    }
    in_flight = []

    class Grads:
        def __init__(self):
            self.core = core
            self.away = {}

        def send_sums(self, group, sums):
            sems, sums, parts, tok = _chip_send_start(f"chip_send_start_{group}", list(sums))
            in_flight.append((sems, sums, parts))
            return (tok,)

        def full(self, group, arrays, received=None):
            nms = grad_groups[group]
            if received is None:
                blocks = [to_blocks[nm](g) for nm, g in zip(nms, arrays)]
                got = _pair_exchange(f"pair_exchange_{group}", blocks, [transposed.get(nm) for nm in nms])
            else:
                blocks, got = [self.away[group][1]], [self.received(group, received)]
            sums = [(_pair_sum_rows if nm in transposed else _pair_sum)(f"pair_sum_{nm}", g, r, core)
                    for nm, g, r in zip(nms, blocks, got)]
            return self.send_sums(group, sums)

        def send_away(self, group, half):
            nm = grad_groups[group][0]
            rows = transposed.get(nm)
            sems, src, land, tok = _pair_send_start(f"pair_send_start_{group}", half if rows is None else to_blocks[nm](half), rows)
            self.away[group] = (sems, src, land, rows)
            return (tok,)

        def received(self, group, after):
            sems, src, land, rows = self.away[group]
            return _pair_send_wait(f"pair_send_wait_{group}", sems, src, land, after, rows)

    grad_x, small = _local_step(x[0], loss_target[0], gains, weights, Grads(), first_after=(token,))

    big_out = {}

    def update(tag, first, last, after):
        groups = in_flight[first:last]
        parts = _chip_send_wait("chip_send_wait_" + tag, groups, after)
        nms = [nm for grp in grad_groups[first:last] for nm in grp]
        sums = [a for _, s, _ in groups for a in s]
        for nm, p, s in zip(nms, parts, sums):
            view = (lambda a: jnp.swapaxes(a, 1, 2)) if nm in transposed else (lambda a: a)
            out = _sum_adam("adam_" + nm, p, s, chip, view(big[nm]), view(big_m[nm]), view(big_v[nm]), after=after)
            after = (out[0],)
            big_out[nm] = [view(o) for o in out]
        return after

    after = update("early", 0, len(in_flight) - 1, (grad_x,))
    total = _small_all_reduce(small, after=after)
    update("late", len(in_flight) - 1, len(in_flight), (total,))
    big_out = [big_out[nm] for nm in names]

    gain_out = _adam_gains(total, gains, gain_m, gain_v)
    taps_out = _adam_taps(total, sum(g.shape[1] for g in gains), me.astype(jnp.int32).reshape(1), conv_w, m_conv_w, v_conv_w)
    loss = total[0, total.shape[1] - 1]

    order = (0, "w_in", "conv", 1, "w_uq", 2, "w_ukv", 3, 4, "w_o", 5, 6, "w_up", "w_down", 7)
    by_name = dict(zip(names, big_out))
    outs = [loss, grad_x[None]]
    for kind in range(4):
        for item in order:
            if item == "conv":
                outs.append(taps_out[kind])
            elif isinstance(item, int):
                outs.append(gain_out[item][kind])
            else:
                outs.append(by_name[item][kind])
    return tuple(outs)
```

```python
import math

import jax
import jax.numpy as jnp
from jax import lax
from jax.experimental import pallas as pl
from jax.experimental.pallas import tpu as pltpu

F32 = jnp.float32
BF16 = jnp.bfloat16

EPS = 1e-6
NEG_INF = -1e30
HEAD = 128
ROPE = 64
QK = HEAD + ROPE
CHUNK = 64
ROPE_THETA = 10000.0
ADAM_LR, ADAM_B1, ADAM_B2, ADAM_EPS, ADAM_WD, ADAM_STEP = 0.001, 0.9, 0.999, 1e-08, 0.01, 10

LANE = 128
SUBLANE = 8
VMEM_LIMIT_BYTES = 56 * 1024 * 1024

N_DEV = 8
N_CHIP = 4
MESH = pl.DeviceIdType.MESH


def _params(*sem):
    return pltpu.CompilerParams(dimension_semantics=sem, vmem_limit_bytes=VMEM_LIMIT_BYTES)


ANY = pl.BlockSpec(memory_space=pl.ANY)


def _call(body, *, in_specs, after=(), **kw):
    n_in, n_after = len(in_specs), len(after)

    def ordered(*refs):
        body(*refs[:n_in], *refs[n_in + n_after:])

    call = pl.pallas_call(ordered, in_specs=[*in_specs, *[ANY] * n_after], **kw)
    return lambda *operands: call(*operands, *after)


def _sublane_sum(v):
    r, w = v.shape
    return jnp.sum(v.reshape(r // SUBLANE, SUBLANE, w), axis=0)


def _rstd(x):
    return lax.rsqrt(jnp.mean(x * x, axis=-1, keepdims=True) + EPS)


def _rms_bwd(x, g, dy):
    r = _rstd(x)
    xh = x * r
    dxh = dy * g
    dx = r * (dxh - xh * jnp.mean(dxh * xh, axis=-1, keepdims=True))
    return dx, dy * xh


def _accumulate(ref, val, step):
    @pl.when(step == 0)
    def _():
        ref[...] = val

    @pl.when(step > 0)
    def _():
        ref[...] += val


NN = ((1,), (0,))
NT = ((1,), (1,))
TN = ((0,), (0,))


def _matmul(name, a, b, *, grid, a_spec, b_spec, out_shape, out_specs, contract, nk=1, acc_shape=None,
            extras=(), extra_specs=(), epilogue=None, after=()):
    multi = isinstance(out_shape, (tuple, list))
    out_shapes = tuple(out_shape) if multi else (out_shape,)
    n_out = len(out_shapes)
    n_extra = len(extras)

    def body(a_ref, b_ref, *rest):
        x_refs = rest[:n_extra]
        o_refs = rest[n_extra:n_extra + n_out]

        def emit(acc):
            vals = epilogue(acc, *[r[...] for r in x_refs]) if epilogue else (acc,)
            for r, v in zip(o_refs, vals):
                r[...] = v.astype(r.dtype)

        p = lax.dot_general(a_ref[...], b_ref[...], (contract, ((), ())), preferred_element_type=F32)
        if nk == 1:
            emit(p)
        else:
            acc_ref = rest[n_extra + n_out]
            k = pl.program_id(2)
            _accumulate(acc_ref, p, k)

            @pl.when(k == nk - 1)
            def _():
                emit(acc_ref[...])

    sem = ("parallel", "parallel") + (("arbitrary",) if nk > 1 else ())
    return _call(
        body, name=name, grid=grid, after=after,
        in_specs=[a_spec, b_spec, *extra_specs],
        out_specs=out_specs,
        out_shape=out_shape,
        scratch_shapes=[pltpu.VMEM(acc_shape, F32)] if nk > 1 else [],
        compiler_params=_params(*sem),
    )(a, b, *extras)


def _fit(n, tile):
    if n <= tile:
        return n
    t = tile - tile % LANE
    while n % t:
        t -= LANE
    return t


def _mm_nn(name, a, b, out_dtype, tm, tn, after=()):
    m, k = a.shape
    n = b.shape[1]
    tm, tn = _fit(m, tm), _fit(n, tn)
    return _matmul(name, a, b, grid=(m // tm, n // tn), after=after,
                   a_spec=pl.BlockSpec((tm, k), lambda i, j: (i, 0)),
                   b_spec=pl.BlockSpec((k, tn), lambda i, j: (0, j)),
                   out_shape=jax.ShapeDtypeStruct((m, n), out_dtype),
                   out_specs=pl.BlockSpec((tm, tn), lambda i, j: (i, j)), contract=NN)


def _mm_nt(name, a, b, out_dtype, tm, tn, after=()):
    m, k = a.shape
    n = b.shape[0]
    tm, tn = _fit(m, tm), _fit(n, tn)
    return _matmul(name, a, b, grid=(m // tm, n // tn), after=after,
                   a_spec=pl.BlockSpec((tm, k), lambda i, j: (i, 0)),
                   b_spec=pl.BlockSpec((tn, k), lambda i, j: (j, 0)),
                   out_shape=jax.ShapeDtypeStruct((m, n), out_dtype),
                   out_specs=pl.BlockSpec((tm, tn), lambda i, j: (i, j)), contract=NT)


def _mm_tn(name, a, b, out_dtype, tm, tn):
    s, m = a.shape
    n = b.shape[1]
    tm, tn = _fit(m, tm), _fit(n, tn)
    return _matmul(name, a, b, grid=(m // tm, n // tn),
                   a_spec=pl.BlockSpec((s, tm), lambda i, j: (0, i)),
                   b_spec=pl.BlockSpec((s, tn), lambda i, j: (0, j)),
                   out_shape=jax.ShapeDtypeStruct((m, n), out_dtype),
                   out_specs=pl.BlockSpec((tm, tn), lambda i, j: (i, j)), contract=TN)


ROWS = 256


def _row_spec(rows, width):
    return pl.BlockSpec((rows, width), lambda i: (i, 0))


def _fixed_spec(rows, width):
    return pl.BlockSpec((rows, width), lambda i: (0, 0))


def _column_pieces(rows, start, width):
    piece = math.gcd(start, width)
    assert piece % LANE == 0
    return [pl.BlockSpec((rows, piece), lambda i, b=start // piece + p: (i, b)) for p in range(width // piece)]


def _rms_fwd(name, x, g, cols=None, after=()):
    s = x.shape[0]
    start, w = cols or (0, x.shape[1])
    rows = min(ROWS, s)
    pieces = _column_pieces(rows, start, w) if cols else [_row_spec(rows, w)]
    n = len(pieces)

    def body(*refs):
        g_ref, o_ref = refs[n:]
        xv = refs[0][...] if n == 1 else jnp.concatenate([r[...] for r in refs[:n]], axis=1)
        o_ref[...] = (xv * _rstd(xv) * g_ref[...]).astype(o_ref.dtype)

    return _call(
        body, name=name, grid=(s // rows,), after=after,
        in_specs=[*pieces, _fixed_spec(1, w)],
        out_specs=_row_spec(rows, w),
        out_shape=jax.ShapeDtypeStruct((s, w), BF16),
        compiler_params=_params("parallel"),
    )(*[x] * n, g)


def _rms_bwd_call(name, x, g, dy, out_dtype, cols=None, after=()):
    s = x.shape[0]
    start, w = cols or (0, x.shape[1])
    rows = min(ROWS, s)
    pieces = _column_pieces(rows, start, w) if cols else [_row_spec(rows, w)]
    n = len(pieces)

    def body(*refs):
        g_ref, dy_ref, dx_ref, dg_ref = refs[n:]
        xv = refs[0][...] if n == 1 else jnp.concatenate([r[...] for r in refs[:n]], axis=1)
        dx, dgc = _rms_bwd(xv, g_ref[...], dy_ref[...].astype(F32))
        dx_ref[...] = dx.astype(dx_ref.dtype)
        _accumulate(dg_ref, _sublane_sum(dgc), pl.program_id(0))

    return _call(
        body, name=name, grid=(s // rows,), after=after,
        in_specs=[*pieces, _fixed_spec(1, w), _row_spec(rows, w)],
        out_specs=[_row_spec(rows, w), _fixed_spec(SUBLANE, w)],
        out_shape=[jax.ShapeDtypeStruct((s, w), out_dtype), jax.ShapeDtypeStruct((SUBLANE, w), F32)],
        compiler_params=_params("arbitrary"),
    )(*[x] * n, g, dy)


def _mid_fwd(x, y, g_post, g_pre, after=()):
    s, w = x.shape
    rows = min(ROWS, s)

    def body(x_ref, y_ref, gp_ref, gq_ref, x2_ref, h2_ref):
        yv = y_ref[...]
        x2 = x_ref[...] + yv * _rstd(yv) * gp_ref[...]
        x2_ref[...] = x2
        h2_ref[...] = (x2 * _rstd(x2) * gq_ref[...]).astype(h2_ref.dtype)

    return _call(
        body, name="mid_fwd", grid=(s // rows,), after=after,
        in_specs=[_row_spec(rows, w), _row_spec(rows, w), _fixed_spec(1, w), _fixed_spec(1, w)],
        out_specs=[_row_spec(rows, w), _row_spec(rows, w)],
        out_shape=[jax.ShapeDtypeStruct((s, w), F32), jax.ShapeDtypeStruct((s, w), BF16)],
        compiler_params=_params("parallel"),
    )(x, y, g_post, g_pre)


def _head(m, x2, tgt, g):
    s, w = m.shape
    rows = min(ROWS, s)

    def body(m_ref, x2_ref, t_ref, g_ref, dout_ref, dm_ref, dg_ref, loss_ref):
        mv = m_ref[...]
        gv = g_ref[...]
        out = x2_ref[...] + mv * _rstd(mv) * gv
        err = out - t_ref[...]
        dout = err * (1.0 / w)
        dout_ref[...] = dout
        dm, dgc = _rms_bwd(mv, gv, dout)
        dm_ref[...] = dm.astype(dm_ref.dtype)
        sq = err * err
        lanes = sq[:, 0:LANE]
        for j in range(1, w // LANE):
            lanes = lanes + sq[:, j * LANE:(j + 1) * LANE]
        step = pl.program_id(0)
        _accumulate(dg_ref, _sublane_sum(dgc), step)
        _accumulate(loss_ref, _sublane_sum(lanes) * (0.5 / w), step)

    return pl.pallas_call(
        body, name="head", grid=(s // rows,),
        in_specs=[_row_spec(rows, w), _row_spec(rows, w), _row_spec(rows, w), _fixed_spec(1, w)],
        out_specs=[_row_spec(rows, w), _row_spec(rows, w), _fixed_spec(SUBLANE, w), _fixed_spec(SUBLANE, LANE)],
        out_shape=[jax.ShapeDtypeStruct((s, w), F32), jax.ShapeDtypeStruct((s, w), BF16),
                   jax.ShapeDtypeStruct((SUBLANE, w), F32), jax.ShapeDtypeStruct((SUBLANE, LANE), F32)],
        compiler_params=_params("arbitrary"),
    )(m, x2, tgt, g)


def _mid_bwd(x2, y, d_out, d_h2, g_pre, g_post, after=()):
    s, w = x2.shape
    rows = min(ROWS, s)

    def body(x2_ref, y_ref, dout_ref, dh2_ref, gq_ref, gp_ref, dx2_ref, dy_ref, dgq_ref, dgp_ref):
        dx, dgq = _rms_bwd(x2_ref[...], gq_ref[...], dh2_ref[...])
        dx2 = dout_ref[...] + dx
        dx2_ref[...] = dx2
        dy, dgp = _rms_bwd(y_ref[...], gp_ref[...], dx2)
        dy_ref[...] = dy.astype(dy_ref.dtype)
        step = pl.program_id(0)
        _accumulate(dgq_ref, _sublane_sum(dgq), step)
        _accumulate(dgp_ref, _sublane_sum(dgp), step)

    return _call(
        body, name="mid_bwd", grid=(s // rows,), after=after,
        in_specs=[_row_spec(rows, w)] * 4 + [_fixed_spec(1, w)] * 2,
        out_specs=[_row_spec(rows, w), _row_spec(rows, w), _fixed_spec(SUBLANE, w), _fixed_spec(SUBLANE, w)],
        out_shape=[jax.ShapeDtypeStruct((s, w), F32), jax.ShapeDtypeStruct((s, w), BF16),
                   jax.ShapeDtypeStruct((SUBLANE, w), F32), jax.ShapeDtypeStruct((SUBLANE, w), F32)],
        compiler_params=_params("arbitrary"),
    )(x2, y, d_out, d_h2, g_pre, g_post)


def _first_bwd(x, g, d_h1, d_x2, after=()):
    s, w = x.shape
    rows = min(ROWS, s)

    def body(x_ref, g_ref, dh_ref, dx2_ref, dx_ref, dg_ref):
        dx, dgc = _rms_bwd(x_ref[...], g_ref[...], dh_ref[...])
        dx_ref[...] = dx2_ref[...] + dx
        _accumulate(dg_ref, _sublane_sum(dgc), pl.program_id(0))

    return _call(
        body, name="first_bwd", grid=(s // rows,), after=after,
        in_specs=[_row_spec(rows, w), _fixed_spec(1, w), _row_spec(rows, w), _row_spec(rows, w)],
        out_specs=[_row_spec(rows, w), _fixed_spec(SUBLANE, w)],
        out_shape=[jax.ShapeDtypeStruct((s, w), F32), jax.ShapeDtypeStruct((SUBLANE, w), F32)],
        compiler_params=_params("arbitrary"),
    )(x, g, d_h1, d_x2)


def _shift_down(v, k):
    t = lax.broadcasted_iota(jnp.int32, v.shape, 0)
    return jnp.where(t >= k, pltpu.roll(v, k, 0), 0.0)


def _shift_up(v, k):
    n = v.shape[0]
    t = lax.broadcasted_iota(jnp.int32, v.shape, 0)
    return jnp.where(t < n - k, pltpu.roll(v, n - k, 0), 0.0)


def _conv_core(u, b, c, w):
    z = c * u
    conv = w[0:1, :] * _shift_down(z, 2) + w[1:2, :] * _shift_down(z, 1) + w[2:3, :] * z
    return z, conv, b * conv


def _conv_fwd(proj, conv_w, g, n_groups, out_width):
    s = proj.shape[0]

    def body(u_ref, b_ref, c_ref, w_ref, g_ref, o_ref):
        _, _, yr = _conv_core(u_ref[...], b_ref[...], c_ref[...], w_ref[...])
        o_ref[...] = (yr * _rstd(yr) * g_ref[...]).astype(o_ref.dtype)

    col = lambda k: pl.BlockSpec((s, HEAD), lambda i: (0, k * n_groups + i))
    return pl.pallas_call(
        body, name="conv_fwd", grid=(n_groups,),
        in_specs=[col(0), col(1), col(2), pl.BlockSpec((3, HEAD), lambda i: (0, i)), pl.BlockSpec((1, HEAD), lambda i: (0, i))],
        out_specs=pl.BlockSpec((s, HEAD), lambda i: (0, i)),
        out_shape=jax.ShapeDtypeStruct((s, out_width), BF16),
        compiler_params=_params("parallel"),
    )(proj, proj, proj, conv_w, g)


def _conv_bwd(proj, d_mix, conv_w, g, n_groups):
    s = proj.shape[0]
    width = n_groups * HEAD

    def body(u_ref, b_ref, c_ref, dy_ref, w_ref, g_ref, du_ref, db_ref, dc_ref, dg_ref, dw_ref):
        u, b, c, w = u_ref[...], b_ref[...], c_ref[...], w_ref[...]
        z, conv, yr = _conv_core(u, b, c, w)
        dyr, dgc = _rms_bwd(yr, g_ref[...], dy_ref[...])
        dconv = dyr * b
        db_ref[...] = (dyr * conv).astype(db_ref.dtype)
        dz = w[2:3, :] * dconv + w[1:2, :] * _shift_up(dconv, 1) + w[0:1, :] * _shift_up(dconv, 2)
        dc_ref[...] = (dz * u).astype(dc_ref.dtype)
        du_ref[...] = (dz * c).astype(du_ref.dtype)
        dg_ref[...] = _sublane_sum(dgc)
        dw_ref[0] = _sublane_sum(dconv * _shift_down(z, 2))
        dw_ref[1] = _sublane_sum(dconv * _shift_down(z, 1))
        dw_ref[2] = _sublane_sum(dconv * z)

    col = lambda k: pl.BlockSpec((s, HEAD), lambda i: (0, k * n_groups + i))
    grp = pl.BlockSpec((s, HEAD), lambda i: (0, i))
    return pl.pallas_call(
        body, name="conv_bwd", grid=(n_groups,),
        in_specs=[col(0), col(1), col(2), grp, pl.BlockSpec((3, HEAD), lambda i: (0, i)), pl.BlockSpec((1, HEAD), lambda i: (0, i))],
        out_specs=[grp, grp, grp, pl.BlockSpec((SUBLANE, HEAD), lambda i: (0, i)),
                   pl.BlockSpec((3, SUBLANE, HEAD), lambda i: (0, 0, i))],
        out_shape=[jax.ShapeDtypeStruct((s, width), BF16)] * 3
        + [jax.ShapeDtypeStruct((SUBLANE, width), F32), jax.ShapeDtypeStruct((3, SUBLANE, width), F32)],
        compiler_params=_params("parallel"),
    )(proj, proj, proj, d_mix, conv_w, g)


def _rope_tables(s, n_heads):
    pos = jnp.arange(s, dtype=F32)
    inv_freq = jnp.power(ROPE_THETA, -jnp.arange(0, ROPE, 2, dtype=F32) / ROPE)
    ang = pos[:, None] * inv_freq[None, :]
    cos, sin = jnp.cos(ang), jnp.sin(ang)
    cs = jnp.concatenate([cos, cos], axis=1)
    sn = jnp.concatenate([-sin, sin], axis=1)
    pad = jnp.zeros((s, LANE - ROPE), F32)
    return (jnp.tile(cs, (1, n_heads)), jnp.tile(sn, (1, n_heads)),
            jnp.concatenate([cs, pad], axis=1), jnp.concatenate([sn, pad], axis=1))


def _swap_halves(v):
    w = v.shape[1]
    lane = lax.broadcasted_iota(jnp.int32, v.shape, 1)
    first = (lane % ROPE) < (ROPE // 2)
    return jnp.where(first, pltpu.roll(v, w - ROPE // 2, 1), pltpu.roll(v, ROPE // 2, 1))


def _pack_heads(q, kv, proj, kr_col, tables, n_heads, after=()):
    s = q.shape[0]
    rows = min(ROWS, s)
    cq, sq, ck, sk = tables
    wq = n_heads * ROPE

    def body(q_ref, kv_ref, kr_ref, cq_ref, sq_ref, ck_ref, sk_ref, qo_ref, ko_ref, vo_ref):
        qr = q_ref[:, n_heads * HEAD:]
        qr = qr * cq_ref[...] + _swap_halves(qr) * sq_ref[...]
        krv = kr_ref[...]
        krv = krv * ck_ref[...] + _swap_halves(krv) * sk_ref[...]
        for h in range(n_heads):
            qo_ref[h] = jnp.concatenate([q_ref[:, h * HEAD:(h + 1) * HEAD], qr[:, h * ROPE:(h + 1) * ROPE]], axis=1).astype(BF16)
            ko_ref[h] = jnp.concatenate([kv_ref[:, 2 * h * HEAD:(2 * h + 1) * HEAD], krv[:, :ROPE]], axis=1).astype(BF16)
            vo_ref[h] = kv_ref[:, (2 * h + 1) * HEAD:(2 * h + 2) * HEAD].astype(BF16)

    hs = lambda w: pl.BlockSpec((n_heads, rows, w), lambda i: (0, i, 0))
    return _call(
        body, name="pack_heads", grid=(s // rows,), after=after,
        in_specs=[_row_spec(rows, q.shape[1]), _row_spec(rows, kv.shape[1]), pl.BlockSpec((rows, LANE), lambda i: (i, kr_col // LANE)),
                  _row_spec(rows, wq), _row_spec(rows, wq), _row_spec(rows, LANE), _row_spec(rows, LANE)],
        out_specs=[hs(QK), hs(QK), hs(HEAD)],
        out_shape=[jax.ShapeDtypeStruct((n_heads, s, QK), BF16), jax.ShapeDtypeStruct((n_heads, s, QK), BF16),
                   jax.ShapeDtypeStruct((n_heads, s, HEAD), BF16)],
        compiler_params=_params("parallel"),
    )(q, kv, proj, cq, sq, ck, sk)


def _unpack_heads(dq, dk, dv, tables, n_heads):
    s = dq.shape[1]
    rows = min(ROWS, s)
    cq, sq, ck, sk = tables
    wq = n_heads * ROPE

    def body(dq_ref, dk_ref, dv_ref, cq_ref, sq_ref, ck_ref, sk_ref, qo_ref, kvo_ref, kro_ref):
        dqr = jnp.concatenate([dq_ref[h][:, HEAD:] for h in range(n_heads)], axis=1)
        dqr = dqr * cq_ref[...] - _swap_halves(dqr) * sq_ref[...]
        dkr = dk_ref[0][:, HEAD:]
        for h in range(1, n_heads):
            dkr = dkr + dk_ref[h][:, HEAD:]
        dkr = jnp.concatenate([dkr, jnp.zeros((rows, LANE - ROPE), F32)], axis=1)
        dkr = dkr * ck_ref[...] - _swap_halves(dkr) * sk_ref[...]
        kro_ref[...] = dkr.astype(kro_ref.dtype)
        qo_ref[:, n_heads * HEAD:] = dqr.astype(qo_ref.dtype)
        for h in range(n_heads):
            qo_ref[:, h * HEAD:(h + 1) * HEAD] = dq_ref[h][:, :HEAD].astype(qo_ref.dtype)
            kvo_ref[:, 2 * h * HEAD:(2 * h + 1) * HEAD] = dk_ref[h][:, :HEAD].astype(kvo_ref.dtype)
            kvo_ref[:, (2 * h + 1) * HEAD:(2 * h + 2) * HEAD] = dv_ref[h].astype(kvo_ref.dtype)

    hs = lambda w: pl.BlockSpec((n_heads, rows, w), lambda i: (0, i, 0))
    return pl.pallas_call(
        body, name="unpack_heads", grid=(s // rows,),
        in_specs=[hs(QK), hs(QK), hs(HEAD), _row_spec(rows, wq), _row_spec(rows, wq), _row_spec(rows, LANE), _row_spec(rows, LANE)],
        out_specs=[_row_spec(rows, n_heads * QK), _row_spec(rows, 2 * n_heads * HEAD), _row_spec(rows, LANE)],
        out_shape=[jax.ShapeDtypeStruct((s, n_heads * QK), BF16), jax.ShapeDtypeStruct((s, 2 * n_heads * HEAD), BF16),
                   jax.ShapeDtypeStruct((s, LANE), BF16)],
        compiler_params=_params("parallel"),
    )(dq, dk, dv, cq, sq, ck, sk)


TQ = 256


LOG2_E = 1.4426950408889634


def _softmax_parts(q, k):
    tq, n_keys = q.shape[0], k.shape[0]
    sc = lax.dot_general(q, k, (NT, ((), ())), preferred_element_type=F32) * (QK ** -0.5 * LOG2_E)
    row = lax.broadcasted_iota(jnp.int32, (tq, tq), 0)
    col = lax.broadcasted_iota(jnp.int32, (tq, tq), 1)
    own = jnp.where(col // CHUNK <= row // CHUNK, sc[:, n_keys - tq:], NEG_INF)
    sc = own if n_keys == tq else jnp.concatenate([sc[:, :n_keys - tq], own], axis=1)
    e = jnp.exp2(sc - jnp.max(sc, axis=-1, keepdims=True))
    return e, 1.0 / jnp.sum(e, axis=-1, keepdims=True)


def _attn_fwd(q, k, v, g, mix, col0):
    n_heads, s, _ = q.shape
    tq = min(TQ, s)
    assert tq % CHUNK == 0 and s % tq == 0

    def body(q_ref, k_ref, v_ref, g_ref, mix_ref, o_ref, y_ref):
        for c in range(s // tq):
            rows, n_keys = pl.ds(c * tq, tq), (c + 1) * tq
            e, inv = _softmax_parts(q_ref[rows, :], k_ref[0:n_keys, :])
            o = jnp.dot(e.astype(BF16), v_ref[0:n_keys, :], preferred_element_type=F32) * inv
            o_ref[rows, :] = o
            y_ref[rows, :] = (o * _rstd(o) * g_ref[...]).astype(y_ref.dtype)

    head = lambda w: pl.BlockSpec((None, s, w), lambda h: (h, 0, 0))
    return pl.pallas_call(
        body, name="attn_fwd", grid=(n_heads,),
        in_specs=[head(QK), head(QK), head(HEAD), pl.BlockSpec((1, HEAD), lambda h: (0, h)), ANY],
        out_specs=[head(HEAD), pl.BlockSpec((s, HEAD), lambda h: (0, col0 // HEAD + h))],
        out_shape=[jax.ShapeDtypeStruct((n_heads, s, HEAD), F32), jax.ShapeDtypeStruct(mix.shape, mix.dtype)],
        input_output_aliases={4: 1},
        compiler_params=_params("parallel"),
    )(q, k, v, g, mix)


def _attn_bwd(q, k, v, o, d_mix, g, col0, after=()):
    n_heads, s, _ = q.shape
    tq = min(TQ, s)

    def body(q_ref, k_ref, v_ref, o_ref, dy_ref, g_ref, dq_ref, dk_ref, dv_ref, dg_ref):
        dg = None
        for c in reversed(range(s // tq)):
            rows, n_keys = pl.ds(c * tq, tq), (c + 1) * tq
            qv, kv_, vv = q_ref[rows, :], k_ref[0:n_keys, :], v_ref[0:n_keys, :]
            do, dgc = _rms_bwd(o_ref[rows, :], g_ref[...], dy_ref[rows, :])
            do = do.astype(BF16)
            dg = _sublane_sum(dgc) if dg is None else dg + _sublane_sum(dgc)
            e, inv = _softmax_parts(qv, kv_)
            p = e * inv
            dp = lax.dot_general(do, vv, (NT, ((), ())), preferred_element_type=F32)
            ds = (p * (dp - jnp.sum(p * dp, axis=-1, keepdims=True)) * (QK ** -0.5)).astype(BF16)
            dq_ref[rows, :] = jnp.dot(ds, kv_, preferred_element_type=F32)
            dk = lax.dot_general(ds, qv, (TN, ((), ())), preferred_element_type=F32)
            dv = lax.dot_general(p.astype(BF16), do, (TN, ((), ())), preferred_element_type=F32)
            if n_keys == s:
                dk_ref[...] = dk
                dv_ref[...] = dv
            else:
                dk_ref[0:n_keys, :] += dk
                dv_ref[0:n_keys, :] += dv
        dg_ref[...] = dg

    c0 = col0 // HEAD
    head = lambda w: pl.BlockSpec((None, s, w), lambda h: (h, 0, 0))
    return _call(
        body, name="attn_bwd", grid=(n_heads,), after=after,
        in_specs=[head(QK), head(QK), head(HEAD), head(HEAD), pl.BlockSpec((s, HEAD), lambda h: (0, c0 + h)),
                  pl.BlockSpec((1, HEAD), lambda h: (0, h))],
        out_specs=[head(QK), head(QK), head(HEAD), pl.BlockSpec((SUBLANE, HEAD), lambda h: (0, h))],
        out_shape=[jax.ShapeDtypeStruct((n_heads, s, QK), F32), jax.ShapeDtypeStruct((n_heads, s, QK), F32),
                   jax.ShapeDtypeStruct((n_heads, s, HEAD), F32), jax.ShapeDtypeStruct((SUBLANE, n_heads * HEAD), F32)],
        compiler_params=_params("parallel"),
    )(q, k, v, o, d_mix, g)


TILE_M = 1024
TILE_N = 1024


def _up_fwd(h2, w_up):
    s, d = h2.shape
    nb, _, fb = w_up.shape
    tm = min(TILE_M,s)

    def epilogue(acc):
        r = jnp.maximum(acc, 0.0)
        return r * r, r

    blk = pl.BlockSpec((tm, fb), lambda i, j: (i, j))
    return _matmul("up_fwd", h2, w_up, grid=(s // tm, nb),
                   a_spec=pl.BlockSpec((tm, d), lambda i, j: (i, 0)),
                   b_spec=pl.BlockSpec((None, d, fb), lambda i, j: (j, 0, 0)),
                   out_shape=[jax.ShapeDtypeStruct((s, nb * fb), BF16)] * 2, out_specs=[blk, blk],
                   contract=NN, epilogue=epilogue)


def _down_fwd(a, w_down):
    s, f = a.shape
    d = w_down.shape[1]
    tm, tn, tk = min(TILE_M,s), min(TILE_N,d), 2048
    nk = f // tk
    return _matmul("down_fwd", a, w_down, grid=(s // tm, d // tn, nk),
                   a_spec=pl.BlockSpec((tm, tk), lambda i, j, k: (i, k)),
                   b_spec=pl.BlockSpec((tk, tn), lambda i, j, k: (k, j)),
                   out_shape=jax.ShapeDtypeStruct((s, d), F32),
                   out_specs=pl.BlockSpec((tm, tn), lambda i, j, k: (i, j)),
                   contract=NN, nk=nk, acc_shape=(tm, tn))


def _down_bwd_act(d_m, w_down, r, after=()):
    s, d = d_m.shape
    f = w_down.shape[0]
    tm, tn = min(TILE_M,s), min(TILE_N,f)
    blk = pl.BlockSpec((tm, tn), lambda i, j: (i, j))
    return _matmul("down_bwd_act", d_m, w_down, grid=(s // tm, f // tn), after=after,
                   a_spec=pl.BlockSpec((tm, d), lambda i, j: (i, 0)),
                   b_spec=pl.BlockSpec((tn, d), lambda i, j: (j, 0)),
                   out_shape=jax.ShapeDtypeStruct((s, f), BF16), out_specs=blk, contract=NT,
                   extras=(r,), extra_specs=(blk,),
                   epilogue=lambda acc, rv: (acc * (2.0 * rv.astype(F32)),))


def _up_bwd_act(d_up, w_up, after=()):
    s, _ = d_up.shape
    nb, d, fb = w_up.shape
    tm, tn = min(TILE_M, s), min(TILE_N,d)
    pair = 2
    n_after = len(after)

    def body(a_ref, w_ref, *rest):
        o_ref, acc_ref = rest[n_after:]
        k = pl.program_id(2)
        p = None
        for t in range(pair):
            term = lax.dot_general(a_ref[:, t * fb:(t + 1) * fb], w_ref[t], (NT, ((), ())), preferred_element_type=F32)
            p = term if p is None else p + term
        _accumulate(acc_ref, p, k)

        @pl.when(k == nb // pair - 1)
        def _():
            o_ref[...] = acc_ref[...]

    return pl.pallas_call(
        body, name="up_bwd_act", grid=(s // tm, d // tn, nb // pair),
        in_specs=[pl.BlockSpec((tm, pair * fb), lambda i, j, k: (i, k)),
                  pl.BlockSpec((pair, tn, fb), lambda i, j, k: (k, j, 0))] + [ANY] * n_after,
        out_specs=pl.BlockSpec((tm, tn), lambda i, j, k: (i, j)),
        out_shape=jax.ShapeDtypeStruct((s, d), F32),
        scratch_shapes=[pltpu.VMEM((tm, tn), F32)],
        compiler_params=_params("parallel", "parallel", "arbitrary"),
    )(d_up, w_up, *after)


def _half_grad(name, a, b, core, home, received, after, *, grid, a_block, a_map, b_block, b_map, o_block, o_map, out_shape):
    n_after = len(after)
    pick = (lambda ref: ref[0]) if home else (lambda ref: 1 - ref[0])

    def body(core_ref, a_ref, b_ref, *rest):
        acc = lax.dot_general(a_ref[...], b_ref[...], (TN, ((), ())), preferred_element_type=F32)
        if received is not None:
            acc = acc + rest[0][...].astype(F32)
        rest[-1][...] = acc.astype(rest[-1].dtype)

    wrap = lambda fn: (lambda i, j, core_ref: fn(i, j, pick(core_ref)))
    o_spec = pl.BlockSpec(o_block, wrap(o_map))
    extra = [] if received is None else [o_spec]
    operands = [] if received is None else [received]
    return pl.pallas_call(
        body, name=name,
        grid_spec=pltpu.PrefetchScalarGridSpec(
            num_scalar_prefetch=1, grid=grid,
            in_specs=[pl.BlockSpec(a_block, wrap(a_map)), pl.BlockSpec(b_block, wrap(b_map))] + extra + [ANY] * n_after,
            out_specs=o_spec),
        out_shape=out_shape,
        compiler_params=_params("parallel", "parallel"),
    )(core, a, b, *operands, *after)


def _down_half_grad(name, a, d_m, core, home, received=None, after=()):
    s, f = a.shape
    d = d_m.shape[1]
    r = f // N_DEV
    tn = min(TILE_N, d)
    return _half_grad(name, a, d_m, core, home, received, after, grid=(N_CHIP, d // tn),
                      a_block=(s, r), a_map=lambda k, j, p: (0, 2 * k + p),
                      b_block=(s, tn), b_map=lambda k, j, p: (0, j),
                      o_block=(None, r, tn), o_map=lambda k, j, p: (k, 0, j),
                      out_shape=jax.ShapeDtypeStruct((N_CHIP, r, d), BF16))


def _up_half_grad(name, h2, d_up, core, home, received=None, after=()):
    s, d = h2.shape
    fb = d_up.shape[1] // N_DEV
    tm = min(TILE_M, d)
    return _half_grad(name, h2, d_up, core, home, received, after, grid=(d // tm, N_CHIP),
                      a_block=(s, tm), a_map=lambda i, k, p: (0, i),
                      b_block=(s, fb), b_map=lambda i, k, p: (0, 2 * k + p),
                      o_block=(None, tm, fb), o_map=lambda i, k, p: (k, i, 0),
                      out_shape=jax.ShapeDtypeStruct((N_CHIP, d, fb), BF16))


def _in_pad(in_width):
    return -(-in_width // LANE) * LANE


def _join_col_shards(blocks):
    n, r, w = blocks.shape
    rows = min(ROWS, r)
    width = _in_pad(n * w)

    def body(x_ref, o_ref):
        tail = [jnp.zeros((rows, width - n * w), o_ref.dtype)] if width > n * w else []
        o_ref[...] = jnp.concatenate([x_ref[j] for j in range(n)] + tail, axis=1)

    return pl.pallas_call(
        body, name="join_col_shards", grid=(r // rows,),
        in_specs=[pl.BlockSpec((n, rows, w), lambda i: (0, i, 0))], out_specs=_row_spec(rows, width),
        out_shape=jax.ShapeDtypeStruct((r, width), blocks.dtype),
        compiler_params=_params("parallel"),
    )(blocks)


def _permute_q_cols(w_uq, n_heads):
    r = w_uq.shape[0]
    w3 = w_uq.reshape(r, n_heads, QK)
    return jnp.concatenate([w3[:, :, :HEAD].reshape(r, n_heads * HEAD), w3[:, :, HEAD:].reshape(r, n_heads * ROPE)], axis=1)


def _unpermute_q_rows(wt, n_heads):
    r = wt.shape[1]
    nope = wt[:n_heads * HEAD].reshape(n_heads, HEAD, r)
    rope = wt[n_heads * HEAD:].reshape(n_heads, ROPE, r)
    return jnp.concatenate([nope, rope], axis=1).reshape(n_heads * QK, r)


def _local_step(x, tgt, gains, weights, grads, first_after=()):
    pre_mix_g, q_norm_g, kv_norm_g, conv_out_g, attn_out_g, post_mix_g, pre_mlp_g, post_mlp_g = gains
    s, d = x.shape
    conv_width = conv_out_g.shape[1]
    n_groups = conv_width // HEAD
    r_q, r_kv = q_norm_g.shape[1], kv_norm_g.shape[1]
    n_heads = attn_out_g.shape[1] // HEAD
    c_q0 = 3 * conv_width
    c_kv0 = c_q0 + r_q
    c_kr0 = c_kv0 + r_kv
    in_pad = _in_pad(c_kr0 + ROPE)
    tn_in = in_pad // 5 if in_pad % (5 * LANE) == 0 else LANE
    tables = _rope_tables(s, n_heads)

    h1 = _rms_fwd("pre_mix_norm", x, pre_mix_g, after=first_after)
    weights.forward(0, (h1,))
    weights.relay(0, tables)
    w_in_p, conv_w = weights.ready(0, ())
    proj = _mm_nn("in_proj", h1, w_in_p, F32, TILE_M, tn_in)
    y_conv = _conv_fwd(proj, conv_w, conv_out_g, n_groups, conv_width + n_heads * HEAD)
    qn = _rms_fwd("q_norm", proj, q_norm_g, cols=(c_q0, r_q))
    kvn = _rms_fwd("kv_norm", proj, kv_norm_g, cols=(c_kv0, r_kv))
    weights.forward(1, (y_conv, qn, kvn))
    w_uq_p, w_ukv, w_o = weights.ready(1, ())
    q = _mm_nn("q_up", qn, w_uq_p, F32, TILE_M, TILE_N)
    kv = _mm_nn("kv_up", kvn, w_ukv, F32, TILE_M, TILE_N)
    qh, kh, vh = _pack_heads(q, kv, proj, c_kr0, tables, n_heads, after=weights.forward(2, (q, kv)))
    o, mix = _attn_fwd(qh, kh, vh, attn_out_g, y_conv, conv_width)
    y = _mm_nn("out_proj", mix, w_o, F32, TILE_M, TILE_N, after=weights.forward(3, (mix,)))
    x2, h2 = _mid_fwd(x, y, post_mix_g, pre_mlp_g)
    weights.relay(2, (h2,))
    (w_up,) = weights.ready(2, ())
    a, r = _up_fwd(h2, w_up)
    weights.relay(3, (a,))
    (w_down,) = weights.ready(3, ())
    m = _down_fwd(a, w_down)

    d_out, d_m, dg_post_mlp, loss_part = _head(m, x2, tgt, post_mlp_g)
    core = grads.core
    away = _down_half_grad("down_bwd_w_away", a, d_m, core, home=False)
    d_up = _down_bwd_act(d_m, w_down, r, after=grads.send_away(0, away))
    sums = _down_half_grad("down_bwd_w_home", a, d_m, core, home=True, received=grads.received(0, (d_up,)))
    away = _up_half_grad("up_bwd_w_away", h2, d_up, core, home=False, after=grads.send_sums(0, (sums,)))
    d_h2 = _up_bwd_act(d_up, w_up, after=grads.send_away(1, away))
    sums = _up_half_grad("up_bwd_w_home", h2, d_up, core, home=True, received=grads.received(1, (d_h2,)))
    d_x2, d_y, dg_pre_mlp, dg_post_mix = _mid_bwd(x2, y, d_out, d_h2, pre_mlp_g, post_mix_g, after=grads.send_sums(1, (sums,)))
    d_mix = _mm_nt("out_proj_bwd_act", d_y, w_o, F32, TILE_M, TILE_N)
    gw_o = _mm_tn("out_proj_bwd_w", mix, d_y, BF16, TILE_M, TILE_N)
    dqh, dkh, dvh, dg_attn = _attn_bwd(qh, kh, vh, o, d_mix, attn_out_g, conv_width, after=grads.full(2, (gw_o,)))
    d_q, d_kv, d_kr = _unpack_heads(dqh, dkh, dvh, tables, n_heads)
    d_qn = _mm_nt("q_up_bwd_act", d_q, w_uq_p, F32, TILE_M, TILE_N)
    d_kvn = _mm_nt("kv_up_bwd_act", d_kv, w_ukv, F32, TILE_M, TILE_N)
    gw_uq_t = _mm_tn("q_up_bwd_w", d_q, qn, F32, TILE_M, TILE_N)
    gw_ukv = _mm_tn("kv_up_bwd_w", kvn, d_kv, BF16, TILE_M, TILE_N)
    d_cq, dg_q = _rms_bwd_call("q_norm_bwd", proj, q_norm_g, d_qn, BF16, cols=(c_q0, r_q), after=grads.full(3, (gw_uq_t, gw_ukv)))
    d_ckv, dg_kv = _rms_bwd_call("kv_norm_bwd", proj, kv_norm_g, d_kvn, BF16, cols=(c_kv0, r_kv))
    d_u, d_b, d_c, dg_conv, dw_conv = _conv_bwd(proj, d_mix, conv_w, conv_out_g, n_groups)
    d_proj = jnp.concatenate([d_u, d_b, d_c, d_cq, d_ckv, d_kr[:, :in_pad - c_kr0]], axis=1)
    gw_in_t = _mm_tn("in_proj_bwd_w", d_proj, h1, F32, tn_in, TILE_N)
    d_h1 = _mm_nt("in_proj_bwd_act", d_proj, w_in_p, F32, TILE_M, 512, after=grads.send_away(4, gw_in_t))
    grad_x, dg_pre_mix = _first_bwd(x, pre_mix_g, d_h1, d_x2, after=grads.full(4, (gw_in_t,), received=(d_h1,)))

    small = [dg_pre_mix, dg_q, dg_kv, dg_conv, dg_attn, dg_post_mix, dg_pre_mlp, dg_post_mlp,
             dw_conv[0], dw_conv[1], dw_conv[2], loss_part]
    return grad_x, jnp.concatenate(small, axis=1)


HBM = pl.BlockSpec(memory_space=pltpu.HBM)
SEM = pl.BlockSpec(memory_space=pltpu.SEMAPHORE)
IN_VMEM = pl.BlockSpec(memory_space=pltpu.VMEM)
SPLIT = pltpu.CompilerParams(has_side_effects=pltpu.SideEffectType.DATAFLOW_SIDE_EFFECTING)


def _in_hbm(a):
    return pltpu.with_memory_space_constraint(a, pltpu.HBM)


def _hbm_like(a):
    return pltpu.HBM(a.shape, a.dtype)


def _place():
    x, y, c = lax.axis_index("x"), lax.axis_index("y"), lax.axis_index("c")
    other_chips = [(1 - x, y), (x, 1 - y), (1 - x, 1 - y)]
    return x, y, c, other_chips


def _block(px, py, pc):
    return 4 * px + 2 * py + pc


def _await(block, sem):
    pltpu.make_async_copy(block, block, sem).wait()


def _relay_route(x, y, c):
    came_from = ((1 - x) * (1 - c) + x * c, y * (1 - c) + (1 - y) * c)
    goes_to = (x * (1 - c) + (1 - x) * c, (1 - y) * (1 - c) + y * c)
    return came_from, goes_to


def _gather_start(name, shards, groups, relayed=(), after=()):
    n, ng = len(shards), len(groups)
    lands = [lax.empty((N_DEV, *a.shape), a.dtype) for a in shards]

    def body(*refs):
        src, land = refs[:n], refs[n:2 * n]
        sems, token = refs[2 * n + len(after):2 * n + len(after) + 2 * ng], refs[-1]
        x, y, c, chips = _place()
        targets = [(x, y, 1 - c)] + [(*chip, c) for chip in chips]
        for gi, group in enumerate(groups):
            for i, w in enumerate(group):
                for k, to in enumerate(targets[:3] if gi in relayed else targets):
                    pltpu.make_async_remote_copy(
                        src_ref=src[w], dst_ref=land[w].at[_block(x, y, c)],
                        send_sem=sems[2 * gi].at[4 * i + k], recv_sem=sems[2 * gi + 1].at[4 * i + k],
                        device_id=to, device_id_type=MESH).start()
        token[...] = jnp.zeros_like(token)

    sem_shapes = [pltpu.SemaphoreType.DMA((4 * len(g),)) for g in groups for _ in range(2)]
    out = pl.pallas_call(
        body, name=name,
        in_specs=[HBM] * (2 * n) + [ANY] * len(after),
        out_specs=[SEM] * (2 * ng) + [HBM] * (2 * n) + [IN_VMEM],
        out_shape=sem_shapes + [_hbm_like(a) for a in shards] + [_hbm_like(a) for a in lands]
        + [jax.ShapeDtypeStruct((SUBLANE, LANE), F32)],
        input_output_aliases={i: 2 * ng + i for i in range(2 * n)},
        compiler_params=SPLIT,
    )(*[_in_hbm(a) for a in shards], *[_in_hbm(a) for a in lands], *after)
    sems = [(out[2 * gi], out[2 * gi + 1]) for gi in range(ng)]
    return sems, out[2 * ng:2 * ng + n], out[2 * ng + n:2 * ng + 2 * n], out[-1]


def _gather_forward(name, shards, lands, send1, recv1, after, relayed=False):
    n = len(lands)

    def body(*refs):
        src, land = refs[:n], refs[n:2 * n]
        s1, r1 = refs[2 * n], refs[2 * n + 1]
        s2, r2 = refs[2 * n + 2 + len(after)], refs[2 * n + 3 + len(after)]
        x, y, c, chips = _place()
        me, sibling = (x, y, c), (x, y, 1 - c)
        for j, chip in enumerate(chips[:2] if relayed else chips):
            for i in range(n):
                blk = land[i].at[_block(*chip, c)]
                pltpu.make_async_remote_copy(src_ref=blk, dst_ref=blk, send_sem=s1.at[4 * i + 1 + j], recv_sem=r1.at[4 * i + 1 + j],
                                             device_id=me, device_id_type=MESH).wait_recv()
                pltpu.make_async_remote_copy(src_ref=blk, dst_ref=blk, send_sem=s2.at[3 * i + j], recv_sem=r2.at[3 * i + j],
                                             device_id=sibling, device_id_type=MESH).start()
        if relayed:
            came_from, goes_to = _relay_route(x, y, c)
            for i in range(n):
                blk = land[i].at[_block(*came_from, c)]
                pltpu.make_async_remote_copy(src_ref=blk, dst_ref=blk, send_sem=s2.at[3 * i + 2], recv_sem=r2.at[3 * i + 2],
                                             device_id=(*goes_to, c), device_id_type=MESH).start()
        for i in range(n):
            blk = land[i].at[_block(x, y, 1 - c)]
            pltpu.make_async_remote_copy(src_ref=blk, dst_ref=blk, send_sem=s1.at[4 * i], recv_sem=r1.at[4 * i],
                                         device_id=me, device_id_type=MESH).wait_recv()
            for k in range(3 if relayed else 4):
                pltpu.make_async_remote_copy(src_ref=src[i], dst_ref=land[i].at[_block(x, y, c)], send_sem=s1.at[4 * i + k],
                                             recv_sem=r1.at[4 * i + k], device_id=sibling, device_id_type=MESH).wait_send()

    sem = pltpu.SemaphoreType.DMA((3 * n,))
    out = pl.pallas_call(
        body, name=name,
        in_specs=[HBM] * (2 * n) + [SEM, SEM] + [ANY] * len(after),
        out_specs=[SEM, SEM] + [HBM] * n,
        out_shape=[sem, sem] + [_hbm_like(a) for a in lands],
        input_output_aliases={n + i: 2 + i for i in range(n)},
        compiler_params=SPLIT,
    )(*shards, *lands, send1, recv1, *after)
    return (out[0], out[1]), out[2:]


def _gather_relay_forward(name, lands, send2, recv2, after):
    n = len(lands)

    def body(*refs):
        land, s2, r2 = refs[:n], refs[n], refs[n + 1]
        s3, r3 = refs[n + 2 + len(after)], refs[n + 3 + len(after)]
        x, y, c, _ = _place()
        me, sibling = (x, y, c), (x, y, 1 - c)
        came_from, _ = _relay_route(x, y, c)
        for i in range(n):
            blk = land[i].at[_block(1 - x, 1 - y, c)]
            pltpu.make_async_remote_copy(src_ref=blk, dst_ref=blk, send_sem=s2.at[3 * i + 2], recv_sem=r2.at[3 * i + 2],
                                         device_id=me, device_id_type=MESH).wait_recv()
            pltpu.make_async_remote_copy(src_ref=blk, dst_ref=blk, send_sem=s3.at[i], recv_sem=r3.at[i],
                                         device_id=sibling, device_id_type=MESH).start()
            sent = land[i].at[_block(*came_from, c)]
            pltpu.make_async_remote_copy(src_ref=sent, dst_ref=sent, send_sem=s2.at[3 * i + 2], recv_sem=r2.at[3 * i + 2],
                                         device_id=me, device_id_type=MESH).wait_send()

    sem = pltpu.SemaphoreType.DMA((n,))
    out = pl.pallas_call(
        body, name=name,
        in_specs=[HBM] * n + [SEM, SEM] + [ANY] * len(after),
        out_specs=[SEM, SEM] + [HBM] * n,
        out_shape=[sem, sem] + [_hbm_like(a) for a in lands],
        input_output_aliases={i: 2 + i for i in range(n)},
        compiler_params=SPLIT,
    )(*lands, send2, recv2, *after)
    return (out[0], out[1]), out[2:]


def _gather_wait(name, lands, send2, recv2, after, relay_sems=None):
    n = len(lands)
    n_sems = 2 if relay_sems is None else 4

    def body(*refs):
        land, s2, r2 = refs[:n], refs[n], refs[n + 1]
        for i in range(n):
            for j in range(3 if relay_sems is None else 2):
                _await(land[i].at[0], r2.at[3 * i + j])
                _await(land[i].at[0], s2.at[3 * i + j])
            if relay_sems is not None:
                _await(land[i].at[0], refs[n + 3].at[i])
                _await(land[i].at[0], refs[n + 2].at[i])

    return pl.pallas_call(
        body, name=name,
        in_specs=[HBM] * n + [SEM] * n_sems + [ANY] * len(after), out_specs=[HBM] * n, out_shape=[_hbm_like(a) for a in lands],
        input_output_aliases={i: i for i in range(n)},
        compiler_params=SPLIT,
    )(*lands, send2, recv2, *(relay_sems or ()), *after)


def _pair_exchange(name, grads, shard_rows):
    n = len(grads)
    shapes = [(g.shape[1:] if r is None else (r, g.shape[1])) for g, r in zip(grads, shard_rows)]

    def body(*refs):
        ins, recv = refs[:n], refs[n:2 * n]
        send_sems, recv_sems = refs[2 * n:]
        x, y, c, _ = _place()
        sends = []
        for w in range(n):
            for k in range(N_CHIP):
                j, r = 2 * k + 1 - c, shard_rows[w]
                src = ins[w].at[j] if r is None else ins[w].at[pl.ds(pl.multiple_of(j * r, SUBLANE), r), :]
                sends.append(pltpu.make_async_remote_copy(
                    src_ref=src, dst_ref=recv[w].at[k],
                    send_sem=send_sems.at[w, k], recv_sem=recv_sems.at[w, k],
                    device_id=(x, y, 1 - c), device_id_type=MESH))
        for cp in sends:
            cp.start()
        for cp in sends:
            cp.wait()

    return pl.pallas_call(
        body, name=name,
        in_specs=[ANY] * n, out_specs=[ANY] * n,
        out_shape=[jax.ShapeDtypeStruct((N_CHIP, *shape), g.dtype) for g, shape in zip(grads, shapes)],
        scratch_shapes=[pltpu.SemaphoreType.DMA((n, N_CHIP))] * 2,
    )(*grads)


def _pair_sum_rows(name, grad, received, core):
    _, r, c = received.shape
    tc = _fit(c, 512)

    def body(core_ref, a_ref, b_ref, o_ref):
        o_ref[...] = (a_ref[...] + b_ref[...]).astype(o_ref.dtype)

    spec = pl.BlockSpec((None, r, tc), lambda k, i, core_ref: (k, 0, i))
    return pl.pallas_call(
        body, name=name,
        grid_spec=pltpu.PrefetchScalarGridSpec(
            num_scalar_prefetch=1, grid=(N_CHIP, c // tc),
            in_specs=[pl.BlockSpec((r, tc), lambda k, i, core_ref: (2 * k + core_ref[0], i)), spec],
            out_specs=spec),
        out_shape=jax.ShapeDtypeStruct(received.shape, BF16),
        compiler_params=_params("parallel", "parallel"),
    )(core, grad, received)


def _pair_sum(name, grad, received, core):
    _, r, c = received.shape
    rows = min(ROWS, r)
    assert r % rows == 0

    def body(core_ref, a_ref, b_ref, o_ref):
        o_ref[...] = (a_ref[...].astype(F32) + b_ref[...].astype(F32)).astype(o_ref.dtype)

    spec = pl.BlockSpec((None, rows, c), lambda k, i, core_ref: (k, i, 0))
    return pl.pallas_call(
        body, name=name,
        grid_spec=pltpu.PrefetchScalarGridSpec(
            num_scalar_prefetch=1, grid=(N_CHIP, r // rows),
            in_specs=[pl.BlockSpec((None, None, rows, c), lambda k, i, core_ref: (k, core_ref[0], i, 0)), spec],
            out_specs=spec),
        out_shape=jax.ShapeDtypeStruct(received.shape, received.dtype),
        compiler_params=_params("parallel", "parallel"),
    )(core, grad.reshape(N_CHIP, 2, r, c), received)


def _away_shard(src, k, c, shard_rows):
    if shard_rows is None:
        return src.at[k]
    return src.at[pl.ds(pl.multiple_of((2 * k + 1 - c) * shard_rows, SUBLANE), shard_rows), :]


def _pair_send_start(name, away, shard_rows=None):
    shape = away.shape if shard_rows is None else (N_CHIP, shard_rows, away.shape[1])
    land = lax.empty(shape, away.dtype)

    def body(src, dst, send, recv, src_thru, dst_thru, token):
        x, y, c, _ = _place()
        for k in range(N_CHIP):
            pltpu.make_async_remote_copy(src_ref=_away_shard(src, k, c, shard_rows), dst_ref=dst.at[k], send_sem=send.at[k],
                                         recv_sem=recv.at[k], device_id=(x, y, 1 - c), device_id_type=MESH).start()
        token[...] = jnp.zeros_like(token)

    sem = pltpu.SemaphoreType.DMA((N_CHIP,))
    out = pl.pallas_call(
        body, name=name,
        in_specs=[HBM, HBM], out_specs=[SEM, SEM, HBM, HBM, IN_VMEM],
        out_shape=[sem, sem, _hbm_like(away), _hbm_like(land), jax.ShapeDtypeStruct((SUBLANE, LANE), F32)],
        input_output_aliases={0: 2, 1: 3},
        compiler_params=SPLIT,
    )(_in_hbm(away), _in_hbm(land))
    return (out[0], out[1]), out[2], out[3], out[4]


def _pair_send_wait(name, sems, src, land, after, shard_rows=None):
    def body(src_ref, dst_ref, send, recv, *rest):
        for k in range(N_CHIP):
            _await(dst_ref.at[k], send.at[k])
            _await(dst_ref.at[k], recv.at[k])

    return pl.pallas_call(
        body, name=name,
        in_specs=[HBM, HBM, SEM, SEM] + [ANY] * len(after), out_specs=HBM, out_shape=_hbm_like(land),
        input_output_aliases={1: 0},
        compiler_params=SPLIT,
    )(src, land, *sems, *after)


def _chip_send_start(name, sums):
    n = len(sums)
    lands = [lax.empty(a.shape, a.dtype) for a in sums]

    def body(*refs):
        src, land = refs[:n], refs[n:2 * n]
        send, recv, token = refs[2 * n], refs[2 * n + 1], refs[-1]
        x, y, c, chips = _place()
        for w in range(n):
            for j, (px, py) in enumerate(chips):
                pltpu.make_async_remote_copy(
                    src_ref=src[w].at[2 * px + py], dst_ref=land[w].at[2 * x + y],
                    send_sem=send.at[3 * w + j], recv_sem=recv.at[3 * w + j],
                    device_id=(px, py, c), device_id_type=MESH).start()
        token[...] = jnp.zeros_like(token)

    sem = pltpu.SemaphoreType.DMA((3 * n,))
    out = pl.pallas_call(
        body, name=name,
        in_specs=[HBM] * (2 * n),
        out_specs=[SEM, SEM] + [HBM] * (2 * n) + [IN_VMEM],
        out_shape=[sem, sem] + [_hbm_like(a) for a in sums] + [_hbm_like(a) for a in lands]
        + [jax.ShapeDtypeStruct((SUBLANE, LANE), F32)],
        input_output_aliases={i: 2 + i for i in range(2 * n)},
        compiler_params=SPLIT,
    )(*[_in_hbm(a) for a in sums], *[_in_hbm(a) for a in lands])
    return (out[0], out[1]), out[2:2 + n], out[2 + n:2 + 2 * n], out[-1]


def _chip_send_wait(name, groups, after):
    counts = [len(g[1]) for g in groups]
    n = sum(counts)

    def body(*refs):
        land = refs[n:2 * n]
        sems = refs[2 * n:2 * n + 2 * len(groups)]
        w = 0
        for gi, count in enumerate(counts):
            for i in range(count):
                for j in range(3):
                    _await(land[w].at[0], sems[2 * gi].at[3 * i + j])
                    _await(land[w].at[0], sems[2 * gi + 1].at[3 * i + j])
                w += 1

    sums = [a for g in groups for a in g[1]]
    lands = [a for g in groups for a in g[2]]
    sems = [s for g in groups for s in g[0]]
    return pl.pallas_call(
        body, name=name,
        in_specs=[HBM] * (2 * n) + [SEM] * len(sems) + [ANY] * len(after),
        out_specs=[HBM] * n, out_shape=[_hbm_like(a) for a in lands],
        input_output_aliases={n + i: i for i in range(n)},
        compiler_params=SPLIT,
    )(*sums, *lands, *sems, *after)


def _small_all_reduce(part, after=()):
    _, w = part.shape

    def body(p_ref, *rest):
        o_ref, buf, send_sems, recv_sems = rest[len(after):]
        x, y, c, _ = _place()
        me = 4 * x + 2 * y + c
        buf[me] = jnp.sum(p_ref[...], axis=0, keepdims=True)
        copies = []
        for k in range(1, N_DEV):
            dx, dy, dc = (k >> 2) & 1, (k >> 1) & 1, k & 1
            copies.append(pltpu.make_async_remote_copy(
                src_ref=buf.at[me], dst_ref=buf.at[me], send_sem=send_sems.at[k - 1], recv_sem=recv_sems.at[k - 1],
                device_id=(x ^ dx, y ^ dy, c ^ dc), device_id_type=MESH))
        for cp in copies:
            cp.start()
        for cp in copies:
            cp.wait()
        tot = buf[0]
        for d in range(1, N_DEV):
            tot = tot + buf[d]
        o_ref[...] = tot
        loss = jnp.sum(tot[:, w - LANE:], axis=1, keepdims=True)
        o_ref[:, w - LANE:] = jnp.broadcast_to(loss, (1, LANE))

    return pl.pallas_call(
        body, name="small_all_reduce",
        in_specs=[IN_VMEM] + [ANY] * len(after), out_specs=IN_VMEM,
        out_shape=jax.ShapeDtypeStruct((1, w), F32),
        scratch_shapes=[pltpu.VMEM((N_DEV, 1, w), F32), pltpu.SemaphoreType.DMA((N_DEV - 1,)), pltpu.SemaphoreType.DMA((N_DEV - 1,))],
        compiler_params=pltpu.CompilerParams(vmem_limit_bytes=VMEM_LIMIT_BYTES),
    )(part, *after)


def _adamw(w, g, m, v):
    m = ADAM_B1 * m + (1.0 - ADAM_B1) * g
    v = ADAM_B2 * v + (1.0 - ADAM_B2) * (g * g)
    m_hat = m / (1.0 - ADAM_B1 ** ADAM_STEP)
    v_hat = v / (1.0 - ADAM_B2 ** ADAM_STEP)
    delta = -ADAM_LR * (m_hat / (jnp.sqrt(v_hat) + ADAM_EPS) + ADAM_WD * w)
    return delta, m, v


def _sum_adam(name, parts, sums, chip, w, m, v, after=()):
    _, r, c = w.shape
    n_after = len(after)
    by_rows = r % ROWS == 0 or r < ROWS
    tr, tc = (min(ROWS, r), c) if by_rows else (r, _fit(c, 512))
    at = (lambda i: (i, 0)) if by_rows else (lambda i: (0, i))

    def body(chip_ref, p_ref, own_ref, w_ref, m_ref, v_ref, *rest):
        g_ref, d_ref, mo_ref, vo_ref = rest[n_after:]
        g = None
        for k in range(N_CHIP):
            term = jnp.where(chip_ref[0] == k, own_ref[...], p_ref[k]).astype(F32)
            g = term if g is None else g + term
        g_ref[...] = g
        d_ref[...], mo_ref[...], vo_ref[...] = _adamw(w_ref[...], g, m_ref[...], v_ref[...])

    blk = pl.BlockSpec((None, tr, tc), lambda i, chip_ref: (0, *at(i)))
    out = jax.ShapeDtypeStruct((1, r, c), F32)
    return pl.pallas_call(
        body, name=name,
        grid_spec=pltpu.PrefetchScalarGridSpec(
            num_scalar_prefetch=1, grid=(r // tr if by_rows else c // tc,),
            in_specs=[pl.BlockSpec((N_CHIP, tr, tc), lambda i, chip_ref: (0, *at(i))),
                      pl.BlockSpec((None, tr, tc), lambda i, chip_ref: (chip_ref[0], *at(i))), blk, blk, blk]
            + [ANY] * n_after,
            out_specs=[blk] * 4),
        out_shape=[out] * 4,
        compiler_params=_params("parallel"),
    )(chip, parts, sums, w, m, v, *after)


def _adam_gains(total, ws, ms, vs):
    n = len(ws)
    widths = [w.shape[1] for w in ws]

    def body(t_ref, *refs):
        w_refs, m_refs, v_refs, outs = refs[:n], refs[n:2 * n], refs[2 * n:3 * n], refs[3 * n:]
        off = 0
        for i in range(n):
            g = t_ref[:, off:off + widths[i]]
            off += widths[i]
            g_ref, d_ref, mo_ref, vo_ref = outs[4 * i:4 * i + 4]
            g_ref[...] = g
            d_ref[...], mo_ref[...], vo_ref[...] = _adamw(w_refs[i][...], g, m_refs[i][...], v_refs[i][...])

    out = pl.pallas_call(
        body, name="adam_gains",
        out_shape=[jax.ShapeDtypeStruct(w.shape, F32) for w in ws for _ in range(4)],
    )(total, *ws, *ms, *vs)
    return [tuple(out[4 * i:4 * i + 4]) for i in range(n)]


def _adam_taps(total, first_col, device, w, m, v):
    _, n_taps, cw = w.shape
    col_block = lambda t, dev: (0, first_col // cw + t * N_DEV + dev[0])
    tap = pl.BlockSpec((None, 1, cw), lambda t, dev: (t, 0, 0))

    def body(dev_ref, t_ref, w_ref, m_ref, v_ref, g_ref, d_ref, mo_ref, vo_ref):
        g = t_ref[...]
        g_ref[...] = g
        d_ref[...], mo_ref[...], vo_ref[...] = _adamw(w_ref[...], g, m_ref[...], v_ref[...])

    shape3 = (n_taps, 1, cw)
    out = pl.pallas_call(
        body, name="adam_taps",
        grid_spec=pltpu.PrefetchScalarGridSpec(
            num_scalar_prefetch=1, grid=(n_taps,),
            in_specs=[pl.BlockSpec((1, cw), col_block), tap, tap, tap], out_specs=[tap] * 4),
        out_shape=[jax.ShapeDtypeStruct(shape3, F32)] * 4,
    )(device, total, w.reshape(shape3), m.reshape(shape3), v.reshape(shape3))
    return tuple(o.reshape(w.shape) for o in out)


def kernel(x, pre_mix_g, w_in, conv_w, q_norm_g, w_uq, kv_norm_g, w_ukv, conv_out_g, attn_out_g, w_o, post_mix_g, pre_mlp_g, w_up, w_down, post_mlp_g, loss_target, m_pre_mix_g, m_w_in, m_conv_w, m_q_norm_g, m_w_uq, m_kv_norm_g, m_w_ukv, m_conv_out_g, m_attn_out_g, m_w_o, m_post_mix_g, m_pre_mlp_g, m_w_up, m_w_down, m_post_mlp_g, v_pre_mix_g, v_w_in, v_conv_w, v_q_norm_g, v_w_uq, v_kv_norm_g, v_w_ukv, v_conv_out_g, v_attn_out_g, v_w_o, v_post_mix_g, v_pre_mlp_g, v_w_up, v_w_down, v_post_mlp_g):
    me = 4 * lax.axis_index("x") + 2 * lax.axis_index("y") + lax.axis_index("c")
    core = lax.axis_index("c").astype(jnp.int32).reshape(1)
    chip = (2 * lax.axis_index("x") + lax.axis_index("y")).astype(jnp.int32).reshape(1)
    gains = (pre_mix_g, q_norm_g, kv_norm_g, conv_out_g, attn_out_g, post_mix_g, pre_mlp_g, post_mlp_g)
    gain_m = (m_pre_mix_g, m_q_norm_g, m_kv_norm_g, m_conv_out_g, m_attn_out_g, m_post_mix_g, m_pre_mlp_g, m_post_mlp_g)
    gain_v = (v_pre_mix_g, v_q_norm_g, v_kv_norm_g, v_conv_out_g, v_attn_out_g, v_post_mix_g, v_pre_mlp_g, v_post_mlp_g)
    names = ("w_in", "w_uq", "w_ukv", "w_o", "w_up", "w_down")
    big = dict(zip(names, (w_in, w_uq, w_ukv, w_o, w_up, w_down)))
    big_m = dict(zip(names, (m_w_in, m_w_uq, m_w_ukv, m_w_o, m_w_up, m_w_down)))
    big_v = dict(zip(names, (v_w_in, v_w_uq, v_w_ukv, v_w_o, v_w_up, v_w_down)))
    n_heads = attn_out_g.shape[1] // HEAD
    n_taps = conv_w.shape[1]

    gathered = ("w_in", "conv", "w_uq", "w_ukv", "w_o", "w_up", "w_down")
    gather_groups = ((0, 1), (2, 3, 4), (5,), (6,))
    taps = jnp.pad(conv_w[0], ((0, SUBLANE - n_taps), (0, 0)))
    relayed_groups = (0, 2, 3)
    sems1, shards, lands, token = _gather_start("gather_start_first", [w_in[0].astype(BF16), taps], ((0, 1),), relayed=(0,))
    sems1, shards, lands = list(sems1), list(shards), list(lands)
    behind = token[0, 0]
    rest = [(big[nm][0] + behind).astype(BF16) for nm in gathered[2:]]

    def start_rest(after):
        sems_b, shards_b, lands_b, started = _gather_start("gather_start_rest", rest, ((0, 1, 2), (3,), (4,)), relayed=(1, 2),
                                                          after=after)
        sems1.extend(sems_b)
        shards.extend(shards_b)
        lands.extend(lands_b)
        return started

    cols = lambda a: jnp.concatenate([a[j] for j in range(N_DEV)], axis=1)
    rows = lambda a: a.reshape(N_DEV * a.shape[1], a.shape[2])
    ready = {
        "w_in": _join_col_shards,
        "conv": lambda a: cols(a)[:n_taps],
        "w_uq": lambda a: _permute_q_cols(cols(a), n_heads),
        "w_ukv": cols, "w_o": rows, "w_up": lambda a: a, "w_down": rows,
    }

    class Weights:
        def __init__(self):
            self.passed, self.relayed = {}, {}

        def forward(self, group, after):
            idx = gather_groups[group]
            if group == 0:
                after = (*after, *rest)
            self.passed[group] = _gather_forward(f"gather_forward_{group}", [shards[i] for i in idx], [lands[i] for i in idx],
                                                 *sems1[group], after, relayed=group in relayed_groups)
            return tuple(self.passed[group][1])

        def relay(self, group, after):
            sems2, mid = self.passed[group]
            self.relayed[group], mid = _gather_relay_forward(f"gather_relay_{group}", mid, *sems2, after)
            self.passed[group] = (sems2, mid)
            if group == 0:
                start_rest(tuple(mid))
            return tuple(mid)

        def ready(self, group, after):
            sems2, mid = self.passed[group]
            full = _gather_wait(f"gather_wait_{group}", mid, *sems2, after, relay_sems=self.relayed.get(group))
            out = []
            for i, a in zip(gather_groups[group], full):
                a = lax.dynamic_update_index_in_dim(a, shards[i], me, 0)
                out.append(ready[gathered[i]](a))
            return out

    weights = Weights()

    col_blocks = lambda g: g.reshape(g.shape[0], N_DEV, g.shape[1] // N_DEV).transpose(1, 0, 2)
    row_blocks = lambda g: g.reshape(N_DEV, g.shape[0] // N_DEV, g.shape[1])
    grad_groups = (("w_down",), ("w_up",), ("w_o",), ("w_uq", "w_ukv"), ("w_in",))
    transposed = {"w_in": w_in.shape[2], "w_uq": w_uq.shape[2]}
    to_blocks = {
        "w_in": lambda g: g, "w_uq": lambda g: _unpermute_q_rows(g, n_heads),
        "w_ukv": col_blocks, "w_o": row_blocks, "w_up": lambda g: g, "w_down": row_blocks,
    }
    in_flight = []

    class Grads:
        def __init__(self):
            self.core = core
            self.away = {}

        def send_sums(self, group, sums):
            sems, sums, parts, tok = _chip_send_start(f"chip_send_start_{group}", list(sums))
            in_flight.append((sems, sums, parts))
            return (tok,)

        def full(self, group, arrays, received=None):
            nms = grad_groups[group]
            if received is None:
                blocks = [to_blocks[nm](g) for nm, g in zip(nms, arrays)]
                got = _pair_exchange(f"pair_exchange_{group}", blocks, [transposed.get(nm) for nm in nms])
            else:
                blocks, got = [self.away[group][1]], [self.received(group, received)]
            sums = [(_pair_sum_rows if nm in transposed else _pair_sum)(f"pair_sum_{nm}", g, r, core)
                    for nm, g, r in zip(nms, blocks, got)]
            return self.send_sums(group, sums)

        def send_away(self, group, half):
            nm = grad_groups[group][0]
            rows = transposed.get(nm)
            sems, src, land, tok = _pair_send_start(f"pair_send_start_{group}", half if rows is None else to_blocks[nm](half), rows)
            self.away[group] = (sems, src, land, rows)
            return (tok,)

        def received(self, group, after):
            sems, src, land, rows = self.away[group]
            return _pair_send_wait(f"pair_send_wait_{group}", sems, src, land, after, rows)

    grad_x, small = _local_step(x[0], loss_target[0], gains, weights, Grads(), first_after=(token,))

    big_out = {}

    def update(tag, first, last, after):
        groups = in_flight[first:last]
        parts = _chip_send_wait("chip_send_wait_" + tag, groups, after)
        nms = [nm for grp in grad_groups[first:last] for nm in grp]
        sums = [a for _, s, _ in groups for a in s]
        for nm, p, s in zip(nms, parts, sums):
            view = (lambda a: jnp.swapaxes(a, 1, 2)) if nm in transposed else (lambda a: a)
            out = _sum_adam("adam_" + nm, p, s, chip, view(big[nm]), view(big_m[nm]), view(big_v[nm]), after=after)
            after = (out[0],)
            big_out[nm] = [view(o) for o in out]
        return after

    after = update("early", 0, len(in_flight) - 1, (grad_x,))
    total = _small_all_reduce(small, after=after)
    update("late", len(in_flight) - 1, len(in_flight), (total,))
    big_out = [big_out[nm] for nm in names]

    gain_out = _adam_gains(total, gains, gain_m, gain_v)
    taps_out = _adam_taps(total, sum(g.shape[1] for g in gains), me.astype(jnp.int32).reshape(1), conv_w, m_conv_w, v_conv_w)
    loss = total[0, total.shape[1] - 1]

    order = (0, "w_in", "conv", 1, "w_uq", 2, "w_ukv", 3, 4, "w_o", 5, 6, "w_up", "w_down", 7)
    by_name = dict(zip(names, big_out))
    outs = [loss, grad_x[None]]
    for kind in range(4):
        for item in order:
            if item == "conv":
                outs.append(taps_out[kind])
            elif isinstance(item, int):
                outs.append(gain_out[item][kind])
            else:
                outs.append(by_name[item][kind])
    return tuple(outs)
```

```python
import math

import jax
import jax.numpy as jnp
from jax import lax
from jax.experimental import pallas as pl
from jax.experimental.pallas import tpu as pltpu

F32 = jnp.float32
BF16 = jnp.bfloat16

EPS = 1e-6
NEG_INF = -1e30
HEAD = 128
ROPE = 64
QK = HEAD + ROPE
CHUNK = 64
ROPE_THETA = 10000.0
ADAM_LR, ADAM_B1, ADAM_B2, ADAM_EPS, ADAM_WD, ADAM_STEP = 0.001, 0.9, 0.999, 1e-08, 0.01, 10

LANE = 128
SUBLANE = 8
VMEM_LIMIT_BYTES = 56 * 1024 * 1024

N_DEV = 8
N_CHIP = 4
MESH = pl.DeviceIdType.MESH


def _params(*sem):
    return pltpu.CompilerParams(dimension_semantics=sem, vmem_limit_bytes=VMEM_LIMIT_BYTES)


ANY = pl.BlockSpec(memory_space=pl.ANY)


def _call(body, *, in_specs, after=(), **kw):
    n_in, n_after = len(in_specs), len(after)

    def ordered(*refs):
        body(*refs[:n_in], *refs[n_in + n_after:])

    call = pl.pallas_call(ordered, in_specs=[*in_specs, *[ANY] * n_after], **kw)
    return lambda *operands: call(*operands, *after)


def _sublane_sum(v):
    r, w = v.shape
    return jnp.sum(v.reshape(r // SUBLANE, SUBLANE, w), axis=0)


def _rstd(x):
    return lax.rsqrt(jnp.mean(x * x, axis=-1, keepdims=True) + EPS)


def _rms_bwd(x, g, dy):
    r = _rstd(x)
    xh = x * r
    dxh = dy * g
    dx = r * (dxh - xh * jnp.mean(dxh * xh, axis=-1, keepdims=True))
    return dx, dy * xh


def _accumulate(ref, val, step):
    @pl.when(step == 0)
    def _():
        ref[...] = val

    @pl.when(step > 0)
    def _():
        ref[...] += val


NN = ((1,), (0,))
NT = ((1,), (1,))
TN = ((0,), (0,))


def _matmul(name, a, b, *, grid, a_spec, b_spec, out_shape, out_specs, contract, nk=1, acc_shape=None,
            extras=(), extra_specs=(), epilogue=None, after=()):
    multi = isinstance(out_shape, (tuple, list))
    out_shapes = tuple(out_shape) if multi else (out_shape,)
    n_out = len(out_shapes)
    n_extra = len(extras)

    def body(a_ref, b_ref, *rest):
        x_refs = rest[:n_extra]
        o_refs = rest[n_extra:n_extra + n_out]

        def emit(acc):
            vals = epilogue(acc, *[r[...] for r in x_refs]) if epilogue else (acc,)
            for r, v in zip(o_refs, vals):
                r[...] = v.astype(r.dtype)

        p = lax.dot_general(a_ref[...], b_ref[...], (contract, ((), ())), preferred_element_type=F32)
        if nk == 1:
            emit(p)
        else:
            acc_ref = rest[n_extra + n_out]
            k = pl.program_id(2)
            _accumulate(acc_ref, p, k)

            @pl.when(k == nk - 1)
            def _():
                emit(acc_ref[...])

    sem = ("parallel", "parallel") + (("arbitrary",) if nk > 1 else ())
    return _call(
        body, name=name, grid=grid, after=after,
        in_specs=[a_spec, b_spec, *extra_specs],
        out_specs=out_specs,
        out_shape=out_shape,
        scratch_shapes=[pltpu.VMEM(acc_shape, F32)] if nk > 1 else [],
        compiler_params=_params(*sem),
    )(a, b, *extras)


def _fit(n, tile):
    if n <= tile:
        return n
    t = tile - tile % LANE
    while n % t:
        t -= LANE
    return t


def _mm_nn(name, a, b, out_dtype, tm, tn, after=()):
    m, k = a.shape
    n = b.shape[1]
    tm, tn = _fit(m, tm), _fit(n, tn)
    return _matmul(name, a, b, grid=(m // tm, n // tn), after=after,
                   a_spec=pl.BlockSpec((tm, k), lambda i, j: (i, 0)),
                   b_spec=pl.BlockSpec((k, tn), lambda i, j: (0, j)),
                   out_shape=jax.ShapeDtypeStruct((m, n), out_dtype),
                   out_specs=pl.BlockSpec((tm, tn), lambda i, j: (i, j)), contract=NN)


def _mm_nt(name, a, b, out_dtype, tm, tn, after=()):
    m, k = a.shape
    n = b.shape[0]
    tm, tn = _fit(m, tm), _fit(n, tn)
    return _matmul(name, a, b, grid=(m // tm, n // tn), after=after,
                   a_spec=pl.BlockSpec((tm, k), lambda i, j: (i, 0)),
                   b_spec=pl.BlockSpec((tn, k), lambda i, j: (j, 0)),
                   out_shape=jax.ShapeDtypeStruct((m, n), out_dtype),
                   out_specs=pl.BlockSpec((tm, tn), lambda i, j: (i, j)), contract=NT)


def _mm_tn(name, a, b, out_dtype, tm, tn):
    s, m = a.shape
    n = b.shape[1]
    tm, tn = _fit(m, tm), _fit(n, tn)
    return _matmul(name, a, b, grid=(m // tm, n // tn),
                   a_spec=pl.BlockSpec((s, tm), lambda i, j: (0, i)),
                   b_spec=pl.BlockSpec((s, tn), lambda i, j: (0, j)),
                   out_shape=jax.ShapeDtypeStruct((m, n), out_dtype),
                   out_specs=pl.BlockSpec((tm, tn), lambda i, j: (i, j)), contract=TN)


ROWS = 256


def _row_spec(rows, width):
    return pl.BlockSpec((rows, width), lambda i: (i, 0))


def _fixed_spec(rows, width):
    return pl.BlockSpec((rows, width), lambda i: (0, 0))


def _column_pieces(rows, start, width):
    piece = math.gcd(start, width)
    assert piece % LANE == 0
    return [pl.BlockSpec((rows, piece), lambda i, b=start // piece + p: (i, b)) for p in range(width // piece)]


def _rms_fwd(name, x, g, cols=None, after=()):
    s = x.shape[0]
    start, w = cols or (0, x.shape[1])
    rows = min(ROWS, s)
    pieces = _column_pieces(rows, start, w) if cols else [_row_spec(rows, w)]
    n = len(pieces)

    def body(*refs):
        g_ref, o_ref = refs[n:]
        xv = refs[0][...] if n == 1 else jnp.concatenate([r[...] for r in refs[:n]], axis=1)
        o_ref[...] = (xv * _rstd(xv) * g_ref[...]).astype(o_ref.dtype)

    return _call(
        body, name=name, grid=(s // rows,), after=after,
        in_specs=[*pieces, _fixed_spec(1, w)],
        out_specs=_row_spec(rows, w),
        out_shape=jax.ShapeDtypeStruct((s, w), BF16),
        compiler_params=_params("parallel"),
    )(*[x] * n, g)


def _rms_bwd_call(name, x, g, dy, out_dtype, cols=None, after=()):
    s = x.shape[0]
    start, w = cols or (0, x.shape[1])
    rows = min(ROWS, s)
    pieces = _column_pieces(rows, start, w) if cols else [_row_spec(rows, w)]
    n = len(pieces)

    def body(*refs):
        g_ref, dy_ref, dx_ref, dg_ref = refs[n:]
        xv = refs[0][...] if n == 1 else jnp.concatenate([r[...] for r in refs[:n]], axis=1)
        dx, dgc = _rms_bwd(xv, g_ref[...], dy_ref[...].astype(F32))
        dx_ref[...] = dx.astype(dx_ref.dtype)
        _accumulate(dg_ref, _sublane_sum(dgc), pl.program_id(0))

    return _call(
        body, name=name, grid=(s // rows,), after=after,
        in_specs=[*pieces, _fixed_spec(1, w), _row_spec(rows, w)],
        out_specs=[_row_spec(rows, w), _fixed_spec(SUBLANE, w)],
        out_shape=[jax.ShapeDtypeStruct((s, w), out_dtype), jax.ShapeDtypeStruct((SUBLANE, w), F32)],
        compiler_params=_params("arbitrary"),
    )(*[x] * n, g, dy)


def _mid_fwd(x, y, g_post, g_pre, after=()):
    s, w = x.shape
    rows = min(ROWS, s)

    def body(x_ref, y_ref, gp_ref, gq_ref, x2_ref, h2_ref):
        yv = y_ref[...]
        x2 = x_ref[...] + yv * _rstd(yv) * gp_ref[...]
        x2_ref[...] = x2
        h2_ref[...] = (x2 * _rstd(x2) * gq_ref[...]).astype(h2_ref.dtype)

    return _call(
        body, name="mid_fwd", grid=(s // rows,), after=after,
        in_specs=[_row_spec(rows, w), _row_spec(rows, w), _fixed_spec(1, w), _fixed_spec(1, w)],
        out_specs=[_row_spec(rows, w), _row_spec(rows, w)],
        out_shape=[jax.ShapeDtypeStruct((s, w), F32), jax.ShapeDtypeStruct((s, w), BF16)],
        compiler_params=_params("parallel"),
    )(x, y, g_post, g_pre)


def _head(m, x2, tgt, g):
    s, w = m.shape
    rows = min(ROWS, s)

    def body(m_ref, x2_ref, t_ref, g_ref, dout_ref, dm_ref, dg_ref, loss_ref):
        mv = m_ref[...]
        gv = g_ref[...]
        out = x2_ref[...] + mv * _rstd(mv) * gv
        err = out - t_ref[...]
        dout = err * (1.0 / w)
        dout_ref[...] = dout
        dm, dgc = _rms_bwd(mv, gv, dout)
        dm_ref[...] = dm.astype(dm_ref.dtype)
        sq = err * err
        lanes = sq[:, 0:LANE]
        for j in range(1, w // LANE):
            lanes = lanes + sq[:, j * LANE:(j + 1) * LANE]
        step = pl.program_id(0)
        _accumulate(dg_ref, _sublane_sum(dgc), step)
        _accumulate(loss_ref, _sublane_sum(lanes) * (0.5 / w), step)

    return pl.pallas_call(
        body, name="head", grid=(s // rows,),
        in_specs=[_row_spec(rows, w), _row_spec(rows, w), _row_spec(rows, w), _fixed_spec(1, w)],
        out_specs=[_row_spec(rows, w), _row_spec(rows, w), _fixed_spec(SUBLANE, w), _fixed_spec(SUBLANE, LANE)],
        out_shape=[jax.ShapeDtypeStruct((s, w), F32), jax.ShapeDtypeStruct((s, w), BF16),
                   jax.ShapeDtypeStruct((SUBLANE, w), F32), jax.ShapeDtypeStruct((SUBLANE, LANE), F32)],
        compiler_params=_params("arbitrary"),
    )(m, x2, tgt, g)


def _mid_bwd(x2, y, d_out, d_h2, g_pre, g_post, after=()):
    s, w = x2.shape
    rows = min(ROWS, s)

    def body(x2_ref, y_ref, dout_ref, dh2_ref, gq_ref, gp_ref, dx2_ref, dy_ref, dgq_ref, dgp_ref):
        dx, dgq = _rms_bwd(x2_ref[...], gq_ref[...], dh2_ref[...])
        dx2 = dout_ref[...] + dx
        dx2_ref[...] = dx2
        dy, dgp = _rms_bwd(y_ref[...], gp_ref[...], dx2)
        dy_ref[...] = dy.astype(dy_ref.dtype)
        step = pl.program_id(0)
        _accumulate(dgq_ref, _sublane_sum(dgq), step)
        _accumulate(dgp_ref, _sublane_sum(dgp), step)

    return _call(
        body, name="mid_bwd", grid=(s // rows,), after=after,
        in_specs=[_row_spec(rows, w)] * 4 + [_fixed_spec(1, w)] * 2,
        out_specs=[_row_spec(rows, w), _row_spec(rows, w), _fixed_spec(SUBLANE, w), _fixed_spec(SUBLANE, w)],
        out_shape=[jax.ShapeDtypeStruct((s, w), F32), jax.ShapeDtypeStruct((s, w), BF16),
                   jax.ShapeDtypeStruct((SUBLANE, w), F32), jax.ShapeDtypeStruct((SUBLANE, w), F32)],
        compiler_params=_params("arbitrary"),
    )(x2, y, d_out, d_h2, g_pre, g_post)


def _first_bwd(x, g, d_h1, d_x2, after=()):
    s, w = x.shape
    rows = min(ROWS, s)

    def body(x_ref, g_ref, dh_ref, dx2_ref, dx_ref, dg_ref):
        dx, dgc = _rms_bwd(x_ref[...], g_ref[...], dh_ref[...])
        dx_ref[...] = dx2_ref[...] + dx
        _accumulate(dg_ref, _sublane_sum(dgc), pl.program_id(0))

    return _call(
        body, name="first_bwd", grid=(s // rows,), after=after,
        in_specs=[_row_spec(rows, w), _fixed_spec(1, w), _row_spec(rows, w), _row_spec(rows, w)],
        out_specs=[_row_spec(rows, w), _fixed_spec(SUBLANE, w)],
        out_shape=[jax.ShapeDtypeStruct((s, w), F32), jax.ShapeDtypeStruct((SUBLANE, w), F32)],
        compiler_params=_params("arbitrary"),
    )(x, g, d_h1, d_x2)


def _shift_down(v, k):
    t = lax.broadcasted_iota(jnp.int32, v.shape, 0)
    return jnp.where(t >= k, pltpu.roll(v, k, 0), 0.0)


def _shift_up(v, k):
    n = v.shape[0]
    t = lax.broadcasted_iota(jnp.int32, v.shape, 0)
    return jnp.where(t < n - k, pltpu.roll(v, n - k, 0), 0.0)


def _conv_core(u, b, c, w):
    z = c * u
    conv = w[0:1, :] * _shift_down(z, 2) + w[1:2, :] * _shift_down(z, 1) + w[2:3, :] * z
    return z, conv, b * conv


def _conv_fwd(proj, conv_w, g, n_groups, out_width):
    s = proj.shape[0]

    def body(u_ref, b_ref, c_ref, w_ref, g_ref, o_ref):
        _, _, yr = _conv_core(u_ref[...], b_ref[...], c_ref[...], w_ref[...])
        o_ref[...] = (yr * _rstd(yr) * g_ref[...]).astype(o_ref.dtype)

    col = lambda k: pl.BlockSpec((s, HEAD), lambda i: (0, k * n_groups + i))
    return pl.pallas_call(
        body, name="conv_fwd", grid=(n_groups,),
        in_specs=[col(0), col(1), col(2), pl.BlockSpec((3, HEAD), lambda i: (0, i)), pl.BlockSpec((1, HEAD), lambda i: (0, i))],
        out_specs=pl.BlockSpec((s, HEAD), lambda i: (0, i)),
        out_shape=jax.ShapeDtypeStruct((s, out_width), BF16),
        compiler_params=_params("parallel"),
    )(proj, proj, proj, conv_w, g)


def _conv_bwd(proj, d_mix, conv_w, g, n_groups):
    s = proj.shape[0]
    width = n_groups * HEAD

    def body(u_ref, b_ref, c_ref, dy_ref, w_ref, g_ref, du_ref, db_ref, dc_ref, dg_ref, dw_ref):
        u, b, c, w = u_ref[...], b_ref[...], c_ref[...], w_ref[...]
        z, conv, yr = _conv_core(u, b, c, w)
        dyr, dgc = _rms_bwd(yr, g_ref[...], dy_ref[...])
        dconv = dyr * b
        db_ref[...] = (dyr * conv).astype(db_ref.dtype)
        dz = w[2:3, :] * dconv + w[1:2, :] * _shift_up(dconv, 1) + w[0:1, :] * _shift_up(dconv, 2)
        dc_ref[...] = (dz * u).astype(dc_ref.dtype)
        du_ref[...] = (dz * c).astype(du_ref.dtype)
        dg_ref[...] = _sublane_sum(dgc)
        dw_ref[0] = _sublane_sum(dconv * _shift_down(z, 2))
        dw_ref[1] = _sublane_sum(dconv * _shift_down(z, 1))
        dw_ref[2] = _sublane_sum(dconv * z)

    col = lambda k: pl.BlockSpec((s, HEAD), lambda i: (0, k * n_groups + i))
    grp = pl.BlockSpec((s, HEAD), lambda i: (0, i))
    return pl.pallas_call(
        body, name="conv_bwd", grid=(n_groups,),
        in_specs=[col(0), col(1), col(2), grp, pl.BlockSpec((3, HEAD), lambda i: (0, i)), pl.BlockSpec((1, HEAD), lambda i: (0, i))],
        out_specs=[grp, grp, grp, pl.BlockSpec((SUBLANE, HEAD), lambda i: (0, i)),
                   pl.BlockSpec((3, SUBLANE, HEAD), lambda i: (0, 0, i))],
        out_shape=[jax.ShapeDtypeStruct((s, width), BF16)] * 3
        + [jax.ShapeDtypeStruct((SUBLANE, width), F32), jax.ShapeDtypeStruct((3, SUBLANE, width), F32)],
        compiler_params=_params("parallel"),
    )(proj, proj, proj, d_mix, conv_w, g)


def _rope_tables(s, n_heads):
    pos = jnp.arange(s, dtype=F32)
    inv_freq = jnp.power(ROPE_THETA, -jnp.arange(0, ROPE, 2, dtype=F32) / ROPE)
    ang = pos[:, None] * inv_freq[None, :]
    cos, sin = jnp.cos(ang), jnp.sin(ang)
    cs = jnp.concatenate([cos, cos], axis=1)
    sn = jnp.concatenate([-sin, sin], axis=1)
    pad = jnp.zeros((s, LANE - ROPE), F32)
    return (jnp.tile(cs, (1, n_heads)), jnp.tile(sn, (1, n_heads)),
            jnp.concatenate([cs, pad], axis=1), jnp.concatenate([sn, pad], axis=1))


def _swap_halves(v):
    w = v.shape[1]
    lane = lax.broadcasted_iota(jnp.int32, v.shape, 1)
    first = (lane % ROPE) < (ROPE // 2)
    return jnp.where(first, pltpu.roll(v, w - ROPE // 2, 1), pltpu.roll(v, ROPE // 2, 1))


def _pack_heads(q, kv, proj, kr_col, tables, n_heads, after=()):
    s = q.shape[0]
    rows = min(ROWS, s)
    cq, sq, ck, sk = tables
    wq = n_heads * ROPE

    def body(q_ref, kv_ref, kr_ref, cq_ref, sq_ref, ck_ref, sk_ref, qo_ref, ko_ref, vo_ref):
        qr = q_ref[:, n_heads * HEAD:]
        qr = qr * cq_ref[...] + _swap_halves(qr) * sq_ref[...]
        krv = kr_ref[...]
        krv = krv * ck_ref[...] + _swap_halves(krv) * sk_ref[...]
        for h in range(n_heads):
            qo_ref[h] = jnp.concatenate([q_ref[:, h * HEAD:(h + 1) * HEAD], qr[:, h * ROPE:(h + 1) * ROPE]], axis=1).astype(BF16)
            ko_ref[h] = jnp.concatenate([kv_ref[:, 2 * h * HEAD:(2 * h + 1) * HEAD], krv[:, :ROPE]], axis=1).astype(BF16)
            vo_ref[h] = kv_ref[:, (2 * h + 1) * HEAD:(2 * h + 2) * HEAD].astype(BF16)

    hs = lambda w: pl.BlockSpec((n_heads, rows, w), lambda i: (0, i, 0))
    return _call(
        body, name="pack_heads", grid=(s // rows,), after=after,
        in_specs=[_row_spec(rows, q.shape[1]), _row_spec(rows, kv.shape[1]), pl.BlockSpec((rows, LANE), lambda i: (i, kr_col // LANE)),
                  _row_spec(rows, wq), _row_spec(rows, wq), _row_spec(rows, LANE), _row_spec(rows, LANE)],
        out_specs=[hs(QK), hs(QK), hs(HEAD)],
        out_shape=[jax.ShapeDtypeStruct((n_heads, s, QK), BF16), jax.ShapeDtypeStruct((n_heads, s, QK), BF16),
                   jax.ShapeDtypeStruct((n_heads, s, HEAD), BF16)],
        compiler_params=_params("parallel"),
    )(q, kv, proj, cq, sq, ck, sk)


def _unpack_heads(dq, dk, dv, tables, n_heads):
    s = dq.shape[1]
    rows = min(ROWS, s)
    cq, sq, ck, sk = tables
    wq = n_heads * ROPE

    def body(dq_ref, dk_ref, dv_ref, cq_ref, sq_ref, ck_ref, sk_ref, qo_ref, kvo_ref, kro_ref):
        dqr = jnp.concatenate([dq_ref[h][:, HEAD:] for h in range(n_heads)], axis=1)
        dqr = dqr * cq_ref[...] - _swap_halves(dqr) * sq_ref[...]
        dkr = dk_ref[0][:, HEAD:]
        for h in range(1, n_heads):
            dkr = dkr + dk_ref[h][:, HEAD:]
        dkr = jnp.concatenate([dkr, jnp.zeros((rows, LANE - ROPE), F32)], axis=1)
        dkr = dkr * ck_ref[...] - _swap_halves(dkr) * sk_ref[...]
        kro_ref[...] = dkr.astype(kro_ref.dtype)
        qo_ref[:, n_heads * HEAD:] = dqr.astype(qo_ref.dtype)
        for h in range(n_heads):
            qo_ref[:, h * HEAD:(h + 1) * HEAD] = dq_ref[h][:, :HEAD].astype(qo_ref.dtype)
            kvo_ref[:, 2 * h * HEAD:(2 * h + 1) * HEAD] = dk_ref[h][:, :HEAD].astype(kvo_ref.dtype)
            kvo_ref[:, (2 * h + 1) * HEAD:(2 * h + 2) * HEAD] = dv_ref[h].astype(kvo_ref.dtype)

    hs = lambda w: pl.BlockSpec((n_heads, rows, w), lambda i: (0, i, 0))
    return pl.pallas_call(
        body, name="unpack_heads", grid=(s // rows,),
        in_specs=[hs(QK), hs(QK), hs(HEAD), _row_spec(rows, wq), _row_spec(rows, wq), _row_spec(rows, LANE), _row_spec(rows, LANE)],
        out_specs=[_row_spec(rows, n_heads * QK), _row_spec(rows, 2 * n_heads * HEAD), _row_spec(rows, LANE)],
        out_shape=[jax.ShapeDtypeStruct((s, n_heads * QK), BF16), jax.ShapeDtypeStruct((s, 2 * n_heads * HEAD), BF16),
                   jax.ShapeDtypeStruct((s, LANE), BF16)],
        compiler_params=_params("parallel"),
    )(dq, dk, dv, cq, sq, ck, sk)


TQ = 256


LOG2_E = 1.4426950408889634


def _softmax_parts(q, k):
    tq, n_keys = q.shape[0], k.shape[0]
    sc = lax.dot_general(q, k, (NT, ((), ())), preferred_element_type=F32) * (QK ** -0.5 * LOG2_E)
    row = lax.broadcasted_iota(jnp.int32, (tq, tq), 0)
    col = lax.broadcasted_iota(jnp.int32, (tq, tq), 1)
    own = jnp.where(col // CHUNK <= row // CHUNK, sc[:, n_keys - tq:], NEG_INF)
    sc = own if n_keys == tq else jnp.concatenate([sc[:, :n_keys - tq], own], axis=1)
    e = jnp.exp2(sc - jnp.max(sc, axis=-1, keepdims=True))
    return e, 1.0 / jnp.sum(e, axis=-1, keepdims=True)


def _attn_fwd(q, k, v, g, mix, col0):
    n_heads, s, _ = q.shape
    tq = min(TQ, s)
    assert tq % CHUNK == 0 and s % tq == 0

    def body(q_ref, k_ref, v_ref, g_ref, mix_ref, o_ref, y_ref):
        for c in range(s // tq):
            rows, n_keys = pl.ds(c * tq, tq), (c + 1) * tq
            e, inv = _softmax_parts(q_ref[rows, :], k_ref[0:n_keys, :])
            o = jnp.dot(e.astype(BF16), v_ref[0:n_keys, :], preferred_element_type=F32) * inv
            o_ref[rows, :] = o
            y_ref[rows, :] = (o * _rstd(o) * g_ref[...]).astype(y_ref.dtype)

    head = lambda w: pl.BlockSpec((None, s, w), lambda h: (h, 0, 0))
    return pl.pallas_call(
        body, name="attn_fwd", grid=(n_heads,),
        in_specs=[head(QK), head(QK), head(HEAD), pl.BlockSpec((1, HEAD), lambda h: (0, h)), ANY],
        out_specs=[head(HEAD), pl.BlockSpec((s, HEAD), lambda h: (0, col0 // HEAD + h))],
        out_shape=[jax.ShapeDtypeStruct((n_heads, s, HEAD), F32), jax.ShapeDtypeStruct(mix.shape, mix.dtype)],
        input_output_aliases={4: 1},
        compiler_params=_params("parallel"),
    )(q, k, v, g, mix)


def _attn_bwd(q, k, v, o, d_mix, g, col0, after=()):
    n_heads, s, _ = q.shape
    tq = min(TQ, s)

    def body(q_ref, k_ref, v_ref, o_ref, dy_ref, g_ref, dq_ref, dk_ref, dv_ref, dg_ref):
        dg = None
        for c in reversed(range(s // tq)):
            rows, n_keys = pl.ds(c * tq, tq), (c + 1) * tq
            qv, kv_, vv = q_ref[rows, :], k_ref[0:n_keys, :], v_ref[0:n_keys, :]
            do, dgc = _rms_bwd(o_ref[rows, :], g_ref[...], dy_ref[rows, :])
            do = do.astype(BF16)
            dg = _sublane_sum(dgc) if dg is None else dg + _sublane_sum(dgc)
            e, inv = _softmax_parts(qv, kv_)
            p = e * inv
            dp = lax.dot_general(do, vv, (NT, ((), ())), preferred_element_type=F32)
            ds = (p * (dp - jnp.sum(p * dp, axis=-1, keepdims=True)) * (QK ** -0.5)).astype(BF16)
            dq_ref[rows, :] = jnp.dot(ds, kv_, preferred_element_type=F32)
            dk = lax.dot_general(ds, qv, (TN, ((), ())), preferred_element_type=F32)
            dv = lax.dot_general(p.astype(BF16), do, (TN, ((), ())), preferred_element_type=F32)
            if n_keys == s:
                dk_ref[...] = dk
                dv_ref[...] = dv
            else:
                dk_ref[0:n_keys, :] += dk
                dv_ref[0:n_keys, :] += dv
        dg_ref[...] = dg

    c0 = col0 // HEAD
    head = lambda w: pl.BlockSpec((None, s, w), lambda h: (h, 0, 0))
    return _call(
        body, name="attn_bwd", grid=(n_heads,), after=after,
        in_specs=[head(QK), head(QK), head(HEAD), head(HEAD), pl.BlockSpec((s, HEAD), lambda h: (0, c0 + h)),
                  pl.BlockSpec((1, HEAD), lambda h: (0, h))],
        out_specs=[head(QK), head(QK), head(HEAD), pl.BlockSpec((SUBLANE, HEAD), lambda h: (0, h))],
        out_shape=[jax.ShapeDtypeStruct((n_heads, s, QK), F32), jax.ShapeDtypeStruct((n_heads, s, QK), F32),
                   jax.ShapeDtypeStruct((n_heads, s, HEAD), F32), jax.ShapeDtypeStruct((SUBLANE, n_heads * HEAD), F32)],
        compiler_params=_params("parallel"),
    )(q, k, v, o, d_mix, g)


TILE_M = 1024
TILE_N = 1024


def _up_fwd(h2, w_up):
    s, d = h2.shape
    nb, _, fb = w_up.shape
    tm = min(TILE_M,s)

    def epilogue(acc):
        r = jnp.maximum(acc, 0.0)
        return r * r, r

    blk = pl.BlockSpec((tm, fb), lambda i, j: (i, j))
    return _matmul("up_fwd", h2, w_up, grid=(s // tm, nb),
                   a_spec=pl.BlockSpec((tm, d), lambda i, j: (i, 0)),
                   b_spec=pl.BlockSpec((None, d, fb), lambda i, j: (j, 0, 0)),
                   out_shape=[jax.ShapeDtypeStruct((s, nb * fb), BF16)] * 2, out_specs=[blk, blk],
                   contract=NN, epilogue=epilogue)


def _down_fwd(a, w_down):
    s, f = a.shape
    d = w_down.shape[1]
    tm, tn, tk = min(TILE_M,s), min(TILE_N,d), 2048
    nk = f // tk
    return _matmul("down_fwd", a, w_down, grid=(s // tm, d // tn, nk),
                   a_spec=pl.BlockSpec((tm, tk), lambda i, j, k: (i, k)),
                   b_spec=pl.BlockSpec((tk, tn), lambda i, j, k: (k, j)),
                   out_shape=jax.ShapeDtypeStruct((s, d), F32),
                   out_specs=pl.BlockSpec((tm, tn), lambda i, j, k: (i, j)),
                   contract=NN, nk=nk, acc_shape=(tm, tn))


def _down_bwd_act(d_m, w_down, r, after=()):
    s, d = d_m.shape
    f = w_down.shape[0]
    tm, tn = min(TILE_M,s), min(TILE_N,f)
    blk = pl.BlockSpec((tm, tn), lambda i, j: (i, j))
    return _matmul("down_bwd_act", d_m, w_down, grid=(s // tm, f // tn), after=after,
                   a_spec=pl.BlockSpec((tm, d), lambda i, j: (i, 0)),
                   b_spec=pl.BlockSpec((tn, d), lambda i, j: (j, 0)),
                   out_shape=jax.ShapeDtypeStruct((s, f), BF16), out_specs=blk, contract=NT,
                   extras=(r,), extra_specs=(blk,),
                   epilogue=lambda acc, rv: (acc * (2.0 * rv.astype(F32)),))


def _up_bwd_act(d_up, w_up, after=()):
    s, _ = d_up.shape
    nb, d, fb = w_up.shape
    tm, tn = min(TILE_M, s), min(TILE_N,d)
    pair = 2
    n_after = len(after)

    def body(a_ref, w_ref, *rest):
        o_ref, acc_ref = rest[n_after:]
        k = pl.program_id(2)
        p = None
        for t in range(pair):
            term = lax.dot_general(a_ref[:, t * fb:(t + 1) * fb], w_ref[t], (NT, ((), ())), preferred_element_type=F32)
            p = term if p is None else p + term
        _accumulate(acc_ref, p, k)

        @pl.when(k == nb // pair - 1)
        def _():
            o_ref[...] = acc_ref[...]

    return pl.pallas_call(
        body, name="up_bwd_act", grid=(s // tm, d // tn, nb // pair),
        in_specs=[pl.BlockSpec((tm, pair * fb), lambda i, j, k: (i, k)),
                  pl.BlockSpec((pair, tn, fb), lambda i, j, k: (k, j, 0))] + [ANY] * n_after,
        out_specs=pl.BlockSpec((tm, tn), lambda i, j, k: (i, j)),
        out_shape=jax.ShapeDtypeStruct((s, d), F32),
        scratch_shapes=[pltpu.VMEM((tm, tn), F32)],
        compiler_params=_params("parallel", "parallel", "arbitrary"),
    )(d_up, w_up, *after)


def _half_grad(name, a, b, core, home, received, after, *, grid, a_block, a_map, b_block, b_map, o_block, o_map, out_shape):
    n_after = len(after)
    pick = (lambda ref: ref[0]) if home else (lambda ref: 1 - ref[0])

    def body(core_ref, a_ref, b_ref, *rest):
        acc = lax.dot_general(a_ref[...], b_ref[...], (TN, ((), ())), preferred_element_type=F32)
        if received is not None:
            acc = acc + rest[0][...].astype(F32)
        rest[-1][...] = acc.astype(rest[-1].dtype)

    wrap = lambda fn: (lambda i, j, core_ref: fn(i, j, pick(core_ref)))
    o_spec = pl.BlockSpec(o_block, wrap(o_map))
    extra = [] if received is None else [o_spec]
    operands = [] if received is None else [received]
    return pl.pallas_call(
        body, name=name,
        grid_spec=pltpu.PrefetchScalarGridSpec(
            num_scalar_prefetch=1, grid=grid,
            in_specs=[pl.BlockSpec(a_block, wrap(a_map)), pl.BlockSpec(b_block, wrap(b_map))] + extra + [ANY] * n_after,
            out_specs=o_spec),
        out_shape=out_shape,
        compiler_params=_params("parallel", "parallel"),
    )(core, a, b, *operands, *after)


def _down_half_grad(name, a, d_m, core, home, received=None, after=()):
    s, f = a.shape
    d = d_m.shape[1]
    r = f // N_DEV
    tn = min(TILE_N, d)
    return _half_grad(name, a, d_m, core, home, received, after, grid=(N_CHIP, d // tn),
                      a_block=(s, r), a_map=lambda k, j, p: (0, 2 * k + p),
                      b_block=(s, tn), b_map=lambda k, j, p: (0, j),
                      o_block=(None, r, tn), o_map=lambda k, j, p: (k, 0, j),
                      out_shape=jax.ShapeDtypeStruct((N_CHIP, r, d), BF16))


def _up_half_grad(name, h2, d_up, core, home, received=None, after=()):
    s, d = h2.shape
    fb = d_up.shape[1] // N_DEV
    tm = min(TILE_M, d)
    return _half_grad(name, h2, d_up, core, home, received, after, grid=(d // tm, N_CHIP),
                      a_block=(s, tm), a_map=lambda i, k, p: (0, i),
                      b_block=(s, fb), b_map=lambda i, k, p: (0, 2 * k + p),
                      o_block=(None, tm, fb), o_map=lambda i, k, p: (k, i, 0),
                      out_shape=jax.ShapeDtypeStruct((N_CHIP, d, fb), BF16))


def _in_pad(in_width):
    return -(-in_width // LANE) * LANE


def _join_col_shards(blocks):
    n, r, w = blocks.shape
    rows = min(ROWS, r)
    width = _in_pad(n * w)

    def body(x_ref, o_ref):
        tail = [jnp.zeros((rows, width - n * w), o_ref.dtype)] if width > n * w else []
        o_ref[...] = jnp.concatenate([x_ref[j] for j in range(n)] + tail, axis=1)

    return pl.pallas_call(
        body, name="join_col_shards", grid=(r // rows,),
        in_specs=[pl.BlockSpec((n, rows, w), lambda i: (0, i, 0))], out_specs=_row_spec(rows, width),
        out_shape=jax.ShapeDtypeStruct((r, width), blocks.dtype),
        compiler_params=_params("parallel"),
    )(blocks)


def _permute_q_cols(w_uq, n_heads):
    r = w_uq.shape[0]
    w3 = w_uq.reshape(r, n_heads, QK)
    return jnp.concatenate([w3[:, :, :HEAD].reshape(r, n_heads * HEAD), w3[:, :, HEAD:].reshape(r, n_heads * ROPE)], axis=1)


def _unpermute_q_rows(wt, n_heads):
    r = wt.shape[1]
    nope = wt[:n_heads * HEAD].reshape(n_heads, HEAD, r)
    rope = wt[n_heads * HEAD:].reshape(n_heads, ROPE, r)
    return jnp.concatenate([nope, rope], axis=1).reshape(n_heads * QK, r)


def _local_step(x, tgt, gains, weights, grads, first_after=()):
    pre_mix_g, q_norm_g, kv_norm_g, conv_out_g, attn_out_g, post_mix_g, pre_mlp_g, post_mlp_g = gains
    s, d = x.shape
    conv_width = conv_out_g.shape[1]
    n_groups = conv_width // HEAD
    r_q, r_kv = q_norm_g.shape[1], kv_norm_g.shape[1]
    n_heads = attn_out_g.shape[1] // HEAD
    c_q0 = 3 * conv_width
    c_kv0 = c_q0 + r_q
    c_kr0 = c_kv0 + r_kv
    in_pad = _in_pad(c_kr0 + ROPE)
    tn_in = in_pad // 5 if in_pad % (5 * LANE) == 0 else LANE
    tables = _rope_tables(s, n_heads)

    h1 = _rms_fwd("pre_mix_norm", x, pre_mix_g, after=first_after)
    weights.forward(0, (h1,))
    weights.relay(0, tables)
    w_in_p, conv_w = weights.ready(0, ())
    proj = _mm_nn("in_proj", h1, w_in_p, F32, TILE_M, tn_in)
    y_conv = _conv_fwd(proj, conv_w, conv_out_g, n_groups, conv_width + n_heads * HEAD)
    qn = _rms_fwd("q_norm", proj, q_norm_g, cols=(c_q0, r_q), after=weights.forward(1, (y_conv,)))
    kvn = _rms_fwd("kv_norm", proj, kv_norm_g, cols=(c_kv0, r_kv))
    w_uq_p, w_ukv, w_o = weights.ready(1, (qn, kvn))
    q = _mm_nn("q_up", qn, w_uq_p, F32, TILE_M, TILE_N)
    kv = _mm_nn("kv_up", kvn, w_ukv, F32, TILE_M, TILE_N)
    qh, kh, vh = _pack_heads(q, kv, proj, c_kr0, tables, n_heads, after=weights.forward(2, (q, kv)))
    o, mix = _attn_fwd(qh, kh, vh, attn_out_g, y_conv, conv_width)
    y = _mm_nn("out_proj", mix, w_o, F32, TILE_M, TILE_N)
    x2, h2 = _mid_fwd(x, y, post_mix_g, pre_mlp_g, after=weights.forward(3, (y,)))
    weights.relay(2, (h2,))
    (w_up,) = weights.ready(2, ())
    a, r = _up_fwd(h2, w_up)
    weights.relay(3, (a,))
    (w_down,) = weights.ready(3, ())
    m = _down_fwd(a, w_down)

    d_out, d_m, dg_post_mlp, loss_part = _head(m, x2, tgt, post_mlp_g)
    core = grads.core
    away = _down_half_grad("down_bwd_w_away", a, d_m, core, home=False)
    d_up = _down_bwd_act(d_m, w_down, r, after=grads.send_away(0, away))
    sums = _down_half_grad("down_bwd_w_home", a, d_m, core, home=True, received=grads.received(0, (d_up,)))
    away = _up_half_grad("up_bwd_w_away", h2, d_up, core, home=False, after=grads.send_sums(0, (sums,)))
    d_h2 = _up_bwd_act(d_up, w_up, after=grads.send_away(1, away))
    sums = _up_half_grad("up_bwd_w_home", h2, d_up, core, home=True, received=grads.received(1, (d_h2,)))
    d_x2, d_y, dg_pre_mlp, dg_post_mix = _mid_bwd(x2, y, d_out, d_h2, pre_mlp_g, post_mix_g, after=grads.send_sums(1, (sums,)))
    d_mix = _mm_nt("out_proj_bwd_act", d_y, w_o, F32, TILE_M, TILE_N)
    gw_o = _mm_tn("out_proj_bwd_w", mix, d_y, BF16, TILE_M, TILE_N)
    dqh, dkh, dvh, dg_attn = _attn_bwd(qh, kh, vh, o, d_mix, attn_out_g, conv_width, after=grads.full(2, (gw_o,)))
    d_q, d_kv, d_kr = _unpack_heads(dqh, dkh, dvh, tables, n_heads)
    d_qn = _mm_nt("q_up_bwd_act", d_q, w_uq_p, F32, TILE_M, TILE_N)
    d_kvn = _mm_nt("kv_up_bwd_act", d_kv, w_ukv, F32, TILE_M, TILE_N)
    gw_uq_t = _mm_tn("q_up_bwd_w", d_q, qn, F32, TILE_M, TILE_N)
    gw_ukv = _mm_tn("kv_up_bwd_w", kvn, d_kv, BF16, TILE_M, TILE_N)
    d_cq, dg_q = _rms_bwd_call("q_norm_bwd", proj, q_norm_g, d_qn, BF16, cols=(c_q0, r_q), after=grads.full(3, (gw_uq_t, gw_ukv)))
    d_ckv, dg_kv = _rms_bwd_call("kv_norm_bwd", proj, kv_norm_g, d_kvn, BF16, cols=(c_kv0, r_kv))
    d_u, d_b, d_c, dg_conv, dw_conv = _conv_bwd(proj, d_mix, conv_w, conv_out_g, n_groups)
    d_proj = jnp.concatenate([d_u, d_b, d_c, d_cq, d_ckv, d_kr[:, :in_pad - c_kr0]], axis=1)
    gw_in_t = _mm_tn("in_proj_bwd_w", d_proj, h1, F32, tn_in, TILE_N)
    d_h1 = _mm_nt("in_proj_bwd_act", d_proj, w_in_p, F32, TILE_M, 512, after=grads.send_away(4, gw_in_t))
    grad_x, dg_pre_mix = _first_bwd(x, pre_mix_g, d_h1, d_x2, after=grads.full(4, (gw_in_t,), received=(d_h1,)))

    small = [dg_pre_mix, dg_q, dg_kv, dg_conv, dg_attn, dg_post_mix, dg_pre_mlp, dg_post_mlp,
             dw_conv[0], dw_conv[1], dw_conv[2], loss_part]
    return grad_x, jnp.concatenate(small, axis=1)


HBM = pl.BlockSpec(memory_space=pltpu.HBM)
SEM = pl.BlockSpec(memory_space=pltpu.SEMAPHORE)
IN_VMEM = pl.BlockSpec(memory_space=pltpu.VMEM)
SPLIT = pltpu.CompilerParams(has_side_effects=pltpu.SideEffectType.DATAFLOW_SIDE_EFFECTING)


def _in_hbm(a):
    return pltpu.with_memory_space_constraint(a, pltpu.HBM)


def _hbm_like(a):
    return pltpu.HBM(a.shape, a.dtype)


def _place():
    x, y, c = lax.axis_index("x"), lax.axis_index("y"), lax.axis_index("c")
    other_chips = [(1 - x, y), (x, 1 - y), (1 - x, 1 - y)]
    return x, y, c, other_chips


def _block(px, py, pc):
    return 4 * px + 2 * py + pc


def _await(block, sem):
    pltpu.make_async_copy(block, block, sem).wait()


def _relay_route(x, y, c):
    came_from = ((1 - x) * (1 - c) + x * c, y * (1 - c) + (1 - y) * c)
    goes_to = (x * (1 - c) + (1 - x) * c, (1 - y) * (1 - c) + y * c)
    return came_from, goes_to


def _gather_start(name, shards, groups, relayed=(), after=()):
    n, ng = len(shards), len(groups)
    lands = [lax.empty((N_DEV, *a.shape), a.dtype) for a in shards]

    def body(*refs):
        src, land = refs[:n], refs[n:2 * n]
        sems, token = refs[2 * n + len(after):2 * n + len(after) + 2 * ng], refs[-1]
        x, y, c, chips = _place()
        targets = [(x, y, 1 - c)] + [(*chip, c) for chip in chips]
        for gi, group in enumerate(groups):
            for i, w in enumerate(group):
                for k, to in enumerate(targets[:3] if gi in relayed else targets):
                    pltpu.make_async_remote_copy(
                        src_ref=src[w], dst_ref=land[w].at[_block(x, y, c)],
                        send_sem=sems[2 * gi].at[4 * i + k], recv_sem=sems[2 * gi + 1].at[4 * i + k],
                        device_id=to, device_id_type=MESH).start()
        token[...] = jnp.zeros_like(token)

    sem_shapes = [pltpu.SemaphoreType.DMA((4 * len(g),)) for g in groups for _ in range(2)]
    out = pl.pallas_call(
        body, name=name,
        in_specs=[HBM] * (2 * n) + [ANY] * len(after),
        out_specs=[SEM] * (2 * ng) + [HBM] * (2 * n) + [IN_VMEM],
        out_shape=sem_shapes + [_hbm_like(a) for a in shards] + [_hbm_like(a) for a in lands]
        + [jax.ShapeDtypeStruct((SUBLANE, LANE), F32)],
        input_output_aliases={i: 2 * ng + i for i in range(2 * n)},
        compiler_params=SPLIT,
    )(*[_in_hbm(a) for a in shards], *[_in_hbm(a) for a in lands], *after)
    sems = [(out[2 * gi], out[2 * gi + 1]) for gi in range(ng)]
    return sems, out[2 * ng:2 * ng + n], out[2 * ng + n:2 * ng + 2 * n], out[-1]


def _gather_forward(name, shards, lands, send1, recv1, after, relayed=False):
    n = len(lands)

    def body(*refs):
        src, land = refs[:n], refs[n:2 * n]
        s1, r1 = refs[2 * n], refs[2 * n + 1]
        s2, r2 = refs[2 * n + 2 + len(after)], refs[2 * n + 3 + len(after)]
        x, y, c, chips = _place()
        me, sibling = (x, y, c), (x, y, 1 - c)
        for j, chip in enumerate(chips[:2] if relayed else chips):
            for i in range(n):
                blk = land[i].at[_block(*chip, c)]
                pltpu.make_async_remote_copy(src_ref=blk, dst_ref=blk, send_sem=s1.at[4 * i + 1 + j], recv_sem=r1.at[4 * i + 1 + j],
                                             device_id=me, device_id_type=MESH).wait_recv()
                pltpu.make_async_remote_copy(src_ref=blk, dst_ref=blk, send_sem=s2.at[3 * i + j], recv_sem=r2.at[3 * i + j],
                                             device_id=sibling, device_id_type=MESH).start()
        if relayed:
            came_from, goes_to = _relay_route(x, y, c)
            for i in range(n):
                blk = land[i].at[_block(*came_from, c)]
                pltpu.make_async_remote_copy(src_ref=blk, dst_ref=blk, send_sem=s2.at[3 * i + 2], recv_sem=r2.at[3 * i + 2],
                                             device_id=(*goes_to, c), device_id_type=MESH).start()
        for i in range(n):
            blk = land[i].at[_block(x, y, 1 - c)]
            pltpu.make_async_remote_copy(src_ref=blk, dst_ref=blk, send_sem=s1.at[4 * i], recv_sem=r1.at[4 * i],
                                         device_id=me, device_id_type=MESH).wait_recv()
            for k in range(3 if relayed else 4):
                pltpu.make_async_remote_copy(src_ref=src[i], dst_ref=land[i].at[_block(x, y, c)], send_sem=s1.at[4 * i + k],
                                             recv_sem=r1.at[4 * i + k], device_id=sibling, device_id_type=MESH).wait_send()

    sem = pltpu.SemaphoreType.DMA((3 * n,))
    out = pl.pallas_call(
        body, name=name,
        in_specs=[HBM] * (2 * n) + [SEM, SEM] + [ANY] * len(after),
        out_specs=[SEM, SEM] + [HBM] * n,
        out_shape=[sem, sem] + [_hbm_like(a) for a in lands],
        input_output_aliases={n + i: 2 + i for i in range(n)},
        compiler_params=SPLIT,
    )(*shards, *lands, send1, recv1, *after)
    return (out[0], out[1]), out[2:]


def _gather_relay_forward(name, lands, send2, recv2, after):
    n = len(lands)

    def body(*refs):
        land, s2, r2 = refs[:n], refs[n], refs[n + 1]
        s3, r3 = refs[n + 2 + len(after)], refs[n + 3 + len(after)]
        x, y, c, _ = _place()
        me, sibling = (x, y, c), (x, y, 1 - c)
        came_from, _ = _relay_route(x, y, c)
        for i in range(n):
            blk = land[i].at[_block(1 - x, 1 - y, c)]
            pltpu.make_async_remote_copy(src_ref=blk, dst_ref=blk, send_sem=s2.at[3 * i + 2], recv_sem=r2.at[3 * i + 2],
                                         device_id=me, device_id_type=MESH).wait_recv()
            pltpu.make_async_remote_copy(src_ref=blk, dst_ref=blk, send_sem=s3.at[i], recv_sem=r3.at[i],
                                         device_id=sibling, device_id_type=MESH).start()
            sent = land[i].at[_block(*came_from, c)]
            pltpu.make_async_remote_copy(src_ref=sent, dst_ref=sent, send_sem=s2.at[3 * i + 2], recv_sem=r2.at[3 * i + 2],
                                         device_id=me, device_id_type=MESH).wait_send()

    sem = pltpu.SemaphoreType.DMA((n,))
    out = pl.pallas_call(
        body, name=name,
        in_specs=[HBM] * n + [SEM, SEM] + [ANY] * len(after),
        out_specs=[SEM, SEM] + [HBM] * n,
        out_shape=[sem, sem] + [_hbm_like(a) for a in lands],
        input_output_aliases={i: 2 + i for i in range(n)},
        compiler_params=SPLIT,
    )(*lands, send2, recv2, *after)
    return (out[0], out[1]), out[2:]


def _gather_wait(name, lands, send2, recv2, after, relay_sems=None):
    n = len(lands)
    n_sems = 2 if relay_sems is None else 4

    def body(*refs):
        land, s2, r2 = refs[:n], refs[n], refs[n + 1]
        for i in range(n):
            for j in range(3 if relay_sems is None else 2):
                _await(land[i].at[0], r2.at[3 * i + j])
                _await(land[i].at[0], s2.at[3 * i + j])
            if relay_sems is not None:
                _await(land[i].at[0], refs[n + 3].at[i])
                _await(land[i].at[0], refs[n + 2].at[i])

    return pl.pallas_call(
        body, name=name,
        in_specs=[HBM] * n + [SEM] * n_sems + [ANY] * len(after), out_specs=[HBM] * n, out_shape=[_hbm_like(a) for a in lands],
        input_output_aliases={i: i for i in range(n)},
        compiler_params=SPLIT,
    )(*lands, send2, recv2, *(relay_sems or ()), *after)


def _pair_exchange(name, grads, shard_rows):
    n = len(grads)
    shapes = [(g.shape[1:] if r is None else (r, g.shape[1])) for g, r in zip(grads, shard_rows)]

    def body(*refs):
        ins, recv = refs[:n], refs[n:2 * n]
        send_sems, recv_sems = refs[2 * n:]
        x, y, c, _ = _place()
        sends = []
        for w in range(n):
            for k in range(N_CHIP):
                j, r = 2 * k + 1 - c, shard_rows[w]
                src = ins[w].at[j] if r is None else ins[w].at[pl.ds(pl.multiple_of(j * r, SUBLANE), r), :]
                sends.append(pltpu.make_async_remote_copy(
                    src_ref=src, dst_ref=recv[w].at[k],
                    send_sem=send_sems.at[w, k], recv_sem=recv_sems.at[w, k],
                    device_id=(x, y, 1 - c), device_id_type=MESH))
        for cp in sends:
            cp.start()
        for cp in sends:
            cp.wait()

    return pl.pallas_call(
        body, name=name,
        in_specs=[ANY] * n, out_specs=[ANY] * n,
        out_shape=[jax.ShapeDtypeStruct((N_CHIP, *shape), g.dtype) for g, shape in zip(grads, shapes)],
        scratch_shapes=[pltpu.SemaphoreType.DMA((n, N_CHIP))] * 2,
    )(*grads)


def _pair_sum_rows(name, grad, received, core):
    _, r, c = received.shape
    tc = _fit(c, 512)

    def body(core_ref, a_ref, b_ref, o_ref):
        o_ref[...] = (a_ref[...] + b_ref[...]).astype(o_ref.dtype)

    spec = pl.BlockSpec((None, r, tc), lambda k, i, core_ref: (k, 0, i))
    return pl.pallas_call(
        body, name=name,
        grid_spec=pltpu.PrefetchScalarGridSpec(
            num_scalar_prefetch=1, grid=(N_CHIP, c // tc),
            in_specs=[pl.BlockSpec((r, tc), lambda k, i, core_ref: (2 * k + core_ref[0], i)), spec],
            out_specs=spec),
        out_shape=jax.ShapeDtypeStruct(received.shape, BF16),
        compiler_params=_params("parallel", "parallel"),
    )(core, grad, received)


def _pair_sum(name, grad, received, core):
    _, r, c = received.shape
    rows = min(ROWS, r)
    assert r % rows == 0

    def body(core_ref, a_ref, b_ref, o_ref):
        o_ref[...] = (a_ref[...].astype(F32) + b_ref[...].astype(F32)).astype(o_ref.dtype)

    spec = pl.BlockSpec((None, rows, c), lambda k, i, core_ref: (k, i, 0))
    return pl.pallas_call(
        body, name=name,
        grid_spec=pltpu.PrefetchScalarGridSpec(
            num_scalar_prefetch=1, grid=(N_CHIP, r // rows),
            in_specs=[pl.BlockSpec((None, None, rows, c), lambda k, i, core_ref: (k, core_ref[0], i, 0)), spec],
            out_specs=spec),
        out_shape=jax.ShapeDtypeStruct(received.shape, received.dtype),
        compiler_params=_params("parallel", "parallel"),
    )(core, grad.reshape(N_CHIP, 2, r, c), received)


def _away_shard(src, k, c, shard_rows):
    if shard_rows is None:
        return src.at[k]
    return src.at[pl.ds(pl.multiple_of((2 * k + 1 - c) * shard_rows, SUBLANE), shard_rows), :]


def _pair_send_start(name, away, shard_rows=None):
    shape = away.shape if shard_rows is None else (N_CHIP, shard_rows, away.shape[1])
    land = lax.empty(shape, away.dtype)

    def body(src, dst, send, recv, src_thru, dst_thru, token):
        x, y, c, _ = _place()
        for k in range(N_CHIP):
            pltpu.make_async_remote_copy(src_ref=_away_shard(src, k, c, shard_rows), dst_ref=dst.at[k], send_sem=send.at[k],
                                         recv_sem=recv.at[k], device_id=(x, y, 1 - c), device_id_type=MESH).start()
        token[...] = jnp.zeros_like(token)

    sem = pltpu.SemaphoreType.DMA((N_CHIP,))
    out = pl.pallas_call(
        body, name=name,
        in_specs=[HBM, HBM], out_specs=[SEM, SEM, HBM, HBM, IN_VMEM],
        out_shape=[sem, sem, _hbm_like(away), _hbm_like(land), jax.ShapeDtypeStruct((SUBLANE, LANE), F32)],
        input_output_aliases={0: 2, 1: 3},
        compiler_params=SPLIT,
    )(_in_hbm(away), _in_hbm(land))
    return (out[0], out[1]), out[2], out[3], out[4]


def _pair_send_wait(name, sems, src, land, after, shard_rows=None):
    def body(src_ref, dst_ref, send, recv, *rest):
        for k in range(N_CHIP):
            _await(dst_ref.at[k], send.at[k])
            _await(dst_ref.at[k], recv.at[k])

    return pl.pallas_call(
        body, name=name,
        in_specs=[HBM, HBM, SEM, SEM] + [ANY] * len(after), out_specs=HBM, out_shape=_hbm_like(land),
        input_output_aliases={1: 0},
        compiler_params=SPLIT,
    )(src, land, *sems, *after)


def _chip_send_start(name, sums):
    n = len(sums)
    lands = [lax.empty(a.shape, a.dtype) for a in sums]

    def body(*refs):
        src, land = refs[:n], refs[n:2 * n]
        send, recv, token = refs[2 * n], refs[2 * n + 1], refs[-1]
        x, y, c, chips = _place()
        for w in range(n):
            for j, (px, py) in enumerate(chips):
                pltpu.make_async_remote_copy(
                    src_ref=src[w].at[2 * px + py], dst_ref=land[w].at[2 * x + y],
                    send_sem=send.at[3 * w + j], recv_sem=recv.at[3 * w + j],
                    device_id=(px, py, c), device_id_type=MESH).start()
        token[...] = jnp.zeros_like(token)

    sem = pltpu.SemaphoreType.DMA((3 * n,))
    out = pl.pallas_call(
        body, name=name,
        in_specs=[HBM] * (2 * n),
        out_specs=[SEM, SEM] + [HBM] * (2 * n) + [IN_VMEM],
        out_shape=[sem, sem] + [_hbm_like(a) for a in sums] + [_hbm_like(a) for a in lands]
        + [jax.ShapeDtypeStruct((SUBLANE, LANE), F32)],
        input_output_aliases={i: 2 + i for i in range(2 * n)},
        compiler_params=SPLIT,
    )(*[_in_hbm(a) for a in sums], *[_in_hbm(a) for a in lands])
    return (out[0], out[1]), out[2:2 + n], out[2 + n:2 + 2 * n], out[-1]


def _chip_send_wait(name, groups, after):
    counts = [len(g[1]) for g in groups]
    n = sum(counts)

    def body(*refs):
        land = refs[n:2 * n]
        sems = refs[2 * n:2 * n + 2 * len(groups)]
        w = 0
        for gi, count in enumerate(counts):
            for i in range(count):
                for j in range(3):
                    _await(land[w].at[0], sems[2 * gi].at[3 * i + j])
                    _await(land[w].at[0], sems[2 * gi + 1].at[3 * i + j])
                w += 1

    sums = [a for g in groups for a in g[1]]
    lands = [a for g in groups for a in g[2]]
    sems = [s for g in groups for s in g[0]]
    return pl.pallas_call(
        body, name=name,
        in_specs=[HBM] * (2 * n) + [SEM] * len(sems) + [ANY] * len(after),
        out_specs=[HBM] * n, out_shape=[_hbm_like(a) for a in lands],
        input_output_aliases={n + i: i for i in range(n)},
        compiler_params=SPLIT,
    )(*sums, *lands, *sems, *after)


def _small_all_reduce(part, after=()):
    _, w = part.shape

    def body(p_ref, *rest):
        o_ref, buf, send_sems, recv_sems = rest[len(after):]
        x, y, c, _ = _place()
        me = 4 * x + 2 * y + c
        buf[me] = jnp.sum(p_ref[...], axis=0, keepdims=True)
        copies = []
        for k in range(1, N_DEV):
            dx, dy, dc = (k >> 2) & 1, (k >> 1) & 1, k & 1
            copies.append(pltpu.make_async_remote_copy(
                src_ref=buf.at[me], dst_ref=buf.at[me], send_sem=send_sems.at[k - 1], recv_sem=recv_sems.at[k - 1],
                device_id=(x ^ dx, y ^ dy, c ^ dc), device_id_type=MESH))
        for cp in copies:
            cp.start()
        for cp in copies:
            cp.wait()
        tot = buf[0]
        for d in range(1, N_DEV):
            tot = tot + buf[d]
        o_ref[...] = tot
        loss = jnp.sum(tot[:, w - LANE:], axis=1, keepdims=True)
        o_ref[:, w - LANE:] = jnp.broadcast_to(loss, (1, LANE))

    return pl.pallas_call(
        body, name="small_all_reduce",
        in_specs=[IN_VMEM] + [ANY] * len(after), out_specs=IN_VMEM,
        out_shape=jax.ShapeDtypeStruct((1, w), F32),
        scratch_shapes=[pltpu.VMEM((N_DEV, 1, w), F32), pltpu.SemaphoreType.DMA((N_DEV - 1,)), pltpu.SemaphoreType.DMA((N_DEV - 1,))],
        compiler_params=pltpu.CompilerParams(vmem_limit_bytes=VMEM_LIMIT_BYTES),
    )(part, *after)


def _adamw(w, g, m, v):
    m = ADAM_B1 * m + (1.0 - ADAM_B1) * g
    v = ADAM_B2 * v + (1.0 - ADAM_B2) * (g * g)
    m_hat = m / (1.0 - ADAM_B1 ** ADAM_STEP)
    v_hat = v / (1.0 - ADAM_B2 ** ADAM_STEP)
    delta = -ADAM_LR * (m_hat / (jnp.sqrt(v_hat) + ADAM_EPS) + ADAM_WD * w)
    return delta, m, v


def _sum_adam(name, parts, sums, chip, w, m, v, after=()):
    _, r, c = w.shape
    n_after = len(after)
    by_rows = r % ROWS == 0 or r < ROWS
    tr, tc = (min(ROWS, r), c) if by_rows else (r, _fit(c, 512))
    at = (lambda i: (i, 0)) if by_rows else (lambda i: (0, i))

    def body(chip_ref, p_ref, own_ref, w_ref, m_ref, v_ref, *rest):
        g_ref, d_ref, mo_ref, vo_ref = rest[n_after:]
        g = None
        for k in range(N_CHIP):
            term = jnp.where(chip_ref[0] == k, own_ref[...], p_ref[k]).astype(F32)
            g = term if g is None else g + term
        g_ref[...] = g
        d_ref[...], mo_ref[...], vo_ref[...] = _adamw(w_ref[...], g, m_ref[...], v_ref[...])

    blk = pl.BlockSpec((None, tr, tc), lambda i, chip_ref: (0, *at(i)))
    out = jax.ShapeDtypeStruct((1, r, c), F32)
    return pl.pallas_call(
        body, name=name,
        grid_spec=pltpu.PrefetchScalarGridSpec(
            num_scalar_prefetch=1, grid=(r // tr if by_rows else c // tc,),
            in_specs=[pl.BlockSpec((N_CHIP, tr, tc), lambda i, chip_ref: (0, *at(i))),
                      pl.BlockSpec((None, tr, tc), lambda i, chip_ref: (chip_ref[0], *at(i))), blk, blk, blk]
            + [ANY] * n_after,
            out_specs=[blk] * 4),
        out_shape=[out] * 4,
        compiler_params=_params("parallel"),
    )(chip, parts, sums, w, m, v, *after)


def _adam_gains(total, ws, ms, vs):
    n = len(ws)
    widths = [w.shape[1] for w in ws]

    def body(t_ref, *refs):
        w_refs, m_refs, v_refs, outs = refs[:n], refs[n:2 * n], refs[2 * n:3 * n], refs[3 * n:]
        off = 0
        for i in range(n):
            g = t_ref[:, off:off + widths[i]]
            off += widths[i]
            g_ref, d_ref, mo_ref, vo_ref = outs[4 * i:4 * i + 4]
            g_ref[...] = g
            d_ref[...], mo_ref[...], vo_ref[...] = _adamw(w_refs[i][...], g, m_refs[i][...], v_refs[i][...])

    out = pl.pallas_call(
        body, name="adam_gains",
        out_shape=[jax.ShapeDtypeStruct(w.shape, F32) for w in ws for _ in range(4)],
    )(total, *ws, *ms, *vs)
    return [tuple(out[4 * i:4 * i + 4]) for i in range(n)]


def _adam_taps(total, first_col, device, w, m, v):
    _, n_taps, cw = w.shape
    col_block = lambda t, dev: (0, first_col // cw + t * N_DEV + dev[0])
    tap = pl.BlockSpec((None, 1, cw), lambda t, dev: (t, 0, 0))

    def body(dev_ref, t_ref, w_ref, m_ref, v_ref, g_ref, d_ref, mo_ref, vo_ref):
        g = t_ref[...]
        g_ref[...] = g
        d_ref[...], mo_ref[...], vo_ref[...] = _adamw(w_ref[...], g, m_ref[...], v_ref[...])

    shape3 = (n_taps, 1, cw)
    out = pl.pallas_call(
        body, name="adam_taps",
        grid_spec=pltpu.PrefetchScalarGridSpec(
            num_scalar_prefetch=1, grid=(n_taps,),
            in_specs=[pl.BlockSpec((1, cw), col_block), tap, tap, tap], out_specs=[tap] * 4),
        out_shape=[jax.ShapeDtypeStruct(shape3, F32)] * 4,
    )(device, total, w.reshape(shape3), m.reshape(shape3), v.reshape(shape3))
    return tuple(o.reshape(w.shape) for o in out)


def kernel(x, pre_mix_g, w_in, conv_w, q_norm_g, w_uq, kv_norm_g, w_ukv, conv_out_g, attn_out_g, w_o, post_mix_g, pre_mlp_g, w_up, w_down, post_mlp_g, loss_target, m_pre_mix_g, m_w_in, m_conv_w, m_q_norm_g, m_w_uq, m_kv_norm_g, m_w_ukv, m_conv_out_g, m_attn_out_g, m_w_o, m_post_mix_g, m_pre_mlp_g, m_w_up, m_w_down, m_post_mlp_g, v_pre_mix_g, v_w_in, v_conv_w, v_q_norm_g, v_w_uq, v_kv_norm_g, v_w_ukv, v_conv_out_g, v_attn_out_g, v_w_o, v_post_mix_g, v_pre_mlp_g, v_w_up, v_w_down, v_post_mlp_g):
    me = 4 * lax.axis_index("x") + 2 * lax.axis_index("y") + lax.axis_index("c")
    core = lax.axis_index("c").astype(jnp.int32).reshape(1)
    chip = (2 * lax.axis_index("x") + lax.axis_index("y")).astype(jnp.int32).reshape(1)
    gains = (pre_mix_g, q_norm_g, kv_norm_g, conv_out_g, attn_out_g, post_mix_g, pre_mlp_g, post_mlp_g)
    gain_m = (m_pre_mix_g, m_q_norm_g, m_kv_norm_g, m_conv_out_g, m_attn_out_g, m_post_mix_g, m_pre_mlp_g, m_post_mlp_g)
    gain_v = (v_pre_mix_g, v_q_norm_g, v_kv_norm_g, v_conv_out_g, v_attn_out_g, v_post_mix_g, v_pre_mlp_g, v_post_mlp_g)
    names = ("w_in", "w_uq", "w_ukv", "w_o", "w_up", "w_down")
    big = dict(zip(names, (w_in, w_uq, w_ukv, w_o, w_up, w_down)))
    big_m = dict(zip(names, (m_w_in, m_w_uq, m_w_ukv, m_w_o, m_w_up, m_w_down)))
    big_v = dict(zip(names, (v_w_in, v_w_uq, v_w_ukv, v_w_o, v_w_up, v_w_down)))
    n_heads = attn_out_g.shape[1] // HEAD
    n_taps = conv_w.shape[1]

    gathered = ("w_in", "conv", "w_uq", "w_ukv", "w_o", "w_up", "w_down")
    gather_groups = ((0, 1), (2, 3, 4), (5,), (6,))
    taps = jnp.pad(conv_w[0], ((0, SUBLANE - n_taps), (0, 0)))
    relayed_groups = (0, 2, 3)
    sems1, shards, lands, token = _gather_start("gather_start_first", [w_in[0].astype(BF16), taps], ((0, 1),), relayed=(0,))
    sems1, shards, lands = list(sems1), list(shards), list(lands)
    behind = token[0, 0]
    rest = [(big[nm][0] + behind).astype(BF16) for nm in gathered[2:]]

    def start_rest(after):
        sems_b, shards_b, lands_b, started = _gather_start("gather_start_rest", rest, ((0, 1, 2), (3,), (4,)), relayed=(1, 2),
                                                          after=after)
        sems1.extend(sems_b)
        shards.extend(shards_b)
        lands.extend(lands_b)
        return started

    cols = lambda a: jnp.concatenate([a[j] for j in range(N_DEV)], axis=1)
    rows = lambda a: a.reshape(N_DEV * a.shape[1], a.shape[2])
    ready = {
        "w_in": _join_col_shards,
        "conv": lambda a: cols(a)[:n_taps],
        "w_uq": lambda a: _permute_q_cols(cols(a), n_heads),
        "w_ukv": cols, "w_o": rows, "w_up": lambda a: a, "w_down": rows,
    }

    class Weights:
        def __init__(self):
            self.passed, self.relayed = {}, {}

        def forward(self, group, after):
            idx = gather_groups[group]
            if group == 0:
                after = (*after, *rest)
            self.passed[group] = _gather_forward(f"gather_forward_{group}", [shards[i] for i in idx], [lands[i] for i in idx],
                                                 *sems1[group], after, relayed=group in relayed_groups)
            return tuple(self.passed[group][1])

        def relay(self, group, after):
            sems2, mid = self.passed[group]
            self.relayed[group], mid = _gather_relay_forward(f"gather_relay_{group}", mid, *sems2, after)
            self.passed[group] = (sems2, mid)
            if group == 0:
                start_rest(tuple(mid))
            return tuple(mid)

        def ready(self, group, after):
            sems2, mid = self.passed[group]
            full = _gather_wait(f"gather_wait_{group}", mid, *sems2, after, relay_sems=self.relayed.get(group))
            out = []
            for i, a in zip(gather_groups[group], full):
                a = lax.dynamic_update_index_in_dim(a, shards[i], me, 0)
                out.append(ready[gathered[i]](a))
            return out

    weights = Weights()

    col_blocks = lambda g: g.reshape(g.shape[0], N_DEV, g.shape[1] // N_DEV).transpose(1, 0, 2)
    row_blocks = lambda g: g.reshape(N_DEV, g.shape[0] // N_DEV, g.shape[1])
    grad_groups = (("w_down",), ("w_up",), ("w_o",), ("w_uq", "w_ukv"), ("w_in",))
    transposed = {"w_in": w_in.shape[2], "w_uq": w_uq.shape[2]}
    to_blocks = {
        "w_in": lambda g: g, "w_uq": lambda g: _unpermute_q_rows(g, n_heads),
        "w_ukv": col_blocks, "w_o": row_blocks, "w_up": lambda g: g, "w_down": row_blocks,
    }
    in_flight = []

    class Grads:
        def __init__(self):
            self.core = core
            self.away = {}

        def send_sums(self, group, sums):
            sems, sums, parts, tok = _chip_send_start(f"chip_send_start_{group}", list(sums))
            in_flight.append((sems, sums, parts))
            return (tok,)

        def full(self, group, arrays, received=None):
            nms = grad_groups[group]
            if received is None:
                blocks = [to_blocks[nm](g) for nm, g in zip(nms, arrays)]
                got = _pair_exchange(f"pair_exchange_{group}", blocks, [transposed.get(nm) for nm in nms])
            else:
                blocks, got = [self.away[group][1]], [self.received(group, received)]
            sums = [(_pair_sum_rows if nm in transposed else _pair_sum)(f"pair_sum_{nm}", g, r, core)
                    for nm, g, r in zip(nms, blocks, got)]
            return self.send_sums(group, sums)

        def send_away(self, group, half):
            nm = grad_groups[group][0]
            rows = transposed.get(nm)
            sems, src, land, tok = _pair_send_start(f"pair_send_start_{group}", half if rows is None else to_blocks[nm](half), rows)
            self.away[group] = (sems, src, land, rows)
            return (tok,)

        def received(self, group, after):
            sems, src, land, rows = self.away[group]
            return _pair_send_wait(f"pair_send_wait_{group}", sems, src, land, after, rows)

    grad_x, small = _local_step(x[0], loss_target[0], gains, weights, Grads(), first_after=(token,))

    big_out = {}

    def update(tag, first, last, after):
        groups = in_flight[first:last]
        parts = _chip_send_wait("chip_send_wait_" + tag, groups, after)
        nms = [nm for grp in grad_groups[first:last] for nm in grp]
        sums = [a for _, s, _ in groups for a in s]
        for nm, p, s in zip(nms, parts, sums):
            view = (lambda a: jnp.swapaxes(a, 1, 2)) if nm in transposed else (lambda a: a)
            out = _sum_adam("adam_" + nm, p, s, chip, view(big[nm]), view(big_m[nm]), view(big_v[nm]), after=after)
            after = (out[0],)
            big_out[nm] = [view(o) for o in out]
        return after

    after = update("early", 0, len(in_flight) - 1, (grad_x,))
    total = _small_all_reduce(small, after=after)
    update("late", len(in_flight) - 1, len(in_flight), (total,))
    big_out = [big_out[nm] for nm in names]

    gain_out = _adam_gains(total, gains, gain_m, gain_v)
    taps_out = _adam_taps(total, sum(g.shape[1] for g in gains), me.astype(jnp.int32).reshape(1), conv_w, m_conv_w, v_conv_w)
    loss = total[0, total.shape[1] - 1]

    order = (0, "w_in", "conv", 1, "w_uq", 2, "w_ukv", 3, 4, "w_o", 5, 6, "w_up", "w_down", 7)
    by_name = dict(zip(names, big_out))
    outs = [loss, grad_x[None]]
    for kind in range(4):
        for item in order:
            if item == "conv":
                outs.append(taps_out[kind])
            elif isinstance(item, int):
                outs.append(gain_out[item][kind])
            else:
                outs.append(by_name[item][kind])
    return tuple(outs)
```

```python
import math

import jax
import jax.numpy as jnp
from jax import lax
from jax.experimental import pallas as pl
from jax.experimental.pallas import tpu as pltpu

F32 = jnp.float32
BF16 = jnp.bfloat16

EPS = 1e-6
NEG_INF = -1e30
HEAD = 128
ROPE = 64
QK = HEAD + ROPE
CHUNK = 64
ROPE_THETA = 10000.0
ADAM_LR, ADAM_B1, ADAM_B2, ADAM_EPS, ADAM_WD, ADAM_STEP = 0.001, 0.9, 0.999, 1e-08, 0.01, 10

LANE = 128
SUBLANE = 8
VMEM_LIMIT_BYTES = 56 * 1024 * 1024

N_DEV = 8
N_CHIP = 4
MESH = pl.DeviceIdType.MESH


def _params(*sem):
    return pltpu.CompilerParams(dimension_semantics=sem, vmem_limit_bytes=VMEM_LIMIT_BYTES)


ANY = pl.BlockSpec(memory_space=pl.ANY)


def _call(body, *, in_specs, after=(), **kw):
    n_in, n_after = len(in_specs), len(after)

    def ordered(*refs):
        body(*refs[:n_in], *refs[n_in + n_after:])

    call = pl.pallas_call(ordered, in_specs=[*in_specs, *[ANY] * n_after], **kw)
    return lambda *operands: call(*operands, *after)


def _sublane_sum(v):
    r, w = v.shape
    return jnp.sum(v.reshape(r // SUBLANE, SUBLANE, w), axis=0)


def _rstd(x):
    return lax.rsqrt(jnp.mean(x * x, axis=-1, keepdims=True) + EPS)


def _rms_bwd(x, g, dy):
    r = _rstd(x)
    xh = x * r
    dxh = dy * g
    dx = r * (dxh - xh * jnp.mean(dxh * xh, axis=-1, keepdims=True))
    return dx, dy * xh


def _accumulate(ref, val, step):
    @pl.when(step == 0)
    def _():
        ref[...] = val

    @pl.when(step > 0)
    def _():
        ref[...] += val


NN = ((1,), (0,))
NT = ((1,), (1,))
TN = ((0,), (0,))


def _matmul(name, a, b, *, grid, a_spec, b_spec, out_shape, out_specs, contract, nk=1, acc_shape=None,
            extras=(), extra_specs=(), epilogue=None, after=()):
    multi = isinstance(out_shape, (tuple, list))
    out_shapes = tuple(out_shape) if multi else (out_shape,)
    n_out = len(out_shapes)
    n_extra = len(extras)

    def body(a_ref, b_ref, *rest):
        x_refs = rest[:n_extra]
        o_refs = rest[n_extra:n_extra + n_out]

        def emit(acc):
            vals = epilogue(acc, *[r[...] for r in x_refs]) if epilogue else (acc,)
            for r, v in zip(o_refs, vals):
                r[...] = v.astype(r.dtype)

        p = lax.dot_general(a_ref[...], b_ref[...], (contract, ((), ())), preferred_element_type=F32)
        if nk == 1:
            emit(p)
        else:
            acc_ref = rest[n_extra + n_out]
            k = pl.program_id(2)
            _accumulate(acc_ref, p, k)

            @pl.when(k == nk - 1)
            def _():
                emit(acc_ref[...])

    sem = ("parallel", "parallel") + (("arbitrary",) if nk > 1 else ())
    return _call(
        body, name=name, grid=grid, after=after,
        in_specs=[a_spec, b_spec, *extra_specs],
        out_specs=out_specs,
        out_shape=out_shape,
        scratch_shapes=[pltpu.VMEM(acc_shape, F32)] if nk > 1 else [],
        compiler_params=_params(*sem),
    )(a, b, *extras)


def _fit(n, tile):
    if n <= tile:
        return n
    t = tile - tile % LANE
    while n % t:
        t -= LANE
    return t


def _mm_nn(name, a, b, out_dtype, tm, tn, after=()):
    m, k = a.shape
    n = b.shape[1]
    tm, tn = _fit(m, tm), _fit(n, tn)
    return _matmul(name, a, b, grid=(m // tm, n // tn), after=after,
                   a_spec=pl.BlockSpec((tm, k), lambda i, j: (i, 0)),
                   b_spec=pl.BlockSpec((k, tn), lambda i, j: (0, j)),
                   out_shape=jax.ShapeDtypeStruct((m, n), out_dtype),
                   out_specs=pl.BlockSpec((tm, tn), lambda i, j: (i, j)), contract=NN)


def _mm_nt(name, a, b, out_dtype, tm, tn, after=()):
    m, k = a.shape
    n = b.shape[0]
    tm, tn = _fit(m, tm), _fit(n, tn)
    return _matmul(name, a, b, grid=(m // tm, n // tn), after=after,
                   a_spec=pl.BlockSpec((tm, k), lambda i, j: (i, 0)),
                   b_spec=pl.BlockSpec((tn, k), lambda i, j: (j, 0)),
                   out_shape=jax.ShapeDtypeStruct((m, n), out_dtype),
                   out_specs=pl.BlockSpec((tm, tn), lambda i, j: (i, j)), contract=NT)


def _mm_tn(name, a, b, out_dtype, tm, tn):
    s, m = a.shape
    n = b.shape[1]
    tm, tn = _fit(m, tm), _fit(n, tn)
    return _matmul(name, a, b, grid=(m // tm, n // tn),
                   a_spec=pl.BlockSpec((s, tm), lambda i, j: (0, i)),
                   b_spec=pl.BlockSpec((s, tn), lambda i, j: (0, j)),
                   out_shape=jax.ShapeDtypeStruct((m, n), out_dtype),
                   out_specs=pl.BlockSpec((tm, tn), lambda i, j: (i, j)), contract=TN)


ROWS = 256


def _row_spec(rows, width):
    return pl.BlockSpec((rows, width), lambda i: (i, 0))


def _fixed_spec(rows, width):
    return pl.BlockSpec((rows, width), lambda i: (0, 0))


def _column_pieces(rows, start, width):
    piece = math.gcd(start, width)
    assert piece % LANE == 0
    return [pl.BlockSpec((rows, piece), lambda i, b=start // piece + p: (i, b)) for p in range(width // piece)]


def _rms_fwd(name, x, g, cols=None, after=()):
    s = x.shape[0]
    start, w = cols or (0, x.shape[1])
    rows = min(ROWS, s)
    pieces = _column_pieces(rows, start, w) if cols else [_row_spec(rows, w)]
    n = len(pieces)

    def body(*refs):
        g_ref, o_ref = refs[n:]
        xv = refs[0][...] if n == 1 else jnp.concatenate([r[...] for r in refs[:n]], axis=1)
        o_ref[...] = (xv * _rstd(xv) * g_ref[...]).astype(o_ref.dtype)

    return _call(
        body, name=name, grid=(s // rows,), after=after,
        in_specs=[*pieces, _fixed_spec(1, w)],
        out_specs=_row_spec(rows, w),
        out_shape=jax.ShapeDtypeStruct((s, w), BF16),
        compiler_params=_params("parallel"),
    )(*[x] * n, g)


def _rms_bwd_call(name, x, g, dy, out_dtype, cols=None, after=()):
    s = x.shape[0]
    start, w = cols or (0, x.shape[1])
    rows = min(ROWS, s)
    pieces = _column_pieces(rows, start, w) if cols else [_row_spec(rows, w)]
    n = len(pieces)

    def body(*refs):
        g_ref, dy_ref, dx_ref, dg_ref = refs[n:]
        xv = refs[0][...] if n == 1 else jnp.concatenate([r[...] for r in refs[:n]], axis=1)
        dx, dgc = _rms_bwd(xv, g_ref[...], dy_ref[...].astype(F32))
        dx_ref[...] = dx.astype(dx_ref.dtype)
        _accumulate(dg_ref, _sublane_sum(dgc), pl.program_id(0))

    return _call(
        body, name=name, grid=(s // rows,), after=after,
        in_specs=[*pieces, _fixed_spec(1, w), _row_spec(rows, w)],
        out_specs=[_row_spec(rows, w), _fixed_spec(SUBLANE, w)],
        out_shape=[jax.ShapeDtypeStruct((s, w), out_dtype), jax.ShapeDtypeStruct((SUBLANE, w), F32)],
        compiler_params=_params("arbitrary"),
    )(*[x] * n, g, dy)


def _norm_up(name, x, cols, g, w, after=()):
    s = x.shape[0]
    start, width = cols
    n = w.shape[1]
    tm = min(TILE_M, s)
    pieces = _column_pieces(tm, start, width)
    n_p = len(pieces)

    def body(*refs):
        g_ref, w_ref, xn_ref, o_ref = refs[n_p:]
        xv = refs[0][...] if n_p == 1 else jnp.concatenate([r[...] for r in refs[:n_p]], axis=1)
        xn = (xv * _rstd(xv) * g_ref[...]).astype(BF16)
        xn_ref[...] = xn
        o_ref[...] = jnp.dot(xn, w_ref[...], preferred_element_type=F32)

    return _call(
        body, name=name, grid=(s // tm,), after=after,
        in_specs=[*pieces, _fixed_spec(1, width), _fixed_spec(width, n)],
        out_specs=[_row_spec(tm, width), _row_spec(tm, n)],
        out_shape=[jax.ShapeDtypeStruct((s, width), BF16), jax.ShapeDtypeStruct((s, n), F32)],
        compiler_params=_params("parallel"),
    )(*[x] * n_p, g, w)


def _up_norm_bwd(name, dy, w, x, cols, g, after=()):
    s, n = dy.shape
    start, width = cols
    tm = min(TILE_M, s)
    pieces = _column_pieces(tm, start, width)
    n_p = len(pieces)

    def body(dy_ref, w_ref, *refs):
        g_ref, dx_ref, dg_ref = refs[n_p:]
        xv = refs[0][...] if n_p == 1 else jnp.concatenate([r[...] for r in refs[:n_p]], axis=1)
        dxn = lax.dot_general(dy_ref[...], w_ref[...], (NT, ((), ())), preferred_element_type=F32)
        dx, dgc = _rms_bwd(xv, g_ref[...], dxn)
        dx_ref[...] = dx.astype(dx_ref.dtype)
        _accumulate(dg_ref, _sublane_sum(dgc), pl.program_id(0))

    return _call(
        body, name=name, grid=(s // tm,), after=after,
        in_specs=[_row_spec(tm, n), _fixed_spec(width, n), *pieces, _fixed_spec(1, width)],
        out_specs=[_row_spec(tm, width), _fixed_spec(SUBLANE, width)],
        out_shape=[jax.ShapeDtypeStruct((s, width), BF16), jax.ShapeDtypeStruct((SUBLANE, width), F32)],
        compiler_params=_params("arbitrary"),
    )(dy, w, *[x] * n_p, g)


def _mid_fwd(x, y, g_post, g_pre, after=()):
    s, w = x.shape
    rows = min(ROWS, s)

    def body(x_ref, y_ref, gp_ref, gq_ref, x2_ref, h2_ref):
        yv = y_ref[...]
        x2 = x_ref[...] + yv * _rstd(yv) * gp_ref[...]
        x2_ref[...] = x2
        h2_ref[...] = (x2 * _rstd(x2) * gq_ref[...]).astype(h2_ref.dtype)

    return _call(
        body, name="mid_fwd", grid=(s // rows,), after=after,
        in_specs=[_row_spec(rows, w), _row_spec(rows, w), _fixed_spec(1, w), _fixed_spec(1, w)],
        out_specs=[_row_spec(rows, w), _row_spec(rows, w)],
        out_shape=[jax.ShapeDtypeStruct((s, w), F32), jax.ShapeDtypeStruct((s, w), BF16)],
        compiler_params=_params("parallel"),
    )(x, y, g_post, g_pre)


def _head(m, x2, tgt, g):
    s, w = m.shape
    rows = min(ROWS, s)

    def body(m_ref, x2_ref, t_ref, g_ref, dout_ref, dm_ref, dg_ref, loss_ref):
        mv = m_ref[...]
        gv = g_ref[...]
        out = x2_ref[...] + mv * _rstd(mv) * gv
        err = out - t_ref[...]
        dout = err * (1.0 / w)
        dout_ref[...] = dout
        dm, dgc = _rms_bwd(mv, gv, dout)
        dm_ref[...] = dm.astype(dm_ref.dtype)
        sq = err * err
        lanes = sq[:, 0:LANE]
        for j in range(1, w // LANE):
            lanes = lanes + sq[:, j * LANE:(j + 1) * LANE]
        step = pl.program_id(0)
        _accumulate(dg_ref, _sublane_sum(dgc), step)
        _accumulate(loss_ref, _sublane_sum(lanes) * (0.5 / w), step)

    return pl.pallas_call(
        body, name="head", grid=(s // rows,),
        in_specs=[_row_spec(rows, w), _row_spec(rows, w), _row_spec(rows, w), _fixed_spec(1, w)],
        out_specs=[_row_spec(rows, w), _row_spec(rows, w), _fixed_spec(SUBLANE, w), _fixed_spec(SUBLANE, LANE)],
        out_shape=[jax.ShapeDtypeStruct((s, w), F32), jax.ShapeDtypeStruct((s, w), BF16),
                   jax.ShapeDtypeStruct((SUBLANE, w), F32), jax.ShapeDtypeStruct((SUBLANE, LANE), F32)],
        compiler_params=_params("arbitrary"),
    )(m, x2, tgt, g)


def _mid_bwd(x2, y, d_out, d_h2, g_pre, g_post, after=()):
    s, w = x2.shape
    rows = min(ROWS, s)

    def body(x2_ref, y_ref, dout_ref, dh2_ref, gq_ref, gp_ref, dx2_ref, dy_ref, dgq_ref, dgp_ref):
        dx, dgq = _rms_bwd(x2_ref[...], gq_ref[...], dh2_ref[...])
        dx2 = dout_ref[...] + dx
        dx2_ref[...] = dx2
        dy, dgp = _rms_bwd(y_ref[...], gp_ref[...], dx2)
        dy_ref[...] = dy.astype(dy_ref.dtype)
        step = pl.program_id(0)
        _accumulate(dgq_ref, _sublane_sum(dgq), step)
        _accumulate(dgp_ref, _sublane_sum(dgp), step)

    return _call(
        body, name="mid_bwd", grid=(s // rows,), after=after,
        in_specs=[_row_spec(rows, w)] * 4 + [_fixed_spec(1, w)] * 2,
        out_specs=[_row_spec(rows, w), _row_spec(rows, w), _fixed_spec(SUBLANE, w), _fixed_spec(SUBLANE, w)],
        out_shape=[jax.ShapeDtypeStruct((s, w), F32), jax.ShapeDtypeStruct((s, w), BF16),
                   jax.ShapeDtypeStruct((SUBLANE, w), F32), jax.ShapeDtypeStruct((SUBLANE, w), F32)],
        compiler_params=_params("arbitrary"),
    )(x2, y, d_out, d_h2, g_pre, g_post)


def _first_bwd(x, g, d_h1, d_x2, after=()):
    s, w = x.shape
    rows = min(ROWS, s)

    def body(x_ref, g_ref, dh_ref, dx2_ref, dx_ref, dg_ref):
        dx, dgc = _rms_bwd(x_ref[...], g_ref[...], dh_ref[...])
        dx_ref[...] = dx2_ref[...] + dx
        _accumulate(dg_ref, _sublane_sum(dgc), pl.program_id(0))

    return _call(
        body, name="first_bwd", grid=(s // rows,), after=after,
        in_specs=[_row_spec(rows, w), _fixed_spec(1, w), _row_spec(rows, w), _row_spec(rows, w)],
        out_specs=[_row_spec(rows, w), _fixed_spec(SUBLANE, w)],
        out_shape=[jax.ShapeDtypeStruct((s, w), F32), jax.ShapeDtypeStruct((SUBLANE, w), F32)],
        compiler_params=_params("arbitrary"),
    )(x, g, d_h1, d_x2)


def _shift_down(v, k):
    t = lax.broadcasted_iota(jnp.int32, v.shape, 0)
    return jnp.where(t >= k, pltpu.roll(v, k, 0), 0.0)


def _shift_up(v, k):
    n = v.shape[0]
    t = lax.broadcasted_iota(jnp.int32, v.shape, 0)
    return jnp.where(t < n - k, pltpu.roll(v, n - k, 0), 0.0)


def _conv_core(u, b, c, w):
    z = c * u
    conv = w[0:1, :] * _shift_down(z, 2) + w[1:2, :] * _shift_down(z, 1) + w[2:3, :] * z
    return z, conv, b * conv


def _conv_fwd(proj, conv_w, g, n_groups, out_width, after=()):
    s = proj.shape[0]

    def body(u_ref, b_ref, c_ref, w_ref, g_ref, o_ref):
        _, _, yr = _conv_core(u_ref[...], b_ref[...], c_ref[...], w_ref[...])
        o_ref[...] = (yr * _rstd(yr) * g_ref[...]).astype(o_ref.dtype)

    col = lambda k: pl.BlockSpec((s, HEAD), lambda i: (0, k * n_groups + i))
    return _call(
        body, name="conv_fwd", grid=(n_groups,), after=after,
        in_specs=[col(0), col(1), col(2), pl.BlockSpec((3, HEAD), lambda i: (0, i)), pl.BlockSpec((1, HEAD), lambda i: (0, i))],
        out_specs=pl.BlockSpec((s, HEAD), lambda i: (0, i)),
        out_shape=jax.ShapeDtypeStruct((s, out_width), BF16),
        compiler_params=_params("parallel"),
    )(proj, proj, proj, conv_w, g)


def _conv_bwd(proj, d_mix, conv_w, g, n_groups):
    s = proj.shape[0]
    width = n_groups * HEAD

    def body(u_ref, b_ref, c_ref, dy_ref, w_ref, g_ref, du_ref, db_ref, dc_ref, dg_ref, dw_ref):
        u, b, c, w = u_ref[...], b_ref[...], c_ref[...], w_ref[...]
        z, conv, yr = _conv_core(u, b, c, w)
        dyr, dgc = _rms_bwd(yr, g_ref[...], dy_ref[...])
        dconv = dyr * b
        db_ref[...] = (dyr * conv).astype(db_ref.dtype)
        dz = w[2:3, :] * dconv + w[1:2, :] * _shift_up(dconv, 1) + w[0:1, :] * _shift_up(dconv, 2)
        dc_ref[...] = (dz * u).astype(dc_ref.dtype)
        du_ref[...] = (dz * c).astype(du_ref.dtype)
        dg_ref[...] = _sublane_sum(dgc)
        dw_ref[0] = _sublane_sum(dconv * _shift_down(z, 2))
        dw_ref[1] = _sublane_sum(dconv * _shift_down(z, 1))
        dw_ref[2] = _sublane_sum(dconv * z)

    col = lambda k: pl.BlockSpec((s, HEAD), lambda i: (0, k * n_groups + i))
    grp = pl.BlockSpec((s, HEAD), lambda i: (0, i))
    return pl.pallas_call(
        body, name="conv_bwd", grid=(n_groups,),
        in_specs=[col(0), col(1), col(2), grp, pl.BlockSpec((3, HEAD), lambda i: (0, i)), pl.BlockSpec((1, HEAD), lambda i: (0, i))],
        out_specs=[grp, grp, grp, pl.BlockSpec((SUBLANE, HEAD), lambda i: (0, i)),
                   pl.BlockSpec((3, SUBLANE, HEAD), lambda i: (0, 0, i))],
        out_shape=[jax.ShapeDtypeStruct((s, width), BF16)] * 3
        + [jax.ShapeDtypeStruct((SUBLANE, width), F32), jax.ShapeDtypeStruct((3, SUBLANE, width), F32)],
        compiler_params=_params("parallel"),
    )(proj, proj, proj, d_mix, conv_w, g)


def _rope_tables(s, n_heads):
    pos = jnp.arange(s, dtype=F32)
    inv_freq = jnp.power(ROPE_THETA, -jnp.arange(0, ROPE, 2, dtype=F32) / ROPE)
    ang = pos[:, None] * inv_freq[None, :]
    cos, sin = jnp.cos(ang), jnp.sin(ang)
    cs = jnp.concatenate([cos, cos], axis=1)
    sn = jnp.concatenate([-sin, sin], axis=1)
    pad = jnp.zeros((s, LANE - ROPE), F32)
    return (jnp.tile(cs, (1, n_heads)), jnp.tile(sn, (1, n_heads)),
            jnp.concatenate([cs, pad], axis=1), jnp.concatenate([sn, pad], axis=1))


def _swap_halves(v):
    w = v.shape[1]
    lane = lax.broadcasted_iota(jnp.int32, v.shape, 1)
    first = (lane % ROPE) < (ROPE // 2)
    return jnp.where(first, pltpu.roll(v, w - ROPE // 2, 1), pltpu.roll(v, ROPE // 2, 1))


def _pack_heads(q, kv, proj, kr_col, tables, n_heads, after=()):
    s = q.shape[0]
    rows = min(ROWS, s)
    cq, sq, ck, sk = tables
    wq = n_heads * ROPE

    def body(q_ref, kv_ref, kr_ref, cq_ref, sq_ref, ck_ref, sk_ref, qo_ref, ko_ref, vo_ref):
        qr = q_ref[:, n_heads * HEAD:]
        qr = qr * cq_ref[...] + _swap_halves(qr) * sq_ref[...]
        krv = kr_ref[...]
        krv = krv * ck_ref[...] + _swap_halves(krv) * sk_ref[...]
        for h in range(n_heads):
            qo_ref[h] = jnp.concatenate([q_ref[:, h * HEAD:(h + 1) * HEAD], qr[:, h * ROPE:(h + 1) * ROPE]], axis=1).astype(BF16)
            ko_ref[h] = jnp.concatenate([kv_ref[:, 2 * h * HEAD:(2 * h + 1) * HEAD], krv[:, :ROPE]], axis=1).astype(BF16)
            vo_ref[h] = kv_ref[:, (2 * h + 1) * HEAD:(2 * h + 2) * HEAD].astype(BF16)

    hs = lambda w: pl.BlockSpec((n_heads, rows, w), lambda i: (0, i, 0))
    return _call(
        body, name="pack_heads", grid=(s // rows,), after=after,
        in_specs=[_row_spec(rows, q.shape[1]), _row_spec(rows, kv.shape[1]), pl.BlockSpec((rows, LANE), lambda i: (i, kr_col // LANE)),
                  _row_spec(rows, wq), _row_spec(rows, wq), _row_spec(rows, LANE), _row_spec(rows, LANE)],
        out_specs=[hs(QK), hs(QK), hs(HEAD)],
        out_shape=[jax.ShapeDtypeStruct((n_heads, s, QK), BF16), jax.ShapeDtypeStruct((n_heads, s, QK), BF16),
                   jax.ShapeDtypeStruct((n_heads, s, HEAD), BF16)],
        compiler_params=_params("parallel"),
    )(q, kv, proj, cq, sq, ck, sk)


def _unpack_heads(dq, dk, dv, tables, n_heads):
    s = dq.shape[1]
    rows = min(ROWS, s)
    cq, sq, ck, sk = tables
    wq = n_heads * ROPE

    def body(dq_ref, dk_ref, dv_ref, cq_ref, sq_ref, ck_ref, sk_ref, qo_ref, kvo_ref, kro_ref):
        dqr = jnp.concatenate([dq_ref[h][:, HEAD:] for h in range(n_heads)], axis=1)
        dqr = dqr * cq_ref[...] - _swap_halves(dqr) * sq_ref[...]
        dkr = dk_ref[0][:, HEAD:]
        for h in range(1, n_heads):
            dkr = dkr + dk_ref[h][:, HEAD:]
        dkr = jnp.concatenate([dkr, jnp.zeros((rows, LANE - ROPE), F32)], axis=1)
        dkr = dkr * ck_ref[...] - _swap_halves(dkr) * sk_ref[...]
        kro_ref[...] = dkr.astype(kro_ref.dtype)
        qo_ref[:, n_heads * HEAD:] = dqr.astype(qo_ref.dtype)
        for h in range(n_heads):
            qo_ref[:, h * HEAD:(h + 1) * HEAD] = dq_ref[h][:, :HEAD].astype(qo_ref.dtype)
            kvo_ref[:, 2 * h * HEAD:(2 * h + 1) * HEAD] = dk_ref[h][:, :HEAD].astype(kvo_ref.dtype)
            kvo_ref[:, (2 * h + 1) * HEAD:(2 * h + 2) * HEAD] = dv_ref[h].astype(kvo_ref.dtype)

    hs = lambda w: pl.BlockSpec((n_heads, rows, w), lambda i: (0, i, 0))
    return pl.pallas_call(
        body, name="unpack_heads", grid=(s // rows,),
        in_specs=[hs(QK), hs(QK), hs(HEAD), _row_spec(rows, wq), _row_spec(rows, wq), _row_spec(rows, LANE), _row_spec(rows, LANE)],
        out_specs=[_row_spec(rows, n_heads * QK), _row_spec(rows, 2 * n_heads * HEAD), _row_spec(rows, LANE)],
        out_shape=[jax.ShapeDtypeStruct((s, n_heads * QK), BF16), jax.ShapeDtypeStruct((s, 2 * n_heads * HEAD), BF16),
                   jax.ShapeDtypeStruct((s, LANE), BF16)],
        compiler_params=_params("parallel"),
    )(dq, dk, dv, cq, sq, ck, sk)


TQ = 256


LOG2_E = 1.4426950408889634


def _softmax_parts(q, k):
    tq, n_keys = q.shape[0], k.shape[0]
    sc = lax.dot_general(q, k, (NT, ((), ())), preferred_element_type=F32) * (QK ** -0.5 * LOG2_E)
    row = lax.broadcasted_iota(jnp.int32, (tq, tq), 0)
    col = lax.broadcasted_iota(jnp.int32, (tq, tq), 1)
    own = jnp.where(col // CHUNK <= row // CHUNK, sc[:, n_keys - tq:], NEG_INF)
    sc = own if n_keys == tq else jnp.concatenate([sc[:, :n_keys - tq], own], axis=1)
    e = jnp.exp2(sc - jnp.max(sc, axis=-1, keepdims=True))
    return e, 1.0 / jnp.sum(e, axis=-1, keepdims=True)


def _attn_fwd(q, k, v, g, mix, col0):
    n_heads, s, _ = q.shape
    tq = min(TQ, s)
    assert tq % CHUNK == 0 and s % tq == 0

    def body(q_ref, k_ref, v_ref, g_ref, mix_ref, o_ref, y_ref):
        for c in range(s // tq):
            rows, n_keys = pl.ds(c * tq, tq), (c + 1) * tq
            e, inv = _softmax_parts(q_ref[rows, :], k_ref[0:n_keys, :])
            o = jnp.dot(e.astype(BF16), v_ref[0:n_keys, :], preferred_element_type=F32) * inv
            o_ref[rows, :] = o
            y_ref[rows, :] = (o * _rstd(o) * g_ref[...]).astype(y_ref.dtype)

    head = lambda w: pl.BlockSpec((None, s, w), lambda h: (h, 0, 0))
    return pl.pallas_call(
        body, name="attn_fwd", grid=(n_heads,),
        in_specs=[head(QK), head(QK), head(HEAD), pl.BlockSpec((1, HEAD), lambda h: (0, h)), ANY],
        out_specs=[head(HEAD), pl.BlockSpec((s, HEAD), lambda h: (0, col0 // HEAD + h))],
        out_shape=[jax.ShapeDtypeStruct((n_heads, s, HEAD), F32), jax.ShapeDtypeStruct(mix.shape, mix.dtype)],
        input_output_aliases={4: 1},
        compiler_params=_params("parallel"),
    )(q, k, v, g, mix)


def _attn_bwd(q, k, v, o, d_mix, g, col0, after=()):
    n_heads, s, _ = q.shape
    tq = min(TQ, s)

    def body(q_ref, k_ref, v_ref, o_ref, dy_ref, g_ref, dq_ref, dk_ref, dv_ref, dg_ref):
        dg = None
        for c in reversed(range(s // tq)):
            rows, n_keys = pl.ds(c * tq, tq), (c + 1) * tq
            qv, kv_, vv = q_ref[rows, :], k_ref[0:n_keys, :], v_ref[0:n_keys, :]
            do, dgc = _rms_bwd(o_ref[rows, :], g_ref[...], dy_ref[rows, :])
            do = do.astype(BF16)
            dg = _sublane_sum(dgc) if dg is None else dg + _sublane_sum(dgc)
            e, inv = _softmax_parts(qv, kv_)
            p = e * inv
            dp = lax.dot_general(do, vv, (NT, ((), ())), preferred_element_type=F32)
            ds = (p * (dp - jnp.sum(p * dp, axis=-1, keepdims=True)) * (QK ** -0.5)).astype(BF16)
            dq_ref[rows, :] = jnp.dot(ds, kv_, preferred_element_type=F32)
            dk = lax.dot_general(ds, qv, (TN, ((), ())), preferred_element_type=F32)
            dv = lax.dot_general(p.astype(BF16), do, (TN, ((), ())), preferred_element_type=F32)
            if n_keys == s:
                dk_ref[...] = dk
                dv_ref[...] = dv
            else:
                dk_ref[0:n_keys, :] += dk
                dv_ref[0:n_keys, :] += dv
        dg_ref[...] = dg

    c0 = col0 // HEAD
    head = lambda w: pl.BlockSpec((None, s, w), lambda h: (h, 0, 0))
    return _call(
        body, name="attn_bwd", grid=(n_heads,), after=after,
        in_specs=[head(QK), head(QK), head(HEAD), head(HEAD), pl.BlockSpec((s, HEAD), lambda h: (0, c0 + h)),
                  pl.BlockSpec((1, HEAD), lambda h: (0, h))],
        out_specs=[head(QK), head(QK), head(HEAD), pl.BlockSpec((SUBLANE, HEAD), lambda h: (0, h))],
        out_shape=[jax.ShapeDtypeStruct((n_heads, s, QK), F32), jax.ShapeDtypeStruct((n_heads, s, QK), F32),
                   jax.ShapeDtypeStruct((n_heads, s, HEAD), F32), jax.ShapeDtypeStruct((SUBLANE, n_heads * HEAD), F32)],
        compiler_params=_params("parallel"),
    )(q, k, v, o, d_mix, g)


TILE_M = 1024
TILE_N = 1024


def _up_fwd(h2, w_up):
    s, d = h2.shape
    nb, _, fb = w_up.shape
    tm = min(TILE_M,s)

    def epilogue(acc):
        r = jnp.maximum(acc, 0.0)
        return r * r, r

    blk = pl.BlockSpec((tm, fb), lambda i, j: (i, j))
    return _matmul("up_fwd", h2, w_up, grid=(s // tm, nb),
                   a_spec=pl.BlockSpec((tm, d), lambda i, j: (i, 0)),
                   b_spec=pl.BlockSpec((None, d, fb), lambda i, j: (j, 0, 0)),
                   out_shape=[jax.ShapeDtypeStruct((s, nb * fb), BF16)] * 2, out_specs=[blk, blk],
                   contract=NN, epilogue=epilogue)


def _down_fwd(a, w_down):
    s, f = a.shape
    d = w_down.shape[1]
    tm, tn, tk = min(TILE_M,s), min(TILE_N,d), 2048
    nk = f // tk
    return _matmul("down_fwd", a, w_down, grid=(s // tm, d // tn, nk),
                   a_spec=pl.BlockSpec((tm, tk), lambda i, j, k: (i, k)),
                   b_spec=pl.BlockSpec((tk, tn), lambda i, j, k: (k, j)),
                   out_shape=jax.ShapeDtypeStruct((s, d), F32),
                   out_specs=pl.BlockSpec((tm, tn), lambda i, j, k: (i, j)),
                   contract=NN, nk=nk, acc_shape=(tm, tn))


def _down_bwd_act(d_m, w_down, r, after=()):
    s, d = d_m.shape
    f = w_down.shape[0]
    tm, tn = min(TILE_M,s), min(TILE_N,f)
    blk = pl.BlockSpec((tm, tn), lambda i, j: (i, j))
    return _matmul("down_bwd_act", d_m, w_down, grid=(s // tm, f // tn), after=after,
                   a_spec=pl.BlockSpec((tm, d), lambda i, j: (i, 0)),
                   b_spec=pl.BlockSpec((tn, d), lambda i, j: (j, 0)),
                   out_shape=jax.ShapeDtypeStruct((s, f), BF16), out_specs=blk, contract=NT,
                   extras=(r,), extra_specs=(blk,),
                   epilogue=lambda acc, rv: (acc * (2.0 * rv.astype(F32)),))


def _up_bwd_act(d_up, w_up, after=()):
    s, _ = d_up.shape
    nb, d, fb = w_up.shape
    tm, tn = min(TILE_M, s), min(TILE_N,d)
    pair = 2
    n_after = len(after)

    def body(a_ref, w_ref, *rest):
        o_ref, acc_ref = rest[n_after:]
        k = pl.program_id(2)
        p = None
        for t in range(pair):
            term = lax.dot_general(a_ref[:, t * fb:(t + 1) * fb], w_ref[t], (NT, ((), ())), preferred_element_type=F32)
            p = term if p is None else p + term
        _accumulate(acc_ref, p, k)

        @pl.when(k == nb // pair - 1)
        def _():
            o_ref[...] = acc_ref[...]

    return pl.pallas_call(
        body, name="up_bwd_act", grid=(s // tm, d // tn, nb // pair),
        in_specs=[pl.BlockSpec((tm, pair * fb), lambda i, j, k: (i, k)),
                  pl.BlockSpec((pair, tn, fb), lambda i, j, k: (k, j, 0))] + [ANY] * n_after,
        out_specs=pl.BlockSpec((tm, tn), lambda i, j, k: (i, j)),
        out_shape=jax.ShapeDtypeStruct((s, d), F32),
        scratch_shapes=[pltpu.VMEM((tm, tn), F32)],
        compiler_params=_params("parallel", "parallel", "arbitrary"),
    )(d_up, w_up, *after)


def _half_grad(name, a, b, core, home, received, after, *, grid, a_block, a_map, b_block, b_map, o_block, o_map, out_shape):
    n_after = len(after)
    pick = (lambda ref: ref[0]) if home else (lambda ref: 1 - ref[0])

    def body(core_ref, a_ref, b_ref, *rest):
        acc = lax.dot_general(a_ref[...], b_ref[...], (TN, ((), ())), preferred_element_type=F32)
        if received is not None:
            acc = acc + rest[0][...].astype(F32)
        rest[-1][...] = acc.astype(rest[-1].dtype)

    wrap = lambda fn: (lambda i, j, core_ref: fn(i, j, pick(core_ref)))
    o_spec = pl.BlockSpec(o_block, wrap(o_map))
    extra = [] if received is None else [o_spec]
    operands = [] if received is None else [received]
    return pl.pallas_call(
        body, name=name,
        grid_spec=pltpu.PrefetchScalarGridSpec(
            num_scalar_prefetch=1, grid=grid,
            in_specs=[pl.BlockSpec(a_block, wrap(a_map)), pl.BlockSpec(b_block, wrap(b_map))] + extra + [ANY] * n_after,
            out_specs=o_spec),
        out_shape=out_shape,
        compiler_params=_params("parallel", "parallel"),
    )(core, a, b, *operands, *after)


def _down_half_grad(name, a, d_m, core, home, received=None, after=()):
    s, f = a.shape
    d = d_m.shape[1]
    r = f // N_DEV
    tn = min(TILE_N, d)
    return _half_grad(name, a, d_m, core, home, received, after, grid=(N_CHIP, d // tn),
                      a_block=(s, r), a_map=lambda k, j, p: (0, 2 * k + p),
                      b_block=(s, tn), b_map=lambda k, j, p: (0, j),
                      o_block=(None, r, tn), o_map=lambda k, j, p: (k, 0, j),
                      out_shape=jax.ShapeDtypeStruct((N_CHIP, r, d), BF16))


def _up_half_grad(name, h2, d_up, core, home, received=None, after=()):
    s, d = h2.shape
    fb = d_up.shape[1] // N_DEV
    tm = min(TILE_M, d)
    return _half_grad(name, h2, d_up, core, home, received, after, grid=(d // tm, N_CHIP),
                      a_block=(s, tm), a_map=lambda i, k, p: (0, i),
                      b_block=(s, fb), b_map=lambda i, k, p: (0, 2 * k + p),
                      o_block=(None, tm, fb), o_map=lambda i, k, p: (k, i, 0),
                      out_shape=jax.ShapeDtypeStruct((N_CHIP, d, fb), BF16))


def _in_pad(in_width):
    return -(-in_width // LANE) * LANE


def _join_col_shards(blocks):
    n, r, w = blocks.shape
    rows = min(ROWS, r)
    width = _in_pad(n * w)

    def body(x_ref, o_ref):
        tail = [jnp.zeros((rows, width - n * w), o_ref.dtype)] if width > n * w else []
        o_ref[...] = jnp.concatenate([x_ref[j] for j in range(n)] + tail, axis=1)

    return pl.pallas_call(
        body, name="join_col_shards", grid=(r // rows,),
        in_specs=[pl.BlockSpec((n, rows, w), lambda i: (0, i, 0))], out_specs=_row_spec(rows, width),
        out_shape=jax.ShapeDtypeStruct((r, width), blocks.dtype),
        compiler_params=_params("parallel"),
    )(blocks)


def _permute_q_cols(w_uq, n_heads):
    r = w_uq.shape[0]
    w3 = w_uq.reshape(r, n_heads, QK)
    return jnp.concatenate([w3[:, :, :HEAD].reshape(r, n_heads * HEAD), w3[:, :, HEAD:].reshape(r, n_heads * ROPE)], axis=1)


def _unpermute_q_rows(wt, n_heads):
    r = wt.shape[1]
    nope = wt[:n_heads * HEAD].reshape(n_heads, HEAD, r)
    rope = wt[n_heads * HEAD:].reshape(n_heads, ROPE, r)
    return jnp.concatenate([nope, rope], axis=1).reshape(n_heads * QK, r)


def _local_step(x, tgt, gains, weights, grads, first_after=()):
    pre_mix_g, q_norm_g, kv_norm_g, conv_out_g, attn_out_g, post_mix_g, pre_mlp_g, post_mlp_g = gains
    s, d = x.shape
    conv_width = conv_out_g.shape[1]
    n_groups = conv_width // HEAD
    r_q, r_kv = q_norm_g.shape[1], kv_norm_g.shape[1]
    n_heads = attn_out_g.shape[1] // HEAD
    c_q0 = 3 * conv_width
    c_kv0 = c_q0 + r_q
    c_kr0 = c_kv0 + r_kv
    in_pad = _in_pad(c_kr0 + ROPE)
    tn_in = in_pad // 5 if in_pad % (5 * LANE) == 0 else LANE
    tables = _rope_tables(s, n_heads)

    h1 = _rms_fwd("pre_mix_norm", x, pre_mix_g, after=first_after)
    weights.forward(0, (h1,))
    weights.relay(0, tables)
    w_in_p, conv_w = weights.ready(0, ())
    proj = _mm_nn("in_proj", h1, w_in_p, F32, TILE_M, tn_in)
    y_conv = _conv_fwd(proj, conv_w, conv_out_g, n_groups, conv_width + n_heads * HEAD, after=weights.forward(1, (proj,)))
    w_uq_p, w_ukv, w_o = weights.ready(1, (y_conv,))
    qn, q = _norm_up("q_up", proj, (c_q0, r_q), q_norm_g, w_uq_p)
    kvn, kv = _norm_up("kv_up", proj, (c_kv0, r_kv), kv_norm_g, w_ukv)
    qh, kh, vh = _pack_heads(q, kv, proj, c_kr0, tables, n_heads, after=weights.forward(2, (q, kv)))
    o, mix = _attn_fwd(qh, kh, vh, attn_out_g, y_conv, conv_width)
    y = _mm_nn("out_proj", mix, w_o, F32, TILE_M, TILE_N)
    x2, h2 = _mid_fwd(x, y, post_mix_g, pre_mlp_g, after=weights.forward(3, (y,)))
    weights.relay(2, (h2,))
    (w_up,) = weights.ready(2, ())
    a, r = _up_fwd(h2, w_up)
    weights.relay(3, (a,))
    (w_down,) = weights.ready(3, ())
    m = _down_fwd(a, w_down)

    d_out, d_m, dg_post_mlp, loss_part = _head(m, x2, tgt, post_mlp_g)
    core = grads.core
    away = _down_half_grad("down_bwd_w_away", a, d_m, core, home=False)
    d_up = _down_bwd_act(d_m, w_down, r, after=grads.send_away(0, away))
    sums = _down_half_grad("down_bwd_w_home", a, d_m, core, home=True, received=grads.received(0, (d_up,)))
    away = _up_half_grad("up_bwd_w_away", h2, d_up, core, home=False, after=grads.send_sums(0, (sums,)))
    d_h2 = _up_bwd_act(d_up, w_up, after=grads.send_away(1, away))
    sums = _up_half_grad("up_bwd_w_home", h2, d_up, core, home=True, received=grads.received(1, (d_h2,)))
    d_x2, d_y, dg_pre_mlp, dg_post_mix = _mid_bwd(x2, y, d_out, d_h2, pre_mlp_g, post_mix_g, after=grads.send_sums(1, (sums,)))
    d_mix = _mm_nt("out_proj_bwd_act", d_y, w_o, F32, TILE_M, TILE_N)
    gw_o = _mm_tn("out_proj_bwd_w", mix, d_y, BF16, TILE_M, TILE_N)
    dqh, dkh, dvh, dg_attn = _attn_bwd(qh, kh, vh, o, d_mix, attn_out_g, conv_width, after=grads.full(2, (gw_o,)))
    d_q, d_kv, d_kr = _unpack_heads(dqh, dkh, dvh, tables, n_heads)
    gw_uq_t = _mm_tn("q_up_bwd_w", d_q, qn, F32, TILE_M, TILE_N)
    gw_ukv = _mm_tn("kv_up_bwd_w", kvn, d_kv, BF16, TILE_M, TILE_N)
    d_cq, dg_q = _up_norm_bwd("q_up_bwd_act", d_q, w_uq_p, proj, (c_q0, r_q), q_norm_g, after=grads.full(3, (gw_uq_t, gw_ukv)))
    d_ckv, dg_kv = _up_norm_bwd("kv_up_bwd_act", d_kv, w_ukv, proj, (c_kv0, r_kv), kv_norm_g)
    d_u, d_b, d_c, dg_conv, dw_conv = _conv_bwd(proj, d_mix, conv_w, conv_out_g, n_groups)
    d_proj = jnp.concatenate([d_u, d_b, d_c, d_cq, d_ckv, d_kr[:, :in_pad - c_kr0]], axis=1)
    gw_in_t = _mm_tn("in_proj_bwd_w", d_proj, h1, F32, tn_in, TILE_N)
    d_h1 = _mm_nt("in_proj_bwd_act", d_proj, w_in_p, F32, TILE_M, 512, after=grads.send_away(4, gw_in_t))
    grad_x, dg_pre_mix = _first_bwd(x, pre_mix_g, d_h1, d_x2, after=grads.full(4, (gw_in_t,), received=(d_h1,)))

    small = [dg_pre_mix, dg_q, dg_kv, dg_conv, dg_attn, dg_post_mix, dg_pre_mlp, dg_post_mlp,
             dw_conv[0], dw_conv[1], dw_conv[2], loss_part]
    return grad_x, jnp.concatenate(small, axis=1)


HBM = pl.BlockSpec(memory_space=pltpu.HBM)
SEM = pl.BlockSpec(memory_space=pltpu.SEMAPHORE)
IN_VMEM = pl.BlockSpec(memory_space=pltpu.VMEM)
SPLIT = pltpu.CompilerParams(has_side_effects=pltpu.SideEffectType.DATAFLOW_SIDE_EFFECTING)


def _in_hbm(a):
    return pltpu.with_memory_space_constraint(a, pltpu.HBM)


def _hbm_like(a):
    return pltpu.HBM(a.shape, a.dtype)


def _place():
    x, y, c = lax.axis_index("x"), lax.axis_index("y"), lax.axis_index("c")
    other_chips = [(1 - x, y), (x, 1 - y), (1 - x, 1 - y)]
    return x, y, c, other_chips


def _block(px, py, pc):
    return 4 * px + 2 * py + pc


def _await(block, sem):
    pltpu.make_async_copy(block, block, sem).wait()


def _relay_route(x, y, c):
    came_from = ((1 - x) * (1 - c) + x * c, y * (1 - c) + (1 - y) * c)
    goes_to = (x * (1 - c) + (1 - x) * c, (1 - y) * (1 - c) + y * c)
    return came_from, goes_to


def _gather_start(name, shards, groups, relayed=(), after=()):
    n, ng = len(shards), len(groups)
    lands = [lax.empty((N_DEV, *a.shape), a.dtype) for a in shards]

    def body(*refs):
        src, land = refs[:n], refs[n:2 * n]
        sems, token = refs[2 * n + len(after):2 * n + len(after) + 2 * ng], refs[-1]
        x, y, c, chips = _place()
        targets = [(x, y, 1 - c)] + [(*chip, c) for chip in chips]
        for gi, group in enumerate(groups):
            for i, w in enumerate(group):
                for k, to in enumerate(targets[:3] if gi in relayed else targets):
                    pltpu.make_async_remote_copy(
                        src_ref=src[w], dst_ref=land[w].at[_block(x, y, c)],
                        send_sem=sems[2 * gi].at[4 * i + k], recv_sem=sems[2 * gi + 1].at[4 * i + k],
                        device_id=to, device_id_type=MESH).start()
        token[...] = jnp.zeros_like(token)

    sem_shapes = [pltpu.SemaphoreType.DMA((4 * len(g),)) for g in groups for _ in range(2)]
    out = pl.pallas_call(
        body, name=name,
        in_specs=[HBM] * (2 * n) + [ANY] * len(after),
        out_specs=[SEM] * (2 * ng) + [HBM] * (2 * n) + [IN_VMEM],
        out_shape=sem_shapes + [_hbm_like(a) for a in shards] + [_hbm_like(a) for a in lands]
        + [jax.ShapeDtypeStruct((SUBLANE, LANE), F32)],
        input_output_aliases={i: 2 * ng + i for i in range(2 * n)},
        compiler_params=SPLIT,
    )(*[_in_hbm(a) for a in shards], *[_in_hbm(a) for a in lands], *after)
    sems = [(out[2 * gi], out[2 * gi + 1]) for gi in range(ng)]
    return sems, out[2 * ng:2 * ng + n], out[2 * ng + n:2 * ng + 2 * n], out[-1]


def _gather_forward(name, shards, lands, send1, recv1, after, relayed=False):
    n = len(lands)

    def body(*refs):
        src, land = refs[:n], refs[n:2 * n]
        s1, r1 = refs[2 * n], refs[2 * n + 1]
        s2, r2 = refs[2 * n + 2 + len(after)], refs[2 * n + 3 + len(after)]
        x, y, c, chips = _place()
        me, sibling = (x, y, c), (x, y, 1 - c)
        for j, chip in enumerate(chips[:2] if relayed else chips):
            for i in range(n):
                blk = land[i].at[_block(*chip, c)]
                pltpu.make_async_remote_copy(src_ref=blk, dst_ref=blk, send_sem=s1.at[4 * i + 1 + j], recv_sem=r1.at[4 * i + 1 + j],
                                             device_id=me, device_id_type=MESH).wait_recv()
                pltpu.make_async_remote_copy(src_ref=blk, dst_ref=blk, send_sem=s2.at[3 * i + j], recv_sem=r2.at[3 * i + j],
                                             device_id=sibling, device_id_type=MESH).start()
        if relayed:
            came_from, goes_to = _relay_route(x, y, c)
            for i in range(n):
                blk = land[i].at[_block(*came_from, c)]
                pltpu.make_async_remote_copy(src_ref=blk, dst_ref=blk, send_sem=s2.at[3 * i + 2], recv_sem=r2.at[3 * i + 2],
                                             device_id=(*goes_to, c), device_id_type=MESH).start()
        for i in range(n):
            blk = land[i].at[_block(x, y, 1 - c)]
            pltpu.make_async_remote_copy(src_ref=blk, dst_ref=blk, send_sem=s1.at[4 * i], recv_sem=r1.at[4 * i],
                                         device_id=me, device_id_type=MESH).wait_recv()
            for k in range(3 if relayed else 4):
                pltpu.make_async_remote_copy(src_ref=src[i], dst_ref=land[i].at[_block(x, y, c)], send_sem=s1.at[4 * i + k],
                                             recv_sem=r1.at[4 * i + k], device_id=sibling, device_id_type=MESH).wait_send()

    sem = pltpu.SemaphoreType.DMA((3 * n,))
    out = pl.pallas_call(
        body, name=name,
        in_specs=[HBM] * (2 * n) + [SEM, SEM] + [ANY] * len(after),
        out_specs=[SEM, SEM] + [HBM] * n,
        out_shape=[sem, sem] + [_hbm_like(a) for a in lands],
        input_output_aliases={n + i: 2 + i for i in range(n)},
        compiler_params=SPLIT,
    )(*shards, *lands, send1, recv1, *after)
    return (out[0], out[1]), out[2:]


def _gather_relay_forward(name, lands, send2, recv2, after):
    n = len(lands)

    def body(*refs):
        land, s2, r2 = refs[:n], refs[n], refs[n + 1]
        s3, r3 = refs[n + 2 + len(after)], refs[n + 3 + len(after)]
        x, y, c, _ = _place()
        me, sibling = (x, y, c), (x, y, 1 - c)
        came_from, _ = _relay_route(x, y, c)
        for i in range(n):
            blk = land[i].at[_block(1 - x, 1 - y, c)]
            pltpu.make_async_remote_copy(src_ref=blk, dst_ref=blk, send_sem=s2.at[3 * i + 2], recv_sem=r2.at[3 * i + 2],
                                         device_id=me, device_id_type=MESH).wait_recv()
            pltpu.make_async_remote_copy(src_ref=blk, dst_ref=blk, send_sem=s3.at[i], recv_sem=r3.at[i],
                                         device_id=sibling, device_id_type=MESH).start()
            sent = land[i].at[_block(*came_from, c)]
            pltpu.make_async_remote_copy(src_ref=sent, dst_ref=sent, send_sem=s2.at[3 * i + 2], recv_sem=r2.at[3 * i + 2],
                                         device_id=me, device_id_type=MESH).wait_send()

    sem = pltpu.SemaphoreType.DMA((n,))
    out = pl.pallas_call(
        body, name=name,
        in_specs=[HBM] * n + [SEM, SEM] + [ANY] * len(after),
        out_specs=[SEM, SEM] + [HBM] * n,
        out_shape=[sem, sem] + [_hbm_like(a) for a in lands],
        input_output_aliases={i: 2 + i for i in range(n)},
        compiler_params=SPLIT,
    )(*lands, send2, recv2, *after)
    return (out[0], out[1]), out[2:]


def _gather_wait(name, lands, send2, recv2, after, relay_sems=None):
    n = len(lands)
    n_sems = 2 if relay_sems is None else 4

    def body(*refs):
        land, s2, r2 = refs[:n], refs[n], refs[n + 1]
        for i in range(n):
            for j in range(3 if relay_sems is None else 2):
                _await(land[i].at[0], r2.at[3 * i + j])
                _await(land[i].at[0], s2.at[3 * i + j])
            if relay_sems is not None:
                _await(land[i].at[0], refs[n + 3].at[i])
                _await(land[i].at[0], refs[n + 2].at[i])

    return pl.pallas_call(
        body, name=name,
        in_specs=[HBM] * n + [SEM] * n_sems + [ANY] * len(after), out_specs=[HBM] * n, out_shape=[_hbm_like(a) for a in lands],
        input_output_aliases={i: i for i in range(n)},
        compiler_params=SPLIT,
    )(*lands, send2, recv2, *(relay_sems or ()), *after)


def _pair_exchange(name, grads, shard_rows):
    n = len(grads)
    shapes = [(g.shape[1:] if r is None else (r, g.shape[1])) for g, r in zip(grads, shard_rows)]

    def body(*refs):
        ins, recv = refs[:n], refs[n:2 * n]
        send_sems, recv_sems = refs[2 * n:]
        x, y, c, _ = _place()
        sends = []
        for w in range(n):
            for k in range(N_CHIP):
                j, r = 2 * k + 1 - c, shard_rows[w]
                src = ins[w].at[j] if r is None else ins[w].at[pl.ds(pl.multiple_of(j * r, SUBLANE), r), :]
                sends.append(pltpu.make_async_remote_copy(
                    src_ref=src, dst_ref=recv[w].at[k],
                    send_sem=send_sems.at[w, k], recv_sem=recv_sems.at[w, k],
                    device_id=(x, y, 1 - c), device_id_type=MESH))
        for cp in sends:
            cp.start()
        for cp in sends:
            cp.wait()

    return pl.pallas_call(
        body, name=name,
        in_specs=[ANY] * n, out_specs=[ANY] * n,
        out_shape=[jax.ShapeDtypeStruct((N_CHIP, *shape), g.dtype) for g, shape in zip(grads, shapes)],
        scratch_shapes=[pltpu.SemaphoreType.DMA((n, N_CHIP))] * 2,
    )(*grads)


def _pair_sum_rows(name, grad, received, core):
    _, r, c = received.shape
    tc = _fit(c, 512)

    def body(core_ref, a_ref, b_ref, o_ref):
        o_ref[...] = (a_ref[...] + b_ref[...]).astype(o_ref.dtype)

    spec = pl.BlockSpec((None, r, tc), lambda k, i, core_ref: (k, 0, i))
    return pl.pallas_call(
        body, name=name,
        grid_spec=pltpu.PrefetchScalarGridSpec(
            num_scalar_prefetch=1, grid=(N_CHIP, c // tc),
            in_specs=[pl.BlockSpec((r, tc), lambda k, i, core_ref: (2 * k + core_ref[0], i)), spec],
            out_specs=spec),
        out_shape=jax.ShapeDtypeStruct(received.shape, BF16),
        compiler_params=_params("parallel", "parallel"),
    )(core, grad, received)


def _pair_sum(name, grad, received, core):
    _, r, c = received.shape
    rows = min(ROWS, r)
    assert r % rows == 0

    def body(core_ref, a_ref, b_ref, o_ref):
        o_ref[...] = (a_ref[...].astype(F32) + b_ref[...].astype(F32)).astype(o_ref.dtype)

    spec = pl.BlockSpec((None, rows, c), lambda k, i, core_ref: (k, i, 0))
    return pl.pallas_call(
        body, name=name,
        grid_spec=pltpu.PrefetchScalarGridSpec(
            num_scalar_prefetch=1, grid=(N_CHIP, r // rows),
            in_specs=[pl.BlockSpec((None, None, rows, c), lambda k, i, core_ref: (k, core_ref[0], i, 0)), spec],
            out_specs=spec),
        out_shape=jax.ShapeDtypeStruct(received.shape, received.dtype),
        compiler_params=_params("parallel", "parallel"),
    )(core, grad.reshape(N_CHIP, 2, r, c), received)


def _away_shard(src, k, c, shard_rows):
    if shard_rows is None:
        return src.at[k]
    return src.at[pl.ds(pl.multiple_of((2 * k + 1 - c) * shard_rows, SUBLANE), shard_rows), :]


def _pair_send_start(name, away, shard_rows=None):
    shape = away.shape if shard_rows is None else (N_CHIP, shard_rows, away.shape[1])
    land = lax.empty(shape, away.dtype)

    def body(src, dst, send, recv, src_thru, dst_thru, token):
        x, y, c, _ = _place()
        for k in range(N_CHIP):
            pltpu.make_async_remote_copy(src_ref=_away_shard(src, k, c, shard_rows), dst_ref=dst.at[k], send_sem=send.at[k],
                                         recv_sem=recv.at[k], device_id=(x, y, 1 - c), device_id_type=MESH).start()
        token[...] = jnp.zeros_like(token)

    sem = pltpu.SemaphoreType.DMA((N_CHIP,))
    out = pl.pallas_call(
        body, name=name,
        in_specs=[HBM, HBM], out_specs=[SEM, SEM, HBM, HBM, IN_VMEM],
        out_shape=[sem, sem, _hbm_like(away), _hbm_like(land), jax.ShapeDtypeStruct((SUBLANE, LANE), F32)],
        input_output_aliases={0: 2, 1: 3},
        compiler_params=SPLIT,
    )(_in_hbm(away), _in_hbm(land))
    return (out[0], out[1]), out[2], out[3], out[4]


def _pair_send_wait(name, sems, src, land, after, shard_rows=None):
    def body(src_ref, dst_ref, send, recv, *rest):
        for k in range(N_CHIP):
            _await(dst_ref.at[k], send.at[k])
            _await(dst_ref.at[k], recv.at[k])

    return pl.pallas_call(
        body, name=name,
        in_specs=[HBM, HBM, SEM, SEM] + [ANY] * len(after), out_specs=HBM, out_shape=_hbm_like(land),
        input_output_aliases={1: 0},
        compiler_params=SPLIT,
    )(src, land, *sems, *after)


def _chip_send_start(name, sums):
    n = len(sums)
    lands = [lax.empty(a.shape, a.dtype) for a in sums]

    def body(*refs):
        src, land = refs[:n], refs[n:2 * n]
        send, recv, token = refs[2 * n], refs[2 * n + 1], refs[-1]
        x, y, c, chips = _place()
        for w in range(n):
            for j, (px, py) in enumerate(chips):
                pltpu.make_async_remote_copy(
                    src_ref=src[w].at[2 * px + py], dst_ref=land[w].at[2 * x + y],
                    send_sem=send.at[3 * w + j], recv_sem=recv.at[3 * w + j],
                    device_id=(px, py, c), device_id_type=MESH).start()
        token[...] = jnp.zeros_like(token)

    sem = pltpu.SemaphoreType.DMA((3 * n,))
    out = pl.pallas_call(
        body, name=name,
        in_specs=[HBM] * (2 * n),
        out_specs=[SEM, SEM] + [HBM] * (2 * n) + [IN_VMEM],
        out_shape=[sem, sem] + [_hbm_like(a) for a in sums] + [_hbm_like(a) for a in lands]
        + [jax.ShapeDtypeStruct((SUBLANE, LANE), F32)],
        input_output_aliases={i: 2 + i for i in range(2 * n)},
        compiler_params=SPLIT,
    )(*[_in_hbm(a) for a in sums], *[_in_hbm(a) for a in lands])
    return (out[0], out[1]), out[2:2 + n], out[2 + n:2 + 2 * n], out[-1]


def _chip_send_wait(name, groups, after):
    counts = [len(g[1]) for g in groups]
    n = sum(counts)

    def body(*refs):
        land = refs[n:2 * n]
        sems = refs[2 * n:2 * n + 2 * len(groups)]
        w = 0
        for gi, count in enumerate(counts):
            for i in range(count):
                for j in range(3):
                    _await(land[w].at[0], sems[2 * gi].at[3 * i + j])
                    _await(land[w].at[0], sems[2 * gi + 1].at[3 * i + j])
                w += 1

    sums = [a for g in groups for a in g[1]]
    lands = [a for g in groups for a in g[2]]
    sems = [s for g in groups for s in g[0]]
    return pl.pallas_call(
        body, name=name,
        in_specs=[HBM] * (2 * n) + [SEM] * len(sems) + [ANY] * len(after),
        out_specs=[HBM] * n, out_shape=[_hbm_like(a) for a in lands],
        input_output_aliases={n + i: i for i in range(n)},
        compiler_params=SPLIT,
    )(*sums, *lands, *sems, *after)


def _small_all_reduce(part, after=()):
    _, w = part.shape

    def body(p_ref, *rest):
        o_ref, buf, send_sems, recv_sems = rest[len(after):]
        x, y, c, _ = _place()
        me = 4 * x + 2 * y + c
        buf[me] = jnp.sum(p_ref[...], axis=0, keepdims=True)
        copies = []
        for k in range(1, N_DEV):
            dx, dy, dc = (k >> 2) & 1, (k >> 1) & 1, k & 1
            copies.append(pltpu.make_async_remote_copy(
                src_ref=buf.at[me], dst_ref=buf.at[me], send_sem=send_sems.at[k - 1], recv_sem=recv_sems.at[k - 1],
                device_id=(x ^ dx, y ^ dy, c ^ dc), device_id_type=MESH))
        for cp in copies:
            cp.start()
        for cp in copies:
            cp.wait()
        tot = buf[0]
        for d in range(1, N_DEV):
            tot = tot + buf[d]
        o_ref[...] = tot
        loss = jnp.sum(tot[:, w - LANE:], axis=1, keepdims=True)
        o_ref[:, w - LANE:] = jnp.broadcast_to(loss, (1, LANE))

    return pl.pallas_call(
        body, name="small_all_reduce",
        in_specs=[IN_VMEM] + [ANY] * len(after), out_specs=IN_VMEM,
        out_shape=jax.ShapeDtypeStruct((1, w), F32),
        scratch_shapes=[pltpu.VMEM((N_DEV, 1, w), F32), pltpu.SemaphoreType.DMA((N_DEV - 1,)), pltpu.SemaphoreType.DMA((N_DEV - 1,))],
        compiler_params=pltpu.CompilerParams(vmem_limit_bytes=VMEM_LIMIT_BYTES),
    )(part, *after)


def _adamw(w, g, m, v):
    m = ADAM_B1 * m + (1.0 - ADAM_B1) * g
    v = ADAM_B2 * v + (1.0 - ADAM_B2) * (g * g)
    m_hat = m / (1.0 - ADAM_B1 ** ADAM_STEP)
    v_hat = v / (1.0 - ADAM_B2 ** ADAM_STEP)
    delta = -ADAM_LR * (m_hat / (jnp.sqrt(v_hat) + ADAM_EPS) + ADAM_WD * w)
    return delta, m, v


def _sum_adam(name, parts, sums, chip, w, m, v, after=()):
    _, r, c = w.shape
    n_after = len(after)
    by_rows = r % ROWS == 0 or r < ROWS
    tr, tc = (min(ROWS, r), c) if by_rows else (r, _fit(c, 512))
    at = (lambda i: (i, 0)) if by_rows else (lambda i: (0, i))

    def body(chip_ref, p_ref, own_ref, w_ref, m_ref, v_ref, *rest):
        g_ref, d_ref, mo_ref, vo_ref = rest[n_after:]
        g = None
        for k in range(N_CHIP):
            term = jnp.where(chip_ref[0] == k, own_ref[...], p_ref[k]).astype(F32)
            g = term if g is None else g + term
        g_ref[...] = g
        d_ref[...], mo_ref[...], vo_ref[...] = _adamw(w_ref[...], g, m_ref[...], v_ref[...])

    blk = pl.BlockSpec((None, tr, tc), lambda i, chip_ref: (0, *at(i)))
    out = jax.ShapeDtypeStruct((1, r, c), F32)
    return pl.pallas_call(
        body, name=name,
        grid_spec=pltpu.PrefetchScalarGridSpec(
            num_scalar_prefetch=1, grid=(r // tr if by_rows else c // tc,),
            in_specs=[pl.BlockSpec((N_CHIP, tr, tc), lambda i, chip_ref: (0, *at(i))),
                      pl.BlockSpec((None, tr, tc), lambda i, chip_ref: (chip_ref[0], *at(i))), blk, blk, blk]
            + [ANY] * n_after,
            out_specs=[blk] * 4),
        out_shape=[out] * 4,
        compiler_params=_params("parallel"),
    )(chip, parts, sums, w, m, v, *after)


def _adam_gains(total, ws, ms, vs):
    n = len(ws)
    widths = [w.shape[1] for w in ws]

    def body(t_ref, *refs):
        w_refs, m_refs, v_refs, outs = refs[:n], refs[n:2 * n], refs[2 * n:3 * n], refs[3 * n:]
        off = 0
        for i in range(n):
            g = t_ref[:, off:off + widths[i]]
            off += widths[i]
            g_ref, d_ref, mo_ref, vo_ref = outs[4 * i:4 * i + 4]
            g_ref[...] = g
            d_ref[...], mo_ref[...], vo_ref[...] = _adamw(w_refs[i][...], g, m_refs[i][...], v_refs[i][...])

    out = pl.pallas_call(
        body, name="adam_gains",
        out_shape=[jax.ShapeDtypeStruct(w.shape, F32) for w in ws for _ in range(4)],
    )(total, *ws, *ms, *vs)
    return [tuple(out[4 * i:4 * i + 4]) for i in range(n)]


def _adam_taps(total, first_col, device, w, m, v):
    _, n_taps, cw = w.shape
    col_block = lambda t, dev: (0, first_col // cw + t * N_DEV + dev[0])
    tap = pl.BlockSpec((None, 1, cw), lambda t, dev: (t, 0, 0))

    def body(dev_ref, t_ref, w_ref, m_ref, v_ref, g_ref, d_ref, mo_ref, vo_ref):
        g = t_ref[...]
        g_ref[...] = g
        d_ref[...], mo_ref[...], vo_ref[...] = _adamw(w_ref[...], g, m_ref[...], v_ref[...])

    shape3 = (n_taps, 1, cw)
    out = pl.pallas_call(
        body, name="adam_taps",
        grid_spec=pltpu.PrefetchScalarGridSpec(
            num_scalar_prefetch=1, grid=(n_taps,),
            in_specs=[pl.BlockSpec((1, cw), col_block), tap, tap, tap], out_specs=[tap] * 4),
        out_shape=[jax.ShapeDtypeStruct(shape3, F32)] * 4,
    )(device, total, w.reshape(shape3), m.reshape(shape3), v.reshape(shape3))
    return tuple(o.reshape(w.shape) for o in out)


def kernel(x, pre_mix_g, w_in, conv_w, q_norm_g, w_uq, kv_norm_g, w_ukv, conv_out_g, attn_out_g, w_o, post_mix_g, pre_mlp_g, w_up, w_down, post_mlp_g, loss_target, m_pre_mix_g, m_w_in, m_conv_w, m_q_norm_g, m_w_uq, m_kv_norm_g, m_w_ukv, m_conv_out_g, m_attn_out_g, m_w_o, m_post_mix_g, m_pre_mlp_g, m_w_up, m_w_down, m_post_mlp_g, v_pre_mix_g, v_w_in, v_conv_w, v_q_norm_g, v_w_uq, v_kv_norm_g, v_w_ukv, v_conv_out_g, v_attn_out_g, v_w_o, v_post_mix_g, v_pre_mlp_g, v_w_up, v_w_down, v_post_mlp_g):
    me = 4 * lax.axis_index("x") + 2 * lax.axis_index("y") + lax.axis_index("c")
    core = lax.axis_index("c").astype(jnp.int32).reshape(1)
    chip = (2 * lax.axis_index("x") + lax.axis_index("y")).astype(jnp.int32).reshape(1)
    gains = (pre_mix_g, q_norm_g, kv_norm_g, conv_out_g, attn_out_g, post_mix_g, pre_mlp_g, post_mlp_g)
    gain_m = (m_pre_mix_g, m_q_norm_g, m_kv_norm_g, m_conv_out_g, m_attn_out_g, m_post_mix_g, m_pre_mlp_g, m_post_mlp_g)
    gain_v = (v_pre_mix_g, v_q_norm_g, v_kv_norm_g, v_conv_out_g, v_attn_out_g, v_post_mix_g, v_pre_mlp_g, v_post_mlp_g)
    names = ("w_in", "w_uq", "w_ukv", "w_o", "w_up", "w_down")
    big = dict(zip(names, (w_in, w_uq, w_ukv, w_o, w_up, w_down)))
    big_m = dict(zip(names, (m_w_in, m_w_uq, m_w_ukv, m_w_o, m_w_up, m_w_down)))
    big_v = dict(zip(names, (v_w_in, v_w_uq, v_w_ukv, v_w_o, v_w_up, v_w_down)))
    n_heads = attn_out_g.shape[1] // HEAD
    n_taps = conv_w.shape[1]

    gathered = ("w_in", "conv", "w_uq", "w_ukv", "w_o", "w_up", "w_down")
    gather_groups = ((0, 1), (2, 3, 4), (5,), (6,))
    taps = jnp.pad(conv_w[0], ((0, SUBLANE - n_taps), (0, 0)))
    relayed_groups = (0, 2, 3)
    sems1, shards, lands, token = _gather_start("gather_start_first", [w_in[0].astype(BF16), taps], ((0, 1),), relayed=(0,))
    sems1, shards, lands = list(sems1), list(shards), list(lands)
    behind = token[0, 0]
    rest = [(big[nm][0] + behind).astype(BF16) for nm in gathered[2:]]

    def start_rest(after):
        sems_b, shards_b, lands_b, started = _gather_start("gather_start_rest", rest, ((0, 1, 2), (3,), (4,)), relayed=(1, 2),
                                                          after=after)
        sems1.extend(sems_b)
        shards.extend(shards_b)
        lands.extend(lands_b)
        return started

    cols = lambda a: jnp.concatenate([a[j] for j in range(N_DEV)], axis=1)
    rows = lambda a: a.reshape(N_DEV * a.shape[1], a.shape[2])
    ready = {
        "w_in": _join_col_shards,
        "conv": lambda a: cols(a)[:n_taps],
        "w_uq": lambda a: _permute_q_cols(cols(a), n_heads),
        "w_ukv": cols, "w_o": rows, "w_up": lambda a: a, "w_down": rows,
    }

    class Weights:
        def __init__(self):
            self.passed, self.relayed = {}, {}

        def forward(self, group, after):
            idx = gather_groups[group]
            if group == 0:
                after = (*after, *rest)
            self.passed[group] = _gather_forward(f"gather_forward_{group}", [shards[i] for i in idx], [lands[i] for i in idx],
                                                 *sems1[group], after, relayed=group in relayed_groups)
            return tuple(self.passed[group][1])

        def relay(self, group, after):
            sems2, mid = self.passed[group]
            self.relayed[group], mid = _gather_relay_forward(f"gather_relay_{group}", mid, *sems2, after)
            self.passed[group] = (sems2, mid)
            if group == 0:
                start_rest(tuple(mid))
            return tuple(mid)

        def ready(self, group, after):
            sems2, mid = self.passed[group]
            full = _gather_wait(f"gather_wait_{group}", mid, *sems2, after, relay_sems=self.relayed.get(group))
            out = []
            for i, a in zip(gather_groups[group], full):
                a = lax.dynamic_update_index_in_dim(a, shards[i], me, 0)
                out.append(ready[gathered[i]](a))
            return out

    weights = Weights()

    col_blocks = lambda g: g.reshape(g.shape[0], N_DEV, g.shape[1] // N_DEV).transpose(1, 0, 2)
    row_blocks = lambda g: g.reshape(N_DEV, g.shape[0] // N_DEV, g.shape[1])
    grad_groups = (("w_down",), ("w_up",), ("w_o",), ("w_uq", "w_ukv"), ("w_in",))
    transposed = {"w_in": w_in.shape[2], "w_uq": w_uq.shape[2]}
    to_blocks = {
        "w_in": lambda g: g, "w_uq": lambda g: _unpermute_q_rows(g, n_heads),
        "w_ukv": col_blocks, "w_o": row_blocks, "w_up": lambda g: g, "w_down": row_blocks,
    }
    in_flight = []

    class Grads:
        def __init__(self):
            self.core = core
            self.away = {}

        def send_sums(self, group, sums):
            sems, sums, parts, tok = _chip_send_start(f"chip_send_start_{group}", list(sums))
            in_flight.append((sems, sums, parts))
            return (tok,)

        def full(self, group, arrays, received=None):
            nms = grad_groups[group]
            if received is None:
                blocks = [to_blocks[nm](g) for nm, g in zip(nms, arrays)]
                got = _pair_exchange(f"pair_exchange_{group}", blocks, [transposed.get(nm) for nm in nms])
            else:
                blocks, got = [self.away[group][1]], [self.received(group, received)]
            sums = [(_pair_sum_rows if nm in transposed else _pair_sum)(f"pair_sum_{nm}", g, r, core)
                    for nm, g, r in zip(nms, blocks, got)]
            return self.send_sums(group, sums)

        def send_away(self, group, half):
            nm = grad_groups[group][0]
            rows = transposed.get(nm)
            sems, src, land, tok = _pair_send_start(f"pair_send_start_{group}", half if rows is None else to_blocks[nm](half), rows)
            self.away[group] = (sems, src, land, rows)
            return (tok,)

        def received(self, group, after):
            sems, src, land, rows = self.away[group]
            return _pair_send_wait(f"pair_send_wait_{group}", sems, src, land, after, rows)

    grad_x, small = _local_step(x[0], loss_target[0], gains, weights, Grads(), first_after=(token,))

    big_out = {}

    def update(tag, first, last, after):
        groups = in_flight[first:last]
        parts = _chip_send_wait("chip_send_wait_" + tag, groups, after)
        nms = [nm for grp in grad_groups[first:last] for nm in grp]
        sums = [a for _, s, _ in groups for a in s]
        for nm, p, s in zip(nms, parts, sums):
            view = (lambda a: jnp.swapaxes(a, 1, 2)) if nm in transposed else (lambda a: a)
            out = _sum_adam("adam_" + nm, p, s, chip, view(big[nm]), view(big_m[nm]), view(big_v[nm]), after=after)
            after = (out[0],)
            big_out[nm] = [view(o) for o in out]
        return after

    after = update("early", 0, len(in_flight) - 1, (grad_x,))
    total = _small_all_reduce(small, after=after)
    update("late", len(in_flight) - 1, len(in_flight), (total,))
    big_out = [big_out[nm] for nm in names]

    gain_out = _adam_gains(total, gains, gain_m, gain_v)
    taps_out = _adam_taps(total, sum(g.shape[1] for g in gains), me.astype(jnp.int32).reshape(1), conv_w, m_conv_w, v_conv_w)
    loss = total[0, total.shape[1] - 1]

    order = (0, "w_in", "conv", 1, "w_uq", 2, "w_ukv", 3, 4, "w_o", 5, 6, "w_up", "w_down", 7)
    by_name = dict(zip(names, big_out))
    outs = [loss, grad_x[None]]
    for kind in range(4):
        for item in order:
            if item == "conv":
                outs.append(taps_out[kind])
            elif isinstance(item, int):
                outs.append(gain_out[item][kind])
            else:
                outs.append(by_name[item][kind])
    return tuple(outs)
```

```python
import math

import jax
import jax.numpy as jnp
from jax import lax
from jax.experimental import pallas as pl
from jax.experimental.pallas import tpu as pltpu

F32 = jnp.float32
BF16 = jnp.bfloat16

EPS = 1e-6
NEG_INF = -1e30
HEAD = 128
ROPE = 64
QK = HEAD + ROPE
CHUNK = 64
ROPE_THETA = 10000.0
ADAM_LR, ADAM_B1, ADAM_B2, ADAM_EPS, ADAM_WD, ADAM_STEP = 0.001, 0.9, 0.999, 1e-08, 0.01, 10

LANE = 128
SUBLANE = 8
VMEM_LIMIT_BYTES = 56 * 1024 * 1024

N_DEV = 8
N_CHIP = 4
MESH = pl.DeviceIdType.MESH


def _params(*sem):
    return pltpu.CompilerParams(dimension_semantics=sem, vmem_limit_bytes=VMEM_LIMIT_BYTES)


ANY = pl.BlockSpec(memory_space=pl.ANY)


def _call(body, *, in_specs, after=(), **kw):
    n_in, n_after = len(in_specs), len(after)

    def ordered(*refs):
        body(*refs[:n_in], *refs[n_in + n_after:])

    call = pl.pallas_call(ordered, in_specs=[*in_specs, *[ANY] * n_after], **kw)
    return lambda *operands: call(*operands, *after)


def _sublane_sum(v):
    r, w = v.shape
    return jnp.sum(v.reshape(r // SUBLANE, SUBLANE, w), axis=0)


def _rstd(x):
    return lax.rsqrt(jnp.mean(x * x, axis=-1, keepdims=True) + EPS)


def _rms_bwd(x, g, dy):
    r = _rstd(x)
    xh = x * r
    dxh = dy * g
    dx = r * (dxh - xh * jnp.mean(dxh * xh, axis=-1, keepdims=True))
    return dx, dy * xh


def _accumulate(ref, val, step):
    @pl.when(step == 0)
    def _():
        ref[...] = val

    @pl.when(step > 0)
    def _():
        ref[...] += val


NN = ((1,), (0,))
NT = ((1,), (1,))
TN = ((0,), (0,))


def _matmul(name, a, b, *, grid, a_spec, b_spec, out_shape, out_specs, contract, nk=1, acc_shape=None,
            extras=(), extra_specs=(), epilogue=None, after=()):
    multi = isinstance(out_shape, (tuple, list))
    out_shapes = tuple(out_shape) if multi else (out_shape,)
    n_out = len(out_shapes)
    n_extra = len(extras)

    def body(a_ref, b_ref, *rest):
        x_refs = rest[:n_extra]
        o_refs = rest[n_extra:n_extra + n_out]

        def emit(acc):
            vals = epilogue(acc, *[r[...] for r in x_refs]) if epilogue else (acc,)
            for r, v in zip(o_refs, vals):
                r[...] = v.astype(r.dtype)

        p = lax.dot_general(a_ref[...], b_ref[...], (contract, ((), ())), preferred_element_type=F32)
        if nk == 1:
            emit(p)
        else:
            acc_ref = rest[n_extra + n_out]
            k = pl.program_id(2)
            _accumulate(acc_ref, p, k)

            @pl.when(k == nk - 1)
            def _():
                emit(acc_ref[...])

    sem = ("parallel", "parallel") + (("arbitrary",) if nk > 1 else ())
    return _call(
        body, name=name, grid=grid, after=after,
        in_specs=[a_spec, b_spec, *extra_specs],
        out_specs=out_specs,
        out_shape=out_shape,
        scratch_shapes=[pltpu.VMEM(acc_shape, F32)] if nk > 1 else [],
        compiler_params=_params(*sem),
    )(a, b, *extras)


def _fit(n, tile):
    if n <= tile:
        return n
    t = tile - tile % LANE
    while n % t:
        t -= LANE
    return t


def _mm_nn(name, a, b, out_dtype, tm, tn, after=()):
    m, k = a.shape
    n = b.shape[1]
    tm, tn = _fit(m, tm), _fit(n, tn)
    return _matmul(name, a, b, grid=(m // tm, n // tn), after=after,
                   a_spec=pl.BlockSpec((tm, k), lambda i, j: (i, 0)),
                   b_spec=pl.BlockSpec((k, tn), lambda i, j: (0, j)),
                   out_shape=jax.ShapeDtypeStruct((m, n), out_dtype),
                   out_specs=pl.BlockSpec((tm, tn), lambda i, j: (i, j)), contract=NN)


def _mm_nt(name, a, b, out_dtype, tm, tn, after=()):
    m, k = a.shape
    n = b.shape[0]
    tm, tn = _fit(m, tm), _fit(n, tn)
    return _matmul(name, a, b, grid=(m // tm, n // tn), after=after,
                   a_spec=pl.BlockSpec((tm, k), lambda i, j: (i, 0)),
                   b_spec=pl.BlockSpec((tn, k), lambda i, j: (j, 0)),
                   out_shape=jax.ShapeDtypeStruct((m, n), out_dtype),
                   out_specs=pl.BlockSpec((tm, tn), lambda i, j: (i, j)), contract=NT)


def _mm_tn(name, a, b, out_dtype, tm, tn):
    s, m = a.shape
    n = b.shape[1]
    tm, tn = _fit(m, tm), _fit(n, tn)
    return _matmul(name, a, b, grid=(m // tm, n // tn),
                   a_spec=pl.BlockSpec((s, tm), lambda i, j: (0, i)),
                   b_spec=pl.BlockSpec((s, tn), lambda i, j: (0, j)),
                   out_shape=jax.ShapeDtypeStruct((m, n), out_dtype),
                   out_specs=pl.BlockSpec((tm, tn), lambda i, j: (i, j)), contract=TN)


ROWS = 256


def _row_spec(rows, width):
    return pl.BlockSpec((rows, width), lambda i: (i, 0))


def _fixed_spec(rows, width):
    return pl.BlockSpec((rows, width), lambda i: (0, 0))


def _column_pieces(rows, start, width):
    piece = math.gcd(start, width)
    assert piece % LANE == 0
    return [pl.BlockSpec((rows, piece), lambda i, b=start // piece + p: (i, b)) for p in range(width // piece)]


def _rms_fwd(name, x, g, cols=None, after=()):
    s = x.shape[0]
    start, w = cols or (0, x.shape[1])
    rows = min(ROWS, s)
    pieces = _column_pieces(rows, start, w) if cols else [_row_spec(rows, w)]
    n = len(pieces)

    def body(*refs):
        g_ref, o_ref = refs[n:]
        xv = refs[0][...] if n == 1 else jnp.concatenate([r[...] for r in refs[:n]], axis=1)
        o_ref[...] = (xv * _rstd(xv) * g_ref[...]).astype(o_ref.dtype)

    return _call(
        body, name=name, grid=(s // rows,), after=after,
        in_specs=[*pieces, _fixed_spec(1, w)],
        out_specs=_row_spec(rows, w),
        out_shape=jax.ShapeDtypeStruct((s, w), BF16),
        compiler_params=_params("parallel"),
    )(*[x] * n, g)


def _rms_bwd_call(name, x, g, dy, out_dtype, cols=None, after=()):
    s = x.shape[0]
    start, w = cols or (0, x.shape[1])
    rows = min(ROWS, s)
    pieces = _column_pieces(rows, start, w) if cols else [_row_spec(rows, w)]
    n = len(pieces)

    def body(*refs):
        g_ref, dy_ref, dx_ref, dg_ref = refs[n:]
        xv = refs[0][...] if n == 1 else jnp.concatenate([r[...] for r in refs[:n]], axis=1)
        dx, dgc = _rms_bwd(xv, g_ref[...], dy_ref[...].astype(F32))
        dx_ref[...] = dx.astype(dx_ref.dtype)
        _accumulate(dg_ref, _sublane_sum(dgc), pl.program_id(0))

    return _call(
        body, name=name, grid=(s // rows,), after=after,
        in_specs=[*pieces, _fixed_spec(1, w), _row_spec(rows, w)],
        out_specs=[_row_spec(rows, w), _fixed_spec(SUBLANE, w)],
        out_shape=[jax.ShapeDtypeStruct((s, w), out_dtype), jax.ShapeDtypeStruct((SUBLANE, w), F32)],
        compiler_params=_params("arbitrary"),
    )(*[x] * n, g, dy)


def _norm_up(name, x, cols, g, w, after=()):
    s = x.shape[0]
    start, width = cols
    n = w.shape[1]
    tm = min(TILE_M, s)
    pieces = _column_pieces(tm, start, width)
    n_p = len(pieces)

    def body(*refs):
        g_ref, w_ref, xn_ref, o_ref = refs[n_p:]
        xv = refs[0][...] if n_p == 1 else jnp.concatenate([r[...] for r in refs[:n_p]], axis=1)
        xn = (xv * _rstd(xv) * g_ref[...]).astype(BF16)
        xn_ref[...] = xn
        o_ref[...] = jnp.dot(xn, w_ref[...], preferred_element_type=F32)

    return _call(
        body, name=name, grid=(s // tm,), after=after,
        in_specs=[*pieces, _fixed_spec(1, width), _fixed_spec(width, n)],
        out_specs=[_row_spec(tm, width), _row_spec(tm, n)],
        out_shape=[jax.ShapeDtypeStruct((s, width), BF16), jax.ShapeDtypeStruct((s, n), F32)],
        compiler_params=_params("parallel"),
    )(*[x] * n_p, g, w)


def _up_norm_bwd(name, dy, w, x, cols, g, after=()):
    s, n = dy.shape
    start, width = cols
    tm = min(TILE_M, s)
    pieces = _column_pieces(tm, start, width)
    n_p = len(pieces)

    def body(dy_ref, w_ref, *refs):
        g_ref, dx_ref, dg_ref = refs[n_p:]
        xv = refs[0][...] if n_p == 1 else jnp.concatenate([r[...] for r in refs[:n_p]], axis=1)
        dxn = lax.dot_general(dy_ref[...], w_ref[...], (NT, ((), ())), preferred_element_type=F32)
        dx, dgc = _rms_bwd(xv, g_ref[...], dxn)
        dx_ref[...] = dx.astype(dx_ref.dtype)
        _accumulate(dg_ref, _sublane_sum(dgc), pl.program_id(0))

    return _call(
        body, name=name, grid=(s // tm,), after=after,
        in_specs=[_row_spec(tm, n), _fixed_spec(width, n), *pieces, _fixed_spec(1, width)],
        out_specs=[_row_spec(tm, width), _fixed_spec(SUBLANE, width)],
        out_shape=[jax.ShapeDtypeStruct((s, width), BF16), jax.ShapeDtypeStruct((SUBLANE, width), F32)],
        compiler_params=_params("arbitrary"),
    )(dy, w, *[x] * n_p, g)


def _mid_fwd(x, y, g_post, g_pre, after=()):
    s, w = x.shape
    rows = min(ROWS, s)

    def body(x_ref, y_ref, gp_ref, gq_ref, x2_ref, h2_ref):
        yv = y_ref[...]
        x2 = x_ref[...] + yv * _rstd(yv) * gp_ref[...]
        x2_ref[...] = x2
        h2_ref[...] = (x2 * _rstd(x2) * gq_ref[...]).astype(h2_ref.dtype)

    return _call(
        body, name="mid_fwd", grid=(s // rows,), after=after,
        in_specs=[_row_spec(rows, w), _row_spec(rows, w), _fixed_spec(1, w), _fixed_spec(1, w)],
        out_specs=[_row_spec(rows, w), _row_spec(rows, w)],
        out_shape=[jax.ShapeDtypeStruct((s, w), F32), jax.ShapeDtypeStruct((s, w), BF16)],
        compiler_params=_params("parallel"),
    )(x, y, g_post, g_pre)


def _head(m, x2, tgt, g):
    s, w = m.shape
    rows = min(ROWS, s)

    def body(m_ref, x2_ref, t_ref, g_ref, dout_ref, dm_ref, dg_ref, loss_ref):
        mv = m_ref[...]
        gv = g_ref[...]
        out = x2_ref[...] + mv * _rstd(mv) * gv
        err = out - t_ref[...]
        dout = err * (1.0 / w)
        dout_ref[...] = dout
        dm, dgc = _rms_bwd(mv, gv, dout)
        dm_ref[...] = dm.astype(dm_ref.dtype)
        sq = err * err
        lanes = sq[:, 0:LANE]
        for j in range(1, w // LANE):
            lanes = lanes + sq[:, j * LANE:(j + 1) * LANE]
        step = pl.program_id(0)
        _accumulate(dg_ref, _sublane_sum(dgc), step)
        _accumulate(loss_ref, _sublane_sum(lanes) * (0.5 / w), step)

    return pl.pallas_call(
        body, name="head", grid=(s // rows,),
        in_specs=[_row_spec(rows, w), _row_spec(rows, w), _row_spec(rows, w), _fixed_spec(1, w)],
        out_specs=[_row_spec(rows, w), _row_spec(rows, w), _fixed_spec(SUBLANE, w), _fixed_spec(SUBLANE, LANE)],
        out_shape=[jax.ShapeDtypeStruct((s, w), F32), jax.ShapeDtypeStruct((s, w), BF16),
                   jax.ShapeDtypeStruct((SUBLANE, w), F32), jax.ShapeDtypeStruct((SUBLANE, LANE), F32)],
        compiler_params=_params("arbitrary"),
    )(m, x2, tgt, g)


def _mid_bwd(x2, y, d_out, d_h2, g_pre, g_post, after=()):
    s, w = x2.shape
    rows = min(ROWS, s)

    def body(x2_ref, y_ref, dout_ref, dh2_ref, gq_ref, gp_ref, dx2_ref, dy_ref, dgq_ref, dgp_ref):
        dx, dgq = _rms_bwd(x2_ref[...], gq_ref[...], dh2_ref[...])
        dx2 = dout_ref[...] + dx
        dx2_ref[...] = dx2
        dy, dgp = _rms_bwd(y_ref[...], gp_ref[...], dx2)
        dy_ref[...] = dy.astype(dy_ref.dtype)
        step = pl.program_id(0)
        _accumulate(dgq_ref, _sublane_sum(dgq), step)
        _accumulate(dgp_ref, _sublane_sum(dgp), step)

    return _call(
        body, name="mid_bwd", grid=(s // rows,), after=after,
        in_specs=[_row_spec(rows, w)] * 4 + [_fixed_spec(1, w)] * 2,
        out_specs=[_row_spec(rows, w), _row_spec(rows, w), _fixed_spec(SUBLANE, w), _fixed_spec(SUBLANE, w)],
        out_shape=[jax.ShapeDtypeStruct((s, w), F32), jax.ShapeDtypeStruct((s, w), BF16),
                   jax.ShapeDtypeStruct((SUBLANE, w), F32), jax.ShapeDtypeStruct((SUBLANE, w), F32)],
        compiler_params=_params("arbitrary"),
    )(x2, y, d_out, d_h2, g_pre, g_post)


def _first_bwd(x, g, d_h1, d_x2, after=()):
    s, w = x.shape
    rows = min(ROWS, s)

    def body(x_ref, g_ref, dh_ref, dx2_ref, dx_ref, dg_ref):
        dx, dgc = _rms_bwd(x_ref[...], g_ref[...], dh_ref[...])
        dx_ref[...] = dx2_ref[...] + dx
        _accumulate(dg_ref, _sublane_sum(dgc), pl.program_id(0))

    return _call(
        body, name="first_bwd", grid=(s // rows,), after=after,
        in_specs=[_row_spec(rows, w), _fixed_spec(1, w), _row_spec(rows, w), _row_spec(rows, w)],
        out_specs=[_row_spec(rows, w), _fixed_spec(SUBLANE, w)],
        out_shape=[jax.ShapeDtypeStruct((s, w), F32), jax.ShapeDtypeStruct((SUBLANE, w), F32)],
        compiler_params=_params("arbitrary"),
    )(x, g, d_h1, d_x2)


def _shift_down(v, k):
    t = lax.broadcasted_iota(jnp.int32, v.shape, 0)
    return jnp.where(t >= k, pltpu.roll(v, k, 0), 0.0)


def _shift_up(v, k):
    n = v.shape[0]
    t = lax.broadcasted_iota(jnp.int32, v.shape, 0)
    return jnp.where(t < n - k, pltpu.roll(v, n - k, 0), 0.0)


def _conv_core(u, b, c, w):
    z = c * u
    conv = w[0:1, :] * _shift_down(z, 2) + w[1:2, :] * _shift_down(z, 1) + w[2:3, :] * z
    return z, conv, b * conv


def _conv_fwd(proj, conv_w, g, n_groups, out_width, after=()):
    s = proj.shape[0]

    def body(u_ref, b_ref, c_ref, w_ref, g_ref, o_ref):
        _, _, yr = _conv_core(u_ref[...], b_ref[...], c_ref[...], w_ref[...])
        o_ref[...] = (yr * _rstd(yr) * g_ref[...]).astype(o_ref.dtype)

    col = lambda k: pl.BlockSpec((s, HEAD), lambda i: (0, k * n_groups + i))
    return _call(
        body, name="conv_fwd", grid=(n_groups,), after=after,
        in_specs=[col(0), col(1), col(2), pl.BlockSpec((3, HEAD), lambda i: (0, i)), pl.BlockSpec((1, HEAD), lambda i: (0, i))],
        out_specs=pl.BlockSpec((s, HEAD), lambda i: (0, i)),
        out_shape=jax.ShapeDtypeStruct((s, out_width), BF16),
        compiler_params=_params("parallel"),
    )(proj, proj, proj, conv_w, g)


def _conv_bwd(proj, d_mix, conv_w, g, n_groups):
    s = proj.shape[0]
    width = n_groups * HEAD

    def body(u_ref, b_ref, c_ref, dy_ref, w_ref, g_ref, du_ref, db_ref, dc_ref, dg_ref, dw_ref):
        u, b, c, w = u_ref[...], b_ref[...], c_ref[...], w_ref[...]
        z, conv, yr = _conv_core(u, b, c, w)
        dyr, dgc = _rms_bwd(yr, g_ref[...], dy_ref[...])
        dconv = dyr * b
        db_ref[...] = (dyr * conv).astype(db_ref.dtype)
        dz = w[2:3, :] * dconv + w[1:2, :] * _shift_up(dconv, 1) + w[0:1, :] * _shift_up(dconv, 2)
        dc_ref[...] = (dz * u).astype(dc_ref.dtype)
        du_ref[...] = (dz * c).astype(du_ref.dtype)
        dg_ref[...] = _sublane_sum(dgc)
        dw_ref[0] = _sublane_sum(dconv * _shift_down(z, 2))
        dw_ref[1] = _sublane_sum(dconv * _shift_down(z, 1))
        dw_ref[2] = _sublane_sum(dconv * z)

    col = lambda k: pl.BlockSpec((s, HEAD), lambda i: (0, k * n_groups + i))
    grp = pl.BlockSpec((s, HEAD), lambda i: (0, i))
    return pl.pallas_call(
        body, name="conv_bwd", grid=(n_groups,),
        in_specs=[col(0), col(1), col(2), grp, pl.BlockSpec((3, HEAD), lambda i: (0, i)), pl.BlockSpec((1, HEAD), lambda i: (0, i))],
        out_specs=[grp, grp, grp, pl.BlockSpec((SUBLANE, HEAD), lambda i: (0, i)),
                   pl.BlockSpec((3, SUBLANE, HEAD), lambda i: (0, 0, i))],
        out_shape=[jax.ShapeDtypeStruct((s, width), BF16)] * 3
        + [jax.ShapeDtypeStruct((SUBLANE, width), F32), jax.ShapeDtypeStruct((3, SUBLANE, width), F32)],
        compiler_params=_params("parallel"),
    )(proj, proj, proj, d_mix, conv_w, g)


def _rope_tables(s, n_heads):
    pos = jnp.arange(s, dtype=F32)
    inv_freq = jnp.power(ROPE_THETA, -jnp.arange(0, ROPE, 2, dtype=F32) / ROPE)
    ang = pos[:, None] * inv_freq[None, :]
    cos, sin = jnp.cos(ang), jnp.sin(ang)
    cs = jnp.concatenate([cos, cos], axis=1)
    sn = jnp.concatenate([-sin, sin], axis=1)
    pad = jnp.zeros((s, LANE - ROPE), F32)
    return (jnp.tile(cs, (1, n_heads)), jnp.tile(sn, (1, n_heads)),
            jnp.concatenate([cs, pad], axis=1), jnp.concatenate([sn, pad], axis=1))


def _swap_halves(v):
    w = v.shape[1]
    lane = lax.broadcasted_iota(jnp.int32, v.shape, 1)
    first = (lane % ROPE) < (ROPE // 2)
    return jnp.where(first, pltpu.roll(v, w - ROPE // 2, 1), pltpu.roll(v, ROPE // 2, 1))


def _pack_heads(q, kv, proj, kr_col, tables, n_heads, after=()):
    s = q.shape[0]
    rows = min(ROWS, s)
    cq, sq, ck, sk = tables
    wq = n_heads * ROPE

    def body(q_ref, kv_ref, kr_ref, cq_ref, sq_ref, ck_ref, sk_ref, qo_ref, ko_ref, vo_ref):
        qr = q_ref[:, n_heads * HEAD:]
        qr = qr * cq_ref[...] + _swap_halves(qr) * sq_ref[...]
        krv = kr_ref[...]
        krv = krv * ck_ref[...] + _swap_halves(krv) * sk_ref[...]
        for h in range(n_heads):
            qo_ref[h] = jnp.concatenate([q_ref[:, h * HEAD:(h + 1) * HEAD], qr[:, h * ROPE:(h + 1) * ROPE]], axis=1).astype(BF16)
            ko_ref[h] = jnp.concatenate([kv_ref[:, 2 * h * HEAD:(2 * h + 1) * HEAD], krv[:, :ROPE]], axis=1).astype(BF16)
            vo_ref[h] = kv_ref[:, (2 * h + 1) * HEAD:(2 * h + 2) * HEAD].astype(BF16)

    hs = lambda w: pl.BlockSpec((n_heads, rows, w), lambda i: (0, i, 0))
    return _call(
        body, name="pack_heads", grid=(s // rows,), after=after,
        in_specs=[_row_spec(rows, q.shape[1]), _row_spec(rows, kv.shape[1]), pl.BlockSpec((rows, LANE), lambda i: (i, kr_col // LANE)),
                  _row_spec(rows, wq), _row_spec(rows, wq), _row_spec(rows, LANE), _row_spec(rows, LANE)],
        out_specs=[hs(QK), hs(QK), hs(HEAD)],
        out_shape=[jax.ShapeDtypeStruct((n_heads, s, QK), BF16), jax.ShapeDtypeStruct((n_heads, s, QK), BF16),
                   jax.ShapeDtypeStruct((n_heads, s, HEAD), BF16)],
        compiler_params=_params("parallel"),
    )(q, kv, proj, cq, sq, ck, sk)


def _unpack_heads(dq, dk, dv, tables, n_heads):
    s = dq.shape[1]
    rows = min(ROWS, s)
    cq, sq, ck, sk = tables
    wq = n_heads * ROPE

    def body(dq_ref, dk_ref, dv_ref, cq_ref, sq_ref, ck_ref, sk_ref, qo_ref, kvo_ref, kro_ref):
        dqr = jnp.concatenate([dq_ref[h][:, HEAD:] for h in range(n_heads)], axis=1)
        dqr = dqr * cq_ref[...] - _swap_halves(dqr) * sq_ref[...]
        dkr = dk_ref[0][:, HEAD:]
        for h in range(1, n_heads):
            dkr = dkr + dk_ref[h][:, HEAD:]
        dkr = jnp.concatenate([dkr, jnp.zeros((rows, LANE - ROPE), F32)], axis=1)
        dkr = dkr * ck_ref[...] - _swap_halves(dkr) * sk_ref[...]
        kro_ref[...] = dkr.astype(kro_ref.dtype)
        qo_ref[:, n_heads * HEAD:] = dqr.astype(qo_ref.dtype)
        for h in range(n_heads):
            qo_ref[:, h * HEAD:(h + 1) * HEAD] = dq_ref[h][:, :HEAD].astype(qo_ref.dtype)
            kvo_ref[:, 2 * h * HEAD:(2 * h + 1) * HEAD] = dk_ref[h][:, :HEAD].astype(kvo_ref.dtype)
            kvo_ref[:, (2 * h + 1) * HEAD:(2 * h + 2) * HEAD] = dv_ref[h].astype(kvo_ref.dtype)

    hs = lambda w: pl.BlockSpec((n_heads, rows, w), lambda i: (0, i, 0))
    return pl.pallas_call(
        body, name="unpack_heads", grid=(s // rows,),
        in_specs=[hs(QK), hs(QK), hs(HEAD), _row_spec(rows, wq), _row_spec(rows, wq), _row_spec(rows, LANE), _row_spec(rows, LANE)],
        out_specs=[_row_spec(rows, n_heads * QK), _row_spec(rows, 2 * n_heads * HEAD), _row_spec(rows, LANE)],
        out_shape=[jax.ShapeDtypeStruct((s, n_heads * QK), BF16), jax.ShapeDtypeStruct((s, 2 * n_heads * HEAD), BF16),
                   jax.ShapeDtypeStruct((s, LANE), BF16)],
        compiler_params=_params("parallel"),
    )(dq, dk, dv, cq, sq, ck, sk)


TQ = 256


LOG2_E = 1.4426950408889634


def _softmax_parts(q, k):
    tq, n_keys = q.shape[0], k.shape[0]
    sc = lax.dot_general(q, k, (NT, ((), ())), preferred_element_type=F32) * (QK ** -0.5 * LOG2_E)
    row = lax.broadcasted_iota(jnp.int32, (tq, tq), 0)
    col = lax.broadcasted_iota(jnp.int32, (tq, tq), 1)
    own = jnp.where(col // CHUNK <= row // CHUNK, sc[:, n_keys - tq:], NEG_INF)
    sc = own if n_keys == tq else jnp.concatenate([sc[:, :n_keys - tq], own], axis=1)
    e = jnp.exp2(sc - jnp.max(sc, axis=-1, keepdims=True))
    return e, 1.0 / jnp.sum(e, axis=-1, keepdims=True)


def _attn_fwd(q, k, v, g, mix, col0):
    n_heads, s, _ = q.shape
    tq = min(TQ, s)
    assert tq % CHUNK == 0 and s % tq == 0

    def body(q_ref, k_ref, v_ref, g_ref, mix_ref, o_ref, y_ref):
        for c in range(s // tq):
            rows, n_keys = pl.ds(c * tq, tq), (c + 1) * tq
            e, inv = _softmax_parts(q_ref[rows, :], k_ref[0:n_keys, :])
            o = jnp.dot(e.astype(BF16), v_ref[0:n_keys, :], preferred_element_type=F32) * inv
            o_ref[rows, :] = o
            y_ref[rows, :] = (o * _rstd(o) * g_ref[...]).astype(y_ref.dtype)

    head = lambda w: pl.BlockSpec((None, s, w), lambda h: (h, 0, 0))
    return pl.pallas_call(
        body, name="attn_fwd", grid=(n_heads,),
        in_specs=[head(QK), head(QK), head(HEAD), pl.BlockSpec((1, HEAD), lambda h: (0, h)), ANY],
        out_specs=[head(HEAD), pl.BlockSpec((s, HEAD), lambda h: (0, col0 // HEAD + h))],
        out_shape=[jax.ShapeDtypeStruct((n_heads, s, HEAD), F32), jax.ShapeDtypeStruct(mix.shape, mix.dtype)],
        input_output_aliases={4: 1},
        compiler_params=_params("parallel"),
    )(q, k, v, g, mix)


def _attn_bwd(q, k, v, o, d_mix, g, col0, after=()):
    n_heads, s, _ = q.shape
    tq = min(TQ, s)

    def body(q_ref, k_ref, v_ref, o_ref, dy_ref, g_ref, dq_ref, dk_ref, dv_ref, dg_ref):
        dg = None
        for c in reversed(range(s // tq)):
            rows, n_keys = pl.ds(c * tq, tq), (c + 1) * tq
            qv, kv_, vv = q_ref[rows, :], k_ref[0:n_keys, :], v_ref[0:n_keys, :]
            do, dgc = _rms_bwd(o_ref[rows, :], g_ref[...], dy_ref[rows, :])
            do = do.astype(BF16)
            dg = _sublane_sum(dgc) if dg is None else dg + _sublane_sum(dgc)
            e, inv = _softmax_parts(qv, kv_)
            p = e * inv
            dp = lax.dot_general(do, vv, (NT, ((), ())), preferred_element_type=F32)
            ds = (p * (dp - jnp.sum(p * dp, axis=-1, keepdims=True)) * (QK ** -0.5)).astype(BF16)
            dq_ref[rows, :] = jnp.dot(ds, kv_, preferred_element_type=F32)
            dk = lax.dot_general(ds, qv, (TN, ((), ())), preferred_element_type=F32)
            dv = lax.dot_general(p.astype(BF16), do, (TN, ((), ())), preferred_element_type=F32)
            if n_keys == s:
                dk_ref[...] = dk
                dv_ref[...] = dv
            else:
                dk_ref[0:n_keys, :] += dk
                dv_ref[0:n_keys, :] += dv
        dg_ref[...] = dg

    c0 = col0 // HEAD
    head = lambda w: pl.BlockSpec((None, s, w), lambda h: (h, 0, 0))
    return _call(
        body, name="attn_bwd", grid=(n_heads,), after=after,
        in_specs=[head(QK), head(QK), head(HEAD), head(HEAD), pl.BlockSpec((s, HEAD), lambda h: (0, c0 + h)),
                  pl.BlockSpec((1, HEAD), lambda h: (0, h))],
        out_specs=[head(QK), head(QK), head(HEAD), pl.BlockSpec((SUBLANE, HEAD), lambda h: (0, h))],
        out_shape=[jax.ShapeDtypeStruct((n_heads, s, QK), F32), jax.ShapeDtypeStruct((n_heads, s, QK), F32),
                   jax.ShapeDtypeStruct((n_heads, s, HEAD), F32), jax.ShapeDtypeStruct((SUBLANE, n_heads * HEAD), F32)],
        compiler_params=_params("parallel"),
    )(q, k, v, o, d_mix, g)


TILE_M = 1024
TILE_N = 1024


def _up_fwd(h2, w_up):
    s, d = h2.shape
    nb, _, fb = w_up.shape
    tm = min(TILE_M,s)

    def epilogue(acc):
        r = jnp.maximum(acc, 0.0)
        return r * r, r

    blk = pl.BlockSpec((tm, fb), lambda i, j: (i, j))
    return _matmul("up_fwd", h2, w_up, grid=(s // tm, nb),
                   a_spec=pl.BlockSpec((tm, d), lambda i, j: (i, 0)),
                   b_spec=pl.BlockSpec((None, d, fb), lambda i, j: (j, 0, 0)),
                   out_shape=[jax.ShapeDtypeStruct((s, nb * fb), BF16)] * 2, out_specs=[blk, blk],
                   contract=NN, epilogue=epilogue)


def _down_fwd(a, w_down):
    s, f = a.shape
    d = w_down.shape[1]
    tm, tn, tk = min(TILE_M,s), min(TILE_N,d), 2048
    nk = f // tk
    return _matmul("down_fwd", a, w_down, grid=(s // tm, d // tn, nk),
                   a_spec=pl.BlockSpec((tm, tk), lambda i, j, k: (i, k)),
                   b_spec=pl.BlockSpec((tk, tn), lambda i, j, k: (k, j)),
                   out_shape=jax.ShapeDtypeStruct((s, d), F32),
                   out_specs=pl.BlockSpec((tm, tn), lambda i, j, k: (i, j)),
                   contract=NN, nk=nk, acc_shape=(tm, tn))


def _down_bwd_act(d_m, w_down, r, after=()):
    s, d = d_m.shape
    f = w_down.shape[0]
    tm, tn = min(TILE_M,s), min(TILE_N,f)
    blk = pl.BlockSpec((tm, tn), lambda i, j: (i, j))
    return _matmul("down_bwd_act", d_m, w_down, grid=(s // tm, f // tn), after=after,
                   a_spec=pl.BlockSpec((tm, d), lambda i, j: (i, 0)),
                   b_spec=pl.BlockSpec((tn, d), lambda i, j: (j, 0)),
                   out_shape=jax.ShapeDtypeStruct((s, f), BF16), out_specs=blk, contract=NT,
                   extras=(r,), extra_specs=(blk,),
                   epilogue=lambda acc, rv: (acc * (2.0 * rv.astype(F32)),))


def _up_bwd_act(d_up, w_up, after=()):
    s, _ = d_up.shape
    nb, d, fb = w_up.shape
    tm, tn = min(TILE_M, s), min(TILE_N,d)
    pair = 2
    n_after = len(after)

    def body(a_ref, w_ref, *rest):
        o_ref, acc_ref = rest[n_after:]
        k = pl.program_id(2)
        p = None
        for t in range(pair):
            term = lax.dot_general(a_ref[:, t * fb:(t + 1) * fb], w_ref[t], (NT, ((), ())), preferred_element_type=F32)
            p = term if p is None else p + term
        _accumulate(acc_ref, p, k)

        @pl.when(k == nb // pair - 1)
        def _():
            o_ref[...] = acc_ref[...]

    return pl.pallas_call(
        body, name="up_bwd_act", grid=(s // tm, d // tn, nb // pair),
        in_specs=[pl.BlockSpec((tm, pair * fb), lambda i, j, k: (i, k)),
                  pl.BlockSpec((pair, tn, fb), lambda i, j, k: (k, j, 0))] + [ANY] * n_after,
        out_specs=pl.BlockSpec((tm, tn), lambda i, j, k: (i, j)),
        out_shape=jax.ShapeDtypeStruct((s, d), F32),
        scratch_shapes=[pltpu.VMEM((tm, tn), F32)],
        compiler_params=_params("parallel", "parallel", "arbitrary"),
    )(d_up, w_up, *after)


def _half_grad(name, a, b, core, home, received, after, *, grid, a_block, a_map, b_block, b_map, o_block, o_map, out_shape):
    n_after = len(after)
    pick = (lambda ref: ref[0]) if home else (lambda ref: 1 - ref[0])

    def body(core_ref, a_ref, b_ref, *rest):
        acc = lax.dot_general(a_ref[...], b_ref[...], (TN, ((), ())), preferred_element_type=F32)
        if received is not None:
            acc = acc + rest[0][...].astype(F32)
        rest[-1][...] = acc.astype(rest[-1].dtype)

    wrap = lambda fn: (lambda i, j, core_ref: fn(i, j, pick(core_ref)))
    o_spec = pl.BlockSpec(o_block, wrap(o_map))
    extra = [] if received is None else [o_spec]
    operands = [] if received is None else [received]
    return pl.pallas_call(
        body, name=name,
        grid_spec=pltpu.PrefetchScalarGridSpec(
            num_scalar_prefetch=1, grid=grid,
            in_specs=[pl.BlockSpec(a_block, wrap(a_map)), pl.BlockSpec(b_block, wrap(b_map))] + extra + [ANY] * n_after,
            out_specs=o_spec),
        out_shape=out_shape,
        compiler_params=_params("parallel", "parallel"),
    )(core, a, b, *operands, *after)


def _down_half_grad(name, a, d_m, core, home, received=None, after=()):
    s, f = a.shape
    d = d_m.shape[1]
    r = f // N_DEV
    tn = min(TILE_N, d)
    return _half_grad(name, a, d_m, core, home, received, after, grid=(N_CHIP, d // tn),
                      a_block=(s, r), a_map=lambda k, j, p: (0, 2 * k + p),
                      b_block=(s, tn), b_map=lambda k, j, p: (0, j),
                      o_block=(None, r, tn), o_map=lambda k, j, p: (k, 0, j),
                      out_shape=jax.ShapeDtypeStruct((N_CHIP, r, d), BF16))


def _up_half_grad(name, h2, d_up, core, home, received=None, after=()):
    s, d = h2.shape
    fb = d_up.shape[1] // N_DEV
    tm = min(TILE_M, d)
    return _half_grad(name, h2, d_up, core, home, received, after, grid=(d // tm, N_CHIP),
                      a_block=(s, tm), a_map=lambda i, k, p: (0, i),
                      b_block=(s, fb), b_map=lambda i, k, p: (0, 2 * k + p),
                      o_block=(None, tm, fb), o_map=lambda i, k, p: (k, i, 0),
                      out_shape=jax.ShapeDtypeStruct((N_CHIP, d, fb), BF16))


def _in_pad(in_width):
    return -(-in_width // LANE) * LANE


def _join_col_shards(name, blocks, own, device, pieces=None):
    n, r, w = blocks.shape
    rows = min(ROWS, r)
    pieces = pieces or [(j, 0, w) for j in range(n)]
    used = sum(b - a for _, a, b in pieces)
    width = _in_pad(used)

    def body(dev_ref, x_ref, own_ref, o_ref):
        block = lambda j: jnp.where(dev_ref[0] == j, own_ref[...], x_ref[j])
        cols = [block(j)[:, a:b] for j, a, b in pieces]
        tail = [jnp.zeros((rows, width - used), o_ref.dtype)] if width > used else []
        o_ref[...] = jnp.concatenate(cols + tail, axis=1)

    return pl.pallas_call(
        body, name=name,
        grid_spec=pltpu.PrefetchScalarGridSpec(
            num_scalar_prefetch=1, grid=(r // rows,),
            in_specs=[pl.BlockSpec((n, rows, w), lambda i, dev: (0, i, 0)), pl.BlockSpec((rows, w), lambda i, dev: (i, 0))],
            out_specs=pl.BlockSpec((rows, width), lambda i, dev: (i, 0))),
        out_shape=jax.ShapeDtypeStruct((r, width), blocks.dtype),
        compiler_params=_params("parallel"),
    )(device, blocks, own)


def _permute_q_cols(w_uq, n_heads):
    r = w_uq.shape[0]
    w3 = w_uq.reshape(r, n_heads, QK)
    return jnp.concatenate([w3[:, :, :HEAD].reshape(r, n_heads * HEAD), w3[:, :, HEAD:].reshape(r, n_heads * ROPE)], axis=1)


def _unpermute_q_rows(wt, n_heads):
    r = wt.shape[1]
    nope = wt[:n_heads * HEAD].reshape(n_heads, HEAD, r)
    rope = wt[n_heads * HEAD:].reshape(n_heads, ROPE, r)
    return jnp.concatenate([nope, rope], axis=1).reshape(n_heads * QK, r)


def _local_step(x, tgt, gains, weights, grads, first_after=()):
    pre_mix_g, q_norm_g, kv_norm_g, conv_out_g, attn_out_g, post_mix_g, pre_mlp_g, post_mlp_g = gains
    s, d = x.shape
    conv_width = conv_out_g.shape[1]
    n_groups = conv_width // HEAD
    r_q, r_kv = q_norm_g.shape[1], kv_norm_g.shape[1]
    n_heads = attn_out_g.shape[1] // HEAD
    c_q0 = 3 * conv_width
    c_kv0 = c_q0 + r_q
    c_kr0 = c_kv0 + r_kv
    in_pad = _in_pad(c_kr0 + ROPE)
    tn_in = in_pad // 5 if in_pad % (5 * LANE) == 0 else LANE
    tables = _rope_tables(s, n_heads)

    h1 = _rms_fwd("pre_mix_norm", x, pre_mix_g, after=first_after)
    weights.forward(0, (h1,))
    weights.relay(0, tables)
    w_in_p, conv_w = weights.ready(0, ())
    proj = _mm_nn("in_proj", h1, w_in_p, F32, TILE_M, tn_in)
    y_conv = _conv_fwd(proj, conv_w, conv_out_g, n_groups, conv_width + n_heads * HEAD, after=weights.forward(1, (proj,)))
    w_uq_p, w_ukv, w_o = weights.ready(1, (y_conv,))
    qn, q = _norm_up("q_up", proj, (c_q0, r_q), q_norm_g, w_uq_p)
    kvn, kv = _norm_up("kv_up", proj, (c_kv0, r_kv), kv_norm_g, w_ukv)
    qh, kh, vh = _pack_heads(q, kv, proj, c_kr0, tables, n_heads, after=weights.forward(2, (q, kv)))
    o, mix = _attn_fwd(qh, kh, vh, attn_out_g, y_conv, conv_width)
    y = _mm_nn("out_proj", mix, w_o, F32, TILE_M, TILE_N)
    x2, h2 = _mid_fwd(x, y, post_mix_g, pre_mlp_g, after=weights.forward(3, (y,)))
    weights.relay(2, (h2,))
    (w_up,) = weights.ready(2, ())
    a, r = _up_fwd(h2, w_up)
    weights.relay(3, (a,))
    (w_down,) = weights.ready(3, ())
    m = _down_fwd(a, w_down)

    d_out, d_m, dg_post_mlp, loss_part = _head(m, x2, tgt, post_mlp_g)
    core = grads.core
    away = _down_half_grad("down_bwd_w_away", a, d_m, core, home=False)
    d_up = _down_bwd_act(d_m, w_down, r, after=grads.send_away(0, away))
    sums = _down_half_grad("down_bwd_w_home", a, d_m, core, home=True, received=grads.received(0, (d_up,)))
    away = _up_half_grad("up_bwd_w_away", h2, d_up, core, home=False, after=grads.send_sums(0, (sums,)))
    d_h2 = _up_bwd_act(d_up, w_up, after=grads.send_away(1, away))
    sums = _up_half_grad("up_bwd_w_home", h2, d_up, core, home=True, received=grads.received(1, (d_h2,)))
    d_x2, d_y, dg_pre_mlp, dg_post_mix = _mid_bwd(x2, y, d_out, d_h2, pre_mlp_g, post_mix_g, after=grads.send_sums(1, (sums,)))
    d_mix = _mm_nt("out_proj_bwd_act", d_y, w_o, F32, TILE_M, TILE_N)
    gw_o = _mm_tn("out_proj_bwd_w", mix, d_y, BF16, TILE_M, TILE_N)
    dqh, dkh, dvh, dg_attn = _attn_bwd(qh, kh, vh, o, d_mix, attn_out_g, conv_width, after=grads.full(2, (gw_o,)))
    d_q, d_kv, d_kr = _unpack_heads(dqh, dkh, dvh, tables, n_heads)
    gw_uq_t = _mm_tn("q_up_bwd_w", d_q, qn, F32, TILE_M, TILE_N)
    gw_ukv = _mm_tn("kv_up_bwd_w", kvn, d_kv, BF16, TILE_M, TILE_N)
    d_cq, dg_q = _up_norm_bwd("q_up_bwd_act", d_q, w_uq_p, proj, (c_q0, r_q), q_norm_g, after=grads.full(3, (gw_uq_t, gw_ukv)))
    d_ckv, dg_kv = _up_norm_bwd("kv_up_bwd_act", d_kv, w_ukv, proj, (c_kv0, r_kv), kv_norm_g)
    d_u, d_b, d_c, dg_conv, dw_conv = _conv_bwd(proj, d_mix, conv_w, conv_out_g, n_groups)
    d_proj = jnp.concatenate([d_u, d_b, d_c, d_cq, d_ckv, d_kr[:, :in_pad - c_kr0]], axis=1)
    gw_in_t = _mm_tn("in_proj_bwd_w", d_proj, h1, F32, tn_in, TILE_N)
    d_h1 = _mm_nt("in_proj_bwd_act", d_proj, w_in_p, F32, TILE_M, TILE_N, after=grads.send_away(4, gw_in_t))
    grad_x, dg_pre_mix = _first_bwd(x, pre_mix_g, d_h1, d_x2, after=grads.full(4, (gw_in_t,), received=(d_h1,)))

    small = [dg_pre_mix, dg_q, dg_kv, dg_conv, dg_attn, dg_post_mix, dg_pre_mlp, dg_post_mlp,
             dw_conv[0], dw_conv[1], dw_conv[2], loss_part]
    return grad_x, jnp.concatenate(small, axis=1)


HBM = pl.BlockSpec(memory_space=pltpu.HBM)
SEM = pl.BlockSpec(memory_space=pltpu.SEMAPHORE)
IN_VMEM = pl.BlockSpec(memory_space=pltpu.VMEM)
SPLIT = pltpu.CompilerParams(has_side_effects=pltpu.SideEffectType.DATAFLOW_SIDE_EFFECTING)


def _in_hbm(a):
    return pltpu.with_memory_space_constraint(a, pltpu.HBM)


def _hbm_like(a):
    return pltpu.HBM(a.shape, a.dtype)


def _place():
    x, y, c = lax.axis_index("x"), lax.axis_index("y"), lax.axis_index("c")
    other_chips = [(1 - x, y), (x, 1 - y), (1 - x, 1 - y)]
    return x, y, c, other_chips


def _block(px, py, pc):
    return 4 * px + 2 * py + pc


def _await(block, sem):
    pltpu.make_async_copy(block, block, sem).wait()


def _relay_route(x, y, c):
    came_from = ((1 - x) * (1 - c) + x * c, y * (1 - c) + (1 - y) * c)
    goes_to = (x * (1 - c) + (1 - x) * c, (1 - y) * (1 - c) + y * c)
    return came_from, goes_to


def _gather_start(name, shards, groups, relayed=(), after=()):
    n, ng = len(shards), len(groups)
    lands = [lax.empty((N_DEV, *a.shape), a.dtype) for a in shards]

    def body(*refs):
        src, land = refs[:n], refs[n:2 * n]
        sems, token = refs[2 * n + len(after):2 * n + len(after) + 2 * ng], refs[-1]
        x, y, c, chips = _place()
        targets = [(x, y, 1 - c)] + [(*chip, c) for chip in chips]
        for gi, group in enumerate(groups):
            for i, w in enumerate(group):
                for k, to in enumerate(targets[:3] if gi in relayed else targets):
                    pltpu.make_async_remote_copy(
                        src_ref=src[w], dst_ref=land[w].at[_block(x, y, c)],
                        send_sem=sems[2 * gi].at[4 * i + k], recv_sem=sems[2 * gi + 1].at[4 * i + k],
                        device_id=to, device_id_type=MESH).start()
        token[...] = jnp.zeros_like(token)

    sem_shapes = [pltpu.SemaphoreType.DMA((4 * len(g),)) for g in groups for _ in range(2)]
    out = pl.pallas_call(
        body, name=name,
        in_specs=[HBM] * (2 * n) + [ANY] * len(after),
        out_specs=[SEM] * (2 * ng) + [HBM] * (2 * n) + [IN_VMEM],
        out_shape=sem_shapes + [_hbm_like(a) for a in shards] + [_hbm_like(a) for a in lands]
        + [jax.ShapeDtypeStruct((SUBLANE, LANE), F32)],
        input_output_aliases={i: 2 * ng + i for i in range(2 * n)},
        compiler_params=SPLIT,
    )(*[_in_hbm(a) for a in shards], *[_in_hbm(a) for a in lands], *after)
    sems = [(out[2 * gi], out[2 * gi + 1]) for gi in range(ng)]
    return sems, out[2 * ng:2 * ng + n], out[2 * ng + n:2 * ng + 2 * n], out[-1]


def _gather_forward(name, shards, lands, send1, recv1, after, relayed=False):
    n = len(lands)

    def body(*refs):
        src, land = refs[:n], refs[n:2 * n]
        s1, r1 = refs[2 * n], refs[2 * n + 1]
        s2, r2 = refs[2 * n + 2 + len(after)], refs[2 * n + 3 + len(after)]
        x, y, c, chips = _place()
        me, sibling = (x, y, c), (x, y, 1 - c)
        for j, chip in enumerate(chips[:2] if relayed else chips):
            for i in range(n):
                blk = land[i].at[_block(*chip, c)]
                pltpu.make_async_remote_copy(src_ref=blk, dst_ref=blk, send_sem=s1.at[4 * i + 1 + j], recv_sem=r1.at[4 * i + 1 + j],
                                             device_id=me, device_id_type=MESH).wait_recv()
                pltpu.make_async_remote_copy(src_ref=blk, dst_ref=blk, send_sem=s2.at[3 * i + j], recv_sem=r2.at[3 * i + j],
                                             device_id=sibling, device_id_type=MESH).start()
        if relayed:
            came_from, goes_to = _relay_route(x, y, c)
            for i in range(n):
                blk = land[i].at[_block(*came_from, c)]
                pltpu.make_async_remote_copy(src_ref=blk, dst_ref=blk, send_sem=s2.at[3 * i + 2], recv_sem=r2.at[3 * i + 2],
                                             device_id=(*goes_to, c), device_id_type=MESH).start()
        for i in range(n):
            blk = land[i].at[_block(x, y, 1 - c)]
            pltpu.make_async_remote_copy(src_ref=blk, dst_ref=blk, send_sem=s1.at[4 * i], recv_sem=r1.at[4 * i],
                                         device_id=me, device_id_type=MESH).wait_recv()
            for k in range(3 if relayed else 4):
                pltpu.make_async_remote_copy(src_ref=src[i], dst_ref=land[i].at[_block(x, y, c)], send_sem=s1.at[4 * i + k],
                                             recv_sem=r1.at[4 * i + k], device_id=sibling, device_id_type=MESH).wait_send()

    sem = pltpu.SemaphoreType.DMA((3 * n,))
    out = pl.pallas_call(
        body, name=name,
        in_specs=[HBM] * (2 * n) + [SEM, SEM] + [ANY] * len(after),
        out_specs=[SEM, SEM] + [HBM] * n,
        out_shape=[sem, sem] + [_hbm_like(a) for a in lands],
        input_output_aliases={n + i: 2 + i for i in range(n)},
        compiler_params=SPLIT,
    )(*shards, *lands, send1, recv1, *after)
    return (out[0], out[1]), out[2:]


def _gather_relay_forward(name, lands, send2, recv2, after):
    n = len(lands)

    def body(*refs):
        land, s2, r2 = refs[:n], refs[n], refs[n + 1]
        s3, r3 = refs[n + 2 + len(after)], refs[n + 3 + len(after)]
        x, y, c, _ = _place()
        me, sibling = (x, y, c), (x, y, 1 - c)
        came_from, _ = _relay_route(x, y, c)
        for i in range(n):
            blk = land[i].at[_block(1 - x, 1 - y, c)]
            pltpu.make_async_remote_copy(src_ref=blk, dst_ref=blk, send_sem=s2.at[3 * i + 2], recv_sem=r2.at[3 * i + 2],
                                         device_id=me, device_id_type=MESH).wait_recv()
            pltpu.make_async_remote_copy(src_ref=blk, dst_ref=blk, send_sem=s3.at[i], recv_sem=r3.at[i],
                                         device_id=sibling, device_id_type=MESH).start()
            sent = land[i].at[_block(*came_from, c)]
            pltpu.make_async_remote_copy(src_ref=sent, dst_ref=sent, send_sem=s2.at[3 * i + 2], recv_sem=r2.at[3 * i + 2],
                                         device_id=me, device_id_type=MESH).wait_send()

    sem = pltpu.SemaphoreType.DMA((n,))
    out = pl.pallas_call(
        body, name=name,
        in_specs=[HBM] * n + [SEM, SEM] + [ANY] * len(after),
        out_specs=[SEM, SEM] + [HBM] * n,
        out_shape=[sem, sem] + [_hbm_like(a) for a in lands],
        input_output_aliases={i: 2 + i for i in range(n)},
        compiler_params=SPLIT,
    )(*lands, send2, recv2, *after)
    return (out[0], out[1]), out[2:]


def _gather_wait(name, lands, send2, recv2, after, relay_sems=None):
    n = len(lands)
    n_sems = 2 if relay_sems is None else 4

    def body(*refs):
        land, s2, r2 = refs[:n], refs[n], refs[n + 1]
        for i in range(n):
            for j in range(3 if relay_sems is None else 2):
                _await(land[i].at[0], r2.at[3 * i + j])
                _await(land[i].at[0], s2.at[3 * i + j])
            if relay_sems is not None:
                _await(land[i].at[0], refs[n + 3].at[i])
                _await(land[i].at[0], refs[n + 2].at[i])

    return pl.pallas_call(
        body, name=name,
        in_specs=[HBM] * n + [SEM] * n_sems + [ANY] * len(after), out_specs=[HBM] * n, out_shape=[_hbm_like(a) for a in lands],
        input_output_aliases={i: i for i in range(n)},
        compiler_params=SPLIT,
    )(*lands, send2, recv2, *(relay_sems or ()), *after)


def _pair_exchange(name, grads, shard_rows):
    n = len(grads)
    shapes = [(g.shape[1:] if r is None else (r, g.shape[1])) for g, r in zip(grads, shard_rows)]

    def body(*refs):
        ins, recv = refs[:n], refs[n:2 * n]
        send_sems, recv_sems = refs[2 * n:]
        x, y, c, _ = _place()
        sends = []
        for w in range(n):
            for k in range(N_CHIP):
                j, r = 2 * k + 1 - c, shard_rows[w]
                src = ins[w].at[j] if r is None else ins[w].at[pl.ds(pl.multiple_of(j * r, SUBLANE), r), :]
                sends.append(pltpu.make_async_remote_copy(
                    src_ref=src, dst_ref=recv[w].at[k],
                    send_sem=send_sems.at[w, k], recv_sem=recv_sems.at[w, k],
                    device_id=(x, y, 1 - c), device_id_type=MESH))
        for cp in sends:
            cp.start()
        for cp in sends:
            cp.wait()

    return pl.pallas_call(
        body, name=name,
        in_specs=[ANY] * n, out_specs=[ANY] * n,
        out_shape=[jax.ShapeDtypeStruct((N_CHIP, *shape), g.dtype) for g, shape in zip(grads, shapes)],
        scratch_shapes=[pltpu.SemaphoreType.DMA((n, N_CHIP))] * 2,
    )(*grads)


def _pair_sum_rows(name, grad, received, core):
    _, r, c = received.shape
    tc = _fit(c, 512)

    def body(core_ref, a_ref, b_ref, o_ref):
        o_ref[...] = (a_ref[...] + b_ref[...]).astype(o_ref.dtype)

    spec = pl.BlockSpec((None, r, tc), lambda k, i, core_ref: (k, 0, i))
    return pl.pallas_call(
        body, name=name,
        grid_spec=pltpu.PrefetchScalarGridSpec(
            num_scalar_prefetch=1, grid=(N_CHIP, c // tc),
            in_specs=[pl.BlockSpec((r, tc), lambda k, i, core_ref: (2 * k + core_ref[0], i)), spec],
            out_specs=spec),
        out_shape=jax.ShapeDtypeStruct(received.shape, BF16),
        compiler_params=_params("parallel", "parallel"),
    )(core, grad, received)


def _pair_sum(name, grad, received, core):
    _, r, c = received.shape
    rows = min(ROWS, r)
    assert r % rows == 0

    def body(core_ref, a_ref, b_ref, o_ref):
        o_ref[...] = (a_ref[...].astype(F32) + b_ref[...].astype(F32)).astype(o_ref.dtype)

    spec = pl.BlockSpec((None, rows, c), lambda k, i, core_ref: (k, i, 0))
    return pl.pallas_call(
        body, name=name,
        grid_spec=pltpu.PrefetchScalarGridSpec(
            num_scalar_prefetch=1, grid=(N_CHIP, r // rows),
            in_specs=[pl.BlockSpec((None, None, rows, c), lambda k, i, core_ref: (k, core_ref[0], i, 0)), spec],
            out_specs=spec),
        out_shape=jax.ShapeDtypeStruct(received.shape, received.dtype),
        compiler_params=_params("parallel", "parallel"),
    )(core, grad.reshape(N_CHIP, 2, r, c), received)


def _away_shard(src, k, c, shard_rows):
    if shard_rows is None:
        return src.at[k]
    return src.at[pl.ds(pl.multiple_of((2 * k + 1 - c) * shard_rows, SUBLANE), shard_rows), :]


def _pair_send_start(name, away, shard_rows=None):
    shape = away.shape if shard_rows is None else (N_CHIP, shard_rows, away.shape[1])
    land = lax.empty(shape, away.dtype)

    def body(src, dst, send, recv, src_thru, dst_thru, token):
        x, y, c, _ = _place()
        for k in range(N_CHIP):
            pltpu.make_async_remote_copy(src_ref=_away_shard(src, k, c, shard_rows), dst_ref=dst.at[k], send_sem=send.at[k],
                                         recv_sem=recv.at[k], device_id=(x, y, 1 - c), device_id_type=MESH).start()
        token[...] = jnp.zeros_like(token)

    sem = pltpu.SemaphoreType.DMA((N_CHIP,))
    out = pl.pallas_call(
        body, name=name,
        in_specs=[HBM, HBM], out_specs=[SEM, SEM, HBM, HBM, IN_VMEM],
        out_shape=[sem, sem, _hbm_like(away), _hbm_like(land), jax.ShapeDtypeStruct((SUBLANE, LANE), F32)],
        input_output_aliases={0: 2, 1: 3},
        compiler_params=SPLIT,
    )(_in_hbm(away), _in_hbm(land))
    return (out[0], out[1]), out[2], out[3], out[4]


def _pair_send_wait(name, sems, src, land, after, shard_rows=None):
    def body(src_ref, dst_ref, send, recv, *rest):
        for k in range(N_CHIP):
            _await(dst_ref.at[k], send.at[k])
            _await(dst_ref.at[k], recv.at[k])

    return pl.pallas_call(
        body, name=name,
        in_specs=[HBM, HBM, SEM, SEM] + [ANY] * len(after), out_specs=HBM, out_shape=_hbm_like(land),
        input_output_aliases={1: 0},
        compiler_params=SPLIT,
    )(src, land, *sems, *after)


def _chip_send_start(name, sums):
    n = len(sums)
    lands = [lax.empty(a.shape, a.dtype) for a in sums]

    def body(*refs):
        src, land = refs[:n], refs[n:2 * n]
        send, recv, token = refs[2 * n], refs[2 * n + 1], refs[-1]
        x, y, c, chips = _place()
        for w in range(n):
            for j, (px, py) in enumerate(chips):
                pltpu.make_async_remote_copy(
                    src_ref=src[w].at[2 * px + py], dst_ref=land[w].at[2 * x + y],
                    send_sem=send.at[3 * w + j], recv_sem=recv.at[3 * w + j],
                    device_id=(px, py, c), device_id_type=MESH).start()
        token[...] = jnp.zeros_like(token)

    sem = pltpu.SemaphoreType.DMA((3 * n,))
    out = pl.pallas_call(
        body, name=name,
        in_specs=[HBM] * (2 * n),
        out_specs=[SEM, SEM] + [HBM] * (2 * n) + [IN_VMEM],
        out_shape=[sem, sem] + [_hbm_like(a) for a in sums] + [_hbm_like(a) for a in lands]
        + [jax.ShapeDtypeStruct((SUBLANE, LANE), F32)],
        input_output_aliases={i: 2 + i for i in range(2 * n)},
        compiler_params=SPLIT,
    )(*[_in_hbm(a) for a in sums], *[_in_hbm(a) for a in lands])
    return (out[0], out[1]), out[2:2 + n], out[2 + n:2 + 2 * n], out[-1]


def _chip_send_wait(name, groups, after):
    counts = [len(g[1]) for g in groups]
    n = sum(counts)

    def body(*refs):
        land = refs[n:2 * n]
        sems = refs[2 * n:2 * n + 2 * len(groups)]
        w = 0
        for gi, count in enumerate(counts):
            for i in range(count):
                for j in range(3):
                    _await(land[w].at[0], sems[2 * gi].at[3 * i + j])
                    _await(land[w].at[0], sems[2 * gi + 1].at[3 * i + j])
                w += 1

    sums = [a for g in groups for a in g[1]]
    lands = [a for g in groups for a in g[2]]
    sems = [s for g in groups for s in g[0]]
    return pl.pallas_call(
        body, name=name,
        in_specs=[HBM] * (2 * n) + [SEM] * len(sems) + [ANY] * len(after),
        out_specs=[HBM] * n, out_shape=[_hbm_like(a) for a in lands],
        input_output_aliases={n + i: i for i in range(n)},
        compiler_params=SPLIT,
    )(*sums, *lands, *sems, *after)


def _small_all_reduce(part, after=()):
    _, w = part.shape

    def body(p_ref, *rest):
        o_ref, buf, send_sems, recv_sems = rest[len(after):]
        x, y, c, _ = _place()
        me = 4 * x + 2 * y + c
        buf[me] = jnp.sum(p_ref[...], axis=0, keepdims=True)
        copies = []
        for k in range(1, N_DEV):
            dx, dy, dc = (k >> 2) & 1, (k >> 1) & 1, k & 1
            copies.append(pltpu.make_async_remote_copy(
                src_ref=buf.at[me], dst_ref=buf.at[me], send_sem=send_sems.at[k - 1], recv_sem=recv_sems.at[k - 1],
                device_id=(x ^ dx, y ^ dy, c ^ dc), device_id_type=MESH))
        for cp in copies:
            cp.start()
        for cp in copies:
            cp.wait()
        tot = buf[0]
        for d in range(1, N_DEV):
            tot = tot + buf[d]
        o_ref[...] = tot
        loss = jnp.sum(tot[:, w - LANE:], axis=1, keepdims=True)
        o_ref[:, w - LANE:] = jnp.broadcast_to(loss, (1, LANE))

    return pl.pallas_call(
        body, name="small_all_reduce",
        in_specs=[IN_VMEM] + [ANY] * len(after), out_specs=IN_VMEM,
        out_shape=jax.ShapeDtypeStruct((1, w), F32),
        scratch_shapes=[pltpu.VMEM((N_DEV, 1, w), F32), pltpu.SemaphoreType.DMA((N_DEV - 1,)), pltpu.SemaphoreType.DMA((N_DEV - 1,))],
        compiler_params=pltpu.CompilerParams(vmem_limit_bytes=VMEM_LIMIT_BYTES),
    )(part, *after)


def _adamw(w, g, m, v):
    m = ADAM_B1 * m + (1.0 - ADAM_B1) * g
    v = ADAM_B2 * v + (1.0 - ADAM_B2) * (g * g)
    m_hat = m / (1.0 - ADAM_B1 ** ADAM_STEP)
    v_hat = v / (1.0 - ADAM_B2 ** ADAM_STEP)
    delta = -ADAM_LR * (m_hat / (jnp.sqrt(v_hat) + ADAM_EPS) + ADAM_WD * w)
    return delta, m, v


def _sum_adam(name, parts, sums, chip, w, m, v, after=()):
    _, r, c = w.shape
    n_after = len(after)
    by_rows = r % ROWS == 0 or r < ROWS
    tr, tc = (min(ROWS, r), c) if by_rows else (r, _fit(c, 512))
    at = (lambda i: (i, 0)) if by_rows else (lambda i: (0, i))

    def body(chip_ref, p_ref, own_ref, w_ref, m_ref, v_ref, *rest):
        g_ref, d_ref, mo_ref, vo_ref = rest[n_after:]
        g = None
        for k in range(N_CHIP):
            term = jnp.where(chip_ref[0] == k, own_ref[...], p_ref[k]).astype(F32)
            g = term if g is None else g + term
        g_ref[...] = g
        d_ref[...], mo_ref[...], vo_ref[...] = _adamw(w_ref[...], g, m_ref[...], v_ref[...])

    blk = pl.BlockSpec((None, tr, tc), lambda i, chip_ref: (0, *at(i)))
    out = jax.ShapeDtypeStruct((1, r, c), F32)
    return pl.pallas_call(
        body, name=name,
        grid_spec=pltpu.PrefetchScalarGridSpec(
            num_scalar_prefetch=1, grid=(r // tr if by_rows else c // tc,),
            in_specs=[pl.BlockSpec((N_CHIP, tr, tc), lambda i, chip_ref: (0, *at(i))),
                      pl.BlockSpec((None, tr, tc), lambda i, chip_ref: (chip_ref[0], *at(i))), blk, blk, blk]
            + [ANY] * n_after,
            out_specs=[blk] * 4),
        out_shape=[out] * 4,
        compiler_params=_params("parallel"),
    )(chip, parts, sums, w, m, v, *after)


def _adam_gains(total, ws, ms, vs):
    n = len(ws)
    widths = [w.shape[1] for w in ws]

    def body(t_ref, *refs):
        w_refs, m_refs, v_refs, outs = refs[:n], refs[n:2 * n], refs[2 * n:3 * n], refs[3 * n:]
        off = 0
        for i in range(n):
            g = t_ref[:, off:off + widths[i]]
            off += widths[i]
            g_ref, d_ref, mo_ref, vo_ref = outs[4 * i:4 * i + 4]
            g_ref[...] = g
            d_ref[...], mo_ref[...], vo_ref[...] = _adamw(w_refs[i][...], g, m_refs[i][...], v_refs[i][...])

    out = pl.pallas_call(
        body, name="adam_gains",
        out_shape=[jax.ShapeDtypeStruct(w.shape, F32) for w in ws for _ in range(4)],
    )(total, *ws, *ms, *vs)
    return [tuple(out[4 * i:4 * i + 4]) for i in range(n)]


def _adam_taps(total, first_col, device, w, m, v):
    _, n_taps, cw = w.shape
    col_block = lambda t, dev: (0, first_col // cw + t * N_DEV + dev[0])
    tap = pl.BlockSpec((None, 1, cw), lambda t, dev: (t, 0, 0))

    def body(dev_ref, t_ref, w_ref, m_ref, v_ref, g_ref, d_ref, mo_ref, vo_ref):
        g = t_ref[...]
        g_ref[...] = g
        d_ref[...], mo_ref[...], vo_ref[...] = _adamw(w_ref[...], g, m_ref[...], v_ref[...])

    shape3 = (n_taps, 1, cw)
    out = pl.pallas_call(
        body, name="adam_taps",
        grid_spec=pltpu.PrefetchScalarGridSpec(
            num_scalar_prefetch=1, grid=(n_taps,),
            in_specs=[pl.BlockSpec((1, cw), col_block), tap, tap, tap], out_specs=[tap] * 4),
        out_shape=[jax.ShapeDtypeStruct(shape3, F32)] * 4,
    )(device, total, w.reshape(shape3), m.reshape(shape3), v.reshape(shape3))
    return tuple(o.reshape(w.shape) for o in out)


def kernel(x, pre_mix_g, w_in, conv_w, q_norm_g, w_uq, kv_norm_g, w_ukv, conv_out_g, attn_out_g, w_o, post_mix_g, pre_mlp_g, w_up, w_down, post_mlp_g, loss_target, m_pre_mix_g, m_w_in, m_conv_w, m_q_norm_g, m_w_uq, m_kv_norm_g, m_w_ukv, m_conv_out_g, m_attn_out_g, m_w_o, m_post_mix_g, m_pre_mlp_g, m_w_up, m_w_down, m_post_mlp_g, v_pre_mix_g, v_w_in, v_conv_w, v_q_norm_g, v_w_uq, v_kv_norm_g, v_w_ukv, v_conv_out_g, v_attn_out_g, v_w_o, v_post_mix_g, v_pre_mlp_g, v_w_up, v_w_down, v_post_mlp_g):
    me = 4 * lax.axis_index("x") + 2 * lax.axis_index("y") + lax.axis_index("c")
    core = lax.axis_index("c").astype(jnp.int32).reshape(1)
    chip = (2 * lax.axis_index("x") + lax.axis_index("y")).astype(jnp.int32).reshape(1)
    gains = (pre_mix_g, q_norm_g, kv_norm_g, conv_out_g, attn_out_g, post_mix_g, pre_mlp_g, post_mlp_g)
    gain_m = (m_pre_mix_g, m_q_norm_g, m_kv_norm_g, m_conv_out_g, m_attn_out_g, m_post_mix_g, m_pre_mlp_g, m_post_mlp_g)
    gain_v = (v_pre_mix_g, v_q_norm_g, v_kv_norm_g, v_conv_out_g, v_attn_out_g, v_post_mix_g, v_pre_mlp_g, v_post_mlp_g)
    names = ("w_in", "w_uq", "w_ukv", "w_o", "w_up", "w_down")
    big = dict(zip(names, (w_in, w_uq, w_ukv, w_o, w_up, w_down)))
    big_m = dict(zip(names, (m_w_in, m_w_uq, m_w_ukv, m_w_o, m_w_up, m_w_down)))
    big_v = dict(zip(names, (v_w_in, v_w_uq, v_w_ukv, v_w_o, v_w_up, v_w_down)))
    n_heads = attn_out_g.shape[1] // HEAD
    n_taps = conv_w.shape[1]

    gathered = ("w_in", "conv", "w_uq", "w_ukv", "w_o", "w_up", "w_down")
    gather_groups = ((0, 1), (2, 3, 4), (5,), (6,))
    taps = jnp.pad(conv_w[0], ((0, SUBLANE - n_taps), (0, 0)))
    relayed_groups = (0, 2, 3)
    sems1, shards, lands, token = _gather_start("gather_start_first", [w_in[0].astype(BF16), taps], ((0, 1),), relayed=(0,))
    sems1, shards, lands = list(sems1), list(shards), list(lands)
    behind = token[0, 0]
    rest = [(big[nm][0] + behind).astype(BF16) for nm in gathered[2:]]

    def start_rest(after):
        sems_b, shards_b, lands_b, started = _gather_start("gather_start_rest", rest, ((0, 1, 2), (3,), (4,)), relayed=(1, 2),
                                                          after=after)
        sems1.extend(sems_b)
        shards.extend(shards_b)
        lands.extend(lands_b)
        return started

    cols = lambda a: jnp.concatenate([a[j] for j in range(N_DEV)], axis=1)
    rows = lambda a: a.reshape(N_DEV * a.shape[1], a.shape[2])
    device = me.astype(jnp.int32).reshape(1)
    own_in = lambda a, shard: lax.dynamic_update_index_in_dim(a, shard, me, 0)
    q_pieces = [(h, 0, HEAD) for h in range(n_heads)] + [(h, HEAD, QK) for h in range(n_heads)]
    ready = {
        "w_in": lambda a, shard: _join_col_shards("join_w_in", a, shard, device),
        "conv": lambda a, shard: cols(own_in(a, shard))[:n_taps],
        "w_uq": lambda a, shard: _join_col_shards("join_w_uq", a, shard, device, q_pieces),
        "w_ukv": lambda a, shard: cols(own_in(a, shard)),
        "w_o": lambda a, shard: rows(own_in(a, shard)),
        "w_up": own_in,
        "w_down": lambda a, shard: rows(own_in(a, shard)),
    }
    assert w_uq.shape[2] == QK

    class Weights:
        def __init__(self):
            self.passed, self.relayed = {}, {}

        def forward(self, group, after):
            idx = gather_groups[group]
            if group == 0:
                after = (*after, *rest)
            self.passed[group] = _gather_forward(f"gather_forward_{group}", [shards[i] for i in idx], [lands[i] for i in idx],
                                                 *sems1[group], after, relayed=group in relayed_groups)
            return tuple(self.passed[group][1])

        def relay(self, group, after):
            sems2, mid = self.passed[group]
            self.relayed[group], mid = _gather_relay_forward(f"gather_relay_{group}", mid, *sems2, after)
            self.passed[group] = (sems2, mid)
            if group == 0:
                start_rest(tuple(mid))
            return tuple(mid)

        def ready(self, group, after):
            sems2, mid = self.passed[group]
            full = _gather_wait(f"gather_wait_{group}", mid, *sems2, after, relay_sems=self.relayed.get(group))
            out = []
            return [ready[gathered[i]](a, shards[i]) for i, a in zip(gather_groups[group], full)]

    weights = Weights()

    col_blocks = lambda g: g.reshape(g.shape[0], N_DEV, g.shape[1] // N_DEV).transpose(1, 0, 2)
    row_blocks = lambda g: g.reshape(N_DEV, g.shape[0] // N_DEV, g.shape[1])
    grad_groups = (("w_down",), ("w_up",), ("w_o",), ("w_uq", "w_ukv"), ("w_in",))
    transposed = {"w_in": w_in.shape[2], "w_uq": w_uq.shape[2]}
    to_blocks = {
        "w_in": lambda g: g, "w_uq": lambda g: _unpermute_q_rows(g, n_heads),
        "w_ukv": col_blocks, "w_o": row_blocks, "w_up": lambda g: g, "w_down": row_blocks,
    }
    in_flight = []

    class Grads:
        def __init__(self):
            self.core = core
            self.away = {}

        def send_sums(self, group, sums):
            sems, sums, parts, tok = _chip_send_start(f"chip_send_start_{group}", list(sums))
            in_flight.append((sems, sums, parts))
            return (tok,)

        def full(self, group, arrays, received=None):
            nms = grad_groups[group]
            if received is None:
                blocks = [to_blocks[nm](g) for nm, g in zip(nms, arrays)]
                got = _pair_exchange(f"pair_exchange_{group}", blocks, [transposed.get(nm) for nm in nms])
            else:
                blocks, got = [self.away[group][1]], [self.received(group, received)]
            sums = [(_pair_sum_rows if nm in transposed else _pair_sum)(f"pair_sum_{nm}", g, r, core)
                    for nm, g, r in zip(nms, blocks, got)]
            return self.send_sums(group, sums)

        def send_away(self, group, half):
            nm = grad_groups[group][0]
            rows = transposed.get(nm)
            sems, src, land, tok = _pair_send_start(f"pair_send_start_{group}", half if rows is None else to_blocks[nm](half), rows)
            self.away[group] = (sems, src, land, rows)
            return (tok,)

        def received(self, group, after):
            sems, src, land, rows = self.away[group]
            return _pair_send_wait(f"pair_send_wait_{group}", sems, src, land, after, rows)

    grad_x, small = _local_step(x[0], loss_target[0], gains, weights, Grads(), first_after=(token,))

    big_out = {}

    def update(tag, first, last, after):
        groups = in_flight[first:last]
        parts = _chip_send_wait("chip_send_wait_" + tag, groups, after)
        nms = [nm for grp in grad_groups[first:last] for nm in grp]
        sums = [a for _, s, _ in groups for a in s]
        for nm, p, s in zip(nms, parts, sums):
            view = (lambda a: jnp.swapaxes(a, 1, 2)) if nm in transposed else (lambda a: a)
            out = _sum_adam("adam_" + nm, p, s, chip, view(big[nm]), view(big_m[nm]), view(big_v[nm]), after=after)
            after = (out[0],)
            big_out[nm] = [view(o) for o in out]
        return after

    after = update("early", 0, len(in_flight) - 1, (grad_x,))
    total = _small_all_reduce(small, after=after)
    update("late", len(in_flight) - 1, len(in_flight), (total,))
    big_out = [big_out[nm] for nm in names]

    gain_out = _adam_gains(total, gains, gain_m, gain_v)
    taps_out = _adam_taps(total, sum(g.shape[1] for g in gains), me.astype(jnp.int32).reshape(1), conv_w, m_conv_w, v_conv_w)
    loss = total[0, total.shape[1] - 1]

    order = (0, "w_in", "conv", 1, "w_uq", 2, "w_ukv", 3, 4, "w_o", 5, 6, "w_up", "w_down", 7)
    by_name = dict(zip(names, big_out))
    outs = [loss, grad_x[None]]
    for kind in range(4):
        for item in order:
            if item == "conv":
                outs.append(taps_out[kind])
            elif isinstance(item, int):
                outs.append(gain_out[item][kind])
            else:
                outs.append(by_name[item][kind])
    return tuple(outs)
```

```python
import math

import jax
import jax.numpy as jnp
from jax import lax
from jax.experimental import pallas as pl
from jax.experimental.pallas import tpu as pltpu

F32 = jnp.float32
BF16 = jnp.bfloat16

EPS = 1e-6
NEG_INF = -1e30
HEAD = 128
ROPE = 64
QK = HEAD + ROPE
CHUNK = 64
ROPE_THETA = 10000.0
ADAM_LR, ADAM_B1, ADAM_B2, ADAM_EPS, ADAM_WD, ADAM_STEP = 0.001, 0.9, 0.999, 1e-08, 0.01, 10

LANE = 128
SUBLANE = 8
VMEM_LIMIT_BYTES = 56 * 1024 * 1024

N_DEV = 8
N_CHIP = 4
MESH = pl.DeviceIdType.MESH


def _params(*sem):
    return pltpu.CompilerParams(dimension_semantics=sem, vmem_limit_bytes=VMEM_LIMIT_BYTES)


ANY = pl.BlockSpec(memory_space=pl.ANY)


def _call(body, *, in_specs, after=(), **kw):
    n_in, n_after = len(in_specs), len(after)

    def ordered(*refs):
        body(*refs[:n_in], *refs[n_in + n_after:])

    call = pl.pallas_call(ordered, in_specs=[*in_specs, *[ANY] * n_after], **kw)
    return lambda *operands: call(*operands, *after)


def _sublane_sum(v):
    r, w = v.shape
    return jnp.sum(v.reshape(r // SUBLANE, SUBLANE, w), axis=0)


def _rstd(x):
    return lax.rsqrt(jnp.mean(x * x, axis=-1, keepdims=True) + EPS)


def _rms_bwd(x, g, dy):
    r = _rstd(x)
    xh = x * r
    dxh = dy * g
    dx = r * (dxh - xh * jnp.mean(dxh * xh, axis=-1, keepdims=True))
    return dx, dy * xh


def _accumulate(ref, val, step):
    @pl.when(step == 0)
    def _():
        ref[...] = val

    @pl.when(step > 0)
    def _():
        ref[...] += val


NN = ((1,), (0,))
NT = ((1,), (1,))
TN = ((0,), (0,))


def _matmul(name, a, b, *, grid, a_spec, b_spec, out_shape, out_specs, contract, nk=1, acc_shape=None,
            extras=(), extra_specs=(), epilogue=None, after=()):
    multi = isinstance(out_shape, (tuple, list))
    out_shapes = tuple(out_shape) if multi else (out_shape,)
    n_out = len(out_shapes)
    n_extra = len(extras)

    def body(a_ref, b_ref, *rest):
        x_refs = rest[:n_extra]
        o_refs = rest[n_extra:n_extra + n_out]

        def emit(acc):
            vals = epilogue(acc, *[r[...] for r in x_refs]) if epilogue else (acc,)
            for r, v in zip(o_refs, vals):
                r[...] = v.astype(r.dtype)

        p = lax.dot_general(a_ref[...], b_ref[...], (contract, ((), ())), preferred_element_type=F32)
        if nk == 1:
            emit(p)
        else:
            acc_ref = rest[n_extra + n_out]
            k = pl.program_id(2)
            _accumulate(acc_ref, p, k)

            @pl.when(k == nk - 1)
            def _():
                emit(acc_ref[...])

    sem = ("parallel", "parallel") + (("arbitrary",) if nk > 1 else ())
    return _call(
        body, name=name, grid=grid, after=after,
        in_specs=[a_spec, b_spec, *extra_specs],
        out_specs=out_specs,
        out_shape=out_shape,
        scratch_shapes=[pltpu.VMEM(acc_shape, F32)] if nk > 1 else [],
        compiler_params=_params(*sem),
    )(a, b, *extras)


def _fit(n, tile):
    if n <= tile:
        return n
    t = tile - tile % LANE
    while n % t:
        t -= LANE
    return t


def _mm_nn(name, a, b, out_dtype, tm, tn, after=()):
    m, k = a.shape
    n = b.shape[1]
    tm, tn = _fit(m, tm), _fit(n, tn)
    return _matmul(name, a, b, grid=(m // tm, n // tn), after=after,
                   a_spec=pl.BlockSpec((tm, k), lambda i, j: (i, 0)),
                   b_spec=pl.BlockSpec((k, tn), lambda i, j: (0, j)),
                   out_shape=jax.ShapeDtypeStruct((m, n), out_dtype),
                   out_specs=pl.BlockSpec((tm, tn), lambda i, j: (i, j)), contract=NN)


def _mm_nt(name, a, b, out_dtype, tm, tn, after=()):
    m, k = a.shape
    n = b.shape[0]
    tm, tn = _fit(m, tm), _fit(n, tn)
    return _matmul(name, a, b, grid=(m // tm, n // tn), after=after,
                   a_spec=pl.BlockSpec((tm, k), lambda i, j: (i, 0)),
                   b_spec=pl.BlockSpec((tn, k), lambda i, j: (j, 0)),
                   out_shape=jax.ShapeDtypeStruct((m, n), out_dtype),
                   out_specs=pl.BlockSpec((tm, tn), lambda i, j: (i, j)), contract=NT)


def _mm_tn(name, a, b, out_dtype, tm, tn):
    s, m = a.shape
    n = b.shape[1]
    tm, tn = _fit(m, tm), _fit(n, tn)
    return _matmul(name, a, b, grid=(m // tm, n // tn),
                   a_spec=pl.BlockSpec((s, tm), lambda i, j: (0, i)),
                   b_spec=pl.BlockSpec((s, tn), lambda i, j: (0, j)),
                   out_shape=jax.ShapeDtypeStruct((m, n), out_dtype),
                   out_specs=pl.BlockSpec((tm, tn), lambda i, j: (i, j)), contract=TN)


ROWS = 256


def _row_spec(rows, width):
    return pl.BlockSpec((rows, width), lambda i: (i, 0))


def _fixed_spec(rows, width):
    return pl.BlockSpec((rows, width), lambda i: (0, 0))


def _column_pieces(rows, start, width):
    piece = math.gcd(start, width)
    assert piece % LANE == 0
    return [pl.BlockSpec((rows, piece), lambda i, b=start // piece + p: (i, b)) for p in range(width // piece)]


def _rms_fwd(name, x, g, cols=None, after=()):
    s = x.shape[0]
    start, w = cols or (0, x.shape[1])
    rows = min(ROWS, s)
    pieces = _column_pieces(rows, start, w) if cols else [_row_spec(rows, w)]
    n = len(pieces)

    def body(*refs):
        g_ref, o_ref = refs[n:]
        xv = refs[0][...] if n == 1 else jnp.concatenate([r[...] for r in refs[:n]], axis=1)
        o_ref[...] = (xv * _rstd(xv) * g_ref[...]).astype(o_ref.dtype)

    return _call(
        body, name=name, grid=(s // rows,), after=after,
        in_specs=[*pieces, _fixed_spec(1, w)],
        out_specs=_row_spec(rows, w),
        out_shape=jax.ShapeDtypeStruct((s, w), BF16),
        compiler_params=_params("parallel"),
    )(*[x] * n, g)


def _rms_bwd_call(name, x, g, dy, out_dtype, cols=None, after=()):
    s = x.shape[0]
    start, w = cols or (0, x.shape[1])
    rows = min(ROWS, s)
    pieces = _column_pieces(rows, start, w) if cols else [_row_spec(rows, w)]
    n = len(pieces)

    def body(*refs):
        g_ref, dy_ref, dx_ref, dg_ref = refs[n:]
        xv = refs[0][...] if n == 1 else jnp.concatenate([r[...] for r in refs[:n]], axis=1)
        dx, dgc = _rms_bwd(xv, g_ref[...], dy_ref[...].astype(F32))
        dx_ref[...] = dx.astype(dx_ref.dtype)
        _accumulate(dg_ref, _sublane_sum(dgc), pl.program_id(0))

    return _call(
        body, name=name, grid=(s // rows,), after=after,
        in_specs=[*pieces, _fixed_spec(1, w), _row_spec(rows, w)],
        out_specs=[_row_spec(rows, w), _fixed_spec(SUBLANE, w)],
        out_shape=[jax.ShapeDtypeStruct((s, w), out_dtype), jax.ShapeDtypeStruct((SUBLANE, w), F32)],
        compiler_params=_params("arbitrary"),
    )(*[x] * n, g, dy)


def _norm_up(name, x, cols, g, w, after=()):
    s = x.shape[0]
    start, width = cols
    n = w.shape[1]
    tm = min(TILE_M, s)
    pieces = _column_pieces(tm, start, width)
    n_p = len(pieces)

    def body(*refs):
        g_ref, w_ref, xn_ref, o_ref = refs[n_p:]
        xv = refs[0][...] if n_p == 1 else jnp.concatenate([r[...] for r in refs[:n_p]], axis=1)
        xn = (xv * _rstd(xv) * g_ref[...]).astype(BF16)
        xn_ref[...] = xn
        o_ref[...] = jnp.dot(xn, w_ref[...], preferred_element_type=F32)

    return _call(
        body, name=name, grid=(s // tm,), after=after,
        in_specs=[*pieces, _fixed_spec(1, width), _fixed_spec(width, n)],
        out_specs=[_row_spec(tm, width), _row_spec(tm, n)],
        out_shape=[jax.ShapeDtypeStruct((s, width), BF16), jax.ShapeDtypeStruct((s, n), F32)],
        compiler_params=_params("parallel"),
    )(*[x] * n_p, g, w)


def _up_norm_bwd(name, dy, w, x, cols, g, after=()):
    s, n = dy.shape
    start, width = cols
    tm = min(TILE_M, s)
    pieces = _column_pieces(tm, start, width)
    n_p = len(pieces)

    def body(dy_ref, w_ref, *refs):
        g_ref, dx_ref, dg_ref = refs[n_p:]
        xv = refs[0][...] if n_p == 1 else jnp.concatenate([r[...] for r in refs[:n_p]], axis=1)
        dxn = lax.dot_general(dy_ref[...], w_ref[...], (NT, ((), ())), preferred_element_type=F32)
        dx, dgc = _rms_bwd(xv, g_ref[...], dxn)
        dx_ref[...] = dx.astype(dx_ref.dtype)
        _accumulate(dg_ref, _sublane_sum(dgc), pl.program_id(0))

    return _call(
        body, name=name, grid=(s // tm,), after=after,
        in_specs=[_row_spec(tm, n), _fixed_spec(width, n), *pieces, _fixed_spec(1, width)],
        out_specs=[_row_spec(tm, width), _fixed_spec(SUBLANE, width)],
        out_shape=[jax.ShapeDtypeStruct((s, width), BF16), jax.ShapeDtypeStruct((SUBLANE, width), F32)],
        compiler_params=_params("arbitrary"),
    )(dy, w, *[x] * n_p, g)


def _mid_fwd(x, y, g_post, g_pre, after=()):
    s, w = x.shape
    rows = min(ROWS, s)

    def body(x_ref, y_ref, gp_ref, gq_ref, x2_ref, h2_ref):
        yv = y_ref[...]
        x2 = x_ref[...] + yv * _rstd(yv) * gp_ref[...]
        x2_ref[...] = x2
        h2_ref[...] = (x2 * _rstd(x2) * gq_ref[...]).astype(h2_ref.dtype)

    return _call(
        body, name="mid_fwd", grid=(s // rows,), after=after,
        in_specs=[_row_spec(rows, w), _row_spec(rows, w), _fixed_spec(1, w), _fixed_spec(1, w)],
        out_specs=[_row_spec(rows, w), _row_spec(rows, w)],
        out_shape=[jax.ShapeDtypeStruct((s, w), F32), jax.ShapeDtypeStruct((s, w), BF16)],
        compiler_params=_params("parallel"),
    )(x, y, g_post, g_pre)


def _head(m, x2, tgt, g):
    s, w = m.shape
    rows = min(ROWS, s)

    def body(m_ref, x2_ref, t_ref, g_ref, dout_ref, dm_ref, dg_ref, loss_ref):
        mv = m_ref[...]
        gv = g_ref[...]
        out = x2_ref[...] + mv * _rstd(mv) * gv
        err = out - t_ref[...]
        dout = err * (1.0 / w)
        dout_ref[...] = dout
        dm, dgc = _rms_bwd(mv, gv, dout)
        dm_ref[...] = dm.astype(dm_ref.dtype)
        sq = err * err
        lanes = sq[:, 0:LANE]
        for j in range(1, w // LANE):
            lanes = lanes + sq[:, j * LANE:(j + 1) * LANE]
        step = pl.program_id(0)
        _accumulate(dg_ref, _sublane_sum(dgc), step)
        _accumulate(loss_ref, _sublane_sum(lanes) * (0.5 / w), step)

    return pl.pallas_call(
        body, name="head", grid=(s // rows,),
        in_specs=[_row_spec(rows, w), _row_spec(rows, w), _row_spec(rows, w), _fixed_spec(1, w)],
        out_specs=[_row_spec(rows, w), _row_spec(rows, w), _fixed_spec(SUBLANE, w), _fixed_spec(SUBLANE, LANE)],
        out_shape=[jax.ShapeDtypeStruct((s, w), F32), jax.ShapeDtypeStruct((s, w), BF16),
                   jax.ShapeDtypeStruct((SUBLANE, w), F32), jax.ShapeDtypeStruct((SUBLANE, LANE), F32)],
        compiler_params=_params("arbitrary"),
    )(m, x2, tgt, g)


def _mid_bwd(x2, y, d_out, d_h2, g_pre, g_post, after=()):
    s, w = x2.shape
    rows = min(ROWS, s)

    def body(x2_ref, y_ref, dout_ref, dh2_ref, gq_ref, gp_ref, dx2_ref, dy_ref, dgq_ref, dgp_ref):
        dx, dgq = _rms_bwd(x2_ref[...], gq_ref[...], dh2_ref[...])
        dx2 = dout_ref[...] + dx
        dx2_ref[...] = dx2
        dy, dgp = _rms_bwd(y_ref[...], gp_ref[...], dx2)
        dy_ref[...] = dy.astype(dy_ref.dtype)
        step = pl.program_id(0)
        _accumulate(dgq_ref, _sublane_sum(dgq), step)
        _accumulate(dgp_ref, _sublane_sum(dgp), step)

    return _call(
        body, name="mid_bwd", grid=(s // rows,), after=after,
        in_specs=[_row_spec(rows, w)] * 4 + [_fixed_spec(1, w)] * 2,
        out_specs=[_row_spec(rows, w), _row_spec(rows, w), _fixed_spec(SUBLANE, w), _fixed_spec(SUBLANE, w)],
        out_shape=[jax.ShapeDtypeStruct((s, w), F32), jax.ShapeDtypeStruct((s, w), BF16),
                   jax.ShapeDtypeStruct((SUBLANE, w), F32), jax.ShapeDtypeStruct((SUBLANE, w), F32)],
        compiler_params=_params("arbitrary"),
    )(x2, y, d_out, d_h2, g_pre, g_post)


def _first_bwd(x, g, d_h1, d_x2, after=()):
    s, w = x.shape
    rows = min(ROWS, s)

    def body(x_ref, g_ref, dh_ref, dx2_ref, dx_ref, dg_ref):
        dx, dgc = _rms_bwd(x_ref[...], g_ref[...], dh_ref[...])
        dx_ref[...] = dx2_ref[...] + dx
        _accumulate(dg_ref, _sublane_sum(dgc), pl.program_id(0))

    return _call(
        body, name="first_bwd", grid=(s // rows,), after=after,
        in_specs=[_row_spec(rows, w), _fixed_spec(1, w), _row_spec(rows, w), _row_spec(rows, w)],
        out_specs=[_row_spec(rows, w), _fixed_spec(SUBLANE, w)],
        out_shape=[jax.ShapeDtypeStruct((s, w), F32), jax.ShapeDtypeStruct((SUBLANE, w), F32)],
        compiler_params=_params("arbitrary"),
    )(x, g, d_h1, d_x2)


def _shift_down(v, k):
    t = lax.broadcasted_iota(jnp.int32, v.shape, 0)
    return jnp.where(t >= k, pltpu.roll(v, k, 0), 0.0)


def _shift_up(v, k):
    n = v.shape[0]
    t = lax.broadcasted_iota(jnp.int32, v.shape, 0)
    return jnp.where(t < n - k, pltpu.roll(v, n - k, 0), 0.0)


def _conv_core(u, b, c, w):
    z = c * u
    conv = w[0:1, :] * _shift_down(z, 2) + w[1:2, :] * _shift_down(z, 1) + w[2:3, :] * z
    return z, conv, b * conv


def _conv_fwd(proj, conv_w, g, n_groups, out_width, after=()):
    s = proj.shape[0]

    def body(u_ref, b_ref, c_ref, w_ref, g_ref, o_ref):
        _, _, yr = _conv_core(u_ref[...], b_ref[...], c_ref[...], w_ref[...])
        o_ref[...] = (yr * _rstd(yr) * g_ref[...]).astype(o_ref.dtype)

    col = lambda k: pl.BlockSpec((s, HEAD), lambda i: (0, k * n_groups + i))
    return _call(
        body, name="conv_fwd", grid=(n_groups,), after=after,
        in_specs=[col(0), col(1), col(2), pl.BlockSpec((3, HEAD), lambda i: (0, i)), pl.BlockSpec((1, HEAD), lambda i: (0, i))],
        out_specs=pl.BlockSpec((s, HEAD), lambda i: (0, i)),
        out_shape=jax.ShapeDtypeStruct((s, out_width), BF16),
        compiler_params=_params("parallel"),
    )(proj, proj, proj, conv_w, g)


def _conv_bwd(proj, d_mix, conv_w, g, n_groups):
    s = proj.shape[0]
    width = n_groups * HEAD

    def body(u_ref, b_ref, c_ref, dy_ref, w_ref, g_ref, du_ref, db_ref, dc_ref, dg_ref, dw_ref):
        u, b, c, w = u_ref[...], b_ref[...], c_ref[...], w_ref[...]
        z, conv, yr = _conv_core(u, b, c, w)
        dyr, dgc = _rms_bwd(yr, g_ref[...], dy_ref[...])
        dconv = dyr * b
        db_ref[...] = (dyr * conv).astype(db_ref.dtype)
        dz = w[2:3, :] * dconv + w[1:2, :] * _shift_up(dconv, 1) + w[0:1, :] * _shift_up(dconv, 2)
        dc_ref[...] = (dz * u).astype(dc_ref.dtype)
        du_ref[...] = (dz * c).astype(du_ref.dtype)
        dg_ref[...] = _sublane_sum(dgc)
        dw_ref[0] = _sublane_sum(dconv * _shift_down(z, 2))
        dw_ref[1] = _sublane_sum(dconv * _shift_down(z, 1))
        dw_ref[2] = _sublane_sum(dconv * z)

    col = lambda k: pl.BlockSpec((s, HEAD), lambda i: (0, k * n_groups + i))
    grp = pl.BlockSpec((s, HEAD), lambda i: (0, i))
    return pl.pallas_call(
        body, name="conv_bwd", grid=(n_groups,),
        in_specs=[col(0), col(1), col(2), grp, pl.BlockSpec((3, HEAD), lambda i: (0, i)), pl.BlockSpec((1, HEAD), lambda i: (0, i))],
        out_specs=[grp, grp, grp, pl.BlockSpec((SUBLANE, HEAD), lambda i: (0, i)),
                   pl.BlockSpec((3, SUBLANE, HEAD), lambda i: (0, 0, i))],
        out_shape=[jax.ShapeDtypeStruct((s, width), BF16)] * 3
        + [jax.ShapeDtypeStruct((SUBLANE, width), F32), jax.ShapeDtypeStruct((3, SUBLANE, width), F32)],
        compiler_params=_params("parallel"),
    )(proj, proj, proj, d_mix, conv_w, g)


def _rope_tables(s, n_heads):
    pos = jnp.arange(s, dtype=F32)
    inv_freq = jnp.power(ROPE_THETA, -jnp.arange(0, ROPE, 2, dtype=F32) / ROPE)
    ang = pos[:, None] * inv_freq[None, :]
    cos, sin = jnp.cos(ang), jnp.sin(ang)
    cs = jnp.concatenate([cos, cos], axis=1)
    sn = jnp.concatenate([-sin, sin], axis=1)
    pad = jnp.zeros((s, LANE - ROPE), F32)
    return (jnp.tile(cs, (1, n_heads)), jnp.tile(sn, (1, n_heads)),
            jnp.concatenate([cs, pad], axis=1), jnp.concatenate([sn, pad], axis=1))


def _swap_halves(v):
    w = v.shape[1]
    lane = lax.broadcasted_iota(jnp.int32, v.shape, 1)
    first = (lane % ROPE) < (ROPE // 2)
    return jnp.where(first, pltpu.roll(v, w - ROPE // 2, 1), pltpu.roll(v, ROPE // 2, 1))


def _pack_heads(q, kv, proj, kr_col, tables, n_heads, after=()):
    s = q.shape[0]
    rows = min(ROWS, s)
    cq, sq, ck, sk = tables
    wq = n_heads * ROPE

    def body(q_ref, kv_ref, kr_ref, cq_ref, sq_ref, ck_ref, sk_ref, qo_ref, ko_ref, vo_ref):
        qr = q_ref[:, n_heads * HEAD:]
        qr = qr * cq_ref[...] + _swap_halves(qr) * sq_ref[...]
        krv = kr_ref[...]
        krv = krv * ck_ref[...] + _swap_halves(krv) * sk_ref[...]
        for h in range(n_heads):
            qo_ref[h] = jnp.concatenate([q_ref[:, h * HEAD:(h + 1) * HEAD], qr[:, h * ROPE:(h + 1) * ROPE]], axis=1).astype(BF16)
            ko_ref[h] = jnp.concatenate([kv_ref[:, 2 * h * HEAD:(2 * h + 1) * HEAD], krv[:, :ROPE]], axis=1).astype(BF16)
            vo_ref[h] = kv_ref[:, (2 * h + 1) * HEAD:(2 * h + 2) * HEAD].astype(BF16)

    hs = lambda w: pl.BlockSpec((n_heads, rows, w), lambda i: (0, i, 0))
    return _call(
        body, name="pack_heads", grid=(s // rows,), after=after,
        in_specs=[_row_spec(rows, q.shape[1]), _row_spec(rows, kv.shape[1]), pl.BlockSpec((rows, LANE), lambda i: (i, kr_col // LANE)),
                  _row_spec(rows, wq), _row_spec(rows, wq), _row_spec(rows, LANE), _row_spec(rows, LANE)],
        out_specs=[hs(QK), hs(QK), hs(HEAD)],
        out_shape=[jax.ShapeDtypeStruct((n_heads, s, QK), BF16), jax.ShapeDtypeStruct((n_heads, s, QK), BF16),
                   jax.ShapeDtypeStruct((n_heads, s, HEAD), BF16)],
        compiler_params=_params("parallel"),
    )(q, kv, proj, cq, sq, ck, sk)


def _unpack_heads(dq, dk, dv, tables, n_heads):
    s = dq.shape[1]
    rows = min(ROWS, s)
    cq, sq, ck, sk = tables
    wq = n_heads * ROPE

    def body(dq_ref, dk_ref, dv_ref, cq_ref, sq_ref, ck_ref, sk_ref, qo_ref, kvo_ref, kro_ref):
        dqr = jnp.concatenate([dq_ref[h][:, HEAD:] for h in range(n_heads)], axis=1)
        dqr = dqr * cq_ref[...] - _swap_halves(dqr) * sq_ref[...]
        dkr = dk_ref[0][:, HEAD:]
        for h in range(1, n_heads):
            dkr = dkr + dk_ref[h][:, HEAD:]
        dkr = jnp.concatenate([dkr, jnp.zeros((rows, LANE - ROPE), F32)], axis=1)
        dkr = dkr * ck_ref[...] - _swap_halves(dkr) * sk_ref[...]
        kro_ref[...] = dkr.astype(kro_ref.dtype)
        qo_ref[:, n_heads * HEAD:] = dqr.astype(qo_ref.dtype)
        for h in range(n_heads):
            qo_ref[:, h * HEAD:(h + 1) * HEAD] = dq_ref[h][:, :HEAD].astype(qo_ref.dtype)
            kvo_ref[:, 2 * h * HEAD:(2 * h + 1) * HEAD] = dk_ref[h][:, :HEAD].astype(kvo_ref.dtype)
            kvo_ref[:, (2 * h + 1) * HEAD:(2 * h + 2) * HEAD] = dv_ref[h].astype(kvo_ref.dtype)

    hs = lambda w: pl.BlockSpec((n_heads, rows, w), lambda i: (0, i, 0))
    return pl.pallas_call(
        body, name="unpack_heads", grid=(s // rows,),
        in_specs=[hs(QK), hs(QK), hs(HEAD), _row_spec(rows, wq), _row_spec(rows, wq), _row_spec(rows, LANE), _row_spec(rows, LANE)],
        out_specs=[_row_spec(rows, n_heads * QK), _row_spec(rows, 2 * n_heads * HEAD), _row_spec(rows, LANE)],
        out_shape=[jax.ShapeDtypeStruct((s, n_heads * QK), BF16), jax.ShapeDtypeStruct((s, 2 * n_heads * HEAD), BF16),
                   jax.ShapeDtypeStruct((s, LANE), BF16)],
        compiler_params=_params("parallel"),
    )(dq, dk, dv, cq, sq, ck, sk)


TQ = 256


LOG2_E = 1.4426950408889634


def _softmax_parts(q, k):
    tq, n_keys = q.shape[0], k.shape[0]
    sc = lax.dot_general(q, k, (NT, ((), ())), preferred_element_type=F32) * (QK ** -0.5 * LOG2_E)
    row = lax.broadcasted_iota(jnp.int32, (tq, tq), 0)
    col = lax.broadcasted_iota(jnp.int32, (tq, tq), 1)
    own = jnp.where(col // CHUNK <= row // CHUNK, sc[:, n_keys - tq:], NEG_INF)
    sc = own if n_keys == tq else jnp.concatenate([sc[:, :n_keys - tq], own], axis=1)
    e = jnp.exp2(sc - jnp.max(sc, axis=-1, keepdims=True))
    return e, 1.0 / jnp.sum(e, axis=-1, keepdims=True)


def _attn_fwd(q, k, v, g, mix, col0):
    n_heads, s, _ = q.shape
    tq = min(TQ, s)
    assert tq % CHUNK == 0 and s % tq == 0

    def body(q_ref, k_ref, v_ref, g_ref, mix_ref, o_ref, y_ref):
        for c in range(s // tq):
            rows, n_keys = pl.ds(c * tq, tq), (c + 1) * tq
            e, inv = _softmax_parts(q_ref[rows, :], k_ref[0:n_keys, :])
            o = jnp.dot(e.astype(BF16), v_ref[0:n_keys, :], preferred_element_type=F32) * inv
            o_ref[rows, :] = o
            y_ref[rows, :] = (o * _rstd(o) * g_ref[...]).astype(y_ref.dtype)

    head = lambda w: pl.BlockSpec((None, s, w), lambda h: (h, 0, 0))
    return pl.pallas_call(
        body, name="attn_fwd", grid=(n_heads,),
        in_specs=[head(QK), head(QK), head(HEAD), pl.BlockSpec((1, HEAD), lambda h: (0, h)), ANY],
        out_specs=[head(HEAD), pl.BlockSpec((s, HEAD), lambda h: (0, col0 // HEAD + h))],
        out_shape=[jax.ShapeDtypeStruct((n_heads, s, HEAD), F32), jax.ShapeDtypeStruct(mix.shape, mix.dtype)],
        input_output_aliases={4: 1},
        compiler_params=_params("parallel"),
    )(q, k, v, g, mix)


def _attn_bwd(q, k, v, o, d_mix, g, col0, after=(), side=None):
    n_heads, s, _ = q.shape
    tq = min(TQ, s)

    def body(q_ref, k_ref, v_ref, o_ref, dy_ref, g_ref, dq_ref, dk_ref, dv_ref, dg_ref):
        dg = None
        for c in reversed(range(s // tq)):
            rows, n_keys = pl.ds(c * tq, tq), (c + 1) * tq
            qv, kv_, vv = q_ref[rows, :], k_ref[0:n_keys, :], v_ref[0:n_keys, :]
            do, dgc = _rms_bwd(o_ref[rows, :], g_ref[...], dy_ref[rows, :])
            do = do.astype(BF16)
            dg = _sublane_sum(dgc) if dg is None else dg + _sublane_sum(dgc)
            e, inv = _softmax_parts(qv, kv_)
            p = e * inv
            dp = lax.dot_general(do, vv, (NT, ((), ())), preferred_element_type=F32)
            ds = (p * (dp - jnp.sum(p * dp, axis=-1, keepdims=True)) * (QK ** -0.5)).astype(BF16)
            dq_ref[rows, :] = jnp.dot(ds, kv_, preferred_element_type=F32)
            dk = lax.dot_general(ds, qv, (TN, ((), ())), preferred_element_type=F32)
            dv = lax.dot_general(p.astype(BF16), do, (TN, ((), ())), preferred_element_type=F32)
            if n_keys == s:
                dk_ref[...] = dk
                dv_ref[...] = dv
            else:
                dk_ref[0:n_keys, :] += dk
                dv_ref[0:n_keys, :] += dv
        dg_ref[...] = dg

    c0 = col0 // HEAD
    head = lambda w: pl.BlockSpec((None, s, w), lambda h, *_: (h, 0, 0))
    in_specs = [head(QK), head(QK), head(HEAD), head(HEAD), pl.BlockSpec((s, HEAD), lambda h, *_: (0, c0 + h)),
                pl.BlockSpec((1, HEAD), lambda h, *_: (0, h))]
    out_specs = [head(QK), head(QK), head(HEAD), pl.BlockSpec((SUBLANE, HEAD), lambda h, *_: (0, h))]
    out_shape = [jax.ShapeDtypeStruct((n_heads, s, QK), F32), jax.ShapeDtypeStruct((n_heads, s, QK), F32),
                 jax.ShapeDtypeStruct((n_heads, s, HEAD), F32), jax.ShapeDtypeStruct((SUBLANE, n_heads * HEAD), F32)]
    if side is None:
        return _call(body, name="attn_bwd", grid=(n_heads,), after=after, in_specs=in_specs, out_specs=out_specs,
                     out_shape=out_shape, compiler_params=_params("parallel"))(q, k, v, o, d_mix, g)

    chip, update = side
    side_in, side_out, side_shapes, side_operands = _side_update(update, n_heads)
    n_in, n_after = len(in_specs), len(after)

    def both(chip_ref, *refs):
        ins, rest = refs[:n_in], refs[n_in + n_after:]
        upd_in, outs, upd_out = rest[:5], rest[5:9], rest[9:]
        body(*ins, *outs)
        _sum_adam_block(chip_ref, *upd_in, *upd_out)

    return pl.pallas_call(
        both, name="attn_bwd",
        grid_spec=pltpu.PrefetchScalarGridSpec(
            num_scalar_prefetch=1, grid=(n_heads,),
            in_specs=[*in_specs, *[ANY] * n_after, *side_in], out_specs=[*out_specs, *side_out]),
        out_shape=[*out_shape, *side_shapes],
        compiler_params=_params("parallel"),
    )(chip, q, k, v, o, d_mix, g, *after, *side_operands)


TILE_M = 1024
TILE_N = 1024


def _up_fwd(h2, w_up):
    s, d = h2.shape
    nb, _, fb = w_up.shape
    tm = min(TILE_M,s)

    def epilogue(acc):
        r = jnp.maximum(acc, 0.0)
        return r * r, r

    blk = pl.BlockSpec((tm, fb), lambda i, j: (i, j))
    return _matmul("up_fwd", h2, w_up, grid=(s // tm, nb),
                   a_spec=pl.BlockSpec((tm, d), lambda i, j: (i, 0)),
                   b_spec=pl.BlockSpec((None, d, fb), lambda i, j: (j, 0, 0)),
                   out_shape=[jax.ShapeDtypeStruct((s, nb * fb), BF16)] * 2, out_specs=[blk, blk],
                   contract=NN, epilogue=epilogue)


def _down_fwd(a, w_down):
    s, f = a.shape
    d = w_down.shape[1]
    tm, tn, tk = min(TILE_M,s), min(TILE_N,d), 2048
    nk = f // tk
    return _matmul("down_fwd", a, w_down, grid=(s // tm, d // tn, nk),
                   a_spec=pl.BlockSpec((tm, tk), lambda i, j, k: (i, k)),
                   b_spec=pl.BlockSpec((tk, tn), lambda i, j, k: (k, j)),
                   out_shape=jax.ShapeDtypeStruct((s, d), F32),
                   out_specs=pl.BlockSpec((tm, tn), lambda i, j, k: (i, j)),
                   contract=NN, nk=nk, acc_shape=(tm, tn))


def _down_bwd_act(d_m, w_down, r, after=()):
    s, d = d_m.shape
    f = w_down.shape[0]
    tm, tn = min(TILE_M,s), min(TILE_N,f)
    blk = pl.BlockSpec((tm, tn), lambda i, j: (i, j))
    return _matmul("down_bwd_act", d_m, w_down, grid=(s // tm, f // tn), after=after,
                   a_spec=pl.BlockSpec((tm, d), lambda i, j: (i, 0)),
                   b_spec=pl.BlockSpec((tn, d), lambda i, j: (j, 0)),
                   out_shape=jax.ShapeDtypeStruct((s, f), BF16), out_specs=blk, contract=NT,
                   extras=(r,), extra_specs=(blk,),
                   epilogue=lambda acc, rv: (acc * (2.0 * rv.astype(F32)),))


def _up_bwd_act(d_up, w_up, after=()):
    s, _ = d_up.shape
    nb, d, fb = w_up.shape
    tm, tn = min(TILE_M, s), min(TILE_N,d)
    pair = 2
    n_after = len(after)

    def body(a_ref, w_ref, *rest):
        o_ref, acc_ref = rest[n_after:]
        k = pl.program_id(2)
        p = None
        for t in range(pair):
            term = lax.dot_general(a_ref[:, t * fb:(t + 1) * fb], w_ref[t], (NT, ((), ())), preferred_element_type=F32)
            p = term if p is None else p + term
        _accumulate(acc_ref, p, k)

        @pl.when(k == nb // pair - 1)
        def _():
            o_ref[...] = acc_ref[...]

    return pl.pallas_call(
        body, name="up_bwd_act", grid=(s // tm, d // tn, nb // pair),
        in_specs=[pl.BlockSpec((tm, pair * fb), lambda i, j, k: (i, k)),
                  pl.BlockSpec((pair, tn, fb), lambda i, j, k: (k, j, 0))] + [ANY] * n_after,
        out_specs=pl.BlockSpec((tm, tn), lambda i, j, k: (i, j)),
        out_shape=jax.ShapeDtypeStruct((s, d), F32),
        scratch_shapes=[pltpu.VMEM((tm, tn), F32)],
        compiler_params=_params("parallel", "parallel", "arbitrary"),
    )(d_up, w_up, *after)


def _half_grad(name, a, b, core, home, received, after, *, grid, a_block, a_map, b_block, b_map, o_block, o_map, out_shape):
    n_after = len(after)
    pick = (lambda ref: ref[0]) if home else (lambda ref: 1 - ref[0])

    def body(core_ref, a_ref, b_ref, *rest):
        acc = lax.dot_general(a_ref[...], b_ref[...], (TN, ((), ())), preferred_element_type=F32)
        if received is not None:
            acc = acc + rest[0][...].astype(F32)
        rest[-1][...] = acc.astype(rest[-1].dtype)

    wrap = lambda fn: (lambda i, j, core_ref: fn(i, j, pick(core_ref)))
    o_spec = pl.BlockSpec(o_block, wrap(o_map))
    extra = [] if received is None else [o_spec]
    operands = [] if received is None else [received]
    return pl.pallas_call(
        body, name=name,
        grid_spec=pltpu.PrefetchScalarGridSpec(
            num_scalar_prefetch=1, grid=grid,
            in_specs=[pl.BlockSpec(a_block, wrap(a_map)), pl.BlockSpec(b_block, wrap(b_map))] + extra + [ANY] * n_after,
            out_specs=o_spec),
        out_shape=out_shape,
        compiler_params=_params("parallel", "parallel"),
    )(core, a, b, *operands, *after)


def _down_half_grad(name, a, d_m, core, home, received=None, after=()):
    s, f = a.shape
    d = d_m.shape[1]
    r = f // N_DEV
    tn = min(TILE_N, d)
    return _half_grad(name, a, d_m, core, home, received, after, grid=(N_CHIP, d // tn),
                      a_block=(s, r), a_map=lambda k, j, p: (0, 2 * k + p),
                      b_block=(s, tn), b_map=lambda k, j, p: (0, j),
                      o_block=(None, r, tn), o_map=lambda k, j, p: (k, 0, j),
                      out_shape=jax.ShapeDtypeStruct((N_CHIP, r, d), BF16))


def _up_half_grad(name, h2, d_up, core, home, received=None, after=()):
    s, d = h2.shape
    fb = d_up.shape[1] // N_DEV
    tm = min(TILE_M, d)
    return _half_grad(name, h2, d_up, core, home, received, after, grid=(d // tm, N_CHIP),
                      a_block=(s, tm), a_map=lambda i, k, p: (0, i),
                      b_block=(s, fb), b_map=lambda i, k, p: (0, 2 * k + p),
                      o_block=(None, tm, fb), o_map=lambda i, k, p: (k, i, 0),
                      out_shape=jax.ShapeDtypeStruct((N_CHIP, d, fb), BF16))


def _in_pad(in_width):
    return -(-in_width // LANE) * LANE


def _join_col_shards(name, blocks, own, device, pieces=None):
    n, r, w = blocks.shape
    rows = min(ROWS, r)
    pieces = pieces or [(j, 0, w) for j in range(n)]
    used = sum(b - a for _, a, b in pieces)
    width = _in_pad(used)

    def body(dev_ref, x_ref, own_ref, o_ref):
        block = lambda j: jnp.where(dev_ref[0] == j, own_ref[...], x_ref[j])
        cols = [block(j)[:, a:b] for j, a, b in pieces]
        tail = [jnp.zeros((rows, width - used), o_ref.dtype)] if width > used else []
        o_ref[...] = jnp.concatenate(cols + tail, axis=1)

    return pl.pallas_call(
        body, name=name,
        grid_spec=pltpu.PrefetchScalarGridSpec(
            num_scalar_prefetch=1, grid=(r // rows,),
            in_specs=[pl.BlockSpec((n, rows, w), lambda i, dev: (0, i, 0)), pl.BlockSpec((rows, w), lambda i, dev: (i, 0))],
            out_specs=pl.BlockSpec((rows, width), lambda i, dev: (i, 0))),
        out_shape=jax.ShapeDtypeStruct((r, width), blocks.dtype),
        compiler_params=_params("parallel"),
    )(device, blocks, own)


def _permute_q_cols(w_uq, n_heads):
    r = w_uq.shape[0]
    w3 = w_uq.reshape(r, n_heads, QK)
    return jnp.concatenate([w3[:, :, :HEAD].reshape(r, n_heads * HEAD), w3[:, :, HEAD:].reshape(r, n_heads * ROPE)], axis=1)


def _unpermute_q_rows(wt, n_heads):
    r = wt.shape[1]
    nope = wt[:n_heads * HEAD].reshape(n_heads, HEAD, r)
    rope = wt[n_heads * HEAD:].reshape(n_heads, ROPE, r)
    return jnp.concatenate([nope, rope], axis=1).reshape(n_heads * QK, r)


def _local_step(x, tgt, gains, weights, grads, first_after=()):
    pre_mix_g, q_norm_g, kv_norm_g, conv_out_g, attn_out_g, post_mix_g, pre_mlp_g, post_mlp_g = gains
    s, d = x.shape
    conv_width = conv_out_g.shape[1]
    n_groups = conv_width // HEAD
    r_q, r_kv = q_norm_g.shape[1], kv_norm_g.shape[1]
    n_heads = attn_out_g.shape[1] // HEAD
    c_q0 = 3 * conv_width
    c_kv0 = c_q0 + r_q
    c_kr0 = c_kv0 + r_kv
    in_pad = _in_pad(c_kr0 + ROPE)
    tn_in = in_pad // 5 if in_pad % (5 * LANE) == 0 else LANE
    tables = _rope_tables(s, n_heads)

    h1 = _rms_fwd("pre_mix_norm", x, pre_mix_g, after=first_after)
    weights.forward(0, (h1,))
    weights.relay(0, tables)
    w_in_p, conv_w = weights.ready(0, ())
    proj = _mm_nn("in_proj", h1, w_in_p, F32, TILE_M, tn_in)
    y_conv = _conv_fwd(proj, conv_w, conv_out_g, n_groups, conv_width + n_heads * HEAD, after=weights.forward(1, (proj,)))
    w_uq_p, w_ukv, w_o = weights.ready(1, (y_conv,))
    qn, q = _norm_up("q_up", proj, (c_q0, r_q), q_norm_g, w_uq_p)
    kvn, kv = _norm_up("kv_up", proj, (c_kv0, r_kv), kv_norm_g, w_ukv)
    qh, kh, vh = _pack_heads(q, kv, proj, c_kr0, tables, n_heads, after=weights.forward(2, (q, kv)))
    o, mix = _attn_fwd(qh, kh, vh, attn_out_g, y_conv, conv_width)
    y = _mm_nn("out_proj", mix, w_o, F32, TILE_M, TILE_N)
    x2, h2 = _mid_fwd(x, y, post_mix_g, pre_mlp_g, after=weights.forward(3, (y,)))
    weights.relay(2, (h2,))
    (w_up,) = weights.ready(2, ())
    a, r = _up_fwd(h2, w_up)
    weights.relay(3, (a,))
    (w_down,) = weights.ready(3, ())
    m = _down_fwd(a, w_down)

    d_out, d_m, dg_post_mlp, loss_part = _head(m, x2, tgt, post_mlp_g)
    core = grads.core
    away = _down_half_grad("down_bwd_w_away", a, d_m, core, home=False)
    d_up = _down_bwd_act(d_m, w_down, r, after=grads.send_away(0, away))
    sums = _down_half_grad("down_bwd_w_home", a, d_m, core, home=True, received=grads.received(0, (d_up,)))
    away = _up_half_grad("up_bwd_w_away", h2, d_up, core, home=False, after=grads.send_sums(0, (sums,)))
    d_h2 = _up_bwd_act(d_up, w_up, after=grads.send_away(1, away))
    sums = _up_half_grad("up_bwd_w_home", h2, d_up, core, home=True, received=grads.received(1, (d_h2,)))
    d_x2, d_y, dg_pre_mlp, dg_post_mix = _mid_bwd(x2, y, d_out, d_h2, pre_mlp_g, post_mix_g, after=grads.send_sums(1, (sums,)))
    d_mix = _mm_nt("out_proj_bwd_act", d_y, w_o, F32, TILE_M, TILE_N)
    gw_o = _mm_tn("out_proj_bwd_w", mix, d_y, BF16, TILE_M, TILE_N)
    dqh, dkh, dvh, dg_attn, *updated = _attn_bwd(qh, kh, vh, o, d_mix, attn_out_g, conv_width, after=grads.full(2, (gw_o,)),
                                                 side=grads.update_in_passing(0, (gw_o,)))
    grads.updated(0, updated)
    d_q, d_kv, d_kr = _unpack_heads(dqh, dkh, dvh, tables, n_heads)
    gw_uq_t = _mm_tn("q_up_bwd_w", d_q, qn, F32, TILE_M, TILE_N)
    gw_ukv = _mm_tn("kv_up_bwd_w", kvn, d_kv, BF16, TILE_M, TILE_N)
    d_cq, dg_q = _up_norm_bwd("q_up_bwd_act", d_q, w_uq_p, proj, (c_q0, r_q), q_norm_g, after=grads.full(3, (gw_uq_t, gw_ukv)))
    d_ckv, dg_kv = _up_norm_bwd("kv_up_bwd_act", d_kv, w_ukv, proj, (c_kv0, r_kv), kv_norm_g)
    d_u, d_b, d_c, dg_conv, dw_conv = _conv_bwd(proj, d_mix, conv_w, conv_out_g, n_groups)
    d_proj = jnp.concatenate([d_u, d_b, d_c, d_cq, d_ckv, d_kr[:, :in_pad - c_kr0]], axis=1)
    gw_in_t = _mm_tn("in_proj_bwd_w", d_proj, h1, F32, tn_in, TILE_N)
    d_h1 = _mm_nt("in_proj_bwd_act", d_proj, w_in_p, F32, TILE_M, TILE_N, after=grads.send_away(4, gw_in_t))
    grad_x, dg_pre_mix = _first_bwd(x, pre_mix_g, d_h1, d_x2, after=grads.full(4, (gw_in_t,), received=(d_h1,)))

    small = [dg_pre_mix, dg_q, dg_kv, dg_conv, dg_attn, dg_post_mix, dg_pre_mlp, dg_post_mlp,
             dw_conv[0], dw_conv[1], dw_conv[2], loss_part]
    return grad_x, jnp.concatenate(small, axis=1)


HBM = pl.BlockSpec(memory_space=pltpu.HBM)
SEM = pl.BlockSpec(memory_space=pltpu.SEMAPHORE)
IN_VMEM = pl.BlockSpec(memory_space=pltpu.VMEM)
SPLIT = pltpu.CompilerParams(has_side_effects=pltpu.SideEffectType.DATAFLOW_SIDE_EFFECTING)


def _in_hbm(a):
    return pltpu.with_memory_space_constraint(a, pltpu.HBM)


def _hbm_like(a):
    return pltpu.HBM(a.shape, a.dtype)


def _place():
    x, y, c = lax.axis_index("x"), lax.axis_index("y"), lax.axis_index("c")
    other_chips = [(1 - x, y), (x, 1 - y), (1 - x, 1 - y)]
    return x, y, c, other_chips


def _block(px, py, pc):
    return 4 * px + 2 * py + pc


def _await(block, sem):
    pltpu.make_async_copy(block, block, sem).wait()


def _relay_route(x, y, c):
    came_from = ((1 - x) * (1 - c) + x * c, y * (1 - c) + (1 - y) * c)
    goes_to = (x * (1 - c) + (1 - x) * c, (1 - y) * (1 - c) + y * c)
    return came_from, goes_to


def _gather_start(name, shards, groups, relayed=(), after=()):
    n, ng = len(shards), len(groups)
    lands = [lax.empty((N_DEV, *a.shape), a.dtype) for a in shards]

    def body(*refs):
        src, land = refs[:n], refs[n:2 * n]
        sems, token = refs[2 * n + len(after):2 * n + len(after) + 2 * ng], refs[-1]
        x, y, c, chips = _place()
        targets = [(x, y, 1 - c)] + [(*chip, c) for chip in chips]
        for gi, group in enumerate(groups):
            for i, w in enumerate(group):
                for k, to in enumerate(targets[:3] if gi in relayed else targets):
                    pltpu.make_async_remote_copy(
                        src_ref=src[w], dst_ref=land[w].at[_block(x, y, c)],
                        send_sem=sems[2 * gi].at[4 * i + k], recv_sem=sems[2 * gi + 1].at[4 * i + k],
                        device_id=to, device_id_type=MESH).start()
        token[...] = jnp.zeros_like(token)

    sem_shapes = [pltpu.SemaphoreType.DMA((4 * len(g),)) for g in groups for _ in range(2)]
    out = pl.pallas_call(
        body, name=name,
        in_specs=[HBM] * (2 * n) + [ANY] * len(after),
        out_specs=[SEM] * (2 * ng) + [HBM] * (2 * n) + [IN_VMEM],
        out_shape=sem_shapes + [_hbm_like(a) for a in shards] + [_hbm_like(a) for a in lands]
        + [jax.ShapeDtypeStruct((SUBLANE, LANE), F32)],
        input_output_aliases={i: 2 * ng + i for i in range(2 * n)},
        compiler_params=SPLIT,
    )(*[_in_hbm(a) for a in shards], *[_in_hbm(a) for a in lands], *after)
    sems = [(out[2 * gi], out[2 * gi + 1]) for gi in range(ng)]
    return sems, out[2 * ng:2 * ng + n], out[2 * ng + n:2 * ng + 2 * n], out[-1]


def _gather_forward(name, shards, lands, send1, recv1, after, relayed=False):
    n = len(lands)

    def body(*refs):
        src, land = refs[:n], refs[n:2 * n]
        s1, r1 = refs[2 * n], refs[2 * n + 1]
        s2, r2 = refs[2 * n + 2 + len(after)], refs[2 * n + 3 + len(after)]
        x, y, c, chips = _place()
        me, sibling = (x, y, c), (x, y, 1 - c)
        for j, chip in enumerate(chips[:2] if relayed else chips):
            for i in range(n):
                blk = land[i].at[_block(*chip, c)]
                pltpu.make_async_remote_copy(src_ref=blk, dst_ref=blk, send_sem=s1.at[4 * i + 1 + j], recv_sem=r1.at[4 * i + 1 + j],
                                             device_id=me, device_id_type=MESH).wait_recv()
                pltpu.make_async_remote_copy(src_ref=blk, dst_ref=blk, send_sem=s2.at[3 * i + j], recv_sem=r2.at[3 * i + j],
                                             device_id=sibling, device_id_type=MESH).start()
        if relayed:
            came_from, goes_to = _relay_route(x, y, c)
            for i in range(n):
                blk = land[i].at[_block(*came_from, c)]
                pltpu.make_async_remote_copy(src_ref=blk, dst_ref=blk, send_sem=s2.at[3 * i + 2], recv_sem=r2.at[3 * i + 2],
                                             device_id=(*goes_to, c), device_id_type=MESH).start()
        for i in range(n):
            blk = land[i].at[_block(x, y, 1 - c)]
            pltpu.make_async_remote_copy(src_ref=blk, dst_ref=blk, send_sem=s1.at[4 * i], recv_sem=r1.at[4 * i],
                                         device_id=me, device_id_type=MESH).wait_recv()
            for k in range(3 if relayed else 4):
                pltpu.make_async_remote_copy(src_ref=src[i], dst_ref=land[i].at[_block(x, y, c)], send_sem=s1.at[4 * i + k],
                                             recv_sem=r1.at[4 * i + k], device_id=sibling, device_id_type=MESH).wait_send()

    sem = pltpu.SemaphoreType.DMA((3 * n,))
    out = pl.pallas_call(
        body, name=name,
        in_specs=[HBM] * (2 * n) + [SEM, SEM] + [ANY] * len(after),
        out_specs=[SEM, SEM] + [HBM] * n,
        out_shape=[sem, sem] + [_hbm_like(a) for a in lands],
        input_output_aliases={n + i: 2 + i for i in range(n)},
        compiler_params=SPLIT,
    )(*shards, *lands, send1, recv1, *after)
    return (out[0], out[1]), out[2:]


def _gather_relay_forward(name, lands, send2, recv2, after):
    n = len(lands)

    def body(*refs):
        land, s2, r2 = refs[:n], refs[n], refs[n + 1]
        s3, r3 = refs[n + 2 + len(after)], refs[n + 3 + len(after)]
        x, y, c, _ = _place()
        me, sibling = (x, y, c), (x, y, 1 - c)
        came_from, _ = _relay_route(x, y, c)
        for i in range(n):
            blk = land[i].at[_block(1 - x, 1 - y, c)]
            pltpu.make_async_remote_copy(src_ref=blk, dst_ref=blk, send_sem=s2.at[3 * i + 2], recv_sem=r2.at[3 * i + 2],
                                         device_id=me, device_id_type=MESH).wait_recv()
            pltpu.make_async_remote_copy(src_ref=blk, dst_ref=blk, send_sem=s3.at[i], recv_sem=r3.at[i],
                                         device_id=sibling, device_id_type=MESH).start()
            sent = land[i].at[_block(*came_from, c)]
            pltpu.make_async_remote_copy(src_ref=sent, dst_ref=sent, send_sem=s2.at[3 * i + 2], recv_sem=r2.at[3 * i + 2],
                                         device_id=me, device_id_type=MESH).wait_send()

    sem = pltpu.SemaphoreType.DMA((n,))
    out = pl.pallas_call(
        body, name=name,
        in_specs=[HBM] * n + [SEM, SEM] + [ANY] * len(after),
        out_specs=[SEM, SEM] + [HBM] * n,
        out_shape=[sem, sem] + [_hbm_like(a) for a in lands],
        input_output_aliases={i: 2 + i for i in range(n)},
        compiler_params=SPLIT,
    )(*lands, send2, recv2, *after)
    return (out[0], out[1]), out[2:]


def _gather_wait(name, lands, send2, recv2, after, relay_sems=None):
    n = len(lands)
    n_sems = 2 if relay_sems is None else 4

    def body(*refs):
        land, s2, r2 = refs[:n], refs[n], refs[n + 1]
        for i in range(n):
            for j in range(3 if relay_sems is None else 2):
                _await(land[i].at[0], r2.at[3 * i + j])
                _await(land[i].at[0], s2.at[3 * i + j])
            if relay_sems is not None:
                _await(land[i].at[0], refs[n + 3].at[i])
                _await(land[i].at[0], refs[n + 2].at[i])

    return pl.pallas_call(
        body, name=name,
        in_specs=[HBM] * n + [SEM] * n_sems + [ANY] * len(after), out_specs=[HBM] * n, out_shape=[_hbm_like(a) for a in lands],
        input_output_aliases={i: i for i in range(n)},
        compiler_params=SPLIT,
    )(*lands, send2, recv2, *(relay_sems or ()), *after)


def _pair_exchange(name, grads, shard_rows):
    n = len(grads)
    shapes = [(g.shape[1:] if r is None else (r, g.shape[1])) for g, r in zip(grads, shard_rows)]

    def body(*refs):
        ins, recv = refs[:n], refs[n:2 * n]
        send_sems, recv_sems = refs[2 * n:]
        x, y, c, _ = _place()
        sends = []
        for w in range(n):
            for k in range(N_CHIP):
                j, r = 2 * k + 1 - c, shard_rows[w]
                src = ins[w].at[j] if r is None else ins[w].at[pl.ds(pl.multiple_of(j * r, SUBLANE), r), :]
                sends.append(pltpu.make_async_remote_copy(
                    src_ref=src, dst_ref=recv[w].at[k],
                    send_sem=send_sems.at[w, k], recv_sem=recv_sems.at[w, k],
                    device_id=(x, y, 1 - c), device_id_type=MESH))
        for cp in sends:
            cp.start()
        for cp in sends:
            cp.wait()

    return pl.pallas_call(
        body, name=name,
        in_specs=[ANY] * n, out_specs=[ANY] * n,
        out_shape=[jax.ShapeDtypeStruct((N_CHIP, *shape), g.dtype) for g, shape in zip(grads, shapes)],
        scratch_shapes=[pltpu.SemaphoreType.DMA((n, N_CHIP))] * 2,
    )(*grads)


def _pair_sum_rows(name, grad, received, core):
    _, r, c = received.shape
    tc = _fit(c, 512)

    def body(core_ref, a_ref, b_ref, o_ref):
        o_ref[...] = (a_ref[...] + b_ref[...]).astype(o_ref.dtype)

    spec = pl.BlockSpec((None, r, tc), lambda k, i, core_ref: (k, 0, i))
    return pl.pallas_call(
        body, name=name,
        grid_spec=pltpu.PrefetchScalarGridSpec(
            num_scalar_prefetch=1, grid=(N_CHIP, c // tc),
            in_specs=[pl.BlockSpec((r, tc), lambda k, i, core_ref: (2 * k + core_ref[0], i)), spec],
            out_specs=spec),
        out_shape=jax.ShapeDtypeStruct(received.shape, BF16),
        compiler_params=_params("parallel", "parallel"),
    )(core, grad, received)


def _pair_sum(name, grad, received, core):
    _, r, c = received.shape
    rows = min(ROWS, r)
    assert r % rows == 0

    def body(core_ref, a_ref, b_ref, o_ref):
        o_ref[...] = (a_ref[...].astype(F32) + b_ref[...].astype(F32)).astype(o_ref.dtype)

    spec = pl.BlockSpec((None, rows, c), lambda k, i, core_ref: (k, i, 0))
    return pl.pallas_call(
        body, name=name,
        grid_spec=pltpu.PrefetchScalarGridSpec(
            num_scalar_prefetch=1, grid=(N_CHIP, r // rows),
            in_specs=[pl.BlockSpec((None, None, rows, c), lambda k, i, core_ref: (k, core_ref[0], i, 0)), spec],
            out_specs=spec),
        out_shape=jax.ShapeDtypeStruct(received.shape, received.dtype),
        compiler_params=_params("parallel", "parallel"),
    )(core, grad.reshape(N_CHIP, 2, r, c), received)


def _away_shard(src, k, c, shard_rows):
    if shard_rows is None:
        return src.at[k]
    return src.at[pl.ds(pl.multiple_of((2 * k + 1 - c) * shard_rows, SUBLANE), shard_rows), :]


def _pair_send_start(name, away, shard_rows=None):
    shape = away.shape if shard_rows is None else (N_CHIP, shard_rows, away.shape[1])
    land = lax.empty(shape, away.dtype)

    def body(src, dst, send, recv, src_thru, dst_thru, token):
        x, y, c, _ = _place()
        for k in range(N_CHIP):
            pltpu.make_async_remote_copy(src_ref=_away_shard(src, k, c, shard_rows), dst_ref=dst.at[k], send_sem=send.at[k],
                                         recv_sem=recv.at[k], device_id=(x, y, 1 - c), device_id_type=MESH).start()
        token[...] = jnp.zeros_like(token)

    sem = pltpu.SemaphoreType.DMA((N_CHIP,))
    out = pl.pallas_call(
        body, name=name,
        in_specs=[HBM, HBM], out_specs=[SEM, SEM, HBM, HBM, IN_VMEM],
        out_shape=[sem, sem, _hbm_like(away), _hbm_like(land), jax.ShapeDtypeStruct((SUBLANE, LANE), F32)],
        input_output_aliases={0: 2, 1: 3},
        compiler_params=SPLIT,
    )(_in_hbm(away), _in_hbm(land))
    return (out[0], out[1]), out[2], out[3], out[4]


def _pair_send_wait(name, sems, src, land, after, shard_rows=None):
    def body(src_ref, dst_ref, send, recv, *rest):
        for k in range(N_CHIP):
            _await(dst_ref.at[k], send.at[k])
            _await(dst_ref.at[k], recv.at[k])

    return pl.pallas_call(
        body, name=name,
        in_specs=[HBM, HBM, SEM, SEM] + [ANY] * len(after), out_specs=HBM, out_shape=_hbm_like(land),
        input_output_aliases={1: 0},
        compiler_params=SPLIT,
    )(src, land, *sems, *after)


def _chip_send_start(name, sums):
    n = len(sums)
    lands = [lax.empty(a.shape, a.dtype) for a in sums]

    def body(*refs):
        src, land = refs[:n], refs[n:2 * n]
        send, recv, token = refs[2 * n], refs[2 * n + 1], refs[-1]
        x, y, c, chips = _place()
        for w in range(n):
            for j, (px, py) in enumerate(chips):
                pltpu.make_async_remote_copy(
                    src_ref=src[w].at[2 * px + py], dst_ref=land[w].at[2 * x + y],
                    send_sem=send.at[3 * w + j], recv_sem=recv.at[3 * w + j],
                    device_id=(px, py, c), device_id_type=MESH).start()
        token[...] = jnp.zeros_like(token)

    sem = pltpu.SemaphoreType.DMA((3 * n,))
    out = pl.pallas_call(
        body, name=name,
        in_specs=[HBM] * (2 * n),
        out_specs=[SEM, SEM] + [HBM] * (2 * n) + [IN_VMEM],
        out_shape=[sem, sem] + [_hbm_like(a) for a in sums] + [_hbm_like(a) for a in lands]
        + [jax.ShapeDtypeStruct((SUBLANE, LANE), F32)],
        input_output_aliases={i: 2 + i for i in range(2 * n)},
        compiler_params=SPLIT,
    )(*[_in_hbm(a) for a in sums], *[_in_hbm(a) for a in lands])
    return (out[0], out[1]), out[2:2 + n], out[2 + n:2 + 2 * n], out[-1]


def _chip_send_wait(name, groups, after):
    counts = [len(g[1]) for g in groups]
    n = sum(counts)

    def body(*refs):
        land = refs[n:2 * n]
        sems = refs[2 * n:2 * n + 2 * len(groups)]
        w = 0
        for gi, count in enumerate(counts):
            for i in range(count):
                for j in range(3):
                    _await(land[w].at[0], sems[2 * gi].at[3 * i + j])
                    _await(land[w].at[0], sems[2 * gi + 1].at[3 * i + j])
                w += 1

    sums = [a for g in groups for a in g[1]]
    lands = [a for g in groups for a in g[2]]
    sems = [s for g in groups for s in g[0]]
    return pl.pallas_call(
        body, name=name,
        in_specs=[HBM] * (2 * n) + [SEM] * len(sems) + [ANY] * len(after),
        out_specs=[HBM] * n, out_shape=[_hbm_like(a) for a in lands],
        input_output_aliases={n + i: i for i in range(n)},
        compiler_params=SPLIT,
    )(*sums, *lands, *sems, *after)


def _small_all_reduce(part, after=()):
    _, w = part.shape

    def body(p_ref, *rest):
        o_ref, buf, send_sems, recv_sems = rest[len(after):]
        x, y, c, _ = _place()
        me = 4 * x + 2 * y + c
        buf[me] = jnp.sum(p_ref[...], axis=0, keepdims=True)
        copies = []
        for k in range(1, N_DEV):
            dx, dy, dc = (k >> 2) & 1, (k >> 1) & 1, k & 1
            copies.append(pltpu.make_async_remote_copy(
                src_ref=buf.at[me], dst_ref=buf.at[me], send_sem=send_sems.at[k - 1], recv_sem=recv_sems.at[k - 1],
                device_id=(x ^ dx, y ^ dy, c ^ dc), device_id_type=MESH))
        for cp in copies:
            cp.start()
        for cp in copies:
            cp.wait()
        tot = buf[0]
        for d in range(1, N_DEV):
            tot = tot + buf[d]
        o_ref[...] = tot
        loss = jnp.sum(tot[:, w - LANE:], axis=1, keepdims=True)
        o_ref[:, w - LANE:] = jnp.broadcast_to(loss, (1, LANE))

    return pl.pallas_call(
        body, name="small_all_reduce",
        in_specs=[IN_VMEM] + [ANY] * len(after), out_specs=IN_VMEM,
        out_shape=jax.ShapeDtypeStruct((1, w), F32),
        scratch_shapes=[pltpu.VMEM((N_DEV, 1, w), F32), pltpu.SemaphoreType.DMA((N_DEV - 1,)), pltpu.SemaphoreType.DMA((N_DEV - 1,))],
        compiler_params=pltpu.CompilerParams(vmem_limit_bytes=VMEM_LIMIT_BYTES),
    )(part, *after)


def _adamw(w, g, m, v):
    m = ADAM_B1 * m + (1.0 - ADAM_B1) * g
    v = ADAM_B2 * v + (1.0 - ADAM_B2) * (g * g)
    m_hat = m / (1.0 - ADAM_B1 ** ADAM_STEP)
    v_hat = v / (1.0 - ADAM_B2 ** ADAM_STEP)
    delta = -ADAM_LR * (m_hat / (jnp.sqrt(v_hat) + ADAM_EPS) + ADAM_WD * w)
    return delta, m, v


def _sum_adam_block(chip_ref, p_ref, own_ref, w_ref, m_ref, v_ref, g_ref, d_ref, mo_ref, vo_ref):
    g = None
    for k in range(N_CHIP):
        term = jnp.where(chip_ref[0] == k, own_ref[...], p_ref[k]).astype(F32)
        g = term if g is None else g + term
    g_ref[...] = g
    d_ref[...], mo_ref[...], vo_ref[...] = _adamw(w_ref[...], g, m_ref[...], v_ref[...])


def _side_update(side, n_steps):
    parts, sums, w, m, v = side
    _, r, c = w.shape
    br = r // n_steps
    assert r % n_steps == 0 and br % (2 * SUBLANE) == 0
    blk = pl.BlockSpec((None, br, c), lambda h, chip_ref: (0, h, 0))
    in_specs = [pl.BlockSpec((N_CHIP, br, c), lambda h, chip_ref: (0, h, 0)),
                pl.BlockSpec((None, br, c), lambda h, chip_ref: (chip_ref[0], h, 0)), blk, blk, blk]
    return in_specs, [blk] * 4, [jax.ShapeDtypeStruct((1, r, c), F32)] * 4, [parts, sums, w, m, v]


def _sum_adam(name, parts, sums, chip, w, m, v, after=()):
    _, r, c = w.shape
    n_after = len(after)
    by_rows = r % ROWS == 0 or r < ROWS
    tr, tc = (min(ROWS, r), c) if by_rows else (r, _fit(c, 512))
    at = (lambda i: (i, 0)) if by_rows else (lambda i: (0, i))

    def body(chip_ref, p_ref, own_ref, w_ref, m_ref, v_ref, *rest):
        _sum_adam_block(chip_ref, p_ref, own_ref, w_ref, m_ref, v_ref, *rest[n_after:])

    blk = pl.BlockSpec((None, tr, tc), lambda i, chip_ref: (0, *at(i)))
    out = jax.ShapeDtypeStruct((1, r, c), F32)
    return pl.pallas_call(
        body, name=name,
        grid_spec=pltpu.PrefetchScalarGridSpec(
            num_scalar_prefetch=1, grid=(r // tr if by_rows else c // tc,),
            in_specs=[pl.BlockSpec((N_CHIP, tr, tc), lambda i, chip_ref: (0, *at(i))),
                      pl.BlockSpec((None, tr, tc), lambda i, chip_ref: (chip_ref[0], *at(i))), blk, blk, blk]
            + [ANY] * n_after,
            out_specs=[blk] * 4),
        out_shape=[out] * 4,
        compiler_params=_params("parallel"),
    )(chip, parts, sums, w, m, v, *after)


def _adam_gains(total, ws, ms, vs):
    n = len(ws)
    widths = [w.shape[1] for w in ws]

    def body(t_ref, *refs):
        w_refs, m_refs, v_refs, outs = refs[:n], refs[n:2 * n], refs[2 * n:3 * n], refs[3 * n:]
        off = 0
        for i in range(n):
            g = t_ref[:, off:off + widths[i]]
            off += widths[i]
            g_ref, d_ref, mo_ref, vo_ref = outs[4 * i:4 * i + 4]
            g_ref[...] = g
            d_ref[...], mo_ref[...], vo_ref[...] = _adamw(w_refs[i][...], g, m_refs[i][...], v_refs[i][...])

    out = pl.pallas_call(
        body, name="adam_gains",
        out_shape=[jax.ShapeDtypeStruct(w.shape, F32) for w in ws for _ in range(4)],
    )(total, *ws, *ms, *vs)
    return [tuple(out[4 * i:4 * i + 4]) for i in range(n)]


def _adam_taps(total, first_col, device, w, m, v):
    _, n_taps, cw = w.shape
    col_block = lambda t, dev: (0, first_col // cw + t * N_DEV + dev[0])
    tap = pl.BlockSpec((None, 1, cw), lambda t, dev: (t, 0, 0))

    def body(dev_ref, t_ref, w_ref, m_ref, v_ref, g_ref, d_ref, mo_ref, vo_ref):
        g = t_ref[...]
        g_ref[...] = g
        d_ref[...], mo_ref[...], vo_ref[...] = _adamw(w_ref[...], g, m_ref[...], v_ref[...])

    shape3 = (n_taps, 1, cw)
    out = pl.pallas_call(
        body, name="adam_taps",
        grid_spec=pltpu.PrefetchScalarGridSpec(
            num_scalar_prefetch=1, grid=(n_taps,),
            in_specs=[pl.BlockSpec((1, cw), col_block), tap, tap, tap], out_specs=[tap] * 4),
        out_shape=[jax.ShapeDtypeStruct(shape3, F32)] * 4,
    )(device, total, w.reshape(shape3), m.reshape(shape3), v.reshape(shape3))
    return tuple(o.reshape(w.shape) for o in out)


def kernel(x, pre_mix_g, w_in, conv_w, q_norm_g, w_uq, kv_norm_g, w_ukv, conv_out_g, attn_out_g, w_o, post_mix_g, pre_mlp_g, w_up, w_down, post_mlp_g, loss_target, m_pre_mix_g, m_w_in, m_conv_w, m_q_norm_g, m_w_uq, m_kv_norm_g, m_w_ukv, m_conv_out_g, m_attn_out_g, m_w_o, m_post_mix_g, m_pre_mlp_g, m_w_up, m_w_down, m_post_mlp_g, v_pre_mix_g, v_w_in, v_conv_w, v_q_norm_g, v_w_uq, v_kv_norm_g, v_w_ukv, v_conv_out_g, v_attn_out_g, v_w_o, v_post_mix_g, v_pre_mlp_g, v_w_up, v_w_down, v_post_mlp_g):
    me = 4 * lax.axis_index("x") + 2 * lax.axis_index("y") + lax.axis_index("c")
    core = lax.axis_index("c").astype(jnp.int32).reshape(1)
    chip = (2 * lax.axis_index("x") + lax.axis_index("y")).astype(jnp.int32).reshape(1)
    gains = (pre_mix_g, q_norm_g, kv_norm_g, conv_out_g, attn_out_g, post_mix_g, pre_mlp_g, post_mlp_g)
    gain_m = (m_pre_mix_g, m_q_norm_g, m_kv_norm_g, m_conv_out_g, m_attn_out_g, m_post_mix_g, m_pre_mlp_g, m_post_mlp_g)
    gain_v = (v_pre_mix_g, v_q_norm_g, v_kv_norm_g, v_conv_out_g, v_attn_out_g, v_post_mix_g, v_pre_mlp_g, v_post_mlp_g)
    names = ("w_in", "w_uq", "w_ukv", "w_o", "w_up", "w_down")
    big = dict(zip(names, (w_in, w_uq, w_ukv, w_o, w_up, w_down)))
    big_m = dict(zip(names, (m_w_in, m_w_uq, m_w_ukv, m_w_o, m_w_up, m_w_down)))
    big_v = dict(zip(names, (v_w_in, v_w_uq, v_w_ukv, v_w_o, v_w_up, v_w_down)))
    n_heads = attn_out_g.shape[1] // HEAD
    n_taps = conv_w.shape[1]

    gathered = ("w_in", "conv", "w_uq", "w_ukv", "w_o", "w_up", "w_down")
    gather_groups = ((0, 1), (2, 3, 4), (5,), (6,))
    taps = jnp.pad(conv_w[0], ((0, SUBLANE - n_taps), (0, 0)))
    relayed_groups = (0, 2, 3)
    sems1, shards, lands, token = _gather_start("gather_start_first", [w_in[0].astype(BF16), taps], ((0, 1),), relayed=(0,))
    sems1, shards, lands = list(sems1), list(shards), list(lands)
    behind = token[0, 0]
    rest = [(big[nm][0] + behind).astype(BF16) for nm in gathered[2:]]

    def start_rest(after):
        sems_b, shards_b, lands_b, started = _gather_start("gather_start_rest", rest, ((0, 1, 2), (3,), (4,)), relayed=(1, 2),
                                                          after=after)
        sems1.extend(sems_b)
        shards.extend(shards_b)
        lands.extend(lands_b)
        return started

    cols = lambda a: jnp.concatenate([a[j] for j in range(N_DEV)], axis=1)
    rows = lambda a: a.reshape(N_DEV * a.shape[1], a.shape[2])
    device = me.astype(jnp.int32).reshape(1)
    own_in = lambda a, shard: lax.dynamic_update_index_in_dim(a, shard, me, 0)
    q_pieces = [(h, 0, HEAD) for h in range(n_heads)] + [(h, HEAD, QK) for h in range(n_heads)]
    ready = {
        "w_in": lambda a, shard: _join_col_shards("join_w_in", a, shard, device),
        "conv": lambda a, shard: cols(own_in(a, shard))[:n_taps],
        "w_uq": lambda a, shard: _join_col_shards("join_w_uq", a, shard, device, q_pieces),
        "w_ukv": lambda a, shard: cols(own_in(a, shard)),
        "w_o": lambda a, shard: rows(own_in(a, shard)),
        "w_up": own_in,
        "w_down": lambda a, shard: rows(own_in(a, shard)),
    }
    assert w_uq.shape[2] == QK

    class Weights:
        def __init__(self):
            self.passed, self.relayed = {}, {}

        def forward(self, group, after):
            idx = gather_groups[group]
            if group == 0:
                after = (*after, *rest)
            self.passed[group] = _gather_forward(f"gather_forward_{group}", [shards[i] for i in idx], [lands[i] for i in idx],
                                                 *sems1[group], after, relayed=group in relayed_groups)
            return tuple(self.passed[group][1])

        def relay(self, group, after):
            sems2, mid = self.passed[group]
            self.relayed[group], mid = _gather_relay_forward(f"gather_relay_{group}", mid, *sems2, after)
            self.passed[group] = (sems2, mid)
            if group == 0:
                start_rest(tuple(mid))
            return tuple(mid)

        def ready(self, group, after):
            sems2, mid = self.passed[group]
            full = _gather_wait(f"gather_wait_{group}", mid, *sems2, after, relay_sems=self.relayed.get(group))
            out = []
            return [ready[gathered[i]](a, shards[i]) for i, a in zip(gather_groups[group], full)]

    weights = Weights()

    col_blocks = lambda g: g.reshape(g.shape[0], N_DEV, g.shape[1] // N_DEV).transpose(1, 0, 2)
    row_blocks = lambda g: g.reshape(N_DEV, g.shape[0] // N_DEV, g.shape[1])
    grad_groups = (("w_down",), ("w_up",), ("w_o",), ("w_uq", "w_ukv"), ("w_in",))
    transposed = {"w_in": w_in.shape[2], "w_uq": w_uq.shape[2]}
    to_blocks = {
        "w_in": lambda g: g, "w_uq": lambda g: _unpermute_q_rows(g, n_heads),
        "w_ukv": col_blocks, "w_o": row_blocks, "w_up": lambda g: g, "w_down": row_blocks,
    }
    in_flight = []

    class Grads:
        def __init__(self):
            self.core = core
            self.away = {}

        def send_sums(self, group, sums):
            sems, sums, parts, tok = _chip_send_start(f"chip_send_start_{group}", list(sums))
            in_flight.append((sems, sums, parts))
            return (tok,)

        def full(self, group, arrays, received=None):
            nms = grad_groups[group]
            if received is None:
                blocks = [to_blocks[nm](g) for nm, g in zip(nms, arrays)]
                got = _pair_exchange(f"pair_exchange_{group}", blocks, [transposed.get(nm) for nm in nms])
            else:
                blocks, got = [self.away[group][1]], [self.received(group, received)]
            sums = [(_pair_sum_rows if nm in transposed else _pair_sum)(f"pair_sum_{nm}", g, r, core)
                    for nm, g, r in zip(nms, blocks, got)]
            return self.send_sums(group, sums)

        def send_away(self, group, half):
            nm = grad_groups[group][0]
            rows = transposed.get(nm)
            sems, src, land, tok = _pair_send_start(f"pair_send_start_{group}", half if rows is None else to_blocks[nm](half), rows)
            self.away[group] = (sems, src, land, rows)
            return (tok,)

        def received(self, group, after):
            sems, src, land, rows = self.away[group]
            return _pair_send_wait(f"pair_send_wait_{group}", sems, src, land, after, rows)

        def update_in_passing(self, group, after):
            nm, = grad_groups[group]
            _, sums, _ = in_flight[group]
            parts = _chip_send_wait(f"chip_send_wait_{group}", [in_flight[group]], after)
            return chip, (parts[0], sums[0], big[nm], big_m[nm], big_v[nm])

        def updated(self, group, outs):
            big_out[grad_groups[group][0]] = list(outs)

    big_out = {}
    grad_x, small = _local_step(x[0], loss_target[0], gains, weights, Grads(), first_after=(token,))

    def update(tag, first, last, after):
        picked = [i for i in range(first, last) if grad_groups[i][0] not in big_out]
        groups = [in_flight[i] for i in picked]
        parts = _chip_send_wait("chip_send_wait_" + tag, groups, after)
        nms = [nm for i in picked for nm in grad_groups[i]]
        sums = [a for _, s, _ in groups for a in s]
        for nm, p, s in zip(nms, parts, sums):
            view = (lambda a: jnp.swapaxes(a, 1, 2)) if nm in transposed else (lambda a: a)
            out = _sum_adam("adam_" + nm, p, s, chip, view(big[nm]), view(big_m[nm]), view(big_v[nm]), after=after)
            after = (out[0],)
            big_out[nm] = [view(o) for o in out]
        return after

    after = update("early", 0, len(in_flight) - 1, (grad_x,))
    total = _small_all_reduce(small, after=after)
    update("late", len(in_flight) - 1, len(in_flight), (total,))
    big_out = [big_out[nm] for nm in names]

    gain_out = _adam_gains(total, gains, gain_m, gain_v)
    taps_out = _adam_taps(total, sum(g.shape[1] for g in gains), me.astype(jnp.int32).reshape(1), conv_w, m_conv_w, v_conv_w)
    loss = total[0, total.shape[1] - 1]

    order = (0, "w_in", "conv", 1, "w_uq", 2, "w_ukv", 3, 4, "w_o", 5, 6, "w_up", "w_down", 7)
    by_name = dict(zip(names, big_out))
    outs = [loss, grad_x[None]]
    for kind in range(4):
        for item in order:
            if item == "conv":
                outs.append(taps_out[kind])
            elif isinstance(item, int):
                outs.append(gain_out[item][kind])
            else:
                outs.append(by_name[item][kind])
    return tuple(outs)
```

```python
import math

import jax
import jax.numpy as jnp
from jax import lax
from jax.experimental import pallas as pl
from jax.experimental.pallas import tpu as pltpu

F32 = jnp.float32
BF16 = jnp.bfloat16

EPS = 1e-6
NEG_INF = -1e30
HEAD = 128
ROPE = 64
QK = HEAD + ROPE
CHUNK = 64
ROPE_THETA = 10000.0
ADAM_LR, ADAM_B1, ADAM_B2, ADAM_EPS, ADAM_WD, ADAM_STEP = 0.001, 0.9, 0.999, 1e-08, 0.01, 10

LANE = 128
SUBLANE = 8
VMEM_LIMIT_BYTES = 56 * 1024 * 1024

N_DEV = 8
N_CHIP = 4
MESH = pl.DeviceIdType.MESH


def _params(*sem):
    return pltpu.CompilerParams(dimension_semantics=sem, vmem_limit_bytes=VMEM_LIMIT_BYTES)


ANY = pl.BlockSpec(memory_space=pl.ANY)


def _call(body, *, in_specs, after=(), **kw):
    n_in, n_after = len(in_specs), len(after)

    def ordered(*refs):
        body(*refs[:n_in], *refs[n_in + n_after:])

    call = pl.pallas_call(ordered, in_specs=[*in_specs, *[ANY] * n_after], **kw)
    return lambda *operands: call(*operands, *after)


def _sublane_sum(v):
    r, w = v.shape
    return jnp.sum(v.reshape(r // SUBLANE, SUBLANE, w), axis=0)


def _rstd(x):
    return lax.rsqrt(jnp.mean(x * x, axis=-1, keepdims=True) + EPS)


def _rms_bwd(x, g, dy):
    r = _rstd(x)
    xh = x * r
    dxh = dy * g
    dx = r * (dxh - xh * jnp.mean(dxh * xh, axis=-1, keepdims=True))
    return dx, dy * xh


def _accumulate(ref, val, step):
    @pl.when(step == 0)
    def _():
        ref[...] = val

    @pl.when(step > 0)
    def _():
        ref[...] += val


NN = ((1,), (0,))
NT = ((1,), (1,))
TN = ((0,), (0,))


def _matmul(name, a, b, *, grid, a_spec, b_spec, out_shape, out_specs, contract, nk=1, acc_shape=None,
            extras=(), extra_specs=(), epilogue=None, after=()):
    multi = isinstance(out_shape, (tuple, list))
    out_shapes = tuple(out_shape) if multi else (out_shape,)
    n_out = len(out_shapes)
    n_extra = len(extras)

    def body(a_ref, b_ref, *rest):
        x_refs = rest[:n_extra]
        o_refs = rest[n_extra:n_extra + n_out]

        def emit(acc):
            vals = epilogue(acc, *[r[...] for r in x_refs]) if epilogue else (acc,)
            for r, v in zip(o_refs, vals):
                r[...] = v.astype(r.dtype)

        p = lax.dot_general(a_ref[...], b_ref[...], (contract, ((), ())), preferred_element_type=F32)
        if nk == 1:
            emit(p)
        else:
            acc_ref = rest[n_extra + n_out]
            k = pl.program_id(2)
            _accumulate(acc_ref, p, k)

            @pl.when(k == nk - 1)
            def _():
                emit(acc_ref[...])

    sem = ("parallel", "parallel") + (("arbitrary",) if nk > 1 else ())
    return _call(
        body, name=name, grid=grid, after=after,
        in_specs=[a_spec, b_spec, *extra_specs],
        out_specs=out_specs,
        out_shape=out_shape,
        scratch_shapes=[pltpu.VMEM(acc_shape, F32)] if nk > 1 else [],
        compiler_params=_params(*sem),
    )(a, b, *extras)


def _fit(n, tile):
    if n <= tile:
        return n
    t = tile - tile % LANE
    while n % t:
        t -= LANE
    return t


def _mm_nn(name, a, b, out_dtype, tm, tn, after=()):
    m, k = a.shape
    n = b.shape[1]
    tm, tn = _fit(m, tm), _fit(n, tn)
    return _matmul(name, a, b, grid=(m // tm, n // tn), after=after,
                   a_spec=pl.BlockSpec((tm, k), lambda i, j: (i, 0)),
                   b_spec=pl.BlockSpec((k, tn), lambda i, j: (0, j)),
                   out_shape=jax.ShapeDtypeStruct((m, n), out_dtype),
                   out_specs=pl.BlockSpec((tm, tn), lambda i, j: (i, j)), contract=NN)


def _mm_nt(name, a, b, out_dtype, tm, tn, after=()):
    m, k = a.shape
    n = b.shape[0]
    tm, tn = _fit(m, tm), _fit(n, tn)
    return _matmul(name, a, b, grid=(m // tm, n // tn), after=after,
                   a_spec=pl.BlockSpec((tm, k), lambda i, j: (i, 0)),
                   b_spec=pl.BlockSpec((tn, k), lambda i, j: (j, 0)),
                   out_shape=jax.ShapeDtypeStruct((m, n), out_dtype),
                   out_specs=pl.BlockSpec((tm, tn), lambda i, j: (i, j)), contract=NT)


def _mm_tn(name, a, b, out_dtype, tm, tn):
    s, m = a.shape
    n = b.shape[1]
    tm, tn = _fit(m, tm), _fit(n, tn)
    return _matmul(name, a, b, grid=(m // tm, n // tn),
                   a_spec=pl.BlockSpec((s, tm), lambda i, j: (0, i)),
                   b_spec=pl.BlockSpec((s, tn), lambda i, j: (0, j)),
                   out_shape=jax.ShapeDtypeStruct((m, n), out_dtype),
                   out_specs=pl.BlockSpec((tm, tn), lambda i, j: (i, j)), contract=TN)


ROWS = 256


def _row_spec(rows, width):
    return pl.BlockSpec((rows, width), lambda i: (i, 0))


def _fixed_spec(rows, width):
    return pl.BlockSpec((rows, width), lambda i: (0, 0))


def _column_pieces(rows, start, width):
    piece = math.gcd(start, width)
    assert piece % LANE == 0
    return [pl.BlockSpec((rows, piece), lambda i, b=start // piece + p: (i, b)) for p in range(width // piece)]


def _rms_fwd(name, x, g, cols=None, after=()):
    s = x.shape[0]
    start, w = cols or (0, x.shape[1])
    rows = min(ROWS, s)
    pieces = _column_pieces(rows, start, w) if cols else [_row_spec(rows, w)]
    n = len(pieces)

    def body(*refs):
        g_ref, o_ref = refs[n:]
        xv = refs[0][...] if n == 1 else jnp.concatenate([r[...] for r in refs[:n]], axis=1)
        o_ref[...] = (xv * _rstd(xv) * g_ref[...]).astype(o_ref.dtype)

    return _call(
        body, name=name, grid=(s // rows,), after=after,
        in_specs=[*pieces, _fixed_spec(1, w)],
        out_specs=_row_spec(rows, w),
        out_shape=jax.ShapeDtypeStruct((s, w), BF16),
        compiler_params=_params("parallel"),
    )(*[x] * n, g)


def _rms_bwd_call(name, x, g, dy, out_dtype, cols=None, after=()):
    s = x.shape[0]
    start, w = cols or (0, x.shape[1])
    rows = min(ROWS, s)
    pieces = _column_pieces(rows, start, w) if cols else [_row_spec(rows, w)]
    n = len(pieces)

    def body(*refs):
        g_ref, dy_ref, dx_ref, dg_ref = refs[n:]
        xv = refs[0][...] if n == 1 else jnp.concatenate([r[...] for r in refs[:n]], axis=1)
        dx, dgc = _rms_bwd(xv, g_ref[...], dy_ref[...].astype(F32))
        dx_ref[...] = dx.astype(dx_ref.dtype)
        _accumulate(dg_ref, _sublane_sum(dgc), pl.program_id(0))

    return _call(
        body, name=name, grid=(s // rows,), after=after,
        in_specs=[*pieces, _fixed_spec(1, w), _row_spec(rows, w)],
        out_specs=[_row_spec(rows, w), _fixed_spec(SUBLANE, w)],
        out_shape=[jax.ShapeDtypeStruct((s, w), out_dtype), jax.ShapeDtypeStruct((SUBLANE, w), F32)],
        compiler_params=_params("arbitrary"),
    )(*[x] * n, g, dy)


def _norm_up(name, x, cols, g, w, after=()):
    s = x.shape[0]
    start, width = cols
    n = w.shape[1]
    tm = min(TILE_M, s)
    pieces = _column_pieces(tm, start, width)
    n_p = len(pieces)

    def body(*refs):
        g_ref, w_ref, xn_ref, o_ref = refs[n_p:]
        xv = refs[0][...] if n_p == 1 else jnp.concatenate([r[...] for r in refs[:n_p]], axis=1)
        xn = (xv * _rstd(xv) * g_ref[...]).astype(BF16)
        xn_ref[...] = xn
        o_ref[...] = jnp.dot(xn, w_ref[...], preferred_element_type=F32)

    return _call(
        body, name=name, grid=(s // tm,), after=after,
        in_specs=[*pieces, _fixed_spec(1, width), _fixed_spec(width, n)],
        out_specs=[_row_spec(tm, width), _row_spec(tm, n)],
        out_shape=[jax.ShapeDtypeStruct((s, width), BF16), jax.ShapeDtypeStruct((s, n), F32)],
        compiler_params=_params("parallel"),
    )(*[x] * n_p, g, w)


def _up_norm_bwd(name, dy, w, x, cols, g, after=()):
    s, n = dy.shape
    start, width = cols
    tm = min(TILE_M, s)
    pieces = _column_pieces(tm, start, width)
    n_p = len(pieces)

    def body(dy_ref, w_ref, *refs):
        g_ref, dx_ref, dg_ref = refs[n_p:]
        xv = refs[0][...] if n_p == 1 else jnp.concatenate([r[...] for r in refs[:n_p]], axis=1)
        dxn = lax.dot_general(dy_ref[...], w_ref[...], (NT, ((), ())), preferred_element_type=F32)
        dx, dgc = _rms_bwd(xv, g_ref[...], dxn)
        dx_ref[...] = dx.astype(dx_ref.dtype)
        _accumulate(dg_ref, _sublane_sum(dgc), pl.program_id(0))

    return _call(
        body, name=name, grid=(s // tm,), after=after,
        in_specs=[_row_spec(tm, n), _fixed_spec(width, n), *pieces, _fixed_spec(1, width)],
        out_specs=[_row_spec(tm, width), _fixed_spec(SUBLANE, width)],
        out_shape=[jax.ShapeDtypeStruct((s, width), BF16), jax.ShapeDtypeStruct((SUBLANE, width), F32)],
        compiler_params=_params("arbitrary"),
    )(dy, w, *[x] * n_p, g)


def _mid_fwd(x, y, g_post, g_pre, after=()):
    s, w = x.shape
    rows = min(ROWS, s)

    def body(x_ref, y_ref, gp_ref, gq_ref, x2_ref, h2_ref):
        yv = y_ref[...]
        x2 = x_ref[...] + yv * _rstd(yv) * gp_ref[...]
        x2_ref[...] = x2
        h2_ref[...] = (x2 * _rstd(x2) * gq_ref[...]).astype(h2_ref.dtype)

    return _call(
        body, name="mid_fwd", grid=(s // rows,), after=after,
        in_specs=[_row_spec(rows, w), _row_spec(rows, w), _fixed_spec(1, w), _fixed_spec(1, w)],
        out_specs=[_row_spec(rows, w), _row_spec(rows, w)],
        out_shape=[jax.ShapeDtypeStruct((s, w), F32), jax.ShapeDtypeStruct((s, w), BF16)],
        compiler_params=_params("parallel"),
    )(x, y, g_post, g_pre)


def _head(m, x2, tgt, g):
    s, w = m.shape
    rows = min(ROWS, s)

    def body(m_ref, x2_ref, t_ref, g_ref, dout_ref, dm_ref, dg_ref, loss_ref):
        mv = m_ref[...]
        gv = g_ref[...]
        out = x2_ref[...] + mv * _rstd(mv) * gv
        err = out - t_ref[...]
        dout = err * (1.0 / w)
        dout_ref[...] = dout
        dm, dgc = _rms_bwd(mv, gv, dout)
        dm_ref[...] = dm.astype(dm_ref.dtype)
        sq = err * err
        lanes = sq[:, 0:LANE]
        for j in range(1, w // LANE):
            lanes = lanes + sq[:, j * LANE:(j + 1) * LANE]
        step = pl.program_id(0)
        _accumulate(dg_ref, _sublane_sum(dgc), step)
        _accumulate(loss_ref, _sublane_sum(lanes) * (0.5 / w), step)

    return pl.pallas_call(
        body, name="head", grid=(s // rows,),
        in_specs=[_row_spec(rows, w), _row_spec(rows, w), _row_spec(rows, w), _fixed_spec(1, w)],
        out_specs=[_row_spec(rows, w), _row_spec(rows, w), _fixed_spec(SUBLANE, w), _fixed_spec(SUBLANE, LANE)],
        out_shape=[jax.ShapeDtypeStruct((s, w), F32), jax.ShapeDtypeStruct((s, w), BF16),
                   jax.ShapeDtypeStruct((SUBLANE, w), F32), jax.ShapeDtypeStruct((SUBLANE, LANE), F32)],
        compiler_params=_params("arbitrary"),
    )(m, x2, tgt, g)


def _mid_bwd(x2, y, d_out, d_h2, g_pre, g_post, after=()):
    s, w = x2.shape
    rows = min(ROWS, s)

    def body(x2_ref, y_ref, dout_ref, dh2_ref, gq_ref, gp_ref, dx2_ref, dy_ref, dgq_ref, dgp_ref):
        dx, dgq = _rms_bwd(x2_ref[...], gq_ref[...], dh2_ref[...])
        dx2 = dout_ref[...] + dx
        dx2_ref[...] = dx2
        dy, dgp = _rms_bwd(y_ref[...], gp_ref[...], dx2)
        dy_ref[...] = dy.astype(dy_ref.dtype)
        step = pl.program_id(0)
        _accumulate(dgq_ref, _sublane_sum(dgq), step)
        _accumulate(dgp_ref, _sublane_sum(dgp), step)

    return _call(
        body, name="mid_bwd", grid=(s // rows,), after=after,
        in_specs=[_row_spec(rows, w)] * 4 + [_fixed_spec(1, w)] * 2,
        out_specs=[_row_spec(rows, w), _row_spec(rows, w), _fixed_spec(SUBLANE, w), _fixed_spec(SUBLANE, w)],
        out_shape=[jax.ShapeDtypeStruct((s, w), F32), jax.ShapeDtypeStruct((s, w), BF16),
                   jax.ShapeDtypeStruct((SUBLANE, w), F32), jax.ShapeDtypeStruct((SUBLANE, w), F32)],
        compiler_params=_params("arbitrary"),
    )(x2, y, d_out, d_h2, g_pre, g_post)


def _first_bwd(x, g, d_h1, d_x2, after=()):
    s, w = x.shape
    rows = min(ROWS, s)

    def body(x_ref, g_ref, dh_ref, dx2_ref, dx_ref, dg_ref):
        dx, dgc = _rms_bwd(x_ref[...], g_ref[...], dh_ref[...])
        dx_ref[...] = dx2_ref[...] + dx
        _accumulate(dg_ref, _sublane_sum(dgc), pl.program_id(0))

    return _call(
        body, name="first_bwd", grid=(s // rows,), after=after,
        in_specs=[_row_spec(rows, w), _fixed_spec(1, w), _row_spec(rows, w), _row_spec(rows, w)],
        out_specs=[_row_spec(rows, w), _fixed_spec(SUBLANE, w)],
        out_shape=[jax.ShapeDtypeStruct((s, w), F32), jax.ShapeDtypeStruct((SUBLANE, w), F32)],
        compiler_params=_params("arbitrary"),
    )(x, g, d_h1, d_x2)


def _shift_down(v, k):
    t = lax.broadcasted_iota(jnp.int32, v.shape, 0)
    return jnp.where(t >= k, pltpu.roll(v, k, 0), 0.0)


def _shift_up(v, k):
    n = v.shape[0]
    t = lax.broadcasted_iota(jnp.int32, v.shape, 0)
    return jnp.where(t < n - k, pltpu.roll(v, n - k, 0), 0.0)


def _conv_core(u, b, c, w):
    z = c * u
    conv = w[0:1, :] * _shift_down(z, 2) + w[1:2, :] * _shift_down(z, 1) + w[2:3, :] * z
    return z, conv, b * conv


def _conv_fwd(proj, conv_w, g, n_groups, out_width, after=()):
    s = proj.shape[0]

    def body(u_ref, b_ref, c_ref, w_ref, g_ref, o_ref):
        _, _, yr = _conv_core(u_ref[...], b_ref[...], c_ref[...], w_ref[...])
        o_ref[...] = (yr * _rstd(yr) * g_ref[...]).astype(o_ref.dtype)

    col = lambda k: pl.BlockSpec((s, HEAD), lambda i: (0, k * n_groups + i))
    return _call(
        body, name="conv_fwd", grid=(n_groups,), after=after,
        in_specs=[col(0), col(1), col(2), pl.BlockSpec((3, HEAD), lambda i: (0, i)), pl.BlockSpec((1, HEAD), lambda i: (0, i))],
        out_specs=pl.BlockSpec((s, HEAD), lambda i: (0, i)),
        out_shape=jax.ShapeDtypeStruct((s, out_width), BF16),
        compiler_params=_params("parallel"),
    )(proj, proj, proj, conv_w, g)


def _conv_bwd(proj, d_mix, conv_w, g, n_groups):
    s = proj.shape[0]
    width = n_groups * HEAD

    def body(u_ref, b_ref, c_ref, dy_ref, w_ref, g_ref, du_ref, db_ref, dc_ref, dg_ref, dw_ref):
        u, b, c, w = u_ref[...], b_ref[...], c_ref[...], w_ref[...]
        z, conv, yr = _conv_core(u, b, c, w)
        dyr, dgc = _rms_bwd(yr, g_ref[...], dy_ref[...])
        dconv = dyr * b
        db_ref[...] = (dyr * conv).astype(db_ref.dtype)
        dz = w[2:3, :] * dconv + w[1:2, :] * _shift_up(dconv, 1) + w[0:1, :] * _shift_up(dconv, 2)
        dc_ref[...] = (dz * u).astype(dc_ref.dtype)
        du_ref[...] = (dz * c).astype(du_ref.dtype)
        dg_ref[...] = _sublane_sum(dgc)
        dw_ref[0] = _sublane_sum(dconv * _shift_down(z, 2))
        dw_ref[1] = _sublane_sum(dconv * _shift_down(z, 1))
        dw_ref[2] = _sublane_sum(dconv * z)

    col = lambda k: pl.BlockSpec((s, HEAD), lambda i: (0, k * n_groups + i))
    grp = pl.BlockSpec((s, HEAD), lambda i: (0, i))
    return pl.pallas_call(
        body, name="conv_bwd", grid=(n_groups,),
        in_specs=[col(0), col(1), col(2), grp, pl.BlockSpec((3, HEAD), lambda i: (0, i)), pl.BlockSpec((1, HEAD), lambda i: (0, i))],
        out_specs=[grp, grp, grp, pl.BlockSpec((SUBLANE, HEAD), lambda i: (0, i)),
                   pl.BlockSpec((3, SUBLANE, HEAD), lambda i: (0, 0, i))],
        out_shape=[jax.ShapeDtypeStruct((s, width), BF16)] * 3
        + [jax.ShapeDtypeStruct((SUBLANE, width), F32), jax.ShapeDtypeStruct((3, SUBLANE, width), F32)],
        compiler_params=_params("parallel"),
    )(proj, proj, proj, d_mix, conv_w, g)


def _rope_tables(s, n_heads):
    pos = jnp.arange(s, dtype=F32)
    inv_freq = jnp.power(ROPE_THETA, -jnp.arange(0, ROPE, 2, dtype=F32) / ROPE)
    ang = pos[:, None] * inv_freq[None, :]
    cos, sin = jnp.cos(ang), jnp.sin(ang)
    cs = jnp.concatenate([cos, cos], axis=1)
    sn = jnp.concatenate([-sin, sin], axis=1)
    pad = jnp.zeros((s, LANE - ROPE), F32)
    return (jnp.tile(cs, (1, n_heads)), jnp.tile(sn, (1, n_heads)),
            jnp.concatenate([cs, pad], axis=1), jnp.concatenate([sn, pad], axis=1))


def _swap_halves(v):
    w = v.shape[1]
    lane = lax.broadcasted_iota(jnp.int32, v.shape, 1)
    first = (lane % ROPE) < (ROPE // 2)
    return jnp.where(first, pltpu.roll(v, w - ROPE // 2, 1), pltpu.roll(v, ROPE // 2, 1))


def _pack_heads(q, kv, proj, kr_col, tables, n_heads, after=()):
    s = q.shape[0]
    rows = min(ROWS, s)
    cq, sq, ck, sk = tables
    wq = n_heads * ROPE

    def body(q_ref, kv_ref, kr_ref, cq_ref, sq_ref, ck_ref, sk_ref, qo_ref, ko_ref, vo_ref):
        qr = q_ref[:, n_heads * HEAD:]
        qr = qr * cq_ref[...] + _swap_halves(qr) * sq_ref[...]
        krv = kr_ref[...]
        krv = krv * ck_ref[...] + _swap_halves(krv) * sk_ref[...]
        for h in range(n_heads):
            qo_ref[h] = jnp.concatenate([q_ref[:, h * HEAD:(h + 1) * HEAD], qr[:, h * ROPE:(h + 1) * ROPE]], axis=1).astype(BF16)
            ko_ref[h] = jnp.concatenate([kv_ref[:, 2 * h * HEAD:(2 * h + 1) * HEAD], krv[:, :ROPE]], axis=1).astype(BF16)
            vo_ref[h] = kv_ref[:, (2 * h + 1) * HEAD:(2 * h + 2) * HEAD].astype(BF16)

    hs = lambda w: pl.BlockSpec((n_heads, rows, w), lambda i: (0, i, 0))
    return _call(
        body, name="pack_heads", grid=(s // rows,), after=after,
        in_specs=[_row_spec(rows, q.shape[1]), _row_spec(rows, kv.shape[1]), pl.BlockSpec((rows, LANE), lambda i: (i, kr_col // LANE)),
                  _row_spec(rows, wq), _row_spec(rows, wq), _row_spec(rows, LANE), _row_spec(rows, LANE)],
        out_specs=[hs(QK), hs(QK), hs(HEAD)],
        out_shape=[jax.ShapeDtypeStruct((n_heads, s, QK), BF16), jax.ShapeDtypeStruct((n_heads, s, QK), BF16),
                   jax.ShapeDtypeStruct((n_heads, s, HEAD), BF16)],
        compiler_params=_params("parallel"),
    )(q, kv, proj, cq, sq, ck, sk)


def _unpack_heads(dq, dk, dv, tables, n_heads):
    s = dq.shape[1]
    rows = min(ROWS, s)
    cq, sq, ck, sk = tables
    wq = n_heads * ROPE

    def body(dq_ref, dk_ref, dv_ref, cq_ref, sq_ref, ck_ref, sk_ref, qo_ref, kvo_ref, kro_ref):
        dqr = jnp.concatenate([dq_ref[h][:, HEAD:] for h in range(n_heads)], axis=1)
        dqr = dqr * cq_ref[...] - _swap_halves(dqr) * sq_ref[...]
        dkr = dk_ref[0][:, HEAD:]
        for h in range(1, n_heads):
            dkr = dkr + dk_ref[h][:, HEAD:]
        dkr = jnp.concatenate([dkr, jnp.zeros((rows, LANE - ROPE), F32)], axis=1)
        dkr = dkr * ck_ref[...] - _swap_halves(dkr) * sk_ref[...]
        kro_ref[...] = dkr.astype(kro_ref.dtype)
        qo_ref[:, n_heads * HEAD:] = dqr.astype(qo_ref.dtype)
        for h in range(n_heads):
            qo_ref[:, h * HEAD:(h + 1) * HEAD] = dq_ref[h][:, :HEAD].astype(qo_ref.dtype)
            kvo_ref[:, 2 * h * HEAD:(2 * h + 1) * HEAD] = dk_ref[h][:, :HEAD].astype(kvo_ref.dtype)
            kvo_ref[:, (2 * h + 1) * HEAD:(2 * h + 2) * HEAD] = dv_ref[h].astype(kvo_ref.dtype)

    hs = lambda w: pl.BlockSpec((n_heads, rows, w), lambda i: (0, i, 0))
    return pl.pallas_call(
        body, name="unpack_heads", grid=(s // rows,),
        in_specs=[hs(QK), hs(QK), hs(HEAD), _row_spec(rows, wq), _row_spec(rows, wq), _row_spec(rows, LANE), _row_spec(rows, LANE)],
        out_specs=[_row_spec(rows, n_heads * QK), _row_spec(rows, 2 * n_heads * HEAD), _row_spec(rows, LANE)],
        out_shape=[jax.ShapeDtypeStruct((s, n_heads * QK), BF16), jax.ShapeDtypeStruct((s, 2 * n_heads * HEAD), BF16),
                   jax.ShapeDtypeStruct((s, LANE), BF16)],
        compiler_params=_params("parallel"),
    )(dq, dk, dv, cq, sq, ck, sk)


TQ = 256


LOG2_E = 1.4426950408889634


def _softmax_parts(q, k):
    tq, n_keys = q.shape[0], k.shape[0]
    sc = lax.dot_general(q, k, (NT, ((), ())), preferred_element_type=F32) * (QK ** -0.5 * LOG2_E)
    row = lax.broadcasted_iota(jnp.int32, (tq, tq), 0)
    col = lax.broadcasted_iota(jnp.int32, (tq, tq), 1)
    own = jnp.where(col // CHUNK <= row // CHUNK, sc[:, n_keys - tq:], NEG_INF)
    sc = own if n_keys == tq else jnp.concatenate([sc[:, :n_keys - tq], own], axis=1)
    e = jnp.exp2(sc - jnp.max(sc, axis=-1, keepdims=True))
    return e, 1.0 / jnp.sum(e, axis=-1, keepdims=True)


def _attn_fwd(q, k, v, g, mix, col0):
    n_heads, s, _ = q.shape
    tq = min(TQ, s)
    assert tq % CHUNK == 0 and s % tq == 0

    def body(q_ref, k_ref, v_ref, g_ref, mix_ref, o_ref, y_ref):
        for c in range(s // tq):
            rows, n_keys = pl.ds(c * tq, tq), (c + 1) * tq
            e, inv = _softmax_parts(q_ref[rows, :], k_ref[0:n_keys, :])
            o = jnp.dot(e.astype(BF16), v_ref[0:n_keys, :], preferred_element_type=F32) * inv
            o_ref[rows, :] = o
            y_ref[rows, :] = (o * _rstd(o) * g_ref[...]).astype(y_ref.dtype)

    head = lambda w: pl.BlockSpec((None, s, w), lambda h: (h, 0, 0))
    return pl.pallas_call(
        body, name="attn_fwd", grid=(n_heads,),
        in_specs=[head(QK), head(QK), head(HEAD), pl.BlockSpec((1, HEAD), lambda h: (0, h)), ANY],
        out_specs=[head(HEAD), pl.BlockSpec((s, HEAD), lambda h: (0, col0 // HEAD + h))],
        out_shape=[jax.ShapeDtypeStruct((n_heads, s, HEAD), F32), jax.ShapeDtypeStruct(mix.shape, mix.dtype)],
        input_output_aliases={4: 1},
        compiler_params=_params("parallel"),
    )(q, k, v, g, mix)


def _attn_bwd(q, k, v, o, d_mix, g, col0, after=()):
    n_heads, s, _ = q.shape
    tq = min(TQ, s)

    def body(q_ref, k_ref, v_ref, o_ref, dy_ref, g_ref, dq_ref, dk_ref, dv_ref, dg_ref):
        dg = None
        for c in reversed(range(s // tq)):
            rows, n_keys = pl.ds(c * tq, tq), (c + 1) * tq
            qv, kv_, vv = q_ref[rows, :], k_ref[0:n_keys, :], v_ref[0:n_keys, :]
            do, dgc = _rms_bwd(o_ref[rows, :], g_ref[...], dy_ref[rows, :])
            do = do.astype(BF16)
            dg = _sublane_sum(dgc) if dg is None else dg + _sublane_sum(dgc)
            e, inv = _softmax_parts(qv, kv_)
            p = e * inv
            dp = lax.dot_general(do, vv, (NT, ((), ())), preferred_element_type=F32)
            ds = (p * (dp - jnp.sum(p * dp, axis=-1, keepdims=True)) * (QK ** -0.5)).astype(BF16)
            dq_ref[rows, :] = jnp.dot(ds, kv_, preferred_element_type=F32)
            dk = lax.dot_general(ds, qv, (TN, ((), ())), preferred_element_type=F32)
            dv = lax.dot_general(p.astype(BF16), do, (TN, ((), ())), preferred_element_type=F32)
            if n_keys == s:
                dk_ref[...] = dk
                dv_ref[...] = dv
            else:
                dk_ref[0:n_keys, :] += dk
                dv_ref[0:n_keys, :] += dv
        dg_ref[...] = dg

    c0 = col0 // HEAD
    head = lambda w: pl.BlockSpec((None, s, w), lambda h, *_: (h, 0, 0))
    in_specs = [head(QK), head(QK), head(HEAD), head(HEAD), pl.BlockSpec((s, HEAD), lambda h, *_: (0, c0 + h)),
                pl.BlockSpec((1, HEAD), lambda h, *_: (0, h))]
    out_specs = [head(QK), head(QK), head(HEAD), pl.BlockSpec((SUBLANE, HEAD), lambda h, *_: (0, h))]
    out_shape = [jax.ShapeDtypeStruct((n_heads, s, QK), F32), jax.ShapeDtypeStruct((n_heads, s, QK), F32),
                 jax.ShapeDtypeStruct((n_heads, s, HEAD), F32), jax.ShapeDtypeStruct((SUBLANE, n_heads * HEAD), F32)]
    return _call(body, name="attn_bwd", grid=(n_heads,), after=after, in_specs=in_specs, out_specs=out_specs,
                 out_shape=out_shape, compiler_params=_params("parallel"))(q, k, v, o, d_mix, g)


TILE_M = 1024
TILE_N = 1024


def _up_fwd(h2, w_up):
    s, d = h2.shape
    nb, _, fb = w_up.shape
    tm = min(TILE_M,s)

    def epilogue(acc):
        r = jnp.maximum(acc, 0.0)
        return r * r, r

    blk = pl.BlockSpec((tm, fb), lambda i, j: (i, j))
    return _matmul("up_fwd", h2, w_up, grid=(s // tm, nb),
                   a_spec=pl.BlockSpec((tm, d), lambda i, j: (i, 0)),
                   b_spec=pl.BlockSpec((None, d, fb), lambda i, j: (j, 0, 0)),
                   out_shape=[jax.ShapeDtypeStruct((s, nb * fb), BF16)] * 2, out_specs=[blk, blk],
                   contract=NN, epilogue=epilogue)


def _down_fwd(a, w_down):
    s, f = a.shape
    d = w_down.shape[1]
    tm, tn, tk = min(TILE_M,s), min(TILE_N,d), 2048
    nk = f // tk
    return _matmul("down_fwd", a, w_down, grid=(s // tm, d // tn, nk),
                   a_spec=pl.BlockSpec((tm, tk), lambda i, j, k: (i, k)),
                   b_spec=pl.BlockSpec((tk, tn), lambda i, j, k: (k, j)),
                   out_shape=jax.ShapeDtypeStruct((s, d), F32),
                   out_specs=pl.BlockSpec((tm, tn), lambda i, j, k: (i, j)),
                   contract=NN, nk=nk, acc_shape=(tm, tn))


def _down_bwd_act(d_m, w_down, r, after=()):
    s, d = d_m.shape
    f = w_down.shape[0]
    tm, tn = min(TILE_M,s), min(TILE_N,f)
    blk = pl.BlockSpec((tm, tn), lambda i, j: (i, j))
    return _matmul("down_bwd_act", d_m, w_down, grid=(s // tm, f // tn), after=after,
                   a_spec=pl.BlockSpec((tm, d), lambda i, j: (i, 0)),
                   b_spec=pl.BlockSpec((tn, d), lambda i, j: (j, 0)),
                   out_shape=jax.ShapeDtypeStruct((s, f), BF16), out_specs=blk, contract=NT,
                   extras=(r,), extra_specs=(blk,),
                   epilogue=lambda acc, rv: (acc * (2.0 * rv.astype(F32)),))


def _up_bwd_act(d_up, w_up, after=()):
    s, _ = d_up.shape
    nb, d, fb = w_up.shape
    tm, tn = min(TILE_M, s), min(TILE_N,d)
    pair = 2
    n_after = len(after)

    def body(a_ref, w_ref, *rest):
        o_ref, acc_ref = rest[n_after:]
        k = pl.program_id(2)
        p = None
        for t in range(pair):
            term = lax.dot_general(a_ref[:, t * fb:(t + 1) * fb], w_ref[t], (NT, ((), ())), preferred_element_type=F32)
            p = term if p is None else p + term
        _accumulate(acc_ref, p, k)

        @pl.when(k == nb // pair - 1)
        def _():
            o_ref[...] = acc_ref[...]

    return pl.pallas_call(
        body, name="up_bwd_act", grid=(s // tm, d // tn, nb // pair),
        in_specs=[pl.BlockSpec((tm, pair * fb), lambda i, j, k: (i, k)),
                  pl.BlockSpec((pair, tn, fb), lambda i, j, k: (k, j, 0))] + [ANY] * n_after,
        out_specs=pl.BlockSpec((tm, tn), lambda i, j, k: (i, j)),
        out_shape=jax.ShapeDtypeStruct((s, d), F32),
        scratch_shapes=[pltpu.VMEM((tm, tn), F32)],
        compiler_params=_params("parallel", "parallel", "arbitrary"),
    )(d_up, w_up, *after)


def _half_grad(name, a, b, core, home, received, after, *, grid, a_block, a_map, b_block, b_map, o_block, o_map, out_shape):
    n_after = len(after)
    pick = (lambda ref: ref[0]) if home else (lambda ref: 1 - ref[0])

    def body(core_ref, a_ref, b_ref, *rest):
        acc = lax.dot_general(a_ref[...], b_ref[...], (TN, ((), ())), preferred_element_type=F32)
        if received is not None:
            acc = acc + rest[0][...].astype(F32)
        rest[-1][...] = acc.astype(rest[-1].dtype)

    wrap = lambda fn: (lambda i, j, core_ref: fn(i, j, pick(core_ref)))
    o_spec = pl.BlockSpec(o_block, wrap(o_map))
    extra = [] if received is None else [o_spec]
    operands = [] if received is None else [received]
    return pl.pallas_call(
        body, name=name,
        grid_spec=pltpu.PrefetchScalarGridSpec(
            num_scalar_prefetch=1, grid=grid,
            in_specs=[pl.BlockSpec(a_block, wrap(a_map)), pl.BlockSpec(b_block, wrap(b_map))] + extra + [ANY] * n_after,
            out_specs=o_spec),
        out_shape=out_shape,
        compiler_params=_params("parallel", "parallel"),
    )(core, a, b, *operands, *after)


def _down_half_grad(name, a, d_m, core, home, received=None, after=()):
    s, f = a.shape
    d = d_m.shape[1]
    r = f // N_DEV
    tn = min(TILE_N, d)
    return _half_grad(name, a, d_m, core, home, received, after, grid=(N_CHIP, d // tn),
                      a_block=(s, r), a_map=lambda k, j, p: (0, 2 * k + p),
                      b_block=(s, tn), b_map=lambda k, j, p: (0, j),
                      o_block=(None, r, tn), o_map=lambda k, j, p: (k, 0, j),
                      out_shape=jax.ShapeDtypeStruct((N_CHIP, r, d), BF16))


def _up_half_grad(name, h2, d_up, core, home, received=None, after=()):
    s, d = h2.shape
    fb = d_up.shape[1] // N_DEV
    tm = min(TILE_M, d)
    return _half_grad(name, h2, d_up, core, home, received, after, grid=(d // tm, N_CHIP),
                      a_block=(s, tm), a_map=lambda i, k, p: (0, i),
                      b_block=(s, fb), b_map=lambda i, k, p: (0, 2 * k + p),
                      o_block=(None, tm, fb), o_map=lambda i, k, p: (k, i, 0),
                      out_shape=jax.ShapeDtypeStruct((N_CHIP, d, fb), BF16))


def _in_pad(in_width):
    return -(-in_width // LANE) * LANE


def _join_col_shards(name, blocks, own, device, pieces=None):
    n, r, w = blocks.shape
    rows = min(ROWS, r)
    pieces = pieces or [(j, 0, w) for j in range(n)]
    used = sum(b - a for _, a, b in pieces)
    width = _in_pad(used)

    def body(dev_ref, x_ref, own_ref, o_ref):
        block = lambda j: jnp.where(dev_ref[0] == j, own_ref[...], x_ref[j])
        cols = [block(j)[:, a:b] for j, a, b in pieces]
        tail = [jnp.zeros((rows, width - used), o_ref.dtype)] if width > used else []
        o_ref[...] = jnp.concatenate(cols + tail, axis=1)

    return pl.pallas_call(
        body, name=name,
        grid_spec=pltpu.PrefetchScalarGridSpec(
            num_scalar_prefetch=1, grid=(r // rows,),
            in_specs=[pl.BlockSpec((n, rows, w), lambda i, dev: (0, i, 0)), pl.BlockSpec((rows, w), lambda i, dev: (i, 0))],
            out_specs=pl.BlockSpec((rows, width), lambda i, dev: (i, 0))),
        out_shape=jax.ShapeDtypeStruct((r, width), blocks.dtype),
        compiler_params=_params("parallel"),
    )(device, blocks, own)


def _permute_q_cols(w_uq, n_heads):
    r = w_uq.shape[0]
    w3 = w_uq.reshape(r, n_heads, QK)
    return jnp.concatenate([w3[:, :, :HEAD].reshape(r, n_heads * HEAD), w3[:, :, HEAD:].reshape(r, n_heads * ROPE)], axis=1)


def _unpermute_q_rows(wt, n_heads):
    r = wt.shape[1]
    nope = wt[:n_heads * HEAD].reshape(n_heads, HEAD, r)
    rope = wt[n_heads * HEAD:].reshape(n_heads, ROPE, r)
    return jnp.concatenate([nope, rope], axis=1).reshape(n_heads * QK, r)


def _local_step(x, tgt, gains, weights, grads, first_after=()):
    pre_mix_g, q_norm_g, kv_norm_g, conv_out_g, attn_out_g, post_mix_g, pre_mlp_g, post_mlp_g = gains
    s, d = x.shape
    conv_width = conv_out_g.shape[1]
    n_groups = conv_width // HEAD
    r_q, r_kv = q_norm_g.shape[1], kv_norm_g.shape[1]
    n_heads = attn_out_g.shape[1] // HEAD
    c_q0 = 3 * conv_width
    c_kv0 = c_q0 + r_q
    c_kr0 = c_kv0 + r_kv
    in_pad = _in_pad(c_kr0 + ROPE)
    tn_in = in_pad // 5 if in_pad % (5 * LANE) == 0 else LANE
    tables = _rope_tables(s, n_heads)

    h1 = _rms_fwd("pre_mix_norm", x, pre_mix_g, after=first_after)
    weights.forward(0, (h1,))
    weights.relay(0, tables)
    w_in_p, conv_w = weights.ready(0, ())
    proj = _mm_nn("in_proj", h1, w_in_p, F32, TILE_M, tn_in)
    y_conv = _conv_fwd(proj, conv_w, conv_out_g, n_groups, conv_width + n_heads * HEAD, after=weights.forward(1, (proj,)))
    w_uq_p, w_ukv, w_o = weights.ready(1, (y_conv,))
    qn, q = _norm_up("q_up", proj, (c_q0, r_q), q_norm_g, w_uq_p)
    kvn, kv = _norm_up("kv_up", proj, (c_kv0, r_kv), kv_norm_g, w_ukv)
    qh, kh, vh = _pack_heads(q, kv, proj, c_kr0, tables, n_heads, after=weights.forward(2, (q, kv)))
    o, mix = _attn_fwd(qh, kh, vh, attn_out_g, y_conv, conv_width)
    y = _mm_nn("out_proj", mix, w_o, F32, TILE_M, TILE_N)
    x2, h2 = _mid_fwd(x, y, post_mix_g, pre_mlp_g, after=weights.forward(3, (y,)))
    weights.relay(2, (h2,))
    (w_up,) = weights.ready(2, ())
    a, r = _up_fwd(h2, w_up)
    weights.relay(3, (a,))
    (w_down,) = weights.ready(3, ())
    m = _down_fwd(a, w_down)

    d_out, d_m, dg_post_mlp, loss_part = _head(m, x2, tgt, post_mlp_g)
    core = grads.core
    away = _down_half_grad("down_bwd_w_away", a, d_m, core, home=False)
    d_up = _down_bwd_act(d_m, w_down, r, after=grads.send_away(0, away))
    sums = _down_half_grad("down_bwd_w_home", a, d_m, core, home=True, received=grads.received(0, (d_up,)))
    away = _up_half_grad("up_bwd_w_away", h2, d_up, core, home=False, after=grads.send_sums(0, (sums,)))
    d_h2 = _up_bwd_act(d_up, w_up, after=grads.send_away(1, away))
    sums = _up_half_grad("up_bwd_w_home", h2, d_up, core, home=True, received=grads.received(1, (d_h2,)))
    d_x2, d_y, dg_pre_mlp, dg_post_mix = _mid_bwd(x2, y, d_out, d_h2, pre_mlp_g, post_mix_g, after=grads.send_sums(1, (sums,)))
    d_mix = _mm_nt("out_proj_bwd_act", d_y, w_o, F32, TILE_M, TILE_N)
    gw_o = _mm_tn("out_proj_bwd_w", mix, d_y, BF16, TILE_M, TILE_N)
    dqh, dkh, dvh, dg_attn = _attn_bwd(qh, kh, vh, o, d_mix, attn_out_g, conv_width, after=grads.full(2, (gw_o,)))
    d_q, d_kv, d_kr = _unpack_heads(dqh, dkh, dvh, tables, n_heads)
    gw_uq_t = _mm_tn("q_up_bwd_w", d_q, qn, F32, TILE_M, TILE_N)
    gw_ukv = _mm_tn("kv_up_bwd_w", kvn, d_kv, BF16, TILE_M, TILE_N)
    d_cq, dg_q = _up_norm_bwd("q_up_bwd_act", d_q, w_uq_p, proj, (c_q0, r_q), q_norm_g, after=grads.full(3, (gw_uq_t, gw_ukv)))
    d_ckv, dg_kv = _up_norm_bwd("kv_up_bwd_act", d_kv, w_ukv, proj, (c_kv0, r_kv), kv_norm_g)
    d_u, d_b, d_c, dg_conv, dw_conv = _conv_bwd(proj, d_mix, conv_w, conv_out_g, n_groups)
    d_proj = jnp.concatenate([d_u, d_b, d_c, d_cq, d_ckv, d_kr[:, :in_pad - c_kr0]], axis=1)
    gw_in_t = _mm_tn("in_proj_bwd_w", d_proj, h1, F32, tn_in, TILE_N)
    updated = grads.update_now(0, grads.send_away(4, gw_in_t))
    d_h1 = _mm_nt("in_proj_bwd_act", d_proj, w_in_p, F32, TILE_M, TILE_N, after=grads.full(4, (gw_in_t,), received=updated))
    grad_x, dg_pre_mix = _first_bwd(x, pre_mix_g, d_h1, d_x2)

    small = [dg_pre_mix, dg_q, dg_kv, dg_conv, dg_attn, dg_post_mix, dg_pre_mlp, dg_post_mlp,
             dw_conv[0], dw_conv[1], dw_conv[2], loss_part]
    return grad_x, jnp.concatenate(small, axis=1)


HBM = pl.BlockSpec(memory_space=pltpu.HBM)
SEM = pl.BlockSpec(memory_space=pltpu.SEMAPHORE)
IN_VMEM = pl.BlockSpec(memory_space=pltpu.VMEM)
SPLIT = pltpu.CompilerParams(has_side_effects=pltpu.SideEffectType.DATAFLOW_SIDE_EFFECTING)


def _in_hbm(a):
    return pltpu.with_memory_space_constraint(a, pltpu.HBM)


def _hbm_like(a):
    return pltpu.HBM(a.shape, a.dtype)


def _place():
    x, y, c = lax.axis_index("x"), lax.axis_index("y"), lax.axis_index("c")
    other_chips = [(1 - x, y), (x, 1 - y), (1 - x, 1 - y)]
    return x, y, c, other_chips


def _block(px, py, pc):
    return 4 * px + 2 * py + pc


def _await(block, sem):
    pltpu.make_async_copy(block, block, sem).wait()


def _relay_route(x, y, c):
    came_from = ((1 - x) * (1 - c) + x * c, y * (1 - c) + (1 - y) * c)
    goes_to = (x * (1 - c) + (1 - x) * c, (1 - y) * (1 - c) + y * c)
    return came_from, goes_to


def _gather_start(name, shards, groups, relayed=(), after=()):
    n, ng = len(shards), len(groups)
    lands = [lax.empty((N_DEV, *a.shape), a.dtype) for a in shards]

    def body(*refs):
        src, land = refs[:n], refs[n:2 * n]
        sems, token = refs[2 * n + len(after):2 * n + len(after) + 2 * ng], refs[-1]
        x, y, c, chips = _place()
        targets = [(x, y, 1 - c)] + [(*chip, c) for chip in chips]
        for gi, group in enumerate(groups):
            for i, w in enumerate(group):
                for k, to in enumerate(targets[:3] if gi in relayed else targets):
                    pltpu.make_async_remote_copy(
                        src_ref=src[w], dst_ref=land[w].at[_block(x, y, c)],
                        send_sem=sems[2 * gi].at[4 * i + k], recv_sem=sems[2 * gi + 1].at[4 * i + k],
                        device_id=to, device_id_type=MESH).start()
        token[...] = jnp.zeros_like(token)

    sem_shapes = [pltpu.SemaphoreType.DMA((4 * len(g),)) for g in groups for _ in range(2)]
    out = pl.pallas_call(
        body, name=name,
        in_specs=[HBM] * (2 * n) + [ANY] * len(after),
        out_specs=[SEM] * (2 * ng) + [HBM] * (2 * n) + [IN_VMEM],
        out_shape=sem_shapes + [_hbm_like(a) for a in shards] + [_hbm_like(a) for a in lands]
        + [jax.ShapeDtypeStruct((SUBLANE, LANE), F32)],
        input_output_aliases={i: 2 * ng + i for i in range(2 * n)},
        compiler_params=SPLIT,
    )(*[_in_hbm(a) for a in shards], *[_in_hbm(a) for a in lands], *after)
    sems = [(out[2 * gi], out[2 * gi + 1]) for gi in range(ng)]
    return sems, out[2 * ng:2 * ng + n], out[2 * ng + n:2 * ng + 2 * n], out[-1]


def _gather_forward(name, shards, lands, send1, recv1, after, relayed=False):
    n = len(lands)

    def body(*refs):
        src, land = refs[:n], refs[n:2 * n]
        s1, r1 = refs[2 * n], refs[2 * n + 1]
        s2, r2 = refs[2 * n + 2 + len(after)], refs[2 * n + 3 + len(after)]
        x, y, c, chips = _place()
        me, sibling = (x, y, c), (x, y, 1 - c)
        for j, chip in enumerate(chips[:2] if relayed else chips):
            for i in range(n):
                blk = land[i].at[_block(*chip, c)]
                pltpu.make_async_remote_copy(src_ref=blk, dst_ref=blk, send_sem=s1.at[4 * i + 1 + j], recv_sem=r1.at[4 * i + 1 + j],
                                             device_id=me, device_id_type=MESH).wait_recv()
                pltpu.make_async_remote_copy(src_ref=blk, dst_ref=blk, send_sem=s2.at[3 * i + j], recv_sem=r2.at[3 * i + j],
                                             device_id=sibling, device_id_type=MESH).start()
        if relayed:
            came_from, goes_to = _relay_route(x, y, c)
            for i in range(n):
                blk = land[i].at[_block(*came_from, c)]
                pltpu.make_async_remote_copy(src_ref=blk, dst_ref=blk, send_sem=s2.at[3 * i + 2], recv_sem=r2.at[3 * i + 2],
                                             device_id=(*goes_to, c), device_id_type=MESH).start()
        for i in range(n):
            blk = land[i].at[_block(x, y, 1 - c)]
            pltpu.make_async_remote_copy(src_ref=blk, dst_ref=blk, send_sem=s1.at[4 * i], recv_sem=r1.at[4 * i],
                                         device_id=me, device_id_type=MESH).wait_recv()
            for k in range(3 if relayed else 4):
                pltpu.make_async_remote_copy(src_ref=src[i], dst_ref=land[i].at[_block(x, y, c)], send_sem=s1.at[4 * i + k],
                                             recv_sem=r1.at[4 * i + k], device_id=sibling, device_id_type=MESH).wait_send()

    sem = pltpu.SemaphoreType.DMA((3 * n,))
    out = pl.pallas_call(
        body, name=name,
        in_specs=[HBM] * (2 * n) + [SEM, SEM] + [ANY] * len(after),
        out_specs=[SEM, SEM] + [HBM] * n,
        out_shape=[sem, sem] + [_hbm_like(a) for a in lands],
        input_output_aliases={n + i: 2 + i for i in range(n)},
        compiler_params=SPLIT,
    )(*shards, *lands, send1, recv1, *after)
    return (out[0], out[1]), out[2:]


def _gather_relay_forward(name, lands, send2, recv2, after):
    n = len(lands)

    def body(*refs):
        land, s2, r2 = refs[:n], refs[n], refs[n + 1]
        s3, r3 = refs[n + 2 + len(after)], refs[n + 3 + len(after)]
        x, y, c, _ = _place()
        me, sibling = (x, y, c), (x, y, 1 - c)
        came_from, _ = _relay_route(x, y, c)
        for i in range(n):
            blk = land[i].at[_block(1 - x, 1 - y, c)]
            pltpu.make_async_remote_copy(src_ref=blk, dst_ref=blk, send_sem=s2.at[3 * i + 2], recv_sem=r2.at[3 * i + 2],
                                         device_id=me, device_id_type=MESH).wait_recv()
            pltpu.make_async_remote_copy(src_ref=blk, dst_ref=blk, send_sem=s3.at[i], recv_sem=r3.at[i],
                                         device_id=sibling, device_id_type=MESH).start()
            sent = land[i].at[_block(*came_from, c)]
            pltpu.make_async_remote_copy(src_ref=sent, dst_ref=sent, send_sem=s2.at[3 * i + 2], recv_sem=r2.at[3 * i + 2],
                                         device_id=me, device_id_type=MESH).wait_send()

    sem = pltpu.SemaphoreType.DMA((n,))
    out = pl.pallas_call(
        body, name=name,
        in_specs=[HBM] * n + [SEM, SEM] + [ANY] * len(after),
        out_specs=[SEM, SEM] + [HBM] * n,
        out_shape=[sem, sem] + [_hbm_like(a) for a in lands],
        input_output_aliases={i: 2 + i for i in range(n)},
        compiler_params=SPLIT,
    )(*lands, send2, recv2, *after)
    return (out[0], out[1]), out[2:]


def _gather_wait(name, lands, send2, recv2, after, relay_sems=None):
    n = len(lands)
    n_sems = 2 if relay_sems is None else 4

    def body(*refs):
        land, s2, r2 = refs[:n], refs[n], refs[n + 1]
        for i in range(n):
            for j in range(3 if relay_sems is None else 2):
                _await(land[i].at[0], r2.at[3 * i + j])
                _await(land[i].at[0], s2.at[3 * i + j])
            if relay_sems is not None:
                _await(land[i].at[0], refs[n + 3].at[i])
                _await(land[i].at[0], refs[n + 2].at[i])

    return pl.pallas_call(
        body, name=name,
        in_specs=[HBM] * n + [SEM] * n_sems + [ANY] * len(after), out_specs=[HBM] * n, out_shape=[_hbm_like(a) for a in lands],
        input_output_aliases={i: i for i in range(n)},
        compiler_params=SPLIT,
    )(*lands, send2, recv2, *(relay_sems or ()), *after)


def _pair_exchange(name, grads, shard_rows):
    n = len(grads)
    shapes = [(g.shape[1:] if r is None else (r, g.shape[1])) for g, r in zip(grads, shard_rows)]

    def body(*refs):
        ins, recv = refs[:n], refs[n:2 * n]
        send_sems, recv_sems = refs[2 * n:]
        x, y, c, _ = _place()
        sends = []
        for w in range(n):
            for k in range(N_CHIP):
                j, r = 2 * k + 1 - c, shard_rows[w]
                src = ins[w].at[j] if r is None else ins[w].at[pl.ds(pl.multiple_of(j * r, SUBLANE), r), :]
                sends.append(pltpu.make_async_remote_copy(
                    src_ref=src, dst_ref=recv[w].at[k],
                    send_sem=send_sems.at[w, k], recv_sem=recv_sems.at[w, k],
                    device_id=(x, y, 1 - c), device_id_type=MESH))
        for cp in sends:
            cp.start()
        for cp in sends:
            cp.wait()

    return pl.pallas_call(
        body, name=name,
        in_specs=[ANY] * n, out_specs=[ANY] * n,
        out_shape=[jax.ShapeDtypeStruct((N_CHIP, *shape), g.dtype) for g, shape in zip(grads, shapes)],
        scratch_shapes=[pltpu.SemaphoreType.DMA((n, N_CHIP))] * 2,
    )(*grads)


def _pair_sum_rows(name, grad, received, core):
    _, r, c = received.shape
    tc = _fit(c, 512)

    def body(core_ref, a_ref, b_ref, o_ref):
        o_ref[...] = (a_ref[...] + b_ref[...]).astype(o_ref.dtype)

    spec = pl.BlockSpec((None, r, tc), lambda k, i, core_ref: (k, 0, i))
    return pl.pallas_call(
        body, name=name,
        grid_spec=pltpu.PrefetchScalarGridSpec(
            num_scalar_prefetch=1, grid=(N_CHIP, c // tc),
            in_specs=[pl.BlockSpec((r, tc), lambda k, i, core_ref: (2 * k + core_ref[0], i)), spec],
            out_specs=spec),
        out_shape=jax.ShapeDtypeStruct(received.shape, BF16),
        compiler_params=_params("parallel", "parallel"),
    )(core, grad, received)


def _pair_sum(name, grad, received, core):
    _, r, c = received.shape
    rows = min(ROWS, r)
    assert r % rows == 0

    def body(core_ref, a_ref, b_ref, o_ref):
        o_ref[...] = (a_ref[...].astype(F32) + b_ref[...].astype(F32)).astype(o_ref.dtype)

    spec = pl.BlockSpec((None, rows, c), lambda k, i, core_ref: (k, i, 0))
    return pl.pallas_call(
        body, name=name,
        grid_spec=pltpu.PrefetchScalarGridSpec(
            num_scalar_prefetch=1, grid=(N_CHIP, r // rows),
            in_specs=[pl.BlockSpec((None, None, rows, c), lambda k, i, core_ref: (k, core_ref[0], i, 0)), spec],
            out_specs=spec),
        out_shape=jax.ShapeDtypeStruct(received.shape, received.dtype),
        compiler_params=_params("parallel", "parallel"),
    )(core, grad.reshape(N_CHIP, 2, r, c), received)


def _away_shard(src, k, c, shard_rows):
    if shard_rows is None:
        return src.at[k]
    return src.at[pl.ds(pl.multiple_of((2 * k + 1 - c) * shard_rows, SUBLANE), shard_rows), :]


def _pair_send_start(name, away, shard_rows=None):
    shape = away.shape if shard_rows is None else (N_CHIP, shard_rows, away.shape[1])
    land = lax.empty(shape, away.dtype)

    def body(src, dst, send, recv, src_thru, dst_thru, token):
        x, y, c, _ = _place()
        for k in range(N_CHIP):
            pltpu.make_async_remote_copy(src_ref=_away_shard(src, k, c, shard_rows), dst_ref=dst.at[k], send_sem=send.at[k],
                                         recv_sem=recv.at[k], device_id=(x, y, 1 - c), device_id_type=MESH).start()
        token[...] = jnp.zeros_like(token)

    sem = pltpu.SemaphoreType.DMA((N_CHIP,))
    out = pl.pallas_call(
        body, name=name,
        in_specs=[HBM, HBM], out_specs=[SEM, SEM, HBM, HBM, IN_VMEM],
        out_shape=[sem, sem, _hbm_like(away), _hbm_like(land), jax.ShapeDtypeStruct((SUBLANE, LANE), F32)],
        input_output_aliases={0: 2, 1: 3},
        compiler_params=SPLIT,
    )(_in_hbm(away), _in_hbm(land))
    return (out[0], out[1]), out[2], out[3], out[4]


def _pair_send_wait(name, sems, src, land, after, shard_rows=None):
    def body(src_ref, dst_ref, send, recv, *rest):
        for k in range(N_CHIP):
            _await(dst_ref.at[k], send.at[k])
            _await(dst_ref.at[k], recv.at[k])

    return pl.pallas_call(
        body, name=name,
        in_specs=[HBM, HBM, SEM, SEM] + [ANY] * len(after), out_specs=HBM, out_shape=_hbm_like(land),
        input_output_aliases={1: 0},
        compiler_params=SPLIT,
    )(src, land, *sems, *after)


def _chip_send_start(name, sums):
    n = len(sums)
    lands = [lax.empty(a.shape, a.dtype) for a in sums]

    def body(*refs):
        src, land = refs[:n], refs[n:2 * n]
        send, recv, token = refs[2 * n], refs[2 * n + 1], refs[-1]
        x, y, c, chips = _place()
        for w in range(n):
            for j, (px, py) in enumerate(chips):
                pltpu.make_async_remote_copy(
                    src_ref=src[w].at[2 * px + py], dst_ref=land[w].at[2 * x + y],
                    send_sem=send.at[3 * w + j], recv_sem=recv.at[3 * w + j],
                    device_id=(px, py, c), device_id_type=MESH).start()
        token[...] = jnp.zeros_like(token)

    sem = pltpu.SemaphoreType.DMA((3 * n,))
    out = pl.pallas_call(
        body, name=name,
        in_specs=[HBM] * (2 * n),
        out_specs=[SEM, SEM] + [HBM] * (2 * n) + [IN_VMEM],
        out_shape=[sem, sem] + [_hbm_like(a) for a in sums] + [_hbm_like(a) for a in lands]
        + [jax.ShapeDtypeStruct((SUBLANE, LANE), F32)],
        input_output_aliases={i: 2 + i for i in range(2 * n)},
        compiler_params=SPLIT,
    )(*[_in_hbm(a) for a in sums], *[_in_hbm(a) for a in lands])
    return (out[0], out[1]), out[2:2 + n], out[2 + n:2 + 2 * n], out[-1]


def _chip_send_wait(name, groups, after):
    counts = [len(g[1]) for g in groups]
    n = sum(counts)

    def body(*refs):
        land = refs[n:2 * n]
        sems = refs[2 * n:2 * n + 2 * len(groups)]
        w = 0
        for gi, count in enumerate(counts):
            for i in range(count):
                for j in range(3):
                    _await(land[w].at[0], sems[2 * gi].at[3 * i + j])
                    _await(land[w].at[0], sems[2 * gi + 1].at[3 * i + j])
                w += 1

    sums = [a for g in groups for a in g[1]]
    lands = [a for g in groups for a in g[2]]
    sems = [s for g in groups for s in g[0]]
    return pl.pallas_call(
        body, name=name,
        in_specs=[HBM] * (2 * n) + [SEM] * len(sems) + [ANY] * len(after),
        out_specs=[HBM] * n, out_shape=[_hbm_like(a) for a in lands],
        input_output_aliases={n + i: i for i in range(n)},
        compiler_params=SPLIT,
    )(*sums, *lands, *sems, *after)


def _small_all_reduce(part, after=()):
    _, w = part.shape

    def body(p_ref, *rest):
        o_ref, buf, send_sems, recv_sems = rest[len(after):]
        x, y, c, _ = _place()
        me = 4 * x + 2 * y + c
        buf[me] = jnp.sum(p_ref[...], axis=0, keepdims=True)
        copies = []
        for k in range(1, N_DEV):
            dx, dy, dc = (k >> 2) & 1, (k >> 1) & 1, k & 1
            copies.append(pltpu.make_async_remote_copy(
                src_ref=buf.at[me], dst_ref=buf.at[me], send_sem=send_sems.at[k - 1], recv_sem=recv_sems.at[k - 1],
                device_id=(x ^ dx, y ^ dy, c ^ dc), device_id_type=MESH))
        for cp in copies:
            cp.start()
        for cp in copies:
            cp.wait()
        tot = buf[0]
        for d in range(1, N_DEV):
            tot = tot + buf[d]
        o_ref[...] = tot
        loss = jnp.sum(tot[:, w - LANE:], axis=1, keepdims=True)
        o_ref[:, w - LANE:] = jnp.broadcast_to(loss, (1, LANE))

    return pl.pallas_call(
        body, name="small_all_reduce",
        in_specs=[IN_VMEM] + [ANY] * len(after), out_specs=IN_VMEM,
        out_shape=jax.ShapeDtypeStruct((1, w), F32),
        scratch_shapes=[pltpu.VMEM((N_DEV, 1, w), F32), pltpu.SemaphoreType.DMA((N_DEV - 1,)), pltpu.SemaphoreType.DMA((N_DEV - 1,))],
        compiler_params=pltpu.CompilerParams(vmem_limit_bytes=VMEM_LIMIT_BYTES),
    )(part, *after)


def _adamw(w, g, m, v):
    m = ADAM_B1 * m + (1.0 - ADAM_B1) * g
    v = ADAM_B2 * v + (1.0 - ADAM_B2) * (g * g)
    m_hat = m / (1.0 - ADAM_B1 ** ADAM_STEP)
    v_hat = v / (1.0 - ADAM_B2 ** ADAM_STEP)
    delta = -ADAM_LR * (m_hat / (jnp.sqrt(v_hat) + ADAM_EPS) + ADAM_WD * w)
    return delta, m, v


def _sum_adam_block(chip_ref, p_ref, own_ref, w_ref, m_ref, v_ref, g_ref, d_ref, mo_ref, vo_ref):
    g = None
    for k in range(N_CHIP):
        term = jnp.where(chip_ref[0] == k, own_ref[...], p_ref[k]).astype(F32)
        g = term if g is None else g + term
    g_ref[...] = g
    d_ref[...], mo_ref[...], vo_ref[...] = _adamw(w_ref[...], g, m_ref[...], v_ref[...])


def _sum_adam(name, parts, sums, chip, w, m, v, after=()):
    _, r, c = w.shape
    n_after = len(after)
    by_rows = r % ROWS == 0 or r < ROWS
    tr, tc = (min(ROWS, r), c) if by_rows else (r, _fit(c, 512))
    at = (lambda i: (i, 0)) if by_rows else (lambda i: (0, i))

    def body(chip_ref, p_ref, own_ref, w_ref, m_ref, v_ref, *rest):
        _sum_adam_block(chip_ref, p_ref, own_ref, w_ref, m_ref, v_ref, *rest[n_after:])

    blk = pl.BlockSpec((None, tr, tc), lambda i, chip_ref: (0, *at(i)))
    out = jax.ShapeDtypeStruct((1, r, c), F32)
    return pl.pallas_call(
        body, name=name,
        grid_spec=pltpu.PrefetchScalarGridSpec(
            num_scalar_prefetch=1, grid=(r // tr if by_rows else c // tc,),
            in_specs=[pl.BlockSpec((N_CHIP, tr, tc), lambda i, chip_ref: (0, *at(i))),
                      pl.BlockSpec((None, tr, tc), lambda i, chip_ref: (chip_ref[0], *at(i))), blk, blk, blk]
            + [ANY] * n_after,
            out_specs=[blk] * 4),
        out_shape=[out] * 4,
        compiler_params=_params("parallel"),
    )(chip, parts, sums, w, m, v, *after)


def _adam_gains(total, ws, ms, vs):
    n = len(ws)
    widths = [w.shape[1] for w in ws]

    def body(t_ref, *refs):
        w_refs, m_refs, v_refs, outs = refs[:n], refs[n:2 * n], refs[2 * n:3 * n], refs[3 * n:]
        off = 0
        for i in range(n):
            g = t_ref[:, off:off + widths[i]]
            off += widths[i]
            g_ref, d_ref, mo_ref, vo_ref = outs[4 * i:4 * i + 4]
            g_ref[...] = g
            d_ref[...], mo_ref[...], vo_ref[...] = _adamw(w_refs[i][...], g, m_refs[i][...], v_refs[i][...])

    out = pl.pallas_call(
        body, name="adam_gains",
        out_shape=[jax.ShapeDtypeStruct(w.shape, F32) for w in ws for _ in range(4)],
    )(total, *ws, *ms, *vs)
    return [tuple(out[4 * i:4 * i + 4]) for i in range(n)]


def _adam_taps(total, first_col, device, w, m, v):
    _, n_taps, cw = w.shape
    col_block = lambda t, dev: (0, first_col // cw + t * N_DEV + dev[0])
    tap = pl.BlockSpec((None, 1, cw), lambda t, dev: (t, 0, 0))

    def body(dev_ref, t_ref, w_ref, m_ref, v_ref, g_ref, d_ref, mo_ref, vo_ref):
        g = t_ref[...]
        g_ref[...] = g
        d_ref[...], mo_ref[...], vo_ref[...] = _adamw(w_ref[...], g, m_ref[...], v_ref[...])

    shape3 = (n_taps, 1, cw)
    out = pl.pallas_call(
        body, name="adam_taps",
        grid_spec=pltpu.PrefetchScalarGridSpec(
            num_scalar_prefetch=1, grid=(n_taps,),
            in_specs=[pl.BlockSpec((1, cw), col_block), tap, tap, tap], out_specs=[tap] * 4),
        out_shape=[jax.ShapeDtypeStruct(shape3, F32)] * 4,
    )(device, total, w.reshape(shape3), m.reshape(shape3), v.reshape(shape3))
    return tuple(o.reshape(w.shape) for o in out)


def kernel(x, pre_mix_g, w_in, conv_w, q_norm_g, w_uq, kv_norm_g, w_ukv, conv_out_g, attn_out_g, w_o, post_mix_g, pre_mlp_g, w_up, w_down, post_mlp_g, loss_target, m_pre_mix_g, m_w_in, m_conv_w, m_q_norm_g, m_w_uq, m_kv_norm_g, m_w_ukv, m_conv_out_g, m_attn_out_g, m_w_o, m_post_mix_g, m_pre_mlp_g, m_w_up, m_w_down, m_post_mlp_g, v_pre_mix_g, v_w_in, v_conv_w, v_q_norm_g, v_w_uq, v_kv_norm_g, v_w_ukv, v_conv_out_g, v_attn_out_g, v_w_o, v_post_mix_g, v_pre_mlp_g, v_w_up, v_w_down, v_post_mlp_g):
    me = 4 * lax.axis_index("x") + 2 * lax.axis_index("y") + lax.axis_index("c")
    core = lax.axis_index("c").astype(jnp.int32).reshape(1)
    chip = (2 * lax.axis_index("x") + lax.axis_index("y")).astype(jnp.int32).reshape(1)
    gains = (pre_mix_g, q_norm_g, kv_norm_g, conv_out_g, attn_out_g, post_mix_g, pre_mlp_g, post_mlp_g)
    gain_m = (m_pre_mix_g, m_q_norm_g, m_kv_norm_g, m_conv_out_g, m_attn_out_g, m_post_mix_g, m_pre_mlp_g, m_post_mlp_g)
    gain_v = (v_pre_mix_g, v_q_norm_g, v_kv_norm_g, v_conv_out_g, v_attn_out_g, v_post_mix_g, v_pre_mlp_g, v_post_mlp_g)
    names = ("w_in", "w_uq", "w_ukv", "w_o", "w_up", "w_down")
    big = dict(zip(names, (w_in, w_uq, w_ukv, w_o, w_up, w_down)))
    big_m = dict(zip(names, (m_w_in, m_w_uq, m_w_ukv, m_w_o, m_w_up, m_w_down)))
    big_v = dict(zip(names, (v_w_in, v_w_uq, v_w_ukv, v_w_o, v_w_up, v_w_down)))
    n_heads = attn_out_g.shape[1] // HEAD
    n_taps = conv_w.shape[1]

    gathered = ("w_in", "conv", "w_uq", "w_ukv", "w_o", "w_up", "w_down")
    gather_groups = ((0, 1), (2, 3, 4), (5,), (6,))
    taps = jnp.pad(conv_w[0], ((0, SUBLANE - n_taps), (0, 0)))
    relayed_groups = (0, 2, 3)
    sems1, shards, lands, token = _gather_start("gather_start_first", [w_in[0].astype(BF16), taps], ((0, 1),), relayed=(0,))
    sems1, shards, lands = list(sems1), list(shards), list(lands)
    behind = token[0, 0]
    rest = [(big[nm][0] + behind).astype(BF16) for nm in gathered[2:]]

    def start_rest(after):
        sems_b, shards_b, lands_b, started = _gather_start("gather_start_rest", rest, ((0, 1, 2), (3,), (4,)), relayed=(1, 2),
                                                          after=after)
        sems1.extend(sems_b)
        shards.extend(shards_b)
        lands.extend(lands_b)
        return started

    cols = lambda a: jnp.concatenate([a[j] for j in range(N_DEV)], axis=1)
    rows = lambda a: a.reshape(N_DEV * a.shape[1], a.shape[2])
    device = me.astype(jnp.int32).reshape(1)
    own_in = lambda a, shard: lax.dynamic_update_index_in_dim(a, shard, me, 0)
    q_pieces = [(h, 0, HEAD) for h in range(n_heads)] + [(h, HEAD, QK) for h in range(n_heads)]
    ready = {
        "w_in": lambda a, shard: _join_col_shards("join_w_in", a, shard, device),
        "conv": lambda a, shard: cols(own_in(a, shard))[:n_taps],
        "w_uq": lambda a, shard: _join_col_shards("join_w_uq", a, shard, device, q_pieces),
        "w_ukv": lambda a, shard: cols(own_in(a, shard)),
        "w_o": lambda a, shard: rows(own_in(a, shard)),
        "w_up": own_in,
        "w_down": lambda a, shard: rows(own_in(a, shard)),
    }
    assert w_uq.shape[2] == QK

    class Weights:
        def __init__(self):
            self.passed, self.relayed = {}, {}

        def forward(self, group, after):
            idx = gather_groups[group]
            if group == 0:
                after = (*after, *rest)
            self.passed[group] = _gather_forward(f"gather_forward_{group}", [shards[i] for i in idx], [lands[i] for i in idx],
                                                 *sems1[group], after, relayed=group in relayed_groups)
            return tuple(self.passed[group][1])

        def relay(self, group, after):
            sems2, mid = self.passed[group]
            self.relayed[group], mid = _gather_relay_forward(f"gather_relay_{group}", mid, *sems2, after)
            self.passed[group] = (sems2, mid)
            if group == 0:
                start_rest(tuple(mid))
            return tuple(mid)

        def ready(self, group, after):
            sems2, mid = self.passed[group]
            full = _gather_wait(f"gather_wait_{group}", mid, *sems2, after, relay_sems=self.relayed.get(group))
            out = []
            return [ready[gathered[i]](a, shards[i]) for i, a in zip(gather_groups[group], full)]

    weights = Weights()

    col_blocks = lambda g: g.reshape(g.shape[0], N_DEV, g.shape[1] // N_DEV).transpose(1, 0, 2)
    row_blocks = lambda g: g.reshape(N_DEV, g.shape[0] // N_DEV, g.shape[1])
    grad_groups = (("w_down",), ("w_up",), ("w_o",), ("w_uq", "w_ukv"), ("w_in",))
    transposed = {"w_in": w_in.shape[2], "w_uq": w_uq.shape[2]}
    to_blocks = {
        "w_in": lambda g: g, "w_uq": lambda g: _unpermute_q_rows(g, n_heads),
        "w_ukv": col_blocks, "w_o": row_blocks, "w_up": lambda g: g, "w_down": row_blocks,
    }
    in_flight = []

    class Grads:
        def __init__(self):
            self.core = core
            self.away = {}

        def send_sums(self, group, sums):
            sems, sums, parts, tok = _chip_send_start(f"chip_send_start_{group}", list(sums))
            in_flight.append((sems, sums, parts))
            return (tok,)

        def full(self, group, arrays, received=None):
            nms = grad_groups[group]
            if received is None:
                blocks = [to_blocks[nm](g) for nm, g in zip(nms, arrays)]
                got = _pair_exchange(f"pair_exchange_{group}", blocks, [transposed.get(nm) for nm in nms])
            else:
                blocks, got = [self.away[group][1]], [self.received(group, received)]
            sums = [(_pair_sum_rows if nm in transposed else _pair_sum)(f"pair_sum_{nm}", g, r, core)
                    for nm, g, r in zip(nms, blocks, got)]
            return self.send_sums(group, sums)

        def send_away(self, group, half):
            nm = grad_groups[group][0]
            rows = transposed.get(nm)
            sems, src, land, tok = _pair_send_start(f"pair_send_start_{group}", half if rows is None else to_blocks[nm](half), rows)
            self.away[group] = (sems, src, land, rows)
            return (tok,)

        def received(self, group, after):
            sems, src, land, rows = self.away[group]
            return _pair_send_wait(f"pair_send_wait_{group}", sems, src, land, after, rows)

        def update_now(self, group, after):
            return update(str(group), group, group + 1, after)

    big_out = {}

    def update(tag, first, last, after):
        picked = [i for i in range(first, last) if grad_groups[i][0] not in big_out]
        groups = [in_flight[i] for i in picked]
        parts = _chip_send_wait("chip_send_wait_" + tag, groups, after)
        nms = [nm for i in picked for nm in grad_groups[i]]
        sums = [a for _, s, _ in groups for a in s]
        for nm, p, s in zip(nms, parts, sums):
            view = (lambda a: jnp.swapaxes(a, 1, 2)) if nm in transposed else (lambda a: a)
            out = _sum_adam("adam_" + nm, p, s, chip, view(big[nm]), view(big_m[nm]), view(big_v[nm]), after=after)
            after = (out[0],)
            big_out[nm] = [view(o) for o in out]
        return after

    grad_x, small = _local_step(x[0], loss_target[0], gains, weights, Grads(), first_after=(token,))

    after = update("early", 0, len(in_flight) - 1, (grad_x,))
    total = _small_all_reduce(small, after=after)
    update("late", len(in_flight) - 1, len(in_flight), (total,))
    big_out = [big_out[nm] for nm in names]

    gain_out = _adam_gains(total, gains, gain_m, gain_v)
    taps_out = _adam_taps(total, sum(g.shape[1] for g in gains), me.astype(jnp.int32).reshape(1), conv_w, m_conv_w, v_conv_w)
    loss = total[0, total.shape[1] - 1]

    order = (0, "w_in", "conv", 1, "w_uq", 2, "w_ukv", 3, 4, "w_o", 5, 6, "w_up", "w_down", 7)
    by_name = dict(zip(names, big_out))
    outs = [loss, grad_x[None]]
    for kind in range(4):
        for item in order:
            if item == "conv":
                outs.append(taps_out[kind])
            elif isinstance(item, int):
                outs.append(gain_out[item][kind])
            else:
                outs.append(by_name[item][kind])
    return tuple(outs)
```

```python
import math

import jax
import jax.numpy as jnp
from jax import lax
from jax.experimental import pallas as pl
from jax.experimental.pallas import tpu as pltpu

F32 = jnp.float32
BF16 = jnp.bfloat16

EPS = 1e-6
NEG_INF = -1e30
HEAD = 128
ROPE = 64
QK = HEAD + ROPE
CHUNK = 64
ROPE_THETA = 10000.0
ADAM_LR, ADAM_B1, ADAM_B2, ADAM_EPS, ADAM_WD, ADAM_STEP = 0.001, 0.9, 0.999, 1e-08, 0.01, 10

LANE = 128
SUBLANE = 8
VMEM_LIMIT_BYTES = 56 * 1024 * 1024

N_DEV = 8
N_CHIP = 4
MESH = pl.DeviceIdType.MESH


def _params(*sem):
    return pltpu.CompilerParams(dimension_semantics=sem, vmem_limit_bytes=VMEM_LIMIT_BYTES)


ANY = pl.BlockSpec(memory_space=pl.ANY)


def _call(body, *, in_specs, after=(), **kw):
    n_in, n_after = len(in_specs), len(after)

    def ordered(*refs):
        body(*refs[:n_in], *refs[n_in + n_after:])

    call = pl.pallas_call(ordered, in_specs=[*in_specs, *[ANY] * n_after], **kw)
    return lambda *operands: call(*operands, *after)


def _sublane_sum(v):
    r, w = v.shape
    return jnp.sum(v.reshape(r // SUBLANE, SUBLANE, w), axis=0)


def _rstd(x):
    return lax.rsqrt(jnp.mean(x * x, axis=-1, keepdims=True) + EPS)


def _rms_bwd(x, g, dy):
    r = _rstd(x)
    xh = x * r
    dxh = dy * g
    dx = r * (dxh - xh * jnp.mean(dxh * xh, axis=-1, keepdims=True))
    return dx, dy * xh


def _accumulate(ref, val, step):
    @pl.when(step == 0)
    def _():
        ref[...] = val

    @pl.when(step > 0)
    def _():
        ref[...] += val


NN = ((1,), (0,))
NT = ((1,), (1,))
TN = ((0,), (0,))


def _matmul(name, a, b, *, grid, a_spec, b_spec, out_shape, out_specs, contract, nk=1, acc_shape=None,
            extras=(), extra_specs=(), epilogue=None, after=()):
    multi = isinstance(out_shape, (tuple, list))
    out_shapes = tuple(out_shape) if multi else (out_shape,)
    n_out = len(out_shapes)
    n_extra = len(extras)

    def body(a_ref, b_ref, *rest):
        x_refs = rest[:n_extra]
        o_refs = rest[n_extra:n_extra + n_out]

        def emit(acc):
            vals = epilogue(acc, *[r[...] for r in x_refs]) if epilogue else (acc,)
            for r, v in zip(o_refs, vals):
                r[...] = v.astype(r.dtype)

        p = lax.dot_general(a_ref[...], b_ref[...], (contract, ((), ())), preferred_element_type=F32)
        if nk == 1:
            emit(p)
        else:
            acc_ref = rest[n_extra + n_out]
            k = pl.program_id(2)
            _accumulate(acc_ref, p, k)

            @pl.when(k == nk - 1)
            def _():
                emit(acc_ref[...])

    sem = ("parallel", "parallel") + (("arbitrary",) if nk > 1 else ())
    return _call(
        body, name=name, grid=grid, after=after,
        in_specs=[a_spec, b_spec, *extra_specs],
        out_specs=out_specs,
        out_shape=out_shape,
        scratch_shapes=[pltpu.VMEM(acc_shape, F32)] if nk > 1 else [],
        compiler_params=_params(*sem),
    )(a, b, *extras)


def _fit(n, tile):
    if n <= tile:
        return n
    t = tile - tile % LANE
    while n % t:
        t -= LANE
    return t


def _mm_nn(name, a, b, out_dtype, tm, tn, after=()):
    m, k = a.shape
    n = b.shape[1]
    tm, tn = _fit(m, tm), _fit(n, tn)
    return _matmul(name, a, b, grid=(m // tm, n // tn), after=after,
                   a_spec=pl.BlockSpec((tm, k), lambda i, j: (i, 0)),
                   b_spec=pl.BlockSpec((k, tn), lambda i, j: (0, j)),
                   out_shape=jax.ShapeDtypeStruct((m, n), out_dtype),
                   out_specs=pl.BlockSpec((tm, tn), lambda i, j: (i, j)), contract=NN)


def _mm_nt(name, a, b, out_dtype, tm, tn, after=()):
    m, k = a.shape
    n = b.shape[0]
    tm, tn = _fit(m, tm), _fit(n, tn)
    return _matmul(name, a, b, grid=(m // tm, n // tn), after=after,
                   a_spec=pl.BlockSpec((tm, k), lambda i, j: (i, 0)),
                   b_spec=pl.BlockSpec((tn, k), lambda i, j: (j, 0)),
                   out_shape=jax.ShapeDtypeStruct((m, n), out_dtype),
                   out_specs=pl.BlockSpec((tm, tn), lambda i, j: (i, j)), contract=NT)


def _mm_tn(name, a, b, out_dtype, tm, tn):
    s, m = a.shape
    n = b.shape[1]
    tm, tn = _fit(m, tm), _fit(n, tn)
    return _matmul(name, a, b, grid=(m // tm, n // tn),
                   a_spec=pl.BlockSpec((s, tm), lambda i, j: (0, i)),
                   b_spec=pl.BlockSpec((s, tn), lambda i, j: (0, j)),
                   out_shape=jax.ShapeDtypeStruct((m, n), out_dtype),
                   out_specs=pl.BlockSpec((tm, tn), lambda i, j: (i, j)), contract=TN)


ROWS = 256


def _row_spec(rows, width):
    return pl.BlockSpec((rows, width), lambda i: (i, 0))


def _fixed_spec(rows, width):
    return pl.BlockSpec((rows, width), lambda i: (0, 0))


def _column_pieces(rows, start, width):
    piece = math.gcd(start, width)
    assert piece % LANE == 0
    return [pl.BlockSpec((rows, piece), lambda i, b=start // piece + p: (i, b)) for p in range(width // piece)]


def _rms_fwd(name, x, g, cols=None, after=()):
    s = x.shape[0]
    start, w = cols or (0, x.shape[1])
    rows = min(ROWS, s)
    pieces = _column_pieces(rows, start, w) if cols else [_row_spec(rows, w)]
    n = len(pieces)

    def body(*refs):
        g_ref, o_ref = refs[n:]
        xv = refs[0][...] if n == 1 else jnp.concatenate([r[...] for r in refs[:n]], axis=1)
        o_ref[...] = (xv * _rstd(xv) * g_ref[...]).astype(o_ref.dtype)

    return _call(
        body, name=name, grid=(s // rows,), after=after,
        in_specs=[*pieces, _fixed_spec(1, w)],
        out_specs=_row_spec(rows, w),
        out_shape=jax.ShapeDtypeStruct((s, w), BF16),
        compiler_params=_params("parallel"),
    )(*[x] * n, g)


def _rms_bwd_call(name, x, g, dy, out_dtype, cols=None, after=()):
    s = x.shape[0]
    start, w = cols or (0, x.shape[1])
    rows = min(ROWS, s)
    pieces = _column_pieces(rows, start, w) if cols else [_row_spec(rows, w)]
    n = len(pieces)

    def body(*refs):
        g_ref, dy_ref, dx_ref, dg_ref = refs[n:]
        xv = refs[0][...] if n == 1 else jnp.concatenate([r[...] for r in refs[:n]], axis=1)
        dx, dgc = _rms_bwd(xv, g_ref[...], dy_ref[...].astype(F32))
        dx_ref[...] = dx.astype(dx_ref.dtype)
        _accumulate(dg_ref, _sublane_sum(dgc), pl.program_id(0))

    return _call(
        body, name=name, grid=(s // rows,), after=after,
        in_specs=[*pieces, _fixed_spec(1, w), _row_spec(rows, w)],
        out_specs=[_row_spec(rows, w), _fixed_spec(SUBLANE, w)],
        out_shape=[jax.ShapeDtypeStruct((s, w), out_dtype), jax.ShapeDtypeStruct((SUBLANE, w), F32)],
        compiler_params=_params("arbitrary"),
    )(*[x] * n, g, dy)


def _norm_up(name, x, cols, g, w, after=()):
    s = x.shape[0]
    start, width = cols
    n = w.shape[1]
    tm = min(TILE_M, s)
    pieces = _column_pieces(tm, start, width)
    n_p = len(pieces)

    def body(*refs):
        g_ref, w_ref, xn_ref, o_ref = refs[n_p:]
        xv = refs[0][...] if n_p == 1 else jnp.concatenate([r[...] for r in refs[:n_p]], axis=1)
        xn = (xv * _rstd(xv) * g_ref[...]).astype(BF16)
        xn_ref[...] = xn
        o_ref[...] = jnp.dot(xn, w_ref[...], preferred_element_type=F32)

    return _call(
        body, name=name, grid=(s // tm,), after=after,
        in_specs=[*pieces, _fixed_spec(1, width), _fixed_spec(width, n)],
        out_specs=[_row_spec(tm, width), _row_spec(tm, n)],
        out_shape=[jax.ShapeDtypeStruct((s, width), BF16), jax.ShapeDtypeStruct((s, n), F32)],
        compiler_params=_params("parallel"),
    )(*[x] * n_p, g, w)


def _up_norm_bwd(name, dy, w, x, cols, g, after=()):
    s, n = dy.shape
    start, width = cols
    tm = min(TILE_M, s)
    pieces = _column_pieces(tm, start, width)
    n_p = len(pieces)

    def body(dy_ref, w_ref, *refs):
        g_ref, dx_ref, dg_ref = refs[n_p:]
        xv = refs[0][...] if n_p == 1 else jnp.concatenate([r[...] for r in refs[:n_p]], axis=1)
        dxn = lax.dot_general(dy_ref[...], w_ref[...], (NT, ((), ())), preferred_element_type=F32)
        dx, dgc = _rms_bwd(xv, g_ref[...], dxn)
        dx_ref[...] = dx.astype(dx_ref.dtype)
        _accumulate(dg_ref, _sublane_sum(dgc), pl.program_id(0))

    return _call(
        body, name=name, grid=(s // tm,), after=after,
        in_specs=[_row_spec(tm, n), _fixed_spec(width, n), *pieces, _fixed_spec(1, width)],
        out_specs=[_row_spec(tm, width), _fixed_spec(SUBLANE, width)],
        out_shape=[jax.ShapeDtypeStruct((s, width), BF16), jax.ShapeDtypeStruct((SUBLANE, width), F32)],
        compiler_params=_params("arbitrary"),
    )(dy, w, *[x] * n_p, g)


def _mid_fwd(x, y, g_post, g_pre, after=()):
    s, w = x.shape
    rows = min(ROWS, s)

    def body(x_ref, y_ref, gp_ref, gq_ref, x2_ref, h2_ref):
        yv = y_ref[...]
        x2 = x_ref[...] + yv * _rstd(yv) * gp_ref[...]
        x2_ref[...] = x2
        h2_ref[...] = (x2 * _rstd(x2) * gq_ref[...]).astype(h2_ref.dtype)

    return _call(
        body, name="mid_fwd", grid=(s // rows,), after=after,
        in_specs=[_row_spec(rows, w), _row_spec(rows, w), _fixed_spec(1, w), _fixed_spec(1, w)],
        out_specs=[_row_spec(rows, w), _row_spec(rows, w)],
        out_shape=[jax.ShapeDtypeStruct((s, w), F32), jax.ShapeDtypeStruct((s, w), BF16)],
        compiler_params=_params("parallel"),
    )(x, y, g_post, g_pre)


def _head(m, x2, tgt, g):
    s, w = m.shape
    rows = min(ROWS, s)

    def body(m_ref, x2_ref, t_ref, g_ref, dout_ref, dm_ref, dg_ref, loss_ref):
        mv = m_ref[...]
        gv = g_ref[...]
        out = x2_ref[...] + mv * _rstd(mv) * gv
        err = out - t_ref[...]
        dout = err * (1.0 / w)
        dout_ref[...] = dout
        dm, dgc = _rms_bwd(mv, gv, dout)
        dm_ref[...] = dm.astype(dm_ref.dtype)
        sq = err * err
        lanes = sq[:, 0:LANE]
        for j in range(1, w // LANE):
            lanes = lanes + sq[:, j * LANE:(j + 1) * LANE]
        step = pl.program_id(0)
        _accumulate(dg_ref, _sublane_sum(dgc), step)
        _accumulate(loss_ref, _sublane_sum(lanes) * (0.5 / w), step)

    return pl.pallas_call(
        body, name="head", grid=(s // rows,),
        in_specs=[_row_spec(rows, w), _row_spec(rows, w), _row_spec(rows, w), _fixed_spec(1, w)],
        out_specs=[_row_spec(rows, w), _row_spec(rows, w), _fixed_spec(SUBLANE, w), _fixed_spec(SUBLANE, LANE)],
        out_shape=[jax.ShapeDtypeStruct((s, w), F32), jax.ShapeDtypeStruct((s, w), BF16),
                   jax.ShapeDtypeStruct((SUBLANE, w), F32), jax.ShapeDtypeStruct((SUBLANE, LANE), F32)],
        compiler_params=_params("arbitrary"),
    )(m, x2, tgt, g)


def _mid_bwd(x2, y, d_out, d_h2, g_pre, g_post, after=()):
    s, w = x2.shape
    rows = min(ROWS, s)

    def body(x2_ref, y_ref, dout_ref, dh2_ref, gq_ref, gp_ref, dx2_ref, dy_ref, dgq_ref, dgp_ref):
        dx, dgq = _rms_bwd(x2_ref[...], gq_ref[...], dh2_ref[...])
        dx2 = dout_ref[...] + dx
        dx2_ref[...] = dx2
        dy, dgp = _rms_bwd(y_ref[...], gp_ref[...], dx2)
        dy_ref[...] = dy.astype(dy_ref.dtype)
        step = pl.program_id(0)
        _accumulate(dgq_ref, _sublane_sum(dgq), step)
        _accumulate(dgp_ref, _sublane_sum(dgp), step)

    return _call(
        body, name="mid_bwd", grid=(s // rows,), after=after,
        in_specs=[_row_spec(rows, w)] * 4 + [_fixed_spec(1, w)] * 2,
        out_specs=[_row_spec(rows, w), _row_spec(rows, w), _fixed_spec(SUBLANE, w), _fixed_spec(SUBLANE, w)],
        out_shape=[jax.ShapeDtypeStruct((s, w), F32), jax.ShapeDtypeStruct((s, w), BF16),
                   jax.ShapeDtypeStruct((SUBLANE, w), F32), jax.ShapeDtypeStruct((SUBLANE, w), F32)],
        compiler_params=_params("arbitrary"),
    )(x2, y, d_out, d_h2, g_pre, g_post)


def _first_bwd(x, g, d_h1, d_x2, after=()):
    s, w = x.shape
    rows = min(ROWS, s)

    def body(x_ref, g_ref, dh_ref, dx2_ref, dx_ref, dg_ref):
        dx, dgc = _rms_bwd(x_ref[...], g_ref[...], dh_ref[...])
        dx_ref[...] = dx2_ref[...] + dx
        _accumulate(dg_ref, _sublane_sum(dgc), pl.program_id(0))

    return _call(
        body, name="first_bwd", grid=(s // rows,), after=after,
        in_specs=[_row_spec(rows, w), _fixed_spec(1, w), _row_spec(rows, w), _row_spec(rows, w)],
        out_specs=[_row_spec(rows, w), _fixed_spec(SUBLANE, w)],
        out_shape=[jax.ShapeDtypeStruct((s, w), F32), jax.ShapeDtypeStruct((SUBLANE, w), F32)],
        compiler_params=_params("arbitrary"),
    )(x, g, d_h1, d_x2)


def _shift_down(v, k):
    t = lax.broadcasted_iota(jnp.int32, v.shape, 0)
    return jnp.where(t >= k, pltpu.roll(v, k, 0), 0.0)


def _shift_up(v, k):
    n = v.shape[0]
    t = lax.broadcasted_iota(jnp.int32, v.shape, 0)
    return jnp.where(t < n - k, pltpu.roll(v, n - k, 0), 0.0)


def _conv_core(u, b, c, w):
    z = c * u
    conv = w[0:1, :] * _shift_down(z, 2) + w[1:2, :] * _shift_down(z, 1) + w[2:3, :] * z
    return z, conv, b * conv


def _conv_fwd(proj, conv_w, g, n_groups, out_width, after=()):
    s = proj.shape[0]

    def body(u_ref, b_ref, c_ref, w_ref, g_ref, o_ref):
        _, _, yr = _conv_core(u_ref[...], b_ref[...], c_ref[...], w_ref[...])
        o_ref[...] = (yr * _rstd(yr) * g_ref[...]).astype(o_ref.dtype)

    col = lambda k: pl.BlockSpec((s, HEAD), lambda i: (0, k * n_groups + i))
    return _call(
        body, name="conv_fwd", grid=(n_groups,), after=after,
        in_specs=[col(0), col(1), col(2), pl.BlockSpec((3, HEAD), lambda i: (0, i)), pl.BlockSpec((1, HEAD), lambda i: (0, i))],
        out_specs=pl.BlockSpec((s, HEAD), lambda i: (0, i)),
        out_shape=jax.ShapeDtypeStruct((s, out_width), BF16),
        compiler_params=_params("parallel"),
    )(proj, proj, proj, conv_w, g)


def _conv_bwd(proj, d_mix, conv_w, g, n_groups):
    s = proj.shape[0]
    width = n_groups * HEAD

    def body(u_ref, b_ref, c_ref, dy_ref, w_ref, g_ref, du_ref, db_ref, dc_ref, dg_ref, dw_ref):
        u, b, c, w = u_ref[...], b_ref[...], c_ref[...], w_ref[...]
        z, conv, yr = _conv_core(u, b, c, w)
        dyr, dgc = _rms_bwd(yr, g_ref[...], dy_ref[...])
        dconv = dyr * b
        db_ref[...] = (dyr * conv).astype(db_ref.dtype)
        dz = w[2:3, :] * dconv + w[1:2, :] * _shift_up(dconv, 1) + w[0:1, :] * _shift_up(dconv, 2)
        dc_ref[...] = (dz * u).astype(dc_ref.dtype)
        du_ref[...] = (dz * c).astype(du_ref.dtype)
        dg_ref[...] = _sublane_sum(dgc)
        dw_ref[0] = _sublane_sum(dconv * _shift_down(z, 2))
        dw_ref[1] = _sublane_sum(dconv * _shift_down(z, 1))
        dw_ref[2] = _sublane_sum(dconv * z)

    col = lambda k: pl.BlockSpec((s, HEAD), lambda i: (0, k * n_groups + i))
    grp = pl.BlockSpec((s, HEAD), lambda i: (0, i))
    return pl.pallas_call(
        body, name="conv_bwd", grid=(n_groups,),
        in_specs=[col(0), col(1), col(2), grp, pl.BlockSpec((3, HEAD), lambda i: (0, i)), pl.BlockSpec((1, HEAD), lambda i: (0, i))],
        out_specs=[grp, grp, grp, pl.BlockSpec((SUBLANE, HEAD), lambda i: (0, i)),
                   pl.BlockSpec((3, SUBLANE, HEAD), lambda i: (0, 0, i))],
        out_shape=[jax.ShapeDtypeStruct((s, width), BF16)] * 3
        + [jax.ShapeDtypeStruct((SUBLANE, width), F32), jax.ShapeDtypeStruct((3, SUBLANE, width), F32)],
        compiler_params=_params("parallel"),
    )(proj, proj, proj, d_mix, conv_w, g)


def _rope_tables(s, n_heads):
    pos = jnp.arange(s, dtype=F32)
    inv_freq = jnp.power(ROPE_THETA, -jnp.arange(0, ROPE, 2, dtype=F32) / ROPE)
    ang = pos[:, None] * inv_freq[None, :]
    cos, sin = jnp.cos(ang), jnp.sin(ang)
    cs = jnp.concatenate([cos, cos], axis=1)
    sn = jnp.concatenate([-sin, sin], axis=1)
    pad = jnp.zeros((s, LANE - ROPE), F32)
    return (jnp.tile(cs, (1, n_heads)), jnp.tile(sn, (1, n_heads)),
            jnp.concatenate([cs, pad], axis=1), jnp.concatenate([sn, pad], axis=1))


def _swap_halves(v):
    w = v.shape[1]
    lane = lax.broadcasted_iota(jnp.int32, v.shape, 1)
    first = (lane % ROPE) < (ROPE // 2)
    return jnp.where(first, pltpu.roll(v, w - ROPE // 2, 1), pltpu.roll(v, ROPE // 2, 1))


def _pack_heads(q, kv, proj, kr_col, tables, n_heads, after=()):
    s = q.shape[0]
    rows = min(ROWS, s)
    cq, sq, ck, sk = tables
    wq = n_heads * ROPE

    def body(q_ref, kv_ref, kr_ref, cq_ref, sq_ref, ck_ref, sk_ref, qo_ref, ko_ref, vo_ref):
        qr = q_ref[:, n_heads * HEAD:]
        qr = qr * cq_ref[...] + _swap_halves(qr) * sq_ref[...]
        krv = kr_ref[...]
        krv = krv * ck_ref[...] + _swap_halves(krv) * sk_ref[...]
        for h in range(n_heads):
            qo_ref[h] = jnp.concatenate([q_ref[:, h * HEAD:(h + 1) * HEAD], qr[:, h * ROPE:(h + 1) * ROPE]], axis=1).astype(BF16)
            ko_ref[h] = jnp.concatenate([kv_ref[:, 2 * h * HEAD:(2 * h + 1) * HEAD], krv[:, :ROPE]], axis=1).astype(BF16)
            vo_ref[h] = kv_ref[:, (2 * h + 1) * HEAD:(2 * h + 2) * HEAD].astype(BF16)

    hs = lambda w: pl.BlockSpec((n_heads, rows, w), lambda i: (0, i, 0))
    return _call(
        body, name="pack_heads", grid=(s // rows,), after=after,
        in_specs=[_row_spec(rows, q.shape[1]), _row_spec(rows, kv.shape[1]), pl.BlockSpec((rows, LANE), lambda i: (i, kr_col // LANE)),
                  _row_spec(rows, wq), _row_spec(rows, wq), _row_spec(rows, LANE), _row_spec(rows, LANE)],
        out_specs=[hs(QK), hs(QK), hs(HEAD)],
        out_shape=[jax.ShapeDtypeStruct((n_heads, s, QK), BF16), jax.ShapeDtypeStruct((n_heads, s, QK), BF16),
                   jax.ShapeDtypeStruct((n_heads, s, HEAD), BF16)],
        compiler_params=_params("parallel"),
    )(q, kv, proj, cq, sq, ck, sk)


def _unpack_heads(dq, dk, dv, tables, n_heads):
    s = dq.shape[1]
    rows = min(ROWS, s)
    cq, sq, ck, sk = tables
    wq = n_heads * ROPE

    def body(dq_ref, dk_ref, dv_ref, cq_ref, sq_ref, ck_ref, sk_ref, qo_ref, kvo_ref, kro_ref):
        dqr = jnp.concatenate([dq_ref[h][:, HEAD:] for h in range(n_heads)], axis=1)
        dqr = dqr * cq_ref[...] - _swap_halves(dqr) * sq_ref[...]
        dkr = dk_ref[0][:, HEAD:]
        for h in range(1, n_heads):
            dkr = dkr + dk_ref[h][:, HEAD:]
        dkr = jnp.concatenate([dkr, jnp.zeros((rows, LANE - ROPE), F32)], axis=1)
        dkr = dkr * ck_ref[...] - _swap_halves(dkr) * sk_ref[...]
        kro_ref[...] = dkr.astype(kro_ref.dtype)
        qo_ref[:, n_heads * HEAD:] = dqr.astype(qo_ref.dtype)
        for h in range(n_heads):
            qo_ref[:, h * HEAD:(h + 1) * HEAD] = dq_ref[h][:, :HEAD].astype(qo_ref.dtype)
            kvo_ref[:, 2 * h * HEAD:(2 * h + 1) * HEAD] = dk_ref[h][:, :HEAD].astype(kvo_ref.dtype)
            kvo_ref[:, (2 * h + 1) * HEAD:(2 * h + 2) * HEAD] = dv_ref[h].astype(kvo_ref.dtype)

    hs = lambda w: pl.BlockSpec((n_heads, rows, w), lambda i: (0, i, 0))
    return pl.pallas_call(
        body, name="unpack_heads", grid=(s // rows,),
        in_specs=[hs(QK), hs(QK), hs(HEAD), _row_spec(rows, wq), _row_spec(rows, wq), _row_spec(rows, LANE), _row_spec(rows, LANE)],
        out_specs=[_row_spec(rows, n_heads * QK), _row_spec(rows, 2 * n_heads * HEAD), _row_spec(rows, LANE)],
        out_shape=[jax.ShapeDtypeStruct((s, n_heads * QK), BF16), jax.ShapeDtypeStruct((s, 2 * n_heads * HEAD), BF16),
                   jax.ShapeDtypeStruct((s, LANE), BF16)],
        compiler_params=_params("parallel"),
    )(dq, dk, dv, cq, sq, ck, sk)


TQ = 256


LOG2_E = 1.4426950408889634


def _softmax_parts(q, k):
    tq, n_keys = q.shape[0], k.shape[0]
    sc = lax.dot_general(q, k, (NT, ((), ())), preferred_element_type=F32) * (QK ** -0.5 * LOG2_E)
    row = lax.broadcasted_iota(jnp.int32, (tq, tq), 0)
    col = lax.broadcasted_iota(jnp.int32, (tq, tq), 1)
    own = jnp.where(col // CHUNK <= row // CHUNK, sc[:, n_keys - tq:], NEG_INF)
    sc = own if n_keys == tq else jnp.concatenate([sc[:, :n_keys - tq], own], axis=1)
    e = jnp.exp2(sc - jnp.max(sc, axis=-1, keepdims=True))
    return e, 1.0 / jnp.sum(e, axis=-1, keepdims=True)


def _attn_fwd(q, k, v, g, mix, col0):
    n_heads, s, _ = q.shape
    tq = min(TQ, s)
    assert tq % CHUNK == 0 and s % tq == 0

    def body(q_ref, k_ref, v_ref, g_ref, mix_ref, o_ref, y_ref):
        for c in range(s // tq):
            rows, n_keys = pl.ds(c * tq, tq), (c + 1) * tq
            e, inv = _softmax_parts(q_ref[rows, :], k_ref[0:n_keys, :])
            o = jnp.dot(e.astype(BF16), v_ref[0:n_keys, :], preferred_element_type=F32) * inv
            o_ref[rows, :] = o
            y_ref[rows, :] = (o * _rstd(o) * g_ref[...]).astype(y_ref.dtype)

    head = lambda w: pl.BlockSpec((None, s, w), lambda h: (h, 0, 0))
    return pl.pallas_call(
        body, name="attn_fwd", grid=(n_heads,),
        in_specs=[head(QK), head(QK), head(HEAD), pl.BlockSpec((1, HEAD), lambda h: (0, h)), ANY],
        out_specs=[head(HEAD), pl.BlockSpec((s, HEAD), lambda h: (0, col0 // HEAD + h))],
        out_shape=[jax.ShapeDtypeStruct((n_heads, s, HEAD), F32), jax.ShapeDtypeStruct(mix.shape, mix.dtype)],
        input_output_aliases={4: 1},
        compiler_params=_params("parallel"),
    )(q, k, v, g, mix)


def _attn_bwd(q, k, v, o, d_mix, g, col0, after=()):
    n_heads, s, _ = q.shape
    tq = min(TQ, s)

    def body(q_ref, k_ref, v_ref, o_ref, dy_ref, g_ref, dq_ref, dk_ref, dv_ref, dg_ref):
        dg = None
        for c in reversed(range(s // tq)):
            rows, n_keys = pl.ds(c * tq, tq), (c + 1) * tq
            qv, kv_, vv = q_ref[rows, :], k_ref[0:n_keys, :], v_ref[0:n_keys, :]
            do, dgc = _rms_bwd(o_ref[rows, :], g_ref[...], dy_ref[rows, :])
            do = do.astype(BF16)
            dg = _sublane_sum(dgc) if dg is None else dg + _sublane_sum(dgc)
            e, inv = _softmax_parts(qv, kv_)
            p = e * inv
            dp = lax.dot_general(do, vv, (NT, ((), ())), preferred_element_type=F32)
            ds = (p * (dp - jnp.sum(p * dp, axis=-1, keepdims=True)) * (QK ** -0.5)).astype(BF16)
            dq_ref[rows, :] = jnp.dot(ds, kv_, preferred_element_type=F32)
            dk = lax.dot_general(ds, qv, (TN, ((), ())), preferred_element_type=F32)
            dv = lax.dot_general(p.astype(BF16), do, (TN, ((), ())), preferred_element_type=F32)
            if n_keys == s:
                dk_ref[...] = dk
                dv_ref[...] = dv
            else:
                dk_ref[0:n_keys, :] += dk
                dv_ref[0:n_keys, :] += dv
        dg_ref[...] = dg

    c0 = col0 // HEAD
    head = lambda w: pl.BlockSpec((None, s, w), lambda h, *_: (h, 0, 0))
    in_specs = [head(QK), head(QK), head(HEAD), head(HEAD), pl.BlockSpec((s, HEAD), lambda h, *_: (0, c0 + h)),
                pl.BlockSpec((1, HEAD), lambda h, *_: (0, h))]
    out_specs = [head(QK), head(QK), head(HEAD), pl.BlockSpec((SUBLANE, HEAD), lambda h, *_: (0, h))]
    out_shape = [jax.ShapeDtypeStruct((n_heads, s, QK), F32), jax.ShapeDtypeStruct((n_heads, s, QK), F32),
                 jax.ShapeDtypeStruct((n_heads, s, HEAD), F32), jax.ShapeDtypeStruct((SUBLANE, n_heads * HEAD), F32)]
    return _call(body, name="attn_bwd", grid=(n_heads,), after=after, in_specs=in_specs, out_specs=out_specs,
                 out_shape=out_shape, compiler_params=_params("parallel"))(q, k, v, o, d_mix, g)


TILE_M = 1024
TILE_N = 1024


def _up_fwd(h2, w_up):
    s, d = h2.shape
    nb, _, fb = w_up.shape
    tm = min(TILE_M,s)

    def epilogue(acc):
        r = jnp.maximum(acc, 0.0)
        return r * r, r

    blk = pl.BlockSpec((tm, fb), lambda i, j: (i, j))
    return _matmul("up_fwd", h2, w_up, grid=(s // tm, nb),
                   a_spec=pl.BlockSpec((tm, d), lambda i, j: (i, 0)),
                   b_spec=pl.BlockSpec((None, d, fb), lambda i, j: (j, 0, 0)),
                   out_shape=[jax.ShapeDtypeStruct((s, nb * fb), BF16)] * 2, out_specs=[blk, blk],
                   contract=NN, epilogue=epilogue)


def _down_fwd(a, w_down):
    s, f = a.shape
    d = w_down.shape[1]
    tm, tn, tk = min(TILE_M,s), min(TILE_N,d), 2048
    nk = f // tk
    return _matmul("down_fwd", a, w_down, grid=(s // tm, d // tn, nk),
                   a_spec=pl.BlockSpec((tm, tk), lambda i, j, k: (i, k)),
                   b_spec=pl.BlockSpec((tk, tn), lambda i, j, k: (k, j)),
                   out_shape=jax.ShapeDtypeStruct((s, d), F32),
                   out_specs=pl.BlockSpec((tm, tn), lambda i, j, k: (i, j)),
                   contract=NN, nk=nk, acc_shape=(tm, tn))


def _down_bwd_act(d_m, w_down, r, after=()):
    s, d = d_m.shape
    f = w_down.shape[0]
    tm, tn = min(TILE_M,s), min(TILE_N,f)
    blk = pl.BlockSpec((tm, tn), lambda i, j: (i, j))
    return _matmul("down_bwd_act", d_m, w_down, grid=(s // tm, f // tn), after=after,
                   a_spec=pl.BlockSpec((tm, d), lambda i, j: (i, 0)),
                   b_spec=pl.BlockSpec((tn, d), lambda i, j: (j, 0)),
                   out_shape=jax.ShapeDtypeStruct((s, f), BF16), out_specs=blk, contract=NT,
                   extras=(r,), extra_specs=(blk,),
                   epilogue=lambda acc, rv: (acc * (2.0 * rv.astype(F32)),))


def _up_bwd_act(d_up, w_up, after=()):
    s, _ = d_up.shape
    nb, d, fb = w_up.shape
    tm, tn = min(TILE_M, s), min(TILE_N,d)
    pair = 2
    n_after = len(after)

    def body(a_ref, w_ref, *rest):
        o_ref, acc_ref = rest[n_after:]
        k = pl.program_id(2)
        p = None
        for t in range(pair):
            term = lax.dot_general(a_ref[:, t * fb:(t + 1) * fb], w_ref[t], (NT, ((), ())), preferred_element_type=F32)
            p = term if p is None else p + term
        _accumulate(acc_ref, p, k)

        @pl.when(k == nb // pair - 1)
        def _():
            o_ref[...] = acc_ref[...]

    return pl.pallas_call(
        body, name="up_bwd_act", grid=(s // tm, d // tn, nb // pair),
        in_specs=[pl.BlockSpec((tm, pair * fb), lambda i, j, k: (i, k)),
                  pl.BlockSpec((pair, tn, fb), lambda i, j, k: (k, j, 0))] + [ANY] * n_after,
        out_specs=pl.BlockSpec((tm, tn), lambda i, j, k: (i, j)),
        out_shape=jax.ShapeDtypeStruct((s, d), F32),
        scratch_shapes=[pltpu.VMEM((tm, tn), F32)],
        compiler_params=_params("parallel", "parallel", "arbitrary"),
    )(d_up, w_up, *after)


def _half_grad(name, a, b, core, home, received, after, *, grid, a_block, a_map, b_block, b_map, o_block, o_map, out_shape):
    n_after = len(after)
    pick = (lambda ref: ref[0]) if home else (lambda ref: 1 - ref[0])

    def body(core_ref, a_ref, b_ref, *rest):
        acc = lax.dot_general(a_ref[...], b_ref[...], (TN, ((), ())), preferred_element_type=F32)
        if received is not None:
            acc = acc + rest[0][...].astype(F32)
        rest[-1][...] = acc.astype(rest[-1].dtype)

    wrap = lambda fn: (lambda i, j, core_ref: fn(i, j, pick(core_ref)))
    o_spec = pl.BlockSpec(o_block, wrap(o_map))
    extra = [] if received is None else [o_spec]
    operands = [] if received is None else [received]
    return pl.pallas_call(
        body, name=name,
        grid_spec=pltpu.PrefetchScalarGridSpec(
            num_scalar_prefetch=1, grid=grid,
            in_specs=[pl.BlockSpec(a_block, wrap(a_map)), pl.BlockSpec(b_block, wrap(b_map))] + extra + [ANY] * n_after,
            out_specs=o_spec),
        out_shape=out_shape,
        compiler_params=_params("parallel", "parallel"),
    )(core, a, b, *operands, *after)


def _down_half_grad(name, a, d_m, core, home, received=None, after=()):
    s, f = a.shape
    d = d_m.shape[1]
    r = f // N_DEV
    tn = min(TILE_N, d)
    return _half_grad(name, a, d_m, core, home, received, after, grid=(N_CHIP, d // tn),
                      a_block=(s, r), a_map=lambda k, j, p: (0, 2 * k + p),
                      b_block=(s, tn), b_map=lambda k, j, p: (0, j),
                      o_block=(None, r, tn), o_map=lambda k, j, p: (k, 0, j),
                      out_shape=jax.ShapeDtypeStruct((N_CHIP, r, d), BF16))


def _up_half_grad(name, h2, d_up, core, home, received=None, after=()):
    s, d = h2.shape
    fb = d_up.shape[1] // N_DEV
    tm = min(TILE_M, d)
    return _half_grad(name, h2, d_up, core, home, received, after, grid=(d // tm, N_CHIP),
                      a_block=(s, tm), a_map=lambda i, k, p: (0, i),
                      b_block=(s, fb), b_map=lambda i, k, p: (0, 2 * k + p),
                      o_block=(None, tm, fb), o_map=lambda i, k, p: (k, i, 0),
                      out_shape=jax.ShapeDtypeStruct((N_CHIP, d, fb), BF16))


MXU_WIDTH = 256


def _in_pad(in_width):
    return -(-in_width // MXU_WIDTH) * MXU_WIDTH


def _join_col_shards(name, blocks, own, device, pieces=None):
    n, r, w = blocks.shape
    rows = min(ROWS, r)
    pieces = pieces or [(j, 0, w) for j in range(n)]
    used = sum(b - a for _, a, b in pieces)
    width = _in_pad(used)

    def body(dev_ref, x_ref, own_ref, o_ref):
        block = lambda j: jnp.where(dev_ref[0] == j, own_ref[...], x_ref[j])
        cols = [block(j)[:, a:b] for j, a, b in pieces]
        tail = [jnp.zeros((rows, width - used), o_ref.dtype)] if width > used else []
        o_ref[...] = jnp.concatenate(cols + tail, axis=1)

    return pl.pallas_call(
        body, name=name,
        grid_spec=pltpu.PrefetchScalarGridSpec(
            num_scalar_prefetch=1, grid=(r // rows,),
            in_specs=[pl.BlockSpec((n, rows, w), lambda i, dev: (0, i, 0)), pl.BlockSpec((rows, w), lambda i, dev: (i, 0))],
            out_specs=pl.BlockSpec((rows, width), lambda i, dev: (i, 0))),
        out_shape=jax.ShapeDtypeStruct((r, width), blocks.dtype),
        compiler_params=_params("parallel"),
    )(device, blocks, own)


def _permute_q_cols(w_uq, n_heads):
    r = w_uq.shape[0]
    w3 = w_uq.reshape(r, n_heads, QK)
    return jnp.concatenate([w3[:, :, :HEAD].reshape(r, n_heads * HEAD), w3[:, :, HEAD:].reshape(r, n_heads * ROPE)], axis=1)


def _unpermute_q_rows(wt, n_heads):
    r = wt.shape[1]
    nope = wt[:n_heads * HEAD].reshape(n_heads, HEAD, r)
    rope = wt[n_heads * HEAD:].reshape(n_heads, ROPE, r)
    return jnp.concatenate([nope, rope], axis=1).reshape(n_heads * QK, r)


def _local_step(x, tgt, gains, weights, grads, first_after=()):
    pre_mix_g, q_norm_g, kv_norm_g, conv_out_g, attn_out_g, post_mix_g, pre_mlp_g, post_mlp_g = gains
    s, d = x.shape
    conv_width = conv_out_g.shape[1]
    n_groups = conv_width // HEAD
    r_q, r_kv = q_norm_g.shape[1], kv_norm_g.shape[1]
    n_heads = attn_out_g.shape[1] // HEAD
    c_q0 = 3 * conv_width
    c_kv0 = c_q0 + r_q
    c_kr0 = c_kv0 + r_kv
    in_pad = _in_pad(c_kr0 + ROPE)
    tn_in = _fit(in_pad, 6 * MXU_WIDTH)
    tables = _rope_tables(s, n_heads)

    h1 = _rms_fwd("pre_mix_norm", x, pre_mix_g, after=first_after)
    weights.forward(0, (h1,))
    weights.relay(0, tables)
    w_in_p, conv_w = weights.ready(0, ())
    proj = _mm_nn("in_proj", h1, w_in_p, F32, TILE_M, tn_in)
    y_conv = _conv_fwd(proj, conv_w, conv_out_g, n_groups, conv_width + n_heads * HEAD, after=weights.forward(1, (proj,)))
    w_uq_p, w_ukv, w_o = weights.ready(1, (y_conv,))
    qn, q = _norm_up("q_up", proj, (c_q0, r_q), q_norm_g, w_uq_p)
    kvn, kv = _norm_up("kv_up", proj, (c_kv0, r_kv), kv_norm_g, w_ukv)
    qh, kh, vh = _pack_heads(q, kv, proj, c_kr0, tables, n_heads, after=weights.forward(2, (q, kv)))
    o, mix = _attn_fwd(qh, kh, vh, attn_out_g, y_conv, conv_width)
    y = _mm_nn("out_proj", mix, w_o, F32, TILE_M, TILE_N)
    x2, h2 = _mid_fwd(x, y, post_mix_g, pre_mlp_g, after=weights.forward(3, (y,)))
    weights.relay(2, (h2,))
    (w_up,) = weights.ready(2, ())
    a, r = _up_fwd(h2, w_up)
    weights.relay(3, (a,))
    (w_down,) = weights.ready(3, ())
    m = _down_fwd(a, w_down)

    d_out, d_m, dg_post_mlp, loss_part = _head(m, x2, tgt, post_mlp_g)
    core = grads.core
    away = _down_half_grad("down_bwd_w_away", a, d_m, core, home=False)
    d_up = _down_bwd_act(d_m, w_down, r, after=grads.send_away(0, away))
    sums = _down_half_grad("down_bwd_w_home", a, d_m, core, home=True, received=grads.received(0, (d_up,)))
    away = _up_half_grad("up_bwd_w_away", h2, d_up, core, home=False, after=grads.send_sums(0, (sums,)))
    d_h2 = _up_bwd_act(d_up, w_up, after=grads.send_away(1, away))
    sums = _up_half_grad("up_bwd_w_home", h2, d_up, core, home=True, received=grads.received(1, (d_h2,)))
    d_x2, d_y, dg_pre_mlp, dg_post_mix = _mid_bwd(x2, y, d_out, d_h2, pre_mlp_g, post_mix_g, after=grads.send_sums(1, (sums,)))
    d_mix = _mm_nt("out_proj_bwd_act", d_y, w_o, F32, TILE_M, TILE_N)
    gw_o = _mm_tn("out_proj_bwd_w", mix, d_y, BF16, TILE_M, TILE_N)
    dqh, dkh, dvh, dg_attn = _attn_bwd(qh, kh, vh, o, d_mix, attn_out_g, conv_width, after=grads.full(2, (gw_o,)))
    d_q, d_kv, d_kr = _unpack_heads(dqh, dkh, dvh, tables, n_heads)
    gw_uq_t = _mm_tn("q_up_bwd_w", d_q, qn, F32, TILE_M, TILE_N)
    gw_ukv = _mm_tn("kv_up_bwd_w", kvn, d_kv, BF16, TILE_M, TILE_N)
    d_cq, dg_q = _up_norm_bwd("q_up_bwd_act", d_q, w_uq_p, proj, (c_q0, r_q), q_norm_g, after=grads.full(3, (gw_uq_t, gw_ukv)))
    d_ckv, dg_kv = _up_norm_bwd("kv_up_bwd_act", d_kv, w_ukv, proj, (c_kv0, r_kv), kv_norm_g)
    d_u, d_b, d_c, dg_conv, dw_conv = _conv_bwd(proj, d_mix, conv_w, conv_out_g, n_groups)
    d_proj = jnp.concatenate([d_u, d_b, d_c, d_cq, d_ckv, d_kr, jnp.zeros((s, in_pad - c_kr0 - LANE), BF16)], axis=1)
    gw_in_t = _mm_tn("in_proj_bwd_w", d_proj, h1, F32, tn_in, TILE_N)
    updated = grads.update_now(0, grads.send_away(4, gw_in_t))
    d_h1 = _mm_nt("in_proj_bwd_act", d_proj, w_in_p, F32, TILE_M, TILE_N, after=grads.full(4, (gw_in_t,), received=updated))
    grad_x, dg_pre_mix = _first_bwd(x, pre_mix_g, d_h1, d_x2)

    small = [dg_pre_mix, dg_q, dg_kv, dg_conv, dg_attn, dg_post_mix, dg_pre_mlp, dg_post_mlp,
             dw_conv[0], dw_conv[1], dw_conv[2], loss_part]
    return grad_x, jnp.concatenate(small, axis=1)


HBM = pl.BlockSpec(memory_space=pltpu.HBM)
SEM = pl.BlockSpec(memory_space=pltpu.SEMAPHORE)
IN_VMEM = pl.BlockSpec(memory_space=pltpu.VMEM)
SPLIT = pltpu.CompilerParams(has_side_effects=pltpu.SideEffectType.DATAFLOW_SIDE_EFFECTING)


def _in_hbm(a):
    return pltpu.with_memory_space_constraint(a, pltpu.HBM)


def _hbm_like(a):
    return pltpu.HBM(a.shape, a.dtype)


def _place():
    x, y, c = lax.axis_index("x"), lax.axis_index("y"), lax.axis_index("c")
    other_chips = [(1 - x, y), (x, 1 - y), (1 - x, 1 - y)]
    return x, y, c, other_chips


def _block(px, py, pc):
    return 4 * px + 2 * py + pc


def _await(block, sem):
    pltpu.make_async_copy(block, block, sem).wait()


def _relay_route(x, y, c):
    came_from = ((1 - x) * (1 - c) + x * c, y * (1 - c) + (1 - y) * c)
    goes_to = (x * (1 - c) + (1 - x) * c, (1 - y) * (1 - c) + y * c)
    return came_from, goes_to


def _gather_start(name, shards, groups, relayed=(), after=()):
    n, ng = len(shards), len(groups)
    lands = [lax.empty((N_DEV, *a.shape), a.dtype) for a in shards]

    def body(*refs):
        src, land = refs[:n], refs[n:2 * n]
        sems, token = refs[2 * n + len(after):2 * n + len(after) + 2 * ng], refs[-1]
        x, y, c, chips = _place()
        targets = [(x, y, 1 - c)] + [(*chip, c) for chip in chips]
        for gi, group in enumerate(groups):
            for i, w in enumerate(group):
                for k, to in enumerate(targets[:3] if gi in relayed else targets):
                    pltpu.make_async_remote_copy(
                        src_ref=src[w], dst_ref=land[w].at[_block(x, y, c)],
                        send_sem=sems[2 * gi].at[4 * i + k], recv_sem=sems[2 * gi + 1].at[4 * i + k],
                        device_id=to, device_id_type=MESH).start()
        token[...] = jnp.zeros_like(token)

    sem_shapes = [pltpu.SemaphoreType.DMA((4 * len(g),)) for g in groups for _ in range(2)]
    out = pl.pallas_call(
        body, name=name,
        in_specs=[HBM] * (2 * n) + [ANY] * len(after),
        out_specs=[SEM] * (2 * ng) + [HBM] * (2 * n) + [IN_VMEM],
        out_shape=sem_shapes + [_hbm_like(a) for a in shards] + [_hbm_like(a) for a in lands]
        + [jax.ShapeDtypeStruct((SUBLANE, LANE), F32)],
        input_output_aliases={i: 2 * ng + i for i in range(2 * n)},
        compiler_params=SPLIT,
    )(*[_in_hbm(a) for a in shards], *[_in_hbm(a) for a in lands], *after)
    sems = [(out[2 * gi], out[2 * gi + 1]) for gi in range(ng)]
    return sems, out[2 * ng:2 * ng + n], out[2 * ng + n:2 * ng + 2 * n], out[-1]


def _gather_forward(name, shards, lands, send1, recv1, after, relayed=False):
    n = len(lands)

    def body(*refs):
        src, land = refs[:n], refs[n:2 * n]
        s1, r1 = refs[2 * n], refs[2 * n + 1]
        s2, r2 = refs[2 * n + 2 + len(after)], refs[2 * n + 3 + len(after)]
        x, y, c, chips = _place()
        me, sibling = (x, y, c), (x, y, 1 - c)
        for j, chip in enumerate(chips[:2] if relayed else chips):
            for i in range(n):
                blk = land[i].at[_block(*chip, c)]
                pltpu.make_async_remote_copy(src_ref=blk, dst_ref=blk, send_sem=s1.at[4 * i + 1 + j], recv_sem=r1.at[4 * i + 1 + j],
                                             device_id=me, device_id_type=MESH).wait_recv()
                pltpu.make_async_remote_copy(src_ref=blk, dst_ref=blk, send_sem=s2.at[3 * i + j], recv_sem=r2.at[3 * i + j],
                                             device_id=sibling, device_id_type=MESH).start()
        if relayed:
            came_from, goes_to = _relay_route(x, y, c)
            for i in range(n):
                blk = land[i].at[_block(*came_from, c)]
                pltpu.make_async_remote_copy(src_ref=blk, dst_ref=blk, send_sem=s2.at[3 * i + 2], recv_sem=r2.at[3 * i + 2],
                                             device_id=(*goes_to, c), device_id_type=MESH).start()
        for i in range(n):
            blk = land[i].at[_block(x, y, 1 - c)]
            pltpu.make_async_remote_copy(src_ref=blk, dst_ref=blk, send_sem=s1.at[4 * i], recv_sem=r1.at[4 * i],
                                         device_id=me, device_id_type=MESH).wait_recv()
            for k in range(3 if relayed else 4):
                pltpu.make_async_remote_copy(src_ref=src[i], dst_ref=land[i].at[_block(x, y, c)], send_sem=s1.at[4 * i + k],
                                             recv_sem=r1.at[4 * i + k], device_id=sibling, device_id_type=MESH).wait_send()

    sem = pltpu.SemaphoreType.DMA((3 * n,))
    out = pl.pallas_call(
        body, name=name,
        in_specs=[HBM] * (2 * n) + [SEM, SEM] + [ANY] * len(after),
        out_specs=[SEM, SEM] + [HBM] * n,
        out_shape=[sem, sem] + [_hbm_like(a) for a in lands],
        input_output_aliases={n + i: 2 + i for i in range(n)},
        compiler_params=SPLIT,
    )(*shards, *lands, send1, recv1, *after)
    return (out[0], out[1]), out[2:]


def _gather_relay_forward(name, lands, send2, recv2, after):
    n = len(lands)

    def body(*refs):
        land, s2, r2 = refs[:n], refs[n], refs[n + 1]
        s3, r3 = refs[n + 2 + len(after)], refs[n + 3 + len(after)]
        x, y, c, _ = _place()
        me, sibling = (x, y, c), (x, y, 1 - c)
        came_from, _ = _relay_route(x, y, c)
        for i in range(n):
            blk = land[i].at[_block(1 - x, 1 - y, c)]
            pltpu.make_async_remote_copy(src_ref=blk, dst_ref=blk, send_sem=s2.at[3 * i + 2], recv_sem=r2.at[3 * i + 2],
                                         device_id=me, device_id_type=MESH).wait_recv()
            pltpu.make_async_remote_copy(src_ref=blk, dst_ref=blk, send_sem=s3.at[i], recv_sem=r3.at[i],
                                         device_id=sibling, device_id_type=MESH).start()
            sent = land[i].at[_block(*came_from, c)]
            pltpu.make_async_remote_copy(src_ref=sent, dst_ref=sent, send_sem=s2.at[3 * i + 2], recv_sem=r2.at[3 * i + 2],
                                         device_id=me, device_id_type=MESH).wait_send()

    sem = pltpu.SemaphoreType.DMA((n,))
    out = pl.pallas_call(
        body, name=name,
        in_specs=[HBM] * n + [SEM, SEM] + [ANY] * len(after),
        out_specs=[SEM, SEM] + [HBM] * n,
        out_shape=[sem, sem] + [_hbm_like(a) for a in lands],
        input_output_aliases={i: 2 + i for i in range(n)},
        compiler_params=SPLIT,
    )(*lands, send2, recv2, *after)
    return (out[0], out[1]), out[2:]


def _gather_wait(name, lands, send2, recv2, after, relay_sems=None):
    n = len(lands)
    n_sems = 2 if relay_sems is None else 4

    def body(*refs):
        land, s2, r2 = refs[:n], refs[n], refs[n + 1]
        for i in range(n):
            for j in range(3 if relay_sems is None else 2):
                _await(land[i].at[0], r2.at[3 * i + j])
                _await(land[i].at[0], s2.at[3 * i + j])
            if relay_sems is not None:
                _await(land[i].at[0], refs[n + 3].at[i])
                _await(land[i].at[0], refs[n + 2].at[i])

    return pl.pallas_call(
        body, name=name,
        in_specs=[HBM] * n + [SEM] * n_sems + [ANY] * len(after), out_specs=[HBM] * n, out_shape=[_hbm_like(a) for a in lands],
        input_output_aliases={i: i for i in range(n)},
        compiler_params=SPLIT,
    )(*lands, send2, recv2, *(relay_sems or ()), *after)


def _pair_exchange(name, grads, shard_rows):
    n = len(grads)
    shapes = [(g.shape[1:] if r is None else (r, g.shape[1])) for g, r in zip(grads, shard_rows)]

    def body(*refs):
        ins, recv = refs[:n], refs[n:2 * n]
        send_sems, recv_sems = refs[2 * n:]
        x, y, c, _ = _place()
        sends = []
        for w in range(n):
            for k in range(N_CHIP):
                j, r = 2 * k + 1 - c, shard_rows[w]
                src = ins[w].at[j] if r is None else ins[w].at[pl.ds(pl.multiple_of(j * r, SUBLANE), r), :]
                sends.append(pltpu.make_async_remote_copy(
                    src_ref=src, dst_ref=recv[w].at[k],
                    send_sem=send_sems.at[w, k], recv_sem=recv_sems.at[w, k],
                    device_id=(x, y, 1 - c), device_id_type=MESH))
        for cp in sends:
            cp.start()
        for cp in sends:
            cp.wait()

    return pl.pallas_call(
        body, name=name,
        in_specs=[ANY] * n, out_specs=[ANY] * n,
        out_shape=[jax.ShapeDtypeStruct((N_CHIP, *shape), g.dtype) for g, shape in zip(grads, shapes)],
        scratch_shapes=[pltpu.SemaphoreType.DMA((n, N_CHIP))] * 2,
    )(*grads)


def _pair_sum_rows(name, grad, received, core):
    _, r, c = received.shape
    tc = _fit(c, 512)

    def body(core_ref, a_ref, b_ref, o_ref):
        o_ref[...] = (a_ref[...] + b_ref[...]).astype(o_ref.dtype)

    spec = pl.BlockSpec((None, r, tc), lambda k, i, core_ref: (k, 0, i))
    return pl.pallas_call(
        body, name=name,
        grid_spec=pltpu.PrefetchScalarGridSpec(
            num_scalar_prefetch=1, grid=(N_CHIP, c // tc),
            in_specs=[pl.BlockSpec((r, tc), lambda k, i, core_ref: (2 * k + core_ref[0], i)), spec],
            out_specs=spec),
        out_shape=jax.ShapeDtypeStruct(received.shape, BF16),
        compiler_params=_params("parallel", "parallel"),
    )(core, grad, received)


def _pair_sum(name, grad, received, core):
    _, r, c = received.shape
    rows = min(ROWS, r)
    assert r % rows == 0

    def body(core_ref, a_ref, b_ref, o_ref):
        o_ref[...] = (a_ref[...].astype(F32) + b_ref[...].astype(F32)).astype(o_ref.dtype)

    spec = pl.BlockSpec((None, rows, c), lambda k, i, core_ref: (k, i, 0))
    return pl.pallas_call(
        body, name=name,
        grid_spec=pltpu.PrefetchScalarGridSpec(
            num_scalar_prefetch=1, grid=(N_CHIP, r // rows),
            in_specs=[pl.BlockSpec((None, None, rows, c), lambda k, i, core_ref: (k, core_ref[0], i, 0)), spec],
            out_specs=spec),
        out_shape=jax.ShapeDtypeStruct(received.shape, received.dtype),
        compiler_params=_params("parallel", "parallel"),
    )(core, grad.reshape(N_CHIP, 2, r, c), received)


def _away_shard(src, k, c, shard_rows):
    if shard_rows is None:
        return src.at[k]
    return src.at[pl.ds(pl.multiple_of((2 * k + 1 - c) * shard_rows, SUBLANE), shard_rows), :]


def _pair_send_start(name, away, shard_rows=None):
    shape = away.shape if shard_rows is None else (N_CHIP, shard_rows, away.shape[1])
    land = lax.empty(shape, away.dtype)

    def body(src, dst, send, recv, src_thru, dst_thru, token):
        x, y, c, _ = _place()
        for k in range(N_CHIP):
            pltpu.make_async_remote_copy(src_ref=_away_shard(src, k, c, shard_rows), dst_ref=dst.at[k], send_sem=send.at[k],
                                         recv_sem=recv.at[k], device_id=(x, y, 1 - c), device_id_type=MESH).start()
        token[...] = jnp.zeros_like(token)

    sem = pltpu.SemaphoreType.DMA((N_CHIP,))
    out = pl.pallas_call(
        body, name=name,
        in_specs=[HBM, HBM], out_specs=[SEM, SEM, HBM, HBM, IN_VMEM],
        out_shape=[sem, sem, _hbm_like(away), _hbm_like(land), jax.ShapeDtypeStruct((SUBLANE, LANE), F32)],
        input_output_aliases={0: 2, 1: 3},
        compiler_params=SPLIT,
    )(_in_hbm(away), _in_hbm(land))
    return (out[0], out[1]), out[2], out[3], out[4]


def _pair_send_wait(name, sems, src, land, after, shard_rows=None):
    def body(src_ref, dst_ref, send, recv, *rest):
        for k in range(N_CHIP):
            _await(dst_ref.at[k], send.at[k])
            _await(dst_ref.at[k], recv.at[k])

    return pl.pallas_call(
        body, name=name,
        in_specs=[HBM, HBM, SEM, SEM] + [ANY] * len(after), out_specs=HBM, out_shape=_hbm_like(land),
        input_output_aliases={1: 0},
        compiler_params=SPLIT,
    )(src, land, *sems, *after)


def _chip_send_start(name, sums):
    n = len(sums)
    lands = [lax.empty(a.shape, a.dtype) for a in sums]

    def body(*refs):
        src, land = refs[:n], refs[n:2 * n]
        send, recv, token = refs[2 * n], refs[2 * n + 1], refs[-1]
        x, y, c, chips = _place()
        for w in range(n):
            for j, (px, py) in enumerate(chips):
                pltpu.make_async_remote_copy(
                    src_ref=src[w].at[2 * px + py], dst_ref=land[w].at[2 * x + y],
                    send_sem=send.at[3 * w + j], recv_sem=recv.at[3 * w + j],
                    device_id=(px, py, c), device_id_type=MESH).start()
        token[...] = jnp.zeros_like(token)

    sem = pltpu.SemaphoreType.DMA((3 * n,))
    out = pl.pallas_call(
        body, name=name,
        in_specs=[HBM] * (2 * n),
        out_specs=[SEM, SEM] + [HBM] * (2 * n) + [IN_VMEM],
        out_shape=[sem, sem] + [_hbm_like(a) for a in sums] + [_hbm_like(a) for a in lands]
        + [jax.ShapeDtypeStruct((SUBLANE, LANE), F32)],
        input_output_aliases={i: 2 + i for i in range(2 * n)},
        compiler_params=SPLIT,
    )(*[_in_hbm(a) for a in sums], *[_in_hbm(a) for a in lands])
    return (out[0], out[1]), out[2:2 + n], out[2 + n:2 + 2 * n], out[-1]


def _chip_send_wait(name, groups, after):
    counts = [len(g[1]) for g in groups]
    n = sum(counts)

    def body(*refs):
        land = refs[n:2 * n]
        sems = refs[2 * n:2 * n + 2 * len(groups)]
        w = 0
        for gi, count in enumerate(counts):
            for i in range(count):
                for j in range(3):
                    _await(land[w].at[0], sems[2 * gi].at[3 * i + j])
                    _await(land[w].at[0], sems[2 * gi + 1].at[3 * i + j])
                w += 1

    sums = [a for g in groups for a in g[1]]
    lands = [a for g in groups for a in g[2]]
    sems = [s for g in groups for s in g[0]]
    return pl.pallas_call(
        body, name=name,
        in_specs=[HBM] * (2 * n) + [SEM] * len(sems) + [ANY] * len(after),
        out_specs=[HBM] * n, out_shape=[_hbm_like(a) for a in lands],
        input_output_aliases={n + i: i for i in range(n)},
        compiler_params=SPLIT,
    )(*sums, *lands, *sems, *after)


def _small_all_reduce(part, after=()):
    _, w = part.shape

    def body(p_ref, *rest):
        o_ref, buf, send_sems, recv_sems = rest[len(after):]
        x, y, c, _ = _place()
        me = 4 * x + 2 * y + c
        buf[me] = jnp.sum(p_ref[...], axis=0, keepdims=True)
        copies = []
        for k in range(1, N_DEV):
            dx, dy, dc = (k >> 2) & 1, (k >> 1) & 1, k & 1
            copies.append(pltpu.make_async_remote_copy(
                src_ref=buf.at[me], dst_ref=buf.at[me], send_sem=send_sems.at[k - 1], recv_sem=recv_sems.at[k - 1],
                device_id=(x ^ dx, y ^ dy, c ^ dc), device_id_type=MESH))
        for cp in copies:
            cp.start()
        for cp in copies:
            cp.wait()
        tot = buf[0]
        for d in range(1, N_DEV):
            tot = tot + buf[d]
        o_ref[...] = tot
        loss = jnp.sum(tot[:, w - LANE:], axis=1, keepdims=True)
        o_ref[:, w - LANE:] = jnp.broadcast_to(loss, (1, LANE))

    return pl.pallas_call(
        body, name="small_all_reduce",
        in_specs=[IN_VMEM] + [ANY] * len(after), out_specs=IN_VMEM,
        out_shape=jax.ShapeDtypeStruct((1, w), F32),
        scratch_shapes=[pltpu.VMEM((N_DEV, 1, w), F32), pltpu.SemaphoreType.DMA((N_DEV - 1,)), pltpu.SemaphoreType.DMA((N_DEV - 1,))],
        compiler_params=pltpu.CompilerParams(vmem_limit_bytes=VMEM_LIMIT_BYTES),
    )(part, *after)


def _adamw(w, g, m, v):
    m = ADAM_B1 * m + (1.0 - ADAM_B1) * g
    v = ADAM_B2 * v + (1.0 - ADAM_B2) * (g * g)
    m_hat = m / (1.0 - ADAM_B1 ** ADAM_STEP)
    v_hat = v / (1.0 - ADAM_B2 ** ADAM_STEP)
    delta = -ADAM_LR * (m_hat / (jnp.sqrt(v_hat) + ADAM_EPS) + ADAM_WD * w)
    return delta, m, v


def _sum_adam_block(chip_ref, p_ref, own_ref, w_ref, m_ref, v_ref, g_ref, d_ref, mo_ref, vo_ref):
    g = None
    for k in range(N_CHIP):
        term = jnp.where(chip_ref[0] == k, own_ref[...], p_ref[k]).astype(F32)
        g = term if g is None else g + term
    g_ref[...] = g
    d_ref[...], mo_ref[...], vo_ref[...] = _adamw(w_ref[...], g, m_ref[...], v_ref[...])


def _sum_adam(name, parts, sums, chip, w, m, v, after=()):
    _, r, c = w.shape
    n_after = len(after)
    by_rows = r % ROWS == 0 or r < ROWS
    tr, tc = (min(ROWS, r), c) if by_rows else (r, _fit(c, 512))
    at = (lambda i: (i, 0)) if by_rows else (lambda i: (0, i))

    def body(chip_ref, p_ref, own_ref, w_ref, m_ref, v_ref, *rest):
        _sum_adam_block(chip_ref, p_ref, own_ref, w_ref, m_ref, v_ref, *rest[n_after:])

    blk = pl.BlockSpec((None, tr, tc), lambda i, chip_ref: (0, *at(i)))
    out = jax.ShapeDtypeStruct((1, r, c), F32)
    return pl.pallas_call(
        body, name=name,
        grid_spec=pltpu.PrefetchScalarGridSpec(
            num_scalar_prefetch=1, grid=(r // tr if by_rows else c // tc,),
            in_specs=[pl.BlockSpec((N_CHIP, tr, tc), lambda i, chip_ref: (0, *at(i))),
                      pl.BlockSpec((None, tr, tc), lambda i, chip_ref: (chip_ref[0], *at(i))), blk, blk, blk]
            + [ANY] * n_after,
            out_specs=[blk] * 4),
        out_shape=[out] * 4,
        compiler_params=_params("parallel"),
    )(chip, parts, sums, w, m, v, *after)


def _adam_gains(total, ws, ms, vs):
    n = len(ws)
    widths = [w.shape[1] for w in ws]

    def body(t_ref, *refs):
        w_refs, m_refs, v_refs, outs = refs[:n], refs[n:2 * n], refs[2 * n:3 * n], refs[3 * n:]
        off = 0
        for i in range(n):
            g = t_ref[:, off:off + widths[i]]
            off += widths[i]
            g_ref, d_ref, mo_ref, vo_ref = outs[4 * i:4 * i + 4]
            g_ref[...] = g
            d_ref[...], mo_ref[...], vo_ref[...] = _adamw(w_refs[i][...], g, m_refs[i][...], v_refs[i][...])

    out = pl.pallas_call(
        body, name="adam_gains",
        out_shape=[jax.ShapeDtypeStruct(w.shape, F32) for w in ws for _ in range(4)],
    )(total, *ws, *ms, *vs)
    return [tuple(out[4 * i:4 * i + 4]) for i in range(n)]


def _adam_taps(total, first_col, device, w, m, v):
    _, n_taps, cw = w.shape
    col_block = lambda t, dev: (0, first_col // cw + t * N_DEV + dev[0])
    tap = pl.BlockSpec((None, 1, cw), lambda t, dev: (t, 0, 0))

    def body(dev_ref, t_ref, w_ref, m_ref, v_ref, g_ref, d_ref, mo_ref, vo_ref):
        g = t_ref[...]
        g_ref[...] = g
        d_ref[...], mo_ref[...], vo_ref[...] = _adamw(w_ref[...], g, m_ref[...], v_ref[...])

    shape3 = (n_taps, 1, cw)
    out = pl.pallas_call(
        body, name="adam_taps",
        grid_spec=pltpu.PrefetchScalarGridSpec(
            num_scalar_prefetch=1, grid=(n_taps,),
            in_specs=[pl.BlockSpec((1, cw), col_block), tap, tap, tap], out_specs=[tap] * 4),
        out_shape=[jax.ShapeDtypeStruct(shape3, F32)] * 4,
    )(device, total, w.reshape(shape3), m.reshape(shape3), v.reshape(shape3))
    return tuple(o.reshape(w.shape) for o in out)


def kernel(x, pre_mix_g, w_in, conv_w, q_norm_g, w_uq, kv_norm_g, w_ukv, conv_out_g, attn_out_g, w_o, post_mix_g, pre_mlp_g, w_up, w_down, post_mlp_g, loss_target, m_pre_mix_g, m_w_in, m_conv_w, m_q_norm_g, m_w_uq, m_kv_norm_g, m_w_ukv, m_conv_out_g, m_attn_out_g, m_w_o, m_post_mix_g, m_pre_mlp_g, m_w_up, m_w_down, m_post_mlp_g, v_pre_mix_g, v_w_in, v_conv_w, v_q_norm_g, v_w_uq, v_kv_norm_g, v_w_ukv, v_conv_out_g, v_attn_out_g, v_w_o, v_post_mix_g, v_pre_mlp_g, v_w_up, v_w_down, v_post_mlp_g):
    me = 4 * lax.axis_index("x") + 2 * lax.axis_index("y") + lax.axis_index("c")
    core = lax.axis_index("c").astype(jnp.int32).reshape(1)
    chip = (2 * lax.axis_index("x") + lax.axis_index("y")).astype(jnp.int32).reshape(1)
    gains = (pre_mix_g, q_norm_g, kv_norm_g, conv_out_g, attn_out_g, post_mix_g, pre_mlp_g, post_mlp_g)
    gain_m = (m_pre_mix_g, m_q_norm_g, m_kv_norm_g, m_conv_out_g, m_attn_out_g, m_post_mix_g, m_pre_mlp_g, m_post_mlp_g)
    gain_v = (v_pre_mix_g, v_q_norm_g, v_kv_norm_g, v_conv_out_g, v_attn_out_g, v_post_mix_g, v_pre_mlp_g, v_post_mlp_g)
    names = ("w_in", "w_uq", "w_ukv", "w_o", "w_up", "w_down")
    big = dict(zip(names, (w_in, w_uq, w_ukv, w_o, w_up, w_down)))
    big_m = dict(zip(names, (m_w_in, m_w_uq, m_w_ukv, m_w_o, m_w_up, m_w_down)))
    big_v = dict(zip(names, (v_w_in, v_w_uq, v_w_ukv, v_w_o, v_w_up, v_w_down)))
    n_heads = attn_out_g.shape[1] // HEAD
    n_taps = conv_w.shape[1]

    gathered = ("w_in", "conv", "w_uq", "w_ukv", "w_o", "w_up", "w_down")
    gather_groups = ((0, 1), (2, 3, 4), (5,), (6,))
    taps = jnp.pad(conv_w[0], ((0, SUBLANE - n_taps), (0, 0)))
    relayed_groups = (0, 2, 3)
    sems1, shards, lands, token = _gather_start("gather_start_first", [w_in[0].astype(BF16), taps], ((0, 1),), relayed=(0,))
    sems1, shards, lands = list(sems1), list(shards), list(lands)
    behind = token[0, 0]
    rest = [(big[nm][0] + behind).astype(BF16) for nm in gathered[2:]]

    def start_rest(after):
        sems_b, shards_b, lands_b, started = _gather_start("gather_start_rest", rest, ((0, 1, 2), (3,), (4,)), relayed=(1, 2),
                                                          after=after)
        sems1.extend(sems_b)
        shards.extend(shards_b)
        lands.extend(lands_b)
        return started

    cols = lambda a: jnp.concatenate([a[j] for j in range(N_DEV)], axis=1)
    rows = lambda a: a.reshape(N_DEV * a.shape[1], a.shape[2])
    device = me.astype(jnp.int32).reshape(1)
    own_in = lambda a, shard: lax.dynamic_update_index_in_dim(a, shard, me, 0)
    q_pieces = [(h, 0, HEAD) for h in range(n_heads)] + [(h, HEAD, QK) for h in range(n_heads)]
    ready = {
        "w_in": lambda a, shard: _join_col_shards("join_w_in", a, shard, device),
        "conv": lambda a, shard: cols(own_in(a, shard))[:n_taps],
        "w_uq": lambda a, shard: _join_col_shards("join_w_uq", a, shard, device, q_pieces),
        "w_ukv": lambda a, shard: cols(own_in(a, shard)),
        "w_o": lambda a, shard: rows(own_in(a, shard)),
        "w_up": own_in,
        "w_down": lambda a, shard: rows(own_in(a, shard)),
    }
    assert w_uq.shape[2] == QK

    class Weights:
        def __init__(self):
            self.passed, self.relayed = {}, {}

        def forward(self, group, after):
            idx = gather_groups[group]
            if group == 0:
                after = (*after, *rest)
            self.passed[group] = _gather_forward(f"gather_forward_{group}", [shards[i] for i in idx], [lands[i] for i in idx],
                                                 *sems1[group], after, relayed=group in relayed_groups)
            return tuple(self.passed[group][1])

        def relay(self, group, after):
            sems2, mid = self.passed[group]
            self.relayed[group], mid = _gather_relay_forward(f"gather_relay_{group}", mid, *sems2, after)
            self.passed[group] = (sems2, mid)
            if group == 0:
                start_rest(tuple(mid))
            return tuple(mid)

        def ready(self, group, after):
            sems2, mid = self.passed[group]
            full = _gather_wait(f"gather_wait_{group}", mid, *sems2, after, relay_sems=self.relayed.get(group))
            out = []
            return [ready[gathered[i]](a, shards[i]) for i, a in zip(gather_groups[group], full)]

    weights = Weights()

    col_blocks = lambda g: g.reshape(g.shape[0], N_DEV, g.shape[1] // N_DEV).transpose(1, 0, 2)
    row_blocks = lambda g: g.reshape(N_DEV, g.shape[0] // N_DEV, g.shape[1])
    grad_groups = (("w_down",), ("w_up",), ("w_o",), ("w_uq", "w_ukv"), ("w_in",))
    transposed = {"w_in": w_in.shape[2], "w_uq": w_uq.shape[2]}
    to_blocks = {
        "w_in": lambda g: g, "w_uq": lambda g: _unpermute_q_rows(g, n_heads),
        "w_ukv": col_blocks, "w_o": row_blocks, "w_up": lambda g: g, "w_down": row_blocks,
    }
    in_flight = []

    class Grads:
        def __init__(self):
            self.core = core
            self.away = {}

        def send_sums(self, group, sums):
            sems, sums, parts, tok = _chip_send_start(f"chip_send_start_{group}", list(sums))
            in_flight.append((sems, sums, parts))
            return (tok,)

        def full(self, group, arrays, received=None):
            nms = grad_groups[group]
            if received is None:
                blocks = [to_blocks[nm](g) for nm, g in zip(nms, arrays)]
                got = _pair_exchange(f"pair_exchange_{group}", blocks, [transposed.get(nm) for nm in nms])
            else:
                blocks, got = [self.away[group][1]], [self.received(group, received)]
            sums = [(_pair_sum_rows if nm in transposed else _pair_sum)(f"pair_sum_{nm}", g, r, core)
                    for nm, g, r in zip(nms, blocks, got)]
            return self.send_sums(group, sums)

        def send_away(self, group, half):
            nm = grad_groups[group][0]
            rows = transposed.get(nm)
            sems, src, land, tok = _pair_send_start(f"pair_send_start_{group}", half if rows is None else to_blocks[nm](half), rows)
            self.away[group] = (sems, src, land, rows)
            return (tok,)

        def received(self, group, after):
            sems, src, land, rows = self.away[group]
            return _pair_send_wait(f"pair_send_wait_{group}", sems, src, land, after, rows)

        def update_now(self, group, after):
            return update(str(group), group, group + 1, after)

    big_out = {}

    def update(tag, first, last, after):
        picked = [i for i in range(first, last) if grad_groups[i][0] not in big_out]
        groups = [in_flight[i] for i in picked]
        parts = _chip_send_wait("chip_send_wait_" + tag, groups, after)
        nms = [nm for i in picked for nm in grad_groups[i]]
        sums = [a for _, s, _ in groups for a in s]
        for nm, p, s in zip(nms, parts, sums):
            view = (lambda a: jnp.swapaxes(a, 1, 2)) if nm in transposed else (lambda a: a)
            out = _sum_adam("adam_" + nm, p, s, chip, view(big[nm]), view(big_m[nm]), view(big_v[nm]), after=after)
            after = (out[0],)
            big_out[nm] = [view(o) for o in out]
        return after

    grad_x, small = _local_step(x[0], loss_target[0], gains, weights, Grads(), first_after=(token,))

    after = update("early", 0, len(in_flight) - 1, (grad_x,))
    total = _small_all_reduce(small, after=after)
    update("late", len(in_flight) - 1, len(in_flight), (total,))
    big_out = [big_out[nm] for nm in names]

    gain_out = _adam_gains(total, gains, gain_m, gain_v)
    taps_out = _adam_taps(total, sum(g.shape[1] for g in gains), me.astype(jnp.int32).reshape(1), conv_w, m_conv_w, v_conv_w)
    loss = total[0, total.shape[1] - 1]

    order = (0, "w_in", "conv", 1, "w_uq", 2, "w_ukv", 3, 4, "w_o", 5, 6, "w_up", "w_down", 7)
    by_name = dict(zip(names, big_out))
    outs = [loss, grad_x[None]]
    for kind in range(4):
        for item in order:
            if item == "conv":
                outs.append(taps_out[kind])
            elif isinstance(item, int):
                outs.append(gain_out[item][kind])
            else:
                outs.append(by_name[item][kind])
    return tuple(outs)
```

```python
import math

import jax
import jax.numpy as jnp
from jax import lax
from jax.experimental import pallas as pl
from jax.experimental.pallas import tpu as pltpu

F32 = jnp.float32
BF16 = jnp.bfloat16

EPS = 1e-6
NEG_INF = -1e30
HEAD = 128
ROPE = 64
QK = HEAD + ROPE
CHUNK = 64
ROPE_THETA = 10000.0
ADAM_LR, ADAM_B1, ADAM_B2, ADAM_EPS, ADAM_WD, ADAM_STEP = 0.001, 0.9, 0.999, 1e-08, 0.01, 10

LANE = 128
SUBLANE = 8
VMEM_LIMIT_BYTES = 56 * 1024 * 1024

N_DEV = 8
N_CHIP = 4
MESH = pl.DeviceIdType.MESH


def _params(*sem):
    return pltpu.CompilerParams(dimension_semantics=sem, vmem_limit_bytes=VMEM_LIMIT_BYTES)


ANY = pl.BlockSpec(memory_space=pl.ANY)


def _call(body, *, in_specs, after=(), **kw):
    n_in, n_after = len(in_specs), len(after)

    def ordered(*refs):
        body(*refs[:n_in], *refs[n_in + n_after:])

    call = pl.pallas_call(ordered, in_specs=[*in_specs, *[ANY] * n_after], **kw)
    return lambda *operands: call(*operands, *after)


def _sublane_sum(v):
    r, w = v.shape
    return jnp.sum(v.reshape(r // SUBLANE, SUBLANE, w), axis=0)


def _rstd(x):
    return lax.rsqrt(jnp.mean(x * x, axis=-1, keepdims=True) + EPS)


def _rms_bwd(x, g, dy):
    r = _rstd(x)
    xh = x * r
    dxh = dy * g
    dx = r * (dxh - xh * jnp.mean(dxh * xh, axis=-1, keepdims=True))
    return dx, dy * xh


def _accumulate(ref, val, step):
    @pl.when(step == 0)
    def _():
        ref[...] = val

    @pl.when(step > 0)
    def _():
        ref[...] += val


NN = ((1,), (0,))
NT = ((1,), (1,))
TN = ((0,), (0,))


def _matmul(name, a, b, *, grid, a_spec, b_spec, out_shape, out_specs, contract, nk=1, acc_shape=None,
            extras=(), extra_specs=(), epilogue=None, after=()):
    multi = isinstance(out_shape, (tuple, list))
    out_shapes = tuple(out_shape) if multi else (out_shape,)
    n_out = len(out_shapes)
    n_extra = len(extras)

    def body(a_ref, b_ref, *rest):
        x_refs = rest[:n_extra]
        o_refs = rest[n_extra:n_extra + n_out]

        def emit(acc):
            vals = epilogue(acc, *[r[...] for r in x_refs]) if epilogue else (acc,)
            for r, v in zip(o_refs, vals):
                r[...] = v.astype(r.dtype)

        p = lax.dot_general(a_ref[...], b_ref[...], (contract, ((), ())), preferred_element_type=F32)
        if nk == 1:
            emit(p)
        else:
            acc_ref = rest[n_extra + n_out]
            k = pl.program_id(2)
            _accumulate(acc_ref, p, k)

            @pl.when(k == nk - 1)
            def _():
                emit(acc_ref[...])

    sem = ("parallel", "parallel") + (("arbitrary",) if nk > 1 else ())
    return _call(
        body, name=name, grid=grid, after=after,
        in_specs=[a_spec, b_spec, *extra_specs],
        out_specs=out_specs,
        out_shape=out_shape,
        scratch_shapes=[pltpu.VMEM(acc_shape, F32)] if nk > 1 else [],
        compiler_params=_params(*sem),
    )(a, b, *extras)


def _fit(n, tile):
    if n <= tile:
        return n
    t = tile - tile % LANE
    while n % t:
        t -= LANE
    return t


def _mm_nn(name, a, b, out_dtype, tm, tn, after=()):
    m, k = a.shape
    n = b.shape[1]
    tm, tn = _fit(m, tm), _fit(n, tn)
    return _matmul(name, a, b, grid=(m // tm, n // tn), after=after,
                   a_spec=pl.BlockSpec((tm, k), lambda i, j: (i, 0)),
                   b_spec=pl.BlockSpec((k, tn), lambda i, j: (0, j)),
                   out_shape=jax.ShapeDtypeStruct((m, n), out_dtype),
                   out_specs=pl.BlockSpec((tm, tn), lambda i, j: (i, j)), contract=NN)


def _mm_nt(name, a, b, out_dtype, tm, tn, after=()):
    m, k = a.shape
    n = b.shape[0]
    tm, tn = _fit(m, tm), _fit(n, tn)
    return _matmul(name, a, b, grid=(m // tm, n // tn), after=after,
                   a_spec=pl.BlockSpec((tm, k), lambda i, j: (i, 0)),
                   b_spec=pl.BlockSpec((tn, k), lambda i, j: (j, 0)),
                   out_shape=jax.ShapeDtypeStruct((m, n), out_dtype),
                   out_specs=pl.BlockSpec((tm, tn), lambda i, j: (i, j)), contract=NT)


def _mm_tn(name, a, b, out_dtype, tm, tn):
    s, m = a.shape
    n = b.shape[1]
    tm, tn = _fit(m, tm), _fit(n, tn)
    return _matmul(name, a, b, grid=(m // tm, n // tn),
                   a_spec=pl.BlockSpec((s, tm), lambda i, j: (0, i)),
                   b_spec=pl.BlockSpec((s, tn), lambda i, j: (0, j)),
                   out_shape=jax.ShapeDtypeStruct((m, n), out_dtype),
                   out_specs=pl.BlockSpec((tm, tn), lambda i, j: (i, j)), contract=TN)


ROWS = 256


def _row_spec(rows, width):
    return pl.BlockSpec((rows, width), lambda i: (i, 0))


def _fixed_spec(rows, width):
    return pl.BlockSpec((rows, width), lambda i: (0, 0))


def _column_pieces(rows, start, width):
    piece = math.gcd(start, width)
    assert piece % LANE == 0
    return [pl.BlockSpec((rows, piece), lambda i, b=start // piece + p: (i, b)) for p in range(width // piece)]


def _rms_fwd(name, x, g, cols=None, after=()):
    s = x.shape[0]
    start, w = cols or (0, x.shape[1])
    rows = min(ROWS, s)
    pieces = _column_pieces(rows, start, w) if cols else [_row_spec(rows, w)]
    n = len(pieces)

    def body(*refs):
        g_ref, o_ref = refs[n:]
        xv = refs[0][...] if n == 1 else jnp.concatenate([r[...] for r in refs[:n]], axis=1)
        o_ref[...] = (xv * _rstd(xv) * g_ref[...]).astype(o_ref.dtype)

    return _call(
        body, name=name, grid=(s // rows,), after=after,
        in_specs=[*pieces, _fixed_spec(1, w)],
        out_specs=_row_spec(rows, w),
        out_shape=jax.ShapeDtypeStruct((s, w), BF16),
        compiler_params=_params("parallel"),
    )(*[x] * n, g)


def _rms_bwd_call(name, x, g, dy, out_dtype, cols=None, after=()):
    s = x.shape[0]
    start, w = cols or (0, x.shape[1])
    rows = min(ROWS, s)
    pieces = _column_pieces(rows, start, w) if cols else [_row_spec(rows, w)]
    n = len(pieces)

    def body(*refs):
        g_ref, dy_ref, dx_ref, dg_ref = refs[n:]
        xv = refs[0][...] if n == 1 else jnp.concatenate([r[...] for r in refs[:n]], axis=1)
        dx, dgc = _rms_bwd(xv, g_ref[...], dy_ref[...].astype(F32))
        dx_ref[...] = dx.astype(dx_ref.dtype)
        _accumulate(dg_ref, _sublane_sum(dgc), pl.program_id(0))

    return _call(
        body, name=name, grid=(s // rows,), after=after,
        in_specs=[*pieces, _fixed_spec(1, w), _row_spec(rows, w)],
        out_specs=[_row_spec(rows, w), _fixed_spec(SUBLANE, w)],
        out_shape=[jax.ShapeDtypeStruct((s, w), out_dtype), jax.ShapeDtypeStruct((SUBLANE, w), F32)],
        compiler_params=_params("arbitrary"),
    )(*[x] * n, g, dy)


def _norm_up(name, x, cols, g, w, after=()):
    s = x.shape[0]
    start, width = cols
    n = w.shape[1]
    tm = min(TILE_M, s)
    pieces = _column_pieces(tm, start, width)
    n_p = len(pieces)

    def body(*refs):
        g_ref, w_ref, xn_ref, o_ref = refs[n_p:]
        xv = refs[0][...] if n_p == 1 else jnp.concatenate([r[...] for r in refs[:n_p]], axis=1)
        xn = (xv * _rstd(xv) * g_ref[...]).astype(BF16)
        xn_ref[...] = xn
        o_ref[...] = jnp.dot(xn, w_ref[...], preferred_element_type=F32)

    return _call(
        body, name=name, grid=(s // tm,), after=after,
        in_specs=[*pieces, _fixed_spec(1, width), _fixed_spec(width, n)],
        out_specs=[_row_spec(tm, width), _row_spec(tm, n)],
        out_shape=[jax.ShapeDtypeStruct((s, width), BF16), jax.ShapeDtypeStruct((s, n), F32)],
        compiler_params=_params("parallel"),
    )(*[x] * n_p, g, w)


def _up_norm_bwd(name, dy, w, x, cols, g, after=()):
    s, n = dy.shape
    start, width = cols
    tm = min(TILE_M, s)
    pieces = _column_pieces(tm, start, width)
    n_p = len(pieces)

    def body(dy_ref, w_ref, *refs):
        g_ref, dx_ref, dg_ref = refs[n_p:]
        xv = refs[0][...] if n_p == 1 else jnp.concatenate([r[...] for r in refs[:n_p]], axis=1)
        dxn = lax.dot_general(dy_ref[...], w_ref[...], (NT, ((), ())), preferred_element_type=F32)
        dx, dgc = _rms_bwd(xv, g_ref[...], dxn)
        dx_ref[...] = dx.astype(dx_ref.dtype)
        _accumulate(dg_ref, _sublane_sum(dgc), pl.program_id(0))

    return _call(
        body, name=name, grid=(s // tm,), after=after,
        in_specs=[_row_spec(tm, n), _fixed_spec(width, n), *pieces, _fixed_spec(1, width)],
        out_specs=[_row_spec(tm, width), _fixed_spec(SUBLANE, width)],
        out_shape=[jax.ShapeDtypeStruct((s, width), BF16), jax.ShapeDtypeStruct((SUBLANE, width), F32)],
        compiler_params=_params("arbitrary"),
    )(dy, w, *[x] * n_p, g)


def _mid_fwd(x, y, g_post, g_pre, after=()):
    s, w = x.shape
    rows = min(ROWS, s)

    def body(x_ref, y_ref, gp_ref, gq_ref, x2_ref, h2_ref):
        yv = y_ref[...]
        x2 = x_ref[...] + yv * _rstd(yv) * gp_ref[...]
        x2_ref[...] = x2
        h2_ref[...] = (x2 * _rstd(x2) * gq_ref[...]).astype(h2_ref.dtype)

    return _call(
        body, name="mid_fwd", grid=(s // rows,), after=after,
        in_specs=[_row_spec(rows, w), _row_spec(rows, w), _fixed_spec(1, w), _fixed_spec(1, w)],
        out_specs=[_row_spec(rows, w), _row_spec(rows, w)],
        out_shape=[jax.ShapeDtypeStruct((s, w), F32), jax.ShapeDtypeStruct((s, w), BF16)],
        compiler_params=_params("parallel"),
    )(x, y, g_post, g_pre)


def _head(m, x2, tgt, g):
    s, w = m.shape
    rows = min(ROWS, s)

    def body(m_ref, x2_ref, t_ref, g_ref, dout_ref, dm_ref, dg_ref, loss_ref):
        mv = m_ref[...]
        gv = g_ref[...]
        out = x2_ref[...] + mv * _rstd(mv) * gv
        err = out - t_ref[...]
        dout = err * (1.0 / w)
        dout_ref[...] = dout
        dm, dgc = _rms_bwd(mv, gv, dout)
        dm_ref[...] = dm.astype(dm_ref.dtype)
        sq = err * err
        lanes = sq[:, 0:LANE]
        for j in range(1, w // LANE):
            lanes = lanes + sq[:, j * LANE:(j + 1) * LANE]
        step = pl.program_id(0)
        _accumulate(dg_ref, _sublane_sum(dgc), step)
        _accumulate(loss_ref, _sublane_sum(lanes) * (0.5 / w), step)

    return pl.pallas_call(
        body, name="head", grid=(s // rows,),
        in_specs=[_row_spec(rows, w), _row_spec(rows, w), _row_spec(rows, w), _fixed_spec(1, w)],
        out_specs=[_row_spec(rows, w), _row_spec(rows, w), _fixed_spec(SUBLANE, w), _fixed_spec(SUBLANE, LANE)],
        out_shape=[jax.ShapeDtypeStruct((s, w), F32), jax.ShapeDtypeStruct((s, w), BF16),
                   jax.ShapeDtypeStruct((SUBLANE, w), F32), jax.ShapeDtypeStruct((SUBLANE, LANE), F32)],
        compiler_params=_params("arbitrary"),
    )(m, x2, tgt, g)


def _mid_bwd(x2, y, d_out, d_h2, g_pre, g_post, after=()):
    s, w = x2.shape
    rows = min(ROWS, s)

    def body(x2_ref, y_ref, dout_ref, dh2_ref, gq_ref, gp_ref, dx2_ref, dy_ref, dgq_ref, dgp_ref):
        dx, dgq = _rms_bwd(x2_ref[...], gq_ref[...], dh2_ref[...])
        dx2 = dout_ref[...] + dx
        dx2_ref[...] = dx2
        dy, dgp = _rms_bwd(y_ref[...], gp_ref[...], dx2)
        dy_ref[...] = dy.astype(dy_ref.dtype)
        step = pl.program_id(0)
        _accumulate(dgq_ref, _sublane_sum(dgq), step)
        _accumulate(dgp_ref, _sublane_sum(dgp), step)

    return _call(
        body, name="mid_bwd", grid=(s // rows,), after=after,
        in_specs=[_row_spec(rows, w)] * 4 + [_fixed_spec(1, w)] * 2,
        out_specs=[_row_spec(rows, w), _row_spec(rows, w), _fixed_spec(SUBLANE, w), _fixed_spec(SUBLANE, w)],
        out_shape=[jax.ShapeDtypeStruct((s, w), F32), jax.ShapeDtypeStruct((s, w), BF16),
                   jax.ShapeDtypeStruct((SUBLANE, w), F32), jax.ShapeDtypeStruct((SUBLANE, w), F32)],
        compiler_params=_params("arbitrary"),
    )(x2, y, d_out, d_h2, g_pre, g_post)


def _first_bwd(x, g, d_h1, d_x2, after=()):
    s, w = x.shape
    rows = min(ROWS, s)

    def body(x_ref, g_ref, dh_ref, dx2_ref, dx_ref, dg_ref):
        dx, dgc = _rms_bwd(x_ref[...], g_ref[...], dh_ref[...])
        dx_ref[...] = dx2_ref[...] + dx
        _accumulate(dg_ref, _sublane_sum(dgc), pl.program_id(0))

    return _call(
        body, name="first_bwd", grid=(s // rows,), after=after,
        in_specs=[_row_spec(rows, w), _fixed_spec(1, w), _row_spec(rows, w), _row_spec(rows, w)],
        out_specs=[_row_spec(rows, w), _fixed_spec(SUBLANE, w)],
        out_shape=[jax.ShapeDtypeStruct((s, w), F32), jax.ShapeDtypeStruct((SUBLANE, w), F32)],
        compiler_params=_params("arbitrary"),
    )(x, g, d_h1, d_x2)


def _shift_down(v, k):
    t = lax.broadcasted_iota(jnp.int32, v.shape, 0)
    return jnp.where(t >= k, pltpu.roll(v, k, 0), 0.0)


def _shift_up(v, k):
    n = v.shape[0]
    t = lax.broadcasted_iota(jnp.int32, v.shape, 0)
    return jnp.where(t < n - k, pltpu.roll(v, n - k, 0), 0.0)


def _conv_core(u, b, c, w):
    z = c * u
    conv = w[0:1, :] * _shift_down(z, 2) + w[1:2, :] * _shift_down(z, 1) + w[2:3, :] * z
    return z, conv, b * conv


def _conv_fwd(proj, conv_w, g, n_groups, out_width, after=()):
    s = proj.shape[0]

    def body(u_ref, b_ref, c_ref, w_ref, g_ref, o_ref):
        _, _, yr = _conv_core(u_ref[...], b_ref[...], c_ref[...], w_ref[...])
        o_ref[...] = (yr * _rstd(yr) * g_ref[...]).astype(o_ref.dtype)

    col = lambda k: pl.BlockSpec((s, HEAD), lambda i: (0, k * n_groups + i))
    return _call(
        body, name="conv_fwd", grid=(n_groups,), after=after,
        in_specs=[col(0), col(1), col(2), pl.BlockSpec((3, HEAD), lambda i: (0, i)), pl.BlockSpec((1, HEAD), lambda i: (0, i))],
        out_specs=pl.BlockSpec((s, HEAD), lambda i: (0, i)),
        out_shape=jax.ShapeDtypeStruct((s, out_width), BF16),
        compiler_params=_params("parallel"),
    )(proj, proj, proj, conv_w, g)


def _conv_bwd(proj, d_mix, conv_w, g, n_groups):
    s = proj.shape[0]
    width = n_groups * HEAD

    def body(u_ref, b_ref, c_ref, dy_ref, w_ref, g_ref, du_ref, db_ref, dc_ref, dg_ref, dw_ref):
        u, b, c, w = u_ref[...], b_ref[...], c_ref[...], w_ref[...]
        z, conv, yr = _conv_core(u, b, c, w)
        dyr, dgc = _rms_bwd(yr, g_ref[...], dy_ref[...])
        dconv = dyr * b
        db_ref[...] = (dyr * conv).astype(db_ref.dtype)
        dz = w[2:3, :] * dconv + w[1:2, :] * _shift_up(dconv, 1) + w[0:1, :] * _shift_up(dconv, 2)
        dc_ref[...] = (dz * u).astype(dc_ref.dtype)
        du_ref[...] = (dz * c).astype(du_ref.dtype)
        dg_ref[...] = _sublane_sum(dgc)
        dw_ref[0] = _sublane_sum(dconv * _shift_down(z, 2))
        dw_ref[1] = _sublane_sum(dconv * _shift_down(z, 1))
        dw_ref[2] = _sublane_sum(dconv * z)

    col = lambda k: pl.BlockSpec((s, HEAD), lambda i: (0, k * n_groups + i))
    grp = pl.BlockSpec((s, HEAD), lambda i: (0, i))
    return pl.pallas_call(
        body, name="conv_bwd", grid=(n_groups,),
        in_specs=[col(0), col(1), col(2), grp, pl.BlockSpec((3, HEAD), lambda i: (0, i)), pl.BlockSpec((1, HEAD), lambda i: (0, i))],
        out_specs=[grp, grp, grp, pl.BlockSpec((SUBLANE, HEAD), lambda i: (0, i)),
                   pl.BlockSpec((3, SUBLANE, HEAD), lambda i: (0, 0, i))],
        out_shape=[jax.ShapeDtypeStruct((s, width), BF16)] * 3
        + [jax.ShapeDtypeStruct((SUBLANE, width), F32), jax.ShapeDtypeStruct((3, SUBLANE, width), F32)],
        compiler_params=_params("parallel"),
    )(proj, proj, proj, d_mix, conv_w, g)


def _rope_tables(s, n_heads):
    pos = jnp.arange(s, dtype=F32)
    inv_freq = jnp.power(ROPE_THETA, -jnp.arange(0, ROPE, 2, dtype=F32) / ROPE)
    ang = pos[:, None] * inv_freq[None, :]
    cos, sin = jnp.cos(ang), jnp.sin(ang)
    cs = jnp.concatenate([cos, cos], axis=1)
    sn = jnp.concatenate([-sin, sin], axis=1)
    pad = jnp.zeros((s, LANE - ROPE), F32)
    return (jnp.tile(cs, (1, n_heads)), jnp.tile(sn, (1, n_heads)),
            jnp.concatenate([cs, pad], axis=1), jnp.concatenate([sn, pad], axis=1))


def _swap_halves(v):
    w = v.shape[1]
    lane = lax.broadcasted_iota(jnp.int32, v.shape, 1)
    first = (lane % ROPE) < (ROPE // 2)
    return jnp.where(first, pltpu.roll(v, w - ROPE // 2, 1), pltpu.roll(v, ROPE // 2, 1))


def _pack_heads(q, kv, proj, kr_col, tables, n_heads, after=()):
    s = q.shape[0]
    rows = min(ROWS, s)
    cq, sq, ck, sk = tables
    wq = n_heads * ROPE

    def body(q_ref, kv_ref, kr_ref, cq_ref, sq_ref, ck_ref, sk_ref, qo_ref, ko_ref, vo_ref):
        qr = q_ref[:, n_heads * HEAD:]
        qr = qr * cq_ref[...] + _swap_halves(qr) * sq_ref[...]
        krv = kr_ref[...]
        krv = krv * ck_ref[...] + _swap_halves(krv) * sk_ref[...]
        for h in range(n_heads):
            qo_ref[h] = jnp.concatenate([q_ref[:, h * HEAD:(h + 1) * HEAD], qr[:, h * ROPE:(h + 1) * ROPE]], axis=1).astype(BF16)
            ko_ref[h] = jnp.concatenate([kv_ref[:, 2 * h * HEAD:(2 * h + 1) * HEAD], krv[:, :ROPE]], axis=1).astype(BF16)
            vo_ref[h] = kv_ref[:, (2 * h + 1) * HEAD:(2 * h + 2) * HEAD].astype(BF16)

    hs = lambda w: pl.BlockSpec((n_heads, rows, w), lambda i: (0, i, 0))
    return _call(
        body, name="pack_heads", grid=(s // rows,), after=after,
        in_specs=[_row_spec(rows, q.shape[1]), _row_spec(rows, kv.shape[1]), pl.BlockSpec((rows, LANE), lambda i: (i, kr_col // LANE)),
                  _row_spec(rows, wq), _row_spec(rows, wq), _row_spec(rows, LANE), _row_spec(rows, LANE)],
        out_specs=[hs(QK), hs(QK), hs(HEAD)],
        out_shape=[jax.ShapeDtypeStruct((n_heads, s, QK), BF16), jax.ShapeDtypeStruct((n_heads, s, QK), BF16),
                   jax.ShapeDtypeStruct((n_heads, s, HEAD), BF16)],
        compiler_params=_params("parallel"),
    )(q, kv, proj, cq, sq, ck, sk)


def _unpack_heads(dq, dk, dv, tables, n_heads):
    s = dq.shape[1]
    rows = min(ROWS, s)
    cq, sq, ck, sk = tables
    wq = n_heads * ROPE

    def body(dq_ref, dk_ref, dv_ref, cq_ref, sq_ref, ck_ref, sk_ref, qo_ref, kvo_ref, kro_ref):
        dqr = jnp.concatenate([dq_ref[h][:, HEAD:] for h in range(n_heads)], axis=1)
        dqr = dqr * cq_ref[...] - _swap_halves(dqr) * sq_ref[...]
        dkr = dk_ref[0][:, HEAD:]
        for h in range(1, n_heads):
            dkr = dkr + dk_ref[h][:, HEAD:]
        dkr = jnp.concatenate([dkr, jnp.zeros((rows, LANE - ROPE), F32)], axis=1)
        dkr = dkr * ck_ref[...] - _swap_halves(dkr) * sk_ref[...]
        kro_ref[...] = dkr.astype(kro_ref.dtype)
        qo_ref[:, n_heads * HEAD:] = dqr.astype(qo_ref.dtype)
        for h in range(n_heads):
            qo_ref[:, h * HEAD:(h + 1) * HEAD] = dq_ref[h][:, :HEAD].astype(qo_ref.dtype)
            kvo_ref[:, 2 * h * HEAD:(2 * h + 1) * HEAD] = dk_ref[h][:, :HEAD].astype(kvo_ref.dtype)
            kvo_ref[:, (2 * h + 1) * HEAD:(2 * h + 2) * HEAD] = dv_ref[h].astype(kvo_ref.dtype)

    hs = lambda w: pl.BlockSpec((n_heads, rows, w), lambda i: (0, i, 0))
    return pl.pallas_call(
        body, name="unpack_heads", grid=(s // rows,),
        in_specs=[hs(QK), hs(QK), hs(HEAD), _row_spec(rows, wq), _row_spec(rows, wq), _row_spec(rows, LANE), _row_spec(rows, LANE)],
        out_specs=[_row_spec(rows, n_heads * QK), _row_spec(rows, 2 * n_heads * HEAD), _row_spec(rows, LANE)],
        out_shape=[jax.ShapeDtypeStruct((s, n_heads * QK), BF16), jax.ShapeDtypeStruct((s, 2 * n_heads * HEAD), BF16),
                   jax.ShapeDtypeStruct((s, LANE), BF16)],
        compiler_params=_params("parallel"),
    )(dq, dk, dv, cq, sq, ck, sk)


TQ = 256


LOG2_E = 1.4426950408889634


def _softmax_parts(q, k):
    tq, n_keys = q.shape[0], k.shape[0]
    sc = lax.dot_general(q, k, (NT, ((), ())), preferred_element_type=F32) * (QK ** -0.5 * LOG2_E)
    row = lax.broadcasted_iota(jnp.int32, (tq, tq), 0)
    col = lax.broadcasted_iota(jnp.int32, (tq, tq), 1)
    own = jnp.where(col // CHUNK <= row // CHUNK, sc[:, n_keys - tq:], NEG_INF)
    sc = own if n_keys == tq else jnp.concatenate([sc[:, :n_keys - tq], own], axis=1)
    e = jnp.exp2(sc - jnp.max(sc, axis=-1, keepdims=True))
    return e, 1.0 / jnp.sum(e, axis=-1, keepdims=True)


def _attn_fwd(q, k, v, g, mix, col0):
    n_heads, s, _ = q.shape
    tq = min(TQ, s)
    assert tq % CHUNK == 0 and s % tq == 0

    def body(q_ref, k_ref, v_ref, g_ref, mix_ref, o_ref, y_ref):
        for c in range(s // tq):
            rows, n_keys = pl.ds(c * tq, tq), (c + 1) * tq
            e, inv = _softmax_parts(q_ref[rows, :], k_ref[0:n_keys, :])
            o = jnp.dot(e.astype(BF16), v_ref[0:n_keys, :], preferred_element_type=F32) * inv
            o_ref[rows, :] = o
            y_ref[rows, :] = (o * _rstd(o) * g_ref[...]).astype(y_ref.dtype)

    head = lambda w: pl.BlockSpec((None, s, w), lambda h: (h, 0, 0))
    return pl.pallas_call(
        body, name="attn_fwd", grid=(n_heads,),
        in_specs=[head(QK), head(QK), head(HEAD), pl.BlockSpec((1, HEAD), lambda h: (0, h)), ANY],
        out_specs=[head(HEAD), pl.BlockSpec((s, HEAD), lambda h: (0, col0 // HEAD + h))],
        out_shape=[jax.ShapeDtypeStruct((n_heads, s, HEAD), F32), jax.ShapeDtypeStruct(mix.shape, mix.dtype)],
        input_output_aliases={4: 1},
        compiler_params=_params("parallel"),
    )(q, k, v, g, mix)


def _attn_bwd(q, k, v, o, d_mix, g, col0, after=()):
    n_heads, s, _ = q.shape
    tq = min(TQ, s)

    def body(q_ref, k_ref, v_ref, o_ref, dy_ref, g_ref, dq_ref, dk_ref, dv_ref, dg_ref):
        dg = None
        for c in reversed(range(s // tq)):
            rows, n_keys = pl.ds(c * tq, tq), (c + 1) * tq
            qv, kv_, vv = q_ref[rows, :], k_ref[0:n_keys, :], v_ref[0:n_keys, :]
            do, dgc = _rms_bwd(o_ref[rows, :], g_ref[...], dy_ref[rows, :])
            do = do.astype(BF16)
            dg = _sublane_sum(dgc) if dg is None else dg + _sublane_sum(dgc)
            e, inv = _softmax_parts(qv, kv_)
            p = e * inv
            dp = lax.dot_general(do, vv, (NT, ((), ())), preferred_element_type=F32)
            ds = (p * (dp - jnp.sum(p * dp, axis=-1, keepdims=True)) * (QK ** -0.5)).astype(BF16)
            dq_ref[rows, :] = jnp.dot(ds, kv_, preferred_element_type=F32)
            dk = lax.dot_general(ds, qv, (TN, ((), ())), preferred_element_type=F32)
            dv = lax.dot_general(p.astype(BF16), do, (TN, ((), ())), preferred_element_type=F32)
            if n_keys == s:
                dk_ref[...] = dk
                dv_ref[...] = dv
            else:
                dk_ref[0:n_keys, :] += dk
                dv_ref[0:n_keys, :] += dv
        dg_ref[...] = dg

    c0 = col0 // HEAD
    head = lambda w: pl.BlockSpec((None, s, w), lambda h, *_: (h, 0, 0))
    in_specs = [head(QK), head(QK), head(HEAD), head(HEAD), pl.BlockSpec((s, HEAD), lambda h, *_: (0, c0 + h)),
                pl.BlockSpec((1, HEAD), lambda h, *_: (0, h))]
    out_specs = [head(QK), head(QK), head(HEAD), pl.BlockSpec((SUBLANE, HEAD), lambda h, *_: (0, h))]
    out_shape = [jax.ShapeDtypeStruct((n_heads, s, QK), F32), jax.ShapeDtypeStruct((n_heads, s, QK), F32),
                 jax.ShapeDtypeStruct((n_heads, s, HEAD), F32), jax.ShapeDtypeStruct((SUBLANE, n_heads * HEAD), F32)]
    return _call(body, name="attn_bwd", grid=(n_heads,), after=after, in_specs=in_specs, out_specs=out_specs,
                 out_shape=out_shape, compiler_params=_params("parallel"))(q, k, v, o, d_mix, g)


TILE_M = 1024
TILE_N = 1024


def _up_fwd(h2, w_up):
    s, d = h2.shape
    nb, _, fb = w_up.shape
    tm = min(TILE_M,s)

    def epilogue(acc):
        r = jnp.maximum(acc, 0.0)
        return r * r, r

    blk = pl.BlockSpec((tm, fb), lambda i, j: (i, j))
    return _matmul("up_fwd", h2, w_up, grid=(s // tm, nb),
                   a_spec=pl.BlockSpec((tm, d), lambda i, j: (i, 0)),
                   b_spec=pl.BlockSpec((None, d, fb), lambda i, j: (j, 0, 0)),
                   out_shape=[jax.ShapeDtypeStruct((s, nb * fb), BF16)] * 2, out_specs=[blk, blk],
                   contract=NN, epilogue=epilogue)


def _down_fwd(a, w_down):
    s, f = a.shape
    d = w_down.shape[1]
    tm, tn, tk = min(TILE_M,s), min(TILE_N,d), 2048
    nk = f // tk
    return _matmul("down_fwd", a, w_down, grid=(s // tm, d // tn, nk),
                   a_spec=pl.BlockSpec((tm, tk), lambda i, j, k: (i, k)),
                   b_spec=pl.BlockSpec((tk, tn), lambda i, j, k: (k, j)),
                   out_shape=jax.ShapeDtypeStruct((s, d), F32),
                   out_specs=pl.BlockSpec((tm, tn), lambda i, j, k: (i, j)),
                   contract=NN, nk=nk, acc_shape=(tm, tn))


def _down_bwd_act(d_m, w_down, r, after=()):
    s, d = d_m.shape
    f = w_down.shape[0]
    tm, tn = min(TILE_M,s), min(TILE_N,f)
    blk = pl.BlockSpec((tm, tn), lambda i, j: (i, j))
    return _matmul("down_bwd_act", d_m, w_down, grid=(s // tm, f // tn), after=after,
                   a_spec=pl.BlockSpec((tm, d), lambda i, j: (i, 0)),
                   b_spec=pl.BlockSpec((tn, d), lambda i, j: (j, 0)),
                   out_shape=jax.ShapeDtypeStruct((s, f), BF16), out_specs=blk, contract=NT,
                   extras=(r,), extra_specs=(blk,),
                   epilogue=lambda acc, rv: (acc * (2.0 * rv.astype(F32)),))


def _up_bwd_act(d_up, w_up, after=()):
    s, _ = d_up.shape
    nb, d, fb = w_up.shape
    tm, tn = min(TILE_M, s), min(TILE_N,d)
    pair = 2
    n_after = len(after)

    def body(a_ref, w_ref, *rest):
        o_ref, acc_ref = rest[n_after:]
        k = pl.program_id(2)
        p = None
        for t in range(pair):
            term = lax.dot_general(a_ref[:, t * fb:(t + 1) * fb], w_ref[t], (NT, ((), ())), preferred_element_type=F32)
            p = term if p is None else p + term
        _accumulate(acc_ref, p, k)

        @pl.when(k == nb // pair - 1)
        def _():
            o_ref[...] = acc_ref[...]

    return pl.pallas_call(
        body, name="up_bwd_act", grid=(s // tm, d // tn, nb // pair),
        in_specs=[pl.BlockSpec((tm, pair * fb), lambda i, j, k: (i, k)),
                  pl.BlockSpec((pair, tn, fb), lambda i, j, k: (k, j, 0))] + [ANY] * n_after,
        out_specs=pl.BlockSpec((tm, tn), lambda i, j, k: (i, j)),
        out_shape=jax.ShapeDtypeStruct((s, d), F32),
        scratch_shapes=[pltpu.VMEM((tm, tn), F32)],
        compiler_params=_params("parallel", "parallel", "arbitrary"),
    )(d_up, w_up, *after)


def _half_grad(name, a, b, core, home, received, after, *, grid, a_block, a_map, b_block, b_map, o_block, o_map, out_shape):
    n_after = len(after)
    pick = (lambda ref: ref[0]) if home else (lambda ref: 1 - ref[0])

    def body(core_ref, a_ref, b_ref, *rest):
        acc = lax.dot_general(a_ref[...], b_ref[...], (TN, ((), ())), preferred_element_type=F32)
        if received is not None:
            acc = acc + rest[0][...].astype(F32)
        rest[-1][...] = acc.astype(rest[-1].dtype)

    wrap = lambda fn: (lambda i, j, core_ref: fn(i, j, pick(core_ref)))
    o_spec = pl.BlockSpec(o_block, wrap(o_map))
    extra = [] if received is None else [o_spec]
    operands = [] if received is None else [received]
    return pl.pallas_call(
        body, name=name,
        grid_spec=pltpu.PrefetchScalarGridSpec(
            num_scalar_prefetch=1, grid=grid,
            in_specs=[pl.BlockSpec(a_block, wrap(a_map)), pl.BlockSpec(b_block, wrap(b_map))] + extra + [ANY] * n_after,
            out_specs=o_spec),
        out_shape=out_shape,
        compiler_params=_params("parallel", "parallel"),
    )(core, a, b, *operands, *after)


def _down_half_grad(name, a, d_m, core, home, received=None, after=()):
    s, f = a.shape
    d = d_m.shape[1]
    r = f // N_DEV
    tn = min(TILE_N, d)
    return _half_grad(name, a, d_m, core, home, received, after, grid=(N_CHIP, d // tn),
                      a_block=(s, r), a_map=lambda k, j, p: (0, 2 * k + p),
                      b_block=(s, tn), b_map=lambda k, j, p: (0, j),
                      o_block=(None, r, tn), o_map=lambda k, j, p: (k, 0, j),
                      out_shape=jax.ShapeDtypeStruct((N_CHIP, r, d), BF16))


def _up_half_grad(name, h2, d_up, core, home, received=None, after=()):
    s, d = h2.shape
    fb = d_up.shape[1] // N_DEV
    tm = min(TILE_M, d)
    return _half_grad(name, h2, d_up, core, home, received, after, grid=(d // tm, N_CHIP),
                      a_block=(s, tm), a_map=lambda i, k, p: (0, i),
                      b_block=(s, fb), b_map=lambda i, k, p: (0, 2 * k + p),
                      o_block=(None, tm, fb), o_map=lambda i, k, p: (k, i, 0),
                      out_shape=jax.ShapeDtypeStruct((N_CHIP, d, fb), BF16))


MXU_WIDTH = 256


def _in_pad(in_width):
    return -(-in_width // MXU_WIDTH) * MXU_WIDTH


def _join_col_shards(name, blocks, own, device, pieces=None):
    n, r, w = blocks.shape
    rows = min(ROWS, r)
    pieces = pieces or [(j, 0, w) for j in range(n)]
    used = sum(b - a for _, a, b in pieces)
    width = _in_pad(used)

    def body(dev_ref, x_ref, own_ref, o_ref):
        block = lambda j: jnp.where(dev_ref[0] == j, own_ref[...], x_ref[j])
        cols = [block(j)[:, a:b] for j, a, b in pieces]
        tail = [jnp.zeros((rows, width - used), o_ref.dtype)] if width > used else []
        o_ref[...] = jnp.concatenate(cols + tail, axis=1)

    return pl.pallas_call(
        body, name=name,
        grid_spec=pltpu.PrefetchScalarGridSpec(
            num_scalar_prefetch=1, grid=(r // rows,),
            in_specs=[pl.BlockSpec((n, rows, w), lambda i, dev: (0, i, 0)), pl.BlockSpec((rows, w), lambda i, dev: (i, 0))],
            out_specs=pl.BlockSpec((rows, width), lambda i, dev: (i, 0))),
        out_shape=jax.ShapeDtypeStruct((r, width), blocks.dtype),
        compiler_params=_params("parallel"),
    )(device, blocks, own)


def _permute_q_cols(w_uq, n_heads):
    r = w_uq.shape[0]
    w3 = w_uq.reshape(r, n_heads, QK)
    return jnp.concatenate([w3[:, :, :HEAD].reshape(r, n_heads * HEAD), w3[:, :, HEAD:].reshape(r, n_heads * ROPE)], axis=1)


def _unpermute_q_rows(wt, n_heads):
    r = wt.shape[1]
    nope = wt[:n_heads * HEAD].reshape(n_heads, HEAD, r)
    rope = wt[n_heads * HEAD:].reshape(n_heads, ROPE, r)
    return jnp.concatenate([nope, rope], axis=1).reshape(n_heads * QK, r)


def _local_step(x, tgt, gains, weights, grads, first_after=()):
    pre_mix_g, q_norm_g, kv_norm_g, conv_out_g, attn_out_g, post_mix_g, pre_mlp_g, post_mlp_g = gains
    s, d = x.shape
    conv_width = conv_out_g.shape[1]
    n_groups = conv_width // HEAD
    r_q, r_kv = q_norm_g.shape[1], kv_norm_g.shape[1]
    n_heads = attn_out_g.shape[1] // HEAD
    c_q0 = 3 * conv_width
    c_kv0 = c_q0 + r_q
    c_kr0 = c_kv0 + r_kv
    in_pad = _in_pad(c_kr0 + ROPE)
    tn_in = _fit(in_pad, 6 * MXU_WIDTH)
    tables = _rope_tables(s, n_heads)

    h1 = _rms_fwd("pre_mix_norm", x, pre_mix_g, after=first_after)
    weights.forward(0, (h1,))
    weights.relay(0, tables)
    w_in_p, conv_w = weights.ready(0, ())
    proj = _mm_nn("in_proj", h1, w_in_p, F32, TILE_M, tn_in)
    y_conv = _conv_fwd(proj, conv_w, conv_out_g, n_groups, conv_width + n_heads * HEAD, after=weights.forward(1, (proj,)))
    w_uq_p, w_ukv, w_o = weights.ready(1, (y_conv,))
    qn, q = _norm_up("q_up", proj, (c_q0, r_q), q_norm_g, w_uq_p)
    kvn, kv = _norm_up("kv_up", proj, (c_kv0, r_kv), kv_norm_g, w_ukv)
    qh, kh, vh = _pack_heads(q, kv, proj, c_kr0, tables, n_heads, after=weights.forward(2, (q, kv)))
    o, mix = _attn_fwd(qh, kh, vh, attn_out_g, y_conv, conv_width)
    y = _mm_nn("out_proj", mix, w_o, F32, TILE_M, TILE_N)
    x2, h2 = _mid_fwd(x, y, post_mix_g, pre_mlp_g, after=weights.forward(3, (y,)))
    weights.relay(2, (h2,))
    (w_up,) = weights.ready(2, ())
    a, r = _up_fwd(h2, w_up)
    weights.relay(3, (a,))
    (w_down,) = weights.ready(3, ())
    m = _down_fwd(a, w_down)

    d_out, d_m, dg_post_mlp, loss_part = _head(m, x2, tgt, post_mlp_g)
    core = grads.core
    away = _down_half_grad("down_bwd_w_away", a, d_m, core, home=False)
    d_up = _down_bwd_act(d_m, w_down, r, after=grads.send_away(0, away))
    sums = _down_half_grad("down_bwd_w_home", a, d_m, core, home=True, received=grads.received(0, (d_up,)))
    away = _up_half_grad("up_bwd_w_away", h2, d_up, core, home=False, after=grads.send_sums(0, (sums,)))
    d_h2 = _up_bwd_act(d_up, w_up, after=grads.send_away(1, away))
    sums = _up_half_grad("up_bwd_w_home", h2, d_up, core, home=True, received=grads.received(1, (d_h2,)))
    d_x2, d_y, dg_pre_mlp, dg_post_mix = _mid_bwd(x2, y, d_out, d_h2, pre_mlp_g, post_mix_g, after=grads.send_sums(1, (sums,)))
    d_mix = _mm_nt("out_proj_bwd_act", d_y, w_o, F32, TILE_M, TILE_N)
    gw_o = _mm_tn("out_proj_bwd_w", mix, d_y, BF16, TILE_M, TILE_N)
    dqh, dkh, dvh, dg_attn = _attn_bwd(qh, kh, vh, o, d_mix, attn_out_g, conv_width)
    d_q, d_kv, d_kr = _unpack_heads(dqh, dkh, dvh, tables, n_heads)
    gw_uq_t = _mm_tn("q_up_bwd_w", d_q, qn, F32, TILE_M, TILE_N)
    gw_ukv = _mm_tn("kv_up_bwd_w", kvn, d_kv, BF16, TILE_M, TILE_N)
    d_cq, dg_q = _up_norm_bwd("q_up_bwd_act", d_q, w_uq_p, proj, (c_q0, r_q), q_norm_g, after=grads.full(2, (gw_o, gw_uq_t, gw_ukv)))
    d_ckv, dg_kv = _up_norm_bwd("kv_up_bwd_act", d_kv, w_ukv, proj, (c_kv0, r_kv), kv_norm_g)
    d_u, d_b, d_c, dg_conv, dw_conv = _conv_bwd(proj, d_mix, conv_w, conv_out_g, n_groups)
    d_proj = jnp.concatenate([d_u, d_b, d_c, d_cq, d_ckv, d_kr, jnp.zeros((s, in_pad - c_kr0 - LANE), BF16)], axis=1)
    gw_in_t = _mm_tn("in_proj_bwd_w", d_proj, h1, F32, tn_in, TILE_N)
    updated = grads.update_now(0, grads.send_away(3, gw_in_t))
    d_h1 = _mm_nt("in_proj_bwd_act", d_proj, w_in_p, F32, TILE_M, TILE_N, after=grads.full(3, (gw_in_t,), received=updated))
    grad_x, dg_pre_mix = _first_bwd(x, pre_mix_g, d_h1, d_x2)

    small = [dg_pre_mix, dg_q, dg_kv, dg_conv, dg_attn, dg_post_mix, dg_pre_mlp, dg_post_mlp,
             dw_conv[0], dw_conv[1], dw_conv[2], loss_part]
    return grad_x, jnp.concatenate(small, axis=1)


HBM = pl.BlockSpec(memory_space=pltpu.HBM)
SEM = pl.BlockSpec(memory_space=pltpu.SEMAPHORE)
IN_VMEM = pl.BlockSpec(memory_space=pltpu.VMEM)
SPLIT = pltpu.CompilerParams(has_side_effects=pltpu.SideEffectType.DATAFLOW_SIDE_EFFECTING)


def _in_hbm(a):
    return pltpu.with_memory_space_constraint(a, pltpu.HBM)


def _hbm_like(a):
    return pltpu.HBM(a.shape, a.dtype)


def _place():
    x, y, c = lax.axis_index("x"), lax.axis_index("y"), lax.axis_index("c")
    other_chips = [(1 - x, y), (x, 1 - y), (1 - x, 1 - y)]
    return x, y, c, other_chips


def _block(px, py, pc):
    return 4 * px + 2 * py + pc


def _await(block, sem):
    pltpu.make_async_copy(block, block, sem).wait()


def _relay_route(x, y, c):
    came_from = ((1 - x) * (1 - c) + x * c, y * (1 - c) + (1 - y) * c)
    goes_to = (x * (1 - c) + (1 - x) * c, (1 - y) * (1 - c) + y * c)
    return came_from, goes_to


def _gather_start(name, shards, groups, relayed=(), after=()):
    n, ng = len(shards), len(groups)
    lands = [lax.empty((N_DEV, *a.shape), a.dtype) for a in shards]

    def body(*refs):
        src, land = refs[:n], refs[n:2 * n]
        sems, token = refs[2 * n + len(after):2 * n + len(after) + 2 * ng], refs[-1]
        x, y, c, chips = _place()
        targets = [(x, y, 1 - c)] + [(*chip, c) for chip in chips]
        for gi, group in enumerate(groups):
            for i, w in enumerate(group):
                for k, to in enumerate(targets[:3] if gi in relayed else targets):
                    pltpu.make_async_remote_copy(
                        src_ref=src[w], dst_ref=land[w].at[_block(x, y, c)],
                        send_sem=sems[2 * gi].at[4 * i + k], recv_sem=sems[2 * gi + 1].at[4 * i + k],
                        device_id=to, device_id_type=MESH).start()
        token[...] = jnp.zeros_like(token)

    sem_shapes = [pltpu.SemaphoreType.DMA((4 * len(g),)) for g in groups for _ in range(2)]
    out = pl.pallas_call(
        body, name=name,
        in_specs=[HBM] * (2 * n) + [ANY] * len(after),
        out_specs=[SEM] * (2 * ng) + [HBM] * (2 * n) + [IN_VMEM],
        out_shape=sem_shapes + [_hbm_like(a) for a in shards] + [_hbm_like(a) for a in lands]
        + [jax.ShapeDtypeStruct((SUBLANE, LANE), F32)],
        input_output_aliases={i: 2 * ng + i for i in range(2 * n)},
        compiler_params=SPLIT,
    )(*[_in_hbm(a) for a in shards], *[_in_hbm(a) for a in lands], *after)
    sems = [(out[2 * gi], out[2 * gi + 1]) for gi in range(ng)]
    return sems, out[2 * ng:2 * ng + n], out[2 * ng + n:2 * ng + 2 * n], out[-1]


def _gather_forward(name, shards, lands, send1, recv1, after, relayed=False):
    n = len(lands)

    def body(*refs):
        src, land = refs[:n], refs[n:2 * n]
        s1, r1 = refs[2 * n], refs[2 * n + 1]
        s2, r2 = refs[2 * n + 2 + len(after)], refs[2 * n + 3 + len(after)]
        x, y, c, chips = _place()
        me, sibling = (x, y, c), (x, y, 1 - c)
        for j, chip in enumerate(chips[:2] if relayed else chips):
            for i in range(n):
                blk = land[i].at[_block(*chip, c)]
                pltpu.make_async_remote_copy(src_ref=blk, dst_ref=blk, send_sem=s1.at[4 * i + 1 + j], recv_sem=r1.at[4 * i + 1 + j],
                                             device_id=me, device_id_type=MESH).wait_recv()
                pltpu.make_async_remote_copy(src_ref=blk, dst_ref=blk, send_sem=s2.at[3 * i + j], recv_sem=r2.at[3 * i + j],
                                             device_id=sibling, device_id_type=MESH).start()
        if relayed:
            came_from, goes_to = _relay_route(x, y, c)
            for i in range(n):
                blk = land[i].at[_block(*came_from, c)]
                pltpu.make_async_remote_copy(src_ref=blk, dst_ref=blk, send_sem=s2.at[3 * i + 2], recv_sem=r2.at[3 * i + 2],
                                             device_id=(*goes_to, c), device_id_type=MESH).start()
        for i in range(n):
            blk = land[i].at[_block(x, y, 1 - c)]
            pltpu.make_async_remote_copy(src_ref=blk, dst_ref=blk, send_sem=s1.at[4 * i], recv_sem=r1.at[4 * i],
                                         device_id=me, device_id_type=MESH).wait_recv()
            for k in range(3 if relayed else 4):
                pltpu.make_async_remote_copy(src_ref=src[i], dst_ref=land[i].at[_block(x, y, c)], send_sem=s1.at[4 * i + k],
                                             recv_sem=r1.at[4 * i + k], device_id=sibling, device_id_type=MESH).wait_send()

    sem = pltpu.SemaphoreType.DMA((3 * n,))
    out = pl.pallas_call(
        body, name=name,
        in_specs=[HBM] * (2 * n) + [SEM, SEM] + [ANY] * len(after),
        out_specs=[SEM, SEM] + [HBM] * n,
        out_shape=[sem, sem] + [_hbm_like(a) for a in lands],
        input_output_aliases={n + i: 2 + i for i in range(n)},
        compiler_params=SPLIT,
    )(*shards, *lands, send1, recv1, *after)
    return (out[0], out[1]), out[2:]


def _gather_relay_forward(name, lands, send2, recv2, after):
    n = len(lands)

    def body(*refs):
        land, s2, r2 = refs[:n], refs[n], refs[n + 1]
        s3, r3 = refs[n + 2 + len(after)], refs[n + 3 + len(after)]
        x, y, c, _ = _place()
        me, sibling = (x, y, c), (x, y, 1 - c)
        came_from, _ = _relay_route(x, y, c)
        for i in range(n):
            blk = land[i].at[_block(1 - x, 1 - y, c)]
            pltpu.make_async_remote_copy(src_ref=blk, dst_ref=blk, send_sem=s2.at[3 * i + 2], recv_sem=r2.at[3 * i + 2],
                                         device_id=me, device_id_type=MESH).wait_recv()
            pltpu.make_async_remote_copy(src_ref=blk, dst_ref=blk, send_sem=s3.at[i], recv_sem=r3.at[i],
                                         device_id=sibling, device_id_type=MESH).start()
            sent = land[i].at[_block(*came_from, c)]
            pltpu.make_async_remote_copy(src_ref=sent, dst_ref=sent, send_sem=s2.at[3 * i + 2], recv_sem=r2.at[3 * i + 2],
                                         device_id=me, device_id_type=MESH).wait_send()

    sem = pltpu.SemaphoreType.DMA((n,))
    out = pl.pallas_call(
        body, name=name,
        in_specs=[HBM] * n + [SEM, SEM] + [ANY] * len(after),
        out_specs=[SEM, SEM] + [HBM] * n,
        out_shape=[sem, sem] + [_hbm_like(a) for a in lands],
        input_output_aliases={i: 2 + i for i in range(n)},
        compiler_params=SPLIT,
    )(*lands, send2, recv2, *after)
    return (out[0], out[1]), out[2:]


def _gather_wait(name, lands, send2, recv2, after, relay_sems=None):
    n = len(lands)
    n_sems = 2 if relay_sems is None else 4

    def body(*refs):
        land, s2, r2 = refs[:n], refs[n], refs[n + 1]
        for i in range(n):
            for j in range(3 if relay_sems is None else 2):
                _await(land[i].at[0], r2.at[3 * i + j])
                _await(land[i].at[0], s2.at[3 * i + j])
            if relay_sems is not None:
                _await(land[i].at[0], refs[n + 3].at[i])
                _await(land[i].at[0], refs[n + 2].at[i])

    return pl.pallas_call(
        body, name=name,
        in_specs=[HBM] * n + [SEM] * n_sems + [ANY] * len(after), out_specs=[HBM] * n, out_shape=[_hbm_like(a) for a in lands],
        input_output_aliases={i: i for i in range(n)},
        compiler_params=SPLIT,
    )(*lands, send2, recv2, *(relay_sems or ()), *after)


def _pair_exchange(name, grads, shard_rows):
    n = len(grads)
    shapes = [(g.shape[1:] if r is None else (r, g.shape[1])) for g, r in zip(grads, shard_rows)]

    def body(*refs):
        ins, recv = refs[:n], refs[n:2 * n]
        send_sems, recv_sems = refs[2 * n:]
        x, y, c, _ = _place()
        sends = []
        for w in range(n):
            for k in range(N_CHIP):
                j, r = 2 * k + 1 - c, shard_rows[w]
                src = ins[w].at[j] if r is None else ins[w].at[pl.ds(pl.multiple_of(j * r, SUBLANE), r), :]
                sends.append(pltpu.make_async_remote_copy(
                    src_ref=src, dst_ref=recv[w].at[k],
                    send_sem=send_sems.at[w, k], recv_sem=recv_sems.at[w, k],
                    device_id=(x, y, 1 - c), device_id_type=MESH))
        for cp in sends:
            cp.start()
        for cp in sends:
            cp.wait()

    return pl.pallas_call(
        body, name=name,
        in_specs=[ANY] * n, out_specs=[ANY] * n,
        out_shape=[jax.ShapeDtypeStruct((N_CHIP, *shape), g.dtype) for g, shape in zip(grads, shapes)],
        scratch_shapes=[pltpu.SemaphoreType.DMA((n, N_CHIP))] * 2,
    )(*grads)


def _pair_sum_rows(name, grad, received, core):
    _, r, c = received.shape
    tc = _fit(c, 512)

    def body(core_ref, a_ref, b_ref, o_ref):
        o_ref[...] = (a_ref[...] + b_ref[...]).astype(o_ref.dtype)

    spec = pl.BlockSpec((None, r, tc), lambda k, i, core_ref: (k, 0, i))
    return pl.pallas_call(
        body, name=name,
        grid_spec=pltpu.PrefetchScalarGridSpec(
            num_scalar_prefetch=1, grid=(N_CHIP, c // tc),
            in_specs=[pl.BlockSpec((r, tc), lambda k, i, core_ref: (2 * k + core_ref[0], i)), spec],
            out_specs=spec),
        out_shape=jax.ShapeDtypeStruct(received.shape, BF16),
        compiler_params=_params("parallel", "parallel"),
    )(core, grad, received)


def _pair_sum(name, grad, received, core):
    _, r, c = received.shape
    rows = min(ROWS, r)
    assert r % rows == 0

    def body(core_ref, a_ref, b_ref, o_ref):
        o_ref[...] = (a_ref[...].astype(F32) + b_ref[...].astype(F32)).astype(o_ref.dtype)

    spec = pl.BlockSpec((None, rows, c), lambda k, i, core_ref: (k, i, 0))
    return pl.pallas_call(
        body, name=name,
        grid_spec=pltpu.PrefetchScalarGridSpec(
            num_scalar_prefetch=1, grid=(N_CHIP, r // rows),
            in_specs=[pl.BlockSpec((None, None, rows, c), lambda k, i, core_ref: (k, core_ref[0], i, 0)), spec],
            out_specs=spec),
        out_shape=jax.ShapeDtypeStruct(received.shape, received.dtype),
        compiler_params=_params("parallel", "parallel"),
    )(core, grad.reshape(N_CHIP, 2, r, c), received)


def _away_shard(src, k, c, shard_rows):
    if shard_rows is None:
        return src.at[k]
    return src.at[pl.ds(pl.multiple_of((2 * k + 1 - c) * shard_rows, SUBLANE), shard_rows), :]


def _pair_send_start(name, away, shard_rows=None):
    shape = away.shape if shard_rows is None else (N_CHIP, shard_rows, away.shape[1])
    land = lax.empty(shape, away.dtype)

    def body(src, dst, send, recv, src_thru, dst_thru, token):
        x, y, c, _ = _place()
        for k in range(N_CHIP):
            pltpu.make_async_remote_copy(src_ref=_away_shard(src, k, c, shard_rows), dst_ref=dst.at[k], send_sem=send.at[k],
                                         recv_sem=recv.at[k], device_id=(x, y, 1 - c), device_id_type=MESH).start()
        token[...] = jnp.zeros_like(token)

    sem = pltpu.SemaphoreType.DMA((N_CHIP,))
    out = pl.pallas_call(
        body, name=name,
        in_specs=[HBM, HBM], out_specs=[SEM, SEM, HBM, HBM, IN_VMEM],
        out_shape=[sem, sem, _hbm_like(away), _hbm_like(land), jax.ShapeDtypeStruct((SUBLANE, LANE), F32)],
        input_output_aliases={0: 2, 1: 3},
        compiler_params=SPLIT,
    )(_in_hbm(away), _in_hbm(land))
    return (out[0], out[1]), out[2], out[3], out[4]


def _pair_send_wait(name, sems, src, land, after, shard_rows=None):
    def body(src_ref, dst_ref, send, recv, *rest):
        for k in range(N_CHIP):
            _await(dst_ref.at[k], send.at[k])
            _await(dst_ref.at[k], recv.at[k])

    return pl.pallas_call(
        body, name=name,
        in_specs=[HBM, HBM, SEM, SEM] + [ANY] * len(after), out_specs=HBM, out_shape=_hbm_like(land),
        input_output_aliases={1: 0},
        compiler_params=SPLIT,
    )(src, land, *sems, *after)


def _chip_send_start(name, sums):
    n = len(sums)
    lands = [lax.empty(a.shape, a.dtype) for a in sums]

    def body(*refs):
        src, land = refs[:n], refs[n:2 * n]
        send, recv, token = refs[2 * n], refs[2 * n + 1], refs[-1]
        x, y, c, chips = _place()
        for w in range(n):
            for j, (px, py) in enumerate(chips):
                pltpu.make_async_remote_copy(
                    src_ref=src[w].at[2 * px + py], dst_ref=land[w].at[2 * x + y],
                    send_sem=send.at[3 * w + j], recv_sem=recv.at[3 * w + j],
                    device_id=(px, py, c), device_id_type=MESH).start()
        token[...] = jnp.zeros_like(token)

    sem = pltpu.SemaphoreType.DMA((3 * n,))
    out = pl.pallas_call(
        body, name=name,
        in_specs=[HBM] * (2 * n),
        out_specs=[SEM, SEM] + [HBM] * (2 * n) + [IN_VMEM],
        out_shape=[sem, sem] + [_hbm_like(a) for a in sums] + [_hbm_like(a) for a in lands]
        + [jax.ShapeDtypeStruct((SUBLANE, LANE), F32)],
        input_output_aliases={i: 2 + i for i in range(2 * n)},
        compiler_params=SPLIT,
    )(*[_in_hbm(a) for a in sums], *[_in_hbm(a) for a in lands])
    return (out[0], out[1]), out[2:2 + n], out[2 + n:2 + 2 * n], out[-1]


def _chip_send_wait(name, groups, after):
    counts = [len(g[1]) for g in groups]
    n = sum(counts)

    def body(*refs):
        land = refs[n:2 * n]
        sems = refs[2 * n:2 * n + 2 * len(groups)]
        w = 0
        for gi, count in enumerate(counts):
            for i in range(count):
                for j in range(3):
                    _await(land[w].at[0], sems[2 * gi].at[3 * i + j])
                    _await(land[w].at[0], sems[2 * gi + 1].at[3 * i + j])
                w += 1

    sums = [a for g in groups for a in g[1]]
    lands = [a for g in groups for a in g[2]]
    sems = [s for g in groups for s in g[0]]
    return pl.pallas_call(
        body, name=name,
        in_specs=[HBM] * (2 * n) + [SEM] * len(sems) + [ANY] * len(after),
        out_specs=[HBM] * n, out_shape=[_hbm_like(a) for a in lands],
        input_output_aliases={n + i: i for i in range(n)},
        compiler_params=SPLIT,
    )(*sums, *lands, *sems, *after)


def _small_all_reduce(part, after=()):
    _, w = part.shape

    def body(p_ref, *rest):
        o_ref, buf, send_sems, recv_sems = rest[len(after):]
        x, y, c, _ = _place()
        me = 4 * x + 2 * y + c
        buf[me] = jnp.sum(p_ref[...], axis=0, keepdims=True)
        copies = []
        for k in range(1, N_DEV):
            dx, dy, dc = (k >> 2) & 1, (k >> 1) & 1, k & 1
            copies.append(pltpu.make_async_remote_copy(
                src_ref=buf.at[me], dst_ref=buf.at[me], send_sem=send_sems.at[k - 1], recv_sem=recv_sems.at[k - 1],
                device_id=(x ^ dx, y ^ dy, c ^ dc), device_id_type=MESH))
        for cp in copies:
            cp.start()
        for cp in copies:
            cp.wait()
        tot = buf[0]
        for d in range(1, N_DEV):
            tot = tot + buf[d]
        o_ref[...] = tot
        loss = jnp.sum(tot[:, w - LANE:], axis=1, keepdims=True)
        o_ref[:, w - LANE:] = jnp.broadcast_to(loss, (1, LANE))

    return pl.pallas_call(
        body, name="small_all_reduce",
        in_specs=[IN_VMEM] + [ANY] * len(after), out_specs=IN_VMEM,
        out_shape=jax.ShapeDtypeStruct((1, w), F32),
        scratch_shapes=[pltpu.VMEM((N_DEV, 1, w), F32), pltpu.SemaphoreType.DMA((N_DEV - 1,)), pltpu.SemaphoreType.DMA((N_DEV - 1,))],
        compiler_params=pltpu.CompilerParams(vmem_limit_bytes=VMEM_LIMIT_BYTES),
    )(part, *after)


def _adamw(w, g, m, v):
    m = ADAM_B1 * m + (1.0 - ADAM_B1) * g
    v = ADAM_B2 * v + (1.0 - ADAM_B2) * (g * g)
    m_hat = m / (1.0 - ADAM_B1 ** ADAM_STEP)
    v_hat = v / (1.0 - ADAM_B2 ** ADAM_STEP)
    delta = -ADAM_LR * (m_hat / (jnp.sqrt(v_hat) + ADAM_EPS) + ADAM_WD * w)
    return delta, m, v


def _sum_adam_block(chip_ref, p_ref, own_ref, w_ref, m_ref, v_ref, g_ref, d_ref, mo_ref, vo_ref):
    g = None
    for k in range(N_CHIP):
        term = jnp.where(chip_ref[0] == k, own_ref[...], p_ref[k]).astype(F32)
        g = term if g is None else g + term
    g_ref[...] = g
    d_ref[...], mo_ref[...], vo_ref[...] = _adamw(w_ref[...], g, m_ref[...], v_ref[...])


def _sum_adam(name, parts, sums, chip, w, m, v, after=()):
    _, r, c = w.shape
    n_after = len(after)
    by_rows = r % ROWS == 0 or r < ROWS
    tr, tc = (min(ROWS, r), c) if by_rows else (r, _fit(c, 512))
    at = (lambda i: (i, 0)) if by_rows else (lambda i: (0, i))

    def body(chip_ref, p_ref, own_ref, w_ref, m_ref, v_ref, *rest):
        _sum_adam_block(chip_ref, p_ref, own_ref, w_ref, m_ref, v_ref, *rest[n_after:])

    blk = pl.BlockSpec((None, tr, tc), lambda i, chip_ref: (0, *at(i)))
    out = jax.ShapeDtypeStruct((1, r, c), F32)
    return pl.pallas_call(
        body, name=name,
        grid_spec=pltpu.PrefetchScalarGridSpec(
            num_scalar_prefetch=1, grid=(r // tr if by_rows else c // tc,),
            in_specs=[pl.BlockSpec((N_CHIP, tr, tc), lambda i, chip_ref: (0, *at(i))),
                      pl.BlockSpec((None, tr, tc), lambda i, chip_ref: (chip_ref[0], *at(i))), blk, blk, blk]
            + [ANY] * n_after,
            out_specs=[blk] * 4),
        out_shape=[out] * 4,
        compiler_params=_params("parallel"),
    )(chip, parts, sums, w, m, v, *after)


def _adam_gains(total, ws, ms, vs):
    n = len(ws)
    widths = [w.shape[1] for w in ws]

    def body(t_ref, *refs):
        w_refs, m_refs, v_refs, outs = refs[:n], refs[n:2 * n], refs[2 * n:3 * n], refs[3 * n:]
        off = 0
        for i in range(n):
            g = t_ref[:, off:off + widths[i]]
            off += widths[i]
            g_ref, d_ref, mo_ref, vo_ref = outs[4 * i:4 * i + 4]
            g_ref[...] = g
            d_ref[...], mo_ref[...], vo_ref[...] = _adamw(w_refs[i][...], g, m_refs[i][...], v_refs[i][...])

    out = pl.pallas_call(
        body, name="adam_gains",
        out_shape=[jax.ShapeDtypeStruct(w.shape, F32) for w in ws for _ in range(4)],
    )(total, *ws, *ms, *vs)
    return [tuple(out[4 * i:4 * i + 4]) for i in range(n)]


def _adam_taps(total, first_col, device, w, m, v):
    _, n_taps, cw = w.shape
    col_block = lambda t, dev: (0, first_col // cw + t * N_DEV + dev[0])
    tap = pl.BlockSpec((None, 1, cw), lambda t, dev: (t, 0, 0))

    def body(dev_ref, t_ref, w_ref, m_ref, v_ref, g_ref, d_ref, mo_ref, vo_ref):
        g = t_ref[...]
        g_ref[...] = g
        d_ref[...], mo_ref[...], vo_ref[...] = _adamw(w_ref[...], g, m_ref[...], v_ref[...])

    shape3 = (n_taps, 1, cw)
    out = pl.pallas_call(
        body, name="adam_taps",
        grid_spec=pltpu.PrefetchScalarGridSpec(
            num_scalar_prefetch=1, grid=(n_taps,),
            in_specs=[pl.BlockSpec((1, cw), col_block), tap, tap, tap], out_specs=[tap] * 4),
        out_shape=[jax.ShapeDtypeStruct(shape3, F32)] * 4,
    )(device, total, w.reshape(shape3), m.reshape(shape3), v.reshape(shape3))
    return tuple(o.reshape(w.shape) for o in out)


def kernel(x, pre_mix_g, w_in, conv_w, q_norm_g, w_uq, kv_norm_g, w_ukv, conv_out_g, attn_out_g, w_o, post_mix_g, pre_mlp_g, w_up, w_down, post_mlp_g, loss_target, m_pre_mix_g, m_w_in, m_conv_w, m_q_norm_g, m_w_uq, m_kv_norm_g, m_w_ukv, m_conv_out_g, m_attn_out_g, m_w_o, m_post_mix_g, m_pre_mlp_g, m_w_up, m_w_down, m_post_mlp_g, v_pre_mix_g, v_w_in, v_conv_w, v_q_norm_g, v_w_uq, v_kv_norm_g, v_w_ukv, v_conv_out_g, v_attn_out_g, v_w_o, v_post_mix_g, v_pre_mlp_g, v_w_up, v_w_down, v_post_mlp_g):
    me = 4 * lax.axis_index("x") + 2 * lax.axis_index("y") + lax.axis_index("c")
    core = lax.axis_index("c").astype(jnp.int32).reshape(1)
    chip = (2 * lax.axis_index("x") + lax.axis_index("y")).astype(jnp.int32).reshape(1)
    gains = (pre_mix_g, q_norm_g, kv_norm_g, conv_out_g, attn_out_g, post_mix_g, pre_mlp_g, post_mlp_g)
    gain_m = (m_pre_mix_g, m_q_norm_g, m_kv_norm_g, m_conv_out_g, m_attn_out_g, m_post_mix_g, m_pre_mlp_g, m_post_mlp_g)
    gain_v = (v_pre_mix_g, v_q_norm_g, v_kv_norm_g, v_conv_out_g, v_attn_out_g, v_post_mix_g, v_pre_mlp_g, v_post_mlp_g)
    names = ("w_in", "w_uq", "w_ukv", "w_o", "w_up", "w_down")
    big = dict(zip(names, (w_in, w_uq, w_ukv, w_o, w_up, w_down)))
    big_m = dict(zip(names, (m_w_in, m_w_uq, m_w_ukv, m_w_o, m_w_up, m_w_down)))
    big_v = dict(zip(names, (v_w_in, v_w_uq, v_w_ukv, v_w_o, v_w_up, v_w_down)))
    n_heads = attn_out_g.shape[1] // HEAD
    n_taps = conv_w.shape[1]

    gathered = ("w_in", "conv", "w_uq", "w_ukv", "w_o", "w_up", "w_down")
    gather_groups = ((0, 1), (2, 3, 4), (5,), (6,))
    taps = jnp.pad(conv_w[0], ((0, SUBLANE - n_taps), (0, 0)))
    relayed_groups = (0, 2, 3)
    sems1, shards, lands, token = _gather_start("gather_start_first", [w_in[0].astype(BF16), taps], ((0, 1),), relayed=(0,))
    sems1, shards, lands = list(sems1), list(shards), list(lands)
    behind = token[0, 0]
    rest = [(big[nm][0] + behind).astype(BF16) for nm in gathered[2:]]

    def start_rest(after):
        sems_b, shards_b, lands_b, started = _gather_start("gather_start_rest", rest, ((0, 1, 2), (3,), (4,)), relayed=(1, 2),
                                                          after=after)
        sems1.extend(sems_b)
        shards.extend(shards_b)
        lands.extend(lands_b)
        return started

    cols = lambda a: jnp.concatenate([a[j] for j in range(N_DEV)], axis=1)
    rows = lambda a: a.reshape(N_DEV * a.shape[1], a.shape[2])
    device = me.astype(jnp.int32).reshape(1)
    own_in = lambda a, shard: lax.dynamic_update_index_in_dim(a, shard, me, 0)
    q_pieces = [(h, 0, HEAD) for h in range(n_heads)] + [(h, HEAD, QK) for h in range(n_heads)]
    ready = {
        "w_in": lambda a, shard: _join_col_shards("join_w_in", a, shard, device),
        "conv": lambda a, shard: cols(own_in(a, shard))[:n_taps],
        "w_uq": lambda a, shard: _join_col_shards("join_w_uq", a, shard, device, q_pieces),
        "w_ukv": lambda a, shard: cols(own_in(a, shard)),
        "w_o": lambda a, shard: rows(own_in(a, shard)),
        "w_up": own_in,
        "w_down": lambda a, shard: rows(own_in(a, shard)),
    }
    assert w_uq.shape[2] == QK

    class Weights:
        def __init__(self):
            self.passed, self.relayed = {}, {}

        def forward(self, group, after):
            idx = gather_groups[group]
            if group == 0:
                after = (*after, *rest)
            self.passed[group] = _gather_forward(f"gather_forward_{group}", [shards[i] for i in idx], [lands[i] for i in idx],
                                                 *sems1[group], after, relayed=group in relayed_groups)
            return tuple(self.passed[group][1])

        def relay(self, group, after):
            sems2, mid = self.passed[group]
            self.relayed[group], mid = _gather_relay_forward(f"gather_relay_{group}", mid, *sems2, after)
            self.passed[group] = (sems2, mid)
            if group == 0:
                start_rest(tuple(mid))
            return tuple(mid)

        def ready(self, group, after):
            sems2, mid = self.passed[group]
            full = _gather_wait(f"gather_wait_{group}", mid, *sems2, after, relay_sems=self.relayed.get(group))
            out = []
            return [ready[gathered[i]](a, shards[i]) for i, a in zip(gather_groups[group], full)]

    weights = Weights()

    col_blocks = lambda g: g.reshape(g.shape[0], N_DEV, g.shape[1] // N_DEV).transpose(1, 0, 2)
    row_blocks = lambda g: g.reshape(N_DEV, g.shape[0] // N_DEV, g.shape[1])
    grad_groups = (("w_down",), ("w_up",), ("w_o", "w_uq", "w_ukv"), ("w_in",))
    transposed = {"w_in": w_in.shape[2], "w_uq": w_uq.shape[2]}
    to_blocks = {
        "w_in": lambda g: g, "w_uq": lambda g: _unpermute_q_rows(g, n_heads),
        "w_ukv": col_blocks, "w_o": row_blocks, "w_up": lambda g: g, "w_down": row_blocks,
    }
    in_flight = []

    class Grads:
        def __init__(self):
            self.core = core
            self.away = {}

        def send_sums(self, group, sums):
            sems, sums, parts, tok = _chip_send_start(f"chip_send_start_{group}", list(sums))
            in_flight.append((sems, sums, parts))
            return (tok,)

        def full(self, group, arrays, received=None):
            nms = grad_groups[group]
            if received is None:
                blocks = [to_blocks[nm](g) for nm, g in zip(nms, arrays)]
                got = _pair_exchange(f"pair_exchange_{group}", blocks, [transposed.get(nm) for nm in nms])
            else:
                blocks, got = [self.away[group][1]], [self.received(group, received)]
            sums = [(_pair_sum_rows if nm in transposed else _pair_sum)(f"pair_sum_{nm}", g, r, core)
                    for nm, g, r in zip(nms, blocks, got)]
            return self.send_sums(group, sums)

        def send_away(self, group, half):
            nm = grad_groups[group][0]
            rows = transposed.get(nm)
            sems, src, land, tok = _pair_send_start(f"pair_send_start_{group}", half if rows is None else to_blocks[nm](half), rows)
            self.away[group] = (sems, src, land, rows)
            return (tok,)

        def received(self, group, after):
            sems, src, land, rows = self.away[group]
            return _pair_send_wait(f"pair_send_wait_{group}", sems, src, land, after, rows)

        def update_now(self, group, after):
            return update(str(group), group, group + 1, after)

    big_out = {}

    def update(tag, first, last, after):
        picked = [i for i in range(first, last) if grad_groups[i][0] not in big_out]
        groups = [in_flight[i] for i in picked]
        parts = _chip_send_wait("chip_send_wait_" + tag, groups, after)
        nms = [nm for i in picked for nm in grad_groups[i]]
        sums = [a for _, s, _ in groups for a in s]
        for nm, p, s in zip(nms, parts, sums):
            view = (lambda a: jnp.swapaxes(a, 1, 2)) if nm in transposed else (lambda a: a)
            out = _sum_adam("adam_" + nm, p, s, chip, view(big[nm]), view(big_m[nm]), view(big_v[nm]), after=after)
            after = (out[0],)
            big_out[nm] = [view(o) for o in out]
        return after

    grad_x, small = _local_step(x[0], loss_target[0], gains, weights, Grads(), first_after=(token,))

    after = update("early", 0, len(in_flight) - 1, (grad_x,))
    total = _small_all_reduce(small, after=after)
    update("late", len(in_flight) - 1, len(in_flight), (total,))
    big_out = [big_out[nm] for nm in names]

    gain_out = _adam_gains(total, gains, gain_m, gain_v)
    taps_out = _adam_taps(total, sum(g.shape[1] for g in gains), me.astype(jnp.int32).reshape(1), conv_w, m_conv_w, v_conv_w)
    loss = total[0, total.shape[1] - 1]

    order = (0, "w_in", "conv", 1, "w_uq", 2, "w_ukv", 3, 4, "w_o", 5, 6, "w_up", "w_down", 7)
    by_name = dict(zip(names, big_out))
    outs = [loss, grad_x[None]]
    for kind in range(4):
        for item in order:
            if item == "conv":
                outs.append(taps_out[kind])
            elif isinstance(item, int):
                outs.append(gain_out[item][kind])
            else:
                outs.append(by_name[item][kind])
    return tuple(outs)
```

```python
import math

import jax
import jax.numpy as jnp
from jax import lax
from jax.experimental import pallas as pl
from jax.experimental.pallas import tpu as pltpu

F32 = jnp.float32
BF16 = jnp.bfloat16

EPS = 1e-6
NEG_INF = -1e30
HEAD = 128
ROPE = 64
QK = HEAD + ROPE
CHUNK = 64
ROPE_THETA = 10000.0
ADAM_LR, ADAM_B1, ADAM_B2, ADAM_EPS, ADAM_WD, ADAM_STEP = 0.001, 0.9, 0.999, 1e-08, 0.01, 10

LANE = 128
SUBLANE = 8
VMEM_LIMIT_BYTES = 56 * 1024 * 1024

N_DEV = 8
N_CHIP = 4
MESH = pl.DeviceIdType.MESH


def _params(*sem):
    return pltpu.CompilerParams(dimension_semantics=sem, vmem_limit_bytes=VMEM_LIMIT_BYTES)


ANY = pl.BlockSpec(memory_space=pl.ANY)


def _call(body, *, in_specs, after=(), **kw):
    n_in, n_after = len(in_specs), len(after)

    def ordered(*refs):
        body(*refs[:n_in], *refs[n_in + n_after:])

    call = pl.pallas_call(ordered, in_specs=[*in_specs, *[ANY] * n_after], **kw)
    return lambda *operands: call(*operands, *after)


def _sublane_sum(v):
    r, w = v.shape
    return jnp.sum(v.reshape(r // SUBLANE, SUBLANE, w), axis=0)


def _rstd(x):
    return lax.rsqrt(jnp.mean(x * x, axis=-1, keepdims=True) + EPS)


def _rms_bwd(x, g, dy):
    r = _rstd(x)
    xh = x * r
    dxh = dy * g
    dx = r * (dxh - xh * jnp.mean(dxh * xh, axis=-1, keepdims=True))
    return dx, dy * xh


def _accumulate(ref, val, step):
    @pl.when(step == 0)
    def _():
        ref[...] = val

    @pl.when(step > 0)
    def _():
        ref[...] += val


NN = ((1,), (0,))
NT = ((1,), (1,))
TN = ((0,), (0,))


def _matmul(name, a, b, *, grid, a_spec, b_spec, out_shape, out_specs, contract, nk=1, acc_shape=None,
            extras=(), extra_specs=(), epilogue=None, after=()):
    multi = isinstance(out_shape, (tuple, list))
    out_shapes = tuple(out_shape) if multi else (out_shape,)
    n_out = len(out_shapes)
    n_extra = len(extras)

    def body(a_ref, b_ref, *rest):
        x_refs = rest[:n_extra]
        o_refs = rest[n_extra:n_extra + n_out]

        def emit(acc):
            vals = epilogue(acc, *[r[...] for r in x_refs]) if epilogue else (acc,)
            for r, v in zip(o_refs, vals):
                r[...] = v.astype(r.dtype)

        p = lax.dot_general(a_ref[...], b_ref[...], (contract, ((), ())), preferred_element_type=F32)
        if nk == 1:
            emit(p)
        else:
            acc_ref = rest[n_extra + n_out]
            k = pl.program_id(2)
            _accumulate(acc_ref, p, k)

            @pl.when(k == nk - 1)
            def _():
                emit(acc_ref[...])

    sem = ("parallel", "parallel") + (("arbitrary",) if nk > 1 else ())
    return _call(
        body, name=name, grid=grid, after=after,
        in_specs=[a_spec, b_spec, *extra_specs],
        out_specs=out_specs,
        out_shape=out_shape,
        scratch_shapes=[pltpu.VMEM(acc_shape, F32)] if nk > 1 else [],
        compiler_params=_params(*sem),
    )(a, b, *extras)


def _fit(n, tile):
    if n <= tile:
        return n
    t = tile - tile % LANE
    while n % t:
        t -= LANE
    return t


def _mm_nn(name, a, b, out_dtype, tm, tn, after=()):
    m, k = a.shape
    n = b.shape[1]
    tm, tn = _fit(m, tm), _fit(n, tn)
    return _matmul(name, a, b, grid=(m // tm, n // tn), after=after,
                   a_spec=pl.BlockSpec((tm, k), lambda i, j: (i, 0)),
                   b_spec=pl.BlockSpec((k, tn), lambda i, j: (0, j)),
                   out_shape=jax.ShapeDtypeStruct((m, n), out_dtype),
                   out_specs=pl.BlockSpec((tm, tn), lambda i, j: (i, j)), contract=NN)


def _mm_nt(name, a, b, out_dtype, tm, tn, after=()):
    m, k = a.shape
    n = b.shape[0]
    tm, tn = _fit(m, tm), _fit(n, tn)
    return _matmul(name, a, b, grid=(m // tm, n // tn), after=after,
                   a_spec=pl.BlockSpec((tm, k), lambda i, j: (i, 0)),
                   b_spec=pl.BlockSpec((tn, k), lambda i, j: (j, 0)),
                   out_shape=jax.ShapeDtypeStruct((m, n), out_dtype),
                   out_specs=pl.BlockSpec((tm, tn), lambda i, j: (i, j)), contract=NT)


def _mm_tn(name, a, b, out_dtype, tm, tn):
    s, m = a.shape
    n = b.shape[1]
    tm, tn = _fit(m, tm), _fit(n, tn)
    return _matmul(name, a, b, grid=(m // tm, n // tn),
                   a_spec=pl.BlockSpec((s, tm), lambda i, j: (0, i)),
                   b_spec=pl.BlockSpec((s, tn), lambda i, j: (0, j)),
                   out_shape=jax.ShapeDtypeStruct((m, n), out_dtype),
                   out_specs=pl.BlockSpec((tm, tn), lambda i, j: (i, j)), contract=TN)


ROWS = 256


def _row_spec(rows, width):
    return pl.BlockSpec((rows, width), lambda i: (i, 0))


def _fixed_spec(rows, width):
    return pl.BlockSpec((rows, width), lambda i: (0, 0))


def _column_pieces(rows, start, width):
    piece = math.gcd(start, width)
    assert piece % LANE == 0
    return [pl.BlockSpec((rows, piece), lambda i, b=start // piece + p: (i, b)) for p in range(width // piece)]


def _rms_fwd(name, x, g, cols=None, after=()):
    s = x.shape[0]
    start, w = cols or (0, x.shape[1])
    rows = min(ROWS, s)
    pieces = _column_pieces(rows, start, w) if cols else [_row_spec(rows, w)]
    n = len(pieces)

    def body(*refs):
        g_ref, o_ref = refs[n:]
        xv = refs[0][...] if n == 1 else jnp.concatenate([r[...] for r in refs[:n]], axis=1)
        o_ref[...] = (xv * _rstd(xv) * g_ref[...]).astype(o_ref.dtype)

    return _call(
        body, name=name, grid=(s // rows,), after=after,
        in_specs=[*pieces, _fixed_spec(1, w)],
        out_specs=_row_spec(rows, w),
        out_shape=jax.ShapeDtypeStruct((s, w), BF16),
        compiler_params=_params("parallel"),
    )(*[x] * n, g)


def _rms_bwd_call(name, x, g, dy, out_dtype, cols=None, after=()):
    s = x.shape[0]
    start, w = cols or (0, x.shape[1])
    rows = min(ROWS, s)
    pieces = _column_pieces(rows, start, w) if cols else [_row_spec(rows, w)]
    n = len(pieces)

    def body(*refs):
        g_ref, dy_ref, dx_ref, dg_ref = refs[n:]
        xv = refs[0][...] if n == 1 else jnp.concatenate([r[...] for r in refs[:n]], axis=1)
        dx, dgc = _rms_bwd(xv, g_ref[...], dy_ref[...].astype(F32))
        dx_ref[...] = dx.astype(dx_ref.dtype)
        _accumulate(dg_ref, _sublane_sum(dgc), pl.program_id(0))

    return _call(
        body, name=name, grid=(s // rows,), after=after,
        in_specs=[*pieces, _fixed_spec(1, w), _row_spec(rows, w)],
        out_specs=[_row_spec(rows, w), _fixed_spec(SUBLANE, w)],
        out_shape=[jax.ShapeDtypeStruct((s, w), out_dtype), jax.ShapeDtypeStruct((SUBLANE, w), F32)],
        compiler_params=_params("arbitrary"),
    )(*[x] * n, g, dy)


def _norm_up(name, x, cols, g, w, after=()):
    s = x.shape[0]
    start, width = cols
    n = w.shape[1]
    tm = min(TILE_M, s)
    pieces = _column_pieces(tm, start, width)
    n_p = len(pieces)

    def body(*refs):
        g_ref, w_ref, xn_ref, o_ref = refs[n_p:]
        xv = refs[0][...] if n_p == 1 else jnp.concatenate([r[...] for r in refs[:n_p]], axis=1)
        xn = (xv * _rstd(xv) * g_ref[...]).astype(BF16)
        xn_ref[...] = xn
        o_ref[...] = jnp.dot(xn, w_ref[...], preferred_element_type=F32)

    return _call(
        body, name=name, grid=(s // tm,), after=after,
        in_specs=[*pieces, _fixed_spec(1, width), _fixed_spec(width, n)],
        out_specs=[_row_spec(tm, width), _row_spec(tm, n)],
        out_shape=[jax.ShapeDtypeStruct((s, width), BF16), jax.ShapeDtypeStruct((s, n), F32)],
        compiler_params=_params("parallel"),
    )(*[x] * n_p, g, w)


def _up_norm_bwd(name, dy, w, x, cols, g, after=()):
    s, n = dy.shape
    start, width = cols
    tm = min(TILE_M, s)
    pieces = _column_pieces(tm, start, width)
    n_p = len(pieces)

    def body(dy_ref, w_ref, *refs):
        g_ref, dx_ref, dg_ref = refs[n_p:]
        xv = refs[0][...] if n_p == 1 else jnp.concatenate([r[...] for r in refs[:n_p]], axis=1)
        dxn = lax.dot_general(dy_ref[...], w_ref[...], (NT, ((), ())), preferred_element_type=F32)
        dx, dgc = _rms_bwd(xv, g_ref[...], dxn)
        dx_ref[...] = dx.astype(dx_ref.dtype)
        _accumulate(dg_ref, _sublane_sum(dgc), pl.program_id(0))

    return _call(
        body, name=name, grid=(s // tm,), after=after,
        in_specs=[_row_spec(tm, n), _fixed_spec(width, n), *pieces, _fixed_spec(1, width)],
        out_specs=[_row_spec(tm, width), _fixed_spec(SUBLANE, width)],
        out_shape=[jax.ShapeDtypeStruct((s, width), BF16), jax.ShapeDtypeStruct((SUBLANE, width), F32)],
        compiler_params=_params("arbitrary"),
    )(dy, w, *[x] * n_p, g)


def _mid_fwd(x, y, g_post, g_pre, after=()):
    s, w = x.shape
    rows = min(ROWS, s)

    def body(x_ref, y_ref, gp_ref, gq_ref, x2_ref, h2_ref):
        yv = y_ref[...]
        x2 = x_ref[...] + yv * _rstd(yv) * gp_ref[...]
        x2_ref[...] = x2
        h2_ref[...] = (x2 * _rstd(x2) * gq_ref[...]).astype(h2_ref.dtype)

    return _call(
        body, name="mid_fwd", grid=(s // rows,), after=after,
        in_specs=[_row_spec(rows, w), _row_spec(rows, w), _fixed_spec(1, w), _fixed_spec(1, w)],
        out_specs=[_row_spec(rows, w), _row_spec(rows, w)],
        out_shape=[jax.ShapeDtypeStruct((s, w), F32), jax.ShapeDtypeStruct((s, w), BF16)],
        compiler_params=_params("parallel"),
    )(x, y, g_post, g_pre)


def _head(m, x2, tgt, g):
    s, w = m.shape
    rows = min(ROWS, s)

    def body(m_ref, x2_ref, t_ref, g_ref, dout_ref, dm_ref, dg_ref, loss_ref):
        mv = m_ref[...]
        gv = g_ref[...]
        out = x2_ref[...] + mv * _rstd(mv) * gv
        err = out - t_ref[...]
        dout = err * (1.0 / w)
        dout_ref[...] = dout
        dm, dgc = _rms_bwd(mv, gv, dout)
        dm_ref[...] = dm.astype(dm_ref.dtype)
        sq = err * err
        lanes = sq[:, 0:LANE]
        for j in range(1, w // LANE):
            lanes = lanes + sq[:, j * LANE:(j + 1) * LANE]
        step = pl.program_id(0)
        _accumulate(dg_ref, _sublane_sum(dgc), step)
        _accumulate(loss_ref, _sublane_sum(lanes) * (0.5 / w), step)

    return pl.pallas_call(
        body, name="head", grid=(s // rows,),
        in_specs=[_row_spec(rows, w), _row_spec(rows, w), _row_spec(rows, w), _fixed_spec(1, w)],
        out_specs=[_row_spec(rows, w), _row_spec(rows, w), _fixed_spec(SUBLANE, w), _fixed_spec(SUBLANE, LANE)],
        out_shape=[jax.ShapeDtypeStruct((s, w), F32), jax.ShapeDtypeStruct((s, w), BF16),
                   jax.ShapeDtypeStruct((SUBLANE, w), F32), jax.ShapeDtypeStruct((SUBLANE, LANE), F32)],
        compiler_params=_params("arbitrary"),
    )(m, x2, tgt, g)


def _mid_bwd(x2, y, d_out, d_h2, g_pre, g_post, after=()):
    s, w = x2.shape
    rows = min(ROWS, s)

    def body(x2_ref, y_ref, dout_ref, dh2_ref, gq_ref, gp_ref, dx2_ref, dy_ref, dgq_ref, dgp_ref):
        dx, dgq = _rms_bwd(x2_ref[...], gq_ref[...], dh2_ref[...])
        dx2 = dout_ref[...] + dx
        dx2_ref[...] = dx2
        dy, dgp = _rms_bwd(y_ref[...], gp_ref[...], dx2)
        dy_ref[...] = dy.astype(dy_ref.dtype)
        step = pl.program_id(0)
        _accumulate(dgq_ref, _sublane_sum(dgq), step)
        _accumulate(dgp_ref, _sublane_sum(dgp), step)

    return _call(
        body, name="mid_bwd", grid=(s // rows,), after=after,
        in_specs=[_row_spec(rows, w)] * 4 + [_fixed_spec(1, w)] * 2,
        out_specs=[_row_spec(rows, w), _row_spec(rows, w), _fixed_spec(SUBLANE, w), _fixed_spec(SUBLANE, w)],
        out_shape=[jax.ShapeDtypeStruct((s, w), F32), jax.ShapeDtypeStruct((s, w), BF16),
                   jax.ShapeDtypeStruct((SUBLANE, w), F32), jax.ShapeDtypeStruct((SUBLANE, w), F32)],
        compiler_params=_params("arbitrary"),
    )(x2, y, d_out, d_h2, g_pre, g_post)


def _first_bwd(x, g, d_h1, d_x2, after=()):
    s, w = x.shape
    rows = min(ROWS, s)

    def body(x_ref, g_ref, dh_ref, dx2_ref, dx_ref, dg_ref):
        dx, dgc = _rms_bwd(x_ref[...], g_ref[...], dh_ref[...])
        dx_ref[...] = dx2_ref[...] + dx
        _accumulate(dg_ref, _sublane_sum(dgc), pl.program_id(0))

    return _call(
        body, name="first_bwd", grid=(s // rows,), after=after,
        in_specs=[_row_spec(rows, w), _fixed_spec(1, w), _row_spec(rows, w), _row_spec(rows, w)],
        out_specs=[_row_spec(rows, w), _fixed_spec(SUBLANE, w)],
        out_shape=[jax.ShapeDtypeStruct((s, w), F32), jax.ShapeDtypeStruct((SUBLANE, w), F32)],
        compiler_params=_params("arbitrary"),
    )(x, g, d_h1, d_x2)


def _shift_down(v, k):
    t = lax.broadcasted_iota(jnp.int32, v.shape, 0)
    return jnp.where(t >= k, pltpu.roll(v, k, 0), 0.0)


def _shift_up(v, k):
    n = v.shape[0]
    t = lax.broadcasted_iota(jnp.int32, v.shape, 0)
    return jnp.where(t < n - k, pltpu.roll(v, n - k, 0), 0.0)


def _conv_core(u, b, c, w):
    z = c * u
    conv = w[0:1, :] * _shift_down(z, 2) + w[1:2, :] * _shift_down(z, 1) + w[2:3, :] * z
    return z, conv, b * conv


def _conv_fwd(proj, conv_w, g, n_groups, out_width, after=()):
    s = proj.shape[0]

    def body(u_ref, b_ref, c_ref, w_ref, g_ref, o_ref):
        _, _, yr = _conv_core(u_ref[...], b_ref[...], c_ref[...], w_ref[...])
        o_ref[...] = (yr * _rstd(yr) * g_ref[...]).astype(o_ref.dtype)

    col = lambda k: pl.BlockSpec((s, HEAD), lambda i: (0, k * n_groups + i))
    return _call(
        body, name="conv_fwd", grid=(n_groups,), after=after,
        in_specs=[col(0), col(1), col(2), pl.BlockSpec((3, HEAD), lambda i: (0, i)), pl.BlockSpec((1, HEAD), lambda i: (0, i))],
        out_specs=pl.BlockSpec((s, HEAD), lambda i: (0, i)),
        out_shape=jax.ShapeDtypeStruct((s, out_width), BF16),
        compiler_params=_params("parallel"),
    )(proj, proj, proj, conv_w, g)


def _conv_bwd(proj, d_mix, conv_w, g, n_groups):
    s = proj.shape[0]
    width = n_groups * HEAD

    def body(u_ref, b_ref, c_ref, dy_ref, w_ref, g_ref, du_ref, db_ref, dc_ref, dg_ref, dw_ref):
        u, b, c, w = u_ref[...], b_ref[...], c_ref[...], w_ref[...]
        z, conv, yr = _conv_core(u, b, c, w)
        dyr, dgc = _rms_bwd(yr, g_ref[...], dy_ref[...])
        dconv = dyr * b
        db_ref[...] = (dyr * conv).astype(db_ref.dtype)
        dz = w[2:3, :] * dconv + w[1:2, :] * _shift_up(dconv, 1) + w[0:1, :] * _shift_up(dconv, 2)
        dc_ref[...] = (dz * u).astype(dc_ref.dtype)
        du_ref[...] = (dz * c).astype(du_ref.dtype)
        dg_ref[...] = _sublane_sum(dgc)
        dw_ref[0] = _sublane_sum(dconv * _shift_down(z, 2))
        dw_ref[1] = _sublane_sum(dconv * _shift_down(z, 1))
        dw_ref[2] = _sublane_sum(dconv * z)

    col = lambda k: pl.BlockSpec((s, HEAD), lambda i: (0, k * n_groups + i))
    grp = pl.BlockSpec((s, HEAD), lambda i: (0, i))
    return pl.pallas_call(
        body, name="conv_bwd", grid=(n_groups,),
        in_specs=[col(0), col(1), col(2), grp, pl.BlockSpec((3, HEAD), lambda i: (0, i)), pl.BlockSpec((1, HEAD), lambda i: (0, i))],
        out_specs=[grp, grp, grp, pl.BlockSpec((SUBLANE, HEAD), lambda i: (0, i)),
                   pl.BlockSpec((3, SUBLANE, HEAD), lambda i: (0, 0, i))],
        out_shape=[jax.ShapeDtypeStruct((s, width), BF16)] * 3
        + [jax.ShapeDtypeStruct((SUBLANE, width), F32), jax.ShapeDtypeStruct((3, SUBLANE, width), F32)],
        compiler_params=_params("parallel"),
    )(proj, proj, proj, d_mix, conv_w, g)


def _rope_tables(s, n_heads):
    pos = jnp.arange(s, dtype=F32)
    inv_freq = jnp.power(ROPE_THETA, -jnp.arange(0, ROPE, 2, dtype=F32) / ROPE)
    ang = pos[:, None] * inv_freq[None, :]
    cos, sin = jnp.cos(ang), jnp.sin(ang)
    cs = jnp.concatenate([cos, cos], axis=1)
    sn = jnp.concatenate([-sin, sin], axis=1)
    pad = jnp.zeros((s, LANE - ROPE), F32)
    return (jnp.tile(cs, (1, n_heads)), jnp.tile(sn, (1, n_heads)),
            jnp.concatenate([cs, pad], axis=1), jnp.concatenate([sn, pad], axis=1))


def _swap_halves(v):
    w = v.shape[1]
    lane = lax.broadcasted_iota(jnp.int32, v.shape, 1)
    first = (lane % ROPE) < (ROPE // 2)
    return jnp.where(first, pltpu.roll(v, w - ROPE // 2, 1), pltpu.roll(v, ROPE // 2, 1))


def _pack_heads(q, kv, proj, kr_col, tables, n_heads, after=()):
    s = q.shape[0]
    rows = min(ROWS, s)
    cq, sq, ck, sk = tables
    wq = n_heads * ROPE

    def body(q_ref, kv_ref, kr_ref, cq_ref, sq_ref, ck_ref, sk_ref, qo_ref, ko_ref, vo_ref):
        qr = q_ref[:, n_heads * HEAD:]
        qr = qr * cq_ref[...] + _swap_halves(qr) * sq_ref[...]
        krv = kr_ref[...]
        krv = krv * ck_ref[...] + _swap_halves(krv) * sk_ref[...]
        for h in range(n_heads):
            qo_ref[h] = jnp.concatenate([q_ref[:, h * HEAD:(h + 1) * HEAD], qr[:, h * ROPE:(h + 1) * ROPE]], axis=1).astype(BF16)
            ko_ref[h] = jnp.concatenate([kv_ref[:, 2 * h * HEAD:(2 * h + 1) * HEAD], krv[:, :ROPE]], axis=1).astype(BF16)
            vo_ref[h] = kv_ref[:, (2 * h + 1) * HEAD:(2 * h + 2) * HEAD].astype(BF16)

    hs = lambda w: pl.BlockSpec((n_heads, rows, w), lambda i: (0, i, 0))
    return _call(
        body, name="pack_heads", grid=(s // rows,), after=after,
        in_specs=[_row_spec(rows, q.shape[1]), _row_spec(rows, kv.shape[1]), pl.BlockSpec((rows, LANE), lambda i: (i, kr_col // LANE)),
                  _row_spec(rows, wq), _row_spec(rows, wq), _row_spec(rows, LANE), _row_spec(rows, LANE)],
        out_specs=[hs(QK), hs(QK), hs(HEAD)],
        out_shape=[jax.ShapeDtypeStruct((n_heads, s, QK), BF16), jax.ShapeDtypeStruct((n_heads, s, QK), BF16),
                   jax.ShapeDtypeStruct((n_heads, s, HEAD), BF16)],
        compiler_params=_params("parallel"),
    )(q, kv, proj, cq, sq, ck, sk)


def _unpack_heads(dq, dk, dv, tables, n_heads):
    s = dq.shape[1]
    rows = min(ROWS, s)
    cq, sq, ck, sk = tables
    wq = n_heads * ROPE

    def body(dq_ref, dk_ref, dv_ref, cq_ref, sq_ref, ck_ref, sk_ref, qo_ref, kvo_ref, kro_ref):
        dqr = jnp.concatenate([dq_ref[h][:, HEAD:] for h in range(n_heads)], axis=1)
        dqr = dqr * cq_ref[...] - _swap_halves(dqr) * sq_ref[...]
        dkr = dk_ref[0][:, HEAD:]
        for h in range(1, n_heads):
            dkr = dkr + dk_ref[h][:, HEAD:]
        dkr = jnp.concatenate([dkr, jnp.zeros((rows, LANE - ROPE), F32)], axis=1)
        dkr = dkr * ck_ref[...] - _swap_halves(dkr) * sk_ref[...]
        kro_ref[...] = dkr.astype(kro_ref.dtype)
        qo_ref[:, n_heads * HEAD:] = dqr.astype(qo_ref.dtype)
        for h in range(n_heads):
            qo_ref[:, h * HEAD:(h + 1) * HEAD] = dq_ref[h][:, :HEAD].astype(qo_ref.dtype)
            kvo_ref[:, 2 * h * HEAD:(2 * h + 1) * HEAD] = dk_ref[h][:, :HEAD].astype(kvo_ref.dtype)
            kvo_ref[:, (2 * h + 1) * HEAD:(2 * h + 2) * HEAD] = dv_ref[h].astype(kvo_ref.dtype)

    hs = lambda w: pl.BlockSpec((n_heads, rows, w), lambda i: (0, i, 0))
    return pl.pallas_call(
        body, name="unpack_heads", grid=(s // rows,),
        in_specs=[hs(QK), hs(QK), hs(HEAD), _row_spec(rows, wq), _row_spec(rows, wq), _row_spec(rows, LANE), _row_spec(rows, LANE)],
        out_specs=[_row_spec(rows, n_heads * QK), _row_spec(rows, 2 * n_heads * HEAD), _row_spec(rows, LANE)],
        out_shape=[jax.ShapeDtypeStruct((s, n_heads * QK), BF16), jax.ShapeDtypeStruct((s, 2 * n_heads * HEAD), BF16),
                   jax.ShapeDtypeStruct((s, LANE), BF16)],
        compiler_params=_params("parallel"),
    )(dq, dk, dv, cq, sq, ck, sk)


TQ = 256


LOG2_E = 1.4426950408889634


def _softmax_parts(q, k):
    tq, n_keys = q.shape[0], k.shape[0]
    sc = lax.dot_general(q, k, (NT, ((), ())), preferred_element_type=F32) * (QK ** -0.5 * LOG2_E)
    row = lax.broadcasted_iota(jnp.int32, (tq, tq), 0)
    col = lax.broadcasted_iota(jnp.int32, (tq, tq), 1)
    own = jnp.where(col // CHUNK <= row // CHUNK, sc[:, n_keys - tq:], NEG_INF)
    sc = own if n_keys == tq else jnp.concatenate([sc[:, :n_keys - tq], own], axis=1)
    e = jnp.exp2(sc - jnp.max(sc, axis=-1, keepdims=True))
    return e, 1.0 / jnp.sum(e, axis=-1, keepdims=True)


def _attn_fwd(q, k, v, g, mix, col0):
    n_heads, s, _ = q.shape
    tq = min(TQ, s)
    assert tq % CHUNK == 0 and s % tq == 0

    def body(q_ref, k_ref, v_ref, g_ref, mix_ref, o_ref, y_ref):
        for c in range(s // tq):
            rows, n_keys = pl.ds(c * tq, tq), (c + 1) * tq
            e, inv = _softmax_parts(q_ref[rows, :], k_ref[0:n_keys, :])
            o = jnp.dot(e.astype(BF16), v_ref[0:n_keys, :], preferred_element_type=F32) * inv
            o_ref[rows, :] = o
            y_ref[rows, :] = (o * _rstd(o) * g_ref[...]).astype(y_ref.dtype)

    head = lambda w: pl.BlockSpec((None, s, w), lambda h: (h, 0, 0))
    return pl.pallas_call(
        body, name="attn_fwd", grid=(n_heads,),
        in_specs=[head(QK), head(QK), head(HEAD), pl.BlockSpec((1, HEAD), lambda h: (0, h)), ANY],
        out_specs=[head(HEAD), pl.BlockSpec((s, HEAD), lambda h: (0, col0 // HEAD + h))],
        out_shape=[jax.ShapeDtypeStruct((n_heads, s, HEAD), F32), jax.ShapeDtypeStruct(mix.shape, mix.dtype)],
        input_output_aliases={4: 1},
        compiler_params=_params("parallel"),
    )(q, k, v, g, mix)


def _attn_bwd(q, k, v, o, d_mix, g, col0, after=()):
    n_heads, s, _ = q.shape
    tq = min(TQ, s)

    def body(q_ref, k_ref, v_ref, o_ref, dy_ref, g_ref, dq_ref, dk_ref, dv_ref, dg_ref):
        dg = None
        for c in reversed(range(s // tq)):
            rows, n_keys = pl.ds(c * tq, tq), (c + 1) * tq
            qv, kv_, vv = q_ref[rows, :], k_ref[0:n_keys, :], v_ref[0:n_keys, :]
            do, dgc = _rms_bwd(o_ref[rows, :], g_ref[...], dy_ref[rows, :])
            do = do.astype(BF16)
            dg = _sublane_sum(dgc) if dg is None else dg + _sublane_sum(dgc)
            e, inv = _softmax_parts(qv, kv_)
            p = e * inv
            dp = lax.dot_general(do, vv, (NT, ((), ())), preferred_element_type=F32)
            ds = (p * (dp - jnp.sum(p * dp, axis=-1, keepdims=True)) * (QK ** -0.5)).astype(BF16)
            dq_ref[rows, :] = jnp.dot(ds, kv_, preferred_element_type=F32)
            dk = lax.dot_general(ds, qv, (TN, ((), ())), preferred_element_type=F32)
            dv = lax.dot_general(p.astype(BF16), do, (TN, ((), ())), preferred_element_type=F32)
            if n_keys == s:
                dk_ref[...] = dk
                dv_ref[...] = dv
            else:
                dk_ref[0:n_keys, :] += dk
                dv_ref[0:n_keys, :] += dv
        dg_ref[...] = dg

    c0 = col0 // HEAD
    head = lambda w: pl.BlockSpec((None, s, w), lambda h, *_: (h, 0, 0))
    in_specs = [head(QK), head(QK), head(HEAD), head(HEAD), pl.BlockSpec((s, HEAD), lambda h, *_: (0, c0 + h)),
                pl.BlockSpec((1, HEAD), lambda h, *_: (0, h))]
    out_specs = [head(QK), head(QK), head(HEAD), pl.BlockSpec((SUBLANE, HEAD), lambda h, *_: (0, h))]
    out_shape = [jax.ShapeDtypeStruct((n_heads, s, QK), F32), jax.ShapeDtypeStruct((n_heads, s, QK), F32),
                 jax.ShapeDtypeStruct((n_heads, s, HEAD), F32), jax.ShapeDtypeStruct((SUBLANE, n_heads * HEAD), F32)]
    return _call(body, name="attn_bwd", grid=(n_heads,), after=after, in_specs=in_specs, out_specs=out_specs,
                 out_shape=out_shape, compiler_params=_params("parallel"))(q, k, v, o, d_mix, g)


TILE_M = 1024
TILE_N = 1024


def _up_fwd(h2, w_up):
    s, d = h2.shape
    nb, _, fb = w_up.shape
    tm = min(TILE_M,s)

    def epilogue(acc):
        r = jnp.maximum(acc, 0.0)
        return r * r, r

    blk = pl.BlockSpec((tm, fb), lambda i, j: (i, j))
    return _matmul("up_fwd", h2, w_up, grid=(s // tm, nb),
                   a_spec=pl.BlockSpec((tm, d), lambda i, j: (i, 0)),
                   b_spec=pl.BlockSpec((None, d, fb), lambda i, j: (j, 0, 0)),
                   out_shape=[jax.ShapeDtypeStruct((s, nb * fb), BF16)] * 2, out_specs=[blk, blk],
                   contract=NN, epilogue=epilogue)


def _down_fwd(a, w_down):
    s, f = a.shape
    d = w_down.shape[1]
    tm, tn, tk = min(TILE_M,s), min(TILE_N,d), 2048
    nk = f // tk
    return _matmul("down_fwd", a, w_down, grid=(s // tm, d // tn, nk),
                   a_spec=pl.BlockSpec((tm, tk), lambda i, j, k: (i, k)),
                   b_spec=pl.BlockSpec((tk, tn), lambda i, j, k: (k, j)),
                   out_shape=jax.ShapeDtypeStruct((s, d), F32),
                   out_specs=pl.BlockSpec((tm, tn), lambda i, j, k: (i, j)),
                   contract=NN, nk=nk, acc_shape=(tm, tn))


def _down_bwd_act(d_m, w_down, r, after=()):
    s, d = d_m.shape
    f = w_down.shape[0]
    tm, tn = min(TILE_M,s), min(TILE_N,f)
    blk = pl.BlockSpec((tm, tn), lambda i, j: (i, j))
    return _matmul("down_bwd_act", d_m, w_down, grid=(s // tm, f // tn), after=after,
                   a_spec=pl.BlockSpec((tm, d), lambda i, j: (i, 0)),
                   b_spec=pl.BlockSpec((tn, d), lambda i, j: (j, 0)),
                   out_shape=jax.ShapeDtypeStruct((s, f), BF16), out_specs=blk, contract=NT,
                   extras=(r,), extra_specs=(blk,),
                   epilogue=lambda acc, rv: (acc * (2.0 * rv.astype(F32)),))


def _up_bwd_act(d_up, w_up, after=()):
    s, _ = d_up.shape
    nb, d, fb = w_up.shape
    tm, tn = min(TILE_M, s), min(TILE_N,d)
    pair = 2
    n_after = len(after)

    def body(a_ref, w_ref, *rest):
        o_ref, acc_ref = rest[n_after:]
        k = pl.program_id(2)
        p = None
        for t in range(pair):
            term = lax.dot_general(a_ref[:, t * fb:(t + 1) * fb], w_ref[t], (NT, ((), ())), preferred_element_type=F32)
            p = term if p is None else p + term
        _accumulate(acc_ref, p, k)

        @pl.when(k == nb // pair - 1)
        def _():
            o_ref[...] = acc_ref[...]

    return pl.pallas_call(
        body, name="up_bwd_act", grid=(s // tm, d // tn, nb // pair),
        in_specs=[pl.BlockSpec((tm, pair * fb), lambda i, j, k: (i, k)),
                  pl.BlockSpec((pair, tn, fb), lambda i, j, k: (k, j, 0))] + [ANY] * n_after,
        out_specs=pl.BlockSpec((tm, tn), lambda i, j, k: (i, j)),
        out_shape=jax.ShapeDtypeStruct((s, d), F32),
        scratch_shapes=[pltpu.VMEM((tm, tn), F32)],
        compiler_params=_params("parallel", "parallel", "arbitrary"),
    )(d_up, w_up, *after)


def _half_grad(name, a, b, core, home, received, after, *, grid, a_block, a_map, b_block, b_map, o_block, o_map, out_shape):
    n_after = len(after)
    pick = (lambda ref: ref[0]) if home else (lambda ref: 1 - ref[0])

    def body(core_ref, a_ref, b_ref, *rest):
        acc = lax.dot_general(a_ref[...], b_ref[...], (TN, ((), ())), preferred_element_type=F32)
        if received is not None:
            acc = acc + rest[0][...].astype(F32)
        rest[-1][...] = acc.astype(rest[-1].dtype)

    wrap = lambda fn: (lambda i, j, core_ref: fn(i, j, pick(core_ref)))
    o_spec = pl.BlockSpec(o_block, wrap(o_map))
    extra = [] if received is None else [o_spec]
    operands = [] if received is None else [received]
    return pl.pallas_call(
        body, name=name,
        grid_spec=pltpu.PrefetchScalarGridSpec(
            num_scalar_prefetch=1, grid=grid,
            in_specs=[pl.BlockSpec(a_block, wrap(a_map)), pl.BlockSpec(b_block, wrap(b_map))] + extra + [ANY] * n_after,
            out_specs=o_spec),
        out_shape=out_shape,
        compiler_params=_params("parallel", "parallel"),
    )(core, a, b, *operands, *after)


def _down_half_grad(name, a, d_m, core, home, received=None, after=()):
    s, f = a.shape
    d = d_m.shape[1]
    r = f // N_DEV
    tn = min(TILE_N, d)
    return _half_grad(name, a, d_m, core, home, received, after, grid=(N_CHIP, d // tn),
                      a_block=(s, r), a_map=lambda k, j, p: (0, 2 * k + p),
                      b_block=(s, tn), b_map=lambda k, j, p: (0, j),
                      o_block=(None, r, tn), o_map=lambda k, j, p: (k, 0, j),
                      out_shape=jax.ShapeDtypeStruct((N_CHIP, r, d), BF16))


def _up_half_grad(name, h2, d_up, core, home, received=None, after=()):
    s, d = h2.shape
    fb = d_up.shape[1] // N_DEV
    tm = min(TILE_M, d)
    return _half_grad(name, h2, d_up, core, home, received, after, grid=(d // tm, N_CHIP),
                      a_block=(s, tm), a_map=lambda i, k, p: (0, i),
                      b_block=(s, fb), b_map=lambda i, k, p: (0, 2 * k + p),
                      o_block=(None, tm, fb), o_map=lambda i, k, p: (k, i, 0),
                      out_shape=jax.ShapeDtypeStruct((N_CHIP, d, fb), BF16))


MXU_WIDTH = 256


def _in_pad(in_width):
    return -(-in_width // MXU_WIDTH) * MXU_WIDTH


def _join_col_shards(name, blocks, own, device, pieces=None):
    n, r, w = blocks.shape
    rows = min(ROWS, r)
    pieces = pieces or [(j, 0, w) for j in range(n)]
    used = sum(b - a for _, a, b in pieces)
    width = _in_pad(used)

    def body(dev_ref, x_ref, own_ref, o_ref):
        block = lambda j: jnp.where(dev_ref[0] == j, own_ref[...], x_ref[j])
        cols = [block(j)[:, a:b] for j, a, b in pieces]
        tail = [jnp.zeros((rows, width - used), o_ref.dtype)] if width > used else []
        o_ref[...] = jnp.concatenate(cols + tail, axis=1)

    return pl.pallas_call(
        body, name=name,
        grid_spec=pltpu.PrefetchScalarGridSpec(
            num_scalar_prefetch=1, grid=(r // rows,),
            in_specs=[pl.BlockSpec((n, rows, w), lambda i, dev: (0, i, 0)), pl.BlockSpec((rows, w), lambda i, dev: (i, 0))],
            out_specs=pl.BlockSpec((rows, width), lambda i, dev: (i, 0))),
        out_shape=jax.ShapeDtypeStruct((r, width), blocks.dtype),
        compiler_params=_params("parallel"),
    )(device, blocks, own)


def _permute_q_cols(w_uq, n_heads):
    r = w_uq.shape[0]
    w3 = w_uq.reshape(r, n_heads, QK)
    return jnp.concatenate([w3[:, :, :HEAD].reshape(r, n_heads * HEAD), w3[:, :, HEAD:].reshape(r, n_heads * ROPE)], axis=1)


def _unpermute_q_rows(wt, n_heads):
    r = wt.shape[1]
    nope = wt[:n_heads * HEAD].reshape(n_heads, HEAD, r)
    rope = wt[n_heads * HEAD:].reshape(n_heads, ROPE, r)
    return jnp.concatenate([nope, rope], axis=1).reshape(n_heads * QK, r)


def _local_step(x, tgt, gains, weights, grads, first_after=()):
    pre_mix_g, q_norm_g, kv_norm_g, conv_out_g, attn_out_g, post_mix_g, pre_mlp_g, post_mlp_g = gains
    s, d = x.shape
    conv_width = conv_out_g.shape[1]
    n_groups = conv_width // HEAD
    r_q, r_kv = q_norm_g.shape[1], kv_norm_g.shape[1]
    n_heads = attn_out_g.shape[1] // HEAD
    c_q0 = 3 * conv_width
    c_kv0 = c_q0 + r_q
    c_kr0 = c_kv0 + r_kv
    in_pad = _in_pad(c_kr0 + ROPE)
    tn_in = _fit(in_pad, 6 * MXU_WIDTH)
    tables = _rope_tables(s, n_heads)

    h1 = _rms_fwd("pre_mix_norm", x, pre_mix_g, after=first_after)
    weights.forward(0, (h1,))
    weights.relay(0, tables)
    w_in_p, conv_w = weights.ready(0, ())
    proj = _mm_nn("in_proj", h1, w_in_p, F32, TILE_M, tn_in)
    y_conv = _conv_fwd(proj, conv_w, conv_out_g, n_groups, conv_width + n_heads * HEAD, after=weights.forward(1, (proj,)))
    w_uq_p, w_ukv, w_o = weights.ready(1, (y_conv,))
    qn, q = _norm_up("q_up", proj, (c_q0, r_q), q_norm_g, w_uq_p)
    kvn, kv = _norm_up("kv_up", proj, (c_kv0, r_kv), kv_norm_g, w_ukv)
    qh, kh, vh = _pack_heads(q, kv, proj, c_kr0, tables, n_heads, after=weights.forward(2, (q, kv)))
    o, mix = _attn_fwd(qh, kh, vh, attn_out_g, y_conv, conv_width)
    y = _mm_nn("out_proj", mix, w_o, F32, TILE_M, TILE_N)
    x2, h2 = _mid_fwd(x, y, post_mix_g, pre_mlp_g, after=weights.forward(3, (y,)))
    weights.relay(2, (h2,))
    (w_up,) = weights.ready(2, ())
    a, r = _up_fwd(h2, w_up)
    weights.relay(3, (a,))
    (w_down,) = weights.ready(3, ())
    m = _down_fwd(a, w_down)

    d_out, d_m, dg_post_mlp, loss_part = _head(m, x2, tgt, post_mlp_g)
    core = grads.core
    away = _down_half_grad("down_bwd_w_away", a, d_m, core, home=False)
    d_up = _down_bwd_act(d_m, w_down, r, after=grads.send_away(0, away))
    sums = _down_half_grad("down_bwd_w_home", a, d_m, core, home=True, received=grads.received(0, (d_up,)))
    away = _up_half_grad("up_bwd_w_away", h2, d_up, core, home=False, after=grads.send_sums(0, (sums,)))
    d_h2 = _up_bwd_act(d_up, w_up, after=grads.send_away(1, away))
    sums = _up_half_grad("up_bwd_w_home", h2, d_up, core, home=True, received=grads.received(1, (d_h2,)))
    d_x2, d_y, dg_pre_mlp, dg_post_mix = _mid_bwd(x2, y, d_out, d_h2, pre_mlp_g, post_mix_g, after=grads.send_sums(1, (sums,)))
    d_mix = _mm_nt("out_proj_bwd_act", d_y, w_o, F32, TILE_M, TILE_N)
    gw_o = _mm_tn("out_proj_bwd_w", mix, d_y, BF16, TILE_M, TILE_N)
    dqh, dkh, dvh, dg_attn = _attn_bwd(qh, kh, vh, o, d_mix, attn_out_g, conv_width)
    d_q, d_kv, d_kr = _unpack_heads(dqh, dkh, dvh, tables, n_heads)
    gw_uq_t = _mm_tn("q_up_bwd_w", d_q, qn, F32, TILE_M, TILE_N)
    gw_ukv = _mm_tn("kv_up_bwd_w", kvn, d_kv, BF16, TILE_M, TILE_N)
    d_cq, dg_q = _up_norm_bwd("q_up_bwd_act", d_q, w_uq_p, proj, (c_q0, r_q), q_norm_g, after=grads.full(2, (gw_o, gw_uq_t, gw_ukv)))
    d_ckv, dg_kv = _up_norm_bwd("kv_up_bwd_act", d_kv, w_ukv, proj, (c_kv0, r_kv), kv_norm_g)
    d_u, d_b, d_c, dg_conv, dw_conv = _conv_bwd(proj, d_mix, conv_w, conv_out_g, n_groups)
    d_proj = jnp.concatenate([d_u, d_b, d_c, d_cq, d_ckv, d_kr, jnp.zeros((s, in_pad - c_kr0 - LANE), BF16)], axis=1)
    gw_in_t = _mm_tn("in_proj_bwd_w", d_proj, h1, F32, tn_in, TILE_N)
    updated = grads.update_now(0, grads.send_away(3, gw_in_t))
    d_h1 = _mm_nt("in_proj_bwd_act", d_proj, w_in_p, F32, TILE_M, TILE_N, after=grads.full(3, (gw_in_t,), received=updated))
    grad_x, dg_pre_mix = _first_bwd(x, pre_mix_g, d_h1, d_x2)

    small = [dg_pre_mix, dg_q, dg_kv, dg_conv, dg_attn, dg_post_mix, dg_pre_mlp, dg_post_mlp,
             dw_conv[0], dw_conv[1], dw_conv[2], loss_part]
    return grad_x, jnp.concatenate(small, axis=1)


HBM = pl.BlockSpec(memory_space=pltpu.HBM)
SEM = pl.BlockSpec(memory_space=pltpu.SEMAPHORE)
IN_VMEM = pl.BlockSpec(memory_space=pltpu.VMEM)
SPLIT = pltpu.CompilerParams(has_side_effects=pltpu.SideEffectType.DATAFLOW_SIDE_EFFECTING)


def _in_hbm(a):
    return pltpu.with_memory_space_constraint(a, pltpu.HBM)


def _hbm_like(a):
    return pltpu.HBM(a.shape, a.dtype)


def _place():
    x, y, c = lax.axis_index("x"), lax.axis_index("y"), lax.axis_index("c")
    other_chips = [(1 - x, y), (x, 1 - y), (1 - x, 1 - y)]
    return x, y, c, other_chips


def _block(px, py, pc):
    return 4 * px + 2 * py + pc


def _await(block, sem):
    pltpu.make_async_copy(block, block, sem).wait()


def _relay_route(x, y, c):
    came_from = ((1 - x) * (1 - c) + x * c, y * (1 - c) + (1 - y) * c)
    goes_to = (x * (1 - c) + (1 - x) * c, (1 - y) * (1 - c) + y * c)
    return came_from, goes_to


def _gather_start(name, shards, groups, relayed=(), after=()):
    n, ng = len(shards), len(groups)
    lands = [lax.empty((N_DEV, *a.shape), a.dtype) for a in shards]

    def body(*refs):
        src, land = refs[:n], refs[n:2 * n]
        sems, token = refs[2 * n + len(after):2 * n + len(after) + 2 * ng], refs[-1]
        x, y, c, chips = _place()
        targets = [(x, y, 1 - c)] + [(*chip, c) for chip in chips]
        for gi, group in enumerate(groups):
            for i, w in enumerate(group):
                for k, to in enumerate(targets[:3] if gi in relayed else targets):
                    pltpu.make_async_remote_copy(
                        src_ref=src[w], dst_ref=land[w].at[_block(x, y, c)],
                        send_sem=sems[2 * gi].at[4 * i + k], recv_sem=sems[2 * gi + 1].at[4 * i + k],
                        device_id=to, device_id_type=MESH).start()
        token[...] = jnp.zeros_like(token)

    sem_shapes = [pltpu.SemaphoreType.DMA((4 * len(g),)) for g in groups for _ in range(2)]
    out = pl.pallas_call(
        body, name=name,
        in_specs=[HBM] * (2 * n) + [ANY] * len(after),
        out_specs=[SEM] * (2 * ng) + [HBM] * (2 * n) + [IN_VMEM],
        out_shape=sem_shapes + [_hbm_like(a) for a in shards] + [_hbm_like(a) for a in lands]
        + [jax.ShapeDtypeStruct((SUBLANE, LANE), F32)],
        input_output_aliases={i: 2 * ng + i for i in range(2 * n)},
        compiler_params=SPLIT,
    )(*[_in_hbm(a) for a in shards], *[_in_hbm(a) for a in lands], *after)
    sems = [(out[2 * gi], out[2 * gi + 1]) for gi in range(ng)]
    return sems, out[2 * ng:2 * ng + n], out[2 * ng + n:2 * ng + 2 * n], out[-1]


def _gather_forward(name, shards, lands, send1, recv1, after, relayed=False):
    n = len(lands)

    def body(*refs):
        src, land = refs[:n], refs[n:2 * n]
        s1, r1 = refs[2 * n], refs[2 * n + 1]
        s2, r2 = refs[2 * n + 2 + len(after)], refs[2 * n + 3 + len(after)]
        x, y, c, chips = _place()
        me, sibling = (x, y, c), (x, y, 1 - c)
        for j, chip in enumerate(chips[:2] if relayed else chips):
            for i in range(n):
                blk = land[i].at[_block(*chip, c)]
                pltpu.make_async_remote_copy(src_ref=blk, dst_ref=blk, send_sem=s1.at[4 * i + 1 + j], recv_sem=r1.at[4 * i + 1 + j],
                                             device_id=me, device_id_type=MESH).wait_recv()
                pltpu.make_async_remote_copy(src_ref=blk, dst_ref=blk, send_sem=s2.at[3 * i + j], recv_sem=r2.at[3 * i + j],
                                             device_id=sibling, device_id_type=MESH).start()
        if relayed:
            came_from, goes_to = _relay_route(x, y, c)
            for i in range(n):
                blk = land[i].at[_block(*came_from, c)]
                pltpu.make_async_remote_copy(src_ref=blk, dst_ref=blk, send_sem=s2.at[3 * i + 2], recv_sem=r2.at[3 * i + 2],
                                             device_id=(*goes_to, c), device_id_type=MESH).start()
        for i in range(n):
            blk = land[i].at[_block(x, y, 1 - c)]
            pltpu.make_async_remote_copy(src_ref=blk, dst_ref=blk, send_sem=s1.at[4 * i], recv_sem=r1.at[4 * i],
                                         device_id=me, device_id_type=MESH).wait_recv()
            for k in range(3 if relayed else 4):
                pltpu.make_async_remote_copy(src_ref=src[i], dst_ref=land[i].at[_block(x, y, c)], send_sem=s1.at[4 * i + k],
                                             recv_sem=r1.at[4 * i + k], device_id=sibling, device_id_type=MESH).wait_send()

    sem = pltpu.SemaphoreType.DMA((3 * n,))
    out = pl.pallas_call(
        body, name=name,
        in_specs=[HBM] * (2 * n) + [SEM, SEM] + [ANY] * len(after),
        out_specs=[SEM, SEM] + [HBM] * n,
        out_shape=[sem, sem] + [_hbm_like(a) for a in lands],
        input_output_aliases={n + i: 2 + i for i in range(n)},
        compiler_params=SPLIT,
    )(*shards, *lands, send1, recv1, *after)
    return (out[0], out[1]), out[2:]


def _gather_relay_forward(name, lands, send2, recv2, after):
    n = len(lands)

    def body(*refs):
        land, s2, r2 = refs[:n], refs[n], refs[n + 1]
        s3, r3 = refs[n + 2 + len(after)], refs[n + 3 + len(after)]
        x, y, c, _ = _place()
        me, sibling = (x, y, c), (x, y, 1 - c)
        came_from, _ = _relay_route(x, y, c)
        for i in range(n):
            blk = land[i].at[_block(1 - x, 1 - y, c)]
            pltpu.make_async_remote_copy(src_ref=blk, dst_ref=blk, send_sem=s2.at[3 * i + 2], recv_sem=r2.at[3 * i + 2],
                                         device_id=me, device_id_type=MESH).wait_recv()
            pltpu.make_async_remote_copy(src_ref=blk, dst_ref=blk, send_sem=s3.at[i], recv_sem=r3.at[i],
                                         device_id=sibling, device_id_type=MESH).start()
            sent = land[i].at[_block(*came_from, c)]
            pltpu.make_async_remote_copy(src_ref=sent, dst_ref=sent, send_sem=s2.at[3 * i + 2], recv_sem=r2.at[3 * i + 2],
                                         device_id=me, device_id_type=MESH).wait_send()

    sem = pltpu.SemaphoreType.DMA((n,))
    out = pl.pallas_call(
        body, name=name,
        in_specs=[HBM] * n + [SEM, SEM] + [ANY] * len(after),
        out_specs=[SEM, SEM] + [HBM] * n,
        out_shape=[sem, sem] + [_hbm_like(a) for a in lands],
        input_output_aliases={i: 2 + i for i in range(n)},
        compiler_params=SPLIT,
    )(*lands, send2, recv2, *after)
    return (out[0], out[1]), out[2:]


def _gather_wait(name, lands, send2, recv2, after, relay_sems=None):
    n = len(lands)
    n_sems = 2 if relay_sems is None else 4

    def body(*refs):
        land, s2, r2 = refs[:n], refs[n], refs[n + 1]
        for i in range(n):
            for j in range(3 if relay_sems is None else 2):
                _await(land[i].at[0], r2.at[3 * i + j])
                _await(land[i].at[0], s2.at[3 * i + j])
            if relay_sems is not None:
                _await(land[i].at[0], refs[n + 3].at[i])
                _await(land[i].at[0], refs[n + 2].at[i])

    return pl.pallas_call(
        body, name=name,
        in_specs=[HBM] * n + [SEM] * n_sems + [ANY] * len(after), out_specs=[HBM] * n, out_shape=[_hbm_like(a) for a in lands],
        input_output_aliases={i: i for i in range(n)},
        compiler_params=SPLIT,
    )(*lands, send2, recv2, *(relay_sems or ()), *after)


def _pair_exchange(name, grads, shard_rows):
    n = len(grads)
    shapes = [(g.shape[1:] if r is None else (r, g.shape[1])) for g, r in zip(grads, shard_rows)]

    def body(*refs):
        ins, recv = refs[:n], refs[n:2 * n]
        send_sems, recv_sems = refs[2 * n:]
        x, y, c, _ = _place()
        sends = []
        for w in range(n):
            for k in range(N_CHIP):
                j, r = 2 * k + 1 - c, shard_rows[w]
                src = ins[w].at[j] if r is None else ins[w].at[pl.ds(pl.multiple_of(j * r, SUBLANE), r), :]
                sends.append(pltpu.make_async_remote_copy(
                    src_ref=src, dst_ref=recv[w].at[k],
                    send_sem=send_sems.at[w, k], recv_sem=recv_sems.at[w, k],
                    device_id=(x, y, 1 - c), device_id_type=MESH))
        for cp in sends:
            cp.start()
        for cp in sends:
            cp.wait()

    return pl.pallas_call(
        body, name=name,
        in_specs=[ANY] * n, out_specs=[ANY] * n,
        out_shape=[jax.ShapeDtypeStruct((N_CHIP, *shape), g.dtype) for g, shape in zip(grads, shapes)],
        scratch_shapes=[pltpu.SemaphoreType.DMA((n, N_CHIP))] * 2,
    )(*grads)


def _pair_sum_rows(name, grad, received, core):
    _, r, c = received.shape
    tc = _fit(c, 512)

    def body(core_ref, a_ref, b_ref, o_ref):
        o_ref[...] = (a_ref[...] + b_ref[...]).astype(o_ref.dtype)

    spec = pl.BlockSpec((None, r, tc), lambda k, i, core_ref: (k, 0, i))
    return pl.pallas_call(
        body, name=name,
        grid_spec=pltpu.PrefetchScalarGridSpec(
            num_scalar_prefetch=1, grid=(N_CHIP, c // tc),
            in_specs=[pl.BlockSpec((r, tc), lambda k, i, core_ref: (2 * k + core_ref[0], i)), spec],
            out_specs=spec),
        out_shape=jax.ShapeDtypeStruct(received.shape, BF16),
        compiler_params=_params("parallel", "parallel"),
    )(core, grad, received)


def _pair_sum(name, grad, received, core):
    _, r, c = received.shape
    rows = min(ROWS, r)
    assert r % rows == 0

    def body(core_ref, a_ref, b_ref, o_ref):
        o_ref[...] = (a_ref[...].astype(F32) + b_ref[...].astype(F32)).astype(o_ref.dtype)

    spec = pl.BlockSpec((None, rows, c), lambda k, i, core_ref: (k, i, 0))
    return pl.pallas_call(
        body, name=name,
        grid_spec=pltpu.PrefetchScalarGridSpec(
            num_scalar_prefetch=1, grid=(N_CHIP, r // rows),
            in_specs=[pl.BlockSpec((None, None, rows, c), lambda k, i, core_ref: (k, core_ref[0], i, 0)), spec],
            out_specs=spec),
        out_shape=jax.ShapeDtypeStruct(received.shape, received.dtype),
        compiler_params=_params("parallel", "parallel"),
    )(core, grad.reshape(N_CHIP, 2, r, c), received)


def _away_shard(src, k, c, shard_rows):
    if shard_rows is None:
        return src.at[k]
    return src.at[pl.ds(pl.multiple_of((2 * k + 1 - c) * shard_rows, SUBLANE), shard_rows), :]


def _pair_send_start(name, away, shard_rows=None):
    shape = away.shape if shard_rows is None else (N_CHIP, shard_rows, away.shape[1])
    land = lax.empty(shape, away.dtype)

    def body(src, dst, send, recv, src_thru, dst_thru, token):
        x, y, c, _ = _place()
        for k in range(N_CHIP):
            pltpu.make_async_remote_copy(src_ref=_away_shard(src, k, c, shard_rows), dst_ref=dst.at[k], send_sem=send.at[k],
                                         recv_sem=recv.at[k], device_id=(x, y, 1 - c), device_id_type=MESH).start()
        token[...] = jnp.zeros_like(token)

    sem = pltpu.SemaphoreType.DMA((N_CHIP,))
    out = pl.pallas_call(
        body, name=name,
        in_specs=[HBM, HBM], out_specs=[SEM, SEM, HBM, HBM, IN_VMEM],
        out_shape=[sem, sem, _hbm_like(away), _hbm_like(land), jax.ShapeDtypeStruct((SUBLANE, LANE), F32)],
        input_output_aliases={0: 2, 1: 3},
        compiler_params=SPLIT,
    )(_in_hbm(away), _in_hbm(land))
    return (out[0], out[1]), out[2], out[3], out[4]


def _pair_send_wait(name, sems, src, land, after, shard_rows=None):
    def body(src_ref, dst_ref, send, recv, *rest):
        for k in range(N_CHIP):
            _await(dst_ref.at[k], send.at[k])
            _await(dst_ref.at[k], recv.at[k])

    return pl.pallas_call(
        body, name=name,
        in_specs=[HBM, HBM, SEM, SEM] + [ANY] * len(after), out_specs=HBM, out_shape=_hbm_like(land),
        input_output_aliases={1: 0},
        compiler_params=SPLIT,
    )(src, land, *sems, *after)


def _chip_send_start(name, sums):
    n = len(sums)
    lands = [lax.empty(a.shape, a.dtype) for a in sums]

    def body(*refs):
        src, land = refs[:n], refs[n:2 * n]
        send, recv, token = refs[2 * n], refs[2 * n + 1], refs[-1]
        x, y, c, chips = _place()
        for w in range(n):
            for j, (px, py) in enumerate(chips):
                pltpu.make_async_remote_copy(
                    src_ref=src[w].at[2 * px + py], dst_ref=land[w].at[2 * x + y],
                    send_sem=send.at[3 * w + j], recv_sem=recv.at[3 * w + j],
                    device_id=(px, py, c), device_id_type=MESH).start()
        token[...] = jnp.zeros_like(token)

    sem = pltpu.SemaphoreType.DMA((3 * n,))
    out = pl.pallas_call(
        body, name=name,
        in_specs=[HBM] * (2 * n),
        out_specs=[SEM, SEM] + [HBM] * (2 * n) + [IN_VMEM],
        out_shape=[sem, sem] + [_hbm_like(a) for a in sums] + [_hbm_like(a) for a in lands]
        + [jax.ShapeDtypeStruct((SUBLANE, LANE), F32)],
        input_output_aliases={i: 2 + i for i in range(2 * n)},
        compiler_params=SPLIT,
    )(*[_in_hbm(a) for a in sums], *[_in_hbm(a) for a in lands])
    return (out[0], out[1]), out[2:2 + n], out[2 + n:2 + 2 * n], out[-1]


def _chip_send_wait(name, groups, after):
    counts = [len(g[1]) for g in groups]
    n = sum(counts)

    def body(*refs):
        land = refs[n:2 * n]
        sems = refs[2 * n:2 * n + 2 * len(groups)]
        w = 0
        for gi, count in enumerate(counts):
            for i in range(count):
                for j in range(3):
                    _await(land[w].at[0], sems[2 * gi].at[3 * i + j])
                    _await(land[w].at[0], sems[2 * gi + 1].at[3 * i + j])
                w += 1

    sums = [a for g in groups for a in g[1]]
    lands = [a for g in groups for a in g[2]]
    sems = [s for g in groups for s in g[0]]
    return pl.pallas_call(
        body, name=name,
        in_specs=[HBM] * (2 * n) + [SEM] * len(sems) + [ANY] * len(after),
        out_specs=[HBM] * n, out_shape=[_hbm_like(a) for a in lands],
        input_output_aliases={n + i: i for i in range(n)},
        compiler_params=SPLIT,
    )(*sums, *lands, *sems, *after)


def _small_all_reduce(part, after=()):
    _, w = part.shape

    def body(p_ref, *rest):
        o_ref, buf, send_sems, recv_sems = rest[len(after):]
        x, y, c, _ = _place()
        me = 4 * x + 2 * y + c
        buf[me] = jnp.sum(p_ref[...], axis=0, keepdims=True)
        copies = []
        for k in range(1, N_DEV):
            dx, dy, dc = (k >> 2) & 1, (k >> 1) & 1, k & 1
            copies.append(pltpu.make_async_remote_copy(
                src_ref=buf.at[me], dst_ref=buf.at[me], send_sem=send_sems.at[k - 1], recv_sem=recv_sems.at[k - 1],
                device_id=(x ^ dx, y ^ dy, c ^ dc), device_id_type=MESH))
        for cp in copies:
            cp.start()
        for cp in copies:
            cp.wait()
        tot = buf[0]
        for d in range(1, N_DEV):
            tot = tot + buf[d]
        o_ref[...] = tot
        loss = jnp.sum(tot[:, w - LANE:], axis=1, keepdims=True)
        o_ref[:, w - LANE:] = jnp.broadcast_to(loss, (1, LANE))

    return pl.pallas_call(
        body, name="small_all_reduce",
        in_specs=[IN_VMEM] + [ANY] * len(after), out_specs=IN_VMEM,
        out_shape=jax.ShapeDtypeStruct((1, w), F32),
        scratch_shapes=[pltpu.VMEM((N_DEV, 1, w), F32), pltpu.SemaphoreType.DMA((N_DEV - 1,)), pltpu.SemaphoreType.DMA((N_DEV - 1,))],
        compiler_params=pltpu.CompilerParams(vmem_limit_bytes=VMEM_LIMIT_BYTES),
    )(part, *after)


def _adamw(w, g, m, v):
    m = ADAM_B1 * m + (1.0 - ADAM_B1) * g
    v = ADAM_B2 * v + (1.0 - ADAM_B2) * (g * g)
    m_hat = m / (1.0 - ADAM_B1 ** ADAM_STEP)
    v_hat = v / (1.0 - ADAM_B2 ** ADAM_STEP)
    delta = -ADAM_LR * (m_hat / (jnp.sqrt(v_hat) + ADAM_EPS) + ADAM_WD * w)
    return delta, m, v


def _sum_adam_block(chip_ref, p_ref, own_ref, w_ref, m_ref, v_ref, g_ref, d_ref, mo_ref, vo_ref):
    g = None
    for k in range(N_CHIP):
        term = jnp.where(chip_ref[0] == k, own_ref[...], p_ref[k]).astype(F32)
        g = term if g is None else g + term
    g_ref[...] = g
    d_ref[...], mo_ref[...], vo_ref[...] = _adamw(w_ref[...], g, m_ref[...], v_ref[...])


def _sum_adam(name, parts, sums, chip, w, m, v, after=()):
    _, r, c = w.shape
    n_after = len(after)
    by_rows = r % ROWS == 0 or r < ROWS
    tr, tc = (min(ROWS, r), c) if by_rows else (r, _fit(c, 512))
    at = (lambda i: (i, 0)) if by_rows else (lambda i: (0, i))

    def body(chip_ref, p_ref, own_ref, w_ref, m_ref, v_ref, *rest):
        _sum_adam_block(chip_ref, p_ref, own_ref, w_ref, m_ref, v_ref, *rest[n_after:])

    blk = pl.BlockSpec((None, tr, tc), lambda i, chip_ref: (0, *at(i)))
    out = jax.ShapeDtypeStruct((1, r, c), F32)
    return pl.pallas_call(
        body, name=name,
        grid_spec=pltpu.PrefetchScalarGridSpec(
            num_scalar_prefetch=1, grid=(r // tr if by_rows else c // tc,),
            in_specs=[pl.BlockSpec((N_CHIP, tr, tc), lambda i, chip_ref: (0, *at(i))),
                      pl.BlockSpec((None, tr, tc), lambda i, chip_ref: (chip_ref[0], *at(i))), blk, blk, blk]
            + [ANY] * n_after,
            out_specs=[blk] * 4),
        out_shape=[out] * 4,
        compiler_params=_params("parallel"),
    )(chip, parts, sums, w, m, v, *after)


def _adam_gains(total, ws, ms, vs):
    n = len(ws)
    widths = [w.shape[1] for w in ws]

    def body(t_ref, *refs):
        w_refs, m_refs, v_refs, outs = refs[:n], refs[n:2 * n], refs[2 * n:3 * n], refs[3 * n:]
        off = 0
        for i in range(n):
            g = t_ref[:, off:off + widths[i]]
            off += widths[i]
            g_ref, d_ref, mo_ref, vo_ref = outs[4 * i:4 * i + 4]
            g_ref[...] = g
            d_ref[...], mo_ref[...], vo_ref[...] = _adamw(w_refs[i][...], g, m_refs[i][...], v_refs[i][...])

    out = pl.pallas_call(
        body, name="adam_gains",
        out_shape=[jax.ShapeDtypeStruct(w.shape, F32) for w in ws for _ in range(4)],
    )(total, *ws, *ms, *vs)
    return [tuple(out[4 * i:4 * i + 4]) for i in range(n)]


def _adam_taps(total, first_col, device, w, m, v):
    _, n_taps, cw = w.shape
    col_block = lambda t, dev: (0, first_col // cw + t * N_DEV + dev[0])
    tap = pl.BlockSpec((None, 1, cw), lambda t, dev: (t, 0, 0))

    def body(dev_ref, t_ref, w_ref, m_ref, v_ref, g_ref, d_ref, mo_ref, vo_ref):
        g = t_ref[...]
        g_ref[...] = g
        d_ref[...], mo_ref[...], vo_ref[...] = _adamw(w_ref[...], g, m_ref[...], v_ref[...])

    shape3 = (n_taps, 1, cw)
    out = pl.pallas_call(
        body, name="adam_taps",
        grid_spec=pltpu.PrefetchScalarGridSpec(
            num_scalar_prefetch=1, grid=(n_taps,),
            in_specs=[pl.BlockSpec((1, cw), col_block), tap, tap, tap], out_specs=[tap] * 4),
        out_shape=[jax.ShapeDtypeStruct(shape3, F32)] * 4,
    )(device, total, w.reshape(shape3), m.reshape(shape3), v.reshape(shape3))
    return tuple(o.reshape(w.shape) for o in out)


def kernel(x, pre_mix_g, w_in, conv_w, q_norm_g, w_uq, kv_norm_g, w_ukv, conv_out_g, attn_out_g, w_o, post_mix_g, pre_mlp_g, w_up, w_down, post_mlp_g, loss_target, m_pre_mix_g, m_w_in, m_conv_w, m_q_norm_g, m_w_uq, m_kv_norm_g, m_w_ukv, m_conv_out_g, m_attn_out_g, m_w_o, m_post_mix_g, m_pre_mlp_g, m_w_up, m_w_down, m_post_mlp_g, v_pre_mix_g, v_w_in, v_conv_w, v_q_norm_g, v_w_uq, v_kv_norm_g, v_w_ukv, v_conv_out_g, v_attn_out_g, v_w_o, v_post_mix_g, v_pre_mlp_g, v_w_up, v_w_down, v_post_mlp_g):
    me = 4 * lax.axis_index("x") + 2 * lax.axis_index("y") + lax.axis_index("c")
    core = lax.axis_index("c").astype(jnp.int32).reshape(1)
    chip = (2 * lax.axis_index("x") + lax.axis_index("y")).astype(jnp.int32).reshape(1)
    gains = (pre_mix_g, q_norm_g, kv_norm_g, conv_out_g, attn_out_g, post_mix_g, pre_mlp_g, post_mlp_g)
    gain_m = (m_pre_mix_g, m_q_norm_g, m_kv_norm_g, m_conv_out_g, m_attn_out_g, m_post_mix_g, m_pre_mlp_g, m_post_mlp_g)
    gain_v = (v_pre_mix_g, v_q_norm_g, v_kv_norm_g, v_conv_out_g, v_attn_out_g, v_post_mix_g, v_pre_mlp_g, v_post_mlp_g)
    names = ("w_in", "w_uq", "w_ukv", "w_o", "w_up", "w_down")
    big = dict(zip(names, (w_in, w_uq, w_ukv, w_o, w_up, w_down)))
    big_m = dict(zip(names, (m_w_in, m_w_uq, m_w_ukv, m_w_o, m_w_up, m_w_down)))
    big_v = dict(zip(names, (v_w_in, v_w_uq, v_w_ukv, v_w_o, v_w_up, v_w_down)))
    n_heads = attn_out_g.shape[1] // HEAD
    n_taps = conv_w.shape[1]

    gathered = ("w_in", "conv", "w_uq", "w_ukv", "w_o", "w_up", "w_down")
    gather_groups = ((0, 1), (2, 3, 4), (5,), (6,))
    taps = jnp.pad(conv_w[0], ((0, SUBLANE - n_taps), (0, 0)))
    relayed_groups = (0, 2, 3)
    sems1, shards, lands, token = _gather_start("gather_start_first", [w_in[0].astype(BF16), taps], ((0, 1),), relayed=(0,))
    sems1, shards, lands = list(sems1), list(shards), list(lands)
    behind = token[0, 0]
    rest = [(big[nm][0] + behind).astype(BF16) for nm in gathered[2:]]

    def start_more(name, some, groups, relayed, after):
        sems_b, shards_b, lands_b, started = _gather_start(name, some, groups, relayed=relayed, after=after)
        sems1.extend(sems_b)
        shards.extend(shards_b)
        lands.extend(lands_b)
        return started

    start_rest = lambda after: start_more("gather_start_rest", rest[:4], ((0, 1, 2), (3,)), (1,), after)
    start_last = lambda after: start_more("gather_start_last", rest[4:], ((0,),), (0,), after)

    cols = lambda a: jnp.concatenate([a[j] for j in range(N_DEV)], axis=1)
    rows = lambda a: a.reshape(N_DEV * a.shape[1], a.shape[2])
    device = me.astype(jnp.int32).reshape(1)
    own_in = lambda a, shard: lax.dynamic_update_index_in_dim(a, shard, me, 0)
    q_pieces = [(h, 0, HEAD) for h in range(n_heads)] + [(h, HEAD, QK) for h in range(n_heads)]
    ready = {
        "w_in": lambda a, shard: _join_col_shards("join_w_in", a, shard, device),
        "conv": lambda a, shard: cols(own_in(a, shard))[:n_taps],
        "w_uq": lambda a, shard: _join_col_shards("join_w_uq", a, shard, device, q_pieces),
        "w_ukv": lambda a, shard: cols(own_in(a, shard)),
        "w_o": lambda a, shard: rows(own_in(a, shard)),
        "w_up": own_in,
        "w_down": lambda a, shard: rows(own_in(a, shard)),
    }
    assert w_uq.shape[2] == QK

    class Weights:
        def __init__(self):
            self.passed, self.relayed = {}, {}

        def forward(self, group, after):
            idx = gather_groups[group]
            if group == 0:
                after = (*after, *rest)
            self.passed[group] = _gather_forward(f"gather_forward_{group}", [shards[i] for i in idx], [lands[i] for i in idx],
                                                 *sems1[group], after, relayed=group in relayed_groups)
            behind = tuple(self.passed[group][1])
            if group == 2:
                behind = (*behind, start_last(behind))
            return behind

        def relay(self, group, after):
            sems2, mid = self.passed[group]
            self.relayed[group], mid = _gather_relay_forward(f"gather_relay_{group}", mid, *sems2, after)
            self.passed[group] = (sems2, mid)
            if group == 0:
                start_rest(tuple(mid))
            return tuple(mid)

        def ready(self, group, after):
            sems2, mid = self.passed[group]
            full = _gather_wait(f"gather_wait_{group}", mid, *sems2, after, relay_sems=self.relayed.get(group))
            out = []
            return [ready[gathered[i]](a, shards[i]) for i, a in zip(gather_groups[group], full)]

    weights = Weights()

    col_blocks = lambda g: g.reshape(g.shape[0], N_DEV, g.shape[1] // N_DEV).transpose(1, 0, 2)
    row_blocks = lambda g: g.reshape(N_DEV, g.shape[0] // N_DEV, g.shape[1])
    grad_groups = (("w_down",), ("w_up",), ("w_o", "w_uq", "w_ukv"), ("w_in",))
    transposed = {"w_in": w_in.shape[2], "w_uq": w_uq.shape[2]}
    to_blocks = {
        "w_in": lambda g: g, "w_uq": lambda g: _unpermute_q_rows(g, n_heads),
        "w_ukv": col_blocks, "w_o": row_blocks, "w_up": lambda g: g, "w_down": row_blocks,
    }
    in_flight = []

    class Grads:
        def __init__(self):
            self.core = core
            self.away = {}

        def send_sums(self, group, sums):
            sems, sums, parts, tok = _chip_send_start(f"chip_send_start_{group}", list(sums))
            in_flight.append((sems, sums, parts))
            return (tok,)

        def full(self, group, arrays, received=None):
            nms = grad_groups[group]
            if received is None:
                blocks = [to_blocks[nm](g) for nm, g in zip(nms, arrays)]
                got = _pair_exchange(f"pair_exchange_{group}", blocks, [transposed.get(nm) for nm in nms])
            else:
                blocks, got = [self.away[group][1]], [self.received(group, received)]
            sums = [(_pair_sum_rows if nm in transposed else _pair_sum)(f"pair_sum_{nm}", g, r, core)
                    for nm, g, r in zip(nms, blocks, got)]
            return self.send_sums(group, sums)

        def send_away(self, group, half):
            nm = grad_groups[group][0]
            rows = transposed.get(nm)
            sems, src, land, tok = _pair_send_start(f"pair_send_start_{group}", half if rows is None else to_blocks[nm](half), rows)
            self.away[group] = (sems, src, land, rows)
            return (tok,)

        def received(self, group, after):
            sems, src, land, rows = self.away[group]
            return _pair_send_wait(f"pair_send_wait_{group}", sems, src, land, after, rows)

        def update_now(self, group, after):
            return update(str(group), group, group + 1, after)

    big_out = {}

    def update(tag, first, last, after):
        picked = [i for i in range(first, last) if grad_groups[i][0] not in big_out]
        groups = [in_flight[i] for i in picked]
        parts = _chip_send_wait("chip_send_wait_" + tag, groups, after)
        nms = [nm for i in picked for nm in grad_groups[i]]
        sums = [a for _, s, _ in groups for a in s]
        for nm, p, s in zip(nms, parts, sums):
            view = (lambda a: jnp.swapaxes(a, 1, 2)) if nm in transposed else (lambda a: a)
            out = _sum_adam("adam_" + nm, p, s, chip, view(big[nm]), view(big_m[nm]), view(big_v[nm]), after=after)
            after = (out[0],)
            big_out[nm] = [view(o) for o in out]
        return after

    grad_x, small = _local_step(x[0], loss_target[0], gains, weights, Grads(), first_after=(token,))

    after = update("early", 0, len(in_flight) - 1, (grad_x,))
    total = _small_all_reduce(small, after=after)
    update("late", len(in_flight) - 1, len(in_flight), (total,))
    big_out = [big_out[nm] for nm in names]

    gain_out = _adam_gains(total, gains, gain_m, gain_v)
    taps_out = _adam_taps(total, sum(g.shape[1] for g in gains), me.astype(jnp.int32).reshape(1), conv_w, m_conv_w, v_conv_w)
    loss = total[0, total.shape[1] - 1]

    order = (0, "w_in", "conv", 1, "w_uq", 2, "w_ukv", 3, 4, "w_o", 5, 6, "w_up", "w_down", 7)
    by_name = dict(zip(names, big_out))
    outs = [loss, grad_x[None]]
    for kind in range(4):
        for item in order:
            if item == "conv":
                outs.append(taps_out[kind])
            elif isinstance(item, int):
                outs.append(gain_out[item][kind])
            else:
                outs.append(by_name[item][kind])
    return tuple(outs)
```

```python
import math

import jax
import jax.numpy as jnp
from jax import lax
from jax.experimental import pallas as pl
from jax.experimental.pallas import tpu as pltpu

F32 = jnp.float32
BF16 = jnp.bfloat16

EPS = 1e-6
NEG_INF = -1e30
HEAD = 128
ROPE = 64
QK = HEAD + ROPE
CHUNK = 64
ROPE_THETA = 10000.0
ADAM_LR, ADAM_B1, ADAM_B2, ADAM_EPS, ADAM_WD, ADAM_STEP = 0.001, 0.9, 0.999, 1e-08, 0.01, 10

LANE = 128
SUBLANE = 8
VMEM_LIMIT_BYTES = 56 * 1024 * 1024

N_DEV = 8
N_CHIP = 4
MESH = pl.DeviceIdType.MESH


def _params(*sem):
    return pltpu.CompilerParams(dimension_semantics=sem, vmem_limit_bytes=VMEM_LIMIT_BYTES)


ANY = pl.BlockSpec(memory_space=pl.ANY)


def _call(body, *, in_specs, after=(), **kw):
    n_in, n_after = len(in_specs), len(after)

    def ordered(*refs):
        body(*refs[:n_in], *refs[n_in + n_after:])

    call = pl.pallas_call(ordered, in_specs=[*in_specs, *[ANY] * n_after], **kw)
    return lambda *operands: call(*operands, *after)


def _sublane_sum(v):
    r, w = v.shape
    return jnp.sum(v.reshape(r // SUBLANE, SUBLANE, w), axis=0)


def _rstd(x):
    return lax.rsqrt(jnp.mean(x * x, axis=-1, keepdims=True) + EPS)


def _rms_bwd(x, g, dy):
    r = _rstd(x)
    xh = x * r
    dxh = dy * g
    dx = r * (dxh - xh * jnp.mean(dxh * xh, axis=-1, keepdims=True))
    return dx, dy * xh


def _accumulate(ref, val, step):
    @pl.when(step == 0)
    def _():
        ref[...] = val

    @pl.when(step > 0)
    def _():
        ref[...] += val


NN = ((1,), (0,))
NT = ((1,), (1,))
TN = ((0,), (0,))


def _matmul(name, a, b, *, grid, a_spec, b_spec, out_shape, out_specs, contract, nk=1, acc_shape=None,
            extras=(), extra_specs=(), epilogue=None, after=()):
    multi = isinstance(out_shape, (tuple, list))
    out_shapes = tuple(out_shape) if multi else (out_shape,)
    n_out = len(out_shapes)
    n_extra = len(extras)

    def body(a_ref, b_ref, *rest):
        x_refs = rest[:n_extra]
        o_refs = rest[n_extra:n_extra + n_out]

        def emit(acc):
            vals = epilogue(acc, *[r[...] for r in x_refs]) if epilogue else (acc,)
            for r, v in zip(o_refs, vals):
                r[...] = v.astype(r.dtype)

        p = lax.dot_general(a_ref[...], b_ref[...], (contract, ((), ())), preferred_element_type=F32)
        if nk == 1:
            emit(p)
        else:
            acc_ref = rest[n_extra + n_out]
            k = pl.program_id(2)
            _accumulate(acc_ref, p, k)

            @pl.when(k == nk - 1)
            def _():
                emit(acc_ref[...])

    sem = ("parallel", "parallel") + (("arbitrary",) if nk > 1 else ())
    return _call(
        body, name=name, grid=grid, after=after,
        in_specs=[a_spec, b_spec, *extra_specs],
        out_specs=out_specs,
        out_shape=out_shape,
        scratch_shapes=[pltpu.VMEM(acc_shape, F32)] if nk > 1 else [],
        compiler_params=_params(*sem),
    )(a, b, *extras)


def _fit(n, tile):
    if n <= tile:
        return n
    t = tile - tile % LANE
    while n % t:
        t -= LANE
    return t


def _mm_nn(name, a, b, out_dtype, tm, tn, after=()):
    m, k = a.shape
    n = b.shape[1]
    tm, tn = _fit(m, tm), _fit(n, tn)
    return _matmul(name, a, b, grid=(m // tm, n // tn), after=after,
                   a_spec=pl.BlockSpec((tm, k), lambda i, j: (i, 0)),
                   b_spec=pl.BlockSpec((k, tn), lambda i, j: (0, j)),
                   out_shape=jax.ShapeDtypeStruct((m, n), out_dtype),
                   out_specs=pl.BlockSpec((tm, tn), lambda i, j: (i, j)), contract=NN)


def _mm_nt(name, a, b, out_dtype, tm, tn, after=()):
    m, k = a.shape
    n = b.shape[0]
    tm, tn = _fit(m, tm), _fit(n, tn)
    return _matmul(name, a, b, grid=(m // tm, n // tn), after=after,
                   a_spec=pl.BlockSpec((tm, k), lambda i, j: (i, 0)),
                   b_spec=pl.BlockSpec((tn, k), lambda i, j: (j, 0)),
                   out_shape=jax.ShapeDtypeStruct((m, n), out_dtype),
                   out_specs=pl.BlockSpec((tm, tn), lambda i, j: (i, j)), contract=NT)


def _mm_tn(name, a, b, out_dtype, tm, tn):
    s, m = a.shape
    n = b.shape[1]
    tm, tn = _fit(m, tm), _fit(n, tn)
    return _matmul(name, a, b, grid=(m // tm, n // tn),
                   a_spec=pl.BlockSpec((s, tm), lambda i, j: (0, i)),
                   b_spec=pl.BlockSpec((s, tn), lambda i, j: (0, j)),
                   out_shape=jax.ShapeDtypeStruct((m, n), out_dtype),
                   out_specs=pl.BlockSpec((tm, tn), lambda i, j: (i, j)), contract=TN)


ROWS = 256


def _row_spec(rows, width):
    return pl.BlockSpec((rows, width), lambda i: (i, 0))


def _fixed_spec(rows, width):
    return pl.BlockSpec((rows, width), lambda i: (0, 0))


def _column_pieces(rows, start, width):
    piece = math.gcd(start, width)
    assert piece % LANE == 0
    return [pl.BlockSpec((rows, piece), lambda i, b=start // piece + p: (i, b)) for p in range(width // piece)]


def _rms_fwd(name, x, g, cols=None, after=()):
    s = x.shape[0]
    start, w = cols or (0, x.shape[1])
    rows = min(ROWS, s)
    pieces = _column_pieces(rows, start, w) if cols else [_row_spec(rows, w)]
    n = len(pieces)

    def body(*refs):
        g_ref, o_ref = refs[n:]
        xv = refs[0][...] if n == 1 else jnp.concatenate([r[...] for r in refs[:n]], axis=1)
        o_ref[...] = (xv * _rstd(xv) * g_ref[...]).astype(o_ref.dtype)

    return _call(
        body, name=name, grid=(s // rows,), after=after,
        in_specs=[*pieces, _fixed_spec(1, w)],
        out_specs=_row_spec(rows, w),
        out_shape=jax.ShapeDtypeStruct((s, w), BF16),
        compiler_params=_params("parallel"),
    )(*[x] * n, g)


def _rms_bwd_call(name, x, g, dy, out_dtype, cols=None, after=()):
    s = x.shape[0]
    start, w = cols or (0, x.shape[1])
    rows = min(ROWS, s)
    pieces = _column_pieces(rows, start, w) if cols else [_row_spec(rows, w)]
    n = len(pieces)

    def body(*refs):
        g_ref, dy_ref, dx_ref, dg_ref = refs[n:]
        xv = refs[0][...] if n == 1 else jnp.concatenate([r[...] for r in refs[:n]], axis=1)
        dx, dgc = _rms_bwd(xv, g_ref[...], dy_ref[...].astype(F32))
        dx_ref[...] = dx.astype(dx_ref.dtype)
        _accumulate(dg_ref, _sublane_sum(dgc), pl.program_id(0))

    return _call(
        body, name=name, grid=(s // rows,), after=after,
        in_specs=[*pieces, _fixed_spec(1, w), _row_spec(rows, w)],
        out_specs=[_row_spec(rows, w), _fixed_spec(SUBLANE, w)],
        out_shape=[jax.ShapeDtypeStruct((s, w), out_dtype), jax.ShapeDtypeStruct((SUBLANE, w), F32)],
        compiler_params=_params("arbitrary"),
    )(*[x] * n, g, dy)


def _norm_up(name, x, cols, g, w, after=()):
    s = x.shape[0]
    start, width = cols
    n = w.shape[1]
    tm = min(TILE_M, s)
    pieces = _column_pieces(tm, start, width)
    n_p = len(pieces)

    def body(*refs):
        g_ref, w_ref, xn_ref, o_ref = refs[n_p:]
        xv = refs[0][...] if n_p == 1 else jnp.concatenate([r[...] for r in refs[:n_p]], axis=1)
        xn = (xv * _rstd(xv) * g_ref[...]).astype(BF16)
        xn_ref[...] = xn
        o_ref[...] = jnp.dot(xn, w_ref[...], preferred_element_type=F32)

    return _call(
        body, name=name, grid=(s // tm,), after=after,
        in_specs=[*pieces, _fixed_spec(1, width), _fixed_spec(width, n)],
        out_specs=[_row_spec(tm, width), _row_spec(tm, n)],
        out_shape=[jax.ShapeDtypeStruct((s, width), BF16), jax.ShapeDtypeStruct((s, n), F32)],
        compiler_params=_params("parallel"),
    )(*[x] * n_p, g, w)


def _up_norm_bwd(name, dy, w, x, cols, g, after=()):
    s, n = dy.shape
    start, width = cols
    tm = min(TILE_M, s)
    pieces = _column_pieces(tm, start, width)
    n_p = len(pieces)

    def body(dy_ref, w_ref, *refs):
        g_ref, dx_ref, dg_ref = refs[n_p:]
        xv = refs[0][...] if n_p == 1 else jnp.concatenate([r[...] for r in refs[:n_p]], axis=1)
        dxn = lax.dot_general(dy_ref[...], w_ref[...], (NT, ((), ())), preferred_element_type=F32)
        dx, dgc = _rms_bwd(xv, g_ref[...], dxn)
        dx_ref[...] = dx.astype(dx_ref.dtype)
        _accumulate(dg_ref, _sublane_sum(dgc), pl.program_id(0))

    return _call(
        body, name=name, grid=(s // tm,), after=after,
        in_specs=[_row_spec(tm, n), _fixed_spec(width, n), *pieces, _fixed_spec(1, width)],
        out_specs=[_row_spec(tm, width), _fixed_spec(SUBLANE, width)],
        out_shape=[jax.ShapeDtypeStruct((s, width), BF16), jax.ShapeDtypeStruct((SUBLANE, width), F32)],
        compiler_params=_params("arbitrary"),
    )(dy, w, *[x] * n_p, g)


def _mid_fwd(x, y, g_post, g_pre, after=()):
    s, w = x.shape
    rows = min(ROWS, s)

    def body(x_ref, y_ref, gp_ref, gq_ref, x2_ref, h2_ref):
        yv = y_ref[...]
        x2 = x_ref[...] + yv * _rstd(yv) * gp_ref[...]
        x2_ref[...] = x2
        h2_ref[...] = (x2 * _rstd(x2) * gq_ref[...]).astype(h2_ref.dtype)

    return _call(
        body, name="mid_fwd", grid=(s // rows,), after=after,
        in_specs=[_row_spec(rows, w), _row_spec(rows, w), _fixed_spec(1, w), _fixed_spec(1, w)],
        out_specs=[_row_spec(rows, w), _row_spec(rows, w)],
        out_shape=[jax.ShapeDtypeStruct((s, w), F32), jax.ShapeDtypeStruct((s, w), BF16)],
        compiler_params=_params("parallel"),
    )(x, y, g_post, g_pre)


def _head(m, x2, tgt, g):
    s, w = m.shape
    rows = min(ROWS, s)

    def body(m_ref, x2_ref, t_ref, g_ref, dout_ref, dm_ref, dg_ref, loss_ref):
        mv = m_ref[...]
        gv = g_ref[...]
        out = x2_ref[...] + mv * _rstd(mv) * gv
        err = out - t_ref[...]
        dout = err * (1.0 / w)
        dout_ref[...] = dout
        dm, dgc = _rms_bwd(mv, gv, dout)
        dm_ref[...] = dm.astype(dm_ref.dtype)
        sq = err * err
        lanes = sq[:, 0:LANE]
        for j in range(1, w // LANE):
            lanes = lanes + sq[:, j * LANE:(j + 1) * LANE]
        step = pl.program_id(0)
        _accumulate(dg_ref, _sublane_sum(dgc), step)
        _accumulate(loss_ref, _sublane_sum(lanes) * (0.5 / w), step)

    return pl.pallas_call(
        body, name="head", grid=(s // rows,),
        in_specs=[_row_spec(rows, w), _row_spec(rows, w), _row_spec(rows, w), _fixed_spec(1, w)],
        out_specs=[_row_spec(rows, w), _row_spec(rows, w), _fixed_spec(SUBLANE, w), _fixed_spec(SUBLANE, LANE)],
        out_shape=[jax.ShapeDtypeStruct((s, w), F32), jax.ShapeDtypeStruct((s, w), BF16),
                   jax.ShapeDtypeStruct((SUBLANE, w), F32), jax.ShapeDtypeStruct((SUBLANE, LANE), F32)],
        compiler_params=_params("arbitrary"),
    )(m, x2, tgt, g)


def _mid_bwd(x2, y, d_out, d_h2, g_pre, g_post, after=()):
    s, w = x2.shape
    rows = min(ROWS, s)

    def body(x2_ref, y_ref, dout_ref, dh2_ref, gq_ref, gp_ref, dx2_ref, dy_ref, dgq_ref, dgp_ref):
        dx, dgq = _rms_bwd(x2_ref[...], gq_ref[...], dh2_ref[...])
        dx2 = dout_ref[...] + dx
        dx2_ref[...] = dx2
        dy, dgp = _rms_bwd(y_ref[...], gp_ref[...], dx2)
        dy_ref[...] = dy.astype(dy_ref.dtype)
        step = pl.program_id(0)
        _accumulate(dgq_ref, _sublane_sum(dgq), step)
        _accumulate(dgp_ref, _sublane_sum(dgp), step)

    return _call(
        body, name="mid_bwd", grid=(s // rows,), after=after,
        in_specs=[_row_spec(rows, w)] * 4 + [_fixed_spec(1, w)] * 2,
        out_specs=[_row_spec(rows, w), _row_spec(rows, w), _fixed_spec(SUBLANE, w), _fixed_spec(SUBLANE, w)],
        out_shape=[jax.ShapeDtypeStruct((s, w), F32), jax.ShapeDtypeStruct((s, w), BF16),
                   jax.ShapeDtypeStruct((SUBLANE, w), F32), jax.ShapeDtypeStruct((SUBLANE, w), F32)],
        compiler_params=_params("arbitrary"),
    )(x2, y, d_out, d_h2, g_pre, g_post)


def _first_bwd(x, g, d_h1, d_x2, after=()):
    s, w = x.shape
    rows = min(ROWS, s)

    def body(x_ref, g_ref, dh_ref, dx2_ref, dx_ref, dg_ref):
        dx, dgc = _rms_bwd(x_ref[...], g_ref[...], dh_ref[...])
        dx_ref[...] = dx2_ref[...] + dx
        _accumulate(dg_ref, _sublane_sum(dgc), pl.program_id(0))

    return _call(
        body, name="first_bwd", grid=(s // rows,), after=after,
        in_specs=[_row_spec(rows, w), _fixed_spec(1, w), _row_spec(rows, w), _row_spec(rows, w)],
        out_specs=[_row_spec(rows, w), _fixed_spec(SUBLANE, w)],
        out_shape=[jax.ShapeDtypeStruct((s, w), F32), jax.ShapeDtypeStruct((SUBLANE, w), F32)],
        compiler_params=_params("arbitrary"),
    )(x, g, d_h1, d_x2)


def _shift_down(v, k):
    t = lax.broadcasted_iota(jnp.int32, v.shape, 0)
    return jnp.where(t >= k, pltpu.roll(v, k, 0), 0.0)


def _shift_up(v, k):
    n = v.shape[0]
    t = lax.broadcasted_iota(jnp.int32, v.shape, 0)
    return jnp.where(t < n - k, pltpu.roll(v, n - k, 0), 0.0)


def _conv_core(u, b, c, w):
    z = c * u
    conv = w[0:1, :] * _shift_down(z, 2) + w[1:2, :] * _shift_down(z, 1) + w[2:3, :] * z
    return z, conv, b * conv


def _conv_fwd(proj, conv_w, g, n_groups, out_width, after=()):
    s = proj.shape[0]

    def body(u_ref, b_ref, c_ref, w_ref, g_ref, o_ref):
        _, _, yr = _conv_core(u_ref[...], b_ref[...], c_ref[...], w_ref[...])
        o_ref[...] = (yr * _rstd(yr) * g_ref[...]).astype(o_ref.dtype)

    col = lambda k: pl.BlockSpec((s, HEAD), lambda i: (0, k * n_groups + i))
    return _call(
        body, name="conv_fwd", grid=(n_groups,), after=after,
        in_specs=[col(0), col(1), col(2), pl.BlockSpec((3, HEAD), lambda i: (0, i)), pl.BlockSpec((1, HEAD), lambda i: (0, i))],
        out_specs=pl.BlockSpec((s, HEAD), lambda i: (0, i)),
        out_shape=jax.ShapeDtypeStruct((s, out_width), BF16),
        compiler_params=_params("parallel"),
    )(proj, proj, proj, conv_w, g)


def _conv_bwd(proj, d_mix, conv_w, g, n_groups):
    s = proj.shape[0]
    width = n_groups * HEAD

    def body(u_ref, b_ref, c_ref, dy_ref, w_ref, g_ref, du_ref, db_ref, dc_ref, dg_ref, dw_ref):
        u, b, c, w = u_ref[...], b_ref[...], c_ref[...], w_ref[...]
        z, conv, yr = _conv_core(u, b, c, w)
        dyr, dgc = _rms_bwd(yr, g_ref[...], dy_ref[...])
        dconv = dyr * b
        db_ref[...] = (dyr * conv).astype(db_ref.dtype)
        dz = w[2:3, :] * dconv + w[1:2, :] * _shift_up(dconv, 1) + w[0:1, :] * _shift_up(dconv, 2)
        dc_ref[...] = (dz * u).astype(dc_ref.dtype)
        du_ref[...] = (dz * c).astype(du_ref.dtype)
        dg_ref[...] = _sublane_sum(dgc)
        dw_ref[0] = _sublane_sum(dconv * _shift_down(z, 2))
        dw_ref[1] = _sublane_sum(dconv * _shift_down(z, 1))
        dw_ref[2] = _sublane_sum(dconv * z)

    col = lambda k: pl.BlockSpec((s, HEAD), lambda i: (0, k * n_groups + i))
    grp = pl.BlockSpec((s, HEAD), lambda i: (0, i))
    return pl.pallas_call(
        body, name="conv_bwd", grid=(n_groups,),
        in_specs=[col(0), col(1), col(2), grp, pl.BlockSpec((3, HEAD), lambda i: (0, i)), pl.BlockSpec((1, HEAD), lambda i: (0, i))],
        out_specs=[grp, grp, grp, pl.BlockSpec((SUBLANE, HEAD), lambda i: (0, i)),
                   pl.BlockSpec((3, SUBLANE, HEAD), lambda i: (0, 0, i))],
        out_shape=[jax.ShapeDtypeStruct((s, width), BF16)] * 3
        + [jax.ShapeDtypeStruct((SUBLANE, width), F32), jax.ShapeDtypeStruct((3, SUBLANE, width), F32)],
        compiler_params=_params("parallel"),
    )(proj, proj, proj, d_mix, conv_w, g)


def _rope_tables(s, n_heads):
    pos = jnp.arange(s, dtype=F32)
    inv_freq = jnp.power(ROPE_THETA, -jnp.arange(0, ROPE, 2, dtype=F32) / ROPE)
    ang = pos[:, None] * inv_freq[None, :]
    cos, sin = jnp.cos(ang), jnp.sin(ang)
    cs = jnp.concatenate([cos, cos], axis=1)
    sn = jnp.concatenate([-sin, sin], axis=1)
    pad = jnp.zeros((s, LANE - ROPE), F32)
    return (jnp.tile(cs, (1, n_heads)), jnp.tile(sn, (1, n_heads)),
            jnp.concatenate([cs, pad], axis=1), jnp.concatenate([sn, pad], axis=1))


def _swap_halves(v):
    w = v.shape[1]
    lane = lax.broadcasted_iota(jnp.int32, v.shape, 1)
    first = (lane % ROPE) < (ROPE // 2)
    return jnp.where(first, pltpu.roll(v, w - ROPE // 2, 1), pltpu.roll(v, ROPE // 2, 1))


def _pack_heads(q, kv, proj, kr_col, tables, n_heads, after=()):
    s = q.shape[0]
    rows = min(ROWS, s)
    cq, sq, ck, sk = tables
    wq = n_heads * ROPE

    def body(q_ref, kv_ref, kr_ref, cq_ref, sq_ref, ck_ref, sk_ref, qo_ref, ko_ref, vo_ref):
        qr = q_ref[:, n_heads * HEAD:]
        qr = qr * cq_ref[...] + _swap_halves(qr) * sq_ref[...]
        krv = kr_ref[...]
        krv = krv * ck_ref[...] + _swap_halves(krv) * sk_ref[...]
        for h in range(n_heads):
            qo_ref[h] = jnp.concatenate([q_ref[:, h * HEAD:(h + 1) * HEAD], qr[:, h * ROPE:(h + 1) * ROPE]], axis=1).astype(BF16)
            ko_ref[h] = jnp.concatenate([kv_ref[:, 2 * h * HEAD:(2 * h + 1) * HEAD], krv[:, :ROPE]], axis=1).astype(BF16)
            vo_ref[h] = kv_ref[:, (2 * h + 1) * HEAD:(2 * h + 2) * HEAD].astype(BF16)

    hs = lambda w: pl.BlockSpec((n_heads, rows, w), lambda i: (0, i, 0))
    return _call(
        body, name="pack_heads", grid=(s // rows,), after=after,
        in_specs=[_row_spec(rows, q.shape[1]), _row_spec(rows, kv.shape[1]), pl.BlockSpec((rows, LANE), lambda i: (i, kr_col // LANE)),
                  _row_spec(rows, wq), _row_spec(rows, wq), _row_spec(rows, LANE), _row_spec(rows, LANE)],
        out_specs=[hs(QK), hs(QK), hs(HEAD)],
        out_shape=[jax.ShapeDtypeStruct((n_heads, s, QK), BF16), jax.ShapeDtypeStruct((n_heads, s, QK), BF16),
                   jax.ShapeDtypeStruct((n_heads, s, HEAD), BF16)],
        compiler_params=_params("parallel"),
    )(q, kv, proj, cq, sq, ck, sk)


def _unpack_heads(dq, dk, dv, tables, n_heads):
    s = dq.shape[1]
    rows = min(ROWS, s)
    cq, sq, ck, sk = tables
    wq = n_heads * ROPE

    def body(dq_ref, dk_ref, dv_ref, cq_ref, sq_ref, ck_ref, sk_ref, qo_ref, kvo_ref, kro_ref):
        dqr = jnp.concatenate([dq_ref[h][:, HEAD:] for h in range(n_heads)], axis=1)
        dqr = dqr * cq_ref[...] - _swap_halves(dqr) * sq_ref[...]
        dkr = dk_ref[0][:, HEAD:]
        for h in range(1, n_heads):
            dkr = dkr + dk_ref[h][:, HEAD:]
        dkr = jnp.concatenate([dkr, jnp.zeros((rows, LANE - ROPE), F32)], axis=1)
        dkr = dkr * ck_ref[...] - _swap_halves(dkr) * sk_ref[...]
        kro_ref[...] = dkr.astype(kro_ref.dtype)
        qo_ref[:, n_heads * HEAD:] = dqr.astype(qo_ref.dtype)
        for h in range(n_heads):
            qo_ref[:, h * HEAD:(h + 1) * HEAD] = dq_ref[h][:, :HEAD].astype(qo_ref.dtype)
            kvo_ref[:, 2 * h * HEAD:(2 * h + 1) * HEAD] = dk_ref[h][:, :HEAD].astype(kvo_ref.dtype)
            kvo_ref[:, (2 * h + 1) * HEAD:(2 * h + 2) * HEAD] = dv_ref[h].astype(kvo_ref.dtype)

    hs = lambda w: pl.BlockSpec((n_heads, rows, w), lambda i: (0, i, 0))
    return pl.pallas_call(
        body, name="unpack_heads", grid=(s // rows,),
        in_specs=[hs(QK), hs(QK), hs(HEAD), _row_spec(rows, wq), _row_spec(rows, wq), _row_spec(rows, LANE), _row_spec(rows, LANE)],
        out_specs=[_row_spec(rows, n_heads * QK), _row_spec(rows, 2 * n_heads * HEAD), _row_spec(rows, LANE)],
        out_shape=[jax.ShapeDtypeStruct((s, n_heads * QK), BF16), jax.ShapeDtypeStruct((s, 2 * n_heads * HEAD), BF16),
                   jax.ShapeDtypeStruct((s, LANE), BF16)],
        compiler_params=_params("parallel"),
    )(dq, dk, dv, cq, sq, ck, sk)


TQ = 256


LOG2_E = 1.4426950408889634


def _softmax_parts(q, k):
    tq, n_keys = q.shape[0], k.shape[0]
    sc = lax.dot_general(q, k, (NT, ((), ())), preferred_element_type=F32) * (QK ** -0.5 * LOG2_E)
    row = lax.broadcasted_iota(jnp.int32, (tq, tq), 0)
    col = lax.broadcasted_iota(jnp.int32, (tq, tq), 1)
    own = jnp.where(col // CHUNK <= row // CHUNK, sc[:, n_keys - tq:], NEG_INF)
    sc = own if n_keys == tq else jnp.concatenate([sc[:, :n_keys - tq], own], axis=1)
    e = jnp.exp2(sc - jnp.max(sc, axis=-1, keepdims=True))
    return e, 1.0 / jnp.sum(e, axis=-1, keepdims=True)


def _attn_fwd(q, k, v, g, mix, col0, after=()):
    n_heads, s, _ = q.shape
    tq = min(TQ, s)
    assert tq % CHUNK == 0 and s % tq == 0

    def body(q_ref, k_ref, v_ref, g_ref, mix_ref, o_ref, y_ref):
        for c in range(s // tq):
            rows, n_keys = pl.ds(c * tq, tq), (c + 1) * tq
            e, inv = _softmax_parts(q_ref[rows, :], k_ref[0:n_keys, :])
            o = jnp.dot(e.astype(BF16), v_ref[0:n_keys, :], preferred_element_type=F32) * inv
            o_ref[rows, :] = o
            y_ref[rows, :] = (o * _rstd(o) * g_ref[...]).astype(y_ref.dtype)

    head = lambda w: pl.BlockSpec((None, s, w), lambda h: (h, 0, 0))
    return _call(
        body, name="attn_fwd", grid=(n_heads,), after=after,
        in_specs=[head(QK), head(QK), head(HEAD), pl.BlockSpec((1, HEAD), lambda h: (0, h)), ANY],
        out_specs=[head(HEAD), pl.BlockSpec((s, HEAD), lambda h: (0, col0 // HEAD + h))],
        out_shape=[jax.ShapeDtypeStruct((n_heads, s, HEAD), F32), jax.ShapeDtypeStruct(mix.shape, mix.dtype)],
        input_output_aliases={4: 1},
        compiler_params=_params("parallel"),
    )(q, k, v, g, mix)


def _attn_bwd(q, k, v, o, d_mix, g, col0, after=()):
    n_heads, s, _ = q.shape
    tq = min(TQ, s)

    def body(q_ref, k_ref, v_ref, o_ref, dy_ref, g_ref, dq_ref, dk_ref, dv_ref, dg_ref):
        dg = None
        for c in reversed(range(s // tq)):
            rows, n_keys = pl.ds(c * tq, tq), (c + 1) * tq
            qv, kv_, vv = q_ref[rows, :], k_ref[0:n_keys, :], v_ref[0:n_keys, :]
            do, dgc = _rms_bwd(o_ref[rows, :], g_ref[...], dy_ref[rows, :])
            do = do.astype(BF16)
            dg = _sublane_sum(dgc) if dg is None else dg + _sublane_sum(dgc)
            e, inv = _softmax_parts(qv, kv_)
            p = e * inv
            dp = lax.dot_general(do, vv, (NT, ((), ())), preferred_element_type=F32)
            ds = (p * (dp - jnp.sum(p * dp, axis=-1, keepdims=True)) * (QK ** -0.5)).astype(BF16)
            dq_ref[rows, :] = jnp.dot(ds, kv_, preferred_element_type=F32)
            dk = lax.dot_general(ds, qv, (TN, ((), ())), preferred_element_type=F32)
            dv = lax.dot_general(p.astype(BF16), do, (TN, ((), ())), preferred_element_type=F32)
            if n_keys == s:
                dk_ref[...] = dk
                dv_ref[...] = dv
            else:
                dk_ref[0:n_keys, :] += dk
                dv_ref[0:n_keys, :] += dv
        dg_ref[...] = dg

    c0 = col0 // HEAD
    head = lambda w: pl.BlockSpec((None, s, w), lambda h, *_: (h, 0, 0))
    in_specs = [head(QK), head(QK), head(HEAD), head(HEAD), pl.BlockSpec((s, HEAD), lambda h, *_: (0, c0 + h)),
                pl.BlockSpec((1, HEAD), lambda h, *_: (0, h))]
    out_specs = [head(QK), head(QK), head(HEAD), pl.BlockSpec((SUBLANE, HEAD), lambda h, *_: (0, h))]
    out_shape = [jax.ShapeDtypeStruct((n_heads, s, QK), F32), jax.ShapeDtypeStruct((n_heads, s, QK), F32),
                 jax.ShapeDtypeStruct((n_heads, s, HEAD), F32), jax.ShapeDtypeStruct((SUBLANE, n_heads * HEAD), F32)]
    return _call(body, name="attn_bwd", grid=(n_heads,), after=after, in_specs=in_specs, out_specs=out_specs,
                 out_shape=out_shape, compiler_params=_params("parallel"))(q, k, v, o, d_mix, g)


TILE_M = 1024
TILE_N = 1024


def _up_fwd(h2, w_up):
    s, d = h2.shape
    nb, _, fb = w_up.shape
    tm = min(TILE_M,s)

    def epilogue(acc):
        r = jnp.maximum(acc, 0.0)
        return r * r, r

    blk = pl.BlockSpec((tm, fb), lambda i, j: (i, j))
    return _matmul("up_fwd", h2, w_up, grid=(s // tm, nb),
                   a_spec=pl.BlockSpec((tm, d), lambda i, j: (i, 0)),
                   b_spec=pl.BlockSpec((None, d, fb), lambda i, j: (j, 0, 0)),
                   out_shape=[jax.ShapeDtypeStruct((s, nb * fb), BF16)] * 2, out_specs=[blk, blk],
                   contract=NN, epilogue=epilogue)


def _down_fwd(a, w_down):
    s, f = a.shape
    d = w_down.shape[1]
    tm, tn, tk = min(TILE_M,s), min(TILE_N,d), 2048
    nk = f // tk
    return _matmul("down_fwd", a, w_down, grid=(s // tm, d // tn, nk),
                   a_spec=pl.BlockSpec((tm, tk), lambda i, j, k: (i, k)),
                   b_spec=pl.BlockSpec((tk, tn), lambda i, j, k: (k, j)),
                   out_shape=jax.ShapeDtypeStruct((s, d), F32),
                   out_specs=pl.BlockSpec((tm, tn), lambda i, j, k: (i, j)),
                   contract=NN, nk=nk, acc_shape=(tm, tn))


def _down_bwd_act(d_m, w_down, r, after=()):
    s, d = d_m.shape
    f = w_down.shape[0]
    tm, tn = min(TILE_M,s), min(TILE_N,f)
    blk = pl.BlockSpec((tm, tn), lambda i, j: (i, j))
    return _matmul("down_bwd_act", d_m, w_down, grid=(s // tm, f // tn), after=after,
                   a_spec=pl.BlockSpec((tm, d), lambda i, j: (i, 0)),
                   b_spec=pl.BlockSpec((tn, d), lambda i, j: (j, 0)),
                   out_shape=jax.ShapeDtypeStruct((s, f), BF16), out_specs=blk, contract=NT,
                   extras=(r,), extra_specs=(blk,),
                   epilogue=lambda acc, rv: (acc * (2.0 * rv.astype(F32)),))


def _up_bwd_act(d_up, w_up, after=()):
    s, _ = d_up.shape
    nb, d, fb = w_up.shape
    tm, tn = min(TILE_M, s), min(TILE_N,d)
    pair = 2
    n_after = len(after)

    def body(a_ref, w_ref, *rest):
        o_ref, acc_ref = rest[n_after:]
        k = pl.program_id(2)
        p = None
        for t in range(pair):
            term = lax.dot_general(a_ref[:, t * fb:(t + 1) * fb], w_ref[t], (NT, ((), ())), preferred_element_type=F32)
            p = term if p is None else p + term
        _accumulate(acc_ref, p, k)

        @pl.when(k == nb // pair - 1)
        def _():
            o_ref[...] = acc_ref[...]

    return pl.pallas_call(
        body, name="up_bwd_act", grid=(s // tm, d // tn, nb // pair),
        in_specs=[pl.BlockSpec((tm, pair * fb), lambda i, j, k: (i, k)),
                  pl.BlockSpec((pair, tn, fb), lambda i, j, k: (k, j, 0))] + [ANY] * n_after,
        out_specs=pl.BlockSpec((tm, tn), lambda i, j, k: (i, j)),
        out_shape=jax.ShapeDtypeStruct((s, d), F32),
        scratch_shapes=[pltpu.VMEM((tm, tn), F32)],
        compiler_params=_params("parallel", "parallel", "arbitrary"),
    )(d_up, w_up, *after)


def _half_grad(name, a, b, core, home, received, after, *, grid, a_block, a_map, b_block, b_map, o_block, o_map, out_shape):
    n_after = len(after)
    pick = (lambda ref: ref[0]) if home else (lambda ref: 1 - ref[0])

    def body(core_ref, a_ref, b_ref, *rest):
        acc = lax.dot_general(a_ref[...], b_ref[...], (TN, ((), ())), preferred_element_type=F32)
        if received is not None:
            acc = acc + rest[0][...].astype(F32)
        rest[-1][...] = acc.astype(rest[-1].dtype)

    wrap = lambda fn: (lambda i, j, core_ref: fn(i, j, pick(core_ref)))
    o_spec = pl.BlockSpec(o_block, wrap(o_map))
    extra = [] if received is None else [o_spec]
    operands = [] if received is None else [received]
    return pl.pallas_call(
        body, name=name,
        grid_spec=pltpu.PrefetchScalarGridSpec(
            num_scalar_prefetch=1, grid=grid,
            in_specs=[pl.BlockSpec(a_block, wrap(a_map)), pl.BlockSpec(b_block, wrap(b_map))] + extra + [ANY] * n_after,
            out_specs=o_spec),
        out_shape=out_shape,
        compiler_params=_params("parallel", "parallel"),
    )(core, a, b, *operands, *after)


def _down_half_grad(name, a, d_m, core, home, received=None, after=()):
    s, f = a.shape
    d = d_m.shape[1]
    r = f // N_DEV
    tn = min(TILE_N, d)
    return _half_grad(name, a, d_m, core, home, received, after, grid=(N_CHIP, d // tn),
                      a_block=(s, r), a_map=lambda k, j, p: (0, 2 * k + p),
                      b_block=(s, tn), b_map=lambda k, j, p: (0, j),
                      o_block=(None, r, tn), o_map=lambda k, j, p: (k, 0, j),
                      out_shape=jax.ShapeDtypeStruct((N_CHIP, r, d), BF16))


def _up_half_grad(name, h2, d_up, core, home, received=None, after=()):
    s, d = h2.shape
    fb = d_up.shape[1] // N_DEV
    tm = min(TILE_M, d)
    return _half_grad(name, h2, d_up, core, home, received, after, grid=(d // tm, N_CHIP),
                      a_block=(s, tm), a_map=lambda i, k, p: (0, i),
                      b_block=(s, fb), b_map=lambda i, k, p: (0, 2 * k + p),
                      o_block=(None, tm, fb), o_map=lambda i, k, p: (k, i, 0),
                      out_shape=jax.ShapeDtypeStruct((N_CHIP, d, fb), BF16))


MXU_WIDTH = 256


def _in_pad(in_width):
    return -(-in_width // MXU_WIDTH) * MXU_WIDTH


def _join_col_shards(name, blocks, own, device, pieces=None):
    n, r, w = blocks.shape
    rows = min(ROWS, r)
    pieces = pieces or [(j, 0, w) for j in range(n)]
    used = sum(b - a for _, a, b in pieces)
    width = _in_pad(used)

    def body(dev_ref, x_ref, own_ref, o_ref):
        block = lambda j: jnp.where(dev_ref[0] == j, own_ref[...], x_ref[j])
        cols = [block(j)[:, a:b] for j, a, b in pieces]
        tail = [jnp.zeros((rows, width - used), o_ref.dtype)] if width > used else []
        o_ref[...] = jnp.concatenate(cols + tail, axis=1)

    return pl.pallas_call(
        body, name=name,
        grid_spec=pltpu.PrefetchScalarGridSpec(
            num_scalar_prefetch=1, grid=(r // rows,),
            in_specs=[pl.BlockSpec((n, rows, w), lambda i, dev: (0, i, 0)), pl.BlockSpec((rows, w), lambda i, dev: (i, 0))],
            out_specs=pl.BlockSpec((rows, width), lambda i, dev: (i, 0))),
        out_shape=jax.ShapeDtypeStruct((r, width), blocks.dtype),
        compiler_params=_params("parallel"),
    )(device, blocks, own)


def _permute_q_cols(w_uq, n_heads):
    r = w_uq.shape[0]
    w3 = w_uq.reshape(r, n_heads, QK)
    return jnp.concatenate([w3[:, :, :HEAD].reshape(r, n_heads * HEAD), w3[:, :, HEAD:].reshape(r, n_heads * ROPE)], axis=1)


def _unpermute_q_rows(wt, n_heads):
    r = wt.shape[1]
    nope = wt[:n_heads * HEAD].reshape(n_heads, HEAD, r)
    rope = wt[n_heads * HEAD:].reshape(n_heads, ROPE, r)
    return jnp.concatenate([nope, rope], axis=1).reshape(n_heads * QK, r)


def _local_step(x, tgt, gains, weights, grads, first_after=()):
    pre_mix_g, q_norm_g, kv_norm_g, conv_out_g, attn_out_g, post_mix_g, pre_mlp_g, post_mlp_g = gains
    s, d = x.shape
    conv_width = conv_out_g.shape[1]
    n_groups = conv_width // HEAD
    r_q, r_kv = q_norm_g.shape[1], kv_norm_g.shape[1]
    n_heads = attn_out_g.shape[1] // HEAD
    c_q0 = 3 * conv_width
    c_kv0 = c_q0 + r_q
    c_kr0 = c_kv0 + r_kv
    in_pad = _in_pad(c_kr0 + ROPE)
    tn_in = _fit(in_pad, 6 * MXU_WIDTH)
    tables = _rope_tables(s, n_heads)

    h1 = _rms_fwd("pre_mix_norm", x, pre_mix_g, after=first_after)
    weights.forward(0, (h1,))
    weights.relay(0, tables)
    w_in_p, conv_w = weights.ready(0, ())
    proj = _mm_nn("in_proj", h1, w_in_p, F32, TILE_M, tn_in)
    y_conv = _conv_fwd(proj, conv_w, conv_out_g, n_groups, conv_width + n_heads * HEAD, after=weights.forward(1, (proj,)))
    w_uq_p, w_ukv, w_o = weights.ready(1, (y_conv,))
    qn, q = _norm_up("q_up", proj, (c_q0, r_q), q_norm_g, w_uq_p)
    kvn, kv = _norm_up("kv_up", proj, (c_kv0, r_kv), kv_norm_g, w_ukv)
    qh, kh, vh = _pack_heads(q, kv, proj, c_kr0, tables, n_heads)
    o, mix = _attn_fwd(qh, kh, vh, attn_out_g, y_conv, conv_width, after=weights.forward(2, (qh, kh, vh)))
    y = _mm_nn("out_proj", mix, w_o, F32, TILE_M, TILE_N)
    x2, h2 = _mid_fwd(x, y, post_mix_g, pre_mlp_g, after=weights.forward(3, (y,)))
    weights.relay(2, (h2,))
    (w_up,) = weights.ready(2, ())
    a, r = _up_fwd(h2, w_up)
    weights.relay(3, (a,))
    (w_down,) = weights.ready(3, ())
    m = _down_fwd(a, w_down)

    d_out, d_m, dg_post_mlp, loss_part = _head(m, x2, tgt, post_mlp_g)
    core = grads.core
    away = _down_half_grad("down_bwd_w_away", a, d_m, core, home=False)
    d_up = _down_bwd_act(d_m, w_down, r, after=grads.send_away(0, away))
    sums = _down_half_grad("down_bwd_w_home", a, d_m, core, home=True, received=grads.received(0, (d_up,)))
    away = _up_half_grad("up_bwd_w_away", h2, d_up, core, home=False, after=grads.send_sums(0, (sums,)))
    d_h2 = _up_bwd_act(d_up, w_up, after=grads.send_away(1, away))
    sums = _up_half_grad("up_bwd_w_home", h2, d_up, core, home=True, received=grads.received(1, (d_h2,)))
    d_x2, d_y, dg_pre_mlp, dg_post_mix = _mid_bwd(x2, y, d_out, d_h2, pre_mlp_g, post_mix_g, after=grads.send_sums(1, (sums,)))
    d_mix = _mm_nt("out_proj_bwd_act", d_y, w_o, F32, TILE_M, TILE_N)
    gw_o = _mm_tn("out_proj_bwd_w", mix, d_y, BF16, TILE_M, TILE_N)
    dqh, dkh, dvh, dg_attn = _attn_bwd(qh, kh, vh, o, d_mix, attn_out_g, conv_width)
    d_q, d_kv, d_kr = _unpack_heads(dqh, dkh, dvh, tables, n_heads)
    gw_uq_t = _mm_tn("q_up_bwd_w", d_q, qn, F32, TILE_M, TILE_N)
    gw_ukv = _mm_tn("kv_up_bwd_w", kvn, d_kv, BF16, TILE_M, TILE_N)
    d_cq, dg_q = _up_norm_bwd("q_up_bwd_act", d_q, w_uq_p, proj, (c_q0, r_q), q_norm_g, after=grads.full(2, (gw_o, gw_uq_t, gw_ukv)))
    d_ckv, dg_kv = _up_norm_bwd("kv_up_bwd_act", d_kv, w_ukv, proj, (c_kv0, r_kv), kv_norm_g)
    d_u, d_b, d_c, dg_conv, dw_conv = _conv_bwd(proj, d_mix, conv_w, conv_out_g, n_groups)
    d_proj = jnp.concatenate([d_u, d_b, d_c, d_cq, d_ckv, d_kr, jnp.zeros((s, in_pad - c_kr0 - LANE), BF16)], axis=1)
    gw_in_t = _mm_tn("in_proj_bwd_w", d_proj, h1, F32, tn_in, TILE_N)
    updated = grads.update_now(0, grads.send_away(3, gw_in_t))
    d_h1 = _mm_nt("in_proj_bwd_act", d_proj, w_in_p, F32, TILE_M, TILE_N, after=grads.full(3, (gw_in_t,), received=updated))
    grad_x, dg_pre_mix = _first_bwd(x, pre_mix_g, d_h1, d_x2)

    small = [dg_pre_mix, dg_q, dg_kv, dg_conv, dg_attn, dg_post_mix, dg_pre_mlp, dg_post_mlp,
             dw_conv[0], dw_conv[1], dw_conv[2], loss_part]
    return grad_x, jnp.concatenate(small, axis=1)


HBM = pl.BlockSpec(memory_space=pltpu.HBM)
SEM = pl.BlockSpec(memory_space=pltpu.SEMAPHORE)
IN_VMEM = pl.BlockSpec(memory_space=pltpu.VMEM)
SPLIT = pltpu.CompilerParams(has_side_effects=pltpu.SideEffectType.DATAFLOW_SIDE_EFFECTING)


def _in_hbm(a):
    return pltpu.with_memory_space_constraint(a, pltpu.HBM)


def _hbm_like(a):
    return pltpu.HBM(a.shape, a.dtype)


def _place():
    x, y, c = lax.axis_index("x"), lax.axis_index("y"), lax.axis_index("c")
    other_chips = [(1 - x, y), (x, 1 - y), (1 - x, 1 - y)]
    return x, y, c, other_chips


def _block(px, py, pc):
    return 4 * px + 2 * py + pc


def _await(block, sem):
    pltpu.make_async_copy(block, block, sem).wait()


def _relay_route(x, y, c):
    came_from = ((1 - x) * (1 - c) + x * c, y * (1 - c) + (1 - y) * c)
    goes_to = (x * (1 - c) + (1 - x) * c, (1 - y) * (1 - c) + y * c)
    return came_from, goes_to


def _gather_start(name, shards, groups, relayed=(), after=()):
    n, ng = len(shards), len(groups)
    lands = [lax.empty((N_DEV, *a.shape), a.dtype) for a in shards]

    def body(*refs):
        src, land = refs[:n], refs[n:2 * n]
        sems, token = refs[2 * n + len(after):2 * n + len(after) + 2 * ng], refs[-1]
        x, y, c, chips = _place()
        targets = [(x, y, 1 - c)] + [(*chip, c) for chip in chips]
        for gi, group in enumerate(groups):
            for i, w in enumerate(group):
                for k, to in enumerate(targets[:3] if gi in relayed else targets):
                    pltpu.make_async_remote_copy(
                        src_ref=src[w], dst_ref=land[w].at[_block(x, y, c)],
                        send_sem=sems[2 * gi].at[4 * i + k], recv_sem=sems[2 * gi + 1].at[4 * i + k],
                        device_id=to, device_id_type=MESH).start()
        token[...] = jnp.zeros_like(token)

    sem_shapes = [pltpu.SemaphoreType.DMA((4 * len(g),)) for g in groups for _ in range(2)]
    out = pl.pallas_call(
        body, name=name,
        in_specs=[HBM] * (2 * n) + [ANY] * len(after),
        out_specs=[SEM] * (2 * ng) + [HBM] * (2 * n) + [IN_VMEM],
        out_shape=sem_shapes + [_hbm_like(a) for a in shards] + [_hbm_like(a) for a in lands]
        + [jax.ShapeDtypeStruct((SUBLANE, LANE), F32)],
        input_output_aliases={i: 2 * ng + i for i in range(2 * n)},
        compiler_params=SPLIT,
    )(*[_in_hbm(a) for a in shards], *[_in_hbm(a) for a in lands], *after)
    sems = [(out[2 * gi], out[2 * gi + 1]) for gi in range(ng)]
    return sems, out[2 * ng:2 * ng + n], out[2 * ng + n:2 * ng + 2 * n], out[-1]


def _gather_forward(name, shards, lands, send1, recv1, after, relayed=False):
    n = len(lands)

    def body(*refs):
        src, land = refs[:n], refs[n:2 * n]
        s1, r1 = refs[2 * n], refs[2 * n + 1]
        s2, r2 = refs[2 * n + 2 + len(after)], refs[2 * n + 3 + len(after)]
        x, y, c, chips = _place()
        me, sibling = (x, y, c), (x, y, 1 - c)
        for j, chip in enumerate(chips[:2] if relayed else chips):
            for i in range(n):
                blk = land[i].at[_block(*chip, c)]
                pltpu.make_async_remote_copy(src_ref=blk, dst_ref=blk, send_sem=s1.at[4 * i + 1 + j], recv_sem=r1.at[4 * i + 1 + j],
                                             device_id=me, device_id_type=MESH).wait_recv()
                pltpu.make_async_remote_copy(src_ref=blk, dst_ref=blk, send_sem=s2.at[3 * i + j], recv_sem=r2.at[3 * i + j],
                                             device_id=sibling, device_id_type=MESH).start()
        if relayed:
            came_from, goes_to = _relay_route(x, y, c)
            for i in range(n):
                blk = land[i].at[_block(*came_from, c)]
                pltpu.make_async_remote_copy(src_ref=blk, dst_ref=blk, send_sem=s2.at[3 * i + 2], recv_sem=r2.at[3 * i + 2],
                                             device_id=(*goes_to, c), device_id_type=MESH).start()
        for i in range(n):
            blk = land[i].at[_block(x, y, 1 - c)]
            pltpu.make_async_remote_copy(src_ref=blk, dst_ref=blk, send_sem=s1.at[4 * i], recv_sem=r1.at[4 * i],
                                         device_id=me, device_id_type=MESH).wait_recv()
            for k in range(3 if relayed else 4):
                pltpu.make_async_remote_copy(src_ref=src[i], dst_ref=land[i].at[_block(x, y, c)], send_sem=s1.at[4 * i + k],
                                             recv_sem=r1.at[4 * i + k], device_id=sibling, device_id_type=MESH).wait_send()

    sem = pltpu.SemaphoreType.DMA((3 * n,))
    out = pl.pallas_call(
        body, name=name,
        in_specs=[HBM] * (2 * n) + [SEM, SEM] + [ANY] * len(after),
        out_specs=[SEM, SEM] + [HBM] * n,
        out_shape=[sem, sem] + [_hbm_like(a) for a in lands],
        input_output_aliases={n + i: 2 + i for i in range(n)},
        compiler_params=SPLIT,
    )(*shards, *lands, send1, recv1, *after)
    return (out[0], out[1]), out[2:]


def _gather_relay_forward(name, lands, send2, recv2, after):
    n = len(lands)

    def body(*refs):
        land, s2, r2 = refs[:n], refs[n], refs[n + 1]
        s3, r3 = refs[n + 2 + len(after)], refs[n + 3 + len(after)]
        x, y, c, _ = _place()
        me, sibling = (x, y, c), (x, y, 1 - c)
        came_from, _ = _relay_route(x, y, c)
        for i in range(n):
            blk = land[i].at[_block(1 - x, 1 - y, c)]
            pltpu.make_async_remote_copy(src_ref=blk, dst_ref=blk, send_sem=s2.at[3 * i + 2], recv_sem=r2.at[3 * i + 2],
                                         device_id=me, device_id_type=MESH).wait_recv()
            pltpu.make_async_remote_copy(src_ref=blk, dst_ref=blk, send_sem=s3.at[i], recv_sem=r3.at[i],
                                         device_id=sibling, device_id_type=MESH).start()
            sent = land[i].at[_block(*came_from, c)]
            pltpu.make_async_remote_copy(src_ref=sent, dst_ref=sent, send_sem=s2.at[3 * i + 2], recv_sem=r2.at[3 * i + 2],
                                         device_id=me, device_id_type=MESH).wait_send()

    sem = pltpu.SemaphoreType.DMA((n,))
    out = pl.pallas_call(
        body, name=name,
        in_specs=[HBM] * n + [SEM, SEM] + [ANY] * len(after),
        out_specs=[SEM, SEM] + [HBM] * n,
        out_shape=[sem, sem] + [_hbm_like(a) for a in lands],
        input_output_aliases={i: 2 + i for i in range(n)},
        compiler_params=SPLIT,
    )(*lands, send2, recv2, *after)
    return (out[0], out[1]), out[2:]


def _gather_wait(name, lands, send2, recv2, after, relay_sems=None):
    n = len(lands)
    n_sems = 2 if relay_sems is None else 4

    def body(*refs):
        land, s2, r2 = refs[:n], refs[n], refs[n + 1]
        for i in range(n):
            for j in range(3 if relay_sems is None else 2):
                _await(land[i].at[0], r2.at[3 * i + j])
                _await(land[i].at[0], s2.at[3 * i + j])
            if relay_sems is not None:
                _await(land[i].at[0], refs[n + 3].at[i])
                _await(land[i].at[0], refs[n + 2].at[i])

    return pl.pallas_call(
        body, name=name,
        in_specs=[HBM] * n + [SEM] * n_sems + [ANY] * len(after), out_specs=[HBM] * n, out_shape=[_hbm_like(a) for a in lands],
        input_output_aliases={i: i for i in range(n)},
        compiler_params=SPLIT,
    )(*lands, send2, recv2, *(relay_sems or ()), *after)


def _pair_exchange(name, grads, shard_rows):
    n = len(grads)
    shapes = [(g.shape[1:] if r is None else (r, g.shape[1])) for g, r in zip(grads, shard_rows)]

    def body(*refs):
        ins, recv = refs[:n], refs[n:2 * n]
        send_sems, recv_sems = refs[2 * n:]
        x, y, c, _ = _place()
        sends = []
        for w in range(n):
            for k in range(N_CHIP):
                j, r = 2 * k + 1 - c, shard_rows[w]
                src = ins[w].at[j] if r is None else ins[w].at[pl.ds(pl.multiple_of(j * r, SUBLANE), r), :]
                sends.append(pltpu.make_async_remote_copy(
                    src_ref=src, dst_ref=recv[w].at[k],
                    send_sem=send_sems.at[w, k], recv_sem=recv_sems.at[w, k],
                    device_id=(x, y, 1 - c), device_id_type=MESH))
        for cp in sends:
            cp.start()
        for cp in sends:
            cp.wait()

    return pl.pallas_call(
        body, name=name,
        in_specs=[ANY] * n, out_specs=[ANY] * n,
        out_shape=[jax.ShapeDtypeStruct((N_CHIP, *shape), g.dtype) for g, shape in zip(grads, shapes)],
        scratch_shapes=[pltpu.SemaphoreType.DMA((n, N_CHIP))] * 2,
    )(*grads)


def _pair_sum_rows(name, grad, received, core):
    _, r, c = received.shape
    tc = _fit(c, 512)

    def body(core_ref, a_ref, b_ref, o_ref):
        o_ref[...] = (a_ref[...] + b_ref[...]).astype(o_ref.dtype)

    spec = pl.BlockSpec((None, r, tc), lambda k, i, core_ref: (k, 0, i))
    return pl.pallas_call(
        body, name=name,
        grid_spec=pltpu.PrefetchScalarGridSpec(
            num_scalar_prefetch=1, grid=(N_CHIP, c // tc),
            in_specs=[pl.BlockSpec((r, tc), lambda k, i, core_ref: (2 * k + core_ref[0], i)), spec],
            out_specs=spec),
        out_shape=jax.ShapeDtypeStruct(received.shape, BF16),
        compiler_params=_params("parallel", "parallel"),
    )(core, grad, received)


def _pair_sum(name, grad, received, core):
    _, r, c = received.shape
    rows = min(ROWS, r)
    assert r % rows == 0

    def body(core_ref, a_ref, b_ref, o_ref):
        o_ref[...] = (a_ref[...].astype(F32) + b_ref[...].astype(F32)).astype(o_ref.dtype)

    spec = pl.BlockSpec((None, rows, c), lambda k, i, core_ref: (k, i, 0))
    return pl.pallas_call(
        body, name=name,
        grid_spec=pltpu.PrefetchScalarGridSpec(
            num_scalar_prefetch=1, grid=(N_CHIP, r // rows),
            in_specs=[pl.BlockSpec((None, None, rows, c), lambda k, i, core_ref: (k, core_ref[0], i, 0)), spec],
            out_specs=spec),
        out_shape=jax.ShapeDtypeStruct(received.shape, received.dtype),
        compiler_params=_params("parallel", "parallel"),
    )(core, grad.reshape(N_CHIP, 2, r, c), received)


def _away_shard(src, k, c, shard_rows):
    if shard_rows is None:
        return src.at[k]
    return src.at[pl.ds(pl.multiple_of((2 * k + 1 - c) * shard_rows, SUBLANE), shard_rows), :]


def _pair_send_start(name, away, shard_rows=None):
    shape = away.shape if shard_rows is None else (N_CHIP, shard_rows, away.shape[1])
    land = lax.empty(shape, away.dtype)

    def body(src, dst, send, recv, src_thru, dst_thru, token):
        x, y, c, _ = _place()
        for k in range(N_CHIP):
            pltpu.make_async_remote_copy(src_ref=_away_shard(src, k, c, shard_rows), dst_ref=dst.at[k], send_sem=send.at[k],
                                         recv_sem=recv.at[k], device_id=(x, y, 1 - c), device_id_type=MESH).start()
        token[...] = jnp.zeros_like(token)

    sem = pltpu.SemaphoreType.DMA((N_CHIP,))
    out = pl.pallas_call(
        body, name=name,
        in_specs=[HBM, HBM], out_specs=[SEM, SEM, HBM, HBM, IN_VMEM],
        out_shape=[sem, sem, _hbm_like(away), _hbm_like(land), jax.ShapeDtypeStruct((SUBLANE, LANE), F32)],
        input_output_aliases={0: 2, 1: 3},
        compiler_params=SPLIT,
    )(_in_hbm(away), _in_hbm(land))
    return (out[0], out[1]), out[2], out[3], out[4]


def _pair_send_wait(name, sems, src, land, after, shard_rows=None):
    def body(src_ref, dst_ref, send, recv, *rest):
        for k in range(N_CHIP):
            _await(dst_ref.at[k], send.at[k])
            _await(dst_ref.at[k], recv.at[k])

    return pl.pallas_call(
        body, name=name,
        in_specs=[HBM, HBM, SEM, SEM] + [ANY] * len(after), out_specs=HBM, out_shape=_hbm_like(land),
        input_output_aliases={1: 0},
        compiler_params=SPLIT,
    )(src, land, *sems, *after)


def _chip_send_start(name, sums):
    n = len(sums)
    lands = [lax.empty(a.shape, a.dtype) for a in sums]

    def body(*refs):
        src, land = refs[:n], refs[n:2 * n]
        send, recv, token = refs[2 * n], refs[2 * n + 1], refs[-1]
        x, y, c, chips = _place()
        for w in range(n):
            for j, (px, py) in enumerate(chips):
                pltpu.make_async_remote_copy(
                    src_ref=src[w].at[2 * px + py], dst_ref=land[w].at[2 * x + y],
                    send_sem=send.at[3 * w + j], recv_sem=recv.at[3 * w + j],
                    device_id=(px, py, c), device_id_type=MESH).start()
        token[...] = jnp.zeros_like(token)

    sem = pltpu.SemaphoreType.DMA((3 * n,))
    out = pl.pallas_call(
        body, name=name,
        in_specs=[HBM] * (2 * n),
        out_specs=[SEM, SEM] + [HBM] * (2 * n) + [IN_VMEM],
        out_shape=[sem, sem] + [_hbm_like(a) for a in sums] + [_hbm_like(a) for a in lands]
        + [jax.ShapeDtypeStruct((SUBLANE, LANE), F32)],
        input_output_aliases={i: 2 + i for i in range(2 * n)},
        compiler_params=SPLIT,
    )(*[_in_hbm(a) for a in sums], *[_in_hbm(a) for a in lands])
    return (out[0], out[1]), out[2:2 + n], out[2 + n:2 + 2 * n], out[-1]


def _chip_send_wait(name, groups, after):
    counts = [len(g[1]) for g in groups]
    n = sum(counts)

    def body(*refs):
        land = refs[n:2 * n]
        sems = refs[2 * n:2 * n + 2 * len(groups)]
        w = 0
        for gi, count in enumerate(counts):
            for i in range(count):
                for j in range(3):
                    _await(land[w].at[0], sems[2 * gi].at[3 * i + j])
                    _await(land[w].at[0], sems[2 * gi + 1].at[3 * i + j])
                w += 1

    sums = [a for g in groups for a in g[1]]
    lands = [a for g in groups for a in g[2]]
    sems = [s for g in groups for s in g[0]]
    return pl.pallas_call(
        body, name=name,
        in_specs=[HBM] * (2 * n) + [SEM] * len(sems) + [ANY] * len(after),
        out_specs=[HBM] * n, out_shape=[_hbm_like(a) for a in lands],
        input_output_aliases={n + i: i for i in range(n)},
        compiler_params=SPLIT,
    )(*sums, *lands, *sems, *after)


def _small_all_reduce(part, after=()):
    _, w = part.shape

    def body(p_ref, *rest):
        o_ref, buf, send_sems, recv_sems = rest[len(after):]
        x, y, c, _ = _place()
        me = 4 * x + 2 * y + c
        buf[me] = jnp.sum(p_ref[...], axis=0, keepdims=True)
        copies = []
        for k in range(1, N_DEV):
            dx, dy, dc = (k >> 2) & 1, (k >> 1) & 1, k & 1
            copies.append(pltpu.make_async_remote_copy(
                src_ref=buf.at[me], dst_ref=buf.at[me], send_sem=send_sems.at[k - 1], recv_sem=recv_sems.at[k - 1],
                device_id=(x ^ dx, y ^ dy, c ^ dc), device_id_type=MESH))
        for cp in copies:
            cp.start()
        for cp in copies:
            cp.wait()
        tot = buf[0]
        for d in range(1, N_DEV):
            tot = tot + buf[d]
        o_ref[...] = tot
        loss = jnp.sum(tot[:, w - LANE:], axis=1, keepdims=True)
        o_ref[:, w - LANE:] = jnp.broadcast_to(loss, (1, LANE))

    return pl.pallas_call(
        body, name="small_all_reduce",
        in_specs=[IN_VMEM] + [ANY] * len(after), out_specs=IN_VMEM,
        out_shape=jax.ShapeDtypeStruct((1, w), F32),
        scratch_shapes=[pltpu.VMEM((N_DEV, 1, w), F32), pltpu.SemaphoreType.DMA((N_DEV - 1,)), pltpu.SemaphoreType.DMA((N_DEV - 1,))],
        compiler_params=pltpu.CompilerParams(vmem_limit_bytes=VMEM_LIMIT_BYTES),
    )(part, *after)


def _adamw(w, g, m, v):
    m = ADAM_B1 * m + (1.0 - ADAM_B1) * g
    v = ADAM_B2 * v + (1.0 - ADAM_B2) * (g * g)
    m_hat = m / (1.0 - ADAM_B1 ** ADAM_STEP)
    v_hat = v / (1.0 - ADAM_B2 ** ADAM_STEP)
    delta = -ADAM_LR * (m_hat / (jnp.sqrt(v_hat) + ADAM_EPS) + ADAM_WD * w)
    return delta, m, v


def _sum_adam_block(chip_ref, p_ref, own_ref, w_ref, m_ref, v_ref, g_ref, d_ref, mo_ref, vo_ref):
    g = None
    for k in range(N_CHIP):
        term = jnp.where(chip_ref[0] == k, own_ref[...], p_ref[k]).astype(F32)
        g = term if g is None else g + term
    g_ref[...] = g
    d_ref[...], mo_ref[...], vo_ref[...] = _adamw(w_ref[...], g, m_ref[...], v_ref[...])


def _sum_adam(name, parts, sums, chip, w, m, v, after=()):
    _, r, c = w.shape
    n_after = len(after)
    by_rows = r % ROWS == 0 or r < ROWS
    tr, tc = (min(ROWS, r), c) if by_rows else (r, _fit(c, 512))
    at = (lambda i: (i, 0)) if by_rows else (lambda i: (0, i))

    def body(chip_ref, p_ref, own_ref, w_ref, m_ref, v_ref, *rest):
        _sum_adam_block(chip_ref, p_ref, own_ref, w_ref, m_ref, v_ref, *rest[n_after:])

    blk = pl.BlockSpec((None, tr, tc), lambda i, chip_ref: (0, *at(i)))
    out = jax.ShapeDtypeStruct((1, r, c), F32)
    return pl.pallas_call(
        body, name=name,
        grid_spec=pltpu.PrefetchScalarGridSpec(
            num_scalar_prefetch=1, grid=(r // tr if by_rows else c // tc,),
            in_specs=[pl.BlockSpec((N_CHIP, tr, tc), lambda i, chip_ref: (0, *at(i))),
                      pl.BlockSpec((None, tr, tc), lambda i, chip_ref: (chip_ref[0], *at(i))), blk, blk, blk]
            + [ANY] * n_after,
            out_specs=[blk] * 4),
        out_shape=[out] * 4,
        compiler_params=_params("parallel"),
    )(chip, parts, sums, w, m, v, *after)


def _adam_gains(total, ws, ms, vs):
    n = len(ws)
    widths = [w.shape[1] for w in ws]

    def body(t_ref, *refs):
        w_refs, m_refs, v_refs, outs = refs[:n], refs[n:2 * n], refs[2 * n:3 * n], refs[3 * n:]
        off = 0
        for i in range(n):
            g = t_ref[:, off:off + widths[i]]
            off += widths[i]
            g_ref, d_ref, mo_ref, vo_ref = outs[4 * i:4 * i + 4]
            g_ref[...] = g
            d_ref[...], mo_ref[...], vo_ref[...] = _adamw(w_refs[i][...], g, m_refs[i][...], v_refs[i][...])

    out = pl.pallas_call(
        body, name="adam_gains",
        out_shape=[jax.ShapeDtypeStruct(w.shape, F32) for w in ws for _ in range(4)],
    )(total, *ws, *ms, *vs)
    return [tuple(out[4 * i:4 * i + 4]) for i in range(n)]


def _adam_taps(total, first_col, device, w, m, v):
    _, n_taps, cw = w.shape
    col_block = lambda t, dev: (0, first_col // cw + t * N_DEV + dev[0])
    tap = pl.BlockSpec((None, 1, cw), lambda t, dev: (t, 0, 0))

    def body(dev_ref, t_ref, w_ref, m_ref, v_ref, g_ref, d_ref, mo_ref, vo_ref):
        g = t_ref[...]
        g_ref[...] = g
        d_ref[...], mo_ref[...], vo_ref[...] = _adamw(w_ref[...], g, m_ref[...], v_ref[...])

    shape3 = (n_taps, 1, cw)
    out = pl.pallas_call(
        body, name="adam_taps",
        grid_spec=pltpu.PrefetchScalarGridSpec(
            num_scalar_prefetch=1, grid=(n_taps,),
            in_specs=[pl.BlockSpec((1, cw), col_block), tap, tap, tap], out_specs=[tap] * 4),
        out_shape=[jax.ShapeDtypeStruct(shape3, F32)] * 4,
    )(device, total, w.reshape(shape3), m.reshape(shape3), v.reshape(shape3))
    return tuple(o.reshape(w.shape) for o in out)


def kernel(x, pre_mix_g, w_in, conv_w, q_norm_g, w_uq, kv_norm_g, w_ukv, conv_out_g, attn_out_g, w_o, post_mix_g, pre_mlp_g, w_up, w_down, post_mlp_g, loss_target, m_pre_mix_g, m_w_in, m_conv_w, m_q_norm_g, m_w_uq, m_kv_norm_g, m_w_ukv, m_conv_out_g, m_attn_out_g, m_w_o, m_post_mix_g, m_pre_mlp_g, m_w_up, m_w_down, m_post_mlp_g, v_pre_mix_g, v_w_in, v_conv_w, v_q_norm_g, v_w_uq, v_kv_norm_g, v_w_ukv, v_conv_out_g, v_attn_out_g, v_w_o, v_post_mix_g, v_pre_mlp_g, v_w_up, v_w_down, v_post_mlp_g):
    me = 4 * lax.axis_index("x") + 2 * lax.axis_index("y") + lax.axis_index("c")
    core = lax.axis_index("c").astype(jnp.int32).reshape(1)
    chip = (2 * lax.axis_index("x") + lax.axis_index("y")).astype(jnp.int32).reshape(1)
    gains = (pre_mix_g, q_norm_g, kv_norm_g, conv_out_g, attn_out_g, post_mix_g, pre_mlp_g, post_mlp_g)
    gain_m = (m_pre_mix_g, m_q_norm_g, m_kv_norm_g, m_conv_out_g, m_attn_out_g, m_post_mix_g, m_pre_mlp_g, m_post_mlp_g)
    gain_v = (v_pre_mix_g, v_q_norm_g, v_kv_norm_g, v_conv_out_g, v_attn_out_g, v_post_mix_g, v_pre_mlp_g, v_post_mlp_g)
    names = ("w_in", "w_uq", "w_ukv", "w_o", "w_up", "w_down")
    big = dict(zip(names, (w_in, w_uq, w_ukv, w_o, w_up, w_down)))
    big_m = dict(zip(names, (m_w_in, m_w_uq, m_w_ukv, m_w_o, m_w_up, m_w_down)))
    big_v = dict(zip(names, (v_w_in, v_w_uq, v_w_ukv, v_w_o, v_w_up, v_w_down)))
    n_heads = attn_out_g.shape[1] // HEAD
    n_taps = conv_w.shape[1]

    gathered = ("w_in", "conv", "w_uq", "w_ukv", "w_o", "w_up", "w_down")
    gather_groups = ((0, 1), (2, 3, 4), (5,), (6,))
    taps = jnp.pad(conv_w[0], ((0, SUBLANE - n_taps), (0, 0)))
    relayed_groups = (0, 2, 3)
    sems1, shards, lands, token = _gather_start("gather_start_first", [w_in[0].astype(BF16), taps], ((0, 1),), relayed=(0,))
    sems1, shards, lands = list(sems1), list(shards), list(lands)
    behind = token[0, 0]
    rest = [(big[nm][0] + behind).astype(BF16) for nm in gathered[2:]]

    def start_more(name, some, groups, relayed, after):
        sems_b, shards_b, lands_b, _ = _gather_start(name, some, groups, relayed=relayed, after=after)
        sems1.extend(sems_b)
        shards.extend(shards_b)
        lands.extend(lands_b)

    start_rest = lambda after: start_more("gather_start_rest", rest, ((0, 1, 2), (3,), (4,)), (1, 2), after)

    cols = lambda a: jnp.concatenate([a[j] for j in range(N_DEV)], axis=1)
    rows = lambda a: a.reshape(N_DEV * a.shape[1], a.shape[2])
    device = me.astype(jnp.int32).reshape(1)
    own_in = lambda a, shard: lax.dynamic_update_index_in_dim(a, shard, me, 0)
    q_pieces = [(h, 0, HEAD) for h in range(n_heads)] + [(h, HEAD, QK) for h in range(n_heads)]
    ready = {
        "w_in": lambda a, shard: _join_col_shards("join_w_in", a, shard, device),
        "conv": lambda a, shard: cols(own_in(a, shard))[:n_taps],
        "w_uq": lambda a, shard: _join_col_shards("join_w_uq", a, shard, device, q_pieces),
        "w_ukv": lambda a, shard: cols(own_in(a, shard)),
        "w_o": lambda a, shard: rows(own_in(a, shard)),
        "w_up": own_in,
        "w_down": lambda a, shard: rows(own_in(a, shard)),
    }
    assert w_uq.shape[2] == QK

    class Weights:
        def __init__(self):
            self.passed, self.relayed = {}, {}

        def forward(self, group, after):
            idx = gather_groups[group]
            if group == 0:
                after = (*after, *rest)
            self.passed[group] = _gather_forward(f"gather_forward_{group}", [shards[i] for i in idx], [lands[i] for i in idx],
                                                 *sems1[group], after, relayed=group in relayed_groups)
            return tuple(self.passed[group][1])

        def relay(self, group, after):
            sems2, mid = self.passed[group]
            self.relayed[group], mid = _gather_relay_forward(f"gather_relay_{group}", mid, *sems2, after)
            self.passed[group] = (sems2, mid)
            if group == 0:
                start_rest(tuple(mid))
            return tuple(mid)

        def ready(self, group, after):
            sems2, mid = self.passed[group]
            full = _gather_wait(f"gather_wait_{group}", mid, *sems2, after, relay_sems=self.relayed.get(group))
            out = []
            return [ready[gathered[i]](a, shards[i]) for i, a in zip(gather_groups[group], full)]

    weights = Weights()

    col_blocks = lambda g: g.reshape(g.shape[0], N_DEV, g.shape[1] // N_DEV).transpose(1, 0, 2)
    row_blocks = lambda g: g.reshape(N_DEV, g.shape[0] // N_DEV, g.shape[1])
    grad_groups = (("w_down",), ("w_up",), ("w_o", "w_uq", "w_ukv"), ("w_in",))
    transposed = {"w_in": w_in.shape[2], "w_uq": w_uq.shape[2]}
    to_blocks = {
        "w_in": lambda g: g, "w_uq": lambda g: _unpermute_q_rows(g, n_heads),
        "w_ukv": col_blocks, "w_o": row_blocks, "w_up": lambda g: g, "w_down": row_blocks,
    }
    in_flight = []

    class Grads:
        def __init__(self):
            self.core = core
            self.away = {}

        def send_sums(self, group, sums):
            sems, sums, parts, tok = _chip_send_start(f"chip_send_start_{group}", list(sums))
            in_flight.append((sems, sums, parts))
            return (tok,)

        def full(self, group, arrays, received=None):
            nms = grad_groups[group]
            if received is None:
                blocks = [to_blocks[nm](g) for nm, g in zip(nms, arrays)]
                got = _pair_exchange(f"pair_exchange_{group}", blocks, [transposed.get(nm) for nm in nms])
            else:
                blocks, got = [self.away[group][1]], [self.received(group, received)]
            sums = [(_pair_sum_rows if nm in transposed else _pair_sum)(f"pair_sum_{nm}", g, r, core)
                    for nm, g, r in zip(nms, blocks, got)]
            return self.send_sums(group, sums)

        def send_away(self, group, half):
            nm = grad_groups[group][0]
            rows = transposed.get(nm)
            sems, src, land, tok = _pair_send_start(f"pair_send_start_{group}", half if rows is None else to_blocks[nm](half), rows)
            self.away[group] = (sems, src, land, rows)
            return (tok,)

        def received(self, group, after):
            sems, src, land, rows = self.away[group]
            return _pair_send_wait(f"pair_send_wait_{group}", sems, src, land, after, rows)

        def update_now(self, group, after):
            return update(str(group), group, group + 1, after)

    big_out = {}

    def update(tag, first, last, after):
        picked = [i for i in range(first, last) if grad_groups[i][0] not in big_out]
        groups = [in_flight[i] for i in picked]
        parts = _chip_send_wait("chip_send_wait_" + tag, groups, after)
        nms = [nm for i in picked for nm in grad_groups[i]]
        sums = [a for _, s, _ in groups for a in s]
        for nm, p, s in zip(nms, parts, sums):
            view = (lambda a: jnp.swapaxes(a, 1, 2)) if nm in transposed else (lambda a: a)
            out = _sum_adam("adam_" + nm, p, s, chip, view(big[nm]), view(big_m[nm]), view(big_v[nm]), after=after)
            after = (out[0],)
            big_out[nm] = [view(o) for o in out]
        return after

    grad_x, small = _local_step(x[0], loss_target[0], gains, weights, Grads(), first_after=(token,))

    after = update("early", 0, len(in_flight) - 1, (grad_x,))
    total = _small_all_reduce(small, after=after)
    update("late", len(in_flight) - 1, len(in_flight), (total,))
    big_out = [big_out[nm] for nm in names]

    gain_out = _adam_gains(total, gains, gain_m, gain_v)
    taps_out = _adam_taps(total, sum(g.shape[1] for g in gains), me.astype(jnp.int32).reshape(1), conv_w, m_conv_w, v_conv_w)
    loss = total[0, total.shape[1] - 1]

    order = (0, "w_in", "conv", 1, "w_uq", 2, "w_ukv", 3, 4, "w_o", 5, 6, "w_up", "w_down", 7)
    by_name = dict(zip(names, big_out))
    outs = [loss, grad_x[None]]
    for kind in range(4):
        for item in order:
            if item == "conv":
                outs.append(taps_out[kind])
            elif isinstance(item, int):
                outs.append(gain_out[item][kind])
            else:
                outs.append(by_name[item][kind])
    return tuple(outs)
```

```python
import math

import jax
import jax.numpy as jnp
from jax import lax
from jax.experimental import pallas as pl
from jax.experimental.pallas import tpu as pltpu

F32 = jnp.float32
BF16 = jnp.bfloat16

EPS = 1e-6
NEG_INF = -1e30
HEAD = 128
ROPE = 64
QK = HEAD + ROPE
CHUNK = 64
ROPE_THETA = 10000.0
ADAM_LR, ADAM_B1, ADAM_B2, ADAM_EPS, ADAM_WD, ADAM_STEP = 0.001, 0.9, 0.999, 1e-08, 0.01, 10

LANE = 128
SUBLANE = 8
VMEM_LIMIT_BYTES = 56 * 1024 * 1024

N_DEV = 8
N_CHIP = 4
MESH = pl.DeviceIdType.MESH


def _params(*sem):
    return pltpu.CompilerParams(dimension_semantics=sem, vmem_limit_bytes=VMEM_LIMIT_BYTES)


ANY = pl.BlockSpec(memory_space=pl.ANY)


def _call(body, *, in_specs, after=(), **kw):
    n_in, n_after = len(in_specs), len(after)

    def ordered(*refs):
        body(*refs[:n_in], *refs[n_in + n_after:])

    call = pl.pallas_call(ordered, in_specs=[*in_specs, *[ANY] * n_after], **kw)
    return lambda *operands: call(*operands, *after)


def _sublane_sum(v):
    r, w = v.shape
    return jnp.sum(v.reshape(r // SUBLANE, SUBLANE, w), axis=0)


def _rstd(x):
    return lax.rsqrt(jnp.mean(x * x, axis=-1, keepdims=True) + EPS)


def _rms_bwd(x, g, dy):
    r = _rstd(x)
    xh = x * r
    dxh = dy * g
    dx = r * (dxh - xh * jnp.mean(dxh * xh, axis=-1, keepdims=True))
    return dx, dy * xh


def _accumulate(ref, val, step):
    @pl.when(step == 0)
    def _():
        ref[...] = val

    @pl.when(step > 0)
    def _():
        ref[...] += val


NN = ((1,), (0,))
NT = ((1,), (1,))
TN = ((0,), (0,))


def _matmul(name, a, b, *, grid, a_spec, b_spec, out_shape, out_specs, contract, nk=1, acc_shape=None,
            extras=(), extra_specs=(), epilogue=None, after=()):
    multi = isinstance(out_shape, (tuple, list))
    out_shapes = tuple(out_shape) if multi else (out_shape,)
    n_out = len(out_shapes)
    n_extra = len(extras)

    def body(a_ref, b_ref, *rest):
        x_refs = rest[:n_extra]
        o_refs = rest[n_extra:n_extra + n_out]

        def emit(acc):
            vals = epilogue(acc, *[r[...] for r in x_refs]) if epilogue else (acc,)
            for r, v in zip(o_refs, vals):
                r[...] = v.astype(r.dtype)

        p = lax.dot_general(a_ref[...], b_ref[...], (contract, ((), ())), preferred_element_type=F32)
        if nk == 1:
            emit(p)
        else:
            acc_ref = rest[n_extra + n_out]
            k = pl.program_id(2)
            _accumulate(acc_ref, p, k)

            @pl.when(k == nk - 1)
            def _():
                emit(acc_ref[...])

    sem = ("parallel", "parallel") + (("arbitrary",) if nk > 1 else ())
    return _call(
        body, name=name, grid=grid, after=after,
        in_specs=[a_spec, b_spec, *extra_specs],
        out_specs=out_specs,
        out_shape=out_shape,
        scratch_shapes=[pltpu.VMEM(acc_shape, F32)] if nk > 1 else [],
        compiler_params=_params(*sem),
    )(a, b, *extras)


def _fit(n, tile):
    if n <= tile:
        return n
    t = tile - tile % LANE
    while n % t:
        t -= LANE
    return t


def _mm_nn(name, a, b, out_dtype, tm, tn, after=()):
    m, k = a.shape
    n = b.shape[1]
    tm, tn = _fit(m, tm), _fit(n, tn)
    return _matmul(name, a, b, grid=(m // tm, n // tn), after=after,
                   a_spec=pl.BlockSpec((tm, k), lambda i, j: (i, 0)),
                   b_spec=pl.BlockSpec((k, tn), lambda i, j: (0, j)),
                   out_shape=jax.ShapeDtypeStruct((m, n), out_dtype),
                   out_specs=pl.BlockSpec((tm, tn), lambda i, j: (i, j)), contract=NN)


def _mm_nt(name, a, b, out_dtype, tm, tn, after=()):
    m, k = a.shape
    n = b.shape[0]
    tm, tn = _fit(m, tm), _fit(n, tn)
    return _matmul(name, a, b, grid=(m // tm, n // tn), after=after,
                   a_spec=pl.BlockSpec((tm, k), lambda i, j: (i, 0)),
                   b_spec=pl.BlockSpec((tn, k), lambda i, j: (j, 0)),
                   out_shape=jax.ShapeDtypeStruct((m, n), out_dtype),
                   out_specs=pl.BlockSpec((tm, tn), lambda i, j: (i, j)), contract=NT)


def _mm_tn(name, a, b, out_dtype, tm, tn):
    s, m = a.shape
    n = b.shape[1]
    tm, tn = _fit(m, tm), _fit(n, tn)
    return _matmul(name, a, b, grid=(m // tm, n // tn),
                   a_spec=pl.BlockSpec((s, tm), lambda i, j: (0, i)),
                   b_spec=pl.BlockSpec((s, tn), lambda i, j: (0, j)),
                   out_shape=jax.ShapeDtypeStruct((m, n), out_dtype),
                   out_specs=pl.BlockSpec((tm, tn), lambda i, j: (i, j)), contract=TN)


ROWS = 256


def _row_spec(rows, width):
    return pl.BlockSpec((rows, width), lambda i: (i, 0))


def _fixed_spec(rows, width):
    return pl.BlockSpec((rows, width), lambda i: (0, 0))


def _column_pieces(rows, start, width):
    piece = math.gcd(start, width)
    assert piece % LANE == 0
    return [pl.BlockSpec((rows, piece), lambda i, b=start // piece + p: (i, b)) for p in range(width // piece)]


def _rms_fwd(name, x, g, cols=None, after=()):
    s = x.shape[0]
    start, w = cols or (0, x.shape[1])
    rows = min(ROWS, s)
    pieces = _column_pieces(rows, start, w) if cols else [_row_spec(rows, w)]
    n = len(pieces)

    def body(*refs):
        g_ref, o_ref = refs[n:]
        xv = refs[0][...] if n == 1 else jnp.concatenate([r[...] for r in refs[:n]], axis=1)
        o_ref[...] = (xv * _rstd(xv) * g_ref[...]).astype(o_ref.dtype)

    return _call(
        body, name=name, grid=(s // rows,), after=after,
        in_specs=[*pieces, _fixed_spec(1, w)],
        out_specs=_row_spec(rows, w),
        out_shape=jax.ShapeDtypeStruct((s, w), BF16),
        compiler_params=_params("parallel"),
    )(*[x] * n, g)


def _rms_bwd_call(name, x, g, dy, out_dtype, cols=None, after=()):
    s = x.shape[0]
    start, w = cols or (0, x.shape[1])
    rows = min(ROWS, s)
    pieces = _column_pieces(rows, start, w) if cols else [_row_spec(rows, w)]
    n = len(pieces)

    def body(*refs):
        g_ref, dy_ref, dx_ref, dg_ref = refs[n:]
        xv = refs[0][...] if n == 1 else jnp.concatenate([r[...] for r in refs[:n]], axis=1)
        dx, dgc = _rms_bwd(xv, g_ref[...], dy_ref[...].astype(F32))
        dx_ref[...] = dx.astype(dx_ref.dtype)
        _accumulate(dg_ref, _sublane_sum(dgc), pl.program_id(0))

    return _call(
        body, name=name, grid=(s // rows,), after=after,
        in_specs=[*pieces, _fixed_spec(1, w), _row_spec(rows, w)],
        out_specs=[_row_spec(rows, w), _fixed_spec(SUBLANE, w)],
        out_shape=[jax.ShapeDtypeStruct((s, w), out_dtype), jax.ShapeDtypeStruct((SUBLANE, w), F32)],
        compiler_params=_params("arbitrary"),
    )(*[x] * n, g, dy)


def _norm_up(name, x, cols, g, w, after=()):
    s = x.shape[0]
    start, width = cols
    n = w.shape[1]
    tm = min(TILE_M, s)
    pieces = _column_pieces(tm, start, width)
    n_p = len(pieces)

    def body(*refs):
        g_ref, w_ref, xn_ref, o_ref = refs[n_p:]
        xv = refs[0][...] if n_p == 1 else jnp.concatenate([r[...] for r in refs[:n_p]], axis=1)
        xn = (xv * _rstd(xv) * g_ref[...]).astype(BF16)
        xn_ref[...] = xn
        o_ref[...] = jnp.dot(xn, w_ref[...], preferred_element_type=F32)

    return _call(
        body, name=name, grid=(s // tm,), after=after,
        in_specs=[*pieces, _fixed_spec(1, width), _fixed_spec(width, n)],
        out_specs=[_row_spec(tm, width), _row_spec(tm, n)],
        out_shape=[jax.ShapeDtypeStruct((s, width), BF16), jax.ShapeDtypeStruct((s, n), F32)],
        compiler_params=_params("parallel"),
    )(*[x] * n_p, g, w)


def _up_norm_bwd(name, dy, w, x, cols, g, after=()):
    s, n = dy.shape
    start, width = cols
    tm = min(TILE_M, s)
    pieces = _column_pieces(tm, start, width)
    n_p = len(pieces)

    def body(dy_ref, w_ref, *refs):
        g_ref, dx_ref, dg_ref = refs[n_p:]
        xv = refs[0][...] if n_p == 1 else jnp.concatenate([r[...] for r in refs[:n_p]], axis=1)
        dxn = lax.dot_general(dy_ref[...], w_ref[...], (NT, ((), ())), preferred_element_type=F32)
        dx, dgc = _rms_bwd(xv, g_ref[...], dxn)
        dx_ref[...] = dx.astype(dx_ref.dtype)
        _accumulate(dg_ref, _sublane_sum(dgc), pl.program_id(0))

    return _call(
        body, name=name, grid=(s // tm,), after=after,
        in_specs=[_row_spec(tm, n), _fixed_spec(width, n), *pieces, _fixed_spec(1, width)],
        out_specs=[_row_spec(tm, width), _fixed_spec(SUBLANE, width)],
        out_shape=[jax.ShapeDtypeStruct((s, width), BF16), jax.ShapeDtypeStruct((SUBLANE, width), F32)],
        compiler_params=_params("arbitrary"),
    )(dy, w, *[x] * n_p, g)


def _mid_fwd(x, y, g_post, g_pre, after=()):
    s, w = x.shape
    rows = min(ROWS, s)

    def body(x_ref, y_ref, gp_ref, gq_ref, x2_ref, h2_ref):
        yv = y_ref[...]
        x2 = x_ref[...] + yv * _rstd(yv) * gp_ref[...]
        x2_ref[...] = x2
        h2_ref[...] = (x2 * _rstd(x2) * gq_ref[...]).astype(h2_ref.dtype)

    return _call(
        body, name="mid_fwd", grid=(s // rows,), after=after,
        in_specs=[_row_spec(rows, w), _row_spec(rows, w), _fixed_spec(1, w), _fixed_spec(1, w)],
        out_specs=[_row_spec(rows, w), _row_spec(rows, w)],
        out_shape=[jax.ShapeDtypeStruct((s, w), F32), jax.ShapeDtypeStruct((s, w), BF16)],
        compiler_params=_params("parallel"),
    )(x, y, g_post, g_pre)


def _head(m, x2, tgt, g):
    s, w = m.shape
    rows = min(ROWS, s)

    def body(m_ref, x2_ref, t_ref, g_ref, dout_ref, dm_ref, dg_ref, loss_ref):
        mv = m_ref[...]
        gv = g_ref[...]
        out = x2_ref[...] + mv * _rstd(mv) * gv
        err = out - t_ref[...]
        dout = err * (1.0 / w)
        dout_ref[...] = dout
        dm, dgc = _rms_bwd(mv, gv, dout)
        dm_ref[...] = dm.astype(dm_ref.dtype)
        sq = err * err
        lanes = sq[:, 0:LANE]
        for j in range(1, w // LANE):
            lanes = lanes + sq[:, j * LANE:(j + 1) * LANE]
        step = pl.program_id(0)
        _accumulate(dg_ref, _sublane_sum(dgc), step)
        _accumulate(loss_ref, _sublane_sum(lanes) * (0.5 / w), step)

    return pl.pallas_call(
        body, name="head", grid=(s // rows,),
        in_specs=[_row_spec(rows, w), _row_spec(rows, w), _row_spec(rows, w), _fixed_spec(1, w)],
        out_specs=[_row_spec(rows, w), _row_spec(rows, w), _fixed_spec(SUBLANE, w), _fixed_spec(SUBLANE, LANE)],
        out_shape=[jax.ShapeDtypeStruct((s, w), F32), jax.ShapeDtypeStruct((s, w), BF16),
                   jax.ShapeDtypeStruct((SUBLANE, w), F32), jax.ShapeDtypeStruct((SUBLANE, LANE), F32)],
        compiler_params=_params("arbitrary"),
    )(m, x2, tgt, g)


def _mid_bwd(x2, y, d_out, d_h2, g_pre, g_post, after=()):
    s, w = x2.shape
    rows = min(ROWS, s)

    def body(x2_ref, y_ref, dout_ref, dh2_ref, gq_ref, gp_ref, dx2_ref, dy_ref, dgq_ref, dgp_ref):
        dx, dgq = _rms_bwd(x2_ref[...], gq_ref[...], dh2_ref[...])
        dx2 = dout_ref[...] + dx
        dx2_ref[...] = dx2
        dy, dgp = _rms_bwd(y_ref[...], gp_ref[...], dx2)
        dy_ref[...] = dy.astype(dy_ref.dtype)
        step = pl.program_id(0)
        _accumulate(dgq_ref, _sublane_sum(dgq), step)
        _accumulate(dgp_ref, _sublane_sum(dgp), step)

    return _call(
        body, name="mid_bwd", grid=(s // rows,), after=after,
        in_specs=[_row_spec(rows, w)] * 4 + [_fixed_spec(1, w)] * 2,
        out_specs=[_row_spec(rows, w), _row_spec(rows, w), _fixed_spec(SUBLANE, w), _fixed_spec(SUBLANE, w)],
        out_shape=[jax.ShapeDtypeStruct((s, w), F32), jax.ShapeDtypeStruct((s, w), BF16),
                   jax.ShapeDtypeStruct((SUBLANE, w), F32), jax.ShapeDtypeStruct((SUBLANE, w), F32)],
        compiler_params=_params("arbitrary"),
    )(x2, y, d_out, d_h2, g_pre, g_post)


def _first_bwd(x, g, d_h1, d_x2, after=()):
    s, w = x.shape
    rows = min(ROWS, s)

    def body(x_ref, g_ref, dh_ref, dx2_ref, dx_ref, dg_ref):
        dx, dgc = _rms_bwd(x_ref[...], g_ref[...], dh_ref[...])
        dx_ref[...] = dx2_ref[...] + dx
        _accumulate(dg_ref, _sublane_sum(dgc), pl.program_id(0))

    return _call(
        body, name="first_bwd", grid=(s // rows,), after=after,
        in_specs=[_row_spec(rows, w), _fixed_spec(1, w), _row_spec(rows, w), _row_spec(rows, w)],
        out_specs=[_row_spec(rows, w), _fixed_spec(SUBLANE, w)],
        out_shape=[jax.ShapeDtypeStruct((s, w), F32), jax.ShapeDtypeStruct((SUBLANE, w), F32)],
        compiler_params=_params("arbitrary"),
    )(x, g, d_h1, d_x2)


def _shift_down(v, k):
    t = lax.broadcasted_iota(jnp.int32, v.shape, 0)
    return jnp.where(t >= k, pltpu.roll(v, k, 0), 0.0)


def _shift_up(v, k):
    n = v.shape[0]
    t = lax.broadcasted_iota(jnp.int32, v.shape, 0)
    return jnp.where(t < n - k, pltpu.roll(v, n - k, 0), 0.0)


def _conv_core(u, b, c, w):
    z = c * u
    conv = w[0:1, :] * _shift_down(z, 2) + w[1:2, :] * _shift_down(z, 1) + w[2:3, :] * z
    return z, conv, b * conv


def _conv_fwd(proj, conv_w, g, n_groups, out_width, after=()):
    s = proj.shape[0]

    def body(u_ref, b_ref, c_ref, w_ref, g_ref, o_ref):
        _, _, yr = _conv_core(u_ref[...], b_ref[...], c_ref[...], w_ref[...])
        o_ref[...] = (yr * _rstd(yr) * g_ref[...]).astype(o_ref.dtype)

    col = lambda k: pl.BlockSpec((s, HEAD), lambda i: (0, k * n_groups + i))
    return _call(
        body, name="conv_fwd", grid=(n_groups,), after=after,
        in_specs=[col(0), col(1), col(2), pl.BlockSpec((3, HEAD), lambda i: (0, i)), pl.BlockSpec((1, HEAD), lambda i: (0, i))],
        out_specs=pl.BlockSpec((s, HEAD), lambda i: (0, i)),
        out_shape=jax.ShapeDtypeStruct((s, out_width), BF16),
        compiler_params=_params("parallel"),
    )(proj, proj, proj, conv_w, g)


def _conv_bwd(proj, d_mix, conv_w, g, n_groups):
    s = proj.shape[0]
    width = n_groups * HEAD

    def body(u_ref, b_ref, c_ref, dy_ref, w_ref, g_ref, du_ref, db_ref, dc_ref, dg_ref, dw_ref):
        u, b, c, w = u_ref[...], b_ref[...], c_ref[...], w_ref[...]
        z, conv, yr = _conv_core(u, b, c, w)
        dyr, dgc = _rms_bwd(yr, g_ref[...], dy_ref[...])
        dconv = dyr * b
        db_ref[...] = (dyr * conv).astype(db_ref.dtype)
        dz = w[2:3, :] * dconv + w[1:2, :] * _shift_up(dconv, 1) + w[0:1, :] * _shift_up(dconv, 2)
        dc_ref[...] = (dz * u).astype(dc_ref.dtype)
        du_ref[...] = (dz * c).astype(du_ref.dtype)
        dg_ref[...] = _sublane_sum(dgc)
        dw_ref[0] = _sublane_sum(dconv * _shift_down(z, 2))
        dw_ref[1] = _sublane_sum(dconv * _shift_down(z, 1))
        dw_ref[2] = _sublane_sum(dconv * z)

    col = lambda k: pl.BlockSpec((s, HEAD), lambda i: (0, k * n_groups + i))
    grp = pl.BlockSpec((s, HEAD), lambda i: (0, i))
    return pl.pallas_call(
        body, name="conv_bwd", grid=(n_groups,),
        in_specs=[col(0), col(1), col(2), grp, pl.BlockSpec((3, HEAD), lambda i: (0, i)), pl.BlockSpec((1, HEAD), lambda i: (0, i))],
        out_specs=[grp, grp, grp, pl.BlockSpec((SUBLANE, HEAD), lambda i: (0, i)),
                   pl.BlockSpec((3, SUBLANE, HEAD), lambda i: (0, 0, i))],
        out_shape=[jax.ShapeDtypeStruct((s, width), BF16)] * 3
        + [jax.ShapeDtypeStruct((SUBLANE, width), F32), jax.ShapeDtypeStruct((3, SUBLANE, width), F32)],
        compiler_params=_params("parallel"),
    )(proj, proj, proj, d_mix, conv_w, g)


def _rope_tables(s, n_heads):
    pos = jnp.arange(s, dtype=F32)
    inv_freq = jnp.power(ROPE_THETA, -jnp.arange(0, ROPE, 2, dtype=F32) / ROPE)
    ang = pos[:, None] * inv_freq[None, :]
    cos, sin = jnp.cos(ang), jnp.sin(ang)
    cs = jnp.concatenate([cos, cos], axis=1)
    sn = jnp.concatenate([-sin, sin], axis=1)
    pad = jnp.zeros((s, LANE - ROPE), F32)
    return (jnp.tile(cs, (1, n_heads)), jnp.tile(sn, (1, n_heads)),
            jnp.concatenate([cs, pad], axis=1), jnp.concatenate([sn, pad], axis=1))


def _swap_halves(v):
    w = v.shape[1]
    lane = lax.broadcasted_iota(jnp.int32, v.shape, 1)
    first = (lane % ROPE) < (ROPE // 2)
    return jnp.where(first, pltpu.roll(v, w - ROPE // 2, 1), pltpu.roll(v, ROPE // 2, 1))


def _pack_heads(q, kv, proj, kr_col, tables, n_heads, after=()):
    s = q.shape[0]
    rows = min(ROWS, s)
    cq, sq, ck, sk = tables
    wq = n_heads * ROPE

    def body(q_ref, kv_ref, kr_ref, cq_ref, sq_ref, ck_ref, sk_ref, qo_ref, ko_ref, vo_ref):
        qr = q_ref[:, n_heads * HEAD:]
        qr = qr * cq_ref[...] + _swap_halves(qr) * sq_ref[...]
        krv = kr_ref[...]
        krv = krv * ck_ref[...] + _swap_halves(krv) * sk_ref[...]
        for h in range(n_heads):
            qo_ref[h] = jnp.concatenate([q_ref[:, h * HEAD:(h + 1) * HEAD], qr[:, h * ROPE:(h + 1) * ROPE]], axis=1).astype(BF16)
            ko_ref[h] = jnp.concatenate([kv_ref[:, 2 * h * HEAD:(2 * h + 1) * HEAD], krv[:, :ROPE]], axis=1).astype(BF16)
            vo_ref[h] = kv_ref[:, (2 * h + 1) * HEAD:(2 * h + 2) * HEAD].astype(BF16)

    hs = lambda w: pl.BlockSpec((n_heads, rows, w), lambda i: (0, i, 0))
    return _call(
        body, name="pack_heads", grid=(s // rows,), after=after,
        in_specs=[_row_spec(rows, q.shape[1]), _row_spec(rows, kv.shape[1]), pl.BlockSpec((rows, LANE), lambda i: (i, kr_col // LANE)),
                  _row_spec(rows, wq), _row_spec(rows, wq), _row_spec(rows, LANE), _row_spec(rows, LANE)],
        out_specs=[hs(QK), hs(QK), hs(HEAD)],
        out_shape=[jax.ShapeDtypeStruct((n_heads, s, QK), BF16), jax.ShapeDtypeStruct((n_heads, s, QK), BF16),
                   jax.ShapeDtypeStruct((n_heads, s, HEAD), BF16)],
        compiler_params=_params("parallel"),
    )(q, kv, proj, cq, sq, ck, sk)


def _unpack_heads(dq, dk, dv, tables, n_heads):
    s = dq.shape[1]
    rows = min(ROWS, s)
    cq, sq, ck, sk = tables
    wq = n_heads * ROPE

    def body(dq_ref, dk_ref, dv_ref, cq_ref, sq_ref, ck_ref, sk_ref, qo_ref, kvo_ref, kro_ref):
        dqr = jnp.concatenate([dq_ref[h][:, HEAD:] for h in range(n_heads)], axis=1)
        dqr = dqr * cq_ref[...] - _swap_halves(dqr) * sq_ref[...]
        dkr = dk_ref[0][:, HEAD:]
        for h in range(1, n_heads):
            dkr = dkr + dk_ref[h][:, HEAD:]
        dkr = jnp.concatenate([dkr, jnp.zeros((rows, LANE - ROPE), F32)], axis=1)
        dkr = dkr * ck_ref[...] - _swap_halves(dkr) * sk_ref[...]
        kro_ref[...] = dkr.astype(kro_ref.dtype)
        qo_ref[:, n_heads * HEAD:] = dqr.astype(qo_ref.dtype)
        for h in range(n_heads):
            qo_ref[:, h * HEAD:(h + 1) * HEAD] = dq_ref[h][:, :HEAD].astype(qo_ref.dtype)
            kvo_ref[:, 2 * h * HEAD:(2 * h + 1) * HEAD] = dk_ref[h][:, :HEAD].astype(kvo_ref.dtype)
            kvo_ref[:, (2 * h + 1) * HEAD:(2 * h + 2) * HEAD] = dv_ref[h].astype(kvo_ref.dtype)

    hs = lambda w: pl.BlockSpec((n_heads, rows, w), lambda i: (0, i, 0))
    return pl.pallas_call(
        body, name="unpack_heads", grid=(s // rows,),
        in_specs=[hs(QK), hs(QK), hs(HEAD), _row_spec(rows, wq), _row_spec(rows, wq), _row_spec(rows, LANE), _row_spec(rows, LANE)],
        out_specs=[_row_spec(rows, n_heads * QK), _row_spec(rows, 2 * n_heads * HEAD), _row_spec(rows, LANE)],
        out_shape=[jax.ShapeDtypeStruct((s, n_heads * QK), BF16), jax.ShapeDtypeStruct((s, 2 * n_heads * HEAD), BF16),
                   jax.ShapeDtypeStruct((s, LANE), BF16)],
        compiler_params=_params("parallel"),
    )(dq, dk, dv, cq, sq, ck, sk)


TQ = 256


LOG2_E = 1.4426950408889634


def _softmax_parts(q, k):
    tq, n_keys = q.shape[0], k.shape[0]
    sc = lax.dot_general(q, k, (NT, ((), ())), preferred_element_type=F32) * (QK ** -0.5 * LOG2_E)
    row = lax.broadcasted_iota(jnp.int32, (tq, tq), 0)
    col = lax.broadcasted_iota(jnp.int32, (tq, tq), 1)
    own = jnp.where(col // CHUNK <= row // CHUNK, sc[:, n_keys - tq:], NEG_INF)
    sc = own if n_keys == tq else jnp.concatenate([sc[:, :n_keys - tq], own], axis=1)
    e = jnp.exp2(sc - jnp.max(sc, axis=-1, keepdims=True))
    return e, 1.0 / jnp.sum(e, axis=-1, keepdims=True)


def _attn_fwd(q, k, v, g, mix, col0, after=()):
    n_heads, s, _ = q.shape
    tq = min(TQ, s)
    assert tq % CHUNK == 0 and s % tq == 0

    def body(q_ref, k_ref, v_ref, g_ref, mix_ref, o_ref, y_ref):
        for c in range(s // tq):
            rows, n_keys = pl.ds(c * tq, tq), (c + 1) * tq
            e, inv = _softmax_parts(q_ref[rows, :], k_ref[0:n_keys, :])
            o = jnp.dot(e.astype(BF16), v_ref[0:n_keys, :], preferred_element_type=F32) * inv
            o_ref[rows, :] = o
            y_ref[rows, :] = (o * _rstd(o) * g_ref[...]).astype(y_ref.dtype)

    head = lambda w: pl.BlockSpec((None, s, w), lambda h: (h, 0, 0))
    return _call(
        body, name="attn_fwd", grid=(n_heads,), after=after,
        in_specs=[head(QK), head(QK), head(HEAD), pl.BlockSpec((1, HEAD), lambda h: (0, h)), ANY],
        out_specs=[head(HEAD), pl.BlockSpec((s, HEAD), lambda h: (0, col0 // HEAD + h))],
        out_shape=[jax.ShapeDtypeStruct((n_heads, s, HEAD), F32), jax.ShapeDtypeStruct(mix.shape, mix.dtype)],
        input_output_aliases={4: 1},
        compiler_params=_params("parallel"),
    )(q, k, v, g, mix)


def _attn_bwd(q, k, v, o, d_mix, g, col0, after=()):
    n_heads, s, _ = q.shape
    tq = min(TQ, s)

    def body(q_ref, k_ref, v_ref, o_ref, dy_ref, g_ref, dq_ref, dk_ref, dv_ref, dg_ref):
        dg = None
        for c in reversed(range(s // tq)):
            rows, n_keys = pl.ds(c * tq, tq), (c + 1) * tq
            qv, kv_, vv = q_ref[rows, :], k_ref[0:n_keys, :], v_ref[0:n_keys, :]
            do, dgc = _rms_bwd(o_ref[rows, :], g_ref[...], dy_ref[rows, :])
            do = do.astype(BF16)
            dg = _sublane_sum(dgc) if dg is None else dg + _sublane_sum(dgc)
            e, inv = _softmax_parts(qv, kv_)
            p = e * inv
            dp = lax.dot_general(do, vv, (NT, ((), ())), preferred_element_type=F32)
            ds = (p * (dp - jnp.sum(p * dp, axis=-1, keepdims=True)) * (QK ** -0.5)).astype(BF16)
            dq_ref[rows, :] = jnp.dot(ds, kv_, preferred_element_type=F32)
            dk = lax.dot_general(ds, qv, (TN, ((), ())), preferred_element_type=F32)
            dv = lax.dot_general(p.astype(BF16), do, (TN, ((), ())), preferred_element_type=F32)
            if n_keys == s:
                dk_ref[...] = dk
                dv_ref[...] = dv
            else:
                dk_ref[0:n_keys, :] += dk
                dv_ref[0:n_keys, :] += dv
        dg_ref[...] = dg

    c0 = col0 // HEAD
    head = lambda w: pl.BlockSpec((None, s, w), lambda h, *_: (h, 0, 0))
    in_specs = [head(QK), head(QK), head(HEAD), head(HEAD), pl.BlockSpec((s, HEAD), lambda h, *_: (0, c0 + h)),
                pl.BlockSpec((1, HEAD), lambda h, *_: (0, h))]
    out_specs = [head(QK), head(QK), head(HEAD), pl.BlockSpec((SUBLANE, HEAD), lambda h, *_: (0, h))]
    out_shape = [jax.ShapeDtypeStruct((n_heads, s, QK), F32), jax.ShapeDtypeStruct((n_heads, s, QK), F32),
                 jax.ShapeDtypeStruct((n_heads, s, HEAD), F32), jax.ShapeDtypeStruct((SUBLANE, n_heads * HEAD), F32)]
    return _call(body, name="attn_bwd", grid=(n_heads,), after=after, in_specs=in_specs, out_specs=out_specs,
                 out_shape=out_shape, compiler_params=_params("parallel"))(q, k, v, o, d_mix, g)


TILE_M = 1024
TILE_N = 1024


def _up_fwd(h2, w_up):
    s, d = h2.shape
    nb, _, fb = w_up.shape
    tm = min(TILE_M,s)

    def epilogue(acc):
        r = jnp.maximum(acc, 0.0)
        return r * r, r

    blk = pl.BlockSpec((tm, fb), lambda i, j: (i, j))
    return _matmul("up_fwd", h2, w_up, grid=(s // tm, nb),
                   a_spec=pl.BlockSpec((tm, d), lambda i, j: (i, 0)),
                   b_spec=pl.BlockSpec((None, d, fb), lambda i, j: (j, 0, 0)),
                   out_shape=[jax.ShapeDtypeStruct((s, nb * fb), BF16)] * 2, out_specs=[blk, blk],
                   contract=NN, epilogue=epilogue)


def _down_fwd(a, w_down):
    s, f = a.shape
    d = w_down.shape[1]
    tm, tn, tk = min(TILE_M,s), min(TILE_N,d), 2048
    nk = f // tk
    return _matmul("down_fwd", a, w_down, grid=(s // tm, d // tn, nk),
                   a_spec=pl.BlockSpec((tm, tk), lambda i, j, k: (i, k)),
                   b_spec=pl.BlockSpec((tk, tn), lambda i, j, k: (k, j)),
                   out_shape=jax.ShapeDtypeStruct((s, d), F32),
                   out_specs=pl.BlockSpec((tm, tn), lambda i, j, k: (i, j)),
                   contract=NN, nk=nk, acc_shape=(tm, tn))


def _down_bwd_act(d_m, w_down, r, after=()):
    s, d = d_m.shape
    f = w_down.shape[0]
    tm, tn = min(TILE_M,s), min(TILE_N,f)
    blk = pl.BlockSpec((tm, tn), lambda i, j: (i, j))
    return _matmul("down_bwd_act", d_m, w_down, grid=(s // tm, f // tn), after=after,
                   a_spec=pl.BlockSpec((tm, d), lambda i, j: (i, 0)),
                   b_spec=pl.BlockSpec((tn, d), lambda i, j: (j, 0)),
                   out_shape=jax.ShapeDtypeStruct((s, f), BF16), out_specs=blk, contract=NT,
                   extras=(r,), extra_specs=(blk,),
                   epilogue=lambda acc, rv: (acc * (2.0 * rv.astype(F32)),))


def _up_bwd_act(d_up, w_up, after=()):
    s, _ = d_up.shape
    nb, d, fb = w_up.shape
    tm, tn = min(TILE_M, s), min(TILE_N,d)
    pair = 2
    n_after = len(after)

    def body(a_ref, w_ref, *rest):
        o_ref, acc_ref = rest[n_after:]
        k = pl.program_id(2)
        p = None
        for t in range(pair):
            term = lax.dot_general(a_ref[:, t * fb:(t + 1) * fb], w_ref[t], (NT, ((), ())), preferred_element_type=F32)
            p = term if p is None else p + term
        _accumulate(acc_ref, p, k)

        @pl.when(k == nb // pair - 1)
        def _():
            o_ref[...] = acc_ref[...]

    return pl.pallas_call(
        body, name="up_bwd_act", grid=(s // tm, d // tn, nb // pair),
        in_specs=[pl.BlockSpec((tm, pair * fb), lambda i, j, k: (i, k)),
                  pl.BlockSpec((pair, tn, fb), lambda i, j, k: (k, j, 0))] + [ANY] * n_after,
        out_specs=pl.BlockSpec((tm, tn), lambda i, j, k: (i, j)),
        out_shape=jax.ShapeDtypeStruct((s, d), F32),
        scratch_shapes=[pltpu.VMEM((tm, tn), F32)],
        compiler_params=_params("parallel", "parallel", "arbitrary"),
    )(d_up, w_up, *after)


def _half_grad(name, a, b, core, home, received, after, *, grid, a_block, a_map, b_block, b_map, o_block, o_map, out_shape):
    n_after = len(after)
    pick = (lambda ref: ref[0]) if home else (lambda ref: 1 - ref[0])

    def body(core_ref, a_ref, b_ref, *rest):
        acc = lax.dot_general(a_ref[...], b_ref[...], (TN, ((), ())), preferred_element_type=F32)
        if received is not None:
            acc = acc + rest[0][...].astype(F32)
        rest[-1][...] = acc.astype(rest[-1].dtype)

    wrap = lambda fn: (lambda i, j, core_ref: fn(i, j, pick(core_ref)))
    o_spec = pl.BlockSpec(o_block, wrap(o_map))
    extra = [] if received is None else [o_spec]
    operands = [] if received is None else [received]
    return pl.pallas_call(
        body, name=name,
        grid_spec=pltpu.PrefetchScalarGridSpec(
            num_scalar_prefetch=1, grid=grid,
            in_specs=[pl.BlockSpec(a_block, wrap(a_map)), pl.BlockSpec(b_block, wrap(b_map))] + extra + [ANY] * n_after,
            out_specs=o_spec),
        out_shape=out_shape,
        compiler_params=_params("parallel", "parallel"),
    )(core, a, b, *operands, *after)


def _down_half_grad(name, a, d_m, core, home, received=None, after=()):
    s, f = a.shape
    d = d_m.shape[1]
    r = f // N_DEV
    tn = min(TILE_N, d)
    return _half_grad(name, a, d_m, core, home, received, after, grid=(N_CHIP, d // tn),
                      a_block=(s, r), a_map=lambda k, j, p: (0, 2 * k + p),
                      b_block=(s, tn), b_map=lambda k, j, p: (0, j),
                      o_block=(None, r, tn), o_map=lambda k, j, p: (k, 0, j),
                      out_shape=jax.ShapeDtypeStruct((N_CHIP, r, d), BF16))


def _up_half_grad(name, h2, d_up, core, home, received=None, after=()):
    s, d = h2.shape
    fb = d_up.shape[1] // N_DEV
    tm = min(TILE_M, d)
    return _half_grad(name, h2, d_up, core, home, received, after, grid=(d // tm, N_CHIP),
                      a_block=(s, tm), a_map=lambda i, k, p: (0, i),
                      b_block=(s, fb), b_map=lambda i, k, p: (0, 2 * k + p),
                      o_block=(None, tm, fb), o_map=lambda i, k, p: (k, i, 0),
                      out_shape=jax.ShapeDtypeStruct((N_CHIP, d, fb), BF16))


MXU_WIDTH = 256


def _in_pad(in_width):
    return -(-in_width // MXU_WIDTH) * MXU_WIDTH


def _join_col_shards(name, blocks, own, device, pieces=None):
    n, r, w = blocks.shape
    rows = min(ROWS, r)
    pieces = pieces or [(j, 0, w) for j in range(n)]
    used = sum(b - a for _, a, b in pieces)
    width = _in_pad(used)

    def body(dev_ref, x_ref, own_ref, o_ref):
        block = lambda j: jnp.where(dev_ref[0] == j, own_ref[...], x_ref[j])
        cols = [block(j)[:, a:b] for j, a, b in pieces]
        tail = [jnp.zeros((rows, width - used), o_ref.dtype)] if width > used else []
        o_ref[...] = jnp.concatenate(cols + tail, axis=1)

    return pl.pallas_call(
        body, name=name,
        grid_spec=pltpu.PrefetchScalarGridSpec(
            num_scalar_prefetch=1, grid=(r // rows,),
            in_specs=[pl.BlockSpec((n, rows, w), lambda i, dev: (0, i, 0)), pl.BlockSpec((rows, w), lambda i, dev: (i, 0))],
            out_specs=pl.BlockSpec((rows, width), lambda i, dev: (i, 0))),
        out_shape=jax.ShapeDtypeStruct((r, width), blocks.dtype),
        compiler_params=_params("parallel"),
    )(device, blocks, own)


def _permute_q_cols(w_uq, n_heads):
    r = w_uq.shape[0]
    w3 = w_uq.reshape(r, n_heads, QK)
    return jnp.concatenate([w3[:, :, :HEAD].reshape(r, n_heads * HEAD), w3[:, :, HEAD:].reshape(r, n_heads * ROPE)], axis=1)


def _unpermute_q_rows(wt, n_heads):
    r = wt.shape[1]
    nope = wt[:n_heads * HEAD].reshape(n_heads, HEAD, r)
    rope = wt[n_heads * HEAD:].reshape(n_heads, ROPE, r)
    return jnp.concatenate([nope, rope], axis=1).reshape(n_heads * QK, r)


def _local_step(x, tgt, gains, weights, grads, first_after=()):
    pre_mix_g, q_norm_g, kv_norm_g, conv_out_g, attn_out_g, post_mix_g, pre_mlp_g, post_mlp_g = gains
    s, d = x.shape
    conv_width = conv_out_g.shape[1]
    n_groups = conv_width // HEAD
    r_q, r_kv = q_norm_g.shape[1], kv_norm_g.shape[1]
    n_heads = attn_out_g.shape[1] // HEAD
    c_q0 = 3 * conv_width
    c_kv0 = c_q0 + r_q
    c_kr0 = c_kv0 + r_kv
    in_pad = _in_pad(c_kr0 + ROPE)
    tn_in = _fit(in_pad, 6 * MXU_WIDTH)
    tables = _rope_tables(s, n_heads)

    h1 = _rms_fwd("pre_mix_norm", x, pre_mix_g, after=first_after)
    weights.forward(0, (h1,))
    weights.relay(0, tables)
    w_in_p, conv_w = weights.ready(0, ())
    proj = _mm_nn("in_proj", h1, w_in_p, F32, TILE_M, tn_in)
    y_conv = _conv_fwd(proj, conv_w, conv_out_g, n_groups, conv_width + n_heads * HEAD, after=weights.forward(1, (proj,)))
    w_uq_p, w_ukv, w_o = weights.ready(1, (y_conv,))
    qn, q = _norm_up("q_up", proj, (c_q0, r_q), q_norm_g, w_uq_p)
    kvn, kv = _norm_up("kv_up", proj, (c_kv0, r_kv), kv_norm_g, w_ukv)
    qh, kh, vh = _pack_heads(q, kv, proj, c_kr0, tables, n_heads)
    o, mix = _attn_fwd(qh, kh, vh, attn_out_g, y_conv, conv_width, after=weights.forward(2, (qh, kh, vh)))
    y = _mm_nn("out_proj", mix, w_o, F32, TILE_M, TILE_N)
    x2, h2 = _mid_fwd(x, y, post_mix_g, pre_mlp_g)
    weights.relay(2, weights.forward(3, (h2,)))
    (w_up,) = weights.ready(2, ())
    a, r = _up_fwd(h2, w_up)
    weights.relay(3, (a,))
    (w_down,) = weights.ready(3, ())
    m = _down_fwd(a, w_down)

    d_out, d_m, dg_post_mlp, loss_part = _head(m, x2, tgt, post_mlp_g)
    core = grads.core
    away = _down_half_grad("down_bwd_w_away", a, d_m, core, home=False)
    d_up = _down_bwd_act(d_m, w_down, r, after=grads.send_away(0, away))
    sums = _down_half_grad("down_bwd_w_home", a, d_m, core, home=True, received=grads.received(0, (d_up,)))
    away = _up_half_grad("up_bwd_w_away", h2, d_up, core, home=False, after=grads.send_sums(0, (sums,)))
    d_h2 = _up_bwd_act(d_up, w_up, after=grads.send_away(1, away))
    sums = _up_half_grad("up_bwd_w_home", h2, d_up, core, home=True, received=grads.received(1, (d_h2,)))
    d_x2, d_y, dg_pre_mlp, dg_post_mix = _mid_bwd(x2, y, d_out, d_h2, pre_mlp_g, post_mix_g, after=grads.send_sums(1, (sums,)))
    d_mix = _mm_nt("out_proj_bwd_act", d_y, w_o, F32, TILE_M, TILE_N)
    gw_o = _mm_tn("out_proj_bwd_w", mix, d_y, BF16, TILE_M, TILE_N)
    dqh, dkh, dvh, dg_attn = _attn_bwd(qh, kh, vh, o, d_mix, attn_out_g, conv_width)
    d_q, d_kv, d_kr = _unpack_heads(dqh, dkh, dvh, tables, n_heads)
    gw_uq_t = _mm_tn("q_up_bwd_w", d_q, qn, F32, TILE_M, TILE_N)
    gw_ukv = _mm_tn("kv_up_bwd_w", kvn, d_kv, BF16, TILE_M, TILE_N)
    d_cq, dg_q = _up_norm_bwd("q_up_bwd_act", d_q, w_uq_p, proj, (c_q0, r_q), q_norm_g, after=grads.full(2, (gw_o, gw_uq_t, gw_ukv)))
    d_ckv, dg_kv = _up_norm_bwd("kv_up_bwd_act", d_kv, w_ukv, proj, (c_kv0, r_kv), kv_norm_g)
    d_u, d_b, d_c, dg_conv, dw_conv = _conv_bwd(proj, d_mix, conv_w, conv_out_g, n_groups)
    d_proj = jnp.concatenate([d_u, d_b, d_c, d_cq, d_ckv, d_kr, jnp.zeros((s, in_pad - c_kr0 - LANE), BF16)], axis=1)
    gw_in_t = _mm_tn("in_proj_bwd_w", d_proj, h1, F32, tn_in, TILE_N)
    updated = grads.update_now(0, grads.send_away(3, gw_in_t))
    d_h1 = _mm_nt("in_proj_bwd_act", d_proj, w_in_p, F32, TILE_M, TILE_N, after=grads.full(3, (gw_in_t,), received=updated))
    grad_x, dg_pre_mix = _first_bwd(x, pre_mix_g, d_h1, d_x2)

    small = [dg_pre_mix, dg_q, dg_kv, dg_conv, dg_attn, dg_post_mix, dg_pre_mlp, dg_post_mlp,
             dw_conv[0], dw_conv[1], dw_conv[2], loss_part]
    return grad_x, jnp.concatenate(small, axis=1)


HBM = pl.BlockSpec(memory_space=pltpu.HBM)
SEM = pl.BlockSpec(memory_space=pltpu.SEMAPHORE)
IN_VMEM = pl.BlockSpec(memory_space=pltpu.VMEM)
SPLIT = pltpu.CompilerParams(has_side_effects=pltpu.SideEffectType.DATAFLOW_SIDE_EFFECTING)


def _in_hbm(a):
    return pltpu.with_memory_space_constraint(a, pltpu.HBM)


def _hbm_like(a):
    return pltpu.HBM(a.shape, a.dtype)


def _place():
    x, y, c = lax.axis_index("x"), lax.axis_index("y"), lax.axis_index("c")
    other_chips = [(1 - x, y), (x, 1 - y), (1 - x, 1 - y)]
    return x, y, c, other_chips


def _block(px, py, pc):
    return 4 * px + 2 * py + pc


def _await(block, sem):
    pltpu.make_async_copy(block, block, sem).wait()


def _relay_route(x, y, c):
    came_from = ((1 - x) * (1 - c) + x * c, y * (1 - c) + (1 - y) * c)
    goes_to = (x * (1 - c) + (1 - x) * c, (1 - y) * (1 - c) + y * c)
    return came_from, goes_to


def _gather_start(name, shards, groups, relayed=(), after=()):
    n, ng = len(shards), len(groups)
    lands = [lax.empty((N_DEV, *a.shape), a.dtype) for a in shards]

    def body(*refs):
        src, land = refs[:n], refs[n:2 * n]
        sems, token = refs[2 * n + len(after):2 * n + len(after) + 2 * ng], refs[-1]
        x, y, c, chips = _place()
        targets = [(x, y, 1 - c)] + [(*chip, c) for chip in chips]
        for gi, group in enumerate(groups):
            for i, w in enumerate(group):
                for k, to in enumerate(targets[:3] if gi in relayed else targets):
                    pltpu.make_async_remote_copy(
                        src_ref=src[w], dst_ref=land[w].at[_block(x, y, c)],
                        send_sem=sems[2 * gi].at[4 * i + k], recv_sem=sems[2 * gi + 1].at[4 * i + k],
                        device_id=to, device_id_type=MESH).start()
        token[...] = jnp.zeros_like(token)

    sem_shapes = [pltpu.SemaphoreType.DMA((4 * len(g),)) for g in groups for _ in range(2)]
    out = pl.pallas_call(
        body, name=name,
        in_specs=[HBM] * (2 * n) + [ANY] * len(after),
        out_specs=[SEM] * (2 * ng) + [HBM] * (2 * n) + [IN_VMEM],
        out_shape=sem_shapes + [_hbm_like(a) for a in shards] + [_hbm_like(a) for a in lands]
        + [jax.ShapeDtypeStruct((SUBLANE, LANE), F32)],
        input_output_aliases={i: 2 * ng + i for i in range(2 * n)},
        compiler_params=SPLIT,
    )(*[_in_hbm(a) for a in shards], *[_in_hbm(a) for a in lands], *after)
    sems = [(out[2 * gi], out[2 * gi + 1]) for gi in range(ng)]
    return sems, out[2 * ng:2 * ng + n], out[2 * ng + n:2 * ng + 2 * n], out[-1]


def _gather_forward(name, shards, lands, send1, recv1, after, relayed=False):
    n = len(lands)

    def body(*refs):
        src, land = refs[:n], refs[n:2 * n]
        s1, r1 = refs[2 * n], refs[2 * n + 1]
        s2, r2 = refs[2 * n + 2 + len(after)], refs[2 * n + 3 + len(after)]
        x, y, c, chips = _place()
        me, sibling = (x, y, c), (x, y, 1 - c)
        for j, chip in enumerate(chips[:2] if relayed else chips):
            for i in range(n):
                blk = land[i].at[_block(*chip, c)]
                pltpu.make_async_remote_copy(src_ref=blk, dst_ref=blk, send_sem=s1.at[4 * i + 1 + j], recv_sem=r1.at[4 * i + 1 + j],
                                             device_id=me, device_id_type=MESH).wait_recv()
                pltpu.make_async_remote_copy(src_ref=blk, dst_ref=blk, send_sem=s2.at[3 * i + j], recv_sem=r2.at[3 * i + j],
                                             device_id=sibling, device_id_type=MESH).start()
        if relayed:
            came_from, goes_to = _relay_route(x, y, c)
            for i in range(n):
                blk = land[i].at[_block(*came_from, c)]
                pltpu.make_async_remote_copy(src_ref=blk, dst_ref=blk, send_sem=s2.at[3 * i + 2], recv_sem=r2.at[3 * i + 2],
                                             device_id=(*goes_to, c), device_id_type=MESH).start()
        for i in range(n):
            blk = land[i].at[_block(x, y, 1 - c)]
            pltpu.make_async_remote_copy(src_ref=blk, dst_ref=blk, send_sem=s1.at[4 * i], recv_sem=r1.at[4 * i],
                                         device_id=me, device_id_type=MESH).wait_recv()
            for k in range(3 if relayed else 4):
                pltpu.make_async_remote_copy(src_ref=src[i], dst_ref=land[i].at[_block(x, y, c)], send_sem=s1.at[4 * i + k],
                                             recv_sem=r1.at[4 * i + k], device_id=sibling, device_id_type=MESH).wait_send()

    sem = pltpu.SemaphoreType.DMA((3 * n,))
    out = pl.pallas_call(
        body, name=name,
        in_specs=[HBM] * (2 * n) + [SEM, SEM] + [ANY] * len(after),
        out_specs=[SEM, SEM] + [HBM] * n,
        out_shape=[sem, sem] + [_hbm_like(a) for a in lands],
        input_output_aliases={n + i: 2 + i for i in range(n)},
        compiler_params=SPLIT,
    )(*shards, *lands, send1, recv1, *after)
    return (out[0], out[1]), out[2:]


def _gather_relay_forward(name, lands, send2, recv2, after):
    n = len(lands)

    def body(*refs):
        land, s2, r2 = refs[:n], refs[n], refs[n + 1]
        s3, r3 = refs[n + 2 + len(after)], refs[n + 3 + len(after)]
        x, y, c, _ = _place()
        me, sibling = (x, y, c), (x, y, 1 - c)
        came_from, _ = _relay_route(x, y, c)
        for i in range(n):
            blk = land[i].at[_block(1 - x, 1 - y, c)]
            pltpu.make_async_remote_copy(src_ref=blk, dst_ref=blk, send_sem=s2.at[3 * i + 2], recv_sem=r2.at[3 * i + 2],
                                         device_id=me, device_id_type=MESH).wait_recv()
            pltpu.make_async_remote_copy(src_ref=blk, dst_ref=blk, send_sem=s3.at[i], recv_sem=r3.at[i],
                                         device_id=sibling, device_id_type=MESH).start()
            sent = land[i].at[_block(*came_from, c)]
            pltpu.make_async_remote_copy(src_ref=sent, dst_ref=sent, send_sem=s2.at[3 * i + 2], recv_sem=r2.at[3 * i + 2],
                                         device_id=me, device_id_type=MESH).wait_send()

    sem = pltpu.SemaphoreType.DMA((n,))
    out = pl.pallas_call(
        body, name=name,
        in_specs=[HBM] * n + [SEM, SEM] + [ANY] * len(after),
        out_specs=[SEM, SEM] + [HBM] * n,
        out_shape=[sem, sem] + [_hbm_like(a) for a in lands],
        input_output_aliases={i: 2 + i for i in range(n)},
        compiler_params=SPLIT,
    )(*lands, send2, recv2, *after)
    return (out[0], out[1]), out[2:]


def _gather_wait(name, lands, send2, recv2, after, relay_sems=None):
    n = len(lands)
    n_sems = 2 if relay_sems is None else 4

    def body(*refs):
        land, s2, r2 = refs[:n], refs[n], refs[n + 1]
        for i in range(n):
            for j in range(3 if relay_sems is None else 2):
                _await(land[i].at[0], r2.at[3 * i + j])
                _await(land[i].at[0], s2.at[3 * i + j])
            if relay_sems is not None:
                _await(land[i].at[0], refs[n + 3].at[i])
                _await(land[i].at[0], refs[n + 2].at[i])

    return pl.pallas_call(
        body, name=name,
        in_specs=[HBM] * n + [SEM] * n_sems + [ANY] * len(after), out_specs=[HBM] * n, out_shape=[_hbm_like(a) for a in lands],
        input_output_aliases={i: i for i in range(n)},
        compiler_params=SPLIT,
    )(*lands, send2, recv2, *(relay_sems or ()), *after)


def _pair_exchange(name, grads, shard_rows):
    n = len(grads)
    shapes = [(g.shape[1:] if r is None else (r, g.shape[1])) for g, r in zip(grads, shard_rows)]

    def body(*refs):
        ins, recv = refs[:n], refs[n:2 * n]
        send_sems, recv_sems = refs[2 * n:]
        x, y, c, _ = _place()
        sends = []
        for w in range(n):
            for k in range(N_CHIP):
                j, r = 2 * k + 1 - c, shard_rows[w]
                src = ins[w].at[j] if r is None else ins[w].at[pl.ds(pl.multiple_of(j * r, SUBLANE), r), :]
                sends.append(pltpu.make_async_remote_copy(
                    src_ref=src, dst_ref=recv[w].at[k],
                    send_sem=send_sems.at[w, k], recv_sem=recv_sems.at[w, k],
                    device_id=(x, y, 1 - c), device_id_type=MESH))
        for cp in sends:
            cp.start()
        for cp in sends:
            cp.wait()

    return pl.pallas_call(
        body, name=name,
        in_specs=[ANY] * n, out_specs=[ANY] * n,
        out_shape=[jax.ShapeDtypeStruct((N_CHIP, *shape), g.dtype) for g, shape in zip(grads, shapes)],
        scratch_shapes=[pltpu.SemaphoreType.DMA((n, N_CHIP))] * 2,
    )(*grads)


def _pair_sum_rows(name, grad, received, core):
    _, r, c = received.shape
    tc = _fit(c, 512)

    def body(core_ref, a_ref, b_ref, o_ref):
        o_ref[...] = (a_ref[...] + b_ref[...]).astype(o_ref.dtype)

    spec = pl.BlockSpec((None, r, tc), lambda k, i, core_ref: (k, 0, i))
    return pl.pallas_call(
        body, name=name,
        grid_spec=pltpu.PrefetchScalarGridSpec(
            num_scalar_prefetch=1, grid=(N_CHIP, c // tc),
            in_specs=[pl.BlockSpec((r, tc), lambda k, i, core_ref: (2 * k + core_ref[0], i)), spec],
            out_specs=spec),
        out_shape=jax.ShapeDtypeStruct(received.shape, BF16),
        compiler_params=_params("parallel", "parallel"),
    )(core, grad, received)


def _pair_sum(name, grad, received, core):
    _, r, c = received.shape
    rows = min(ROWS, r)
    assert r % rows == 0

    def body(core_ref, a_ref, b_ref, o_ref):
        o_ref[...] = (a_ref[...].astype(F32) + b_ref[...].astype(F32)).astype(o_ref.dtype)

    spec = pl.BlockSpec((None, rows, c), lambda k, i, core_ref: (k, i, 0))
    return pl.pallas_call(
        body, name=name,
        grid_spec=pltpu.PrefetchScalarGridSpec(
            num_scalar_prefetch=1, grid=(N_CHIP, r // rows),
            in_specs=[pl.BlockSpec((None, None, rows, c), lambda k, i, core_ref: (k, core_ref[0], i, 0)), spec],
            out_specs=spec),
        out_shape=jax.ShapeDtypeStruct(received.shape, received.dtype),
        compiler_params=_params("parallel", "parallel"),
    )(core, grad.reshape(N_CHIP, 2, r, c), received)


def _away_shard(src, k, c, shard_rows):
    if shard_rows is None:
        return src.at[k]
    return src.at[pl.ds(pl.multiple_of((2 * k + 1 - c) * shard_rows, SUBLANE), shard_rows), :]


def _pair_send_start(name, away, shard_rows=None):
    shape = away.shape if shard_rows is None else (N_CHIP, shard_rows, away.shape[1])
    land = lax.empty(shape, away.dtype)

    def body(src, dst, send, recv, src_thru, dst_thru, token):
        x, y, c, _ = _place()
        for k in range(N_CHIP):
            pltpu.make_async_remote_copy(src_ref=_away_shard(src, k, c, shard_rows), dst_ref=dst.at[k], send_sem=send.at[k],
                                         recv_sem=recv.at[k], device_id=(x, y, 1 - c), device_id_type=MESH).start()
        token[...] = jnp.zeros_like(token)

    sem = pltpu.SemaphoreType.DMA((N_CHIP,))
    out = pl.pallas_call(
        body, name=name,
        in_specs=[HBM, HBM], out_specs=[SEM, SEM, HBM, HBM, IN_VMEM],
        out_shape=[sem, sem, _hbm_like(away), _hbm_like(land), jax.ShapeDtypeStruct((SUBLANE, LANE), F32)],
        input_output_aliases={0: 2, 1: 3},
        compiler_params=SPLIT,
    )(_in_hbm(away), _in_hbm(land))
    return (out[0], out[1]), out[2], out[3], out[4]


def _pair_send_wait(name, sems, src, land, after, shard_rows=None):
    def body(src_ref, dst_ref, send, recv, *rest):
        for k in range(N_CHIP):
            _await(dst_ref.at[k], send.at[k])
            _await(dst_ref.at[k], recv.at[k])

    return pl.pallas_call(
        body, name=name,
        in_specs=[HBM, HBM, SEM, SEM] + [ANY] * len(after), out_specs=HBM, out_shape=_hbm_like(land),
        input_output_aliases={1: 0},
        compiler_params=SPLIT,
    )(src, land, *sems, *after)


def _chip_send_start(name, sums):
    n = len(sums)
    lands = [lax.empty(a.shape, a.dtype) for a in sums]

    def body(*refs):
        src, land = refs[:n], refs[n:2 * n]
        send, recv, token = refs[2 * n], refs[2 * n + 1], refs[-1]
        x, y, c, chips = _place()
        for w in range(n):
            for j, (px, py) in enumerate(chips):
                pltpu.make_async_remote_copy(
                    src_ref=src[w].at[2 * px + py], dst_ref=land[w].at[2 * x + y],
                    send_sem=send.at[3 * w + j], recv_sem=recv.at[3 * w + j],
                    device_id=(px, py, c), device_id_type=MESH).start()
        token[...] = jnp.zeros_like(token)

    sem = pltpu.SemaphoreType.DMA((3 * n,))
    out = pl.pallas_call(
        body, name=name,
        in_specs=[HBM] * (2 * n),
        out_specs=[SEM, SEM] + [HBM] * (2 * n) + [IN_VMEM],
        out_shape=[sem, sem] + [_hbm_like(a) for a in sums] + [_hbm_like(a) for a in lands]
        + [jax.ShapeDtypeStruct((SUBLANE, LANE), F32)],
        input_output_aliases={i: 2 + i for i in range(2 * n)},
        compiler_params=SPLIT,
    )(*[_in_hbm(a) for a in sums], *[_in_hbm(a) for a in lands])
    return (out[0], out[1]), out[2:2 + n], out[2 + n:2 + 2 * n], out[-1]


def _chip_send_wait(name, groups, after):
    counts = [len(g[1]) for g in groups]
    n = sum(counts)

    def body(*refs):
        land = refs[n:2 * n]
        sems = refs[2 * n:2 * n + 2 * len(groups)]
        w = 0
        for gi, count in enumerate(counts):
            for i in range(count):
                for j in range(3):
                    _await(land[w].at[0], sems[2 * gi].at[3 * i + j])
                    _await(land[w].at[0], sems[2 * gi + 1].at[3 * i + j])
                w += 1

    sums = [a for g in groups for a in g[1]]
    lands = [a for g in groups for a in g[2]]
    sems = [s for g in groups for s in g[0]]
    return pl.pallas_call(
        body, name=name,
        in_specs=[HBM] * (2 * n) + [SEM] * len(sems) + [ANY] * len(after),
        out_specs=[HBM] * n, out_shape=[_hbm_like(a) for a in lands],
        input_output_aliases={n + i: i for i in range(n)},
        compiler_params=SPLIT,
    )(*sums, *lands, *sems, *after)


def _small_all_reduce(part, after=()):
    _, w = part.shape

    def body(p_ref, *rest):
        o_ref, buf, send_sems, recv_sems = rest[len(after):]
        x, y, c, _ = _place()
        me = 4 * x + 2 * y + c
        buf[me] = jnp.sum(p_ref[...], axis=0, keepdims=True)
        copies = []
        for k in range(1, N_DEV):
            dx, dy, dc = (k >> 2) & 1, (k >> 1) & 1, k & 1
            copies.append(pltpu.make_async_remote_copy(
                src_ref=buf.at[me], dst_ref=buf.at[me], send_sem=send_sems.at[k - 1], recv_sem=recv_sems.at[k - 1],
                device_id=(x ^ dx, y ^ dy, c ^ dc), device_id_type=MESH))
        for cp in copies:
            cp.start()
        for cp in copies:
            cp.wait()
        tot = buf[0]
        for d in range(1, N_DEV):
            tot = tot + buf[d]
        o_ref[...] = tot
        loss = jnp.sum(tot[:, w - LANE:], axis=1, keepdims=True)
        o_ref[:, w - LANE:] = jnp.broadcast_to(loss, (1, LANE))

    return pl.pallas_call(
        body, name="small_all_reduce",
        in_specs=[IN_VMEM] + [ANY] * len(after), out_specs=IN_VMEM,
        out_shape=jax.ShapeDtypeStruct((1, w), F32),
        scratch_shapes=[pltpu.VMEM((N_DEV, 1, w), F32), pltpu.SemaphoreType.DMA((N_DEV - 1,)), pltpu.SemaphoreType.DMA((N_DEV - 1,))],
        compiler_params=pltpu.CompilerParams(vmem_limit_bytes=VMEM_LIMIT_BYTES),
    )(part, *after)


def _adamw(w, g, m, v):
    m = ADAM_B1 * m + (1.0 - ADAM_B1) * g
    v = ADAM_B2 * v + (1.0 - ADAM_B2) * (g * g)
    m_hat = m / (1.0 - ADAM_B1 ** ADAM_STEP)
    v_hat = v / (1.0 - ADAM_B2 ** ADAM_STEP)
    delta = -ADAM_LR * (m_hat / (jnp.sqrt(v_hat) + ADAM_EPS) + ADAM_WD * w)
    return delta, m, v


def _sum_adam_block(chip_ref, p_ref, own_ref, w_ref, m_ref, v_ref, g_ref, d_ref, mo_ref, vo_ref):
    g = None
    for k in range(N_CHIP):
        term = jnp.where(chip_ref[0] == k, own_ref[...], p_ref[k]).astype(F32)
        g = term if g is None else g + term
    g_ref[...] = g
    d_ref[...], mo_ref[...], vo_ref[...] = _adamw(w_ref[...], g, m_ref[...], v_ref[...])


def _sum_adam(name, parts, sums, chip, w, m, v, after=()):
    _, r, c = w.shape
    n_after = len(after)
    by_rows = r % ROWS == 0 or r < ROWS
    tr, tc = (min(ROWS, r), c) if by_rows else (r, _fit(c, 512))
    at = (lambda i: (i, 0)) if by_rows else (lambda i: (0, i))

    def body(chip_ref, p_ref, own_ref, w_ref, m_ref, v_ref, *rest):
        _sum_adam_block(chip_ref, p_ref, own_ref, w_ref, m_ref, v_ref, *rest[n_after:])

    blk = pl.BlockSpec((None, tr, tc), lambda i, chip_ref: (0, *at(i)))
    out = jax.ShapeDtypeStruct((1, r, c), F32)
    return pl.pallas_call(
        body, name=name,
        grid_spec=pltpu.PrefetchScalarGridSpec(
            num_scalar_prefetch=1, grid=(r // tr if by_rows else c // tc,),
            in_specs=[pl.BlockSpec((N_CHIP, tr, tc), lambda i, chip_ref: (0, *at(i))),
                      pl.BlockSpec((None, tr, tc), lambda i, chip_ref: (chip_ref[0], *at(i))), blk, blk, blk]
            + [ANY] * n_after,
            out_specs=[blk] * 4),
        out_shape=[out] * 4,
        compiler_params=_params("parallel"),
    )(chip, parts, sums, w, m, v, *after)


def _adam_gains(total, ws, ms, vs):
    n = len(ws)
    widths = [w.shape[1] for w in ws]

    def body(t_ref, *refs):
        w_refs, m_refs, v_refs, outs = refs[:n], refs[n:2 * n], refs[2 * n:3 * n], refs[3 * n:]
        off = 0
        for i in range(n):
            g = t_ref[:, off:off + widths[i]]
            off += widths[i]
            g_ref, d_ref, mo_ref, vo_ref = outs[4 * i:4 * i + 4]
            g_ref[...] = g
            d_ref[...], mo_ref[...], vo_ref[...] = _adamw(w_refs[i][...], g, m_refs[i][...], v_refs[i][...])

    out = pl.pallas_call(
        body, name="adam_gains",
        out_shape=[jax.ShapeDtypeStruct(w.shape, F32) for w in ws for _ in range(4)],
    )(total, *ws, *ms, *vs)
    return [tuple(out[4 * i:4 * i + 4]) for i in range(n)]


def _adam_taps(total, first_col, device, w, m, v):
    _, n_taps, cw = w.shape
    col_block = lambda t, dev: (0, first_col // cw + t * N_DEV + dev[0])
    tap = pl.BlockSpec((None, 1, cw), lambda t, dev: (t, 0, 0))

    def body(dev_ref, t_ref, w_ref, m_ref, v_ref, g_ref, d_ref, mo_ref, vo_ref):
        g = t_ref[...]
        g_ref[...] = g
        d_ref[...], mo_ref[...], vo_ref[...] = _adamw(w_ref[...], g, m_ref[...], v_ref[...])

    shape3 = (n_taps, 1, cw)
    out = pl.pallas_call(
        body, name="adam_taps",
        grid_spec=pltpu.PrefetchScalarGridSpec(
            num_scalar_prefetch=1, grid=(n_taps,),
            in_specs=[pl.BlockSpec((1, cw), col_block), tap, tap, tap], out_specs=[tap] * 4),
        out_shape=[jax.ShapeDtypeStruct(shape3, F32)] * 4,
    )(device, total, w.reshape(shape3), m.reshape(shape3), v.reshape(shape3))
    return tuple(o.reshape(w.shape) for o in out)


def kernel(x, pre_mix_g, w_in, conv_w, q_norm_g, w_uq, kv_norm_g, w_ukv, conv_out_g, attn_out_g, w_o, post_mix_g, pre_mlp_g, w_up, w_down, post_mlp_g, loss_target, m_pre_mix_g, m_w_in, m_conv_w, m_q_norm_g, m_w_uq, m_kv_norm_g, m_w_ukv, m_conv_out_g, m_attn_out_g, m_w_o, m_post_mix_g, m_pre_mlp_g, m_w_up, m_w_down, m_post_mlp_g, v_pre_mix_g, v_w_in, v_conv_w, v_q_norm_g, v_w_uq, v_kv_norm_g, v_w_ukv, v_conv_out_g, v_attn_out_g, v_w_o, v_post_mix_g, v_pre_mlp_g, v_w_up, v_w_down, v_post_mlp_g):
    me = 4 * lax.axis_index("x") + 2 * lax.axis_index("y") + lax.axis_index("c")
    core = lax.axis_index("c").astype(jnp.int32).reshape(1)
    chip = (2 * lax.axis_index("x") + lax.axis_index("y")).astype(jnp.int32).reshape(1)
    gains = (pre_mix_g, q_norm_g, kv_norm_g, conv_out_g, attn_out_g, post_mix_g, pre_mlp_g, post_mlp_g)
    gain_m = (m_pre_mix_g, m_q_norm_g, m_kv_norm_g, m_conv_out_g, m_attn_out_g, m_post_mix_g, m_pre_mlp_g, m_post_mlp_g)
    gain_v = (v_pre_mix_g, v_q_norm_g, v_kv_norm_g, v_conv_out_g, v_attn_out_g, v_post_mix_g, v_pre_mlp_g, v_post_mlp_g)
    names = ("w_in", "w_uq", "w_ukv", "w_o", "w_up", "w_down")
    big = dict(zip(names, (w_in, w_uq, w_ukv, w_o, w_up, w_down)))
    big_m = dict(zip(names, (m_w_in, m_w_uq, m_w_ukv, m_w_o, m_w_up, m_w_down)))
    big_v = dict(zip(names, (v_w_in, v_w_uq, v_w_ukv, v_w_o, v_w_up, v_w_down)))
    n_heads = attn_out_g.shape[1] // HEAD
    n_taps = conv_w.shape[1]

    gathered = ("w_in", "conv", "w_uq", "w_ukv", "w_o", "w_up", "w_down")
    gather_groups = ((0, 1), (2, 3, 4), (5,), (6,))
    taps = jnp.pad(conv_w[0], ((0, SUBLANE - n_taps), (0, 0)))
    relayed_groups = (0, 2, 3)
    sems1, shards, lands, token = _gather_start("gather_start_first", [w_in[0].astype(BF16), taps], ((0, 1),), relayed=(0,))
    sems1, shards, lands = list(sems1), list(shards), list(lands)
    behind = token[0, 0]
    rest = [(big[nm][0] + behind).astype(BF16) for nm in gathered[2:]]

    def start_more(name, some, groups, relayed, after):
        sems_b, shards_b, lands_b, _ = _gather_start(name, some, groups, relayed=relayed, after=after)
        sems1.extend(sems_b)
        shards.extend(shards_b)
        lands.extend(lands_b)

    start_rest = lambda after: start_more("gather_start_rest", rest, ((0, 1, 2), (3,), (4,)), (1, 2), after)

    cols = lambda a: jnp.concatenate([a[j] for j in range(N_DEV)], axis=1)
    rows = lambda a: a.reshape(N_DEV * a.shape[1], a.shape[2])
    device = me.astype(jnp.int32).reshape(1)
    own_in = lambda a, shard: lax.dynamic_update_index_in_dim(a, shard, me, 0)
    q_pieces = [(h, 0, HEAD) for h in range(n_heads)] + [(h, HEAD, QK) for h in range(n_heads)]
    ready = {
        "w_in": lambda a, shard: _join_col_shards("join_w_in", a, shard, device),
        "conv": lambda a, shard: cols(own_in(a, shard))[:n_taps],
        "w_uq": lambda a, shard: _join_col_shards("join_w_uq", a, shard, device, q_pieces),
        "w_ukv": lambda a, shard: cols(own_in(a, shard)),
        "w_o": lambda a, shard: rows(own_in(a, shard)),
        "w_up": own_in,
        "w_down": lambda a, shard: rows(own_in(a, shard)),
    }
    assert w_uq.shape[2] == QK

    class Weights:
        def __init__(self):
            self.passed, self.relayed = {}, {}

        def forward(self, group, after):
            idx = gather_groups[group]
            if group == 0:
                after = (*after, *rest)
            self.passed[group] = _gather_forward(f"gather_forward_{group}", [shards[i] for i in idx], [lands[i] for i in idx],
                                                 *sems1[group], after, relayed=group in relayed_groups)
            return tuple(self.passed[group][1])

        def relay(self, group, after):
            sems2, mid = self.passed[group]
            self.relayed[group], mid = _gather_relay_forward(f"gather_relay_{group}", mid, *sems2, after)
            self.passed[group] = (sems2, mid)
            if group == 0:
                start_rest(tuple(mid))
            return tuple(mid)

        def ready(self, group, after):
            sems2, mid = self.passed[group]
            full = _gather_wait(f"gather_wait_{group}", mid, *sems2, after, relay_sems=self.relayed.get(group))
            out = []
            return [ready[gathered[i]](a, shards[i]) for i, a in zip(gather_groups[group], full)]

    weights = Weights()

    col_blocks = lambda g: g.reshape(g.shape[0], N_DEV, g.shape[1] // N_DEV).transpose(1, 0, 2)
    row_blocks = lambda g: g.reshape(N_DEV, g.shape[0] // N_DEV, g.shape[1])
    grad_groups = (("w_down",), ("w_up",), ("w_o", "w_uq", "w_ukv"), ("w_in",))
    transposed = {"w_in": w_in.shape[2], "w_uq": w_uq.shape[2]}
    to_blocks = {
        "w_in": lambda g: g, "w_uq": lambda g: _unpermute_q_rows(g, n_heads),
        "w_ukv": col_blocks, "w_o": row_blocks, "w_up": lambda g: g, "w_down": row_blocks,
    }
    in_flight = []

    class Grads:
        def __init__(self):
            self.core = core
            self.away = {}

        def send_sums(self, group, sums):
            sems, sums, parts, tok = _chip_send_start(f"chip_send_start_{group}", list(sums))
            in_flight.append((sems, sums, parts))
            return (tok,)

        def full(self, group, arrays, received=None):
            nms = grad_groups[group]
            if received is None:
                blocks = [to_blocks[nm](g) for nm, g in zip(nms, arrays)]
                got = _pair_exchange(f"pair_exchange_{group}", blocks, [transposed.get(nm) for nm in nms])
            else:
                blocks, got = [self.away[group][1]], [self.received(group, received)]
            sums = [(_pair_sum_rows if nm in transposed else _pair_sum)(f"pair_sum_{nm}", g, r, core)
                    for nm, g, r in zip(nms, blocks, got)]
            return self.send_sums(group, sums)

        def send_away(self, group, half):
            nm = grad_groups[group][0]
            rows = transposed.get(nm)
            sems, src, land, tok = _pair_send_start(f"pair_send_start_{group}", half if rows is None else to_blocks[nm](half), rows)
            self.away[group] = (sems, src, land, rows)
            return (tok,)

        def received(self, group, after):
            sems, src, land, rows = self.away[group]
            return _pair_send_wait(f"pair_send_wait_{group}", sems, src, land, after, rows)

        def update_now(self, group, after):
            return update(str(group), group, group + 1, after)

    big_out = {}

    def update(tag, first, last, after):
        picked = [i for i in range(first, last) if grad_groups[i][0] not in big_out]
        groups = [in_flight[i] for i in picked]
        parts = _chip_send_wait("chip_send_wait_" + tag, groups, after)
        nms = [nm for i in picked for nm in grad_groups[i]]
        sums = [a for _, s, _ in groups for a in s]
        for nm, p, s in zip(nms, parts, sums):
            view = (lambda a: jnp.swapaxes(a, 1, 2)) if nm in transposed else (lambda a: a)
            out = _sum_adam("adam_" + nm, p, s, chip, view(big[nm]), view(big_m[nm]), view(big_v[nm]), after=after)
            after = (out[0],)
            big_out[nm] = [view(o) for o in out]
        return after

    grad_x, small = _local_step(x[0], loss_target[0], gains, weights, Grads(), first_after=(token,))

    after = update("early", 0, len(in_flight) - 1, (grad_x,))
    total = _small_all_reduce(small, after=after)
    update("late", len(in_flight) - 1, len(in_flight), (total,))
    big_out = [big_out[nm] for nm in names]

    gain_out = _adam_gains(total, gains, gain_m, gain_v)
    taps_out = _adam_taps(total, sum(g.shape[1] for g in gains), me.astype(jnp.int32).reshape(1), conv_w, m_conv_w, v_conv_w)
    loss = total[0, total.shape[1] - 1]

    order = (0, "w_in", "conv", 1, "w_uq", 2, "w_ukv", 3, 4, "w_o", 5, 6, "w_up", "w_down", 7)
    by_name = dict(zip(names, big_out))
    outs = [loss, grad_x[None]]
    for kind in range(4):
        for item in order:
            if item == "conv":
                outs.append(taps_out[kind])
            elif isinstance(item, int):
                outs.append(gain_out[item][kind])
            else:
                outs.append(by_name[item][kind])
    return tuple(outs)
```

```python
import math

import jax
import jax.numpy as jnp
from jax import lax
from jax.experimental import pallas as pl
from jax.experimental.pallas import tpu as pltpu

F32 = jnp.float32
BF16 = jnp.bfloat16

EPS = 1e-6
NEG_INF = -1e30
HEAD = 128
ROPE = 64
QK = HEAD + ROPE
CHUNK = 64
ROPE_THETA = 10000.0
ADAM_LR, ADAM_B1, ADAM_B2, ADAM_EPS, ADAM_WD, ADAM_STEP = 0.001, 0.9, 0.999, 1e-08, 0.01, 10

LANE = 128
SUBLANE = 8
VMEM_LIMIT_BYTES = 56 * 1024 * 1024

N_DEV = 8
N_CHIP = 4
MESH = pl.DeviceIdType.MESH


def _params(*sem):
    return pltpu.CompilerParams(dimension_semantics=sem, vmem_limit_bytes=VMEM_LIMIT_BYTES)


ANY = pl.BlockSpec(memory_space=pl.ANY)


def _call(body, *, in_specs, after=(), **kw):
    n_in, n_after = len(in_specs), len(after)

    def ordered(*refs):
        body(*refs[:n_in], *refs[n_in + n_after:])

    call = pl.pallas_call(ordered, in_specs=[*in_specs, *[ANY] * n_after], **kw)
    return lambda *operands: call(*operands, *after)


def _sublane_sum(v):
    r, w = v.shape
    return jnp.sum(v.reshape(r // SUBLANE, SUBLANE, w), axis=0)


def _rstd(x):
    return lax.rsqrt(jnp.mean(x * x, axis=-1, keepdims=True) + EPS)


def _rms_bwd(x, g, dy):
    r = _rstd(x)
    xh = x * r
    dxh = dy * g
    dx = r * (dxh - xh * jnp.mean(dxh * xh, axis=-1, keepdims=True))
    return dx, dy * xh


def _accumulate(ref, val, step):
    @pl.when(step == 0)
    def _():
        ref[...] = val

    @pl.when(step > 0)
    def _():
        ref[...] += val


NN = ((1,), (0,))
NT = ((1,), (1,))
TN = ((0,), (0,))


def _matmul(name, a, b, *, grid, a_spec, b_spec, out_shape, out_specs, contract, nk=1, acc_shape=None,
            extras=(), extra_specs=(), epilogue=None, after=()):
    multi = isinstance(out_shape, (tuple, list))
    out_shapes = tuple(out_shape) if multi else (out_shape,)
    n_out = len(out_shapes)
    n_extra = len(extras)

    def body(a_ref, b_ref, *rest):
        x_refs = rest[:n_extra]
        o_refs = rest[n_extra:n_extra + n_out]

        def emit(acc):
            vals = epilogue(acc, *[r[...] for r in x_refs]) if epilogue else (acc,)
            for r, v in zip(o_refs, vals):
                r[...] = v.astype(r.dtype)

        p = lax.dot_general(a_ref[...], b_ref[...], (contract, ((), ())), preferred_element_type=F32)
        if nk == 1:
            emit(p)
        else:
            acc_ref = rest[n_extra + n_out]
            k = pl.program_id(2)
            _accumulate(acc_ref, p, k)

            @pl.when(k == nk - 1)
            def _():
                emit(acc_ref[...])

    sem = ("parallel", "parallel") + (("arbitrary",) if nk > 1 else ())
    return _call(
        body, name=name, grid=grid, after=after,
        in_specs=[a_spec, b_spec, *extra_specs],
        out_specs=out_specs,
        out_shape=out_shape,
        scratch_shapes=[pltpu.VMEM(acc_shape, F32)] if nk > 1 else [],
        compiler_params=_params(*sem),
    )(a, b, *extras)


def _fit(n, tile):
    if n <= tile:
        return n
    t = tile - tile % LANE
    while n % t:
        t -= LANE
    return t


def _mm_nn(name, a, b, out_dtype, tm, tn, after=()):
    m, k = a.shape
    n = b.shape[1]
    tm, tn = _fit(m, tm), _fit(n, tn)
    return _matmul(name, a, b, grid=(m // tm, n // tn), after=after,
                   a_spec=pl.BlockSpec((tm, k), lambda i, j: (i, 0)),
                   b_spec=pl.BlockSpec((k, tn), lambda i, j: (0, j)),
                   out_shape=jax.ShapeDtypeStruct((m, n), out_dtype),
                   out_specs=pl.BlockSpec((tm, tn), lambda i, j: (i, j)), contract=NN)


def _mm_nt(name, a, b, out_dtype, tm, tn, after=()):
    m, k = a.shape
    n = b.shape[0]
    tm, tn = _fit(m, tm), _fit(n, tn)
    return _matmul(name, a, b, grid=(m // tm, n // tn), after=after,
                   a_spec=pl.BlockSpec((tm, k), lambda i, j: (i, 0)),
                   b_spec=pl.BlockSpec((tn, k), lambda i, j: (j, 0)),
                   out_shape=jax.ShapeDtypeStruct((m, n), out_dtype),
                   out_specs=pl.BlockSpec((tm, tn), lambda i, j: (i, j)), contract=NT)


def _mm_tn(name, a, b, out_dtype, tm, tn):
    s, m = a.shape
    n = b.shape[1]
    tm, tn = _fit(m, tm), _fit(n, tn)
    return _matmul(name, a, b, grid=(m // tm, n // tn),
                   a_spec=pl.BlockSpec((s, tm), lambda i, j: (0, i)),
                   b_spec=pl.BlockSpec((s, tn), lambda i, j: (0, j)),
                   out_shape=jax.ShapeDtypeStruct((m, n), out_dtype),
                   out_specs=pl.BlockSpec((tm, tn), lambda i, j: (i, j)), contract=TN)


ROWS = 256


def _row_spec(rows, width):
    return pl.BlockSpec((rows, width), lambda i: (i, 0))


def _fixed_spec(rows, width):
    return pl.BlockSpec((rows, width), lambda i: (0, 0))


def _column_pieces(rows, start, width):
    piece = math.gcd(start, width)
    assert piece % LANE == 0
    return [pl.BlockSpec((rows, piece), lambda i, b=start // piece + p: (i, b)) for p in range(width // piece)]


def _rms_fwd(name, x, g, after=()):
    s, w = x.shape
    rows = min(ROWS, s)

    def body(x_ref, g_ref, o_ref):
        xv = x_ref[...]
        o_ref[...] = (xv * _rstd(xv) * g_ref[...]).astype(o_ref.dtype)

    return _call(
        body, name=name, grid=(s // rows,), after=after,
        in_specs=[_row_spec(rows, w), _fixed_spec(1, w)],
        out_specs=_row_spec(rows, w),
        out_shape=jax.ShapeDtypeStruct((s, w), BF16),
        compiler_params=_params("parallel"),
    )(x, g)


def _norm_up(name, x, cols, g, w, after=()):
    s = x.shape[0]
    start, width = cols
    n = w.shape[1]
    tm = min(TILE_M, s)
    pieces = _column_pieces(tm, start, width)
    n_p = len(pieces)

    def body(*refs):
        g_ref, w_ref, xn_ref, o_ref = refs[n_p:]
        xv = refs[0][...] if n_p == 1 else jnp.concatenate([r[...] for r in refs[:n_p]], axis=1)
        xn = (xv * _rstd(xv) * g_ref[...]).astype(BF16)
        xn_ref[...] = xn
        o_ref[...] = jnp.dot(xn, w_ref[...], preferred_element_type=F32)

    return _call(
        body, name=name, grid=(s // tm,), after=after,
        in_specs=[*pieces, _fixed_spec(1, width), _fixed_spec(width, n)],
        out_specs=[_row_spec(tm, width), _row_spec(tm, n)],
        out_shape=[jax.ShapeDtypeStruct((s, width), BF16), jax.ShapeDtypeStruct((s, n), F32)],
        compiler_params=_params("parallel"),
    )(*[x] * n_p, g, w)


def _up_norm_bwd(name, dy, w, x, cols, g, after=()):
    s, n = dy.shape
    start, width = cols
    tm = min(TILE_M, s)
    pieces = _column_pieces(tm, start, width)
    n_p = len(pieces)

    def body(dy_ref, w_ref, *refs):
        g_ref, dx_ref, dg_ref = refs[n_p:]
        xv = refs[0][...] if n_p == 1 else jnp.concatenate([r[...] for r in refs[:n_p]], axis=1)
        dxn = lax.dot_general(dy_ref[...], w_ref[...], (NT, ((), ())), preferred_element_type=F32)
        dx, dgc = _rms_bwd(xv, g_ref[...], dxn)
        dx_ref[...] = dx.astype(dx_ref.dtype)
        _accumulate(dg_ref, _sublane_sum(dgc), pl.program_id(0))

    return _call(
        body, name=name, grid=(s // tm,), after=after,
        in_specs=[_row_spec(tm, n), _fixed_spec(width, n), *pieces, _fixed_spec(1, width)],
        out_specs=[_row_spec(tm, width), _fixed_spec(SUBLANE, width)],
        out_shape=[jax.ShapeDtypeStruct((s, width), BF16), jax.ShapeDtypeStruct((SUBLANE, width), F32)],
        compiler_params=_params("arbitrary"),
    )(dy, w, *[x] * n_p, g)


def _mid_fwd(x, y, g_post, g_pre, after=()):
    s, w = x.shape
    rows = min(ROWS, s)

    def body(x_ref, y_ref, gp_ref, gq_ref, x2_ref, h2_ref):
        yv = y_ref[...]
        x2 = x_ref[...] + yv * _rstd(yv) * gp_ref[...]
        x2_ref[...] = x2
        h2_ref[...] = (x2 * _rstd(x2) * gq_ref[...]).astype(h2_ref.dtype)

    return _call(
        body, name="mid_fwd", grid=(s // rows,), after=after,
        in_specs=[_row_spec(rows, w), _row_spec(rows, w), _fixed_spec(1, w), _fixed_spec(1, w)],
        out_specs=[_row_spec(rows, w), _row_spec(rows, w)],
        out_shape=[jax.ShapeDtypeStruct((s, w), F32), jax.ShapeDtypeStruct((s, w), BF16)],
        compiler_params=_params("parallel"),
    )(x, y, g_post, g_pre)


def _head(m, x2, tgt, g):
    s, w = m.shape
    rows = min(ROWS, s)

    def body(m_ref, x2_ref, t_ref, g_ref, dout_ref, dm_ref, dg_ref, loss_ref):
        mv = m_ref[...]
        gv = g_ref[...]
        out = x2_ref[...] + mv * _rstd(mv) * gv
        err = out - t_ref[...]
        dout = err * (1.0 / w)
        dout_ref[...] = dout
        dm, dgc = _rms_bwd(mv, gv, dout)
        dm_ref[...] = dm.astype(dm_ref.dtype)
        sq = err * err
        lanes = sq[:, 0:LANE]
        for j in range(1, w // LANE):
            lanes = lanes + sq[:, j * LANE:(j + 1) * LANE]
        step = pl.program_id(0)
        _accumulate(dg_ref, _sublane_sum(dgc), step)
        _accumulate(loss_ref, _sublane_sum(lanes) * (0.5 / w), step)

    return pl.pallas_call(
        body, name="head", grid=(s // rows,),
        in_specs=[_row_spec(rows, w), _row_spec(rows, w), _row_spec(rows, w), _fixed_spec(1, w)],
        out_specs=[_row_spec(rows, w), _row_spec(rows, w), _fixed_spec(SUBLANE, w), _fixed_spec(SUBLANE, LANE)],
        out_shape=[jax.ShapeDtypeStruct((s, w), F32), jax.ShapeDtypeStruct((s, w), BF16),
                   jax.ShapeDtypeStruct((SUBLANE, w), F32), jax.ShapeDtypeStruct((SUBLANE, LANE), F32)],
        compiler_params=_params("arbitrary"),
    )(m, x2, tgt, g)


def _mid_bwd(x2, y, d_out, d_h2, g_pre, g_post, after=()):
    s, w = x2.shape
    rows = min(ROWS, s)

    def body(x2_ref, y_ref, dout_ref, dh2_ref, gq_ref, gp_ref, dx2_ref, dy_ref, dgq_ref, dgp_ref):
        dx, dgq = _rms_bwd(x2_ref[...], gq_ref[...], dh2_ref[...])
        dx2 = dout_ref[...] + dx
        dx2_ref[...] = dx2
        dy, dgp = _rms_bwd(y_ref[...], gp_ref[...], dx2)
        dy_ref[...] = dy.astype(dy_ref.dtype)
        step = pl.program_id(0)
        _accumulate(dgq_ref, _sublane_sum(dgq), step)
        _accumulate(dgp_ref, _sublane_sum(dgp), step)

    return _call(
        body, name="mid_bwd", grid=(s // rows,), after=after,
        in_specs=[_row_spec(rows, w)] * 4 + [_fixed_spec(1, w)] * 2,
        out_specs=[_row_spec(rows, w), _row_spec(rows, w), _fixed_spec(SUBLANE, w), _fixed_spec(SUBLANE, w)],
        out_shape=[jax.ShapeDtypeStruct((s, w), F32), jax.ShapeDtypeStruct((s, w), BF16),
                   jax.ShapeDtypeStruct((SUBLANE, w), F32), jax.ShapeDtypeStruct((SUBLANE, w), F32)],
        compiler_params=_params("arbitrary"),
    )(x2, y, d_out, d_h2, g_pre, g_post)


def _first_bwd(x, g, d_h1, d_x2, after=()):
    s, w = x.shape
    rows = min(ROWS, s)

    def body(x_ref, g_ref, dh_ref, dx2_ref, dx_ref, dg_ref):
        dx, dgc = _rms_bwd(x_ref[...], g_ref[...], dh_ref[...])
        dx_ref[...] = dx2_ref[...] + dx
        _accumulate(dg_ref, _sublane_sum(dgc), pl.program_id(0))

    return _call(
        body, name="first_bwd", grid=(s // rows,), after=after,
        in_specs=[_row_spec(rows, w), _fixed_spec(1, w), _row_spec(rows, w), _row_spec(rows, w)],
        out_specs=[_row_spec(rows, w), _fixed_spec(SUBLANE, w)],
        out_shape=[jax.ShapeDtypeStruct((s, w), F32), jax.ShapeDtypeStruct((SUBLANE, w), F32)],
        compiler_params=_params("arbitrary"),
    )(x, g, d_h1, d_x2)


def _shift_down(v, k):
    t = lax.broadcasted_iota(jnp.int32, v.shape, 0)
    return jnp.where(t >= k, pltpu.roll(v, k, 0), 0.0)


def _shift_up(v, k):
    n = v.shape[0]
    t = lax.broadcasted_iota(jnp.int32, v.shape, 0)
    return jnp.where(t < n - k, pltpu.roll(v, n - k, 0), 0.0)


def _conv_core(u, b, c, w):
    z = c * u
    conv = w[0:1, :] * _shift_down(z, 2) + w[1:2, :] * _shift_down(z, 1) + w[2:3, :] * z
    return z, conv, b * conv


def _conv_fwd(proj, conv_w, g, n_groups, out_width, after=()):
    s = proj.shape[0]

    def body(u_ref, b_ref, c_ref, w_ref, g_ref, o_ref):
        _, _, yr = _conv_core(u_ref[...], b_ref[...], c_ref[...], w_ref[...])
        o_ref[...] = (yr * _rstd(yr) * g_ref[...]).astype(o_ref.dtype)

    col = lambda k: pl.BlockSpec((s, HEAD), lambda i: (0, k * n_groups + i))
    return _call(
        body, name="conv_fwd", grid=(n_groups,), after=after,
        in_specs=[col(0), col(1), col(2), pl.BlockSpec((3, HEAD), lambda i: (0, i)), pl.BlockSpec((1, HEAD), lambda i: (0, i))],
        out_specs=pl.BlockSpec((s, HEAD), lambda i: (0, i)),
        out_shape=jax.ShapeDtypeStruct((s, out_width), BF16),
        compiler_params=_params("parallel"),
    )(proj, proj, proj, conv_w, g)


def _conv_bwd(proj, d_mix, conv_w, g, n_groups):
    s = proj.shape[0]
    width = n_groups * HEAD

    def body(u_ref, b_ref, c_ref, dy_ref, w_ref, g_ref, du_ref, db_ref, dc_ref, dg_ref, dw_ref):
        u, b, c, w = u_ref[...], b_ref[...], c_ref[...], w_ref[...]
        z, conv, yr = _conv_core(u, b, c, w)
        dyr, dgc = _rms_bwd(yr, g_ref[...], dy_ref[...])
        dconv = dyr * b
        db_ref[...] = (dyr * conv).astype(db_ref.dtype)
        dz = w[2:3, :] * dconv + w[1:2, :] * _shift_up(dconv, 1) + w[0:1, :] * _shift_up(dconv, 2)
        dc_ref[...] = (dz * u).astype(dc_ref.dtype)
        du_ref[...] = (dz * c).astype(du_ref.dtype)
        dg_ref[...] = _sublane_sum(dgc)
        dw_ref[0] = _sublane_sum(dconv * _shift_down(z, 2))
        dw_ref[1] = _sublane_sum(dconv * _shift_down(z, 1))
        dw_ref[2] = _sublane_sum(dconv * z)

    col = lambda k: pl.BlockSpec((s, HEAD), lambda i: (0, k * n_groups + i))
    grp = pl.BlockSpec((s, HEAD), lambda i: (0, i))
    return pl.pallas_call(
        body, name="conv_bwd", grid=(n_groups,),
        in_specs=[col(0), col(1), col(2), grp, pl.BlockSpec((3, HEAD), lambda i: (0, i)), pl.BlockSpec((1, HEAD), lambda i: (0, i))],
        out_specs=[grp, grp, grp, pl.BlockSpec((SUBLANE, HEAD), lambda i: (0, i)),
                   pl.BlockSpec((3, SUBLANE, HEAD), lambda i: (0, 0, i))],
        out_shape=[jax.ShapeDtypeStruct((s, width), BF16)] * 3
        + [jax.ShapeDtypeStruct((SUBLANE, width), F32), jax.ShapeDtypeStruct((3, SUBLANE, width), F32)],
        compiler_params=_params("parallel"),
    )(proj, proj, proj, d_mix, conv_w, g)


def _rope_tables(s, n_heads):
    pos = jnp.arange(s, dtype=F32)
    inv_freq = jnp.power(ROPE_THETA, -jnp.arange(0, ROPE, 2, dtype=F32) / ROPE)
    ang = pos[:, None] * inv_freq[None, :]
    cos, sin = jnp.cos(ang), jnp.sin(ang)
    cs = jnp.concatenate([cos, cos], axis=1)
    sn = jnp.concatenate([-sin, sin], axis=1)
    pad = jnp.zeros((s, LANE - ROPE), F32)
    return (jnp.tile(cs, (1, n_heads)), jnp.tile(sn, (1, n_heads)),
            jnp.concatenate([cs, pad], axis=1), jnp.concatenate([sn, pad], axis=1))


def _swap_halves(v):
    w = v.shape[1]
    lane = lax.broadcasted_iota(jnp.int32, v.shape, 1)
    first = (lane % ROPE) < (ROPE // 2)
    return jnp.where(first, pltpu.roll(v, w - ROPE // 2, 1), pltpu.roll(v, ROPE // 2, 1))


def _pack_heads(q, kv, proj, kr_col, tables, n_heads, after=()):
    s = q.shape[0]
    rows = min(ROWS, s)
    cq, sq, ck, sk = tables
    wq = n_heads * ROPE

    def body(q_ref, kv_ref, kr_ref, cq_ref, sq_ref, ck_ref, sk_ref, qo_ref, ko_ref, vo_ref):
        qr = q_ref[:, n_heads * HEAD:]
        qr = qr * cq_ref[...] + _swap_halves(qr) * sq_ref[...]
        krv = kr_ref[...]
        krv = krv * ck_ref[...] + _swap_halves(krv) * sk_ref[...]
        for h in range(n_heads):
            qo_ref[h] = jnp.concatenate([q_ref[:, h * HEAD:(h + 1) * HEAD], qr[:, h * ROPE:(h + 1) * ROPE]], axis=1).astype(BF16)
            ko_ref[h] = jnp.concatenate([kv_ref[:, 2 * h * HEAD:(2 * h + 1) * HEAD], krv[:, :ROPE]], axis=1).astype(BF16)
            vo_ref[h] = kv_ref[:, (2 * h + 1) * HEAD:(2 * h + 2) * HEAD].astype(BF16)

    hs = lambda w: pl.BlockSpec((n_heads, rows, w), lambda i: (0, i, 0))
    return _call(
        body, name="pack_heads", grid=(s // rows,), after=after,
        in_specs=[_row_spec(rows, q.shape[1]), _row_spec(rows, kv.shape[1]), pl.BlockSpec((rows, LANE), lambda i: (i, kr_col // LANE)),
                  _row_spec(rows, wq), _row_spec(rows, wq), _row_spec(rows, LANE), _row_spec(rows, LANE)],
        out_specs=[hs(QK), hs(QK), hs(HEAD)],
        out_shape=[jax.ShapeDtypeStruct((n_heads, s, QK), BF16), jax.ShapeDtypeStruct((n_heads, s, QK), BF16),
                   jax.ShapeDtypeStruct((n_heads, s, HEAD), BF16)],
        compiler_params=_params("parallel"),
    )(q, kv, proj, cq, sq, ck, sk)


def _unpack_heads(dq, dk, dv, tables, n_heads):
    s = dq.shape[1]
    rows = min(ROWS, s)
    cq, sq, ck, sk = tables
    wq = n_heads * ROPE

    def body(dq_ref, dk_ref, dv_ref, cq_ref, sq_ref, ck_ref, sk_ref, qo_ref, kvo_ref, kro_ref):
        dqr = jnp.concatenate([dq_ref[h][:, HEAD:] for h in range(n_heads)], axis=1)
        dqr = dqr * cq_ref[...] - _swap_halves(dqr) * sq_ref[...]
        dkr = dk_ref[0][:, HEAD:]
        for h in range(1, n_heads):
            dkr = dkr + dk_ref[h][:, HEAD:]
        dkr = jnp.concatenate([dkr, jnp.zeros((rows, LANE - ROPE), F32)], axis=1)
        dkr = dkr * ck_ref[...] - _swap_halves(dkr) * sk_ref[...]
        kro_ref[...] = dkr.astype(kro_ref.dtype)
        qo_ref[:, n_heads * HEAD:] = dqr.astype(qo_ref.dtype)
        for h in range(n_heads):
            qo_ref[:, h * HEAD:(h + 1) * HEAD] = dq_ref[h][:, :HEAD].astype(qo_ref.dtype)
            kvo_ref[:, 2 * h * HEAD:(2 * h + 1) * HEAD] = dk_ref[h][:, :HEAD].astype(kvo_ref.dtype)
            kvo_ref[:, (2 * h + 1) * HEAD:(2 * h + 2) * HEAD] = dv_ref[h].astype(kvo_ref.dtype)

    hs = lambda w: pl.BlockSpec((n_heads, rows, w), lambda i: (0, i, 0))
    return pl.pallas_call(
        body, name="unpack_heads", grid=(s // rows,),
        in_specs=[hs(QK), hs(QK), hs(HEAD), _row_spec(rows, wq), _row_spec(rows, wq), _row_spec(rows, LANE), _row_spec(rows, LANE)],
        out_specs=[_row_spec(rows, n_heads * QK), _row_spec(rows, 2 * n_heads * HEAD), _row_spec(rows, LANE)],
        out_shape=[jax.ShapeDtypeStruct((s, n_heads * QK), BF16), jax.ShapeDtypeStruct((s, 2 * n_heads * HEAD), BF16),
                   jax.ShapeDtypeStruct((s, LANE), BF16)],
        compiler_params=_params("parallel"),
    )(dq, dk, dv, cq, sq, ck, sk)


TQ = 256


LOG2_E = 1.4426950408889634


def _softmax_parts(q, k):
    tq, n_keys = q.shape[0], k.shape[0]
    sc = lax.dot_general(q, k, (NT, ((), ())), preferred_element_type=F32) * (QK ** -0.5 * LOG2_E)
    row = lax.broadcasted_iota(jnp.int32, (tq, tq), 0)
    col = lax.broadcasted_iota(jnp.int32, (tq, tq), 1)
    own = jnp.where(col // CHUNK <= row // CHUNK, sc[:, n_keys - tq:], NEG_INF)
    sc = own if n_keys == tq else jnp.concatenate([sc[:, :n_keys - tq], own], axis=1)
    e = jnp.exp2(sc - jnp.max(sc, axis=-1, keepdims=True))
    return e, 1.0 / jnp.sum(e, axis=-1, keepdims=True)


def _prob_columns(c, tq):
    return pl.ds(tq * (c * (c + 1) // 2), (c + 1) * tq)


def _attn_fwd(q, k, v, g, mix, col0, after=()):
    n_heads, s, _ = q.shape
    tq = min(TQ, s)
    assert tq % CHUNK == 0 and s % tq == 0
    n_blocks = s // tq
    p_cols = tq * (n_blocks * (n_blocks + 1) // 2)

    def body(q_ref, k_ref, v_ref, g_ref, mix_ref, o_ref, p_ref, y_ref):
        for c in range(n_blocks):
            rows, n_keys = pl.ds(c * tq, tq), (c + 1) * tq
            e, inv = _softmax_parts(q_ref[rows, :], k_ref[0:n_keys, :])
            p = (e * inv).astype(BF16)
            p_ref[:, _prob_columns(c, tq)] = p
            o = jnp.dot(p, v_ref[0:n_keys, :], preferred_element_type=F32)
            o_ref[rows, :] = o
            y_ref[rows, :] = (o * _rstd(o) * g_ref[...]).astype(y_ref.dtype)

    head = lambda w: pl.BlockSpec((None, s, w), lambda h: (h, 0, 0))
    return _call(
        body, name="attn_fwd", grid=(n_heads,), after=after,
        in_specs=[head(QK), head(QK), head(HEAD), pl.BlockSpec((1, HEAD), lambda h: (0, h)), ANY],
        out_specs=[head(HEAD), pl.BlockSpec((None, tq, p_cols), lambda h: (h, 0, 0)),
                   pl.BlockSpec((s, HEAD), lambda h: (0, col0 // HEAD + h))],
        out_shape=[jax.ShapeDtypeStruct((n_heads, s, HEAD), F32), jax.ShapeDtypeStruct((n_heads, tq, p_cols), BF16),
                   jax.ShapeDtypeStruct(mix.shape, mix.dtype)],
        input_output_aliases={4: 2},
        compiler_params=_params("parallel"),
    )(q, k, v, g, mix)


def _attn_bwd(q, k, v, o, probs, d_mix, g, col0, after=()):
    n_heads, s, _ = q.shape
    tq = probs.shape[1]

    def body(q_ref, k_ref, v_ref, o_ref, p_ref, dy_ref, g_ref, dq_ref, dk_ref, dv_ref, dg_ref):
        dg = None
        for c in reversed(range(s // tq)):
            rows, n_keys = pl.ds(c * tq, tq), (c + 1) * tq
            o = o_ref[rows, :]
            do, dgc = _rms_bwd(o, g_ref[...], dy_ref[rows, :])
            do = do.astype(BF16)
            dg = _sublane_sum(dgc) if dg is None else dg + _sublane_sum(dgc)
            p = p_ref[:, _prob_columns(c, tq)]
            dp = lax.dot_general(do, v_ref[0:n_keys, :], (NT, ((), ())), preferred_element_type=F32)
            ds = (p.astype(F32) * (dp - jnp.sum(do.astype(F32) * o, axis=-1, keepdims=True))).astype(BF16)
            dq_ref[rows, :] = jnp.dot(ds, k_ref[0:n_keys, :], preferred_element_type=F32) * (QK ** -0.5)
            dk = lax.dot_general(ds, q_ref[rows, :], (TN, ((), ())), preferred_element_type=F32)
            dv = lax.dot_general(p, do, (TN, ((), ())), preferred_element_type=F32)
            if n_keys == s:
                dk_ref[...] = dk
                dv_ref[...] = dv
            else:
                dk_ref[0:n_keys, :] += dk
                dv_ref[0:n_keys, :] += dv
        dk_ref[...] = dk_ref[...] * (QK ** -0.5)
        dg_ref[...] = dg

    c0 = col0 // HEAD
    head = lambda w: pl.BlockSpec((None, s, w), lambda h, *_: (h, 0, 0))
    in_specs = [head(QK), head(QK), head(HEAD), head(HEAD), pl.BlockSpec((None, tq, probs.shape[2]), lambda h, *_: (h, 0, 0)),
                pl.BlockSpec((s, HEAD), lambda h, *_: (0, c0 + h)), pl.BlockSpec((1, HEAD), lambda h, *_: (0, h))]
    out_specs = [head(QK), head(QK), head(HEAD), pl.BlockSpec((SUBLANE, HEAD), lambda h, *_: (0, h))]
    out_shape = [jax.ShapeDtypeStruct((n_heads, s, QK), F32), jax.ShapeDtypeStruct((n_heads, s, QK), F32),
                 jax.ShapeDtypeStruct((n_heads, s, HEAD), F32), jax.ShapeDtypeStruct((SUBLANE, n_heads * HEAD), F32)]
    return _call(body, name="attn_bwd", grid=(n_heads,), after=after, in_specs=in_specs, out_specs=out_specs,
                 out_shape=out_shape, compiler_params=_params("parallel"))(q, k, v, o, probs, d_mix, g)


TILE_M = 1024
TILE_N = 1024


def _up_fwd(h2, w_up):
    s, d = h2.shape
    nb, _, fb = w_up.shape
    tm = min(TILE_M,s)

    def epilogue(acc):
        r = jnp.maximum(acc, 0.0)
        return r * r, r

    blk = pl.BlockSpec((tm, fb), lambda i, j: (i, j))
    return _matmul("up_fwd", h2, w_up, grid=(s // tm, nb),
                   a_spec=pl.BlockSpec((tm, d), lambda i, j: (i, 0)),
                   b_spec=pl.BlockSpec((None, d, fb), lambda i, j: (j, 0, 0)),
                   out_shape=[jax.ShapeDtypeStruct((s, nb * fb), BF16)] * 2, out_specs=[blk, blk],
                   contract=NN, epilogue=epilogue)


def _down_fwd(a, w_down):
    s, f = a.shape
    d = w_down.shape[1]
    tm, tn, tk = min(TILE_M,s), min(TILE_N,d), 2048
    nk = f // tk
    return _matmul("down_fwd", a, w_down, grid=(s // tm, d // tn, nk),
                   a_spec=pl.BlockSpec((tm, tk), lambda i, j, k: (i, k)),
                   b_spec=pl.BlockSpec((tk, tn), lambda i, j, k: (k, j)),
                   out_shape=jax.ShapeDtypeStruct((s, d), F32),
                   out_specs=pl.BlockSpec((tm, tn), lambda i, j, k: (i, j)),
                   contract=NN, nk=nk, acc_shape=(tm, tn))


def _down_bwd_act(d_m, w_down, r, after=()):
    s, d = d_m.shape
    f = w_down.shape[0]
    tm, tn = min(TILE_M,s), min(TILE_N,f)
    blk = pl.BlockSpec((tm, tn), lambda i, j: (i, j))
    return _matmul("down_bwd_act", d_m, w_down, grid=(s // tm, f // tn), after=after,
                   a_spec=pl.BlockSpec((tm, d), lambda i, j: (i, 0)),
                   b_spec=pl.BlockSpec((tn, d), lambda i, j: (j, 0)),
                   out_shape=jax.ShapeDtypeStruct((s, f), BF16), out_specs=blk, contract=NT,
                   extras=(r,), extra_specs=(blk,),
                   epilogue=lambda acc, rv: (acc * (2.0 * rv.astype(F32)),))


def _up_bwd_act(d_up, w_up, after=()):
    s, _ = d_up.shape
    nb, d, fb = w_up.shape
    tm, tn = min(TILE_M, s), min(TILE_N,d)
    pair = 2
    n_after = len(after)

    def body(a_ref, w_ref, *rest):
        o_ref, acc_ref = rest[n_after:]
        k = pl.program_id(2)
        p = None
        for t in range(pair):
            term = lax.dot_general(a_ref[:, t * fb:(t + 1) * fb], w_ref[t], (NT, ((), ())), preferred_element_type=F32)
            p = term if p is None else p + term
        _accumulate(acc_ref, p, k)

        @pl.when(k == nb // pair - 1)
        def _():
            o_ref[...] = acc_ref[...]

    return pl.pallas_call(
        body, name="up_bwd_act", grid=(s // tm, d // tn, nb // pair),
        in_specs=[pl.BlockSpec((tm, pair * fb), lambda i, j, k: (i, k)),
                  pl.BlockSpec((pair, tn, fb), lambda i, j, k: (k, j, 0))] + [ANY] * n_after,
        out_specs=pl.BlockSpec((tm, tn), lambda i, j, k: (i, j)),
        out_shape=jax.ShapeDtypeStruct((s, d), F32),
        scratch_shapes=[pltpu.VMEM((tm, tn), F32)],
        compiler_params=_params("parallel", "parallel", "arbitrary"),
    )(d_up, w_up, *after)


def _half_grad(name, a, b, core, home, received, after, *, grid, a_block, a_map, b_block, b_map, o_block, o_map, out_shape):
    n_after = len(after)
    pick = (lambda ref: ref[0]) if home else (lambda ref: 1 - ref[0])

    def body(core_ref, a_ref, b_ref, *rest):
        acc = lax.dot_general(a_ref[...], b_ref[...], (TN, ((), ())), preferred_element_type=F32)
        if received is not None:
            acc = acc + rest[0][...].astype(F32)
        rest[-1][...] = acc.astype(rest[-1].dtype)

    wrap = lambda fn: (lambda i, j, core_ref: fn(i, j, pick(core_ref)))
    o_spec = pl.BlockSpec(o_block, wrap(o_map))
    extra = [] if received is None else [o_spec]
    operands = [] if received is None else [received]
    return pl.pallas_call(
        body, name=name,
        grid_spec=pltpu.PrefetchScalarGridSpec(
            num_scalar_prefetch=1, grid=grid,
            in_specs=[pl.BlockSpec(a_block, wrap(a_map)), pl.BlockSpec(b_block, wrap(b_map))] + extra + [ANY] * n_after,
            out_specs=o_spec),
        out_shape=out_shape,
        compiler_params=_params("parallel", "parallel"),
    )(core, a, b, *operands, *after)


def _down_half_grad(name, a, d_m, core, home, received=None, after=()):
    s, f = a.shape
    d = d_m.shape[1]
    r = f // N_DEV
    tn = min(TILE_N, d)
    return _half_grad(name, a, d_m, core, home, received, after, grid=(N_CHIP, d // tn),
                      a_block=(s, r), a_map=lambda k, j, p: (0, 2 * k + p),
                      b_block=(s, tn), b_map=lambda k, j, p: (0, j),
                      o_block=(None, r, tn), o_map=lambda k, j, p: (k, 0, j),
                      out_shape=jax.ShapeDtypeStruct((N_CHIP, r, d), BF16))


def _up_half_grad(name, h2, d_up, core, home, received=None, after=()):
    s, d = h2.shape
    fb = d_up.shape[1] // N_DEV
    tm = min(TILE_M, d)
    return _half_grad(name, h2, d_up, core, home, received, after, grid=(d // tm, N_CHIP),
                      a_block=(s, tm), a_map=lambda i, k, p: (0, i),
                      b_block=(s, fb), b_map=lambda i, k, p: (0, 2 * k + p),
                      o_block=(None, tm, fb), o_map=lambda i, k, p: (k, i, 0),
                      out_shape=jax.ShapeDtypeStruct((N_CHIP, d, fb), BF16))


MXU_WIDTH = 256


def _in_pad(in_width):
    return -(-in_width // MXU_WIDTH) * MXU_WIDTH


def _join_col_shards(name, blocks, own, device, pieces=None):
    n, r, w = blocks.shape
    rows = min(ROWS, r)
    pieces = pieces or [(j, 0, w) for j in range(n)]
    used = sum(b - a for _, a, b in pieces)
    width = _in_pad(used)

    def body(dev_ref, x_ref, own_ref, o_ref):
        block = lambda j: jnp.where(dev_ref[0] == j, own_ref[...], x_ref[j])
        cols = [block(j)[:, a:b] for j, a, b in pieces]
        tail = [jnp.zeros((rows, width - used), o_ref.dtype)] if width > used else []
        o_ref[...] = jnp.concatenate(cols + tail, axis=1)

    return pl.pallas_call(
        body, name=name,
        grid_spec=pltpu.PrefetchScalarGridSpec(
            num_scalar_prefetch=1, grid=(r // rows,),
            in_specs=[pl.BlockSpec((n, rows, w), lambda i, dev: (0, i, 0)), pl.BlockSpec((rows, w), lambda i, dev: (i, 0))],
            out_specs=pl.BlockSpec((rows, width), lambda i, dev: (i, 0))),
        out_shape=jax.ShapeDtypeStruct((r, width), blocks.dtype),
        compiler_params=_params("parallel"),
    )(device, blocks, own)


def _unpermute_q_rows(wt, n_heads):
    r = wt.shape[1]
    nope = wt[:n_heads * HEAD].reshape(n_heads, HEAD, r)
    rope = wt[n_heads * HEAD:].reshape(n_heads, ROPE, r)
    return jnp.concatenate([nope, rope], axis=1).reshape(n_heads * QK, r)


def _local_step(x, tgt, gains, weights, grads, first_after=()):
    pre_mix_g, q_norm_g, kv_norm_g, conv_out_g, attn_out_g, post_mix_g, pre_mlp_g, post_mlp_g = gains
    s, d = x.shape
    conv_width = conv_out_g.shape[1]
    n_groups = conv_width // HEAD
    r_q, r_kv = q_norm_g.shape[1], kv_norm_g.shape[1]
    n_heads = attn_out_g.shape[1] // HEAD
    c_q0 = 3 * conv_width
    c_kv0 = c_q0 + r_q
    c_kr0 = c_kv0 + r_kv
    in_pad = _in_pad(c_kr0 + ROPE)
    tn_in = _fit(in_pad, 6 * MXU_WIDTH)
    tables = _rope_tables(s, n_heads)

    h1 = _rms_fwd("pre_mix_norm", x, pre_mix_g, after=first_after)
    weights.forward(0, (h1,))
    weights.relay(0, tables)
    w_in_p, conv_w = weights.ready(0, ())
    proj = _mm_nn("in_proj", h1, w_in_p, F32, TILE_M, tn_in)
    y_conv = _conv_fwd(proj, conv_w, conv_out_g, n_groups, conv_width + n_heads * HEAD, after=weights.forward(1, (proj,)))
    w_uq_p, w_ukv, w_o = weights.ready(1, (y_conv,))
    qn, q = _norm_up("q_up", proj, (c_q0, r_q), q_norm_g, w_uq_p)
    kvn, kv = _norm_up("kv_up", proj, (c_kv0, r_kv), kv_norm_g, w_ukv)
    qh, kh, vh = _pack_heads(q, kv, proj, c_kr0, tables, n_heads)
    o, probs, mix = _attn_fwd(qh, kh, vh, attn_out_g, y_conv, conv_width, after=weights.forward(2, (qh, kh, vh)))
    y = _mm_nn("out_proj", mix, w_o, F32, TILE_M, TILE_N)
    x2, h2 = _mid_fwd(x, y, post_mix_g, pre_mlp_g)
    weights.relay(2, weights.forward(3, (h2,)))
    (w_up,) = weights.ready(2, ())
    a, r = _up_fwd(h2, w_up)
    weights.relay(3, (a,))
    (w_down,) = weights.ready(3, ())
    m = _down_fwd(a, w_down)

    d_out, d_m, dg_post_mlp, loss_part = _head(m, x2, tgt, post_mlp_g)
    core = grads.core
    away = _down_half_grad("down_bwd_w_away", a, d_m, core, home=False)
    d_up = _down_bwd_act(d_m, w_down, r, after=grads.send_away(0, away))
    sums = _down_half_grad("down_bwd_w_home", a, d_m, core, home=True, received=grads.received(0, (d_up,)))
    away = _up_half_grad("up_bwd_w_away", h2, d_up, core, home=False, after=grads.send_sums(0, (sums,)))
    d_h2 = _up_bwd_act(d_up, w_up, after=grads.send_away(1, away))
    sums = _up_half_grad("up_bwd_w_home", h2, d_up, core, home=True, received=grads.received(1, (d_h2,)))
    d_x2, d_y, dg_pre_mlp, dg_post_mix = _mid_bwd(x2, y, d_out, d_h2, pre_mlp_g, post_mix_g, after=grads.send_sums(1, (sums,)))
    d_mix = _mm_nt("out_proj_bwd_act", d_y, w_o, F32, TILE_M, TILE_N)
    gw_o = _mm_tn("out_proj_bwd_w", mix, d_y, BF16, TILE_M, TILE_N)
    dqh, dkh, dvh, dg_attn = _attn_bwd(qh, kh, vh, o, probs, d_mix, attn_out_g, conv_width)
    d_q, d_kv, d_kr = _unpack_heads(dqh, dkh, dvh, tables, n_heads)
    gw_uq_t = _mm_tn("q_up_bwd_w", d_q, qn, F32, TILE_M, TILE_N)
    gw_ukv = _mm_tn("kv_up_bwd_w", kvn, d_kv, BF16, TILE_M, TILE_N)
    d_cq, dg_q = _up_norm_bwd("q_up_bwd_act", d_q, w_uq_p, proj, (c_q0, r_q), q_norm_g, after=grads.full(2, (gw_o, gw_uq_t, gw_ukv)))
    d_ckv, dg_kv = _up_norm_bwd("kv_up_bwd_act", d_kv, w_ukv, proj, (c_kv0, r_kv), kv_norm_g)
    d_u, d_b, d_c, dg_conv, dw_conv = _conv_bwd(proj, d_mix, conv_w, conv_out_g, n_groups)
    d_proj = jnp.concatenate([d_u, d_b, d_c, d_cq, d_ckv, d_kr, jnp.zeros((s, in_pad - c_kr0 - LANE), BF16)], axis=1)
    gw_in_t = _mm_tn("in_proj_bwd_w", d_proj, h1, F32, tn_in, TILE_N)
    updated = grads.update_now(0, grads.send_away(3, gw_in_t))
    d_h1 = _mm_nt("in_proj_bwd_act", d_proj, w_in_p, F32, TILE_M, TILE_N, after=grads.full(3, (gw_in_t,), received=updated))
    grad_x, dg_pre_mix = _first_bwd(x, pre_mix_g, d_h1, d_x2)

    small = [dg_pre_mix, dg_q, dg_kv, dg_conv, dg_attn, dg_post_mix, dg_pre_mlp, dg_post_mlp,
             dw_conv[0], dw_conv[1], dw_conv[2], loss_part]
    return grad_x, jnp.concatenate(small, axis=1)


HBM = pl.BlockSpec(memory_space=pltpu.HBM)
SEM = pl.BlockSpec(memory_space=pltpu.SEMAPHORE)
IN_VMEM = pl.BlockSpec(memory_space=pltpu.VMEM)
SPLIT = pltpu.CompilerParams(has_side_effects=pltpu.SideEffectType.DATAFLOW_SIDE_EFFECTING)


def _in_hbm(a):
    return pltpu.with_memory_space_constraint(a, pltpu.HBM)


def _hbm_like(a):
    return pltpu.HBM(a.shape, a.dtype)


def _place():
    x, y, c = lax.axis_index("x"), lax.axis_index("y"), lax.axis_index("c")
    other_chips = [(1 - x, y), (x, 1 - y), (1 - x, 1 - y)]
    return x, y, c, other_chips


def _block(px, py, pc):
    return 4 * px + 2 * py + pc


def _await(block, sem):
    pltpu.make_async_copy(block, block, sem).wait()


def _relay_route(x, y, c):
    came_from = ((1 - x) * (1 - c) + x * c, y * (1 - c) + (1 - y) * c)
    goes_to = (x * (1 - c) + (1 - x) * c, (1 - y) * (1 - c) + y * c)
    return came_from, goes_to


def _gather_start(name, shards, groups, relayed=(), after=()):
    n, ng = len(shards), len(groups)
    lands = [lax.empty((N_DEV, *a.shape), a.dtype) for a in shards]

    def body(*refs):
        src, land = refs[:n], refs[n:2 * n]
        sems, token = refs[2 * n + len(after):2 * n + len(after) + 2 * ng], refs[-1]
        x, y, c, chips = _place()
        targets = [(x, y, 1 - c)] + [(*chip, c) for chip in chips]
        for gi, group in enumerate(groups):
            for i, w in enumerate(group):
                for k, to in enumerate(targets[:3] if gi in relayed else targets):
                    pltpu.make_async_remote_copy(
                        src_ref=src[w], dst_ref=land[w].at[_block(x, y, c)],
                        send_sem=sems[2 * gi].at[4 * i + k], recv_sem=sems[2 * gi + 1].at[4 * i + k],
                        device_id=to, device_id_type=MESH).start()
        token[...] = jnp.zeros_like(token)

    sem_shapes = [pltpu.SemaphoreType.DMA((4 * len(g),)) for g in groups for _ in range(2)]
    out = pl.pallas_call(
        body, name=name,
        in_specs=[HBM] * (2 * n) + [ANY] * len(after),
        out_specs=[SEM] * (2 * ng) + [HBM] * (2 * n) + [IN_VMEM],
        out_shape=sem_shapes + [_hbm_like(a) for a in shards] + [_hbm_like(a) for a in lands]
        + [jax.ShapeDtypeStruct((SUBLANE, LANE), F32)],
        input_output_aliases={i: 2 * ng + i for i in range(2 * n)},
        compiler_params=SPLIT,
    )(*[_in_hbm(a) for a in shards], *[_in_hbm(a) for a in lands], *after)
    sems = [(out[2 * gi], out[2 * gi + 1]) for gi in range(ng)]
    return sems, out[2 * ng:2 * ng + n], out[2 * ng + n:2 * ng + 2 * n], out[-1]


def _gather_forward(name, shards, lands, send1, recv1, after, relayed=False):
    n = len(lands)

    def body(*refs):
        src, land = refs[:n], refs[n:2 * n]
        s1, r1 = refs[2 * n], refs[2 * n + 1]
        s2, r2 = refs[2 * n + 2 + len(after)], refs[2 * n + 3 + len(after)]
        x, y, c, chips = _place()
        me, sibling = (x, y, c), (x, y, 1 - c)
        for j, chip in enumerate(chips[:2] if relayed else chips):
            for i in range(n):
                blk = land[i].at[_block(*chip, c)]
                pltpu.make_async_remote_copy(src_ref=blk, dst_ref=blk, send_sem=s1.at[4 * i + 1 + j], recv_sem=r1.at[4 * i + 1 + j],
                                             device_id=me, device_id_type=MESH).wait_recv()
                pltpu.make_async_remote_copy(src_ref=blk, dst_ref=blk, send_sem=s2.at[3 * i + j], recv_sem=r2.at[3 * i + j],
                                             device_id=sibling, device_id_type=MESH).start()
        if relayed:
            came_from, goes_to = _relay_route(x, y, c)
            for i in range(n):
                blk = land[i].at[_block(*came_from, c)]
                pltpu.make_async_remote_copy(src_ref=blk, dst_ref=blk, send_sem=s2.at[3 * i + 2], recv_sem=r2.at[3 * i + 2],
                                             device_id=(*goes_to, c), device_id_type=MESH).start()
        for i in range(n):
            blk = land[i].at[_block(x, y, 1 - c)]
            pltpu.make_async_remote_copy(src_ref=blk, dst_ref=blk, send_sem=s1.at[4 * i], recv_sem=r1.at[4 * i],
                                         device_id=me, device_id_type=MESH).wait_recv()
            for k in range(3 if relayed else 4):
                pltpu.make_async_remote_copy(src_ref=src[i], dst_ref=land[i].at[_block(x, y, c)], send_sem=s1.at[4 * i + k],
                                             recv_sem=r1.at[4 * i + k], device_id=sibling, device_id_type=MESH).wait_send()

    sem = pltpu.SemaphoreType.DMA((3 * n,))
    out = pl.pallas_call(
        body, name=name,
        in_specs=[HBM] * (2 * n) + [SEM, SEM] + [ANY] * len(after),
        out_specs=[SEM, SEM] + [HBM] * n,
        out_shape=[sem, sem] + [_hbm_like(a) for a in lands],
        input_output_aliases={n + i: 2 + i for i in range(n)},
        compiler_params=SPLIT,
    )(*shards, *lands, send1, recv1, *after)
    return (out[0], out[1]), out[2:]


def _gather_relay_forward(name, lands, send2, recv2, after):
    n = len(lands)

    def body(*refs):
        land, s2, r2 = refs[:n], refs[n], refs[n + 1]
        s3, r3 = refs[n + 2 + len(after)], refs[n + 3 + len(after)]
        x, y, c, _ = _place()
        me, sibling = (x, y, c), (x, y, 1 - c)
        came_from, _ = _relay_route(x, y, c)
        for i in range(n):
            blk = land[i].at[_block(1 - x, 1 - y, c)]
            pltpu.make_async_remote_copy(src_ref=blk, dst_ref=blk, send_sem=s2.at[3 * i + 2], recv_sem=r2.at[3 * i + 2],
                                         device_id=me, device_id_type=MESH).wait_recv()
            pltpu.make_async_remote_copy(src_ref=blk, dst_ref=blk, send_sem=s3.at[i], recv_sem=r3.at[i],
                                         device_id=sibling, device_id_type=MESH).start()
            sent = land[i].at[_block(*came_from, c)]
            pltpu.make_async_remote_copy(src_ref=sent, dst_ref=sent, send_sem=s2.at[3 * i + 2], recv_sem=r2.at[3 * i + 2],
                                         device_id=me, device_id_type=MESH).wait_send()

    sem = pltpu.SemaphoreType.DMA((n,))
    out = pl.pallas_call(
        body, name=name,
        in_specs=[HBM] * n + [SEM, SEM] + [ANY] * len(after),
        out_specs=[SEM, SEM] + [HBM] * n,
        out_shape=[sem, sem] + [_hbm_like(a) for a in lands],
        input_output_aliases={i: 2 + i for i in range(n)},
        compiler_params=SPLIT,
    )(*lands, send2, recv2, *after)
    return (out[0], out[1]), out[2:]


def _gather_wait(name, lands, send2, recv2, after, relay_sems=None):
    n = len(lands)
    n_sems = 2 if relay_sems is None else 4

    def body(*refs):
        land, s2, r2 = refs[:n], refs[n], refs[n + 1]
        for i in range(n):
            for j in range(3 if relay_sems is None else 2):
                _await(land[i].at[0], r2.at[3 * i + j])
                _await(land[i].at[0], s2.at[3 * i + j])
            if relay_sems is not None:
                _await(land[i].at[0], refs[n + 3].at[i])
                _await(land[i].at[0], refs[n + 2].at[i])

    return pl.pallas_call(
        body, name=name,
        in_specs=[HBM] * n + [SEM] * n_sems + [ANY] * len(after), out_specs=[HBM] * n, out_shape=[_hbm_like(a) for a in lands],
        input_output_aliases={i: i for i in range(n)},
        compiler_params=SPLIT,
    )(*lands, send2, recv2, *(relay_sems or ()), *after)


def _pair_exchange(name, grads, shard_rows):
    n = len(grads)
    shapes = [(g.shape[1:] if r is None else (r, g.shape[1])) for g, r in zip(grads, shard_rows)]

    def body(*refs):
        ins, recv = refs[:n], refs[n:2 * n]
        send_sems, recv_sems = refs[2 * n:]
        x, y, c, _ = _place()
        sends = []
        for w in range(n):
            for k in range(N_CHIP):
                j, r = 2 * k + 1 - c, shard_rows[w]
                src = ins[w].at[j] if r is None else ins[w].at[pl.ds(pl.multiple_of(j * r, SUBLANE), r), :]
                sends.append(pltpu.make_async_remote_copy(
                    src_ref=src, dst_ref=recv[w].at[k],
                    send_sem=send_sems.at[w, k], recv_sem=recv_sems.at[w, k],
                    device_id=(x, y, 1 - c), device_id_type=MESH))
        for cp in sends:
            cp.start()
        for cp in sends:
            cp.wait()

    return pl.pallas_call(
        body, name=name,
        in_specs=[ANY] * n, out_specs=[ANY] * n,
        out_shape=[jax.ShapeDtypeStruct((N_CHIP, *shape), g.dtype) for g, shape in zip(grads, shapes)],
        scratch_shapes=[pltpu.SemaphoreType.DMA((n, N_CHIP))] * 2,
    )(*grads)


def _pair_sum_rows(name, grad, received, core):
    _, r, c = received.shape
    tc = _fit(c, 512)

    def body(core_ref, a_ref, b_ref, o_ref):
        o_ref[...] = (a_ref[...] + b_ref[...]).astype(o_ref.dtype)

    spec = pl.BlockSpec((None, r, tc), lambda k, i, core_ref: (k, 0, i))
    return pl.pallas_call(
        body, name=name,
        grid_spec=pltpu.PrefetchScalarGridSpec(
            num_scalar_prefetch=1, grid=(N_CHIP, c // tc),
            in_specs=[pl.BlockSpec((r, tc), lambda k, i, core_ref: (2 * k + core_ref[0], i)), spec],
            out_specs=spec),
        out_shape=jax.ShapeDtypeStruct(received.shape, BF16),
        compiler_params=_params("parallel", "parallel"),
    )(core, grad, received)


def _pair_sum(name, grad, received, core):
    _, r, c = received.shape
    rows = min(ROWS, r)
    assert r % rows == 0

    def body(core_ref, a_ref, b_ref, o_ref):
        o_ref[...] = (a_ref[...].astype(F32) + b_ref[...].astype(F32)).astype(o_ref.dtype)

    spec = pl.BlockSpec((None, rows, c), lambda k, i, core_ref: (k, i, 0))
    return pl.pallas_call(
        body, name=name,
        grid_spec=pltpu.PrefetchScalarGridSpec(
            num_scalar_prefetch=1, grid=(N_CHIP, r // rows),
            in_specs=[pl.BlockSpec((None, None, rows, c), lambda k, i, core_ref: (k, core_ref[0], i, 0)), spec],
            out_specs=spec),
        out_shape=jax.ShapeDtypeStruct(received.shape, received.dtype),
        compiler_params=_params("parallel", "parallel"),
    )(core, grad.reshape(N_CHIP, 2, r, c), received)


def _away_shard(src, k, c, shard_rows):
    if shard_rows is None:
        return src.at[k]
    return src.at[pl.ds(pl.multiple_of((2 * k + 1 - c) * shard_rows, SUBLANE), shard_rows), :]


def _pair_send_start(name, away, shard_rows=None):
    shape = away.shape if shard_rows is None else (N_CHIP, shard_rows, away.shape[1])
    land = lax.empty(shape, away.dtype)

    def body(src, dst, send, recv, src_thru, dst_thru, token):
        x, y, c, _ = _place()
        for k in range(N_CHIP):
            pltpu.make_async_remote_copy(src_ref=_away_shard(src, k, c, shard_rows), dst_ref=dst.at[k], send_sem=send.at[k],
                                         recv_sem=recv.at[k], device_id=(x, y, 1 - c), device_id_type=MESH).start()
        token[...] = jnp.zeros_like(token)

    sem = pltpu.SemaphoreType.DMA((N_CHIP,))
    out = pl.pallas_call(
        body, name=name,
        in_specs=[HBM, HBM], out_specs=[SEM, SEM, HBM, HBM, IN_VMEM],
        out_shape=[sem, sem, _hbm_like(away), _hbm_like(land), jax.ShapeDtypeStruct((SUBLANE, LANE), F32)],
        input_output_aliases={0: 2, 1: 3},
        compiler_params=SPLIT,
    )(_in_hbm(away), _in_hbm(land))
    return (out[0], out[1]), out[2], out[3], out[4]


def _pair_send_wait(name, sems, src, land, after, shard_rows=None):
    def body(src_ref, dst_ref, send, recv, *rest):
        for k in range(N_CHIP):
            _await(dst_ref.at[k], send.at[k])
            _await(dst_ref.at[k], recv.at[k])

    return pl.pallas_call(
        body, name=name,
        in_specs=[HBM, HBM, SEM, SEM] + [ANY] * len(after), out_specs=HBM, out_shape=_hbm_like(land),
        input_output_aliases={1: 0},
        compiler_params=SPLIT,
    )(src, land, *sems, *after)


def _chip_send_start(name, sums):
    n = len(sums)
    lands = [lax.empty(a.shape, a.dtype) for a in sums]

    def body(*refs):
        src, land = refs[:n], refs[n:2 * n]
        send, recv, token = refs[2 * n], refs[2 * n + 1], refs[-1]
        x, y, c, chips = _place()
        for w in range(n):
            for j, (px, py) in enumerate(chips):
                pltpu.make_async_remote_copy(
                    src_ref=src[w].at[2 * px + py], dst_ref=land[w].at[2 * x + y],
                    send_sem=send.at[3 * w + j], recv_sem=recv.at[3 * w + j],
                    device_id=(px, py, c), device_id_type=MESH).start()
        token[...] = jnp.zeros_like(token)

    sem = pltpu.SemaphoreType.DMA((3 * n,))
    out = pl.pallas_call(
        body, name=name,
        in_specs=[HBM] * (2 * n),
        out_specs=[SEM, SEM] + [HBM] * (2 * n) + [IN_VMEM],
        out_shape=[sem, sem] + [_hbm_like(a) for a in sums] + [_hbm_like(a) for a in lands]
        + [jax.ShapeDtypeStruct((SUBLANE, LANE), F32)],
        input_output_aliases={i: 2 + i for i in range(2 * n)},
        compiler_params=SPLIT,
    )(*[_in_hbm(a) for a in sums], *[_in_hbm(a) for a in lands])
    return (out[0], out[1]), out[2:2 + n], out[2 + n:2 + 2 * n], out[-1]


def _chip_send_wait(name, groups, after):
    counts = [len(g[1]) for g in groups]
    n = sum(counts)

    def body(*refs):
        land = refs[n:2 * n]
        sems = refs[2 * n:2 * n + 2 * len(groups)]
        w = 0
        for gi, count in enumerate(counts):
            for i in range(count):
                for j in range(3):
                    _await(land[w].at[0], sems[2 * gi].at[3 * i + j])
                    _await(land[w].at[0], sems[2 * gi + 1].at[3 * i + j])
                w += 1

    sums = [a for g in groups for a in g[1]]
    lands = [a for g in groups for a in g[2]]
    sems = [s for g in groups for s in g[0]]
    return pl.pallas_call(
        body, name=name,
        in_specs=[HBM] * (2 * n) + [SEM] * len(sems) + [ANY] * len(after),
        out_specs=[HBM] * n, out_shape=[_hbm_like(a) for a in lands],
        input_output_aliases={n + i: i for i in range(n)},
        compiler_params=SPLIT,
    )(*sums, *lands, *sems, *after)


def _small_all_reduce(part, after=()):
    _, w = part.shape

    def body(p_ref, *rest):
        o_ref, buf, send_sems, recv_sems = rest[len(after):]
        x, y, c, _ = _place()
        me = 4 * x + 2 * y + c
        buf[me] = jnp.sum(p_ref[...], axis=0, keepdims=True)
        copies = []
        for k in range(1, N_DEV):
            dx, dy, dc = (k >> 2) & 1, (k >> 1) & 1, k & 1
            copies.append(pltpu.make_async_remote_copy(
                src_ref=buf.at[me], dst_ref=buf.at[me], send_sem=send_sems.at[k - 1], recv_sem=recv_sems.at[k - 1],
                device_id=(x ^ dx, y ^ dy, c ^ dc), device_id_type=MESH))
        for cp in copies:
            cp.start()
        for cp in copies:
            cp.wait()
        tot = buf[0]
        for d in range(1, N_DEV):
            tot = tot + buf[d]
        o_ref[...] = tot
        loss = jnp.sum(tot[:, w - LANE:], axis=1, keepdims=True)
        o_ref[:, w - LANE:] = jnp.broadcast_to(loss, (1, LANE))

    return pl.pallas_call(
        body, name="small_all_reduce",
        in_specs=[IN_VMEM] + [ANY] * len(after), out_specs=IN_VMEM,
        out_shape=jax.ShapeDtypeStruct((1, w), F32),
        scratch_shapes=[pltpu.VMEM((N_DEV, 1, w), F32), pltpu.SemaphoreType.DMA((N_DEV - 1,)), pltpu.SemaphoreType.DMA((N_DEV - 1,))],
        compiler_params=pltpu.CompilerParams(vmem_limit_bytes=VMEM_LIMIT_BYTES),
    )(part, *after)


def _adamw(w, g, m, v):
    m = ADAM_B1 * m + (1.0 - ADAM_B1) * g
    v = ADAM_B2 * v + (1.0 - ADAM_B2) * (g * g)
    m_hat = m / (1.0 - ADAM_B1 ** ADAM_STEP)
    v_hat = v / (1.0 - ADAM_B2 ** ADAM_STEP)
    delta = -ADAM_LR * (m_hat / (jnp.sqrt(v_hat) + ADAM_EPS) + ADAM_WD * w)
    return delta, m, v


def _sum_adam_block(chip_ref, p_ref, own_ref, w_ref, m_ref, v_ref, g_ref, d_ref, mo_ref, vo_ref):
    g = None
    for k in range(N_CHIP):
        term = jnp.where(chip_ref[0] == k, own_ref[...], p_ref[k]).astype(F32)
        g = term if g is None else g + term
    g_ref[...] = g
    d_ref[...], mo_ref[...], vo_ref[...] = _adamw(w_ref[...], g, m_ref[...], v_ref[...])


def _sum_adam(name, parts, sums, chip, w, m, v, after=()):
    _, r, c = w.shape
    n_after = len(after)
    by_rows = r % ROWS == 0 or r < ROWS
    tr, tc = (min(ROWS, r), c) if by_rows else (r, _fit(c, 512))
    at = (lambda i: (i, 0)) if by_rows else (lambda i: (0, i))

    def body(chip_ref, p_ref, own_ref, w_ref, m_ref, v_ref, *rest):
        _sum_adam_block(chip_ref, p_ref, own_ref, w_ref, m_ref, v_ref, *rest[n_after:])

    blk = pl.BlockSpec((None, tr, tc), lambda i, chip_ref: (0, *at(i)))
    out = jax.ShapeDtypeStruct((1, r, c), F32)
    return pl.pallas_call(
        body, name=name,
        grid_spec=pltpu.PrefetchScalarGridSpec(
            num_scalar_prefetch=1, grid=(r // tr if by_rows else c // tc,),
            in_specs=[pl.BlockSpec((N_CHIP, tr, tc), lambda i, chip_ref: (0, *at(i))),
                      pl.BlockSpec((None, tr, tc), lambda i, chip_ref: (chip_ref[0], *at(i))), blk, blk, blk]
            + [ANY] * n_after,
            out_specs=[blk] * 4),
        out_shape=[out] * 4,
        compiler_params=_params("parallel"),
    )(chip, parts, sums, w, m, v, *after)


def _adam_gains(total, ws, ms, vs):
    n = len(ws)
    widths = [w.shape[1] for w in ws]

    def body(t_ref, *refs):
        w_refs, m_refs, v_refs, outs = refs[:n], refs[n:2 * n], refs[2 * n:3 * n], refs[3 * n:]
        off = 0
        for i in range(n):
            g = t_ref[:, off:off + widths[i]]
            off += widths[i]
            g_ref, d_ref, mo_ref, vo_ref = outs[4 * i:4 * i + 4]
            g_ref[...] = g
            d_ref[...], mo_ref[...], vo_ref[...] = _adamw(w_refs[i][...], g, m_refs[i][...], v_refs[i][...])

    out = pl.pallas_call(
        body, name="adam_gains",
        out_shape=[jax.ShapeDtypeStruct(w.shape, F32) for w in ws for _ in range(4)],
    )(total, *ws, *ms, *vs)
    return [tuple(out[4 * i:4 * i + 4]) for i in range(n)]


def _adam_taps(total, first_col, device, w, m, v):
    _, n_taps, cw = w.shape
    col_block = lambda t, dev: (0, first_col // cw + t * N_DEV + dev[0])
    tap = pl.BlockSpec((None, 1, cw), lambda t, dev: (t, 0, 0))

    def body(dev_ref, t_ref, w_ref, m_ref, v_ref, g_ref, d_ref, mo_ref, vo_ref):
        g = t_ref[...]
        g_ref[...] = g
        d_ref[...], mo_ref[...], vo_ref[...] = _adamw(w_ref[...], g, m_ref[...], v_ref[...])

    shape3 = (n_taps, 1, cw)
    out = pl.pallas_call(
        body, name="adam_taps",
        grid_spec=pltpu.PrefetchScalarGridSpec(
            num_scalar_prefetch=1, grid=(n_taps,),
            in_specs=[pl.BlockSpec((1, cw), col_block), tap, tap, tap], out_specs=[tap] * 4),
        out_shape=[jax.ShapeDtypeStruct(shape3, F32)] * 4,
    )(device, total, w.reshape(shape3), m.reshape(shape3), v.reshape(shape3))
    return tuple(o.reshape(w.shape) for o in out)


def kernel(x, pre_mix_g, w_in, conv_w, q_norm_g, w_uq, kv_norm_g, w_ukv, conv_out_g, attn_out_g, w_o, post_mix_g, pre_mlp_g, w_up, w_down, post_mlp_g, loss_target, m_pre_mix_g, m_w_in, m_conv_w, m_q_norm_g, m_w_uq, m_kv_norm_g, m_w_ukv, m_conv_out_g, m_attn_out_g, m_w_o, m_post_mix_g, m_pre_mlp_g, m_w_up, m_w_down, m_post_mlp_g, v_pre_mix_g, v_w_in, v_conv_w, v_q_norm_g, v_w_uq, v_kv_norm_g, v_w_ukv, v_conv_out_g, v_attn_out_g, v_w_o, v_post_mix_g, v_pre_mlp_g, v_w_up, v_w_down, v_post_mlp_g):
    me = 4 * lax.axis_index("x") + 2 * lax.axis_index("y") + lax.axis_index("c")
    core = lax.axis_index("c").astype(jnp.int32).reshape(1)
    chip = (2 * lax.axis_index("x") + lax.axis_index("y")).astype(jnp.int32).reshape(1)
    gains = (pre_mix_g, q_norm_g, kv_norm_g, conv_out_g, attn_out_g, post_mix_g, pre_mlp_g, post_mlp_g)
    gain_m = (m_pre_mix_g, m_q_norm_g, m_kv_norm_g, m_conv_out_g, m_attn_out_g, m_post_mix_g, m_pre_mlp_g, m_post_mlp_g)
    gain_v = (v_pre_mix_g, v_q_norm_g, v_kv_norm_g, v_conv_out_g, v_attn_out_g, v_post_mix_g, v_pre_mlp_g, v_post_mlp_g)
    names = ("w_in", "w_uq", "w_ukv", "w_o", "w_up", "w_down")
    big = dict(zip(names, (w_in, w_uq, w_ukv, w_o, w_up, w_down)))
    big_m = dict(zip(names, (m_w_in, m_w_uq, m_w_ukv, m_w_o, m_w_up, m_w_down)))
    big_v = dict(zip(names, (v_w_in, v_w_uq, v_w_ukv, v_w_o, v_w_up, v_w_down)))
    n_heads = attn_out_g.shape[1] // HEAD
    n_taps = conv_w.shape[1]

    gathered = ("w_in", "conv", "w_uq", "w_ukv", "w_o", "w_up", "w_down")
    gather_groups = ((0, 1), (2, 3, 4), (5,), (6,))
    taps = jnp.pad(conv_w[0], ((0, SUBLANE - n_taps), (0, 0)))
    relayed_groups = (0, 2, 3)
    sems1, shards, lands, token = _gather_start("gather_start_first", [w_in[0].astype(BF16), taps], ((0, 1),), relayed=(0,))
    sems1, shards, lands = list(sems1), list(shards), list(lands)
    behind = token[0, 0]
    rest = [(big[nm][0] + behind).astype(BF16) for nm in gathered[2:]]

    def start_more(name, some, groups, relayed, after):
        sems_b, shards_b, lands_b, _ = _gather_start(name, some, groups, relayed=relayed, after=after)
        sems1.extend(sems_b)
        shards.extend(shards_b)
        lands.extend(lands_b)

    start_rest = lambda after: start_more("gather_start_rest", rest, ((0, 1, 2), (3,), (4,)), (1, 2), after)

    cols = lambda a: jnp.concatenate([a[j] for j in range(N_DEV)], axis=1)
    rows = lambda a: a.reshape(N_DEV * a.shape[1], a.shape[2])
    device = me.astype(jnp.int32).reshape(1)
    own_in = lambda a, shard: lax.dynamic_update_index_in_dim(a, shard, me, 0)
    q_pieces = [(h, 0, HEAD) for h in range(n_heads)] + [(h, HEAD, QK) for h in range(n_heads)]
    ready = {
        "w_in": lambda a, shard: _join_col_shards("join_w_in", a, shard, device),
        "conv": lambda a, shard: cols(own_in(a, shard))[:n_taps],
        "w_uq": lambda a, shard: _join_col_shards("join_w_uq", a, shard, device, q_pieces),
        "w_ukv": lambda a, shard: cols(own_in(a, shard)),
        "w_o": lambda a, shard: rows(own_in(a, shard)),
        "w_up": own_in,
        "w_down": lambda a, shard: rows(own_in(a, shard)),
    }
    assert w_uq.shape[2] == QK

    class Weights:
        def __init__(self):
            self.passed, self.relayed = {}, {}

        def forward(self, group, after):
            idx = gather_groups[group]
            if group == 0:
                after = (*after, *rest)
            self.passed[group] = _gather_forward(f"gather_forward_{group}", [shards[i] for i in idx], [lands[i] for i in idx],
                                                 *sems1[group], after, relayed=group in relayed_groups)
            return tuple(self.passed[group][1])

        def relay(self, group, after):
            sems2, mid = self.passed[group]
            self.relayed[group], mid = _gather_relay_forward(f"gather_relay_{group}", mid, *sems2, after)
            self.passed[group] = (sems2, mid)
            if group == 0:
                start_rest(tuple(mid))
            return tuple(mid)

        def ready(self, group, after):
            sems2, mid = self.passed[group]
            full = _gather_wait(f"gather_wait_{group}", mid, *sems2, after, relay_sems=self.relayed.get(group))
            out = []
            return [ready[gathered[i]](a, shards[i]) for i, a in zip(gather_groups[group], full)]

    weights = Weights()

    col_blocks = lambda g: g.reshape(g.shape[0], N_DEV, g.shape[1] // N_DEV).transpose(1, 0, 2)
    row_blocks = lambda g: g.reshape(N_DEV, g.shape[0] // N_DEV, g.shape[1])
    grad_groups = (("w_down",), ("w_up",), ("w_o", "w_uq", "w_ukv"), ("w_in",))
    transposed = {"w_in": w_in.shape[2], "w_uq": w_uq.shape[2]}
    to_blocks = {
        "w_in": lambda g: g, "w_uq": lambda g: _unpermute_q_rows(g, n_heads),
        "w_ukv": col_blocks, "w_o": row_blocks, "w_up": lambda g: g, "w_down": row_blocks,
    }
    in_flight = []

    class Grads:
        def __init__(self):
            self.core = core
            self.away = {}

        def send_sums(self, group, sums):
            sems, sums, parts, tok = _chip_send_start(f"chip_send_start_{group}", list(sums))
            in_flight.append((sems, sums, parts))
            return (tok,)

        def full(self, group, arrays, received=None):
            nms = grad_groups[group]
            if received is None:
                blocks = [to_blocks[nm](g) for nm, g in zip(nms, arrays)]
                got = _pair_exchange(f"pair_exchange_{group}", blocks, [transposed.get(nm) for nm in nms])
            else:
                blocks, got = [self.away[group][1]], [self.received(group, received)]
            sums = [(_pair_sum_rows if nm in transposed else _pair_sum)(f"pair_sum_{nm}", g, r, core)
                    for nm, g, r in zip(nms, blocks, got)]
            return self.send_sums(group, sums)

        def send_away(self, group, half):
            nm = grad_groups[group][0]
            rows = transposed.get(nm)
            sems, src, land, tok = _pair_send_start(f"pair_send_start_{group}", half if rows is None else to_blocks[nm](half), rows)
            self.away[group] = (sems, src, land, rows)
            return (tok,)

        def received(self, group, after):
            sems, src, land, rows = self.away[group]
            return _pair_send_wait(f"pair_send_wait_{group}", sems, src, land, after, rows)

        def update_now(self, group, after):
            return update(str(group), group, group + 1, after)

    big_out = {}

    def update(tag, first, last, after):
        picked = [i for i in range(first, last) if grad_groups[i][0] not in big_out]
        groups = [in_flight[i] for i in picked]
        parts = _chip_send_wait("chip_send_wait_" + tag, groups, after)
        nms = [nm for i in picked for nm in grad_groups[i]]
        sums = [a for _, s, _ in groups for a in s]
        for nm, p, s in zip(nms, parts, sums):
            view = (lambda a: jnp.swapaxes(a, 1, 2)) if nm in transposed else (lambda a: a)
            out = _sum_adam("adam_" + nm, p, s, chip, view(big[nm]), view(big_m[nm]), view(big_v[nm]), after=after)
            after = (out[0],)
            big_out[nm] = [view(o) for o in out]
        return after

    grad_x, small = _local_step(x[0], loss_target[0], gains, weights, Grads(), first_after=(token,))

    after = update("early", 0, len(in_flight) - 1, (grad_x,))
    total = _small_all_reduce(small, after=after)
    update("late", len(in_flight) - 1, len(in_flight), (total,))
    big_out = [big_out[nm] for nm in names]

    gain_out = _adam_gains(total, gains, gain_m, gain_v)
    taps_out = _adam_taps(total, sum(g.shape[1] for g in gains), me.astype(jnp.int32).reshape(1), conv_w, m_conv_w, v_conv_w)
    loss = total[0, total.shape[1] - 1]

    order = (0, "w_in", "conv", 1, "w_uq", 2, "w_ukv", 3, 4, "w_o", 5, 6, "w_up", "w_down", 7)
    by_name = dict(zip(names, big_out))
    outs = [loss, grad_x[None]]
    for kind in range(4):
        for item in order:
            if item == "conv":
                outs.append(taps_out[kind])
            elif isinstance(item, int):
                outs.append(gain_out[item][kind])
            else:
                outs.append(by_name[item][kind])
    return tuple(outs)
```

```python
import math

import jax
import jax.numpy as jnp
from jax import lax
from jax.experimental import pallas as pl
from jax.experimental.pallas import tpu as pltpu

F32 = jnp.float32
BF16 = jnp.bfloat16

EPS = 1e-6
NEG_INF = -1e30
HEAD = 128
ROPE = 64
QK = HEAD + ROPE
CHUNK = 64
ROPE_THETA = 10000.0
ADAM_LR, ADAM_B1, ADAM_B2, ADAM_EPS, ADAM_WD, ADAM_STEP = 0.001, 0.9, 0.999, 1e-08, 0.01, 10

LANE = 128
SUBLANE = 8
VMEM_LIMIT_BYTES = 56 * 1024 * 1024

N_DEV = 8
N_CHIP = 4
MESH = pl.DeviceIdType.MESH


def _params(*sem):
    return pltpu.CompilerParams(dimension_semantics=sem, vmem_limit_bytes=VMEM_LIMIT_BYTES)


ANY = pl.BlockSpec(memory_space=pl.ANY)


def _call(body, *, in_specs, after=(), **kw):
    n_in, n_after = len(in_specs), len(after)

    def ordered(*refs):
        body(*refs[:n_in], *refs[n_in + n_after:])

    call = pl.pallas_call(ordered, in_specs=[*in_specs, *[ANY] * n_after], **kw)
    return lambda *operands: call(*operands, *after)


def _sublane_sum(v):
    r, w = v.shape
    return jnp.sum(v.reshape(r // SUBLANE, SUBLANE, w), axis=0)


def _rstd(x):
    return lax.rsqrt(jnp.mean(x * x, axis=-1, keepdims=True) + EPS)


def _rms_bwd(x, g, dy):
    r = _rstd(x)
    xh = x * r
    dxh = dy * g
    dx = r * (dxh - xh * jnp.mean(dxh * xh, axis=-1, keepdims=True))
    return dx, dy * xh


def _accumulate(ref, val, step):
    @pl.when(step == 0)
    def _():
        ref[...] = val

    @pl.when(step > 0)
    def _():
        ref[...] += val


NN = ((1,), (0,))
NT = ((1,), (1,))
TN = ((0,), (0,))


def _matmul(name, a, b, *, grid, a_spec, b_spec, out_shape, out_specs, contract, nk=1, acc_shape=None,
            extras=(), extra_specs=(), epilogue=None, after=()):
    multi = isinstance(out_shape, (tuple, list))
    out_shapes = tuple(out_shape) if multi else (out_shape,)
    n_out = len(out_shapes)
    n_extra = len(extras)

    def body(a_ref, b_ref, *rest):
        x_refs = rest[:n_extra]
        o_refs = rest[n_extra:n_extra + n_out]

        def emit(acc):
            vals = epilogue(acc, *[r[...] for r in x_refs]) if epilogue else (acc,)
            for r, v in zip(o_refs, vals):
                r[...] = v.astype(r.dtype)

        p = lax.dot_general(a_ref[...], b_ref[...], (contract, ((), ())), preferred_element_type=F32)
        if nk == 1:
            emit(p)
        else:
            acc_ref = rest[n_extra + n_out]
            k = pl.program_id(2)
            _accumulate(acc_ref, p, k)

            @pl.when(k == nk - 1)
            def _():
                emit(acc_ref[...])

    sem = ("parallel", "parallel") + (("arbitrary",) if nk > 1 else ())
    return _call(
        body, name=name, grid=grid, after=after,
        in_specs=[a_spec, b_spec, *extra_specs],
        out_specs=out_specs,
        out_shape=out_shape,
        scratch_shapes=[pltpu.VMEM(acc_shape, F32)] if nk > 1 else [],
        compiler_params=_params(*sem),
    )(a, b, *extras)


def _fit(n, tile):
    if n <= tile:
        return n
    t = tile - tile % LANE
    while n % t:
        t -= LANE
    return t


def _mm_nn(name, a, b, out_dtype, tm, tn, after=()):
    m, k = a.shape
    n = b.shape[1]
    tm, tn = _fit(m, tm), _fit(n, tn)
    return _matmul(name, a, b, grid=(m // tm, n // tn), after=after,
                   a_spec=pl.BlockSpec((tm, k), lambda i, j: (i, 0)),
                   b_spec=pl.BlockSpec((k, tn), lambda i, j: (0, j)),
                   out_shape=jax.ShapeDtypeStruct((m, n), out_dtype),
                   out_specs=pl.BlockSpec((tm, tn), lambda i, j: (i, j)), contract=NN)


def _mm_nt(name, a, b, out_dtype, tm, tn, after=()):
    m, k = a.shape
    n = b.shape[0]
    tm, tn = _fit(m, tm), _fit(n, tn)
    return _matmul(name, a, b, grid=(m // tm, n // tn), after=after,
                   a_spec=pl.BlockSpec((tm, k), lambda i, j: (i, 0)),
                   b_spec=pl.BlockSpec((tn, k), lambda i, j: (j, 0)),
                   out_shape=jax.ShapeDtypeStruct((m, n), out_dtype),
                   out_specs=pl.BlockSpec((tm, tn), lambda i, j: (i, j)), contract=NT)


def _mm_tn(name, a, b, out_dtype, tm, tn):
    s, m = a.shape
    n = b.shape[1]
    tm, tn = _fit(m, tm), _fit(n, tn)
    return _matmul(name, a, b, grid=(m // tm, n // tn),
                   a_spec=pl.BlockSpec((s, tm), lambda i, j: (0, i)),
                   b_spec=pl.BlockSpec((s, tn), lambda i, j: (0, j)),
                   out_shape=jax.ShapeDtypeStruct((m, n), out_dtype),
                   out_specs=pl.BlockSpec((tm, tn), lambda i, j: (i, j)), contract=TN)


ROWS = 256


def _row_spec(rows, width):
    return pl.BlockSpec((rows, width), lambda i: (i, 0))


def _fixed_spec(rows, width):
    return pl.BlockSpec((rows, width), lambda i: (0, 0))


def _column_pieces(rows, start, width):
    piece = math.gcd(start, width)
    assert piece % LANE == 0
    return [pl.BlockSpec((rows, piece), lambda i, b=start // piece + p: (i, b)) for p in range(width // piece)]


def _rms_fwd(name, x, g, after=()):
    s, w = x.shape
    rows = min(ROWS, s)

    def body(x_ref, g_ref, o_ref):
        xv = x_ref[...]
        o_ref[...] = (xv * _rstd(xv) * g_ref[...]).astype(o_ref.dtype)

    return _call(
        body, name=name, grid=(s // rows,), after=after,
        in_specs=[_row_spec(rows, w), _fixed_spec(1, w)],
        out_specs=_row_spec(rows, w),
        out_shape=jax.ShapeDtypeStruct((s, w), BF16),
        compiler_params=_params("parallel"),
    )(x, g)


def _norm_up(name, x, cols, g, w, after=()):
    s = x.shape[0]
    start, width = cols
    n = w.shape[1]
    tm = min(TILE_M, s)
    pieces = _column_pieces(tm, start, width)
    n_p = len(pieces)

    def body(*refs):
        g_ref, w_ref, xn_ref, o_ref = refs[n_p:]
        xv = refs[0][...] if n_p == 1 else jnp.concatenate([r[...] for r in refs[:n_p]], axis=1)
        xn = (xv * _rstd(xv) * g_ref[...]).astype(BF16)
        xn_ref[...] = xn
        o_ref[...] = jnp.dot(xn, w_ref[...], preferred_element_type=F32)

    return _call(
        body, name=name, grid=(s // tm,), after=after,
        in_specs=[*pieces, _fixed_spec(1, width), _fixed_spec(width, n)],
        out_specs=[_row_spec(tm, width), _row_spec(tm, n)],
        out_shape=[jax.ShapeDtypeStruct((s, width), BF16), jax.ShapeDtypeStruct((s, n), F32)],
        compiler_params=_params("parallel"),
    )(*[x] * n_p, g, w)


def _up_norm_bwd(name, dy, w, x, cols, g, after=()):
    s, n = dy.shape
    start, width = cols
    tm = min(TILE_M, s)
    pieces = _column_pieces(tm, start, width)
    n_p = len(pieces)

    def body(dy_ref, w_ref, *refs):
        g_ref, dx_ref, dg_ref = refs[n_p:]
        xv = refs[0][...] if n_p == 1 else jnp.concatenate([r[...] for r in refs[:n_p]], axis=1)
        dxn = lax.dot_general(dy_ref[...], w_ref[...], (NT, ((), ())), preferred_element_type=F32)
        dx, dgc = _rms_bwd(xv, g_ref[...], dxn)
        dx_ref[...] = dx.astype(dx_ref.dtype)
        _accumulate(dg_ref, _sublane_sum(dgc), pl.program_id(0))

    return _call(
        body, name=name, grid=(s // tm,), after=after,
        in_specs=[_row_spec(tm, n), _fixed_spec(width, n), *pieces, _fixed_spec(1, width)],
        out_specs=[_row_spec(tm, width), _fixed_spec(SUBLANE, width)],
        out_shape=[jax.ShapeDtypeStruct((s, width), BF16), jax.ShapeDtypeStruct((SUBLANE, width), F32)],
        compiler_params=_params("arbitrary"),
    )(dy, w, *[x] * n_p, g)


def _mid_fwd(x, y, g_post, g_pre, after=()):
    s, w = x.shape
    rows = min(ROWS, s)

    def body(x_ref, y_ref, gp_ref, gq_ref, x2_ref, h2_ref):
        yv = y_ref[...]
        x2 = x_ref[...] + yv * _rstd(yv) * gp_ref[...]
        x2_ref[...] = x2
        h2_ref[...] = (x2 * _rstd(x2) * gq_ref[...]).astype(h2_ref.dtype)

    return _call(
        body, name="mid_fwd", grid=(s // rows,), after=after,
        in_specs=[_row_spec(rows, w), _row_spec(rows, w), _fixed_spec(1, w), _fixed_spec(1, w)],
        out_specs=[_row_spec(rows, w), _row_spec(rows, w)],
        out_shape=[jax.ShapeDtypeStruct((s, w), F32), jax.ShapeDtypeStruct((s, w), BF16)],
        compiler_params=_params("parallel"),
    )(x, y, g_post, g_pre)


def _head(m, x2, tgt, g):
    s, w = m.shape
    rows = min(ROWS, s)

    def body(m_ref, x2_ref, t_ref, g_ref, dout_ref, dm_ref, dg_ref, loss_ref):
        mv = m_ref[...]
        gv = g_ref[...]
        out = x2_ref[...] + mv * _rstd(mv) * gv
        err = out - t_ref[...]
        dout = err * (1.0 / w)
        dout_ref[...] = dout
        dm, dgc = _rms_bwd(mv, gv, dout)
        dm_ref[...] = dm.astype(dm_ref.dtype)
        sq = err * err
        lanes = sq[:, 0:LANE]
        for j in range(1, w // LANE):
            lanes = lanes + sq[:, j * LANE:(j + 1) * LANE]
        step = pl.program_id(0)
        _accumulate(dg_ref, _sublane_sum(dgc), step)
        _accumulate(loss_ref, _sublane_sum(lanes) * (0.5 / w), step)

    return pl.pallas_call(
        body, name="head", grid=(s // rows,),
        in_specs=[_row_spec(rows, w), _row_spec(rows, w), _row_spec(rows, w), _fixed_spec(1, w)],
        out_specs=[_row_spec(rows, w), _row_spec(rows, w), _fixed_spec(SUBLANE, w), _fixed_spec(SUBLANE, LANE)],
        out_shape=[jax.ShapeDtypeStruct((s, w), F32), jax.ShapeDtypeStruct((s, w), BF16),
                   jax.ShapeDtypeStruct((SUBLANE, w), F32), jax.ShapeDtypeStruct((SUBLANE, LANE), F32)],
        compiler_params=_params("arbitrary"),
    )(m, x2, tgt, g)


def _mid_bwd(x2, y, d_out, d_h2, g_pre, g_post, after=()):
    s, w = x2.shape
    rows = min(ROWS, s)

    def body(x2_ref, y_ref, dout_ref, dh2_ref, gq_ref, gp_ref, dx2_ref, dy_ref, dgq_ref, dgp_ref):
        dx, dgq = _rms_bwd(x2_ref[...], gq_ref[...], dh2_ref[...])
        dx2 = dout_ref[...] + dx
        dx2_ref[...] = dx2
        dy, dgp = _rms_bwd(y_ref[...], gp_ref[...], dx2)
        dy_ref[...] = dy.astype(dy_ref.dtype)
        step = pl.program_id(0)
        _accumulate(dgq_ref, _sublane_sum(dgq), step)
        _accumulate(dgp_ref, _sublane_sum(dgp), step)

    return _call(
        body, name="mid_bwd", grid=(s // rows,), after=after,
        in_specs=[_row_spec(rows, w)] * 4 + [_fixed_spec(1, w)] * 2,
        out_specs=[_row_spec(rows, w), _row_spec(rows, w), _fixed_spec(SUBLANE, w), _fixed_spec(SUBLANE, w)],
        out_shape=[jax.ShapeDtypeStruct((s, w), F32), jax.ShapeDtypeStruct((s, w), BF16),
                   jax.ShapeDtypeStruct((SUBLANE, w), F32), jax.ShapeDtypeStruct((SUBLANE, w), F32)],
        compiler_params=_params("arbitrary"),
    )(x2, y, d_out, d_h2, g_pre, g_post)


def _first_bwd(x, g, d_h1, d_x2, after=()):
    s, w = x.shape
    rows = min(ROWS, s)

    def body(x_ref, g_ref, dh_ref, dx2_ref, dx_ref, dg_ref):
        dx, dgc = _rms_bwd(x_ref[...], g_ref[...], dh_ref[...])
        dx_ref[...] = dx2_ref[...] + dx
        _accumulate(dg_ref, _sublane_sum(dgc), pl.program_id(0))

    return _call(
        body, name="first_bwd", grid=(s // rows,), after=after,
        in_specs=[_row_spec(rows, w), _fixed_spec(1, w), _row_spec(rows, w), _row_spec(rows, w)],
        out_specs=[_row_spec(rows, w), _fixed_spec(SUBLANE, w)],
        out_shape=[jax.ShapeDtypeStruct((s, w), F32), jax.ShapeDtypeStruct((SUBLANE, w), F32)],
        compiler_params=_params("arbitrary"),
    )(x, g, d_h1, d_x2)


def _shift_down(v, k):
    t = lax.broadcasted_iota(jnp.int32, v.shape, 0)
    return jnp.where(t >= k, pltpu.roll(v, k, 0), 0.0)


def _shift_up(v, k):
    n = v.shape[0]
    t = lax.broadcasted_iota(jnp.int32, v.shape, 0)
    return jnp.where(t < n - k, pltpu.roll(v, n - k, 0), 0.0)


def _conv_core(u, b, c, w):
    z = c * u
    conv = w[0:1, :] * _shift_down(z, 2) + w[1:2, :] * _shift_down(z, 1) + w[2:3, :] * z
    return z, conv, b * conv


def _conv_fwd(proj, conv_w, g, n_groups, out_width, after=()):
    s = proj.shape[0]

    def body(u_ref, b_ref, c_ref, w_ref, g_ref, o_ref):
        _, _, yr = _conv_core(u_ref[...], b_ref[...], c_ref[...], w_ref[...])
        o_ref[...] = (yr * _rstd(yr) * g_ref[...]).astype(o_ref.dtype)

    col = lambda k: pl.BlockSpec((s, HEAD), lambda i: (0, k * n_groups + i))
    return _call(
        body, name="conv_fwd", grid=(n_groups,), after=after,
        in_specs=[col(0), col(1), col(2), pl.BlockSpec((3, HEAD), lambda i: (0, i)), pl.BlockSpec((1, HEAD), lambda i: (0, i))],
        out_specs=pl.BlockSpec((s, HEAD), lambda i: (0, i)),
        out_shape=jax.ShapeDtypeStruct((s, out_width), BF16),
        compiler_params=_params("parallel"),
    )(proj, proj, proj, conv_w, g)


def _conv_bwd(proj, d_mix, conv_w, g, n_groups):
    s = proj.shape[0]
    width = n_groups * HEAD

    def body(u_ref, b_ref, c_ref, dy_ref, w_ref, g_ref, du_ref, db_ref, dc_ref, dg_ref, dw_ref):
        u, b, c, w = u_ref[...], b_ref[...], c_ref[...], w_ref[...]
        z, conv, yr = _conv_core(u, b, c, w)
        dyr, dgc = _rms_bwd(yr, g_ref[...], dy_ref[...])
        dconv = dyr * b
        db_ref[...] = (dyr * conv).astype(db_ref.dtype)
        dz = w[2:3, :] * dconv + w[1:2, :] * _shift_up(dconv, 1) + w[0:1, :] * _shift_up(dconv, 2)
        dc_ref[...] = (dz * u).astype(dc_ref.dtype)
        du_ref[...] = (dz * c).astype(du_ref.dtype)
        dg_ref[...] = _sublane_sum(dgc)
        dw_ref[0] = _sublane_sum(dconv * _shift_down(z, 2))
        dw_ref[1] = _sublane_sum(dconv * _shift_down(z, 1))
        dw_ref[2] = _sublane_sum(dconv * z)

    col = lambda k: pl.BlockSpec((s, HEAD), lambda i: (0, k * n_groups + i))
    grp = pl.BlockSpec((s, HEAD), lambda i: (0, i))
    return pl.pallas_call(
        body, name="conv_bwd", grid=(n_groups,),
        in_specs=[col(0), col(1), col(2), grp, pl.BlockSpec((3, HEAD), lambda i: (0, i)), pl.BlockSpec((1, HEAD), lambda i: (0, i))],
        out_specs=[grp, grp, grp, pl.BlockSpec((SUBLANE, HEAD), lambda i: (0, i)),
                   pl.BlockSpec((3, SUBLANE, HEAD), lambda i: (0, 0, i))],
        out_shape=[jax.ShapeDtypeStruct((s, width), BF16)] * 3
        + [jax.ShapeDtypeStruct((SUBLANE, width), F32), jax.ShapeDtypeStruct((3, SUBLANE, width), F32)],
        compiler_params=_params("parallel"),
    )(proj, proj, proj, d_mix, conv_w, g)


def _rope_tables(s, n_heads):
    pos = jnp.arange(s, dtype=F32)
    inv_freq = jnp.power(ROPE_THETA, -jnp.arange(0, ROPE, 2, dtype=F32) / ROPE)
    ang = pos[:, None] * inv_freq[None, :]
    cos, sin = jnp.cos(ang), jnp.sin(ang)
    cs = jnp.concatenate([cos, cos], axis=1)
    sn = jnp.concatenate([-sin, sin], axis=1)
    pad = jnp.zeros((s, LANE - ROPE), F32)
    return (jnp.tile(cs, (1, n_heads)), jnp.tile(sn, (1, n_heads)),
            jnp.concatenate([cs, pad], axis=1), jnp.concatenate([sn, pad], axis=1))


def _swap_halves(v):
    w = v.shape[1]
    lane = lax.broadcasted_iota(jnp.int32, v.shape, 1)
    first = (lane % ROPE) < (ROPE // 2)
    return jnp.where(first, pltpu.roll(v, w - ROPE // 2, 1), pltpu.roll(v, ROPE // 2, 1))


def _pack_heads(q, kv, proj, kr_col, tables, n_heads, after=()):
    s = q.shape[0]
    rows = min(ROWS, s)
    cq, sq, ck, sk = tables
    wq = n_heads * ROPE

    def body(q_ref, kv_ref, kr_ref, cq_ref, sq_ref, ck_ref, sk_ref, qo_ref, ko_ref, vo_ref):
        qr = q_ref[:, n_heads * HEAD:]
        qr = qr * cq_ref[...] + _swap_halves(qr) * sq_ref[...]
        krv = kr_ref[...]
        krv = krv * ck_ref[...] + _swap_halves(krv) * sk_ref[...]
        for h in range(n_heads):
            qo_ref[h] = jnp.concatenate([q_ref[:, h * HEAD:(h + 1) * HEAD], qr[:, h * ROPE:(h + 1) * ROPE]], axis=1).astype(BF16)
            ko_ref[h] = jnp.concatenate([kv_ref[:, 2 * h * HEAD:(2 * h + 1) * HEAD], krv[:, :ROPE]], axis=1).astype(BF16)
            vo_ref[h] = kv_ref[:, (2 * h + 1) * HEAD:(2 * h + 2) * HEAD].astype(BF16)

    hs = lambda w: pl.BlockSpec((n_heads, rows, w), lambda i: (0, i, 0))
    return _call(
        body, name="pack_heads", grid=(s // rows,), after=after,
        in_specs=[_row_spec(rows, q.shape[1]), _row_spec(rows, kv.shape[1]), pl.BlockSpec((rows, LANE), lambda i: (i, kr_col // LANE)),
                  _row_spec(rows, wq), _row_spec(rows, wq), _row_spec(rows, LANE), _row_spec(rows, LANE)],
        out_specs=[hs(QK), hs(QK), hs(HEAD)],
        out_shape=[jax.ShapeDtypeStruct((n_heads, s, QK), BF16), jax.ShapeDtypeStruct((n_heads, s, QK), BF16),
                   jax.ShapeDtypeStruct((n_heads, s, HEAD), BF16)],
        compiler_params=_params("parallel"),
    )(q, kv, proj, cq, sq, ck, sk)


def _unpack_heads(dq, dk, dv, tables, n_heads):
    s = dq.shape[1]
    rows = min(ROWS, s)
    cq, sq, ck, sk = tables
    wq = n_heads * ROPE

    def body(dq_ref, dk_ref, dv_ref, cq_ref, sq_ref, ck_ref, sk_ref, qo_ref, kvo_ref, kro_ref):
        dqr = jnp.concatenate([dq_ref[h][:, HEAD:] for h in range(n_heads)], axis=1)
        dqr = dqr * cq_ref[...] - _swap_halves(dqr) * sq_ref[...]
        dkr = dk_ref[0][:, HEAD:]
        for h in range(1, n_heads):
            dkr = dkr + dk_ref[h][:, HEAD:]
        dkr = jnp.concatenate([dkr, jnp.zeros((rows, LANE - ROPE), F32)], axis=1)
        dkr = dkr * ck_ref[...] - _swap_halves(dkr) * sk_ref[...]
        kro_ref[...] = dkr.astype(kro_ref.dtype)
        qo_ref[:, n_heads * HEAD:] = dqr.astype(qo_ref.dtype)
        for h in range(n_heads):
            qo_ref[:, h * HEAD:(h + 1) * HEAD] = dq_ref[h][:, :HEAD].astype(qo_ref.dtype)
            kvo_ref[:, 2 * h * HEAD:(2 * h + 1) * HEAD] = dk_ref[h][:, :HEAD].astype(kvo_ref.dtype)
            kvo_ref[:, (2 * h + 1) * HEAD:(2 * h + 2) * HEAD] = dv_ref[h].astype(kvo_ref.dtype)

    hs = lambda w: pl.BlockSpec((n_heads, rows, w), lambda i: (0, i, 0))
    return pl.pallas_call(
        body, name="unpack_heads", grid=(s // rows,),
        in_specs=[hs(QK), hs(QK), hs(HEAD), _row_spec(rows, wq), _row_spec(rows, wq), _row_spec(rows, LANE), _row_spec(rows, LANE)],
        out_specs=[_row_spec(rows, n_heads * QK), _row_spec(rows, 2 * n_heads * HEAD), _row_spec(rows, LANE)],
        out_shape=[jax.ShapeDtypeStruct((s, n_heads * QK), BF16), jax.ShapeDtypeStruct((s, 2 * n_heads * HEAD), BF16),
                   jax.ShapeDtypeStruct((s, LANE), BF16)],
        compiler_params=_params("parallel"),
    )(dq, dk, dv, cq, sq, ck, sk)


TQ = 256


LOG2_E = 1.4426950408889634


def _softmax_parts(q, k):
    tq, n_keys = q.shape[0], k.shape[0]
    sc = lax.dot_general(q, k, (NT, ((), ())), preferred_element_type=F32) * (QK ** -0.5 * LOG2_E)
    row = lax.broadcasted_iota(jnp.int32, (tq, tq), 0)
    col = lax.broadcasted_iota(jnp.int32, (tq, tq), 1)
    own = jnp.where(col // CHUNK <= row // CHUNK, sc[:, n_keys - tq:], NEG_INF)
    sc = own if n_keys == tq else jnp.concatenate([sc[:, :n_keys - tq], own], axis=1)
    e = jnp.exp2(sc - jnp.max(sc, axis=-1, keepdims=True))
    return e, 1.0 / jnp.sum(e, axis=-1, keepdims=True)


def _prob_columns(c, tq):
    return pl.ds(tq * (c * (c + 1) // 2), (c + 1) * tq)


def _attn_fwd(q, k, v, g, mix, col0, after=()):
    n_heads, s, _ = q.shape
    tq = min(TQ, s)
    assert tq % CHUNK == 0 and s % tq == 0
    n_blocks = s // tq
    p_cols = tq * (n_blocks * (n_blocks + 1) // 2)

    def body(q_ref, k_ref, v_ref, g_ref, mix_ref, o_ref, p_ref, y_ref):
        for c in range(n_blocks):
            rows, n_keys = pl.ds(c * tq, tq), (c + 1) * tq
            e, inv = _softmax_parts(q_ref[rows, :], k_ref[0:n_keys, :])
            p = (e * inv).astype(BF16)
            p_ref[:, _prob_columns(c, tq)] = p
            o = jnp.dot(p, v_ref[0:n_keys, :], preferred_element_type=F32)
            o_ref[rows, :] = o
            y_ref[rows, :] = (o * _rstd(o) * g_ref[...]).astype(y_ref.dtype)

    head = lambda w: pl.BlockSpec((None, s, w), lambda h: (h, 0, 0))
    return _call(
        body, name="attn_fwd", grid=(n_heads,), after=after,
        in_specs=[head(QK), head(QK), head(HEAD), pl.BlockSpec((1, HEAD), lambda h: (0, h)), ANY],
        out_specs=[head(HEAD), pl.BlockSpec((None, tq, p_cols), lambda h: (h, 0, 0)),
                   pl.BlockSpec((s, HEAD), lambda h: (0, col0 // HEAD + h))],
        out_shape=[jax.ShapeDtypeStruct((n_heads, s, HEAD), F32), jax.ShapeDtypeStruct((n_heads, tq, p_cols), BF16),
                   jax.ShapeDtypeStruct(mix.shape, mix.dtype)],
        input_output_aliases={4: 2},
        compiler_params=_params("parallel"),
    )(q, k, v, g, mix)


def _attn_bwd(q, k, v, o, probs, d_mix, g, col0, after=()):
    n_heads, s, _ = q.shape
    tq = probs.shape[1]

    def body(q_ref, k_ref, v_ref, o_ref, p_ref, dy_ref, g_ref, dq_ref, dk_ref, dv_ref, dg_ref):
        dg = None
        for c in reversed(range(s // tq)):
            rows, n_keys = pl.ds(c * tq, tq), (c + 1) * tq
            o = o_ref[rows, :]
            do, dgc = _rms_bwd(o, g_ref[...], dy_ref[rows, :])
            do = do.astype(BF16)
            dg = _sublane_sum(dgc) if dg is None else dg + _sublane_sum(dgc)
            p = p_ref[:, _prob_columns(c, tq)]
            dp = lax.dot_general(do, v_ref[0:n_keys, :], (NT, ((), ())), preferred_element_type=F32)
            ds = (p.astype(F32) * (dp - jnp.sum(do.astype(F32) * o, axis=-1, keepdims=True))).astype(BF16)
            dq_ref[rows, :] = jnp.dot(ds, k_ref[0:n_keys, :], preferred_element_type=F32) * (QK ** -0.5)
            dk = lax.dot_general(ds, q_ref[rows, :], (TN, ((), ())), preferred_element_type=F32)
            dv = lax.dot_general(p, do, (TN, ((), ())), preferred_element_type=F32)
            if n_keys == s:
                dk_ref[...] = dk
                dv_ref[...] = dv
            else:
                dk_ref[0:n_keys, :] += dk
                dv_ref[0:n_keys, :] += dv
        dk_ref[...] = dk_ref[...] * (QK ** -0.5)
        dg_ref[...] = dg

    c0 = col0 // HEAD
    head = lambda w: pl.BlockSpec((None, s, w), lambda h, *_: (h, 0, 0))
    in_specs = [head(QK), head(QK), head(HEAD), head(HEAD), pl.BlockSpec((None, tq, probs.shape[2]), lambda h, *_: (h, 0, 0)),
                pl.BlockSpec((s, HEAD), lambda h, *_: (0, c0 + h)), pl.BlockSpec((1, HEAD), lambda h, *_: (0, h))]
    out_specs = [head(QK), head(QK), head(HEAD), pl.BlockSpec((SUBLANE, HEAD), lambda h, *_: (0, h))]
    out_shape = [jax.ShapeDtypeStruct((n_heads, s, QK), F32), jax.ShapeDtypeStruct((n_heads, s, QK), F32),
                 jax.ShapeDtypeStruct((n_heads, s, HEAD), F32), jax.ShapeDtypeStruct((SUBLANE, n_heads * HEAD), F32)]
    return _call(body, name="attn_bwd", grid=(n_heads,), after=after, in_specs=in_specs, out_specs=out_specs,
                 out_shape=out_shape, compiler_params=_params("parallel"))(q, k, v, o, probs, d_mix, g)


TILE_M = 1024
TILE_N = 1024


def _up_fwd(h2, w_up, between):
    s, d = h2.shape
    nb, _, fb = w_up.shape
    tm = min(TILE_M, s)
    done, after = (), ()
    for tile in range(s // tm):

        def body(h_ref, w_ref, *rest):
            a_ref, r_ref = rest[-2:]
            r = jnp.maximum(jnp.dot(h_ref[...], w_ref[...], preferred_element_type=F32), 0.0)
            a_ref[...] = (r * r).astype(a_ref.dtype)
            r_ref[...] = r.astype(r_ref.dtype)

        blk = pl.BlockSpec((tm, fb), lambda j, tile=tile: (tile, j))
        done = _call(
            body, name=f"up_fwd_{tile}", grid=(nb,), after=after,
            in_specs=[pl.BlockSpec((tm, d), lambda j, tile=tile: (tile, 0)), pl.BlockSpec((None, d, fb), lambda j: (j, 0, 0))]
                     + [ANY] * len(done),
            out_specs=[blk, blk], out_shape=[jax.ShapeDtypeStruct((s, nb * fb), BF16)] * 2,
            input_output_aliases={2 + i: i for i in range(len(done))},
            compiler_params=_params("parallel"),
        )(h2, w_up, *done)
        after = between(done) if tile == 0 else ()
    return done


def _down_fwd(a, w_down):
    s, f = a.shape
    d = w_down.shape[1]
    tm, tn, tk = min(TILE_M,s), min(TILE_N,d), 2048
    nk = f // tk
    return _matmul("down_fwd", a, w_down, grid=(s // tm, d // tn, nk),
                   a_spec=pl.BlockSpec((tm, tk), lambda i, j, k: (i, k)),
                   b_spec=pl.BlockSpec((tk, tn), lambda i, j, k: (k, j)),
                   out_shape=jax.ShapeDtypeStruct((s, d), F32),
                   out_specs=pl.BlockSpec((tm, tn), lambda i, j, k: (i, j)),
                   contract=NN, nk=nk, acc_shape=(tm, tn))


def _down_bwd_act(d_m, w_down, r, after=()):
    s, d = d_m.shape
    f = w_down.shape[0]
    tm, tn = min(TILE_M,s), min(TILE_N,f)
    blk = pl.BlockSpec((tm, tn), lambda i, j: (i, j))
    return _matmul("down_bwd_act", d_m, w_down, grid=(s // tm, f // tn), after=after,
                   a_spec=pl.BlockSpec((tm, d), lambda i, j: (i, 0)),
                   b_spec=pl.BlockSpec((tn, d), lambda i, j: (j, 0)),
                   out_shape=jax.ShapeDtypeStruct((s, f), BF16), out_specs=blk, contract=NT,
                   extras=(r,), extra_specs=(blk,),
                   epilogue=lambda acc, rv: (acc * (2.0 * rv.astype(F32)),))


def _up_bwd_act(d_up, w_up, after=()):
    s, _ = d_up.shape
    nb, d, fb = w_up.shape
    tm, tn = min(TILE_M, s), min(TILE_N,d)
    pair = 2
    n_after = len(after)

    def body(a_ref, w_ref, *rest):
        o_ref, acc_ref = rest[n_after:]
        k = pl.program_id(2)
        p = None
        for t in range(pair):
            term = lax.dot_general(a_ref[:, t * fb:(t + 1) * fb], w_ref[t], (NT, ((), ())), preferred_element_type=F32)
            p = term if p is None else p + term
        _accumulate(acc_ref, p, k)

        @pl.when(k == nb // pair - 1)
        def _():
            o_ref[...] = acc_ref[...]

    return pl.pallas_call(
        body, name="up_bwd_act", grid=(s // tm, d // tn, nb // pair),
        in_specs=[pl.BlockSpec((tm, pair * fb), lambda i, j, k: (i, k)),
                  pl.BlockSpec((pair, tn, fb), lambda i, j, k: (k, j, 0))] + [ANY] * n_after,
        out_specs=pl.BlockSpec((tm, tn), lambda i, j, k: (i, j)),
        out_shape=jax.ShapeDtypeStruct((s, d), F32),
        scratch_shapes=[pltpu.VMEM((tm, tn), F32)],
        compiler_params=_params("parallel", "parallel", "arbitrary"),
    )(d_up, w_up, *after)


def _half_grad(name, a, b, core, home, received, after, *, grid, a_block, a_map, b_block, b_map, o_block, o_map, out_shape):
    n_after = len(after)
    pick = (lambda ref: ref[0]) if home else (lambda ref: 1 - ref[0])

    def body(core_ref, a_ref, b_ref, *rest):
        acc = lax.dot_general(a_ref[...], b_ref[...], (TN, ((), ())), preferred_element_type=F32)
        if received is not None:
            acc = acc + rest[0][...].astype(F32)
        rest[-1][...] = acc.astype(rest[-1].dtype)

    wrap = lambda fn: (lambda i, j, core_ref: fn(i, j, pick(core_ref)))
    o_spec = pl.BlockSpec(o_block, wrap(o_map))
    extra = [] if received is None else [o_spec]
    operands = [] if received is None else [received]
    return pl.pallas_call(
        body, name=name,
        grid_spec=pltpu.PrefetchScalarGridSpec(
            num_scalar_prefetch=1, grid=grid,
            in_specs=[pl.BlockSpec(a_block, wrap(a_map)), pl.BlockSpec(b_block, wrap(b_map))] + extra + [ANY] * n_after,
            out_specs=o_spec),
        out_shape=out_shape,
        compiler_params=_params("parallel", "parallel"),
    )(core, a, b, *operands, *after)


def _down_half_grad(name, a, d_m, core, home, received=None, after=()):
    s, f = a.shape
    d = d_m.shape[1]
    r = f // N_DEV
    tn = min(TILE_N, d)
    return _half_grad(name, a, d_m, core, home, received, after, grid=(N_CHIP, d // tn),
                      a_block=(s, r), a_map=lambda k, j, p: (0, 2 * k + p),
                      b_block=(s, tn), b_map=lambda k, j, p: (0, j),
                      o_block=(None, r, tn), o_map=lambda k, j, p: (k, 0, j),
                      out_shape=jax.ShapeDtypeStruct((N_CHIP, r, d), BF16))


def _up_half_grad(name, h2, d_up, core, home, received=None, after=()):
    s, d = h2.shape
    fb = d_up.shape[1] // N_DEV
    tm = min(TILE_M, d)
    return _half_grad(name, h2, d_up, core, home, received, after, grid=(d // tm, N_CHIP),
                      a_block=(s, tm), a_map=lambda i, k, p: (0, i),
                      b_block=(s, fb), b_map=lambda i, k, p: (0, 2 * k + p),
                      o_block=(None, tm, fb), o_map=lambda i, k, p: (k, i, 0),
                      out_shape=jax.ShapeDtypeStruct((N_CHIP, d, fb), BF16))


MXU_WIDTH = 256


def _in_pad(in_width):
    return -(-in_width // MXU_WIDTH) * MXU_WIDTH


def _join_col_shards(name, blocks, own, device, pieces=None):
    n, r, w = blocks.shape
    rows = min(ROWS, r)
    pieces = pieces or [(j, 0, w) for j in range(n)]
    used = sum(b - a for _, a, b in pieces)
    width = _in_pad(used)

    def body(dev_ref, x_ref, own_ref, o_ref):
        block = lambda j: jnp.where(dev_ref[0] == j, own_ref[...], x_ref[j])
        cols = [block(j)[:, a:b] for j, a, b in pieces]
        tail = [jnp.zeros((rows, width - used), o_ref.dtype)] if width > used else []
        o_ref[...] = jnp.concatenate(cols + tail, axis=1)

    return pl.pallas_call(
        body, name=name,
        grid_spec=pltpu.PrefetchScalarGridSpec(
            num_scalar_prefetch=1, grid=(r // rows,),
            in_specs=[pl.BlockSpec((n, rows, w), lambda i, dev: (0, i, 0)), pl.BlockSpec((rows, w), lambda i, dev: (i, 0))],
            out_specs=pl.BlockSpec((rows, width), lambda i, dev: (i, 0))),
        out_shape=jax.ShapeDtypeStruct((r, width), blocks.dtype),
        compiler_params=_params("parallel"),
    )(device, blocks, own)


def _unpermute_q_rows(wt, n_heads):
    r = wt.shape[1]
    nope = wt[:n_heads * HEAD].reshape(n_heads, HEAD, r)
    rope = wt[n_heads * HEAD:].reshape(n_heads, ROPE, r)
    return jnp.concatenate([nope, rope], axis=1).reshape(n_heads * QK, r)


def _local_step(x, tgt, gains, weights, grads, first_after=()):
    pre_mix_g, q_norm_g, kv_norm_g, conv_out_g, attn_out_g, post_mix_g, pre_mlp_g, post_mlp_g = gains
    s, d = x.shape
    conv_width = conv_out_g.shape[1]
    n_groups = conv_width // HEAD
    r_q, r_kv = q_norm_g.shape[1], kv_norm_g.shape[1]
    n_heads = attn_out_g.shape[1] // HEAD
    c_q0 = 3 * conv_width
    c_kv0 = c_q0 + r_q
    c_kr0 = c_kv0 + r_kv
    in_pad = _in_pad(c_kr0 + ROPE)
    tn_in = _fit(in_pad, 6 * MXU_WIDTH)
    tables = _rope_tables(s, n_heads)

    h1 = _rms_fwd("pre_mix_norm", x, pre_mix_g, after=first_after)
    weights.forward(0, (h1,))
    weights.relay(0, tables)
    w_in_p, conv_w = weights.ready(0, ())
    proj = _mm_nn("in_proj", h1, w_in_p, F32, TILE_M, tn_in)
    y_conv = _conv_fwd(proj, conv_w, conv_out_g, n_groups, conv_width + n_heads * HEAD, after=weights.forward(1, (proj,)))
    w_uq_p, w_ukv, w_o = weights.ready(1, (y_conv,))
    qn, q = _norm_up("q_up", proj, (c_q0, r_q), q_norm_g, w_uq_p)
    kvn, kv = _norm_up("kv_up", proj, (c_kv0, r_kv), kv_norm_g, w_ukv)
    qh, kh, vh = _pack_heads(q, kv, proj, c_kr0, tables, n_heads)
    o, probs, mix = _attn_fwd(qh, kh, vh, attn_out_g, y_conv, conv_width, after=weights.forward(2, (qh, kh, vh)))
    y = _mm_nn("out_proj", mix, w_o, F32, TILE_M, TILE_N, after=weights.start(3, (mix,)))
    x2, h2 = _mid_fwd(x, y, post_mix_g, pre_mlp_g)
    weights.relay(2, (h2,))
    (w_up,) = weights.ready(2, ())
    a, r = _up_fwd(h2, w_up, lambda done: weights.forward(3, tuple(done)))
    weights.relay(3, (a,))
    (w_down,) = weights.ready(3, ())
    m = _down_fwd(a, w_down)

    d_out, d_m, dg_post_mlp, loss_part = _head(m, x2, tgt, post_mlp_g)
    core = grads.core
    away = _down_half_grad("down_bwd_w_away", a, d_m, core, home=False)
    d_up = _down_bwd_act(d_m, w_down, r, after=grads.send_away(0, away))
    sums = _down_half_grad("down_bwd_w_home", a, d_m, core, home=True, received=grads.received(0, (d_up,)))
    away = _up_half_grad("up_bwd_w_away", h2, d_up, core, home=False, after=grads.send_sums(0, (sums,)))
    d_h2 = _up_bwd_act(d_up, w_up, after=grads.send_away(1, away))
    sums = _up_half_grad("up_bwd_w_home", h2, d_up, core, home=True, received=grads.received(1, (d_h2,)))
    d_x2, d_y, dg_pre_mlp, dg_post_mix = _mid_bwd(x2, y, d_out, d_h2, pre_mlp_g, post_mix_g, after=grads.send_sums(1, (sums,)))
    d_mix = _mm_nt("out_proj_bwd_act", d_y, w_o, F32, TILE_M, TILE_N)
    gw_o = _mm_tn("out_proj_bwd_w", mix, d_y, BF16, TILE_M, TILE_N)
    dqh, dkh, dvh, dg_attn = _attn_bwd(qh, kh, vh, o, probs, d_mix, attn_out_g, conv_width)
    d_q, d_kv, d_kr = _unpack_heads(dqh, dkh, dvh, tables, n_heads)
    gw_uq_t = _mm_tn("q_up_bwd_w", d_q, qn, F32, TILE_M, TILE_N)
    gw_ukv = _mm_tn("kv_up_bwd_w", kvn, d_kv, BF16, TILE_M, TILE_N)
    d_cq, dg_q = _up_norm_bwd("q_up_bwd_act", d_q, w_uq_p, proj, (c_q0, r_q), q_norm_g, after=grads.full(2, (gw_o, gw_uq_t, gw_ukv)))
    d_ckv, dg_kv = _up_norm_bwd("kv_up_bwd_act", d_kv, w_ukv, proj, (c_kv0, r_kv), kv_norm_g)
    d_u, d_b, d_c, dg_conv, dw_conv = _conv_bwd(proj, d_mix, conv_w, conv_out_g, n_groups)
    d_proj = jnp.concatenate([d_u, d_b, d_c, d_cq, d_ckv, d_kr, jnp.zeros((s, in_pad - c_kr0 - LANE), BF16)], axis=1)
    gw_in_t = _mm_tn("in_proj_bwd_w", d_proj, h1, F32, tn_in, TILE_N)
    updated = grads.update_now(0, grads.send_away(3, gw_in_t))
    d_h1 = _mm_nt("in_proj_bwd_act", d_proj, w_in_p, F32, TILE_M, TILE_N, after=grads.full(3, (gw_in_t,), received=updated))
    grad_x, dg_pre_mix = _first_bwd(x, pre_mix_g, d_h1, d_x2)

    small = [dg_pre_mix, dg_q, dg_kv, dg_conv, dg_attn, dg_post_mix, dg_pre_mlp, dg_post_mlp,
             dw_conv[0], dw_conv[1], dw_conv[2], loss_part]
    return grad_x, jnp.concatenate(small, axis=1)


HBM = pl.BlockSpec(memory_space=pltpu.HBM)
SEM = pl.BlockSpec(memory_space=pltpu.SEMAPHORE)
IN_VMEM = pl.BlockSpec(memory_space=pltpu.VMEM)
SPLIT = pltpu.CompilerParams(has_side_effects=pltpu.SideEffectType.DATAFLOW_SIDE_EFFECTING)


def _in_hbm(a):
    return pltpu.with_memory_space_constraint(a, pltpu.HBM)


def _hbm_like(a):
    return pltpu.HBM(a.shape, a.dtype)


def _place():
    x, y, c = lax.axis_index("x"), lax.axis_index("y"), lax.axis_index("c")
    other_chips = [(1 - x, y), (x, 1 - y), (1 - x, 1 - y)]
    return x, y, c, other_chips


def _block(px, py, pc):
    return 4 * px + 2 * py + pc


def _await(block, sem):
    pltpu.make_async_copy(block, block, sem).wait()


def _relay_route(x, y, c):
    came_from = ((1 - x) * (1 - c) + x * c, y * (1 - c) + (1 - y) * c)
    goes_to = (x * (1 - c) + (1 - x) * c, (1 - y) * (1 - c) + y * c)
    return came_from, goes_to


def _gather_start(name, shards, groups, relayed=(), after=()):
    n, ng = len(shards), len(groups)
    lands = [lax.empty((N_DEV, *a.shape), a.dtype) for a in shards]

    def body(*refs):
        src, land = refs[:n], refs[n:2 * n]
        sems, token = refs[2 * n + len(after):2 * n + len(after) + 2 * ng], refs[-1]
        x, y, c, chips = _place()
        targets = [(x, y, 1 - c)] + [(*chip, c) for chip in chips]
        for gi, group in enumerate(groups):
            for i, w in enumerate(group):
                for k, to in enumerate(targets[:3] if gi in relayed else targets):
                    pltpu.make_async_remote_copy(
                        src_ref=src[w], dst_ref=land[w].at[_block(x, y, c)],
                        send_sem=sems[2 * gi].at[4 * i + k], recv_sem=sems[2 * gi + 1].at[4 * i + k],
                        device_id=to, device_id_type=MESH).start()
        token[...] = jnp.zeros_like(token)

    sem_shapes = [pltpu.SemaphoreType.DMA((4 * len(g),)) for g in groups for _ in range(2)]
    out = pl.pallas_call(
        body, name=name,
        in_specs=[HBM] * (2 * n) + [ANY] * len(after),
        out_specs=[SEM] * (2 * ng) + [HBM] * (2 * n) + [IN_VMEM],
        out_shape=sem_shapes + [_hbm_like(a) for a in shards] + [_hbm_like(a) for a in lands]
        + [jax.ShapeDtypeStruct((SUBLANE, LANE), F32)],
        input_output_aliases={i: 2 * ng + i for i in range(2 * n)},
        compiler_params=SPLIT,
    )(*[_in_hbm(a) for a in shards], *[_in_hbm(a) for a in lands], *after)
    sems = [(out[2 * gi], out[2 * gi + 1]) for gi in range(ng)]
    return sems, out[2 * ng:2 * ng + n], out[2 * ng + n:2 * ng + 2 * n], out[-1]


def _gather_forward(name, shards, lands, send1, recv1, after, relayed=False):
    n = len(lands)

    def body(*refs):
        src, land = refs[:n], refs[n:2 * n]
        s1, r1 = refs[2 * n], refs[2 * n + 1]
        s2, r2 = refs[2 * n + 2 + len(after)], refs[2 * n + 3 + len(after)]
        x, y, c, chips = _place()
        me, sibling = (x, y, c), (x, y, 1 - c)
        for j, chip in enumerate(chips[:2] if relayed else chips):
            for i in range(n):
                blk = land[i].at[_block(*chip, c)]
                pltpu.make_async_remote_copy(src_ref=blk, dst_ref=blk, send_sem=s1.at[4 * i + 1 + j], recv_sem=r1.at[4 * i + 1 + j],
                                             device_id=me, device_id_type=MESH).wait_recv()
                pltpu.make_async_remote_copy(src_ref=blk, dst_ref=blk, send_sem=s2.at[3 * i + j], recv_sem=r2.at[3 * i + j],
                                             device_id=sibling, device_id_type=MESH).start()
        if relayed:
            came_from, goes_to = _relay_route(x, y, c)
            for i in range(n):
                blk = land[i].at[_block(*came_from, c)]
                pltpu.make_async_remote_copy(src_ref=blk, dst_ref=blk, send_sem=s2.at[3 * i + 2], recv_sem=r2.at[3 * i + 2],
                                             device_id=(*goes_to, c), device_id_type=MESH).start()
        for i in range(n):
            blk = land[i].at[_block(x, y, 1 - c)]
            pltpu.make_async_remote_copy(src_ref=blk, dst_ref=blk, send_sem=s1.at[4 * i], recv_sem=r1.at[4 * i],
                                         device_id=me, device_id_type=MESH).wait_recv()
            for k in range(3 if relayed else 4):
                pltpu.make_async_remote_copy(src_ref=src[i], dst_ref=land[i].at[_block(x, y, c)], send_sem=s1.at[4 * i + k],
                                             recv_sem=r1.at[4 * i + k], device_id=sibling, device_id_type=MESH).wait_send()

    sem = pltpu.SemaphoreType.DMA((3 * n,))
    out = pl.pallas_call(
        body, name=name,
        in_specs=[HBM] * (2 * n) + [SEM, SEM] + [ANY] * len(after),
        out_specs=[SEM, SEM] + [HBM] * n,
        out_shape=[sem, sem] + [_hbm_like(a) for a in lands],
        input_output_aliases={n + i: 2 + i for i in range(n)},
        compiler_params=SPLIT,
    )(*shards, *lands, send1, recv1, *after)
    return (out[0], out[1]), out[2:]


def _gather_relay_forward(name, lands, send2, recv2, after):
    n = len(lands)

    def body(*refs):
        land, s2, r2 = refs[:n], refs[n], refs[n + 1]
        s3, r3 = refs[n + 2 + len(after)], refs[n + 3 + len(after)]
        x, y, c, _ = _place()
        me, sibling = (x, y, c), (x, y, 1 - c)
        came_from, _ = _relay_route(x, y, c)
        for i in range(n):
            blk = land[i].at[_block(1 - x, 1 - y, c)]
            pltpu.make_async_remote_copy(src_ref=blk, dst_ref=blk, send_sem=s2.at[3 * i + 2], recv_sem=r2.at[3 * i + 2],
                                         device_id=me, device_id_type=MESH).wait_recv()
            pltpu.make_async_remote_copy(src_ref=blk, dst_ref=blk, send_sem=s3.at[i], recv_sem=r3.at[i],
                                         device_id=sibling, device_id_type=MESH).start()
            sent = land[i].at[_block(*came_from, c)]
            pltpu.make_async_remote_copy(src_ref=sent, dst_ref=sent, send_sem=s2.at[3 * i + 2], recv_sem=r2.at[3 * i + 2],
                                         device_id=me, device_id_type=MESH).wait_send()

    sem = pltpu.SemaphoreType.DMA((n,))
    out = pl.pallas_call(
        body, name=name,
        in_specs=[HBM] * n + [SEM, SEM] + [ANY] * len(after),
        out_specs=[SEM, SEM] + [HBM] * n,
        out_shape=[sem, sem] + [_hbm_like(a) for a in lands],
        input_output_aliases={i: 2 + i for i in range(n)},
        compiler_params=SPLIT,
    )(*lands, send2, recv2, *after)
    return (out[0], out[1]), out[2:]


def _gather_wait(name, lands, send2, recv2, after, relay_sems=None):
    n = len(lands)
    n_sems = 2 if relay_sems is None else 4

    def body(*refs):
        land, s2, r2 = refs[:n], refs[n], refs[n + 1]
        for i in range(n):
            for j in range(3 if relay_sems is None else 2):
                _await(land[i].at[0], r2.at[3 * i + j])
                _await(land[i].at[0], s2.at[3 * i + j])
            if relay_sems is not None:
                _await(land[i].at[0], refs[n + 3].at[i])
                _await(land[i].at[0], refs[n + 2].at[i])

    return pl.pallas_call(
        body, name=name,
        in_specs=[HBM] * n + [SEM] * n_sems + [ANY] * len(after), out_specs=[HBM] * n, out_shape=[_hbm_like(a) for a in lands],
        input_output_aliases={i: i for i in range(n)},
        compiler_params=SPLIT,
    )(*lands, send2, recv2, *(relay_sems or ()), *after)


def _pair_exchange(name, grads, shard_rows):
    n = len(grads)
    shapes = [(g.shape[1:] if r is None else (r, g.shape[1])) for g, r in zip(grads, shard_rows)]

    def body(*refs):
        ins, recv = refs[:n], refs[n:2 * n]
        send_sems, recv_sems = refs[2 * n:]
        x, y, c, _ = _place()
        sends = []
        for w in range(n):
            for k in range(N_CHIP):
                j, r = 2 * k + 1 - c, shard_rows[w]
                src = ins[w].at[j] if r is None else ins[w].at[pl.ds(pl.multiple_of(j * r, SUBLANE), r), :]
                sends.append(pltpu.make_async_remote_copy(
                    src_ref=src, dst_ref=recv[w].at[k],
                    send_sem=send_sems.at[w, k], recv_sem=recv_sems.at[w, k],
                    device_id=(x, y, 1 - c), device_id_type=MESH))
        for cp in sends:
            cp.start()
        for cp in sends:
            cp.wait()

    return pl.pallas_call(
        body, name=name,
        in_specs=[ANY] * n, out_specs=[ANY] * n,
        out_shape=[jax.ShapeDtypeStruct((N_CHIP, *shape), g.dtype) for g, shape in zip(grads, shapes)],
        scratch_shapes=[pltpu.SemaphoreType.DMA((n, N_CHIP))] * 2,
    )(*grads)


def _pair_sum_rows(name, grad, received, core):
    _, r, c = received.shape
    tc = _fit(c, 512)

    def body(core_ref, a_ref, b_ref, o_ref):
        o_ref[...] = (a_ref[...] + b_ref[...]).astype(o_ref.dtype)

    spec = pl.BlockSpec((None, r, tc), lambda k, i, core_ref: (k, 0, i))
    return pl.pallas_call(
        body, name=name,
        grid_spec=pltpu.PrefetchScalarGridSpec(
            num_scalar_prefetch=1, grid=(N_CHIP, c // tc),
            in_specs=[pl.BlockSpec((r, tc), lambda k, i, core_ref: (2 * k + core_ref[0], i)), spec],
            out_specs=spec),
        out_shape=jax.ShapeDtypeStruct(received.shape, BF16),
        compiler_params=_params("parallel", "parallel"),
    )(core, grad, received)


def _pair_sum(name, grad, received, core):
    _, r, c = received.shape
    rows = min(ROWS, r)
    assert r % rows == 0

    def body(core_ref, a_ref, b_ref, o_ref):
        o_ref[...] = (a_ref[...].astype(F32) + b_ref[...].astype(F32)).astype(o_ref.dtype)

    spec = pl.BlockSpec((None, rows, c), lambda k, i, core_ref: (k, i, 0))
    return pl.pallas_call(
        body, name=name,
        grid_spec=pltpu.PrefetchScalarGridSpec(
            num_scalar_prefetch=1, grid=(N_CHIP, r // rows),
            in_specs=[pl.BlockSpec((None, None, rows, c), lambda k, i, core_ref: (k, core_ref[0], i, 0)), spec],
            out_specs=spec),
        out_shape=jax.ShapeDtypeStruct(received.shape, received.dtype),
        compiler_params=_params("parallel", "parallel"),
    )(core, grad.reshape(N_CHIP, 2, r, c), received)


def _away_shard(src, k, c, shard_rows):
    if shard_rows is None:
        return src.at[k]
    return src.at[pl.ds(pl.multiple_of((2 * k + 1 - c) * shard_rows, SUBLANE), shard_rows), :]


def _pair_send_start(name, away, shard_rows=None):
    shape = away.shape if shard_rows is None else (N_CHIP, shard_rows, away.shape[1])
    land = lax.empty(shape, away.dtype)

    def body(src, dst, send, recv, src_thru, dst_thru, token):
        x, y, c, _ = _place()
        for k in range(N_CHIP):
            pltpu.make_async_remote_copy(src_ref=_away_shard(src, k, c, shard_rows), dst_ref=dst.at[k], send_sem=send.at[k],
                                         recv_sem=recv.at[k], device_id=(x, y, 1 - c), device_id_type=MESH).start()
        token[...] = jnp.zeros_like(token)

    sem = pltpu.SemaphoreType.DMA((N_CHIP,))
    out = pl.pallas_call(
        body, name=name,
        in_specs=[HBM, HBM], out_specs=[SEM, SEM, HBM, HBM, IN_VMEM],
        out_shape=[sem, sem, _hbm_like(away), _hbm_like(land), jax.ShapeDtypeStruct((SUBLANE, LANE), F32)],
        input_output_aliases={0: 2, 1: 3},
        compiler_params=SPLIT,
    )(_in_hbm(away), _in_hbm(land))
    return (out[0], out[1]), out[2], out[3], out[4]


def _pair_send_wait(name, sems, src, land, after, shard_rows=None):
    def body(src_ref, dst_ref, send, recv, *rest):
        for k in range(N_CHIP):
            _await(dst_ref.at[k], send.at[k])
            _await(dst_ref.at[k], recv.at[k])

    return pl.pallas_call(
        body, name=name,
        in_specs=[HBM, HBM, SEM, SEM] + [ANY] * len(after), out_specs=HBM, out_shape=_hbm_like(land),
        input_output_aliases={1: 0},
        compiler_params=SPLIT,
    )(src, land, *sems, *after)


def _chip_send_start(name, sums):
    n = len(sums)
    lands = [lax.empty(a.shape, a.dtype) for a in sums]

    def body(*refs):
        src, land = refs[:n], refs[n:2 * n]
        send, recv, token = refs[2 * n], refs[2 * n + 1], refs[-1]
        x, y, c, chips = _place()
        for w in range(n):
            for j, (px, py) in enumerate(chips):
                pltpu.make_async_remote_copy(
                    src_ref=src[w].at[2 * px + py], dst_ref=land[w].at[2 * x + y],
                    send_sem=send.at[3 * w + j], recv_sem=recv.at[3 * w + j],
                    device_id=(px, py, c), device_id_type=MESH).start()
        token[...] = jnp.zeros_like(token)

    sem = pltpu.SemaphoreType.DMA((3 * n,))
    out = pl.pallas_call(
        body, name=name,
        in_specs=[HBM] * (2 * n),
        out_specs=[SEM, SEM] + [HBM] * (2 * n) + [IN_VMEM],
        out_shape=[sem, sem] + [_hbm_like(a) for a in sums] + [_hbm_like(a) for a in lands]
        + [jax.ShapeDtypeStruct((SUBLANE, LANE), F32)],
        input_output_aliases={i: 2 + i for i in range(2 * n)},
        compiler_params=SPLIT,
    )(*[_in_hbm(a) for a in sums], *[_in_hbm(a) for a in lands])
    return (out[0], out[1]), out[2:2 + n], out[2 + n:2 + 2 * n], out[-1]


def _chip_send_wait(name, groups, after):
    counts = [len(g[1]) for g in groups]
    n = sum(counts)

    def body(*refs):
        land = refs[n:2 * n]
        sems = refs[2 * n:2 * n + 2 * len(groups)]
        w = 0
        for gi, count in enumerate(counts):
            for i in range(count):
                for j in range(3):
                    _await(land[w].at[0], sems[2 * gi].at[3 * i + j])
                    _await(land[w].at[0], sems[2 * gi + 1].at[3 * i + j])
                w += 1

    sums = [a for g in groups for a in g[1]]
    lands = [a for g in groups for a in g[2]]
    sems = [s for g in groups for s in g[0]]
    return pl.pallas_call(
        body, name=name,
        in_specs=[HBM] * (2 * n) + [SEM] * len(sems) + [ANY] * len(after),
        out_specs=[HBM] * n, out_shape=[_hbm_like(a) for a in lands],
        input_output_aliases={n + i: i for i in range(n)},
        compiler_params=SPLIT,
    )(*sums, *lands, *sems, *after)


def _small_all_reduce(part, after=()):
    _, w = part.shape

    def body(p_ref, *rest):
        o_ref, buf, send_sems, recv_sems = rest[len(after):]
        x, y, c, _ = _place()
        me = 4 * x + 2 * y + c
        buf[me] = jnp.sum(p_ref[...], axis=0, keepdims=True)
        copies = []
        for k in range(1, N_DEV):
            dx, dy, dc = (k >> 2) & 1, (k >> 1) & 1, k & 1
            copies.append(pltpu.make_async_remote_copy(
                src_ref=buf.at[me], dst_ref=buf.at[me], send_sem=send_sems.at[k - 1], recv_sem=recv_sems.at[k - 1],
                device_id=(x ^ dx, y ^ dy, c ^ dc), device_id_type=MESH))
        for cp in copies:
            cp.start()
        for cp in copies:
            cp.wait()
        tot = buf[0]
        for d in range(1, N_DEV):
            tot = tot + buf[d]
        o_ref[...] = tot
        loss = jnp.sum(tot[:, w - LANE:], axis=1, keepdims=True)
        o_ref[:, w - LANE:] = jnp.broadcast_to(loss, (1, LANE))

    return pl.pallas_call(
        body, name="small_all_reduce",
        in_specs=[IN_VMEM] + [ANY] * len(after), out_specs=IN_VMEM,
        out_shape=jax.ShapeDtypeStruct((1, w), F32),
        scratch_shapes=[pltpu.VMEM((N_DEV, 1, w), F32), pltpu.SemaphoreType.DMA((N_DEV - 1,)), pltpu.SemaphoreType.DMA((N_DEV - 1,))],
        compiler_params=pltpu.CompilerParams(vmem_limit_bytes=VMEM_LIMIT_BYTES),
    )(part, *after)


def _adamw(w, g, m, v):
    m = ADAM_B1 * m + (1.0 - ADAM_B1) * g
    v = ADAM_B2 * v + (1.0 - ADAM_B2) * (g * g)
    m_hat = m / (1.0 - ADAM_B1 ** ADAM_STEP)
    v_hat = v / (1.0 - ADAM_B2 ** ADAM_STEP)
    delta = -ADAM_LR * (m_hat / (jnp.sqrt(v_hat) + ADAM_EPS) + ADAM_WD * w)
    return delta, m, v


def _sum_adam_block(chip_ref, p_ref, own_ref, w_ref, m_ref, v_ref, g_ref, d_ref, mo_ref, vo_ref):
    g = None
    for k in range(N_CHIP):
        term = jnp.where(chip_ref[0] == k, own_ref[...], p_ref[k]).astype(F32)
        g = term if g is None else g + term
    g_ref[...] = g
    d_ref[...], mo_ref[...], vo_ref[...] = _adamw(w_ref[...], g, m_ref[...], v_ref[...])


def _sum_adam(name, parts, sums, chip, w, m, v, after=()):
    _, r, c = w.shape
    n_after = len(after)
    by_rows = r % ROWS == 0 or r < ROWS
    tr, tc = (min(ROWS, r), c) if by_rows else (r, _fit(c, 512))
    at = (lambda i: (i, 0)) if by_rows else (lambda i: (0, i))

    def body(chip_ref, p_ref, own_ref, w_ref, m_ref, v_ref, *rest):
        _sum_adam_block(chip_ref, p_ref, own_ref, w_ref, m_ref, v_ref, *rest[n_after:])

    blk = pl.BlockSpec((None, tr, tc), lambda i, chip_ref: (0, *at(i)))
    out = jax.ShapeDtypeStruct((1, r, c), F32)
    return pl.pallas_call(
        body, name=name,
        grid_spec=pltpu.PrefetchScalarGridSpec(
            num_scalar_prefetch=1, grid=(r // tr if by_rows else c // tc,),
            in_specs=[pl.BlockSpec((N_CHIP, tr, tc), lambda i, chip_ref: (0, *at(i))),
                      pl.BlockSpec((None, tr, tc), lambda i, chip_ref: (chip_ref[0], *at(i))), blk, blk, blk]
            + [ANY] * n_after,
            out_specs=[blk] * 4),
        out_shape=[out] * 4,
        compiler_params=_params("parallel"),
    )(chip, parts, sums, w, m, v, *after)


def _adam_gains(total, ws, ms, vs):
    n = len(ws)
    widths = [w.shape[1] for w in ws]

    def body(t_ref, *refs):
        w_refs, m_refs, v_refs, outs = refs[:n], refs[n:2 * n], refs[2 * n:3 * n], refs[3 * n:]
        off = 0
        for i in range(n):
            g = t_ref[:, off:off + widths[i]]
            off += widths[i]
            g_ref, d_ref, mo_ref, vo_ref = outs[4 * i:4 * i + 4]
            g_ref[...] = g
            d_ref[...], mo_ref[...], vo_ref[...] = _adamw(w_refs[i][...], g, m_refs[i][...], v_refs[i][...])

    out = pl.pallas_call(
        body, name="adam_gains",
        out_shape=[jax.ShapeDtypeStruct(w.shape, F32) for w in ws for _ in range(4)],
    )(total, *ws, *ms, *vs)
    return [tuple(out[4 * i:4 * i + 4]) for i in range(n)]


def _adam_taps(total, first_col, device, w, m, v):
    _, n_taps, cw = w.shape
    col_block = lambda t, dev: (0, first_col // cw + t * N_DEV + dev[0])
    tap = pl.BlockSpec((None, 1, cw), lambda t, dev: (t, 0, 0))

    def body(dev_ref, t_ref, w_ref, m_ref, v_ref, g_ref, d_ref, mo_ref, vo_ref):
        g = t_ref[...]
        g_ref[...] = g
        d_ref[...], mo_ref[...], vo_ref[...] = _adamw(w_ref[...], g, m_ref[...], v_ref[...])

    shape3 = (n_taps, 1, cw)
    out = pl.pallas_call(
        body, name="adam_taps",
        grid_spec=pltpu.PrefetchScalarGridSpec(
            num_scalar_prefetch=1, grid=(n_taps,),
            in_specs=[pl.BlockSpec((1, cw), col_block), tap, tap, tap], out_specs=[tap] * 4),
        out_shape=[jax.ShapeDtypeStruct(shape3, F32)] * 4,
    )(device, total, w.reshape(shape3), m.reshape(shape3), v.reshape(shape3))
    return tuple(o.reshape(w.shape) for o in out)


def kernel(x, pre_mix_g, w_in, conv_w, q_norm_g, w_uq, kv_norm_g, w_ukv, conv_out_g, attn_out_g, w_o, post_mix_g, pre_mlp_g, w_up, w_down, post_mlp_g, loss_target, m_pre_mix_g, m_w_in, m_conv_w, m_q_norm_g, m_w_uq, m_kv_norm_g, m_w_ukv, m_conv_out_g, m_attn_out_g, m_w_o, m_post_mix_g, m_pre_mlp_g, m_w_up, m_w_down, m_post_mlp_g, v_pre_mix_g, v_w_in, v_conv_w, v_q_norm_g, v_w_uq, v_kv_norm_g, v_w_ukv, v_conv_out_g, v_attn_out_g, v_w_o, v_post_mix_g, v_pre_mlp_g, v_w_up, v_w_down, v_post_mlp_g):
    me = 4 * lax.axis_index("x") + 2 * lax.axis_index("y") + lax.axis_index("c")
    core = lax.axis_index("c").astype(jnp.int32).reshape(1)
    chip = (2 * lax.axis_index("x") + lax.axis_index("y")).astype(jnp.int32).reshape(1)
    gains = (pre_mix_g, q_norm_g, kv_norm_g, conv_out_g, attn_out_g, post_mix_g, pre_mlp_g, post_mlp_g)
    gain_m = (m_pre_mix_g, m_q_norm_g, m_kv_norm_g, m_conv_out_g, m_attn_out_g, m_post_mix_g, m_pre_mlp_g, m_post_mlp_g)
    gain_v = (v_pre_mix_g, v_q_norm_g, v_kv_norm_g, v_conv_out_g, v_attn_out_g, v_post_mix_g, v_pre_mlp_g, v_post_mlp_g)
    names = ("w_in", "w_uq", "w_ukv", "w_o", "w_up", "w_down")
    big = dict(zip(names, (w_in, w_uq, w_ukv, w_o, w_up, w_down)))
    big_m = dict(zip(names, (m_w_in, m_w_uq, m_w_ukv, m_w_o, m_w_up, m_w_down)))
    big_v = dict(zip(names, (v_w_in, v_w_uq, v_w_ukv, v_w_o, v_w_up, v_w_down)))
    n_heads = attn_out_g.shape[1] // HEAD
    n_taps = conv_w.shape[1]

    gathered = ("w_in", "conv", "w_uq", "w_ukv", "w_o", "w_up", "w_down")
    gather_groups = ((0, 1), (2, 3, 4), (5,), (6,))
    taps = jnp.pad(conv_w[0], ((0, SUBLANE - n_taps), (0, 0)))
    relayed_groups = (0, 2, 3)
    sems1, shards, lands, token = _gather_start("gather_start_first", [w_in[0].astype(BF16), taps], ((0, 1),), relayed=(0,))
    sems1, shards, lands = list(sems1), list(shards), list(lands)
    behind = token[0, 0]
    rest = [(big[nm][0] + behind).astype(BF16) for nm in gathered[2:]]

    def start_more(name, some, groups, relayed, after):
        sems_b, shards_b, lands_b, started = _gather_start(name, some, groups, relayed=relayed, after=after)
        sems1.extend(sems_b)
        shards.extend(shards_b)
        lands.extend(lands_b)
        return started

    start_rest = lambda after: start_more("gather_start_rest", rest[:4], ((0, 1, 2), (3,)), (1,), after)
    start_last = lambda after: start_more("gather_start_last", rest[4:], ((0,),), (0,), after)

    cols = lambda a: jnp.concatenate([a[j] for j in range(N_DEV)], axis=1)
    rows = lambda a: a.reshape(N_DEV * a.shape[1], a.shape[2])
    device = me.astype(jnp.int32).reshape(1)
    own_in = lambda a, shard: lax.dynamic_update_index_in_dim(a, shard, me, 0)
    q_pieces = [(h, 0, HEAD) for h in range(n_heads)] + [(h, HEAD, QK) for h in range(n_heads)]
    ready = {
        "w_in": lambda a, shard: _join_col_shards("join_w_in", a, shard, device),
        "conv": lambda a, shard: cols(own_in(a, shard))[:n_taps],
        "w_uq": lambda a, shard: _join_col_shards("join_w_uq", a, shard, device, q_pieces),
        "w_ukv": lambda a, shard: cols(own_in(a, shard)),
        "w_o": lambda a, shard: rows(own_in(a, shard)),
        "w_up": own_in,
        "w_down": lambda a, shard: rows(own_in(a, shard)),
    }
    assert w_uq.shape[2] == QK

    class Weights:
        def __init__(self):
            self.passed, self.relayed = {}, {}

        def forward(self, group, after):
            idx = gather_groups[group]
            if group == 0:
                after = (*after, *rest)
            self.passed[group] = _gather_forward(f"gather_forward_{group}", [shards[i] for i in idx], [lands[i] for i in idx],
                                                 *sems1[group], after, relayed=group in relayed_groups)
            return tuple(self.passed[group][1])

        def start(self, group, after):
            assert group == len(gather_groups) - 1
            return (start_last(after),)

        def relay(self, group, after):
            sems2, mid = self.passed[group]
            self.relayed[group], mid = _gather_relay_forward(f"gather_relay_{group}", mid, *sems2, after)
            self.passed[group] = (sems2, mid)
            if group == 0:
                start_rest(tuple(mid))
            return tuple(mid)

        def ready(self, group, after):
            sems2, mid = self.passed[group]
            full = _gather_wait(f"gather_wait_{group}", mid, *sems2, after, relay_sems=self.relayed.get(group))
            out = []
            return [ready[gathered[i]](a, shards[i]) for i, a in zip(gather_groups[group], full)]

    weights = Weights()

    col_blocks = lambda g: g.reshape(g.shape[0], N_DEV, g.shape[1] // N_DEV).transpose(1, 0, 2)
    row_blocks = lambda g: g.reshape(N_DEV, g.shape[0] // N_DEV, g.shape[1])
    grad_groups = (("w_down",), ("w_up",), ("w_o", "w_uq", "w_ukv"), ("w_in",))
    transposed = {"w_in": w_in.shape[2], "w_uq": w_uq.shape[2]}
    to_blocks = {
        "w_in": lambda g: g, "w_uq": lambda g: _unpermute_q_rows(g, n_heads),
        "w_ukv": col_blocks, "w_o": row_blocks, "w_up": lambda g: g, "w_down": row_blocks,
    }
    in_flight = []

    class Grads:
        def __init__(self):
            self.core = core
            self.away = {}

        def send_sums(self, group, sums):
            sems, sums, parts, tok = _chip_send_start(f"chip_send_start_{group}", list(sums))
            in_flight.append((sems, sums, parts))
            return (tok,)

        def full(self, group, arrays, received=None):
            nms = grad_groups[group]
            if received is None:
                blocks = [to_blocks[nm](g) for nm, g in zip(nms, arrays)]
                got = _pair_exchange(f"pair_exchange_{group}", blocks, [transposed.get(nm) for nm in nms])
            else:
                blocks, got = [self.away[group][1]], [self.received(group, received)]
            sums = [(_pair_sum_rows if nm in transposed else _pair_sum)(f"pair_sum_{nm}", g, r, core)
                    for nm, g, r in zip(nms, blocks, got)]
            return self.send_sums(group, sums)

        def send_away(self, group, half):
            nm = grad_groups[group][0]
            rows = transposed.get(nm)
            sems, src, land, tok = _pair_send_start(f"pair_send_start_{group}", half if rows is None else to_blocks[nm](half), rows)
            self.away[group] = (sems, src, land, rows)
            return (tok,)

        def received(self, group, after):
            sems, src, land, rows = self.away[group]
            return _pair_send_wait(f"pair_send_wait_{group}", sems, src, land, after, rows)

        def update_now(self, group, after):
            return update(str(group), group, group + 1, after)

    big_out = {}

    def update(tag, first, last, after):
        picked = [i for i in range(first, last) if grad_groups[i][0] not in big_out]
        groups = [in_flight[i] for i in picked]
        parts = _chip_send_wait("chip_send_wait_" + tag, groups, after)
        nms = [nm for i in picked for nm in grad_groups[i]]
        sums = [a for _, s, _ in groups for a in s]
        for nm, p, s in zip(nms, parts, sums):
            view = (lambda a: jnp.swapaxes(a, 1, 2)) if nm in transposed else (lambda a: a)
            out = _sum_adam("adam_" + nm, p, s, chip, view(big[nm]), view(big_m[nm]), view(big_v[nm]), after=after)
            after = (out[0],)
            big_out[nm] = [view(o) for o in out]
        return after

    grad_x, small = _local_step(x[0], loss_target[0], gains, weights, Grads(), first_after=(token,))

    after = update("early", 0, len(in_flight) - 1, (grad_x,))
    total = _small_all_reduce(small, after=after)
    update("late", len(in_flight) - 1, len(in_flight), (total,))
    big_out = [big_out[nm] for nm in names]

    gain_out = _adam_gains(total, gains, gain_m, gain_v)
    taps_out = _adam_taps(total, sum(g.shape[1] for g in gains), me.astype(jnp.int32).reshape(1), conv_w, m_conv_w, v_conv_w)
    loss = total[0, total.shape[1] - 1]

    order = (0, "w_in", "conv", 1, "w_uq", 2, "w_ukv", 3, 4, "w_o", 5, 6, "w_up", "w_down", 7)
    by_name = dict(zip(names, big_out))
    outs = [loss, grad_x[None]]
    for kind in range(4):
        for item in order:
            if item == "conv":
                outs.append(taps_out[kind])
            elif isinstance(item, int):
                outs.append(gain_out[item][kind])
            else:
                outs.append(by_name[item][kind])
    return tuple(outs)
```

```python
import math

import jax
import jax.numpy as jnp
from jax import lax
from jax.experimental import pallas as pl
from jax.experimental.pallas import tpu as pltpu

F32 = jnp.float32
BF16 = jnp.bfloat16

EPS = 1e-6
NEG_INF = -1e30
HEAD = 128
ROPE = 64
QK = HEAD + ROPE
CHUNK = 64
ROPE_THETA = 10000.0
ADAM_LR, ADAM_B1, ADAM_B2, ADAM_EPS, ADAM_WD, ADAM_STEP = 0.001, 0.9, 0.999, 1e-08, 0.01, 10

LANE = 128
SUBLANE = 8
VMEM_LIMIT_BYTES = 56 * 1024 * 1024

N_DEV = 8
N_CHIP = 4
MESH = pl.DeviceIdType.MESH


def _params(*sem):
    return pltpu.CompilerParams(dimension_semantics=sem, vmem_limit_bytes=VMEM_LIMIT_BYTES)


ANY = pl.BlockSpec(memory_space=pl.ANY)


def _call(body, *, in_specs, after=(), **kw):
    n_in, n_after = len(in_specs), len(after)

    def ordered(*refs):
        body(*refs[:n_in], *refs[n_in + n_after:])

    call = pl.pallas_call(ordered, in_specs=[*in_specs, *[ANY] * n_after], **kw)
    return lambda *operands: call(*operands, *after)


def _sublane_sum(v):
    r, w = v.shape
    return jnp.sum(v.reshape(r // SUBLANE, SUBLANE, w), axis=0)


def _rstd(x):
    return lax.rsqrt(jnp.mean(x * x, axis=-1, keepdims=True) + EPS)


def _rms_bwd(x, g, dy):
    r = _rstd(x)
    xh = x * r
    dxh = dy * g
    dx = r * (dxh - xh * jnp.mean(dxh * xh, axis=-1, keepdims=True))
    return dx, dy * xh


def _accumulate(ref, val, step):
    @pl.when(step == 0)
    def _():
        ref[...] = val

    @pl.when(step > 0)
    def _():
        ref[...] += val


NN = ((1,), (0,))
NT = ((1,), (1,))
TN = ((0,), (0,))


def _matmul(name, a, b, *, grid, a_spec, b_spec, out_shape, out_specs, contract, nk=1, acc_shape=None,
            extras=(), extra_specs=(), epilogue=None, after=()):
    multi = isinstance(out_shape, (tuple, list))
    out_shapes = tuple(out_shape) if multi else (out_shape,)
    n_out = len(out_shapes)
    n_extra = len(extras)

    def body(a_ref, b_ref, *rest):
        x_refs = rest[:n_extra]
        o_refs = rest[n_extra:n_extra + n_out]

        def emit(acc):
            vals = epilogue(acc, *[r[...] for r in x_refs]) if epilogue else (acc,)
            for r, v in zip(o_refs, vals):
                r[...] = v.astype(r.dtype)

        p = lax.dot_general(a_ref[...], b_ref[...], (contract, ((), ())), preferred_element_type=F32)
        if nk == 1:
            emit(p)
        else:
            acc_ref = rest[n_extra + n_out]
            k = pl.program_id(2)
            _accumulate(acc_ref, p, k)

            @pl.when(k == nk - 1)
            def _():
                emit(acc_ref[...])

    sem = ("parallel", "parallel") + (("arbitrary",) if nk > 1 else ())
    return _call(
        body, name=name, grid=grid, after=after,
        in_specs=[a_spec, b_spec, *extra_specs],
        out_specs=out_specs,
        out_shape=out_shape,
        scratch_shapes=[pltpu.VMEM(acc_shape, F32)] if nk > 1 else [],
        compiler_params=_params(*sem),
    )(a, b, *extras)


def _fit(n, tile):
    if n <= tile:
        return n
    t = tile - tile % LANE
    while n % t:
        t -= LANE
    return t


def _mm_nn(name, a, b, out_dtype, tm, tn, after=()):
    m, k = a.shape
    n = b.shape[1]
    tm, tn = _fit(m, tm), _fit(n, tn)
    return _matmul(name, a, b, grid=(m // tm, n // tn), after=after,
                   a_spec=pl.BlockSpec((tm, k), lambda i, j: (i, 0)),
                   b_spec=pl.BlockSpec((k, tn), lambda i, j: (0, j)),
                   out_shape=jax.ShapeDtypeStruct((m, n), out_dtype),
                   out_specs=pl.BlockSpec((tm, tn), lambda i, j: (i, j)), contract=NN)


def _mm_nt(name, a, b, out_dtype, tm, tn, after=()):
    m, k = a.shape
    n = b.shape[0]
    tm, tn = _fit(m, tm), _fit(n, tn)
    return _matmul(name, a, b, grid=(m // tm, n // tn), after=after,
                   a_spec=pl.BlockSpec((tm, k), lambda i, j: (i, 0)),
                   b_spec=pl.BlockSpec((tn, k), lambda i, j: (j, 0)),
                   out_shape=jax.ShapeDtypeStruct((m, n), out_dtype),
                   out_specs=pl.BlockSpec((tm, tn), lambda i, j: (i, j)), contract=NT)


def _mm_tn(name, a, b, out_dtype, tm, tn):
    s, m = a.shape
    n = b.shape[1]
    tm, tn = _fit(m, tm), _fit(n, tn)
    return _matmul(name, a, b, grid=(m // tm, n // tn),
                   a_spec=pl.BlockSpec((s, tm), lambda i, j: (0, i)),
                   b_spec=pl.BlockSpec((s, tn), lambda i, j: (0, j)),
                   out_shape=jax.ShapeDtypeStruct((m, n), out_dtype),
                   out_specs=pl.BlockSpec((tm, tn), lambda i, j: (i, j)), contract=TN)


ROWS = 256


def _row_spec(rows, width):
    return pl.BlockSpec((rows, width), lambda i: (i, 0))


def _fixed_spec(rows, width):
    return pl.BlockSpec((rows, width), lambda i: (0, 0))


def _column_pieces(rows, start, width):
    piece = math.gcd(start, width)
    assert piece % LANE == 0
    return [pl.BlockSpec((rows, piece), lambda i, b=start // piece + p: (i, b)) for p in range(width // piece)]


def _rms_fwd(name, x, g, after=()):
    s, w = x.shape
    rows = min(ROWS, s)

    def body(x_ref, g_ref, o_ref):
        xv = x_ref[...]
        o_ref[...] = (xv * _rstd(xv) * g_ref[...]).astype(o_ref.dtype)

    return _call(
        body, name=name, grid=(s // rows,), after=after,
        in_specs=[_row_spec(rows, w), _fixed_spec(1, w)],
        out_specs=_row_spec(rows, w),
        out_shape=jax.ShapeDtypeStruct((s, w), BF16),
        compiler_params=_params("parallel"),
    )(x, g)


def _norm_up(name, x, cols, g, w, after=()):
    s = x.shape[0]
    start, width = cols
    n = w.shape[1]
    tm = min(TILE_M, s)
    pieces = _column_pieces(tm, start, width)
    n_p = len(pieces)

    def body(*refs):
        g_ref, w_ref, xn_ref, o_ref = refs[n_p:]
        xv = refs[0][...] if n_p == 1 else jnp.concatenate([r[...] for r in refs[:n_p]], axis=1)
        xn = (xv * _rstd(xv) * g_ref[...]).astype(BF16)
        xn_ref[...] = xn
        o_ref[...] = jnp.dot(xn, w_ref[...], preferred_element_type=F32)

    return _call(
        body, name=name, grid=(s // tm,), after=after,
        in_specs=[*pieces, _fixed_spec(1, width), _fixed_spec(width, n)],
        out_specs=[_row_spec(tm, width), _row_spec(tm, n)],
        out_shape=[jax.ShapeDtypeStruct((s, width), BF16), jax.ShapeDtypeStruct((s, n), F32)],
        compiler_params=_params("parallel"),
    )(*[x] * n_p, g, w)


def _up_norm_bwd(name, dy, w, x, cols, g, after=()):
    s, n = dy.shape
    start, width = cols
    tm = min(TILE_M, s)
    pieces = _column_pieces(tm, start, width)
    n_p = len(pieces)

    def body(dy_ref, w_ref, *refs):
        g_ref, dx_ref, dg_ref = refs[n_p:]
        xv = refs[0][...] if n_p == 1 else jnp.concatenate([r[...] for r in refs[:n_p]], axis=1)
        dxn = lax.dot_general(dy_ref[...], w_ref[...], (NT, ((), ())), preferred_element_type=F32)
        dx, dgc = _rms_bwd(xv, g_ref[...], dxn)
        dx_ref[...] = dx.astype(dx_ref.dtype)
        _accumulate(dg_ref, _sublane_sum(dgc), pl.program_id(0))

    return _call(
        body, name=name, grid=(s // tm,), after=after,
        in_specs=[_row_spec(tm, n), _fixed_spec(width, n), *pieces, _fixed_spec(1, width)],
        out_specs=[_row_spec(tm, width), _fixed_spec(SUBLANE, width)],
        out_shape=[jax.ShapeDtypeStruct((s, width), BF16), jax.ShapeDtypeStruct((SUBLANE, width), F32)],
        compiler_params=_params("arbitrary"),
    )(dy, w, *[x] * n_p, g)


def _mid_fwd(x, y, g_post, g_pre, after=()):
    s, w = x.shape
    rows = min(ROWS, s)

    def body(x_ref, y_ref, gp_ref, gq_ref, x2_ref, h2_ref):
        yv = y_ref[...]
        x2 = x_ref[...] + yv * _rstd(yv) * gp_ref[...]
        x2_ref[...] = x2
        h2_ref[...] = (x2 * _rstd(x2) * gq_ref[...]).astype(h2_ref.dtype)

    return _call(
        body, name="mid_fwd", grid=(s // rows,), after=after,
        in_specs=[_row_spec(rows, w), _row_spec(rows, w), _fixed_spec(1, w), _fixed_spec(1, w)],
        out_specs=[_row_spec(rows, w), _row_spec(rows, w)],
        out_shape=[jax.ShapeDtypeStruct((s, w), F32), jax.ShapeDtypeStruct((s, w), BF16)],
        compiler_params=_params("parallel"),
    )(x, y, g_post, g_pre)


def _head(m, x2, tgt, g):
    s, w = m.shape
    rows = min(ROWS, s)

    def body(m_ref, x2_ref, t_ref, g_ref, dout_ref, dm_ref, dg_ref, loss_ref):
        mv = m_ref[...]
        gv = g_ref[...]
        out = x2_ref[...] + mv * _rstd(mv) * gv
        err = out - t_ref[...]
        dout = err * (1.0 / w)
        dout_ref[...] = dout
        dm, dgc = _rms_bwd(mv, gv, dout)
        dm_ref[...] = dm.astype(dm_ref.dtype)
        sq = err * err
        lanes = sq[:, 0:LANE]
        for j in range(1, w // LANE):
            lanes = lanes + sq[:, j * LANE:(j + 1) * LANE]
        step = pl.program_id(0)
        _accumulate(dg_ref, _sublane_sum(dgc), step)
        _accumulate(loss_ref, _sublane_sum(lanes) * (0.5 / w), step)

    return pl.pallas_call(
        body, name="head", grid=(s // rows,),
        in_specs=[_row_spec(rows, w), _row_spec(rows, w), _row_spec(rows, w), _fixed_spec(1, w)],
        out_specs=[_row_spec(rows, w), _row_spec(rows, w), _fixed_spec(SUBLANE, w), _fixed_spec(SUBLANE, LANE)],
        out_shape=[jax.ShapeDtypeStruct((s, w), F32), jax.ShapeDtypeStruct((s, w), BF16),
                   jax.ShapeDtypeStruct((SUBLANE, w), F32), jax.ShapeDtypeStruct((SUBLANE, LANE), F32)],
        compiler_params=_params("arbitrary"),
    )(m, x2, tgt, g)


def _mid_bwd(x2, y, d_out, d_h2, g_pre, g_post, after=()):
    s, w = x2.shape
    rows = min(ROWS, s)

    def body(x2_ref, y_ref, dout_ref, dh2_ref, gq_ref, gp_ref, dx2_ref, dy_ref, dgq_ref, dgp_ref):
        dx, dgq = _rms_bwd(x2_ref[...], gq_ref[...], dh2_ref[...])
        dx2 = dout_ref[...] + dx
        dx2_ref[...] = dx2
        dy, dgp = _rms_bwd(y_ref[...], gp_ref[...], dx2)
        dy_ref[...] = dy.astype(dy_ref.dtype)
        step = pl.program_id(0)
        _accumulate(dgq_ref, _sublane_sum(dgq), step)
        _accumulate(dgp_ref, _sublane_sum(dgp), step)

    return _call(
        body, name="mid_bwd", grid=(s // rows,), after=after,
        in_specs=[_row_spec(rows, w)] * 4 + [_fixed_spec(1, w)] * 2,
        out_specs=[_row_spec(rows, w), _row_spec(rows, w), _fixed_spec(SUBLANE, w), _fixed_spec(SUBLANE, w)],
        out_shape=[jax.ShapeDtypeStruct((s, w), F32), jax.ShapeDtypeStruct((s, w), BF16),
                   jax.ShapeDtypeStruct((SUBLANE, w), F32), jax.ShapeDtypeStruct((SUBLANE, w), F32)],
        compiler_params=_params("arbitrary"),
    )(x2, y, d_out, d_h2, g_pre, g_post)


def _first_bwd(x, g, d_h1, d_x2, after=()):
    s, w = x.shape
    rows = min(ROWS, s)

    def body(x_ref, g_ref, dh_ref, dx2_ref, dx_ref, dg_ref):
        dx, dgc = _rms_bwd(x_ref[...], g_ref[...], dh_ref[...])
        dx_ref[...] = dx2_ref[...] + dx
        _accumulate(dg_ref, _sublane_sum(dgc), pl.program_id(0))

    return _call(
        body, name="first_bwd", grid=(s // rows,), after=after,
        in_specs=[_row_spec(rows, w), _fixed_spec(1, w), _row_spec(rows, w), _row_spec(rows, w)],
        out_specs=[_row_spec(rows, w), _fixed_spec(SUBLANE, w)],
        out_shape=[jax.ShapeDtypeStruct((s, w), F32), jax.ShapeDtypeStruct((SUBLANE, w), F32)],
        compiler_params=_params("arbitrary"),
    )(x, g, d_h1, d_x2)


def _shift_down(v, k):
    t = lax.broadcasted_iota(jnp.int32, v.shape, 0)
    return jnp.where(t >= k, pltpu.roll(v, k, 0), 0.0)


def _shift_up(v, k):
    n = v.shape[0]
    t = lax.broadcasted_iota(jnp.int32, v.shape, 0)
    return jnp.where(t < n - k, pltpu.roll(v, n - k, 0), 0.0)


def _conv_core(u, b, c, w):
    z = c * u
    conv = w[0:1, :] * _shift_down(z, 2) + w[1:2, :] * _shift_down(z, 1) + w[2:3, :] * z
    return z, conv, b * conv


def _conv_fwd(proj, conv_w, g, n_groups, out_width, after=()):
    s = proj.shape[0]

    def body(u_ref, b_ref, c_ref, w_ref, g_ref, o_ref):
        _, _, yr = _conv_core(u_ref[...], b_ref[...], c_ref[...], w_ref[...])
        o_ref[...] = (yr * _rstd(yr) * g_ref[...]).astype(o_ref.dtype)

    col = lambda k: pl.BlockSpec((s, HEAD), lambda i: (0, k * n_groups + i))
    return _call(
        body, name="conv_fwd", grid=(n_groups,), after=after,
        in_specs=[col(0), col(1), col(2), pl.BlockSpec((3, HEAD), lambda i: (0, i)), pl.BlockSpec((1, HEAD), lambda i: (0, i))],
        out_specs=pl.BlockSpec((s, HEAD), lambda i: (0, i)),
        out_shape=jax.ShapeDtypeStruct((s, out_width), BF16),
        compiler_params=_params("parallel"),
    )(proj, proj, proj, conv_w, g)


def _conv_bwd(proj, d_mix, conv_w, g, n_groups):
    s = proj.shape[0]
    width = n_groups * HEAD

    def body(u_ref, b_ref, c_ref, dy_ref, w_ref, g_ref, du_ref, db_ref, dc_ref, dg_ref, dw_ref):
        u, b, c, w = u_ref[...], b_ref[...], c_ref[...], w_ref[...]
        z, conv, yr = _conv_core(u, b, c, w)
        dyr, dgc = _rms_bwd(yr, g_ref[...], dy_ref[...])
        dconv = dyr * b
        db_ref[...] = (dyr * conv).astype(db_ref.dtype)
        dz = w[2:3, :] * dconv + w[1:2, :] * _shift_up(dconv, 1) + w[0:1, :] * _shift_up(dconv, 2)
        dc_ref[...] = (dz * u).astype(dc_ref.dtype)
        du_ref[...] = (dz * c).astype(du_ref.dtype)
        dg_ref[...] = _sublane_sum(dgc)
        dw_ref[0] = _sublane_sum(dconv * _shift_down(z, 2))
        dw_ref[1] = _sublane_sum(dconv * _shift_down(z, 1))
        dw_ref[2] = _sublane_sum(dconv * z)

    col = lambda k: pl.BlockSpec((s, HEAD), lambda i: (0, k * n_groups + i))
    grp = pl.BlockSpec((s, HEAD), lambda i: (0, i))
    return pl.pallas_call(
        body, name="conv_bwd", grid=(n_groups,),
        in_specs=[col(0), col(1), col(2), grp, pl.BlockSpec((3, HEAD), lambda i: (0, i)), pl.BlockSpec((1, HEAD), lambda i: (0, i))],
        out_specs=[grp, grp, grp, pl.BlockSpec((SUBLANE, HEAD), lambda i: (0, i)),
                   pl.BlockSpec((3, SUBLANE, HEAD), lambda i: (0, 0, i))],
        out_shape=[jax.ShapeDtypeStruct((s, width), BF16)] * 3
        + [jax.ShapeDtypeStruct((SUBLANE, width), F32), jax.ShapeDtypeStruct((3, SUBLANE, width), F32)],
        compiler_params=_params("parallel"),
    )(proj, proj, proj, d_mix, conv_w, g)


def _rope_tables(s, n_heads):
    pos = jnp.arange(s, dtype=F32)
    inv_freq = jnp.power(ROPE_THETA, -jnp.arange(0, ROPE, 2, dtype=F32) / ROPE)
    ang = pos[:, None] * inv_freq[None, :]
    cos, sin = jnp.cos(ang), jnp.sin(ang)
    cs = jnp.concatenate([cos, cos], axis=1)
    sn = jnp.concatenate([-sin, sin], axis=1)
    pad = jnp.zeros((s, LANE - ROPE), F32)
    return (jnp.tile(cs, (1, n_heads)), jnp.tile(sn, (1, n_heads)),
            jnp.concatenate([cs, pad], axis=1), jnp.concatenate([sn, pad], axis=1))


def _swap_halves(v):
    w = v.shape[1]
    lane = lax.broadcasted_iota(jnp.int32, v.shape, 1)
    first = (lane % ROPE) < (ROPE // 2)
    return jnp.where(first, pltpu.roll(v, w - ROPE // 2, 1), pltpu.roll(v, ROPE // 2, 1))


def _pack_heads(q, kv, proj, kr_col, tables, n_heads, after=()):
    s = q.shape[0]
    rows = min(ROWS, s)
    cq, sq, ck, sk = tables
    wq = n_heads * ROPE

    def body(q_ref, kv_ref, kr_ref, cq_ref, sq_ref, ck_ref, sk_ref, qo_ref, ko_ref, vo_ref):
        qr = q_ref[:, n_heads * HEAD:]
        qr = qr * cq_ref[...] + _swap_halves(qr) * sq_ref[...]
        krv = kr_ref[...]
        krv = krv * ck_ref[...] + _swap_halves(krv) * sk_ref[...]
        for h in range(n_heads):
            qo_ref[h] = jnp.concatenate([q_ref[:, h * HEAD:(h + 1) * HEAD], qr[:, h * ROPE:(h + 1) * ROPE]], axis=1).astype(BF16)
            ko_ref[h] = jnp.concatenate([kv_ref[:, 2 * h * HEAD:(2 * h + 1) * HEAD], krv[:, :ROPE]], axis=1).astype(BF16)
            vo_ref[h] = kv_ref[:, (2 * h + 1) * HEAD:(2 * h + 2) * HEAD].astype(BF16)

    hs = lambda w: pl.BlockSpec((n_heads, rows, w), lambda i: (0, i, 0))
    return _call(
        body, name="pack_heads", grid=(s // rows,), after=after,
        in_specs=[_row_spec(rows, q.shape[1]), _row_spec(rows, kv.shape[1]), pl.BlockSpec((rows, LANE), lambda i: (i, kr_col // LANE)),
                  _row_spec(rows, wq), _row_spec(rows, wq), _row_spec(rows, LANE), _row_spec(rows, LANE)],
        out_specs=[hs(QK), hs(QK), hs(HEAD)],
        out_shape=[jax.ShapeDtypeStruct((n_heads, s, QK), BF16), jax.ShapeDtypeStruct((n_heads, s, QK), BF16),
                   jax.ShapeDtypeStruct((n_heads, s, HEAD), BF16)],
        compiler_params=_params("parallel"),
    )(q, kv, proj, cq, sq, ck, sk)


def _unpack_heads(dq, dk, dv, tables, n_heads):
    s = dq.shape[1]
    rows = min(ROWS, s)
    cq, sq, ck, sk = tables
    wq = n_heads * ROPE

    def body(dq_ref, dk_ref, dv_ref, cq_ref, sq_ref, ck_ref, sk_ref, qo_ref, kvo_ref, kro_ref):
        dqr = jnp.concatenate([dq_ref[h][:, HEAD:] for h in range(n_heads)], axis=1)
        dqr = dqr * cq_ref[...] - _swap_halves(dqr) * sq_ref[...]
        dkr = dk_ref[0][:, HEAD:]
        for h in range(1, n_heads):
            dkr = dkr + dk_ref[h][:, HEAD:]
        dkr = jnp.concatenate([dkr, jnp.zeros((rows, LANE - ROPE), F32)], axis=1)
        dkr = dkr * ck_ref[...] - _swap_halves(dkr) * sk_ref[...]
        kro_ref[...] = dkr.astype(kro_ref.dtype)
        qo_ref[:, n_heads * HEAD:] = dqr.astype(qo_ref.dtype)
        for h in range(n_heads):
            qo_ref[:, h * HEAD:(h + 1) * HEAD] = dq_ref[h][:, :HEAD].astype(qo_ref.dtype)
            kvo_ref[:, 2 * h * HEAD:(2 * h + 1) * HEAD] = dk_ref[h][:, :HEAD].astype(kvo_ref.dtype)
            kvo_ref[:, (2 * h + 1) * HEAD:(2 * h + 2) * HEAD] = dv_ref[h].astype(kvo_ref.dtype)

    hs = lambda w: pl.BlockSpec((n_heads, rows, w), lambda i: (0, i, 0))
    return pl.pallas_call(
        body, name="unpack_heads", grid=(s // rows,),
        in_specs=[hs(QK), hs(QK), hs(HEAD), _row_spec(rows, wq), _row_spec(rows, wq), _row_spec(rows, LANE), _row_spec(rows, LANE)],
        out_specs=[_row_spec(rows, n_heads * QK), _row_spec(rows, 2 * n_heads * HEAD), _row_spec(rows, LANE)],
        out_shape=[jax.ShapeDtypeStruct((s, n_heads * QK), BF16), jax.ShapeDtypeStruct((s, 2 * n_heads * HEAD), BF16),
                   jax.ShapeDtypeStruct((s, LANE), BF16)],
        compiler_params=_params("parallel"),
    )(dq, dk, dv, cq, sq, ck, sk)


TQ = 256


LOG2_E = 1.4426950408889634


def _softmax_parts(q, k):
    tq, n_keys = q.shape[0], k.shape[0]
    sc = lax.dot_general(q, k, (NT, ((), ())), preferred_element_type=F32) * (QK ** -0.5 * LOG2_E)
    row = lax.broadcasted_iota(jnp.int32, (tq, tq), 0)
    col = lax.broadcasted_iota(jnp.int32, (tq, tq), 1)
    own = jnp.where(col // CHUNK <= row // CHUNK, sc[:, n_keys - tq:], NEG_INF)
    sc = own if n_keys == tq else jnp.concatenate([sc[:, :n_keys - tq], own], axis=1)
    e = jnp.exp2(sc - jnp.max(sc, axis=-1, keepdims=True))
    return e, 1.0 / jnp.sum(e, axis=-1, keepdims=True)


def _prob_columns(c, tq):
    return pl.ds(tq * (c * (c + 1) // 2), (c + 1) * tq)


def _attn_fwd(q, k, v, g, mix, col0, first, between):
    n_heads, s, _ = q.shape
    tq = min(TQ, s)
    assert tq % CHUNK == 0 and s % tq == 0
    n_blocks = s // tq
    p_cols = tq * (n_blocks * (n_blocks + 1) // 2)
    out_shape = [jax.ShapeDtypeStruct((n_heads, s, HEAD), F32), jax.ShapeDtypeStruct((n_heads, tq, p_cols), BF16),
                 jax.ShapeDtypeStruct(mix.shape, mix.dtype)]
    done, after = (mix,), ()
    for part, (h0, h1) in enumerate(((0, first), (first, n_heads))):

        def body(q_ref, k_ref, v_ref, g_ref, *rest):
            o_ref, p_ref, y_ref = rest[-3:]
            for c in range(n_blocks):
                rows, n_keys = pl.ds(c * tq, tq), (c + 1) * tq
                e, inv = _softmax_parts(q_ref[rows, :], k_ref[0:n_keys, :])
                p = (e * inv).astype(BF16)
                p_ref[:, _prob_columns(c, tq)] = p
                o = jnp.dot(p, v_ref[0:n_keys, :], preferred_element_type=F32)
                o_ref[rows, :] = o
                y_ref[rows, :] = (o * _rstd(o) * g_ref[...]).astype(y_ref.dtype)

        head = lambda w, h0=h0: pl.BlockSpec((None, s, w), lambda h: (h0 + h, 0, 0))
        n_done = len(done)
        done = _call(
            body, name=f"attn_fwd_{part}", grid=(h1 - h0,), after=after,
            in_specs=[head(QK), head(QK), head(HEAD), pl.BlockSpec((1, HEAD), lambda h, h0=h0: (0, h0 + h))] + [ANY] * n_done,
            out_specs=[head(HEAD), pl.BlockSpec((None, tq, p_cols), lambda h, h0=h0: (h0 + h, 0, 0)),
                       pl.BlockSpec((s, HEAD), lambda h, h0=h0: (0, col0 // HEAD + h0 + h))],
            out_shape=out_shape,
            input_output_aliases={4 + i: 3 - n_done + i for i in range(n_done)},
            compiler_params=_params("parallel"),
        )(q, k, v, g, *done)
        after = between(done) if part == 0 else ()
    return done


def _attn_bwd(q, k, v, o, probs, d_mix, g, col0, after=()):
    n_heads, s, _ = q.shape
    tq = probs.shape[1]

    def body(q_ref, k_ref, v_ref, o_ref, p_ref, dy_ref, g_ref, dq_ref, dk_ref, dv_ref, dg_ref):
        dg = None
        for c in reversed(range(s // tq)):
            rows, n_keys = pl.ds(c * tq, tq), (c + 1) * tq
            o = o_ref[rows, :]
            do, dgc = _rms_bwd(o, g_ref[...], dy_ref[rows, :])
            do = do.astype(BF16)
            dg = _sublane_sum(dgc) if dg is None else dg + _sublane_sum(dgc)
            p = p_ref[:, _prob_columns(c, tq)]
            dp = lax.dot_general(do, v_ref[0:n_keys, :], (NT, ((), ())), preferred_element_type=F32)
            ds = (p.astype(F32) * (dp - jnp.sum(do.astype(F32) * o, axis=-1, keepdims=True))).astype(BF16)
            dq_ref[rows, :] = jnp.dot(ds, k_ref[0:n_keys, :], preferred_element_type=F32) * (QK ** -0.5)
            dk = lax.dot_general(ds, q_ref[rows, :], (TN, ((), ())), preferred_element_type=F32)
            dv = lax.dot_general(p, do, (TN, ((), ())), preferred_element_type=F32)
            if n_keys == s:
                dk_ref[...] = dk
                dv_ref[...] = dv
            else:
                dk_ref[0:n_keys, :] += dk
                dv_ref[0:n_keys, :] += dv
        dk_ref[...] = dk_ref[...] * (QK ** -0.5)
        dg_ref[...] = dg

    c0 = col0 // HEAD
    head = lambda w: pl.BlockSpec((None, s, w), lambda h, *_: (h, 0, 0))
    in_specs = [head(QK), head(QK), head(HEAD), head(HEAD), pl.BlockSpec((None, tq, probs.shape[2]), lambda h, *_: (h, 0, 0)),
                pl.BlockSpec((s, HEAD), lambda h, *_: (0, c0 + h)), pl.BlockSpec((1, HEAD), lambda h, *_: (0, h))]
    out_specs = [head(QK), head(QK), head(HEAD), pl.BlockSpec((SUBLANE, HEAD), lambda h, *_: (0, h))]
    out_shape = [jax.ShapeDtypeStruct((n_heads, s, QK), F32), jax.ShapeDtypeStruct((n_heads, s, QK), F32),
                 jax.ShapeDtypeStruct((n_heads, s, HEAD), F32), jax.ShapeDtypeStruct((SUBLANE, n_heads * HEAD), F32)]
    return _call(body, name="attn_bwd", grid=(n_heads,), after=after, in_specs=in_specs, out_specs=out_specs,
                 out_shape=out_shape, compiler_params=_params("parallel"))(q, k, v, o, probs, d_mix, g)


TILE_M = 1024
TILE_N = 1024


def _up_fwd(h2, w_up, between):
    s, d = h2.shape
    nb, _, fb = w_up.shape
    tm = min(TILE_M, s)
    done, after = (), ()
    for tile in range(s // tm):

        def body(h_ref, w_ref, *rest):
            a_ref, r_ref = rest[-2:]
            r = jnp.maximum(jnp.dot(h_ref[...], w_ref[...], preferred_element_type=F32), 0.0)
            a_ref[...] = (r * r).astype(a_ref.dtype)
            r_ref[...] = r.astype(r_ref.dtype)

        blk = pl.BlockSpec((tm, fb), lambda j, tile=tile: (tile, j))
        done = _call(
            body, name=f"up_fwd_{tile}", grid=(nb,), after=after,
            in_specs=[pl.BlockSpec((tm, d), lambda j, tile=tile: (tile, 0)), pl.BlockSpec((None, d, fb), lambda j: (j, 0, 0))]
                     + [ANY] * len(done),
            out_specs=[blk, blk], out_shape=[jax.ShapeDtypeStruct((s, nb * fb), BF16)] * 2,
            input_output_aliases={2 + i: i for i in range(len(done))},
            compiler_params=_params("parallel"),
        )(h2, w_up, *done)
        after = between(done) if tile == 0 else ()
    return done


def _down_fwd(a, w_down):
    s, f = a.shape
    d = w_down.shape[1]
    tm, tn, tk = min(TILE_M,s), min(TILE_N,d), 2048
    nk = f // tk
    return _matmul("down_fwd", a, w_down, grid=(s // tm, d // tn, nk),
                   a_spec=pl.BlockSpec((tm, tk), lambda i, j, k: (i, k)),
                   b_spec=pl.BlockSpec((tk, tn), lambda i, j, k: (k, j)),
                   out_shape=jax.ShapeDtypeStruct((s, d), F32),
                   out_specs=pl.BlockSpec((tm, tn), lambda i, j, k: (i, j)),
                   contract=NN, nk=nk, acc_shape=(tm, tn))


def _down_bwd_act(d_m, w_down, r, after=()):
    s, d = d_m.shape
    f = w_down.shape[0]
    tm, tn = min(TILE_M,s), min(TILE_N,f)
    blk = pl.BlockSpec((tm, tn), lambda i, j: (i, j))
    return _matmul("down_bwd_act", d_m, w_down, grid=(s // tm, f // tn), after=after,
                   a_spec=pl.BlockSpec((tm, d), lambda i, j: (i, 0)),
                   b_spec=pl.BlockSpec((tn, d), lambda i, j: (j, 0)),
                   out_shape=jax.ShapeDtypeStruct((s, f), BF16), out_specs=blk, contract=NT,
                   extras=(r,), extra_specs=(blk,),
                   epilogue=lambda acc, rv: (acc * (2.0 * rv.astype(F32)),))


def _up_bwd_act(d_up, w_up, after=()):
    s, _ = d_up.shape
    nb, d, fb = w_up.shape
    tm, tn = min(TILE_M, s), min(TILE_N,d)
    pair = 2
    n_after = len(after)

    def body(a_ref, w_ref, *rest):
        o_ref, acc_ref = rest[n_after:]
        k = pl.program_id(2)
        p = None
        for t in range(pair):
            term = lax.dot_general(a_ref[:, t * fb:(t + 1) * fb], w_ref[t], (NT, ((), ())), preferred_element_type=F32)
            p = term if p is None else p + term
        _accumulate(acc_ref, p, k)

        @pl.when(k == nb // pair - 1)
        def _():
            o_ref[...] = acc_ref[...]

    return pl.pallas_call(
        body, name="up_bwd_act", grid=(s // tm, d // tn, nb // pair),
        in_specs=[pl.BlockSpec((tm, pair * fb), lambda i, j, k: (i, k)),
                  pl.BlockSpec((pair, tn, fb), lambda i, j, k: (k, j, 0))] + [ANY] * n_after,
        out_specs=pl.BlockSpec((tm, tn), lambda i, j, k: (i, j)),
        out_shape=jax.ShapeDtypeStruct((s, d), F32),
        scratch_shapes=[pltpu.VMEM((tm, tn), F32)],
        compiler_params=_params("parallel", "parallel", "arbitrary"),
    )(d_up, w_up, *after)


def _half_grad(name, a, b, core, home, received, after, *, grid, a_block, a_map, b_block, b_map, o_block, o_map, out_shape):
    n_after = len(after)
    pick = (lambda ref: ref[0]) if home else (lambda ref: 1 - ref[0])

    def body(core_ref, a_ref, b_ref, *rest):
        acc = lax.dot_general(a_ref[...], b_ref[...], (TN, ((), ())), preferred_element_type=F32)
        if received is not None:
            acc = acc + rest[0][...].astype(F32)
        rest[-1][...] = acc.astype(rest[-1].dtype)

    wrap = lambda fn: (lambda i, j, core_ref: fn(i, j, pick(core_ref)))
    o_spec = pl.BlockSpec(o_block, wrap(o_map))
    extra = [] if received is None else [o_spec]
    operands = [] if received is None else [received]
    return pl.pallas_call(
        body, name=name,
        grid_spec=pltpu.PrefetchScalarGridSpec(
            num_scalar_prefetch=1, grid=grid,
            in_specs=[pl.BlockSpec(a_block, wrap(a_map)), pl.BlockSpec(b_block, wrap(b_map))] + extra + [ANY] * n_after,
            out_specs=o_spec),
        out_shape=out_shape,
        compiler_params=_params("parallel", "parallel"),
    )(core, a, b, *operands, *after)


def _down_half_grad(name, a, d_m, core, home, received=None, after=()):
    s, f = a.shape
    d = d_m.shape[1]
    r = f // N_DEV
    tn = min(TILE_N, d)
    return _half_grad(name, a, d_m, core, home, received, after, grid=(N_CHIP, d // tn),
                      a_block=(s, r), a_map=lambda k, j, p: (0, 2 * k + p),
                      b_block=(s, tn), b_map=lambda k, j, p: (0, j),
                      o_block=(None, r, tn), o_map=lambda k, j, p: (k, 0, j),
                      out_shape=jax.ShapeDtypeStruct((N_CHIP, r, d), BF16))


def _up_half_grad(name, h2, d_up, core, home, received=None, after=()):
    s, d = h2.shape
    fb = d_up.shape[1] // N_DEV
    tm = min(TILE_M, d)
    return _half_grad(name, h2, d_up, core, home, received, after, grid=(d // tm, N_CHIP),
                      a_block=(s, tm), a_map=lambda i, k, p: (0, i),
                      b_block=(s, fb), b_map=lambda i, k, p: (0, 2 * k + p),
                      o_block=(None, tm, fb), o_map=lambda i, k, p: (k, i, 0),
                      out_shape=jax.ShapeDtypeStruct((N_CHIP, d, fb), BF16))


MXU_WIDTH = 256


def _in_pad(in_width):
    return -(-in_width // MXU_WIDTH) * MXU_WIDTH


def _join_col_shards(name, blocks, own, device, pieces=None):
    n, r, w = blocks.shape
    rows = min(ROWS, r)
    pieces = pieces or [(j, 0, w) for j in range(n)]
    used = sum(b - a for _, a, b in pieces)
    width = _in_pad(used)

    def body(dev_ref, x_ref, own_ref, o_ref):
        block = lambda j: jnp.where(dev_ref[0] == j, own_ref[...], x_ref[j])
        cols = [block(j)[:, a:b] for j, a, b in pieces]
        tail = [jnp.zeros((rows, width - used), o_ref.dtype)] if width > used else []
        o_ref[...] = jnp.concatenate(cols + tail, axis=1)

    return pl.pallas_call(
        body, name=name,
        grid_spec=pltpu.PrefetchScalarGridSpec(
            num_scalar_prefetch=1, grid=(r // rows,),
            in_specs=[pl.BlockSpec((n, rows, w), lambda i, dev: (0, i, 0)), pl.BlockSpec((rows, w), lambda i, dev: (i, 0))],
            out_specs=pl.BlockSpec((rows, width), lambda i, dev: (i, 0))),
        out_shape=jax.ShapeDtypeStruct((r, width), blocks.dtype),
        compiler_params=_params("parallel"),
    )(device, blocks, own)


def _unpermute_q_rows(wt, n_heads):
    r = wt.shape[1]
    nope = wt[:n_heads * HEAD].reshape(n_heads, HEAD, r)
    rope = wt[n_heads * HEAD:].reshape(n_heads, ROPE, r)
    return jnp.concatenate([nope, rope], axis=1).reshape(n_heads * QK, r)


def _local_step(x, tgt, gains, weights, grads, first_after=()):
    pre_mix_g, q_norm_g, kv_norm_g, conv_out_g, attn_out_g, post_mix_g, pre_mlp_g, post_mlp_g = gains
    s, d = x.shape
    conv_width = conv_out_g.shape[1]
    n_groups = conv_width // HEAD
    r_q, r_kv = q_norm_g.shape[1], kv_norm_g.shape[1]
    n_heads = attn_out_g.shape[1] // HEAD
    c_q0 = 3 * conv_width
    c_kv0 = c_q0 + r_q
    c_kr0 = c_kv0 + r_kv
    in_pad = _in_pad(c_kr0 + ROPE)
    tn_in = _fit(in_pad, 6 * MXU_WIDTH)
    tables = _rope_tables(s, n_heads)

    h1 = _rms_fwd("pre_mix_norm", x, pre_mix_g, after=first_after)
    weights.forward(0, (h1,))
    weights.relay(0, tables)
    w_in_p, conv_w = weights.ready(0, ())
    proj = _mm_nn("in_proj", h1, w_in_p, F32, TILE_M, tn_in)
    y_conv = _conv_fwd(proj, conv_w, conv_out_g, n_groups, conv_width + n_heads * HEAD, after=weights.forward(1, (proj,)))
    w_uq_p, w_ukv, w_o = weights.ready(1, (y_conv,))
    qn, q = _norm_up("q_up", proj, (c_q0, r_q), q_norm_g, w_uq_p)
    kvn, kv = _norm_up("kv_up", proj, (c_kv0, r_kv), kv_norm_g, w_ukv)
    qh, kh, vh = _pack_heads(q, kv, proj, c_kr0, tables, n_heads)
    o, probs, mix = _attn_fwd(qh, kh, vh, attn_out_g, y_conv, conv_width, n_heads // 4,
                              lambda done: weights.forward(2, tuple(done)))
    y = _mm_nn("out_proj", mix, w_o, F32, TILE_M, TILE_N, after=weights.start(3, (mix,)))
    x2, h2 = _mid_fwd(x, y, post_mix_g, pre_mlp_g)
    weights.relay(2, (h2,))
    (w_up,) = weights.ready(2, ())
    a, r = _up_fwd(h2, w_up, lambda done: weights.forward(3, tuple(done)))
    weights.relay(3, (a,))
    (w_down,) = weights.ready(3, ())
    m = _down_fwd(a, w_down)

    d_out, d_m, dg_post_mlp, loss_part = _head(m, x2, tgt, post_mlp_g)
    core = grads.core
    away = _down_half_grad("down_bwd_w_away", a, d_m, core, home=False)
    d_up = _down_bwd_act(d_m, w_down, r, after=grads.send_away(0, away))
    sums = _down_half_grad("down_bwd_w_home", a, d_m, core, home=True, received=grads.received(0, (d_up,)))
    away = _up_half_grad("up_bwd_w_away", h2, d_up, core, home=False, after=grads.send_sums(0, (sums,)))
    d_h2 = _up_bwd_act(d_up, w_up, after=grads.send_away(1, away))
    sums = _up_half_grad("up_bwd_w_home", h2, d_up, core, home=True, received=grads.received(1, (d_h2,)))
    d_x2, d_y, dg_pre_mlp, dg_post_mix = _mid_bwd(x2, y, d_out, d_h2, pre_mlp_g, post_mix_g, after=grads.send_sums(1, (sums,)))
    d_mix = _mm_nt("out_proj_bwd_act", d_y, w_o, F32, TILE_M, TILE_N)
    gw_o = _mm_tn("out_proj_bwd_w", mix, d_y, BF16, TILE_M, TILE_N)
    dqh, dkh, dvh, dg_attn = _attn_bwd(qh, kh, vh, o, probs, d_mix, attn_out_g, conv_width)
    d_q, d_kv, d_kr = _unpack_heads(dqh, dkh, dvh, tables, n_heads)
    gw_uq_t = _mm_tn("q_up_bwd_w", d_q, qn, F32, TILE_M, TILE_N)
    gw_ukv = _mm_tn("kv_up_bwd_w", kvn, d_kv, BF16, TILE_M, TILE_N)
    d_cq, dg_q = _up_norm_bwd("q_up_bwd_act", d_q, w_uq_p, proj, (c_q0, r_q), q_norm_g, after=grads.full(2, (gw_o, gw_uq_t, gw_ukv)))
    d_ckv, dg_kv = _up_norm_bwd("kv_up_bwd_act", d_kv, w_ukv, proj, (c_kv0, r_kv), kv_norm_g)
    d_u, d_b, d_c, dg_conv, dw_conv = _conv_bwd(proj, d_mix, conv_w, conv_out_g, n_groups)
    d_proj = jnp.concatenate([d_u, d_b, d_c, d_cq, d_ckv, d_kr, jnp.zeros((s, in_pad - c_kr0 - LANE), BF16)], axis=1)
    gw_in_t = _mm_tn("in_proj_bwd_w", d_proj, h1, F32, tn_in, TILE_N)
    updated = grads.update_now(0, grads.send_away(3, gw_in_t))
    d_h1 = _mm_nt("in_proj_bwd_act", d_proj, w_in_p, F32, TILE_M, TILE_N, after=grads.full(3, (gw_in_t,), received=updated))
    grad_x, dg_pre_mix = _first_bwd(x, pre_mix_g, d_h1, d_x2)

    small = [dg_pre_mix, dg_q, dg_kv, dg_conv, dg_attn, dg_post_mix, dg_pre_mlp, dg_post_mlp,
             dw_conv[0], dw_conv[1], dw_conv[2], loss_part]
    return grad_x, jnp.concatenate(small, axis=1)


HBM = pl.BlockSpec(memory_space=pltpu.HBM)
SEM = pl.BlockSpec(memory_space=pltpu.SEMAPHORE)
IN_VMEM = pl.BlockSpec(memory_space=pltpu.VMEM)
SPLIT = pltpu.CompilerParams(has_side_effects=pltpu.SideEffectType.DATAFLOW_SIDE_EFFECTING)


def _in_hbm(a):
    return pltpu.with_memory_space_constraint(a, pltpu.HBM)


def _hbm_like(a):
    return pltpu.HBM(a.shape, a.dtype)


def _place():
    x, y, c = lax.axis_index("x"), lax.axis_index("y"), lax.axis_index("c")
    other_chips = [(1 - x, y), (x, 1 - y), (1 - x, 1 - y)]
    return x, y, c, other_chips


def _block(px, py, pc):
    return 4 * px + 2 * py + pc


def _await(block, sem):
    pltpu.make_async_copy(block, block, sem).wait()


def _relay_route(x, y, c):
    came_from = ((1 - x) * (1 - c) + x * c, y * (1 - c) + (1 - y) * c)
    goes_to = (x * (1 - c) + (1 - x) * c, (1 - y) * (1 - c) + y * c)
    return came_from, goes_to


def _gather_start(name, shards, groups, relayed=(), after=()):
    n, ng = len(shards), len(groups)
    lands = [lax.empty((N_DEV, *a.shape), a.dtype) for a in shards]

    def body(*refs):
        src, land = refs[:n], refs[n:2 * n]
        sems, token = refs[2 * n + len(after):2 * n + len(after) + 2 * ng], refs[-1]
        x, y, c, chips = _place()
        targets = [(x, y, 1 - c)] + [(*chip, c) for chip in chips]
        for gi, group in enumerate(groups):
            for i, w in enumerate(group):
                for k, to in enumerate(targets[:3] if gi in relayed else targets):
                    pltpu.make_async_remote_copy(
                        src_ref=src[w], dst_ref=land[w].at[_block(x, y, c)],
                        send_sem=sems[2 * gi].at[4 * i + k], recv_sem=sems[2 * gi + 1].at[4 * i + k],
                        device_id=to, device_id_type=MESH).start()
        token[...] = jnp.zeros_like(token)

    sem_shapes = [pltpu.SemaphoreType.DMA((4 * len(g),)) for g in groups for _ in range(2)]
    out = pl.pallas_call(
        body, name=name,
        in_specs=[HBM] * (2 * n) + [ANY] * len(after),
        out_specs=[SEM] * (2 * ng) + [HBM] * (2 * n) + [IN_VMEM],
        out_shape=sem_shapes + [_hbm_like(a) for a in shards] + [_hbm_like(a) for a in lands]
        + [jax.ShapeDtypeStruct((SUBLANE, LANE), F32)],
        input_output_aliases={i: 2 * ng + i for i in range(2 * n)},
        compiler_params=SPLIT,
    )(*[_in_hbm(a) for a in shards], *[_in_hbm(a) for a in lands], *after)
    sems = [(out[2 * gi], out[2 * gi + 1]) for gi in range(ng)]
    return sems, out[2 * ng:2 * ng + n], out[2 * ng + n:2 * ng + 2 * n], out[-1]


def _gather_forward(name, shards, lands, send1, recv1, after, relayed=False):
    n = len(lands)

    def body(*refs):
        src, land = refs[:n], refs[n:2 * n]
        s1, r1 = refs[2 * n], refs[2 * n + 1]
        s2, r2 = refs[2 * n + 2 + len(after)], refs[2 * n + 3 + len(after)]
        x, y, c, chips = _place()
        me, sibling = (x, y, c), (x, y, 1 - c)
        for j, chip in enumerate(chips[:2] if relayed else chips):
            for i in range(n):
                blk = land[i].at[_block(*chip, c)]
                pltpu.make_async_remote_copy(src_ref=blk, dst_ref=blk, send_sem=s1.at[4 * i + 1 + j], recv_sem=r1.at[4 * i + 1 + j],
                                             device_id=me, device_id_type=MESH).wait_recv()
                pltpu.make_async_remote_copy(src_ref=blk, dst_ref=blk, send_sem=s2.at[3 * i + j], recv_sem=r2.at[3 * i + j],
                                             device_id=sibling, device_id_type=MESH).start()
        if relayed:
            came_from, goes_to = _relay_route(x, y, c)
            for i in range(n):
                blk = land[i].at[_block(*came_from, c)]
                pltpu.make_async_remote_copy(src_ref=blk, dst_ref=blk, send_sem=s2.at[3 * i + 2], recv_sem=r2.at[3 * i + 2],
                                             device_id=(*goes_to, c), device_id_type=MESH).start()
        for i in range(n):
            blk = land[i].at[_block(x, y, 1 - c)]
            pltpu.make_async_remote_copy(src_ref=blk, dst_ref=blk, send_sem=s1.at[4 * i], recv_sem=r1.at[4 * i],
                                         device_id=me, device_id_type=MESH).wait_recv()
            for k in range(3 if relayed else 4):
                pltpu.make_async_remote_copy(src_ref=src[i], dst_ref=land[i].at[_block(x, y, c)], send_sem=s1.at[4 * i + k],
                                             recv_sem=r1.at[4 * i + k], device_id=sibling, device_id_type=MESH).wait_send()

    sem = pltpu.SemaphoreType.DMA((3 * n,))
    out = pl.pallas_call(
        body, name=name,
        in_specs=[HBM] * (2 * n) + [SEM, SEM] + [ANY] * len(after),
        out_specs=[SEM, SEM] + [HBM] * n,
        out_shape=[sem, sem] + [_hbm_like(a) for a in lands],
        input_output_aliases={n + i: 2 + i for i in range(n)},
        compiler_params=SPLIT,
    )(*shards, *lands, send1, recv1, *after)
    return (out[0], out[1]), out[2:]


def _gather_relay_forward(name, lands, send2, recv2, after):
    n = len(lands)

    def body(*refs):
        land, s2, r2 = refs[:n], refs[n], refs[n + 1]
        s3, r3 = refs[n + 2 + len(after)], refs[n + 3 + len(after)]
        x, y, c, _ = _place()
        me, sibling = (x, y, c), (x, y, 1 - c)
        came_from, _ = _relay_route(x, y, c)
        for i in range(n):
            blk = land[i].at[_block(1 - x, 1 - y, c)]
            pltpu.make_async_remote_copy(src_ref=blk, dst_ref=blk, send_sem=s2.at[3 * i + 2], recv_sem=r2.at[3 * i + 2],
                                         device_id=me, device_id_type=MESH).wait_recv()
            pltpu.make_async_remote_copy(src_ref=blk, dst_ref=blk, send_sem=s3.at[i], recv_sem=r3.at[i],
                                         device_id=sibling, device_id_type=MESH).start()
            sent = land[i].at[_block(*came_from, c)]
            pltpu.make_async_remote_copy(src_ref=sent, dst_ref=sent, send_sem=s2.at[3 * i + 2], recv_sem=r2.at[3 * i + 2],
                                         device_id=me, device_id_type=MESH).wait_send()

    sem = pltpu.SemaphoreType.DMA((n,))
    out = pl.pallas_call(
        body, name=name,
        in_specs=[HBM] * n + [SEM, SEM] + [ANY] * len(after),
        out_specs=[SEM, SEM] + [HBM] * n,
        out_shape=[sem, sem] + [_hbm_like(a) for a in lands],
        input_output_aliases={i: 2 + i for i in range(n)},
        compiler_params=SPLIT,
    )(*lands, send2, recv2, *after)
    return (out[0], out[1]), out[2:]


def _gather_wait(name, lands, send2, recv2, after, relay_sems=None):
    n = len(lands)
    n_sems = 2 if relay_sems is None else 4

    def body(*refs):
        land, s2, r2 = refs[:n], refs[n], refs[n + 1]
        for i in range(n):
            for j in range(3 if relay_sems is None else 2):
                _await(land[i].at[0], r2.at[3 * i + j])
                _await(land[i].at[0], s2.at[3 * i + j])
            if relay_sems is not None:
                _await(land[i].at[0], refs[n + 3].at[i])
                _await(land[i].at[0], refs[n + 2].at[i])

    return pl.pallas_call(
        body, name=name,
        in_specs=[HBM] * n + [SEM] * n_sems + [ANY] * len(after), out_specs=[HBM] * n, out_shape=[_hbm_like(a) for a in lands],
        input_output_aliases={i: i for i in range(n)},
        compiler_params=SPLIT,
    )(*lands, send2, recv2, *(relay_sems or ()), *after)


def _pair_exchange(name, grads, shard_rows):
    n = len(grads)
    shapes = [(g.shape[1:] if r is None else (r, g.shape[1])) for g, r in zip(grads, shard_rows)]

    def body(*refs):
        ins, recv = refs[:n], refs[n:2 * n]
        send_sems, recv_sems = refs[2 * n:]
        x, y, c, _ = _place()
        sends = []
        for w in range(n):
            for k in range(N_CHIP):
                j, r = 2 * k + 1 - c, shard_rows[w]
                src = ins[w].at[j] if r is None else ins[w].at[pl.ds(pl.multiple_of(j * r, SUBLANE), r), :]
                sends.append(pltpu.make_async_remote_copy(
                    src_ref=src, dst_ref=recv[w].at[k],
                    send_sem=send_sems.at[w, k], recv_sem=recv_sems.at[w, k],
                    device_id=(x, y, 1 - c), device_id_type=MESH))
        for cp in sends:
            cp.start()
        for cp in sends:
            cp.wait()

    return pl.pallas_call(
        body, name=name,
        in_specs=[ANY] * n, out_specs=[ANY] * n,
        out_shape=[jax.ShapeDtypeStruct((N_CHIP, *shape), g.dtype) for g, shape in zip(grads, shapes)],
        scratch_shapes=[pltpu.SemaphoreType.DMA((n, N_CHIP))] * 2,
    )(*grads)


def _pair_sum_rows(name, grad, received, core):
    _, r, c = received.shape
    tc = _fit(c, 512)

    def body(core_ref, a_ref, b_ref, o_ref):
        o_ref[...] = (a_ref[...] + b_ref[...]).astype(o_ref.dtype)

    spec = pl.BlockSpec((None, r, tc), lambda k, i, core_ref: (k, 0, i))
    return pl.pallas_call(
        body, name=name,
        grid_spec=pltpu.PrefetchScalarGridSpec(
            num_scalar_prefetch=1, grid=(N_CHIP, c // tc),
            in_specs=[pl.BlockSpec((r, tc), lambda k, i, core_ref: (2 * k + core_ref[0], i)), spec],
            out_specs=spec),
        out_shape=jax.ShapeDtypeStruct(received.shape, BF16),
        compiler_params=_params("parallel", "parallel"),
    )(core, grad, received)


def _pair_sum(name, grad, received, core):
    _, r, c = received.shape
    rows = min(ROWS, r)
    assert r % rows == 0

    def body(core_ref, a_ref, b_ref, o_ref):
        o_ref[...] = (a_ref[...].astype(F32) + b_ref[...].astype(F32)).astype(o_ref.dtype)

    spec = pl.BlockSpec((None, rows, c), lambda k, i, core_ref: (k, i, 0))
    return pl.pallas_call(
        body, name=name,
        grid_spec=pltpu.PrefetchScalarGridSpec(
            num_scalar_prefetch=1, grid=(N_CHIP, r // rows),
            in_specs=[pl.BlockSpec((None, None, rows, c), lambda k, i, core_ref: (k, core_ref[0], i, 0)), spec],
            out_specs=spec),
        out_shape=jax.ShapeDtypeStruct(received.shape, received.dtype),
        compiler_params=_params("parallel", "parallel"),
    )(core, grad.reshape(N_CHIP, 2, r, c), received)


def _away_shard(src, k, c, shard_rows):
    if shard_rows is None:
        return src.at[k]
    return src.at[pl.ds(pl.multiple_of((2 * k + 1 - c) * shard_rows, SUBLANE), shard_rows), :]


def _pair_send_start(name, away, shard_rows=None):
    shape = away.shape if shard_rows is None else (N_CHIP, shard_rows, away.shape[1])
    land = lax.empty(shape, away.dtype)

    def body(src, dst, send, recv, src_thru, dst_thru, token):
        x, y, c, _ = _place()
        for k in range(N_CHIP):
            pltpu.make_async_remote_copy(src_ref=_away_shard(src, k, c, shard_rows), dst_ref=dst.at[k], send_sem=send.at[k],
                                         recv_sem=recv.at[k], device_id=(x, y, 1 - c), device_id_type=MESH).start()
        token[...] = jnp.zeros_like(token)

    sem = pltpu.SemaphoreType.DMA((N_CHIP,))
    out = pl.pallas_call(
        body, name=name,
        in_specs=[HBM, HBM], out_specs=[SEM, SEM, HBM, HBM, IN_VMEM],
        out_shape=[sem, sem, _hbm_like(away), _hbm_like(land), jax.ShapeDtypeStruct((SUBLANE, LANE), F32)],
        input_output_aliases={0: 2, 1: 3},
        compiler_params=SPLIT,
    )(_in_hbm(away), _in_hbm(land))
    return (out[0], out[1]), out[2], out[3], out[4]


def _pair_send_wait(name, sems, src, land, after, shard_rows=None):
    def body(src_ref, dst_ref, send, recv, *rest):
        for k in range(N_CHIP):
            _await(dst_ref.at[k], send.at[k])
            _await(dst_ref.at[k], recv.at[k])

    return pl.pallas_call(
        body, name=name,
        in_specs=[HBM, HBM, SEM, SEM] + [ANY] * len(after), out_specs=HBM, out_shape=_hbm_like(land),
        input_output_aliases={1: 0},
        compiler_params=SPLIT,
    )(src, land, *sems, *after)


def _chip_send_start(name, sums):
    n = len(sums)
    lands = [lax.empty(a.shape, a.dtype) for a in sums]

    def body(*refs):
        src, land = refs[:n], refs[n:2 * n]
        send, recv, token = refs[2 * n], refs[2 * n + 1], refs[-1]
        x, y, c, chips = _place()
        for w in range(n):
            for j, (px, py) in enumerate(chips):
                pltpu.make_async_remote_copy(
                    src_ref=src[w].at[2 * px + py], dst_ref=land[w].at[2 * x + y],
                    send_sem=send.at[3 * w + j], recv_sem=recv.at[3 * w + j],
                    device_id=(px, py, c), device_id_type=MESH).start()
        token[...] = jnp.zeros_like(token)

    sem = pltpu.SemaphoreType.DMA((3 * n,))
    out = pl.pallas_call(
        body, name=name,
        in_specs=[HBM] * (2 * n),
        out_specs=[SEM, SEM] + [HBM] * (2 * n) + [IN_VMEM],
        out_shape=[sem, sem] + [_hbm_like(a) for a in sums] + [_hbm_like(a) for a in lands]
        + [jax.ShapeDtypeStruct((SUBLANE, LANE), F32)],
        input_output_aliases={i: 2 + i for i in range(2 * n)},
        compiler_params=SPLIT,
    )(*[_in_hbm(a) for a in sums], *[_in_hbm(a) for a in lands])
    return (out[0], out[1]), out[2:2 + n], out[2 + n:2 + 2 * n], out[-1]


def _chip_send_wait(name, groups, after):
    counts = [len(g[1]) for g in groups]
    n = sum(counts)

    def body(*refs):
        land = refs[n:2 * n]
        sems = refs[2 * n:2 * n + 2 * len(groups)]
        w = 0
        for gi, count in enumerate(counts):
            for i in range(count):
                for j in range(3):
                    _await(land[w].at[0], sems[2 * gi].at[3 * i + j])
                    _await(land[w].at[0], sems[2 * gi + 1].at[3 * i + j])
                w += 1

    sums = [a for g in groups for a in g[1]]
    lands = [a for g in groups for a in g[2]]
    sems = [s for g in groups for s in g[0]]
    return pl.pallas_call(
        body, name=name,
        in_specs=[HBM] * (2 * n) + [SEM] * len(sems) + [ANY] * len(after),
        out_specs=[HBM] * n, out_shape=[_hbm_like(a) for a in lands],
        input_output_aliases={n + i: i for i in range(n)},
        compiler_params=SPLIT,
    )(*sums, *lands, *sems, *after)


def _small_all_reduce(part, after=()):
    _, w = part.shape

    def body(p_ref, *rest):
        o_ref, buf, send_sems, recv_sems = rest[len(after):]
        x, y, c, _ = _place()
        me = 4 * x + 2 * y + c
        buf[me] = jnp.sum(p_ref[...], axis=0, keepdims=True)
        copies = []
        for k in range(1, N_DEV):
            dx, dy, dc = (k >> 2) & 1, (k >> 1) & 1, k & 1
            copies.append(pltpu.make_async_remote_copy(
                src_ref=buf.at[me], dst_ref=buf.at[me], send_sem=send_sems.at[k - 1], recv_sem=recv_sems.at[k - 1],
                device_id=(x ^ dx, y ^ dy, c ^ dc), device_id_type=MESH))
        for cp in copies:
            cp.start()
        for cp in copies:
            cp.wait()
        tot = buf[0]
        for d in range(1, N_DEV):
            tot = tot + buf[d]
        o_ref[...] = tot
        loss = jnp.sum(tot[:, w - LANE:], axis=1, keepdims=True)
        o_ref[:, w - LANE:] = jnp.broadcast_to(loss, (1, LANE))

    return pl.pallas_call(
        body, name="small_all_reduce",
        in_specs=[IN_VMEM] + [ANY] * len(after), out_specs=IN_VMEM,
        out_shape=jax.ShapeDtypeStruct((1, w), F32),
        scratch_shapes=[pltpu.VMEM((N_DEV, 1, w), F32), pltpu.SemaphoreType.DMA((N_DEV - 1,)), pltpu.SemaphoreType.DMA((N_DEV - 1,))],
        compiler_params=pltpu.CompilerParams(vmem_limit_bytes=VMEM_LIMIT_BYTES),
    )(part, *after)


def _adamw(w, g, m, v):
    m = ADAM_B1 * m + (1.0 - ADAM_B1) * g
    v = ADAM_B2 * v + (1.0 - ADAM_B2) * (g * g)
    m_hat = m / (1.0 - ADAM_B1 ** ADAM_STEP)
    v_hat = v / (1.0 - ADAM_B2 ** ADAM_STEP)
    delta = -ADAM_LR * (m_hat / (jnp.sqrt(v_hat) + ADAM_EPS) + ADAM_WD * w)
    return delta, m, v


def _sum_adam_block(chip_ref, p_ref, own_ref, w_ref, m_ref, v_ref, g_ref, d_ref, mo_ref, vo_ref):
    g = None
    for k in range(N_CHIP):
        term = jnp.where(chip_ref[0] == k, own_ref[...], p_ref[k]).astype(F32)
        g = term if g is None else g + term
    g_ref[...] = g
    d_ref[...], mo_ref[...], vo_ref[...] = _adamw(w_ref[...], g, m_ref[...], v_ref[...])


def _sum_adam(name, parts, sums, chip, w, m, v, after=()):
    _, r, c = w.shape
    n_after = len(after)
    by_rows = r % ROWS == 0 or r < ROWS
    tr, tc = (min(ROWS, r), c) if by_rows else (r, _fit(c, 512))
    at = (lambda i: (i, 0)) if by_rows else (lambda i: (0, i))

    def body(chip_ref, p_ref, own_ref, w_ref, m_ref, v_ref, *rest):
        _sum_adam_block(chip_ref, p_ref, own_ref, w_ref, m_ref, v_ref, *rest[n_after:])

    blk = pl.BlockSpec((None, tr, tc), lambda i, chip_ref: (0, *at(i)))
    out = jax.ShapeDtypeStruct((1, r, c), F32)
    return pl.pallas_call(
        body, name=name,
        grid_spec=pltpu.PrefetchScalarGridSpec(
            num_scalar_prefetch=1, grid=(r // tr if by_rows else c // tc,),
            in_specs=[pl.BlockSpec((N_CHIP, tr, tc), lambda i, chip_ref: (0, *at(i))),
                      pl.BlockSpec((None, tr, tc), lambda i, chip_ref: (chip_ref[0], *at(i))), blk, blk, blk]
            + [ANY] * n_after,
            out_specs=[blk] * 4),
        out_shape=[out] * 4,
        compiler_params=_params("parallel"),
    )(chip, parts, sums, w, m, v, *after)


def _adam_gains(total, ws, ms, vs):
    n = len(ws)
    widths = [w.shape[1] for w in ws]

    def body(t_ref, *refs):
        w_refs, m_refs, v_refs, outs = refs[:n], refs[n:2 * n], refs[2 * n:3 * n], refs[3 * n:]
        off = 0
        for i in range(n):
            g = t_ref[:, off:off + widths[i]]
            off += widths[i]
            g_ref, d_ref, mo_ref, vo_ref = outs[4 * i:4 * i + 4]
            g_ref[...] = g
            d_ref[...], mo_ref[...], vo_ref[...] = _adamw(w_refs[i][...], g, m_refs[i][...], v_refs[i][...])

    out = pl.pallas_call(
        body, name="adam_gains",
        out_shape=[jax.ShapeDtypeStruct(w.shape, F32) for w in ws for _ in range(4)],
    )(total, *ws, *ms, *vs)
    return [tuple(out[4 * i:4 * i + 4]) for i in range(n)]


def _adam_taps(total, first_col, device, w, m, v):
    _, n_taps, cw = w.shape
    col_block = lambda t, dev: (0, first_col // cw + t * N_DEV + dev[0])
    tap = pl.BlockSpec((None, 1, cw), lambda t, dev: (t, 0, 0))

    def body(dev_ref, t_ref, w_ref, m_ref, v_ref, g_ref, d_ref, mo_ref, vo_ref):
        g = t_ref[...]
        g_ref[...] = g
        d_ref[...], mo_ref[...], vo_ref[...] = _adamw(w_ref[...], g, m_ref[...], v_ref[...])

    shape3 = (n_taps, 1, cw)
    out = pl.pallas_call(
        body, name="adam_taps",
        grid_spec=pltpu.PrefetchScalarGridSpec(
            num_scalar_prefetch=1, grid=(n_taps,),
            in_specs=[pl.BlockSpec((1, cw), col_block), tap, tap, tap], out_specs=[tap] * 4),
        out_shape=[jax.ShapeDtypeStruct(shape3, F32)] * 4,
    )(device, total, w.reshape(shape3), m.reshape(shape3), v.reshape(shape3))
    return tuple(o.reshape(w.shape) for o in out)


def kernel(x, pre_mix_g, w_in, conv_w, q_norm_g, w_uq, kv_norm_g, w_ukv, conv_out_g, attn_out_g, w_o, post_mix_g, pre_mlp_g, w_up, w_down, post_mlp_g, loss_target, m_pre_mix_g, m_w_in, m_conv_w, m_q_norm_g, m_w_uq, m_kv_norm_g, m_w_ukv, m_conv_out_g, m_attn_out_g, m_w_o, m_post_mix_g, m_pre_mlp_g, m_w_up, m_w_down, m_post_mlp_g, v_pre_mix_g, v_w_in, v_conv_w, v_q_norm_g, v_w_uq, v_kv_norm_g, v_w_ukv, v_conv_out_g, v_attn_out_g, v_w_o, v_post_mix_g, v_pre_mlp_g, v_w_up, v_w_down, v_post_mlp_g):
    me = 4 * lax.axis_index("x") + 2 * lax.axis_index("y") + lax.axis_index("c")
    core = lax.axis_index("c").astype(jnp.int32).reshape(1)
    chip = (2 * lax.axis_index("x") + lax.axis_index("y")).astype(jnp.int32).reshape(1)
    gains = (pre_mix_g, q_norm_g, kv_norm_g, conv_out_g, attn_out_g, post_mix_g, pre_mlp_g, post_mlp_g)
    gain_m = (m_pre_mix_g, m_q_norm_g, m_kv_norm_g, m_conv_out_g, m_attn_out_g, m_post_mix_g, m_pre_mlp_g, m_post_mlp_g)
    gain_v = (v_pre_mix_g, v_q_norm_g, v_kv_norm_g, v_conv_out_g, v_attn_out_g, v_post_mix_g, v_pre_mlp_g, v_post_mlp_g)
    names = ("w_in", "w_uq", "w_ukv", "w_o", "w_up", "w_down")
    big = dict(zip(names, (w_in, w_uq, w_ukv, w_o, w_up, w_down)))
    big_m = dict(zip(names, (m_w_in, m_w_uq, m_w_ukv, m_w_o, m_w_up, m_w_down)))
    big_v = dict(zip(names, (v_w_in, v_w_uq, v_w_ukv, v_w_o, v_w_up, v_w_down)))
    n_heads = attn_out_g.shape[1] // HEAD
    n_taps = conv_w.shape[1]

    gathered = ("w_in", "conv", "w_uq", "w_ukv", "w_o", "w_up", "w_down")
    gather_groups = ((0, 1), (2, 3, 4), (5,), (6,))
    taps = jnp.pad(conv_w[0], ((0, SUBLANE - n_taps), (0, 0)))
    relayed_groups = (0, 2, 3)
    sems1, shards, lands, token = _gather_start("gather_start_first", [w_in[0].astype(BF16), taps], ((0, 1),), relayed=(0,))
    sems1, shards, lands = list(sems1), list(shards), list(lands)
    behind = token[0, 0]
    rest = [(big[nm][0] + behind).astype(BF16) for nm in gathered[2:]]

    def start_more(name, some, groups, relayed, after):
        sems_b, shards_b, lands_b, started = _gather_start(name, some, groups, relayed=relayed, after=after)
        sems1.extend(sems_b)
        shards.extend(shards_b)
        lands.extend(lands_b)
        return started

    start_rest = lambda after: start_more("gather_start_rest", rest[:4], ((0, 1, 2), (3,)), (1,), after)
    start_last = lambda after: start_more("gather_start_last", rest[4:], ((0,),), (0,), after)

    cols = lambda a: jnp.concatenate([a[j] for j in range(N_DEV)], axis=1)
    rows = lambda a: a.reshape(N_DEV * a.shape[1], a.shape[2])
    device = me.astype(jnp.int32).reshape(1)
    own_in = lambda a, shard: lax.dynamic_update_index_in_dim(a, shard, me, 0)
    q_pieces = [(h, 0, HEAD) for h in range(n_heads)] + [(h, HEAD, QK) for h in range(n_heads)]
    ready = {
        "w_in": lambda a, shard: _join_col_shards("join_w_in", a, shard, device),
        "conv": lambda a, shard: cols(own_in(a, shard))[:n_taps],
        "w_uq": lambda a, shard: _join_col_shards("join_w_uq", a, shard, device, q_pieces),
        "w_ukv": lambda a, shard: cols(own_in(a, shard)),
        "w_o": lambda a, shard: rows(own_in(a, shard)),
        "w_up": own_in,
        "w_down": lambda a, shard: rows(own_in(a, shard)),
    }
    assert w_uq.shape[2] == QK

    class Weights:
        def __init__(self):
            self.passed, self.relayed = {}, {}

        def forward(self, group, after):
            idx = gather_groups[group]
            if group == 0:
                after = (*after, *rest)
            self.passed[group] = _gather_forward(f"gather_forward_{group}", [shards[i] for i in idx], [lands[i] for i in idx],
                                                 *sems1[group], after, relayed=group in relayed_groups)
            return tuple(self.passed[group][1])

        def start(self, group, after):
            assert group == len(gather_groups) - 1
            return (start_last(after),)

        def relay(self, group, after):
            sems2, mid = self.passed[group]
            self.relayed[group], mid = _gather_relay_forward(f"gather_relay_{group}", mid, *sems2, after)
            self.passed[group] = (sems2, mid)
            if group == 0:
                start_rest(tuple(mid))
            return tuple(mid)

        def ready(self, group, after):
            sems2, mid = self.passed[group]
            full = _gather_wait(f"gather_wait_{group}", mid, *sems2, after, relay_sems=self.relayed.get(group))
            out = []
            return [ready[gathered[i]](a, shards[i]) for i, a in zip(gather_groups[group], full)]

    weights = Weights()

    col_blocks = lambda g: g.reshape(g.shape[0], N_DEV, g.shape[1] // N_DEV).transpose(1, 0, 2)
    row_blocks = lambda g: g.reshape(N_DEV, g.shape[0] // N_DEV, g.shape[1])
    grad_groups = (("w_down",), ("w_up",), ("w_o", "w_uq", "w_ukv"), ("w_in",))
    transposed = {"w_in": w_in.shape[2], "w_uq": w_uq.shape[2]}
    to_blocks = {
        "w_in": lambda g: g, "w_uq": lambda g: _unpermute_q_rows(g, n_heads),
        "w_ukv": col_blocks, "w_o": row_blocks, "w_up": lambda g: g, "w_down": row_blocks,
    }
    in_flight = []

    class Grads:
        def __init__(self):
            self.core = core
            self.away = {}

        def send_sums(self, group, sums):
            sems, sums, parts, tok = _chip_send_start(f"chip_send_start_{group}", list(sums))
            in_flight.append((sems, sums, parts))
            return (tok,)

        def full(self, group, arrays, received=None):
            nms = grad_groups[group]
            if received is None:
                blocks = [to_blocks[nm](g) for nm, g in zip(nms, arrays)]
                got = _pair_exchange(f"pair_exchange_{group}", blocks, [transposed.get(nm) for nm in nms])
            else:
                blocks, got = [self.away[group][1]], [self.received(group, received)]
            sums = [(_pair_sum_rows if nm in transposed else _pair_sum)(f"pair_sum_{nm}", g, r, core)
                    for nm, g, r in zip(nms, blocks, got)]
            return self.send_sums(group, sums)

        def send_away(self, group, half):
            nm = grad_groups[group][0]
            rows = transposed.get(nm)
            sems, src, land, tok = _pair_send_start(f"pair_send_start_{group}", half if rows is None else to_blocks[nm](half), rows)
            self.away[group] = (sems, src, land, rows)
            return (tok,)

        def received(self, group, after):
            sems, src, land, rows = self.away[group]
            return _pair_send_wait(f"pair_send_wait_{group}", sems, src, land, after, rows)

        def update_now(self, group, after):
            return update(str(group), group, group + 1, after)

    big_out = {}

    def update(tag, first, last, after):
        picked = [i for i in range(first, last) if grad_groups[i][0] not in big_out]
        groups = [in_flight[i] for i in picked]
        parts = _chip_send_wait("chip_send_wait_" + tag, groups, after)
        nms = [nm for i in picked for nm in grad_groups[i]]
        sums = [a for _, s, _ in groups for a in s]
        for nm, p, s in zip(nms, parts, sums):
            view = (lambda a: jnp.swapaxes(a, 1, 2)) if nm in transposed else (lambda a: a)
            out = _sum_adam("adam_" + nm, p, s, chip, view(big[nm]), view(big_m[nm]), view(big_v[nm]), after=after)
            after = (out[0],)
            big_out[nm] = [view(o) for o in out]
        return after

    grad_x, small = _local_step(x[0], loss_target[0], gains, weights, Grads(), first_after=(token,))

    after = update("early", 0, len(in_flight) - 1, (grad_x,))
    total = _small_all_reduce(small, after=after)
    update("late", len(in_flight) - 1, len(in_flight), (total,))
    big_out = [big_out[nm] for nm in names]

    gain_out = _adam_gains(total, gains, gain_m, gain_v)
    taps_out = _adam_taps(total, sum(g.shape[1] for g in gains), me.astype(jnp.int32).reshape(1), conv_w, m_conv_w, v_conv_w)
    loss = total[0, total.shape[1] - 1]

    order = (0, "w_in", "conv", 1, "w_uq", 2, "w_ukv", 3, 4, "w_o", 5, 6, "w_up", "w_down", 7)
    by_name = dict(zip(names, big_out))
    outs = [loss, grad_x[None]]
    for kind in range(4):
        for item in order:
            if item == "conv":
                outs.append(taps_out[kind])
            elif isinstance(item, int):
                outs.append(gain_out[item][kind])
            else:
                outs.append(by_name[item][kind])
    return tuple(outs)
```

```python
import math

import jax
import jax.numpy as jnp
from jax import lax
from jax.experimental import pallas as pl
from jax.experimental.pallas import tpu as pltpu

F32 = jnp.float32
BF16 = jnp.bfloat16

EPS = 1e-6
NEG_INF = -1e30
HEAD = 128
ROPE = 64
QK = HEAD + ROPE
CHUNK = 64
ROPE_THETA = 10000.0
ADAM_LR, ADAM_B1, ADAM_B2, ADAM_EPS, ADAM_WD, ADAM_STEP = 0.001, 0.9, 0.999, 1e-08, 0.01, 10

LANE = 128
SUBLANE = 8
VMEM_LIMIT_BYTES = 56 * 1024 * 1024

N_DEV = 8
N_CHIP = 4
MESH = pl.DeviceIdType.MESH


def _params(*sem):
    return pltpu.CompilerParams(dimension_semantics=sem, vmem_limit_bytes=VMEM_LIMIT_BYTES)


ANY = pl.BlockSpec(memory_space=pl.ANY)


def _call(body, *, in_specs, after=(), **kw):
    n_in, n_after = len(in_specs), len(after)

    def ordered(*refs):
        body(*refs[:n_in], *refs[n_in + n_after:])

    call = pl.pallas_call(ordered, in_specs=[*in_specs, *[ANY] * n_after], **kw)
    return lambda *operands: call(*operands, *after)


def _sublane_sum(v):
    r, w = v.shape
    return jnp.sum(v.reshape(r // SUBLANE, SUBLANE, w), axis=0)


def _rstd(x):
    return lax.rsqrt(jnp.mean(x * x, axis=-1, keepdims=True) + EPS)


def _rms_bwd(x, g, dy):
    r = _rstd(x)
    xh = x * r
    dxh = dy * g
    dx = r * (dxh - xh * jnp.mean(dxh * xh, axis=-1, keepdims=True))
    return dx, dy * xh


def _accumulate(ref, val, step):
    @pl.when(step == 0)
    def _():
        ref[...] = val

    @pl.when(step > 0)
    def _():
        ref[...] += val


NN = ((1,), (0,))
NT = ((1,), (1,))
TN = ((0,), (0,))


def _matmul(name, a, b, *, grid, a_spec, b_spec, out_shape, out_specs, contract, nk=1, acc_shape=None,
            extras=(), extra_specs=(), epilogue=None, after=()):
    multi = isinstance(out_shape, (tuple, list))
    out_shapes = tuple(out_shape) if multi else (out_shape,)
    n_out = len(out_shapes)
    n_extra = len(extras)

    def body(a_ref, b_ref, *rest):
        x_refs = rest[:n_extra]
        o_refs = rest[n_extra:n_extra + n_out]

        def emit(acc):
            vals = epilogue(acc, *[r[...] for r in x_refs]) if epilogue else (acc,)
            for r, v in zip(o_refs, vals):
                r[...] = v.astype(r.dtype)

        p = lax.dot_general(a_ref[...], b_ref[...], (contract, ((), ())), preferred_element_type=F32)
        if nk == 1:
            emit(p)
        else:
            acc_ref = rest[n_extra + n_out]
            k = pl.program_id(2)
            _accumulate(acc_ref, p, k)

            @pl.when(k == nk - 1)
            def _():
                emit(acc_ref[...])

    sem = ("parallel", "parallel") + (("arbitrary",) if nk > 1 else ())
    return _call(
        body, name=name, grid=grid, after=after,
        in_specs=[a_spec, b_spec, *extra_specs],
        out_specs=out_specs,
        out_shape=out_shape,
        scratch_shapes=[pltpu.VMEM(acc_shape, F32)] if nk > 1 else [],
        compiler_params=_params(*sem),
    )(a, b, *extras)


def _fit(n, tile):
    if n <= tile:
        return n
    t = tile - tile % LANE
    while n % t:
        t -= LANE
    return t


def _mm_nn(name, a, b, out_dtype, tm, tn, after=()):
    m, k = a.shape
    n = b.shape[1]
    tm, tn = _fit(m, tm), _fit(n, tn)
    return _matmul(name, a, b, grid=(m // tm, n // tn), after=after,
                   a_spec=pl.BlockSpec((tm, k), lambda i, j: (i, 0)),
                   b_spec=pl.BlockSpec((k, tn), lambda i, j: (0, j)),
                   out_shape=jax.ShapeDtypeStruct((m, n), out_dtype),
                   out_specs=pl.BlockSpec((tm, tn), lambda i, j: (i, j)), contract=NN)


def _mm_nt(name, a, b, out_dtype, tm, tn, after=()):
    m, k = a.shape
    n = b.shape[0]
    tm, tn = _fit(m, tm), _fit(n, tn)
    return _matmul(name, a, b, grid=(m // tm, n // tn), after=after,
                   a_spec=pl.BlockSpec((tm, k), lambda i, j: (i, 0)),
                   b_spec=pl.BlockSpec((tn, k), lambda i, j: (j, 0)),
                   out_shape=jax.ShapeDtypeStruct((m, n), out_dtype),
                   out_specs=pl.BlockSpec((tm, tn), lambda i, j: (i, j)), contract=NT)


def _mm_tn(name, a, b, out_dtype, tm, tn):
    s, m = a.shape
    n = b.shape[1]
    tm, tn = _fit(m, tm), _fit(n, tn)
    return _matmul(name, a, b, grid=(m // tm, n // tn),
                   a_spec=pl.BlockSpec((s, tm), lambda i, j: (0, i)),
                   b_spec=pl.BlockSpec((s, tn), lambda i, j: (0, j)),
                   out_shape=jax.ShapeDtypeStruct((m, n), out_dtype),
                   out_specs=pl.BlockSpec((tm, tn), lambda i, j: (i, j)), contract=TN)


ROWS = 256


def _row_spec(rows, width):
    return pl.BlockSpec((rows, width), lambda i: (i, 0))


def _fixed_spec(rows, width):
    return pl.BlockSpec((rows, width), lambda i: (0, 0))


def _column_pieces(rows, start, width):
    piece = math.gcd(start, width)
    assert piece % LANE == 0
    return [pl.BlockSpec((rows, piece), lambda i, b=start // piece + p: (i, b)) for p in range(width // piece)]


def _rms_fwd(name, x, g, after=()):
    s, w = x.shape
    rows = min(ROWS, s)

    def body(x_ref, g_ref, o_ref):
        xv = x_ref[...]
        o_ref[...] = (xv * _rstd(xv) * g_ref[...]).astype(o_ref.dtype)

    return _call(
        body, name=name, grid=(s // rows,), after=after,
        in_specs=[_row_spec(rows, w), _fixed_spec(1, w)],
        out_specs=_row_spec(rows, w),
        out_shape=jax.ShapeDtypeStruct((s, w), BF16),
        compiler_params=_params("parallel"),
    )(x, g)


def _norm_up(name, x, cols, g, w, after=()):
    s = x.shape[0]
    start, width = cols
    n = w.shape[1]
    tm = min(TILE_M, s)
    pieces = _column_pieces(tm, start, width)
    n_p = len(pieces)

    def body(*refs):
        g_ref, w_ref, xn_ref, o_ref = refs[n_p:]
        xv = refs[0][...] if n_p == 1 else jnp.concatenate([r[...] for r in refs[:n_p]], axis=1)
        xn = (xv * _rstd(xv) * g_ref[...]).astype(BF16)
        xn_ref[...] = xn
        o_ref[...] = jnp.dot(xn, w_ref[...], preferred_element_type=F32)

    return _call(
        body, name=name, grid=(s // tm,), after=after,
        in_specs=[*pieces, _fixed_spec(1, width), _fixed_spec(width, n)],
        out_specs=[_row_spec(tm, width), _row_spec(tm, n)],
        out_shape=[jax.ShapeDtypeStruct((s, width), BF16), jax.ShapeDtypeStruct((s, n), F32)],
        compiler_params=_params("parallel"),
    )(*[x] * n_p, g, w)


def _up_norm_bwd(name, dy, w, x, cols, g, after=()):
    s, n = dy.shape
    start, width = cols
    tm = min(TILE_M, s)
    pieces = _column_pieces(tm, start, width)
    n_p = len(pieces)

    def body(dy_ref, w_ref, *refs):
        g_ref, dx_ref, dg_ref = refs[n_p:]
        xv = refs[0][...] if n_p == 1 else jnp.concatenate([r[...] for r in refs[:n_p]], axis=1)
        dxn = lax.dot_general(dy_ref[...], w_ref[...], (NT, ((), ())), preferred_element_type=F32)
        dx, dgc = _rms_bwd(xv, g_ref[...], dxn)
        dx_ref[...] = dx.astype(dx_ref.dtype)
        _accumulate(dg_ref, _sublane_sum(dgc), pl.program_id(0))

    return _call(
        body, name=name, grid=(s // tm,), after=after,
        in_specs=[_row_spec(tm, n), _fixed_spec(width, n), *pieces, _fixed_spec(1, width)],
        out_specs=[_row_spec(tm, width), _fixed_spec(SUBLANE, width)],
        out_shape=[jax.ShapeDtypeStruct((s, width), BF16), jax.ShapeDtypeStruct((SUBLANE, width), F32)],
        compiler_params=_params("arbitrary"),
    )(dy, w, *[x] * n_p, g)


def _mid_fwd(x, y, g_post, g_pre, after=()):
    s, w = x.shape
    rows = min(ROWS, s)

    def body(x_ref, y_ref, gp_ref, gq_ref, x2_ref, h2_ref):
        yv = y_ref[...]
        x2 = x_ref[...] + yv * _rstd(yv) * gp_ref[...]
        x2_ref[...] = x2
        h2_ref[...] = (x2 * _rstd(x2) * gq_ref[...]).astype(h2_ref.dtype)

    return _call(
        body, name="mid_fwd", grid=(s // rows,), after=after,
        in_specs=[_row_spec(rows, w), _row_spec(rows, w), _fixed_spec(1, w), _fixed_spec(1, w)],
        out_specs=[_row_spec(rows, w), _row_spec(rows, w)],
        out_shape=[jax.ShapeDtypeStruct((s, w), F32), jax.ShapeDtypeStruct((s, w), BF16)],
        compiler_params=_params("parallel"),
    )(x, y, g_post, g_pre)


def _head(m, x2, tgt, g):
    s, w = m.shape
    rows = min(ROWS, s)

    def body(m_ref, x2_ref, t_ref, g_ref, dout_ref, dm_ref, dg_ref, loss_ref):
        mv = m_ref[...]
        gv = g_ref[...]
        out = x2_ref[...] + mv * _rstd(mv) * gv
        err = out - t_ref[...]
        dout = err * (1.0 / w)
        dout_ref[...] = dout
        dm, dgc = _rms_bwd(mv, gv, dout)
        dm_ref[...] = dm.astype(dm_ref.dtype)
        sq = err * err
        lanes = sq[:, 0:LANE]
        for j in range(1, w // LANE):
            lanes = lanes + sq[:, j * LANE:(j + 1) * LANE]
        step = pl.program_id(0)
        _accumulate(dg_ref, _sublane_sum(dgc), step)
        _accumulate(loss_ref, _sublane_sum(lanes) * (0.5 / w), step)

    return pl.pallas_call(
        body, name="head", grid=(s // rows,),
        in_specs=[_row_spec(rows, w), _row_spec(rows, w), _row_spec(rows, w), _fixed_spec(1, w)],
        out_specs=[_row_spec(rows, w), _row_spec(rows, w), _fixed_spec(SUBLANE, w), _fixed_spec(SUBLANE, LANE)],
        out_shape=[jax.ShapeDtypeStruct((s, w), F32), jax.ShapeDtypeStruct((s, w), BF16),
                   jax.ShapeDtypeStruct((SUBLANE, w), F32), jax.ShapeDtypeStruct((SUBLANE, LANE), F32)],
        compiler_params=_params("arbitrary"),
    )(m, x2, tgt, g)


def _mid_bwd(x2, y, d_out, d_h2, g_pre, g_post, after=()):
    s, w = x2.shape
    rows = min(ROWS, s)

    def body(x2_ref, y_ref, dout_ref, dh2_ref, gq_ref, gp_ref, dx2_ref, dy_ref, dgq_ref, dgp_ref):
        dx, dgq = _rms_bwd(x2_ref[...], gq_ref[...], dh2_ref[...])
        dx2 = dout_ref[...] + dx
        dx2_ref[...] = dx2
        dy, dgp = _rms_bwd(y_ref[...], gp_ref[...], dx2)
        dy_ref[...] = dy.astype(dy_ref.dtype)
        step = pl.program_id(0)
        _accumulate(dgq_ref, _sublane_sum(dgq), step)
        _accumulate(dgp_ref, _sublane_sum(dgp), step)

    return _call(
        body, name="mid_bwd", grid=(s // rows,), after=after,
        in_specs=[_row_spec(rows, w)] * 4 + [_fixed_spec(1, w)] * 2,
        out_specs=[_row_spec(rows, w), _row_spec(rows, w), _fixed_spec(SUBLANE, w), _fixed_spec(SUBLANE, w)],
        out_shape=[jax.ShapeDtypeStruct((s, w), F32), jax.ShapeDtypeStruct((s, w), BF16),
                   jax.ShapeDtypeStruct((SUBLANE, w), F32), jax.ShapeDtypeStruct((SUBLANE, w), F32)],
        compiler_params=_params("arbitrary"),
    )(x2, y, d_out, d_h2, g_pre, g_post)


def _first_bwd(x, g, d_h1, d_x2, after=()):
    s, w = x.shape
    rows = min(ROWS, s)

    def body(x_ref, g_ref, dh_ref, dx2_ref, dx_ref, dg_ref):
        dx, dgc = _rms_bwd(x_ref[...], g_ref[...], dh_ref[...])
        dx_ref[...] = dx2_ref[...] + dx
        _accumulate(dg_ref, _sublane_sum(dgc), pl.program_id(0))

    return _call(
        body, name="first_bwd", grid=(s // rows,), after=after,
        in_specs=[_row_spec(rows, w), _fixed_spec(1, w), _row_spec(rows, w), _row_spec(rows, w)],
        out_specs=[_row_spec(rows, w), _fixed_spec(SUBLANE, w)],
        out_shape=[jax.ShapeDtypeStruct((s, w), F32), jax.ShapeDtypeStruct((SUBLANE, w), F32)],
        compiler_params=_params("arbitrary"),
    )(x, g, d_h1, d_x2)


def _shift_down(v, k):
    t = lax.broadcasted_iota(jnp.int32, v.shape, 0)
    return jnp.where(t >= k, pltpu.roll(v, k, 0), 0.0)


def _shift_up(v, k):
    n = v.shape[0]
    t = lax.broadcasted_iota(jnp.int32, v.shape, 0)
    return jnp.where(t < n - k, pltpu.roll(v, n - k, 0), 0.0)


def _conv_core(u, b, c, w):
    z = c * u
    conv = w[0:1, :] * _shift_down(z, 2) + w[1:2, :] * _shift_down(z, 1) + w[2:3, :] * z
    return z, conv, b * conv


def _conv_fwd(proj, conv_w, g, n_groups, out_width, after=()):
    s = proj.shape[0]

    def body(u_ref, b_ref, c_ref, w_ref, g_ref, o_ref):
        _, _, yr = _conv_core(u_ref[...], b_ref[...], c_ref[...], w_ref[...])
        o_ref[...] = (yr * _rstd(yr) * g_ref[...]).astype(o_ref.dtype)

    col = lambda k: pl.BlockSpec((s, HEAD), lambda i: (0, k * n_groups + i))
    return _call(
        body, name="conv_fwd", grid=(n_groups,), after=after,
        in_specs=[col(0), col(1), col(2), pl.BlockSpec((3, HEAD), lambda i: (0, i)), pl.BlockSpec((1, HEAD), lambda i: (0, i))],
        out_specs=pl.BlockSpec((s, HEAD), lambda i: (0, i)),
        out_shape=jax.ShapeDtypeStruct((s, out_width), BF16),
        compiler_params=_params("parallel"),
    )(proj, proj, proj, conv_w, g)


def _conv_bwd(proj, d_mix, conv_w, g, n_groups):
    s = proj.shape[0]
    width = n_groups * HEAD

    def body(u_ref, b_ref, c_ref, dy_ref, w_ref, g_ref, du_ref, db_ref, dc_ref, dg_ref, dw_ref):
        u, b, c, w = u_ref[...], b_ref[...], c_ref[...], w_ref[...]
        z, conv, yr = _conv_core(u, b, c, w)
        dyr, dgc = _rms_bwd(yr, g_ref[...], dy_ref[...])
        dconv = dyr * b
        db_ref[...] = (dyr * conv).astype(db_ref.dtype)
        dz = w[2:3, :] * dconv + w[1:2, :] * _shift_up(dconv, 1) + w[0:1, :] * _shift_up(dconv, 2)
        dc_ref[...] = (dz * u).astype(dc_ref.dtype)
        du_ref[...] = (dz * c).astype(du_ref.dtype)
        dg_ref[...] = _sublane_sum(dgc)
        dw_ref[0] = _sublane_sum(dconv * _shift_down(z, 2))
        dw_ref[1] = _sublane_sum(dconv * _shift_down(z, 1))
        dw_ref[2] = _sublane_sum(dconv * z)

    col = lambda k: pl.BlockSpec((s, HEAD), lambda i: (0, k * n_groups + i))
    grp = pl.BlockSpec((s, HEAD), lambda i: (0, i))
    return pl.pallas_call(
        body, name="conv_bwd", grid=(n_groups,),
        in_specs=[col(0), col(1), col(2), grp, pl.BlockSpec((3, HEAD), lambda i: (0, i)), pl.BlockSpec((1, HEAD), lambda i: (0, i))],
        out_specs=[grp, grp, grp, pl.BlockSpec((SUBLANE, HEAD), lambda i: (0, i)),
                   pl.BlockSpec((3, SUBLANE, HEAD), lambda i: (0, 0, i))],
        out_shape=[jax.ShapeDtypeStruct((s, width), BF16)] * 3
        + [jax.ShapeDtypeStruct((SUBLANE, width), F32), jax.ShapeDtypeStruct((3, SUBLANE, width), F32)],
        compiler_params=_params("parallel"),
    )(proj, proj, proj, d_mix, conv_w, g)


def _rope_tables(s, n_heads):
    pos = jnp.arange(s, dtype=F32)
    inv_freq = jnp.power(ROPE_THETA, -jnp.arange(0, ROPE, 2, dtype=F32) / ROPE)
    ang = pos[:, None] * inv_freq[None, :]
    cos, sin = jnp.cos(ang), jnp.sin(ang)
    cs = jnp.concatenate([cos, cos], axis=1)
    sn = jnp.concatenate([-sin, sin], axis=1)
    pad = jnp.zeros((s, LANE - ROPE), F32)
    return (jnp.tile(cs, (1, n_heads)), jnp.tile(sn, (1, n_heads)),
            jnp.concatenate([cs, pad], axis=1), jnp.concatenate([sn, pad], axis=1))


def _swap_halves(v):
    w = v.shape[1]
    lane = lax.broadcasted_iota(jnp.int32, v.shape, 1)
    first = (lane % ROPE) < (ROPE // 2)
    return jnp.where(first, pltpu.roll(v, w - ROPE // 2, 1), pltpu.roll(v, ROPE // 2, 1))


def _pack_heads(q, kv, proj, kr_col, tables, n_heads, after=()):
    s = q.shape[0]
    rows = min(ROWS, s)
    cq, sq, ck, sk = tables
    wq = n_heads * ROPE

    def body(q_ref, kv_ref, kr_ref, cq_ref, sq_ref, ck_ref, sk_ref, qo_ref, ko_ref, vo_ref):
        qr = q_ref[:, n_heads * HEAD:]
        qr = qr * cq_ref[...] + _swap_halves(qr) * sq_ref[...]
        krv = kr_ref[...]
        krv = krv * ck_ref[...] + _swap_halves(krv) * sk_ref[...]
        for h in range(n_heads):
            qo_ref[h] = jnp.concatenate([q_ref[:, h * HEAD:(h + 1) * HEAD], qr[:, h * ROPE:(h + 1) * ROPE]], axis=1).astype(BF16)
            ko_ref[h] = jnp.concatenate([kv_ref[:, 2 * h * HEAD:(2 * h + 1) * HEAD], krv[:, :ROPE]], axis=1).astype(BF16)
            vo_ref[h] = kv_ref[:, (2 * h + 1) * HEAD:(2 * h + 2) * HEAD].astype(BF16)

    hs = lambda w: pl.BlockSpec((n_heads, rows, w), lambda i: (0, i, 0))
    return _call(
        body, name="pack_heads", grid=(s // rows,), after=after,
        in_specs=[_row_spec(rows, q.shape[1]), _row_spec(rows, kv.shape[1]), pl.BlockSpec((rows, LANE), lambda i: (i, kr_col // LANE)),
                  _row_spec(rows, wq), _row_spec(rows, wq), _row_spec(rows, LANE), _row_spec(rows, LANE)],
        out_specs=[hs(QK), hs(QK), hs(HEAD)],
        out_shape=[jax.ShapeDtypeStruct((n_heads, s, QK), BF16), jax.ShapeDtypeStruct((n_heads, s, QK), BF16),
                   jax.ShapeDtypeStruct((n_heads, s, HEAD), BF16)],
        compiler_params=_params("parallel"),
    )(q, kv, proj, cq, sq, ck, sk)


def _unpack_heads(dq, dk, dv, tables, n_heads):
    s = dq.shape[1]
    rows = min(ROWS, s)
    cq, sq, ck, sk = tables
    wq = n_heads * ROPE

    def body(dq_ref, dk_ref, dv_ref, cq_ref, sq_ref, ck_ref, sk_ref, qo_ref, kvo_ref, kro_ref):
        dqr = jnp.concatenate([dq_ref[h][:, HEAD:] for h in range(n_heads)], axis=1)
        dqr = dqr * cq_ref[...] - _swap_halves(dqr) * sq_ref[...]
        dkr = dk_ref[0][:, HEAD:]
        for h in range(1, n_heads):
            dkr = dkr + dk_ref[h][:, HEAD:]
        dkr = jnp.concatenate([dkr, jnp.zeros((rows, LANE - ROPE), F32)], axis=1)
        dkr = dkr * ck_ref[...] - _swap_halves(dkr) * sk_ref[...]
        kro_ref[...] = dkr.astype(kro_ref.dtype)
        qo_ref[:, n_heads * HEAD:] = dqr.astype(qo_ref.dtype)
        for h in range(n_heads):
            qo_ref[:, h * HEAD:(h + 1) * HEAD] = dq_ref[h][:, :HEAD].astype(qo_ref.dtype)
            kvo_ref[:, 2 * h * HEAD:(2 * h + 1) * HEAD] = dk_ref[h][:, :HEAD].astype(kvo_ref.dtype)
            kvo_ref[:, (2 * h + 1) * HEAD:(2 * h + 2) * HEAD] = dv_ref[h].astype(kvo_ref.dtype)

    hs = lambda w: pl.BlockSpec((n_heads, rows, w), lambda i: (0, i, 0))
    return pl.pallas_call(
        body, name="unpack_heads", grid=(s // rows,),
        in_specs=[hs(QK), hs(QK), hs(HEAD), _row_spec(rows, wq), _row_spec(rows, wq), _row_spec(rows, LANE), _row_spec(rows, LANE)],
        out_specs=[_row_spec(rows, n_heads * QK), _row_spec(rows, 2 * n_heads * HEAD), _row_spec(rows, LANE)],
        out_shape=[jax.ShapeDtypeStruct((s, n_heads * QK), BF16), jax.ShapeDtypeStruct((s, 2 * n_heads * HEAD), BF16),
                   jax.ShapeDtypeStruct((s, LANE), BF16)],
        compiler_params=_params("parallel"),
    )(dq, dk, dv, cq, sq, ck, sk)


TQ = 256


LOG2_E = 1.4426950408889634


def _softmax_parts(q, k):
    tq, n_keys = q.shape[0], k.shape[0]
    sc = lax.dot_general(q, k, (NT, ((), ())), preferred_element_type=F32) * (QK ** -0.5 * LOG2_E)
    row = lax.broadcasted_iota(jnp.int32, (tq, tq), 0)
    col = lax.broadcasted_iota(jnp.int32, (tq, tq), 1)
    own = jnp.where(col // CHUNK <= row // CHUNK, sc[:, n_keys - tq:], NEG_INF)
    sc = own if n_keys == tq else jnp.concatenate([sc[:, :n_keys - tq], own], axis=1)
    e = jnp.exp2(sc - jnp.max(sc, axis=-1, keepdims=True))
    return e, 1.0 / jnp.sum(e, axis=-1, keepdims=True)


def _prob_columns(c, tq):
    return pl.ds(tq * (c * (c + 1) // 2), (c + 1) * tq)


def _attn_fwd(q, k, v, g, mix, col0, first, between):
    n_heads, s, _ = q.shape
    tq = min(TQ, s)
    assert tq % CHUNK == 0 and s % tq == 0
    n_blocks = s // tq
    p_cols = tq * (n_blocks * (n_blocks + 1) // 2)
    out_shape = [jax.ShapeDtypeStruct((n_heads, s, HEAD), F32), jax.ShapeDtypeStruct((n_heads, tq, p_cols), BF16),
                 jax.ShapeDtypeStruct(mix.shape, mix.dtype)]
    done, after = (mix,), ()
    for part, (h0, h1) in enumerate(((0, first), (first, n_heads))):

        def body(q_ref, k_ref, v_ref, g_ref, *rest):
            o_ref, p_ref, y_ref = rest[-3:]
            for c in range(n_blocks):
                rows, n_keys = pl.ds(c * tq, tq), (c + 1) * tq
                e, inv = _softmax_parts(q_ref[rows, :], k_ref[0:n_keys, :])
                p = (e * inv).astype(BF16)
                p_ref[:, _prob_columns(c, tq)] = p
                o = jnp.dot(p, v_ref[0:n_keys, :], preferred_element_type=F32)
                o_ref[rows, :] = o
                y_ref[rows, :] = (o * _rstd(o) * g_ref[...]).astype(y_ref.dtype)

        head = lambda w, h0=h0: pl.BlockSpec((None, s, w), lambda h: (h0 + h, 0, 0))
        n_done = len(done)
        done = _call(
            body, name=f"attn_fwd_{part}", grid=(h1 - h0,), after=after,
            in_specs=[head(QK), head(QK), head(HEAD), pl.BlockSpec((1, HEAD), lambda h, h0=h0: (0, h0 + h))] + [ANY] * n_done,
            out_specs=[head(HEAD), pl.BlockSpec((None, tq, p_cols), lambda h, h0=h0: (h0 + h, 0, 0)),
                       pl.BlockSpec((s, HEAD), lambda h, h0=h0: (0, col0 // HEAD + h0 + h))],
            out_shape=out_shape,
            input_output_aliases={4 + i: 3 - n_done + i for i in range(n_done)},
            compiler_params=_params("parallel"),
        )(q, k, v, g, *done)
        after = between(done) if part == 0 else ()
    return done


def _attn_bwd(q, k, v, o, probs, d_mix, g, col0, after=()):
    n_heads, s, _ = q.shape
    tq = probs.shape[1]

    def body(q_ref, k_ref, v_ref, o_ref, p_ref, dy_ref, g_ref, dq_ref, dk_ref, dv_ref, dg_ref):
        dg = None
        for c in reversed(range(s // tq)):
            rows, n_keys = pl.ds(c * tq, tq), (c + 1) * tq
            o = o_ref[rows, :]
            do, dgc = _rms_bwd(o, g_ref[...], dy_ref[rows, :])
            do = do.astype(BF16)
            dg = _sublane_sum(dgc) if dg is None else dg + _sublane_sum(dgc)
            p = p_ref[:, _prob_columns(c, tq)]
            dp = lax.dot_general(do, v_ref[0:n_keys, :], (NT, ((), ())), preferred_element_type=F32)
            ds = (p.astype(F32) * (dp - jnp.sum(do.astype(F32) * o, axis=-1, keepdims=True))).astype(BF16)
            dq_ref[rows, :] = jnp.dot(ds, k_ref[0:n_keys, :], preferred_element_type=F32) * (QK ** -0.5)
            dk = lax.dot_general(ds, q_ref[rows, :], (TN, ((), ())), preferred_element_type=F32)
            dv = lax.dot_general(p, do, (TN, ((), ())), preferred_element_type=F32)
            if n_keys == s:
                dk_ref[...] = dk
                dv_ref[...] = dv
            else:
                dk_ref[0:n_keys, :] += dk
                dv_ref[0:n_keys, :] += dv
        dk_ref[...] = dk_ref[...] * (QK ** -0.5)
        dg_ref[...] = dg

    c0 = col0 // HEAD
    head = lambda w: pl.BlockSpec((None, s, w), lambda h, *_: (h, 0, 0))
    in_specs = [head(QK), head(QK), head(HEAD), head(HEAD), pl.BlockSpec((None, tq, probs.shape[2]), lambda h, *_: (h, 0, 0)),
                pl.BlockSpec((s, HEAD), lambda h, *_: (0, c0 + h)), pl.BlockSpec((1, HEAD), lambda h, *_: (0, h))]
    out_specs = [head(QK), head(QK), head(HEAD), pl.BlockSpec((SUBLANE, HEAD), lambda h, *_: (0, h))]
    out_shape = [jax.ShapeDtypeStruct((n_heads, s, QK), F32), jax.ShapeDtypeStruct((n_heads, s, QK), F32),
                 jax.ShapeDtypeStruct((n_heads, s, HEAD), F32), jax.ShapeDtypeStruct((SUBLANE, n_heads * HEAD), F32)]
    return _call(body, name="attn_bwd", grid=(n_heads,), after=after, in_specs=in_specs, out_specs=out_specs,
                 out_shape=out_shape, compiler_params=_params("parallel"))(q, k, v, o, probs, d_mix, g)


TILE_M = 1024
TILE_N = 1024


def _up_fwd(h2, w_up, between):
    s, d = h2.shape
    nb, _, fb = w_up.shape
    tm = min(TILE_M, s)
    done, after = (), ()
    for tile in range(s // tm):

        def body(h_ref, w_ref, *rest):
            a_ref, r_ref = rest[-2:]
            r = jnp.maximum(jnp.dot(h_ref[...], w_ref[...], preferred_element_type=F32), 0.0)
            a_ref[...] = (r * r).astype(a_ref.dtype)
            r_ref[...] = r.astype(r_ref.dtype)

        blk = pl.BlockSpec((tm, fb), lambda j, tile=tile: (tile, j))
        done = _call(
            body, name=f"up_fwd_{tile}", grid=(nb,), after=after,
            in_specs=[pl.BlockSpec((tm, d), lambda j, tile=tile: (tile, 0)), pl.BlockSpec((None, d, fb), lambda j: (j, 0, 0))]
                     + [ANY] * len(done),
            out_specs=[blk, blk], out_shape=[jax.ShapeDtypeStruct((s, nb * fb), BF16)] * 2,
            input_output_aliases={2 + i: i for i in range(len(done))},
            compiler_params=_params("parallel"),
        )(h2, w_up, *done)
        after = between(done) if tile == 0 else ()
    return done


def _down_fwd(a, w_down):
    s, f = a.shape
    d = w_down.shape[1]
    tm, tn, tk = min(TILE_M,s), min(TILE_N,d), 2048
    nk = f // tk
    return _matmul("down_fwd", a, w_down, grid=(s // tm, d // tn, nk),
                   a_spec=pl.BlockSpec((tm, tk), lambda i, j, k: (i, k)),
                   b_spec=pl.BlockSpec((tk, tn), lambda i, j, k: (k, j)),
                   out_shape=jax.ShapeDtypeStruct((s, d), F32),
                   out_specs=pl.BlockSpec((tm, tn), lambda i, j, k: (i, j)),
                   contract=NN, nk=nk, acc_shape=(tm, tn))


def _down_bwd_act(d_m, w_down, r, after=()):
    s, d = d_m.shape
    f = w_down.shape[0]
    tm, tn = min(TILE_M,s), min(TILE_N,f)
    blk = pl.BlockSpec((tm, tn), lambda i, j: (i, j))
    return _matmul("down_bwd_act", d_m, w_down, grid=(s // tm, f // tn), after=after,
                   a_spec=pl.BlockSpec((tm, d), lambda i, j: (i, 0)),
                   b_spec=pl.BlockSpec((tn, d), lambda i, j: (j, 0)),
                   out_shape=jax.ShapeDtypeStruct((s, f), BF16), out_specs=blk, contract=NT,
                   extras=(r,), extra_specs=(blk,),
                   epilogue=lambda acc, rv: (acc * (2.0 * rv.astype(F32)),))


def _up_bwd_act(d_up, w_up, after=()):
    s, _ = d_up.shape
    nb, d, fb = w_up.shape
    tm, tn = min(TILE_M, s), min(TILE_N,d)
    pair = 2
    n_after = len(after)

    def body(a_ref, w_ref, *rest):
        o_ref, acc_ref = rest[n_after:]
        k = pl.program_id(2)
        p = None
        for t in range(pair):
            term = lax.dot_general(a_ref[:, t * fb:(t + 1) * fb], w_ref[t], (NT, ((), ())), preferred_element_type=F32)
            p = term if p is None else p + term
        _accumulate(acc_ref, p, k)

        @pl.when(k == nb // pair - 1)
        def _():
            o_ref[...] = acc_ref[...]

    return pl.pallas_call(
        body, name="up_bwd_act", grid=(s // tm, d // tn, nb // pair),
        in_specs=[pl.BlockSpec((tm, pair * fb), lambda i, j, k: (i, k)),
                  pl.BlockSpec((pair, tn, fb), lambda i, j, k: (k, j, 0))] + [ANY] * n_after,
        out_specs=pl.BlockSpec((tm, tn), lambda i, j, k: (i, j)),
        out_shape=jax.ShapeDtypeStruct((s, d), F32),
        scratch_shapes=[pltpu.VMEM((tm, tn), F32)],
        compiler_params=_params("parallel", "parallel", "arbitrary"),
    )(d_up, w_up, *after)


def _half_grad(name, a, b, core, home, received, after, *, grid, a_block, a_map, b_block, b_map, o_block, o_map, out_shape):
    n_after = len(after)
    pick = (lambda ref: ref[0]) if home else (lambda ref: 1 - ref[0])

    def body(core_ref, a_ref, b_ref, *rest):
        acc = lax.dot_general(a_ref[...], b_ref[...], (TN, ((), ())), preferred_element_type=F32)
        if received is not None:
            acc = acc + rest[0][...].astype(F32)
        rest[-1][...] = acc.astype(rest[-1].dtype)

    wrap = lambda fn: (lambda i, j, core_ref: fn(i, j, pick(core_ref)))
    o_spec = pl.BlockSpec(o_block, wrap(o_map))
    extra = [] if received is None else [o_spec]
    operands = [] if received is None else [received]
    return pl.pallas_call(
        body, name=name,
        grid_spec=pltpu.PrefetchScalarGridSpec(
            num_scalar_prefetch=1, grid=grid,
            in_specs=[pl.BlockSpec(a_block, wrap(a_map)), pl.BlockSpec(b_block, wrap(b_map))] + extra + [ANY] * n_after,
            out_specs=o_spec),
        out_shape=out_shape,
        compiler_params=_params("parallel", "parallel"),
    )(core, a, b, *operands, *after)


def _down_half_grad(name, a, d_m, core, home, received=None, after=()):
    s, f = a.shape
    d = d_m.shape[1]
    r = f // N_DEV
    tn = min(TILE_N, d)
    return _half_grad(name, a, d_m, core, home, received, after, grid=(N_CHIP, d // tn),
                      a_block=(s, r), a_map=lambda k, j, p: (0, 2 * k + p),
                      b_block=(s, tn), b_map=lambda k, j, p: (0, j),
                      o_block=(None, r, tn), o_map=lambda k, j, p: (k, 0, j),
                      out_shape=jax.ShapeDtypeStruct((N_CHIP, r, d), BF16))


def _up_half_grad(name, h2, d_up, core, home, received=None, after=()):
    s, d = h2.shape
    fb = d_up.shape[1] // N_DEV
    tm = min(TILE_M, d)
    return _half_grad(name, h2, d_up, core, home, received, after, grid=(d // tm, N_CHIP),
                      a_block=(s, tm), a_map=lambda i, k, p: (0, i),
                      b_block=(s, fb), b_map=lambda i, k, p: (0, 2 * k + p),
                      o_block=(None, tm, fb), o_map=lambda i, k, p: (k, i, 0),
                      out_shape=jax.ShapeDtypeStruct((N_CHIP, d, fb), BF16))


MXU_WIDTH = 256


def _in_pad(in_width):
    return -(-in_width // MXU_WIDTH) * MXU_WIDTH


def _join_col_shards(name, blocks, own, device, pieces=None):
    n, r, w = blocks.shape
    rows = min(ROWS, r)
    pieces = pieces or [(j, 0, w) for j in range(n)]
    used = sum(b - a for _, a, b in pieces)
    width = _in_pad(used)

    def body(dev_ref, x_ref, own_ref, o_ref):
        block = lambda j: jnp.where(dev_ref[0] == j, own_ref[...], x_ref[j])
        cols = [block(j)[:, a:b] for j, a, b in pieces]
        tail = [jnp.zeros((rows, width - used), o_ref.dtype)] if width > used else []
        o_ref[...] = jnp.concatenate(cols + tail, axis=1)

    return pl.pallas_call(
        body, name=name,
        grid_spec=pltpu.PrefetchScalarGridSpec(
            num_scalar_prefetch=1, grid=(r // rows,),
            in_specs=[pl.BlockSpec((n, rows, w), lambda i, dev: (0, i, 0)), pl.BlockSpec((rows, w), lambda i, dev: (i, 0))],
            out_specs=pl.BlockSpec((rows, width), lambda i, dev: (i, 0))),
        out_shape=jax.ShapeDtypeStruct((r, width), blocks.dtype),
        compiler_params=_params("parallel"),
    )(device, blocks, own)


def _unpermute_q_rows(wt, n_heads):
    r = wt.shape[1]
    nope = wt[:n_heads * HEAD].reshape(n_heads, HEAD, r)
    rope = wt[n_heads * HEAD:].reshape(n_heads, ROPE, r)
    return jnp.concatenate([nope, rope], axis=1).reshape(n_heads * QK, r)


def _local_step(x, tgt, gains, weights, grads, first_after=()):
    pre_mix_g, q_norm_g, kv_norm_g, conv_out_g, attn_out_g, post_mix_g, pre_mlp_g, post_mlp_g = gains
    s, d = x.shape
    conv_width = conv_out_g.shape[1]
    n_groups = conv_width // HEAD
    r_q, r_kv = q_norm_g.shape[1], kv_norm_g.shape[1]
    n_heads = attn_out_g.shape[1] // HEAD
    c_q0 = 3 * conv_width
    c_kv0 = c_q0 + r_q
    c_kr0 = c_kv0 + r_kv
    in_pad = _in_pad(c_kr0 + ROPE)
    tn_in = _fit(in_pad, 6 * MXU_WIDTH)
    tables = _rope_tables(s, n_heads)

    h1 = _rms_fwd("pre_mix_norm", x, pre_mix_g, after=first_after)
    weights.forward(0, (h1,))
    weights.relay(0, tables)
    w_in_p, conv_w = weights.ready(0, ())
    proj = _mm_nn("in_proj", h1, w_in_p, F32, TILE_M, tn_in)
    y_conv = _conv_fwd(proj, conv_w, conv_out_g, n_groups, conv_width + n_heads * HEAD, after=weights.forward(1, (proj,)))
    w_uq_p, w_ukv, w_o = weights.ready(1, (y_conv,))
    qn, q = _norm_up("q_up", proj, (c_q0, r_q), q_norm_g, w_uq_p)
    kvn, kv = _norm_up("kv_up", proj, (c_kv0, r_kv), kv_norm_g, w_ukv)
    qh, kh, vh = _pack_heads(q, kv, proj, c_kr0, tables, n_heads)
    o, probs, mix = _attn_fwd(qh, kh, vh, attn_out_g, y_conv, conv_width, n_heads // 4,
                              lambda done: weights.forward(2, tuple(done)))
    y = _mm_nn("out_proj", mix, w_o, F32, TILE_M, TILE_N, after=weights.start(3, (mix,)))
    x2, h2 = _mid_fwd(x, y, post_mix_g, pre_mlp_g)
    weights.relay(2, (h2,))
    (w_up,) = weights.ready(2, ())
    a, r = _up_fwd(h2, w_up, lambda done: weights.forward(3, tuple(done)))
    weights.relay(3, (a,))
    (w_down,) = weights.ready(3, ())
    m = _down_fwd(a, w_down)

    d_out, d_m, dg_post_mlp, loss_part = _head(m, x2, tgt, post_mlp_g)
    core = grads.core
    away = _down_half_grad("down_bwd_w_away", a, d_m, core, home=False)
    d_up = _down_bwd_act(d_m, w_down, r, after=grads.send_away(0, away))
    sums = _down_half_grad("down_bwd_w_home", a, d_m, core, home=True, received=grads.received(0, (d_up,)))
    away = _up_half_grad("up_bwd_w_away", h2, d_up, core, home=False, after=grads.send_sums(0, (sums,)))
    d_h2 = _up_bwd_act(d_up, w_up, after=grads.send_away(1, away))
    sums = _up_half_grad("up_bwd_w_home", h2, d_up, core, home=True, received=grads.received(1, (d_h2,)))
    d_x2, d_y, dg_pre_mlp, dg_post_mix = _mid_bwd(x2, y, d_out, d_h2, pre_mlp_g, post_mix_g, after=grads.send_sums(1, (sums,)))
    d_mix = _mm_nt("out_proj_bwd_act", d_y, w_o, F32, TILE_M, TILE_N)
    gw_o = _mm_tn("out_proj_bwd_w", mix, d_y, BF16, TILE_M, TILE_N)
    dqh, dkh, dvh, dg_attn = _attn_bwd(qh, kh, vh, o, probs, d_mix, attn_out_g, conv_width)
    d_q, d_kv, d_kr = _unpack_heads(dqh, dkh, dvh, tables, n_heads)
    gw_uq_t = _mm_tn("q_up_bwd_w", d_q, qn, F32, TILE_M, TILE_N)
    gw_ukv = _mm_tn("kv_up_bwd_w", kvn, d_kv, BF16, TILE_M, TILE_N)
    d_cq, dg_q = _up_norm_bwd("q_up_bwd_act", d_q, w_uq_p, proj, (c_q0, r_q), q_norm_g, after=grads.full(2, (gw_o, gw_uq_t, gw_ukv)))
    d_ckv, dg_kv = _up_norm_bwd("kv_up_bwd_act", d_kv, w_ukv, proj, (c_kv0, r_kv), kv_norm_g)
    d_u, d_b, d_c, dg_conv, dw_conv = _conv_bwd(proj, d_mix, conv_w, conv_out_g, n_groups)
    d_proj = jnp.concatenate([d_u, d_b, d_c, d_cq, d_ckv, d_kr, jnp.zeros((s, in_pad - c_kr0 - LANE), BF16)], axis=1)
    gw_in_t = _mm_tn("in_proj_bwd_w", d_proj, h1, F32, tn_in, TILE_N)
    updated = grads.update_now(0, grads.send_away(3, gw_in_t))
    d_h1 = _mm_nt("in_proj_bwd_act", d_proj, w_in_p, F32, TILE_M, TILE_N, after=grads.full(3, (gw_in_t,), received=updated))
    grad_x, dg_pre_mix = _first_bwd(x, pre_mix_g, d_h1, d_x2)

    small = [dg_pre_mix, dg_q, dg_kv, dg_conv, dg_attn, dg_post_mix, dg_pre_mlp, dg_post_mlp,
             dw_conv[0], dw_conv[1], dw_conv[2], loss_part]
    return grad_x, jnp.concatenate(small, axis=1)


HBM = pl.BlockSpec(memory_space=pltpu.HBM)
SEM = pl.BlockSpec(memory_space=pltpu.SEMAPHORE)
IN_VMEM = pl.BlockSpec(memory_space=pltpu.VMEM)
SPLIT = pltpu.CompilerParams(has_side_effects=pltpu.SideEffectType.DATAFLOW_SIDE_EFFECTING)


def _in_hbm(a):
    return pltpu.with_memory_space_constraint(a, pltpu.HBM)


def _hbm_like(a):
    return pltpu.HBM(a.shape, a.dtype)


def _place():
    x, y, c = lax.axis_index("x"), lax.axis_index("y"), lax.axis_index("c")
    other_chips = [(1 - x, y), (x, 1 - y), (1 - x, 1 - y)]
    return x, y, c, other_chips


def _block(px, py, pc):
    return 4 * px + 2 * py + pc


def _await(block, sem):
    pltpu.make_async_copy(block, block, sem).wait()


def _relay_route(x, y, c):
    came_from = ((1 - x) * (1 - c) + x * c, y * (1 - c) + (1 - y) * c)
    goes_to = (x * (1 - c) + (1 - x) * c, (1 - y) * (1 - c) + y * c)
    return came_from, goes_to


def _gather_start(name, shards, groups, relayed=(), after=(), own=None):
    n, ng = len(shards), len(groups)
    lands = [lax.empty((N_DEV, *a.shape), a.dtype) for a in shards]
    if own is not None:
        lands = [lax.dynamic_update_index_in_dim(land, a, own, 0) for land, a in zip(lands, shards)]

    def body(*refs):
        src, land = refs[:n], refs[n:2 * n]
        sems, token = refs[2 * n + len(after):2 * n + len(after) + 2 * ng], refs[-1]
        x, y, c, chips = _place()
        targets = [(x, y, 1 - c)] + [(*chip, c) for chip in chips]
        for gi, group in enumerate(groups):
            for i, w in enumerate(group):
                for k, to in enumerate(targets[:3] if gi in relayed else targets):
                    pltpu.make_async_remote_copy(
                        src_ref=src[w], dst_ref=land[w].at[_block(x, y, c)],
                        send_sem=sems[2 * gi].at[4 * i + k], recv_sem=sems[2 * gi + 1].at[4 * i + k],
                        device_id=to, device_id_type=MESH).start()
        token[...] = jnp.zeros_like(token)

    sem_shapes = [pltpu.SemaphoreType.DMA((4 * len(g),)) for g in groups for _ in range(2)]
    out = pl.pallas_call(
        body, name=name,
        in_specs=[HBM] * (2 * n) + [ANY] * len(after),
        out_specs=[SEM] * (2 * ng) + [HBM] * (2 * n) + [IN_VMEM],
        out_shape=sem_shapes + [_hbm_like(a) for a in shards] + [_hbm_like(a) for a in lands]
        + [jax.ShapeDtypeStruct((SUBLANE, LANE), F32)],
        input_output_aliases={i: 2 * ng + i for i in range(2 * n)},
        compiler_params=SPLIT,
    )(*[_in_hbm(a) for a in shards], *[_in_hbm(a) for a in lands], *after)
    sems = [(out[2 * gi], out[2 * gi + 1]) for gi in range(ng)]
    return sems, out[2 * ng:2 * ng + n], out[2 * ng + n:2 * ng + 2 * n], out[-1]


def _gather_forward(name, shards, lands, send1, recv1, after, relayed=False):
    n = len(lands)

    def body(*refs):
        src, land = refs[:n], refs[n:2 * n]
        s1, r1 = refs[2 * n], refs[2 * n + 1]
        s2, r2 = refs[2 * n + 2 + len(after)], refs[2 * n + 3 + len(after)]
        x, y, c, chips = _place()
        me, sibling = (x, y, c), (x, y, 1 - c)
        for j, chip in enumerate(chips[:2] if relayed else chips):
            for i in range(n):
                blk = land[i].at[_block(*chip, c)]
                pltpu.make_async_remote_copy(src_ref=blk, dst_ref=blk, send_sem=s1.at[4 * i + 1 + j], recv_sem=r1.at[4 * i + 1 + j],
                                             device_id=me, device_id_type=MESH).wait_recv()
                pltpu.make_async_remote_copy(src_ref=blk, dst_ref=blk, send_sem=s2.at[3 * i + j], recv_sem=r2.at[3 * i + j],
                                             device_id=sibling, device_id_type=MESH).start()
        if relayed:
            came_from, goes_to = _relay_route(x, y, c)
            for i in range(n):
                blk = land[i].at[_block(*came_from, c)]
                pltpu.make_async_remote_copy(src_ref=blk, dst_ref=blk, send_sem=s2.at[3 * i + 2], recv_sem=r2.at[3 * i + 2],
                                             device_id=(*goes_to, c), device_id_type=MESH).start()
        for i in range(n):
            blk = land[i].at[_block(x, y, 1 - c)]
            pltpu.make_async_remote_copy(src_ref=blk, dst_ref=blk, send_sem=s1.at[4 * i], recv_sem=r1.at[4 * i],
                                         device_id=me, device_id_type=MESH).wait_recv()
            for k in range(3 if relayed else 4):
                pltpu.make_async_remote_copy(src_ref=src[i], dst_ref=land[i].at[_block(x, y, c)], send_sem=s1.at[4 * i + k],
                                             recv_sem=r1.at[4 * i + k], device_id=sibling, device_id_type=MESH).wait_send()

    sem = pltpu.SemaphoreType.DMA((3 * n,))
    out = pl.pallas_call(
        body, name=name,
        in_specs=[HBM] * (2 * n) + [SEM, SEM] + [ANY] * len(after),
        out_specs=[SEM, SEM] + [HBM] * n,
        out_shape=[sem, sem] + [_hbm_like(a) for a in lands],
        input_output_aliases={n + i: 2 + i for i in range(n)},
        compiler_params=SPLIT,
    )(*shards, *lands, send1, recv1, *after)
    return (out[0], out[1]), out[2:]


def _gather_relay_forward(name, lands, send2, recv2, after):
    n = len(lands)

    def body(*refs):
        land, s2, r2 = refs[:n], refs[n], refs[n + 1]
        s3, r3 = refs[n + 2 + len(after)], refs[n + 3 + len(after)]
        x, y, c, _ = _place()
        me, sibling = (x, y, c), (x, y, 1 - c)
        came_from, _ = _relay_route(x, y, c)
        for i in range(n):
            blk = land[i].at[_block(1 - x, 1 - y, c)]
            pltpu.make_async_remote_copy(src_ref=blk, dst_ref=blk, send_sem=s2.at[3 * i + 2], recv_sem=r2.at[3 * i + 2],
                                         device_id=me, device_id_type=MESH).wait_recv()
            pltpu.make_async_remote_copy(src_ref=blk, dst_ref=blk, send_sem=s3.at[i], recv_sem=r3.at[i],
                                         device_id=sibling, device_id_type=MESH).start()
            sent = land[i].at[_block(*came_from, c)]
            pltpu.make_async_remote_copy(src_ref=sent, dst_ref=sent, send_sem=s2.at[3 * i + 2], recv_sem=r2.at[3 * i + 2],
                                         device_id=me, device_id_type=MESH).wait_send()

    sem = pltpu.SemaphoreType.DMA((n,))
    out = pl.pallas_call(
        body, name=name,
        in_specs=[HBM] * n + [SEM, SEM] + [ANY] * len(after),
        out_specs=[SEM, SEM] + [HBM] * n,
        out_shape=[sem, sem] + [_hbm_like(a) for a in lands],
        input_output_aliases={i: 2 + i for i in range(n)},
        compiler_params=SPLIT,
    )(*lands, send2, recv2, *after)
    return (out[0], out[1]), out[2:]


def _gather_wait(name, lands, send2, recv2, after, relay_sems=None):
    n = len(lands)
    n_sems = 2 if relay_sems is None else 4

    def body(*refs):
        land, s2, r2 = refs[:n], refs[n], refs[n + 1]
        for i in range(n):
            for j in range(3 if relay_sems is None else 2):
                _await(land[i].at[0], r2.at[3 * i + j])
                _await(land[i].at[0], s2.at[3 * i + j])
            if relay_sems is not None:
                _await(land[i].at[0], refs[n + 3].at[i])
                _await(land[i].at[0], refs[n + 2].at[i])

    return pl.pallas_call(
        body, name=name,
        in_specs=[HBM] * n + [SEM] * n_sems + [ANY] * len(after), out_specs=[HBM] * n, out_shape=[_hbm_like(a) for a in lands],
        input_output_aliases={i: i for i in range(n)},
        compiler_params=SPLIT,
    )(*lands, send2, recv2, *(relay_sems or ()), *after)


def _pair_exchange(name, grads, shard_rows):
    n = len(grads)
    shapes = [(g.shape[1:] if r is None else (r, g.shape[1])) for g, r in zip(grads, shard_rows)]

    def body(*refs):
        ins, recv = refs[:n], refs[n:2 * n]
        send_sems, recv_sems = refs[2 * n:]
        x, y, c, _ = _place()
        sends = []
        for w in range(n):
            for k in range(N_CHIP):
                j, r = 2 * k + 1 - c, shard_rows[w]
                src = ins[w].at[j] if r is None else ins[w].at[pl.ds(pl.multiple_of(j * r, SUBLANE), r), :]
                sends.append(pltpu.make_async_remote_copy(
                    src_ref=src, dst_ref=recv[w].at[k],
                    send_sem=send_sems.at[w, k], recv_sem=recv_sems.at[w, k],
                    device_id=(x, y, 1 - c), device_id_type=MESH))
        for cp in sends:
            cp.start()
        for cp in sends:
            cp.wait()

    return pl.pallas_call(
        body, name=name,
        in_specs=[ANY] * n, out_specs=[ANY] * n,
        out_shape=[jax.ShapeDtypeStruct((N_CHIP, *shape), g.dtype) for g, shape in zip(grads, shapes)],
        scratch_shapes=[pltpu.SemaphoreType.DMA((n, N_CHIP))] * 2,
    )(*grads)


def _pair_sum_rows(name, grad, received, core):
    _, r, c = received.shape
    tc = _fit(c, 512)

    def body(core_ref, a_ref, b_ref, o_ref):
        o_ref[...] = (a_ref[...] + b_ref[...]).astype(o_ref.dtype)

    spec = pl.BlockSpec((None, r, tc), lambda k, i, core_ref: (k, 0, i))
    return pl.pallas_call(
        body, name=name,
        grid_spec=pltpu.PrefetchScalarGridSpec(
            num_scalar_prefetch=1, grid=(N_CHIP, c // tc),
            in_specs=[pl.BlockSpec((r, tc), lambda k, i, core_ref: (2 * k + core_ref[0], i)), spec],
            out_specs=spec),
        out_shape=jax.ShapeDtypeStruct(received.shape, BF16),
        compiler_params=_params("parallel", "parallel"),
    )(core, grad, received)


def _pair_sum(name, grad, received, core):
    _, r, c = received.shape
    rows = min(ROWS, r)
    assert r % rows == 0

    def body(core_ref, a_ref, b_ref, o_ref):
        o_ref[...] = (a_ref[...].astype(F32) + b_ref[...].astype(F32)).astype(o_ref.dtype)

    spec = pl.BlockSpec((None, rows, c), lambda k, i, core_ref: (k, i, 0))
    return pl.pallas_call(
        body, name=name,
        grid_spec=pltpu.PrefetchScalarGridSpec(
            num_scalar_prefetch=1, grid=(N_CHIP, r // rows),
            in_specs=[pl.BlockSpec((None, None, rows, c), lambda k, i, core_ref: (k, core_ref[0], i, 0)), spec],
            out_specs=spec),
        out_shape=jax.ShapeDtypeStruct(received.shape, received.dtype),
        compiler_params=_params("parallel", "parallel"),
    )(core, grad.reshape(N_CHIP, 2, r, c), received)


def _away_shard(src, k, c, shard_rows):
    if shard_rows is None:
        return src.at[k]
    return src.at[pl.ds(pl.multiple_of((2 * k + 1 - c) * shard_rows, SUBLANE), shard_rows), :]


def _pair_send_start(name, away, shard_rows=None):
    shape = away.shape if shard_rows is None else (N_CHIP, shard_rows, away.shape[1])
    land = lax.empty(shape, away.dtype)

    def body(src, dst, send, recv, src_thru, dst_thru, token):
        x, y, c, _ = _place()
        for k in range(N_CHIP):
            pltpu.make_async_remote_copy(src_ref=_away_shard(src, k, c, shard_rows), dst_ref=dst.at[k], send_sem=send.at[k],
                                         recv_sem=recv.at[k], device_id=(x, y, 1 - c), device_id_type=MESH).start()
        token[...] = jnp.zeros_like(token)

    sem = pltpu.SemaphoreType.DMA((N_CHIP,))
    out = pl.pallas_call(
        body, name=name,
        in_specs=[HBM, HBM], out_specs=[SEM, SEM, HBM, HBM, IN_VMEM],
        out_shape=[sem, sem, _hbm_like(away), _hbm_like(land), jax.ShapeDtypeStruct((SUBLANE, LANE), F32)],
        input_output_aliases={0: 2, 1: 3},
        compiler_params=SPLIT,
    )(_in_hbm(away), _in_hbm(land))
    return (out[0], out[1]), out[2], out[3], out[4]


def _pair_send_wait(name, sems, src, land, after, shard_rows=None):
    def body(src_ref, dst_ref, send, recv, *rest):
        for k in range(N_CHIP):
            _await(dst_ref.at[k], send.at[k])
            _await(dst_ref.at[k], recv.at[k])

    return pl.pallas_call(
        body, name=name,
        in_specs=[HBM, HBM, SEM, SEM] + [ANY] * len(after), out_specs=HBM, out_shape=_hbm_like(land),
        input_output_aliases={1: 0},
        compiler_params=SPLIT,
    )(src, land, *sems, *after)


def _chip_send_start(name, sums):
    n = len(sums)
    lands = [lax.empty(a.shape, a.dtype) for a in sums]

    def body(*refs):
        src, land = refs[:n], refs[n:2 * n]
        send, recv, token = refs[2 * n], refs[2 * n + 1], refs[-1]
        x, y, c, chips = _place()
        for w in range(n):
            for j, (px, py) in enumerate(chips):
                pltpu.make_async_remote_copy(
                    src_ref=src[w].at[2 * px + py], dst_ref=land[w].at[2 * x + y],
                    send_sem=send.at[3 * w + j], recv_sem=recv.at[3 * w + j],
                    device_id=(px, py, c), device_id_type=MESH).start()
        token[...] = jnp.zeros_like(token)

    sem = pltpu.SemaphoreType.DMA((3 * n,))
    out = pl.pallas_call(
        body, name=name,
        in_specs=[HBM] * (2 * n),
        out_specs=[SEM, SEM] + [HBM] * (2 * n) + [IN_VMEM],
        out_shape=[sem, sem] + [_hbm_like(a) for a in sums] + [_hbm_like(a) for a in lands]
        + [jax.ShapeDtypeStruct((SUBLANE, LANE), F32)],
        input_output_aliases={i: 2 + i for i in range(2 * n)},
        compiler_params=SPLIT,
    )(*[_in_hbm(a) for a in sums], *[_in_hbm(a) for a in lands])
    return (out[0], out[1]), out[2:2 + n], out[2 + n:2 + 2 * n], out[-1]


def _chip_send_wait(name, groups, after):
    counts = [len(g[1]) for g in groups]
    n = sum(counts)

    def body(*refs):
        land = refs[n:2 * n]
        sems = refs[2 * n:2 * n + 2 * len(groups)]
        w = 0
        for gi, count in enumerate(counts):
            for i in range(count):
                for j in range(3):
                    _await(land[w].at[0], sems[2 * gi].at[3 * i + j])
                    _await(land[w].at[0], sems[2 * gi + 1].at[3 * i + j])
                w += 1

    sums = [a for g in groups for a in g[1]]
    lands = [a for g in groups for a in g[2]]
    sems = [s for g in groups for s in g[0]]
    return pl.pallas_call(
        body, name=name,
        in_specs=[HBM] * (2 * n) + [SEM] * len(sems) + [ANY] * len(after),
        out_specs=[HBM] * n, out_shape=[_hbm_like(a) for a in lands],
        input_output_aliases={n + i: i for i in range(n)},
        compiler_params=SPLIT,
    )(*sums, *lands, *sems, *after)


def _small_all_reduce(part, after=()):
    _, w = part.shape

    def body(p_ref, *rest):
        o_ref, buf, send_sems, recv_sems = rest[len(after):]
        x, y, c, _ = _place()
        me = 4 * x + 2 * y + c
        buf[me] = jnp.sum(p_ref[...], axis=0, keepdims=True)
        copies = []
        for k in range(1, N_DEV):
            dx, dy, dc = (k >> 2) & 1, (k >> 1) & 1, k & 1
            copies.append(pltpu.make_async_remote_copy(
                src_ref=buf.at[me], dst_ref=buf.at[me], send_sem=send_sems.at[k - 1], recv_sem=recv_sems.at[k - 1],
                device_id=(x ^ dx, y ^ dy, c ^ dc), device_id_type=MESH))
        for cp in copies:
            cp.start()
        for cp in copies:
            cp.wait()
        tot = buf[0]
        for d in range(1, N_DEV):
            tot = tot + buf[d]
        o_ref[...] = tot
        loss = jnp.sum(tot[:, w - LANE:], axis=1, keepdims=True)
        o_ref[:, w - LANE:] = jnp.broadcast_to(loss, (1, LANE))

    return pl.pallas_call(
        body, name="small_all_reduce",
        in_specs=[IN_VMEM] + [ANY] * len(after), out_specs=IN_VMEM,
        out_shape=jax.ShapeDtypeStruct((1, w), F32),
        scratch_shapes=[pltpu.VMEM((N_DEV, 1, w), F32), pltpu.SemaphoreType.DMA((N_DEV - 1,)), pltpu.SemaphoreType.DMA((N_DEV - 1,))],
        compiler_params=pltpu.CompilerParams(vmem_limit_bytes=VMEM_LIMIT_BYTES),
    )(part, *after)


def _adamw(w, g, m, v):
    m = ADAM_B1 * m + (1.0 - ADAM_B1) * g
    v = ADAM_B2 * v + (1.0 - ADAM_B2) * (g * g)
    m_hat = m / (1.0 - ADAM_B1 ** ADAM_STEP)
    v_hat = v / (1.0 - ADAM_B2 ** ADAM_STEP)
    delta = -ADAM_LR * (m_hat / (jnp.sqrt(v_hat) + ADAM_EPS) + ADAM_WD * w)
    return delta, m, v


def _sum_adam_block(chip_ref, p_ref, own_ref, w_ref, m_ref, v_ref, g_ref, d_ref, mo_ref, vo_ref):
    g = None
    for k in range(N_CHIP):
        term = jnp.where(chip_ref[0] == k, own_ref[...], p_ref[k]).astype(F32)
        g = term if g is None else g + term
    g_ref[...] = g
    d_ref[...], mo_ref[...], vo_ref[...] = _adamw(w_ref[...], g, m_ref[...], v_ref[...])


def _sum_adam(name, parts, sums, chip, w, m, v, after=()):
    _, r, c = w.shape
    n_after = len(after)
    by_rows = r % ROWS == 0 or r < ROWS
    tr, tc = (min(ROWS, r), c) if by_rows else (r, _fit(c, 512))
    at = (lambda i: (i, 0)) if by_rows else (lambda i: (0, i))

    def body(chip_ref, p_ref, own_ref, w_ref, m_ref, v_ref, *rest):
        _sum_adam_block(chip_ref, p_ref, own_ref, w_ref, m_ref, v_ref, *rest[n_after:])

    blk = pl.BlockSpec((None, tr, tc), lambda i, chip_ref: (0, *at(i)))
    out = jax.ShapeDtypeStruct((1, r, c), F32)
    return pl.pallas_call(
        body, name=name,
        grid_spec=pltpu.PrefetchScalarGridSpec(
            num_scalar_prefetch=1, grid=(r // tr if by_rows else c // tc,),
            in_specs=[pl.BlockSpec((N_CHIP, tr, tc), lambda i, chip_ref: (0, *at(i))),
                      pl.BlockSpec((None, tr, tc), lambda i, chip_ref: (chip_ref[0], *at(i))), blk, blk, blk]
            + [ANY] * n_after,
            out_specs=[blk] * 4),
        out_shape=[out] * 4,
        compiler_params=_params("parallel"),
    )(chip, parts, sums, w, m, v, *after)


def _adam_gains(total, ws, ms, vs):
    n = len(ws)
    widths = [w.shape[1] for w in ws]

    def body(t_ref, *refs):
        w_refs, m_refs, v_refs, outs = refs[:n], refs[n:2 * n], refs[2 * n:3 * n], refs[3 * n:]
        off = 0
        for i in range(n):
            g = t_ref[:, off:off + widths[i]]
            off += widths[i]
            g_ref, d_ref, mo_ref, vo_ref = outs[4 * i:4 * i + 4]
            g_ref[...] = g
            d_ref[...], mo_ref[...], vo_ref[...] = _adamw(w_refs[i][...], g, m_refs[i][...], v_refs[i][...])

    out = pl.pallas_call(
        body, name="adam_gains",
        out_shape=[jax.ShapeDtypeStruct(w.shape, F32) for w in ws for _ in range(4)],
    )(total, *ws, *ms, *vs)
    return [tuple(out[4 * i:4 * i + 4]) for i in range(n)]


def _adam_taps(total, first_col, device, w, m, v):
    _, n_taps, cw = w.shape
    col_block = lambda t, dev: (0, first_col // cw + t * N_DEV + dev[0])
    tap = pl.BlockSpec((None, 1, cw), lambda t, dev: (t, 0, 0))

    def body(dev_ref, t_ref, w_ref, m_ref, v_ref, g_ref, d_ref, mo_ref, vo_ref):
        g = t_ref[...]
        g_ref[...] = g
        d_ref[...], mo_ref[...], vo_ref[...] = _adamw(w_ref[...], g, m_ref[...], v_ref[...])

    shape3 = (n_taps, 1, cw)
    out = pl.pallas_call(
        body, name="adam_taps",
        grid_spec=pltpu.PrefetchScalarGridSpec(
            num_scalar_prefetch=1, grid=(n_taps,),
            in_specs=[pl.BlockSpec((1, cw), col_block), tap, tap, tap], out_specs=[tap] * 4),
        out_shape=[jax.ShapeDtypeStruct(shape3, F32)] * 4,
    )(device, total, w.reshape(shape3), m.reshape(shape3), v.reshape(shape3))
    return tuple(o.reshape(w.shape) for o in out)


def kernel(x, pre_mix_g, w_in, conv_w, q_norm_g, w_uq, kv_norm_g, w_ukv, conv_out_g, attn_out_g, w_o, post_mix_g, pre_mlp_g, w_up, w_down, post_mlp_g, loss_target, m_pre_mix_g, m_w_in, m_conv_w, m_q_norm_g, m_w_uq, m_kv_norm_g, m_w_ukv, m_conv_out_g, m_attn_out_g, m_w_o, m_post_mix_g, m_pre_mlp_g, m_w_up, m_w_down, m_post_mlp_g, v_pre_mix_g, v_w_in, v_conv_w, v_q_norm_g, v_w_uq, v_kv_norm_g, v_w_ukv, v_conv_out_g, v_attn_out_g, v_w_o, v_post_mix_g, v_pre_mlp_g, v_w_up, v_w_down, v_post_mlp_g):
    me = 4 * lax.axis_index("x") + 2 * lax.axis_index("y") + lax.axis_index("c")
    core = lax.axis_index("c").astype(jnp.int32).reshape(1)
    chip = (2 * lax.axis_index("x") + lax.axis_index("y")).astype(jnp.int32).reshape(1)
    gains = (pre_mix_g, q_norm_g, kv_norm_g, conv_out_g, attn_out_g, post_mix_g, pre_mlp_g, post_mlp_g)
    gain_m = (m_pre_mix_g, m_q_norm_g, m_kv_norm_g, m_conv_out_g, m_attn_out_g, m_post_mix_g, m_pre_mlp_g, m_post_mlp_g)
    gain_v = (v_pre_mix_g, v_q_norm_g, v_kv_norm_g, v_conv_out_g, v_attn_out_g, v_post_mix_g, v_pre_mlp_g, v_post_mlp_g)
    names = ("w_in", "w_uq", "w_ukv", "w_o", "w_up", "w_down")
    big = dict(zip(names, (w_in, w_uq, w_ukv, w_o, w_up, w_down)))
    big_m = dict(zip(names, (m_w_in, m_w_uq, m_w_ukv, m_w_o, m_w_up, m_w_down)))
    big_v = dict(zip(names, (v_w_in, v_w_uq, v_w_ukv, v_w_o, v_w_up, v_w_down)))
    n_heads = attn_out_g.shape[1] // HEAD
    n_taps = conv_w.shape[1]

    gathered = ("w_in", "conv", "w_uq", "w_ukv", "w_o", "w_up", "w_down")
    gather_groups = ((0, 1), (2, 3, 4), (5,), (6,))
    taps = jnp.pad(conv_w[0], ((0, SUBLANE - n_taps), (0, 0)))
    relayed_groups = (0, 2, 3)
    sems1, shards, lands, token = _gather_start("gather_start_first", [w_in[0].astype(BF16), taps], ((0, 1),), relayed=(0,))
    sems1, shards, lands = list(sems1), list(shards), list(lands)
    behind = token[0, 0]
    rest = [(big[nm][0] + behind).astype(BF16) for nm in gathered[2:]]

    def start_more(name, some, groups, relayed, after):
        sems_b, shards_b, lands_b, started = _gather_start(name, some, groups, relayed=relayed, after=after, own=me)
        sems1.extend(sems_b)
        shards.extend(shards_b)
        lands.extend(lands_b)
        return started

    start_rest = lambda after: start_more("gather_start_rest", rest[:4], ((0, 1, 2), (3,)), (1,), after)
    start_last = lambda after: start_more("gather_start_last", rest[4:], ((0,),), (0,), after)

    cols = lambda a: jnp.concatenate([a[j] for j in range(N_DEV)], axis=1)
    rows = lambda a: a.reshape(N_DEV * a.shape[1], a.shape[2])
    device = me.astype(jnp.int32).reshape(1)
    own_in = lambda a, shard: lax.dynamic_update_index_in_dim(a, shard, me, 0)
    q_pieces = [(h, 0, HEAD) for h in range(n_heads)] + [(h, HEAD, QK) for h in range(n_heads)]
    ready = {
        "w_in": lambda a, shard: _join_col_shards("join_w_in", a, shard, device),
        "conv": lambda a, shard: cols(own_in(a, shard))[:n_taps],
        "w_uq": lambda a, shard: _join_col_shards("join_w_uq", a, shard, device, q_pieces),
        "w_ukv": lambda a, shard: cols(a),
        "w_o": lambda a, shard: rows(a),
        "w_up": lambda a, shard: a,
        "w_down": lambda a, shard: rows(a),
    }
    assert w_uq.shape[2] == QK

    class Weights:
        def __init__(self):
            self.passed, self.relayed = {}, {}

        def forward(self, group, after):
            idx = gather_groups[group]
            if group == 0:
                after = (*after, *rest)
            self.passed[group] = _gather_forward(f"gather_forward_{group}", [shards[i] for i in idx], [lands[i] for i in idx],
                                                 *sems1[group], after, relayed=group in relayed_groups)
            return tuple(self.passed[group][1])

        def start(self, group, after):
            assert group == len(gather_groups) - 1
            return (start_last(after),)

        def relay(self, group, after):
            sems2, mid = self.passed[group]
            self.relayed[group], mid = _gather_relay_forward(f"gather_relay_{group}", mid, *sems2, after)
            self.passed[group] = (sems2, mid)
            if group == 0:
                start_rest(tuple(mid))
            return tuple(mid)

        def ready(self, group, after):
            sems2, mid = self.passed[group]
            full = _gather_wait(f"gather_wait_{group}", mid, *sems2, after, relay_sems=self.relayed.get(group))
            out = []
            return [ready[gathered[i]](a, shards[i]) for i, a in zip(gather_groups[group], full)]

    weights = Weights()

    col_blocks = lambda g: g.reshape(g.shape[0], N_DEV, g.shape[1] // N_DEV).transpose(1, 0, 2)
    row_blocks = lambda g: g.reshape(N_DEV, g.shape[0] // N_DEV, g.shape[1])
    grad_groups = (("w_down",), ("w_up",), ("w_o", "w_uq", "w_ukv"), ("w_in",))
    transposed = {"w_in": w_in.shape[2], "w_uq": w_uq.shape[2]}
    to_blocks = {
        "w_in": lambda g: g, "w_uq": lambda g: _unpermute_q_rows(g, n_heads),
        "w_ukv": col_blocks, "w_o": row_blocks, "w_up": lambda g: g, "w_down": row_blocks,
    }
    in_flight = []

    class Grads:
        def __init__(self):
            self.core = core
            self.away = {}

        def send_sums(self, group, sums):
            sems, sums, parts, tok = _chip_send_start(f"chip_send_start_{group}", list(sums))
            in_flight.append((sems, sums, parts))
            return (tok,)

        def full(self, group, arrays, received=None):
            nms = grad_groups[group]
            if received is None:
                blocks = [to_blocks[nm](g) for nm, g in zip(nms, arrays)]
                got = _pair_exchange(f"pair_exchange_{group}", blocks, [transposed.get(nm) for nm in nms])
            else:
                blocks, got = [self.away[group][1]], [self.received(group, received)]
            sums = [(_pair_sum_rows if nm in transposed else _pair_sum)(f"pair_sum_{nm}", g, r, core)
                    for nm, g, r in zip(nms, blocks, got)]
            return self.send_sums(group, sums)

        def send_away(self, group, half):
            nm = grad_groups[group][0]
            rows = transposed.get(nm)
            sems, src, land, tok = _pair_send_start(f"pair_send_start_{group}", half if rows is None else to_blocks[nm](half), rows)
            self.away[group] = (sems, src, land, rows)
            return (tok,)

        def received(self, group, after):
            sems, src, land, rows = self.away[group]
            return _pair_send_wait(f"pair_send_wait_{group}", sems, src, land, after, rows)

        def update_now(self, group, after):
            return update(str(group), group, group + 1, after)

    big_out = {}

    def update(tag, first, last, after):
        picked = [i for i in range(first, last) if grad_groups[i][0] not in big_out]
        groups = [in_flight[i] for i in picked]
        parts = _chip_send_wait("chip_send_wait_" + tag, groups, after)
        nms = [nm for i in picked for nm in grad_groups[i]]
        sums = [a for _, s, _ in groups for a in s]
        for nm, p, s in zip(nms, parts, sums):
            view = (lambda a: jnp.swapaxes(a, 1, 2)) if nm in transposed else (lambda a: a)
            out = _sum_adam("adam_" + nm, p, s, chip, view(big[nm]), view(big_m[nm]), view(big_v[nm]), after=after)
            after = (out[0],)
            big_out[nm] = [view(o) for o in out]
        return after

    grad_x, small = _local_step(x[0], loss_target[0], gains, weights, Grads(), first_after=(token,))

    after = update("early", 0, len(in_flight) - 1, (grad_x,))
    total = _small_all_reduce(small, after=after)
    update("late", len(in_flight) - 1, len(in_flight), (total,))
    big_out = [big_out[nm] for nm in names]

    gain_out = _adam_gains(total, gains, gain_m, gain_v)
    taps_out = _adam_taps(total, sum(g.shape[1] for g in gains), me.astype(jnp.int32).reshape(1), conv_w, m_conv_w, v_conv_w)
    loss = total[0, total.shape[1] - 1]

    order = (0, "w_in", "conv", 1, "w_uq", 2, "w_ukv", 3, 4, "w_o", 5, 6, "w_up", "w_down", 7)
    by_name = dict(zip(names, big_out))
    outs = [loss, grad_x[None]]
    for kind in range(4):
        for item in order:
            if item == "conv":
                outs.append(taps_out[kind])
            elif isinstance(item, int):
                outs.append(gain_out[item][kind])
            else:
                outs.append(by_name[item][kind])
    return tuple(outs)
```

```python
import math

import jax
import jax.numpy as jnp
from jax import lax
from jax.experimental import pallas as pl
from jax.experimental.pallas import tpu as pltpu

F32 = jnp.float32
BF16 = jnp.bfloat16

EPS = 1e-6
NEG_INF = -1e30
HEAD = 128
ROPE = 64
QK = HEAD + ROPE
CHUNK = 64
ROPE_THETA = 10000.0
ADAM_LR, ADAM_B1, ADAM_B2, ADAM_EPS, ADAM_WD, ADAM_STEP = 0.001, 0.9, 0.999, 1e-08, 0.01, 10

LANE = 128
SUBLANE = 8
VMEM_LIMIT_BYTES = 56 * 1024 * 1024

N_DEV = 8
N_CHIP = 4
MESH = pl.DeviceIdType.MESH


def _params(*sem):
    return pltpu.CompilerParams(dimension_semantics=sem, vmem_limit_bytes=VMEM_LIMIT_BYTES)


ANY = pl.BlockSpec(memory_space=pl.ANY)


def _call(body, *, in_specs, after=(), **kw):
    n_in, n_after = len(in_specs), len(after)

    def ordered(*refs):
        body(*refs[:n_in], *refs[n_in + n_after:])

    call = pl.pallas_call(ordered, in_specs=[*in_specs, *[ANY] * n_after], **kw)
    return lambda *operands: call(*operands, *after)


def _sublane_sum(v):
    r, w = v.shape
    return jnp.sum(v.reshape(r // SUBLANE, SUBLANE, w), axis=0)


def _rstd(x):
    return lax.rsqrt(jnp.mean(x * x, axis=-1, keepdims=True) + EPS)


def _rms_bwd(x, g, dy):
    r = _rstd(x)
    xh = x * r
    dxh = dy * g
    dx = r * (dxh - xh * jnp.mean(dxh * xh, axis=-1, keepdims=True))
    return dx, dy * xh


def _accumulate(ref, val, step):
    @pl.when(step == 0)
    def _():
        ref[...] = val

    @pl.when(step > 0)
    def _():
        ref[...] += val


NN = ((1,), (0,))
NT = ((1,), (1,))
TN = ((0,), (0,))


def _matmul(name, a, b, *, grid, a_spec, b_spec, out_shape, out_specs, contract, nk=1, acc_shape=None,
            extras=(), extra_specs=(), epilogue=None, after=()):
    multi = isinstance(out_shape, (tuple, list))
    out_shapes = tuple(out_shape) if multi else (out_shape,)
    n_out = len(out_shapes)
    n_extra = len(extras)

    def body(a_ref, b_ref, *rest):
        x_refs = rest[:n_extra]
        o_refs = rest[n_extra:n_extra + n_out]

        def emit(acc):
            vals = epilogue(acc, *[r[...] for r in x_refs]) if epilogue else (acc,)
            for r, v in zip(o_refs, vals):
                r[...] = v.astype(r.dtype)

        p = lax.dot_general(a_ref[...], b_ref[...], (contract, ((), ())), preferred_element_type=F32)
        if nk == 1:
            emit(p)
        else:
            acc_ref = rest[n_extra + n_out]
            k = pl.program_id(2)
            _accumulate(acc_ref, p, k)

            @pl.when(k == nk - 1)
            def _():
                emit(acc_ref[...])

    sem = ("parallel", "parallel") + (("arbitrary",) if nk > 1 else ())
    return _call(
        body, name=name, grid=grid, after=after,
        in_specs=[a_spec, b_spec, *extra_specs],
        out_specs=out_specs,
        out_shape=out_shape,
        scratch_shapes=[pltpu.VMEM(acc_shape, F32)] if nk > 1 else [],
        compiler_params=_params(*sem),
    )(a, b, *extras)


def _fit(n, tile):
    if n <= tile:
        return n
    t = tile - tile % LANE
    while n % t:
        t -= LANE
    return t


def _mm_nn(name, a, b, out_dtype, tm, tn, after=()):
    m, k = a.shape
    n = b.shape[1]
    tm, tn = _fit(m, tm), _fit(n, tn)
    return _matmul(name, a, b, grid=(m // tm, n // tn), after=after,
                   a_spec=pl.BlockSpec((tm, k), lambda i, j: (i, 0)),
                   b_spec=pl.BlockSpec((k, tn), lambda i, j: (0, j)),
                   out_shape=jax.ShapeDtypeStruct((m, n), out_dtype),
                   out_specs=pl.BlockSpec((tm, tn), lambda i, j: (i, j)), contract=NN)


def _mm_nt(name, a, b, out_dtype, tm, tn, after=()):
    m, k = a.shape
    n = b.shape[0]
    tm, tn = _fit(m, tm), _fit(n, tn)
    return _matmul(name, a, b, grid=(m // tm, n // tn), after=after,
                   a_spec=pl.BlockSpec((tm, k), lambda i, j: (i, 0)),
                   b_spec=pl.BlockSpec((tn, k), lambda i, j: (j, 0)),
                   out_shape=jax.ShapeDtypeStruct((m, n), out_dtype),
                   out_specs=pl.BlockSpec((tm, tn), lambda i, j: (i, j)), contract=NT)


def _mm_tn(name, a, b, out_dtype, tm, tn):
    s, m = a.shape
    n = b.shape[1]
    tm, tn = _fit(m, tm), _fit(n, tn)
    return _matmul(name, a, b, grid=(m // tm, n // tn),
                   a_spec=pl.BlockSpec((s, tm), lambda i, j: (0, i)),
                   b_spec=pl.BlockSpec((s, tn), lambda i, j: (0, j)),
                   out_shape=jax.ShapeDtypeStruct((m, n), out_dtype),
                   out_specs=pl.BlockSpec((tm, tn), lambda i, j: (i, j)), contract=TN)


ROWS = 256


def _row_spec(rows, width):
    return pl.BlockSpec((rows, width), lambda i: (i, 0))


def _fixed_spec(rows, width):
    return pl.BlockSpec((rows, width), lambda i: (0, 0))


def _column_pieces(rows, start, width):
    piece = math.gcd(start, width)
    assert piece % LANE == 0
    return [pl.BlockSpec((rows, piece), lambda i, b=start // piece + p: (i, b)) for p in range(width // piece)]


def _rms_fwd(name, x, g, after=()):
    s, w = x.shape
    rows = min(ROWS, s)

    def body(x_ref, g_ref, o_ref):
        xv = x_ref[...]
        o_ref[...] = (xv * _rstd(xv) * g_ref[...]).astype(o_ref.dtype)

    return _call(
        body, name=name, grid=(s // rows,), after=after,
        in_specs=[_row_spec(rows, w), _fixed_spec(1, w)],
        out_specs=_row_spec(rows, w),
        out_shape=jax.ShapeDtypeStruct((s, w), BF16),
        compiler_params=_params("parallel"),
    )(x, g)


def _norm_up(name, x, cols, g, w, after=()):
    s = x.shape[0]
    start, width = cols
    n = w.shape[1]
    tm = min(TILE_M, s)
    pieces = _column_pieces(tm, start, width)
    n_p = len(pieces)

    def body(*refs):
        g_ref, w_ref, xn_ref, o_ref = refs[n_p:]
        xv = refs[0][...] if n_p == 1 else jnp.concatenate([r[...] for r in refs[:n_p]], axis=1)
        xn = (xv * _rstd(xv) * g_ref[...]).astype(BF16)
        xn_ref[...] = xn
        o_ref[...] = jnp.dot(xn, w_ref[...], preferred_element_type=F32)

    return _call(
        body, name=name, grid=(s // tm,), after=after,
        in_specs=[*pieces, _fixed_spec(1, width), _fixed_spec(width, n)],
        out_specs=[_row_spec(tm, width), _row_spec(tm, n)],
        out_shape=[jax.ShapeDtypeStruct((s, width), BF16), jax.ShapeDtypeStruct((s, n), F32)],
        compiler_params=_params("parallel"),
    )(*[x] * n_p, g, w)


def _up_norm_bwd(name, dy, w, x, cols, g, into, tail=None, after=()):
    s, n = dy.shape
    start, width = cols
    tm = min(TILE_M, s)
    pieces = _column_pieces(tm, start, width)
    n_p = len(pieces)
    tails = () if tail is None else (tail,)
    out_width = width if tail is None else into.shape[1] - start
    assert start % out_width == 0 and into.dtype == BF16

    def body(dy_ref, w_ref, *refs):
        g_ref = refs[n_p]
        dx_ref, dg_ref = refs[-2:]
        xv = refs[0][...] if n_p == 1 else jnp.concatenate([r[...] for r in refs[:n_p]], axis=1)
        dxn = lax.dot_general(dy_ref[...], w_ref[...], (NT, ((), ())), preferred_element_type=F32)
        dx, dgc = _rms_bwd(xv, g_ref[...], dxn)
        dx_ref[:, :width] = dx.astype(dx_ref.dtype)
        if tails:
            t = refs[n_p + 1][...]
            dx_ref[:, width:width + t.shape[1]] = t
            dx_ref[:, width + t.shape[1]:] = jnp.zeros((tm, out_width - width - t.shape[1]), dx_ref.dtype)
        _accumulate(dg_ref, _sublane_sum(dgc), pl.program_id(0))

    n_in = 3 + n_p + len(tails)
    return _call(
        body, name=name, grid=(s // tm,), after=after,
        in_specs=[_row_spec(tm, n), _fixed_spec(width, n), *pieces, _fixed_spec(1, width)]
                 + [_row_spec(tm, t.shape[1]) for t in tails] + [ANY],
        out_specs=[pl.BlockSpec((tm, out_width), lambda i: (i, start // out_width)), _fixed_spec(SUBLANE, width)],
        out_shape=[jax.ShapeDtypeStruct(into.shape, into.dtype), jax.ShapeDtypeStruct((SUBLANE, width), F32)],
        input_output_aliases={n_in: 0},
        compiler_params=_params("arbitrary"),
    )(dy, w, *[x] * n_p, g, *tails, into)


def _mid_fwd(x, y, g_post, g_pre, after=()):
    s, w = x.shape
    rows = min(ROWS, s)

    def body(x_ref, y_ref, gp_ref, gq_ref, x2_ref, h2_ref):
        yv = y_ref[...]
        x2 = x_ref[...] + yv * _rstd(yv) * gp_ref[...]
        x2_ref[...] = x2
        h2_ref[...] = (x2 * _rstd(x2) * gq_ref[...]).astype(h2_ref.dtype)

    return _call(
        body, name="mid_fwd", grid=(s // rows,), after=after,
        in_specs=[_row_spec(rows, w), _row_spec(rows, w), _fixed_spec(1, w), _fixed_spec(1, w)],
        out_specs=[_row_spec(rows, w), _row_spec(rows, w)],
        out_shape=[jax.ShapeDtypeStruct((s, w), F32), jax.ShapeDtypeStruct((s, w), BF16)],
        compiler_params=_params("parallel"),
    )(x, y, g_post, g_pre)


def _head(m, x2, tgt, g):
    s, w = m.shape
    rows = min(ROWS, s)

    def body(m_ref, x2_ref, t_ref, g_ref, dout_ref, dm_ref, dg_ref, loss_ref):
        mv = m_ref[...]
        gv = g_ref[...]
        out = x2_ref[...] + mv * _rstd(mv) * gv
        err = out - t_ref[...]
        dout = err * (1.0 / w)
        dout_ref[...] = dout
        dm, dgc = _rms_bwd(mv, gv, dout)
        dm_ref[...] = dm.astype(dm_ref.dtype)
        sq = err * err
        lanes = sq[:, 0:LANE]
        for j in range(1, w // LANE):
            lanes = lanes + sq[:, j * LANE:(j + 1) * LANE]
        step = pl.program_id(0)
        _accumulate(dg_ref, _sublane_sum(dgc), step)
        _accumulate(loss_ref, _sublane_sum(lanes) * (0.5 / w), step)

    return pl.pallas_call(
        body, name="head", grid=(s // rows,),
        in_specs=[_row_spec(rows, w), _row_spec(rows, w), _row_spec(rows, w), _fixed_spec(1, w)],
        out_specs=[_row_spec(rows, w), _row_spec(rows, w), _fixed_spec(SUBLANE, w), _fixed_spec(SUBLANE, LANE)],
        out_shape=[jax.ShapeDtypeStruct((s, w), F32), jax.ShapeDtypeStruct((s, w), BF16),
                   jax.ShapeDtypeStruct((SUBLANE, w), F32), jax.ShapeDtypeStruct((SUBLANE, LANE), F32)],
        compiler_params=_params("arbitrary"),
    )(m, x2, tgt, g)


def _mid_bwd(x2, y, d_out, d_h2, g_pre, g_post, after=()):
    s, w = x2.shape
    rows = min(ROWS, s)

    def body(x2_ref, y_ref, dout_ref, dh2_ref, gq_ref, gp_ref, dx2_ref, dy_ref, dgq_ref, dgp_ref):
        dx, dgq = _rms_bwd(x2_ref[...], gq_ref[...], dh2_ref[...])
        dx2 = dout_ref[...] + dx
        dx2_ref[...] = dx2
        dy, dgp = _rms_bwd(y_ref[...], gp_ref[...], dx2)
        dy_ref[...] = dy.astype(dy_ref.dtype)
        step = pl.program_id(0)
        _accumulate(dgq_ref, _sublane_sum(dgq), step)
        _accumulate(dgp_ref, _sublane_sum(dgp), step)

    return _call(
        body, name="mid_bwd", grid=(s // rows,), after=after,
        in_specs=[_row_spec(rows, w)] * 4 + [_fixed_spec(1, w)] * 2,
        out_specs=[_row_spec(rows, w), _row_spec(rows, w), _fixed_spec(SUBLANE, w), _fixed_spec(SUBLANE, w)],
        out_shape=[jax.ShapeDtypeStruct((s, w), F32), jax.ShapeDtypeStruct((s, w), BF16),
                   jax.ShapeDtypeStruct((SUBLANE, w), F32), jax.ShapeDtypeStruct((SUBLANE, w), F32)],
        compiler_params=_params("arbitrary"),
    )(x2, y, d_out, d_h2, g_pre, g_post)


def _first_bwd(x, g, d_h1, d_x2, after=()):
    s, w = x.shape
    rows = min(ROWS, s)

    def body(x_ref, g_ref, dh_ref, dx2_ref, dx_ref, dg_ref):
        dx, dgc = _rms_bwd(x_ref[...], g_ref[...], dh_ref[...])
        dx_ref[...] = dx2_ref[...] + dx
        _accumulate(dg_ref, _sublane_sum(dgc), pl.program_id(0))

    return _call(
        body, name="first_bwd", grid=(s // rows,), after=after,
        in_specs=[_row_spec(rows, w), _fixed_spec(1, w), _row_spec(rows, w), _row_spec(rows, w)],
        out_specs=[_row_spec(rows, w), _fixed_spec(SUBLANE, w)],
        out_shape=[jax.ShapeDtypeStruct((s, w), F32), jax.ShapeDtypeStruct((SUBLANE, w), F32)],
        compiler_params=_params("arbitrary"),
    )(x, g, d_h1, d_x2)


def _shift_down(v, k):
    t = lax.broadcasted_iota(jnp.int32, v.shape, 0)
    return jnp.where(t >= k, pltpu.roll(v, k, 0), 0.0)


def _shift_up(v, k):
    n = v.shape[0]
    t = lax.broadcasted_iota(jnp.int32, v.shape, 0)
    return jnp.where(t < n - k, pltpu.roll(v, n - k, 0), 0.0)


def _conv_core(u, b, c, w):
    z = c * u
    conv = w[0:1, :] * _shift_down(z, 2) + w[1:2, :] * _shift_down(z, 1) + w[2:3, :] * z
    return z, conv, b * conv


def _conv_fwd(proj, conv_w, g, n_groups, out_width, after=()):
    s = proj.shape[0]

    def body(u_ref, b_ref, c_ref, w_ref, g_ref, o_ref):
        _, _, yr = _conv_core(u_ref[...], b_ref[...], c_ref[...], w_ref[...])
        o_ref[...] = (yr * _rstd(yr) * g_ref[...]).astype(o_ref.dtype)

    col = lambda k: pl.BlockSpec((s, HEAD), lambda i: (0, k * n_groups + i))
    return _call(
        body, name="conv_fwd", grid=(n_groups,), after=after,
        in_specs=[col(0), col(1), col(2), pl.BlockSpec((3, HEAD), lambda i: (0, i)), pl.BlockSpec((1, HEAD), lambda i: (0, i))],
        out_specs=pl.BlockSpec((s, HEAD), lambda i: (0, i)),
        out_shape=jax.ShapeDtypeStruct((s, out_width), BF16),
        compiler_params=_params("parallel"),
    )(proj, proj, proj, conv_w, g)


def _conv_bwd(proj, d_mix, conv_w, g, n_groups):
    s = proj.shape[0]
    width = n_groups * HEAD

    def body(u_ref, b_ref, c_ref, dy_ref, w_ref, g_ref, du_ref, db_ref, dc_ref, dg_ref, dw_ref):
        u, b, c, w = u_ref[...], b_ref[...], c_ref[...], w_ref[...]
        z, conv, yr = _conv_core(u, b, c, w)
        dyr, dgc = _rms_bwd(yr, g_ref[...], dy_ref[...])
        dconv = dyr * b
        db_ref[...] = (dyr * conv).astype(db_ref.dtype)
        dz = w[2:3, :] * dconv + w[1:2, :] * _shift_up(dconv, 1) + w[0:1, :] * _shift_up(dconv, 2)
        dc_ref[...] = (dz * u).astype(dc_ref.dtype)
        du_ref[...] = (dz * c).astype(du_ref.dtype)
        dg_ref[...] = _sublane_sum(dgc)
        dw_ref[0] = _sublane_sum(dconv * _shift_down(z, 2))
        dw_ref[1] = _sublane_sum(dconv * _shift_down(z, 1))
        dw_ref[2] = _sublane_sum(dconv * z)

    col = lambda k: pl.BlockSpec((s, HEAD), lambda i: (0, k * n_groups + i))
    grp = pl.BlockSpec((s, HEAD), lambda i: (0, i))
    return pl.pallas_call(
        body, name="conv_bwd", grid=(n_groups,),
        in_specs=[col(0), col(1), col(2), grp, pl.BlockSpec((3, HEAD), lambda i: (0, i)), pl.BlockSpec((1, HEAD), lambda i: (0, i))],
        out_specs=[grp, grp, grp, pl.BlockSpec((SUBLANE, HEAD), lambda i: (0, i)),
                   pl.BlockSpec((3, SUBLANE, HEAD), lambda i: (0, 0, i))],
        out_shape=[jax.ShapeDtypeStruct((s, width), BF16)] * 3
        + [jax.ShapeDtypeStruct((SUBLANE, width), F32), jax.ShapeDtypeStruct((3, SUBLANE, width), F32)],
        compiler_params=_params("parallel"),
    )(proj, proj, proj, d_mix, conv_w, g)


def _rope_tables(s, n_heads):
    pos = jnp.arange(s, dtype=F32)
    inv_freq = jnp.power(ROPE_THETA, -jnp.arange(0, ROPE, 2, dtype=F32) / ROPE)
    ang = pos[:, None] * inv_freq[None, :]
    cos, sin = jnp.cos(ang), jnp.sin(ang)
    cs = jnp.concatenate([cos, cos], axis=1)
    sn = jnp.concatenate([-sin, sin], axis=1)
    pad = jnp.zeros((s, LANE - ROPE), F32)
    return (jnp.tile(cs, (1, n_heads)), jnp.tile(sn, (1, n_heads)),
            jnp.concatenate([cs, pad], axis=1), jnp.concatenate([sn, pad], axis=1))


def _swap_halves(v):
    w = v.shape[1]
    lane = lax.broadcasted_iota(jnp.int32, v.shape, 1)
    first = (lane % ROPE) < (ROPE // 2)
    return jnp.where(first, pltpu.roll(v, w - ROPE // 2, 1), pltpu.roll(v, ROPE // 2, 1))


def _pack_heads(q, kv, proj, kr_col, tables, n_heads, after=()):
    s = q.shape[0]
    rows = min(ROWS, s)
    cq, sq, ck, sk = tables
    wq = n_heads * ROPE

    def body(q_ref, kv_ref, kr_ref, cq_ref, sq_ref, ck_ref, sk_ref, qo_ref, ko_ref, vo_ref):
        qr = q_ref[:, n_heads * HEAD:]
        qr = qr * cq_ref[...] + _swap_halves(qr) * sq_ref[...]
        krv = kr_ref[...]
        krv = krv * ck_ref[...] + _swap_halves(krv) * sk_ref[...]
        for h in range(n_heads):
            qo_ref[h] = jnp.concatenate([q_ref[:, h * HEAD:(h + 1) * HEAD], qr[:, h * ROPE:(h + 1) * ROPE]], axis=1).astype(BF16)
            ko_ref[h] = jnp.concatenate([kv_ref[:, 2 * h * HEAD:(2 * h + 1) * HEAD], krv[:, :ROPE]], axis=1).astype(BF16)
            vo_ref[h] = kv_ref[:, (2 * h + 1) * HEAD:(2 * h + 2) * HEAD].astype(BF16)

    hs = lambda w: pl.BlockSpec((n_heads, rows, w), lambda i: (0, i, 0))
    return _call(
        body, name="pack_heads", grid=(s // rows,), after=after,
        in_specs=[_row_spec(rows, q.shape[1]), _row_spec(rows, kv.shape[1]), pl.BlockSpec((rows, LANE), lambda i: (i, kr_col // LANE)),
                  _row_spec(rows, wq), _row_spec(rows, wq), _row_spec(rows, LANE), _row_spec(rows, LANE)],
        out_specs=[hs(QK), hs(QK), hs(HEAD)],
        out_shape=[jax.ShapeDtypeStruct((n_heads, s, QK), BF16), jax.ShapeDtypeStruct((n_heads, s, QK), BF16),
                   jax.ShapeDtypeStruct((n_heads, s, HEAD), BF16)],
        compiler_params=_params("parallel"),
    )(q, kv, proj, cq, sq, ck, sk)


def _unpack_heads(dq, dk, dv, tables, n_heads):
    s = dq.shape[1]
    rows = min(ROWS, s)
    cq, sq, ck, sk = tables
    wq = n_heads * ROPE

    def body(dq_ref, dk_ref, dv_ref, cq_ref, sq_ref, ck_ref, sk_ref, qo_ref, kvo_ref, kro_ref):
        dqr = jnp.concatenate([dq_ref[h][:, HEAD:] for h in range(n_heads)], axis=1)
        dqr = dqr * cq_ref[...] - _swap_halves(dqr) * sq_ref[...]
        dkr = dk_ref[0][:, HEAD:]
        for h in range(1, n_heads):
            dkr = dkr + dk_ref[h][:, HEAD:]
        dkr = jnp.concatenate([dkr, jnp.zeros((rows, LANE - ROPE), F32)], axis=1)
        dkr = dkr * ck_ref[...] - _swap_halves(dkr) * sk_ref[...]
        kro_ref[...] = dkr.astype(kro_ref.dtype)
        qo_ref[:, n_heads * HEAD:] = dqr.astype(qo_ref.dtype)
        for h in range(n_heads):
            qo_ref[:, h * HEAD:(h + 1) * HEAD] = dq_ref[h][:, :HEAD].astype(qo_ref.dtype)
            kvo_ref[:, 2 * h * HEAD:(2 * h + 1) * HEAD] = dk_ref[h][:, :HEAD].astype(kvo_ref.dtype)
            kvo_ref[:, (2 * h + 1) * HEAD:(2 * h + 2) * HEAD] = dv_ref[h].astype(kvo_ref.dtype)

    hs = lambda w: pl.BlockSpec((n_heads, rows, w), lambda i: (0, i, 0))
    return pl.pallas_call(
        body, name="unpack_heads", grid=(s // rows,),
        in_specs=[hs(QK), hs(QK), hs(HEAD), _row_spec(rows, wq), _row_spec(rows, wq), _row_spec(rows, LANE), _row_spec(rows, LANE)],
        out_specs=[_row_spec(rows, n_heads * QK), _row_spec(rows, 2 * n_heads * HEAD), _row_spec(rows, LANE)],
        out_shape=[jax.ShapeDtypeStruct((s, n_heads * QK), BF16), jax.ShapeDtypeStruct((s, 2 * n_heads * HEAD), BF16),
                   jax.ShapeDtypeStruct((s, LANE), BF16)],
        compiler_params=_params("parallel"),
    )(dq, dk, dv, cq, sq, ck, sk)


TQ = 256


LOG2_E = 1.4426950408889634


def _softmax_parts(q, k):
    tq, n_keys = q.shape[0], k.shape[0]
    sc = lax.dot_general(q, k, (NT, ((), ())), preferred_element_type=F32) * (QK ** -0.5 * LOG2_E)
    row = lax.broadcasted_iota(jnp.int32, (tq, tq), 0)
    col = lax.broadcasted_iota(jnp.int32, (tq, tq), 1)
    own = jnp.where(col // CHUNK <= row // CHUNK, sc[:, n_keys - tq:], NEG_INF)
    sc = own if n_keys == tq else jnp.concatenate([sc[:, :n_keys - tq], own], axis=1)
    e = jnp.exp2(sc - jnp.max(sc, axis=-1, keepdims=True))
    return e, 1.0 / jnp.sum(e, axis=-1, keepdims=True)


def _prob_columns(c, tq):
    return pl.ds(tq * (c * (c + 1) // 2), (c + 1) * tq)


def _attn_fwd(q, k, v, g, mix, col0, first, between):
    n_heads, s, _ = q.shape
    tq = min(TQ, s)
    assert tq % CHUNK == 0 and s % tq == 0
    n_blocks = s // tq
    p_cols = tq * (n_blocks * (n_blocks + 1) // 2)
    out_shape = [jax.ShapeDtypeStruct((n_heads, s, HEAD), F32), jax.ShapeDtypeStruct((n_heads, tq, p_cols), BF16),
                 jax.ShapeDtypeStruct(mix.shape, mix.dtype)]
    done, after = (mix,), ()
    for part, (h0, h1) in enumerate(((0, first), (first, n_heads))):

        def body(q_ref, k_ref, v_ref, g_ref, *rest):
            o_ref, p_ref, y_ref = rest[-3:]
            for c in range(n_blocks):
                rows, n_keys = pl.ds(c * tq, tq), (c + 1) * tq
                e, inv = _softmax_parts(q_ref[rows, :], k_ref[0:n_keys, :])
                p = (e * inv).astype(BF16)
                p_ref[:, _prob_columns(c, tq)] = p
                o = jnp.dot(p, v_ref[0:n_keys, :], preferred_element_type=F32)
                o_ref[rows, :] = o
                y_ref[rows, :] = (o * _rstd(o) * g_ref[...]).astype(y_ref.dtype)

        head = lambda w, h0=h0: pl.BlockSpec((None, s, w), lambda h: (h0 + h, 0, 0))
        n_done = len(done)
        done = _call(
            body, name=f"attn_fwd_{part}", grid=(h1 - h0,), after=after,
            in_specs=[head(QK), head(QK), head(HEAD), pl.BlockSpec((1, HEAD), lambda h, h0=h0: (0, h0 + h))] + [ANY] * n_done,
            out_specs=[head(HEAD), pl.BlockSpec((None, tq, p_cols), lambda h, h0=h0: (h0 + h, 0, 0)),
                       pl.BlockSpec((s, HEAD), lambda h, h0=h0: (0, col0 // HEAD + h0 + h))],
            out_shape=out_shape,
            input_output_aliases={4 + i: 3 - n_done + i for i in range(n_done)},
            compiler_params=_params("parallel"),
        )(q, k, v, g, *done)
        after = between(done) if part == 0 else ()
    return done


def _attn_bwd(q, k, v, o, probs, d_mix, g, col0, after=()):
    n_heads, s, _ = q.shape
    tq = probs.shape[1]

    def body(q_ref, k_ref, v_ref, o_ref, p_ref, dy_ref, g_ref, dq_ref, dk_ref, dv_ref, dg_ref):
        dg = None
        for c in reversed(range(s // tq)):
            rows, n_keys = pl.ds(c * tq, tq), (c + 1) * tq
            o = o_ref[rows, :]
            do, dgc = _rms_bwd(o, g_ref[...], dy_ref[rows, :])
            do = do.astype(BF16)
            dg = _sublane_sum(dgc) if dg is None else dg + _sublane_sum(dgc)
            p = p_ref[:, _prob_columns(c, tq)]
            dp = lax.dot_general(do, v_ref[0:n_keys, :], (NT, ((), ())), preferred_element_type=F32)
            ds = (p.astype(F32) * (dp - jnp.sum(do.astype(F32) * o, axis=-1, keepdims=True))).astype(BF16)
            dq_ref[rows, :] = jnp.dot(ds, k_ref[0:n_keys, :], preferred_element_type=F32) * (QK ** -0.5)
            dk = lax.dot_general(ds, q_ref[rows, :], (TN, ((), ())), preferred_element_type=F32)
            dv = lax.dot_general(p, do, (TN, ((), ())), preferred_element_type=F32)
            if n_keys == s:
                dk_ref[...] = dk
                dv_ref[...] = dv
            else:
                dk_ref[0:n_keys, :] += dk
                dv_ref[0:n_keys, :] += dv
        dk_ref[...] = dk_ref[...] * (QK ** -0.5)
        dg_ref[...] = dg

    c0 = col0 // HEAD
    head = lambda w: pl.BlockSpec((None, s, w), lambda h, *_: (h, 0, 0))
    in_specs = [head(QK), head(QK), head(HEAD), head(HEAD), pl.BlockSpec((None, tq, probs.shape[2]), lambda h, *_: (h, 0, 0)),
                pl.BlockSpec((s, HEAD), lambda h, *_: (0, c0 + h)), pl.BlockSpec((1, HEAD), lambda h, *_: (0, h))]
    out_specs = [head(QK), head(QK), head(HEAD), pl.BlockSpec((SUBLANE, HEAD), lambda h, *_: (0, h))]
    out_shape = [jax.ShapeDtypeStruct((n_heads, s, QK), F32), jax.ShapeDtypeStruct((n_heads, s, QK), F32),
                 jax.ShapeDtypeStruct((n_heads, s, HEAD), F32), jax.ShapeDtypeStruct((SUBLANE, n_heads * HEAD), F32)]
    return _call(body, name="attn_bwd", grid=(n_heads,), after=after, in_specs=in_specs, out_specs=out_specs,
                 out_shape=out_shape, compiler_params=_params("parallel"))(q, k, v, o, probs, d_mix, g)


TILE_M = 1024
TILE_N = 1024


def _up_fwd(h2, w_up, between):
    s, d = h2.shape
    nb, _, fb = w_up.shape
    tm = min(TILE_M, s)
    done, after = (), ()
    for tile in range(s // tm):

        def body(h_ref, w_ref, *rest):
            a_ref, r_ref = rest[-2:]
            r = jnp.maximum(jnp.dot(h_ref[...], w_ref[...], preferred_element_type=F32), 0.0)
            a_ref[...] = (r * r).astype(a_ref.dtype)
            r_ref[...] = r.astype(r_ref.dtype)

        blk = pl.BlockSpec((tm, fb), lambda j, tile=tile: (tile, j))
        done = _call(
            body, name=f"up_fwd_{tile}", grid=(nb,), after=after,
            in_specs=[pl.BlockSpec((tm, d), lambda j, tile=tile: (tile, 0)), pl.BlockSpec((None, d, fb), lambda j: (j, 0, 0))]
                     + [ANY] * len(done),
            out_specs=[blk, blk], out_shape=[jax.ShapeDtypeStruct((s, nb * fb), BF16)] * 2,
            input_output_aliases={2 + i: i for i in range(len(done))},
            compiler_params=_params("parallel"),
        )(h2, w_up, *done)
        after = between(done) if tile == 0 else ()
    return done


def _down_fwd(a, w_down):
    s, f = a.shape
    d = w_down.shape[1]
    tm, tn, tk = min(TILE_M,s), min(TILE_N,d), 2048
    nk = f // tk
    return _matmul("down_fwd", a, w_down, grid=(s // tm, d // tn, nk),
                   a_spec=pl.BlockSpec((tm, tk), lambda i, j, k: (i, k)),
                   b_spec=pl.BlockSpec((tk, tn), lambda i, j, k: (k, j)),
                   out_shape=jax.ShapeDtypeStruct((s, d), F32),
                   out_specs=pl.BlockSpec((tm, tn), lambda i, j, k: (i, j)),
                   contract=NN, nk=nk, acc_shape=(tm, tn))


def _down_bwd_act(d_m, w_down, r, after=()):
    s, d = d_m.shape
    f = w_down.shape[0]
    tm, tn = min(TILE_M,s), min(TILE_N,f)
    blk = pl.BlockSpec((tm, tn), lambda i, j: (i, j))
    return _matmul("down_bwd_act", d_m, w_down, grid=(s // tm, f // tn), after=after,
                   a_spec=pl.BlockSpec((tm, d), lambda i, j: (i, 0)),
                   b_spec=pl.BlockSpec((tn, d), lambda i, j: (j, 0)),
                   out_shape=jax.ShapeDtypeStruct((s, f), BF16), out_specs=blk, contract=NT,
                   extras=(r,), extra_specs=(blk,),
                   epilogue=lambda acc, rv: (acc * (2.0 * rv.astype(F32)),))


def _up_bwd_act(d_up, w_up, after=()):
    s, _ = d_up.shape
    nb, d, fb = w_up.shape
    tm, tn = min(TILE_M, s), min(TILE_N,d)
    pair = 2
    n_after = len(after)

    def body(a_ref, w_ref, *rest):
        o_ref, acc_ref = rest[n_after:]
        k = pl.program_id(2)
        p = None
        for t in range(pair):
            term = lax.dot_general(a_ref[:, t * fb:(t + 1) * fb], w_ref[t], (NT, ((), ())), preferred_element_type=F32)
            p = term if p is None else p + term
        _accumulate(acc_ref, p, k)

        @pl.when(k == nb // pair - 1)
        def _():
            o_ref[...] = acc_ref[...]

    return pl.pallas_call(
        body, name="up_bwd_act", grid=(s // tm, d // tn, nb // pair),
        in_specs=[pl.BlockSpec((tm, pair * fb), lambda i, j, k: (i, k)),
                  pl.BlockSpec((pair, tn, fb), lambda i, j, k: (k, j, 0))] + [ANY] * n_after,
        out_specs=pl.BlockSpec((tm, tn), lambda i, j, k: (i, j)),
        out_shape=jax.ShapeDtypeStruct((s, d), F32),
        scratch_shapes=[pltpu.VMEM((tm, tn), F32)],
        compiler_params=_params("parallel", "parallel", "arbitrary"),
    )(d_up, w_up, *after)


def _half_grad(name, a, b, core, home, received, after, *, grid, a_block, a_map, b_block, b_map, o_block, o_map, out_shape):
    n_after = len(after)
    pick = (lambda ref: ref[0]) if home else (lambda ref: 1 - ref[0])

    def body(core_ref, a_ref, b_ref, *rest):
        acc = lax.dot_general(a_ref[...], b_ref[...], (TN, ((), ())), preferred_element_type=F32)
        if received is not None:
            acc = acc + rest[0][...].astype(F32)
        rest[-1][...] = acc.astype(rest[-1].dtype)

    wrap = lambda fn: (lambda i, j, core_ref: fn(i, j, pick(core_ref)))
    o_spec = pl.BlockSpec(o_block, wrap(o_map))
    extra = [] if received is None else [o_spec]
    operands = [] if received is None else [received]
    return pl.pallas_call(
        body, name=name,
        grid_spec=pltpu.PrefetchScalarGridSpec(
            num_scalar_prefetch=1, grid=grid,
            in_specs=[pl.BlockSpec(a_block, wrap(a_map)), pl.BlockSpec(b_block, wrap(b_map))] + extra + [ANY] * n_after,
            out_specs=o_spec),
        out_shape=out_shape,
        compiler_params=_params("parallel", "parallel"),
    )(core, a, b, *operands, *after)


def _down_half_grad(name, a, d_m, core, home, received=None, after=()):
    s, f = a.shape
    d = d_m.shape[1]
    r = f // N_DEV
    tn = min(TILE_N, d)
    return _half_grad(name, a, d_m, core, home, received, after, grid=(N_CHIP, d // tn),
                      a_block=(s, r), a_map=lambda k, j, p: (0, 2 * k + p),
                      b_block=(s, tn), b_map=lambda k, j, p: (0, j),
                      o_block=(None, r, tn), o_map=lambda k, j, p: (k, 0, j),
                      out_shape=jax.ShapeDtypeStruct((N_CHIP, r, d), BF16))


def _up_half_grad(name, h2, d_up, core, home, received=None, after=()):
    s, d = h2.shape
    fb = d_up.shape[1] // N_DEV
    tm = min(TILE_M, d)
    return _half_grad(name, h2, d_up, core, home, received, after, grid=(d // tm, N_CHIP),
                      a_block=(s, tm), a_map=lambda i, k, p: (0, i),
                      b_block=(s, fb), b_map=lambda i, k, p: (0, 2 * k + p),
                      o_block=(None, tm, fb), o_map=lambda i, k, p: (k, i, 0),
                      out_shape=jax.ShapeDtypeStruct((N_CHIP, d, fb), BF16))


MXU_WIDTH = 256


def _in_pad(in_width):
    return -(-in_width // MXU_WIDTH) * MXU_WIDTH


def _join_col_shards(name, blocks, own, device, pieces=None):
    n, r, w = blocks.shape
    rows = min(ROWS, r)
    pieces = pieces or [(j, 0, w) for j in range(n)]
    used = sum(b - a for _, a, b in pieces)
    width = _in_pad(used)

    def body(dev_ref, x_ref, own_ref, o_ref):
        block = lambda j: jnp.where(dev_ref[0] == j, own_ref[...], x_ref[j])
        cols = [block(j)[:, a:b] for j, a, b in pieces]
        tail = [jnp.zeros((rows, width - used), o_ref.dtype)] if width > used else []
        o_ref[...] = jnp.concatenate(cols + tail, axis=1)

    return pl.pallas_call(
        body, name=name,
        grid_spec=pltpu.PrefetchScalarGridSpec(
            num_scalar_prefetch=1, grid=(r // rows,),
            in_specs=[pl.BlockSpec((n, rows, w), lambda i, dev: (0, i, 0)), pl.BlockSpec((rows, w), lambda i, dev: (i, 0))],
            out_specs=pl.BlockSpec((rows, width), lambda i, dev: (i, 0))),
        out_shape=jax.ShapeDtypeStruct((r, width), blocks.dtype),
        compiler_params=_params("parallel"),
    )(device, blocks, own)


def _unpermute_q_rows(wt, n_heads):
    r = wt.shape[1]
    nope = wt[:n_heads * HEAD].reshape(n_heads, HEAD, r)
    rope = wt[n_heads * HEAD:].reshape(n_heads, ROPE, r)
    return jnp.concatenate([nope, rope], axis=1).reshape(n_heads * QK, r)


def _local_step(x, tgt, gains, weights, grads, first_after=()):
    pre_mix_g, q_norm_g, kv_norm_g, conv_out_g, attn_out_g, post_mix_g, pre_mlp_g, post_mlp_g = gains
    s, d = x.shape
    conv_width = conv_out_g.shape[1]
    n_groups = conv_width // HEAD
    r_q, r_kv = q_norm_g.shape[1], kv_norm_g.shape[1]
    n_heads = attn_out_g.shape[1] // HEAD
    c_q0 = 3 * conv_width
    c_kv0 = c_q0 + r_q
    c_kr0 = c_kv0 + r_kv
    in_pad = _in_pad(c_kr0 + ROPE)
    tn_in = _fit(in_pad, 6 * MXU_WIDTH)
    tables = _rope_tables(s, n_heads)

    h1 = _rms_fwd("pre_mix_norm", x, pre_mix_g, after=first_after)
    weights.forward(0, (h1,))
    weights.relay(0, tables)
    w_in_p, conv_w = weights.ready(0, ())
    proj = _mm_nn("in_proj", h1, w_in_p, F32, TILE_M, tn_in)
    y_conv = _conv_fwd(proj, conv_w, conv_out_g, n_groups, conv_width + n_heads * HEAD, after=weights.forward(1, (proj,)))
    w_uq_p, w_ukv, w_o = weights.ready(1, (y_conv,))
    qn, q = _norm_up("q_up", proj, (c_q0, r_q), q_norm_g, w_uq_p)
    kvn, kv = _norm_up("kv_up", proj, (c_kv0, r_kv), kv_norm_g, w_ukv)
    qh, kh, vh = _pack_heads(q, kv, proj, c_kr0, tables, n_heads)
    o, probs, mix = _attn_fwd(qh, kh, vh, attn_out_g, y_conv, conv_width, n_heads // 4,
                              lambda done: weights.forward(2, tuple(done)))
    y = _mm_nn("out_proj", mix, w_o, F32, TILE_M, TILE_N, after=weights.start(3, (mix,)))
    x2, h2 = _mid_fwd(x, y, post_mix_g, pre_mlp_g)
    weights.relay(2, (h2,))
    (w_up,) = weights.ready(2, ())
    a, r = _up_fwd(h2, w_up, lambda done: weights.forward(3, tuple(done)))
    weights.relay(3, (a,))
    (w_down,) = weights.ready(3, ())
    m = _down_fwd(a, w_down)

    d_out, d_m, dg_post_mlp, loss_part = _head(m, x2, tgt, post_mlp_g)
    core = grads.core
    away = _down_half_grad("down_bwd_w_away", a, d_m, core, home=False)
    d_up = _down_bwd_act(d_m, w_down, r, after=grads.send_away(0, away))
    sums = _down_half_grad("down_bwd_w_home", a, d_m, core, home=True, received=grads.received(0, (d_up,)))
    away = _up_half_grad("up_bwd_w_away", h2, d_up, core, home=False, after=grads.send_sums(0, (sums,)))
    d_h2 = _up_bwd_act(d_up, w_up, after=grads.send_away(1, away))
    sums = _up_half_grad("up_bwd_w_home", h2, d_up, core, home=True, received=grads.received(1, (d_h2,)))
    d_x2, d_y, dg_pre_mlp, dg_post_mix = _mid_bwd(x2, y, d_out, d_h2, pre_mlp_g, post_mix_g, after=grads.send_sums(1, (sums,)))
    d_mix = _mm_nt("out_proj_bwd_act", d_y, w_o, F32, TILE_M, TILE_N)
    gw_o = _mm_tn("out_proj_bwd_w", mix, d_y, BF16, TILE_M, TILE_N)
    dqh, dkh, dvh, dg_attn = _attn_bwd(qh, kh, vh, o, probs, d_mix, attn_out_g, conv_width)
    d_q, d_kv, d_kr = _unpack_heads(dqh, dkh, dvh, tables, n_heads)
    gw_uq_t = _mm_tn("q_up_bwd_w", d_q, qn, F32, TILE_M, TILE_N)
    gw_ukv = _mm_tn("kv_up_bwd_w", kvn, d_kv, BF16, TILE_M, TILE_N)
    d_u, d_b, d_c, dg_conv, dw_conv = _conv_bwd(proj, d_mix, conv_w, conv_out_g, n_groups)
    d_proj = jnp.concatenate([d_u, d_b, d_c, jnp.zeros((s, in_pad - c_q0), BF16)], axis=1)
    d_proj, dg_q = _up_norm_bwd("q_up_bwd_act", d_q, w_uq_p, proj, (c_q0, r_q), q_norm_g, d_proj,
                                after=grads.full(2, (gw_o, gw_uq_t, gw_ukv)))
    d_proj, dg_kv = _up_norm_bwd("kv_up_bwd_act", d_kv, w_ukv, proj, (c_kv0, r_kv), kv_norm_g, d_proj, tail=d_kr)
    gw_in_t = _mm_tn("in_proj_bwd_w", d_proj, h1, F32, tn_in, TILE_N)
    updated = grads.update_now(0, grads.send_away(3, gw_in_t))
    d_h1 = _mm_nt("in_proj_bwd_act", d_proj, w_in_p, F32, TILE_M, TILE_N, after=grads.full(3, (gw_in_t,), received=updated))
    grad_x, dg_pre_mix = _first_bwd(x, pre_mix_g, d_h1, d_x2)

    small = [dg_pre_mix, dg_q, dg_kv, dg_conv, dg_attn, dg_post_mix, dg_pre_mlp, dg_post_mlp,
             dw_conv[0], dw_conv[1], dw_conv[2], loss_part]
    return grad_x, jnp.concatenate(small, axis=1)


HBM = pl.BlockSpec(memory_space=pltpu.HBM)
SEM = pl.BlockSpec(memory_space=pltpu.SEMAPHORE)
IN_VMEM = pl.BlockSpec(memory_space=pltpu.VMEM)
SPLIT = pltpu.CompilerParams(has_side_effects=pltpu.SideEffectType.DATAFLOW_SIDE_EFFECTING)


def _in_hbm(a):
    return pltpu.with_memory_space_constraint(a, pltpu.HBM)


def _hbm_like(a):
    return pltpu.HBM(a.shape, a.dtype)


def _place():
    x, y, c = lax.axis_index("x"), lax.axis_index("y"), lax.axis_index("c")
    other_chips = [(1 - x, y), (x, 1 - y), (1 - x, 1 - y)]
    return x, y, c, other_chips


def _block(px, py, pc):
    return 4 * px + 2 * py + pc


def _await(block, sem):
    pltpu.make_async_copy(block, block, sem).wait()


def _relay_route(x, y, c):
    came_from = ((1 - x) * (1 - c) + x * c, y * (1 - c) + (1 - y) * c)
    goes_to = (x * (1 - c) + (1 - x) * c, (1 - y) * (1 - c) + y * c)
    return came_from, goes_to


def _gather_start(name, shards, groups, relayed=(), after=(), lands=None):
    n, ng = len(shards), len(groups)
    if lands is None:
        lands = [lax.empty((N_DEV, *a.shape), a.dtype) for a in shards]

    def body(*refs):
        src, land = refs[:n], refs[n:2 * n]
        sems, token = refs[2 * n + len(after):2 * n + len(after) + 2 * ng], refs[-1]
        x, y, c, chips = _place()
        targets = [(x, y, 1 - c)] + [(*chip, c) for chip in chips]
        for gi, group in enumerate(groups):
            for i, w in enumerate(group):
                for k, to in enumerate(targets[:3] if gi in relayed else targets):
                    pltpu.make_async_remote_copy(
                        src_ref=src[w], dst_ref=land[w].at[_block(x, y, c)],
                        send_sem=sems[2 * gi].at[4 * i + k], recv_sem=sems[2 * gi + 1].at[4 * i + k],
                        device_id=to, device_id_type=MESH).start()
        token[...] = jnp.zeros_like(token)

    sem_shapes = [pltpu.SemaphoreType.DMA((4 * len(g),)) for g in groups for _ in range(2)]
    out = pl.pallas_call(
        body, name=name,
        in_specs=[HBM] * (2 * n) + [ANY] * len(after),
        out_specs=[SEM] * (2 * ng) + [HBM] * (2 * n) + [IN_VMEM],
        out_shape=sem_shapes + [_hbm_like(a) for a in shards] + [_hbm_like(a) for a in lands]
        + [jax.ShapeDtypeStruct((SUBLANE, LANE), F32)],
        input_output_aliases={i: 2 * ng + i for i in range(2 * n)},
        compiler_params=SPLIT,
    )(*[_in_hbm(a) for a in shards], *[_in_hbm(a) for a in lands], *after)
    sems = [(out[2 * gi], out[2 * gi + 1]) for gi in range(ng)]
    return sems, out[2 * ng:2 * ng + n], out[2 * ng + n:2 * ng + 2 * n], out[-1]


def _gather_forward(name, shards, lands, send1, recv1, after, relayed=False):
    n = len(lands)

    def body(*refs):
        src, land = refs[:n], refs[n:2 * n]
        s1, r1 = refs[2 * n], refs[2 * n + 1]
        s2, r2 = refs[2 * n + 2 + len(after)], refs[2 * n + 3 + len(after)]
        x, y, c, chips = _place()
        me, sibling = (x, y, c), (x, y, 1 - c)
        for j, chip in enumerate(chips[:2] if relayed else chips):
            for i in range(n):
                blk = land[i].at[_block(*chip, c)]
                pltpu.make_async_remote_copy(src_ref=blk, dst_ref=blk, send_sem=s1.at[4 * i + 1 + j], recv_sem=r1.at[4 * i + 1 + j],
                                             device_id=me, device_id_type=MESH).wait_recv()
                pltpu.make_async_remote_copy(src_ref=blk, dst_ref=blk, send_sem=s2.at[3 * i + j], recv_sem=r2.at[3 * i + j],
                                             device_id=sibling, device_id_type=MESH).start()
        if relayed:
            came_from, goes_to = _relay_route(x, y, c)
            for i in range(n):
                blk = land[i].at[_block(*came_from, c)]
                pltpu.make_async_remote_copy(src_ref=blk, dst_ref=blk, send_sem=s2.at[3 * i + 2], recv_sem=r2.at[3 * i + 2],
                                             device_id=(*goes_to, c), device_id_type=MESH).start()
        for i in range(n):
            blk = land[i].at[_block(x, y, 1 - c)]
            pltpu.make_async_remote_copy(src_ref=blk, dst_ref=blk, send_sem=s1.at[4 * i], recv_sem=r1.at[4 * i],
                                         device_id=me, device_id_type=MESH).wait_recv()
            for k in range(3 if relayed else 4):
                pltpu.make_async_remote_copy(src_ref=src[i], dst_ref=land[i].at[_block(x, y, c)], send_sem=s1.at[4 * i + k],
                                             recv_sem=r1.at[4 * i + k], device_id=sibling, device_id_type=MESH).wait_send()

    sem = pltpu.SemaphoreType.DMA((3 * n,))
    out = pl.pallas_call(
        body, name=name,
        in_specs=[HBM] * (2 * n) + [SEM, SEM] + [ANY] * len(after),
        out_specs=[SEM, SEM] + [HBM] * n,
        out_shape=[sem, sem] + [_hbm_like(a) for a in lands],
        input_output_aliases={n + i: 2 + i for i in range(n)},
        compiler_params=SPLIT,
    )(*shards, *lands, send1, recv1, *after)
    return (out[0], out[1]), out[2:]


def _gather_relay_forward(name, lands, send2, recv2, after):
    n = len(lands)

    def body(*refs):
        land, s2, r2 = refs[:n], refs[n], refs[n + 1]
        s3, r3 = refs[n + 2 + len(after)], refs[n + 3 + len(after)]
        x, y, c, _ = _place()
        me, sibling = (x, y, c), (x, y, 1 - c)
        came_from, _ = _relay_route(x, y, c)
        for i in range(n):
            blk = land[i].at[_block(1 - x, 1 - y, c)]
            pltpu.make_async_remote_copy(src_ref=blk, dst_ref=blk, send_sem=s2.at[3 * i + 2], recv_sem=r2.at[3 * i + 2],
                                         device_id=me, device_id_type=MESH).wait_recv()
            pltpu.make_async_remote_copy(src_ref=blk, dst_ref=blk, send_sem=s3.at[i], recv_sem=r3.at[i],
                                         device_id=sibling, device_id_type=MESH).start()
            sent = land[i].at[_block(*came_from, c)]
            pltpu.make_async_remote_copy(src_ref=sent, dst_ref=sent, send_sem=s2.at[3 * i + 2], recv_sem=r2.at[3 * i + 2],
                                         device_id=me, device_id_type=MESH).wait_send()

    sem = pltpu.SemaphoreType.DMA((n,))
    out = pl.pallas_call(
        body, name=name,
        in_specs=[HBM] * n + [SEM, SEM] + [ANY] * len(after),
        out_specs=[SEM, SEM] + [HBM] * n,
        out_shape=[sem, sem] + [_hbm_like(a) for a in lands],
        input_output_aliases={i: 2 + i for i in range(n)},
        compiler_params=SPLIT,
    )(*lands, send2, recv2, *after)
    return (out[0], out[1]), out[2:]


def _gather_wait(name, lands, send2, recv2, after, relay_sems=None):
    n = len(lands)
    n_sems = 2 if relay_sems is None else 4

    def body(*refs):
        land, s2, r2 = refs[:n], refs[n], refs[n + 1]
        for i in range(n):
            for j in range(3 if relay_sems is None else 2):
                _await(land[i].at[0], r2.at[3 * i + j])
                _await(land[i].at[0], s2.at[3 * i + j])
            if relay_sems is not None:
                _await(land[i].at[0], refs[n + 3].at[i])
                _await(land[i].at[0], refs[n + 2].at[i])

    return pl.pallas_call(
        body, name=name,
        in_specs=[HBM] * n + [SEM] * n_sems + [ANY] * len(after), out_specs=[HBM] * n, out_shape=[_hbm_like(a) for a in lands],
        input_output_aliases={i: i for i in range(n)},
        compiler_params=SPLIT,
    )(*lands, send2, recv2, *(relay_sems or ()), *after)


def _pair_exchange(name, grads, shard_rows):
    n = len(grads)
    shapes = [(g.shape[1:] if r is None else (r, g.shape[1])) for g, r in zip(grads, shard_rows)]

    def body(*refs):
        ins, recv = refs[:n], refs[n:2 * n]
        send_sems, recv_sems = refs[2 * n:]
        x, y, c, _ = _place()
        sends = []
        for w in range(n):
            for k in range(N_CHIP):
                j, r = 2 * k + 1 - c, shard_rows[w]
                src = ins[w].at[j] if r is None else ins[w].at[pl.ds(pl.multiple_of(j * r, SUBLANE), r), :]
                sends.append(pltpu.make_async_remote_copy(
                    src_ref=src, dst_ref=recv[w].at[k],
                    send_sem=send_sems.at[w, k], recv_sem=recv_sems.at[w, k],
                    device_id=(x, y, 1 - c), device_id_type=MESH))
        for cp in sends:
            cp.start()
        for cp in sends:
            cp.wait()

    return pl.pallas_call(
        body, name=name,
        in_specs=[ANY] * n, out_specs=[ANY] * n,
        out_shape=[jax.ShapeDtypeStruct((N_CHIP, *shape), g.dtype) for g, shape in zip(grads, shapes)],
        scratch_shapes=[pltpu.SemaphoreType.DMA((n, N_CHIP))] * 2,
    )(*grads)


def _pair_sum_rows(name, grad, received, core):
    _, r, c = received.shape
    tc = _fit(c, 512)

    def body(core_ref, a_ref, b_ref, o_ref):
        o_ref[...] = (a_ref[...] + b_ref[...]).astype(o_ref.dtype)

    spec = pl.BlockSpec((None, r, tc), lambda k, i, core_ref: (k, 0, i))
    return pl.pallas_call(
        body, name=name,
        grid_spec=pltpu.PrefetchScalarGridSpec(
            num_scalar_prefetch=1, grid=(N_CHIP, c // tc),
            in_specs=[pl.BlockSpec((r, tc), lambda k, i, core_ref: (2 * k + core_ref[0], i)), spec],
            out_specs=spec),
        out_shape=jax.ShapeDtypeStruct(received.shape, BF16),
        compiler_params=_params("parallel", "parallel"),
    )(core, grad, received)


def _pair_sum(name, grad, received, core):
    _, r, c = received.shape
    rows = min(ROWS, r)
    assert r % rows == 0

    def body(core_ref, a_ref, b_ref, o_ref):
        o_ref[...] = (a_ref[...].astype(F32) + b_ref[...].astype(F32)).astype(o_ref.dtype)

    spec = pl.BlockSpec((None, rows, c), lambda k, i, core_ref: (k, i, 0))
    return pl.pallas_call(
        body, name=name,
        grid_spec=pltpu.PrefetchScalarGridSpec(
            num_scalar_prefetch=1, grid=(N_CHIP, r // rows),
            in_specs=[pl.BlockSpec((None, None, rows, c), lambda k, i, core_ref: (k, core_ref[0], i, 0)), spec],
            out_specs=spec),
        out_shape=jax.ShapeDtypeStruct(received.shape, received.dtype),
        compiler_params=_params("parallel", "parallel"),
    )(core, grad.reshape(N_CHIP, 2, r, c), received)


def _away_shard(src, k, c, shard_rows):
    if shard_rows is None:
        return src.at[k]
    return src.at[pl.ds(pl.multiple_of((2 * k + 1 - c) * shard_rows, SUBLANE), shard_rows), :]


def _pair_send_start(name, away, shard_rows=None):
    shape = away.shape if shard_rows is None else (N_CHIP, shard_rows, away.shape[1])
    land = lax.empty(shape, away.dtype)

    def body(src, dst, send, recv, src_thru, dst_thru, token):
        x, y, c, _ = _place()
        for k in range(N_CHIP):
            pltpu.make_async_remote_copy(src_ref=_away_shard(src, k, c, shard_rows), dst_ref=dst.at[k], send_sem=send.at[k],
                                         recv_sem=recv.at[k], device_id=(x, y, 1 - c), device_id_type=MESH).start()
        token[...] = jnp.zeros_like(token)

    sem = pltpu.SemaphoreType.DMA((N_CHIP,))
    out = pl.pallas_call(
        body, name=name,
        in_specs=[HBM, HBM], out_specs=[SEM, SEM, HBM, HBM, IN_VMEM],
        out_shape=[sem, sem, _hbm_like(away), _hbm_like(land), jax.ShapeDtypeStruct((SUBLANE, LANE), F32)],
        input_output_aliases={0: 2, 1: 3},
        compiler_params=SPLIT,
    )(_in_hbm(away), _in_hbm(land))
    return (out[0], out[1]), out[2], out[3], out[4]


def _pair_send_wait(name, sems, src, land, after, shard_rows=None):
    def body(src_ref, dst_ref, send, recv, *rest):
        for k in range(N_CHIP):
            _await(dst_ref.at[k], send.at[k])
            _await(dst_ref.at[k], recv.at[k])

    return pl.pallas_call(
        body, name=name,
        in_specs=[HBM, HBM, SEM, SEM] + [ANY] * len(after), out_specs=HBM, out_shape=_hbm_like(land),
        input_output_aliases={1: 0},
        compiler_params=SPLIT,
    )(src, land, *sems, *after)


def _chip_send_start(name, sums):
    n = len(sums)
    lands = [lax.empty(a.shape, a.dtype) for a in sums]

    def body(*refs):
        src, land = refs[:n], refs[n:2 * n]
        send, recv, token = refs[2 * n], refs[2 * n + 1], refs[-1]
        x, y, c, chips = _place()
        for w in range(n):
            for j, (px, py) in enumerate(chips):
                pltpu.make_async_remote_copy(
                    src_ref=src[w].at[2 * px + py], dst_ref=land[w].at[2 * x + y],
                    send_sem=send.at[3 * w + j], recv_sem=recv.at[3 * w + j],
                    device_id=(px, py, c), device_id_type=MESH).start()
        token[...] = jnp.zeros_like(token)

    sem = pltpu.SemaphoreType.DMA((3 * n,))
    out = pl.pallas_call(
        body, name=name,
        in_specs=[HBM] * (2 * n),
        out_specs=[SEM, SEM] + [HBM] * (2 * n) + [IN_VMEM],
        out_shape=[sem, sem] + [_hbm_like(a) for a in sums] + [_hbm_like(a) for a in lands]
        + [jax.ShapeDtypeStruct((SUBLANE, LANE), F32)],
        input_output_aliases={i: 2 + i for i in range(2 * n)},
        compiler_params=SPLIT,
    )(*[_in_hbm(a) for a in sums], *[_in_hbm(a) for a in lands])
    return (out[0], out[1]), out[2:2 + n], out[2 + n:2 + 2 * n], out[-1]


def _chip_send_wait(name, groups, after):
    counts = [len(g[1]) for g in groups]
    n = sum(counts)

    def body(*refs):
        land = refs[n:2 * n]
        sems = refs[2 * n:2 * n + 2 * len(groups)]
        w = 0
        for gi, count in enumerate(counts):
            for i in range(count):
                for j in range(3):
                    _await(land[w].at[0], sems[2 * gi].at[3 * i + j])
                    _await(land[w].at[0], sems[2 * gi + 1].at[3 * i + j])
                w += 1

    sums = [a for g in groups for a in g[1]]
    lands = [a for g in groups for a in g[2]]
    sems = [s for g in groups for s in g[0]]
    return pl.pallas_call(
        body, name=name,
        in_specs=[HBM] * (2 * n) + [SEM] * len(sems) + [ANY] * len(after),
        out_specs=[HBM] * n, out_shape=[_hbm_like(a) for a in lands],
        input_output_aliases={n + i: i for i in range(n)},
        compiler_params=SPLIT,
    )(*sums, *lands, *sems, *after)


def _small_all_reduce(part, after=()):
    _, w = part.shape

    def body(p_ref, *rest):
        o_ref, buf, send_sems, recv_sems = rest[len(after):]
        x, y, c, _ = _place()
        me = 4 * x + 2 * y + c
        buf[me] = jnp.sum(p_ref[...], axis=0, keepdims=True)
        copies = []
        for k in range(1, N_DEV):
            dx, dy, dc = (k >> 2) & 1, (k >> 1) & 1, k & 1
            copies.append(pltpu.make_async_remote_copy(
                src_ref=buf.at[me], dst_ref=buf.at[me], send_sem=send_sems.at[k - 1], recv_sem=recv_sems.at[k - 1],
                device_id=(x ^ dx, y ^ dy, c ^ dc), device_id_type=MESH))
        for cp in copies:
            cp.start()
        for cp in copies:
            cp.wait()
        tot = buf[0]
        for d in range(1, N_DEV):
            tot = tot + buf[d]
        o_ref[...] = tot
        loss = jnp.sum(tot[:, w - LANE:], axis=1, keepdims=True)
        o_ref[:, w - LANE:] = jnp.broadcast_to(loss, (1, LANE))

    return pl.pallas_call(
        body, name="small_all_reduce",
        in_specs=[IN_VMEM] + [ANY] * len(after), out_specs=IN_VMEM,
        out_shape=jax.ShapeDtypeStruct((1, w), F32),
        scratch_shapes=[pltpu.VMEM((N_DEV, 1, w), F32), pltpu.SemaphoreType.DMA((N_DEV - 1,)), pltpu.SemaphoreType.DMA((N_DEV - 1,))],
        compiler_params=pltpu.CompilerParams(vmem_limit_bytes=VMEM_LIMIT_BYTES),
    )(part, *after)


def _adamw(w, g, m, v):
    m = ADAM_B1 * m + (1.0 - ADAM_B1) * g
    v = ADAM_B2 * v + (1.0 - ADAM_B2) * (g * g)
    m_hat = m / (1.0 - ADAM_B1 ** ADAM_STEP)
    v_hat = v / (1.0 - ADAM_B2 ** ADAM_STEP)
    delta = -ADAM_LR * (m_hat / (jnp.sqrt(v_hat) + ADAM_EPS) + ADAM_WD * w)
    return delta, m, v


def _sum_adam_block(chip_ref, p_ref, own_ref, w_ref, m_ref, v_ref, g_ref, d_ref, mo_ref, vo_ref):
    g = None
    for k in range(N_CHIP):
        term = jnp.where(chip_ref[0] == k, own_ref[...], p_ref[k]).astype(F32)
        g = term if g is None else g + term
    g_ref[...] = g
    d_ref[...], mo_ref[...], vo_ref[...] = _adamw(w_ref[...], g, m_ref[...], v_ref[...])


def _sum_adam(name, parts, sums, chip, w, m, v, after=()):
    _, r, c = w.shape
    n_after = len(after)
    by_rows = r % ROWS == 0 or r < ROWS
    tr, tc = (min(ROWS, r), c) if by_rows else (r, _fit(c, 512))
    at = (lambda i: (i, 0)) if by_rows else (lambda i: (0, i))

    def body(chip_ref, p_ref, own_ref, w_ref, m_ref, v_ref, *rest):
        _sum_adam_block(chip_ref, p_ref, own_ref, w_ref, m_ref, v_ref, *rest[n_after:])

    blk = pl.BlockSpec((None, tr, tc), lambda i, chip_ref: (0, *at(i)))
    out = jax.ShapeDtypeStruct((1, r, c), F32)
    return pl.pallas_call(
        body, name=name,
        grid_spec=pltpu.PrefetchScalarGridSpec(
            num_scalar_prefetch=1, grid=(r // tr if by_rows else c // tc,),
            in_specs=[pl.BlockSpec((N_CHIP, tr, tc), lambda i, chip_ref: (0, *at(i))),
                      pl.BlockSpec((None, tr, tc), lambda i, chip_ref: (chip_ref[0], *at(i))), blk, blk, blk]
            + [ANY] * n_after,
            out_specs=[blk] * 4),
        out_shape=[out] * 4,
        compiler_params=_params("parallel"),
    )(chip, parts, sums, w, m, v, *after)


def _adam_gains(total, ws, ms, vs):
    n = len(ws)
    widths = [w.shape[1] for w in ws]

    def body(t_ref, *refs):
        w_refs, m_refs, v_refs, outs = refs[:n], refs[n:2 * n], refs[2 * n:3 * n], refs[3 * n:]
        off = 0
        for i in range(n):
            g = t_ref[:, off:off + widths[i]]
            off += widths[i]
            g_ref, d_ref, mo_ref, vo_ref = outs[4 * i:4 * i + 4]
            g_ref[...] = g
            d_ref[...], mo_ref[...], vo_ref[...] = _adamw(w_refs[i][...], g, m_refs[i][...], v_refs[i][...])

    out = pl.pallas_call(
        body, name="adam_gains",
        out_shape=[jax.ShapeDtypeStruct(w.shape, F32) for w in ws for _ in range(4)],
    )(total, *ws, *ms, *vs)
    return [tuple(out[4 * i:4 * i + 4]) for i in range(n)]


def _adam_taps(total, first_col, device, w, m, v):
    _, n_taps, cw = w.shape
    col_block = lambda t, dev: (0, first_col // cw + t * N_DEV + dev[0])
    tap = pl.BlockSpec((None, 1, cw), lambda t, dev: (t, 0, 0))

    def body(dev_ref, t_ref, w_ref, m_ref, v_ref, g_ref, d_ref, mo_ref, vo_ref):
        g = t_ref[...]
        g_ref[...] = g
        d_ref[...], mo_ref[...], vo_ref[...] = _adamw(w_ref[...], g, m_ref[...], v_ref[...])

    shape3 = (n_taps, 1, cw)
    out = pl.pallas_call(
        body, name="adam_taps",
        grid_spec=pltpu.PrefetchScalarGridSpec(
            num_scalar_prefetch=1, grid=(n_taps,),
            in_specs=[pl.BlockSpec((1, cw), col_block), tap, tap, tap], out_specs=[tap] * 4),
        out_shape=[jax.ShapeDtypeStruct(shape3, F32)] * 4,
    )(device, total, w.reshape(shape3), m.reshape(shape3), v.reshape(shape3))
    return tuple(o.reshape(w.shape) for o in out)


def kernel(x, pre_mix_g, w_in, conv_w, q_norm_g, w_uq, kv_norm_g, w_ukv, conv_out_g, attn_out_g, w_o, post_mix_g, pre_mlp_g, w_up, w_down, post_mlp_g, loss_target, m_pre_mix_g, m_w_in, m_conv_w, m_q_norm_g, m_w_uq, m_kv_norm_g, m_w_ukv, m_conv_out_g, m_attn_out_g, m_w_o, m_post_mix_g, m_pre_mlp_g, m_w_up, m_w_down, m_post_mlp_g, v_pre_mix_g, v_w_in, v_conv_w, v_q_norm_g, v_w_uq, v_kv_norm_g, v_w_ukv, v_conv_out_g, v_attn_out_g, v_w_o, v_post_mix_g, v_pre_mlp_g, v_w_up, v_w_down, v_post_mlp_g):
    me = 4 * lax.axis_index("x") + 2 * lax.axis_index("y") + lax.axis_index("c")
    core = lax.axis_index("c").astype(jnp.int32).reshape(1)
    chip = (2 * lax.axis_index("x") + lax.axis_index("y")).astype(jnp.int32).reshape(1)
    gains = (pre_mix_g, q_norm_g, kv_norm_g, conv_out_g, attn_out_g, post_mix_g, pre_mlp_g, post_mlp_g)
    gain_m = (m_pre_mix_g, m_q_norm_g, m_kv_norm_g, m_conv_out_g, m_attn_out_g, m_post_mix_g, m_pre_mlp_g, m_post_mlp_g)
    gain_v = (v_pre_mix_g, v_q_norm_g, v_kv_norm_g, v_conv_out_g, v_attn_out_g, v_post_mix_g, v_pre_mlp_g, v_post_mlp_g)
    names = ("w_in", "w_uq", "w_ukv", "w_o", "w_up", "w_down")
    big = dict(zip(names, (w_in, w_uq, w_ukv, w_o, w_up, w_down)))
    big_m = dict(zip(names, (m_w_in, m_w_uq, m_w_ukv, m_w_o, m_w_up, m_w_down)))
    big_v = dict(zip(names, (v_w_in, v_w_uq, v_w_ukv, v_w_o, v_w_up, v_w_down)))
    n_heads = attn_out_g.shape[1] // HEAD
    n_taps = conv_w.shape[1]

    gathered = ("w_in", "conv", "w_uq", "w_ukv", "w_o", "w_up", "w_down")
    gather_groups = ((0, 1), (2, 3, 4), (5,), (6,))
    taps = jnp.pad(conv_w[0], ((0, SUBLANE - n_taps), (0, 0)))
    relayed_groups = (0, 2, 3)
    sems1, shards, lands, token = _gather_start("gather_start_first", [w_in[0].astype(BF16), taps], ((0, 1),), relayed=(0,))
    sems1, shards, lands = list(sems1), list(shards), list(lands)
    behind = token[0, 0]
    rest = [(big[nm][0] + behind).astype(BF16) for nm in gathered[2:]]

    rest_lands = [lax.dynamic_update_index_in_dim(lax.empty((N_DEV, *a.shape), a.dtype), a, me, 0) for a in rest]

    def start_more(name, some, groups, relayed, after):
        sems_b, shards_b, lands_b, started = _gather_start(name, rest[some], groups, relayed=relayed, after=after,
                                                           lands=rest_lands[some])
        sems1.extend(sems_b)
        shards.extend(shards_b)
        lands.extend(lands_b)
        return started

    start_rest = lambda after: start_more("gather_start_rest", slice(0, 4), ((0, 1, 2), (3,)), (1,), after)
    start_last = lambda after: start_more("gather_start_last", slice(4, 5), ((0,),), (0,), after)

    cols = lambda a: jnp.concatenate([a[j] for j in range(N_DEV)], axis=1)
    rows = lambda a: a.reshape(N_DEV * a.shape[1], a.shape[2])
    device = me.astype(jnp.int32).reshape(1)
    own_in = lambda a, shard: lax.dynamic_update_index_in_dim(a, shard, me, 0)
    q_pieces = [(h, 0, HEAD) for h in range(n_heads)] + [(h, HEAD, QK) for h in range(n_heads)]
    ready = {
        "w_in": lambda a, shard: _join_col_shards("join_w_in", a, shard, device),
        "conv": lambda a, shard: cols(own_in(a, shard))[:n_taps],
        "w_uq": lambda a, shard: _join_col_shards("join_w_uq", a, shard, device, q_pieces),
        "w_ukv": lambda a, shard: cols(a),
        "w_o": lambda a, shard: rows(a),
        "w_up": lambda a, shard: a,
        "w_down": lambda a, shard: rows(a),
    }
    assert w_uq.shape[2] == QK

    class Weights:
        def __init__(self):
            self.passed, self.relayed = {}, {}

        def forward(self, group, after):
            idx = gather_groups[group]
            if group == 0:
                after = (*after, *rest_lands)
            self.passed[group] = _gather_forward(f"gather_forward_{group}", [shards[i] for i in idx], [lands[i] for i in idx],
                                                 *sems1[group], after, relayed=group in relayed_groups)
            return tuple(self.passed[group][1])

        def start(self, group, after):
            assert group == len(gather_groups) - 1
            return (start_last(after),)

        def relay(self, group, after):
            sems2, mid = self.passed[group]
            self.relayed[group], mid = _gather_relay_forward(f"gather_relay_{group}", mid, *sems2, after)
            self.passed[group] = (sems2, mid)
            if group == 0:
                start_rest(tuple(mid))
            return tuple(mid)

        def ready(self, group, after):
            sems2, mid = self.passed[group]
            full = _gather_wait(f"gather_wait_{group}", mid, *sems2, after, relay_sems=self.relayed.get(group))
            out = []
            return [ready[gathered[i]](a, shards[i]) for i, a in zip(gather_groups[group], full)]

    weights = Weights()

    col_blocks = lambda g: g.reshape(g.shape[0], N_DEV, g.shape[1] // N_DEV).transpose(1, 0, 2)
    row_blocks = lambda g: g.reshape(N_DEV, g.shape[0] // N_DEV, g.shape[1])
    grad_groups = (("w_down",), ("w_up",), ("w_o", "w_uq", "w_ukv"), ("w_in",))
    transposed = {"w_in": w_in.shape[2], "w_uq": w_uq.shape[2]}
    to_blocks = {
        "w_in": lambda g: g, "w_uq": lambda g: _unpermute_q_rows(g, n_heads),
        "w_ukv": col_blocks, "w_o": row_blocks, "w_up": lambda g: g, "w_down": row_blocks,
    }
    in_flight = []

    class Grads:
        def __init__(self):
            self.core = core
            self.away = {}

        def send_sums(self, group, sums):
            sems, sums, parts, tok = _chip_send_start(f"chip_send_start_{group}", list(sums))
            in_flight.append((sems, sums, parts))
            return (tok,)

        def full(self, group, arrays, received=None):
            nms = grad_groups[group]
            if received is None:
                blocks = [to_blocks[nm](g) for nm, g in zip(nms, arrays)]
                got = _pair_exchange(f"pair_exchange_{group}", blocks, [transposed.get(nm) for nm in nms])
            else:
                blocks, got = [self.away[group][1]], [self.received(group, received)]
            sums = [(_pair_sum_rows if nm in transposed else _pair_sum)(f"pair_sum_{nm}", g, r, core)
                    for nm, g, r in zip(nms, blocks, got)]
            return self.send_sums(group, sums)

        def send_away(self, group, half):
            nm = grad_groups[group][0]
            rows = transposed.get(nm)
            sems, src, land, tok = _pair_send_start(f"pair_send_start_{group}", half if rows is None else to_blocks[nm](half), rows)
            self.away[group] = (sems, src, land, rows)
            return (tok,)

        def received(self, group, after):
            sems, src, land, rows = self.away[group]
            return _pair_send_wait(f"pair_send_wait_{group}", sems, src, land, after, rows)

        def update_now(self, group, after):
            return update(str(group), group, group + 1, after)

    big_out = {}

    def update(tag, first, last, after):
        picked = [i for i in range(first, last) if grad_groups[i][0] not in big_out]
        groups = [in_flight[i] for i in picked]
        parts = _chip_send_wait("chip_send_wait_" + tag, groups, after)
        nms = [nm for i in picked for nm in grad_groups[i]]
        sums = [a for _, s, _ in groups for a in s]
        for nm, p, s in zip(nms, parts, sums):
            view = (lambda a: jnp.swapaxes(a, 1, 2)) if nm in transposed else (lambda a: a)
            out = _sum_adam("adam_" + nm, p, s, chip, view(big[nm]), view(big_m[nm]), view(big_v[nm]), after=after)
            after = (out[0],)
            big_out[nm] = [view(o) for o in out]
        return after

    grad_x, small = _local_step(x[0], loss_target[0], gains, weights, Grads(), first_after=(token,))

    after = update("early", 0, len(in_flight) - 1, (grad_x,))
    total = _small_all_reduce(small, after=after)
    update("late", len(in_flight) - 1, len(in_flight), (total,))
    big_out = [big_out[nm] for nm in names]

    gain_out = _adam_gains(total, gains, gain_m, gain_v)
    taps_out = _adam_taps(total, sum(g.shape[1] for g in gains), me.astype(jnp.int32).reshape(1), conv_w, m_conv_w, v_conv_w)
    loss = total[0, total.shape[1] - 1]

    order = (0, "w_in", "conv", 1, "w_uq", 2, "w_ukv", 3, 4, "w_o", 5, 6, "w_up", "w_down", 7)
    by_name = dict(zip(names, big_out))
    outs = [loss, grad_x[None]]
    for kind in range(4):
        for item in order:
            if item == "conv":
                outs.append(taps_out[kind])
            elif isinstance(item, int):
                outs.append(gain_out[item][kind])
            else:
                outs.append(by_name[item][kind])
    return tuple(outs)
```

```python
import math

import jax
import jax.numpy as jnp
from jax import lax
from jax.experimental import pallas as pl
from jax.experimental.pallas import tpu as pltpu

F32 = jnp.float32
BF16 = jnp.bfloat16

EPS = 1e-6
NEG_INF = -1e30
HEAD = 128
ROPE = 64
QK = HEAD + ROPE
CHUNK = 64
ROPE_THETA = 10000.0
ADAM_LR, ADAM_B1, ADAM_B2, ADAM_EPS, ADAM_WD, ADAM_STEP = 0.001, 0.9, 0.999, 1e-08, 0.01, 10

LANE = 128
SUBLANE = 8
VMEM_LIMIT_BYTES = 56 * 1024 * 1024

N_DEV = 8
N_CHIP = 4
MESH = pl.DeviceIdType.MESH


def _params(*sem):
    return pltpu.CompilerParams(dimension_semantics=sem, vmem_limit_bytes=VMEM_LIMIT_BYTES)


ANY = pl.BlockSpec(memory_space=pl.ANY)


def _call(body, *, in_specs, after=(), **kw):
    n_in, n_after = len(in_specs), len(after)

    def ordered(*refs):
        body(*refs[:n_in], *refs[n_in + n_after:])

    call = pl.pallas_call(ordered, in_specs=[*in_specs, *[ANY] * n_after], **kw)
    return lambda *operands: call(*operands, *after)


def _sublane_sum(v):
    r, w = v.shape
    return jnp.sum(v.reshape(r // SUBLANE, SUBLANE, w), axis=0)


def _rstd(x):
    return lax.rsqrt(jnp.mean(x * x, axis=-1, keepdims=True) + EPS)


def _rms_bwd(x, g, dy):
    r = _rstd(x)
    xh = x * r
    dxh = dy * g
    dx = r * (dxh - xh * jnp.mean(dxh * xh, axis=-1, keepdims=True))
    return dx, dy * xh


def _accumulate(ref, val, step):
    @pl.when(step == 0)
    def _():
        ref[...] = val

    @pl.when(step > 0)
    def _():
        ref[...] += val


NN = ((1,), (0,))
NT = ((1,), (1,))
TN = ((0,), (0,))


def _matmul(name, a, b, *, grid, a_spec, b_spec, out_shape, out_specs, contract, nk=1, acc_shape=None,
            extras=(), extra_specs=(), epilogue=None, after=()):
    multi = isinstance(out_shape, (tuple, list))
    out_shapes = tuple(out_shape) if multi else (out_shape,)
    n_out = len(out_shapes)
    n_extra = len(extras)

    def body(a_ref, b_ref, *rest):
        x_refs = rest[:n_extra]
        o_refs = rest[n_extra:n_extra + n_out]

        def emit(acc):
            vals = epilogue(acc, *[r[...] for r in x_refs]) if epilogue else (acc,)
            for r, v in zip(o_refs, vals):
                r[...] = v.astype(r.dtype)

        p = lax.dot_general(a_ref[...], b_ref[...], (contract, ((), ())), preferred_element_type=F32)
        if nk == 1:
            emit(p)
        else:
            acc_ref = rest[n_extra + n_out]
            k = pl.program_id(2)
            _accumulate(acc_ref, p, k)

            @pl.when(k == nk - 1)
            def _():
                emit(acc_ref[...])

    sem = ("parallel", "parallel") + (("arbitrary",) if nk > 1 else ())
    return _call(
        body, name=name, grid=grid, after=after,
        in_specs=[a_spec, b_spec, *extra_specs],
        out_specs=out_specs,
        out_shape=out_shape,
        scratch_shapes=[pltpu.VMEM(acc_shape, F32)] if nk > 1 else [],
        compiler_params=_params(*sem),
    )(a, b, *extras)


def _fit(n, tile):
    if n <= tile:
        return n
    t = tile - tile % LANE
    while n % t:
        t -= LANE
    return t


def _mm_nn(name, a, b, out_dtype, tm, tn, after=()):
    m, k = a.shape
    n = b.shape[1]
    tm, tn = _fit(m, tm), _fit(n, tn)
    return _matmul(name, a, b, grid=(m // tm, n // tn), after=after,
                   a_spec=pl.BlockSpec((tm, k), lambda i, j: (i, 0)),
                   b_spec=pl.BlockSpec((k, tn), lambda i, j: (0, j)),
                   out_shape=jax.ShapeDtypeStruct((m, n), out_dtype),
                   out_specs=pl.BlockSpec((tm, tn), lambda i, j: (i, j)), contract=NN)


def _mm_nt(name, a, b, out_dtype, tm, tn, after=()):
    m, k = a.shape
    n = b.shape[0]
    tm, tn = _fit(m, tm), _fit(n, tn)
    return _matmul(name, a, b, grid=(m // tm, n // tn), after=after,
                   a_spec=pl.BlockSpec((tm, k), lambda i, j: (i, 0)),
                   b_spec=pl.BlockSpec((tn, k), lambda i, j: (j, 0)),
                   out_shape=jax.ShapeDtypeStruct((m, n), out_dtype),
                   out_specs=pl.BlockSpec((tm, tn), lambda i, j: (i, j)), contract=NT)


def _mm_tn(name, a, b, out_dtype, tm, tn):
    s, m = a.shape
    n = b.shape[1]
    tm, tn = _fit(m, tm), _fit(n, tn)
    return _matmul(name, a, b, grid=(m // tm, n // tn),
                   a_spec=pl.BlockSpec((s, tm), lambda i, j: (0, i)),
                   b_spec=pl.BlockSpec((s, tn), lambda i, j: (0, j)),
                   out_shape=jax.ShapeDtypeStruct((m, n), out_dtype),
                   out_specs=pl.BlockSpec((tm, tn), lambda i, j: (i, j)), contract=TN)


ROWS = 256


def _row_spec(rows, width):
    return pl.BlockSpec((rows, width), lambda i: (i, 0))


def _fixed_spec(rows, width):
    return pl.BlockSpec((rows, width), lambda i: (0, 0))


def _column_pieces(rows, start, width):
    piece = math.gcd(start, width)
    assert piece % LANE == 0
    return [pl.BlockSpec((rows, piece), lambda i, b=start // piece + p: (i, b)) for p in range(width // piece)]


def _rms_fwd(name, x, g, after=()):
    s, w = x.shape
    rows = min(ROWS, s)

    def body(x_ref, g_ref, o_ref):
        xv = x_ref[...]
        o_ref[...] = (xv * _rstd(xv) * g_ref[...]).astype(o_ref.dtype)

    return _call(
        body, name=name, grid=(s // rows,), after=after,
        in_specs=[_row_spec(rows, w), _fixed_spec(1, w)],
        out_specs=_row_spec(rows, w),
        out_shape=jax.ShapeDtypeStruct((s, w), BF16),
        compiler_params=_params("parallel"),
    )(x, g)


def _norm_up(name, x, cols, g, w, after=()):
    s = x.shape[0]
    start, width = cols
    n = w.shape[1]
    tm = min(TILE_M, s)
    pieces = _column_pieces(tm, start, width)
    n_p = len(pieces)

    def body(*refs):
        g_ref, w_ref, xn_ref, o_ref = refs[n_p:]
        xv = refs[0][...] if n_p == 1 else jnp.concatenate([r[...] for r in refs[:n_p]], axis=1)
        xn = (xv * _rstd(xv) * g_ref[...]).astype(BF16)
        xn_ref[...] = xn
        o_ref[...] = jnp.dot(xn, w_ref[...], preferred_element_type=F32)

    return _call(
        body, name=name, grid=(s // tm,), after=after,
        in_specs=[*pieces, _fixed_spec(1, width), _fixed_spec(width, n)],
        out_specs=[_row_spec(tm, width), _row_spec(tm, n)],
        out_shape=[jax.ShapeDtypeStruct((s, width), BF16), jax.ShapeDtypeStruct((s, n), F32)],
        compiler_params=_params("parallel"),
    )(*[x] * n_p, g, w)


def _up_norm_bwd(name, dy, w, x, cols, g, into, tail=None, after=()):
    s, n = dy.shape
    start, width = cols
    tm = min(TILE_M, s)
    pieces = _column_pieces(tm, start, width)
    n_p = len(pieces)
    tails = () if tail is None else (tail,)
    out_width = width if tail is None else into.shape[1] - start
    assert start % out_width == 0 and into.dtype == BF16

    def body(dy_ref, w_ref, *refs):
        g_ref = refs[n_p]
        dx_ref, dg_ref = refs[-2:]
        xv = refs[0][...] if n_p == 1 else jnp.concatenate([r[...] for r in refs[:n_p]], axis=1)
        dxn = lax.dot_general(dy_ref[...], w_ref[...], (NT, ((), ())), preferred_element_type=F32)
        dx, dgc = _rms_bwd(xv, g_ref[...], dxn)
        dx_ref[:, :width] = dx.astype(dx_ref.dtype)
        if tails:
            t = refs[n_p + 1][...]
            dx_ref[:, width:width + t.shape[1]] = t
            dx_ref[:, width + t.shape[1]:] = jnp.zeros((tm, out_width - width - t.shape[1]), dx_ref.dtype)
        _accumulate(dg_ref, _sublane_sum(dgc), pl.program_id(0))

    n_in = 3 + n_p + len(tails)
    return _call(
        body, name=name, grid=(s // tm,), after=after,
        in_specs=[_row_spec(tm, n), _fixed_spec(width, n), *pieces, _fixed_spec(1, width)]
                 + [_row_spec(tm, t.shape[1]) for t in tails] + [ANY],
        out_specs=[pl.BlockSpec((tm, out_width), lambda i: (i, start // out_width)), _fixed_spec(SUBLANE, width)],
        out_shape=[jax.ShapeDtypeStruct(into.shape, into.dtype), jax.ShapeDtypeStruct((SUBLANE, width), F32)],
        input_output_aliases={n_in: 0},
        compiler_params=_params("arbitrary"),
    )(dy, w, *[x] * n_p, g, *tails, into)


def _mid_fwd(x, y, g_post, g_pre, after=()):
    s, w = x.shape
    rows = min(ROWS, s)

    def body(x_ref, y_ref, gp_ref, gq_ref, x2_ref, h2_ref):
        yv = y_ref[...]
        x2 = x_ref[...] + yv * _rstd(yv) * gp_ref[...]
        x2_ref[...] = x2
        h2_ref[...] = (x2 * _rstd(x2) * gq_ref[...]).astype(h2_ref.dtype)

    return _call(
        body, name="mid_fwd", grid=(s // rows,), after=after,
        in_specs=[_row_spec(rows, w), _row_spec(rows, w), _fixed_spec(1, w), _fixed_spec(1, w)],
        out_specs=[_row_spec(rows, w), _row_spec(rows, w)],
        out_shape=[jax.ShapeDtypeStruct((s, w), F32), jax.ShapeDtypeStruct((s, w), BF16)],
        compiler_params=_params("parallel"),
    )(x, y, g_post, g_pre)


def _head(m, x2, tgt, g):
    s, w = m.shape
    rows = min(ROWS, s)

    def body(m_ref, x2_ref, t_ref, g_ref, dout_ref, dm_ref, dg_ref, loss_ref):
        mv = m_ref[...]
        gv = g_ref[...]
        out = x2_ref[...] + mv * _rstd(mv) * gv
        err = out - t_ref[...]
        dout = err * (1.0 / w)
        dout_ref[...] = dout
        dm, dgc = _rms_bwd(mv, gv, dout)
        dm_ref[...] = dm.astype(dm_ref.dtype)
        sq = err * err
        lanes = sq[:, 0:LANE]
        for j in range(1, w // LANE):
            lanes = lanes + sq[:, j * LANE:(j + 1) * LANE]
        step = pl.program_id(0)
        _accumulate(dg_ref, _sublane_sum(dgc), step)
        _accumulate(loss_ref, _sublane_sum(lanes) * (0.5 / w), step)

    return pl.pallas_call(
        body, name="head", grid=(s // rows,),
        in_specs=[_row_spec(rows, w), _row_spec(rows, w), _row_spec(rows, w), _fixed_spec(1, w)],
        out_specs=[_row_spec(rows, w), _row_spec(rows, w), _fixed_spec(SUBLANE, w), _fixed_spec(SUBLANE, LANE)],
        out_shape=[jax.ShapeDtypeStruct((s, w), F32), jax.ShapeDtypeStruct((s, w), BF16),
                   jax.ShapeDtypeStruct((SUBLANE, w), F32), jax.ShapeDtypeStruct((SUBLANE, LANE), F32)],
        compiler_params=_params("arbitrary"),
    )(m, x2, tgt, g)


def _mid_bwd(x2, y, d_out, d_h2, g_pre, g_post, after=()):
    s, w = x2.shape
    rows = min(ROWS, s)

    def body(x2_ref, y_ref, dout_ref, dh2_ref, gq_ref, gp_ref, dx2_ref, dy_ref, dgq_ref, dgp_ref):
        dx, dgq = _rms_bwd(x2_ref[...], gq_ref[...], dh2_ref[...])
        dx2 = dout_ref[...] + dx
        dx2_ref[...] = dx2
        dy, dgp = _rms_bwd(y_ref[...], gp_ref[...], dx2)
        dy_ref[...] = dy.astype(dy_ref.dtype)
        step = pl.program_id(0)
        _accumulate(dgq_ref, _sublane_sum(dgq), step)
        _accumulate(dgp_ref, _sublane_sum(dgp), step)

    return _call(
        body, name="mid_bwd", grid=(s // rows,), after=after,
        in_specs=[_row_spec(rows, w)] * 4 + [_fixed_spec(1, w)] * 2,
        out_specs=[_row_spec(rows, w), _row_spec(rows, w), _fixed_spec(SUBLANE, w), _fixed_spec(SUBLANE, w)],
        out_shape=[jax.ShapeDtypeStruct((s, w), F32), jax.ShapeDtypeStruct((s, w), BF16),
                   jax.ShapeDtypeStruct((SUBLANE, w), F32), jax.ShapeDtypeStruct((SUBLANE, w), F32)],
        compiler_params=_params("arbitrary"),
    )(x2, y, d_out, d_h2, g_pre, g_post)


def _first_bwd(x, g, d_h1, d_x2, after=()):
    s, w = x.shape
    rows = min(ROWS, s)

    def body(x_ref, g_ref, dh_ref, dx2_ref, dx_ref, dg_ref):
        dx, dgc = _rms_bwd(x_ref[...], g_ref[...], dh_ref[...])
        dx_ref[...] = dx2_ref[...] + dx
        _accumulate(dg_ref, _sublane_sum(dgc), pl.program_id(0))

    return _call(
        body, name="first_bwd", grid=(s // rows,), after=after,
        in_specs=[_row_spec(rows, w), _fixed_spec(1, w), _row_spec(rows, w), _row_spec(rows, w)],
        out_specs=[_row_spec(rows, w), _fixed_spec(SUBLANE, w)],
        out_shape=[jax.ShapeDtypeStruct((s, w), F32), jax.ShapeDtypeStruct((SUBLANE, w), F32)],
        compiler_params=_params("arbitrary"),
    )(x, g, d_h1, d_x2)


def _shift_down(v, k):
    t = lax.broadcasted_iota(jnp.int32, v.shape, 0)
    return jnp.where(t >= k, pltpu.roll(v, k, 0), 0.0)


def _shift_up(v, k):
    n = v.shape[0]
    t = lax.broadcasted_iota(jnp.int32, v.shape, 0)
    return jnp.where(t < n - k, pltpu.roll(v, n - k, 0), 0.0)


def _conv_core(u, b, c, w):
    z = c * u
    conv = w[0:1, :] * _shift_down(z, 2) + w[1:2, :] * _shift_down(z, 1) + w[2:3, :] * z
    return z, conv, b * conv


def _conv_fwd(proj, conv_w, g, n_groups, out_width, after=()):
    s = proj.shape[0]

    def body(u_ref, b_ref, c_ref, w_ref, g_ref, o_ref):
        _, _, yr = _conv_core(u_ref[...], b_ref[...], c_ref[...], w_ref[...])
        o_ref[...] = (yr * _rstd(yr) * g_ref[...]).astype(o_ref.dtype)

    col = lambda k: pl.BlockSpec((s, HEAD), lambda i: (0, k * n_groups + i))
    return _call(
        body, name="conv_fwd", grid=(n_groups,), after=after,
        in_specs=[col(0), col(1), col(2), pl.BlockSpec((3, HEAD), lambda i: (0, i)), pl.BlockSpec((1, HEAD), lambda i: (0, i))],
        out_specs=pl.BlockSpec((s, HEAD), lambda i: (0, i)),
        out_shape=jax.ShapeDtypeStruct((s, out_width), BF16),
        compiler_params=_params("parallel"),
    )(proj, proj, proj, conv_w, g)


def _conv_bwd(proj, d_mix, conv_w, g, n_groups, out_width):
    s = proj.shape[0]
    width = n_groups * HEAD

    def body(u_ref, b_ref, c_ref, dy_ref, w_ref, g_ref, dproj_ref, dg_ref, dw_ref, buf, sems):
        i = pl.program_id(0)
        slot = i % 2

        def copies(group, slot):
            return [pltpu.make_async_copy(buf.at[slot, k], dproj_ref.at[:, pl.ds(pl.multiple_of((k * n_groups + group) * HEAD, HEAD), HEAD)],
                                          sems.at[slot, k]) for k in range(3)]

        @pl.when(i >= 2)
        def _():
            for cp in copies(i - 2, slot):
                cp.wait()

        u, b, c, w = u_ref[...], b_ref[...], c_ref[...], w_ref[...]
        z, conv, yr = _conv_core(u, b, c, w)
        dyr, dgc = _rms_bwd(yr, g_ref[...], dy_ref[...])
        dconv = dyr * b
        dz = w[2:3, :] * dconv + w[1:2, :] * _shift_up(dconv, 1) + w[0:1, :] * _shift_up(dconv, 2)
        buf[slot, 0] = (dz * c).astype(buf.dtype)
        buf[slot, 1] = (dyr * conv).astype(buf.dtype)
        buf[slot, 2] = (dz * u).astype(buf.dtype)
        for cp in copies(i, slot):
            cp.start()
        dg_ref[...] = _sublane_sum(dgc)
        dw_ref[0] = _sublane_sum(dconv * _shift_down(z, 2))
        dw_ref[1] = _sublane_sum(dconv * _shift_down(z, 1))
        dw_ref[2] = _sublane_sum(dconv * z)

        @pl.when(i == n_groups - 1)
        def _():
            for back in range(min(2, n_groups)):
                for cp in copies(i - back, (n_groups - 1 - back) % 2):
                    cp.wait()

    col = lambda k: pl.BlockSpec((s, HEAD), lambda i: (0, k * n_groups + i))
    grp = pl.BlockSpec((s, HEAD), lambda i: (0, i))
    return pl.pallas_call(
        body, name="conv_bwd", grid=(n_groups,),
        in_specs=[col(0), col(1), col(2), grp, pl.BlockSpec((3, HEAD), lambda i: (0, i)), pl.BlockSpec((1, HEAD), lambda i: (0, i))],
        out_specs=[ANY, pl.BlockSpec((SUBLANE, HEAD), lambda i: (0, i)), pl.BlockSpec((3, SUBLANE, HEAD), lambda i: (0, 0, i))],
        out_shape=[jax.ShapeDtypeStruct((s, out_width), BF16),
                   jax.ShapeDtypeStruct((SUBLANE, width), F32), jax.ShapeDtypeStruct((3, SUBLANE, width), F32)],
        scratch_shapes=[pltpu.VMEM((2, 3, s, HEAD), BF16), pltpu.SemaphoreType.DMA((2, 3))],
        compiler_params=_params("arbitrary"),
    )(proj, proj, proj, d_mix, conv_w, g)


def _rope_tables(s, n_heads):
    pos = jnp.arange(s, dtype=F32)
    inv_freq = jnp.power(ROPE_THETA, -jnp.arange(0, ROPE, 2, dtype=F32) / ROPE)
    ang = pos[:, None] * inv_freq[None, :]
    cos, sin = jnp.cos(ang), jnp.sin(ang)
    cs = jnp.concatenate([cos, cos], axis=1)
    sn = jnp.concatenate([-sin, sin], axis=1)
    pad = jnp.zeros((s, LANE - ROPE), F32)
    return (jnp.tile(cs, (1, n_heads)), jnp.tile(sn, (1, n_heads)),
            jnp.concatenate([cs, pad], axis=1), jnp.concatenate([sn, pad], axis=1))


def _swap_halves(v):
    w = v.shape[1]
    lane = lax.broadcasted_iota(jnp.int32, v.shape, 1)
    first = (lane % ROPE) < (ROPE // 2)
    return jnp.where(first, pltpu.roll(v, w - ROPE // 2, 1), pltpu.roll(v, ROPE // 2, 1))


def _pack_heads(q, kv, proj, kr_col, tables, n_heads, after=()):
    s = q.shape[0]
    rows = min(ROWS, s)
    cq, sq, ck, sk = tables
    wq = n_heads * ROPE

    def body(q_ref, kv_ref, kr_ref, cq_ref, sq_ref, ck_ref, sk_ref, qo_ref, ko_ref, vo_ref):
        qr = q_ref[:, n_heads * HEAD:]
        qr = qr * cq_ref[...] + _swap_halves(qr) * sq_ref[...]
        krv = kr_ref[...]
        krv = krv * ck_ref[...] + _swap_halves(krv) * sk_ref[...]
        for h in range(n_heads):
            qo_ref[h] = jnp.concatenate([q_ref[:, h * HEAD:(h + 1) * HEAD], qr[:, h * ROPE:(h + 1) * ROPE]], axis=1).astype(BF16)
            ko_ref[h] = jnp.concatenate([kv_ref[:, 2 * h * HEAD:(2 * h + 1) * HEAD], krv[:, :ROPE]], axis=1).astype(BF16)
            vo_ref[h] = kv_ref[:, (2 * h + 1) * HEAD:(2 * h + 2) * HEAD].astype(BF16)

    hs = lambda w: pl.BlockSpec((n_heads, rows, w), lambda i: (0, i, 0))
    return _call(
        body, name="pack_heads", grid=(s // rows,), after=after,
        in_specs=[_row_spec(rows, q.shape[1]), _row_spec(rows, kv.shape[1]), pl.BlockSpec((rows, LANE), lambda i: (i, kr_col // LANE)),
                  _row_spec(rows, wq), _row_spec(rows, wq), _row_spec(rows, LANE), _row_spec(rows, LANE)],
        out_specs=[hs(QK), hs(QK), hs(HEAD)],
        out_shape=[jax.ShapeDtypeStruct((n_heads, s, QK), BF16), jax.ShapeDtypeStruct((n_heads, s, QK), BF16),
                   jax.ShapeDtypeStruct((n_heads, s, HEAD), BF16)],
        compiler_params=_params("parallel"),
    )(q, kv, proj, cq, sq, ck, sk)


def _unpack_heads(dq, dk, dv, tables, n_heads):
    s = dq.shape[1]
    rows = min(ROWS, s)
    cq, sq, ck, sk = tables
    wq = n_heads * ROPE

    def body(dq_ref, dk_ref, dv_ref, cq_ref, sq_ref, ck_ref, sk_ref, qo_ref, kvo_ref, kro_ref):
        dqr = jnp.concatenate([dq_ref[h][:, HEAD:] for h in range(n_heads)], axis=1)
        dqr = dqr * cq_ref[...] - _swap_halves(dqr) * sq_ref[...]
        dkr = dk_ref[0][:, HEAD:]
        for h in range(1, n_heads):
            dkr = dkr + dk_ref[h][:, HEAD:]
        dkr = jnp.concatenate([dkr, jnp.zeros((rows, LANE - ROPE), F32)], axis=1)
        dkr = dkr * ck_ref[...] - _swap_halves(dkr) * sk_ref[...]
        kro_ref[...] = dkr.astype(kro_ref.dtype)
        qo_ref[:, n_heads * HEAD:] = dqr.astype(qo_ref.dtype)
        for h in range(n_heads):
            qo_ref[:, h * HEAD:(h + 1) * HEAD] = dq_ref[h][:, :HEAD].astype(qo_ref.dtype)
            kvo_ref[:, 2 * h * HEAD:(2 * h + 1) * HEAD] = dk_ref[h][:, :HEAD].astype(kvo_ref.dtype)
            kvo_ref[:, (2 * h + 1) * HEAD:(2 * h + 2) * HEAD] = dv_ref[h].astype(kvo_ref.dtype)

    hs = lambda w: pl.BlockSpec((n_heads, rows, w), lambda i: (0, i, 0))
    return pl.pallas_call(
        body, name="unpack_heads", grid=(s // rows,),
        in_specs=[hs(QK), hs(QK), hs(HEAD), _row_spec(rows, wq), _row_spec(rows, wq), _row_spec(rows, LANE), _row_spec(rows, LANE)],
        out_specs=[_row_spec(rows, n_heads * QK), _row_spec(rows, 2 * n_heads * HEAD), _row_spec(rows, LANE)],
        out_shape=[jax.ShapeDtypeStruct((s, n_heads * QK), BF16), jax.ShapeDtypeStruct((s, 2 * n_heads * HEAD), BF16),
                   jax.ShapeDtypeStruct((s, LANE), BF16)],
        compiler_params=_params("parallel"),
    )(dq, dk, dv, cq, sq, ck, sk)


TQ = 256


LOG2_E = 1.4426950408889634


def _softmax_parts(q, k):
    tq, n_keys = q.shape[0], k.shape[0]
    sc = lax.dot_general(q, k, (NT, ((), ())), preferred_element_type=F32) * (QK ** -0.5 * LOG2_E)
    row = lax.broadcasted_iota(jnp.int32, (tq, tq), 0)
    col = lax.broadcasted_iota(jnp.int32, (tq, tq), 1)
    own = jnp.where(col // CHUNK <= row // CHUNK, sc[:, n_keys - tq:], NEG_INF)
    sc = own if n_keys == tq else jnp.concatenate([sc[:, :n_keys - tq], own], axis=1)
    e = jnp.exp2(sc - jnp.max(sc, axis=-1, keepdims=True))
    return e, 1.0 / jnp.sum(e, axis=-1, keepdims=True)


def _prob_columns(c, tq):
    return pl.ds(tq * (c * (c + 1) // 2), (c + 1) * tq)


def _attn_fwd(q, k, v, g, mix, col0, first, between):
    n_heads, s, _ = q.shape
    tq = min(TQ, s)
    assert tq % CHUNK == 0 and s % tq == 0
    n_blocks = s // tq
    p_cols = tq * (n_blocks * (n_blocks + 1) // 2)
    out_shape = [jax.ShapeDtypeStruct((n_heads, s, HEAD), F32), jax.ShapeDtypeStruct((n_heads, tq, p_cols), BF16),
                 jax.ShapeDtypeStruct(mix.shape, mix.dtype)]
    done, after = (mix,), ()
    for part, (h0, h1) in enumerate(((0, first), (first, n_heads))):

        def body(q_ref, k_ref, v_ref, g_ref, *rest):
            o_ref, p_ref, y_ref = rest[-3:]
            for c in range(n_blocks):
                rows, n_keys = pl.ds(c * tq, tq), (c + 1) * tq
                e, inv = _softmax_parts(q_ref[rows, :], k_ref[0:n_keys, :])
                p = (e * inv).astype(BF16)
                p_ref[:, _prob_columns(c, tq)] = p
                o = jnp.dot(p, v_ref[0:n_keys, :], preferred_element_type=F32)
                o_ref[rows, :] = o
                y_ref[rows, :] = (o * _rstd(o) * g_ref[...]).astype(y_ref.dtype)

        head = lambda w, h0=h0: pl.BlockSpec((None, s, w), lambda h: (h0 + h, 0, 0))
        n_done = len(done)
        done = _call(
            body, name=f"attn_fwd_{part}", grid=(h1 - h0,), after=after,
            in_specs=[head(QK), head(QK), head(HEAD), pl.BlockSpec((1, HEAD), lambda h, h0=h0: (0, h0 + h))] + [ANY] * n_done,
            out_specs=[head(HEAD), pl.BlockSpec((None, tq, p_cols), lambda h, h0=h0: (h0 + h, 0, 0)),
                       pl.BlockSpec((s, HEAD), lambda h, h0=h0: (0, col0 // HEAD + h0 + h))],
            out_shape=out_shape,
            input_output_aliases={4 + i: 3 - n_done + i for i in range(n_done)},
            compiler_params=_params("parallel"),
        )(q, k, v, g, *done)
        after = between(done) if part == 0 else ()
    return done


def _attn_bwd(q, k, v, o, probs, d_mix, g, col0, after=()):
    n_heads, s, _ = q.shape
    tq = probs.shape[1]

    def body(q_ref, k_ref, v_ref, o_ref, p_ref, dy_ref, g_ref, dq_ref, dk_ref, dv_ref, dg_ref):
        dg = None
        for c in reversed(range(s // tq)):
            rows, n_keys = pl.ds(c * tq, tq), (c + 1) * tq
            o = o_ref[rows, :]
            do, dgc = _rms_bwd(o, g_ref[...], dy_ref[rows, :])
            do = do.astype(BF16)
            dg = _sublane_sum(dgc) if dg is None else dg + _sublane_sum(dgc)
            p = p_ref[:, _prob_columns(c, tq)]
            dp = lax.dot_general(do, v_ref[0:n_keys, :], (NT, ((), ())), preferred_element_type=F32)
            ds = (p.astype(F32) * (dp - jnp.sum(do.astype(F32) * o, axis=-1, keepdims=True))).astype(BF16)
            dq_ref[rows, :] = jnp.dot(ds, k_ref[0:n_keys, :], preferred_element_type=F32) * (QK ** -0.5)
            dk = lax.dot_general(ds, q_ref[rows, :], (TN, ((), ())), preferred_element_type=F32)
            dv = lax.dot_general(p, do, (TN, ((), ())), preferred_element_type=F32)
            if n_keys == s:
                dk_ref[...] = dk
                dv_ref[...] = dv
            else:
                dk_ref[0:n_keys, :] += dk
                dv_ref[0:n_keys, :] += dv
        dk_ref[...] = dk_ref[...] * (QK ** -0.5)
        dg_ref[...] = dg

    c0 = col0 // HEAD
    head = lambda w: pl.BlockSpec((None, s, w), lambda h, *_: (h, 0, 0))
    in_specs = [head(QK), head(QK), head(HEAD), head(HEAD), pl.BlockSpec((None, tq, probs.shape[2]), lambda h, *_: (h, 0, 0)),
                pl.BlockSpec((s, HEAD), lambda h, *_: (0, c0 + h)), pl.BlockSpec((1, HEAD), lambda h, *_: (0, h))]
    out_specs = [head(QK), head(QK), head(HEAD), pl.BlockSpec((SUBLANE, HEAD), lambda h, *_: (0, h))]
    out_shape = [jax.ShapeDtypeStruct((n_heads, s, QK), F32), jax.ShapeDtypeStruct((n_heads, s, QK), F32),
                 jax.ShapeDtypeStruct((n_heads, s, HEAD), F32), jax.ShapeDtypeStruct((SUBLANE, n_heads * HEAD), F32)]
    return _call(body, name="attn_bwd", grid=(n_heads,), after=after, in_specs=in_specs, out_specs=out_specs,
                 out_shape=out_shape, compiler_params=_params("parallel"))(q, k, v, o, probs, d_mix, g)


TILE_M = 1024
TILE_N = 1024


def _up_fwd(h2, w_up, between):
    s, d = h2.shape
    nb, _, fb = w_up.shape
    tm = min(TILE_M, s)
    done, after = (), ()
    for tile in range(s // tm):

        def body(h_ref, w_ref, *rest):
            a_ref, r_ref = rest[-2:]
            r = jnp.maximum(jnp.dot(h_ref[...], w_ref[...], preferred_element_type=F32), 0.0)
            a_ref[...] = (r * r).astype(a_ref.dtype)
            r_ref[...] = r.astype(r_ref.dtype)

        blk = pl.BlockSpec((tm, fb), lambda j, tile=tile: (tile, j))
        done = _call(
            body, name=f"up_fwd_{tile}", grid=(nb,), after=after,
            in_specs=[pl.BlockSpec((tm, d), lambda j, tile=tile: (tile, 0)), pl.BlockSpec((None, d, fb), lambda j: (j, 0, 0))]
                     + [ANY] * len(done),
            out_specs=[blk, blk], out_shape=[jax.ShapeDtypeStruct((s, nb * fb), BF16)] * 2,
            input_output_aliases={2 + i: i for i in range(len(done))},
            compiler_params=_params("parallel"),
        )(h2, w_up, *done)
        after = between(done) if tile == 0 else ()
    return done


def _down_fwd(a, w_down):
    s, f = a.shape
    d = w_down.shape[1]
    tm, tn, tk = min(TILE_M,s), min(TILE_N,d), 2048
    nk = f // tk
    return _matmul("down_fwd", a, w_down, grid=(s // tm, d // tn, nk),
                   a_spec=pl.BlockSpec((tm, tk), lambda i, j, k: (i, k)),
                   b_spec=pl.BlockSpec((tk, tn), lambda i, j, k: (k, j)),
                   out_shape=jax.ShapeDtypeStruct((s, d), F32),
                   out_specs=pl.BlockSpec((tm, tn), lambda i, j, k: (i, j)),
                   contract=NN, nk=nk, acc_shape=(tm, tn))


def _down_bwd_act(d_m, w_down, r, after=()):
    s, d = d_m.shape
    f = w_down.shape[0]
    tm, tn = min(TILE_M,s), min(TILE_N,f)
    blk = pl.BlockSpec((tm, tn), lambda i, j: (i, j))
    return _matmul("down_bwd_act", d_m, w_down, grid=(s // tm, f // tn), after=after,
                   a_spec=pl.BlockSpec((tm, d), lambda i, j: (i, 0)),
                   b_spec=pl.BlockSpec((tn, d), lambda i, j: (j, 0)),
                   out_shape=jax.ShapeDtypeStruct((s, f), BF16), out_specs=blk, contract=NT,
                   extras=(r,), extra_specs=(blk,),
                   epilogue=lambda acc, rv: (acc * (2.0 * rv.astype(F32)),))


def _up_bwd_act(d_up, w_up, after=()):
    s, _ = d_up.shape
    nb, d, fb = w_up.shape
    tm, tn = min(TILE_M, s), min(TILE_N,d)
    pair = 2
    n_after = len(after)

    def body(a_ref, w_ref, *rest):
        o_ref, acc_ref = rest[n_after:]
        k = pl.program_id(2)
        p = None
        for t in range(pair):
            term = lax.dot_general(a_ref[:, t * fb:(t + 1) * fb], w_ref[t], (NT, ((), ())), preferred_element_type=F32)
            p = term if p is None else p + term
        _accumulate(acc_ref, p, k)

        @pl.when(k == nb // pair - 1)
        def _():
            o_ref[...] = acc_ref[...]

    return pl.pallas_call(
        body, name="up_bwd_act", grid=(s // tm, d // tn, nb // pair),
        in_specs=[pl.BlockSpec((tm, pair * fb), lambda i, j, k: (i, k)),
                  pl.BlockSpec((pair, tn, fb), lambda i, j, k: (k, j, 0))] + [ANY] * n_after,
        out_specs=pl.BlockSpec((tm, tn), lambda i, j, k: (i, j)),
        out_shape=jax.ShapeDtypeStruct((s, d), F32),
        scratch_shapes=[pltpu.VMEM((tm, tn), F32)],
        compiler_params=_params("parallel", "parallel", "arbitrary"),
    )(d_up, w_up, *after)


def _half_grad(name, a, b, core, home, received, after, *, grid, a_block, a_map, b_block, b_map, o_block, o_map, out_shape):
    n_after = len(after)
    pick = (lambda ref: ref[0]) if home else (lambda ref: 1 - ref[0])

    def body(core_ref, a_ref, b_ref, *rest):
        acc = lax.dot_general(a_ref[...], b_ref[...], (TN, ((), ())), preferred_element_type=F32)
        if received is not None:
            acc = acc + rest[0][...].astype(F32)
        rest[-1][...] = acc.astype(rest[-1].dtype)

    wrap = lambda fn: (lambda i, j, core_ref: fn(i, j, pick(core_ref)))
    o_spec = pl.BlockSpec(o_block, wrap(o_map))
    extra = [] if received is None else [o_spec]
    operands = [] if received is None else [received]
    return pl.pallas_call(
        body, name=name,
        grid_spec=pltpu.PrefetchScalarGridSpec(
            num_scalar_prefetch=1, grid=grid,
            in_specs=[pl.BlockSpec(a_block, wrap(a_map)), pl.BlockSpec(b_block, wrap(b_map))] + extra + [ANY] * n_after,
            out_specs=o_spec),
        out_shape=out_shape,
        compiler_params=_params("parallel", "parallel"),
    )(core, a, b, *operands, *after)


def _down_half_grad(name, a, d_m, core, home, received=None, after=()):
    s, f = a.shape
    d = d_m.shape[1]
    r = f // N_DEV
    tn = min(TILE_N, d)
    return _half_grad(name, a, d_m, core, home, received, after, grid=(N_CHIP, d // tn),
                      a_block=(s, r), a_map=lambda k, j, p: (0, 2 * k + p),
                      b_block=(s, tn), b_map=lambda k, j, p: (0, j),
                      o_block=(None, r, tn), o_map=lambda k, j, p: (k, 0, j),
                      out_shape=jax.ShapeDtypeStruct((N_CHIP, r, d), BF16))


def _up_half_grad(name, h2, d_up, core, home, received=None, after=()):
    s, d = h2.shape
    fb = d_up.shape[1] // N_DEV
    tm = min(TILE_M, d)
    return _half_grad(name, h2, d_up, core, home, received, after, grid=(d // tm, N_CHIP),
                      a_block=(s, tm), a_map=lambda i, k, p: (0, i),
                      b_block=(s, fb), b_map=lambda i, k, p: (0, 2 * k + p),
                      o_block=(None, tm, fb), o_map=lambda i, k, p: (k, i, 0),
                      out_shape=jax.ShapeDtypeStruct((N_CHIP, d, fb), BF16))


MXU_WIDTH = 256


def _in_pad(in_width):
    return -(-in_width // MXU_WIDTH) * MXU_WIDTH


def _join_col_shards(name, blocks, own, device, pieces=None):
    n, r, w = blocks.shape
    rows = min(ROWS, r)
    pieces = pieces or [(j, 0, w) for j in range(n)]
    used = sum(b - a for _, a, b in pieces)
    width = _in_pad(used)

    def body(dev_ref, x_ref, own_ref, o_ref):
        block = lambda j: jnp.where(dev_ref[0] == j, own_ref[...], x_ref[j])
        cols = [block(j)[:, a:b] for j, a, b in pieces]
        tail = [jnp.zeros((rows, width - used), o_ref.dtype)] if width > used else []
        o_ref[...] = jnp.concatenate(cols + tail, axis=1)

    return pl.pallas_call(
        body, name=name,
        grid_spec=pltpu.PrefetchScalarGridSpec(
            num_scalar_prefetch=1, grid=(r // rows,),
            in_specs=[pl.BlockSpec((n, rows, w), lambda i, dev: (0, i, 0)), pl.BlockSpec((rows, w), lambda i, dev: (i, 0))],
            out_specs=pl.BlockSpec((rows, width), lambda i, dev: (i, 0))),
        out_shape=jax.ShapeDtypeStruct((r, width), blocks.dtype),
        compiler_params=_params("parallel"),
    )(device, blocks, own)


def _unpermute_q_rows(wt, n_heads):
    r = wt.shape[1]
    nope = wt[:n_heads * HEAD].reshape(n_heads, HEAD, r)
    rope = wt[n_heads * HEAD:].reshape(n_heads, ROPE, r)
    return jnp.concatenate([nope, rope], axis=1).reshape(n_heads * QK, r)


def _local_step(x, tgt, gains, weights, grads, first_after=()):
    pre_mix_g, q_norm_g, kv_norm_g, conv_out_g, attn_out_g, post_mix_g, pre_mlp_g, post_mlp_g = gains
    s, d = x.shape
    conv_width = conv_out_g.shape[1]
    n_groups = conv_width // HEAD
    r_q, r_kv = q_norm_g.shape[1], kv_norm_g.shape[1]
    n_heads = attn_out_g.shape[1] // HEAD
    c_q0 = 3 * conv_width
    c_kv0 = c_q0 + r_q
    c_kr0 = c_kv0 + r_kv
    in_pad = _in_pad(c_kr0 + ROPE)
    tn_in = _fit(in_pad, 6 * MXU_WIDTH)
    tables = _rope_tables(s, n_heads)

    h1 = _rms_fwd("pre_mix_norm", x, pre_mix_g, after=first_after)
    weights.forward(0, (h1,))
    weights.relay(0, tables)
    w_in_p, conv_w = weights.ready(0, ())
    proj = _mm_nn("in_proj", h1, w_in_p, F32, TILE_M, tn_in)
    y_conv = _conv_fwd(proj, conv_w, conv_out_g, n_groups, conv_width + n_heads * HEAD, after=weights.forward(1, (proj,)))
    w_uq_p, w_ukv, w_o = weights.ready(1, (y_conv,))
    qn, q = _norm_up("q_up", proj, (c_q0, r_q), q_norm_g, w_uq_p)
    kvn, kv = _norm_up("kv_up", proj, (c_kv0, r_kv), kv_norm_g, w_ukv)
    qh, kh, vh = _pack_heads(q, kv, proj, c_kr0, tables, n_heads)
    o, probs, mix = _attn_fwd(qh, kh, vh, attn_out_g, y_conv, conv_width, n_heads // 4,
                              lambda done: weights.forward(2, tuple(done)))
    y = _mm_nn("out_proj", mix, w_o, F32, TILE_M, TILE_N, after=weights.start(3, (mix,)))
    x2, h2 = _mid_fwd(x, y, post_mix_g, pre_mlp_g)
    weights.relay(2, (h2,))
    (w_up,) = weights.ready(2, ())
    a, r = _up_fwd(h2, w_up, lambda done: weights.forward(3, tuple(done)))
    weights.relay(3, (a,))
    (w_down,) = weights.ready(3, ())
    m = _down_fwd(a, w_down)

    d_out, d_m, dg_post_mlp, loss_part = _head(m, x2, tgt, post_mlp_g)
    core = grads.core
    away = _down_half_grad("down_bwd_w_away", a, d_m, core, home=False)
    d_up = _down_bwd_act(d_m, w_down, r, after=grads.send_away(0, away))
    sums = _down_half_grad("down_bwd_w_home", a, d_m, core, home=True, received=grads.received(0, (d_up,)))
    away = _up_half_grad("up_bwd_w_away", h2, d_up, core, home=False, after=grads.send_sums(0, (sums,)))
    d_h2 = _up_bwd_act(d_up, w_up, after=grads.send_away(1, away))
    sums = _up_half_grad("up_bwd_w_home", h2, d_up, core, home=True, received=grads.received(1, (d_h2,)))
    d_x2, d_y, dg_pre_mlp, dg_post_mix = _mid_bwd(x2, y, d_out, d_h2, pre_mlp_g, post_mix_g, after=grads.send_sums(1, (sums,)))
    d_mix = _mm_nt("out_proj_bwd_act", d_y, w_o, F32, TILE_M, TILE_N)
    gw_o = _mm_tn("out_proj_bwd_w", mix, d_y, BF16, TILE_M, TILE_N)
    dqh, dkh, dvh, dg_attn = _attn_bwd(qh, kh, vh, o, probs, d_mix, attn_out_g, conv_width)
    d_q, d_kv, d_kr = _unpack_heads(dqh, dkh, dvh, tables, n_heads)
    gw_uq_t = _mm_tn("q_up_bwd_w", d_q, qn, F32, TILE_M, TILE_N)
    gw_ukv = _mm_tn("kv_up_bwd_w", kvn, d_kv, BF16, TILE_M, TILE_N)
    d_proj, dg_conv, dw_conv = _conv_bwd(proj, d_mix, conv_w, conv_out_g, n_groups, in_pad)
    d_proj, dg_q = _up_norm_bwd("q_up_bwd_act", d_q, w_uq_p, proj, (c_q0, r_q), q_norm_g, d_proj,
                                after=grads.full(2, (gw_o, gw_uq_t, gw_ukv)))
    d_proj, dg_kv = _up_norm_bwd("kv_up_bwd_act", d_kv, w_ukv, proj, (c_kv0, r_kv), kv_norm_g, d_proj, tail=d_kr)
    gw_in_t = _mm_tn("in_proj_bwd_w", d_proj, h1, F32, tn_in, TILE_N)
    updated = grads.update_now(0, grads.send_away(3, gw_in_t))
    d_h1 = _mm_nt("in_proj_bwd_act", d_proj, w_in_p, F32, TILE_M, TILE_N, after=grads.full(3, (gw_in_t,), received=updated))
    grad_x, dg_pre_mix = _first_bwd(x, pre_mix_g, d_h1, d_x2)

    small = [dg_pre_mix, dg_q, dg_kv, dg_conv, dg_attn, dg_post_mix, dg_pre_mlp, dg_post_mlp,
             dw_conv[0], dw_conv[1], dw_conv[2], loss_part]
    return grad_x, jnp.concatenate(small, axis=1)


HBM = pl.BlockSpec(memory_space=pltpu.HBM)
SEM = pl.BlockSpec(memory_space=pltpu.SEMAPHORE)
IN_VMEM = pl.BlockSpec(memory_space=pltpu.VMEM)
SPLIT = pltpu.CompilerParams(has_side_effects=pltpu.SideEffectType.DATAFLOW_SIDE_EFFECTING)


def _in_hbm(a):
    return pltpu.with_memory_space_constraint(a, pltpu.HBM)


def _hbm_like(a):
    return pltpu.HBM(a.shape, a.dtype)


def _place():
    x, y, c = lax.axis_index("x"), lax.axis_index("y"), lax.axis_index("c")
    other_chips = [(1 - x, y), (x, 1 - y), (1 - x, 1 - y)]
    return x, y, c, other_chips


def _block(px, py, pc):
    return 4 * px + 2 * py + pc


def _await(block, sem):
    pltpu.make_async_copy(block, block, sem).wait()


def _relay_route(x, y, c):
    came_from = ((1 - x) * (1 - c) + x * c, y * (1 - c) + (1 - y) * c)
    goes_to = (x * (1 - c) + (1 - x) * c, (1 - y) * (1 - c) + y * c)
    return came_from, goes_to


def _gather_start(name, shards, groups, relayed=(), after=(), lands=None):
    n, ng = len(shards), len(groups)
    if lands is None:
        lands = [lax.empty((N_DEV, *a.shape), a.dtype) for a in shards]

    def body(*refs):
        src, land = refs[:n], refs[n:2 * n]
        sems, token = refs[2 * n + len(after):2 * n + len(after) + 2 * ng], refs[-1]
        x, y, c, chips = _place()
        targets = [(x, y, 1 - c)] + [(*chip, c) for chip in chips]
        for gi, group in enumerate(groups):
            for i, w in enumerate(group):
                for k, to in enumerate(targets[:3] if gi in relayed else targets):
                    pltpu.make_async_remote_copy(
                        src_ref=src[w], dst_ref=land[w].at[_block(x, y, c)],
                        send_sem=sems[2 * gi].at[4 * i + k], recv_sem=sems[2 * gi + 1].at[4 * i + k],
                        device_id=to, device_id_type=MESH).start()
        token[...] = jnp.zeros_like(token)

    sem_shapes = [pltpu.SemaphoreType.DMA((4 * len(g),)) for g in groups for _ in range(2)]
    out = pl.pallas_call(
        body, name=name,
        in_specs=[HBM] * (2 * n) + [ANY] * len(after),
        out_specs=[SEM] * (2 * ng) + [HBM] * (2 * n) + [IN_VMEM],
        out_shape=sem_shapes + [_hbm_like(a) for a in shards] + [_hbm_like(a) for a in lands]
        + [jax.ShapeDtypeStruct((SUBLANE, LANE), F32)],
        input_output_aliases={i: 2 * ng + i for i in range(2 * n)},
        compiler_params=SPLIT,
    )(*[_in_hbm(a) for a in shards], *[_in_hbm(a) for a in lands], *after)
    sems = [(out[2 * gi], out[2 * gi + 1]) for gi in range(ng)]
    return sems, out[2 * ng:2 * ng + n], out[2 * ng + n:2 * ng + 2 * n], out[-1]


def _gather_forward(name, shards, lands, send1, recv1, after, relayed=False):
    n = len(lands)

    def body(*refs):
        src, land = refs[:n], refs[n:2 * n]
        s1, r1 = refs[2 * n], refs[2 * n + 1]
        s2, r2 = refs[2 * n + 2 + len(after)], refs[2 * n + 3 + len(after)]
        x, y, c, chips = _place()
        me, sibling = (x, y, c), (x, y, 1 - c)
        for j, chip in enumerate(chips[:2] if relayed else chips):
            for i in range(n):
                blk = land[i].at[_block(*chip, c)]
                pltpu.make_async_remote_copy(src_ref=blk, dst_ref=blk, send_sem=s1.at[4 * i + 1 + j], recv_sem=r1.at[4 * i + 1 + j],
                                             device_id=me, device_id_type=MESH).wait_recv()
                pltpu.make_async_remote_copy(src_ref=blk, dst_ref=blk, send_sem=s2.at[3 * i + j], recv_sem=r2.at[3 * i + j],
                                             device_id=sibling, device_id_type=MESH).start()
        if relayed:
            came_from, goes_to = _relay_route(x, y, c)
            for i in range(n):
                blk = land[i].at[_block(*came_from, c)]
                pltpu.make_async_remote_copy(src_ref=blk, dst_ref=blk, send_sem=s2.at[3 * i + 2], recv_sem=r2.at[3 * i + 2],
                                             device_id=(*goes_to, c), device_id_type=MESH).start()
        for i in range(n):
            blk = land[i].at[_block(x, y, 1 - c)]
            pltpu.make_async_remote_copy(src_ref=blk, dst_ref=blk, send_sem=s1.at[4 * i], recv_sem=r1.at[4 * i],
                                         device_id=me, device_id_type=MESH).wait_recv()
            for k in range(3 if relayed else 4):
                pltpu.make_async_remote_copy(src_ref=src[i], dst_ref=land[i].at[_block(x, y, c)], send_sem=s1.at[4 * i + k],
                                             recv_sem=r1.at[4 * i + k], device_id=sibling, device_id_type=MESH).wait_send()

    sem = pltpu.SemaphoreType.DMA((3 * n,))
    out = pl.pallas_call(
        body, name=name,
        in_specs=[HBM] * (2 * n) + [SEM, SEM] + [ANY] * len(after),
        out_specs=[SEM, SEM] + [HBM] * n,
        out_shape=[sem, sem] + [_hbm_like(a) for a in lands],
        input_output_aliases={n + i: 2 + i for i in range(n)},
        compiler_params=SPLIT,
    )(*shards, *lands, send1, recv1, *after)
    return (out[0], out[1]), out[2:]


def _gather_relay_forward(name, lands, send2, recv2, after):
    n = len(lands)

    def body(*refs):
        land, s2, r2 = refs[:n], refs[n], refs[n + 1]
        s3, r3 = refs[n + 2 + len(after)], refs[n + 3 + len(after)]
        x, y, c, _ = _place()
        me, sibling = (x, y, c), (x, y, 1 - c)
        came_from, _ = _relay_route(x, y, c)
        for i in range(n):
            blk = land[i].at[_block(1 - x, 1 - y, c)]
            pltpu.make_async_remote_copy(src_ref=blk, dst_ref=blk, send_sem=s2.at[3 * i + 2], recv_sem=r2.at[3 * i + 2],
                                         device_id=me, device_id_type=MESH).wait_recv()
            pltpu.make_async_remote_copy(src_ref=blk, dst_ref=blk, send_sem=s3.at[i], recv_sem=r3.at[i],
                                         device_id=sibling, device_id_type=MESH).start()
            sent = land[i].at[_block(*came_from, c)]
            pltpu.make_async_remote_copy(src_ref=sent, dst_ref=sent, send_sem=s2.at[3 * i + 2], recv_sem=r2.at[3 * i + 2],
                                         device_id=me, device_id_type=MESH).wait_send()

    sem = pltpu.SemaphoreType.DMA((n,))
    out = pl.pallas_call(
        body, name=name,
        in_specs=[HBM] * n + [SEM, SEM] + [ANY] * len(after),
        out_specs=[SEM, SEM] + [HBM] * n,
        out_shape=[sem, sem] + [_hbm_like(a) for a in lands],
        input_output_aliases={i: 2 + i for i in range(n)},
        compiler_params=SPLIT,
    )(*lands, send2, recv2, *after)
    return (out[0], out[1]), out[2:]


def _gather_wait(name, lands, send2, recv2, after, relay_sems=None):
    n = len(lands)
    n_sems = 2 if relay_sems is None else 4

    def body(*refs):
        land, s2, r2 = refs[:n], refs[n], refs[n + 1]
        for i in range(n):
            for j in range(3 if relay_sems is None else 2):
                _await(land[i].at[0], r2.at[3 * i + j])
                _await(land[i].at[0], s2.at[3 * i + j])
            if relay_sems is not None:
                _await(land[i].at[0], refs[n + 3].at[i])
                _await(land[i].at[0], refs[n + 2].at[i])

    return pl.pallas_call(
        body, name=name,
        in_specs=[HBM] * n + [SEM] * n_sems + [ANY] * len(after), out_specs=[HBM] * n, out_shape=[_hbm_like(a) for a in lands],
        input_output_aliases={i: i for i in range(n)},
        compiler_params=SPLIT,
    )(*lands, send2, recv2, *(relay_sems or ()), *after)


def _pair_exchange(name, grads, shard_rows):
    n = len(grads)
    shapes = [(g.shape[1:] if r is None else (r, g.shape[1])) for g, r in zip(grads, shard_rows)]

    def body(*refs):
        ins, recv = refs[:n], refs[n:2 * n]
        send_sems, recv_sems = refs[2 * n:]
        x, y, c, _ = _place()
        sends = []
        for w in range(n):
            for k in range(N_CHIP):
                j, r = 2 * k + 1 - c, shard_rows[w]
                src = ins[w].at[j] if r is None else ins[w].at[pl.ds(pl.multiple_of(j * r, SUBLANE), r), :]
                sends.append(pltpu.make_async_remote_copy(
                    src_ref=src, dst_ref=recv[w].at[k],
                    send_sem=send_sems.at[w, k], recv_sem=recv_sems.at[w, k],
                    device_id=(x, y, 1 - c), device_id_type=MESH))
        for cp in sends:
            cp.start()
        for cp in sends:
            cp.wait()

    return pl.pallas_call(
        body, name=name,
        in_specs=[ANY] * n, out_specs=[ANY] * n,
        out_shape=[jax.ShapeDtypeStruct((N_CHIP, *shape), g.dtype) for g, shape in zip(grads, shapes)],
        scratch_shapes=[pltpu.SemaphoreType.DMA((n, N_CHIP))] * 2,
    )(*grads)


def _pair_sum_rows(name, grad, received, core):
    _, r, c = received.shape
    tc = _fit(c, 512)

    def body(core_ref, a_ref, b_ref, o_ref):
        o_ref[...] = (a_ref[...] + b_ref[...]).astype(o_ref.dtype)

    spec = pl.BlockSpec((None, r, tc), lambda k, i, core_ref: (k, 0, i))
    return pl.pallas_call(
        body, name=name,
        grid_spec=pltpu.PrefetchScalarGridSpec(
            num_scalar_prefetch=1, grid=(N_CHIP, c // tc),
            in_specs=[pl.BlockSpec((r, tc), lambda k, i, core_ref: (2 * k + core_ref[0], i)), spec],
            out_specs=spec),
        out_shape=jax.ShapeDtypeStruct(received.shape, BF16),
        compiler_params=_params("parallel", "parallel"),
    )(core, grad, received)


def _pair_sum(name, grad, received, core):
    _, r, c = received.shape
    rows = min(ROWS, r)
    assert r % rows == 0

    def body(core_ref, a_ref, b_ref, o_ref):
        o_ref[...] = (a_ref[...].astype(F32) + b_ref[...].astype(F32)).astype(o_ref.dtype)

    spec = pl.BlockSpec((None, rows, c), lambda k, i, core_ref: (k, i, 0))
    return pl.pallas_call(
        body, name=name,
        grid_spec=pltpu.PrefetchScalarGridSpec(
            num_scalar_prefetch=1, grid=(N_CHIP, r // rows),
            in_specs=[pl.BlockSpec((None, None, rows, c), lambda k, i, core_ref: (k, core_ref[0], i, 0)), spec],
            out_specs=spec),
        out_shape=jax.ShapeDtypeStruct(received.shape, received.dtype),
        compiler_params=_params("parallel", "parallel"),
    )(core, grad.reshape(N_CHIP, 2, r, c), received)


def _away_shard(src, k, c, shard_rows):
    if shard_rows is None:
        return src.at[k]
    return src.at[pl.ds(pl.multiple_of((2 * k + 1 - c) * shard_rows, SUBLANE), shard_rows), :]


def _pair_send_start(name, away, shard_rows=None):
    shape = away.shape if shard_rows is None else (N_CHIP, shard_rows, away.shape[1])
    land = lax.empty(shape, away.dtype)

    def body(src, dst, send, recv, src_thru, dst_thru, token):
        x, y, c, _ = _place()
        for k in range(N_CHIP):
            pltpu.make_async_remote_copy(src_ref=_away_shard(src, k, c, shard_rows), dst_ref=dst.at[k], send_sem=send.at[k],
                                         recv_sem=recv.at[k], device_id=(x, y, 1 - c), device_id_type=MESH).start()
        token[...] = jnp.zeros_like(token)

    sem = pltpu.SemaphoreType.DMA((N_CHIP,))
    out = pl.pallas_call(
        body, name=name,
        in_specs=[HBM, HBM], out_specs=[SEM, SEM, HBM, HBM, IN_VMEM],
        out_shape=[sem, sem, _hbm_like(away), _hbm_like(land), jax.ShapeDtypeStruct((SUBLANE, LANE), F32)],
        input_output_aliases={0: 2, 1: 3},
        compiler_params=SPLIT,
    )(_in_hbm(away), _in_hbm(land))
    return (out[0], out[1]), out[2], out[3], out[4]


def _pair_send_wait(name, sems, src, land, after, shard_rows=None):
    def body(src_ref, dst_ref, send, recv, *rest):
        for k in range(N_CHIP):
            _await(dst_ref.at[k], send.at[k])
            _await(dst_ref.at[k], recv.at[k])

    return pl.pallas_call(
        body, name=name,
        in_specs=[HBM, HBM, SEM, SEM] + [ANY] * len(after), out_specs=HBM, out_shape=_hbm_like(land),
        input_output_aliases={1: 0},
        compiler_params=SPLIT,
    )(src, land, *sems, *after)


def _chip_send_start(name, sums):
    n = len(sums)
    lands = [lax.empty(a.shape, a.dtype) for a in sums]

    def body(*refs):
        src, land = refs[:n], refs[n:2 * n]
        send, recv, token = refs[2 * n], refs[2 * n + 1], refs[-1]
        x, y, c, chips = _place()
        for w in range(n):
            for j, (px, py) in enumerate(chips):
                pltpu.make_async_remote_copy(
                    src_ref=src[w].at[2 * px + py], dst_ref=land[w].at[2 * x + y],
                    send_sem=send.at[3 * w + j], recv_sem=recv.at[3 * w + j],
                    device_id=(px, py, c), device_id_type=MESH).start()
        token[...] = jnp.zeros_like(token)

    sem = pltpu.SemaphoreType.DMA((3 * n,))
    out = pl.pallas_call(
        body, name=name,
        in_specs=[HBM] * (2 * n),
        out_specs=[SEM, SEM] + [HBM] * (2 * n) + [IN_VMEM],
        out_shape=[sem, sem] + [_hbm_like(a) for a in sums] + [_hbm_like(a) for a in lands]
        + [jax.ShapeDtypeStruct((SUBLANE, LANE), F32)],
        input_output_aliases={i: 2 + i for i in range(2 * n)},
        compiler_params=SPLIT,
    )(*[_in_hbm(a) for a in sums], *[_in_hbm(a) for a in lands])
    return (out[0], out[1]), out[2:2 + n], out[2 + n:2 + 2 * n], out[-1]


def _chip_send_wait(name, groups, after):
    counts = [len(g[1]) for g in groups]
    n = sum(counts)

    def body(*refs):
        land = refs[n:2 * n]
        sems = refs[2 * n:2 * n + 2 * len(groups)]
        w = 0
        for gi, count in enumerate(counts):
            for i in range(count):
                for j in range(3):
                    _await(land[w].at[0], sems[2 * gi].at[3 * i + j])
                    _await(land[w].at[0], sems[2 * gi + 1].at[3 * i + j])
                w += 1

    sums = [a for g in groups for a in g[1]]
    lands = [a for g in groups for a in g[2]]
    sems = [s for g in groups for s in g[0]]
    return pl.pallas_call(
        body, name=name,
        in_specs=[HBM] * (2 * n) + [SEM] * len(sems) + [ANY] * len(after),
        out_specs=[HBM] * n, out_shape=[_hbm_like(a) for a in lands],
        input_output_aliases={n + i: i for i in range(n)},
        compiler_params=SPLIT,
    )(*sums, *lands, *sems, *after)


def _small_all_reduce(part, after=()):
    _, w = part.shape

    def body(p_ref, *rest):
        o_ref, buf, send_sems, recv_sems = rest[len(after):]
        x, y, c, _ = _place()
        me = 4 * x + 2 * y + c
        buf[me] = jnp.sum(p_ref[...], axis=0, keepdims=True)
        copies = []
        for k in range(1, N_DEV):
            dx, dy, dc = (k >> 2) & 1, (k >> 1) & 1, k & 1
            copies.append(pltpu.make_async_remote_copy(
                src_ref=buf.at[me], dst_ref=buf.at[me], send_sem=send_sems.at[k - 1], recv_sem=recv_sems.at[k - 1],
                device_id=(x ^ dx, y ^ dy, c ^ dc), device_id_type=MESH))
        for cp in copies:
            cp.start()
        for cp in copies:
            cp.wait()
        tot = buf[0]
        for d in range(1, N_DEV):
            tot = tot + buf[d]
        o_ref[...] = tot
        loss = jnp.sum(tot[:, w - LANE:], axis=1, keepdims=True)
        o_ref[:, w - LANE:] = jnp.broadcast_to(loss, (1, LANE))

    return pl.pallas_call(
        body, name="small_all_reduce",
        in_specs=[IN_VMEM] + [ANY] * len(after), out_specs=IN_VMEM,
        out_shape=jax.ShapeDtypeStruct((1, w), F32),
        scratch_shapes=[pltpu.VMEM((N_DEV, 1, w), F32), pltpu.SemaphoreType.DMA((N_DEV - 1,)), pltpu.SemaphoreType.DMA((N_DEV - 1,))],
        compiler_params=pltpu.CompilerParams(vmem_limit_bytes=VMEM_LIMIT_BYTES),
    )(part, *after)


def _adamw(w, g, m, v):
    m = ADAM_B1 * m + (1.0 - ADAM_B1) * g
    v = ADAM_B2 * v + (1.0 - ADAM_B2) * (g * g)
    m_hat = m / (1.0 - ADAM_B1 ** ADAM_STEP)
    v_hat = v / (1.0 - ADAM_B2 ** ADAM_STEP)
    delta = -ADAM_LR * (m_hat / (jnp.sqrt(v_hat) + ADAM_EPS) + ADAM_WD * w)
    return delta, m, v


def _sum_adam_block(chip_ref, p_ref, own_ref, w_ref, m_ref, v_ref, g_ref, d_ref, mo_ref, vo_ref):
    g = None
    for k in range(N_CHIP):
        term = jnp.where(chip_ref[0] == k, own_ref[...], p_ref[k]).astype(F32)
        g = term if g is None else g + term
    g_ref[...] = g
    d_ref[...], mo_ref[...], vo_ref[...] = _adamw(w_ref[...], g, m_ref[...], v_ref[...])


def _sum_adam(name, parts, sums, chip, w, m, v, after=()):
    _, r, c = w.shape
    n_after = len(after)
    by_rows = r % ROWS == 0 or r < ROWS
    tr, tc = (min(ROWS, r), c) if by_rows else (r, _fit(c, 512))
    at = (lambda i: (i, 0)) if by_rows else (lambda i: (0, i))

    def body(chip_ref, p_ref, own_ref, w_ref, m_ref, v_ref, *rest):
        _sum_adam_block(chip_ref, p_ref, own_ref, w_ref, m_ref, v_ref, *rest[n_after:])

    blk = pl.BlockSpec((None, tr, tc), lambda i, chip_ref: (0, *at(i)))
    out = jax.ShapeDtypeStruct((1, r, c), F32)
    return pl.pallas_call(
        body, name=name,
        grid_spec=pltpu.PrefetchScalarGridSpec(
            num_scalar_prefetch=1, grid=(r // tr if by_rows else c // tc,),
            in_specs=[pl.BlockSpec((N_CHIP, tr, tc), lambda i, chip_ref: (0, *at(i))),
                      pl.BlockSpec((None, tr, tc), lambda i, chip_ref: (chip_ref[0], *at(i))), blk, blk, blk]
            + [ANY] * n_after,
            out_specs=[blk] * 4),
        out_shape=[out] * 4,
        compiler_params=_params("parallel"),
    )(chip, parts, sums, w, m, v, *after)


def _adam_gains(total, ws, ms, vs):
    n = len(ws)
    widths = [w.shape[1] for w in ws]

    def body(t_ref, *refs):
        w_refs, m_refs, v_refs, outs = refs[:n], refs[n:2 * n], refs[2 * n:3 * n], refs[3 * n:]
        off = 0
        for i in range(n):
            g = t_ref[:, off:off + widths[i]]
            off += widths[i]
            g_ref, d_ref, mo_ref, vo_ref = outs[4 * i:4 * i + 4]
            g_ref[...] = g
            d_ref[...], mo_ref[...], vo_ref[...] = _adamw(w_refs[i][...], g, m_refs[i][...], v_refs[i][...])

    out = pl.pallas_call(
        body, name="adam_gains",
        out_shape=[jax.ShapeDtypeStruct(w.shape, F32) for w in ws for _ in range(4)],
    )(total, *ws, *ms, *vs)
    return [tuple(out[4 * i:4 * i + 4]) for i in range(n)]


def _adam_taps(total, first_col, device, w, m, v):
    _, n_taps, cw = w.shape
    col_block = lambda t, dev: (0, first_col // cw + t * N_DEV + dev[0])
    tap = pl.BlockSpec((None, 1, cw), lambda t, dev: (t, 0, 0))

    def body(dev_ref, t_ref, w_ref, m_ref, v_ref, g_ref, d_ref, mo_ref, vo_ref):
        g = t_ref[...]
        g_ref[...] = g
        d_ref[...], mo_ref[...], vo_ref[...] = _adamw(w_ref[...], g, m_ref[...], v_ref[...])

    shape3 = (n_taps, 1, cw)
    out = pl.pallas_call(
        body, name="adam_taps",
        grid_spec=pltpu.PrefetchScalarGridSpec(
            num_scalar_prefetch=1, grid=(n_taps,),
            in_specs=[pl.BlockSpec((1, cw), col_block), tap, tap, tap], out_specs=[tap] * 4),
        out_shape=[jax.ShapeDtypeStruct(shape3, F32)] * 4,
    )(device, total, w.reshape(shape3), m.reshape(shape3), v.reshape(shape3))
    return tuple(o.reshape(w.shape) for o in out)


def kernel(x, pre_mix_g, w_in, conv_w, q_norm_g, w_uq, kv_norm_g, w_ukv, conv_out_g, attn_out_g, w_o, post_mix_g, pre_mlp_g, w_up, w_down, post_mlp_g, loss_target, m_pre_mix_g, m_w_in, m_conv_w, m_q_norm_g, m_w_uq, m_kv_norm_g, m_w_ukv, m_conv_out_g, m_attn_out_g, m_w_o, m_post_mix_g, m_pre_mlp_g, m_w_up, m_w_down, m_post_mlp_g, v_pre_mix_g, v_w_in, v_conv_w, v_q_norm_g, v_w_uq, v_kv_norm_g, v_w_ukv, v_conv_out_g, v_attn_out_g, v_w_o, v_post_mix_g, v_pre_mlp_g, v_w_up, v_w_down, v_post_mlp_g):
    me = 4 * lax.axis_index("x") + 2 * lax.axis_index("y") + lax.axis_index("c")
    core = lax.axis_index("c").astype(jnp.int32).reshape(1)
    chip = (2 * lax.axis_index("x") + lax.axis_index("y")).astype(jnp.int32).reshape(1)
    gains = (pre_mix_g, q_norm_g, kv_norm_g, conv_out_g, attn_out_g, post_mix_g, pre_mlp_g, post_mlp_g)
    gain_m = (m_pre_mix_g, m_q_norm_g, m_kv_norm_g, m_conv_out_g, m_attn_out_g, m_post_mix_g, m_pre_mlp_g, m_post_mlp_g)
    gain_v = (v_pre_mix_g, v_q_norm_g, v_kv_norm_g, v_conv_out_g, v_attn_out_g, v_post_mix_g, v_pre_mlp_g, v_post_mlp_g)
    names = ("w_in", "w_uq", "w_ukv", "w_o", "w_up", "w_down")
    big = dict(zip(names, (w_in, w_uq, w_ukv, w_o, w_up, w_down)))
    big_m = dict(zip(names, (m_w_in, m_w_uq, m_w_ukv, m_w_o, m_w_up, m_w_down)))
    big_v = dict(zip(names, (v_w_in, v_w_uq, v_w_ukv, v_w_o, v_w_up, v_w_down)))
    n_heads = attn_out_g.shape[1] // HEAD
    n_taps = conv_w.shape[1]

    gathered = ("w_in", "conv", "w_uq", "w_ukv", "w_o", "w_up", "w_down")
    gather_groups = ((0, 1), (2, 3, 4), (5,), (6,))
    taps = jnp.pad(conv_w[0], ((0, SUBLANE - n_taps), (0, 0)))
    relayed_groups = (0, 2, 3)
    sems1, shards, lands, token = _gather_start("gather_start_first", [w_in[0].astype(BF16), taps], ((0, 1),), relayed=(0,))
    sems1, shards, lands = list(sems1), list(shards), list(lands)
    behind = token[0, 0]
    rest = list(lax.optimization_barrier(tuple((big[nm][0] + behind).astype(BF16) for nm in gathered[2:])))

    rest_lands = [lax.dynamic_update_index_in_dim(lax.empty((N_DEV, *a.shape), a.dtype), a, me, 0) for a in rest]

    def start_more(name, some, groups, relayed, after):
        sems_b, shards_b, lands_b, started = _gather_start(name, rest[some], groups, relayed=relayed, after=after,
                                                           lands=rest_lands[some])
        sems1.extend(sems_b)
        shards.extend(shards_b)
        lands.extend(lands_b)
        return started

    start_rest = lambda after: start_more("gather_start_rest", slice(0, 4), ((0, 1, 2), (3,)), (1,), after)
    start_last = lambda after: start_more("gather_start_last", slice(4, 5), ((0,),), (0,), after)

    cols = lambda a: jnp.concatenate([a[j] for j in range(N_DEV)], axis=1)
    rows = lambda a: a.reshape(N_DEV * a.shape[1], a.shape[2])
    device = me.astype(jnp.int32).reshape(1)
    own_in = lambda a, shard: lax.dynamic_update_index_in_dim(a, shard, me, 0)
    q_pieces = [(h, 0, HEAD) for h in range(n_heads)] + [(h, HEAD, QK) for h in range(n_heads)]
    ready = {
        "w_in": lambda a, shard: _join_col_shards("join_w_in", a, shard, device),
        "conv": lambda a, shard: cols(own_in(a, shard))[:n_taps],
        "w_uq": lambda a, shard: _join_col_shards("join_w_uq", a, shard, device, q_pieces),
        "w_ukv": lambda a, shard: cols(a),
        "w_o": lambda a, shard: rows(a),
        "w_up": lambda a, shard: a,
        "w_down": lambda a, shard: rows(a),
    }
    assert w_uq.shape[2] == QK

    class Weights:
        def __init__(self):
            self.passed, self.relayed = {}, {}

        def forward(self, group, after):
            idx = gather_groups[group]
            if group == 0:
                after = (*after, *rest_lands)
            self.passed[group] = _gather_forward(f"gather_forward_{group}", [shards[i] for i in idx], [lands[i] for i in idx],
                                                 *sems1[group], after, relayed=group in relayed_groups)
            return tuple(self.passed[group][1])

        def start(self, group, after):
            assert group == len(gather_groups) - 1
            return (start_last(after),)

        def relay(self, group, after):
            sems2, mid = self.passed[group]
            self.relayed[group], mid = _gather_relay_forward(f"gather_relay_{group}", mid, *sems2, after)
            self.passed[group] = (sems2, mid)
            if group == 0:
                start_rest(tuple(mid))
            return tuple(mid)

        def ready(self, group, after):
            sems2, mid = self.passed[group]
            full = _gather_wait(f"gather_wait_{group}", mid, *sems2, after, relay_sems=self.relayed.get(group))
            out = []
            return [ready[gathered[i]](a, shards[i]) for i, a in zip(gather_groups[group], full)]

    weights = Weights()

    col_blocks = lambda g: g.reshape(g.shape[0], N_DEV, g.shape[1] // N_DEV).transpose(1, 0, 2)
    row_blocks = lambda g: g.reshape(N_DEV, g.shape[0] // N_DEV, g.shape[1])
    grad_groups = (("w_down",), ("w_up",), ("w_o", "w_uq", "w_ukv"), ("w_in",))
    transposed = {"w_in": w_in.shape[2], "w_uq": w_uq.shape[2]}
    to_blocks = {
        "w_in": lambda g: g, "w_uq": lambda g: _unpermute_q_rows(g, n_heads),
        "w_ukv": col_blocks, "w_o": row_blocks, "w_up": lambda g: g, "w_down": row_blocks,
    }
    in_flight = []

    class Grads:
        def __init__(self):
            self.core = core
            self.away = {}

        def send_sums(self, group, sums):
            sems, sums, parts, tok = _chip_send_start(f"chip_send_start_{group}", list(sums))
            in_flight.append((sems, sums, parts))
            return (tok,)

        def full(self, group, arrays, received=None):
            nms = grad_groups[group]
            if received is None:
                blocks = [to_blocks[nm](g) for nm, g in zip(nms, arrays)]
                got = _pair_exchange(f"pair_exchange_{group}", blocks, [transposed.get(nm) for nm in nms])
            else:
                blocks, got = [self.away[group][1]], [self.received(group, received)]
            sums = [(_pair_sum_rows if nm in transposed else _pair_sum)(f"pair_sum_{nm}", g, r, core)
                    for nm, g, r in zip(nms, blocks, got)]
            return self.send_sums(group, sums)

        def send_away(self, group, half):
            nm = grad_groups[group][0]
            rows = transposed.get(nm)
            sems, src, land, tok = _pair_send_start(f"pair_send_start_{group}", half if rows is None else to_blocks[nm](half), rows)
            self.away[group] = (sems, src, land, rows)
            return (tok,)

        def received(self, group, after):
            sems, src, land, rows = self.away[group]
            return _pair_send_wait(f"pair_send_wait_{group}", sems, src, land, after, rows)

        def update_now(self, group, after):
            return update(str(group), group, group + 1, after)

    big_out = {}

    def update(tag, first, last, after):
        picked = [i for i in range(first, last) if grad_groups[i][0] not in big_out]
        groups = [in_flight[i] for i in picked]
        parts = _chip_send_wait("chip_send_wait_" + tag, groups, after)
        nms = [nm for i in picked for nm in grad_groups[i]]
        sums = [a for _, s, _ in groups for a in s]
        for nm, p, s in zip(nms, parts, sums):
            view = (lambda a: jnp.swapaxes(a, 1, 2)) if nm in transposed else (lambda a: a)
            out = _sum_adam("adam_" + nm, p, s, chip, view(big[nm]), view(big_m[nm]), view(big_v[nm]), after=after)
            after = (out[0],)
            big_out[nm] = [view(o) for o in out]
        return after

    grad_x, small = _local_step(x[0], loss_target[0], gains, weights, Grads(), first_after=(token,))

    after = update("early", 0, len(in_flight) - 1, (grad_x,))
    total = _small_all_reduce(small, after=after)
    update("late", len(in_flight) - 1, len(in_flight), (total,))
    big_out = [big_out[nm] for nm in names]

    gain_out = _adam_gains(total, gains, gain_m, gain_v)
    taps_out = _adam_taps(total, sum(g.shape[1] for g in gains), me.astype(jnp.int32).reshape(1), conv_w, m_conv_w, v_conv_w)
    loss = total[0, total.shape[1] - 1]

    order = (0, "w_in", "conv", 1, "w_uq", 2, "w_ukv", 3, 4, "w_o", 5, 6, "w_up", "w_down", 7)
    by_name = dict(zip(names, big_out))
    outs = [loss, grad_x[None]]
    for kind in range(4):
        for item in order:
            if item == "conv":
                outs.append(taps_out[kind])
            elif isinstance(item, int):
                outs.append(gain_out[item][kind])
            else:
                outs.append(by_name[item][kind])
    return tuple(outs)
```

```python
import math

import jax
import jax.numpy as jnp
from jax import lax
from jax.experimental import pallas as pl
from jax.experimental.pallas import tpu as pltpu

F32 = jnp.float32
BF16 = jnp.bfloat16

EPS = 1e-6
NEG_INF = -1e30
HEAD = 128
ROPE = 64
QK = HEAD + ROPE
CHUNK = 64
ROPE_THETA = 10000.0
ADAM_LR, ADAM_B1, ADAM_B2, ADAM_EPS, ADAM_WD, ADAM_STEP = 0.001, 0.9, 0.999, 1e-08, 0.01, 10

LANE = 128
SUBLANE = 8
VMEM_LIMIT_BYTES = 56 * 1024 * 1024

N_DEV = 8
N_CHIP = 4
MESH = pl.DeviceIdType.MESH


def _params(*sem):
    return pltpu.CompilerParams(dimension_semantics=sem, vmem_limit_bytes=VMEM_LIMIT_BYTES)


ANY = pl.BlockSpec(memory_space=pl.ANY)


def _call(body, *, in_specs, after=(), **kw):
    n_in, n_after = len(in_specs), len(after)

    def ordered(*refs):
        body(*refs[:n_in], *refs[n_in + n_after:])

    call = pl.pallas_call(ordered, in_specs=[*in_specs, *[ANY] * n_after], **kw)
    return lambda *operands: call(*operands, *after)


def _sublane_sum(v):
    r, w = v.shape
    return jnp.sum(v.reshape(r // SUBLANE, SUBLANE, w), axis=0)


def _rstd(x):
    return lax.rsqrt(jnp.mean(x * x, axis=-1, keepdims=True) + EPS)


def _rms_bwd(x, g, dy):
    r = _rstd(x)
    xh = x * r
    dxh = dy * g
    dx = r * (dxh - xh * jnp.mean(dxh * xh, axis=-1, keepdims=True))
    return dx, dy * xh


def _accumulate(ref, val, step):
    @pl.when(step == 0)
    def _():
        ref[...] = val

    @pl.when(step > 0)
    def _():
        ref[...] += val


NN = ((1,), (0,))
NT = ((1,), (1,))
TN = ((0,), (0,))


def _matmul(name, a, b, *, grid, a_spec, b_spec, out_shape, out_specs, contract, nk=1, acc_shape=None,
            extras=(), extra_specs=(), epilogue=None, after=()):
    multi = isinstance(out_shape, (tuple, list))
    out_shapes = tuple(out_shape) if multi else (out_shape,)
    n_out = len(out_shapes)
    n_extra = len(extras)

    def body(a_ref, b_ref, *rest):
        x_refs = rest[:n_extra]
        o_refs = rest[n_extra:n_extra + n_out]

        def emit(acc):
            vals = epilogue(acc, *[r[...] for r in x_refs]) if epilogue else (acc,)
            for r, v in zip(o_refs, vals):
                r[...] = v.astype(r.dtype)

        p = lax.dot_general(a_ref[...], b_ref[...], (contract, ((), ())), preferred_element_type=F32)
        if nk == 1:
            emit(p)
        else:
            acc_ref = rest[n_extra + n_out]
            k = pl.program_id(2)
            _accumulate(acc_ref, p, k)

            @pl.when(k == nk - 1)
            def _():
                emit(acc_ref[...])

    sem = ("parallel", "parallel") + (("arbitrary",) if nk > 1 else ())
    return _call(
        body, name=name, grid=grid, after=after,
        in_specs=[a_spec, b_spec, *extra_specs],
        out_specs=out_specs,
        out_shape=out_shape,
        scratch_shapes=[pltpu.VMEM(acc_shape, F32)] if nk > 1 else [],
        compiler_params=_params(*sem),
    )(a, b, *extras)


def _fit(n, tile):
    if n <= tile:
        return n
    t = tile - tile % LANE
    while n % t:
        t -= LANE
    return t


def _mm_nn(name, a, b, out_dtype, tm, tn, after=()):
    m, k = a.shape
    n = b.shape[1]
    tm, tn = _fit(m, tm), _fit(n, tn)
    return _matmul(name, a, b, grid=(m // tm, n // tn), after=after,
                   a_spec=pl.BlockSpec((tm, k), lambda i, j: (i, 0)),
                   b_spec=pl.BlockSpec((k, tn), lambda i, j: (0, j)),
                   out_shape=jax.ShapeDtypeStruct((m, n), out_dtype),
                   out_specs=pl.BlockSpec((tm, tn), lambda i, j: (i, j)), contract=NN)


def _mm_nt(name, a, b, out_dtype, tm, tn, after=()):
    m, k = a.shape
    n = b.shape[0]
    tm, tn = _fit(m, tm), _fit(n, tn)
    return _matmul(name, a, b, grid=(m // tm, n // tn), after=after,
                   a_spec=pl.BlockSpec((tm, k), lambda i, j: (i, 0)),
                   b_spec=pl.BlockSpec((tn, k), lambda i, j: (j, 0)),
                   out_shape=jax.ShapeDtypeStruct((m, n), out_dtype),
                   out_specs=pl.BlockSpec((tm, tn), lambda i, j: (i, j)), contract=NT)


def _mm_tn(name, a, b, out_dtype, tm, tn):
    s, m = a.shape
    n = b.shape[1]
    tm, tn = _fit(m, tm), _fit(n, tn)
    return _matmul(name, a, b, grid=(m // tm, n // tn),
                   a_spec=pl.BlockSpec((s, tm), lambda i, j: (0, i)),
                   b_spec=pl.BlockSpec((s, tn), lambda i, j: (0, j)),
                   out_shape=jax.ShapeDtypeStruct((m, n), out_dtype),
                   out_specs=pl.BlockSpec((tm, tn), lambda i, j: (i, j)), contract=TN)


ROWS = 256


def _row_spec(rows, width):
    return pl.BlockSpec((rows, width), lambda i: (i, 0))


def _fixed_spec(rows, width):
    return pl.BlockSpec((rows, width), lambda i: (0, 0))


def _column_pieces(rows, start, width):
    piece = math.gcd(start, width)
    assert piece % LANE == 0
    return [pl.BlockSpec((rows, piece), lambda i, b=start // piece + p: (i, b)) for p in range(width // piece)]


def _rms_fwd(name, x, g, after=()):
    s, w = x.shape
    rows = min(ROWS, s)

    def body(x_ref, g_ref, o_ref):
        xv = x_ref[...]
        o_ref[...] = (xv * _rstd(xv) * g_ref[...]).astype(o_ref.dtype)

    return _call(
        body, name=name, grid=(s // rows,), after=after,
        in_specs=[_row_spec(rows, w), _fixed_spec(1, w)],
        out_specs=_row_spec(rows, w),
        out_shape=jax.ShapeDtypeStruct((s, w), BF16),
        compiler_params=_params("parallel"),
    )(x, g)


def _norm_up(name, x, cols, g, w, after=()):
    s = x.shape[0]
    start, width = cols
    n = w.shape[1]
    tm = min(TILE_M, s)
    pieces = _column_pieces(tm, start, width)
    n_p = len(pieces)

    def body(*refs):
        g_ref, w_ref, xn_ref, o_ref = refs[n_p:]
        xv = refs[0][...] if n_p == 1 else jnp.concatenate([r[...] for r in refs[:n_p]], axis=1)
        xn = (xv * _rstd(xv) * g_ref[...]).astype(BF16)
        xn_ref[...] = xn
        o_ref[...] = jnp.dot(xn, w_ref[...], preferred_element_type=F32)

    return _call(
        body, name=name, grid=(s // tm,), after=after,
        in_specs=[*pieces, _fixed_spec(1, width), _fixed_spec(width, n)],
        out_specs=[_row_spec(tm, width), _row_spec(tm, n)],
        out_shape=[jax.ShapeDtypeStruct((s, width), BF16), jax.ShapeDtypeStruct((s, n), F32)],
        compiler_params=_params("parallel"),
    )(*[x] * n_p, g, w)


def _up_norm_bwd(name, dy, w, x, cols, g, into, tail=None, after=()):
    s, n = dy.shape
    start, width = cols
    tm = min(TILE_M, s)
    pieces = _column_pieces(tm, start, width)
    n_p = len(pieces)
    tails = () if tail is None else (tail,)
    out_width = width if tail is None else into.shape[1] - start
    assert start % out_width == 0 and into.dtype == BF16

    def body(dy_ref, w_ref, *refs):
        g_ref = refs[n_p]
        dx_ref, dg_ref = refs[-2:]
        xv = refs[0][...] if n_p == 1 else jnp.concatenate([r[...] for r in refs[:n_p]], axis=1)
        dxn = lax.dot_general(dy_ref[...], w_ref[...], (NT, ((), ())), preferred_element_type=F32)
        dx, dgc = _rms_bwd(xv, g_ref[...], dxn)
        dx_ref[:, :width] = dx.astype(dx_ref.dtype)
        if tails:
            t = refs[n_p + 1][...]
            dx_ref[:, width:width + t.shape[1]] = t
            dx_ref[:, width + t.shape[1]:] = jnp.zeros((tm, out_width - width - t.shape[1]), dx_ref.dtype)
        _accumulate(dg_ref, _sublane_sum(dgc), pl.program_id(0))

    n_in = 3 + n_p + len(tails)
    return _call(
        body, name=name, grid=(s // tm,), after=after,
        in_specs=[_row_spec(tm, n), _fixed_spec(width, n), *pieces, _fixed_spec(1, width)]
                 + [_row_spec(tm, t.shape[1]) for t in tails] + [ANY],
        out_specs=[pl.BlockSpec((tm, out_width), lambda i: (i, start // out_width)), _fixed_spec(SUBLANE, width)],
        out_shape=[jax.ShapeDtypeStruct(into.shape, into.dtype), jax.ShapeDtypeStruct((SUBLANE, width), F32)],
        input_output_aliases={n_in: 0},
        compiler_params=_params("arbitrary"),
    )(dy, w, *[x] * n_p, g, *tails, into)


def _mid_fwd(x, y, g_post, g_pre, after=()):
    s, w = x.shape
    rows = min(ROWS, s)

    def body(x_ref, y_ref, gp_ref, gq_ref, x2_ref, h2_ref):
        yv = y_ref[...]
        x2 = x_ref[...] + yv * _rstd(yv) * gp_ref[...]
        x2_ref[...] = x2
        h2_ref[...] = (x2 * _rstd(x2) * gq_ref[...]).astype(h2_ref.dtype)

    return _call(
        body, name="mid_fwd", grid=(s // rows,), after=after,
        in_specs=[_row_spec(rows, w), _row_spec(rows, w), _fixed_spec(1, w), _fixed_spec(1, w)],
        out_specs=[_row_spec(rows, w), _row_spec(rows, w)],
        out_shape=[jax.ShapeDtypeStruct((s, w), F32), jax.ShapeDtypeStruct((s, w), BF16)],
        compiler_params=_params("parallel"),
    )(x, y, g_post, g_pre)


def _head(m, x2, tgt, g):
    s, w = m.shape
    rows = min(ROWS, s)

    def body(m_ref, x2_ref, t_ref, g_ref, dout_ref, dm_ref, dg_ref, loss_ref):
        mv = m_ref[...]
        gv = g_ref[...]
        out = x2_ref[...] + mv * _rstd(mv) * gv
        err = out - t_ref[...]
        dout = err * (1.0 / w)
        dout_ref[...] = dout
        dm, dgc = _rms_bwd(mv, gv, dout)
        dm_ref[...] = dm.astype(dm_ref.dtype)
        sq = err * err
        lanes = sq[:, 0:LANE]
        for j in range(1, w // LANE):
            lanes = lanes + sq[:, j * LANE:(j + 1) * LANE]
        step = pl.program_id(0)
        _accumulate(dg_ref, _sublane_sum(dgc), step)
        _accumulate(loss_ref, _sublane_sum(lanes) * (0.5 / w), step)

    return pl.pallas_call(
        body, name="head", grid=(s // rows,),
        in_specs=[_row_spec(rows, w), _row_spec(rows, w), _row_spec(rows, w), _fixed_spec(1, w)],
        out_specs=[_row_spec(rows, w), _row_spec(rows, w), _fixed_spec(SUBLANE, w), _fixed_spec(SUBLANE, LANE)],
        out_shape=[jax.ShapeDtypeStruct((s, w), F32), jax.ShapeDtypeStruct((s, w), BF16),
                   jax.ShapeDtypeStruct((SUBLANE, w), F32), jax.ShapeDtypeStruct((SUBLANE, LANE), F32)],
        compiler_params=_params("arbitrary"),
    )(m, x2, tgt, g)


def _mid_bwd(x2, y, d_out, d_h2, g_pre, g_post, after=()):
    s, w = x2.shape
    rows = min(ROWS, s)

    def body(x2_ref, y_ref, dout_ref, dh2_ref, gq_ref, gp_ref, dx2_ref, dy_ref, dgq_ref, dgp_ref):
        dx, dgq = _rms_bwd(x2_ref[...], gq_ref[...], dh2_ref[...])
        dx2 = dout_ref[...] + dx
        dx2_ref[...] = dx2
        dy, dgp = _rms_bwd(y_ref[...], gp_ref[...], dx2)
        dy_ref[...] = dy.astype(dy_ref.dtype)
        step = pl.program_id(0)
        _accumulate(dgq_ref, _sublane_sum(dgq), step)
        _accumulate(dgp_ref, _sublane_sum(dgp), step)

    return _call(
        body, name="mid_bwd", grid=(s // rows,), after=after,
        in_specs=[_row_spec(rows, w)] * 4 + [_fixed_spec(1, w)] * 2,
        out_specs=[_row_spec(rows, w), _row_spec(rows, w), _fixed_spec(SUBLANE, w), _fixed_spec(SUBLANE, w)],
        out_shape=[jax.ShapeDtypeStruct((s, w), F32), jax.ShapeDtypeStruct((s, w), BF16),
                   jax.ShapeDtypeStruct((SUBLANE, w), F32), jax.ShapeDtypeStruct((SUBLANE, w), F32)],
        compiler_params=_params("arbitrary"),
    )(x2, y, d_out, d_h2, g_pre, g_post)


def _first_bwd(x, g, d_h1, d_x2, after=()):
    s, w = x.shape
    rows = min(ROWS, s)

    def body(x_ref, g_ref, dh_ref, dx2_ref, dx_ref, dg_ref):
        dx, dgc = _rms_bwd(x_ref[...], g_ref[...], dh_ref[...])
        dx_ref[...] = dx2_ref[...] + dx
        _accumulate(dg_ref, _sublane_sum(dgc), pl.program_id(0))

    return _call(
        body, name="first_bwd", grid=(s // rows,), after=after,
        in_specs=[_row_spec(rows, w), _fixed_spec(1, w), _row_spec(rows, w), _row_spec(rows, w)],
        out_specs=[_row_spec(rows, w), _fixed_spec(SUBLANE, w)],
        out_shape=[jax.ShapeDtypeStruct((s, w), F32), jax.ShapeDtypeStruct((SUBLANE, w), F32)],
        compiler_params=_params("arbitrary"),
    )(x, g, d_h1, d_x2)


def _shift_down(v, k):
    t = lax.broadcasted_iota(jnp.int32, v.shape, 0)
    return jnp.where(t >= k, pltpu.roll(v, k, 0), 0.0)


def _shift_up(v, k):
    n = v.shape[0]
    t = lax.broadcasted_iota(jnp.int32, v.shape, 0)
    return jnp.where(t < n - k, pltpu.roll(v, n - k, 0), 0.0)


def _conv_core(u, b, c, w):
    z = c * u
    conv = w[0:1, :] * _shift_down(z, 2) + w[1:2, :] * _shift_down(z, 1) + w[2:3, :] * z
    return z, conv, b * conv


def _conv_fwd(proj, conv_w, g, n_groups, out_width, after=()):
    s = proj.shape[0]

    def body(u_ref, b_ref, c_ref, w_ref, g_ref, o_ref):
        _, _, yr = _conv_core(u_ref[...], b_ref[...], c_ref[...], w_ref[...])
        o_ref[...] = (yr * _rstd(yr) * g_ref[...]).astype(o_ref.dtype)

    col = lambda k: pl.BlockSpec((s, HEAD), lambda i: (0, k * n_groups + i))
    return _call(
        body, name="conv_fwd", grid=(n_groups,), after=after,
        in_specs=[col(0), col(1), col(2), pl.BlockSpec((3, HEAD), lambda i: (0, i)), pl.BlockSpec((1, HEAD), lambda i: (0, i))],
        out_specs=pl.BlockSpec((s, HEAD), lambda i: (0, i)),
        out_shape=jax.ShapeDtypeStruct((s, out_width), BF16),
        compiler_params=_params("parallel"),
    )(proj, proj, proj, conv_w, g)


def _conv_bwd(proj, d_mix, conv_w, g, n_groups, out_width):
    s = proj.shape[0]
    width = n_groups * HEAD

    def body(u_ref, b_ref, c_ref, dy_ref, w_ref, g_ref, dproj_ref, dg_ref, dw_ref, buf, sems):
        i = pl.program_id(0)
        slot = i % 2

        def copies(group, slot):
            return [pltpu.make_async_copy(buf.at[slot, k], dproj_ref.at[:, pl.ds(pl.multiple_of((k * n_groups + group) * HEAD, HEAD), HEAD)],
                                          sems.at[slot, k]) for k in range(3)]

        @pl.when(i >= 2)
        def _():
            for cp in copies(i - 2, slot):
                cp.wait()

        u, b, c, w = u_ref[...], b_ref[...], c_ref[...], w_ref[...]
        z, conv, yr = _conv_core(u, b, c, w)
        dyr, dgc = _rms_bwd(yr, g_ref[...], dy_ref[...])
        dconv = dyr * b
        dz = w[2:3, :] * dconv + w[1:2, :] * _shift_up(dconv, 1) + w[0:1, :] * _shift_up(dconv, 2)
        buf[slot, 0] = (dz * c).astype(buf.dtype)
        buf[slot, 1] = (dyr * conv).astype(buf.dtype)
        buf[slot, 2] = (dz * u).astype(buf.dtype)
        for cp in copies(i, slot):
            cp.start()
        dg_ref[...] = _sublane_sum(dgc)
        dw_ref[0] = _sublane_sum(dconv * _shift_down(z, 2))
        dw_ref[1] = _sublane_sum(dconv * _shift_down(z, 1))
        dw_ref[2] = _sublane_sum(dconv * z)

        @pl.when(i == n_groups - 1)
        def _():
            for back in range(min(2, n_groups)):
                for cp in copies(i - back, (n_groups - 1 - back) % 2):
                    cp.wait()

    col = lambda k: pl.BlockSpec((s, HEAD), lambda i: (0, k * n_groups + i))
    grp = pl.BlockSpec((s, HEAD), lambda i: (0, i))
    return pl.pallas_call(
        body, name="conv_bwd", grid=(n_groups,),
        in_specs=[col(0), col(1), col(2), grp, pl.BlockSpec((3, HEAD), lambda i: (0, i)), pl.BlockSpec((1, HEAD), lambda i: (0, i))],
        out_specs=[ANY, pl.BlockSpec((SUBLANE, HEAD), lambda i: (0, i)), pl.BlockSpec((3, SUBLANE, HEAD), lambda i: (0, 0, i))],
        out_shape=[jax.ShapeDtypeStruct((s, out_width), BF16),
                   jax.ShapeDtypeStruct((SUBLANE, width), F32), jax.ShapeDtypeStruct((3, SUBLANE, width), F32)],
        scratch_shapes=[pltpu.VMEM((2, 3, s, HEAD), BF16), pltpu.SemaphoreType.DMA((2, 3))],
        compiler_params=_params("arbitrary"),
    )(proj, proj, proj, d_mix, conv_w, g)


def _rope_tables(s, n_heads):
    pos = jnp.arange(s, dtype=F32)
    inv_freq = jnp.power(ROPE_THETA, -jnp.arange(0, ROPE, 2, dtype=F32) / ROPE)
    ang = pos[:, None] * inv_freq[None, :]
    cos, sin = jnp.cos(ang), jnp.sin(ang)
    cs = jnp.concatenate([cos, cos], axis=1)
    sn = jnp.concatenate([-sin, sin], axis=1)
    pad = jnp.zeros((s, LANE - ROPE), F32)
    return (jnp.tile(cs, (1, n_heads)), jnp.tile(sn, (1, n_heads)),
            jnp.concatenate([cs, pad], axis=1), jnp.concatenate([sn, pad], axis=1))


def _swap_halves(v):
    w = v.shape[1]
    lane = lax.broadcasted_iota(jnp.int32, v.shape, 1)
    first = (lane % ROPE) < (ROPE // 2)
    return jnp.where(first, pltpu.roll(v, w - ROPE // 2, 1), pltpu.roll(v, ROPE // 2, 1))


def _pack_heads(q, kv, proj, kr_col, tables, n_heads, after=()):
    s = q.shape[0]
    rows = min(ROWS, s)
    cq, sq, ck, sk = tables
    wq = n_heads * ROPE

    def body(q_ref, kv_ref, kr_ref, cq_ref, sq_ref, ck_ref, sk_ref, qo_ref, ko_ref, vo_ref):
        qr = q_ref[:, n_heads * HEAD:]
        qr = qr * cq_ref[...] + _swap_halves(qr) * sq_ref[...]
        krv = kr_ref[...]
        krv = krv * ck_ref[...] + _swap_halves(krv) * sk_ref[...]
        for h in range(n_heads):
            qo_ref[h] = jnp.concatenate([q_ref[:, h * HEAD:(h + 1) * HEAD], qr[:, h * ROPE:(h + 1) * ROPE]], axis=1).astype(BF16)
            ko_ref[h] = jnp.concatenate([kv_ref[:, 2 * h * HEAD:(2 * h + 1) * HEAD], krv[:, :ROPE]], axis=1).astype(BF16)
            vo_ref[h] = kv_ref[:, (2 * h + 1) * HEAD:(2 * h + 2) * HEAD].astype(BF16)

    hs = lambda w: pl.BlockSpec((n_heads, rows, w), lambda i: (0, i, 0))
    return _call(
        body, name="pack_heads", grid=(s // rows,), after=after,
        in_specs=[_row_spec(rows, q.shape[1]), _row_spec(rows, kv.shape[1]), pl.BlockSpec((rows, LANE), lambda i: (i, kr_col // LANE)),
                  _row_spec(rows, wq), _row_spec(rows, wq), _row_spec(rows, LANE), _row_spec(rows, LANE)],
        out_specs=[hs(QK), hs(QK), hs(HEAD)],
        out_shape=[jax.ShapeDtypeStruct((n_heads, s, QK), BF16), jax.ShapeDtypeStruct((n_heads, s, QK), BF16),
                   jax.ShapeDtypeStruct((n_heads, s, HEAD), BF16)],
        compiler_params=_params("parallel"),
    )(q, kv, proj, cq, sq, ck, sk)


def _unpack_heads(dq, dk, dv, tables, n_heads):
    s = dq.shape[1]
    rows = min(ROWS, s)
    cq, sq, ck, sk = tables
    wq = n_heads * ROPE

    def body(dq_ref, dk_ref, dv_ref, cq_ref, sq_ref, ck_ref, sk_ref, qo_ref, kvo_ref, kro_ref):
        dqr = jnp.concatenate([dq_ref[h][:, HEAD:] for h in range(n_heads)], axis=1)
        dqr = dqr * cq_ref[...] - _swap_halves(dqr) * sq_ref[...]
        dkr = dk_ref[0][:, HEAD:]
        for h in range(1, n_heads):
            dkr = dkr + dk_ref[h][:, HEAD:]
        dkr = jnp.concatenate([dkr, jnp.zeros((rows, LANE - ROPE), F32)], axis=1)
        dkr = dkr * ck_ref[...] - _swap_halves(dkr) * sk_ref[...]
        kro_ref[...] = dkr.astype(kro_ref.dtype)
        qo_ref[:, n_heads * HEAD:] = dqr.astype(qo_ref.dtype)
        for h in range(n_heads):
            qo_ref[:, h * HEAD:(h + 1) * HEAD] = dq_ref[h][:, :HEAD].astype(qo_ref.dtype)
            kvo_ref[:, 2 * h * HEAD:(2 * h + 1) * HEAD] = dk_ref[h][:, :HEAD].astype(kvo_ref.dtype)
            kvo_ref[:, (2 * h + 1) * HEAD:(2 * h + 2) * HEAD] = dv_ref[h].astype(kvo_ref.dtype)

    hs = lambda w: pl.BlockSpec((n_heads, rows, w), lambda i: (0, i, 0))
    return pl.pallas_call(
        body, name="unpack_heads", grid=(s // rows,),
        in_specs=[hs(QK), hs(QK), hs(HEAD), _row_spec(rows, wq), _row_spec(rows, wq), _row_spec(rows, LANE), _row_spec(rows, LANE)],
        out_specs=[_row_spec(rows, n_heads * QK), _row_spec(rows, 2 * n_heads * HEAD), _row_spec(rows, LANE)],
        out_shape=[jax.ShapeDtypeStruct((s, n_heads * QK), BF16), jax.ShapeDtypeStruct((s, 2 * n_heads * HEAD), BF16),
                   jax.ShapeDtypeStruct((s, LANE), BF16)],
        compiler_params=_params("parallel"),
    )(dq, dk, dv, cq, sq, ck, sk)


TQ = 256


LOG2_E = 1.4426950408889634


def _softmax_parts(q, k):
    tq, n_keys = q.shape[0], k.shape[0]
    sc = lax.dot_general(q, k, (NT, ((), ())), preferred_element_type=F32) * (QK ** -0.5 * LOG2_E)
    row = lax.broadcasted_iota(jnp.int32, (tq, tq), 0)
    col = lax.broadcasted_iota(jnp.int32, (tq, tq), 1)
    own = jnp.where(col // CHUNK <= row // CHUNK, sc[:, n_keys - tq:], NEG_INF)
    sc = own if n_keys == tq else jnp.concatenate([sc[:, :n_keys - tq], own], axis=1)
    e = jnp.exp2(sc - jnp.max(sc, axis=-1, keepdims=True))
    return e, 1.0 / jnp.sum(e, axis=-1, keepdims=True)


def _prob_columns(c, tq):
    return pl.ds(tq * (c * (c + 1) // 2), (c + 1) * tq)


def _attn_fwd(q, k, v, g, mix, col0, first, between):
    n_heads, s, _ = q.shape
    tq = min(TQ, s)
    assert tq % CHUNK == 0 and s % tq == 0
    n_blocks = s // tq
    p_cols = tq * (n_blocks * (n_blocks + 1) // 2)
    out_shape = [jax.ShapeDtypeStruct((n_heads, s, HEAD), F32), jax.ShapeDtypeStruct((n_heads, tq, p_cols), BF16),
                 jax.ShapeDtypeStruct(mix.shape, mix.dtype)]
    done, after = (mix,), ()
    for part, (h0, h1) in enumerate(((0, first), (first, n_heads))):

        def body(q_ref, k_ref, v_ref, g_ref, *rest):
            o_ref, p_ref, y_ref = rest[-3:]
            for c in range(n_blocks):
                rows, n_keys = pl.ds(c * tq, tq), (c + 1) * tq
                e, inv = _softmax_parts(q_ref[rows, :], k_ref[0:n_keys, :])
                p = (e * inv).astype(BF16)
                p_ref[:, _prob_columns(c, tq)] = p
                o = jnp.dot(p, v_ref[0:n_keys, :], preferred_element_type=F32)
                o_ref[rows, :] = o
                y_ref[rows, :] = (o * _rstd(o) * g_ref[...]).astype(y_ref.dtype)

        head = lambda w, h0=h0: pl.BlockSpec((None, s, w), lambda h: (h0 + h, 0, 0))
        n_done = len(done)
        done = _call(
            body, name=f"attn_fwd_{part}", grid=(h1 - h0,), after=after,
            in_specs=[head(QK), head(QK), head(HEAD), pl.BlockSpec((1, HEAD), lambda h, h0=h0: (0, h0 + h))] + [ANY] * n_done,
            out_specs=[head(HEAD), pl.BlockSpec((None, tq, p_cols), lambda h, h0=h0: (h0 + h, 0, 0)),
                       pl.BlockSpec((s, HEAD), lambda h, h0=h0: (0, col0 // HEAD + h0 + h))],
            out_shape=out_shape,
            input_output_aliases={4 + i: 3 - n_done + i for i in range(n_done)},
            compiler_params=_params("parallel"),
        )(q, k, v, g, *done)
        after = between(done) if part == 0 else ()
    return done


def _attn_bwd(q, k, v, o, probs, d_mix, g, col0, after=()):
    n_heads, s, _ = q.shape
    tq = probs.shape[1]

    def body(q_ref, k_ref, v_ref, o_ref, p_ref, dy_ref, g_ref, dq_ref, dk_ref, dv_ref, dg_ref):
        dg = None
        for c in reversed(range(s // tq)):
            rows, n_keys = pl.ds(c * tq, tq), (c + 1) * tq
            o = o_ref[rows, :]
            do, dgc = _rms_bwd(o, g_ref[...], dy_ref[rows, :])
            do = do.astype(BF16)
            dg = _sublane_sum(dgc) if dg is None else dg + _sublane_sum(dgc)
            p = p_ref[:, _prob_columns(c, tq)]
            dp = lax.dot_general(do, v_ref[0:n_keys, :], (NT, ((), ())), preferred_element_type=F32)
            ds = (p.astype(F32) * (dp - jnp.sum(do.astype(F32) * o, axis=-1, keepdims=True))).astype(BF16)
            dq_ref[rows, :] = jnp.dot(ds, k_ref[0:n_keys, :], preferred_element_type=F32) * (QK ** -0.5)
            dk = lax.dot_general(ds, q_ref[rows, :], (TN, ((), ())), preferred_element_type=F32)
            dv = lax.dot_general(p, do, (TN, ((), ())), preferred_element_type=F32)
            if n_keys == s:
                dk_ref[...] = dk
                dv_ref[...] = dv
            else:
                dk_ref[0:n_keys, :] += dk
                dv_ref[0:n_keys, :] += dv
        dk_ref[...] = dk_ref[...] * (QK ** -0.5)
        dg_ref[...] = dg

    c0 = col0 // HEAD
    head = lambda w: pl.BlockSpec((None, s, w), lambda h, *_: (h, 0, 0))
    in_specs = [head(QK), head(QK), head(HEAD), head(HEAD), pl.BlockSpec((None, tq, probs.shape[2]), lambda h, *_: (h, 0, 0)),
                pl.BlockSpec((s, HEAD), lambda h, *_: (0, c0 + h)), pl.BlockSpec((1, HEAD), lambda h, *_: (0, h))]
    out_specs = [head(QK), head(QK), head(HEAD), pl.BlockSpec((SUBLANE, HEAD), lambda h, *_: (0, h))]
    out_shape = [jax.ShapeDtypeStruct((n_heads, s, QK), F32), jax.ShapeDtypeStruct((n_heads, s, QK), F32),
                 jax.ShapeDtypeStruct((n_heads, s, HEAD), F32), jax.ShapeDtypeStruct((SUBLANE, n_heads * HEAD), F32)]
    return _call(body, name="attn_bwd", grid=(n_heads,), after=after, in_specs=in_specs, out_specs=out_specs,
                 out_shape=out_shape, compiler_params=_params("parallel"))(q, k, v, o, probs, d_mix, g)


TILE_M = 1024
TILE_N = 1024


def _up_fwd(h2, w_up, between):
    s, d = h2.shape
    nb, _, fb = w_up.shape
    tm = min(TILE_M, s)
    done, after = (), ()
    for tile in range(s // tm):

        def body(h_ref, w_ref, *rest):
            a_ref, r_ref = rest[-2:]
            r = jnp.maximum(jnp.dot(h_ref[...], w_ref[...], preferred_element_type=F32), 0.0)
            a_ref[...] = (r * r).astype(a_ref.dtype)
            r_ref[...] = r.astype(r_ref.dtype)

        blk = pl.BlockSpec((tm, fb), lambda j, tile=tile: (tile, j))
        done = _call(
            body, name=f"up_fwd_{tile}", grid=(nb,), after=after,
            in_specs=[pl.BlockSpec((tm, d), lambda j, tile=tile: (tile, 0)), pl.BlockSpec((None, d, fb), lambda j: (j, 0, 0))]
                     + [ANY] * len(done),
            out_specs=[blk, blk], out_shape=[jax.ShapeDtypeStruct((s, nb * fb), BF16)] * 2,
            input_output_aliases={2 + i: i for i in range(len(done))},
            compiler_params=_params("parallel"),
        )(h2, w_up, *done)
        after = between(done) if tile == 0 else ()
    return done


def _down_fwd(a, w_down):
    s, f = a.shape
    d = w_down.shape[1]
    tm, tn, tk = min(TILE_M,s), min(TILE_N,d), 2048
    nk = f // tk
    return _matmul("down_fwd", a, w_down, grid=(s // tm, d // tn, nk),
                   a_spec=pl.BlockSpec((tm, tk), lambda i, j, k: (i, k)),
                   b_spec=pl.BlockSpec((tk, tn), lambda i, j, k: (k, j)),
                   out_shape=jax.ShapeDtypeStruct((s, d), F32),
                   out_specs=pl.BlockSpec((tm, tn), lambda i, j, k: (i, j)),
                   contract=NN, nk=nk, acc_shape=(tm, tn))


def _down_bwd_act(d_m, w_down, r, after=()):
    s, d = d_m.shape
    f = w_down.shape[0]
    tm, tn = min(TILE_M,s), min(TILE_N,f)
    blk = pl.BlockSpec((tm, tn), lambda i, j: (i, j))
    return _matmul("down_bwd_act", d_m, w_down, grid=(s // tm, f // tn), after=after,
                   a_spec=pl.BlockSpec((tm, d), lambda i, j: (i, 0)),
                   b_spec=pl.BlockSpec((tn, d), lambda i, j: (j, 0)),
                   out_shape=jax.ShapeDtypeStruct((s, f), BF16), out_specs=blk, contract=NT,
                   extras=(r,), extra_specs=(blk,),
                   epilogue=lambda acc, rv: (acc * (2.0 * rv.astype(F32)),))


def _up_bwd_act(d_up, w_up, after=()):
    s, _ = d_up.shape
    nb, d, fb = w_up.shape
    tm, tn = min(TILE_M, s), min(TILE_N,d)
    pair = 2
    n_after = len(after)

    def body(a_ref, w_ref, *rest):
        o_ref, acc_ref = rest[n_after:]
        k = pl.program_id(2)
        p = None
        for t in range(pair):
            term = lax.dot_general(a_ref[:, t * fb:(t + 1) * fb], w_ref[t], (NT, ((), ())), preferred_element_type=F32)
            p = term if p is None else p + term
        _accumulate(acc_ref, p, k)

        @pl.when(k == nb // pair - 1)
        def _():
            o_ref[...] = acc_ref[...]

    return pl.pallas_call(
        body, name="up_bwd_act", grid=(s // tm, d // tn, nb // pair),
        in_specs=[pl.BlockSpec((tm, pair * fb), lambda i, j, k: (i, k)),
                  pl.BlockSpec((pair, tn, fb), lambda i, j, k: (k, j, 0))] + [ANY] * n_after,
        out_specs=pl.BlockSpec((tm, tn), lambda i, j, k: (i, j)),
        out_shape=jax.ShapeDtypeStruct((s, d), F32),
        scratch_shapes=[pltpu.VMEM((tm, tn), F32)],
        compiler_params=_params("parallel", "parallel", "arbitrary"),
    )(d_up, w_up, *after)


def _half_grad(name, a, b, core, home, received, after, *, grid, a_block, a_map, b_block, b_map, o_block, o_map, out_shape):
    n_after = len(after)
    pick = (lambda ref: ref[0]) if home else (lambda ref: 1 - ref[0])

    def body(core_ref, a_ref, b_ref, *rest):
        acc = lax.dot_general(a_ref[...], b_ref[...], (TN, ((), ())), preferred_element_type=F32)
        if received is not None:
            acc = acc + rest[0][...].astype(F32)
        rest[-1][...] = acc.astype(rest[-1].dtype)

    wrap = lambda fn: (lambda i, j, core_ref: fn(i, j, pick(core_ref)))
    o_spec = pl.BlockSpec(o_block, wrap(o_map))
    extra = [] if received is None else [o_spec]
    operands = [] if received is None else [received]
    return pl.pallas_call(
        body, name=name,
        grid_spec=pltpu.PrefetchScalarGridSpec(
            num_scalar_prefetch=1, grid=grid,
            in_specs=[pl.BlockSpec(a_block, wrap(a_map)), pl.BlockSpec(b_block, wrap(b_map))] + extra + [ANY] * n_after,
            out_specs=o_spec),
        out_shape=out_shape,
        compiler_params=_params("parallel", "parallel"),
    )(core, a, b, *operands, *after)


def _down_half_grad(name, a, d_m, core, home, received=None, after=()):
    s, f = a.shape
    d = d_m.shape[1]
    r = f // N_DEV
    tn = min(TILE_N, d)
    return _half_grad(name, a, d_m, core, home, received, after, grid=(N_CHIP, d // tn),
                      a_block=(s, r), a_map=lambda k, j, p: (0, 2 * k + p),
                      b_block=(s, tn), b_map=lambda k, j, p: (0, j),
                      o_block=(None, r, tn), o_map=lambda k, j, p: (k, 0, j),
                      out_shape=jax.ShapeDtypeStruct((N_CHIP, r, d), BF16))


def _up_half_grad(name, h2, d_up, core, home, received=None, after=()):
    s, d = h2.shape
    fb = d_up.shape[1] // N_DEV
    tm = min(TILE_M, d)
    return _half_grad(name, h2, d_up, core, home, received, after, grid=(d // tm, N_CHIP),
                      a_block=(s, tm), a_map=lambda i, k, p: (0, i),
                      b_block=(s, fb), b_map=lambda i, k, p: (0, 2 * k + p),
                      o_block=(None, tm, fb), o_map=lambda i, k, p: (k, i, 0),
                      out_shape=jax.ShapeDtypeStruct((N_CHIP, d, fb), BF16))


MXU_WIDTH = 256


def _in_pad(in_width):
    return -(-in_width // MXU_WIDTH) * MXU_WIDTH


def _join_col_shards(name, blocks, own, device, pieces=None):
    n, r, w = blocks.shape
    rows = min(ROWS, r)
    pieces = pieces or [(j, 0, w) for j in range(n)]
    used = sum(b - a for _, a, b in pieces)
    width = _in_pad(used)

    def body(dev_ref, x_ref, own_ref, o_ref):
        block = lambda j: jnp.where(dev_ref[0] == j, own_ref[...], x_ref[j])
        cols = [block(j)[:, a:b] for j, a, b in pieces]
        tail = [jnp.zeros((rows, width - used), o_ref.dtype)] if width > used else []
        o_ref[...] = jnp.concatenate(cols + tail, axis=1)

    return pl.pallas_call(
        body, name=name,
        grid_spec=pltpu.PrefetchScalarGridSpec(
            num_scalar_prefetch=1, grid=(r // rows,),
            in_specs=[pl.BlockSpec((n, rows, w), lambda i, dev: (0, i, 0)), pl.BlockSpec((rows, w), lambda i, dev: (i, 0))],
            out_specs=pl.BlockSpec((rows, width), lambda i, dev: (i, 0))),
        out_shape=jax.ShapeDtypeStruct((r, width), blocks.dtype),
        compiler_params=_params("parallel"),
    )(device, blocks, own)


def _unpermute_q_rows(wt, n_heads):
    r = wt.shape[1]
    nope = wt[:n_heads * HEAD].reshape(n_heads, HEAD, r)
    rope = wt[n_heads * HEAD:].reshape(n_heads, ROPE, r)
    return jnp.concatenate([nope, rope], axis=1).reshape(n_heads * QK, r)


def _local_step(x, tgt, gains, weights, grads, first_after=()):
    pre_mix_g, q_norm_g, kv_norm_g, conv_out_g, attn_out_g, post_mix_g, pre_mlp_g, post_mlp_g = gains
    s, d = x.shape
    conv_width = conv_out_g.shape[1]
    n_groups = conv_width // HEAD
    r_q, r_kv = q_norm_g.shape[1], kv_norm_g.shape[1]
    n_heads = attn_out_g.shape[1] // HEAD
    c_q0 = 3 * conv_width
    c_kv0 = c_q0 + r_q
    c_kr0 = c_kv0 + r_kv
    in_pad = _in_pad(c_kr0 + ROPE)
    tn_in = _fit(in_pad, 6 * MXU_WIDTH)
    tables = _rope_tables(s, n_heads)

    h1 = _rms_fwd("pre_mix_norm", x, pre_mix_g, after=first_after)
    weights.forward(0, (h1,))
    weights.relay(0, tables)
    w_in_p, conv_w = weights.ready(0, ())
    proj = _mm_nn("in_proj", h1, w_in_p, F32, TILE_M, tn_in)
    y_conv = _conv_fwd(proj, conv_w, conv_out_g, n_groups, conv_width + n_heads * HEAD, after=weights.forward(1, (proj,)))
    w_uq_p, w_ukv, w_o = weights.ready(1, (y_conv,))
    qn, q = _norm_up("q_up", proj, (c_q0, r_q), q_norm_g, w_uq_p)
    kvn, kv = _norm_up("kv_up", proj, (c_kv0, r_kv), kv_norm_g, w_ukv)
    qh, kh, vh = _pack_heads(q, kv, proj, c_kr0, tables, n_heads)
    o, probs, mix = _attn_fwd(qh, kh, vh, attn_out_g, y_conv, conv_width, n_heads // 4,
                              lambda done: weights.forward(2, tuple(done)))
    y = _mm_nn("out_proj", mix, w_o, F32, TILE_M, TILE_N, after=weights.start(3, (mix,)))
    x2, h2 = _mid_fwd(x, y, post_mix_g, pre_mlp_g, after=weights.relay(2, (y,)))
    (w_up,) = weights.ready(2, (h2,))
    a, r = _up_fwd(h2, w_up, lambda done: weights.forward(3, tuple(done)))
    weights.relay(3, (a,))
    (w_down,) = weights.ready(3, ())
    m = _down_fwd(a, w_down)

    d_out, d_m, dg_post_mlp, loss_part = _head(m, x2, tgt, post_mlp_g)
    core = grads.core
    away = _down_half_grad("down_bwd_w_away", a, d_m, core, home=False)
    d_up = _down_bwd_act(d_m, w_down, r, after=grads.send_away(0, away))
    sums = _down_half_grad("down_bwd_w_home", a, d_m, core, home=True, received=grads.received(0, (d_up,)))
    away = _up_half_grad("up_bwd_w_away", h2, d_up, core, home=False, after=grads.send_sums(0, (sums,)))
    d_h2 = _up_bwd_act(d_up, w_up, after=grads.send_away(1, away))
    sums = _up_half_grad("up_bwd_w_home", h2, d_up, core, home=True, received=grads.received(1, (d_h2,)))
    d_x2, d_y, dg_pre_mlp, dg_post_mix = _mid_bwd(x2, y, d_out, d_h2, pre_mlp_g, post_mix_g, after=grads.send_sums(1, (sums,)))
    d_mix = _mm_nt("out_proj_bwd_act", d_y, w_o, F32, TILE_M, TILE_N)
    gw_o = _mm_tn("out_proj_bwd_w", mix, d_y, BF16, TILE_M, TILE_N)
    dqh, dkh, dvh, dg_attn = _attn_bwd(qh, kh, vh, o, probs, d_mix, attn_out_g, conv_width)
    d_q, d_kv, d_kr = _unpack_heads(dqh, dkh, dvh, tables, n_heads)
    gw_uq_t = _mm_tn("q_up_bwd_w", d_q, qn, F32, TILE_M, TILE_N)
    gw_ukv = _mm_tn("kv_up_bwd_w", kvn, d_kv, BF16, TILE_M, TILE_N)
    d_proj, dg_conv, dw_conv = _conv_bwd(proj, d_mix, conv_w, conv_out_g, n_groups, in_pad)
    d_proj, dg_q = _up_norm_bwd("q_up_bwd_act", d_q, w_uq_p, proj, (c_q0, r_q), q_norm_g, d_proj,
                                after=grads.full(2, (gw_o, gw_uq_t, gw_ukv)))
    d_proj, dg_kv = _up_norm_bwd("kv_up_bwd_act", d_kv, w_ukv, proj, (c_kv0, r_kv), kv_norm_g, d_proj, tail=d_kr)
    gw_in_t = _mm_tn("in_proj_bwd_w", d_proj, h1, F32, tn_in, TILE_N)
    updated = grads.update_now(0, grads.send_away(3, gw_in_t))
    d_h1 = _mm_nt("in_proj_bwd_act", d_proj, w_in_p, F32, TILE_M, TILE_N, after=grads.full(3, (gw_in_t,), received=updated))
    grad_x, dg_pre_mix = _first_bwd(x, pre_mix_g, d_h1, d_x2)

    small = [dg_pre_mix, dg_q, dg_kv, dg_conv, dg_attn, dg_post_mix, dg_pre_mlp, dg_post_mlp,
             dw_conv[0], dw_conv[1], dw_conv[2], loss_part]
    return grad_x, jnp.concatenate(small, axis=1)


HBM = pl.BlockSpec(memory_space=pltpu.HBM)
SEM = pl.BlockSpec(memory_space=pltpu.SEMAPHORE)
IN_VMEM = pl.BlockSpec(memory_space=pltpu.VMEM)
SPLIT = pltpu.CompilerParams(has_side_effects=pltpu.SideEffectType.DATAFLOW_SIDE_EFFECTING)


def _in_hbm(a):
    return pltpu.with_memory_space_constraint(a, pltpu.HBM)


def _hbm_like(a):
    return pltpu.HBM(a.shape, a.dtype)


def _place():
    x, y, c = lax.axis_index("x"), lax.axis_index("y"), lax.axis_index("c")
    other_chips = [(1 - x, y), (x, 1 - y), (1 - x, 1 - y)]
    return x, y, c, other_chips


def _block(px, py, pc):
    return 4 * px + 2 * py + pc


def _await(block, sem):
    pltpu.make_async_copy(block, block, sem).wait()


def _relay_route(x, y, c):
    came_from = ((1 - x) * (1 - c) + x * c, y * (1 - c) + (1 - y) * c)
    goes_to = (x * (1 - c) + (1 - x) * c, (1 - y) * (1 - c) + y * c)
    return came_from, goes_to


def _gather_start(name, shards, groups, relayed=(), after=(), lands=None):
    n, ng = len(shards), len(groups)
    if lands is None:
        lands = [lax.empty((N_DEV, *a.shape), a.dtype) for a in shards]

    def body(*refs):
        src, land = refs[:n], refs[n:2 * n]
        sems, token = refs[2 * n + len(after):2 * n + len(after) + 2 * ng], refs[-1]
        x, y, c, chips = _place()
        targets = [(x, y, 1 - c)] + [(*chip, c) for chip in chips]
        for gi, group in enumerate(groups):
            for i, w in enumerate(group):
                for k, to in enumerate(targets[:3] if gi in relayed else targets):
                    pltpu.make_async_remote_copy(
                        src_ref=src[w], dst_ref=land[w].at[_block(x, y, c)],
                        send_sem=sems[2 * gi].at[4 * i + k], recv_sem=sems[2 * gi + 1].at[4 * i + k],
                        device_id=to, device_id_type=MESH).start()
        token[...] = jnp.zeros_like(token)

    sem_shapes = [pltpu.SemaphoreType.DMA((4 * len(g),)) for g in groups for _ in range(2)]
    out = pl.pallas_call(
        body, name=name,
        in_specs=[HBM] * (2 * n) + [ANY] * len(after),
        out_specs=[SEM] * (2 * ng) + [HBM] * (2 * n) + [IN_VMEM],
        out_shape=sem_shapes + [_hbm_like(a) for a in shards] + [_hbm_like(a) for a in lands]
        + [jax.ShapeDtypeStruct((SUBLANE, LANE), F32)],
        input_output_aliases={i: 2 * ng + i for i in range(2 * n)},
        compiler_params=SPLIT,
    )(*[_in_hbm(a) for a in shards], *[_in_hbm(a) for a in lands], *after)
    sems = [(out[2 * gi], out[2 * gi + 1]) for gi in range(ng)]
    return sems, out[2 * ng:2 * ng + n], out[2 * ng + n:2 * ng + 2 * n], out[-1]


def _gather_forward(name, shards, lands, send1, recv1, after, relayed=False):
    n = len(lands)

    def body(*refs):
        src, land = refs[:n], refs[n:2 * n]
        s1, r1 = refs[2 * n], refs[2 * n + 1]
        s2, r2 = refs[2 * n + 2 + len(after)], refs[2 * n + 3 + len(after)]
        x, y, c, chips = _place()
        me, sibling = (x, y, c), (x, y, 1 - c)
        for j, chip in enumerate(chips[:2] if relayed else chips):
            for i in range(n):
                blk = land[i].at[_block(*chip, c)]
                pltpu.make_async_remote_copy(src_ref=blk, dst_ref=blk, send_sem=s1.at[4 * i + 1 + j], recv_sem=r1.at[4 * i + 1 + j],
                                             device_id=me, device_id_type=MESH).wait_recv()
                pltpu.make_async_remote_copy(src_ref=blk, dst_ref=blk, send_sem=s2.at[3 * i + j], recv_sem=r2.at[3 * i + j],
                                             device_id=sibling, device_id_type=MESH).start()
        if relayed:
            came_from, goes_to = _relay_route(x, y, c)
            for i in range(n):
                blk = land[i].at[_block(*came_from, c)]
                pltpu.make_async_remote_copy(src_ref=blk, dst_ref=blk, send_sem=s2.at[3 * i + 2], recv_sem=r2.at[3 * i + 2],
                                             device_id=(*goes_to, c), device_id_type=MESH).start()
        for i in range(n):
            blk = land[i].at[_block(x, y, 1 - c)]
            pltpu.make_async_remote_copy(src_ref=blk, dst_ref=blk, send_sem=s1.at[4 * i], recv_sem=r1.at[4 * i],
                                         device_id=me, device_id_type=MESH).wait_recv()
            for k in range(3 if relayed else 4):
                pltpu.make_async_remote_copy(src_ref=src[i], dst_ref=land[i].at[_block(x, y, c)], send_sem=s1.at[4 * i + k],
                                             recv_sem=r1.at[4 * i + k], device_id=sibling, device_id_type=MESH).wait_send()

    sem = pltpu.SemaphoreType.DMA((3 * n,))
    out = pl.pallas_call(
        body, name=name,
        in_specs=[HBM] * (2 * n) + [SEM, SEM] + [ANY] * len(after),
        out_specs=[SEM, SEM] + [HBM] * n,
        out_shape=[sem, sem] + [_hbm_like(a) for a in lands],
        input_output_aliases={n + i: 2 + i for i in range(n)},
        compiler_params=SPLIT,
    )(*shards, *lands, send1, recv1, *after)
    return (out[0], out[1]), out[2:]


def _gather_relay_forward(name, lands, send2, recv2, after):
    n = len(lands)

    def body(*refs):
        land, s2, r2 = refs[:n], refs[n], refs[n + 1]
        s3, r3 = refs[n + 2 + len(after)], refs[n + 3 + len(after)]
        x, y, c, _ = _place()
        me, sibling = (x, y, c), (x, y, 1 - c)
        came_from, _ = _relay_route(x, y, c)
        for i in range(n):
            blk = land[i].at[_block(1 - x, 1 - y, c)]
            pltpu.make_async_remote_copy(src_ref=blk, dst_ref=blk, send_sem=s2.at[3 * i + 2], recv_sem=r2.at[3 * i + 2],
                                         device_id=me, device_id_type=MESH).wait_recv()
            pltpu.make_async_remote_copy(src_ref=blk, dst_ref=blk, send_sem=s3.at[i], recv_sem=r3.at[i],
                                         device_id=sibling, device_id_type=MESH).start()
            sent = land[i].at[_block(*came_from, c)]
            pltpu.make_async_remote_copy(src_ref=sent, dst_ref=sent, send_sem=s2.at[3 * i + 2], recv_sem=r2.at[3 * i + 2],
                                         device_id=me, device_id_type=MESH).wait_send()

    sem = pltpu.SemaphoreType.DMA((n,))
    out = pl.pallas_call(
        body, name=name,
        in_specs=[HBM] * n + [SEM, SEM] + [ANY] * len(after),
        out_specs=[SEM, SEM] + [HBM] * n,
        out_shape=[sem, sem] + [_hbm_like(a) for a in lands],
        input_output_aliases={i: 2 + i for i in range(n)},
        compiler_params=SPLIT,
    )(*lands, send2, recv2, *after)
    return (out[0], out[1]), out[2:]


def _gather_wait(name, lands, send2, recv2, after, relay_sems=None):
    n = len(lands)
    n_sems = 2 if relay_sems is None else 4

    def body(*refs):
        land, s2, r2 = refs[:n], refs[n], refs[n + 1]
        for i in range(n):
            for j in range(3 if relay_sems is None else 2):
                _await(land[i].at[0], r2.at[3 * i + j])
                _await(land[i].at[0], s2.at[3 * i + j])
            if relay_sems is not None:
                _await(land[i].at[0], refs[n + 3].at[i])
                _await(land[i].at[0], refs[n + 2].at[i])

    return pl.pallas_call(
        body, name=name,
        in_specs=[HBM] * n + [SEM] * n_sems + [ANY] * len(after), out_specs=[HBM] * n, out_shape=[_hbm_like(a) for a in lands],
        input_output_aliases={i: i for i in range(n)},
        compiler_params=SPLIT,
    )(*lands, send2, recv2, *(relay_sems or ()), *after)


def _pair_exchange(name, grads, shard_rows):
    n = len(grads)
    shapes = [(g.shape[1:] if r is None else (r, g.shape[1])) for g, r in zip(grads, shard_rows)]

    def body(*refs):
        ins, recv = refs[:n], refs[n:2 * n]
        send_sems, recv_sems = refs[2 * n:]
        x, y, c, _ = _place()
        sends = []
        for w in range(n):
            for k in range(N_CHIP):
                j, r = 2 * k + 1 - c, shard_rows[w]
                src = ins[w].at[j] if r is None else ins[w].at[pl.ds(pl.multiple_of(j * r, SUBLANE), r), :]
                sends.append(pltpu.make_async_remote_copy(
                    src_ref=src, dst_ref=recv[w].at[k],
                    send_sem=send_sems.at[w, k], recv_sem=recv_sems.at[w, k],
                    device_id=(x, y, 1 - c), device_id_type=MESH))
        for cp in sends:
            cp.start()
        for cp in sends:
            cp.wait()

    return pl.pallas_call(
        body, name=name,
        in_specs=[ANY] * n, out_specs=[ANY] * n,
        out_shape=[jax.ShapeDtypeStruct((N_CHIP, *shape), g.dtype) for g, shape in zip(grads, shapes)],
        scratch_shapes=[pltpu.SemaphoreType.DMA((n, N_CHIP))] * 2,
    )(*grads)


def _pair_sum_rows(name, grad, received, core):
    _, r, c = received.shape
    tc = _fit(c, 512)

    def body(core_ref, a_ref, b_ref, o_ref):
        o_ref[...] = (a_ref[...] + b_ref[...]).astype(o_ref.dtype)

    spec = pl.BlockSpec((None, r, tc), lambda k, i, core_ref: (k, 0, i))
    return pl.pallas_call(
        body, name=name,
        grid_spec=pltpu.PrefetchScalarGridSpec(
            num_scalar_prefetch=1, grid=(N_CHIP, c // tc),
            in_specs=[pl.BlockSpec((r, tc), lambda k, i, core_ref: (2 * k + core_ref[0], i)), spec],
            out_specs=spec),
        out_shape=jax.ShapeDtypeStruct(received.shape, BF16),
        compiler_params=_params("parallel", "parallel"),
    )(core, grad, received)


def _pair_sum(name, grad, received, core):
    _, r, c = received.shape
    rows = min(ROWS, r)
    assert r % rows == 0

    def body(core_ref, a_ref, b_ref, o_ref):
        o_ref[...] = (a_ref[...].astype(F32) + b_ref[...].astype(F32)).astype(o_ref.dtype)

    spec = pl.BlockSpec((None, rows, c), lambda k, i, core_ref: (k, i, 0))
    return pl.pallas_call(
        body, name=name,
        grid_spec=pltpu.PrefetchScalarGridSpec(
            num_scalar_prefetch=1, grid=(N_CHIP, r // rows),
            in_specs=[pl.BlockSpec((None, None, rows, c), lambda k, i, core_ref: (k, core_ref[0], i, 0)), spec],
            out_specs=spec),
        out_shape=jax.ShapeDtypeStruct(received.shape, received.dtype),
        compiler_params=_params("parallel", "parallel"),
    )(core, grad.reshape(N_CHIP, 2, r, c), received)


def _away_shard(src, k, c, shard_rows):
    if shard_rows is None:
        return src.at[k]
    return src.at[pl.ds(pl.multiple_of((2 * k + 1 - c) * shard_rows, SUBLANE), shard_rows), :]


def _pair_send_start(name, away, shard_rows=None):
    shape = away.shape if shard_rows is None else (N_CHIP, shard_rows, away.shape[1])
    land = lax.empty(shape, away.dtype)

    def body(src, dst, send, recv, src_thru, dst_thru, token):
        x, y, c, _ = _place()
        for k in range(N_CHIP):
            pltpu.make_async_remote_copy(src_ref=_away_shard(src, k, c, shard_rows), dst_ref=dst.at[k], send_sem=send.at[k],
                                         recv_sem=recv.at[k], device_id=(x, y, 1 - c), device_id_type=MESH).start()
        token[...] = jnp.zeros_like(token)

    sem = pltpu.SemaphoreType.DMA((N_CHIP,))
    out = pl.pallas_call(
        body, name=name,
        in_specs=[HBM, HBM], out_specs=[SEM, SEM, HBM, HBM, IN_VMEM],
        out_shape=[sem, sem, _hbm_like(away), _hbm_like(land), jax.ShapeDtypeStruct((SUBLANE, LANE), F32)],
        input_output_aliases={0: 2, 1: 3},
        compiler_params=SPLIT,
    )(_in_hbm(away), _in_hbm(land))
    return (out[0], out[1]), out[2], out[3], out[4]


def _pair_send_wait(name, sems, src, land, after, shard_rows=None):
    def body(src_ref, dst_ref, send, recv, *rest):
        for k in range(N_CHIP):
            _await(dst_ref.at[k], send.at[k])
            _await(dst_ref.at[k], recv.at[k])

    return pl.pallas_call(
        body, name=name,
        in_specs=[HBM, HBM, SEM, SEM] + [ANY] * len(after), out_specs=HBM, out_shape=_hbm_like(land),
        input_output_aliases={1: 0},
        compiler_params=SPLIT,
    )(src, land, *sems, *after)


def _chip_send_start(name, sums):
    n = len(sums)
    lands = [lax.empty(a.shape, a.dtype) for a in sums]

    def body(*refs):
        src, land = refs[:n], refs[n:2 * n]
        send, recv, token = refs[2 * n], refs[2 * n + 1], refs[-1]
        x, y, c, chips = _place()
        for w in range(n):
            for j, (px, py) in enumerate(chips):
                pltpu.make_async_remote_copy(
                    src_ref=src[w].at[2 * px + py], dst_ref=land[w].at[2 * x + y],
                    send_sem=send.at[3 * w + j], recv_sem=recv.at[3 * w + j],
                    device_id=(px, py, c), device_id_type=MESH).start()
        token[...] = jnp.zeros_like(token)

    sem = pltpu.SemaphoreType.DMA((3 * n,))
    out = pl.pallas_call(
        body, name=name,
        in_specs=[HBM] * (2 * n),
        out_specs=[SEM, SEM] + [HBM] * (2 * n) + [IN_VMEM],
        out_shape=[sem, sem] + [_hbm_like(a) for a in sums] + [_hbm_like(a) for a in lands]
        + [jax.ShapeDtypeStruct((SUBLANE, LANE), F32)],
        input_output_aliases={i: 2 + i for i in range(2 * n)},
        compiler_params=SPLIT,
    )(*[_in_hbm(a) for a in sums], *[_in_hbm(a) for a in lands])
    return (out[0], out[1]), out[2:2 + n], out[2 + n:2 + 2 * n], out[-1]


def _chip_send_wait(name, groups, after):
    counts = [len(g[1]) for g in groups]
    n = sum(counts)

    def body(*refs):
        land = refs[n:2 * n]
        sems = refs[2 * n:2 * n + 2 * len(groups)]
        w = 0
        for gi, count in enumerate(counts):
            for i in range(count):
                for j in range(3):
                    _await(land[w].at[0], sems[2 * gi].at[3 * i + j])
                    _await(land[w].at[0], sems[2 * gi + 1].at[3 * i + j])
                w += 1

    sums = [a for g in groups for a in g[1]]
    lands = [a for g in groups for a in g[2]]
    sems = [s for g in groups for s in g[0]]
    return pl.pallas_call(
        body, name=name,
        in_specs=[HBM] * (2 * n) + [SEM] * len(sems) + [ANY] * len(after),
        out_specs=[HBM] * n, out_shape=[_hbm_like(a) for a in lands],
        input_output_aliases={n + i: i for i in range(n)},
        compiler_params=SPLIT,
    )(*sums, *lands, *sems, *after)


def _small_all_reduce(part, after=()):
    _, w = part.shape

    def body(p_ref, *rest):
        o_ref, buf, send_sems, recv_sems = rest[len(after):]
        x, y, c, _ = _place()
        me = 4 * x + 2 * y + c
        buf[me] = jnp.sum(p_ref[...], axis=0, keepdims=True)
        copies = []
        for k in range(1, N_DEV):
            dx, dy, dc = (k >> 2) & 1, (k >> 1) & 1, k & 1
            copies.append(pltpu.make_async_remote_copy(
                src_ref=buf.at[me], dst_ref=buf.at[me], send_sem=send_sems.at[k - 1], recv_sem=recv_sems.at[k - 1],
                device_id=(x ^ dx, y ^ dy, c ^ dc), device_id_type=MESH))
        for cp in copies:
            cp.start()
        for cp in copies:
            cp.wait()
        tot = buf[0]
        for d in range(1, N_DEV):
            tot = tot + buf[d]
        o_ref[...] = tot
        loss = jnp.sum(tot[:, w - LANE:], axis=1, keepdims=True)
        o_ref[:, w - LANE:] = jnp.broadcast_to(loss, (1, LANE))

    return pl.pallas_call(
        body, name="small_all_reduce",
        in_specs=[IN_VMEM] + [ANY] * len(after), out_specs=IN_VMEM,
        out_shape=jax.ShapeDtypeStruct((1, w), F32),
        scratch_shapes=[pltpu.VMEM((N_DEV, 1, w), F32), pltpu.SemaphoreType.DMA((N_DEV - 1,)), pltpu.SemaphoreType.DMA((N_DEV - 1,))],
        compiler_params=pltpu.CompilerParams(vmem_limit_bytes=VMEM_LIMIT_BYTES),
    )(part, *after)


def _adamw(w, g, m, v):
    m = ADAM_B1 * m + (1.0 - ADAM_B1) * g
    v = ADAM_B2 * v + (1.0 - ADAM_B2) * (g * g)
    m_hat = m / (1.0 - ADAM_B1 ** ADAM_STEP)
    v_hat = v / (1.0 - ADAM_B2 ** ADAM_STEP)
    delta = -ADAM_LR * (m_hat / (jnp.sqrt(v_hat) + ADAM_EPS) + ADAM_WD * w)
    return delta, m, v


def _sum_adam_block(chip_ref, p_ref, own_ref, w_ref, m_ref, v_ref, g_ref, d_ref, mo_ref, vo_ref):
    g = None
    for k in range(N_CHIP):
        term = jnp.where(chip_ref[0] == k, own_ref[...], p_ref[k]).astype(F32)
        g = term if g is None else g + term
    g_ref[...] = g
    d_ref[...], mo_ref[...], vo_ref[...] = _adamw(w_ref[...], g, m_ref[...], v_ref[...])


def _sum_adam(name, parts, sums, chip, w, m, v, after=()):
    _, r, c = w.shape
    n_after = len(after)
    by_rows = r % ROWS == 0 or r < ROWS
    tr, tc = (min(ROWS, r), c) if by_rows else (r, _fit(c, 512))
    at = (lambda i: (i, 0)) if by_rows else (lambda i: (0, i))

    def body(chip_ref, p_ref, own_ref, w_ref, m_ref, v_ref, *rest):
        _sum_adam_block(chip_ref, p_ref, own_ref, w_ref, m_ref, v_ref, *rest[n_after:])

    blk = pl.BlockSpec((None, tr, tc), lambda i, chip_ref: (0, *at(i)))
    out = jax.ShapeDtypeStruct((1, r, c), F32)
    return pl.pallas_call(
        body, name=name,
        grid_spec=pltpu.PrefetchScalarGridSpec(
            num_scalar_prefetch=1, grid=(r // tr if by_rows else c // tc,),
            in_specs=[pl.BlockSpec((N_CHIP, tr, tc), lambda i, chip_ref: (0, *at(i))),
                      pl.BlockSpec((None, tr, tc), lambda i, chip_ref: (chip_ref[0], *at(i))), blk, blk, blk]
            + [ANY] * n_after,
            out_specs=[blk] * 4),
        out_shape=[out] * 4,
        compiler_params=_params("parallel"),
    )(chip, parts, sums, w, m, v, *after)


def _adam_gains(total, ws, ms, vs):
    n = len(ws)
    widths = [w.shape[1] for w in ws]

    def body(t_ref, *refs):
        w_refs, m_refs, v_refs, outs = refs[:n], refs[n:2 * n], refs[2 * n:3 * n], refs[3 * n:]
        off = 0
        for i in range(n):
            g = t_ref[:, off:off + widths[i]]
            off += widths[i]
            g_ref, d_ref, mo_ref, vo_ref = outs[4 * i:4 * i + 4]
            g_ref[...] = g
            d_ref[...], mo_ref[...], vo_ref[...] = _adamw(w_refs[i][...], g, m_refs[i][...], v_refs[i][...])

    out = pl.pallas_call(
        body, name="adam_gains",
        out_shape=[jax.ShapeDtypeStruct(w.shape, F32) for w in ws for _ in range(4)],
    )(total, *ws, *ms, *vs)
    return [tuple(out[4 * i:4 * i + 4]) for i in range(n)]


def _adam_taps(total, first_col, device, w, m, v):
    _, n_taps, cw = w.shape
    col_block = lambda t, dev: (0, first_col // cw + t * N_DEV + dev[0])
    tap = pl.BlockSpec((None, 1, cw), lambda t, dev: (t, 0, 0))

    def body(dev_ref, t_ref, w_ref, m_ref, v_ref, g_ref, d_ref, mo_ref, vo_ref):
        g = t_ref[...]
        g_ref[...] = g
        d_ref[...], mo_ref[...], vo_ref[...] = _adamw(w_ref[...], g, m_ref[...], v_ref[...])

    shape3 = (n_taps, 1, cw)
    out = pl.pallas_call(
        body, name="adam_taps",
        grid_spec=pltpu.PrefetchScalarGridSpec(
            num_scalar_prefetch=1, grid=(n_taps,),
            in_specs=[pl.BlockSpec((1, cw), col_block), tap, tap, tap], out_specs=[tap] * 4),
        out_shape=[jax.ShapeDtypeStruct(shape3, F32)] * 4,
    )(device, total, w.reshape(shape3), m.reshape(shape3), v.reshape(shape3))
    return tuple(o.reshape(w.shape) for o in out)


def kernel(x, pre_mix_g, w_in, conv_w, q_norm_g, w_uq, kv_norm_g, w_ukv, conv_out_g, attn_out_g, w_o, post_mix_g, pre_mlp_g, w_up, w_down, post_mlp_g, loss_target, m_pre_mix_g, m_w_in, m_conv_w, m_q_norm_g, m_w_uq, m_kv_norm_g, m_w_ukv, m_conv_out_g, m_attn_out_g, m_w_o, m_post_mix_g, m_pre_mlp_g, m_w_up, m_w_down, m_post_mlp_g, v_pre_mix_g, v_w_in, v_conv_w, v_q_norm_g, v_w_uq, v_kv_norm_g, v_w_ukv, v_conv_out_g, v_attn_out_g, v_w_o, v_post_mix_g, v_pre_mlp_g, v_w_up, v_w_down, v_post_mlp_g):
    me = 4 * lax.axis_index("x") + 2 * lax.axis_index("y") + lax.axis_index("c")
    core = lax.axis_index("c").astype(jnp.int32).reshape(1)
    chip = (2 * lax.axis_index("x") + lax.axis_index("y")).astype(jnp.int32).reshape(1)
    gains = (pre_mix_g, q_norm_g, kv_norm_g, conv_out_g, attn_out_g, post_mix_g, pre_mlp_g, post_mlp_g)
    gain_m = (m_pre_mix_g, m_q_norm_g, m_kv_norm_g, m_conv_out_g, m_attn_out_g, m_post_mix_g, m_pre_mlp_g, m_post_mlp_g)
    gain_v = (v_pre_mix_g, v_q_norm_g, v_kv_norm_g, v_conv_out_g, v_attn_out_g, v_post_mix_g, v_pre_mlp_g, v_post_mlp_g)
    names = ("w_in", "w_uq", "w_ukv", "w_o", "w_up", "w_down")
    big = dict(zip(names, (w_in, w_uq, w_ukv, w_o, w_up, w_down)))
    big_m = dict(zip(names, (m_w_in, m_w_uq, m_w_ukv, m_w_o, m_w_up, m_w_down)))
    big_v = dict(zip(names, (v_w_in, v_w_uq, v_w_ukv, v_w_o, v_w_up, v_w_down)))
    n_heads = attn_out_g.shape[1] // HEAD
    n_taps = conv_w.shape[1]

    gathered = ("w_in", "conv", "w_uq", "w_ukv", "w_o", "w_up", "w_down")
    gather_groups = ((0, 1), (2, 3, 4), (5,), (6,))
    taps = jnp.pad(conv_w[0], ((0, SUBLANE - n_taps), (0, 0)))
    relayed_groups = (0, 2, 3)
    sems1, shards, lands, token = _gather_start("gather_start_first", [w_in[0].astype(BF16), taps], ((0, 1),), relayed=(0,))
    sems1, shards, lands = list(sems1), list(shards), list(lands)
    behind = token[0, 0]
    rest = list(lax.optimization_barrier(tuple((big[nm][0] + behind).astype(BF16) for nm in gathered[2:])))

    rest_lands = [lax.dynamic_update_index_in_dim(lax.empty((N_DEV, *a.shape), a.dtype), a, me, 0) for a in rest]

    def start_more(name, some, groups, relayed, after):
        sems_b, shards_b, lands_b, started = _gather_start(name, rest[some], groups, relayed=relayed, after=after,
                                                           lands=rest_lands[some])
        sems1.extend(sems_b)
        shards.extend(shards_b)
        lands.extend(lands_b)
        return started

    start_rest = lambda after: start_more("gather_start_rest", slice(0, 4), ((0, 1, 2), (3,)), (1,), after)
    start_last = lambda after: start_more("gather_start_last", slice(4, 5), ((0,),), (0,), after)

    cols = lambda a: jnp.concatenate([a[j] for j in range(N_DEV)], axis=1)
    rows = lambda a: a.reshape(N_DEV * a.shape[1], a.shape[2])
    device = me.astype(jnp.int32).reshape(1)
    own_in = lambda a, shard: lax.dynamic_update_index_in_dim(a, shard, me, 0)
    q_pieces = [(h, 0, HEAD) for h in range(n_heads)] + [(h, HEAD, QK) for h in range(n_heads)]
    ready = {
        "w_in": lambda a, shard: _join_col_shards("join_w_in", a, shard, device),
        "conv": lambda a, shard: cols(own_in(a, shard))[:n_taps],
        "w_uq": lambda a, shard: _join_col_shards("join_w_uq", a, shard, device, q_pieces),
        "w_ukv": lambda a, shard: cols(a),
        "w_o": lambda a, shard: rows(a),
        "w_up": lambda a, shard: a,
        "w_down": lambda a, shard: rows(a),
    }
    assert w_uq.shape[2] == QK

    class Weights:
        def __init__(self):
            self.passed, self.relayed = {}, {}

        def forward(self, group, after):
            idx = gather_groups[group]
            if group == 0:
                after = (*after, *rest_lands)
            self.passed[group] = _gather_forward(f"gather_forward_{group}", [shards[i] for i in idx], [lands[i] for i in idx],
                                                 *sems1[group], after, relayed=group in relayed_groups)
            return tuple(self.passed[group][1])

        def start(self, group, after):
            assert group == len(gather_groups) - 1
            return (start_last(after),)

        def relay(self, group, after):
            sems2, mid = self.passed[group]
            self.relayed[group], mid = _gather_relay_forward(f"gather_relay_{group}", mid, *sems2, after)
            self.passed[group] = (sems2, mid)
            if group == 0:
                start_rest(tuple(mid))
            return tuple(mid)

        def ready(self, group, after):
            sems2, mid = self.passed[group]
            full = _gather_wait(f"gather_wait_{group}", mid, *sems2, after, relay_sems=self.relayed.get(group))
            out = []
            return [ready[gathered[i]](a, shards[i]) for i, a in zip(gather_groups[group], full)]

    weights = Weights()

    col_blocks = lambda g: g.reshape(g.shape[0], N_DEV, g.shape[1] // N_DEV).transpose(1, 0, 2)
    row_blocks = lambda g: g.reshape(N_DEV, g.shape[0] // N_DEV, g.shape[1])
    grad_groups = (("w_down",), ("w_up",), ("w_o", "w_uq", "w_ukv"), ("w_in",))
    transposed = {"w_in": w_in.shape[2], "w_uq": w_uq.shape[2]}
    to_blocks = {
        "w_in": lambda g: g, "w_uq": lambda g: _unpermute_q_rows(g, n_heads),
        "w_ukv": col_blocks, "w_o": row_blocks, "w_up": lambda g: g, "w_down": row_blocks,
    }
    in_flight = []

    class Grads:
        def __init__(self):
            self.core = core
            self.away = {}

        def send_sums(self, group, sums):
            sems, sums, parts, tok = _chip_send_start(f"chip_send_start_{group}", list(sums))
            in_flight.append((sems, sums, parts))
            return (tok,)

        def full(self, group, arrays, received=None):
            nms = grad_groups[group]
            if received is None:
                blocks = [to_blocks[nm](g) for nm, g in zip(nms, arrays)]
                got = _pair_exchange(f"pair_exchange_{group}", blocks, [transposed.get(nm) for nm in nms])
            else:
                blocks, got = [self.away[group][1]], [self.received(group, received)]
            sums = [(_pair_sum_rows if nm in transposed else _pair_sum)(f"pair_sum_{nm}", g, r, core)
                    for nm, g, r in zip(nms, blocks, got)]
            return self.send_sums(group, sums)

        def send_away(self, group, half):
            nm = grad_groups[group][0]
            rows = transposed.get(nm)
            sems, src, land, tok = _pair_send_start(f"pair_send_start_{group}", half if rows is None else to_blocks[nm](half), rows)
            self.away[group] = (sems, src, land, rows)
            return (tok,)

        def received(self, group, after):
            sems, src, land, rows = self.away[group]
            return _pair_send_wait(f"pair_send_wait_{group}", sems, src, land, after, rows)

        def update_now(self, group, after):
            return update(str(group), group, group + 1, after)

    big_out = {}

    def update(tag, first, last, after):
        picked = [i for i in range(first, last) if grad_groups[i][0] not in big_out]
        groups = [in_flight[i] for i in picked]
        parts = _chip_send_wait("chip_send_wait_" + tag, groups, after)
        nms = [nm for i in picked for nm in grad_groups[i]]
        sums = [a for _, s, _ in groups for a in s]
        for nm, p, s in zip(nms, parts, sums):
            view = (lambda a: jnp.swapaxes(a, 1, 2)) if nm in transposed else (lambda a: a)
            out = _sum_adam("adam_" + nm, p, s, chip, view(big[nm]), view(big_m[nm]), view(big_v[nm]), after=after)
            after = (out[0],)
            big_out[nm] = [view(o) for o in out]
        return after

    grad_x, small = _local_step(x[0], loss_target[0], gains, weights, Grads(), first_after=(token,))

    after = update("early", 0, len(in_flight) - 1, (grad_x,))
    total = _small_all_reduce(small, after=after)
    update("late", len(in_flight) - 1, len(in_flight), (total,))
    big_out = [big_out[nm] for nm in names]

    gain_out = _adam_gains(total, gains, gain_m, gain_v)
    taps_out = _adam_taps(total, sum(g.shape[1] for g in gains), me.astype(jnp.int32).reshape(1), conv_w, m_conv_w, v_conv_w)
    loss = total[0, total.shape[1] - 1]

    order = (0, "w_in", "conv", 1, "w_uq", 2, "w_ukv", 3, 4, "w_o", 5, 6, "w_up", "w_down", 7)
    by_name = dict(zip(names, big_out))
    outs = [loss, grad_x[None]]
    for kind in range(4):
        for item in order:
            if item == "conv":
                outs.append(taps_out[kind])
            elif isinstance(item, int):
                outs.append(gain_out[item][kind])
            else:
                outs.append(by_name[item][kind])
    return tuple(outs)
```

```python
import math

import jax
import jax.numpy as jnp
from jax import lax
from jax.experimental import pallas as pl
from jax.experimental.pallas import tpu as pltpu

F32 = jnp.float32
BF16 = jnp.bfloat16

EPS = 1e-6
NEG_INF = -1e30
HEAD = 128
ROPE = 64
QK = HEAD + ROPE
CHUNK = 64
ROPE_THETA = 10000.0
ADAM_LR, ADAM_B1, ADAM_B2, ADAM_EPS, ADAM_WD, ADAM_STEP = 0.001, 0.9, 0.999, 1e-08, 0.01, 10

LANE = 128
SUBLANE = 8
VMEM_LIMIT_BYTES = 56 * 1024 * 1024

N_DEV = 8
N_CHIP = 4
MESH = pl.DeviceIdType.MESH


def _params(*sem):
    return pltpu.CompilerParams(dimension_semantics=sem, vmem_limit_bytes=VMEM_LIMIT_BYTES)


ANY = pl.BlockSpec(memory_space=pl.ANY)


def _call(body, *, in_specs, after=(), **kw):
    n_in, n_after = len(in_specs), len(after)

    def ordered(*refs):
        body(*refs[:n_in], *refs[n_in + n_after:])

    call = pl.pallas_call(ordered, in_specs=[*in_specs, *[ANY] * n_after], **kw)
    return lambda *operands: call(*operands, *after)


def _sublane_sum(v):
    r, w = v.shape
    return jnp.sum(v.reshape(r // SUBLANE, SUBLANE, w), axis=0)


def _rstd(x):
    return lax.rsqrt(jnp.mean(x * x, axis=-1, keepdims=True) + EPS)


def _rms_bwd(x, g, dy):
    r = _rstd(x)
    xh = x * r
    dxh = dy * g
    dx = r * (dxh - xh * jnp.mean(dxh * xh, axis=-1, keepdims=True))
    return dx, dy * xh


def _accumulate(ref, val, step):
    @pl.when(step == 0)
    def _():
        ref[...] = val

    @pl.when(step > 0)
    def _():
        ref[...] += val


NN = ((1,), (0,))
NT = ((1,), (1,))
TN = ((0,), (0,))


def _matmul(name, a, b, *, grid, a_spec, b_spec, out_shape, out_specs, contract, nk=1, acc_shape=None,
            extras=(), extra_specs=(), epilogue=None, after=()):
    multi = isinstance(out_shape, (tuple, list))
    out_shapes = tuple(out_shape) if multi else (out_shape,)
    n_out = len(out_shapes)
    n_extra = len(extras)

    def body(a_ref, b_ref, *rest):
        x_refs = rest[:n_extra]
        o_refs = rest[n_extra:n_extra + n_out]

        def emit(acc):
            vals = epilogue(acc, *[r[...] for r in x_refs]) if epilogue else (acc,)
            for r, v in zip(o_refs, vals):
                r[...] = v.astype(r.dtype)

        p = lax.dot_general(a_ref[...], b_ref[...], (contract, ((), ())), preferred_element_type=F32)
        if nk == 1:
            emit(p)
        else:
            acc_ref = rest[n_extra + n_out]
            k = pl.program_id(2)
            _accumulate(acc_ref, p, k)

            @pl.when(k == nk - 1)
            def _():
                emit(acc_ref[...])

    sem = ("parallel", "parallel") + (("arbitrary",) if nk > 1 else ())
    return _call(
        body, name=name, grid=grid, after=after,
        in_specs=[a_spec, b_spec, *extra_specs],
        out_specs=out_specs,
        out_shape=out_shape,
        scratch_shapes=[pltpu.VMEM(acc_shape, F32)] if nk > 1 else [],
        compiler_params=_params(*sem),
    )(a, b, *extras)


def _fit(n, tile):
    if n <= tile:
        return n
    t = tile - tile % LANE
    while n % t:
        t -= LANE
    return t


def _mm_nn(name, a, b, out_dtype, tm, tn, after=()):
    m, k = a.shape
    n = b.shape[1]
    tm, tn = _fit(m, tm), _fit(n, tn)
    return _matmul(name, a, b, grid=(m // tm, n // tn), after=after,
                   a_spec=pl.BlockSpec((tm, k), lambda i, j: (i, 0)),
                   b_spec=pl.BlockSpec((k, tn), lambda i, j: (0, j)),
                   out_shape=jax.ShapeDtypeStruct((m, n), out_dtype),
                   out_specs=pl.BlockSpec((tm, tn), lambda i, j: (i, j)), contract=NN)


def _mm_nt(name, a, b, out_dtype, tm, tn, after=()):
    m, k = a.shape
    n = b.shape[0]
    tm, tn = _fit(m, tm), _fit(n, tn)
    return _matmul(name, a, b, grid=(m // tm, n // tn), after=after,
                   a_spec=pl.BlockSpec((tm, k), lambda i, j: (i, 0)),
                   b_spec=pl.BlockSpec((tn, k), lambda i, j: (j, 0)),
                   out_shape=jax.ShapeDtypeStruct((m, n), out_dtype),
                   out_specs=pl.BlockSpec((tm, tn), lambda i, j: (i, j)), contract=NT)


def _mm_tn(name, a, b, out_dtype, tm, tn):
    s, m = a.shape
    n = b.shape[1]
    tm, tn = _fit(m, tm), _fit(n, tn)
    return _matmul(name, a, b, grid=(m // tm, n // tn),
                   a_spec=pl.BlockSpec((s, tm), lambda i, j: (0, i)),
                   b_spec=pl.BlockSpec((s, tn), lambda i, j: (0, j)),
                   out_shape=jax.ShapeDtypeStruct((m, n), out_dtype),
                   out_specs=pl.BlockSpec((tm, tn), lambda i, j: (i, j)), contract=TN)


ROWS = 256


def _row_spec(rows, width):
    return pl.BlockSpec((rows, width), lambda i: (i, 0))


def _fixed_spec(rows, width):
    return pl.BlockSpec((rows, width), lambda i: (0, 0))


def _column_pieces(rows, start, width):
    piece = math.gcd(start, width)
    assert piece % LANE == 0
    return [pl.BlockSpec((rows, piece), lambda i, b=start // piece + p: (i, b)) for p in range(width // piece)]


def _rms_fwd(name, x, g, after=()):
    s, w = x.shape
    rows = min(ROWS, s)

    def body(x_ref, g_ref, o_ref):
        xv = x_ref[...]
        o_ref[...] = (xv * _rstd(xv) * g_ref[...]).astype(o_ref.dtype)

    return _call(
        body, name=name, grid=(s // rows,), after=after,
        in_specs=[_row_spec(rows, w), _fixed_spec(1, w)],
        out_specs=_row_spec(rows, w),
        out_shape=jax.ShapeDtypeStruct((s, w), BF16),
        compiler_params=_params("parallel"),
    )(x, g)


def _norm_up(name, x, cols, g, w, after=()):
    s = x.shape[0]
    start, width = cols
    n = w.shape[1]
    tm = min(TILE_M, s)
    pieces = _column_pieces(tm, start, width)
    n_p = len(pieces)

    def body(*refs):
        g_ref, w_ref, xn_ref, o_ref = refs[n_p:]
        xv = refs[0][...] if n_p == 1 else jnp.concatenate([r[...] for r in refs[:n_p]], axis=1)
        xn = (xv * _rstd(xv) * g_ref[...]).astype(BF16)
        xn_ref[...] = xn
        o_ref[...] = jnp.dot(xn, w_ref[...], preferred_element_type=F32)

    return _call(
        body, name=name, grid=(s // tm,), after=after,
        in_specs=[*pieces, _fixed_spec(1, width), _fixed_spec(width, n)],
        out_specs=[_row_spec(tm, width), _row_spec(tm, n)],
        out_shape=[jax.ShapeDtypeStruct((s, width), BF16), jax.ShapeDtypeStruct((s, n), F32)],
        compiler_params=_params("parallel"),
    )(*[x] * n_p, g, w)


def _up_norm_bwd(name, dy, w, x, cols, g, into, tail=None, after=()):
    s, n = dy.shape
    start, width = cols
    tm = min(TILE_M, s)
    pieces = _column_pieces(tm, start, width)
    n_p = len(pieces)
    tails = () if tail is None else (tail,)
    out_width = width if tail is None else into.shape[1] - start
    assert start % out_width == 0 and into.dtype == BF16

    def body(dy_ref, w_ref, *refs):
        g_ref = refs[n_p]
        dx_ref, dg_ref = refs[-2:]
        xv = refs[0][...] if n_p == 1 else jnp.concatenate([r[...] for r in refs[:n_p]], axis=1)
        dxn = lax.dot_general(dy_ref[...], w_ref[...], (NT, ((), ())), preferred_element_type=F32)
        dx, dgc = _rms_bwd(xv, g_ref[...], dxn)
        dx_ref[:, :width] = dx.astype(dx_ref.dtype)
        if tails:
            t = refs[n_p + 1][...]
            dx_ref[:, width:width + t.shape[1]] = t
            dx_ref[:, width + t.shape[1]:] = jnp.zeros((tm, out_width - width - t.shape[1]), dx_ref.dtype)
        _accumulate(dg_ref, _sublane_sum(dgc), pl.program_id(0))

    n_in = 3 + n_p + len(tails)
    return _call(
        body, name=name, grid=(s // tm,), after=after,
        in_specs=[_row_spec(tm, n), _fixed_spec(width, n), *pieces, _fixed_spec(1, width)]
                 + [_row_spec(tm, t.shape[1]) for t in tails] + [ANY],
        out_specs=[pl.BlockSpec((tm, out_width), lambda i: (i, start // out_width)), _fixed_spec(SUBLANE, width)],
        out_shape=[jax.ShapeDtypeStruct(into.shape, into.dtype), jax.ShapeDtypeStruct((SUBLANE, width), F32)],
        input_output_aliases={n_in: 0},
        compiler_params=_params("arbitrary"),
    )(dy, w, *[x] * n_p, g, *tails, into)


def _mid_fwd(x, y, g_post, g_pre, after=()):
    s, w = x.shape
    rows = min(ROWS, s)

    def body(x_ref, y_ref, gp_ref, gq_ref, x2_ref, h2_ref):
        yv = y_ref[...]
        x2 = x_ref[...] + yv * _rstd(yv) * gp_ref[...]
        x2_ref[...] = x2
        h2_ref[...] = (x2 * _rstd(x2) * gq_ref[...]).astype(h2_ref.dtype)

    return _call(
        body, name="mid_fwd", grid=(s // rows,), after=after,
        in_specs=[_row_spec(rows, w), _row_spec(rows, w), _fixed_spec(1, w), _fixed_spec(1, w)],
        out_specs=[_row_spec(rows, w), _row_spec(rows, w)],
        out_shape=[jax.ShapeDtypeStruct((s, w), F32), jax.ShapeDtypeStruct((s, w), BF16)],
        compiler_params=_params("parallel"),
    )(x, y, g_post, g_pre)


def _head(m, x2, tgt, g):
    s, w = m.shape
    rows = min(ROWS, s)

    def body(m_ref, x2_ref, t_ref, g_ref, dout_ref, dm_ref, dg_ref, loss_ref):
        mv = m_ref[...]
        gv = g_ref[...]
        out = x2_ref[...] + mv * _rstd(mv) * gv
        err = out - t_ref[...]
        dout = err * (1.0 / w)
        dout_ref[...] = dout
        dm, dgc = _rms_bwd(mv, gv, dout)
        dm_ref[...] = dm.astype(dm_ref.dtype)
        sq = err * err
        lanes = sq[:, 0:LANE]
        for j in range(1, w // LANE):
            lanes = lanes + sq[:, j * LANE:(j + 1) * LANE]
        step = pl.program_id(0)
        _accumulate(dg_ref, _sublane_sum(dgc), step)
        _accumulate(loss_ref, _sublane_sum(lanes) * (0.5 / w), step)

    return pl.pallas_call(
        body, name="head", grid=(s // rows,),
        in_specs=[_row_spec(rows, w), _row_spec(rows, w), _row_spec(rows, w), _fixed_spec(1, w)],
        out_specs=[_row_spec(rows, w), _row_spec(rows, w), _fixed_spec(SUBLANE, w), _fixed_spec(SUBLANE, LANE)],
        out_shape=[jax.ShapeDtypeStruct((s, w), F32), jax.ShapeDtypeStruct((s, w), BF16),
                   jax.ShapeDtypeStruct((SUBLANE, w), F32), jax.ShapeDtypeStruct((SUBLANE, LANE), F32)],
        compiler_params=_params("arbitrary"),
    )(m, x2, tgt, g)


def _mid_bwd(x2, y, d_out, d_h2, g_pre, g_post, after=()):
    s, w = x2.shape
    rows = min(ROWS, s)

    def body(x2_ref, y_ref, dout_ref, dh2_ref, gq_ref, gp_ref, dx2_ref, dy_ref, dgq_ref, dgp_ref):
        dx, dgq = _rms_bwd(x2_ref[...], gq_ref[...], dh2_ref[...])
        dx2 = dout_ref[...] + dx
        dx2_ref[...] = dx2
        dy, dgp = _rms_bwd(y_ref[...], gp_ref[...], dx2)
        dy_ref[...] = dy.astype(dy_ref.dtype)
        step = pl.program_id(0)
        _accumulate(dgq_ref, _sublane_sum(dgq), step)
        _accumulate(dgp_ref, _sublane_sum(dgp), step)

    return _call(
        body, name="mid_bwd", grid=(s // rows,), after=after,
        in_specs=[_row_spec(rows, w)] * 4 + [_fixed_spec(1, w)] * 2,
        out_specs=[_row_spec(rows, w), _row_spec(rows, w), _fixed_spec(SUBLANE, w), _fixed_spec(SUBLANE, w)],
        out_shape=[jax.ShapeDtypeStruct((s, w), F32), jax.ShapeDtypeStruct((s, w), BF16),
                   jax.ShapeDtypeStruct((SUBLANE, w), F32), jax.ShapeDtypeStruct((SUBLANE, w), F32)],
        compiler_params=_params("arbitrary"),
    )(x2, y, d_out, d_h2, g_pre, g_post)


def _first_bwd(x, g, d_h1, d_x2, after=()):
    s, w = x.shape
    rows = min(ROWS, s)

    def body(x_ref, g_ref, dh_ref, dx2_ref, dx_ref, dg_ref):
        dx, dgc = _rms_bwd(x_ref[...], g_ref[...], dh_ref[...])
        dx_ref[...] = dx2_ref[...] + dx
        _accumulate(dg_ref, _sublane_sum(dgc), pl.program_id(0))

    return _call(
        body, name="first_bwd", grid=(s // rows,), after=after,
        in_specs=[_row_spec(rows, w), _fixed_spec(1, w), _row_spec(rows, w), _row_spec(rows, w)],
        out_specs=[_row_spec(rows, w), _fixed_spec(SUBLANE, w)],
        out_shape=[jax.ShapeDtypeStruct((s, w), F32), jax.ShapeDtypeStruct((SUBLANE, w), F32)],
        compiler_params=_params("arbitrary"),
    )(x, g, d_h1, d_x2)


def _shift_down(v, k):
    t = lax.broadcasted_iota(jnp.int32, v.shape, 0)
    return jnp.where(t >= k, pltpu.roll(v, k, 0), 0.0)


def _shift_up(v, k):
    n = v.shape[0]
    t = lax.broadcasted_iota(jnp.int32, v.shape, 0)
    return jnp.where(t < n - k, pltpu.roll(v, n - k, 0), 0.0)


def _conv_core(u, b, c, w):
    z = c * u
    conv = w[0:1, :] * _shift_down(z, 2) + w[1:2, :] * _shift_down(z, 1) + w[2:3, :] * z
    return z, conv, b * conv


def _conv_fwd(proj, conv_w, g, n_groups, out_width, after=()):
    s = proj.shape[0]

    def body(u_ref, b_ref, c_ref, w_ref, g_ref, o_ref):
        _, _, yr = _conv_core(u_ref[...], b_ref[...], c_ref[...], w_ref[...])
        o_ref[...] = (yr * _rstd(yr) * g_ref[...]).astype(o_ref.dtype)

    col = lambda k: pl.BlockSpec((s, HEAD), lambda i: (0, k * n_groups + i))
    return _call(
        body, name="conv_fwd", grid=(n_groups,), after=after,
        in_specs=[col(0), col(1), col(2), pl.BlockSpec((3, HEAD), lambda i: (0, i)), pl.BlockSpec((1, HEAD), lambda i: (0, i))],
        out_specs=pl.BlockSpec((s, HEAD), lambda i: (0, i)),
        out_shape=jax.ShapeDtypeStruct((s, out_width), BF16),
        compiler_params=_params("parallel"),
    )(proj, proj, proj, conv_w, g)


def _conv_bwd(proj, d_mix, conv_w, g, n_groups, out_width):
    s = proj.shape[0]
    width = n_groups * HEAD

    def body(u_ref, b_ref, c_ref, dy_ref, w_ref, g_ref, dproj_ref, dg_ref, dw_ref, buf, sems):
        i = pl.program_id(0)
        slot = i % 2

        def copies(group, slot):
            return [pltpu.make_async_copy(buf.at[slot, k], dproj_ref.at[:, pl.ds(pl.multiple_of((k * n_groups + group) * HEAD, HEAD), HEAD)],
                                          sems.at[slot, k]) for k in range(3)]

        @pl.when(i >= 2)
        def _():
            for cp in copies(i - 2, slot):
                cp.wait()

        u, b, c, w = u_ref[...], b_ref[...], c_ref[...], w_ref[...]
        z, conv, yr = _conv_core(u, b, c, w)
        dyr, dgc = _rms_bwd(yr, g_ref[...], dy_ref[...])
        dconv = dyr * b
        dz = w[2:3, :] * dconv + w[1:2, :] * _shift_up(dconv, 1) + w[0:1, :] * _shift_up(dconv, 2)
        buf[slot, 0] = (dz * c).astype(buf.dtype)
        buf[slot, 1] = (dyr * conv).astype(buf.dtype)
        buf[slot, 2] = (dz * u).astype(buf.dtype)
        for cp in copies(i, slot):
            cp.start()
        dg_ref[...] = _sublane_sum(dgc)
        dw_ref[0] = _sublane_sum(dconv * _shift_down(z, 2))
        dw_ref[1] = _sublane_sum(dconv * _shift_down(z, 1))
        dw_ref[2] = _sublane_sum(dconv * z)

        @pl.when(i == n_groups - 1)
        def _():
            for back in range(min(2, n_groups)):
                for cp in copies(i - back, (n_groups - 1 - back) % 2):
                    cp.wait()

    col = lambda k: pl.BlockSpec((s, HEAD), lambda i: (0, k * n_groups + i))
    grp = pl.BlockSpec((s, HEAD), lambda i: (0, i))
    return pl.pallas_call(
        body, name="conv_bwd", grid=(n_groups,),
        in_specs=[col(0), col(1), col(2), grp, pl.BlockSpec((3, HEAD), lambda i: (0, i)), pl.BlockSpec((1, HEAD), lambda i: (0, i))],
        out_specs=[ANY, pl.BlockSpec((SUBLANE, HEAD), lambda i: (0, i)), pl.BlockSpec((3, SUBLANE, HEAD), lambda i: (0, 0, i))],
        out_shape=[jax.ShapeDtypeStruct((s, out_width), BF16),
                   jax.ShapeDtypeStruct((SUBLANE, width), F32), jax.ShapeDtypeStruct((3, SUBLANE, width), F32)],
        scratch_shapes=[pltpu.VMEM((2, 3, s, HEAD), BF16), pltpu.SemaphoreType.DMA((2, 3))],
        compiler_params=_params("arbitrary"),
    )(proj, proj, proj, d_mix, conv_w, g)


def _rope_tables(s, n_heads):
    pos = jnp.arange(s, dtype=F32)
    inv_freq = jnp.power(ROPE_THETA, -jnp.arange(0, ROPE, 2, dtype=F32) / ROPE)
    ang = pos[:, None] * inv_freq[None, :]
    cos, sin = jnp.cos(ang), jnp.sin(ang)
    cs = jnp.concatenate([cos, cos], axis=1)
    sn = jnp.concatenate([-sin, sin], axis=1)
    pad = jnp.zeros((s, LANE - ROPE), F32)
    return (jnp.tile(cs, (1, n_heads)), jnp.tile(sn, (1, n_heads)),
            jnp.concatenate([cs, pad], axis=1), jnp.concatenate([sn, pad], axis=1))


def _swap_halves(v):
    w = v.shape[1]
    lane = lax.broadcasted_iota(jnp.int32, v.shape, 1)
    first = (lane % ROPE) < (ROPE // 2)
    return jnp.where(first, pltpu.roll(v, w - ROPE // 2, 1), pltpu.roll(v, ROPE // 2, 1))


def _pack_heads(q, kv, proj, kr_col, tables, n_heads, after=()):
    s = q.shape[0]
    rows = min(ROWS, s)
    cq, sq, ck, sk = tables
    wq = n_heads * ROPE

    def body(q_ref, kv_ref, kr_ref, cq_ref, sq_ref, ck_ref, sk_ref, qo_ref, ko_ref, vo_ref):
        qr = q_ref[:, n_heads * HEAD:]
        qr = qr * cq_ref[...] + _swap_halves(qr) * sq_ref[...]
        krv = kr_ref[...]
        krv = krv * ck_ref[...] + _swap_halves(krv) * sk_ref[...]
        for h in range(n_heads):
            qo_ref[h] = jnp.concatenate([q_ref[:, h * HEAD:(h + 1) * HEAD], qr[:, h * ROPE:(h + 1) * ROPE]], axis=1).astype(BF16)
            ko_ref[h] = jnp.concatenate([kv_ref[:, 2 * h * HEAD:(2 * h + 1) * HEAD], krv[:, :ROPE]], axis=1).astype(BF16)
            vo_ref[h] = kv_ref[:, (2 * h + 1) * HEAD:(2 * h + 2) * HEAD].astype(BF16)

    hs = lambda w: pl.BlockSpec((n_heads, rows, w), lambda i: (0, i, 0))
    return _call(
        body, name="pack_heads", grid=(s // rows,), after=after,
        in_specs=[_row_spec(rows, q.shape[1]), _row_spec(rows, kv.shape[1]), pl.BlockSpec((rows, LANE), lambda i: (i, kr_col // LANE)),
                  _row_spec(rows, wq), _row_spec(rows, wq), _row_spec(rows, LANE), _row_spec(rows, LANE)],
        out_specs=[hs(QK), hs(QK), hs(HEAD)],
        out_shape=[jax.ShapeDtypeStruct((n_heads, s, QK), BF16), jax.ShapeDtypeStruct((n_heads, s, QK), BF16),
                   jax.ShapeDtypeStruct((n_heads, s, HEAD), BF16)],
        compiler_params=_params("parallel"),
    )(q, kv, proj, cq, sq, ck, sk)


def _unpack_heads(dq, dk, dv, tables, n_heads):
    s = dq.shape[1]
    rows = min(ROWS, s)
    cq, sq, ck, sk = tables
    wq = n_heads * ROPE

    def body(dq_ref, dk_ref, dv_ref, cq_ref, sq_ref, ck_ref, sk_ref, qo_ref, kvo_ref, kro_ref):
        dqr = jnp.concatenate([dq_ref[h][:, HEAD:] for h in range(n_heads)], axis=1)
        dqr = dqr * cq_ref[...] - _swap_halves(dqr) * sq_ref[...]
        dkr = dk_ref[0][:, HEAD:]
        for h in range(1, n_heads):
            dkr = dkr + dk_ref[h][:, HEAD:]
        dkr = jnp.concatenate([dkr, jnp.zeros((rows, LANE - ROPE), F32)], axis=1)
        dkr = dkr * ck_ref[...] - _swap_halves(dkr) * sk_ref[...]
        kro_ref[...] = dkr.astype(kro_ref.dtype)
        qo_ref[:, n_heads * HEAD:] = dqr.astype(qo_ref.dtype)
        for h in range(n_heads):
            qo_ref[:, h * HEAD:(h + 1) * HEAD] = dq_ref[h][:, :HEAD].astype(qo_ref.dtype)
            kvo_ref[:, 2 * h * HEAD:(2 * h + 1) * HEAD] = dk_ref[h][:, :HEAD].astype(kvo_ref.dtype)
            kvo_ref[:, (2 * h + 1) * HEAD:(2 * h + 2) * HEAD] = dv_ref[h].astype(kvo_ref.dtype)

    hs = lambda w: pl.BlockSpec((n_heads, rows, w), lambda i: (0, i, 0))
    return pl.pallas_call(
        body, name="unpack_heads", grid=(s // rows,),
        in_specs=[hs(QK), hs(QK), hs(HEAD), _row_spec(rows, wq), _row_spec(rows, wq), _row_spec(rows, LANE), _row_spec(rows, LANE)],
        out_specs=[_row_spec(rows, n_heads * QK), _row_spec(rows, 2 * n_heads * HEAD), _row_spec(rows, LANE)],
        out_shape=[jax.ShapeDtypeStruct((s, n_heads * QK), BF16), jax.ShapeDtypeStruct((s, 2 * n_heads * HEAD), BF16),
                   jax.ShapeDtypeStruct((s, LANE), BF16)],
        compiler_params=_params("parallel"),
    )(dq, dk, dv, cq, sq, ck, sk)


TQ = 256


LOG2_E = 1.4426950408889634


def _softmax_parts(q, k):
    tq, n_keys = q.shape[0], k.shape[0]
    sc = lax.dot_general(q, k, (NT, ((), ())), preferred_element_type=F32) * (QK ** -0.5 * LOG2_E)
    row = lax.broadcasted_iota(jnp.int32, (tq, tq), 0)
    col = lax.broadcasted_iota(jnp.int32, (tq, tq), 1)
    own = jnp.where(col // CHUNK <= row // CHUNK, sc[:, n_keys - tq:], NEG_INF)
    sc = own if n_keys == tq else jnp.concatenate([sc[:, :n_keys - tq], own], axis=1)
    e = jnp.exp2(sc - jnp.max(sc, axis=-1, keepdims=True))
    return e, 1.0 / jnp.sum(e, axis=-1, keepdims=True)


def _prob_columns(c, tq):
    return pl.ds(tq * (c * (c + 1) // 2), (c + 1) * tq)


def _attn_fwd(q, k, v, g, mix, col0, first, between):
    n_heads, s, _ = q.shape
    tq = min(TQ, s)
    assert tq % CHUNK == 0 and s % tq == 0
    n_blocks = s // tq
    p_cols = tq * (n_blocks * (n_blocks + 1) // 2)
    out_shape = [jax.ShapeDtypeStruct((n_heads, s, HEAD), F32), jax.ShapeDtypeStruct((n_heads, tq, p_cols), BF16),
                 jax.ShapeDtypeStruct(mix.shape, mix.dtype)]
    done, after = (mix,), ()
    for part, (h0, h1) in enumerate(((0, first), (first, n_heads))):

        def body(q_ref, k_ref, v_ref, g_ref, *rest):
            o_ref, p_ref, y_ref = rest[-3:]
            for c in range(n_blocks):
                rows, n_keys = pl.ds(c * tq, tq), (c + 1) * tq
                e, inv = _softmax_parts(q_ref[rows, :], k_ref[0:n_keys, :])
                p = (e * inv).astype(BF16)
                p_ref[:, _prob_columns(c, tq)] = p
                o = jnp.dot(p, v_ref[0:n_keys, :], preferred_element_type=F32)
                o_ref[rows, :] = o
                y_ref[rows, :] = (o * _rstd(o) * g_ref[...]).astype(y_ref.dtype)

        head = lambda w, h0=h0: pl.BlockSpec((None, s, w), lambda h: (h0 + h, 0, 0))
        n_done = len(done)
        done = _call(
            body, name=f"attn_fwd_{part}", grid=(h1 - h0,), after=after,
            in_specs=[head(QK), head(QK), head(HEAD), pl.BlockSpec((1, HEAD), lambda h, h0=h0: (0, h0 + h))] + [ANY] * n_done,
            out_specs=[head(HEAD), pl.BlockSpec((None, tq, p_cols), lambda h, h0=h0: (h0 + h, 0, 0)),
                       pl.BlockSpec((s, HEAD), lambda h, h0=h0: (0, col0 // HEAD + h0 + h))],
            out_shape=out_shape,
            input_output_aliases={4 + i: 3 - n_done + i for i in range(n_done)},
            compiler_params=_params("parallel"),
        )(q, k, v, g, *done)
        after = between(done) if part == 0 else ()
    return done


def _attn_bwd(q, k, v, o, probs, d_mix, g, col0, after=()):
    n_heads, s, _ = q.shape
    tq = probs.shape[1]

    def body(q_ref, k_ref, v_ref, o_ref, p_ref, dy_ref, g_ref, dq_ref, dk_ref, dv_ref, dg_ref):
        dg = None
        for c in reversed(range(s // tq)):
            rows, n_keys = pl.ds(c * tq, tq), (c + 1) * tq
            o = o_ref[rows, :]
            do, dgc = _rms_bwd(o, g_ref[...], dy_ref[rows, :])
            do = do.astype(BF16)
            dg = _sublane_sum(dgc) if dg is None else dg + _sublane_sum(dgc)
            p = p_ref[:, _prob_columns(c, tq)]
            dp = lax.dot_general(do, v_ref[0:n_keys, :], (NT, ((), ())), preferred_element_type=F32)
            ds = (p.astype(F32) * (dp - jnp.sum(do.astype(F32) * o, axis=-1, keepdims=True))).astype(BF16)
            dq_ref[rows, :] = jnp.dot(ds, k_ref[0:n_keys, :], preferred_element_type=F32) * (QK ** -0.5)
            dk = lax.dot_general(ds, q_ref[rows, :], (TN, ((), ())), preferred_element_type=F32)
            dv = lax.dot_general(p, do, (TN, ((), ())), preferred_element_type=F32)
            if n_keys == s:
                dk_ref[...] = dk
                dv_ref[...] = dv
            else:
                dk_ref[0:n_keys, :] += dk
                dv_ref[0:n_keys, :] += dv
        dk_ref[...] = dk_ref[...] * (QK ** -0.5)
        dg_ref[...] = dg

    c0 = col0 // HEAD
    head = lambda w: pl.BlockSpec((None, s, w), lambda h, *_: (h, 0, 0))
    in_specs = [head(QK), head(QK), head(HEAD), head(HEAD), pl.BlockSpec((None, tq, probs.shape[2]), lambda h, *_: (h, 0, 0)),
                pl.BlockSpec((s, HEAD), lambda h, *_: (0, c0 + h)), pl.BlockSpec((1, HEAD), lambda h, *_: (0, h))]
    out_specs = [head(QK), head(QK), head(HEAD), pl.BlockSpec((SUBLANE, HEAD), lambda h, *_: (0, h))]
    out_shape = [jax.ShapeDtypeStruct((n_heads, s, QK), F32), jax.ShapeDtypeStruct((n_heads, s, QK), F32),
                 jax.ShapeDtypeStruct((n_heads, s, HEAD), F32), jax.ShapeDtypeStruct((SUBLANE, n_heads * HEAD), F32)]
    return _call(body, name="attn_bwd", grid=(n_heads,), after=after, in_specs=in_specs, out_specs=out_specs,
                 out_shape=out_shape, compiler_params=_params("parallel"))(q, k, v, o, probs, d_mix, g)


TILE_M = 1024
TILE_N = 1024


def _up_fwd(h2, w_up, between):
    s, d = h2.shape
    nb, _, fb = w_up.shape
    tm = min(TILE_M // 2, s)
    n_tiles = s // tm
    done, after = (), ()
    for part, (t0, nt) in enumerate([(0, 1), (1, n_tiles - 1)][:min(n_tiles, 2)]):

        def body(h_ref, w_ref, *rest):
            a_ref, r_ref = rest[-2:]
            r = jnp.maximum(jnp.dot(h_ref[...], w_ref[...], preferred_element_type=F32), 0.0)
            a_ref[...] = (r * r).astype(a_ref.dtype)
            r_ref[...] = r.astype(r_ref.dtype)

        blk = pl.BlockSpec((tm, fb), lambda i, j, t0=t0: (t0 + i, j))
        done = _call(
            body, name=f"up_fwd_{part}", grid=(nt, nb), after=after,
            in_specs=[pl.BlockSpec((tm, d), lambda i, j, t0=t0: (t0 + i, 0)), pl.BlockSpec((None, d, fb), lambda i, j: (j, 0, 0))]
                     + [ANY] * len(done),
            out_specs=[blk, blk], out_shape=[jax.ShapeDtypeStruct((s, nb * fb), BF16)] * 2,
            input_output_aliases={2 + i: i for i in range(len(done))},
            compiler_params=_params("parallel", "parallel"),
        )(h2, w_up, *done)
        after = between(done) if part == 0 else ()
    return done


def _down_fwd(a, w_down):
    s, f = a.shape
    d = w_down.shape[1]
    tm, tn, tk = min(TILE_M,s), min(TILE_N,d), 2048
    nk = f // tk
    return _matmul("down_fwd", a, w_down, grid=(s // tm, d // tn, nk),
                   a_spec=pl.BlockSpec((tm, tk), lambda i, j, k: (i, k)),
                   b_spec=pl.BlockSpec((tk, tn), lambda i, j, k: (k, j)),
                   out_shape=jax.ShapeDtypeStruct((s, d), F32),
                   out_specs=pl.BlockSpec((tm, tn), lambda i, j, k: (i, j)),
                   contract=NN, nk=nk, acc_shape=(tm, tn))


def _down_bwd_act(d_m, w_down, r, after=()):
    s, d = d_m.shape
    f = w_down.shape[0]
    tm, tn = min(TILE_M,s), min(TILE_N,f)
    blk = pl.BlockSpec((tm, tn), lambda i, j: (i, j))
    return _matmul("down_bwd_act", d_m, w_down, grid=(s // tm, f // tn), after=after,
                   a_spec=pl.BlockSpec((tm, d), lambda i, j: (i, 0)),
                   b_spec=pl.BlockSpec((tn, d), lambda i, j: (j, 0)),
                   out_shape=jax.ShapeDtypeStruct((s, f), BF16), out_specs=blk, contract=NT,
                   extras=(r,), extra_specs=(blk,),
                   epilogue=lambda acc, rv: (acc * (2.0 * rv.astype(F32)),))


def _up_bwd_act(d_up, w_up, after=()):
    s, _ = d_up.shape
    nb, d, fb = w_up.shape
    tm, tn = min(TILE_M, s), min(TILE_N,d)
    pair = 2
    n_after = len(after)

    def body(a_ref, w_ref, *rest):
        o_ref, acc_ref = rest[n_after:]
        k = pl.program_id(2)
        p = None
        for t in range(pair):
            term = lax.dot_general(a_ref[:, t * fb:(t + 1) * fb], w_ref[t], (NT, ((), ())), preferred_element_type=F32)
            p = term if p is None else p + term
        _accumulate(acc_ref, p, k)

        @pl.when(k == nb // pair - 1)
        def _():
            o_ref[...] = acc_ref[...]

    return pl.pallas_call(
        body, name="up_bwd_act", grid=(s // tm, d // tn, nb // pair),
        in_specs=[pl.BlockSpec((tm, pair * fb), lambda i, j, k: (i, k)),
                  pl.BlockSpec((pair, tn, fb), lambda i, j, k: (k, j, 0))] + [ANY] * n_after,
        out_specs=pl.BlockSpec((tm, tn), lambda i, j, k: (i, j)),
        out_shape=jax.ShapeDtypeStruct((s, d), F32),
        scratch_shapes=[pltpu.VMEM((tm, tn), F32)],
        compiler_params=_params("parallel", "parallel", "arbitrary"),
    )(d_up, w_up, *after)


def _half_grad(name, a, b, core, home, received, after, *, grid, a_block, a_map, b_block, b_map, o_block, o_map, out_shape):
    n_after = len(after)
    pick = (lambda ref: ref[0]) if home else (lambda ref: 1 - ref[0])

    def body(core_ref, a_ref, b_ref, *rest):
        acc = lax.dot_general(a_ref[...], b_ref[...], (TN, ((), ())), preferred_element_type=F32)
        if received is not None:
            acc = acc + rest[0][...].astype(F32)
        rest[-1][...] = acc.astype(rest[-1].dtype)

    wrap = lambda fn: (lambda i, j, core_ref: fn(i, j, pick(core_ref)))
    o_spec = pl.BlockSpec(o_block, wrap(o_map))
    extra = [] if received is None else [o_spec]
    operands = [] if received is None else [received]
    return pl.pallas_call(
        body, name=name,
        grid_spec=pltpu.PrefetchScalarGridSpec(
            num_scalar_prefetch=1, grid=grid,
            in_specs=[pl.BlockSpec(a_block, wrap(a_map)), pl.BlockSpec(b_block, wrap(b_map))] + extra + [ANY] * n_after,
            out_specs=o_spec),
        out_shape=out_shape,
        compiler_params=_params("parallel", "parallel"),
    )(core, a, b, *operands, *after)


def _down_half_grad(name, a, d_m, core, home, received=None, after=()):
    s, f = a.shape
    d = d_m.shape[1]
    r = f // N_DEV
    tn = min(TILE_N, d)
    return _half_grad(name, a, d_m, core, home, received, after, grid=(N_CHIP, d // tn),
                      a_block=(s, r), a_map=lambda k, j, p: (0, 2 * k + p),
                      b_block=(s, tn), b_map=lambda k, j, p: (0, j),
                      o_block=(None, r, tn), o_map=lambda k, j, p: (k, 0, j),
                      out_shape=jax.ShapeDtypeStruct((N_CHIP, r, d), BF16))


def _up_half_grad(name, h2, d_up, core, home, received=None, after=()):
    s, d = h2.shape
    fb = d_up.shape[1] // N_DEV
    tm = min(TILE_M, d)
    return _half_grad(name, h2, d_up, core, home, received, after, grid=(d // tm, N_CHIP),
                      a_block=(s, tm), a_map=lambda i, k, p: (0, i),
                      b_block=(s, fb), b_map=lambda i, k, p: (0, 2 * k + p),
                      o_block=(None, tm, fb), o_map=lambda i, k, p: (k, i, 0),
                      out_shape=jax.ShapeDtypeStruct((N_CHIP, d, fb), BF16))


MXU_WIDTH = 256


def _in_pad(in_width):
    return -(-in_width // MXU_WIDTH) * MXU_WIDTH


def _join_col_shards(name, blocks, own, device, pieces=None):
    n, r, w = blocks.shape
    rows = min(ROWS, r)
    pieces = pieces or [(j, 0, w) for j in range(n)]
    used = sum(b - a for _, a, b in pieces)
    width = _in_pad(used)

    def body(dev_ref, x_ref, own_ref, o_ref):
        block = lambda j: jnp.where(dev_ref[0] == j, own_ref[...], x_ref[j])
        cols = [block(j)[:, a:b] for j, a, b in pieces]
        tail = [jnp.zeros((rows, width - used), o_ref.dtype)] if width > used else []
        o_ref[...] = jnp.concatenate(cols + tail, axis=1)

    return pl.pallas_call(
        body, name=name,
        grid_spec=pltpu.PrefetchScalarGridSpec(
            num_scalar_prefetch=1, grid=(r // rows,),
            in_specs=[pl.BlockSpec((n, rows, w), lambda i, dev: (0, i, 0)), pl.BlockSpec((rows, w), lambda i, dev: (i, 0))],
            out_specs=pl.BlockSpec((rows, width), lambda i, dev: (i, 0))),
        out_shape=jax.ShapeDtypeStruct((r, width), blocks.dtype),
        compiler_params=_params("parallel"),
    )(device, blocks, own)


def _unpermute_q_rows(wt, n_heads):
    r = wt.shape[1]
    nope = wt[:n_heads * HEAD].reshape(n_heads, HEAD, r)
    rope = wt[n_heads * HEAD:].reshape(n_heads, ROPE, r)
    return jnp.concatenate([nope, rope], axis=1).reshape(n_heads * QK, r)


def _local_step(x, tgt, gains, weights, grads, first_after=()):
    pre_mix_g, q_norm_g, kv_norm_g, conv_out_g, attn_out_g, post_mix_g, pre_mlp_g, post_mlp_g = gains
    s, d = x.shape
    conv_width = conv_out_g.shape[1]
    n_groups = conv_width // HEAD
    r_q, r_kv = q_norm_g.shape[1], kv_norm_g.shape[1]
    n_heads = attn_out_g.shape[1] // HEAD
    c_q0 = 3 * conv_width
    c_kv0 = c_q0 + r_q
    c_kr0 = c_kv0 + r_kv
    in_pad = _in_pad(c_kr0 + ROPE)
    tn_in = _fit(in_pad, 6 * MXU_WIDTH)
    tables = _rope_tables(s, n_heads)

    h1 = _rms_fwd("pre_mix_norm", x, pre_mix_g, after=first_after)
    weights.forward(0, (h1,))
    weights.relay(0, tables)
    w_in_p, conv_w = weights.ready(0, ())
    proj = _mm_nn("in_proj", h1, w_in_p, F32, TILE_M, tn_in)
    y_conv = _conv_fwd(proj, conv_w, conv_out_g, n_groups, conv_width + n_heads * HEAD, after=weights.forward(1, (proj,)))
    w_uq_p, w_ukv, w_o = weights.ready(1, (y_conv,))
    qn, q = _norm_up("q_up", proj, (c_q0, r_q), q_norm_g, w_uq_p)
    kvn, kv = _norm_up("kv_up", proj, (c_kv0, r_kv), kv_norm_g, w_ukv)
    qh, kh, vh = _pack_heads(q, kv, proj, c_kr0, tables, n_heads)
    o, probs, mix = _attn_fwd(qh, kh, vh, attn_out_g, y_conv, conv_width, n_heads // 4,
                              lambda done: weights.start(3, weights.forward(2, tuple(done))))
    y = _mm_nn("out_proj", mix, w_o, F32, TILE_M, TILE_N)
    x2, h2 = _mid_fwd(x, y, post_mix_g, pre_mlp_g, after=weights.relay(2, (y,)))
    (w_up,) = weights.ready(2, (h2,))
    a, r = _up_fwd(h2, w_up, lambda done: weights.forward(3, tuple(done)))
    weights.relay(3, (a,))
    (w_down,) = weights.ready(3, ())
    m = _down_fwd(a, w_down)

    d_out, d_m, dg_post_mlp, loss_part = _head(m, x2, tgt, post_mlp_g)
    core = grads.core
    away = _down_half_grad("down_bwd_w_away", a, d_m, core, home=False)
    d_up = _down_bwd_act(d_m, w_down, r, after=grads.send_away(0, away))
    sums = _down_half_grad("down_bwd_w_home", a, d_m, core, home=True, received=grads.received(0, (d_up,)))
    away = _up_half_grad("up_bwd_w_away", h2, d_up, core, home=False, after=grads.send_sums(0, (sums,)))
    d_h2 = _up_bwd_act(d_up, w_up, after=grads.send_away(1, away))
    sums = _up_half_grad("up_bwd_w_home", h2, d_up, core, home=True, received=grads.received(1, (d_h2,)))
    d_x2, d_y, dg_pre_mlp, dg_post_mix = _mid_bwd(x2, y, d_out, d_h2, pre_mlp_g, post_mix_g, after=grads.send_sums(1, (sums,)))
    d_mix = _mm_nt("out_proj_bwd_act", d_y, w_o, F32, TILE_M, TILE_N)
    gw_o = _mm_tn("out_proj_bwd_w", mix, d_y, BF16, TILE_M, TILE_N)
    dqh, dkh, dvh, dg_attn = _attn_bwd(qh, kh, vh, o, probs, d_mix, attn_out_g, conv_width)
    d_q, d_kv, d_kr = _unpack_heads(dqh, dkh, dvh, tables, n_heads)
    gw_uq_t = _mm_tn("q_up_bwd_w", d_q, qn, F32, TILE_M, TILE_N)
    gw_ukv = _mm_tn("kv_up_bwd_w", kvn, d_kv, BF16, TILE_M, TILE_N)
    d_proj, dg_conv, dw_conv = _conv_bwd(proj, d_mix, conv_w, conv_out_g, n_groups, in_pad)
    d_proj, dg_q = _up_norm_bwd("q_up_bwd_act", d_q, w_uq_p, proj, (c_q0, r_q), q_norm_g, d_proj,
                                after=grads.full(2, (gw_o, gw_uq_t, gw_ukv)))
    d_proj, dg_kv = _up_norm_bwd("kv_up_bwd_act", d_kv, w_ukv, proj, (c_kv0, r_kv), kv_norm_g, d_proj, tail=d_kr)
    gw_in_t = _mm_tn("in_proj_bwd_w", d_proj, h1, F32, tn_in, TILE_N)
    updated = grads.update_now(0, grads.send_away(3, gw_in_t))
    d_h1 = _mm_nt("in_proj_bwd_act", d_proj, w_in_p, F32, TILE_M, TILE_N, after=grads.full(3, (gw_in_t,), received=updated))
    grad_x, dg_pre_mix = _first_bwd(x, pre_mix_g, d_h1, d_x2)

    small = [dg_pre_mix, dg_q, dg_kv, dg_conv, dg_attn, dg_post_mix, dg_pre_mlp, dg_post_mlp,
             dw_conv[0], dw_conv[1], dw_conv[2], loss_part]
    return grad_x, jnp.concatenate(small, axis=1)


HBM = pl.BlockSpec(memory_space=pltpu.HBM)
SEM = pl.BlockSpec(memory_space=pltpu.SEMAPHORE)
IN_VMEM = pl.BlockSpec(memory_space=pltpu.VMEM)
SPLIT = pltpu.CompilerParams(has_side_effects=pltpu.SideEffectType.DATAFLOW_SIDE_EFFECTING)


def _in_hbm(a):
    return pltpu.with_memory_space_constraint(a, pltpu.HBM)


def _hbm_like(a):
    return pltpu.HBM(a.shape, a.dtype)


def _place():
    x, y, c = lax.axis_index("x"), lax.axis_index("y"), lax.axis_index("c")
    other_chips = [(1 - x, y), (x, 1 - y), (1 - x, 1 - y)]
    return x, y, c, other_chips


def _block(px, py, pc):
    return 4 * px + 2 * py + pc


def _await(block, sem):
    pltpu.make_async_copy(block, block, sem).wait()


def _relay_route(x, y, c):
    came_from = ((1 - x) * (1 - c) + x * c, y * (1 - c) + (1 - y) * c)
    goes_to = (x * (1 - c) + (1 - x) * c, (1 - y) * (1 - c) + y * c)
    return came_from, goes_to


def _gather_start(name, shards, groups, relayed=(), after=(), lands=None):
    n, ng = len(shards), len(groups)
    if lands is None:
        lands = [lax.empty((N_DEV, *a.shape), a.dtype) for a in shards]

    def body(*refs):
        src, land = refs[:n], refs[n:2 * n]
        sems, token = refs[2 * n + len(after):2 * n + len(after) + 2 * ng], refs[-1]
        x, y, c, chips = _place()
        targets = [(x, y, 1 - c)] + [(*chip, c) for chip in chips]
        for gi, group in enumerate(groups):
            for i, w in enumerate(group):
                for k, to in enumerate(targets[:3] if gi in relayed else targets):
                    pltpu.make_async_remote_copy(
                        src_ref=src[w], dst_ref=land[w].at[_block(x, y, c)],
                        send_sem=sems[2 * gi].at[4 * i + k], recv_sem=sems[2 * gi + 1].at[4 * i + k],
                        device_id=to, device_id_type=MESH).start()
        token[...] = jnp.zeros_like(token)

    sem_shapes = [pltpu.SemaphoreType.DMA((4 * len(g),)) for g in groups for _ in range(2)]
    out = pl.pallas_call(
        body, name=name,
        in_specs=[HBM] * (2 * n) + [ANY] * len(after),
        out_specs=[SEM] * (2 * ng) + [HBM] * (2 * n) + [IN_VMEM],
        out_shape=sem_shapes + [_hbm_like(a) for a in shards] + [_hbm_like(a) for a in lands]
        + [jax.ShapeDtypeStruct((SUBLANE, LANE), F32)],
        input_output_aliases={i: 2 * ng + i for i in range(2 * n)},
        compiler_params=SPLIT,
    )(*[_in_hbm(a) for a in shards], *[_in_hbm(a) for a in lands], *after)
    sems = [(out[2 * gi], out[2 * gi + 1]) for gi in range(ng)]
    return sems, out[2 * ng:2 * ng + n], out[2 * ng + n:2 * ng + 2 * n], out[-1]


def _gather_forward(name, shards, lands, send1, recv1, after, relayed=False):
    n = len(lands)

    def body(*refs):
        src, land = refs[:n], refs[n:2 * n]
        s1, r1 = refs[2 * n], refs[2 * n + 1]
        s2, r2 = refs[2 * n + 2 + len(after)], refs[2 * n + 3 + len(after)]
        x, y, c, chips = _place()
        me, sibling = (x, y, c), (x, y, 1 - c)
        for j, chip in enumerate(chips[:2] if relayed else chips):
            for i in range(n):
                blk = land[i].at[_block(*chip, c)]
                pltpu.make_async_remote_copy(src_ref=blk, dst_ref=blk, send_sem=s1.at[4 * i + 1 + j], recv_sem=r1.at[4 * i + 1 + j],
                                             device_id=me, device_id_type=MESH).wait_recv()
                pltpu.make_async_remote_copy(src_ref=blk, dst_ref=blk, send_sem=s2.at[3 * i + j], recv_sem=r2.at[3 * i + j],
                                             device_id=sibling, device_id_type=MESH).start()
        if relayed:
            came_from, goes_to = _relay_route(x, y, c)
            for i in range(n):
                blk = land[i].at[_block(*came_from, c)]
                pltpu.make_async_remote_copy(src_ref=blk, dst_ref=blk, send_sem=s2.at[3 * i + 2], recv_sem=r2.at[3 * i + 2],
                                             device_id=(*goes_to, c), device_id_type=MESH).start()
        for i in range(n):
            blk = land[i].at[_block(x, y, 1 - c)]
            pltpu.make_async_remote_copy(src_ref=blk, dst_ref=blk, send_sem=s1.at[4 * i], recv_sem=r1.at[4 * i],
                                         device_id=me, device_id_type=MESH).wait_recv()
            for k in range(3 if relayed else 4):
                pltpu.make_async_remote_copy(src_ref=src[i], dst_ref=land[i].at[_block(x, y, c)], send_sem=s1.at[4 * i + k],
                                             recv_sem=r1.at[4 * i + k], device_id=sibling, device_id_type=MESH).wait_send()

    sem = pltpu.SemaphoreType.DMA((3 * n,))
    out = pl.pallas_call(
        body, name=name,
        in_specs=[HBM] * (2 * n) + [SEM, SEM] + [ANY] * len(after),
        out_specs=[SEM, SEM] + [HBM] * n,
        out_shape=[sem, sem] + [_hbm_like(a) for a in lands],
        input_output_aliases={n + i: 2 + i for i in range(n)},
        compiler_params=SPLIT,
    )(*shards, *lands, send1, recv1, *after)
    return (out[0], out[1]), out[2:]


def _gather_relay_forward(name, lands, send2, recv2, after):
    n = len(lands)

    def body(*refs):
        land, s2, r2 = refs[:n], refs[n], refs[n + 1]
        s3, r3 = refs[n + 2 + len(after)], refs[n + 3 + len(after)]
        x, y, c, _ = _place()
        me, sibling = (x, y, c), (x, y, 1 - c)
        came_from, _ = _relay_route(x, y, c)
        for i in range(n):
            blk = land[i].at[_block(1 - x, 1 - y, c)]
            pltpu.make_async_remote_copy(src_ref=blk, dst_ref=blk, send_sem=s2.at[3 * i + 2], recv_sem=r2.at[3 * i + 2],
                                         device_id=me, device_id_type=MESH).wait_recv()
            pltpu.make_async_remote_copy(src_ref=blk, dst_ref=blk, send_sem=s3.at[i], recv_sem=r3.at[i],
                                         device_id=sibling, device_id_type=MESH).start()
            sent = land[i].at[_block(*came_from, c)]
            pltpu.make_async_remote_copy(src_ref=sent, dst_ref=sent, send_sem=s2.at[3 * i + 2], recv_sem=r2.at[3 * i + 2],
                                         device_id=me, device_id_type=MESH).wait_send()

    sem = pltpu.SemaphoreType.DMA((n,))
    out = pl.pallas_call(
        body, name=name,
        in_specs=[HBM] * n + [SEM, SEM] + [ANY] * len(after),
        out_specs=[SEM, SEM] + [HBM] * n,
        out_shape=[sem, sem] + [_hbm_like(a) for a in lands],
        input_output_aliases={i: 2 + i for i in range(n)},
        compiler_params=SPLIT,
    )(*lands, send2, recv2, *after)
    return (out[0], out[1]), out[2:]


def _gather_wait(name, lands, send2, recv2, after, relay_sems=None):
    n = len(lands)
    n_sems = 2 if relay_sems is None else 4

    def body(*refs):
        land, s2, r2 = refs[:n], refs[n], refs[n + 1]
        for i in range(n):
            for j in range(3 if relay_sems is None else 2):
                _await(land[i].at[0], r2.at[3 * i + j])
                _await(land[i].at[0], s2.at[3 * i + j])
            if relay_sems is not None:
                _await(land[i].at[0], refs[n + 3].at[i])
                _await(land[i].at[0], refs[n + 2].at[i])

    return pl.pallas_call(
        body, name=name,
        in_specs=[HBM] * n + [SEM] * n_sems + [ANY] * len(after), out_specs=[HBM] * n, out_shape=[_hbm_like(a) for a in lands],
        input_output_aliases={i: i for i in range(n)},
        compiler_params=SPLIT,
    )(*lands, send2, recv2, *(relay_sems or ()), *after)


def _pair_exchange(name, grads, shard_rows):
    n = len(grads)
    shapes = [(g.shape[1:] if r is None else (r, g.shape[1])) for g, r in zip(grads, shard_rows)]

    def body(*refs):
        ins, recv = refs[:n], refs[n:2 * n]
        send_sems, recv_sems = refs[2 * n:]
        x, y, c, _ = _place()
        sends = []
        for w in range(n):
            for k in range(N_CHIP):
                j, r = 2 * k + 1 - c, shard_rows[w]
                src = ins[w].at[j] if r is None else ins[w].at[pl.ds(pl.multiple_of(j * r, SUBLANE), r), :]
                sends.append(pltpu.make_async_remote_copy(
                    src_ref=src, dst_ref=recv[w].at[k],
                    send_sem=send_sems.at[w, k], recv_sem=recv_sems.at[w, k],
                    device_id=(x, y, 1 - c), device_id_type=MESH))
        for cp in sends:
            cp.start()
        for cp in sends:
            cp.wait()

    return pl.pallas_call(
        body, name=name,
        in_specs=[ANY] * n, out_specs=[ANY] * n,
        out_shape=[jax.ShapeDtypeStruct((N_CHIP, *shape), g.dtype) for g, shape in zip(grads, shapes)],
        scratch_shapes=[pltpu.SemaphoreType.DMA((n, N_CHIP))] * 2,
    )(*grads)


def _pair_sum_rows(name, grad, received, core):
    _, r, c = received.shape
    tc = _fit(c, 512)

    def body(core_ref, a_ref, b_ref, o_ref):
        o_ref[...] = (a_ref[...] + b_ref[...]).astype(o_ref.dtype)

    spec = pl.BlockSpec((None, r, tc), lambda k, i, core_ref: (k, 0, i))
    return pl.pallas_call(
        body, name=name,
        grid_spec=pltpu.PrefetchScalarGridSpec(
            num_scalar_prefetch=1, grid=(N_CHIP, c // tc),
            in_specs=[pl.BlockSpec((r, tc), lambda k, i, core_ref: (2 * k + core_ref[0], i)), spec],
            out_specs=spec),
        out_shape=jax.ShapeDtypeStruct(received.shape, BF16),
        compiler_params=_params("parallel", "parallel"),
    )(core, grad, received)


def _pair_sum(name, grad, received, core):
    _, r, c = received.shape
    rows = min(ROWS, r)
    assert r % rows == 0

    def body(core_ref, a_ref, b_ref, o_ref):
        o_ref[...] = (a_ref[...].astype(F32) + b_ref[...].astype(F32)).astype(o_ref.dtype)

    spec = pl.BlockSpec((None, rows, c), lambda k, i, core_ref: (k, i, 0))
    return pl.pallas_call(
        body, name=name,
        grid_spec=pltpu.PrefetchScalarGridSpec(
            num_scalar_prefetch=1, grid=(N_CHIP, r // rows),
            in_specs=[pl.BlockSpec((None, None, rows, c), lambda k, i, core_ref: (k, core_ref[0], i, 0)), spec],
            out_specs=spec),
        out_shape=jax.ShapeDtypeStruct(received.shape, received.dtype),
        compiler_params=_params("parallel", "parallel"),
    )(core, grad.reshape(N_CHIP, 2, r, c), received)


def _away_shard(src, k, c, shard_rows):
    if shard_rows is None:
        return src.at[k]
    return src.at[pl.ds(pl.multiple_of((2 * k + 1 - c) * shard_rows, SUBLANE), shard_rows), :]


def _pair_send_start(name, away, shard_rows=None):
    shape = away.shape if shard_rows is None else (N_CHIP, shard_rows, away.shape[1])
    land = lax.empty(shape, away.dtype)

    def body(src, dst, send, recv, src_thru, dst_thru, token):
        x, y, c, _ = _place()
        for k in range(N_CHIP):
            pltpu.make_async_remote_copy(src_ref=_away_shard(src, k, c, shard_rows), dst_ref=dst.at[k], send_sem=send.at[k],
                                         recv_sem=recv.at[k], device_id=(x, y, 1 - c), device_id_type=MESH).start()
        token[...] = jnp.zeros_like(token)

    sem = pltpu.SemaphoreType.DMA((N_CHIP,))
    out = pl.pallas_call(
        body, name=name,
        in_specs=[HBM, HBM], out_specs=[SEM, SEM, HBM, HBM, IN_VMEM],
        out_shape=[sem, sem, _hbm_like(away), _hbm_like(land), jax.ShapeDtypeStruct((SUBLANE, LANE), F32)],
        input_output_aliases={0: 2, 1: 3},
        compiler_params=SPLIT,
    )(_in_hbm(away), _in_hbm(land))
    return (out[0], out[1]), out[2], out[3], out[4]


def _pair_send_wait(name, sems, src, land, after, shard_rows=None):
    def body(src_ref, dst_ref, send, recv, *rest):
        for k in range(N_CHIP):
            _await(dst_ref.at[k], send.at[k])
            _await(dst_ref.at[k], recv.at[k])

    return pl.pallas_call(
        body, name=name,
        in_specs=[HBM, HBM, SEM, SEM] + [ANY] * len(after), out_specs=HBM, out_shape=_hbm_like(land),
        input_output_aliases={1: 0},
        compiler_params=SPLIT,
    )(src, land, *sems, *after)


def _chip_send_start(name, sums):
    n = len(sums)
    lands = [lax.empty(a.shape, a.dtype) for a in sums]

    def body(*refs):
        src, land = refs[:n], refs[n:2 * n]
        send, recv, token = refs[2 * n], refs[2 * n + 1], refs[-1]
        x, y, c, chips = _place()
        for w in range(n):
            for j, (px, py) in enumerate(chips):
                pltpu.make_async_remote_copy(
                    src_ref=src[w].at[2 * px + py], dst_ref=land[w].at[2 * x + y],
                    send_sem=send.at[3 * w + j], recv_sem=recv.at[3 * w + j],
                    device_id=(px, py, c), device_id_type=MESH).start()
        token[...] = jnp.zeros_like(token)

    sem = pltpu.SemaphoreType.DMA((3 * n,))
    out = pl.pallas_call(
        body, name=name,
        in_specs=[HBM] * (2 * n),
        out_specs=[SEM, SEM] + [HBM] * (2 * n) + [IN_VMEM],
        out_shape=[sem, sem] + [_hbm_like(a) for a in sums] + [_hbm_like(a) for a in lands]
        + [jax.ShapeDtypeStruct((SUBLANE, LANE), F32)],
        input_output_aliases={i: 2 + i for i in range(2 * n)},
        compiler_params=SPLIT,
    )(*[_in_hbm(a) for a in sums], *[_in_hbm(a) for a in lands])
    return (out[0], out[1]), out[2:2 + n], out[2 + n:2 + 2 * n], out[-1]


def _chip_send_wait(name, groups, after):
    counts = [len(g[1]) for g in groups]
    n = sum(counts)

    def body(*refs):
        land = refs[n:2 * n]
        sems = refs[2 * n:2 * n + 2 * len(groups)]
        w = 0
        for gi, count in enumerate(counts):
            for i in range(count):
                for j in range(3):
                    _await(land[w].at[0], sems[2 * gi].at[3 * i + j])
                    _await(land[w].at[0], sems[2 * gi + 1].at[3 * i + j])
                w += 1

    sums = [a for g in groups for a in g[1]]
    lands = [a for g in groups for a in g[2]]
    sems = [s for g in groups for s in g[0]]
    return pl.pallas_call(
        body, name=name,
        in_specs=[HBM] * (2 * n) + [SEM] * len(sems) + [ANY] * len(after),
        out_specs=[HBM] * n, out_shape=[_hbm_like(a) for a in lands],
        input_output_aliases={n + i: i for i in range(n)},
        compiler_params=SPLIT,
    )(*sums, *lands, *sems, *after)


def _small_all_reduce(part, after=()):
    _, w = part.shape

    def body(p_ref, *rest):
        o_ref, buf, send_sems, recv_sems = rest[len(after):]
        x, y, c, _ = _place()
        me = 4 * x + 2 * y + c
        buf[me] = jnp.sum(p_ref[...], axis=0, keepdims=True)
        copies = []
        for k in range(1, N_DEV):
            dx, dy, dc = (k >> 2) & 1, (k >> 1) & 1, k & 1
            copies.append(pltpu.make_async_remote_copy(
                src_ref=buf.at[me], dst_ref=buf.at[me], send_sem=send_sems.at[k - 1], recv_sem=recv_sems.at[k - 1],
                device_id=(x ^ dx, y ^ dy, c ^ dc), device_id_type=MESH))
        for cp in copies:
            cp.start()
        for cp in copies:
            cp.wait()
        tot = buf[0]
        for d in range(1, N_DEV):
            tot = tot + buf[d]
        o_ref[...] = tot
        loss = jnp.sum(tot[:, w - LANE:], axis=1, keepdims=True)
        o_ref[:, w - LANE:] = jnp.broadcast_to(loss, (1, LANE))

    return pl.pallas_call(
        body, name="small_all_reduce",
        in_specs=[IN_VMEM] + [ANY] * len(after), out_specs=IN_VMEM,
        out_shape=jax.ShapeDtypeStruct((1, w), F32),
        scratch_shapes=[pltpu.VMEM((N_DEV, 1, w), F32), pltpu.SemaphoreType.DMA((N_DEV - 1,)), pltpu.SemaphoreType.DMA((N_DEV - 1,))],
        compiler_params=pltpu.CompilerParams(vmem_limit_bytes=VMEM_LIMIT_BYTES),
    )(part, *after)


def _adamw(w, g, m, v):
    m = ADAM_B1 * m + (1.0 - ADAM_B1) * g
    v = ADAM_B2 * v + (1.0 - ADAM_B2) * (g * g)
    m_hat = m / (1.0 - ADAM_B1 ** ADAM_STEP)
    v_hat = v / (1.0 - ADAM_B2 ** ADAM_STEP)
    delta = -ADAM_LR * (m_hat / (jnp.sqrt(v_hat) + ADAM_EPS) + ADAM_WD * w)
    return delta, m, v


def _sum_adam_block(chip_ref, p_ref, own_ref, w_ref, m_ref, v_ref, g_ref, d_ref, mo_ref, vo_ref):
    g = None
    for k in range(N_CHIP):
        term = jnp.where(chip_ref[0] == k, own_ref[...], p_ref[k]).astype(F32)
        g = term if g is None else g + term
    g_ref[...] = g
    d_ref[...], mo_ref[...], vo_ref[...] = _adamw(w_ref[...], g, m_ref[...], v_ref[...])


def _sum_adam(name, parts, sums, chip, w, m, v, after=()):
    _, r, c = w.shape
    n_after = len(after)
    by_rows = r % ROWS == 0 or r < ROWS
    tr, tc = (min(ROWS, r), c) if by_rows else (r, _fit(c, 512))
    at = (lambda i: (i, 0)) if by_rows else (lambda i: (0, i))

    def body(chip_ref, p_ref, own_ref, w_ref, m_ref, v_ref, *rest):
        _sum_adam_block(chip_ref, p_ref, own_ref, w_ref, m_ref, v_ref, *rest[n_after:])

    blk = pl.BlockSpec((None, tr, tc), lambda i, chip_ref: (0, *at(i)))
    out = jax.ShapeDtypeStruct((1, r, c), F32)
    return pl.pallas_call(
        body, name=name,
        grid_spec=pltpu.PrefetchScalarGridSpec(
            num_scalar_prefetch=1, grid=(r // tr if by_rows else c // tc,),
            in_specs=[pl.BlockSpec((N_CHIP, tr, tc), lambda i, chip_ref: (0, *at(i))),
                      pl.BlockSpec((None, tr, tc), lambda i, chip_ref: (chip_ref[0], *at(i))), blk, blk, blk]
            + [ANY] * n_after,
            out_specs=[blk] * 4),
        out_shape=[out] * 4,
        compiler_params=_params("parallel"),
    )(chip, parts, sums, w, m, v, *after)


def _adam_gains(total, ws, ms, vs):
    n = len(ws)
    widths = [w.shape[1] for w in ws]

    def body(t_ref, *refs):
        w_refs, m_refs, v_refs, outs = refs[:n], refs[n:2 * n], refs[2 * n:3 * n], refs[3 * n:]
        off = 0
        for i in range(n):
            g = t_ref[:, off:off + widths[i]]
            off += widths[i]
            g_ref, d_ref, mo_ref, vo_ref = outs[4 * i:4 * i + 4]
            g_ref[...] = g
            d_ref[...], mo_ref[...], vo_ref[...] = _adamw(w_refs[i][...], g, m_refs[i][...], v_refs[i][...])

    out = pl.pallas_call(
        body, name="adam_gains",
        out_shape=[jax.ShapeDtypeStruct(w.shape, F32) for w in ws for _ in range(4)],
    )(total, *ws, *ms, *vs)
    return [tuple(out[4 * i:4 * i + 4]) for i in range(n)]


def _adam_taps(total, first_col, device, w, m, v):
    _, n_taps, cw = w.shape
    col_block = lambda t, dev: (0, first_col // cw + t * N_DEV + dev[0])
    tap = pl.BlockSpec((None, 1, cw), lambda t, dev: (t, 0, 0))

    def body(dev_ref, t_ref, w_ref, m_ref, v_ref, g_ref, d_ref, mo_ref, vo_ref):
        g = t_ref[...]
        g_ref[...] = g
        d_ref[...], mo_ref[...], vo_ref[...] = _adamw(w_ref[...], g, m_ref[...], v_ref[...])

    shape3 = (n_taps, 1, cw)
    out = pl.pallas_call(
        body, name="adam_taps",
        grid_spec=pltpu.PrefetchScalarGridSpec(
            num_scalar_prefetch=1, grid=(n_taps,),
            in_specs=[pl.BlockSpec((1, cw), col_block), tap, tap, tap], out_specs=[tap] * 4),
        out_shape=[jax.ShapeDtypeStruct(shape3, F32)] * 4,
    )(device, total, w.reshape(shape3), m.reshape(shape3), v.reshape(shape3))
    return tuple(o.reshape(w.shape) for o in out)


def kernel(x, pre_mix_g, w_in, conv_w, q_norm_g, w_uq, kv_norm_g, w_ukv, conv_out_g, attn_out_g, w_o, post_mix_g, pre_mlp_g, w_up, w_down, post_mlp_g, loss_target, m_pre_mix_g, m_w_in, m_conv_w, m_q_norm_g, m_w_uq, m_kv_norm_g, m_w_ukv, m_conv_out_g, m_attn_out_g, m_w_o, m_post_mix_g, m_pre_mlp_g, m_w_up, m_w_down, m_post_mlp_g, v_pre_mix_g, v_w_in, v_conv_w, v_q_norm_g, v_w_uq, v_kv_norm_g, v_w_ukv, v_conv_out_g, v_attn_out_g, v_w_o, v_post_mix_g, v_pre_mlp_g, v_w_up, v_w_down, v_post_mlp_g):
    me = 4 * lax.axis_index("x") + 2 * lax.axis_index("y") + lax.axis_index("c")
    core = lax.axis_index("c").astype(jnp.int32).reshape(1)
    chip = (2 * lax.axis_index("x") + lax.axis_index("y")).astype(jnp.int32).reshape(1)
    gains = (pre_mix_g, q_norm_g, kv_norm_g, conv_out_g, attn_out_g, post_mix_g, pre_mlp_g, post_mlp_g)
    gain_m = (m_pre_mix_g, m_q_norm_g, m_kv_norm_g, m_conv_out_g, m_attn_out_g, m_post_mix_g, m_pre_mlp_g, m_post_mlp_g)
    gain_v = (v_pre_mix_g, v_q_norm_g, v_kv_norm_g, v_conv_out_g, v_attn_out_g, v_post_mix_g, v_pre_mlp_g, v_post_mlp_g)
    names = ("w_in", "w_uq", "w_ukv", "w_o", "w_up", "w_down")
    big = dict(zip(names, (w_in, w_uq, w_ukv, w_o, w_up, w_down)))
    big_m = dict(zip(names, (m_w_in, m_w_uq, m_w_ukv, m_w_o, m_w_up, m_w_down)))
    big_v = dict(zip(names, (v_w_in, v_w_uq, v_w_ukv, v_w_o, v_w_up, v_w_down)))
    n_heads = attn_out_g.shape[1] // HEAD
    n_taps = conv_w.shape[1]

    gathered = ("w_in", "conv", "w_uq", "w_ukv", "w_o", "w_up", "w_down")
    gather_groups = ((0, 1), (2, 3, 4), (5,), (6,))
    taps = jnp.pad(conv_w[0], ((0, SUBLANE - n_taps), (0, 0)))
    relayed_groups = (0, 2, 3)
    sems1, shards, lands, token = _gather_start("gather_start_first", [w_in[0].astype(BF16), taps], ((0, 1),), relayed=(0,))
    sems1, shards, lands = list(sems1), list(shards), list(lands)
    behind = token[0, 0]
    rest = list(lax.optimization_barrier(tuple((big[nm][0] + behind).astype(BF16) for nm in gathered[2:])))

    rest_lands = [lax.dynamic_update_index_in_dim(lax.empty((N_DEV, *a.shape), a.dtype), a, me, 0) for a in rest]

    def start_more(name, some, groups, relayed, after):
        sems_b, shards_b, lands_b, started = _gather_start(name, rest[some], groups, relayed=relayed, after=after,
                                                           lands=rest_lands[some])
        sems1.extend(sems_b)
        shards.extend(shards_b)
        lands.extend(lands_b)
        return started

    start_rest = lambda after: start_more("gather_start_rest", slice(0, 4), ((0, 1, 2), (3,)), (1,), after)
    start_last = lambda after: start_more("gather_start_last", slice(4, 5), ((0,),), (0,), after)

    cols = lambda a: jnp.concatenate([a[j] for j in range(N_DEV)], axis=1)
    rows = lambda a: a.reshape(N_DEV * a.shape[1], a.shape[2])
    device = me.astype(jnp.int32).reshape(1)
    own_in = lambda a, shard: lax.dynamic_update_index_in_dim(a, shard, me, 0)
    q_pieces = [(h, 0, HEAD) for h in range(n_heads)] + [(h, HEAD, QK) for h in range(n_heads)]
    ready = {
        "w_in": lambda a, shard: _join_col_shards("join_w_in", a, shard, device),
        "conv": lambda a, shard: cols(own_in(a, shard))[:n_taps],
        "w_uq": lambda a, shard: _join_col_shards("join_w_uq", a, shard, device, q_pieces),
        "w_ukv": lambda a, shard: cols(a),
        "w_o": lambda a, shard: rows(a),
        "w_up": lambda a, shard: a,
        "w_down": lambda a, shard: rows(a),
    }
    assert w_uq.shape[2] == QK

    class Weights:
        def __init__(self):
            self.passed, self.relayed = {}, {}

        def forward(self, group, after):
            idx = gather_groups[group]
            if group == 0:
                after = (*after, *rest_lands)
            self.passed[group] = _gather_forward(f"gather_forward_{group}", [shards[i] for i in idx], [lands[i] for i in idx],
                                                 *sems1[group], after, relayed=group in relayed_groups)
            return tuple(self.passed[group][1])

        def start(self, group, after):
            assert group == len(gather_groups) - 1
            return (start_last(after),)

        def relay(self, group, after):
            sems2, mid = self.passed[group]
            self.relayed[group], mid = _gather_relay_forward(f"gather_relay_{group}", mid, *sems2, after)
            self.passed[group] = (sems2, mid)
            if group == 0:
                start_rest(tuple(mid))
            return tuple(mid)

        def ready(self, group, after):
            sems2, mid = self.passed[group]
            full = _gather_wait(f"gather_wait_{group}", mid, *sems2, after, relay_sems=self.relayed.get(group))
            out = []
            return [ready[gathered[i]](a, shards[i]) for i, a in zip(gather_groups[group], full)]

    weights = Weights()

    col_blocks = lambda g: g.reshape(g.shape[0], N_DEV, g.shape[1] // N_DEV).transpose(1, 0, 2)
    row_blocks = lambda g: g.reshape(N_DEV, g.shape[0] // N_DEV, g.shape[1])
    grad_groups = (("w_down",), ("w_up",), ("w_o", "w_uq", "w_ukv"), ("w_in",))
    transposed = {"w_in": w_in.shape[2], "w_uq": w_uq.shape[2]}
    to_blocks = {
        "w_in": lambda g: g, "w_uq": lambda g: _unpermute_q_rows(g, n_heads),
        "w_ukv": col_blocks, "w_o": row_blocks, "w_up": lambda g: g, "w_down": row_blocks,
    }
    in_flight = []

    class Grads:
        def __init__(self):
            self.core = core
            self.away = {}

        def send_sums(self, group, sums):
            sems, sums, parts, tok = _chip_send_start(f"chip_send_start_{group}", list(sums))
            in_flight.append((sems, sums, parts))
            return (tok,)

        def full(self, group, arrays, received=None):
            nms = grad_groups[group]
            if received is None:
                blocks = [to_blocks[nm](g) for nm, g in zip(nms, arrays)]
                got = _pair_exchange(f"pair_exchange_{group}", blocks, [transposed.get(nm) for nm in nms])
            else:
                blocks, got = [self.away[group][1]], [self.received(group, received)]
            sums = [(_pair_sum_rows if nm in transposed else _pair_sum)(f"pair_sum_{nm}", g, r, core)
                    for nm, g, r in zip(nms, blocks, got)]
            return self.send_sums(group, sums)

        def send_away(self, group, half):
            nm = grad_groups[group][0]
            rows = transposed.get(nm)
            sems, src, land, tok = _pair_send_start(f"pair_send_start_{group}", half if rows is None else to_blocks[nm](half), rows)
            self.away[group] = (sems, src, land, rows)
            return (tok,)

        def received(self, group, after):
            sems, src, land, rows = self.away[group]
            return _pair_send_wait(f"pair_send_wait_{group}", sems, src, land, after, rows)

        def update_now(self, group, after):
            return update(str(group), group, group + 1, after)

    big_out = {}

    def update(tag, first, last, after):
        picked = [i for i in range(first, last) if grad_groups[i][0] not in big_out]
        groups = [in_flight[i] for i in picked]
        parts = _chip_send_wait("chip_send_wait_" + tag, groups, after)
        nms = [nm for i in picked for nm in grad_groups[i]]
        sums = [a for _, s, _ in groups for a in s]
        for nm, p, s in zip(nms, parts, sums):
            view = (lambda a: jnp.swapaxes(a, 1, 2)) if nm in transposed else (lambda a: a)
            out = _sum_adam("adam_" + nm, p, s, chip, view(big[nm]), view(big_m[nm]), view(big_v[nm]), after=after)
            after = (out[0],)
            big_out[nm] = [view(o) for o in out]
        return after

    grad_x, small = _local_step(x[0], loss_target[0], gains, weights, Grads(), first_after=(token,))

    after = update("early", 0, len(in_flight) - 1, (grad_x,))
    total = _small_all_reduce(small, after=after)
    update("late", len(in_flight) - 1, len(in_flight), (total,))
    big_out = [big_out[nm] for nm in names]

    gain_out = _adam_gains(total, gains, gain_m, gain_v)
    taps_out = _adam_taps(total, sum(g.shape[1] for g in gains), me.astype(jnp.int32).reshape(1), conv_w, m_conv_w, v_conv_w)
    loss = total[0, total.shape[1] - 1]

    order = (0, "w_in", "conv", 1, "w_uq", 2, "w_ukv", 3, 4, "w_o", 5, 6, "w_up", "w_down", 7)
    by_name = dict(zip(names, big_out))
    outs = [loss, grad_x[None]]
    for kind in range(4):
        for item in order:
            if item == "conv":
                outs.append(taps_out[kind])
            elif isinstance(item, int):
                outs.append(gain_out[item][kind])
            else:
                outs.append(by_name[item][kind])
    return tuple(outs)
```

```python
import math

import jax
import jax.numpy as jnp
from jax import lax
from jax.experimental import pallas as pl
from jax.experimental.pallas import tpu as pltpu

F32 = jnp.float32
BF16 = jnp.bfloat16

EPS = 1e-6
NEG_INF = -1e30
HEAD = 128
ROPE = 64
QK = HEAD + ROPE
CHUNK = 64
ROPE_THETA = 10000.0
ADAM_LR, ADAM_B1, ADAM_B2, ADAM_EPS, ADAM_WD, ADAM_STEP = 0.001, 0.9, 0.999, 1e-08, 0.01, 10

LANE = 128
SUBLANE = 8
VMEM_LIMIT_BYTES = 56 * 1024 * 1024

N_DEV = 8
N_CHIP = 4
MESH = pl.DeviceIdType.MESH


def _params(*sem):
    return pltpu.CompilerParams(dimension_semantics=sem, vmem_limit_bytes=VMEM_LIMIT_BYTES)


ANY = pl.BlockSpec(memory_space=pl.ANY)


def _call(body, *, in_specs, after=(), **kw):
    n_in, n_after = len(in_specs), len(after)

    def ordered(*refs):
        body(*refs[:n_in], *refs[n_in + n_after:])

    call = pl.pallas_call(ordered, in_specs=[*in_specs, *[ANY] * n_after], **kw)
    return lambda *operands: call(*operands, *after)


def _sublane_sum(v):
    r, w = v.shape
    return jnp.sum(v.reshape(r // SUBLANE, SUBLANE, w), axis=0)


def _rstd(x):
    return lax.rsqrt(jnp.mean(x * x, axis=-1, keepdims=True) + EPS)


def _rms_bwd(x, g, dy):
    r = _rstd(x)
    xh = x * r
    dxh = dy * g
    dx = r * (dxh - xh * jnp.mean(dxh * xh, axis=-1, keepdims=True))
    return dx, dy * xh


def _accumulate(ref, val, step):
    @pl.when(step == 0)
    def _():
        ref[...] = val

    @pl.when(step > 0)
    def _():
        ref[...] += val


NN = ((1,), (0,))
NT = ((1,), (1,))
TN = ((0,), (0,))


def _matmul(name, a, b, *, grid, a_spec, b_spec, out_shape, out_specs, contract, nk=1, acc_shape=None,
            extras=(), extra_specs=(), epilogue=None, after=()):
    multi = isinstance(out_shape, (tuple, list))
    out_shapes = tuple(out_shape) if multi else (out_shape,)
    n_out = len(out_shapes)
    n_extra = len(extras)

    def body(a_ref, b_ref, *rest):
        x_refs = rest[:n_extra]
        o_refs = rest[n_extra:n_extra + n_out]

        def emit(acc):
            vals = epilogue(acc, *[r[...] for r in x_refs]) if epilogue else (acc,)
            for r, v in zip(o_refs, vals):
                r[...] = v.astype(r.dtype)

        p = lax.dot_general(a_ref[...], b_ref[...], (contract, ((), ())), preferred_element_type=F32)
        if nk == 1:
            emit(p)
        else:
            acc_ref = rest[n_extra + n_out]
            k = pl.program_id(2)
            _accumulate(acc_ref, p, k)

            @pl.when(k == nk - 1)
            def _():
                emit(acc_ref[...])

    sem = ("parallel", "parallel") + (("arbitrary",) if nk > 1 else ())
    return _call(
        body, name=name, grid=grid, after=after,
        in_specs=[a_spec, b_spec, *extra_specs],
        out_specs=out_specs,
        out_shape=out_shape,
        scratch_shapes=[pltpu.VMEM(acc_shape, F32)] if nk > 1 else [],
        compiler_params=_params(*sem),
    )(a, b, *extras)


def _fit(n, tile):
    if n <= tile:
        return n
    t = tile - tile % LANE
    while n % t:
        t -= LANE
    return t


def _mm_nn(name, a, b, out_dtype, tm, tn, after=()):
    m, k = a.shape
    n = b.shape[1]
    tm, tn = _fit(m, tm), _fit(n, tn)
    return _matmul(name, a, b, grid=(m // tm, n // tn), after=after,
                   a_spec=pl.BlockSpec((tm, k), lambda i, j: (i, 0)),
                   b_spec=pl.BlockSpec((k, tn), lambda i, j: (0, j)),
                   out_shape=jax.ShapeDtypeStruct((m, n), out_dtype),
                   out_specs=pl.BlockSpec((tm, tn), lambda i, j: (i, j)), contract=NN)


def _mm_nt(name, a, b, out_dtype, tm, tn, after=()):
    m, k = a.shape
    n = b.shape[0]
    tm, tn = _fit(m, tm), _fit(n, tn)
    return _matmul(name, a, b, grid=(m // tm, n // tn), after=after,
                   a_spec=pl.BlockSpec((tm, k), lambda i, j: (i, 0)),
                   b_spec=pl.BlockSpec((tn, k), lambda i, j: (j, 0)),
                   out_shape=jax.ShapeDtypeStruct((m, n), out_dtype),
                   out_specs=pl.BlockSpec((tm, tn), lambda i, j: (i, j)), contract=NT)


def _mm_tn(name, a, b, out_dtype, tm, tn):
    s, m = a.shape
    n = b.shape[1]
    tm, tn = _fit(m, tm), _fit(n, tn)
    return _matmul(name, a, b, grid=(m // tm, n // tn),
                   a_spec=pl.BlockSpec((s, tm), lambda i, j: (0, i)),
                   b_spec=pl.BlockSpec((s, tn), lambda i, j: (0, j)),
                   out_shape=jax.ShapeDtypeStruct((m, n), out_dtype),
                   out_specs=pl.BlockSpec((tm, tn), lambda i, j: (i, j)), contract=TN)


ROWS = 256


def _row_spec(rows, width):
    return pl.BlockSpec((rows, width), lambda i: (i, 0))


def _fixed_spec(rows, width):
    return pl.BlockSpec((rows, width), lambda i: (0, 0))


def _column_pieces(rows, start, width):
    piece = math.gcd(start, width)
    assert piece % LANE == 0
    return [pl.BlockSpec((rows, piece), lambda i, b=start // piece + p: (i, b)) for p in range(width // piece)]


def _rms_fwd(name, x, g, after=()):
    s, w = x.shape
    rows = min(ROWS, s)

    def body(x_ref, g_ref, o_ref):
        xv = x_ref[...]
        o_ref[...] = (xv * _rstd(xv) * g_ref[...]).astype(o_ref.dtype)

    return _call(
        body, name=name, grid=(s // rows,), after=after,
        in_specs=[_row_spec(rows, w), _fixed_spec(1, w)],
        out_specs=_row_spec(rows, w),
        out_shape=jax.ShapeDtypeStruct((s, w), BF16),
        compiler_params=_params("parallel"),
    )(x, g)


def _norm_up(name, x, cols, g, w, after=()):
    s = x.shape[0]
    start, width = cols
    n = w.shape[1]
    tm = min(TILE_M, s)
    pieces = _column_pieces(tm, start, width)
    n_p = len(pieces)

    def body(*refs):
        g_ref, w_ref, xn_ref, o_ref = refs[n_p:]
        xv = refs[0][...] if n_p == 1 else jnp.concatenate([r[...] for r in refs[:n_p]], axis=1)
        xn = (xv * _rstd(xv) * g_ref[...]).astype(BF16)
        xn_ref[...] = xn
        o_ref[...] = jnp.dot(xn, w_ref[...], preferred_element_type=F32)

    return _call(
        body, name=name, grid=(s // tm,), after=after,
        in_specs=[*pieces, _fixed_spec(1, width), _fixed_spec(width, n)],
        out_specs=[_row_spec(tm, width), _row_spec(tm, n)],
        out_shape=[jax.ShapeDtypeStruct((s, width), BF16), jax.ShapeDtypeStruct((s, n), F32)],
        compiler_params=_params("parallel"),
    )(*[x] * n_p, g, w)


def _up_norm_bwd(name, dy, w, x, cols, g, into, tail=None, after=()):
    s, n = dy.shape
    start, width = cols
    tm = min(TILE_M, s)
    pieces = _column_pieces(tm, start, width)
    n_p = len(pieces)
    tails = () if tail is None else (tail,)
    out_width = width if tail is None else into.shape[1] - start
    assert start % out_width == 0 and into.dtype == BF16

    def body(dy_ref, w_ref, *refs):
        g_ref = refs[n_p]
        dx_ref, dg_ref = refs[-2:]
        xv = refs[0][...] if n_p == 1 else jnp.concatenate([r[...] for r in refs[:n_p]], axis=1)
        dxn = lax.dot_general(dy_ref[...], w_ref[...], (NT, ((), ())), preferred_element_type=F32)
        dx, dgc = _rms_bwd(xv, g_ref[...], dxn)
        dx_ref[:, :width] = dx.astype(dx_ref.dtype)
        if tails:
            t = refs[n_p + 1][...]
            dx_ref[:, width:width + t.shape[1]] = t
            dx_ref[:, width + t.shape[1]:] = jnp.zeros((tm, out_width - width - t.shape[1]), dx_ref.dtype)
        _accumulate(dg_ref, _sublane_sum(dgc), pl.program_id(0))

    n_in = 3 + n_p + len(tails)
    return _call(
        body, name=name, grid=(s // tm,), after=after,
        in_specs=[_row_spec(tm, n), _fixed_spec(width, n), *pieces, _fixed_spec(1, width)]
                 + [_row_spec(tm, t.shape[1]) for t in tails] + [ANY],
        out_specs=[pl.BlockSpec((tm, out_width), lambda i: (i, start // out_width)), _fixed_spec(SUBLANE, width)],
        out_shape=[jax.ShapeDtypeStruct(into.shape, into.dtype), jax.ShapeDtypeStruct((SUBLANE, width), F32)],
        input_output_aliases={n_in: 0},
        compiler_params=_params("arbitrary"),
    )(dy, w, *[x] * n_p, g, *tails, into)


def _mid_fwd(x, y, g_post, g_pre, after=()):
    s, w = x.shape
    rows = min(ROWS, s)

    def body(x_ref, y_ref, gp_ref, gq_ref, x2_ref, h2_ref):
        yv = y_ref[...]
        x2 = x_ref[...] + yv * _rstd(yv) * gp_ref[...]
        x2_ref[...] = x2
        h2_ref[...] = (x2 * _rstd(x2) * gq_ref[...]).astype(h2_ref.dtype)

    return _call(
        body, name="mid_fwd", grid=(s // rows,), after=after,
        in_specs=[_row_spec(rows, w), _row_spec(rows, w), _fixed_spec(1, w), _fixed_spec(1, w)],
        out_specs=[_row_spec(rows, w), _row_spec(rows, w)],
        out_shape=[jax.ShapeDtypeStruct((s, w), F32), jax.ShapeDtypeStruct((s, w), BF16)],
        compiler_params=_params("parallel"),
    )(x, y, g_post, g_pre)


def _head(m, x2, tgt, g):
    s, w = m.shape
    rows = min(ROWS, s)

    def body(m_ref, x2_ref, t_ref, g_ref, dout_ref, dm_ref, dg_ref, loss_ref):
        mv = m_ref[...]
        gv = g_ref[...]
        out = x2_ref[...] + mv * _rstd(mv) * gv
        err = out - t_ref[...]
        dout = err * (1.0 / w)
        dout_ref[...] = dout
        dm, dgc = _rms_bwd(mv, gv, dout)
        dm_ref[...] = dm.astype(dm_ref.dtype)
        sq = err * err
        lanes = sq[:, 0:LANE]
        for j in range(1, w // LANE):
            lanes = lanes + sq[:, j * LANE:(j + 1) * LANE]
        step = pl.program_id(0)
        _accumulate(dg_ref, _sublane_sum(dgc), step)
        _accumulate(loss_ref, _sublane_sum(lanes) * (0.5 / w), step)

    return pl.pallas_call(
        body, name="head", grid=(s // rows,),
        in_specs=[_row_spec(rows, w), _row_spec(rows, w), _row_spec(rows, w), _fixed_spec(1, w)],
        out_specs=[_row_spec(rows, w), _row_spec(rows, w), _fixed_spec(SUBLANE, w), _fixed_spec(SUBLANE, LANE)],
        out_shape=[jax.ShapeDtypeStruct((s, w), F32), jax.ShapeDtypeStruct((s, w), BF16),
                   jax.ShapeDtypeStruct((SUBLANE, w), F32), jax.ShapeDtypeStruct((SUBLANE, LANE), F32)],
        compiler_params=_params("arbitrary"),
    )(m, x2, tgt, g)


def _mid_bwd(x2, y, d_out, d_h2, g_pre, g_post, after=()):
    s, w = x2.shape
    rows = min(ROWS, s)

    def body(x2_ref, y_ref, dout_ref, dh2_ref, gq_ref, gp_ref, dx2_ref, dy_ref, dgq_ref, dgp_ref):
        dx, dgq = _rms_bwd(x2_ref[...], gq_ref[...], dh2_ref[...])
        dx2 = dout_ref[...] + dx
        dx2_ref[...] = dx2
        dy, dgp = _rms_bwd(y_ref[...], gp_ref[...], dx2)
        dy_ref[...] = dy.astype(dy_ref.dtype)
        step = pl.program_id(0)
        _accumulate(dgq_ref, _sublane_sum(dgq), step)
        _accumulate(dgp_ref, _sublane_sum(dgp), step)

    return _call(
        body, name="mid_bwd", grid=(s // rows,), after=after,
        in_specs=[_row_spec(rows, w)] * 4 + [_fixed_spec(1, w)] * 2,
        out_specs=[_row_spec(rows, w), _row_spec(rows, w), _fixed_spec(SUBLANE, w), _fixed_spec(SUBLANE, w)],
        out_shape=[jax.ShapeDtypeStruct((s, w), F32), jax.ShapeDtypeStruct((s, w), BF16),
                   jax.ShapeDtypeStruct((SUBLANE, w), F32), jax.ShapeDtypeStruct((SUBLANE, w), F32)],
        compiler_params=_params("arbitrary"),
    )(x2, y, d_out, d_h2, g_pre, g_post)


def _first_bwd(x, g, d_h1, d_x2, after=()):
    s, w = x.shape
    rows = min(ROWS, s)

    def body(x_ref, g_ref, dh_ref, dx2_ref, dx_ref, dg_ref):
        dx, dgc = _rms_bwd(x_ref[...], g_ref[...], dh_ref[...])
        dx_ref[...] = dx2_ref[...] + dx
        _accumulate(dg_ref, _sublane_sum(dgc), pl.program_id(0))

    return _call(
        body, name="first_bwd", grid=(s // rows,), after=after,
        in_specs=[_row_spec(rows, w), _fixed_spec(1, w), _row_spec(rows, w), _row_spec(rows, w)],
        out_specs=[_row_spec(rows, w), _fixed_spec(SUBLANE, w)],
        out_shape=[jax.ShapeDtypeStruct((s, w), F32), jax.ShapeDtypeStruct((SUBLANE, w), F32)],
        compiler_params=_params("arbitrary"),
    )(x, g, d_h1, d_x2)


def _shift_down(v, k):
    t = lax.broadcasted_iota(jnp.int32, v.shape, 0)
    return jnp.where(t >= k, pltpu.roll(v, k, 0), 0.0)


def _shift_up(v, k):
    n = v.shape[0]
    t = lax.broadcasted_iota(jnp.int32, v.shape, 0)
    return jnp.where(t < n - k, pltpu.roll(v, n - k, 0), 0.0)


def _conv_core(u, b, c, w):
    z = c * u
    conv = w[0:1, :] * _shift_down(z, 2) + w[1:2, :] * _shift_down(z, 1) + w[2:3, :] * z
    return z, conv, b * conv


def _conv_fwd(proj, conv_w, g, n_groups, out_width, after=()):
    s = proj.shape[0]

    def body(u_ref, b_ref, c_ref, w_ref, g_ref, o_ref):
        _, _, yr = _conv_core(u_ref[...], b_ref[...], c_ref[...], w_ref[...])
        o_ref[...] = (yr * _rstd(yr) * g_ref[...]).astype(o_ref.dtype)

    col = lambda k: pl.BlockSpec((s, HEAD), lambda i: (0, k * n_groups + i))
    return _call(
        body, name="conv_fwd", grid=(n_groups,), after=after,
        in_specs=[col(0), col(1), col(2), pl.BlockSpec((3, HEAD), lambda i: (0, i)), pl.BlockSpec((1, HEAD), lambda i: (0, i))],
        out_specs=pl.BlockSpec((s, HEAD), lambda i: (0, i)),
        out_shape=jax.ShapeDtypeStruct((s, out_width), BF16),
        compiler_params=_params("parallel"),
    )(proj, proj, proj, conv_w, g)


def _conv_bwd(proj, d_mix, conv_w, g, n_groups, out_width):
    s = proj.shape[0]
    width = n_groups * HEAD

    def body(u_ref, b_ref, c_ref, dy_ref, w_ref, g_ref, dproj_ref, dg_ref, dw_ref, buf, sems):
        i = pl.program_id(0)
        slot = i % 2

        def copies(group, slot):
            return [pltpu.make_async_copy(buf.at[slot, k], dproj_ref.at[:, pl.ds(pl.multiple_of((k * n_groups + group) * HEAD, HEAD), HEAD)],
                                          sems.at[slot, k]) for k in range(3)]

        @pl.when(i >= 2)
        def _():
            for cp in copies(i - 2, slot):
                cp.wait()

        u, b, c, w = u_ref[...], b_ref[...], c_ref[...], w_ref[...]
        z, conv, yr = _conv_core(u, b, c, w)
        dyr, dgc = _rms_bwd(yr, g_ref[...], dy_ref[...])
        dconv = dyr * b
        dz = w[2:3, :] * dconv + w[1:2, :] * _shift_up(dconv, 1) + w[0:1, :] * _shift_up(dconv, 2)
        buf[slot, 0] = (dz * c).astype(buf.dtype)
        buf[slot, 1] = (dyr * conv).astype(buf.dtype)
        buf[slot, 2] = (dz * u).astype(buf.dtype)
        for cp in copies(i, slot):
            cp.start()
        dg_ref[...] = _sublane_sum(dgc)
        dw_ref[0] = _sublane_sum(dconv * _shift_down(z, 2))
        dw_ref[1] = _sublane_sum(dconv * _shift_down(z, 1))
        dw_ref[2] = _sublane_sum(dconv * z)

        @pl.when(i == n_groups - 1)
        def _():
            for back in range(min(2, n_groups)):
                for cp in copies(i - back, (n_groups - 1 - back) % 2):
                    cp.wait()

    col = lambda k: pl.BlockSpec((s, HEAD), lambda i: (0, k * n_groups + i))
    grp = pl.BlockSpec((s, HEAD), lambda i: (0, i))
    return pl.pallas_call(
        body, name="conv_bwd", grid=(n_groups,),
        in_specs=[col(0), col(1), col(2), grp, pl.BlockSpec((3, HEAD), lambda i: (0, i)), pl.BlockSpec((1, HEAD), lambda i: (0, i))],
        out_specs=[ANY, pl.BlockSpec((SUBLANE, HEAD), lambda i: (0, i)), pl.BlockSpec((3, SUBLANE, HEAD), lambda i: (0, 0, i))],
        out_shape=[jax.ShapeDtypeStruct((s, out_width), BF16),
                   jax.ShapeDtypeStruct((SUBLANE, width), F32), jax.ShapeDtypeStruct((3, SUBLANE, width), F32)],
        scratch_shapes=[pltpu.VMEM((2, 3, s, HEAD), BF16), pltpu.SemaphoreType.DMA((2, 3))],
        compiler_params=_params("arbitrary"),
    )(proj, proj, proj, d_mix, conv_w, g)


def _rope_tables(s, n_heads):
    pos = jnp.arange(s, dtype=F32)
    inv_freq = jnp.power(ROPE_THETA, -jnp.arange(0, ROPE, 2, dtype=F32) / ROPE)
    ang = pos[:, None] * inv_freq[None, :]
    cos, sin = jnp.cos(ang), jnp.sin(ang)
    cs = jnp.concatenate([cos, cos], axis=1)
    sn = jnp.concatenate([-sin, sin], axis=1)
    pad = jnp.zeros((s, LANE - ROPE), F32)
    return (jnp.tile(cs, (1, n_heads)), jnp.tile(sn, (1, n_heads)),
            jnp.concatenate([cs, pad], axis=1), jnp.concatenate([sn, pad], axis=1))


def _swap_halves(v):
    w = v.shape[1]
    lane = lax.broadcasted_iota(jnp.int32, v.shape, 1)
    first = (lane % ROPE) < (ROPE // 2)
    return jnp.where(first, pltpu.roll(v, w - ROPE // 2, 1), pltpu.roll(v, ROPE // 2, 1))


def _pack_heads(q, kv, proj, kr_col, tables, n_heads, after=()):
    s = q.shape[0]
    rows = min(ROWS, s)
    cq, sq, ck, sk = tables
    wq = n_heads * ROPE

    def body(q_ref, kv_ref, kr_ref, cq_ref, sq_ref, ck_ref, sk_ref, qo_ref, ko_ref, vo_ref):
        qr = q_ref[:, n_heads * HEAD:]
        qr = qr * cq_ref[...] + _swap_halves(qr) * sq_ref[...]
        krv = kr_ref[...]
        krv = krv * ck_ref[...] + _swap_halves(krv) * sk_ref[...]
        for h in range(n_heads):
            qo_ref[h] = jnp.concatenate([q_ref[:, h * HEAD:(h + 1) * HEAD], qr[:, h * ROPE:(h + 1) * ROPE]], axis=1).astype(BF16)
            ko_ref[h] = jnp.concatenate([kv_ref[:, 2 * h * HEAD:(2 * h + 1) * HEAD], krv[:, :ROPE]], axis=1).astype(BF16)
            vo_ref[h] = kv_ref[:, (2 * h + 1) * HEAD:(2 * h + 2) * HEAD].astype(BF16)

    hs = lambda w: pl.BlockSpec((n_heads, rows, w), lambda i: (0, i, 0))
    return _call(
        body, name="pack_heads", grid=(s // rows,), after=after,
        in_specs=[_row_spec(rows, q.shape[1]), _row_spec(rows, kv.shape[1]), pl.BlockSpec((rows, LANE), lambda i: (i, kr_col // LANE)),
                  _row_spec(rows, wq), _row_spec(rows, wq), _row_spec(rows, LANE), _row_spec(rows, LANE)],
        out_specs=[hs(QK), hs(QK), hs(HEAD)],
        out_shape=[jax.ShapeDtypeStruct((n_heads, s, QK), BF16), jax.ShapeDtypeStruct((n_heads, s, QK), BF16),
                   jax.ShapeDtypeStruct((n_heads, s, HEAD), BF16)],
        compiler_params=_params("parallel"),
    )(q, kv, proj, cq, sq, ck, sk)


def _unpack_heads(dq, dk, dv, tables, n_heads):
    s = dq.shape[1]
    rows = min(ROWS, s)
    cq, sq, ck, sk = tables
    wq = n_heads * ROPE

    def body(dq_ref, dk_ref, dv_ref, cq_ref, sq_ref, ck_ref, sk_ref, qo_ref, kvo_ref, kro_ref):
        dqr = jnp.concatenate([dq_ref[h][:, HEAD:] for h in range(n_heads)], axis=1)
        dqr = dqr * cq_ref[...] - _swap_halves(dqr) * sq_ref[...]
        dkr = dk_ref[0][:, HEAD:]
        for h in range(1, n_heads):
            dkr = dkr + dk_ref[h][:, HEAD:]
        dkr = jnp.concatenate([dkr, jnp.zeros((rows, LANE - ROPE), F32)], axis=1)
        dkr = dkr * ck_ref[...] - _swap_halves(dkr) * sk_ref[...]
        kro_ref[...] = dkr.astype(kro_ref.dtype)
        qo_ref[:, n_heads * HEAD:] = dqr.astype(qo_ref.dtype)
        for h in range(n_heads):
            qo_ref[:, h * HEAD:(h + 1) * HEAD] = dq_ref[h][:, :HEAD].astype(qo_ref.dtype)
            kvo_ref[:, 2 * h * HEAD:(2 * h + 1) * HEAD] = dk_ref[h][:, :HEAD].astype(kvo_ref.dtype)
            kvo_ref[:, (2 * h + 1) * HEAD:(2 * h + 2) * HEAD] = dv_ref[h].astype(kvo_ref.dtype)

    hs = lambda w: pl.BlockSpec((n_heads, rows, w), lambda i: (0, i, 0))
    return pl.pallas_call(
        body, name="unpack_heads", grid=(s // rows,),
        in_specs=[hs(QK), hs(QK), hs(HEAD), _row_spec(rows, wq), _row_spec(rows, wq), _row_spec(rows, LANE), _row_spec(rows, LANE)],
        out_specs=[_row_spec(rows, n_heads * QK), _row_spec(rows, 2 * n_heads * HEAD), _row_spec(rows, LANE)],
        out_shape=[jax.ShapeDtypeStruct((s, n_heads * QK), BF16), jax.ShapeDtypeStruct((s, 2 * n_heads * HEAD), BF16),
                   jax.ShapeDtypeStruct((s, LANE), BF16)],
        compiler_params=_params("parallel"),
    )(dq, dk, dv, cq, sq, ck, sk)


TQ = 256


LOG2_E = 1.4426950408889634


def _softmax_parts(q, k):
    tq, n_keys = q.shape[0], k.shape[0]
    sc = lax.dot_general(q, k, (NT, ((), ())), preferred_element_type=F32) * (QK ** -0.5 * LOG2_E)
    row = lax.broadcasted_iota(jnp.int32, (tq, tq), 0)
    col = lax.broadcasted_iota(jnp.int32, (tq, tq), 1)
    own = jnp.where(col // CHUNK <= row // CHUNK, sc[:, n_keys - tq:], NEG_INF)
    sc = own if n_keys == tq else jnp.concatenate([sc[:, :n_keys - tq], own], axis=1)
    e = jnp.exp2(sc - jnp.max(sc, axis=-1, keepdims=True))
    return e, 1.0 / jnp.sum(e, axis=-1, keepdims=True)


def _prob_columns(c, tq):
    return pl.ds(tq * (c * (c + 1) // 2), (c + 1) * tq)


def _attn_fwd(q, k, v, g, mix, col0, first, between):
    n_heads, s, _ = q.shape
    tq = min(TQ, s)
    assert tq % CHUNK == 0 and s % tq == 0
    n_blocks = s // tq
    p_cols = tq * (n_blocks * (n_blocks + 1) // 2)
    out_shape = [jax.ShapeDtypeStruct((n_heads, s, HEAD), F32), jax.ShapeDtypeStruct((n_heads, tq, p_cols), BF16),
                 jax.ShapeDtypeStruct(mix.shape, mix.dtype)]
    done, after = (mix,), ()
    for part, (h0, h1) in enumerate(((0, first), (first, n_heads))):

        def body(q_ref, k_ref, v_ref, g_ref, *rest):
            o_ref, p_ref, y_ref = rest[-3:]
            for c in range(n_blocks):
                rows, n_keys = pl.ds(c * tq, tq), (c + 1) * tq
                e, inv = _softmax_parts(q_ref[rows, :], k_ref[0:n_keys, :])
                p = (e * inv).astype(BF16)
                p_ref[:, _prob_columns(c, tq)] = p
                o = jnp.dot(p, v_ref[0:n_keys, :], preferred_element_type=F32)
                o_ref[rows, :] = o
                y_ref[rows, :] = (o * _rstd(o) * g_ref[...]).astype(y_ref.dtype)

        head = lambda w, h0=h0: pl.BlockSpec((None, s, w), lambda h: (h0 + h, 0, 0))
        n_done = len(done)
        done = _call(
            body, name=f"attn_fwd_{part}", grid=(h1 - h0,), after=after,
            in_specs=[head(QK), head(QK), head(HEAD), pl.BlockSpec((1, HEAD), lambda h, h0=h0: (0, h0 + h))] + [ANY] * n_done,
            out_specs=[head(HEAD), pl.BlockSpec((None, tq, p_cols), lambda h, h0=h0: (h0 + h, 0, 0)),
                       pl.BlockSpec((s, HEAD), lambda h, h0=h0: (0, col0 // HEAD + h0 + h))],
            out_shape=out_shape,
            input_output_aliases={4 + i: 3 - n_done + i for i in range(n_done)},
            compiler_params=_params("parallel"),
        )(q, k, v, g, *done)
        after = between(done) if part == 0 else ()
    return done


def _attn_bwd(q, k, v, o, probs, d_mix, g, col0, after=()):
    n_heads, s, _ = q.shape
    tq = probs.shape[1]

    def body(q_ref, k_ref, v_ref, o_ref, p_ref, dy_ref, g_ref, dq_ref, dk_ref, dv_ref, dg_ref):
        dg = None
        for c in reversed(range(s // tq)):
            rows, n_keys = pl.ds(c * tq, tq), (c + 1) * tq
            o = o_ref[rows, :]
            do, dgc = _rms_bwd(o, g_ref[...], dy_ref[rows, :])
            do = do.astype(BF16)
            dg = _sublane_sum(dgc) if dg is None else dg + _sublane_sum(dgc)
            p = p_ref[:, _prob_columns(c, tq)]
            dp = lax.dot_general(do, v_ref[0:n_keys, :], (NT, ((), ())), preferred_element_type=F32)
            ds = (p.astype(F32) * (dp - jnp.sum(do.astype(F32) * o, axis=-1, keepdims=True))).astype(BF16)
            dq_ref[rows, :] = jnp.dot(ds, k_ref[0:n_keys, :], preferred_element_type=F32) * (QK ** -0.5)
            dk = lax.dot_general(ds, q_ref[rows, :], (TN, ((), ())), preferred_element_type=F32)
            dv = lax.dot_general(p, do, (TN, ((), ())), preferred_element_type=F32)
            if n_keys == s:
                dk_ref[...] = dk
                dv_ref[...] = dv
            else:
                dk_ref[0:n_keys, :] += dk
                dv_ref[0:n_keys, :] += dv
        dk_ref[...] = dk_ref[...] * (QK ** -0.5)
        dg_ref[...] = dg

    c0 = col0 // HEAD
    head = lambda w: pl.BlockSpec((None, s, w), lambda h, *_: (h, 0, 0))
    in_specs = [head(QK), head(QK), head(HEAD), head(HEAD), pl.BlockSpec((None, tq, probs.shape[2]), lambda h, *_: (h, 0, 0)),
                pl.BlockSpec((s, HEAD), lambda h, *_: (0, c0 + h)), pl.BlockSpec((1, HEAD), lambda h, *_: (0, h))]
    out_specs = [head(QK), head(QK), head(HEAD), pl.BlockSpec((SUBLANE, HEAD), lambda h, *_: (0, h))]
    out_shape = [jax.ShapeDtypeStruct((n_heads, s, QK), F32), jax.ShapeDtypeStruct((n_heads, s, QK), F32),
                 jax.ShapeDtypeStruct((n_heads, s, HEAD), F32), jax.ShapeDtypeStruct((SUBLANE, n_heads * HEAD), F32)]
    return _call(body, name="attn_bwd", grid=(n_heads,), after=after, in_specs=in_specs, out_specs=out_specs,
                 out_shape=out_shape, compiler_params=_params("parallel"))(q, k, v, o, probs, d_mix, g)


TILE_M = 1024
TILE_N = 1024


def _up_fwd(h2, w_up, between):
    s, d = h2.shape
    nb, _, fb = w_up.shape
    tm = min(TILE_M, s)
    done, after = (), ()
    for tile in range(s // tm):

        def body(h_ref, w_ref, *rest):
            a_ref, r_ref = rest[-2:]
            r = jnp.maximum(jnp.dot(h_ref[...], w_ref[...], preferred_element_type=F32), 0.0)
            a_ref[...] = (r * r).astype(a_ref.dtype)
            r_ref[...] = r.astype(r_ref.dtype)

        blk = pl.BlockSpec((tm, fb), lambda j, tile=tile: (tile, j))
        done = _call(
            body, name=f"up_fwd_{tile}", grid=(nb,), after=after,
            in_specs=[pl.BlockSpec((tm, d), lambda j, tile=tile: (tile, 0)), pl.BlockSpec((None, d, fb), lambda j: (j, 0, 0))]
                     + [ANY] * len(done),
            out_specs=[blk, blk], out_shape=[jax.ShapeDtypeStruct((s, nb * fb), BF16)] * 2,
            input_output_aliases={2 + i: i for i in range(len(done))},
            compiler_params=_params("parallel"),
        )(h2, w_up, *done)
        after = between(done) if tile == 0 else ()
    return done


def _down_fwd(a, w_down):
    s, f = a.shape
    d = w_down.shape[1]
    tm, tn, tk = min(TILE_M,s), min(TILE_N,d), 2048
    nk = f // tk
    return _matmul("down_fwd", a, w_down, grid=(s // tm, d // tn, nk),
                   a_spec=pl.BlockSpec((tm, tk), lambda i, j, k: (i, k)),
                   b_spec=pl.BlockSpec((tk, tn), lambda i, j, k: (k, j)),
                   out_shape=jax.ShapeDtypeStruct((s, d), F32),
                   out_specs=pl.BlockSpec((tm, tn), lambda i, j, k: (i, j)),
                   contract=NN, nk=nk, acc_shape=(tm, tn))


def _down_bwd_act(d_m, w_down, r, after=()):
    s, d = d_m.shape
    f = w_down.shape[0]
    tm, tn = min(TILE_M,s), min(TILE_N,f)
    blk = pl.BlockSpec((tm, tn), lambda i, j: (i, j))
    return _matmul("down_bwd_act", d_m, w_down, grid=(s // tm, f // tn), after=after,
                   a_spec=pl.BlockSpec((tm, d), lambda i, j: (i, 0)),
                   b_spec=pl.BlockSpec((tn, d), lambda i, j: (j, 0)),
                   out_shape=jax.ShapeDtypeStruct((s, f), BF16), out_specs=blk, contract=NT,
                   extras=(r,), extra_specs=(blk,),
                   epilogue=lambda acc, rv: (acc * (2.0 * rv.astype(F32)),))


def _up_bwd_act(d_up, w_up, after=()):
    s, _ = d_up.shape
    nb, d, fb = w_up.shape
    tm, tn = min(TILE_M, s), min(TILE_N,d)
    pair = 2
    n_after = len(after)

    def body(a_ref, w_ref, *rest):
        o_ref, acc_ref = rest[n_after:]
        k = pl.program_id(2)
        p = None
        for t in range(pair):
            term = lax.dot_general(a_ref[:, t * fb:(t + 1) * fb], w_ref[t], (NT, ((), ())), preferred_element_type=F32)
            p = term if p is None else p + term
        _accumulate(acc_ref, p, k)

        @pl.when(k == nb // pair - 1)
        def _():
            o_ref[...] = acc_ref[...]

    return pl.pallas_call(
        body, name="up_bwd_act", grid=(s // tm, d // tn, nb // pair),
        in_specs=[pl.BlockSpec((tm, pair * fb), lambda i, j, k: (i, k)),
                  pl.BlockSpec((pair, tn, fb), lambda i, j, k: (k, j, 0))] + [ANY] * n_after,
        out_specs=pl.BlockSpec((tm, tn), lambda i, j, k: (i, j)),
        out_shape=jax.ShapeDtypeStruct((s, d), F32),
        scratch_shapes=[pltpu.VMEM((tm, tn), F32)],
        compiler_params=_params("parallel", "parallel", "arbitrary"),
    )(d_up, w_up, *after)


def _half_grad(name, a, b, core, home, received, after, *, grid, a_block, a_map, b_block, b_map, o_block, o_map, out_shape):
    n_after = len(after)
    pick = (lambda ref: ref[0]) if home else (lambda ref: 1 - ref[0])

    def body(core_ref, a_ref, b_ref, *rest):
        acc = lax.dot_general(a_ref[...], b_ref[...], (TN, ((), ())), preferred_element_type=F32)
        if received is not None:
            acc = acc + rest[0][...].astype(F32)
        rest[-1][...] = acc.astype(rest[-1].dtype)

    wrap = lambda fn: (lambda i, j, core_ref: fn(i, j, pick(core_ref)))
    o_spec = pl.BlockSpec(o_block, wrap(o_map))
    extra = [] if received is None else [o_spec]
    operands = [] if received is None else [received]
    return pl.pallas_call(
        body, name=name,
        grid_spec=pltpu.PrefetchScalarGridSpec(
            num_scalar_prefetch=1, grid=grid,
            in_specs=[pl.BlockSpec(a_block, wrap(a_map)), pl.BlockSpec(b_block, wrap(b_map))] + extra + [ANY] * n_after,
            out_specs=o_spec),
        out_shape=out_shape,
        compiler_params=_params("parallel", "parallel"),
    )(core, a, b, *operands, *after)


def _down_half_grad(name, a, d_m, core, home, received=None, after=()):
    s, f = a.shape
    d = d_m.shape[1]
    r = f // N_DEV
    tn = min(TILE_N, d)
    return _half_grad(name, a, d_m, core, home, received, after, grid=(N_CHIP, d // tn),
                      a_block=(s, r), a_map=lambda k, j, p: (0, 2 * k + p),
                      b_block=(s, tn), b_map=lambda k, j, p: (0, j),
                      o_block=(None, r, tn), o_map=lambda k, j, p: (k, 0, j),
                      out_shape=jax.ShapeDtypeStruct((N_CHIP, r, d), BF16))


def _up_half_grad(name, h2, d_up, core, home, received=None, after=()):
    s, d = h2.shape
    fb = d_up.shape[1] // N_DEV
    tm = min(TILE_M, d)
    return _half_grad(name, h2, d_up, core, home, received, after, grid=(d // tm, N_CHIP),
                      a_block=(s, tm), a_map=lambda i, k, p: (0, i),
                      b_block=(s, fb), b_map=lambda i, k, p: (0, 2 * k + p),
                      o_block=(None, tm, fb), o_map=lambda i, k, p: (k, i, 0),
                      out_shape=jax.ShapeDtypeStruct((N_CHIP, d, fb), BF16))


MXU_WIDTH = 256


def _in_pad(in_width):
    return -(-in_width // MXU_WIDTH) * MXU_WIDTH


def _join_col_shards(name, blocks, own, device, pieces=None):
    n, r, w = blocks.shape
    rows = min(ROWS, r)
    pieces = pieces or [(j, 0, w) for j in range(n)]
    used = sum(b - a for _, a, b in pieces)
    width = _in_pad(used)

    def body(dev_ref, x_ref, own_ref, o_ref):
        block = lambda j: jnp.where(dev_ref[0] == j, own_ref[...], x_ref[j])
        cols = [block(j)[:, a:b] for j, a, b in pieces]
        tail = [jnp.zeros((rows, width - used), o_ref.dtype)] if width > used else []
        o_ref[...] = jnp.concatenate(cols + tail, axis=1)

    return pl.pallas_call(
        body, name=name,
        grid_spec=pltpu.PrefetchScalarGridSpec(
            num_scalar_prefetch=1, grid=(r // rows,),
            in_specs=[pl.BlockSpec((n, rows, w), lambda i, dev: (0, i, 0)), pl.BlockSpec((rows, w), lambda i, dev: (i, 0))],
            out_specs=pl.BlockSpec((rows, width), lambda i, dev: (i, 0))),
        out_shape=jax.ShapeDtypeStruct((r, width), blocks.dtype),
        compiler_params=_params("parallel"),
    )(device, blocks, own)


def _unpermute_q_rows(wt, n_heads):
    r = wt.shape[1]
    nope = wt[:n_heads * HEAD].reshape(n_heads, HEAD, r)
    rope = wt[n_heads * HEAD:].reshape(n_heads, ROPE, r)
    return jnp.concatenate([nope, rope], axis=1).reshape(n_heads * QK, r)


def _local_step(x, tgt, gains, weights, grads, first_after=()):
    pre_mix_g, q_norm_g, kv_norm_g, conv_out_g, attn_out_g, post_mix_g, pre_mlp_g, post_mlp_g = gains
    s, d = x.shape
    conv_width = conv_out_g.shape[1]
    n_groups = conv_width // HEAD
    r_q, r_kv = q_norm_g.shape[1], kv_norm_g.shape[1]
    n_heads = attn_out_g.shape[1] // HEAD
    c_q0 = 3 * conv_width
    c_kv0 = c_q0 + r_q
    c_kr0 = c_kv0 + r_kv
    in_pad = _in_pad(c_kr0 + ROPE)
    tn_in = _fit(in_pad, 6 * MXU_WIDTH)
    tables = _rope_tables(s, n_heads)

    h1 = _rms_fwd("pre_mix_norm", x, pre_mix_g, after=first_after)
    weights.forward(0, (h1,))
    weights.relay(0, tables)
    w_in_p, conv_w = weights.ready(0, ())
    proj = _mm_nn("in_proj", h1, w_in_p, F32, TILE_M, tn_in)
    y_conv = _conv_fwd(proj, conv_w, conv_out_g, n_groups, conv_width + n_heads * HEAD, after=weights.forward(1, (proj,)))
    w_uq_p, w_ukv, w_o = weights.ready(1, (y_conv,))
    qn, q = _norm_up("q_up", proj, (c_q0, r_q), q_norm_g, w_uq_p)
    kvn, kv = _norm_up("kv_up", proj, (c_kv0, r_kv), kv_norm_g, w_ukv)
    qh, kh, vh = _pack_heads(q, kv, proj, c_kr0, tables, n_heads)
    o, probs, mix = _attn_fwd(qh, kh, vh, attn_out_g, y_conv, conv_width, n_heads // 4,
                              lambda done: weights.forward(2, tuple(done)))
    y = _mm_nn("out_proj", mix, w_o, F32, TILE_M, TILE_N, after=weights.start(3, (mix,)))
    x2, h2 = _mid_fwd(x, y, post_mix_g, pre_mlp_g, after=weights.relay(2, (y,)))
    (w_up,) = weights.ready(2, (h2,))
    a, r = _up_fwd(h2, w_up, lambda done: weights.forward(3, tuple(done)))
    weights.relay(3, (a,))
    (w_down,) = weights.ready(3, ())
    m = _down_fwd(a, w_down)

    d_out, d_m, dg_post_mlp, loss_part = _head(m, x2, tgt, post_mlp_g)
    core = grads.core
    away = _down_half_grad("down_bwd_w_away", a, d_m, core, home=False)
    d_up = _down_bwd_act(d_m, w_down, r, after=grads.send_away(0, away))
    sums = _down_half_grad("down_bwd_w_home", a, d_m, core, home=True, received=grads.received(0, (d_up,)))
    away = _up_half_grad("up_bwd_w_away", h2, d_up, core, home=False, after=grads.send_sums(0, (sums,)))
    d_h2 = _up_bwd_act(d_up, w_up, after=grads.send_away(1, away))
    sums = _up_half_grad("up_bwd_w_home", h2, d_up, core, home=True, received=grads.received(1, (d_h2,)))
    d_x2, d_y, dg_pre_mlp, dg_post_mix = _mid_bwd(x2, y, d_out, d_h2, pre_mlp_g, post_mix_g, after=grads.send_sums(1, (sums,)))
    d_mix = _mm_nt("out_proj_bwd_act", d_y, w_o, F32, TILE_M, TILE_N)
    gw_o = _mm_tn("out_proj_bwd_w", mix, d_y, BF16, TILE_M, TILE_N)
    dqh, dkh, dvh, dg_attn = _attn_bwd(qh, kh, vh, o, probs, d_mix, attn_out_g, conv_width)
    d_q, d_kv, d_kr = _unpack_heads(dqh, dkh, dvh, tables, n_heads)
    gw_uq_t = _mm_tn("q_up_bwd_w", d_q, qn, F32, TILE_M, TILE_N)
    gw_ukv = _mm_tn("kv_up_bwd_w", kvn, d_kv, BF16, TILE_M, TILE_N)
    d_proj, dg_conv, dw_conv = _conv_bwd(proj, d_mix, conv_w, conv_out_g, n_groups, in_pad)
    d_proj, dg_q = _up_norm_bwd("q_up_bwd_act", d_q, w_uq_p, proj, (c_q0, r_q), q_norm_g, d_proj,
                                after=grads.full(2, (gw_o, gw_uq_t, gw_ukv)))
    d_proj, dg_kv = _up_norm_bwd("kv_up_bwd_act", d_kv, w_ukv, proj, (c_kv0, r_kv), kv_norm_g, d_proj, tail=d_kr)
    gw_in_t = _mm_tn("in_proj_bwd_w", d_proj, h1, F32, tn_in, TILE_N)
    d_h1 = _mm_nt("in_proj_bwd_act", d_proj, w_in_p, F32, TILE_M, TILE_N, after=grads.send_away(3, gw_in_t))
    grad_x, dg_pre_mix = _first_bwd(x, pre_mix_g, d_h1, d_x2, after=grads.full(3, (gw_in_t,), received=(d_h1,)))

    small = [dg_pre_mix, dg_q, dg_kv, dg_conv, dg_attn, dg_post_mix, dg_pre_mlp, dg_post_mlp,
             dw_conv[0], dw_conv[1], dw_conv[2], loss_part]
    return grad_x, jnp.concatenate(small, axis=1)


HBM = pl.BlockSpec(memory_space=pltpu.HBM)
SEM = pl.BlockSpec(memory_space=pltpu.SEMAPHORE)
IN_VMEM = pl.BlockSpec(memory_space=pltpu.VMEM)
SPLIT = pltpu.CompilerParams(has_side_effects=pltpu.SideEffectType.DATAFLOW_SIDE_EFFECTING)


def _in_hbm(a):
    return pltpu.with_memory_space_constraint(a, pltpu.HBM)


def _hbm_like(a):
    return pltpu.HBM(a.shape, a.dtype)


def _place():
    x, y, c = lax.axis_index("x"), lax.axis_index("y"), lax.axis_index("c")
    other_chips = [(1 - x, y), (x, 1 - y), (1 - x, 1 - y)]
    return x, y, c, other_chips


def _block(px, py, pc):
    return 4 * px + 2 * py + pc


def _await(block, sem):
    pltpu.make_async_copy(block, block, sem).wait()


def _relay_route(x, y, c):
    came_from = ((1 - x) * (1 - c) + x * c, y * (1 - c) + (1 - y) * c)
    goes_to = (x * (1 - c) + (1 - x) * c, (1 - y) * (1 - c) + y * c)
    return came_from, goes_to


def _gather_start(name, shards, groups, relayed=(), after=(), lands=None):
    n, ng = len(shards), len(groups)
    if lands is None:
        lands = [lax.empty((N_DEV, *a.shape), a.dtype) for a in shards]

    def body(*refs):
        src, land = refs[:n], refs[n:2 * n]
        sems, token = refs[2 * n + len(after):2 * n + len(after) + 2 * ng], refs[-1]
        x, y, c, chips = _place()
        targets = [(x, y, 1 - c)] + [(*chip, c) for chip in chips]
        for gi, group in enumerate(groups):
            for i, w in enumerate(group):
                for k, to in enumerate(targets[:3] if gi in relayed else targets):
                    pltpu.make_async_remote_copy(
                        src_ref=src[w], dst_ref=land[w].at[_block(x, y, c)],
                        send_sem=sems[2 * gi].at[4 * i + k], recv_sem=sems[2 * gi + 1].at[4 * i + k],
                        device_id=to, device_id_type=MESH).start()
        token[...] = jnp.zeros_like(token)

    sem_shapes = [pltpu.SemaphoreType.DMA((4 * len(g),)) for g in groups for _ in range(2)]
    out = pl.pallas_call(
        body, name=name,
        in_specs=[HBM] * (2 * n) + [ANY] * len(after),
        out_specs=[SEM] * (2 * ng) + [HBM] * (2 * n) + [IN_VMEM],
        out_shape=sem_shapes + [_hbm_like(a) for a in shards] + [_hbm_like(a) for a in lands]
        + [jax.ShapeDtypeStruct((SUBLANE, LANE), F32)],
        input_output_aliases={i: 2 * ng + i for i in range(2 * n)},
        compiler_params=SPLIT,
    )(*[_in_hbm(a) for a in shards], *[_in_hbm(a) for a in lands], *after)
    sems = [(out[2 * gi], out[2 * gi + 1]) for gi in range(ng)]
    return sems, out[2 * ng:2 * ng + n], out[2 * ng + n:2 * ng + 2 * n], out[-1]


def _gather_forward(name, shards, lands, send1, recv1, after, relayed=False):
    n = len(lands)

    def body(*refs):
        src, land = refs[:n], refs[n:2 * n]
        s1, r1 = refs[2 * n], refs[2 * n + 1]
        s2, r2 = refs[2 * n + 2 + len(after)], refs[2 * n + 3 + len(after)]
        x, y, c, chips = _place()
        me, sibling = (x, y, c), (x, y, 1 - c)
        for j, chip in enumerate(chips[:2] if relayed else chips):
            for i in range(n):
                blk = land[i].at[_block(*chip, c)]
                pltpu.make_async_remote_copy(src_ref=blk, dst_ref=blk, send_sem=s1.at[4 * i + 1 + j], recv_sem=r1.at[4 * i + 1 + j],
                                             device_id=me, device_id_type=MESH).wait_recv()
                pltpu.make_async_remote_copy(src_ref=blk, dst_ref=blk, send_sem=s2.at[3 * i + j], recv_sem=r2.at[3 * i + j],
                                             device_id=sibling, device_id_type=MESH).start()
        if relayed:
            came_from, goes_to = _relay_route(x, y, c)
            for i in range(n):
                blk = land[i].at[_block(*came_from, c)]
                pltpu.make_async_remote_copy(src_ref=blk, dst_ref=blk, send_sem=s2.at[3 * i + 2], recv_sem=r2.at[3 * i + 2],
                                             device_id=(*goes_to, c), device_id_type=MESH).start()
        for i in range(n):
            blk = land[i].at[_block(x, y, 1 - c)]
            pltpu.make_async_remote_copy(src_ref=blk, dst_ref=blk, send_sem=s1.at[4 * i], recv_sem=r1.at[4 * i],
                                         device_id=me, device_id_type=MESH).wait_recv()
            for k in range(3 if relayed else 4):
                pltpu.make_async_remote_copy(src_ref=src[i], dst_ref=land[i].at[_block(x, y, c)], send_sem=s1.at[4 * i + k],
                                             recv_sem=r1.at[4 * i + k], device_id=sibling, device_id_type=MESH).wait_send()

    sem = pltpu.SemaphoreType.DMA((3 * n,))
    out = pl.pallas_call(
        body, name=name,
        in_specs=[HBM] * (2 * n) + [SEM, SEM] + [ANY] * len(after),
        out_specs=[SEM, SEM] + [HBM] * n,
        out_shape=[sem, sem] + [_hbm_like(a) for a in lands],
        input_output_aliases={n + i: 2 + i for i in range(n)},
        compiler_params=SPLIT,
    )(*shards, *lands, send1, recv1, *after)
    return (out[0], out[1]), out[2:]


def _gather_relay_forward(name, lands, send2, recv2, after):
    n = len(lands)

    def body(*refs):
        land, s2, r2 = refs[:n], refs[n], refs[n + 1]
        s3, r3 = refs[n + 2 + len(after)], refs[n + 3 + len(after)]
        x, y, c, _ = _place()
        me, sibling = (x, y, c), (x, y, 1 - c)
        came_from, _ = _relay_route(x, y, c)
        for i in range(n):
            blk = land[i].at[_block(1 - x, 1 - y, c)]
            pltpu.make_async_remote_copy(src_ref=blk, dst_ref=blk, send_sem=s2.at[3 * i + 2], recv_sem=r2.at[3 * i + 2],
                                         device_id=me, device_id_type=MESH).wait_recv()
            pltpu.make_async_remote_copy(src_ref=blk, dst_ref=blk, send_sem=s3.at[i], recv_sem=r3.at[i],
                                         device_id=sibling, device_id_type=MESH).start()
            sent = land[i].at[_block(*came_from, c)]
            pltpu.make_async_remote_copy(src_ref=sent, dst_ref=sent, send_sem=s2.at[3 * i + 2], recv_sem=r2.at[3 * i + 2],
                                         device_id=me, device_id_type=MESH).wait_send()

    sem = pltpu.SemaphoreType.DMA((n,))
    out = pl.pallas_call(
        body, name=name,
        in_specs=[HBM] * n + [SEM, SEM] + [ANY] * len(after),
        out_specs=[SEM, SEM] + [HBM] * n,
        out_shape=[sem, sem] + [_hbm_like(a) for a in lands],
        input_output_aliases={i: 2 + i for i in range(n)},
        compiler_params=SPLIT,
    )(*lands, send2, recv2, *after)
    return (out[0], out[1]), out[2:]


def _gather_wait(name, lands, send2, recv2, after, relay_sems=None):
    n = len(lands)
    n_sems = 2 if relay_sems is None else 4

    def body(*refs):
        land, s2, r2 = refs[:n], refs[n], refs[n + 1]
        for i in range(n):
            for j in range(3 if relay_sems is None else 2):
                _await(land[i].at[0], r2.at[3 * i + j])
                _await(land[i].at[0], s2.at[3 * i + j])
            if relay_sems is not None:
                _await(land[i].at[0], refs[n + 3].at[i])
                _await(land[i].at[0], refs[n + 2].at[i])

    return pl.pallas_call(
        body, name=name,
        in_specs=[HBM] * n + [SEM] * n_sems + [ANY] * len(after), out_specs=[HBM] * n, out_shape=[_hbm_like(a) for a in lands],
        input_output_aliases={i: i for i in range(n)},
        compiler_params=SPLIT,
    )(*lands, send2, recv2, *(relay_sems or ()), *after)


def _pair_exchange(name, grads, shard_rows):
    n = len(grads)
    shapes = [(g.shape[1:] if r is None else (r, g.shape[1])) for g, r in zip(grads, shard_rows)]

    def body(*refs):
        ins, recv = refs[:n], refs[n:2 * n]
        send_sems, recv_sems = refs[2 * n:]
        x, y, c, _ = _place()
        sends = []
        for w in range(n):
            for k in range(N_CHIP):
                j, r = 2 * k + 1 - c, shard_rows[w]
                src = ins[w].at[j] if r is None else ins[w].at[pl.ds(pl.multiple_of(j * r, SUBLANE), r), :]
                sends.append(pltpu.make_async_remote_copy(
                    src_ref=src, dst_ref=recv[w].at[k],
                    send_sem=send_sems.at[w, k], recv_sem=recv_sems.at[w, k],
                    device_id=(x, y, 1 - c), device_id_type=MESH))
        for cp in sends:
            cp.start()
        for cp in sends:
            cp.wait()

    return pl.pallas_call(
        body, name=name,
        in_specs=[ANY] * n, out_specs=[ANY] * n,
        out_shape=[jax.ShapeDtypeStruct((N_CHIP, *shape), g.dtype) for g, shape in zip(grads, shapes)],
        scratch_shapes=[pltpu.SemaphoreType.DMA((n, N_CHIP))] * 2,
    )(*grads)


def _pair_sum_rows(name, grad, received, core):
    _, r, c = received.shape
    tc = _fit(c, 512)

    def body(core_ref, a_ref, b_ref, o_ref):
        o_ref[...] = (a_ref[...] + b_ref[...]).astype(o_ref.dtype)

    spec = pl.BlockSpec((None, r, tc), lambda k, i, core_ref: (k, 0, i))
    return pl.pallas_call(
        body, name=name,
        grid_spec=pltpu.PrefetchScalarGridSpec(
            num_scalar_prefetch=1, grid=(N_CHIP, c // tc),
            in_specs=[pl.BlockSpec((r, tc), lambda k, i, core_ref: (2 * k + core_ref[0], i)), spec],
            out_specs=spec),
        out_shape=jax.ShapeDtypeStruct(received.shape, BF16),
        compiler_params=_params("parallel", "parallel"),
    )(core, grad, received)


def _pair_sum(name, grad, received, core):
    _, r, c = received.shape
    rows = min(ROWS, r)
    assert r % rows == 0

    def body(core_ref, a_ref, b_ref, o_ref):
        o_ref[...] = (a_ref[...].astype(F32) + b_ref[...].astype(F32)).astype(o_ref.dtype)

    spec = pl.BlockSpec((None, rows, c), lambda k, i, core_ref: (k, i, 0))
    return pl.pallas_call(
        body, name=name,
        grid_spec=pltpu.PrefetchScalarGridSpec(
            num_scalar_prefetch=1, grid=(N_CHIP, r // rows),
            in_specs=[pl.BlockSpec((None, None, rows, c), lambda k, i, core_ref: (k, core_ref[0], i, 0)), spec],
            out_specs=spec),
        out_shape=jax.ShapeDtypeStruct(received.shape, received.dtype),
        compiler_params=_params("parallel", "parallel"),
    )(core, grad.reshape(N_CHIP, 2, r, c), received)


def _away_shard(src, k, c, shard_rows):
    if shard_rows is None:
        return src.at[k]
    return src.at[pl.ds(pl.multiple_of((2 * k + 1 - c) * shard_rows, SUBLANE), shard_rows), :]


def _pair_send_start(name, away, shard_rows=None):
    shape = away.shape if shard_rows is None else (N_CHIP, shard_rows, away.shape[1])
    land = lax.empty(shape, away.dtype)

    def body(src, dst, send, recv, src_thru, dst_thru, token):
        x, y, c, _ = _place()
        for k in range(N_CHIP):
            pltpu.make_async_remote_copy(src_ref=_away_shard(src, k, c, shard_rows), dst_ref=dst.at[k], send_sem=send.at[k],
                                         recv_sem=recv.at[k], device_id=(x, y, 1 - c), device_id_type=MESH).start()
        token[...] = jnp.zeros_like(token)

    sem = pltpu.SemaphoreType.DMA((N_CHIP,))
    out = pl.pallas_call(
        body, name=name,
        in_specs=[HBM, HBM], out_specs=[SEM, SEM, HBM, HBM, IN_VMEM],
        out_shape=[sem, sem, _hbm_like(away), _hbm_like(land), jax.ShapeDtypeStruct((SUBLANE, LANE), F32)],
        input_output_aliases={0: 2, 1: 3},
        compiler_params=SPLIT,
    )(_in_hbm(away), _in_hbm(land))
    return (out[0], out[1]), out[2], out[3], out[4]


def _pair_send_wait(name, sems, src, land, after, shard_rows=None):
    def body(src_ref, dst_ref, send, recv, *rest):
        for k in range(N_CHIP):
            _await(dst_ref.at[k], send.at[k])
            _await(dst_ref.at[k], recv.at[k])

    return pl.pallas_call(
        body, name=name,
        in_specs=[HBM, HBM, SEM, SEM] + [ANY] * len(after), out_specs=HBM, out_shape=_hbm_like(land),
        input_output_aliases={1: 0},
        compiler_params=SPLIT,
    )(src, land, *sems, *after)


def _chip_send_start(name, sums):
    n = len(sums)
    lands = [lax.empty(a.shape, a.dtype) for a in sums]

    def body(*refs):
        src, land = refs[:n], refs[n:2 * n]
        send, recv, token = refs[2 * n], refs[2 * n + 1], refs[-1]
        x, y, c, chips = _place()
        for w in range(n):
            for j, (px, py) in enumerate(chips):
                pltpu.make_async_remote_copy(
                    src_ref=src[w].at[2 * px + py], dst_ref=land[w].at[2 * x + y],
                    send_sem=send.at[3 * w + j], recv_sem=recv.at[3 * w + j],
                    device_id=(px, py, c), device_id_type=MESH).start()
        token[...] = jnp.zeros_like(token)

    sem = pltpu.SemaphoreType.DMA((3 * n,))
    out = pl.pallas_call(
        body, name=name,
        in_specs=[HBM] * (2 * n),
        out_specs=[SEM, SEM] + [HBM] * (2 * n) + [IN_VMEM],
        out_shape=[sem, sem] + [_hbm_like(a) for a in sums] + [_hbm_like(a) for a in lands]
        + [jax.ShapeDtypeStruct((SUBLANE, LANE), F32)],
        input_output_aliases={i: 2 + i for i in range(2 * n)},
        compiler_params=SPLIT,
    )(*[_in_hbm(a) for a in sums], *[_in_hbm(a) for a in lands])
    return (out[0], out[1]), out[2:2 + n], out[2 + n:2 + 2 * n], out[-1]


def _chip_send_wait(name, groups, after):
    counts = [len(g[1]) for g in groups]
    n = sum(counts)

    def body(*refs):
        land = refs[n:2 * n]
        sems = refs[2 * n:2 * n + 2 * len(groups)]
        w = 0
        for gi, count in enumerate(counts):
            for i in range(count):
                for j in range(3):
                    _await(land[w].at[0], sems[2 * gi].at[3 * i + j])
                    _await(land[w].at[0], sems[2 * gi + 1].at[3 * i + j])
                w += 1

    sums = [a for g in groups for a in g[1]]
    lands = [a for g in groups for a in g[2]]
    sems = [s for g in groups for s in g[0]]
    return pl.pallas_call(
        body, name=name,
        in_specs=[HBM] * (2 * n) + [SEM] * len(sems) + [ANY] * len(after),
        out_specs=[HBM] * n, out_shape=[_hbm_like(a) for a in lands],
        input_output_aliases={n + i: i for i in range(n)},
        compiler_params=SPLIT,
    )(*sums, *lands, *sems, *after)


def _small_all_reduce(part, after=()):
    _, w = part.shape

    def body(p_ref, *rest):
        o_ref, buf, send_sems, recv_sems = rest[len(after):]
        x, y, c, _ = _place()
        me = 4 * x + 2 * y + c
        buf[me] = jnp.sum(p_ref[...], axis=0, keepdims=True)
        copies = []
        for k in range(1, N_DEV):
            dx, dy, dc = (k >> 2) & 1, (k >> 1) & 1, k & 1
            copies.append(pltpu.make_async_remote_copy(
                src_ref=buf.at[me], dst_ref=buf.at[me], send_sem=send_sems.at[k - 1], recv_sem=recv_sems.at[k - 1],
                device_id=(x ^ dx, y ^ dy, c ^ dc), device_id_type=MESH))
        for cp in copies:
            cp.start()
        for cp in copies:
            cp.wait()
        tot = buf[0]
        for d in range(1, N_DEV):
            tot = tot + buf[d]
        o_ref[...] = tot
        loss = jnp.sum(tot[:, w - LANE:], axis=1, keepdims=True)
        o_ref[:, w - LANE:] = jnp.broadcast_to(loss, (1, LANE))

    return pl.pallas_call(
        body, name="small_all_reduce",
        in_specs=[IN_VMEM] + [ANY] * len(after), out_specs=IN_VMEM,
        out_shape=jax.ShapeDtypeStruct((1, w), F32),
        scratch_shapes=[pltpu.VMEM((N_DEV, 1, w), F32), pltpu.SemaphoreType.DMA((N_DEV - 1,)), pltpu.SemaphoreType.DMA((N_DEV - 1,))],
        compiler_params=pltpu.CompilerParams(vmem_limit_bytes=VMEM_LIMIT_BYTES),
    )(part, *after)


def _adamw(w, g, m, v):
    m = ADAM_B1 * m + (1.0 - ADAM_B1) * g
    v = ADAM_B2 * v + (1.0 - ADAM_B2) * (g * g)
    m_hat = m / (1.0 - ADAM_B1 ** ADAM_STEP)
    v_hat = v / (1.0 - ADAM_B2 ** ADAM_STEP)
    delta = -ADAM_LR * (m_hat / (jnp.sqrt(v_hat) + ADAM_EPS) + ADAM_WD * w)
    return delta, m, v


def _sum_adam_block(chip_ref, p_ref, own_ref, w_ref, m_ref, v_ref, g_ref, d_ref, mo_ref, vo_ref):
    g = None
    for k in range(N_CHIP):
        term = jnp.where(chip_ref[0] == k, own_ref[...], p_ref[k]).astype(F32)
        g = term if g is None else g + term
    g_ref[...] = g
    d_ref[...], mo_ref[...], vo_ref[...] = _adamw(w_ref[...], g, m_ref[...], v_ref[...])


def _sum_adam(name, parts, sums, chip, w, m, v, after=()):
    _, r, c = w.shape
    n_after = len(after)
    by_rows = r % ROWS == 0 or r < ROWS
    tr, tc = (min(ROWS, r), c) if by_rows else (r, _fit(c, 512))
    at = (lambda i: (i, 0)) if by_rows else (lambda i: (0, i))

    def body(chip_ref, p_ref, own_ref, w_ref, m_ref, v_ref, *rest):
        _sum_adam_block(chip_ref, p_ref, own_ref, w_ref, m_ref, v_ref, *rest[n_after:])

    blk = pl.BlockSpec((None, tr, tc), lambda i, chip_ref: (0, *at(i)))
    out = jax.ShapeDtypeStruct((1, r, c), F32)
    return pl.pallas_call(
        body, name=name,
        grid_spec=pltpu.PrefetchScalarGridSpec(
            num_scalar_prefetch=1, grid=(r // tr if by_rows else c // tc,),
            in_specs=[pl.BlockSpec((N_CHIP, tr, tc), lambda i, chip_ref: (0, *at(i))),
                      pl.BlockSpec((None, tr, tc), lambda i, chip_ref: (chip_ref[0], *at(i))), blk, blk, blk]
            + [ANY] * n_after,
            out_specs=[blk] * 4),
        out_shape=[out] * 4,
        compiler_params=_params("parallel"),
    )(chip, parts, sums, w, m, v, *after)


def _adam_gains(total, ws, ms, vs):
    n = len(ws)
    widths = [w.shape[1] for w in ws]

    def body(t_ref, *refs):
        w_refs, m_refs, v_refs, outs = refs[:n], refs[n:2 * n], refs[2 * n:3 * n], refs[3 * n:]
        off = 0
        for i in range(n):
            g = t_ref[:, off:off + widths[i]]
            off += widths[i]
            g_ref, d_ref, mo_ref, vo_ref = outs[4 * i:4 * i + 4]
            g_ref[...] = g
            d_ref[...], mo_ref[...], vo_ref[...] = _adamw(w_refs[i][...], g, m_refs[i][...], v_refs[i][...])

    out = pl.pallas_call(
        body, name="adam_gains",
        out_shape=[jax.ShapeDtypeStruct(w.shape, F32) for w in ws for _ in range(4)],
    )(total, *ws, *ms, *vs)
    return [tuple(out[4 * i:4 * i + 4]) for i in range(n)]


def _adam_taps(total, first_col, device, w, m, v):
    _, n_taps, cw = w.shape
    col_block = lambda t, dev: (0, first_col // cw + t * N_DEV + dev[0])
    tap = pl.BlockSpec((None, 1, cw), lambda t, dev: (t, 0, 0))

    def body(dev_ref, t_ref, w_ref, m_ref, v_ref, g_ref, d_ref, mo_ref, vo_ref):
        g = t_ref[...]
        g_ref[...] = g
        d_ref[...], mo_ref[...], vo_ref[...] = _adamw(w_ref[...], g, m_ref[...], v_ref[...])

    shape3 = (n_taps, 1, cw)
    out = pl.pallas_call(
        body, name="adam_taps",
        grid_spec=pltpu.PrefetchScalarGridSpec(
            num_scalar_prefetch=1, grid=(n_taps,),
            in_specs=[pl.BlockSpec((1, cw), col_block), tap, tap, tap], out_specs=[tap] * 4),
        out_shape=[jax.ShapeDtypeStruct(shape3, F32)] * 4,
    )(device, total, w.reshape(shape3), m.reshape(shape3), v.reshape(shape3))
    return tuple(o.reshape(w.shape) for o in out)


def kernel(x, pre_mix_g, w_in, conv_w, q_norm_g, w_uq, kv_norm_g, w_ukv, conv_out_g, attn_out_g, w_o, post_mix_g, pre_mlp_g, w_up, w_down, post_mlp_g, loss_target, m_pre_mix_g, m_w_in, m_conv_w, m_q_norm_g, m_w_uq, m_kv_norm_g, m_w_ukv, m_conv_out_g, m_attn_out_g, m_w_o, m_post_mix_g, m_pre_mlp_g, m_w_up, m_w_down, m_post_mlp_g, v_pre_mix_g, v_w_in, v_conv_w, v_q_norm_g, v_w_uq, v_kv_norm_g, v_w_ukv, v_conv_out_g, v_attn_out_g, v_w_o, v_post_mix_g, v_pre_mlp_g, v_w_up, v_w_down, v_post_mlp_g):
    me = 4 * lax.axis_index("x") + 2 * lax.axis_index("y") + lax.axis_index("c")
    core = lax.axis_index("c").astype(jnp.int32).reshape(1)
    chip = (2 * lax.axis_index("x") + lax.axis_index("y")).astype(jnp.int32).reshape(1)
    gains = (pre_mix_g, q_norm_g, kv_norm_g, conv_out_g, attn_out_g, post_mix_g, pre_mlp_g, post_mlp_g)
    gain_m = (m_pre_mix_g, m_q_norm_g, m_kv_norm_g, m_conv_out_g, m_attn_out_g, m_post_mix_g, m_pre_mlp_g, m_post_mlp_g)
    gain_v = (v_pre_mix_g, v_q_norm_g, v_kv_norm_g, v_conv_out_g, v_attn_out_g, v_post_mix_g, v_pre_mlp_g, v_post_mlp_g)
    names = ("w_in", "w_uq", "w_ukv", "w_o", "w_up", "w_down")
    big = dict(zip(names, (w_in, w_uq, w_ukv, w_o, w_up, w_down)))
    big_m = dict(zip(names, (m_w_in, m_w_uq, m_w_ukv, m_w_o, m_w_up, m_w_down)))
    big_v = dict(zip(names, (v_w_in, v_w_uq, v_w_ukv, v_w_o, v_w_up, v_w_down)))
    n_heads = attn_out_g.shape[1] // HEAD
    n_taps = conv_w.shape[1]

    gathered = ("w_in", "conv", "w_uq", "w_ukv", "w_o", "w_up", "w_down")
    gather_groups = ((0, 1), (2, 3, 4), (5,), (6,))
    taps = jnp.pad(conv_w[0], ((0, SUBLANE - n_taps), (0, 0)))
    relayed_groups = (0, 2, 3)
    sems1, shards, lands, token = _gather_start("gather_start_first", [w_in[0].astype(BF16), taps], ((0, 1),), relayed=(0,))
    sems1, shards, lands = list(sems1), list(shards), list(lands)
    behind = token[0, 0]
    rest = list(lax.optimization_barrier(tuple((big[nm][0] + behind).astype(BF16) for nm in gathered[2:])))

    rest_lands = [lax.dynamic_update_index_in_dim(lax.empty((N_DEV, *a.shape), a.dtype), a, me, 0) for a in rest]

    def start_more(name, some, groups, relayed, after):
        sems_b, shards_b, lands_b, started = _gather_start(name, rest[some], groups, relayed=relayed, after=after,
                                                           lands=rest_lands[some])
        sems1.extend(sems_b)
        shards.extend(shards_b)
        lands.extend(lands_b)
        return started

    start_rest = lambda after: start_more("gather_start_rest", slice(0, 4), ((0, 1, 2), (3,)), (1,), after)
    start_last = lambda after: start_more("gather_start_last", slice(4, 5), ((0,),), (0,), after)

    cols = lambda a: jnp.concatenate([a[j] for j in range(N_DEV)], axis=1)
    rows = lambda a: a.reshape(N_DEV * a.shape[1], a.shape[2])
    device = me.astype(jnp.int32).reshape(1)
    own_in = lambda a, shard: lax.dynamic_update_index_in_dim(a, shard, me, 0)
    q_pieces = [(h, 0, HEAD) for h in range(n_heads)] + [(h, HEAD, QK) for h in range(n_heads)]
    ready = {
        "w_in": lambda a, shard: _join_col_shards("join_w_in", a, shard, device),
        "conv": lambda a, shard: cols(own_in(a, shard))[:n_taps],
        "w_uq": lambda a, shard: _join_col_shards("join_w_uq", a, shard, device, q_pieces),
        "w_ukv": lambda a, shard: cols(a),
        "w_o": lambda a, shard: rows(a),
        "w_up": lambda a, shard: a,
        "w_down": lambda a, shard: rows(a),
    }
    assert w_uq.shape[2] == QK

    class Weights:
        def __init__(self):
            self.passed, self.relayed = {}, {}

        def forward(self, group, after):
            idx = gather_groups[group]
            if group == 0:
                after = (*after, *rest_lands)
            self.passed[group] = _gather_forward(f"gather_forward_{group}", [shards[i] for i in idx], [lands[i] for i in idx],
                                                 *sems1[group], after, relayed=group in relayed_groups)
            return tuple(self.passed[group][1])

        def start(self, group, after):
            assert group == len(gather_groups) - 1
            return (start_last(after),)

        def relay(self, group, after):
            sems2, mid = self.passed[group]
            self.relayed[group], mid = _gather_relay_forward(f"gather_relay_{group}", mid, *sems2, after)
            self.passed[group] = (sems2, mid)
            if group == 0:
                start_rest(tuple(mid))
            return tuple(mid)

        def ready(self, group, after):
            sems2, mid = self.passed[group]
            full = _gather_wait(f"gather_wait_{group}", mid, *sems2, after, relay_sems=self.relayed.get(group))
            out = []
            return [ready[gathered[i]](a, shards[i]) for i, a in zip(gather_groups[group], full)]

    weights = Weights()

    col_blocks = lambda g: g.reshape(g.shape[0], N_DEV, g.shape[1] // N_DEV).transpose(1, 0, 2)
    row_blocks = lambda g: g.reshape(N_DEV, g.shape[0] // N_DEV, g.shape[1])
    grad_groups = (("w_down",), ("w_up",), ("w_o", "w_uq", "w_ukv"), ("w_in",))
    transposed = {"w_in": w_in.shape[2], "w_uq": w_uq.shape[2]}
    to_blocks = {
        "w_in": lambda g: g, "w_uq": lambda g: _unpermute_q_rows(g, n_heads),
        "w_ukv": col_blocks, "w_o": row_blocks, "w_up": lambda g: g, "w_down": row_blocks,
    }
    in_flight = []

    class Grads:
        def __init__(self):
            self.core = core
            self.away = {}

        def send_sums(self, group, sums):
            sems, sums, parts, tok = _chip_send_start(f"chip_send_start_{group}", list(sums))
            in_flight.append((sems, sums, parts))
            return (tok,)

        def full(self, group, arrays, received=None):
            nms = grad_groups[group]
            if received is None:
                blocks = [to_blocks[nm](g) for nm, g in zip(nms, arrays)]
                got = _pair_exchange(f"pair_exchange_{group}", blocks, [transposed.get(nm) for nm in nms])
            else:
                blocks, got = [self.away[group][1]], [self.received(group, received)]
            sums = [(_pair_sum_rows if nm in transposed else _pair_sum)(f"pair_sum_{nm}", g, r, core)
                    for nm, g, r in zip(nms, blocks, got)]
            return self.send_sums(group, sums)

        def send_away(self, group, half):
            nm = grad_groups[group][0]
            rows = transposed.get(nm)
            sems, src, land, tok = _pair_send_start(f"pair_send_start_{group}", half if rows is None else to_blocks[nm](half), rows)
            self.away[group] = (sems, src, land, rows)
            return (tok,)

        def received(self, group, after):
            sems, src, land, rows = self.away[group]
            return _pair_send_wait(f"pair_send_wait_{group}", sems, src, land, after, rows)

        def update_now(self, group, after):
            return update(str(group), group, group + 1, after)

    big_out = {}

    def update(tag, first, last, after):
        picked = [i for i in range(first, last) if grad_groups[i][0] not in big_out]
        groups = [in_flight[i] for i in picked]
        parts = _chip_send_wait("chip_send_wait_" + tag, groups, after)
        nms = [nm for i in picked for nm in grad_groups[i]]
        sums = [a for _, s, _ in groups for a in s]
        for nm, p, s in zip(nms, parts, sums):
            view = (lambda a: jnp.swapaxes(a, 1, 2)) if nm in transposed else (lambda a: a)
            out = _sum_adam("adam_" + nm, p, s, chip, view(big[nm]), view(big_m[nm]), view(big_v[nm]), after=after)
            after = (out[0],)
            big_out[nm] = [view(o) for o in out]
        return after

    grad_x, small = _local_step(x[0], loss_target[0], gains, weights, Grads(), first_after=(token,))

    after = update("early", 0, len(in_flight) - 1, (grad_x,))
    total = _small_all_reduce(small, after=after)
    update("late", len(in_flight) - 1, len(in_flight), (total,))
    big_out = [big_out[nm] for nm in names]

    gain_out = _adam_gains(total, gains, gain_m, gain_v)
    taps_out = _adam_taps(total, sum(g.shape[1] for g in gains), me.astype(jnp.int32).reshape(1), conv_w, m_conv_w, v_conv_w)
    loss = total[0, total.shape[1] - 1]

    order = (0, "w_in", "conv", 1, "w_uq", 2, "w_ukv", 3, 4, "w_o", 5, 6, "w_up", "w_down", 7)
    by_name = dict(zip(names, big_out))
    outs = [loss, grad_x[None]]
    for kind in range(4):
        for item in order:
            if item == "conv":
                outs.append(taps_out[kind])
            elif isinstance(item, int):
                outs.append(gain_out[item][kind])
            else:
                outs.append(by_name[item][kind])
    return tuple(outs)
```

```python
import math

import jax
import jax.numpy as jnp
from jax import lax
from jax.experimental import pallas as pl
from jax.experimental.pallas import tpu as pltpu

F32 = jnp.float32
BF16 = jnp.bfloat16

EPS = 1e-6
NEG_INF = -1e30
HEAD = 128
ROPE = 64
QK = HEAD + ROPE
CHUNK = 64
ROPE_THETA = 10000.0
ADAM_LR, ADAM_B1, ADAM_B2, ADAM_EPS, ADAM_WD, ADAM_STEP = 0.001, 0.9, 0.999, 1e-08, 0.01, 10

LANE = 128
SUBLANE = 8
VMEM_LIMIT_BYTES = 56 * 1024 * 1024

N_DEV = 8
N_CHIP = 4
MESH = pl.DeviceIdType.MESH


def _params(*sem):
    return pltpu.CompilerParams(dimension_semantics=sem, vmem_limit_bytes=VMEM_LIMIT_BYTES)


ANY = pl.BlockSpec(memory_space=pl.ANY)


def _call(body, *, in_specs, after=(), **kw):
    n_in, n_after = len(in_specs), len(after)

    def ordered(*refs):
        body(*refs[:n_in], *refs[n_in + n_after:])

    call = pl.pallas_call(ordered, in_specs=[*in_specs, *[ANY] * n_after], **kw)
    return lambda *operands: call(*operands, *after)


def _sublane_sum(v):
    r, w = v.shape
    return jnp.sum(v.reshape(r // SUBLANE, SUBLANE, w), axis=0)


def _rstd(x):
    return lax.rsqrt(jnp.mean(x * x, axis=-1, keepdims=True) + EPS)


def _rms_bwd(x, g, dy):
    r = _rstd(x)
    xh = x * r
    dxh = dy * g
    dx = r * (dxh - xh * jnp.mean(dxh * xh, axis=-1, keepdims=True))
    return dx, dy * xh


def _accumulate(ref, val, step):
    @pl.when(step == 0)
    def _():
        ref[...] = val

    @pl.when(step > 0)
    def _():
        ref[...] += val


NN = ((1,), (0,))
NT = ((1,), (1,))
TN = ((0,), (0,))


def _matmul(name, a, b, *, grid, a_spec, b_spec, out_shape, out_specs, contract, nk=1, acc_shape=None,
            extras=(), extra_specs=(), epilogue=None, after=()):
    multi = isinstance(out_shape, (tuple, list))
    out_shapes = tuple(out_shape) if multi else (out_shape,)
    n_out = len(out_shapes)
    n_extra = len(extras)

    def body(a_ref, b_ref, *rest):
        x_refs = rest[:n_extra]
        o_refs = rest[n_extra:n_extra + n_out]

        def emit(acc):
            vals = epilogue(acc, *[r[...] for r in x_refs]) if epilogue else (acc,)
            for r, v in zip(o_refs, vals):
                r[...] = v.astype(r.dtype)

        p = lax.dot_general(a_ref[...], b_ref[...], (contract, ((), ())), preferred_element_type=F32)
        if nk == 1:
            emit(p)
        else:
            acc_ref = rest[n_extra + n_out]
            k = pl.program_id(2)
            _accumulate(acc_ref, p, k)

            @pl.when(k == nk - 1)
            def _():
                emit(acc_ref[...])

    sem = ("parallel", "parallel") + (("arbitrary",) if nk > 1 else ())
    return _call(
        body, name=name, grid=grid, after=after,
        in_specs=[a_spec, b_spec, *extra_specs],
        out_specs=out_specs,
        out_shape=out_shape,
        scratch_shapes=[pltpu.VMEM(acc_shape, F32)] if nk > 1 else [],
        compiler_params=_params(*sem),
    )(a, b, *extras)


def _fit(n, tile):
    if n <= tile:
        return n
    t = tile - tile % LANE
    while n % t:
        t -= LANE
    return t


def _mm_nn(name, a, b, out_dtype, tm, tn, after=()):
    m, k = a.shape
    n = b.shape[1]
    tm, tn = _fit(m, tm), _fit(n, tn)
    return _matmul(name, a, b, grid=(m // tm, n // tn), after=after,
                   a_spec=pl.BlockSpec((tm, k), lambda i, j: (i, 0)),
                   b_spec=pl.BlockSpec((k, tn), lambda i, j: (0, j)),
                   out_shape=jax.ShapeDtypeStruct((m, n), out_dtype),
                   out_specs=pl.BlockSpec((tm, tn), lambda i, j: (i, j)), contract=NN)


def _mm_nt(name, a, b, out_dtype, tm, tn, after=()):
    m, k = a.shape
    n = b.shape[0]
    tm, tn = _fit(m, tm), _fit(n, tn)
    return _matmul(name, a, b, grid=(m // tm, n // tn), after=after,
                   a_spec=pl.BlockSpec((tm, k), lambda i, j: (i, 0)),
                   b_spec=pl.BlockSpec((tn, k), lambda i, j: (j, 0)),
                   out_shape=jax.ShapeDtypeStruct((m, n), out_dtype),
                   out_specs=pl.BlockSpec((tm, tn), lambda i, j: (i, j)), contract=NT)


def _mm_tn(name, a, b, out_dtype, tm, tn):
    s, m = a.shape
    n = b.shape[1]
    tm, tn = _fit(m, tm), _fit(n, tn)
    return _matmul(name, a, b, grid=(m // tm, n // tn),
                   a_spec=pl.BlockSpec((s, tm), lambda i, j: (0, i)),
                   b_spec=pl.BlockSpec((s, tn), lambda i, j: (0, j)),
                   out_shape=jax.ShapeDtypeStruct((m, n), out_dtype),
                   out_specs=pl.BlockSpec((tm, tn), lambda i, j: (i, j)), contract=TN)


ROWS = 256


def _row_spec(rows, width):
    return pl.BlockSpec((rows, width), lambda i: (i, 0))


def _row_in_spec(rows, width):
    return pl.BlockSpec((rows, width), lambda i: (i, 0), pipeline_mode=pl.Buffered(3))


def _fixed_spec(rows, width):
    return pl.BlockSpec((rows, width), lambda i: (0, 0))


def _column_pieces(rows, start, width):
    piece = math.gcd(start, width)
    assert piece % LANE == 0
    return [pl.BlockSpec((rows, piece), lambda i, b=start // piece + p: (i, b)) for p in range(width // piece)]


def _rms_fwd(name, x, g, after=()):
    s, w = x.shape
    rows = min(ROWS, s)

    def body(x_ref, g_ref, o_ref):
        xv = x_ref[...]
        o_ref[...] = (xv * _rstd(xv) * g_ref[...]).astype(o_ref.dtype)

    return _call(
        body, name=name, grid=(s // rows,), after=after,
        in_specs=[_row_spec(rows, w), _fixed_spec(1, w)],
        out_specs=_row_spec(rows, w),
        out_shape=jax.ShapeDtypeStruct((s, w), BF16),
        compiler_params=_params("parallel"),
    )(x, g)


def _norm_up(name, x, cols, g, w, after=()):
    s = x.shape[0]
    start, width = cols
    n = w.shape[1]
    tm = min(TILE_M, s)
    pieces = _column_pieces(tm, start, width)
    n_p = len(pieces)

    def body(*refs):
        g_ref, w_ref, xn_ref, o_ref = refs[n_p:]
        xv = refs[0][...] if n_p == 1 else jnp.concatenate([r[...] for r in refs[:n_p]], axis=1)
        xn = (xv * _rstd(xv) * g_ref[...]).astype(BF16)
        xn_ref[...] = xn
        o_ref[...] = jnp.dot(xn, w_ref[...], preferred_element_type=F32)

    return _call(
        body, name=name, grid=(s // tm,), after=after,
        in_specs=[*pieces, _fixed_spec(1, width), _fixed_spec(width, n)],
        out_specs=[_row_spec(tm, width), _row_spec(tm, n)],
        out_shape=[jax.ShapeDtypeStruct((s, width), BF16), jax.ShapeDtypeStruct((s, n), F32)],
        compiler_params=_params("parallel"),
    )(*[x] * n_p, g, w)


def _up_norm_bwd(name, dy, w, x, cols, g, into, tail=None, after=()):
    s, n = dy.shape
    start, width = cols
    tm = min(TILE_M, s)
    pieces = _column_pieces(tm, start, width)
    n_p = len(pieces)
    tails = () if tail is None else (tail,)
    out_width = width if tail is None else into.shape[1] - start
    assert start % out_width == 0 and into.dtype == BF16

    def body(dy_ref, w_ref, *refs):
        g_ref = refs[n_p]
        dx_ref, dg_ref = refs[-2:]
        xv = refs[0][...] if n_p == 1 else jnp.concatenate([r[...] for r in refs[:n_p]], axis=1)
        dxn = lax.dot_general(dy_ref[...], w_ref[...], (NT, ((), ())), preferred_element_type=F32)
        dx, dgc = _rms_bwd(xv, g_ref[...], dxn)
        dx_ref[:, :width] = dx.astype(dx_ref.dtype)
        if tails:
            t = refs[n_p + 1][...]
            dx_ref[:, width:width + t.shape[1]] = t
            dx_ref[:, width + t.shape[1]:] = jnp.zeros((tm, out_width - width - t.shape[1]), dx_ref.dtype)
        _accumulate(dg_ref, _sublane_sum(dgc), pl.program_id(0))

    n_in = 3 + n_p + len(tails)
    return _call(
        body, name=name, grid=(s // tm,), after=after,
        in_specs=[_row_spec(tm, n), _fixed_spec(width, n), *pieces, _fixed_spec(1, width)]
                 + [_row_spec(tm, t.shape[1]) for t in tails] + [ANY],
        out_specs=[pl.BlockSpec((tm, out_width), lambda i: (i, start // out_width)), _fixed_spec(SUBLANE, width)],
        out_shape=[jax.ShapeDtypeStruct(into.shape, into.dtype), jax.ShapeDtypeStruct((SUBLANE, width), F32)],
        input_output_aliases={n_in: 0},
        compiler_params=_params("arbitrary"),
    )(dy, w, *[x] * n_p, g, *tails, into)


def _mid_fwd(x, y, g_post, g_pre, after=()):
    s, w = x.shape
    rows = min(ROWS, s)

    def body(x_ref, y_ref, gp_ref, gq_ref, x2_ref, h2_ref):
        yv = y_ref[...]
        x2 = x_ref[...] + yv * _rstd(yv) * gp_ref[...]
        x2_ref[...] = x2
        h2_ref[...] = (x2 * _rstd(x2) * gq_ref[...]).astype(h2_ref.dtype)

    return _call(
        body, name="mid_fwd", grid=(s // rows,), after=after,
        in_specs=[_row_spec(rows, w), _row_spec(rows, w), _fixed_spec(1, w), _fixed_spec(1, w)],
        out_specs=[_row_spec(rows, w), _row_spec(rows, w)],
        out_shape=[jax.ShapeDtypeStruct((s, w), F32), jax.ShapeDtypeStruct((s, w), BF16)],
        compiler_params=_params("parallel"),
    )(x, y, g_post, g_pre)


def _head(m, x2, tgt, g):
    s, w = m.shape
    rows = min(ROWS, s)

    def body(m_ref, x2_ref, t_ref, g_ref, dout_ref, dm_ref, dg_ref, loss_ref):
        mv = m_ref[...]
        gv = g_ref[...]
        out = x2_ref[...] + mv * _rstd(mv) * gv
        err = out - t_ref[...]
        dout = err * (1.0 / w)
        dout_ref[...] = dout
        dm, dgc = _rms_bwd(mv, gv, dout)
        dm_ref[...] = dm.astype(dm_ref.dtype)
        sq = err * err
        lanes = sq[:, 0:LANE]
        for j in range(1, w // LANE):
            lanes = lanes + sq[:, j * LANE:(j + 1) * LANE]
        step = pl.program_id(0)
        _accumulate(dg_ref, _sublane_sum(dgc), step)
        _accumulate(loss_ref, _sublane_sum(lanes) * (0.5 / w), step)

    return pl.pallas_call(
        body, name="head", grid=(s // rows,),
        in_specs=[_row_spec(rows, w), _row_spec(rows, w), _row_spec(rows, w), _fixed_spec(1, w)],
        out_specs=[_row_spec(rows, w), _row_spec(rows, w), _fixed_spec(SUBLANE, w), _fixed_spec(SUBLANE, LANE)],
        out_shape=[jax.ShapeDtypeStruct((s, w), F32), jax.ShapeDtypeStruct((s, w), BF16),
                   jax.ShapeDtypeStruct((SUBLANE, w), F32), jax.ShapeDtypeStruct((SUBLANE, LANE), F32)],
        compiler_params=_params("arbitrary"),
    )(m, x2, tgt, g)


def _mid_bwd(x2, y, d_out, d_h2, g_pre, g_post, after=()):
    s, w = x2.shape
    rows = min(ROWS, s)

    def body(x2_hbm, y_hbm, dout_hbm, dh2_hbm, gq_ref, gp_ref, dx2_hbm, dy_hbm, dgq_ref, dgp_ref):
        dgq_ref[...] = jnp.zeros_like(dgq_ref)
        dgp_ref[...] = jnp.zeros_like(dgp_ref)

        def rows_step(x2_ref, y_ref, dout_ref, dh2_ref, dx2_ref, dy_ref):
            dx, dgq = _rms_bwd(x2_ref[...], gq_ref[...], dh2_ref[...])
            dx2 = dout_ref[...] + dx
            dx2_ref[...] = dx2
            dy, dgp = _rms_bwd(y_ref[...], gp_ref[...], dx2)
            dy_ref[...] = dy.astype(dy_ref.dtype)
            dgq_ref[...] += _sublane_sum(dgq)
            dgp_ref[...] += _sublane_sum(dgp)

        pltpu.emit_pipeline(rows_step, grid=(s // rows,), in_specs=[_row_in_spec(rows, w)] * 4,
                            out_specs=[_row_spec(rows, w)] * 2)(x2_hbm, y_hbm, dout_hbm, dh2_hbm, dx2_hbm, dy_hbm)

    return _call(
        body, name="mid_bwd", after=after,
        in_specs=[ANY] * 4 + [IN_VMEM] * 2,
        out_specs=[ANY, ANY, IN_VMEM, IN_VMEM],
        out_shape=[jax.ShapeDtypeStruct((s, w), F32), jax.ShapeDtypeStruct((s, w), BF16),
                   jax.ShapeDtypeStruct((SUBLANE, w), F32), jax.ShapeDtypeStruct((SUBLANE, w), F32)],
        compiler_params=pltpu.CompilerParams(vmem_limit_bytes=VMEM_LIMIT_BYTES),
    )(x2, y, d_out, d_h2, g_pre, g_post)


def _first_bwd(x, g, d_h1, d_x2, after=()):
    s, w = x.shape
    rows = min(ROWS, s)

    def body(x_ref, g_ref, dh_ref, dx2_ref, dx_ref, dg_ref):
        dx, dgc = _rms_bwd(x_ref[...], g_ref[...], dh_ref[...])
        dx_ref[...] = dx2_ref[...] + dx
        _accumulate(dg_ref, _sublane_sum(dgc), pl.program_id(0))

    return _call(
        body, name="first_bwd", grid=(s // rows,), after=after,
        in_specs=[_row_spec(rows, w), _fixed_spec(1, w), _row_spec(rows, w), _row_spec(rows, w)],
        out_specs=[_row_spec(rows, w), _fixed_spec(SUBLANE, w)],
        out_shape=[jax.ShapeDtypeStruct((s, w), F32), jax.ShapeDtypeStruct((SUBLANE, w), F32)],
        compiler_params=_params("arbitrary"),
    )(x, g, d_h1, d_x2)


def _shift_down(v, k):
    t = lax.broadcasted_iota(jnp.int32, v.shape, 0)
    return jnp.where(t >= k, pltpu.roll(v, k, 0), 0.0)


def _shift_up(v, k):
    n = v.shape[0]
    t = lax.broadcasted_iota(jnp.int32, v.shape, 0)
    return jnp.where(t < n - k, pltpu.roll(v, n - k, 0), 0.0)


def _conv_core(u, b, c, w):
    z = c * u
    conv = w[0:1, :] * _shift_down(z, 2) + w[1:2, :] * _shift_down(z, 1) + w[2:3, :] * z
    return z, conv, b * conv


def _conv_fwd(proj, conv_w, g, n_groups, out_width, after=()):
    s = proj.shape[0]

    def body(u_ref, b_ref, c_ref, w_ref, g_ref, o_ref):
        _, _, yr = _conv_core(u_ref[...], b_ref[...], c_ref[...], w_ref[...])
        o_ref[...] = (yr * _rstd(yr) * g_ref[...]).astype(o_ref.dtype)

    col = lambda k: pl.BlockSpec((s, HEAD), lambda i: (0, k * n_groups + i))
    return _call(
        body, name="conv_fwd", grid=(n_groups,), after=after,
        in_specs=[col(0), col(1), col(2), pl.BlockSpec((3, HEAD), lambda i: (0, i)), pl.BlockSpec((1, HEAD), lambda i: (0, i))],
        out_specs=pl.BlockSpec((s, HEAD), lambda i: (0, i)),
        out_shape=jax.ShapeDtypeStruct((s, out_width), BF16),
        compiler_params=_params("parallel"),
    )(proj, proj, proj, conv_w, g)


def _conv_bwd(proj, d_mix, conv_w, g, n_groups, out_width):
    s = proj.shape[0]
    width = n_groups * HEAD

    def body(u_ref, b_ref, c_ref, dy_ref, w_ref, g_ref, dproj_ref, dg_ref, dw_ref, buf, sems):
        i = pl.program_id(0)
        slot = i % 2

        def copies(group, slot):
            return [pltpu.make_async_copy(buf.at[slot, k], dproj_ref.at[:, pl.ds(pl.multiple_of((k * n_groups + group) * HEAD, HEAD), HEAD)],
                                          sems.at[slot, k]) for k in range(3)]

        @pl.when(i >= 2)
        def _():
            for cp in copies(i - 2, slot):
                cp.wait()

        u, b, c, w = u_ref[...], b_ref[...], c_ref[...], w_ref[...]
        z, conv, yr = _conv_core(u, b, c, w)
        dyr, dgc = _rms_bwd(yr, g_ref[...], dy_ref[...])
        dconv = dyr * b
        dz = w[2:3, :] * dconv + w[1:2, :] * _shift_up(dconv, 1) + w[0:1, :] * _shift_up(dconv, 2)
        buf[slot, 0] = (dz * c).astype(buf.dtype)
        buf[slot, 1] = (dyr * conv).astype(buf.dtype)
        buf[slot, 2] = (dz * u).astype(buf.dtype)
        for cp in copies(i, slot):
            cp.start()
        dg_ref[...] = _sublane_sum(dgc)
        dw_ref[0] = _sublane_sum(dconv * _shift_down(z, 2))
        dw_ref[1] = _sublane_sum(dconv * _shift_down(z, 1))
        dw_ref[2] = _sublane_sum(dconv * z)

        @pl.when(i == n_groups - 1)
        def _():
            for back in range(min(2, n_groups)):
                for cp in copies(i - back, (n_groups - 1 - back) % 2):
                    cp.wait()

    col = lambda k: pl.BlockSpec((s, HEAD), lambda i: (0, k * n_groups + i))
    grp = pl.BlockSpec((s, HEAD), lambda i: (0, i))
    return pl.pallas_call(
        body, name="conv_bwd", grid=(n_groups,),
        in_specs=[col(0), col(1), col(2), grp, pl.BlockSpec((3, HEAD), lambda i: (0, i)), pl.BlockSpec((1, HEAD), lambda i: (0, i))],
        out_specs=[ANY, pl.BlockSpec((SUBLANE, HEAD), lambda i: (0, i)), pl.BlockSpec((3, SUBLANE, HEAD), lambda i: (0, 0, i))],
        out_shape=[jax.ShapeDtypeStruct((s, out_width), BF16),
                   jax.ShapeDtypeStruct((SUBLANE, width), F32), jax.ShapeDtypeStruct((3, SUBLANE, width), F32)],
        scratch_shapes=[pltpu.VMEM((2, 3, s, HEAD), BF16), pltpu.SemaphoreType.DMA((2, 3))],
        compiler_params=_params("arbitrary"),
    )(proj, proj, proj, d_mix, conv_w, g)


def _rope_tables(s, n_heads):
    pos = jnp.arange(s, dtype=F32)
    inv_freq = jnp.power(ROPE_THETA, -jnp.arange(0, ROPE, 2, dtype=F32) / ROPE)
    ang = pos[:, None] * inv_freq[None, :]
    cos, sin = jnp.cos(ang), jnp.sin(ang)
    cs = jnp.concatenate([cos, cos], axis=1)
    sn = jnp.concatenate([-sin, sin], axis=1)
    pad = jnp.zeros((s, LANE - ROPE), F32)
    return (jnp.tile(cs, (1, n_heads)), jnp.tile(sn, (1, n_heads)),
            jnp.concatenate([cs, pad], axis=1), jnp.concatenate([sn, pad], axis=1))


def _swap_halves(v):
    w = v.shape[1]
    lane = lax.broadcasted_iota(jnp.int32, v.shape, 1)
    first = (lane % ROPE) < (ROPE // 2)
    return jnp.where(first, pltpu.roll(v, w - ROPE // 2, 1), pltpu.roll(v, ROPE // 2, 1))


def _pack_heads(q, kv, proj, kr_col, tables, n_heads, after=()):
    s = q.shape[0]
    rows = min(ROWS, s)
    cq, sq, ck, sk = tables
    wq = n_heads * ROPE

    def body(q_ref, kv_ref, kr_ref, cq_ref, sq_ref, ck_ref, sk_ref, qo_ref, ko_ref, vo_ref):
        qr = q_ref[:, n_heads * HEAD:]
        qr = qr * cq_ref[...] + _swap_halves(qr) * sq_ref[...]
        krv = kr_ref[...]
        krv = krv * ck_ref[...] + _swap_halves(krv) * sk_ref[...]
        for h in range(n_heads):
            qo_ref[h] = jnp.concatenate([q_ref[:, h * HEAD:(h + 1) * HEAD], qr[:, h * ROPE:(h + 1) * ROPE]], axis=1).astype(BF16)
            ko_ref[h] = jnp.concatenate([kv_ref[:, 2 * h * HEAD:(2 * h + 1) * HEAD], krv[:, :ROPE]], axis=1).astype(BF16)
            vo_ref[h] = kv_ref[:, (2 * h + 1) * HEAD:(2 * h + 2) * HEAD].astype(BF16)

    hs = lambda w: pl.BlockSpec((n_heads, rows, w), lambda i: (0, i, 0))
    return _call(
        body, name="pack_heads", grid=(s // rows,), after=after,
        in_specs=[_row_spec(rows, q.shape[1]), _row_spec(rows, kv.shape[1]), pl.BlockSpec((rows, LANE), lambda i: (i, kr_col // LANE)),
                  _row_spec(rows, wq), _row_spec(rows, wq), _row_spec(rows, LANE), _row_spec(rows, LANE)],
        out_specs=[hs(QK), hs(QK), hs(HEAD)],
        out_shape=[jax.ShapeDtypeStruct((n_heads, s, QK), BF16), jax.ShapeDtypeStruct((n_heads, s, QK), BF16),
                   jax.ShapeDtypeStruct((n_heads, s, HEAD), BF16)],
        compiler_params=_params("parallel"),
    )(q, kv, proj, cq, sq, ck, sk)


def _unpack_heads(dq, dk, dv, tables, n_heads):
    s = dq.shape[1]
    rows = min(ROWS, s)
    cq, sq, ck, sk = tables
    wq = n_heads * ROPE

    def body(dq_ref, dk_ref, dv_ref, cq_ref, sq_ref, ck_ref, sk_ref, qo_ref, kvo_ref, kro_ref):
        dqr = jnp.concatenate([dq_ref[h][:, HEAD:] for h in range(n_heads)], axis=1)
        dqr = dqr * cq_ref[...] - _swap_halves(dqr) * sq_ref[...]
        dkr = dk_ref[0][:, HEAD:]
        for h in range(1, n_heads):
            dkr = dkr + dk_ref[h][:, HEAD:]
        dkr = jnp.concatenate([dkr, jnp.zeros((rows, LANE - ROPE), F32)], axis=1)
        dkr = dkr * ck_ref[...] - _swap_halves(dkr) * sk_ref[...]
        kro_ref[...] = dkr.astype(kro_ref.dtype)
        qo_ref[:, n_heads * HEAD:] = dqr.astype(qo_ref.dtype)
        for h in range(n_heads):
            qo_ref[:, h * HEAD:(h + 1) * HEAD] = dq_ref[h][:, :HEAD].astype(qo_ref.dtype)
            kvo_ref[:, 2 * h * HEAD:(2 * h + 1) * HEAD] = dk_ref[h][:, :HEAD].astype(kvo_ref.dtype)
            kvo_ref[:, (2 * h + 1) * HEAD:(2 * h + 2) * HEAD] = dv_ref[h].astype(kvo_ref.dtype)

    hs = lambda w: pl.BlockSpec((n_heads, rows, w), lambda i: (0, i, 0))
    return pl.pallas_call(
        body, name="unpack_heads", grid=(s // rows,),
        in_specs=[hs(QK), hs(QK), hs(HEAD), _row_spec(rows, wq), _row_spec(rows, wq), _row_spec(rows, LANE), _row_spec(rows, LANE)],
        out_specs=[_row_spec(rows, n_heads * QK), _row_spec(rows, 2 * n_heads * HEAD), _row_spec(rows, LANE)],
        out_shape=[jax.ShapeDtypeStruct((s, n_heads * QK), BF16), jax.ShapeDtypeStruct((s, 2 * n_heads * HEAD), BF16),
                   jax.ShapeDtypeStruct((s, LANE), BF16)],
        compiler_params=_params("parallel"),
    )(dq, dk, dv, cq, sq, ck, sk)


TQ = 256


LOG2_E = 1.4426950408889634


def _softmax_parts(q, k):
    tq, n_keys = q.shape[0], k.shape[0]
    sc = lax.dot_general(q, k, (NT, ((), ())), preferred_element_type=F32) * (QK ** -0.5 * LOG2_E)
    row = lax.broadcasted_iota(jnp.int32, (tq, tq), 0)
    col = lax.broadcasted_iota(jnp.int32, (tq, tq), 1)
    own = jnp.where(col // CHUNK <= row // CHUNK, sc[:, n_keys - tq:], NEG_INF)
    sc = own if n_keys == tq else jnp.concatenate([sc[:, :n_keys - tq], own], axis=1)
    e = jnp.exp2(sc - jnp.max(sc, axis=-1, keepdims=True))
    return e, 1.0 / jnp.sum(e, axis=-1, keepdims=True)


def _prob_columns(c, tq):
    return pl.ds(tq * (c * (c + 1) // 2), (c + 1) * tq)


def _attn_fwd(q, k, v, g, mix, col0, first, between):
    n_heads, s, _ = q.shape
    tq = min(TQ, s)
    assert tq % CHUNK == 0 and s % tq == 0
    n_blocks = s // tq
    p_cols = tq * (n_blocks * (n_blocks + 1) // 2)
    out_shape = [jax.ShapeDtypeStruct((n_heads, s, HEAD), F32), jax.ShapeDtypeStruct((n_heads, tq, p_cols), BF16),
                 jax.ShapeDtypeStruct(mix.shape, mix.dtype)]
    done, after = (mix,), ()
    for part, (h0, h1) in enumerate(((0, first), (first, n_heads))):

        def body(q_ref, k_ref, v_ref, g_ref, *rest):
            o_ref, p_ref, y_ref = rest[-3:]
            for c in range(n_blocks):
                rows, n_keys = pl.ds(c * tq, tq), (c + 1) * tq
                e, inv = _softmax_parts(q_ref[rows, :], k_ref[0:n_keys, :])
                p = (e * inv).astype(BF16)
                p_ref[:, _prob_columns(c, tq)] = p
                o = jnp.dot(p, v_ref[0:n_keys, :], preferred_element_type=F32)
                o_ref[rows, :] = o
                y_ref[rows, :] = (o * _rstd(o) * g_ref[...]).astype(y_ref.dtype)

        head = lambda w, h0=h0: pl.BlockSpec((None, s, w), lambda h: (h0 + h, 0, 0))
        n_done = len(done)
        done = _call(
            body, name=f"attn_fwd_{part}", grid=(h1 - h0,), after=after,
            in_specs=[head(QK), head(QK), head(HEAD), pl.BlockSpec((1, HEAD), lambda h, h0=h0: (0, h0 + h))] + [ANY] * n_done,
            out_specs=[head(HEAD), pl.BlockSpec((None, tq, p_cols), lambda h, h0=h0: (h0 + h, 0, 0)),
                       pl.BlockSpec((s, HEAD), lambda h, h0=h0: (0, col0 // HEAD + h0 + h))],
            out_shape=out_shape,
            input_output_aliases={4 + i: 3 - n_done + i for i in range(n_done)},
            compiler_params=_params("parallel"),
        )(q, k, v, g, *done)
        after = between(done) if part == 0 else ()
    return done


def _attn_bwd(q, k, v, o, probs, d_mix, g, col0, after=()):
    n_heads, s, _ = q.shape
    tq = probs.shape[1]

    def body(q_ref, k_ref, v_ref, o_ref, p_ref, dy_ref, g_ref, dq_ref, dk_ref, dv_ref, dg_ref):
        dg = None
        for c in reversed(range(s // tq)):
            rows, n_keys = pl.ds(c * tq, tq), (c + 1) * tq
            o = o_ref[rows, :]
            do, dgc = _rms_bwd(o, g_ref[...], dy_ref[rows, :])
            do = do.astype(BF16)
            dg = _sublane_sum(dgc) if dg is None else dg + _sublane_sum(dgc)
            p = p_ref[:, _prob_columns(c, tq)]
            dp = lax.dot_general(do, v_ref[0:n_keys, :], (NT, ((), ())), preferred_element_type=F32)
            ds = (p.astype(F32) * (dp - jnp.sum(do.astype(F32) * o, axis=-1, keepdims=True))).astype(BF16)
            dq_ref[rows, :] = jnp.dot(ds, k_ref[0:n_keys, :], preferred_element_type=F32) * (QK ** -0.5)
            dk = lax.dot_general(ds, q_ref[rows, :], (TN, ((), ())), preferred_element_type=F32)
            dv = lax.dot_general(p, do, (TN, ((), ())), preferred_element_type=F32)
            if n_keys == s:
                dk_ref[...] = dk
                dv_ref[...] = dv
            else:
                dk_ref[0:n_keys, :] += dk
                dv_ref[0:n_keys, :] += dv
        dk_ref[...] = dk_ref[...] * (QK ** -0.5)
        dg_ref[...] = dg

    c0 = col0 // HEAD
    head = lambda w: pl.BlockSpec((None, s, w), lambda h, *_: (h, 0, 0))
    in_specs = [head(QK), head(QK), head(HEAD), head(HEAD), pl.BlockSpec((None, tq, probs.shape[2]), lambda h, *_: (h, 0, 0)),
                pl.BlockSpec((s, HEAD), lambda h, *_: (0, c0 + h)), pl.BlockSpec((1, HEAD), lambda h, *_: (0, h))]
    out_specs = [head(QK), head(QK), head(HEAD), pl.BlockSpec((SUBLANE, HEAD), lambda h, *_: (0, h))]
    out_shape = [jax.ShapeDtypeStruct((n_heads, s, QK), F32), jax.ShapeDtypeStruct((n_heads, s, QK), F32),
                 jax.ShapeDtypeStruct((n_heads, s, HEAD), F32), jax.ShapeDtypeStruct((SUBLANE, n_heads * HEAD), F32)]
    return _call(body, name="attn_bwd", grid=(n_heads,), after=after, in_specs=in_specs, out_specs=out_specs,
                 out_shape=out_shape, compiler_params=_params("parallel"))(q, k, v, o, probs, d_mix, g)


TILE_M = 1024
TILE_N = 1024


def _up_fwd(h2, w_up, between):
    s, d = h2.shape
    nb, _, fb = w_up.shape
    tm = min(TILE_M, s)
    done, after = (), ()
    for tile in range(s // tm):

        def body(h_ref, w_ref, *rest):
            a_ref, r_ref = rest[-2:]
            r = jnp.maximum(jnp.dot(h_ref[...], w_ref[...], preferred_element_type=F32), 0.0)
            a_ref[...] = (r * r).astype(a_ref.dtype)
            r_ref[...] = r.astype(r_ref.dtype)

        blk = pl.BlockSpec((tm, fb), lambda j, tile=tile: (tile, j))
        done = _call(
            body, name=f"up_fwd_{tile}", grid=(nb,), after=after,
            in_specs=[pl.BlockSpec((tm, d), lambda j, tile=tile: (tile, 0)), pl.BlockSpec((None, d, fb), lambda j: (j, 0, 0))]
                     + [ANY] * len(done),
            out_specs=[blk, blk], out_shape=[jax.ShapeDtypeStruct((s, nb * fb), BF16)] * 2,
            input_output_aliases={2 + i: i for i in range(len(done))},
            compiler_params=_params("parallel"),
        )(h2, w_up, *done)
        after = between(done) if tile == 0 else ()
    return done


def _down_fwd(a, w_down):
    s, f = a.shape
    d = w_down.shape[1]
    tm, tn, tk = min(TILE_M,s), min(TILE_N,d), 2048
    nk = f // tk
    return _matmul("down_fwd", a, w_down, grid=(s // tm, d // tn, nk),
                   a_spec=pl.BlockSpec((tm, tk), lambda i, j, k: (i, k)),
                   b_spec=pl.BlockSpec((tk, tn), lambda i, j, k: (k, j)),
                   out_shape=jax.ShapeDtypeStruct((s, d), F32),
                   out_specs=pl.BlockSpec((tm, tn), lambda i, j, k: (i, j)),
                   contract=NN, nk=nk, acc_shape=(tm, tn))


def _down_bwd_act(d_m, w_down, r, after=()):
    s, d = d_m.shape
    f = w_down.shape[0]
    tm, tn = min(TILE_M,s), min(TILE_N,f)
    blk = pl.BlockSpec((tm, tn), lambda i, j: (i, j))
    return _matmul("down_bwd_act", d_m, w_down, grid=(s // tm, f // tn), after=after,
                   a_spec=pl.BlockSpec((tm, d), lambda i, j: (i, 0)),
                   b_spec=pl.BlockSpec((tn, d), lambda i, j: (j, 0)),
                   out_shape=jax.ShapeDtypeStruct((s, f), BF16), out_specs=blk, contract=NT,
                   extras=(r,), extra_specs=(blk,),
                   epilogue=lambda acc, rv: (acc * (2.0 * rv.astype(F32)),))


def _up_bwd_act(d_up, w_up, after=()):
    s, _ = d_up.shape
    nb, d, fb = w_up.shape
    tm, tn = min(TILE_M, s), min(TILE_N,d)
    pair = 2
    n_after = len(after)

    def body(a_ref, w_ref, *rest):
        o_ref, acc_ref = rest[n_after:]
        k = pl.program_id(2)
        p = None
        for t in range(pair):
            term = lax.dot_general(a_ref[:, t * fb:(t + 1) * fb], w_ref[t], (NT, ((), ())), preferred_element_type=F32)
            p = term if p is None else p + term
        _accumulate(acc_ref, p, k)

        @pl.when(k == nb // pair - 1)
        def _():
            o_ref[...] = acc_ref[...]

    return pl.pallas_call(
        body, name="up_bwd_act", grid=(s // tm, d // tn, nb // pair),
        in_specs=[pl.BlockSpec((tm, pair * fb), lambda i, j, k: (i, k)),
                  pl.BlockSpec((pair, tn, fb), lambda i, j, k: (k, j, 0))] + [ANY] * n_after,
        out_specs=pl.BlockSpec((tm, tn), lambda i, j, k: (i, j)),
        out_shape=jax.ShapeDtypeStruct((s, d), F32),
        scratch_shapes=[pltpu.VMEM((tm, tn), F32)],
        compiler_params=_params("parallel", "parallel", "arbitrary"),
    )(d_up, w_up, *after)


def _half_grad(name, a, b, core, home, received, after, *, grid, a_block, a_map, b_block, b_map, o_block, o_map, out_shape):
    n_after = len(after)
    pick = (lambda ref: ref[0]) if home else (lambda ref: 1 - ref[0])

    def body(core_ref, a_ref, b_ref, *rest):
        acc = lax.dot_general(a_ref[...], b_ref[...], (TN, ((), ())), preferred_element_type=F32)
        if received is not None:
            acc = acc + rest[0][...].astype(F32)
        rest[-1][...] = acc.astype(rest[-1].dtype)

    wrap = lambda fn: (lambda i, j, core_ref: fn(i, j, pick(core_ref)))
    o_spec = pl.BlockSpec(o_block, wrap(o_map))
    extra = [] if received is None else [o_spec]
    operands = [] if received is None else [received]
    return pl.pallas_call(
        body, name=name,
        grid_spec=pltpu.PrefetchScalarGridSpec(
            num_scalar_prefetch=1, grid=grid,
            in_specs=[pl.BlockSpec(a_block, wrap(a_map)), pl.BlockSpec(b_block, wrap(b_map))] + extra + [ANY] * n_after,
            out_specs=o_spec),
        out_shape=out_shape,
        compiler_params=_params("parallel", "parallel"),
    )(core, a, b, *operands, *after)


def _down_half_grad(name, a, d_m, core, home, received=None, after=()):
    s, f = a.shape
    d = d_m.shape[1]
    r = f // N_DEV
    tn = min(TILE_N, d)
    return _half_grad(name, a, d_m, core, home, received, after, grid=(N_CHIP, d // tn),
                      a_block=(s, r), a_map=lambda k, j, p: (0, 2 * k + p),
                      b_block=(s, tn), b_map=lambda k, j, p: (0, j),
                      o_block=(None, r, tn), o_map=lambda k, j, p: (k, 0, j),
                      out_shape=jax.ShapeDtypeStruct((N_CHIP, r, d), BF16))


def _up_half_grad(name, h2, d_up, core, home, received=None, after=()):
    s, d = h2.shape
    fb = d_up.shape[1] // N_DEV
    tm = min(TILE_M, d)
    return _half_grad(name, h2, d_up, core, home, received, after, grid=(d // tm, N_CHIP),
                      a_block=(s, tm), a_map=lambda i, k, p: (0, i),
                      b_block=(s, fb), b_map=lambda i, k, p: (0, 2 * k + p),
                      o_block=(None, tm, fb), o_map=lambda i, k, p: (k, i, 0),
                      out_shape=jax.ShapeDtypeStruct((N_CHIP, d, fb), BF16))


MXU_WIDTH = 256


def _in_pad(in_width):
    return -(-in_width // MXU_WIDTH) * MXU_WIDTH


def _join_col_shards(name, blocks, own, device, pieces=None):
    n, r, w = blocks.shape
    rows = min(ROWS, r)
    pieces = pieces or [(j, 0, w) for j in range(n)]
    used = sum(b - a for _, a, b in pieces)
    width = _in_pad(used)

    def body(dev_ref, x_ref, own_ref, o_ref):
        block = lambda j: jnp.where(dev_ref[0] == j, own_ref[...], x_ref[j])
        cols = [block(j)[:, a:b] for j, a, b in pieces]
        tail = [jnp.zeros((rows, width - used), o_ref.dtype)] if width > used else []
        o_ref[...] = jnp.concatenate(cols + tail, axis=1)

    return pl.pallas_call(
        body, name=name,
        grid_spec=pltpu.PrefetchScalarGridSpec(
            num_scalar_prefetch=1, grid=(r // rows,),
            in_specs=[pl.BlockSpec((n, rows, w), lambda i, dev: (0, i, 0)), pl.BlockSpec((rows, w), lambda i, dev: (i, 0))],
            out_specs=pl.BlockSpec((rows, width), lambda i, dev: (i, 0))),
        out_shape=jax.ShapeDtypeStruct((r, width), blocks.dtype),
        compiler_params=_params("parallel"),
    )(device, blocks, own)


def _unpermute_q_rows(wt, n_heads):
    r = wt.shape[1]
    nope = wt[:n_heads * HEAD].reshape(n_heads, HEAD, r)
    rope = wt[n_heads * HEAD:].reshape(n_heads, ROPE, r)
    return jnp.concatenate([nope, rope], axis=1).reshape(n_heads * QK, r)


def _local_step(x, tgt, gains, weights, grads, first_after=()):
    pre_mix_g, q_norm_g, kv_norm_g, conv_out_g, attn_out_g, post_mix_g, pre_mlp_g, post_mlp_g = gains
    s, d = x.shape
    conv_width = conv_out_g.shape[1]
    n_groups = conv_width // HEAD
    r_q, r_kv = q_norm_g.shape[1], kv_norm_g.shape[1]
    n_heads = attn_out_g.shape[1] // HEAD
    c_q0 = 3 * conv_width
    c_kv0 = c_q0 + r_q
    c_kr0 = c_kv0 + r_kv
    in_pad = _in_pad(c_kr0 + ROPE)
    tn_in = _fit(in_pad, 6 * MXU_WIDTH)
    tables = _rope_tables(s, n_heads)

    h1 = _rms_fwd("pre_mix_norm", x, pre_mix_g, after=first_after)
    weights.forward(0, (h1,))
    weights.relay(0, tables)
    w_in_p, conv_w = weights.ready(0, ())
    proj = _mm_nn("in_proj", h1, w_in_p, F32, TILE_M, tn_in)
    y_conv = _conv_fwd(proj, conv_w, conv_out_g, n_groups, conv_width + n_heads * HEAD, after=weights.forward(1, (proj,)))
    w_uq_p, w_ukv, w_o = weights.ready(1, (y_conv,))
    qn, q = _norm_up("q_up", proj, (c_q0, r_q), q_norm_g, w_uq_p)
    kvn, kv = _norm_up("kv_up", proj, (c_kv0, r_kv), kv_norm_g, w_ukv)
    qh, kh, vh = _pack_heads(q, kv, proj, c_kr0, tables, n_heads)
    o, probs, mix = _attn_fwd(qh, kh, vh, attn_out_g, y_conv, conv_width, n_heads // 4,
                              lambda done: weights.forward(2, tuple(done)))
    y = _mm_nn("out_proj", mix, w_o, F32, TILE_M, TILE_N, after=weights.start(3, (mix,)))
    x2, h2 = _mid_fwd(x, y, post_mix_g, pre_mlp_g, after=weights.relay(2, (y,)))
    (w_up,) = weights.ready(2, (h2,))
    a, r = _up_fwd(h2, w_up, lambda done: weights.forward(3, tuple(done)))
    weights.relay(3, (a,))
    (w_down,) = weights.ready(3, ())
    m = _down_fwd(a, w_down)

    d_out, d_m, dg_post_mlp, loss_part = _head(m, x2, tgt, post_mlp_g)
    core = grads.core
    away = _down_half_grad("down_bwd_w_away", a, d_m, core, home=False)
    d_up = _down_bwd_act(d_m, w_down, r, after=grads.send_away(0, away))
    sums = _down_half_grad("down_bwd_w_home", a, d_m, core, home=True, received=grads.received(0, (d_up,)))
    away = _up_half_grad("up_bwd_w_away", h2, d_up, core, home=False, after=grads.send_sums(0, (sums,)))
    d_h2 = _up_bwd_act(d_up, w_up, after=grads.send_away(1, away))
    sums = _up_half_grad("up_bwd_w_home", h2, d_up, core, home=True, received=grads.received(1, (d_h2,)))
    d_x2, d_y, dg_pre_mlp, dg_post_mix = _mid_bwd(x2, y, d_out, d_h2, pre_mlp_g, post_mix_g, after=grads.send_sums(1, (sums,)))
    d_mix = _mm_nt("out_proj_bwd_act", d_y, w_o, F32, TILE_M, TILE_N)
    gw_o = _mm_tn("out_proj_bwd_w", mix, d_y, BF16, TILE_M, TILE_N)
    dqh, dkh, dvh, dg_attn = _attn_bwd(qh, kh, vh, o, probs, d_mix, attn_out_g, conv_width)
    d_q, d_kv, d_kr = _unpack_heads(dqh, dkh, dvh, tables, n_heads)
    gw_uq_t = _mm_tn("q_up_bwd_w", d_q, qn, F32, TILE_M, TILE_N)
    gw_ukv = _mm_tn("kv_up_bwd_w", kvn, d_kv, BF16, TILE_M, TILE_N)
    d_proj, dg_conv, dw_conv = _conv_bwd(proj, d_mix, conv_w, conv_out_g, n_groups, in_pad)
    d_proj, dg_q = _up_norm_bwd("q_up_bwd_act", d_q, w_uq_p, proj, (c_q0, r_q), q_norm_g, d_proj,
                                after=grads.full(2, (gw_o, gw_uq_t, gw_ukv)))
    d_proj, dg_kv = _up_norm_bwd("kv_up_bwd_act", d_kv, w_ukv, proj, (c_kv0, r_kv), kv_norm_g, d_proj, tail=d_kr)
    gw_in_t = _mm_tn("in_proj_bwd_w", d_proj, h1, F32, tn_in, TILE_N)
    d_h1 = _mm_nt("in_proj_bwd_act", d_proj, w_in_p, F32, TILE_M, TILE_N, after=grads.send_away(3, gw_in_t))
    grad_x, dg_pre_mix = _first_bwd(x, pre_mix_g, d_h1, d_x2, after=grads.full(3, (gw_in_t,), received=(d_h1,)))

    small = [dg_pre_mix, dg_q, dg_kv, dg_conv, dg_attn, dg_post_mix, dg_pre_mlp, dg_post_mlp,
             dw_conv[0], dw_conv[1], dw_conv[2], loss_part]
    return grad_x, jnp.concatenate(small, axis=1)


HBM = pl.BlockSpec(memory_space=pltpu.HBM)
SEM = pl.BlockSpec(memory_space=pltpu.SEMAPHORE)
IN_VMEM = pl.BlockSpec(memory_space=pltpu.VMEM)
SPLIT = pltpu.CompilerParams(has_side_effects=pltpu.SideEffectType.DATAFLOW_SIDE_EFFECTING)


def _in_hbm(a):
    return pltpu.with_memory_space_constraint(a, pltpu.HBM)


def _hbm_like(a):
    return pltpu.HBM(a.shape, a.dtype)


def _place():
    x, y, c = lax.axis_index("x"), lax.axis_index("y"), lax.axis_index("c")
    other_chips = [(1 - x, y), (x, 1 - y), (1 - x, 1 - y)]
    return x, y, c, other_chips


def _block(px, py, pc):
    return 4 * px + 2 * py + pc


def _await(block, sem):
    pltpu.make_async_copy(block, block, sem).wait()


def _relay_route(x, y, c):
    came_from = ((1 - x) * (1 - c) + x * c, y * (1 - c) + (1 - y) * c)
    goes_to = (x * (1 - c) + (1 - x) * c, (1 - y) * (1 - c) + y * c)
    return came_from, goes_to


def _gather_start(name, shards, groups, relayed=(), after=(), lands=None):
    n, ng = len(shards), len(groups)
    if lands is None:
        lands = [lax.empty((N_DEV, *a.shape), a.dtype) for a in shards]

    def body(*refs):
        src, land = refs[:n], refs[n:2 * n]
        sems, token = refs[2 * n + len(after):2 * n + len(after) + 2 * ng], refs[-1]
        x, y, c, chips = _place()
        targets = [(x, y, 1 - c)] + [(*chip, c) for chip in chips]
        for gi, group in enumerate(groups):
            for i, w in enumerate(group):
                for k, to in enumerate(targets[:3] if gi in relayed else targets):
                    pltpu.make_async_remote_copy(
                        src_ref=src[w], dst_ref=land[w].at[_block(x, y, c)],
                        send_sem=sems[2 * gi].at[4 * i + k], recv_sem=sems[2 * gi + 1].at[4 * i + k],
                        device_id=to, device_id_type=MESH).start()
        token[...] = jnp.zeros_like(token)

    sem_shapes = [pltpu.SemaphoreType.DMA((4 * len(g),)) for g in groups for _ in range(2)]
    out = pl.pallas_call(
        body, name=name,
        in_specs=[HBM] * (2 * n) + [ANY] * len(after),
        out_specs=[SEM] * (2 * ng) + [HBM] * (2 * n) + [IN_VMEM],
        out_shape=sem_shapes + [_hbm_like(a) for a in shards] + [_hbm_like(a) for a in lands]
        + [jax.ShapeDtypeStruct((SUBLANE, LANE), F32)],
        input_output_aliases={i: 2 * ng + i for i in range(2 * n)},
        compiler_params=SPLIT,
    )(*[_in_hbm(a) for a in shards], *[_in_hbm(a) for a in lands], *after)
    sems = [(out[2 * gi], out[2 * gi + 1]) for gi in range(ng)]
    return sems, out[2 * ng:2 * ng + n], out[2 * ng + n:2 * ng + 2 * n], out[-1]


def _gather_forward(name, shards, lands, send1, recv1, after, relayed=False):
    n = len(lands)

    def body(*refs):
        src, land = refs[:n], refs[n:2 * n]
        s1, r1 = refs[2 * n], refs[2 * n + 1]
        s2, r2 = refs[2 * n + 2 + len(after)], refs[2 * n + 3 + len(after)]
        x, y, c, chips = _place()
        me, sibling = (x, y, c), (x, y, 1 - c)
        for j, chip in enumerate(chips[:2] if relayed else chips):
            for i in range(n):
                blk = land[i].at[_block(*chip, c)]
                pltpu.make_async_remote_copy(src_ref=blk, dst_ref=blk, send_sem=s1.at[4 * i + 1 + j], recv_sem=r1.at[4 * i + 1 + j],
                                             device_id=me, device_id_type=MESH).wait_recv()
                pltpu.make_async_remote_copy(src_ref=blk, dst_ref=blk, send_sem=s2.at[3 * i + j], recv_sem=r2.at[3 * i + j],
                                             device_id=sibling, device_id_type=MESH).start()
        if relayed:
            came_from, goes_to = _relay_route(x, y, c)
            for i in range(n):
                blk = land[i].at[_block(*came_from, c)]
                pltpu.make_async_remote_copy(src_ref=blk, dst_ref=blk, send_sem=s2.at[3 * i + 2], recv_sem=r2.at[3 * i + 2],
                                             device_id=(*goes_to, c), device_id_type=MESH).start()
        for i in range(n):
            blk = land[i].at[_block(x, y, 1 - c)]
            pltpu.make_async_remote_copy(src_ref=blk, dst_ref=blk, send_sem=s1.at[4 * i], recv_sem=r1.at[4 * i],
                                         device_id=me, device_id_type=MESH).wait_recv()
            for k in range(3 if relayed else 4):
                pltpu.make_async_remote_copy(src_ref=src[i], dst_ref=land[i].at[_block(x, y, c)], send_sem=s1.at[4 * i + k],
                                             recv_sem=r1.at[4 * i + k], device_id=sibling, device_id_type=MESH).wait_send()

    sem = pltpu.SemaphoreType.DMA((3 * n,))
    out = pl.pallas_call(
        body, name=name,
        in_specs=[HBM] * (2 * n) + [SEM, SEM] + [ANY] * len(after),
        out_specs=[SEM, SEM] + [HBM] * n,
        out_shape=[sem, sem] + [_hbm_like(a) for a in lands],
        input_output_aliases={n + i: 2 + i for i in range(n)},
        compiler_params=SPLIT,
    )(*shards, *lands, send1, recv1, *after)
    return (out[0], out[1]), out[2:]


def _gather_relay_forward(name, lands, send2, recv2, after):
    n = len(lands)

    def body(*refs):
        land, s2, r2 = refs[:n], refs[n], refs[n + 1]
        s3, r3 = refs[n + 2 + len(after)], refs[n + 3 + len(after)]
        x, y, c, _ = _place()
        me, sibling = (x, y, c), (x, y, 1 - c)
        came_from, _ = _relay_route(x, y, c)
        for i in range(n):
            blk = land[i].at[_block(1 - x, 1 - y, c)]
            pltpu.make_async_remote_copy(src_ref=blk, dst_ref=blk, send_sem=s2.at[3 * i + 2], recv_sem=r2.at[3 * i + 2],
                                         device_id=me, device_id_type=MESH).wait_recv()
            pltpu.make_async_remote_copy(src_ref=blk, dst_ref=blk, send_sem=s3.at[i], recv_sem=r3.at[i],
                                         device_id=sibling, device_id_type=MESH).start()
            sent = land[i].at[_block(*came_from, c)]
            pltpu.make_async_remote_copy(src_ref=sent, dst_ref=sent, send_sem=s2.at[3 * i + 2], recv_sem=r2.at[3 * i + 2],
                                         device_id=me, device_id_type=MESH).wait_send()

    sem = pltpu.SemaphoreType.DMA((n,))
    out = pl.pallas_call(
        body, name=name,
        in_specs=[HBM] * n + [SEM, SEM] + [ANY] * len(after),
        out_specs=[SEM, SEM] + [HBM] * n,
        out_shape=[sem, sem] + [_hbm_like(a) for a in lands],
        input_output_aliases={i: 2 + i for i in range(n)},
        compiler_params=SPLIT,
    )(*lands, send2, recv2, *after)
    return (out[0], out[1]), out[2:]


def _gather_wait(name, lands, send2, recv2, after, relay_sems=None):
    n = len(lands)
    n_sems = 2 if relay_sems is None else 4

    def body(*refs):
        land, s2, r2 = refs[:n], refs[n], refs[n + 1]
        for i in range(n):
            for j in range(3 if relay_sems is None else 2):
                _await(land[i].at[0], r2.at[3 * i + j])
                _await(land[i].at[0], s2.at[3 * i + j])
            if relay_sems is not None:
                _await(land[i].at[0], refs[n + 3].at[i])
                _await(land[i].at[0], refs[n + 2].at[i])

    return pl.pallas_call(
        body, name=name,
        in_specs=[HBM] * n + [SEM] * n_sems + [ANY] * len(after), out_specs=[HBM] * n, out_shape=[_hbm_like(a) for a in lands],
        input_output_aliases={i: i for i in range(n)},
        compiler_params=SPLIT,
    )(*lands, send2, recv2, *(relay_sems or ()), *after)


def _pair_exchange(name, grads, shard_rows):
    n = len(grads)
    shapes = [(g.shape[1:] if r is None else (r, g.shape[1])) for g, r in zip(grads, shard_rows)]

    def body(*refs):
        ins, recv = refs[:n], refs[n:2 * n]
        send_sems, recv_sems = refs[2 * n:]
        x, y, c, _ = _place()
        sends = []
        for w in range(n):
            for k in range(N_CHIP):
                j, r = 2 * k + 1 - c, shard_rows[w]
                src = ins[w].at[j] if r is None else ins[w].at[pl.ds(pl.multiple_of(j * r, SUBLANE), r), :]
                sends.append(pltpu.make_async_remote_copy(
                    src_ref=src, dst_ref=recv[w].at[k],
                    send_sem=send_sems.at[w, k], recv_sem=recv_sems.at[w, k],
                    device_id=(x, y, 1 - c), device_id_type=MESH))
        for cp in sends:
            cp.start()
        for cp in sends:
            cp.wait()

    return pl.pallas_call(
        body, name=name,
        in_specs=[ANY] * n, out_specs=[ANY] * n,
        out_shape=[jax.ShapeDtypeStruct((N_CHIP, *shape), g.dtype) for g, shape in zip(grads, shapes)],
        scratch_shapes=[pltpu.SemaphoreType.DMA((n, N_CHIP))] * 2,
    )(*grads)


def _pair_sum_rows(name, grad, received, core):
    _, r, c = received.shape
    tc = _fit(c, 512)

    def body(core_ref, a_ref, b_ref, o_ref):
        o_ref[...] = (a_ref[...] + b_ref[...]).astype(o_ref.dtype)

    spec = pl.BlockSpec((None, r, tc), lambda k, i, core_ref: (k, 0, i))
    return pl.pallas_call(
        body, name=name,
        grid_spec=pltpu.PrefetchScalarGridSpec(
            num_scalar_prefetch=1, grid=(N_CHIP, c // tc),
            in_specs=[pl.BlockSpec((r, tc), lambda k, i, core_ref: (2 * k + core_ref[0], i)), spec],
            out_specs=spec),
        out_shape=jax.ShapeDtypeStruct(received.shape, BF16),
        compiler_params=_params("parallel", "parallel"),
    )(core, grad, received)


def _pair_sum(name, grad, received, core):
    _, r, c = received.shape
    rows = min(ROWS, r)
    assert r % rows == 0

    def body(core_ref, a_ref, b_ref, o_ref):
        o_ref[...] = (a_ref[...].astype(F32) + b_ref[...].astype(F32)).astype(o_ref.dtype)

    spec = pl.BlockSpec((None, rows, c), lambda k, i, core_ref: (k, i, 0))
    return pl.pallas_call(
        body, name=name,
        grid_spec=pltpu.PrefetchScalarGridSpec(
            num_scalar_prefetch=1, grid=(N_CHIP, r // rows),
            in_specs=[pl.BlockSpec((None, None, rows, c), lambda k, i, core_ref: (k, core_ref[0], i, 0)), spec],
            out_specs=spec),
        out_shape=jax.ShapeDtypeStruct(received.shape, received.dtype),
        compiler_params=_params("parallel", "parallel"),
    )(core, grad.reshape(N_CHIP, 2, r, c), received)


def _away_shard(src, k, c, shard_rows):
    if shard_rows is None:
        return src.at[k]
    return src.at[pl.ds(pl.multiple_of((2 * k + 1 - c) * shard_rows, SUBLANE), shard_rows), :]


def _pair_send_start(name, away, shard_rows=None):
    shape = away.shape if shard_rows is None else (N_CHIP, shard_rows, away.shape[1])
    land = lax.empty(shape, away.dtype)

    def body(src, dst, send, recv, src_thru, dst_thru, token):
        x, y, c, _ = _place()
        for k in range(N_CHIP):
            pltpu.make_async_remote_copy(src_ref=_away_shard(src, k, c, shard_rows), dst_ref=dst.at[k], send_sem=send.at[k],
                                         recv_sem=recv.at[k], device_id=(x, y, 1 - c), device_id_type=MESH).start()
        token[...] = jnp.zeros_like(token)

    sem = pltpu.SemaphoreType.DMA((N_CHIP,))
    out = pl.pallas_call(
        body, name=name,
        in_specs=[HBM, HBM], out_specs=[SEM, SEM, HBM, HBM, IN_VMEM],
        out_shape=[sem, sem, _hbm_like(away), _hbm_like(land), jax.ShapeDtypeStruct((SUBLANE, LANE), F32)],
        input_output_aliases={0: 2, 1: 3},
        compiler_params=SPLIT,
    )(_in_hbm(away), _in_hbm(land))
    return (out[0], out[1]), out[2], out[3], out[4]


def _pair_send_wait(name, sems, src, land, after, shard_rows=None):
    def body(src_ref, dst_ref, send, recv, *rest):
        for k in range(N_CHIP):
            _await(dst_ref.at[k], send.at[k])
            _await(dst_ref.at[k], recv.at[k])

    return pl.pallas_call(
        body, name=name,
        in_specs=[HBM, HBM, SEM, SEM] + [ANY] * len(after), out_specs=HBM, out_shape=_hbm_like(land),
        input_output_aliases={1: 0},
        compiler_params=SPLIT,
    )(src, land, *sems, *after)


def _chip_send_start(name, sums):
    n = len(sums)
    lands = [lax.empty(a.shape, a.dtype) for a in sums]

    def body(*refs):
        src, land = refs[:n], refs[n:2 * n]
        send, recv, token = refs[2 * n], refs[2 * n + 1], refs[-1]
        x, y, c, chips = _place()
        for w in range(n):
            for j, (px, py) in enumerate(chips):
                pltpu.make_async_remote_copy(
                    src_ref=src[w].at[2 * px + py], dst_ref=land[w].at[2 * x + y],
                    send_sem=send.at[3 * w + j], recv_sem=recv.at[3 * w + j],
                    device_id=(px, py, c), device_id_type=MESH).start()
        token[...] = jnp.zeros_like(token)

    sem = pltpu.SemaphoreType.DMA((3 * n,))
    out = pl.pallas_call(
        body, name=name,
        in_specs=[HBM] * (2 * n),
        out_specs=[SEM, SEM] + [HBM] * (2 * n) + [IN_VMEM],
        out_shape=[sem, sem] + [_hbm_like(a) for a in sums] + [_hbm_like(a) for a in lands]
        + [jax.ShapeDtypeStruct((SUBLANE, LANE), F32)],
        input_output_aliases={i: 2 + i for i in range(2 * n)},
        compiler_params=SPLIT,
    )(*[_in_hbm(a) for a in sums], *[_in_hbm(a) for a in lands])
    return (out[0], out[1]), out[2:2 + n], out[2 + n:2 + 2 * n], out[-1]


def _chip_send_wait(name, groups, after):
    counts = [len(g[1]) for g in groups]
    n = sum(counts)

    def body(*refs):
        land = refs[n:2 * n]
        sems = refs[2 * n:2 * n + 2 * len(groups)]
        w = 0
        for gi, count in enumerate(counts):
            for i in range(count):
                for j in range(3):
                    _await(land[w].at[0], sems[2 * gi].at[3 * i + j])
                    _await(land[w].at[0], sems[2 * gi + 1].at[3 * i + j])
                w += 1

    sums = [a for g in groups for a in g[1]]
    lands = [a for g in groups for a in g[2]]
    sems = [s for g in groups for s in g[0]]
    return pl.pallas_call(
        body, name=name,
        in_specs=[HBM] * (2 * n) + [SEM] * len(sems) + [ANY] * len(after),
        out_specs=[HBM] * n, out_shape=[_hbm_like(a) for a in lands],
        input_output_aliases={n + i: i for i in range(n)},
        compiler_params=SPLIT,
    )(*sums, *lands, *sems, *after)


def _small_all_reduce(part, after=()):
    _, w = part.shape

    def body(p_ref, *rest):
        o_ref, buf, send_sems, recv_sems = rest[len(after):]
        x, y, c, _ = _place()
        me = 4 * x + 2 * y + c
        buf[me] = jnp.sum(p_ref[...], axis=0, keepdims=True)
        copies = []
        for k in range(1, N_DEV):
            dx, dy, dc = (k >> 2) & 1, (k >> 1) & 1, k & 1
            copies.append(pltpu.make_async_remote_copy(
                src_ref=buf.at[me], dst_ref=buf.at[me], send_sem=send_sems.at[k - 1], recv_sem=recv_sems.at[k - 1],
                device_id=(x ^ dx, y ^ dy, c ^ dc), device_id_type=MESH))
        for cp in copies:
            cp.start()
        for cp in copies:
            cp.wait()
        tot = buf[0]
        for d in range(1, N_DEV):
            tot = tot + buf[d]
        o_ref[...] = tot
        loss = jnp.sum(tot[:, w - LANE:], axis=1, keepdims=True)
        o_ref[:, w - LANE:] = jnp.broadcast_to(loss, (1, LANE))

    return pl.pallas_call(
        body, name="small_all_reduce",
        in_specs=[IN_VMEM] + [ANY] * len(after), out_specs=IN_VMEM,
        out_shape=jax.ShapeDtypeStruct((1, w), F32),
        scratch_shapes=[pltpu.VMEM((N_DEV, 1, w), F32), pltpu.SemaphoreType.DMA((N_DEV - 1,)), pltpu.SemaphoreType.DMA((N_DEV - 1,))],
        compiler_params=pltpu.CompilerParams(vmem_limit_bytes=VMEM_LIMIT_BYTES),
    )(part, *after)


def _adamw(w, g, m, v):
    m = ADAM_B1 * m + (1.0 - ADAM_B1) * g
    v = ADAM_B2 * v + (1.0 - ADAM_B2) * (g * g)
    m_hat = m / (1.0 - ADAM_B1 ** ADAM_STEP)
    v_hat = v / (1.0 - ADAM_B2 ** ADAM_STEP)
    delta = -ADAM_LR * (m_hat / (jnp.sqrt(v_hat) + ADAM_EPS) + ADAM_WD * w)
    return delta, m, v


def _sum_adam_block(chip_ref, p_ref, own_ref, w_ref, m_ref, v_ref, g_ref, d_ref, mo_ref, vo_ref):
    g = None
    for k in range(N_CHIP):
        term = jnp.where(chip_ref[0] == k, own_ref[...], p_ref[k]).astype(F32)
        g = term if g is None else g + term
    g_ref[...] = g
    d_ref[...], mo_ref[...], vo_ref[...] = _adamw(w_ref[...], g, m_ref[...], v_ref[...])


def _sum_adam(name, parts, sums, chip, w, m, v, after=()):
    _, r, c = w.shape
    n_after = len(after)
    by_rows = r % ROWS == 0 or r < ROWS
    tr, tc = (min(ROWS, r), c) if by_rows else (r, _fit(c, 512))
    at = (lambda i: (i, 0)) if by_rows else (lambda i: (0, i))

    def body(chip_ref, p_ref, own_ref, w_ref, m_ref, v_ref, *rest):
        _sum_adam_block(chip_ref, p_ref, own_ref, w_ref, m_ref, v_ref, *rest[n_after:])

    blk = pl.BlockSpec((None, tr, tc), lambda i, chip_ref: (0, *at(i)))
    out = jax.ShapeDtypeStruct((1, r, c), F32)
    return pl.pallas_call(
        body, name=name,
        grid_spec=pltpu.PrefetchScalarGridSpec(
            num_scalar_prefetch=1, grid=(r // tr if by_rows else c // tc,),
            in_specs=[pl.BlockSpec((N_CHIP, tr, tc), lambda i, chip_ref: (0, *at(i))),
                      pl.BlockSpec((None, tr, tc), lambda i, chip_ref: (chip_ref[0], *at(i))), blk, blk, blk]
            + [ANY] * n_after,
            out_specs=[blk] * 4),
        out_shape=[out] * 4,
        compiler_params=_params("parallel"),
    )(chip, parts, sums, w, m, v, *after)


def _adam_gains(total, ws, ms, vs):
    n = len(ws)
    widths = [w.shape[1] for w in ws]

    def body(t_ref, *refs):
        w_refs, m_refs, v_refs, outs = refs[:n], refs[n:2 * n], refs[2 * n:3 * n], refs[3 * n:]
        off = 0
        for i in range(n):
            g = t_ref[:, off:off + widths[i]]
            off += widths[i]
            g_ref, d_ref, mo_ref, vo_ref = outs[4 * i:4 * i + 4]
            g_ref[...] = g
            d_ref[...], mo_ref[...], vo_ref[...] = _adamw(w_refs[i][...], g, m_refs[i][...], v_refs[i][...])

    out = pl.pallas_call(
        body, name="adam_gains",
        out_shape=[jax.ShapeDtypeStruct(w.shape, F32) for w in ws for _ in range(4)],
    )(total, *ws, *ms, *vs)
    return [tuple(out[4 * i:4 * i + 4]) for i in range(n)]


def _adam_taps(total, first_col, device, w, m, v):
    _, n_taps, cw = w.shape
    col_block = lambda t, dev: (0, first_col // cw + t * N_DEV + dev[0])
    tap = pl.BlockSpec((None, 1, cw), lambda t, dev: (t, 0, 0))

    def body(dev_ref, t_ref, w_ref, m_ref, v_ref, g_ref, d_ref, mo_ref, vo_ref):
        g = t_ref[...]
        g_ref[...] = g
        d_ref[...], mo_ref[...], vo_ref[...] = _adamw(w_ref[...], g, m_ref[...], v_ref[...])

    shape3 = (n_taps, 1, cw)
    out = pl.pallas_call(
        body, name="adam_taps",
        grid_spec=pltpu.PrefetchScalarGridSpec(
            num_scalar_prefetch=1, grid=(n_taps,),
            in_specs=[pl.BlockSpec((1, cw), col_block), tap, tap, tap], out_specs=[tap] * 4),
        out_shape=[jax.ShapeDtypeStruct(shape3, F32)] * 4,
    )(device, total, w.reshape(shape3), m.reshape(shape3), v.reshape(shape3))
    return tuple(o.reshape(w.shape) for o in out)


def kernel(x, pre_mix_g, w_in, conv_w, q_norm_g, w_uq, kv_norm_g, w_ukv, conv_out_g, attn_out_g, w_o, post_mix_g, pre_mlp_g, w_up, w_down, post_mlp_g, loss_target, m_pre_mix_g, m_w_in, m_conv_w, m_q_norm_g, m_w_uq, m_kv_norm_g, m_w_ukv, m_conv_out_g, m_attn_out_g, m_w_o, m_post_mix_g, m_pre_mlp_g, m_w_up, m_w_down, m_post_mlp_g, v_pre_mix_g, v_w_in, v_conv_w, v_q_norm_g, v_w_uq, v_kv_norm_g, v_w_ukv, v_conv_out_g, v_attn_out_g, v_w_o, v_post_mix_g, v_pre_mlp_g, v_w_up, v_w_down, v_post_mlp_g):
    me = 4 * lax.axis_index("x") + 2 * lax.axis_index("y") + lax.axis_index("c")
    core = lax.axis_index("c").astype(jnp.int32).reshape(1)
    chip = (2 * lax.axis_index("x") + lax.axis_index("y")).astype(jnp.int32).reshape(1)
    gains = (pre_mix_g, q_norm_g, kv_norm_g, conv_out_g, attn_out_g, post_mix_g, pre_mlp_g, post_mlp_g)
    gain_m = (m_pre_mix_g, m_q_norm_g, m_kv_norm_g, m_conv_out_g, m_attn_out_g, m_post_mix_g, m_pre_mlp_g, m_post_mlp_g)
    gain_v = (v_pre_mix_g, v_q_norm_g, v_kv_norm_g, v_conv_out_g, v_attn_out_g, v_post_mix_g, v_pre_mlp_g, v_post_mlp_g)
    names = ("w_in", "w_uq", "w_ukv", "w_o", "w_up", "w_down")
    big = dict(zip(names, (w_in, w_uq, w_ukv, w_o, w_up, w_down)))
    big_m = dict(zip(names, (m_w_in, m_w_uq, m_w_ukv, m_w_o, m_w_up, m_w_down)))
    big_v = dict(zip(names, (v_w_in, v_w_uq, v_w_ukv, v_w_o, v_w_up, v_w_down)))
    n_heads = attn_out_g.shape[1] // HEAD
    n_taps = conv_w.shape[1]

    gathered = ("w_in", "conv", "w_uq", "w_ukv", "w_o", "w_up", "w_down")
    gather_groups = ((0, 1), (2, 3, 4), (5,), (6,))
    taps = jnp.pad(conv_w[0], ((0, SUBLANE - n_taps), (0, 0)))
    relayed_groups = (0, 2, 3)
    sems1, shards, lands, token = _gather_start("gather_start_first", [w_in[0].astype(BF16), taps], ((0, 1),), relayed=(0,))
    sems1, shards, lands = list(sems1), list(shards), list(lands)
    behind = token[0, 0]
    rest = list(lax.optimization_barrier(tuple((big[nm][0] + behind).astype(BF16) for nm in gathered[2:])))

    rest_lands = [lax.dynamic_update_index_in_dim(lax.empty((N_DEV, *a.shape), a.dtype), a, me, 0) for a in rest]

    def start_more(name, some, groups, relayed, after):
        sems_b, shards_b, lands_b, started = _gather_start(name, rest[some], groups, relayed=relayed, after=after,
                                                           lands=rest_lands[some])
        sems1.extend(sems_b)
        shards.extend(shards_b)
        lands.extend(lands_b)
        return started

    start_rest = lambda after: start_more("gather_start_rest", slice(0, 4), ((0, 1, 2), (3,)), (1,), after)
    start_last = lambda after: start_more("gather_start_last", slice(4, 5), ((0,),), (0,), after)

    cols = lambda a: jnp.concatenate([a[j] for j in range(N_DEV)], axis=1)
    rows = lambda a: a.reshape(N_DEV * a.shape[1], a.shape[2])
    device = me.astype(jnp.int32).reshape(1)
    own_in = lambda a, shard: lax.dynamic_update_index_in_dim(a, shard, me, 0)
    q_pieces = [(h, 0, HEAD) for h in range(n_heads)] + [(h, HEAD, QK) for h in range(n_heads)]
    ready = {
        "w_in": lambda a, shard: _join_col_shards("join_w_in", a, shard, device),
        "conv": lambda a, shard: cols(own_in(a, shard))[:n_taps],
        "w_uq": lambda a, shard: _join_col_shards("join_w_uq", a, shard, device, q_pieces),
        "w_ukv": lambda a, shard: cols(a),
        "w_o": lambda a, shard: rows(a),
        "w_up": lambda a, shard: a,
        "w_down": lambda a, shard: rows(a),
    }
    assert w_uq.shape[2] == QK

    class Weights:
        def __init__(self):
            self.passed, self.relayed = {}, {}

        def forward(self, group, after):
            idx = gather_groups[group]
            if group == 0:
                after = (*after, *rest_lands)
            self.passed[group] = _gather_forward(f"gather_forward_{group}", [shards[i] for i in idx], [lands[i] for i in idx],
                                                 *sems1[group], after, relayed=group in relayed_groups)
            return tuple(self.passed[group][1])

        def start(self, group, after):
            assert group == len(gather_groups) - 1
            return (start_last(after),)

        def relay(self, group, after):
            sems2, mid = self.passed[group]
            self.relayed[group], mid = _gather_relay_forward(f"gather_relay_{group}", mid, *sems2, after)
            self.passed[group] = (sems2, mid)
            if group == 0:
                start_rest(tuple(mid))
            return tuple(mid)

        def ready(self, group, after):
            sems2, mid = self.passed[group]
            full = _gather_wait(f"gather_wait_{group}", mid, *sems2, after, relay_sems=self.relayed.get(group))
            out = []
            return [ready[gathered[i]](a, shards[i]) for i, a in zip(gather_groups[group], full)]

    weights = Weights()

    col_blocks = lambda g: g.reshape(g.shape[0], N_DEV, g.shape[1] // N_DEV).transpose(1, 0, 2)
    row_blocks = lambda g: g.reshape(N_DEV, g.shape[0] // N_DEV, g.shape[1])
    grad_groups = (("w_down",), ("w_up",), ("w_o", "w_uq", "w_ukv"), ("w_in",))
    transposed = {"w_in": w_in.shape[2], "w_uq": w_uq.shape[2]}
    to_blocks = {
        "w_in": lambda g: g, "w_uq": lambda g: _unpermute_q_rows(g, n_heads),
        "w_ukv": col_blocks, "w_o": row_blocks, "w_up": lambda g: g, "w_down": row_blocks,
    }
    in_flight = []

    class Grads:
        def __init__(self):
            self.core = core
            self.away = {}

        def send_sums(self, group, sums):
            sems, sums, parts, tok = _chip_send_start(f"chip_send_start_{group}", list(sums))
            in_flight.append((sems, sums, parts))
            return (tok,)

        def full(self, group, arrays, received=None):
            nms = grad_groups[group]
            if received is None:
                blocks = [to_blocks[nm](g) for nm, g in zip(nms, arrays)]
                got = _pair_exchange(f"pair_exchange_{group}", blocks, [transposed.get(nm) for nm in nms])
            else:
                blocks, got = [self.away[group][1]], [self.received(group, received)]
            sums = [(_pair_sum_rows if nm in transposed else _pair_sum)(f"pair_sum_{nm}", g, r, core)
                    for nm, g, r in zip(nms, blocks, got)]
            return self.send_sums(group, sums)

        def send_away(self, group, half):
            nm = grad_groups[group][0]
            rows = transposed.get(nm)
            sems, src, land, tok = _pair_send_start(f"pair_send_start_{group}", half if rows is None else to_blocks[nm](half), rows)
            self.away[group] = (sems, src, land, rows)
            return (tok,)

        def received(self, group, after):
            sems, src, land, rows = self.away[group]
            return _pair_send_wait(f"pair_send_wait_{group}", sems, src, land, after, rows)

    big_out = {}

    def update(tag, first, last, after):
        picked = [i for i in range(first, last) if grad_groups[i][0] not in big_out]
        groups = [in_flight[i] for i in picked]
        parts = _chip_send_wait("chip_send_wait_" + tag, groups, after)
        nms = [nm for i in picked for nm in grad_groups[i]]
        sums = [a for _, s, _ in groups for a in s]
        for nm, p, s in zip(nms, parts, sums):
            view = (lambda a: jnp.swapaxes(a, 1, 2)) if nm in transposed else (lambda a: a)
            out = _sum_adam("adam_" + nm, p, s, chip, view(big[nm]), view(big_m[nm]), view(big_v[nm]), after=after)
            after = (out[0],)
            big_out[nm] = [view(o) for o in out]
        return after

    grad_x, small = _local_step(x[0], loss_target[0], gains, weights, Grads(), first_after=(token,))

    after = update("early", 0, len(in_flight) - 1, (grad_x,))
    total = _small_all_reduce(small, after=after)
    update("late", len(in_flight) - 1, len(in_flight), (total,))
    big_out = [big_out[nm] for nm in names]

    gain_out = _adam_gains(total, gains, gain_m, gain_v)
    taps_out = _adam_taps(total, sum(g.shape[1] for g in gains), me.astype(jnp.int32).reshape(1), conv_w, m_conv_w, v_conv_w)
    loss = total[0, total.shape[1] - 1]

    order = (0, "w_in", "conv", 1, "w_uq", 2, "w_ukv", 3, 4, "w_o", 5, 6, "w_up", "w_down", 7)
    by_name = dict(zip(names, big_out))
    outs = [loss, grad_x[None]]
    for kind in range(4):
        for item in order:
            if item == "conv":
                outs.append(taps_out[kind])
            elif isinstance(item, int):
                outs.append(gain_out[item][kind])
            else:
                outs.append(by_name[item][kind])
    return tuple(outs)
```

```python
import math

import jax
import jax.numpy as jnp
from jax import lax
from jax.experimental import pallas as pl
from jax.experimental.pallas import tpu as pltpu

F32 = jnp.float32
BF16 = jnp.bfloat16

EPS = 1e-6
NEG_INF = -1e30
HEAD = 128
ROPE = 64
QK = HEAD + ROPE
CHUNK = 64
ROPE_THETA = 10000.0
ADAM_LR, ADAM_B1, ADAM_B2, ADAM_EPS, ADAM_WD, ADAM_STEP = 0.001, 0.9, 0.999, 1e-08, 0.01, 10

LANE = 128
SUBLANE = 8
VMEM_LIMIT_BYTES = 56 * 1024 * 1024

N_DEV = 8
N_CHIP = 4
MESH = pl.DeviceIdType.MESH


def _params(*sem):
    return pltpu.CompilerParams(dimension_semantics=sem, vmem_limit_bytes=VMEM_LIMIT_BYTES)


ANY = pl.BlockSpec(memory_space=pl.ANY)


def _call(body, *, in_specs, after=(), **kw):
    n_in, n_after = len(in_specs), len(after)

    def ordered(*refs):
        body(*refs[:n_in], *refs[n_in + n_after:])

    call = pl.pallas_call(ordered, in_specs=[*in_specs, *[ANY] * n_after], **kw)
    return lambda *operands: call(*operands, *after)


def _sublane_sum(v):
    r, w = v.shape
    return jnp.sum(v.reshape(r // SUBLANE, SUBLANE, w), axis=0)


def _rstd(x):
    return lax.rsqrt(jnp.mean(x * x, axis=-1, keepdims=True) + EPS)


def _rms_bwd(x, g, dy):
    r = _rstd(x)
    xh = x * r
    dxh = dy * g
    dx = r * (dxh - xh * jnp.mean(dxh * xh, axis=-1, keepdims=True))
    return dx, dy * xh


def _accumulate(ref, val, step):
    @pl.when(step == 0)
    def _():
        ref[...] = val

    @pl.when(step > 0)
    def _():
        ref[...] += val


NN = ((1,), (0,))
NT = ((1,), (1,))
TN = ((0,), (0,))


def _matmul(name, a, b, *, grid, a_spec, b_spec, out_shape, out_specs, contract, nk=1, acc_shape=None,
            extras=(), extra_specs=(), epilogue=None, after=()):
    multi = isinstance(out_shape, (tuple, list))
    out_shapes = tuple(out_shape) if multi else (out_shape,)
    n_out = len(out_shapes)
    n_extra = len(extras)

    def body(a_ref, b_ref, *rest):
        x_refs = rest[:n_extra]
        o_refs = rest[n_extra:n_extra + n_out]

        def emit(acc):
            vals = epilogue(acc, *[r[...] for r in x_refs]) if epilogue else (acc,)
            for r, v in zip(o_refs, vals):
                r[...] = v.astype(r.dtype)

        p = lax.dot_general(a_ref[...], b_ref[...], (contract, ((), ())), preferred_element_type=F32)
        if nk == 1:
            emit(p)
        else:
            acc_ref = rest[n_extra + n_out]
            k = pl.program_id(2)
            _accumulate(acc_ref, p, k)

            @pl.when(k == nk - 1)
            def _():
                emit(acc_ref[...])

    sem = ("parallel", "parallel") + (("arbitrary",) if nk > 1 else ())
    return _call(
        body, name=name, grid=grid, after=after,
        in_specs=[a_spec, b_spec, *extra_specs],
        out_specs=out_specs,
        out_shape=out_shape,
        scratch_shapes=[pltpu.VMEM(acc_shape, F32)] if nk > 1 else [],
        compiler_params=_params(*sem),
    )(a, b, *extras)


def _fit(n, tile):
    if n <= tile:
        return n
    t = tile - tile % LANE
    while n % t:
        t -= LANE
    return t


def _mm_nn(name, a, b, out_dtype, tm, tn, after=()):
    m, k = a.shape
    n = b.shape[1]
    tm, tn = _fit(m, tm), _fit(n, tn)
    return _matmul(name, a, b, grid=(m // tm, n // tn), after=after,
                   a_spec=pl.BlockSpec((tm, k), lambda i, j: (i, 0)),
                   b_spec=pl.BlockSpec((k, tn), lambda i, j: (0, j)),
                   out_shape=jax.ShapeDtypeStruct((m, n), out_dtype),
                   out_specs=pl.BlockSpec((tm, tn), lambda i, j: (i, j)), contract=NN)


def _mm_nt(name, a, b, out_dtype, tm, tn, after=()):
    m, k = a.shape
    n = b.shape[0]
    tm, tn = _fit(m, tm), _fit(n, tn)
    return _matmul(name, a, b, grid=(m // tm, n // tn), after=after,
                   a_spec=pl.BlockSpec((tm, k), lambda i, j: (i, 0)),
                   b_spec=pl.BlockSpec((tn, k), lambda i, j: (j, 0)),
                   out_shape=jax.ShapeDtypeStruct((m, n), out_dtype),
                   out_specs=pl.BlockSpec((tm, tn), lambda i, j: (i, j)), contract=NT)


def _mm_tn(name, a, b, out_dtype, tm, tn):
    s, m = a.shape
    n = b.shape[1]
    tm, tn = _fit(m, tm), _fit(n, tn)
    return _matmul(name, a, b, grid=(m // tm, n // tn),
                   a_spec=pl.BlockSpec((s, tm), lambda i, j: (0, i)),
                   b_spec=pl.BlockSpec((s, tn), lambda i, j: (0, j)),
                   out_shape=jax.ShapeDtypeStruct((m, n), out_dtype),
                   out_specs=pl.BlockSpec((tm, tn), lambda i, j: (i, j)), contract=TN)


ROWS = 256


def _row_spec(rows, width):
    return pl.BlockSpec((rows, width), lambda i: (i, 0))


def _row_in_spec(rows, width):
    return pl.BlockSpec((rows, width), lambda i: (i, 0), pipeline_mode=pl.Buffered(3))


def _fixed_spec(rows, width):
    return pl.BlockSpec((rows, width), lambda i: (0, 0))


def _column_pieces(rows, start, width):
    piece = math.gcd(start, width)
    assert piece % LANE == 0
    return [pl.BlockSpec((rows, piece), lambda i, b=start // piece + p: (i, b)) for p in range(width // piece)]


def _rms_fwd(name, x, g, after=()):
    s, w = x.shape
    rows = min(ROWS, s)

    def body(x_ref, g_ref, o_ref):
        xv = x_ref[...]
        o_ref[...] = (xv * _rstd(xv) * g_ref[...]).astype(o_ref.dtype)

    return _call(
        body, name=name, grid=(s // rows,), after=after,
        in_specs=[_row_spec(rows, w), _fixed_spec(1, w)],
        out_specs=_row_spec(rows, w),
        out_shape=jax.ShapeDtypeStruct((s, w), BF16),
        compiler_params=_params("parallel"),
    )(x, g)


def _norm_up(name, x, cols, g, w, after=()):
    s = x.shape[0]
    start, width = cols
    n = w.shape[1]
    tm = min(TILE_M, s)
    pieces = _column_pieces(tm, start, width)
    n_p = len(pieces)

    def body(*refs):
        g_ref, w_ref, xn_ref, o_ref = refs[n_p:]
        xv = refs[0][...] if n_p == 1 else jnp.concatenate([r[...] for r in refs[:n_p]], axis=1)
        xn = (xv * _rstd(xv) * g_ref[...]).astype(BF16)
        xn_ref[...] = xn
        o_ref[...] = jnp.dot(xn, w_ref[...], preferred_element_type=F32)

    return _call(
        body, name=name, grid=(s // tm,), after=after,
        in_specs=[*pieces, _fixed_spec(1, width), _fixed_spec(width, n)],
        out_specs=[_row_spec(tm, width), _row_spec(tm, n)],
        out_shape=[jax.ShapeDtypeStruct((s, width), BF16), jax.ShapeDtypeStruct((s, n), F32)],
        compiler_params=_params("parallel"),
    )(*[x] * n_p, g, w)


def _up_norm_bwd(name, dy, w, x, cols, g, into, tail=None, after=()):
    s, n = dy.shape
    start, width = cols
    tm = min(TILE_M, s)
    pieces = _column_pieces(tm, start, width)
    n_p = len(pieces)
    tails = () if tail is None else (tail,)
    out_width = width if tail is None else into.shape[1] - start
    assert start % out_width == 0 and into.dtype == BF16

    def body(dy_ref, w_ref, *refs):
        g_ref = refs[n_p]
        dx_ref, dg_ref = refs[-2:]
        xv = refs[0][...] if n_p == 1 else jnp.concatenate([r[...] for r in refs[:n_p]], axis=1)
        dxn = lax.dot_general(dy_ref[...], w_ref[...], (NT, ((), ())), preferred_element_type=F32)
        dx, dgc = _rms_bwd(xv, g_ref[...], dxn)
        dx_ref[:, :width] = dx.astype(dx_ref.dtype)
        if tails:
            t = refs[n_p + 1][...]
            dx_ref[:, width:width + t.shape[1]] = t
            dx_ref[:, width + t.shape[1]:] = jnp.zeros((tm, out_width - width - t.shape[1]), dx_ref.dtype)
        _accumulate(dg_ref, _sublane_sum(dgc), pl.program_id(0))

    n_in = 3 + n_p + len(tails)
    return _call(
        body, name=name, grid=(s // tm,), after=after,
        in_specs=[_row_spec(tm, n), _fixed_spec(width, n), *pieces, _fixed_spec(1, width)]
                 + [_row_spec(tm, t.shape[1]) for t in tails] + [ANY],
        out_specs=[pl.BlockSpec((tm, out_width), lambda i: (i, start // out_width)), _fixed_spec(SUBLANE, width)],
        out_shape=[jax.ShapeDtypeStruct(into.shape, into.dtype), jax.ShapeDtypeStruct((SUBLANE, width), F32)],
        input_output_aliases={n_in: 0},
        compiler_params=_params("arbitrary"),
    )(dy, w, *[x] * n_p, g, *tails, into)


def _mid_fwd(x, y, g_post, g_pre, after=()):
    s, w = x.shape
    rows = min(ROWS, s)

    def body(x_ref, y_ref, gp_ref, gq_ref, x2_ref, h2_ref):
        yv = y_ref[...]
        x2 = x_ref[...] + yv * _rstd(yv) * gp_ref[...]
        x2_ref[...] = x2
        h2_ref[...] = (x2 * _rstd(x2) * gq_ref[...]).astype(h2_ref.dtype)

    return _call(
        body, name="mid_fwd", grid=(s // rows,), after=after,
        in_specs=[_row_spec(rows, w), _row_spec(rows, w), _fixed_spec(1, w), _fixed_spec(1, w)],
        out_specs=[_row_spec(rows, w), _row_spec(rows, w)],
        out_shape=[jax.ShapeDtypeStruct((s, w), F32), jax.ShapeDtypeStruct((s, w), BF16)],
        compiler_params=_params("parallel"),
    )(x, y, g_post, g_pre)


def _head(m, x2, tgt, g):
    s, w = m.shape
    rows = min(ROWS, s)

    def body(m_ref, x2_ref, t_ref, g_ref, dout_ref, dm_ref, dg_ref, loss_ref):
        mv = m_ref[...]
        gv = g_ref[...]
        out = x2_ref[...] + mv * _rstd(mv) * gv
        err = out - t_ref[...]
        dout = err * (1.0 / w)
        dout_ref[...] = dout
        dm, dgc = _rms_bwd(mv, gv, dout)
        dm_ref[...] = dm.astype(dm_ref.dtype)
        sq = err * err
        lanes = sq[:, 0:LANE]
        for j in range(1, w // LANE):
            lanes = lanes + sq[:, j * LANE:(j + 1) * LANE]
        step = pl.program_id(0)
        _accumulate(dg_ref, _sublane_sum(dgc), step)
        _accumulate(loss_ref, _sublane_sum(lanes) * (0.5 / w), step)

    return pl.pallas_call(
        body, name="head", grid=(s // rows,),
        in_specs=[_row_spec(rows, w), _row_spec(rows, w), _row_spec(rows, w), _fixed_spec(1, w)],
        out_specs=[_row_spec(rows, w), _row_spec(rows, w), _fixed_spec(SUBLANE, w), _fixed_spec(SUBLANE, LANE)],
        out_shape=[jax.ShapeDtypeStruct((s, w), F32), jax.ShapeDtypeStruct((s, w), BF16),
                   jax.ShapeDtypeStruct((SUBLANE, w), F32), jax.ShapeDtypeStruct((SUBLANE, LANE), F32)],
        compiler_params=_params("arbitrary"),
    )(m, x2, tgt, g)


def _mid_bwd(x2, y, d_out, d_h2, g_pre, g_post, after=()):
    s, w = x2.shape
    rows = min(ROWS, s)

    def body(x2_hbm, y_hbm, dout_hbm, dh2_hbm, gq_ref, gp_ref, dx2_hbm, dy_hbm, dgq_ref, dgp_ref):
        dgq_ref[...] = jnp.zeros_like(dgq_ref)
        dgp_ref[...] = jnp.zeros_like(dgp_ref)

        def rows_step(x2_ref, y_ref, dout_ref, dh2_ref, dx2_ref, dy_ref):
            dx, dgq = _rms_bwd(x2_ref[...], gq_ref[...], dh2_ref[...])
            dx2 = dout_ref[...] + dx
            dx2_ref[...] = dx2
            dy, dgp = _rms_bwd(y_ref[...], gp_ref[...], dx2)
            dy_ref[...] = dy.astype(dy_ref.dtype)
            dgq_ref[...] += _sublane_sum(dgq)
            dgp_ref[...] += _sublane_sum(dgp)

        pltpu.emit_pipeline(rows_step, grid=(s // rows,), in_specs=[_row_in_spec(rows, w)] * 4,
                            out_specs=[_row_spec(rows, w)] * 2)(x2_hbm, y_hbm, dout_hbm, dh2_hbm, dx2_hbm, dy_hbm)

    return _call(
        body, name="mid_bwd", after=after,
        in_specs=[ANY] * 4 + [IN_VMEM] * 2,
        out_specs=[ANY, ANY, IN_VMEM, IN_VMEM],
        out_shape=[jax.ShapeDtypeStruct((s, w), F32), jax.ShapeDtypeStruct((s, w), BF16),
                   jax.ShapeDtypeStruct((SUBLANE, w), F32), jax.ShapeDtypeStruct((SUBLANE, w), F32)],
        compiler_params=pltpu.CompilerParams(vmem_limit_bytes=VMEM_LIMIT_BYTES),
    )(x2, y, d_out, d_h2, g_pre, g_post)


def _first_bwd(x, g, d_h1, d_x2, after=()):
    s, w = x.shape
    rows = min(ROWS, s)

    def body(x_hbm, g_ref, dh_hbm, dx2_hbm, dx_hbm, dg_ref):
        dg_ref[...] = jnp.zeros_like(dg_ref)

        def rows_step(x_ref, dh_ref, dx2_ref, dx_ref):
            dx, dgc = _rms_bwd(x_ref[...], g_ref[...], dh_ref[...])
            dx_ref[...] = dx2_ref[...] + dx
            dg_ref[...] += _sublane_sum(dgc)

        pltpu.emit_pipeline(rows_step, grid=(s // rows,), in_specs=[_row_in_spec(rows, w)] * 3,
                            out_specs=[_row_spec(rows, w)])(x_hbm, dh_hbm, dx2_hbm, dx_hbm)

    return _call(
        body, name="first_bwd", after=after,
        in_specs=[ANY, IN_VMEM, ANY, ANY], out_specs=[ANY, IN_VMEM],
        out_shape=[jax.ShapeDtypeStruct((s, w), F32), jax.ShapeDtypeStruct((SUBLANE, w), F32)],
        compiler_params=pltpu.CompilerParams(vmem_limit_bytes=VMEM_LIMIT_BYTES),
    )(x, g, d_h1, d_x2)


def _shift_down(v, k):
    t = lax.broadcasted_iota(jnp.int32, v.shape, 0)
    return jnp.where(t >= k, pltpu.roll(v, k, 0), 0.0)


def _shift_up(v, k):
    n = v.shape[0]
    t = lax.broadcasted_iota(jnp.int32, v.shape, 0)
    return jnp.where(t < n - k, pltpu.roll(v, n - k, 0), 0.0)


def _conv_core(u, b, c, w):
    z = c * u
    conv = w[0:1, :] * _shift_down(z, 2) + w[1:2, :] * _shift_down(z, 1) + w[2:3, :] * z
    return z, conv, b * conv


def _conv_fwd(proj, conv_w, g, n_groups, out_width, after=()):
    s = proj.shape[0]

    def body(u_ref, b_ref, c_ref, w_ref, g_ref, o_ref):
        _, _, yr = _conv_core(u_ref[...], b_ref[...], c_ref[...], w_ref[...])
        o_ref[...] = (yr * _rstd(yr) * g_ref[...]).astype(o_ref.dtype)

    col = lambda k: pl.BlockSpec((s, HEAD), lambda i: (0, k * n_groups + i))
    return _call(
        body, name="conv_fwd", grid=(n_groups,), after=after,
        in_specs=[col(0), col(1), col(2), pl.BlockSpec((3, HEAD), lambda i: (0, i)), pl.BlockSpec((1, HEAD), lambda i: (0, i))],
        out_specs=pl.BlockSpec((s, HEAD), lambda i: (0, i)),
        out_shape=jax.ShapeDtypeStruct((s, out_width), BF16),
        compiler_params=_params("parallel"),
    )(proj, proj, proj, conv_w, g)


def _conv_bwd(proj, d_mix, conv_w, g, n_groups, out_width):
    s = proj.shape[0]
    width = n_groups * HEAD

    def body(u_ref, b_ref, c_ref, dy_ref, w_ref, g_ref, dproj_ref, dg_ref, dw_ref, buf, sems):
        i = pl.program_id(0)
        slot = i % 2

        def copies(group, slot):
            return [pltpu.make_async_copy(buf.at[slot, k], dproj_ref.at[:, pl.ds(pl.multiple_of((k * n_groups + group) * HEAD, HEAD), HEAD)],
                                          sems.at[slot, k]) for k in range(3)]

        @pl.when(i >= 2)
        def _():
            for cp in copies(i - 2, slot):
                cp.wait()

        u, b, c, w = u_ref[...], b_ref[...], c_ref[...], w_ref[...]
        z, conv, yr = _conv_core(u, b, c, w)
        dyr, dgc = _rms_bwd(yr, g_ref[...], dy_ref[...])
        dconv = dyr * b
        dz = w[2:3, :] * dconv + w[1:2, :] * _shift_up(dconv, 1) + w[0:1, :] * _shift_up(dconv, 2)
        buf[slot, 0] = (dz * c).astype(buf.dtype)
        buf[slot, 1] = (dyr * conv).astype(buf.dtype)
        buf[slot, 2] = (dz * u).astype(buf.dtype)
        for cp in copies(i, slot):
            cp.start()
        dg_ref[...] = _sublane_sum(dgc)
        dw_ref[0] = _sublane_sum(dconv * _shift_down(z, 2))
        dw_ref[1] = _sublane_sum(dconv * _shift_down(z, 1))
        dw_ref[2] = _sublane_sum(dconv * z)

        @pl.when(i == n_groups - 1)
        def _():
            for back in range(min(2, n_groups)):
                for cp in copies(i - back, (n_groups - 1 - back) % 2):
                    cp.wait()

    col = lambda k: pl.BlockSpec((s, HEAD), lambda i: (0, k * n_groups + i))
    grp = pl.BlockSpec((s, HEAD), lambda i: (0, i))
    return pl.pallas_call(
        body, name="conv_bwd", grid=(n_groups,),
        in_specs=[col(0), col(1), col(2), grp, pl.BlockSpec((3, HEAD), lambda i: (0, i)), pl.BlockSpec((1, HEAD), lambda i: (0, i))],
        out_specs=[ANY, pl.BlockSpec((SUBLANE, HEAD), lambda i: (0, i)), pl.BlockSpec((3, SUBLANE, HEAD), lambda i: (0, 0, i))],
        out_shape=[jax.ShapeDtypeStruct((s, out_width), BF16),
                   jax.ShapeDtypeStruct((SUBLANE, width), F32), jax.ShapeDtypeStruct((3, SUBLANE, width), F32)],
        scratch_shapes=[pltpu.VMEM((2, 3, s, HEAD), BF16), pltpu.SemaphoreType.DMA((2, 3))],
        compiler_params=_params("arbitrary"),
    )(proj, proj, proj, d_mix, conv_w, g)


def _rope_tables(s, n_heads):
    pos = jnp.arange(s, dtype=F32)
    inv_freq = jnp.power(ROPE_THETA, -jnp.arange(0, ROPE, 2, dtype=F32) / ROPE)
    ang = pos[:, None] * inv_freq[None, :]
    cos, sin = jnp.cos(ang), jnp.sin(ang)
    cs = jnp.concatenate([cos, cos], axis=1)
    sn = jnp.concatenate([-sin, sin], axis=1)
    pad = jnp.zeros((s, LANE - ROPE), F32)
    return (jnp.tile(cs, (1, n_heads)), jnp.tile(sn, (1, n_heads)),
            jnp.concatenate([cs, pad], axis=1), jnp.concatenate([sn, pad], axis=1))


def _swap_halves(v):
    w = v.shape[1]
    lane = lax.broadcasted_iota(jnp.int32, v.shape, 1)
    first = (lane % ROPE) < (ROPE // 2)
    return jnp.where(first, pltpu.roll(v, w - ROPE // 2, 1), pltpu.roll(v, ROPE // 2, 1))


def _pack_heads(q, kv, proj, kr_col, tables, n_heads, after=()):
    s = q.shape[0]
    rows = min(ROWS, s)
    cq, sq, ck, sk = tables
    wq = n_heads * ROPE

    def body(q_ref, kv_ref, kr_ref, cq_ref, sq_ref, ck_ref, sk_ref, qo_ref, ko_ref, vo_ref):
        qr = q_ref[:, n_heads * HEAD:]
        qr = qr * cq_ref[...] + _swap_halves(qr) * sq_ref[...]
        krv = kr_ref[...]
        krv = krv * ck_ref[...] + _swap_halves(krv) * sk_ref[...]
        for h in range(n_heads):
            qo_ref[h] = jnp.concatenate([q_ref[:, h * HEAD:(h + 1) * HEAD], qr[:, h * ROPE:(h + 1) * ROPE]], axis=1).astype(BF16)
            ko_ref[h] = jnp.concatenate([kv_ref[:, 2 * h * HEAD:(2 * h + 1) * HEAD], krv[:, :ROPE]], axis=1).astype(BF16)
            vo_ref[h] = kv_ref[:, (2 * h + 1) * HEAD:(2 * h + 2) * HEAD].astype(BF16)

    hs = lambda w: pl.BlockSpec((n_heads, rows, w), lambda i: (0, i, 0))
    return _call(
        body, name="pack_heads", grid=(s // rows,), after=after,
        in_specs=[_row_spec(rows, q.shape[1]), _row_spec(rows, kv.shape[1]), pl.BlockSpec((rows, LANE), lambda i: (i, kr_col // LANE)),
                  _row_spec(rows, wq), _row_spec(rows, wq), _row_spec(rows, LANE), _row_spec(rows, LANE)],
        out_specs=[hs(QK), hs(QK), hs(HEAD)],
        out_shape=[jax.ShapeDtypeStruct((n_heads, s, QK), BF16), jax.ShapeDtypeStruct((n_heads, s, QK), BF16),
                   jax.ShapeDtypeStruct((n_heads, s, HEAD), BF16)],
        compiler_params=_params("parallel"),
    )(q, kv, proj, cq, sq, ck, sk)


def _unpack_heads(dq, dk, dv, tables, n_heads):
    s = dq.shape[1]
    rows = min(ROWS, s)
    cq, sq, ck, sk = tables
    wq = n_heads * ROPE

    def body(dq_ref, dk_ref, dv_ref, cq_ref, sq_ref, ck_ref, sk_ref, qo_ref, kvo_ref, kro_ref):
        dqr = jnp.concatenate([dq_ref[h][:, HEAD:] for h in range(n_heads)], axis=1)
        dqr = dqr * cq_ref[...] - _swap_halves(dqr) * sq_ref[...]
        dkr = dk_ref[0][:, HEAD:]
        for h in range(1, n_heads):
            dkr = dkr + dk_ref[h][:, HEAD:]
        dkr = jnp.concatenate([dkr, jnp.zeros((rows, LANE - ROPE), F32)], axis=1)
        dkr = dkr * ck_ref[...] - _swap_halves(dkr) * sk_ref[...]
        kro_ref[...] = dkr.astype(kro_ref.dtype)
        qo_ref[:, n_heads * HEAD:] = dqr.astype(qo_ref.dtype)
        for h in range(n_heads):
            qo_ref[:, h * HEAD:(h + 1) * HEAD] = dq_ref[h][:, :HEAD].astype(qo_ref.dtype)
            kvo_ref[:, 2 * h * HEAD:(2 * h + 1) * HEAD] = dk_ref[h][:, :HEAD].astype(kvo_ref.dtype)
            kvo_ref[:, (2 * h + 1) * HEAD:(2 * h + 2) * HEAD] = dv_ref[h].astype(kvo_ref.dtype)

    hs = lambda w: pl.BlockSpec((n_heads, rows, w), lambda i: (0, i, 0))
    return pl.pallas_call(
        body, name="unpack_heads", grid=(s // rows,),
        in_specs=[hs(QK), hs(QK), hs(HEAD), _row_spec(rows, wq), _row_spec(rows, wq), _row_spec(rows, LANE), _row_spec(rows, LANE)],
        out_specs=[_row_spec(rows, n_heads * QK), _row_spec(rows, 2 * n_heads * HEAD), _row_spec(rows, LANE)],
        out_shape=[jax.ShapeDtypeStruct((s, n_heads * QK), BF16), jax.ShapeDtypeStruct((s, 2 * n_heads * HEAD), BF16),
                   jax.ShapeDtypeStruct((s, LANE), BF16)],
        compiler_params=_params("parallel"),
    )(dq, dk, dv, cq, sq, ck, sk)


TQ = 256


LOG2_E = 1.4426950408889634


def _softmax_parts(q, k):
    tq, n_keys = q.shape[0], k.shape[0]
    sc = lax.dot_general(q, k, (NT, ((), ())), preferred_element_type=F32) * (QK ** -0.5 * LOG2_E)
    row = lax.broadcasted_iota(jnp.int32, (tq, tq), 0)
    col = lax.broadcasted_iota(jnp.int32, (tq, tq), 1)
    own = jnp.where(col // CHUNK <= row // CHUNK, sc[:, n_keys - tq:], NEG_INF)
    sc = own if n_keys == tq else jnp.concatenate([sc[:, :n_keys - tq], own], axis=1)
    e = jnp.exp2(sc - jnp.max(sc, axis=-1, keepdims=True))
    return e, 1.0 / jnp.sum(e, axis=-1, keepdims=True)


def _prob_columns(c, tq):
    return pl.ds(tq * (c * (c + 1) // 2), (c + 1) * tq)


def _attn_fwd(q, k, v, g, mix, col0, first, between):
    n_heads, s, _ = q.shape
    tq = min(TQ, s)
    assert tq % CHUNK == 0 and s % tq == 0
    n_blocks = s // tq
    p_cols = tq * (n_blocks * (n_blocks + 1) // 2)
    out_shape = [jax.ShapeDtypeStruct((n_heads, s, HEAD), F32), jax.ShapeDtypeStruct((n_heads, tq, p_cols), BF16),
                 jax.ShapeDtypeStruct(mix.shape, mix.dtype)]
    done, after = (mix,), ()
    for part, (h0, h1) in enumerate(((0, first), (first, n_heads))):

        def body(q_ref, k_ref, v_ref, g_ref, *rest):
            o_ref, p_ref, y_ref = rest[-3:]
            for c in range(n_blocks):
                rows, n_keys = pl.ds(c * tq, tq), (c + 1) * tq
                e, inv = _softmax_parts(q_ref[rows, :], k_ref[0:n_keys, :])
                p = (e * inv).astype(BF16)
                p_ref[:, _prob_columns(c, tq)] = p
                o = jnp.dot(p, v_ref[0:n_keys, :], preferred_element_type=F32)
                o_ref[rows, :] = o
                y_ref[rows, :] = (o * _rstd(o) * g_ref[...]).astype(y_ref.dtype)

        head = lambda w, h0=h0: pl.BlockSpec((None, s, w), lambda h: (h0 + h, 0, 0))
        n_done = len(done)
        done = _call(
            body, name=f"attn_fwd_{part}", grid=(h1 - h0,), after=after,
            in_specs=[head(QK), head(QK), head(HEAD), pl.BlockSpec((1, HEAD), lambda h, h0=h0: (0, h0 + h))] + [ANY] * n_done,
            out_specs=[head(HEAD), pl.BlockSpec((None, tq, p_cols), lambda h, h0=h0: (h0 + h, 0, 0)),
                       pl.BlockSpec((s, HEAD), lambda h, h0=h0: (0, col0 // HEAD + h0 + h))],
            out_shape=out_shape,
            input_output_aliases={4 + i: 3 - n_done + i for i in range(n_done)},
            compiler_params=_params("parallel"),
        )(q, k, v, g, *done)
        after = between(done) if part == 0 else ()
    return done


def _attn_bwd(q, k, v, o, probs, d_mix, g, col0, after=()):
    n_heads, s, _ = q.shape
    tq = probs.shape[1]

    def body(q_ref, k_ref, v_ref, o_ref, p_ref, dy_ref, g_ref, dq_ref, dk_ref, dv_ref, dg_ref):
        dg = None
        for c in reversed(range(s // tq)):
            rows, n_keys = pl.ds(c * tq, tq), (c + 1) * tq
            o = o_ref[rows, :]
            do, dgc = _rms_bwd(o, g_ref[...], dy_ref[rows, :])
            do = do.astype(BF16)
            dg = _sublane_sum(dgc) if dg is None else dg + _sublane_sum(dgc)
            p = p_ref[:, _prob_columns(c, tq)]
            dp = lax.dot_general(do, v_ref[0:n_keys, :], (NT, ((), ())), preferred_element_type=F32)
            ds = (p.astype(F32) * (dp - jnp.sum(do.astype(F32) * o, axis=-1, keepdims=True))).astype(BF16)
            dq_ref[rows, :] = jnp.dot(ds, k_ref[0:n_keys, :], preferred_element_type=F32) * (QK ** -0.5)
            dk = lax.dot_general(ds, q_ref[rows, :], (TN, ((), ())), preferred_element_type=F32)
            dv = lax.dot_general(p, do, (TN, ((), ())), preferred_element_type=F32)
            if n_keys == s:
                dk_ref[...] = dk
                dv_ref[...] = dv
            else:
                dk_ref[0:n_keys, :] += dk
                dv_ref[0:n_keys, :] += dv
        dk_ref[...] = dk_ref[...] * (QK ** -0.5)
        dg_ref[...] = dg

    c0 = col0 // HEAD
    head = lambda w: pl.BlockSpec((None, s, w), lambda h, *_: (h, 0, 0))
    in_specs = [head(QK), head(QK), head(HEAD), head(HEAD), pl.BlockSpec((None, tq, probs.shape[2]), lambda h, *_: (h, 0, 0)),
                pl.BlockSpec((s, HEAD), lambda h, *_: (0, c0 + h)), pl.BlockSpec((1, HEAD), lambda h, *_: (0, h))]
    out_specs = [head(QK), head(QK), head(HEAD), pl.BlockSpec((SUBLANE, HEAD), lambda h, *_: (0, h))]
    out_shape = [jax.ShapeDtypeStruct((n_heads, s, QK), F32), jax.ShapeDtypeStruct((n_heads, s, QK), F32),
                 jax.ShapeDtypeStruct((n_heads, s, HEAD), F32), jax.ShapeDtypeStruct((SUBLANE, n_heads * HEAD), F32)]
    return _call(body, name="attn_bwd", grid=(n_heads,), after=after, in_specs=in_specs, out_specs=out_specs,
                 out_shape=out_shape, compiler_params=_params("parallel"))(q, k, v, o, probs, d_mix, g)


TILE_M = 1024
TILE_N = 1024


def _up_fwd(h2, w_up, between):
    s, d = h2.shape
    nb, _, fb = w_up.shape
    tm = min(TILE_M, s)
    done, after = (), ()
    for tile in range(s // tm):

        def body(h_ref, w_ref, *rest):
            a_ref, r_ref = rest[-2:]
            r = jnp.maximum(jnp.dot(h_ref[...], w_ref[...], preferred_element_type=F32), 0.0)
            a_ref[...] = (r * r).astype(a_ref.dtype)
            r_ref[...] = r.astype(r_ref.dtype)

        blk = pl.BlockSpec((tm, fb), lambda j, tile=tile: (tile, j))
        done = _call(
            body, name=f"up_fwd_{tile}", grid=(nb,), after=after,
            in_specs=[pl.BlockSpec((tm, d), lambda j, tile=tile: (tile, 0)), pl.BlockSpec((None, d, fb), lambda j: (j, 0, 0))]
                     + [ANY] * len(done),
            out_specs=[blk, blk], out_shape=[jax.ShapeDtypeStruct((s, nb * fb), BF16)] * 2,
            input_output_aliases={2 + i: i for i in range(len(done))},
            compiler_params=_params("parallel"),
        )(h2, w_up, *done)
        after = between(done) if tile == 0 else ()
    return done


def _down_fwd(a, w_down):
    s, f = a.shape
    d = w_down.shape[1]
    tm, tn, tk = min(TILE_M,s), min(TILE_N,d), 2048
    nk = f // tk
    return _matmul("down_fwd", a, w_down, grid=(s // tm, d // tn, nk),
                   a_spec=pl.BlockSpec((tm, tk), lambda i, j, k: (i, k)),
                   b_spec=pl.BlockSpec((tk, tn), lambda i, j, k: (k, j)),
                   out_shape=jax.ShapeDtypeStruct((s, d), F32),
                   out_specs=pl.BlockSpec((tm, tn), lambda i, j, k: (i, j)),
                   contract=NN, nk=nk, acc_shape=(tm, tn))


def _down_bwd_act(d_m, w_down, r, after=()):
    s, d = d_m.shape
    f = w_down.shape[0]
    tm, tn = min(TILE_M,s), min(TILE_N,f)
    blk = pl.BlockSpec((tm, tn), lambda i, j: (i, j))
    return _matmul("down_bwd_act", d_m, w_down, grid=(s // tm, f // tn), after=after,
                   a_spec=pl.BlockSpec((tm, d), lambda i, j: (i, 0)),
                   b_spec=pl.BlockSpec((tn, d), lambda i, j: (j, 0)),
                   out_shape=jax.ShapeDtypeStruct((s, f), BF16), out_specs=blk, contract=NT,
                   extras=(r,), extra_specs=(blk,),
                   epilogue=lambda acc, rv: (acc * (2.0 * rv.astype(F32)),))


def _up_bwd_act(d_up, w_up, after=()):
    s, _ = d_up.shape
    nb, d, fb = w_up.shape
    tm, tn = min(TILE_M, s), min(TILE_N,d)
    pair = 2
    n_after = len(after)

    def body(a_ref, w_ref, *rest):
        o_ref, acc_ref = rest[n_after:]
        k = pl.program_id(2)
        p = None
        for t in range(pair):
            term = lax.dot_general(a_ref[:, t * fb:(t + 1) * fb], w_ref[t], (NT, ((), ())), preferred_element_type=F32)
            p = term if p is None else p + term
        _accumulate(acc_ref, p, k)

        @pl.when(k == nb // pair - 1)
        def _():
            o_ref[...] = acc_ref[...]

    return pl.pallas_call(
        body, name="up_bwd_act", grid=(s // tm, d // tn, nb // pair),
        in_specs=[pl.BlockSpec((tm, pair * fb), lambda i, j, k: (i, k)),
                  pl.BlockSpec((pair, tn, fb), lambda i, j, k: (k, j, 0))] + [ANY] * n_after,
        out_specs=pl.BlockSpec((tm, tn), lambda i, j, k: (i, j)),
        out_shape=jax.ShapeDtypeStruct((s, d), F32),
        scratch_shapes=[pltpu.VMEM((tm, tn), F32)],
        compiler_params=_params("parallel", "parallel", "arbitrary"),
    )(d_up, w_up, *after)


def _half_grad(name, a, b, core, home, received, after, *, grid, a_block, a_map, b_block, b_map, o_block, o_map, out_shape):
    n_after = len(after)
    pick = (lambda ref: ref[0]) if home else (lambda ref: 1 - ref[0])

    def body(core_ref, a_ref, b_ref, *rest):
        acc = lax.dot_general(a_ref[...], b_ref[...], (TN, ((), ())), preferred_element_type=F32)
        if received is not None:
            acc = acc + rest[0][...].astype(F32)
        rest[-1][...] = acc.astype(rest[-1].dtype)

    wrap = lambda fn: (lambda i, j, core_ref: fn(i, j, pick(core_ref)))
    o_spec = pl.BlockSpec(o_block, wrap(o_map))
    extra = [] if received is None else [o_spec]
    operands = [] if received is None else [received]
    return pl.pallas_call(
        body, name=name,
        grid_spec=pltpu.PrefetchScalarGridSpec(
            num_scalar_prefetch=1, grid=grid,
            in_specs=[pl.BlockSpec(a_block, wrap(a_map)), pl.BlockSpec(b_block, wrap(b_map))] + extra + [ANY] * n_after,
            out_specs=o_spec),
        out_shape=out_shape,
        compiler_params=_params("parallel", "parallel"),
    )(core, a, b, *operands, *after)


def _down_half_grad(name, a, d_m, core, home, received=None, after=()):
    s, f = a.shape
    d = d_m.shape[1]
    r = f // N_DEV
    tn = min(TILE_N, d)
    return _half_grad(name, a, d_m, core, home, received, after, grid=(N_CHIP, d // tn),
                      a_block=(s, r), a_map=lambda k, j, p: (0, 2 * k + p),
                      b_block=(s, tn), b_map=lambda k, j, p: (0, j),
                      o_block=(None, r, tn), o_map=lambda k, j, p: (k, 0, j),
                      out_shape=jax.ShapeDtypeStruct((N_CHIP, r, d), BF16))


def _up_half_grad(name, h2, d_up, core, home, received=None, after=()):
    s, d = h2.shape
    fb = d_up.shape[1] // N_DEV
    tm = min(TILE_M, d)
    return _half_grad(name, h2, d_up, core, home, received, after, grid=(d // tm, N_CHIP),
                      a_block=(s, tm), a_map=lambda i, k, p: (0, i),
                      b_block=(s, fb), b_map=lambda i, k, p: (0, 2 * k + p),
                      o_block=(None, tm, fb), o_map=lambda i, k, p: (k, i, 0),
                      out_shape=jax.ShapeDtypeStruct((N_CHIP, d, fb), BF16))


MXU_WIDTH = 256


def _in_pad(in_width):
    return -(-in_width // MXU_WIDTH) * MXU_WIDTH


def _join_col_shards(name, blocks, own, device, pieces=None):
    n, r, w = blocks.shape
    rows = min(ROWS, r)
    pieces = pieces or [(j, 0, w) for j in range(n)]
    used = sum(b - a for _, a, b in pieces)
    width = _in_pad(used)

    def body(dev_ref, x_ref, own_ref, o_ref):
        block = lambda j: jnp.where(dev_ref[0] == j, own_ref[...], x_ref[j])
        cols = [block(j)[:, a:b] for j, a, b in pieces]
        tail = [jnp.zeros((rows, width - used), o_ref.dtype)] if width > used else []
        o_ref[...] = jnp.concatenate(cols + tail, axis=1)

    return pl.pallas_call(
        body, name=name,
        grid_spec=pltpu.PrefetchScalarGridSpec(
            num_scalar_prefetch=1, grid=(r // rows,),
            in_specs=[pl.BlockSpec((n, rows, w), lambda i, dev: (0, i, 0)), pl.BlockSpec((rows, w), lambda i, dev: (i, 0))],
            out_specs=pl.BlockSpec((rows, width), lambda i, dev: (i, 0))),
        out_shape=jax.ShapeDtypeStruct((r, width), blocks.dtype),
        compiler_params=_params("parallel"),
    )(device, blocks, own)


def _unpermute_q_rows(wt, n_heads):
    r = wt.shape[1]
    nope = wt[:n_heads * HEAD].reshape(n_heads, HEAD, r)
    rope = wt[n_heads * HEAD:].reshape(n_heads, ROPE, r)
    return jnp.concatenate([nope, rope], axis=1).reshape(n_heads * QK, r)


def _local_step(x, tgt, gains, weights, grads, first_after=()):
    pre_mix_g, q_norm_g, kv_norm_g, conv_out_g, attn_out_g, post_mix_g, pre_mlp_g, post_mlp_g = gains
    s, d = x.shape
    conv_width = conv_out_g.shape[1]
    n_groups = conv_width // HEAD
    r_q, r_kv = q_norm_g.shape[1], kv_norm_g.shape[1]
    n_heads = attn_out_g.shape[1] // HEAD
    c_q0 = 3 * conv_width
    c_kv0 = c_q0 + r_q
    c_kr0 = c_kv0 + r_kv
    in_pad = _in_pad(c_kr0 + ROPE)
    tn_in = _fit(in_pad, 6 * MXU_WIDTH)
    tables = _rope_tables(s, n_heads)

    h1 = _rms_fwd("pre_mix_norm", x, pre_mix_g, after=first_after)
    weights.forward(0, (h1,))
    weights.relay(0, tables)
    w_in_p, conv_w = weights.ready(0, ())
    proj = _mm_nn("in_proj", h1, w_in_p, F32, TILE_M, tn_in)
    y_conv = _conv_fwd(proj, conv_w, conv_out_g, n_groups, conv_width + n_heads * HEAD, after=weights.forward(1, (proj,)))
    w_uq_p, w_ukv, w_o = weights.ready(1, (y_conv,))
    qn, q = _norm_up("q_up", proj, (c_q0, r_q), q_norm_g, w_uq_p)
    kvn, kv = _norm_up("kv_up", proj, (c_kv0, r_kv), kv_norm_g, w_ukv)
    qh, kh, vh = _pack_heads(q, kv, proj, c_kr0, tables, n_heads)
    o, probs, mix = _attn_fwd(qh, kh, vh, attn_out_g, y_conv, conv_width, n_heads // 4,
                              lambda done: weights.forward(2, tuple(done)))
    y = _mm_nn("out_proj", mix, w_o, F32, TILE_M, TILE_N, after=weights.start(3, (mix,)))
    x2, h2 = _mid_fwd(x, y, post_mix_g, pre_mlp_g, after=weights.relay(2, (y,)))
    (w_up,) = weights.ready(2, (h2,))
    a, r = _up_fwd(h2, w_up, lambda done: weights.forward(3, tuple(done)))
    weights.relay(3, (a,))
    (w_down,) = weights.ready(3, ())
    m = _down_fwd(a, w_down)

    d_out, d_m, dg_post_mlp, loss_part = _head(m, x2, tgt, post_mlp_g)
    core = grads.core
    away = _down_half_grad("down_bwd_w_away", a, d_m, core, home=False)
    d_up = _down_bwd_act(d_m, w_down, r, after=grads.send_away(0, away))
    sums = _down_half_grad("down_bwd_w_home", a, d_m, core, home=True, received=grads.received(0, (d_up,)))
    away = _up_half_grad("up_bwd_w_away", h2, d_up, core, home=False, after=grads.send_sums(0, (sums,)))
    d_h2 = _up_bwd_act(d_up, w_up, after=grads.send_away(1, away))
    sums = _up_half_grad("up_bwd_w_home", h2, d_up, core, home=True, received=grads.received(1, (d_h2,)))
    d_x2, d_y, dg_pre_mlp, dg_post_mix = _mid_bwd(x2, y, d_out, d_h2, pre_mlp_g, post_mix_g, after=grads.send_sums(1, (sums,)))
    d_mix = _mm_nt("out_proj_bwd_act", d_y, w_o, F32, TILE_M, TILE_N)
    gw_o = _mm_tn("out_proj_bwd_w", mix, d_y, BF16, TILE_M, TILE_N)
    dqh, dkh, dvh, dg_attn = _attn_bwd(qh, kh, vh, o, probs, d_mix, attn_out_g, conv_width)
    d_q, d_kv, d_kr = _unpack_heads(dqh, dkh, dvh, tables, n_heads)
    gw_uq_t = _mm_tn("q_up_bwd_w", d_q, qn, F32, TILE_M, TILE_N)
    gw_ukv = _mm_tn("kv_up_bwd_w", kvn, d_kv, BF16, TILE_M, TILE_N)
    d_proj, dg_conv, dw_conv = _conv_bwd(proj, d_mix, conv_w, conv_out_g, n_groups, in_pad)
    d_proj, dg_q = _up_norm_bwd("q_up_bwd_act", d_q, w_uq_p, proj, (c_q0, r_q), q_norm_g, d_proj,
                                after=grads.full(2, (gw_o, gw_uq_t, gw_ukv)))
    d_proj, dg_kv = _up_norm_bwd("kv_up_bwd_act", d_kv, w_ukv, proj, (c_kv0, r_kv), kv_norm_g, d_proj, tail=d_kr)
    gw_in_t = _mm_tn("in_proj_bwd_w", d_proj, h1, F32, tn_in, TILE_N)
    d_h1 = _mm_nt("in_proj_bwd_act", d_proj, w_in_p, F32, TILE_M, TILE_N, after=grads.send_away(3, gw_in_t))
    grad_x, dg_pre_mix = _first_bwd(x, pre_mix_g, d_h1, d_x2, after=grads.full(3, (gw_in_t,), received=(d_h1,)))

    small = [dg_pre_mix, dg_q, dg_kv, dg_conv, dg_attn, dg_post_mix, dg_pre_mlp, dg_post_mlp,
             dw_conv[0], dw_conv[1], dw_conv[2], loss_part]
    return grad_x, jnp.concatenate(small, axis=1)


HBM = pl.BlockSpec(memory_space=pltpu.HBM)
SEM = pl.BlockSpec(memory_space=pltpu.SEMAPHORE)
IN_VMEM = pl.BlockSpec(memory_space=pltpu.VMEM)
SPLIT = pltpu.CompilerParams(has_side_effects=pltpu.SideEffectType.DATAFLOW_SIDE_EFFECTING)


def _in_hbm(a):
    return pltpu.with_memory_space_constraint(a, pltpu.HBM)


def _hbm_like(a):
    return pltpu.HBM(a.shape, a.dtype)


def _place():
    x, y, c = lax.axis_index("x"), lax.axis_index("y"), lax.axis_index("c")
    other_chips = [(1 - x, y), (x, 1 - y), (1 - x, 1 - y)]
    return x, y, c, other_chips


def _block(px, py, pc):
    return 4 * px + 2 * py + pc


def _await(block, sem):
    pltpu.make_async_copy(block, block, sem).wait()


def _relay_route(x, y, c):
    came_from = ((1 - x) * (1 - c) + x * c, y * (1 - c) + (1 - y) * c)
    goes_to = (x * (1 - c) + (1 - x) * c, (1 - y) * (1 - c) + y * c)
    return came_from, goes_to


def _gather_start(name, shards, groups, relayed=(), after=(), lands=None):
    n, ng = len(shards), len(groups)
    if lands is None:
        lands = [lax.empty((N_DEV, *a.shape), a.dtype) for a in shards]

    def body(*refs):
        src, land = refs[:n], refs[n:2 * n]
        sems, token = refs[2 * n + len(after):2 * n + len(after) + 2 * ng], refs[-1]
        x, y, c, chips = _place()
        targets = [(x, y, 1 - c)] + [(*chip, c) for chip in chips]
        for gi, group in enumerate(groups):
            for i, w in enumerate(group):
                for k, to in enumerate(targets[:3] if gi in relayed else targets):
                    pltpu.make_async_remote_copy(
                        src_ref=src[w], dst_ref=land[w].at[_block(x, y, c)],
                        send_sem=sems[2 * gi].at[4 * i + k], recv_sem=sems[2 * gi + 1].at[4 * i + k],
                        device_id=to, device_id_type=MESH).start()
        token[...] = jnp.zeros_like(token)

    sem_shapes = [pltpu.SemaphoreType.DMA((4 * len(g),)) for g in groups for _ in range(2)]
    out = pl.pallas_call(
        body, name=name,
        in_specs=[HBM] * (2 * n) + [ANY] * len(after),
        out_specs=[SEM] * (2 * ng) + [HBM] * (2 * n) + [IN_VMEM],
        out_shape=sem_shapes + [_hbm_like(a) for a in shards] + [_hbm_like(a) for a in lands]
        + [jax.ShapeDtypeStruct((SUBLANE, LANE), F32)],
        input_output_aliases={i: 2 * ng + i for i in range(2 * n)},
        compiler_params=SPLIT,
    )(*[_in_hbm(a) for a in shards], *[_in_hbm(a) for a in lands], *after)
    sems = [(out[2 * gi], out[2 * gi + 1]) for gi in range(ng)]
    return sems, out[2 * ng:2 * ng + n], out[2 * ng + n:2 * ng + 2 * n], out[-1]


def _gather_forward(name, shards, lands, send1, recv1, after, relayed=False):
    n = len(lands)

    def body(*refs):
        src, land = refs[:n], refs[n:2 * n]
        s1, r1 = refs[2 * n], refs[2 * n + 1]
        s2, r2 = refs[2 * n + 2 + len(after)], refs[2 * n + 3 + len(after)]
        x, y, c, chips = _place()
        me, sibling = (x, y, c), (x, y, 1 - c)
        for j, chip in enumerate(chips[:2] if relayed else chips):
            for i in range(n):
                blk = land[i].at[_block(*chip, c)]
                pltpu.make_async_remote_copy(src_ref=blk, dst_ref=blk, send_sem=s1.at[4 * i + 1 + j], recv_sem=r1.at[4 * i + 1 + j],
                                             device_id=me, device_id_type=MESH).wait_recv()
                pltpu.make_async_remote_copy(src_ref=blk, dst_ref=blk, send_sem=s2.at[3 * i + j], recv_sem=r2.at[3 * i + j],
                                             device_id=sibling, device_id_type=MESH).start()
        if relayed:
            came_from, goes_to = _relay_route(x, y, c)
            for i in range(n):
                blk = land[i].at[_block(*came_from, c)]
                pltpu.make_async_remote_copy(src_ref=blk, dst_ref=blk, send_sem=s2.at[3 * i + 2], recv_sem=r2.at[3 * i + 2],
                                             device_id=(*goes_to, c), device_id_type=MESH).start()
        for i in range(n):
            blk = land[i].at[_block(x, y, 1 - c)]
            pltpu.make_async_remote_copy(src_ref=blk, dst_ref=blk, send_sem=s1.at[4 * i], recv_sem=r1.at[4 * i],
                                         device_id=me, device_id_type=MESH).wait_recv()
            for k in range(3 if relayed else 4):
                pltpu.make_async_remote_copy(src_ref=src[i], dst_ref=land[i].at[_block(x, y, c)], send_sem=s1.at[4 * i + k],
                                             recv_sem=r1.at[4 * i + k], device_id=sibling, device_id_type=MESH).wait_send()

    sem = pltpu.SemaphoreType.DMA((3 * n,))
    out = pl.pallas_call(
        body, name=name,
        in_specs=[HBM] * (2 * n) + [SEM, SEM] + [ANY] * len(after),
        out_specs=[SEM, SEM] + [HBM] * n,
        out_shape=[sem, sem] + [_hbm_like(a) for a in lands],
        input_output_aliases={n + i: 2 + i for i in range(n)},
        compiler_params=SPLIT,
    )(*shards, *lands, send1, recv1, *after)
    return (out[0], out[1]), out[2:]


def _gather_relay_forward(name, lands, send2, recv2, after):
    n = len(lands)

    def body(*refs):
        land, s2, r2 = refs[:n], refs[n], refs[n + 1]
        s3, r3 = refs[n + 2 + len(after)], refs[n + 3 + len(after)]
        x, y, c, _ = _place()
        me, sibling = (x, y, c), (x, y, 1 - c)
        came_from, _ = _relay_route(x, y, c)
        for i in range(n):
            blk = land[i].at[_block(1 - x, 1 - y, c)]
            pltpu.make_async_remote_copy(src_ref=blk, dst_ref=blk, send_sem=s2.at[3 * i + 2], recv_sem=r2.at[3 * i + 2],
                                         device_id=me, device_id_type=MESH).wait_recv()
            pltpu.make_async_remote_copy(src_ref=blk, dst_ref=blk, send_sem=s3.at[i], recv_sem=r3.at[i],
                                         device_id=sibling, device_id_type=MESH).start()
            sent = land[i].at[_block(*came_from, c)]
            pltpu.make_async_remote_copy(src_ref=sent, dst_ref=sent, send_sem=s2.at[3 * i + 2], recv_sem=r2.at[3 * i + 2],
                                         device_id=me, device_id_type=MESH).wait_send()

    sem = pltpu.SemaphoreType.DMA((n,))
    out = pl.pallas_call(
        body, name=name,
        in_specs=[HBM] * n + [SEM, SEM] + [ANY] * len(after),
        out_specs=[SEM, SEM] + [HBM] * n,
        out_shape=[sem, sem] + [_hbm_like(a) for a in lands],
        input_output_aliases={i: 2 + i for i in range(n)},
        compiler_params=SPLIT,
    )(*lands, send2, recv2, *after)
    return (out[0], out[1]), out[2:]


def _gather_wait(name, lands, send2, recv2, after, relay_sems=None):
    n = len(lands)
    n_sems = 2 if relay_sems is None else 4

    def body(*refs):
        land, s2, r2 = refs[:n], refs[n], refs[n + 1]
        for i in range(n):
            for j in range(3 if relay_sems is None else 2):
                _await(land[i].at[0], r2.at[3 * i + j])
                _await(land[i].at[0], s2.at[3 * i + j])
            if relay_sems is not None:
                _await(land[i].at[0], refs[n + 3].at[i])
                _await(land[i].at[0], refs[n + 2].at[i])

    return pl.pallas_call(
        body, name=name,
        in_specs=[HBM] * n + [SEM] * n_sems + [ANY] * len(after), out_specs=[HBM] * n, out_shape=[_hbm_like(a) for a in lands],
        input_output_aliases={i: i for i in range(n)},
        compiler_params=SPLIT,
    )(*lands, send2, recv2, *(relay_sems or ()), *after)


def _pair_exchange(name, grads, shard_rows):
    n = len(grads)
    shapes = [(g.shape[1:] if r is None else (r, g.shape[1])) for g, r in zip(grads, shard_rows)]

    def body(*refs):
        ins, recv = refs[:n], refs[n:2 * n]
        send_sems, recv_sems = refs[2 * n:]
        x, y, c, _ = _place()
        sends = []
        for w in range(n):
            for k in range(N_CHIP):
                j, r = 2 * k + 1 - c, shard_rows[w]
                src = ins[w].at[j] if r is None else ins[w].at[pl.ds(pl.multiple_of(j * r, SUBLANE), r), :]
                sends.append(pltpu.make_async_remote_copy(
                    src_ref=src, dst_ref=recv[w].at[k],
                    send_sem=send_sems.at[w, k], recv_sem=recv_sems.at[w, k],
                    device_id=(x, y, 1 - c), device_id_type=MESH))
        for cp in sends:
            cp.start()
        for cp in sends:
            cp.wait()

    return pl.pallas_call(
        body, name=name,
        in_specs=[ANY] * n, out_specs=[ANY] * n,
        out_shape=[jax.ShapeDtypeStruct((N_CHIP, *shape), g.dtype) for g, shape in zip(grads, shapes)],
        scratch_shapes=[pltpu.SemaphoreType.DMA((n, N_CHIP))] * 2,
    )(*grads)


def _pair_sum_rows(name, grad, received, core):
    _, r, c = received.shape
    tc = _fit(c, 512)

    def body(core_ref, a_ref, b_ref, o_ref):
        o_ref[...] = (a_ref[...] + b_ref[...]).astype(o_ref.dtype)

    spec = pl.BlockSpec((None, r, tc), lambda k, i, core_ref: (k, 0, i))
    return pl.pallas_call(
        body, name=name,
        grid_spec=pltpu.PrefetchScalarGridSpec(
            num_scalar_prefetch=1, grid=(N_CHIP, c // tc),
            in_specs=[pl.BlockSpec((r, tc), lambda k, i, core_ref: (2 * k + core_ref[0], i)), spec],
            out_specs=spec),
        out_shape=jax.ShapeDtypeStruct(received.shape, BF16),
        compiler_params=_params("parallel", "parallel"),
    )(core, grad, received)


def _pair_sum(name, grad, received, core):
    _, r, c = received.shape
    rows = min(ROWS, r)
    assert r % rows == 0

    def body(core_ref, a_ref, b_ref, o_ref):
        o_ref[...] = (a_ref[...].astype(F32) + b_ref[...].astype(F32)).astype(o_ref.dtype)

    spec = pl.BlockSpec((None, rows, c), lambda k, i, core_ref: (k, i, 0))
    return pl.pallas_call(
        body, name=name,
        grid_spec=pltpu.PrefetchScalarGridSpec(
            num_scalar_prefetch=1, grid=(N_CHIP, r // rows),
            in_specs=[pl.BlockSpec((None, None, rows, c), lambda k, i, core_ref: (k, core_ref[0], i, 0)), spec],
            out_specs=spec),
        out_shape=jax.ShapeDtypeStruct(received.shape, received.dtype),
        compiler_params=_params("parallel", "parallel"),
    )(core, grad.reshape(N_CHIP, 2, r, c), received)


def _away_shard(src, k, c, shard_rows):
    if shard_rows is None:
        return src.at[k]
    return src.at[pl.ds(pl.multiple_of((2 * k + 1 - c) * shard_rows, SUBLANE), shard_rows), :]


def _pair_send_start(name, away, shard_rows=None):
    shape = away.shape if shard_rows is None else (N_CHIP, shard_rows, away.shape[1])
    land = lax.empty(shape, away.dtype)

    def body(src, dst, send, recv, src_thru, dst_thru, token):
        x, y, c, _ = _place()
        for k in range(N_CHIP):
            pltpu.make_async_remote_copy(src_ref=_away_shard(src, k, c, shard_rows), dst_ref=dst.at[k], send_sem=send.at[k],
                                         recv_sem=recv.at[k], device_id=(x, y, 1 - c), device_id_type=MESH).start()
        token[...] = jnp.zeros_like(token)

    sem = pltpu.SemaphoreType.DMA((N_CHIP,))
    out = pl.pallas_call(
        body, name=name,
        in_specs=[HBM, HBM], out_specs=[SEM, SEM, HBM, HBM, IN_VMEM],
        out_shape=[sem, sem, _hbm_like(away), _hbm_like(land), jax.ShapeDtypeStruct((SUBLANE, LANE), F32)],
        input_output_aliases={0: 2, 1: 3},
        compiler_params=SPLIT,
    )(_in_hbm(away), _in_hbm(land))
    return (out[0], out[1]), out[2], out[3], out[4]


def _pair_send_wait(name, sems, src, land, after, shard_rows=None):
    def body(src_ref, dst_ref, send, recv, *rest):
        for k in range(N_CHIP):
            _await(dst_ref.at[k], send.at[k])
            _await(dst_ref.at[k], recv.at[k])

    return pl.pallas_call(
        body, name=name,
        in_specs=[HBM, HBM, SEM, SEM] + [ANY] * len(after), out_specs=HBM, out_shape=_hbm_like(land),
        input_output_aliases={1: 0},
        compiler_params=SPLIT,
    )(src, land, *sems, *after)


def _chip_send_start(name, sums):
    n = len(sums)
    lands = [lax.empty(a.shape, a.dtype) for a in sums]

    def body(*refs):
        src, land = refs[:n], refs[n:2 * n]
        send, recv, token = refs[2 * n], refs[2 * n + 1], refs[-1]
        x, y, c, chips = _place()
        for w in range(n):
            for j, (px, py) in enumerate(chips):
                pltpu.make_async_remote_copy(
                    src_ref=src[w].at[2 * px + py], dst_ref=land[w].at[2 * x + y],
                    send_sem=send.at[3 * w + j], recv_sem=recv.at[3 * w + j],
                    device_id=(px, py, c), device_id_type=MESH).start()
        token[...] = jnp.zeros_like(token)

    sem = pltpu.SemaphoreType.DMA((3 * n,))
    out = pl.pallas_call(
        body, name=name,
        in_specs=[HBM] * (2 * n),
        out_specs=[SEM, SEM] + [HBM] * (2 * n) + [IN_VMEM],
        out_shape=[sem, sem] + [_hbm_like(a) for a in sums] + [_hbm_like(a) for a in lands]
        + [jax.ShapeDtypeStruct((SUBLANE, LANE), F32)],
        input_output_aliases={i: 2 + i for i in range(2 * n)},
        compiler_params=SPLIT,
    )(*[_in_hbm(a) for a in sums], *[_in_hbm(a) for a in lands])
    return (out[0], out[1]), out[2:2 + n], out[2 + n:2 + 2 * n], out[-1]


def _chip_send_wait(name, groups, after):
    counts = [len(g[1]) for g in groups]
    n = sum(counts)

    def body(*refs):
        land = refs[n:2 * n]
        sems = refs[2 * n:2 * n + 2 * len(groups)]
        w = 0
        for gi, count in enumerate(counts):
            for i in range(count):
                for j in range(3):
                    _await(land[w].at[0], sems[2 * gi].at[3 * i + j])
                    _await(land[w].at[0], sems[2 * gi + 1].at[3 * i + j])
                w += 1

    sums = [a for g in groups for a in g[1]]
    lands = [a for g in groups for a in g[2]]
    sems = [s for g in groups for s in g[0]]
    return pl.pallas_call(
        body, name=name,
        in_specs=[HBM] * (2 * n) + [SEM] * len(sems) + [ANY] * len(after),
        out_specs=[HBM] * n, out_shape=[_hbm_like(a) for a in lands],
        input_output_aliases={n + i: i for i in range(n)},
        compiler_params=SPLIT,
    )(*sums, *lands, *sems, *after)


def _small_all_reduce(part, after=()):
    _, w = part.shape

    def body(p_ref, *rest):
        o_ref, buf, send_sems, recv_sems = rest[len(after):]
        x, y, c, _ = _place()
        me = 4 * x + 2 * y + c
        buf[me] = jnp.sum(p_ref[...], axis=0, keepdims=True)
        copies = []
        for k in range(1, N_DEV):
            dx, dy, dc = (k >> 2) & 1, (k >> 1) & 1, k & 1
            copies.append(pltpu.make_async_remote_copy(
                src_ref=buf.at[me], dst_ref=buf.at[me], send_sem=send_sems.at[k - 1], recv_sem=recv_sems.at[k - 1],
                device_id=(x ^ dx, y ^ dy, c ^ dc), device_id_type=MESH))
        for cp in copies:
            cp.start()
        for cp in copies:
            cp.wait()
        tot = buf[0]
        for d in range(1, N_DEV):
            tot = tot + buf[d]
        o_ref[...] = tot
        loss = jnp.sum(tot[:, w - LANE:], axis=1, keepdims=True)
        o_ref[:, w - LANE:] = jnp.broadcast_to(loss, (1, LANE))

    return pl.pallas_call(
        body, name="small_all_reduce",
        in_specs=[IN_VMEM] + [ANY] * len(after), out_specs=IN_VMEM,
        out_shape=jax.ShapeDtypeStruct((1, w), F32),
        scratch_shapes=[pltpu.VMEM((N_DEV, 1, w), F32), pltpu.SemaphoreType.DMA((N_DEV - 1,)), pltpu.SemaphoreType.DMA((N_DEV - 1,))],
        compiler_params=pltpu.CompilerParams(vmem_limit_bytes=VMEM_LIMIT_BYTES),
    )(part, *after)


def _adamw(w, g, m, v):
    m = ADAM_B1 * m + (1.0 - ADAM_B1) * g
    v = ADAM_B2 * v + (1.0 - ADAM_B2) * (g * g)
    m_hat = m / (1.0 - ADAM_B1 ** ADAM_STEP)
    v_hat = v / (1.0 - ADAM_B2 ** ADAM_STEP)
    delta = -ADAM_LR * (m_hat / (jnp.sqrt(v_hat) + ADAM_EPS) + ADAM_WD * w)
    return delta, m, v


def _sum_adam_block(chip_ref, p_ref, own_ref, w_ref, m_ref, v_ref, g_ref, d_ref, mo_ref, vo_ref):
    g = None
    for k in range(N_CHIP):
        term = jnp.where(chip_ref[0] == k, own_ref[...], p_ref[k]).astype(F32)
        g = term if g is None else g + term
    g_ref[...] = g
    d_ref[...], mo_ref[...], vo_ref[...] = _adamw(w_ref[...], g, m_ref[...], v_ref[...])


def _sum_adam(name, parts, sums, chip, w, m, v, after=()):
    _, r, c = w.shape
    n_after = len(after)
    by_rows = r % ROWS == 0 or r < ROWS
    tr, tc = (min(ROWS, r), c) if by_rows else (r, _fit(c, 512))
    at = (lambda i: (i, 0)) if by_rows else (lambda i: (0, i))

    def body(chip_ref, p_ref, own_ref, w_ref, m_ref, v_ref, *rest):
        _sum_adam_block(chip_ref, p_ref, own_ref, w_ref, m_ref, v_ref, *rest[n_after:])

    blk = pl.BlockSpec((None, tr, tc), lambda i, chip_ref: (0, *at(i)))
    out = jax.ShapeDtypeStruct((1, r, c), F32)
    return pl.pallas_call(
        body, name=name,
        grid_spec=pltpu.PrefetchScalarGridSpec(
            num_scalar_prefetch=1, grid=(r // tr if by_rows else c // tc,),
            in_specs=[pl.BlockSpec((N_CHIP, tr, tc), lambda i, chip_ref: (0, *at(i))),
                      pl.BlockSpec((None, tr, tc), lambda i, chip_ref: (chip_ref[0], *at(i))), blk, blk, blk]
            + [ANY] * n_after,
            out_specs=[blk] * 4),
        out_shape=[out] * 4,
        compiler_params=_params("parallel"),
    )(chip, parts, sums, w, m, v, *after)


def _adam_gains(total, ws, ms, vs):
    n = len(ws)
    widths = [w.shape[1] for w in ws]

    def body(t_ref, *refs):
        w_refs, m_refs, v_refs, outs = refs[:n], refs[n:2 * n], refs[2 * n:3 * n], refs[3 * n:]
        off = 0
        for i in range(n):
            g = t_ref[:, off:off + widths[i]]
            off += widths[i]
            g_ref, d_ref, mo_ref, vo_ref = outs[4 * i:4 * i + 4]
            g_ref[...] = g
            d_ref[...], mo_ref[...], vo_ref[...] = _adamw(w_refs[i][...], g, m_refs[i][...], v_refs[i][...])

    out = pl.pallas_call(
        body, name="adam_gains",
        out_shape=[jax.ShapeDtypeStruct(w.shape, F32) for w in ws for _ in range(4)],
    )(total, *ws, *ms, *vs)
    return [tuple(out[4 * i:4 * i + 4]) for i in range(n)]


def _adam_taps(total, first_col, device, w, m, v):
    _, n_taps, cw = w.shape
    col_block = lambda t, dev: (0, first_col // cw + t * N_DEV + dev[0])
    tap = pl.BlockSpec((None, 1, cw), lambda t, dev: (t, 0, 0))

    def body(dev_ref, t_ref, w_ref, m_ref, v_ref, g_ref, d_ref, mo_ref, vo_ref):
        g = t_ref[...]
        g_ref[...] = g
        d_ref[...], mo_ref[...], vo_ref[...] = _adamw(w_ref[...], g, m_ref[...], v_ref[...])

    shape3 = (n_taps, 1, cw)
    out = pl.pallas_call(
        body, name="adam_taps",
        grid_spec=pltpu.PrefetchScalarGridSpec(
            num_scalar_prefetch=1, grid=(n_taps,),
            in_specs=[pl.BlockSpec((1, cw), col_block), tap, tap, tap], out_specs=[tap] * 4),
        out_shape=[jax.ShapeDtypeStruct(shape3, F32)] * 4,
    )(device, total, w.reshape(shape3), m.reshape(shape3), v.reshape(shape3))
    return tuple(o.reshape(w.shape) for o in out)


def kernel(x, pre_mix_g, w_in, conv_w, q_norm_g, w_uq, kv_norm_g, w_ukv, conv_out_g, attn_out_g, w_o, post_mix_g, pre_mlp_g, w_up, w_down, post_mlp_g, loss_target, m_pre_mix_g, m_w_in, m_conv_w, m_q_norm_g, m_w_uq, m_kv_norm_g, m_w_ukv, m_conv_out_g, m_attn_out_g, m_w_o, m_post_mix_g, m_pre_mlp_g, m_w_up, m_w_down, m_post_mlp_g, v_pre_mix_g, v_w_in, v_conv_w, v_q_norm_g, v_w_uq, v_kv_norm_g, v_w_ukv, v_conv_out_g, v_attn_out_g, v_w_o, v_post_mix_g, v_pre_mlp_g, v_w_up, v_w_down, v_post_mlp_g):
    me = 4 * lax.axis_index("x") + 2 * lax.axis_index("y") + lax.axis_index("c")
    core = lax.axis_index("c").astype(jnp.int32).reshape(1)
    chip = (2 * lax.axis_index("x") + lax.axis_index("y")).astype(jnp.int32).reshape(1)
    gains = (pre_mix_g, q_norm_g, kv_norm_g, conv_out_g, attn_out_g, post_mix_g, pre_mlp_g, post_mlp_g)
    gain_m = (m_pre_mix_g, m_q_norm_g, m_kv_norm_g, m_conv_out_g, m_attn_out_g, m_post_mix_g, m_pre_mlp_g, m_post_mlp_g)
    gain_v = (v_pre_mix_g, v_q_norm_g, v_kv_norm_g, v_conv_out_g, v_attn_out_g, v_post_mix_g, v_pre_mlp_g, v_post_mlp_g)
    names = ("w_in", "w_uq", "w_ukv", "w_o", "w_up", "w_down")
    big = dict(zip(names, (w_in, w_uq, w_ukv, w_o, w_up, w_down)))
    big_m = dict(zip(names, (m_w_in, m_w_uq, m_w_ukv, m_w_o, m_w_up, m_w_down)))
    big_v = dict(zip(names, (v_w_in, v_w_uq, v_w_ukv, v_w_o, v_w_up, v_w_down)))
    n_heads = attn_out_g.shape[1] // HEAD
    n_taps = conv_w.shape[1]

    gathered = ("w_in", "conv", "w_uq", "w_ukv", "w_o", "w_up", "w_down")
    gather_groups = ((0, 1), (2, 3, 4), (5,), (6,))
    taps = jnp.pad(conv_w[0], ((0, SUBLANE - n_taps), (0, 0)))
    relayed_groups = (0, 2, 3)
    sems1, shards, lands, token = _gather_start("gather_start_first", [w_in[0].astype(BF16), taps], ((0, 1),), relayed=(0,))
    sems1, shards, lands = list(sems1), list(shards), list(lands)
    behind = token[0, 0]
    rest = list(lax.optimization_barrier(tuple((big[nm][0] + behind).astype(BF16) for nm in gathered[2:])))

    rest_lands = [lax.dynamic_update_index_in_dim(lax.empty((N_DEV, *a.shape), a.dtype), a, me, 0) for a in rest]

    def start_more(name, some, groups, relayed, after):
        sems_b, shards_b, lands_b, started = _gather_start(name, rest[some], groups, relayed=relayed, after=after,
                                                           lands=rest_lands[some])
        sems1.extend(sems_b)
        shards.extend(shards_b)
        lands.extend(lands_b)
        return started

    start_rest = lambda after: start_more("gather_start_rest", slice(0, 4), ((0, 1, 2), (3,)), (1,), after)
    start_last = lambda after: start_more("gather_start_last", slice(4, 5), ((0,),), (0,), after)

    cols = lambda a: jnp.concatenate([a[j] for j in range(N_DEV)], axis=1)
    rows = lambda a: a.reshape(N_DEV * a.shape[1], a.shape[2])
    device = me.astype(jnp.int32).reshape(1)
    own_in = lambda a, shard: lax.dynamic_update_index_in_dim(a, shard, me, 0)
    q_pieces = [(h, 0, HEAD) for h in range(n_heads)] + [(h, HEAD, QK) for h in range(n_heads)]
    ready = {
        "w_in": lambda a, shard: _join_col_shards("join_w_in", a, shard, device),
        "conv": lambda a, shard: cols(own_in(a, shard))[:n_taps],
        "w_uq": lambda a, shard: _join_col_shards("join_w_uq", a, shard, device, q_pieces),
        "w_ukv": lambda a, shard: cols(a),
        "w_o": lambda a, shard: rows(a),
        "w_up": lambda a, shard: a,
        "w_down": lambda a, shard: rows(a),
    }
    assert w_uq.shape[2] == QK

    class Weights:
        def __init__(self):
            self.passed, self.relayed = {}, {}

        def forward(self, group, after):
            idx = gather_groups[group]
            if group == 0:
                after = (*after, *rest_lands)
            self.passed[group] = _gather_forward(f"gather_forward_{group}", [shards[i] for i in idx], [lands[i] for i in idx],
                                                 *sems1[group], after, relayed=group in relayed_groups)
            return tuple(self.passed[group][1])

        def start(self, group, after):
            assert group == len(gather_groups) - 1
            return (start_last(after),)

        def relay(self, group, after):
            sems2, mid = self.passed[group]
            self.relayed[group], mid = _gather_relay_forward(f"gather_relay_{group}", mid, *sems2, after)
            self.passed[group] = (sems2, mid)
            if group == 0:
                start_rest(tuple(mid))
            return tuple(mid)

        def ready(self, group, after):
            sems2, mid = self.passed[group]
            full = _gather_wait(f"gather_wait_{group}", mid, *sems2, after, relay_sems=self.relayed.get(group))
            out = []
            return [ready[gathered[i]](a, shards[i]) for i, a in zip(gather_groups[group], full)]

    weights = Weights()

    col_blocks = lambda g: g.reshape(g.shape[0], N_DEV, g.shape[1] // N_DEV).transpose(1, 0, 2)
    row_blocks = lambda g: g.reshape(N_DEV, g.shape[0] // N_DEV, g.shape[1])
    grad_groups = (("w_down",), ("w_up",), ("w_o", "w_uq", "w_ukv"), ("w_in",))
    transposed = {"w_in": w_in.shape[2], "w_uq": w_uq.shape[2]}
    to_blocks = {
        "w_in": lambda g: g, "w_uq": lambda g: _unpermute_q_rows(g, n_heads),
        "w_ukv": col_blocks, "w_o": row_blocks, "w_up": lambda g: g, "w_down": row_blocks,
    }
    in_flight = []

    class Grads:
        def __init__(self):
            self.core = core
            self.away = {}

        def send_sums(self, group, sums):
            sems, sums, parts, tok = _chip_send_start(f"chip_send_start_{group}", list(sums))
            in_flight.append((sems, sums, parts))
            return (tok,)

        def full(self, group, arrays, received=None):
            nms = grad_groups[group]
            if received is None:
                blocks = [to_blocks[nm](g) for nm, g in zip(nms, arrays)]
                got = _pair_exchange(f"pair_exchange_{group}", blocks, [transposed.get(nm) for nm in nms])
            else:
                blocks, got = [self.away[group][1]], [self.received(group, received)]
            sums = [(_pair_sum_rows if nm in transposed else _pair_sum)(f"pair_sum_{nm}", g, r, core)
                    for nm, g, r in zip(nms, blocks, got)]
            return self.send_sums(group, sums)

        def send_away(self, group, half):
            nm = grad_groups[group][0]
            rows = transposed.get(nm)
            sems, src, land, tok = _pair_send_start(f"pair_send_start_{group}", half if rows is None else to_blocks[nm](half), rows)
            self.away[group] = (sems, src, land, rows)
            return (tok,)

        def received(self, group, after):
            sems, src, land, rows = self.away[group]
            return _pair_send_wait(f"pair_send_wait_{group}", sems, src, land, after, rows)

    big_out = {}

    def update(tag, first, last, after):
        picked = [i for i in range(first, last) if grad_groups[i][0] not in big_out]
        groups = [in_flight[i] for i in picked]
        parts = _chip_send_wait("chip_send_wait_" + tag, groups, after)
        nms = [nm for i in picked for nm in grad_groups[i]]
        sums = [a for _, s, _ in groups for a in s]
        for nm, p, s in zip(nms, parts, sums):
            view = (lambda a: jnp.swapaxes(a, 1, 2)) if nm in transposed else (lambda a: a)
            out = _sum_adam("adam_" + nm, p, s, chip, view(big[nm]), view(big_m[nm]), view(big_v[nm]), after=after)
            after = (out[0],)
            big_out[nm] = [view(o) for o in out]
        return after

    grad_x, small = _local_step(x[0], loss_target[0], gains, weights, Grads(), first_after=(token,))

    after = update("early", 0, len(in_flight) - 1, (grad_x,))
    total = _small_all_reduce(small, after=after)
    update("late", len(in_flight) - 1, len(in_flight), (total,))
    big_out = [big_out[nm] for nm in names]

    gain_out = _adam_gains(total, gains, gain_m, gain_v)
    taps_out = _adam_taps(total, sum(g.shape[1] for g in gains), me.astype(jnp.int32).reshape(1), conv_w, m_conv_w, v_conv_w)
    loss = total[0, total.shape[1] - 1]

    order = (0, "w_in", "conv", 1, "w_uq", 2, "w_ukv", 3, 4, "w_o", 5, 6, "w_up", "w_down", 7)
    by_name = dict(zip(names, big_out))
    outs = [loss, grad_x[None]]
    for kind in range(4):
        for item in order:
            if item == "conv":
                outs.append(taps_out[kind])
            elif isinstance(item, int):
                outs.append(gain_out[item][kind])
            else:
                outs.append(by_name[item][kind])
    return tuple(outs)
```
